```python
import jax, jax.numpy as jnp
from jax import lax
import numpy as np

D_MODEL = 1024
BATCH = 8
SEQ = 4096
DEPTH = 2

CHUNK = 64
N_MIXERS = 2
N_POOL_LAYERS = (DEPTH + N_MIXERS - 1) // N_MIXERS
N_GLA_LAYERS = DEPTH // N_MIXERS

POOL_WIDTH = D_MODEL
POOL_WINDOWS = (2, 4, 8, 16)
POOL_GROUPS = len(POOL_WINDOWS)
POOL_GROUP_DIM = POOL_WIDTH // POOL_GROUPS

GLA_HEADS = 4
GLA_KEY_WIDTH = D_MODEL // 2
GLA_VALUE_WIDTH = D_MODEL
GLA_HEAD_K = GLA_KEY_WIDTH // GLA_HEADS
GLA_HEAD_V = GLA_VALUE_WIDTH // GLA_HEADS
GLA_GATE_RANK = 16
GLA_GATE_NORMALIZER = 16.0
GLA_IN_WIDTH = 2 * GLA_KEY_WIDTH + 2 * GLA_VALUE_WIDTH + GLA_GATE_RANK

RMS_EPS = 1e-6

kernel_name = "hybrid_pool_gla_streaming_trunk"


def rms_norm(x, w):
    xf = x.astype(jnp.float32)
    y = xf * lax.rsqrt(jnp.mean(xf * xf, axis=-1, keepdims=True) + RMS_EPS)
    return (y * w.astype(jnp.float32)).astype(x.dtype)


def trailing_mean(u, window):
    seq = u.shape[1]
    cs = jnp.cumsum(u.astype(jnp.float32), axis=1)
    lagged = jnp.pad(cs, ((0, 0), (window, 0), (0, 0)))[:, :seq]
    count = jnp.minimum(jnp.arange(1, seq + 1), window).astype(jnp.float32)
    return ((cs - lagged) / count[None, :, None]).astype(u.dtype)


def pool_mixer(h, in_w, group_w, group_b, scale, out_w):
    b, s, _ = h.shape
    u, gate = jnp.split(h @ in_w, 2, axis=-1)
    ug = u.reshape(b, s, POOL_GROUPS, POOL_GROUP_DIM)
    pooled = jnp.stack([trailing_mean(ug[:, :, g], w) for g, w in enumerate(POOL_WINDOWS)],
                       axis=2) - ug
    mixed = jnp.einsum('bsgc,gcd->bsgd', pooled, group_w) + group_b
    y = mixed.reshape(b, s, POOL_WIDTH) * scale * jax.nn.silu(gate)
    return y @ out_w


def gla_mixer(h, in_w, gk_w, gk_b, head_norm_w, out_w):
    b, s, _ = h.shape
    n_chunks = s // CHUNK
    f32 = jnp.float32
    proj = h @ in_w
    q, k, v, gate, gk_low = jnp.split(
        proj, [GLA_KEY_WIDTH, 2 * GLA_KEY_WIDTH, 2 * GLA_KEY_WIDTH + GLA_VALUE_WIDTH,
               2 * GLA_KEY_WIDTH + 2 * GLA_VALUE_WIDTH], axis=-1)
    log_g = jax.nn.log_sigmoid((gk_low @ gk_w + gk_b).astype(f32)) / GLA_GATE_NORMALIZER

    def to_chunks(t, d):
        return t.astype(f32).reshape(b, n_chunks, CHUNK, GLA_HEADS, d)

    qc = to_chunks(q, GLA_HEAD_K) * (GLA_HEAD_K ** -0.5)
    kc = to_chunks(k, GLA_HEAD_K)
    vc = to_chunks(v, GLA_HEAD_V)
    cum = jnp.cumsum(to_chunks(log_g, GLA_HEAD_K), axis=2)
    cum_last = cum[:, :, -1:]
    e_pos, e_neg = jnp.exp(cum), jnp.exp(-cum)

    fwd = jnp.einsum('bnthk,bnshk->bnhts', qc * e_pos, kc * e_neg)
    bwd = jnp.einsum('bnthk,bnshk->bnhts', qc * e_neg, kc * e_pos)
    idx = jnp.arange(CHUNK)
    lower = idx[:, None] >= idx[None, :]
    scores = jnp.where(lower, fwd, bwd)
    o_intra = jnp.einsum('bnhts,bnshv->bnthv', scores, vc)

    q_dec = qc * e_pos
    k_dec = kc * jnp.exp(cum_last - cum)
    chunk_decay = jnp.exp(cum_last[:, :, 0])

    def step(state, inp):
        q_n, k_n, v_n, d_n = inp
        o_n = jnp.einsum('bthk,bhkv->bthv', q_n, state)
        state = state * d_n[..., None] + jnp.einsum('bthk,bthv->bhkv', k_n, v_n)
        return state, o_n

    xs = (jnp.moveaxis(q_dec, 1, 0), jnp.moveaxis(k_dec, 1, 0),
          jnp.moveaxis(vc, 1, 0), jnp.moveaxis(chunk_decay, 1, 0))
    state0 = jnp.zeros((b, GLA_HEADS, GLA_HEAD_K, GLA_HEAD_V), f32)
    _, o_inter = lax.scan(step, state0, xs)
    o = (o_intra + jnp.moveaxis(o_inter, 0, 1)).reshape(b, s, GLA_HEADS, GLA_HEAD_V)

    o = o * lax.rsqrt(jnp.mean(o * o, axis=-1, keepdims=True) + RMS_EPS) * head_norm_w.astype(f32)
    y = o.reshape(b, s, GLA_VALUE_WIDTH).astype(h.dtype) * jax.nn.silu(gate)
    return y @ out_w


def _fwd_setup_inputs(seed: int = 0) -> dict:
    key = jax.random.key(seed)
    ks = jax.random.split(key, 16)
    nrm = jax.random.normal
    f32 = jnp.float32
    NP, NG = N_POOL_LAYERS, N_GLA_LAYERS
    return {
        "x": nrm(ks[0], (BATCH, SEQ, D_MODEL), f32),
        "norm_w": 1.0 + 0.02 * nrm(ks[1], (DEPTH, D_MODEL), f32),
        "pool_in_w": nrm(ks[2], (NP, D_MODEL, 2 * POOL_WIDTH), f32) * D_MODEL ** -0.5,
        "pool_group_w": nrm(ks[3], (NP, POOL_GROUPS, POOL_GROUP_DIM, POOL_GROUP_DIM), f32) * POOL_GROUP_DIM ** -0.5,
        "pool_group_b": 0.02 * nrm(ks[4], (NP, POOL_GROUPS, POOL_GROUP_DIM), f32),
        "pool_scale": 1.0 + 0.1 * nrm(ks[5], (NP, POOL_WIDTH), f32),
        "pool_out_w": nrm(ks[6], (NP, POOL_WIDTH, D_MODEL), f32) * POOL_WIDTH ** -0.5,
        "gla_in_w": nrm(ks[7], (NG, D_MODEL, GLA_IN_WIDTH), f32) * D_MODEL ** -0.5,
        "gla_gk_w": nrm(ks[8], (NG, GLA_GATE_RANK, GLA_KEY_WIDTH), f32) * GLA_GATE_RANK ** -0.5,
        "gla_gk_b": 0.02 * nrm(ks[9], (NG, GLA_KEY_WIDTH), f32),
        "gla_head_norm_w": 1.0 + 0.02 * nrm(ks[10], (NG, GLA_HEAD_V), f32),
        "gla_out_w": nrm(ks[11], (NG, GLA_VALUE_WIDTH, D_MODEL), f32) * GLA_VALUE_WIDTH ** -0.5,
        "final_norm_w": 1.0 + 0.02 * nrm(ks[12], (D_MODEL,), f32),
    }


def _fwd_reference(x, norm_w, pool_in_w, pool_group_w, pool_group_b, pool_scale, pool_out_w,
              gla_in_w, gla_gk_w, gla_gk_b, gla_head_norm_w, gla_out_w, final_norm_w):
    h = x
    for i in range(DEPTH):
        normed = rms_norm(h, norm_w[i])
        j = i // N_MIXERS
        if i % N_MIXERS == 0:
            h = h + pool_mixer(normed, pool_in_w[j], pool_group_w[j], pool_group_b[j],
                               pool_scale[j], pool_out_w[j])
        else:
            h = h + gla_mixer(normed, gla_in_w[j], gla_gk_w[j], gla_gk_b[j],
                              gla_head_norm_w[j], gla_out_w[j])
    return rms_norm(h, final_norm_w)


import jax as _jax
import jax.numpy as _jnp

TWIN_FORMAT = 'train_step'
FWD_PARAMS = ['x', 'norm_w', 'pool_in_w', 'pool_group_w', 'pool_group_b', 'pool_scale', 'pool_out_w', 'gla_in_w', 'gla_gk_w', 'gla_gk_b', 'gla_head_norm_w', 'gla_out_w', 'final_norm_w']
TWIN_WEIGHTS = ['norm_w', 'pool_in_w', 'pool_group_w', 'pool_group_b', 'pool_scale', 'pool_out_w', 'gla_in_w', 'gla_gk_w', 'gla_gk_b', 'gla_head_norm_w', 'gla_out_w', 'final_norm_w']
TWIN_DIFF_INPUT = 'x'
TWIN_INPUTS = ['x', 'norm_w', 'pool_in_w', 'pool_group_w', 'pool_group_b', 'pool_scale', 'pool_out_w', 'gla_in_w', 'gla_gk_w', 'gla_gk_b', 'gla_head_norm_w', 'gla_out_w', 'final_norm_w', 'loss_target', 'm_norm_w', 'm_pool_in_w', 'm_pool_group_w', 'm_pool_group_b', 'm_pool_scale', 'm_pool_out_w', 'm_gla_in_w', 'm_gla_gk_w', 'm_gla_gk_b', 'm_gla_head_norm_w', 'm_gla_out_w', 'm_final_norm_w', 'v_norm_w', 'v_pool_in_w', 'v_pool_group_w', 'v_pool_group_b', 'v_pool_scale', 'v_pool_out_w', 'v_gla_in_w', 'v_gla_gk_w', 'v_gla_gk_b', 'v_gla_head_norm_w', 'v_gla_out_w', 'v_final_norm_w']
TWIN_OUTPUTS = ['loss', 'grad_x', 'grad_norm_w', 'grad_pool_in_w', 'grad_pool_group_w', 'grad_pool_group_b', 'grad_pool_scale', 'grad_pool_out_w', 'grad_gla_in_w', 'grad_gla_gk_w', 'grad_gla_gk_b', 'grad_gla_head_norm_w', 'grad_gla_out_w', 'grad_final_norm_w', 'delta_norm_w', 'delta_pool_in_w', 'delta_pool_group_w', 'delta_pool_group_b', 'delta_pool_scale', 'delta_pool_out_w', 'delta_gla_in_w', 'delta_gla_gk_w', 'delta_gla_gk_b', 'delta_gla_head_norm_w', 'delta_gla_out_w', 'delta_final_norm_w', 'new_m_norm_w', 'new_m_pool_in_w', 'new_m_pool_group_w', 'new_m_pool_group_b', 'new_m_pool_scale', 'new_m_pool_out_w', 'new_m_gla_in_w', 'new_m_gla_gk_w', 'new_m_gla_gk_b', 'new_m_gla_head_norm_w', 'new_m_gla_out_w', 'new_m_final_norm_w', 'new_v_norm_w', 'new_v_pool_in_w', 'new_v_pool_group_w', 'new_v_pool_group_b', 'new_v_pool_scale', 'new_v_pool_out_w', 'new_v_gla_in_w', 'new_v_gla_gk_w', 'new_v_gla_gk_b', 'new_v_gla_head_norm_w', 'new_v_gla_out_w', 'new_v_final_norm_w']
TWIN_LEAF_KINDS = {'loss': 'loss', 'grad_x': 'grad_x', 'grad_norm_w': 'grad_w', 'grad_pool_in_w': 'grad_w', 'grad_pool_group_w': 'grad_w', 'grad_pool_group_b': 'grad_w', 'grad_pool_scale': 'grad_w', 'grad_pool_out_w': 'grad_w', 'grad_gla_in_w': 'grad_w', 'grad_gla_gk_w': 'grad_w', 'grad_gla_gk_b': 'grad_w', 'grad_gla_head_norm_w': 'grad_w', 'grad_gla_out_w': 'grad_w', 'grad_final_norm_w': 'grad_w', 'delta_norm_w': 'delta_w', 'delta_pool_in_w': 'delta_w', 'delta_pool_group_w': 'delta_w', 'delta_pool_group_b': 'delta_w', 'delta_pool_scale': 'delta_w', 'delta_pool_out_w': 'delta_w', 'delta_gla_in_w': 'delta_w', 'delta_gla_gk_w': 'delta_w', 'delta_gla_gk_b': 'delta_w', 'delta_gla_head_norm_w': 'delta_w', 'delta_gla_out_w': 'delta_w', 'delta_final_norm_w': 'delta_w', 'new_m_norm_w': 'new_m', 'new_m_pool_in_w': 'new_m', 'new_m_pool_group_w': 'new_m', 'new_m_pool_group_b': 'new_m', 'new_m_pool_scale': 'new_m', 'new_m_pool_out_w': 'new_m', 'new_m_gla_in_w': 'new_m', 'new_m_gla_gk_w': 'new_m', 'new_m_gla_gk_b': 'new_m', 'new_m_gla_head_norm_w': 'new_m', 'new_m_gla_out_w': 'new_m', 'new_m_final_norm_w': 'new_m', 'new_v_norm_w': 'new_v', 'new_v_pool_in_w': 'new_v', 'new_v_pool_group_w': 'new_v', 'new_v_pool_group_b': 'new_v', 'new_v_pool_scale': 'new_v', 'new_v_pool_out_w': 'new_v', 'new_v_gla_in_w': 'new_v', 'new_v_gla_gk_w': 'new_v', 'new_v_gla_gk_b': 'new_v', 'new_v_gla_head_norm_w': 'new_v', 'new_v_gla_out_w': 'new_v', 'new_v_final_norm_w': 'new_v'}


def _forward(args):
    return _fwd_reference(*[args[k] for k in FWD_PARAMS])


def _output_shape():
    def fwd():
        inp = _fwd_setup_inputs(0)
        return _fwd_reference(*[inp[k] for k in FWD_PARAMS])
    out = _jax.eval_shape(fwd)
    return out.shape, out.dtype

N_MICROBATCH = 1
ADAM_LR = 0.001
ADAM_B1 = 0.9
ADAM_B2 = 0.999
ADAM_EPS = 1e-08
ADAM_WD = 0.01
ADAM_STEP = 10
PER_EXAMPLE_BATCH_AXIS = {'x': 0, 'loss_target': 0}
SHARED_INPUTS = []
_WEIGHT_DTYPES = {'norm_w': _jnp.float32, 'pool_in_w': _jnp.float32, 'pool_group_w': _jnp.float32, 'pool_group_b': _jnp.float32, 'pool_scale': _jnp.float32, 'pool_out_w': _jnp.float32, 'gla_in_w': _jnp.float32, 'gla_gk_w': _jnp.float32, 'gla_gk_b': _jnp.float32, 'gla_head_norm_w': _jnp.float32, 'gla_out_w': _jnp.float32, 'final_norm_w': _jnp.float32}
MOMENT_SCALE = {'norm_w': 1.636694e-01, 'pool_in_w': 1.078315e-01, 'pool_group_w': 1.064100e-01, 'pool_group_b': 1.225970e-01, 'pool_scale': 1.077958e-01, 'pool_out_w': 1.066297e-01, 'gla_in_w': 9.744800e-02, 'gla_gk_w': 1.282390e-02, 'gla_gk_b': 4.842410e-02, 'gla_head_norm_w': 1.852066e-01, 'gla_out_w': 8.284121e-02, 'final_norm_w': 3.202744e+01}


def _to_microbatches(a, axis):
    t = _jnp.moveaxis(a, axis, 0)
    t = t.reshape((N_MICROBATCH, t.shape[0] // N_MICROBATCH) + t.shape[1:])
    return _jnp.moveaxis(t, 1, axis + 1)


def setup_inputs(seed: int = 0) -> dict:
    inp = _fwd_setup_inputs(seed)
    key = _jax.random.fold_in(_jax.random.key(seed), 7919)
    shape, _ = _output_shape()
    out = dict(inp)
    out["loss_target"] = _jax.random.normal(_jax.random.fold_in(key, 0), shape, _jnp.float32)
    for i, name in enumerate(TWIN_WEIGHTS):
        w = inp[name].astype(_jnp.float32)
        if MOMENT_SCALE is None:
            s = _jnp.sqrt(_jnp.mean(_jnp.square(w)) + 1e-30)
        else:
            s = MOMENT_SCALE[name]
        km, kv = _jax.random.split(_jax.random.fold_in(key, i + 1))
        out[name] = w
        out["m_" + name] = s * _jax.random.normal(km, w.shape, _jnp.float32)
        out["v_" + name] = (s * s) * _jax.random.uniform(kv, w.shape, _jnp.float32, 0.5, 1.5)
    if N_MICROBATCH > 1:
        for name, axis in PER_EXAMPLE_BATCH_AXIS.items():
            out[name] = _to_microbatches(out[name], axis)
    return {'x': out['x'], 'norm_w': out['norm_w'], 'pool_in_w': out['pool_in_w'], 'pool_group_w': out['pool_group_w'], 'pool_group_b': out['pool_group_b'], 'pool_scale': out['pool_scale'], 'pool_out_w': out['pool_out_w'], 'gla_in_w': out['gla_in_w'], 'gla_gk_w': out['gla_gk_w'], 'gla_gk_b': out['gla_gk_b'], 'gla_head_norm_w': out['gla_head_norm_w'], 'gla_out_w': out['gla_out_w'], 'final_norm_w': out['final_norm_w'], 'loss_target': out['loss_target'], 'm_norm_w': out['m_norm_w'], 'm_pool_in_w': out['m_pool_in_w'], 'm_pool_group_w': out['m_pool_group_w'], 'm_pool_group_b': out['m_pool_group_b'], 'm_pool_scale': out['m_pool_scale'], 'm_pool_out_w': out['m_pool_out_w'], 'm_gla_in_w': out['m_gla_in_w'], 'm_gla_gk_w': out['m_gla_gk_w'], 'm_gla_gk_b': out['m_gla_gk_b'], 'm_gla_head_norm_w': out['m_gla_head_norm_w'], 'm_gla_out_w': out['m_gla_out_w'], 'm_final_norm_w': out['m_final_norm_w'], 'v_norm_w': out['v_norm_w'], 'v_pool_in_w': out['v_pool_in_w'], 'v_pool_group_w': out['v_pool_group_w'], 'v_pool_group_b': out['v_pool_group_b'], 'v_pool_scale': out['v_pool_scale'], 'v_pool_out_w': out['v_pool_out_w'], 'v_gla_in_w': out['v_gla_in_w'], 'v_gla_gk_w': out['v_gla_gk_w'], 'v_gla_gk_b': out['v_gla_gk_b'], 'v_gla_head_norm_w': out['v_gla_head_norm_w'], 'v_gla_out_w': out['v_gla_out_w'], 'v_final_norm_w': out['v_final_norm_w']}


def _loss(weights, diff, rest, loss_target):
    with _jax.named_scope("forward"):
        args = {**rest, TWIN_DIFF_INPUT: diff, **{k: w.astype(_WEIGHT_DTYPES[k]) for k, w in weights.items()}}
        y = _forward(args)
    with _jax.named_scope("loss_head"):
        err = _jnp.square(y.astype(_jnp.float32) - loss_target)
        return 0.5 * _jnp.sum(_jnp.mean(err, axis=-1)) if err.ndim else 0.5 * err


def _adamw(w, g, m, v):
    m = ADAM_B1 * m + (1.0 - ADAM_B1) * g
    v = ADAM_B2 * v + (1.0 - ADAM_B2) * _jnp.square(g)
    m_hat = m / (1.0 - ADAM_B1 ** ADAM_STEP)
    v_hat = v / (1.0 - ADAM_B2 ** ADAM_STEP)
    delta = -ADAM_LR * (m_hat / (_jnp.sqrt(v_hat) + ADAM_EPS) + ADAM_WD * w)
    return delta, m, v


def reference(x, norm_w, pool_in_w, pool_group_w, pool_group_b, pool_scale, pool_out_w, gla_in_w, gla_gk_w, gla_gk_b, gla_head_norm_w, gla_out_w, final_norm_w, loss_target, m_norm_w, m_pool_in_w, m_pool_group_w, m_pool_group_b, m_pool_scale, m_pool_out_w, m_gla_in_w, m_gla_gk_w, m_gla_gk_b, m_gla_head_norm_w, m_gla_out_w, m_final_norm_w, v_norm_w, v_pool_in_w, v_pool_group_w, v_pool_group_b, v_pool_scale, v_pool_out_w, v_gla_in_w, v_gla_gk_w, v_gla_gk_b, v_gla_head_norm_w, v_gla_out_w, v_final_norm_w):
    given = dict(x=x, norm_w=norm_w, pool_in_w=pool_in_w, pool_group_w=pool_group_w, pool_group_b=pool_group_b, pool_scale=pool_scale, pool_out_w=pool_out_w, gla_in_w=gla_in_w, gla_gk_w=gla_gk_w, gla_gk_b=gla_gk_b, gla_head_norm_w=gla_head_norm_w, gla_out_w=gla_out_w, final_norm_w=final_norm_w, loss_target=loss_target, m_norm_w=m_norm_w, m_pool_in_w=m_pool_in_w, m_pool_group_w=m_pool_group_w, m_pool_group_b=m_pool_group_b, m_pool_scale=m_pool_scale, m_pool_out_w=m_pool_out_w, m_gla_in_w=m_gla_in_w, m_gla_gk_w=m_gla_gk_w, m_gla_gk_b=m_gla_gk_b, m_gla_head_norm_w=m_gla_head_norm_w, m_gla_out_w=m_gla_out_w, m_final_norm_w=m_final_norm_w, v_norm_w=v_norm_w, v_pool_in_w=v_pool_in_w, v_pool_group_w=v_pool_group_w, v_pool_group_b=v_pool_group_b, v_pool_scale=v_pool_scale, v_pool_out_w=v_pool_out_w, v_gla_in_w=v_gla_in_w, v_gla_gk_w=v_gla_gk_w, v_gla_gk_b=v_gla_gk_b, v_gla_head_norm_w=v_gla_head_norm_w, v_gla_out_w=v_gla_out_w, v_final_norm_w=v_final_norm_w)
    weights = {n: given[n] for n in TWIN_WEIGHTS}
    shared = {n: given[n] for n in SHARED_INPUTS}
    per_example = {n: given[n] for n in ['x']}
    grad_fn = _jax.value_and_grad(_loss, argnums=(0, 1))

    def one_microbatch(ex, loss_target):
        ex = dict(ex)
        diff = ex.pop(TWIN_DIFF_INPUT)
        return grad_fn(weights, diff, {**shared, **ex}, loss_target)

    if N_MICROBATCH == 1:
        loss, (grad_w, grad_x) = one_microbatch(per_example, given["loss_target"])
    else:
        def body(carry, xs):
            loss_sum, grad_sum = carry
            l_k, (gw_k, gx_k) = one_microbatch(xs[0], xs[1])
            with _jax.named_scope("update"):
                return (loss_sum + l_k, _jax.tree.map(_jnp.add, grad_sum, gw_k)), gx_k

        init = (_jnp.zeros((), _jnp.float32), _jax.tree.map(_jnp.zeros_like, weights))
        (loss, grad_w), grad_x = _jax.lax.scan(body, init, (per_example, given["loss_target"]))
    with _jax.named_scope("update"):
        delta_w, new_m, new_v = {}, {}, {}
        for n in TWIN_WEIGHTS:
            delta_w[n], new_m[n], new_v[n] = _adamw(weights[n], grad_w[n], given["m_" + n], given["v_" + n])
    return (loss, grad_x, *[grad_w[n] for n in TWIN_WEIGHTS], *[delta_w[n] for n in TWIN_WEIGHTS],
            *[new_m[n] for n in TWIN_WEIGHTS], *[new_v[n] for n in TWIN_WEIGHTS])
```

```python
import functools

import jax
import jax.numpy as jnp
from jax import lax
from jax.experimental import pallas as pl
from jax.experimental.pallas import tpu as pltpu

F32 = jnp.float32
BF16 = jnp.bfloat16
MESH = pl.DeviceIdType.MESH

D = 1024
POOL_WINDOWS = (2, 4, 8, 16)
GROUPS = 4
GROUP_DIM = 256
HEADS = 4
HEAD_K = 128
HEAD_V = 256
KEY_W = 512
CHUNK = 64
GATE_RANK = 16
GATE_NORM = 16.0
GLA_IN = 3088
GLA_MAIN = 3072
RANK_PAD = 128
EPS = 1e-6
HALO = 16

ADAM_LR = 0.001
ADAM_B1 = 0.9
ADAM_B2 = 0.999
ADAM_EPS = 1e-08
ADAM_WD = 0.01
ADAM_STEP = 10

N_CHIPS = 4
N_DEV = 8
ROWS_PI, ROWS_GW, ROWS_PO, ROWS_GI, ROWS_GO = 512, 64, 256, 772, 256
OFF_PI = 0
OFF_GW = OFF_PI + ROWS_PI
OFF_PO = OFF_GW + ROWS_GW
OFF_GI = OFF_PO + ROWS_PO
OFF_GO = OFF_GI + ROWS_GI
ROWS_USED = OFF_GO + ROWS_GO
PACK_ROWS = 1888
HALF_ROWS = PACK_ROWS // 2
SMALL_ROWS = 16

VMEM_LIMIT = 56 * 1024 * 1024


def _nn(a, b):
    return lax.dot_general(a, b, (((1,), (0,)), ((), ())), preferred_element_type=F32)


def _nt(a, b):
    return lax.dot_general(a, b, (((1,), (1,)), ((), ())), preferred_element_type=F32)


def _tn(a, b):
    return lax.dot_general(a, b, (((0,), (0,)), ((), ())), preferred_element_type=F32)


def _nn_exact(a, b):
    return lax.dot_general(a, b, (((1,), (0,)), ((), ())), preferred_element_type=F32,
                           precision=lax.Precision.HIGHEST)


def _bf(a):
    return a.astype(BF16)


def _params(*sem):
    return pltpu.CompilerParams(dimension_semantics=sem, vmem_limit_bytes=VMEM_LIMIT)


def _full(shape):
    return pl.BlockSpec(shape, lambda i: (0,) * len(shape))


def _position():
    return lax.axis_index("x"), lax.axis_index("y"), lax.axis_index("c")


def gather_small(block, name, reduce=False):
    rows, cols = block.shape

    def body(in_ref, out_ref, *scratch):
        if reduce:
            all_ref, send_sems, recv_sems, local_sem = scratch
        else:
            all_ref = out_ref
            send_sems, recv_sems, local_sem = scratch
        x, y, c = _position()
        me = 4 * x + 2 * y + c
        mine = pltpu.make_async_copy(in_ref, all_ref.at[me], local_sem)
        mine.start()
        sends = []
        for k in range(N_DEV - 1):
            fx, fy, fc = (k + 1) >> 2 & 1, (k + 1) >> 1 & 1, (k + 1) & 1
            cp = pltpu.make_async_remote_copy(
                src_ref=in_ref, dst_ref=all_ref.at[me],
                send_sem=send_sems.at[k], recv_sem=recv_sems.at[k],
                device_id=(x ^ fx, y ^ fy, c ^ fc), device_id_type=MESH)
            cp.start()
            sends.append(cp)
        for k in range(N_DEV - 1):
            fx, fy, fc = (k + 1) >> 2 & 1, (k + 1) >> 1 & 1, (k + 1) & 1
            src_dev = 4 * (x ^ fx) + 2 * (y ^ fy) + (c ^ fc)
            pltpu.make_async_remote_copy(
                src_ref=in_ref, dst_ref=all_ref.at[src_dev],
                send_sem=send_sems.at[k], recv_sem=recv_sems.at[k],
                device_id=(x, y, c), device_id_type=MESH).wait_recv()
        for cp in sends:
            cp.wait_send()
        mine.wait()
        if reduce:
            total = all_ref[0]
            for dev in range(1, N_DEV):
                total = total + all_ref[dev]
            out_ref[...] = total

    sems = [pltpu.SemaphoreType.DMA((N_DEV - 1,)), pltpu.SemaphoreType.DMA((N_DEV - 1,)),
            pltpu.SemaphoreType.DMA]
    gathered = (N_DEV, rows, cols)
    return pl.pallas_call(
        body, name=name,
        out_shape=jax.ShapeDtypeStruct((rows, cols) if reduce else gathered, block.dtype),
        in_specs=[pl.BlockSpec(memory_space=pltpu.VMEM)],
        out_specs=pl.BlockSpec(memory_space=pltpu.VMEM),
        scratch_shapes=([pltpu.VMEM(gathered, block.dtype)] if reduce else []) + sems,
    )(block)


def _other_chips(x, y):
    return [(1 - x, y), (x, 1 - y), (1 - x, 1 - y)]


def allgather_weights(packed):
    def body(w_ref, out_ref, send_sems, recv_sems, local_sem):
        x, y, c = _position()
        q = 2 * x + y
        sibling = (x, y, 1 - c)
        chips = _other_chips(x, y)
        my_half = pl.ds(c * HALF_ROWS, HALF_ROWS)
        other_half = pl.ds((1 - c) * HALF_ROWS, HALF_ROWS)

        def copy(k, quarter, half, to, src=None):
            dst = out_ref.at[quarter, half]
            return pltpu.make_async_remote_copy(
                src_ref=dst if src is None else src, dst_ref=dst,
                send_sem=send_sems.at[k], recv_sem=recv_sems.at[k],
                device_id=to, device_id_type=MESH)

        mine = pltpu.make_async_copy(w_ref, out_ref.at[q], local_sem)
        mine.start()
        first = [copy(j, q, my_half, (*chip, c), src=w_ref.at[my_half]) for j, chip in enumerate(chips)]
        for cp in first:
            cp.start()
        passed = []
        for j, chip in enumerate(chips):
            qj = 2 * chip[0] + chip[1]
            copy(j, qj, my_half, (x, y, c)).wait_recv()
            cp = copy(3 + j, qj, my_half, sibling)
            cp.start()
            passed.append(cp)
        for j, chip in enumerate(chips):
            qj = 2 * chip[0] + chip[1]
            copy(3 + j, qj, other_half, (x, y, c)).wait_recv()
        for cp in first + passed:
            cp.wait_send()
        mine.wait()

    return pl.pallas_call(
        body, name="allgather_weights",
        out_shape=jax.ShapeDtypeStruct((N_CHIPS, PACK_ROWS, D), packed.dtype),
        in_specs=[pl.BlockSpec(memory_space=pl.ANY)],
        out_specs=pl.BlockSpec(memory_space=pl.ANY),
        scratch_shapes=[pltpu.SemaphoreType.DMA((6,)), pltpu.SemaphoreType.DMA((6,)),
                        pltpu.SemaphoreType.DMA],
    )(packed)


def exchange_with_sibling(grads):
    def body(g_ref, mine_ref, theirs_ref, send_sem, recv_sem, local_sem):
        x, y, c = _position()
        my_half = pl.ds(c * HALF_ROWS, HALF_ROWS)
        other_half = pl.ds((1 - c) * HALF_ROWS, HALF_ROWS)
        keep = pltpu.make_async_copy(g_ref.at[:, my_half], mine_ref, local_sem)
        keep.start()
        cp = pltpu.make_async_remote_copy(
            src_ref=g_ref.at[:, other_half], dst_ref=theirs_ref,
            send_sem=send_sem, recv_sem=recv_sem,
            device_id=(x, y, 1 - c), device_id_type=MESH)
        cp.start()
        cp.wait()
        keep.wait()

    shape = jax.ShapeDtypeStruct((N_CHIPS, HALF_ROWS, D), F32)
    return pl.pallas_call(
        body, name="exchange_with_sibling",
        out_shape=(shape, shape),
        in_specs=[pl.BlockSpec(memory_space=pl.ANY)],
        out_specs=(pl.BlockSpec(memory_space=pl.ANY), pl.BlockSpec(memory_space=pl.ANY)),
        scratch_shapes=[pltpu.SemaphoreType.DMA, pltpu.SemaphoreType.DMA, pltpu.SemaphoreType.DMA],
    )(grads)


def scatter_to_owners(chip_sum, chip_sum_bf):
    def body(f_ref, b_ref, own_ref, got_ref, send_sems, recv_sems, local_sem):
        x, y, c = _position()
        q = 2 * x + y
        keep = pltpu.make_async_copy(f_ref.at[q], own_ref, local_sem)
        keep.start()
        sends = []
        for j, chip in enumerate(_other_chips(x, y)):
            qj = 2 * chip[0] + chip[1]
            cp = pltpu.make_async_remote_copy(
                src_ref=b_ref.at[qj], dst_ref=got_ref.at[j],
                send_sem=send_sems.at[j], recv_sem=recv_sems.at[j],
                device_id=(*chip, c), device_id_type=MESH)
            cp.start()
            sends.append(cp)
        for cp in sends:
            cp.wait()
        keep.wait()

    return pl.pallas_call(
        body, name="scatter_to_owners",
        out_shape=(jax.ShapeDtypeStruct((HALF_ROWS, D), F32),
                   jax.ShapeDtypeStruct((N_CHIPS - 1, HALF_ROWS, D), BF16)),
        in_specs=[pl.BlockSpec(memory_space=pl.ANY), pl.BlockSpec(memory_space=pl.ANY)],
        out_specs=(pl.BlockSpec(memory_space=pl.ANY), pl.BlockSpec(memory_space=pl.ANY)),
        scratch_shapes=[pltpu.SemaphoreType.DMA((3,)), pltpu.SemaphoreType.DMA((3,)),
                        pltpu.SemaphoreType.DMA],
    )(chip_sum, chip_sum_bf)


def join_halves(half):
    def body(h_ref, out_ref, send_sem, recv_sem, local_sem):
        x, y, c = _position()
        my_half = pl.ds(c * HALF_ROWS, HALF_ROWS)
        keep = pltpu.make_async_copy(h_ref, out_ref.at[my_half], local_sem)
        keep.start()
        cp = pltpu.make_async_remote_copy(
            src_ref=h_ref, dst_ref=out_ref.at[my_half],
            send_sem=send_sem, recv_sem=recv_sem,
            device_id=(x, y, 1 - c), device_id_type=MESH)
        cp.start()
        cp.wait()
        keep.wait()

    return pl.pallas_call(
        body, name="join_halves",
        out_shape=jax.ShapeDtypeStruct((PACK_ROWS, D), F32),
        in_specs=[pl.BlockSpec(memory_space=pl.ANY)],
        out_specs=pl.BlockSpec(memory_space=pl.ANY),
        scratch_shapes=[pltpu.SemaphoreType.DMA, pltpu.SemaphoreType.DMA, pltpu.SemaphoreType.DMA],
    )(half)


ADD_ROWS = 472


def add_halves(mine, theirs):
    def body(a_ref, b_ref, f_ref, h_ref):
        s = a_ref[...] + b_ref[...]
        f_ref[...] = s
        h_ref[...] = _bf(s)

    spec = pl.BlockSpec((1, ADD_ROWS, D), lambda i, j: (i, j, 0))
    return pl.pallas_call(
        body, name="add_halves", grid=(N_CHIPS, HALF_ROWS // ADD_ROWS),
        out_shape=(jax.ShapeDtypeStruct(mine.shape, F32), jax.ShapeDtypeStruct(mine.shape, BF16)),
        in_specs=[spec, spec], out_specs=(spec, spec),
        compiler_params=_params("parallel", "parallel"),
    )(mine, theirs)


def add_parts(own, got):
    def body(o_ref, g_ref, out_ref):
        s = o_ref[...]
        for j in range(N_CHIPS - 1):
            s = s + g_ref[j].astype(F32)
        out_ref[...] = s

    return pl.pallas_call(
        body, name="add_parts", grid=(HALF_ROWS // ADD_ROWS,),
        out_shape=jax.ShapeDtypeStruct(own.shape, F32),
        in_specs=[pl.BlockSpec((ADD_ROWS, D), lambda i: (i, 0)),
                  pl.BlockSpec((N_CHIPS - 1, ADD_ROWS, D), lambda i: (0, i, 0))],
        out_specs=pl.BlockSpec((ADD_ROWS, D), lambda i: (i, 0)),
        compiler_params=_params("parallel"),
    )(own, got)


def _adam_math(w, g, m, v):
    m = ADAM_B1 * m + (1.0 - ADAM_B1) * g
    v = ADAM_B2 * v + (1.0 - ADAM_B2) * (g * g)
    m_hat = m / (1.0 - ADAM_B1 ** ADAM_STEP)
    v_hat = v / (1.0 - ADAM_B2 ** ADAM_STEP)
    delta = -ADAM_LR * (m_hat / (jnp.sqrt(v_hat) + ADAM_EPS) + ADAM_WD * w)
    return delta, m, v


def adamw(w, g, m, v, name):
    rows, cols = w.shape
    tile = 256 if rows % 256 == 0 else rows

    def body(w_ref, g_ref, m_ref, v_ref, d_ref, nm_ref, nv_ref):
        d, nm, nv = _adam_math(w_ref[...], g_ref[...], m_ref[...], v_ref[...])
        d_ref[...] = d
        nm_ref[...] = nm
        nv_ref[...] = nv

    spec = pl.BlockSpec((tile, cols), lambda i: (i, 0))
    shape = jax.ShapeDtypeStruct((rows, cols), F32)
    return pl.pallas_call(
        body, name=name, grid=(rows // tile,),
        out_shape=(shape, shape, shape),
        in_specs=[spec] * 4, out_specs=(spec, spec, spec),
        compiler_params=_params("parallel"),
    )(w, g, m, v)


def matmul_tn(a, b, name, tile_n=512, tile_s=512):
    s, m = a.shape
    n = b.shape[1]
    tile_n = min(tile_n, n)
    steps = s // tile_s

    def body(a_ref, b_ref, out_ref):
        k = pl.program_id(1)

        @pl.when(k == 0)
        def _():
            out_ref[...] = jnp.zeros_like(out_ref)

        out_ref[...] += _tn(a_ref[...], b_ref[...])

    return pl.pallas_call(
        body, name=name, grid=(n // tile_n, steps),
        out_shape=jax.ShapeDtypeStruct((m, n), F32),
        in_specs=[pl.BlockSpec((tile_s, m), lambda j, k: (k, 0)),
                  pl.BlockSpec((tile_s, tile_n), lambda j, k: (k, j))],
        out_specs=pl.BlockSpec((m, tile_n), lambda j, k: (0, j)),
        compiler_params=_params("parallel", "arbitrary"),
    )(a, b)


ROW_TILE = 256


def _row_index(tile, rows):
    return tile * rows + lax.broadcasted_iota(jnp.int32, (rows, 1), 0)


def _inverse_counts(t_glob):
    return [1.0 / jnp.minimum(t_glob + 1, w).astype(F32) for w in POOL_WINDOWS]


def _sigmoid(z):
    return 1.0 / (1.0 + jnp.exp(-z))


def pool_forward(x, w0, wpi, gw, gb, scale, wpo):
    s = x.shape[0]
    ts = ROW_TILE
    nt = s // ts

    def body(x_ref, w0_ref, wpi_ref, gw_ref, gb_ref, sc_ref, wpo_ref,
             h1_ref, pooled_ref, gt_ref, n0_ref, ubuf):
        i = pl.program_id(0)
        xv = x_ref[...]
        r = lax.rsqrt(jnp.mean(xv * xv, axis=-1, keepdims=True) + EPS)
        n0 = _bf(xv * r * w0_ref[...])
        n0_ref[...] = n0
        proj = _nn(n0, wpi_ref[...])
        u = proj[:, :D]
        gt = proj[:, D:]
        gt_ref[...] = gt

        @pl.when(i == 0)
        def _():
            ubuf[0:HALO, :] = jnp.zeros((HALO, D), F32)

        ubuf[HALO:HALO + ts, :] = u
        inv = _inverse_counts(_row_index(i, ts))
        mixed = []
        for g, w in enumerate(POOL_WINDOWS):
            cols = slice(g * GROUP_DIM, (g + 1) * GROUP_DIM)
            ug = u[:, cols]
            acc = ug
            for j in range(1, w):
                acc = acc + ubuf[HALO - j:HALO - j + ts, cols]
            pooled = acc * inv[g] - ug
            pooled_ref[:, cols] = pooled
            mixed.append(_nn(_bf(pooled), gw_ref[g]))
        ubuf[0:HALO, :] = ubuf[ts:ts + HALO, :]
        mixed = jnp.concatenate(mixed, axis=-1) + gb_ref[...]
        y = mixed * sc_ref[...] * (gt * _sigmoid(gt))
        h1_ref[...] = xv + _nn(_bf(y), wpo_ref[...])

    row = lambda cols: pl.BlockSpec((ts, cols), lambda i: (i, 0))
    return pl.pallas_call(
        body, name="pool_forward", grid=(nt,),
        out_shape=(jax.ShapeDtypeStruct((s, D), F32), jax.ShapeDtypeStruct((s, D), F32),
                   jax.ShapeDtypeStruct((s, D), F32), jax.ShapeDtypeStruct((s, D), BF16)),
        in_specs=[row(D), _full((1, D)), _full((D, 2 * D)), _full((GROUPS, GROUP_DIM, GROUP_DIM)),
                  _full((1, D)), _full((1, D)), _full((D, D))],
        out_specs=(row(D), row(D), row(D), row(D)),
        scratch_shapes=[pltpu.VMEM((HALO + ts, D), F32)],
        compiler_params=_params("arbitrary"),
    )(x, w0, wpi, gw, gb, scale, wpo)


def pool_backward(x, dh1, pooled, gt, w0, wpi, gw, gb, scale, wpo):
    s = x.shape[0]
    ts = ROW_TILE
    nt = s // ts

    def body(x_ref, dh1_ref, pooled_ref, gt_ref, w0_ref, wpi_ref, gw_ref, gb_ref, sc_ref, wpo_ref,
             dx_ref, y_ref, dproj_ref, ggw_ref, small_ref, ebuf):
        i = pl.program_id(0)

        @pl.when(i == 0)
        def _():
            ggw_ref[...] = jnp.zeros_like(ggw_ref)
            small_ref[...] = jnp.zeros_like(small_ref)
            ebuf[ts:ts + HALO, :] = jnp.zeros((HALO, D), F32)

        dh1 = dh1_ref[...]
        gt = gt_ref[...]
        sc = sc_ref[...]
        dy = _nt(_bf(dh1), wpo_ref[...])
        pooled_bf = []
        mixed = []
        for g in range(GROUPS):
            cols = slice(g * GROUP_DIM, (g + 1) * GROUP_DIM)
            pb = _bf(pooled_ref[:, cols])
            pooled_bf.append(pb)
            mixed.append(_nn(pb, gw_ref[g]))
        mixed = jnp.concatenate(mixed, axis=-1) + gb_ref[...]
        sg = _sigmoid(gt)
        silu = gt * sg
        y_ref[...] = _bf(mixed * sc * silu)
        dmixed = dy * sc * silu
        dgt = dy * mixed * sc * (sg * (1.0 + gt * (1.0 - sg)))
        dproj_ref[:, D:] = _bf(dgt)
        small_ref[1:2, :] += jnp.sum(dy * mixed * silu, axis=0, keepdims=True)
        small_ref[2:3, :] += jnp.sum(dmixed, axis=0, keepdims=True)

        inv = _inverse_counts(_row_index(nt - 1 - i, ts))
        dpooled = []
        for g in range(GROUPS):
            cols = slice(g * GROUP_DIM, (g + 1) * GROUP_DIM)
            dm = _bf(dmixed[:, cols])
            ggw_ref[g] += _tn(pooled_bf[g], dm)
            dp = _nt(dm, gw_ref[g])
            dpooled.append(dp)
            ebuf[0:ts, cols] = dp * inv[g]
        du = []
        for g, w in enumerate(POOL_WINDOWS):
            cols = slice(g * GROUP_DIM, (g + 1) * GROUP_DIM)
            acc = -dpooled[g]
            for j in range(w):
                acc = acc + ebuf[j:j + ts, cols]
            du.append(acc)
        ebuf[ts:ts + HALO, :] = ebuf[0:HALO, :]
        du = _bf(jnp.concatenate(du, axis=-1))
        dproj_ref[:, :D] = du
        dn0 = _nt(du, wpi_ref[:, :D]) + _nt(_bf(dgt), wpi_ref[:, D:])

        xv = x_ref[...]
        r = lax.rsqrt(jnp.mean(xv * xv, axis=-1, keepdims=True) + EPS)
        xhat = xv * r
        small_ref[0:1, :] += jnp.sum(dn0 * xhat, axis=0, keepdims=True)
        dxh = dn0 * w0_ref[...]
        dx_ref[...] = dh1 + r * (dxh - xhat * jnp.mean(dxh * xhat, axis=-1, keepdims=True))

    row = lambda cols: pl.BlockSpec((ts, cols), lambda i: (nt - 1 - i, 0))
    return pl.pallas_call(
        body, name="pool_backward", grid=(nt,),
        out_shape=(jax.ShapeDtypeStruct((s, D), F32), jax.ShapeDtypeStruct((s, D), BF16),
                   jax.ShapeDtypeStruct((s, 2 * D), BF16),
                   jax.ShapeDtypeStruct((GROUPS, GROUP_DIM, GROUP_DIM), F32),
                   jax.ShapeDtypeStruct((8, D), F32)),
        in_specs=[row(D), row(D), row(D), row(D), _full((1, D)), _full((D, 2 * D)),
                  _full((GROUPS, GROUP_DIM, GROUP_DIM)), _full((1, D)), _full((1, D)), _full((D, D))],
        out_specs=(row(D), row(D), row(2 * D), _full((GROUPS, GROUP_DIM, GROUP_DIM)), _full((8, D))),
        scratch_shapes=[pltpu.VMEM((ts + HALO, D), F32)],
        compiler_params=_params("arbitrary"),
    )(x, dh1, pooled, gt, w0, wpi, gw, gb, scale, wpo)


def gla_project(h1, w1, wgi, wlow, wgk, bgk):
    s = h1.shape[0]
    ts = ROW_TILE

    def body(h_ref, w1_ref, wgi_ref, wlow_ref, wgk_ref, bgk_ref, proj_ref, low_ref, lg_ref, n1_ref):
        hv = h_ref[...]
        r = lax.rsqrt(jnp.mean(hv * hv, axis=-1, keepdims=True) + EPS)
        n1 = _bf(hv * r * w1_ref[...])
        n1_ref[...] = n1
        proj_ref[...] = _nn(n1, wgi_ref[...])
        low = _nn(n1, wlow_ref[...])
        low_ref[...] = low
        z = _nn(_bf(low), wgk_ref[...]) + bgk_ref[...]
        lg_ref[...] = (jnp.minimum(z, 0.0) - jnp.log(1.0 + jnp.exp(-jnp.abs(z)))) / GATE_NORM

    row = lambda cols: pl.BlockSpec((ts, cols), lambda i: (i, 0))
    return pl.pallas_call(
        body, name="gla_project", grid=(s // ts,),
        out_shape=(jax.ShapeDtypeStruct((s, GLA_MAIN), F32), jax.ShapeDtypeStruct((s, RANK_PAD), F32),
                   jax.ShapeDtypeStruct((s, KEY_W), F32), jax.ShapeDtypeStruct((s, D), BF16)),
        in_specs=[row(D), _full((1, D)), _full((D, GLA_MAIN)), _full((D, RANK_PAD)),
                  _full((RANK_PAD, KEY_W)), _full((1, KEY_W))],
        out_specs=(row(GLA_MAIN), row(RANK_PAD), row(KEY_W), row(D)),
        compiler_params=_params("parallel"),
    )(h1, w1, wgi, wlow, wgk, bgk)


GLA_BLOCK = 512
CHUNKS_PER_BLOCK = GLA_BLOCK // CHUNK


def _chunk_masks():
    t = lax.broadcasted_iota(jnp.int32, (CHUNK, CHUNK), 0)
    u = lax.broadcasted_iota(jnp.int32, (CHUNK, CHUNK), 1)
    return t >= u, t <= u


def _gla_chunk_terms(q, k, lg, lower_f):
    cum = _nn_exact(lower_f, lg)
    ep = jnp.exp(cum)
    en = jnp.exp(-cum)
    qs = q * (HEAD_K ** -0.5)
    last = cum[CHUNK - 1:CHUNK, :]
    ed = jnp.exp(last - cum)
    dec = jnp.exp(last)
    return cum, ep, en, qs, ed, dec


def gla_forward(proj, lg):
    s = proj.shape[0]
    nb = s // GLA_BLOCK
    nc = s // CHUNK

    def body(q_ref, k_ref, v_ref, lg_ref, o_ref, st_ref, state):
        @pl.when(pl.program_id(0) == 0)
        def _():
            state[...] = jnp.zeros_like(state)

        lower, _ = _chunk_masks()
        lower_f = lower.astype(F32)

        def chunk(cc, carry):
            rows = pl.ds(pl.multiple_of(cc * CHUNK, CHUNK), CHUNK)
            for h in range(HEADS):
                kc = slice(h * HEAD_K, (h + 1) * HEAD_K)
                vc = slice(h * HEAD_V, (h + 1) * HEAD_V)
                q = q_ref[rows, kc]
                k = k_ref[rows, kc]
                v = _bf(v_ref[rows, vc])
                _, ep, en, qs, ed, dec = _gla_chunk_terms(q, k, lg_ref[rows, kc], lower_f)
                a = _bf(qs * ep)
                fwd = _nt(a, _bf(k * en))
                bwd = _nt(_bf(qs * en), _bf(k * ep))
                scores = jnp.where(lower, fwd, bwd)
                st = state[h]
                st_ref[cc, h] = st
                o_ref[rows, vc] = _nn(_bf(scores), v) + _nt(a, _bf(st))
                state[h] = st * dec + _tn(v, _bf(k * ed))
            return carry

        lax.fori_loop(0, CHUNKS_PER_BLOCK, chunk, 0)

    return pl.pallas_call(
        body, name="gla_forward", grid=(nb,),
        out_shape=(jax.ShapeDtypeStruct((s, D), F32),
                   jax.ShapeDtypeStruct((nc, HEADS, HEAD_V, HEAD_K), F32)),
        in_specs=[pl.BlockSpec((GLA_BLOCK, KEY_W), lambda i: (i, 0)),
                  pl.BlockSpec((GLA_BLOCK, KEY_W), lambda i: (i, 1)),
                  pl.BlockSpec((GLA_BLOCK, D), lambda i: (i, 1)),
                  pl.BlockSpec((GLA_BLOCK, KEY_W), lambda i: (i, 0))],
        out_specs=(pl.BlockSpec((GLA_BLOCK, D), lambda i: (i, 0)),
                   pl.BlockSpec((CHUNKS_PER_BLOCK, HEADS, HEAD_V, HEAD_K), lambda i: (i, 0, 0, 0))),
        scratch_shapes=[pltpu.VMEM((HEADS, HEAD_V, HEAD_K), F32)],
        compiler_params=_params("arbitrary"),
    )(proj, proj, proj, lg)


def gla_backward(proj, lg, do, states):
    s = proj.shape[0]
    nb = s // GLA_BLOCK

    def body(q_ref, k_ref, v_ref, lg_ref, do_ref, st_ref, dq_ref, dk_ref, dv_ref, dlg_ref, dstate):
        @pl.when(pl.program_id(0) == 0)
        def _():
            dstate[...] = jnp.zeros_like(dstate)

        lower, upper = _chunk_masks()
        lower_f = lower.astype(F32)
        upper_f = upper.astype(F32)
        is_last = lax.broadcasted_iota(jnp.int32, (CHUNK, HEAD_K), 0) == CHUNK - 1

        def chunk(step, carry):
            cc = CHUNKS_PER_BLOCK - 1 - step
            rows = pl.ds(pl.multiple_of(cc * CHUNK, CHUNK), CHUNK)
            for h in range(HEADS):
                kc = slice(h * HEAD_K, (h + 1) * HEAD_K)
                vc = slice(h * HEAD_V, (h + 1) * HEAD_V)
                q = q_ref[rows, kc]
                k = k_ref[rows, kc]
                v = _bf(v_ref[rows, vc])
                do_c = _bf(do_ref[rows, vc])
                _, ep, en, qs, ed, dec = _gla_chunk_terms(q, k, lg_ref[rows, kc], lower_f)
                a = _bf(qs * ep)
                b = _bf(k * en)
                c = _bf(qs * en)
                dk_dec = _bf(k * ep)
                kd = _bf(k * ed)
                scores = _bf(jnp.where(lower, _nt(a, b), _nt(c, dk_dec)))
                st = st_ref[cc, h]
                dst = dstate[h]
                dst_bf = _bf(dst)

                dscores = _nt(do_c, v)
                dfwd = _bf(jnp.where(lower, dscores, 0.0))
                dbwd = _bf(jnp.where(lower, 0.0, dscores))
                dv_ref[rows, vc] = _tn(scores, do_c) + _nt(kd, dst_bf)
                da = _nn(dfwd, b) + _nn(do_c, _bf(st))
                db = _tn(dfwd, a)
                dc = _nn(dbwd, dk_dec)
                ddk = _tn(dbwd, c)
                dkd = _nn(v, dst_bf)
                ddec = jnp.sum(dst * st, axis=0, keepdims=True)
                dstate[h] = dst * dec + _tn(do_c, a)

                m = dkd * k * ed
                dq_ref[rows, kc] = (da * ep + dc * en) * (HEAD_K ** -0.5)
                dk_ref[rows, kc] = db * en + ddk * ep + dkd * ed
                dcum = (da * qs + ddk * k) * ep - (db * k + dc * qs) * en - m
                dlast = jnp.sum(m, axis=0, keepdims=True) + ddec * dec
                dcum = dcum + jnp.where(is_last, dlast, 0.0)
                dlg_ref[rows, kc] = _nn_exact(upper_f, dcum)
            return carry

        lax.fori_loop(0, CHUNKS_PER_BLOCK, chunk, 0)

    rev = lambda cols, col_block: pl.BlockSpec((GLA_BLOCK, cols), lambda i: (nb - 1 - i, col_block))
    return pl.pallas_call(
        body, name="gla_backward", grid=(nb,),
        out_shape=(jax.ShapeDtypeStruct((s, KEY_W), F32), jax.ShapeDtypeStruct((s, KEY_W), F32),
                   jax.ShapeDtypeStruct((s, D), F32), jax.ShapeDtypeStruct((s, KEY_W), F32)),
        in_specs=[rev(KEY_W, 0), rev(KEY_W, 1), rev(D, 1), rev(KEY_W, 0), rev(D, 0),
                  pl.BlockSpec((CHUNKS_PER_BLOCK, HEADS, HEAD_V, HEAD_K), lambda i: (nb - 1 - i, 0, 0, 0))],
        out_specs=(rev(KEY_W, 0), rev(KEY_W, 0), rev(D, 0), rev(KEY_W, 0)),
        scratch_shapes=[pltpu.VMEM((HEADS, HEAD_V, HEAD_K), F32)],
        compiler_params=_params("arbitrary"),
    )(proj, proj, proj, lg, do, states)


def head_and_loss(o, proj, h1, target, hw, wgo, wf):
    s = o.shape[0]
    ts = ROW_TILE

    def body(o_ref, gate_ref, h1_ref, tgt_ref, hw_ref, wgo_ref, wf_ref,
             dh2_ref, dh2bf_ref, do_ref, dgate_ref, y2_ref, small_ref):
        @pl.when(pl.program_id(0) == 0)
        def _():
            small_ref[...] = jnp.zeros_like(small_ref)

        gate = gate_ref[...]
        hw = hw_ref[...]
        sg = _sigmoid(gate)
        silu = gate * sg
        ohat, ro = [], []
        for h in range(HEADS):
            oh = o_ref[:, h * HEAD_V:(h + 1) * HEAD_V]
            rh = lax.rsqrt(jnp.mean(oh * oh, axis=-1, keepdims=True) + EPS)
            ro.append(rh)
            ohat.append(oh * rh)
        ohat = jnp.concatenate(ohat, axis=-1)
        on = ohat * hw
        y2 = _bf(on * silu)
        y2_ref[...] = y2
        h2 = h1_ref[...] + _nn(y2, wgo_ref[...])
        rf = lax.rsqrt(jnp.mean(h2 * h2, axis=-1, keepdims=True) + EPS)
        h2hat = h2 * rf
        wf = wf_ref[...]
        diff = h2hat * wf - tgt_ref[...]
        small_ref[2:3, :] += jnp.zeros((1, D), F32) + 0.5 * jnp.sum(diff * diff) / D
        dout = diff / D
        small_ref[0:1, :] += jnp.sum(dout * h2hat, axis=0, keepdims=True)
        dxh = dout * wf
        dh2 = rf * (dxh - h2hat * jnp.mean(dxh * h2hat, axis=-1, keepdims=True))
        dh2_ref[...] = dh2
        dh2_bf = _bf(dh2)
        dh2bf_ref[...] = dh2_bf
        dy2 = _nt(dh2_bf, wgo_ref[...])
        don = dy2 * silu
        dgate_ref[...] = dy2 * on * (sg * (1.0 + gate * (1.0 - sg)))
        ghw = jnp.sum(don * ohat, axis=0, keepdims=True)
        small_ref[1:2, 0:HEAD_V] += sum(ghw[:, h * HEAD_V:(h + 1) * HEAD_V] for h in range(HEADS))
        dohat = don * hw
        for h in range(HEADS):
            cols = slice(h * HEAD_V, (h + 1) * HEAD_V)
            oh, dh = ohat[:, cols], dohat[:, cols]
            do_ref[:, cols] = ro[h] * (dh - oh * jnp.mean(dh * oh, axis=-1, keepdims=True))

    row = lambda cols: pl.BlockSpec((ts, cols), lambda i: (i, 0))
    act = jax.ShapeDtypeStruct((s, D), F32)
    act_bf = jax.ShapeDtypeStruct((s, D), BF16)
    return pl.pallas_call(
        body, name="head_and_loss", grid=(s // ts,),
        out_shape=(act, act_bf, act, act, act_bf, jax.ShapeDtypeStruct((8, D), F32)),
        in_specs=[row(D), pl.BlockSpec((ts, D), lambda i: (i, 2)), row(D), row(D),
                  _full((1, D)), _full((D, D)), _full((1, D))],
        out_specs=(row(D), row(D), row(D), row(D), row(D), _full((8, D))),
        compiler_params=_params("arbitrary"),
    )(o, proj, h1, target, hw, wgo, wf)


def gla_project_backward(dq, dk, dv, dgate, dlg, low, h1, dh2, w1, wgi, wlow, wgk, bgk):
    s = h1.shape[0]
    ts = ROW_TILE

    def body(dq_ref, dk_ref, dv_ref, dgate_ref, dlg_ref, low_ref, h1_ref, dh2_ref, w1_ref,
             wgi_ref, wlow_ref, wgk_ref, bgk_ref, dh1_ref, dh1bf_ref, dproj_ref, dlow_ref, ggk_ref,
             small_ref):
        @pl.when(pl.program_id(0) == 0)
        def _():
            ggk_ref[...] = jnp.zeros_like(ggk_ref)
            small_ref[...] = jnp.zeros_like(small_ref)

        low = _bf(low_ref[...])
        z = _nn(low, wgk_ref[...]) + bgk_ref[...]
        dz = dlg_ref[...] * (1.0 / GATE_NORM) * _sigmoid(-z)
        dz_bf = _bf(dz)
        ggk_ref[...] += _tn(low, dz_bf)
        small_ref[1:2, 0:KEY_W] += jnp.sum(dz, axis=0, keepdims=True)
        dlow = _bf(_nt(dz_bf, wgk_ref[...]))
        dlow_ref[...] = dlow
        dn1 = _nt(dlow, wlow_ref[...])
        for ref, lo, hi in ((dq_ref, 0, KEY_W), (dk_ref, KEY_W, 2 * KEY_W),
                            (dv_ref, 2 * KEY_W, 2 * KEY_W + D), (dgate_ref, 2 * KEY_W + D, GLA_MAIN)):
            piece = _bf(ref[...])
            dproj_ref[:, lo:hi] = piece
            dn1 = dn1 + _nt(piece, wgi_ref[:, lo:hi])
        hv = h1_ref[...]
        r = lax.rsqrt(jnp.mean(hv * hv, axis=-1, keepdims=True) + EPS)
        hhat = hv * r
        small_ref[0:1, :] += jnp.sum(dn1 * hhat, axis=0, keepdims=True)
        dxh = dn1 * w1_ref[...]
        dh1 = dh2_ref[...] + r * (dxh - hhat * jnp.mean(dxh * hhat, axis=-1, keepdims=True))
        dh1_ref[...] = dh1
        dh1bf_ref[...] = _bf(dh1)

    row = lambda cols: pl.BlockSpec((ts, cols), lambda i: (i, 0))
    return pl.pallas_call(
        body, name="gla_project_backward", grid=(s // ts,),
        out_shape=(jax.ShapeDtypeStruct((s, D), F32), jax.ShapeDtypeStruct((s, D), BF16),
                   jax.ShapeDtypeStruct((s, GLA_MAIN), BF16),
                   jax.ShapeDtypeStruct((s, RANK_PAD), BF16), jax.ShapeDtypeStruct((RANK_PAD, KEY_W), F32),
                   jax.ShapeDtypeStruct((8, D), F32)),
        in_specs=[row(KEY_W), row(KEY_W), row(D), row(D), row(KEY_W), row(RANK_PAD), row(D), row(D),
                  _full((1, D)), _full((D, GLA_MAIN)), _full((D, RANK_PAD)), _full((RANK_PAD, KEY_W)),
                  _full((1, KEY_W))],
        out_specs=(row(D), row(D), row(GLA_MAIN), row(RANK_PAD), _full((RANK_PAD, KEY_W)), _full((8, D))),
        compiler_params=_params("arbitrary"),
    )(dq, dk, dv, dgate, dlg, low, h1, dh2, w1, wgi, wlow, wgk, bgk)


def _pack_quarter(pool_in, group_w, pool_out, gla_in, gla_out, dtype):
    rows = [pool_in.reshape(ROWS_PI, D), group_w.reshape(ROWS_GW, D), pool_out,
            gla_in.reshape(ROWS_GI, D), gla_out]
    rows = [r.astype(dtype) for r in rows]
    rows.append(jnp.zeros((PACK_ROWS - ROWS_USED, D), dtype))
    return jnp.concatenate(rows, axis=0)


def _unpack_quarter(packed):
    return (packed[OFF_PI:OFF_GW].reshape(D, 512),
            packed[OFF_GW:OFF_PO].reshape(GROUPS, 64, GROUP_DIM),
            packed[OFF_PO:OFF_GI],
            packed[OFF_GI:OFF_GO].reshape(D, 772),
            packed[OFF_GO:ROWS_USED])


def _unpack_all(gathered):
    parts = [_unpack_quarter(gathered[q]) for q in range(N_CHIPS)]
    cat = lambda i, axis: jnp.concatenate([p[i] for p in parts], axis=axis)
    return cat(0, 1), cat(1, 1), cat(2, 0), cat(3, 1), cat(4, 0)


def _pack_all(g_pool_in, g_group_w, g_pool_out, g_gla_in, g_gla_out):
    quarters = []
    for q in range(N_CHIPS):
        quarters.append(_pack_quarter(
            g_pool_in[:, 512 * q:512 * (q + 1)], g_group_w[:, 64 * q:64 * (q + 1), :],
            g_pool_out[256 * q:256 * (q + 1)], g_gla_in[:, 772 * q:772 * (q + 1)],
            g_gla_out[256 * q:256 * (q + 1)], F32))
    return jnp.stack(quarters)


def _pad_row(*pieces):
    flat = jnp.concatenate([p.reshape(-1).astype(F32) for p in pieces])
    return jnp.pad(flat, (0, D - flat.shape[0])).reshape(1, D)


def local_gradients(xs, target, w0, w1, wf, wpi, gw, gb, scale, wpo, wgi, wlow, wgk, bgk, hw_tiled, wgo):
    h1, pooled, gt, n0 = pool_forward(xs, w0, wpi, gw, gb, scale, wpo)
    proj, low, lg, n1 = gla_project(h1, w1, wgi, wlow, wgk, bgk)
    o, states = gla_forward(proj, lg)

    dh2, dh2_bf, do, dgate, y2, small_top = head_and_loss(o, proj, h1, target, hw_tiled, wgo, wf)
    g_gla_out = matmul_tn(y2, dh2_bf, "grad_gla_out")
    dq, dk, dv, dlg = gla_backward(proj, lg, do, states)
    dh1, dh1_bf, dproj, dlow, g_gk_pad, small_gla = gla_project_backward(
        dq, dk, dv, dgate, dlg, low, h1, dh2, w1, wgi, wlow, wgk, bgk)
    g_gla_in = jnp.concatenate([matmul_tn(n1, dproj, "grad_gla_in"),
                                matmul_tn(n1, dlow, "grad_gla_low")[:, :GATE_RANK]], axis=1)
    dx, y, dpool, g_group_w, small_pool = pool_backward(
        xs, dh1, pooled, gt, w0, wpi, gw, gb, scale, wpo)
    g_pool_out = matmul_tn(y, dh1_bf, "grad_pool_out")
    g_pool_in = matmul_tn(n0, dpool, "grad_pool_in")
    return (dx, g_pool_in, g_group_w, g_pool_out, g_gla_in, g_gla_out, g_gk_pad,
            small_top, small_gla, small_pool)


def kernel(x, norm_w, pool_in_w, pool_group_w, pool_group_b, pool_scale, pool_out_w, gla_in_w, gla_gk_w, gla_gk_b, gla_head_norm_w, gla_out_w, final_norm_w, loss_target, m_norm_w, m_pool_in_w, m_pool_group_w, m_pool_group_b, m_pool_scale, m_pool_out_w, m_gla_in_w, m_gla_gk_w, m_gla_gk_b, m_gla_head_norm_w, m_gla_out_w, m_final_norm_w, v_norm_w, v_pool_in_w, v_pool_group_w, v_pool_group_b, v_pool_scale, v_pool_out_w, v_gla_in_w, v_gla_gk_w, v_gla_gk_b, v_gla_head_norm_w, v_gla_out_w, v_final_norm_w):
    s = x.shape[1]
    xs = x[0]
    target = loss_target[0]
    q_chip = 2 * lax.axis_index("x") + lax.axis_index("y")

    packed = _pack_quarter(pool_in_w[0], pool_group_w[0], pool_out_w[0], gla_in_w[0], gla_out_w[0], BF16)
    wpi, gw, wpo, wgi_all, wgo = _unpack_all(allgather_weights(packed))
    wgi = wgi_all[:, :GLA_MAIN]
    wlow = jnp.pad(wgi_all[:, GLA_MAIN:], ((0, 0), (0, RANK_PAD - GATE_RANK)))

    small_in = jnp.concatenate([
        _pad_row(gla_gk_b[0], gla_head_norm_w[0], pool_group_b[0]),
        gla_gk_w[0].reshape(2, D),
        jnp.zeros((5, D), F32)], axis=0)
    small_all = gather_small(small_in, "gather_small_weights")[0::2]
    bgk = small_all[:, 0, 0:128].reshape(1, KEY_W)
    hw = small_all[:, 0, 128:192].reshape(1, HEAD_V)
    gb = jnp.concatenate([small_all[q, 0, 192:448].reshape(GROUPS, 64) for q in range(N_CHIPS)],
                         axis=1).reshape(1, D)
    wgk16 = jnp.concatenate([small_all[q, 1:3].reshape(GATE_RANK, 128) for q in range(N_CHIPS)], axis=1)
    wgk = _bf(jnp.pad(wgk16, ((0, RANK_PAD - GATE_RANK), (0, 0))))
    hw_tiled = jnp.tile(hw, (1, HEADS))

    w0 = norm_w[0:1]
    w1 = norm_w[1:2]
    wf = final_norm_w.reshape(1, D)

    (dx, g_pool_in, g_group_w, g_pool_out, g_gla_in, g_gla_out, g_gk_pad,
     small_top, small_gla, small_pool) = local_gradients(
        xs, target, w0, w1, wf, wpi, gw, gb, pool_scale, wpo, wgi, wlow, wgk, bgk, hw_tiled, wgo)

    mine, theirs = exchange_with_sibling(_pack_all(g_pool_in, g_group_w, g_pool_out, g_gla_in, g_gla_out))
    chip_sum, chip_sum_bf = add_halves(mine, theirs)
    own, got = scatter_to_owners(chip_sum, chip_sum_bf)
    reduced = join_halves(add_parts(own, got))
    r_pool_in, r_group_w, r_pool_out, r_gla_in, r_gla_out = _unpack_quarter(reduced)

    small_out = jnp.concatenate([
        small_pool[0:1], small_gla[0:1],
        small_pool[1:2],
        small_top[0:1],
        small_top[2:3],
        _pad_row(small_gla[1, 0:KEY_W], small_top[1, 0:HEAD_V]),
        small_pool[2:3],
        g_gk_pad[:GATE_RANK].reshape(8, D),
        jnp.zeros((1, D), F32)], axis=0)
    total = gather_small(small_out, "allreduce_small_grads", reduce=True)
    loss = total[4, 0]
    g_norm = total[0:2]
    g_scale = total[2:3]
    g_final = total[3]
    pick = lambda full, width: lax.dynamic_slice_in_dim(full, q_chip * width, width, axis=-1)
    g_gk_b = pick(total[5:6, 0:KEY_W], 128)
    g_hnw = pick(total[5:6, KEY_W:KEY_W + HEAD_V], 64)
    g_group_b = pick(total[6].reshape(GROUPS, GROUP_DIM), 64)[None]
    g_gk_w = pick(total[7:15].reshape(GATE_RANK, KEY_W), 128)[None]

    def step(name, w, g, m, v):
        shape = w.shape
        as2d = lambda a: a.reshape(-1, shape[-1])
        d, nm, nv = adamw(as2d(w), as2d(g), as2d(m), as2d(v), "adamw_" + name)
        return g.reshape(shape), d.reshape(shape), nm.reshape(shape), nv.reshape(shape)

    results = [
        step("norm_w", norm_w, g_norm, m_norm_w, v_norm_w),
        step("pool_in_w", pool_in_w, r_pool_in[None], m_pool_in_w, v_pool_in_w),
        step("pool_group_w", pool_group_w, r_group_w[None], m_pool_group_w, v_pool_group_w),
        step("pool_group_b", pool_group_b, g_group_b, m_pool_group_b, v_pool_group_b),
        step("pool_scale", pool_scale, g_scale, m_pool_scale, v_pool_scale),
        step("pool_out_w", pool_out_w, r_pool_out[None], m_pool_out_w, v_pool_out_w),
        step("gla_in_w", gla_in_w, r_gla_in[None], m_gla_in_w, v_gla_in_w),
        step("gla_gk_w", gla_gk_w, g_gk_w, m_gla_gk_w, v_gla_gk_w),
        step("gla_gk_b", gla_gk_b, g_gk_b, m_gla_gk_b, v_gla_gk_b),
        step("gla_head_norm_w", gla_head_norm_w, g_hnw, m_gla_head_norm_w, v_gla_head_norm_w),
        step("gla_out_w", gla_out_w, r_gla_out[None], m_gla_out_w, v_gla_out_w),
        step("final_norm_w", final_norm_w, g_final, m_final_norm_w, v_final_norm_w),
    ]
    grads, deltas, new_m, new_v = zip(*results)
    return (loss, dx[None], *grads, *deltas, *new_m, *new_v)
```

```python
import functools

import jax
import jax.numpy as jnp
from jax import lax
from jax.experimental import pallas as pl
from jax.experimental.pallas import tpu as pltpu

F32 = jnp.float32
BF16 = jnp.bfloat16
MESH = pl.DeviceIdType.MESH

D = 1024
POOL_WINDOWS = (2, 4, 8, 16)
GROUPS = 4
GROUP_DIM = 256
HEADS = 4
HEAD_K = 128
HEAD_V = 256
KEY_W = 512
CHUNK = 64
GATE_RANK = 16
GATE_NORM = 16.0
GLA_IN = 3088
GLA_MAIN = 3072
RANK_PAD = 128
EPS = 1e-6
HALO = 16

ADAM_LR = 0.001
ADAM_B1 = 0.9
ADAM_B2 = 0.999
ADAM_EPS = 1e-08
ADAM_WD = 0.01
ADAM_STEP = 10

N_CHIPS = 4
N_DEV = 8
GLA_IN_QUARTER = GLA_IN // N_CHIPS

VMEM_LIMIT = 56 * 1024 * 1024


def _nn(a, b):
    return lax.dot_general(a, b, (((1,), (0,)), ((), ())), preferred_element_type=F32)


def _nt(a, b):
    return lax.dot_general(a, b, (((1,), (1,)), ((), ())), preferred_element_type=F32)


def _tn(a, b):
    return lax.dot_general(a, b, (((0,), (0,)), ((), ())), preferred_element_type=F32)


def _nn_exact(a, b):
    return lax.dot_general(a, b, (((1,), (0,)), ((), ())), preferred_element_type=F32,
                           precision=lax.Precision.HIGHEST)


def _bf(a):
    return a.astype(BF16)


def _params(*sem):
    return pltpu.CompilerParams(dimension_semantics=sem, vmem_limit_bytes=VMEM_LIMIT)


def _full(shape):
    return pl.BlockSpec(shape, lambda i: (0,) * len(shape))


def _position():
    return lax.axis_index("x"), lax.axis_index("y"), lax.axis_index("c")


def gather_small(block, name, reduce=False):
    rows, cols = block.shape

    def body(in_ref, out_ref, *scratch):
        if reduce:
            all_ref, send_sems, recv_sems, local_sem = scratch
        else:
            all_ref = out_ref
            send_sems, recv_sems, local_sem = scratch
        x, y, c = _position()
        me = 4 * x + 2 * y + c
        mine = pltpu.make_async_copy(in_ref, all_ref.at[me], local_sem)
        mine.start()
        sends = []
        for k in range(N_DEV - 1):
            fx, fy, fc = (k + 1) >> 2 & 1, (k + 1) >> 1 & 1, (k + 1) & 1
            cp = pltpu.make_async_remote_copy(
                src_ref=in_ref, dst_ref=all_ref.at[me],
                send_sem=send_sems.at[k], recv_sem=recv_sems.at[k],
                device_id=(x ^ fx, y ^ fy, c ^ fc), device_id_type=MESH)
            cp.start()
            sends.append(cp)
        for k in range(N_DEV - 1):
            fx, fy, fc = (k + 1) >> 2 & 1, (k + 1) >> 1 & 1, (k + 1) & 1
            src_dev = 4 * (x ^ fx) + 2 * (y ^ fy) + (c ^ fc)
            pltpu.make_async_remote_copy(
                src_ref=in_ref, dst_ref=all_ref.at[src_dev],
                send_sem=send_sems.at[k], recv_sem=recv_sems.at[k],
                device_id=(x, y, c), device_id_type=MESH).wait_recv()
        for cp in sends:
            cp.wait_send()
        mine.wait()
        if reduce:
            total = all_ref[0]
            for dev in range(1, N_DEV):
                total = total + all_ref[dev]
            out_ref[...] = total

    sems = [pltpu.SemaphoreType.DMA((N_DEV - 1,)), pltpu.SemaphoreType.DMA((N_DEV - 1,)),
            pltpu.SemaphoreType.DMA]
    gathered = (N_DEV, rows, cols)
    return pl.pallas_call(
        body, name=name,
        out_shape=jax.ShapeDtypeStruct((rows, cols) if reduce else gathered, block.dtype),
        in_specs=[pl.BlockSpec(memory_space=pltpu.VMEM)],
        out_specs=pl.BlockSpec(memory_space=pltpu.VMEM),
        scratch_shapes=([pltpu.VMEM(gathered, block.dtype)] if reduce else []) + sems,
    )(block)


def _other_chips(x, y):
    return [(1 - x, y), (x, 1 - y), (1 - x, 1 - y)]


def _any_specs(n):
    return [pl.BlockSpec(memory_space=pl.ANY)] * n


def _halves(rows, c):
    half = rows // 2
    return pl.ds(c * half, half), pl.ds((1 - c) * half, half)


CAST_ROWS = 256


def allgather_weights(quarters):
    n = len(quarters)
    shapes = [w.shape for w in quarters]

    def body(*refs):
        w_refs, out_refs = refs[:n], refs[n:2 * n]
        f32_bufs, bf_bufs = refs[2 * n:3 * n], refs[3 * n:4 * n]
        send_sems, recv_sems, local_sems = refs[4 * n:]
        x, y, c = _position()
        q = 2 * x + y
        sibling = (x, y, 1 - c)
        chips = _other_chips(x, y)

        def copy(k, i, quarter, half, to, src=None):
            dst = out_refs[i].at[quarter, half]
            return pltpu.make_async_remote_copy(
                src_ref=dst if src is None else src, dst_ref=dst,
                send_sem=send_sems.at[k * n + i], recv_sem=recv_sems.at[k * n + i],
                device_id=to, device_id_type=MESH)

        loads = [pltpu.make_async_copy(w_refs[i], f32_bufs[i], local_sems.at[i]) for i in range(n)]
        for cp in loads:
            cp.start()
        keeps, sends = [], []
        for i in range(n):
            loads[i].wait()
            for r0 in range(0, shapes[i][0], CAST_ROWS):
                bf_bufs[i][r0:r0 + CAST_ROWS, :] = _bf(f32_bufs[i][r0:r0 + CAST_ROWS, :])
            keep = pltpu.make_async_copy(bf_bufs[i], out_refs[i].at[q], local_sems.at[n + i])
            keep.start()
            keeps.append(keep)
            mine, _ = _halves(shapes[i][0], c)
            for j, chip in enumerate(chips):
                cp = copy(j, i, q, mine, (*chip, c), src=bf_bufs[i].at[mine])
                cp.start()
                sends.append(cp)
        for j, chip in enumerate(chips):
            qj = 2 * chip[0] + chip[1]
            for i in range(n):
                mine, _ = _halves(shapes[i][0], c)
                copy(j, i, qj, mine, (x, y, c)).wait_recv()
                cp = copy(3 + j, i, qj, mine, sibling)
                cp.start()
                sends.append(cp)
        for j, chip in enumerate(chips):
            qj = 2 * chip[0] + chip[1]
            for i in range(n):
                _, other = _halves(shapes[i][0], c)
                copy(3 + j, i, qj, other, (x, y, c)).wait_recv()
        for cp in sends:
            cp.wait_send()
        for cp in keeps:
            cp.wait()

    return pl.pallas_call(
        body, name="allgather_weights",
        out_shape=[jax.ShapeDtypeStruct((N_CHIPS, *s), BF16) for s in shapes],
        in_specs=_any_specs(n), out_specs=_any_specs(n),
        scratch_shapes=([pltpu.VMEM(s, F32) for s in shapes] + [pltpu.VMEM(s, BF16) for s in shapes]
                        + [pltpu.SemaphoreType.DMA((6 * n,)), pltpu.SemaphoreType.DMA((6 * n,)),
                           pltpu.SemaphoreType.DMA((2 * n,))]),
        compiler_params=pltpu.CompilerParams(vmem_limit_bytes=VMEM_LIMIT),
    )(*quarters)


def exchange_with_sibling(grads):
    n = len(grads)

    def body(*refs):
        g_refs, theirs_refs = refs[:n], refs[n:2 * n]
        send_sems, recv_sems = refs[2 * n:]
        x, y, c = _position()
        copies = []
        for i in range(n):
            _, other = _halves(g_refs[i].shape[1], c)
            cp = pltpu.make_async_remote_copy(
                src_ref=g_refs[i].at[:, other], dst_ref=theirs_refs[i],
                send_sem=send_sems.at[i], recv_sem=recv_sems.at[i],
                device_id=(x, y, 1 - c), device_id_type=MESH)
            cp.start()
            copies.append(cp)
        for cp in copies:
            cp.wait()

    return pl.pallas_call(
        body, name="exchange_with_sibling",
        out_shape=[jax.ShapeDtypeStruct((N_CHIPS, g.shape[1] // 2, g.shape[2]), F32) for g in grads],
        in_specs=_any_specs(n), out_specs=_any_specs(n),
        scratch_shapes=[pltpu.SemaphoreType.DMA((n,)), pltpu.SemaphoreType.DMA((n,))],
    )(*grads)


def scatter_to_owners(chip_sums):
    n = len(chip_sums)

    def body(*refs):
        b_refs, got_refs = refs[:n], refs[n:2 * n]
        send_sems, recv_sems = refs[2 * n:]
        x, y, c = _position()
        copies = []
        for j, chip in enumerate(_other_chips(x, y)):
            qj = 2 * chip[0] + chip[1]
            for i in range(n):
                cp = pltpu.make_async_remote_copy(
                    src_ref=b_refs[i].at[qj], dst_ref=got_refs[i].at[j],
                    send_sem=send_sems.at[j * n + i], recv_sem=recv_sems.at[j * n + i],
                    device_id=(*chip, c), device_id_type=MESH)
                cp.start()
                copies.append(cp)
        for cp in copies:
            cp.wait()

    return pl.pallas_call(
        body, name="scatter_to_owners",
        out_shape=[jax.ShapeDtypeStruct((N_CHIPS - 1, *b.shape[1:]), BF16) for b in chip_sums],
        in_specs=_any_specs(n), out_specs=_any_specs(n),
        scratch_shapes=[pltpu.SemaphoreType.DMA((3 * n,)), pltpu.SemaphoreType.DMA((3 * n,))],
    )(*chip_sums)


def join_halves(reduced):
    n = len(reduced)

    def body(*refs):
        buf_refs = refs[n:2 * n]
        send_sems, recv_sems = refs[2 * n:]
        x, y, c = _position()
        copies = []
        for i in range(n):
            mine, _ = _halves(buf_refs[i].shape[0], c)
            cp = pltpu.make_async_remote_copy(
                src_ref=buf_refs[i].at[mine], dst_ref=buf_refs[i].at[mine],
                send_sem=send_sems.at[i], recv_sem=recv_sems.at[i],
                device_id=(x, y, 1 - c), device_id_type=MESH)
            cp.start()
            copies.append(cp)
        for cp in copies:
            cp.wait()

    return pl.pallas_call(
        body, name="join_halves",
        out_shape=[jax.ShapeDtypeStruct(r.shape, F32) for r in reduced],
        in_specs=_any_specs(n), out_specs=_any_specs(n),
        input_output_aliases={i: i for i in range(n)},
        scratch_shapes=[pltpu.SemaphoreType.DMA((n,)), pltpu.SemaphoreType.DMA((n,))],
    )(*reduced)


ADD_ROWS = 256


def add_halves(grad, theirs, place, name):
    _, half, cols = theirs.shape
    rb = min(ADD_ROWS, half)
    steps = half // rb

    def body(place_ref, a_ref, b_ref, f_ref, h_ref):
        s = a_ref[...] + b_ref[...]
        f_ref[...] = s
        h_ref[...] = _bf(s)

    spec = pl.BlockSpec((1, rb, cols), lambda i, j, place: (i, j, 0))
    return pl.pallas_call(
        body, name=name,
        grid_spec=pltpu.PrefetchScalarGridSpec(
            num_scalar_prefetch=1, grid=(N_CHIPS, steps),
            in_specs=[pl.BlockSpec((1, rb, cols), lambda i, j, place: (i, place[0] * steps + j, 0)), spec],
            out_specs=(spec, spec)),
        out_shape=(jax.ShapeDtypeStruct(theirs.shape, F32), jax.ShapeDtypeStruct(theirs.shape, BF16)),
        compiler_params=_params("parallel", "parallel"),
    )(place, grad, theirs)


def add_parts(chip_sum, got, place, name):
    _, half, cols = got.shape
    rb = min(ADD_ROWS, half)
    steps = half // rb

    def body(place_ref, o_ref, g_ref, out_ref):
        s = o_ref[0]
        for j in range(N_CHIPS - 1):
            s = s + g_ref[j].astype(F32)
        out_ref[...] = s

    return pl.pallas_call(
        body, name=name,
        grid_spec=pltpu.PrefetchScalarGridSpec(
            num_scalar_prefetch=1, grid=(steps,),
            in_specs=[pl.BlockSpec((1, rb, cols), lambda j, place: (place[1], j, 0)),
                      pl.BlockSpec((N_CHIPS - 1, rb, cols), lambda j, place: (0, j, 0))],
            out_specs=pl.BlockSpec((rb, cols), lambda j, place: (place[0] * steps + j, 0))),
        out_shape=jax.ShapeDtypeStruct((2 * half, cols), F32),
        compiler_params=_params("parallel"),
    )(place, chip_sum, got)


def _adam_math(w, g, m, v):
    m = ADAM_B1 * m + (1.0 - ADAM_B1) * g
    v = ADAM_B2 * v + (1.0 - ADAM_B2) * (g * g)
    m_hat = m / (1.0 - ADAM_B1 ** ADAM_STEP)
    v_hat = v / (1.0 - ADAM_B2 ** ADAM_STEP)
    delta = -ADAM_LR * (m_hat / (jnp.sqrt(v_hat) + ADAM_EPS) + ADAM_WD * w)
    return delta, m, v


def adamw(w, g, m, v, name):
    rows, cols = w.shape
    tile = 256 if rows % 256 == 0 else rows

    def body(w_ref, g_ref, m_ref, v_ref, d_ref, nm_ref, nv_ref):
        d, nm, nv = _adam_math(w_ref[...], g_ref[...], m_ref[...], v_ref[...])
        d_ref[...] = d
        nm_ref[...] = nm
        nv_ref[...] = nv

    spec = pl.BlockSpec((tile, cols), lambda i: (i, 0))
    shape = jax.ShapeDtypeStruct((rows, cols), F32)
    return pl.pallas_call(
        body, name=name, grid=(rows // tile,),
        out_shape=(shape, shape, shape),
        in_specs=[spec] * 4, out_specs=(spec, spec, spec),
        compiler_params=_params("parallel"),
    )(w, g, m, v)


def matmul_tn(a, b, name, tile_n=512, tile_s=512, by_column_tile=False):
    s, m = a.shape
    n = b.shape[1]
    tile_n = min(tile_n, n)
    steps = s // tile_s
    if by_column_tile:
        out_shape = jax.ShapeDtypeStruct((n // tile_n, m, tile_n), F32)
        out_spec = pl.BlockSpec((None, m, tile_n), lambda j, k: (j, 0, 0))
    else:
        out_shape = jax.ShapeDtypeStruct((m, n), F32)
        out_spec = pl.BlockSpec((m, tile_n), lambda j, k: (0, j))

    def body(a_ref, b_ref, out_ref):
        k = pl.program_id(1)

        @pl.when(k == 0)
        def _():
            out_ref[...] = jnp.zeros_like(out_ref)

        out_ref[...] += _tn(a_ref[...], b_ref[...])

    return pl.pallas_call(
        body, name=name, grid=(n // tile_n, steps),
        out_shape=out_shape,
        in_specs=[pl.BlockSpec((tile_s, m), lambda j, k: (k, 0)),
                  pl.BlockSpec((tile_s, tile_n), lambda j, k: (k, j))],
        out_specs=out_spec,
        compiler_params=_params("parallel", "arbitrary"),
    )(a, b)


ROW_TILE = 256


def _row_index(tile, rows):
    return tile * rows + lax.broadcasted_iota(jnp.int32, (rows, 1), 0)


def _inverse_counts(t_glob):
    return [1.0 / jnp.minimum(t_glob + 1, w).astype(F32) for w in POOL_WINDOWS]


def _sigmoid(z):
    return 1.0 / (1.0 + jnp.exp(-z))


def pool_forward(x, w0, wpi, gw, gb, scale, wpo):
    s = x.shape[0]
    ts = ROW_TILE
    nt = s // ts

    def body(x_ref, w0_ref, wpi_ref, gw_ref, gb_ref, sc_ref, wpo_ref,
             h1_ref, pooled_ref, gt_ref, n0_ref, ubuf):
        i = pl.program_id(0)
        xv = x_ref[...]
        r = lax.rsqrt(jnp.mean(xv * xv, axis=-1, keepdims=True) + EPS)
        n0 = _bf(xv * r * w0_ref[...])
        n0_ref[...] = n0
        u = jnp.concatenate([_nn(n0, wpi_ref[0]), _nn(n0, wpi_ref[1])], axis=-1)
        gt = jnp.concatenate([_nn(n0, wpi_ref[2]), _nn(n0, wpi_ref[3])], axis=-1)
        gt_ref[...] = gt

        @pl.when(i == 0)
        def _():
            ubuf[0:HALO, :] = jnp.zeros((HALO, D), F32)

        ubuf[HALO:HALO + ts, :] = u
        inv = _inverse_counts(_row_index(i, ts))
        mixed = []
        for g, w in enumerate(POOL_WINDOWS):
            cols = slice(g * GROUP_DIM, (g + 1) * GROUP_DIM)
            ug = u[:, cols]
            acc = ug
            for j in range(1, w):
                acc = acc + ubuf[HALO - j:HALO - j + ts, cols]
            pooled = acc * inv[g] - ug
            pooled_ref[:, cols] = pooled
            mixed.append(_nn(_bf(pooled), gw_ref[g]))
        ubuf[0:HALO, :] = ubuf[ts:ts + HALO, :]
        mixed = jnp.concatenate(mixed, axis=-1) + gb_ref[...]
        y = mixed * sc_ref[...] * (gt * _sigmoid(gt))
        h1_ref[...] = xv + _nn(_bf(y), wpo_ref[...])

    row = lambda cols: pl.BlockSpec((ts, cols), lambda i: (i, 0))
    return pl.pallas_call(
        body, name="pool_forward", grid=(nt,),
        out_shape=(jax.ShapeDtypeStruct((s, D), F32), jax.ShapeDtypeStruct((s, D), F32),
                   jax.ShapeDtypeStruct((s, D), F32), jax.ShapeDtypeStruct((s, D), BF16)),
        in_specs=[row(D), _full((1, D)), _full((N_CHIPS, D, D // 2)), _full((GROUPS, GROUP_DIM, GROUP_DIM)),
                  _full((1, D)), _full((1, D)), _full((D, D))],
        out_specs=(row(D), row(D), row(D), row(D)),
        scratch_shapes=[pltpu.VMEM((HALO + ts, D), F32)],
        compiler_params=_params("arbitrary"),
    )(x, w0, wpi, gw, gb, scale, wpo)


def pool_backward(x, dh1, pooled, gt, w0, wpi, gw, gb, scale, wpo):
    s = x.shape[0]
    ts = ROW_TILE
    nt = s // ts

    def body(x_ref, dh1_ref, pooled_ref, gt_ref, w0_ref, wpi_ref, gw_ref, gb_ref, sc_ref, wpo_ref,
             dx_ref, y_ref, dproj_ref, ggw_ref, small_ref, ebuf):
        i = pl.program_id(0)

        @pl.when(i == 0)
        def _():
            ggw_ref[...] = jnp.zeros_like(ggw_ref)
            small_ref[...] = jnp.zeros_like(small_ref)
            ebuf[ts:ts + HALO, :] = jnp.zeros((HALO, D), F32)

        dh1 = dh1_ref[...]
        gt = gt_ref[...]
        sc = sc_ref[...]
        dy = _nt(_bf(dh1), wpo_ref[...])
        pooled_bf = []
        mixed = []
        for g in range(GROUPS):
            cols = slice(g * GROUP_DIM, (g + 1) * GROUP_DIM)
            pb = _bf(pooled_ref[:, cols])
            pooled_bf.append(pb)
            mixed.append(_nn(pb, gw_ref[g]))
        mixed = jnp.concatenate(mixed, axis=-1) + gb_ref[...]
        sg = _sigmoid(gt)
        silu = gt * sg
        y_ref[...] = _bf(mixed * sc * silu)
        dmixed = dy * sc * silu
        dgt = dy * mixed * sc * (sg * (1.0 + gt * (1.0 - sg)))
        dproj_ref[:, D:] = _bf(dgt)
        small_ref[1:2, :] += jnp.sum(dy * mixed * silu, axis=0, keepdims=True)
        small_ref[2:3, :] += jnp.sum(dmixed, axis=0, keepdims=True)

        inv = _inverse_counts(_row_index(nt - 1 - i, ts))
        dpooled = []
        for g in range(GROUPS):
            cols = slice(g * GROUP_DIM, (g + 1) * GROUP_DIM)
            dm = _bf(dmixed[:, cols])
            ggw_ref[g] += _tn(pooled_bf[g], dm)
            dp = _nt(dm, gw_ref[g])
            dpooled.append(dp)
            ebuf[0:ts, cols] = dp * inv[g]
        du = []
        for g, w in enumerate(POOL_WINDOWS):
            cols = slice(g * GROUP_DIM, (g + 1) * GROUP_DIM)
            acc = -dpooled[g]
            for j in range(w):
                acc = acc + ebuf[j:j + ts, cols]
            du.append(acc)
        ebuf[ts:ts + HALO, :] = ebuf[0:HALO, :]
        du = _bf(jnp.concatenate(du, axis=-1))
        dproj_ref[:, :D] = du
        dgt_bf = _bf(dgt)
        half = D // 2
        dn0 = (_nt(du[:, :half], wpi_ref[0]) + _nt(du[:, half:], wpi_ref[1])
               + _nt(dgt_bf[:, :half], wpi_ref[2]) + _nt(dgt_bf[:, half:], wpi_ref[3]))

        xv = x_ref[...]
        r = lax.rsqrt(jnp.mean(xv * xv, axis=-1, keepdims=True) + EPS)
        xhat = xv * r
        small_ref[0:1, :] += jnp.sum(dn0 * xhat, axis=0, keepdims=True)
        dxh = dn0 * w0_ref[...]
        dx_ref[...] = dh1 + r * (dxh - xhat * jnp.mean(dxh * xhat, axis=-1, keepdims=True))

    row = lambda cols: pl.BlockSpec((ts, cols), lambda i: (nt - 1 - i, 0))
    return pl.pallas_call(
        body, name="pool_backward", grid=(nt,),
        out_shape=(jax.ShapeDtypeStruct((s, D), F32), jax.ShapeDtypeStruct((s, D), BF16),
                   jax.ShapeDtypeStruct((s, 2 * D), BF16),
                   jax.ShapeDtypeStruct((GROUPS, GROUP_DIM, GROUP_DIM), F32),
                   jax.ShapeDtypeStruct((8, D), F32)),
        in_specs=[row(D), row(D), row(D), row(D), _full((1, D)), _full((N_CHIPS, D, D // 2)),
                  _full((GROUPS, GROUP_DIM, GROUP_DIM)), _full((1, D)), _full((1, D)), _full((D, D))],
        out_specs=(row(D), row(D), row(2 * D), _full((GROUPS, GROUP_DIM, GROUP_DIM)), _full((8, D))),
        scratch_shapes=[pltpu.VMEM((ts + HALO, D), F32)],
        compiler_params=_params("arbitrary"),
    )(x, dh1, pooled, gt, w0, wpi, gw, gb, scale, wpo)


def gla_project(h1, w1, wgi, wlow, wgk, bgk):
    s = h1.shape[0]
    ts = ROW_TILE

    def body(h_ref, w1_ref, wgi_ref, wlow_ref, wgk_ref, bgk_ref, proj_ref, low_ref, lg_ref, n1_ref):
        hv = h_ref[...]
        r = lax.rsqrt(jnp.mean(hv * hv, axis=-1, keepdims=True) + EPS)
        n1 = _bf(hv * r * w1_ref[...])
        n1_ref[...] = n1
        proj_ref[...] = _nn(n1, wgi_ref[...])
        low = _nn(n1, wlow_ref[...])
        low_ref[...] = low
        z = _nn(_bf(low), wgk_ref[...]) + bgk_ref[...]
        lg_ref[...] = (jnp.minimum(z, 0.0) - jnp.log(1.0 + jnp.exp(-jnp.abs(z)))) / GATE_NORM

    row = lambda cols: pl.BlockSpec((ts, cols), lambda i: (i, 0))
    return pl.pallas_call(
        body, name="gla_project", grid=(s // ts,),
        out_shape=(jax.ShapeDtypeStruct((s, GLA_MAIN), F32), jax.ShapeDtypeStruct((s, RANK_PAD), F32),
                   jax.ShapeDtypeStruct((s, KEY_W), F32), jax.ShapeDtypeStruct((s, D), BF16)),
        in_specs=[row(D), _full((1, D)), _full((D, GLA_MAIN)), _full((D, RANK_PAD)),
                  _full((RANK_PAD, KEY_W)), _full((1, KEY_W))],
        out_specs=(row(GLA_MAIN), row(RANK_PAD), row(KEY_W), row(D)),
        compiler_params=_params("parallel"),
    )(h1, w1, wgi, wlow, wgk, bgk)


GLA_BLOCK = 512
CHUNKS_PER_BLOCK = GLA_BLOCK // CHUNK


def _chunk_masks():
    t = lax.broadcasted_iota(jnp.int32, (CHUNK, CHUNK), 0)
    u = lax.broadcasted_iota(jnp.int32, (CHUNK, CHUNK), 1)
    return t >= u, t <= u


def _gla_chunk_terms(q, k, lg, lower_f):
    cum = _nn_exact(lower_f, lg)
    ep = jnp.exp(cum)
    en = jnp.exp(-cum)
    qs = q * (HEAD_K ** -0.5)
    last = cum[CHUNK - 1:CHUNK, :]
    ed = jnp.exp(last - cum)
    dec = jnp.exp(last)
    return cum, ep, en, qs, ed, dec


def gla_forward(proj, lg):
    s = proj.shape[0]
    nb = s // GLA_BLOCK
    nc = s // CHUNK

    def body(q_ref, k_ref, v_ref, lg_ref, o_ref, st_ref, state):
        @pl.when(pl.program_id(0) == 0)
        def _():
            state[...] = jnp.zeros_like(state)

        lower, _ = _chunk_masks()
        lower_f = lower.astype(F32)

        def chunk(cc, carry):
            rows = pl.ds(pl.multiple_of(cc * CHUNK, CHUNK), CHUNK)
            for h in range(HEADS):
                kc = slice(h * HEAD_K, (h + 1) * HEAD_K)
                vc = slice(h * HEAD_V, (h + 1) * HEAD_V)
                q = q_ref[rows, kc]
                k = k_ref[rows, kc]
                v = _bf(v_ref[rows, vc])
                _, ep, en, qs, ed, dec = _gla_chunk_terms(q, k, lg_ref[rows, kc], lower_f)
                a = _bf(qs * ep)
                fwd = _nt(a, _bf(k * en))
                bwd = _nt(_bf(qs * en), _bf(k * ep))
                scores = jnp.where(lower, fwd, bwd)
                st = state[h]
                st_ref[cc, h] = st
                o_ref[rows, vc] = _nn(_bf(scores), v) + _nt(a, _bf(st))
                state[h] = st * dec + _tn(v, _bf(k * ed))
            return carry

        lax.fori_loop(0, CHUNKS_PER_BLOCK, chunk, 0)

    return pl.pallas_call(
        body, name="gla_forward", grid=(nb,),
        out_shape=(jax.ShapeDtypeStruct((s, D), F32),
                   jax.ShapeDtypeStruct((nc, HEADS, HEAD_V, HEAD_K), F32)),
        in_specs=[pl.BlockSpec((GLA_BLOCK, KEY_W), lambda i: (i, 0)),
                  pl.BlockSpec((GLA_BLOCK, KEY_W), lambda i: (i, 1)),
                  pl.BlockSpec((GLA_BLOCK, D), lambda i: (i, 1)),
                  pl.BlockSpec((GLA_BLOCK, KEY_W), lambda i: (i, 0))],
        out_specs=(pl.BlockSpec((GLA_BLOCK, D), lambda i: (i, 0)),
                   pl.BlockSpec((CHUNKS_PER_BLOCK, HEADS, HEAD_V, HEAD_K), lambda i: (i, 0, 0, 0))),
        scratch_shapes=[pltpu.VMEM((HEADS, HEAD_V, HEAD_K), F32)],
        compiler_params=_params("arbitrary"),
    )(proj, proj, proj, lg)


def gla_backward(proj, lg, do, states):
    s = proj.shape[0]
    nb = s // GLA_BLOCK

    def body(q_ref, k_ref, v_ref, lg_ref, do_ref, st_ref, dq_ref, dk_ref, dv_ref, dlg_ref, dstate):
        @pl.when(pl.program_id(0) == 0)
        def _():
            dstate[...] = jnp.zeros_like(dstate)

        lower, upper = _chunk_masks()
        lower_f = lower.astype(F32)
        upper_f = upper.astype(F32)
        is_last = lax.broadcasted_iota(jnp.int32, (CHUNK, HEAD_K), 0) == CHUNK - 1

        def chunk(step, carry):
            cc = CHUNKS_PER_BLOCK - 1 - step
            rows = pl.ds(pl.multiple_of(cc * CHUNK, CHUNK), CHUNK)
            for h in range(HEADS):
                kc = slice(h * HEAD_K, (h + 1) * HEAD_K)
                vc = slice(h * HEAD_V, (h + 1) * HEAD_V)
                q = q_ref[rows, kc]
                k = k_ref[rows, kc]
                v = _bf(v_ref[rows, vc])
                do_c = _bf(do_ref[rows, vc])
                _, ep, en, qs, ed, dec = _gla_chunk_terms(q, k, lg_ref[rows, kc], lower_f)
                a = _bf(qs * ep)
                b = _bf(k * en)
                c = _bf(qs * en)
                dk_dec = _bf(k * ep)
                kd = _bf(k * ed)
                scores = _bf(jnp.where(lower, _nt(a, b), _nt(c, dk_dec)))
                st = st_ref[cc, h]
                dst = dstate[h]
                dst_bf = _bf(dst)

                dscores = _nt(do_c, v)
                dfwd = _bf(jnp.where(lower, dscores, 0.0))
                dbwd = _bf(jnp.where(lower, 0.0, dscores))
                dv_ref[rows, vc] = _tn(scores, do_c) + _nt(kd, dst_bf)
                da = _nn(dfwd, b) + _nn(do_c, _bf(st))
                db = _tn(dfwd, a)
                dc = _nn(dbwd, dk_dec)
                ddk = _tn(dbwd, c)
                dkd = _nn(v, dst_bf)
                ddec = jnp.sum(dst * st, axis=0, keepdims=True)
                dstate[h] = dst * dec + _tn(do_c, a)

                m = dkd * k * ed
                dq_ref[rows, kc] = (da * ep + dc * en) * (HEAD_K ** -0.5)
                dk_ref[rows, kc] = db * en + ddk * ep + dkd * ed
                dcum = (da * qs + ddk * k) * ep - (db * k + dc * qs) * en - m
                dlast = jnp.sum(m, axis=0, keepdims=True) + ddec * dec
                dcum = dcum + jnp.where(is_last, dlast, 0.0)
                dlg_ref[rows, kc] = _nn_exact(upper_f, dcum)
            return carry

        lax.fori_loop(0, CHUNKS_PER_BLOCK, chunk, 0)

    rev = lambda cols, col_block: pl.BlockSpec((GLA_BLOCK, cols), lambda i: (nb - 1 - i, col_block))
    return pl.pallas_call(
        body, name="gla_backward", grid=(nb,),
        out_shape=(jax.ShapeDtypeStruct((s, KEY_W), F32), jax.ShapeDtypeStruct((s, KEY_W), F32),
                   jax.ShapeDtypeStruct((s, D), F32), jax.ShapeDtypeStruct((s, KEY_W), F32)),
        in_specs=[rev(KEY_W, 0), rev(KEY_W, 1), rev(D, 1), rev(KEY_W, 0), rev(D, 0),
                  pl.BlockSpec((CHUNKS_PER_BLOCK, HEADS, HEAD_V, HEAD_K), lambda i: (nb - 1 - i, 0, 0, 0))],
        out_specs=(rev(KEY_W, 0), rev(KEY_W, 0), rev(D, 0), rev(KEY_W, 0)),
        scratch_shapes=[pltpu.VMEM((HEADS, HEAD_V, HEAD_K), F32)],
        compiler_params=_params("arbitrary"),
    )(proj, proj, proj, lg, do, states)


def head_and_loss(o, proj, h1, target, hw, wgo, wf):
    s = o.shape[0]
    ts = ROW_TILE

    def body(o_ref, gate_ref, h1_ref, tgt_ref, hw_ref, wgo_ref, wf_ref,
             dh2_ref, dh2bf_ref, do_ref, dgate_ref, y2_ref, small_ref):
        @pl.when(pl.program_id(0) == 0)
        def _():
            small_ref[...] = jnp.zeros_like(small_ref)

        gate = gate_ref[...]
        hw = hw_ref[...]
        sg = _sigmoid(gate)
        silu = gate * sg
        ohat, ro = [], []
        for h in range(HEADS):
            oh = o_ref[:, h * HEAD_V:(h + 1) * HEAD_V]
            rh = lax.rsqrt(jnp.mean(oh * oh, axis=-1, keepdims=True) + EPS)
            ro.append(rh)
            ohat.append(oh * rh)
        ohat = jnp.concatenate(ohat, axis=-1)
        on = ohat * hw
        y2 = _bf(on * silu)
        y2_ref[...] = y2
        h2 = h1_ref[...] + _nn(y2, wgo_ref[...])
        rf = lax.rsqrt(jnp.mean(h2 * h2, axis=-1, keepdims=True) + EPS)
        h2hat = h2 * rf
        wf = wf_ref[...]
        diff = h2hat * wf - tgt_ref[...]
        small_ref[2:3, :] += jnp.zeros((1, D), F32) + 0.5 * jnp.sum(diff * diff) / D
        dout = diff / D
        small_ref[0:1, :] += jnp.sum(dout * h2hat, axis=0, keepdims=True)
        dxh = dout * wf
        dh2 = rf * (dxh - h2hat * jnp.mean(dxh * h2hat, axis=-1, keepdims=True))
        dh2_ref[...] = dh2
        dh2_bf = _bf(dh2)
        dh2bf_ref[...] = dh2_bf
        dy2 = _nt(dh2_bf, wgo_ref[...])
        don = dy2 * silu
        dgate_ref[...] = dy2 * on * (sg * (1.0 + gate * (1.0 - sg)))
        ghw = jnp.sum(don * ohat, axis=0, keepdims=True)
        small_ref[1:2, 0:HEAD_V] += sum(ghw[:, h * HEAD_V:(h + 1) * HEAD_V] for h in range(HEADS))
        dohat = don * hw
        for h in range(HEADS):
            cols = slice(h * HEAD_V, (h + 1) * HEAD_V)
            oh, dh = ohat[:, cols], dohat[:, cols]
            do_ref[:, cols] = ro[h] * (dh - oh * jnp.mean(dh * oh, axis=-1, keepdims=True))

    row = lambda cols: pl.BlockSpec((ts, cols), lambda i: (i, 0))
    act = jax.ShapeDtypeStruct((s, D), F32)
    act_bf = jax.ShapeDtypeStruct((s, D), BF16)
    return pl.pallas_call(
        body, name="head_and_loss", grid=(s // ts,),
        out_shape=(act, act_bf, act, act, act_bf, jax.ShapeDtypeStruct((8, D), F32)),
        in_specs=[row(D), pl.BlockSpec((ts, D), lambda i: (i, 2)), row(D), row(D),
                  _full((1, D)), _full((D, D)), _full((1, D))],
        out_specs=(row(D), row(D), row(D), row(D), row(D), _full((8, D))),
        compiler_params=_params("arbitrary"),
    )(o, proj, h1, target, hw, wgo, wf)


def gla_project_backward(dq, dk, dv, dgate, dlg, low, h1, dh2, w1, wgi, wlow, wgk, bgk):
    s = h1.shape[0]
    ts = ROW_TILE

    def body(dq_ref, dk_ref, dv_ref, dgate_ref, dlg_ref, low_ref, h1_ref, dh2_ref, w1_ref,
             wgi_ref, wlow_ref, wgk_ref, bgk_ref, dh1_ref, dh1bf_ref, dproj_ref, dlow_ref, ggk_ref,
             small_ref):
        @pl.when(pl.program_id(0) == 0)
        def _():
            ggk_ref[...] = jnp.zeros_like(ggk_ref)
            small_ref[...] = jnp.zeros_like(small_ref)

        low = _bf(low_ref[...])
        z = _nn(low, wgk_ref[...]) + bgk_ref[...]
        dz = dlg_ref[...] * (1.0 / GATE_NORM) * _sigmoid(-z)
        dz_bf = _bf(dz)
        ggk_ref[...] += _tn(low, dz_bf)
        small_ref[1:2, 0:KEY_W] += jnp.sum(dz, axis=0, keepdims=True)
        dlow = _bf(_nt(dz_bf, wgk_ref[...]))
        dlow_ref[...] = dlow
        dn1 = _nt(dlow, wlow_ref[...])
        for ref, lo, hi in ((dq_ref, 0, KEY_W), (dk_ref, KEY_W, 2 * KEY_W),
                            (dv_ref, 2 * KEY_W, 2 * KEY_W + D), (dgate_ref, 2 * KEY_W + D, GLA_MAIN)):
            piece = _bf(ref[...])
            dproj_ref[:, lo:hi] = piece
            dn1 = dn1 + _nt(piece, wgi_ref[:, lo:hi])
        hv = h1_ref[...]
        r = lax.rsqrt(jnp.mean(hv * hv, axis=-1, keepdims=True) + EPS)
        hhat = hv * r
        small_ref[0:1, :] += jnp.sum(dn1 * hhat, axis=0, keepdims=True)
        dxh = dn1 * w1_ref[...]
        dh1 = dh2_ref[...] + r * (dxh - hhat * jnp.mean(dxh * hhat, axis=-1, keepdims=True))
        dh1_ref[...] = dh1
        dh1bf_ref[...] = _bf(dh1)

    row = lambda cols: pl.BlockSpec((ts, cols), lambda i: (i, 0))
    return pl.pallas_call(
        body, name="gla_project_backward", grid=(s // ts,),
        out_shape=(jax.ShapeDtypeStruct((s, D), F32), jax.ShapeDtypeStruct((s, D), BF16),
                   jax.ShapeDtypeStruct((s, GLA_MAIN), BF16),
                   jax.ShapeDtypeStruct((s, RANK_PAD), BF16), jax.ShapeDtypeStruct((RANK_PAD, KEY_W), F32),
                   jax.ShapeDtypeStruct((8, D), F32)),
        in_specs=[row(KEY_W), row(KEY_W), row(D), row(D), row(KEY_W), row(RANK_PAD), row(D), row(D),
                  _full((1, D)), _full((D, GLA_MAIN)), _full((D, RANK_PAD)), _full((RANK_PAD, KEY_W)),
                  _full((1, KEY_W))],
        out_specs=(row(D), row(D), row(GLA_MAIN), row(RANK_PAD), _full((RANK_PAD, KEY_W)), _full((8, D))),
        compiler_params=_params("arbitrary"),
    )(dq, dk, dv, dgate, dlg, low, h1, dh2, w1, wgi, wlow, wgk, bgk)


def _groups_from_quarters(a):
    return a.reshape(N_CHIPS, GROUPS, 64, GROUP_DIM).transpose(1, 0, 2, 3).reshape(GROUPS, GROUP_DIM, GROUP_DIM)


def _quarters_from_groups(a):
    return a.reshape(GROUPS, N_CHIPS, 64, GROUP_DIM).transpose(1, 0, 2, 3).reshape(N_CHIPS, GROUP_DIM, GROUP_DIM)


def _pad_row(*pieces):
    flat = jnp.concatenate([p.reshape(-1).astype(F32) for p in pieces])
    return jnp.pad(flat, (0, D - flat.shape[0])).reshape(1, D)


def local_gradients(xs, target, w0, w1, wf, wpi, gw, gb, scale, wpo, wgi, wlow, wgk, bgk, hw_tiled, wgo):
    h1, pooled, gt, n0 = pool_forward(xs, w0, wpi, gw, gb, scale, wpo)
    proj, low, lg, n1 = gla_project(h1, w1, wgi, wlow, wgk, bgk)
    o, states = gla_forward(proj, lg)

    dh2, dh2_bf, do, dgate, y2, small_top = head_and_loss(o, proj, h1, target, hw_tiled, wgo, wf)
    g_gla_out = matmul_tn(y2, dh2_bf, "grad_gla_out")
    dq, dk, dv, dlg = gla_backward(proj, lg, do, states)
    dh1, dh1_bf, dproj, dlow, g_gk_pad, small_gla = gla_project_backward(
        dq, dk, dv, dgate, dlg, low, h1, dh2, w1, wgi, wlow, wgk, bgk)
    g_gla_in = jnp.concatenate([matmul_tn(n1, dproj, "grad_gla_in"),
                                matmul_tn(n1, dlow, "grad_gla_low")[:, :GATE_RANK]], axis=1)
    dx, y, dpool, g_group_w, small_pool = pool_backward(
        xs, dh1, pooled, gt, w0, wpi, gw, gb, scale, wpo)
    g_pool_out = matmul_tn(y, dh1_bf, "grad_pool_out")
    g_pool_in = matmul_tn(n0, dpool, "grad_pool_in", by_column_tile=True)
    return (dx, g_pool_in, g_group_w, g_pool_out, g_gla_in, g_gla_out, g_gk_pad,
            small_top, small_gla, small_pool)


def kernel(x, norm_w, pool_in_w, pool_group_w, pool_group_b, pool_scale, pool_out_w, gla_in_w, gla_gk_w, gla_gk_b, gla_head_norm_w, gla_out_w, final_norm_w, loss_target, m_norm_w, m_pool_in_w, m_pool_group_w, m_pool_group_b, m_pool_scale, m_pool_out_w, m_gla_in_w, m_gla_gk_w, m_gla_gk_b, m_gla_head_norm_w, m_gla_out_w, m_final_norm_w, v_norm_w, v_pool_in_w, v_pool_group_w, v_pool_group_b, v_pool_scale, v_pool_out_w, v_gla_in_w, v_gla_gk_w, v_gla_gk_b, v_gla_head_norm_w, v_gla_out_w, v_final_norm_w):
    s = x.shape[1]
    xs = x[0]
    target = loss_target[0]
    q_chip = 2 * lax.axis_index("x") + lax.axis_index("y")
    place = jnp.stack([lax.axis_index("c"), q_chip]).astype(jnp.int32)

    wpi, gw_q, wpo_q, wgi_q, wgo_q = allgather_weights([
        pool_in_w[0], pool_group_w[0].reshape(GROUP_DIM, GROUP_DIM), pool_out_w[0], gla_in_w[0], gla_out_w[0]])
    gw = _groups_from_quarters(gw_q)
    wpo = wpo_q.reshape(D, D)
    wgo = wgo_q.reshape(D, D)
    wgi_all = jnp.concatenate([wgi_q[q] for q in range(N_CHIPS)], axis=1)
    wgi = wgi_all[:, :GLA_MAIN]
    wlow = jnp.pad(wgi_all[:, GLA_MAIN:], ((0, 0), (0, RANK_PAD - GATE_RANK)))

    small_in = jnp.concatenate([
        _pad_row(gla_gk_b[0], gla_head_norm_w[0], pool_group_b[0]),
        gla_gk_w[0].reshape(2, D),
        jnp.zeros((5, D), F32)], axis=0)
    small_all = gather_small(small_in, "gather_small_weights")[0::2]
    bgk = small_all[:, 0, 0:128].reshape(1, KEY_W)
    hw = small_all[:, 0, 128:192].reshape(1, HEAD_V)
    gb = jnp.concatenate([small_all[q, 0, 192:448].reshape(GROUPS, 64) for q in range(N_CHIPS)],
                         axis=1).reshape(1, D)
    wgk16 = jnp.concatenate([small_all[q, 1:3].reshape(GATE_RANK, 128) for q in range(N_CHIPS)], axis=1)
    wgk = _bf(jnp.pad(wgk16, ((0, RANK_PAD - GATE_RANK), (0, 0))))
    hw_tiled = jnp.tile(hw, (1, HEADS))

    w0 = norm_w[0:1]
    w1 = norm_w[1:2]
    wf = final_norm_w.reshape(1, D)

    (dx, g_pool_in, g_group_w, g_pool_out, g_gla_in, g_gla_out, g_gk_pad,
     small_top, small_gla, small_pool) = local_gradients(
        xs, target, w0, w1, wf, wpi, gw, gb, pool_scale, wpo, wgi, wlow, wgk, bgk, hw_tiled, wgo)

    names = ("pool_in", "group", "pool_out", "gla_in", "gla_out")
    grads = [g_pool_in, _quarters_from_groups(g_group_w), g_pool_out.reshape(N_CHIPS, D // N_CHIPS, D),
             jnp.stack([g_gla_in[:, GLA_IN_QUARTER * q:GLA_IN_QUARTER * (q + 1)] for q in range(N_CHIPS)]),
             g_gla_out.reshape(N_CHIPS, D // N_CHIPS, D)]
    theirs = exchange_with_sibling(grads)
    sums = [add_halves(g, t, place, "add_halves_" + n) for g, t, n in zip(grads, theirs, names)]
    got = scatter_to_owners([b for _, b in sums])
    reduced = join_halves([add_parts(f, g, place, "add_parts_" + n) for (f, _), g, n in zip(sums, got, names)])
    r_pool_in, r_group_w, r_pool_out, r_gla_in, r_gla_out = reduced
    r_group_w = r_group_w.reshape(GROUPS, 64, GROUP_DIM)

    small_out = jnp.concatenate([
        small_pool[0:1], small_gla[0:1],
        small_pool[1:2],
        small_top[0:1],
        small_top[2:3],
        _pad_row(small_gla[1, 0:KEY_W], small_top[1, 0:HEAD_V]),
        small_pool[2:3],
        g_gk_pad[:GATE_RANK].reshape(8, D),
        jnp.zeros((1, D), F32)], axis=0)
    total = gather_small(small_out, "allreduce_small_grads", reduce=True)
    loss = total[4, 0]
    g_norm = total[0:2]
    g_scale = total[2:3]
    g_final = total[3]
    pick = lambda full, width: lax.dynamic_slice_in_dim(full, q_chip * width, width, axis=-1)
    g_gk_b = pick(total[5:6, 0:KEY_W], 128)
    g_hnw = pick(total[5:6, KEY_W:KEY_W + HEAD_V], 64)
    g_group_b = pick(total[6].reshape(GROUPS, GROUP_DIM), 64)[None]
    g_gk_w = pick(total[7:15].reshape(GATE_RANK, KEY_W), 128)[None]

    def step(name, w, g, m, v):
        shape = w.shape
        as2d = lambda a: a.reshape(-1, shape[-1])
        d, nm, nv = adamw(as2d(w), as2d(g), as2d(m), as2d(v), "adamw_" + name)
        return g.reshape(shape), d.reshape(shape), nm.reshape(shape), nv.reshape(shape)

    results = [
        step("norm_w", norm_w, g_norm, m_norm_w, v_norm_w),
        step("pool_in_w", pool_in_w, r_pool_in[None], m_pool_in_w, v_pool_in_w),
        step("pool_group_w", pool_group_w, r_group_w[None], m_pool_group_w, v_pool_group_w),
        step("pool_group_b", pool_group_b, g_group_b, m_pool_group_b, v_pool_group_b),
        step("pool_scale", pool_scale, g_scale, m_pool_scale, v_pool_scale),
        step("pool_out_w", pool_out_w, r_pool_out[None], m_pool_out_w, v_pool_out_w),
        step("gla_in_w", gla_in_w, r_gla_in[None], m_gla_in_w, v_gla_in_w),
        step("gla_gk_w", gla_gk_w, g_gk_w, m_gla_gk_w, v_gla_gk_w),
        step("gla_gk_b", gla_gk_b, g_gk_b, m_gla_gk_b, v_gla_gk_b),
        step("gla_head_norm_w", gla_head_norm_w, g_hnw, m_gla_head_norm_w, v_gla_head_norm_w),
        step("gla_out_w", gla_out_w, r_gla_out[None], m_gla_out_w, v_gla_out_w),
        step("final_norm_w", final_norm_w, g_final, m_final_norm_w, v_final_norm_w),
    ]
    grads, deltas, new_m, new_v = zip(*results)
    return (loss, dx[None], *grads, *deltas, *new_m, *new_v)
```

```python
import functools

import jax
import jax.numpy as jnp
from jax import lax
from jax.experimental import pallas as pl
from jax.experimental.pallas import tpu as pltpu

F32 = jnp.float32
BF16 = jnp.bfloat16
MESH = pl.DeviceIdType.MESH

D = 1024
POOL_WINDOWS = (2, 4, 8, 16)
GROUPS = 4
GROUP_DIM = 256
HEADS = 4
HEAD_K = 128
HEAD_V = 256
KEY_W = 512
CHUNK = 64
GATE_RANK = 16
GATE_NORM = 16.0
GLA_IN = 3088
GLA_MAIN = 3072
RANK_PAD = 128
EPS = 1e-6
HALO = 16

ADAM_LR = 0.001
ADAM_B1 = 0.9
ADAM_B2 = 0.999
ADAM_EPS = 1e-08
ADAM_WD = 0.01
ADAM_STEP = 10

N_CHIPS = 4
N_DEV = 8
GLA_IN_QUARTER = GLA_IN // N_CHIPS

VMEM_LIMIT = 56 * 1024 * 1024


def _nn(a, b):
    return lax.dot_general(a, b, (((1,), (0,)), ((), ())), preferred_element_type=F32)


def _nt(a, b):
    return lax.dot_general(a, b, (((1,), (1,)), ((), ())), preferred_element_type=F32)


def _tn(a, b):
    return lax.dot_general(a, b, (((0,), (0,)), ((), ())), preferred_element_type=F32)


def _nn_exact(a, b):
    return lax.dot_general(a, b, (((1,), (0,)), ((), ())), preferred_element_type=F32,
                           precision=lax.Precision.HIGHEST)


def _bf(a):
    return a.astype(BF16)


def _params(*sem):
    return pltpu.CompilerParams(dimension_semantics=sem, vmem_limit_bytes=VMEM_LIMIT)


def _full(shape):
    return pl.BlockSpec(shape, lambda i: (0,) * len(shape))


def _position():
    return lax.axis_index("x"), lax.axis_index("y"), lax.axis_index("c")


def gather_small(block, name, reduce=False):
    rows, cols = block.shape

    def body(in_ref, out_ref, *scratch):
        if reduce:
            all_ref, send_sems, recv_sems, local_sem = scratch
        else:
            all_ref = out_ref
            send_sems, recv_sems, local_sem = scratch
        x, y, c = _position()
        me = 4 * x + 2 * y + c
        mine = pltpu.make_async_copy(in_ref, all_ref.at[me], local_sem)
        mine.start()
        sends = []
        for k in range(N_DEV - 1):
            fx, fy, fc = (k + 1) >> 2 & 1, (k + 1) >> 1 & 1, (k + 1) & 1
            cp = pltpu.make_async_remote_copy(
                src_ref=in_ref, dst_ref=all_ref.at[me],
                send_sem=send_sems.at[k], recv_sem=recv_sems.at[k],
                device_id=(x ^ fx, y ^ fy, c ^ fc), device_id_type=MESH)
            cp.start()
            sends.append(cp)
        for k in range(N_DEV - 1):
            fx, fy, fc = (k + 1) >> 2 & 1, (k + 1) >> 1 & 1, (k + 1) & 1
            src_dev = 4 * (x ^ fx) + 2 * (y ^ fy) + (c ^ fc)
            pltpu.make_async_remote_copy(
                src_ref=in_ref, dst_ref=all_ref.at[src_dev],
                send_sem=send_sems.at[k], recv_sem=recv_sems.at[k],
                device_id=(x, y, c), device_id_type=MESH).wait_recv()
        for cp in sends:
            cp.wait_send()
        mine.wait()
        if reduce:
            total = all_ref[0]
            for dev in range(1, N_DEV):
                total = total + all_ref[dev]
            out_ref[...] = total

    sems = [pltpu.SemaphoreType.DMA((N_DEV - 1,)), pltpu.SemaphoreType.DMA((N_DEV - 1,)),
            pltpu.SemaphoreType.DMA]
    gathered = (N_DEV, rows, cols)
    return pl.pallas_call(
        body, name=name,
        out_shape=jax.ShapeDtypeStruct((rows, cols) if reduce else gathered, block.dtype),
        in_specs=[pl.BlockSpec(memory_space=pltpu.VMEM)],
        out_specs=pl.BlockSpec(memory_space=pltpu.VMEM),
        scratch_shapes=([pltpu.VMEM(gathered, block.dtype)] if reduce else []) + sems,
    )(block)


def _other_chips(x, y):
    return [(1 - x, y), (x, 1 - y), (1 - x, 1 - y)]


def _any_specs(n):
    return [pl.BlockSpec(memory_space=pl.ANY)] * n


def _halves(rows, c):
    half = rows // 2
    return pl.ds(c * half, half), pl.ds((1 - c) * half, half)


CAST_ROWS = 256


def allgather_weights(quarters):
    n = len(quarters)
    shapes = [w.shape for w in quarters]

    def body(*refs):
        w_refs, out_refs = refs[:n], refs[n:2 * n]
        f32_bufs, bf_bufs = refs[2 * n:3 * n], refs[3 * n:4 * n]
        send_sems, recv_sems, local_sems = refs[4 * n:]
        x, y, c = _position()
        q = 2 * x + y
        sibling = (x, y, 1 - c)
        chips = _other_chips(x, y)

        def copy(k, i, quarter, half, to, src=None):
            dst = out_refs[i].at[quarter, half]
            return pltpu.make_async_remote_copy(
                src_ref=dst if src is None else src, dst_ref=dst,
                send_sem=send_sems.at[k * n + i], recv_sem=recv_sems.at[k * n + i],
                device_id=to, device_id_type=MESH)

        loads = [pltpu.make_async_copy(w_refs[i], f32_bufs[i], local_sems.at[i]) for i in range(n)]
        for cp in loads:
            cp.start()
        keeps, sends = [], []
        for i in range(n):
            loads[i].wait()
            for r0 in range(0, shapes[i][0], CAST_ROWS):
                bf_bufs[i][r0:r0 + CAST_ROWS, :] = _bf(f32_bufs[i][r0:r0 + CAST_ROWS, :])
            keep = pltpu.make_async_copy(bf_bufs[i], out_refs[i].at[q], local_sems.at[n + i])
            keep.start()
            keeps.append(keep)
            mine, _ = _halves(shapes[i][0], c)
            for j, chip in enumerate(chips):
                cp = copy(j, i, q, mine, (*chip, c), src=bf_bufs[i].at[mine])
                cp.start()
                sends.append(cp)
        for j, chip in enumerate(chips):
            qj = 2 * chip[0] + chip[1]
            for i in range(n):
                mine, _ = _halves(shapes[i][0], c)
                copy(j, i, qj, mine, (x, y, c)).wait_recv()
                cp = copy(3 + j, i, qj, mine, sibling)
                cp.start()
                sends.append(cp)
        for j, chip in enumerate(chips):
            qj = 2 * chip[0] + chip[1]
            for i in range(n):
                _, other = _halves(shapes[i][0], c)
                copy(3 + j, i, qj, other, (x, y, c)).wait_recv()
        for cp in sends:
            cp.wait_send()
        for cp in keeps:
            cp.wait()

    return pl.pallas_call(
        body, name="allgather_weights",
        out_shape=[jax.ShapeDtypeStruct((N_CHIPS, *s), BF16) for s in shapes],
        in_specs=_any_specs(n), out_specs=_any_specs(n),
        scratch_shapes=([pltpu.VMEM(s, F32) for s in shapes] + [pltpu.VMEM(s, BF16) for s in shapes]
                        + [pltpu.SemaphoreType.DMA((6 * n,)), pltpu.SemaphoreType.DMA((6 * n,)),
                           pltpu.SemaphoreType.DMA((2 * n,))]),
        compiler_params=pltpu.CompilerParams(vmem_limit_bytes=VMEM_LIMIT),
    )(*quarters)


def exchange_with_sibling(grads):
    n = len(grads)

    def body(*refs):
        g_refs, theirs_refs = refs[:n], refs[n:2 * n]
        send_sems, recv_sems = refs[2 * n:]
        x, y, c = _position()
        copies = []
        for i in range(n):
            _, other = _halves(g_refs[i].shape[1], c)
            cp = pltpu.make_async_remote_copy(
                src_ref=g_refs[i].at[:, other], dst_ref=theirs_refs[i],
                send_sem=send_sems.at[i], recv_sem=recv_sems.at[i],
                device_id=(x, y, 1 - c), device_id_type=MESH)
            cp.start()
            copies.append(cp)
        for cp in copies:
            cp.wait()

    return pl.pallas_call(
        body, name="exchange_with_sibling",
        out_shape=[jax.ShapeDtypeStruct((N_CHIPS, g.shape[1] // 2, g.shape[2]), F32) for g in grads],
        in_specs=_any_specs(n), out_specs=_any_specs(n),
        scratch_shapes=[pltpu.SemaphoreType.DMA((n,)), pltpu.SemaphoreType.DMA((n,))],
    )(*grads)


def scatter_to_owners(chip_sums):
    n = len(chip_sums)

    def body(*refs):
        b_refs, got_refs = refs[:n], refs[n:2 * n]
        send_sems, recv_sems = refs[2 * n:]
        x, y, c = _position()
        copies = []
        for j, chip in enumerate(_other_chips(x, y)):
            qj = 2 * chip[0] + chip[1]
            for i in range(n):
                cp = pltpu.make_async_remote_copy(
                    src_ref=b_refs[i].at[qj], dst_ref=got_refs[i].at[j],
                    send_sem=send_sems.at[j * n + i], recv_sem=recv_sems.at[j * n + i],
                    device_id=(*chip, c), device_id_type=MESH)
                cp.start()
                copies.append(cp)
        for cp in copies:
            cp.wait()

    return pl.pallas_call(
        body, name="scatter_to_owners",
        out_shape=[jax.ShapeDtypeStruct((N_CHIPS - 1, *b.shape[1:]), BF16) for b in chip_sums],
        in_specs=_any_specs(n), out_specs=_any_specs(n),
        scratch_shapes=[pltpu.SemaphoreType.DMA((3 * n,)), pltpu.SemaphoreType.DMA((3 * n,))],
    )(*chip_sums)


def join_halves(reduced):
    n = len(reduced)

    def body(*refs):
        buf_refs = refs[n:2 * n]
        send_sems, recv_sems = refs[2 * n:]
        x, y, c = _position()
        copies = []
        for i in range(n):
            mine, _ = _halves(buf_refs[i].shape[0], c)
            cp = pltpu.make_async_remote_copy(
                src_ref=buf_refs[i].at[mine], dst_ref=buf_refs[i].at[mine],
                send_sem=send_sems.at[i], recv_sem=recv_sems.at[i],
                device_id=(x, y, 1 - c), device_id_type=MESH)
            cp.start()
            copies.append(cp)
        for cp in copies:
            cp.wait()

    return pl.pallas_call(
        body, name="join_halves",
        out_shape=[jax.ShapeDtypeStruct(r.shape, F32) for r in reduced],
        in_specs=_any_specs(n), out_specs=_any_specs(n),
        input_output_aliases={i: i for i in range(n)},
        scratch_shapes=[pltpu.SemaphoreType.DMA((n,)), pltpu.SemaphoreType.DMA((n,))],
    )(*reduced)


ADD_ROWS = 256


def add_halves(grad, theirs, place, name):
    _, half, cols = theirs.shape
    rb = min(ADD_ROWS, half)
    steps = half // rb

    def body(place_ref, a_ref, b_ref, f_ref, h_ref):
        s = a_ref[...] + b_ref[...]
        f_ref[...] = s
        h_ref[...] = _bf(s)

    spec = pl.BlockSpec((1, rb, cols), lambda i, j, place: (i, j, 0))
    return pl.pallas_call(
        body, name=name,
        grid_spec=pltpu.PrefetchScalarGridSpec(
            num_scalar_prefetch=1, grid=(N_CHIPS, steps),
            in_specs=[pl.BlockSpec((1, rb, cols), lambda i, j, place: (i, place[0] * steps + j, 0)), spec],
            out_specs=(spec, spec)),
        out_shape=(jax.ShapeDtypeStruct(theirs.shape, F32), jax.ShapeDtypeStruct(theirs.shape, BF16)),
        compiler_params=_params("parallel", "parallel"),
    )(place, grad, theirs)


def add_parts(chip_sum, got, place, name):
    _, half, cols = got.shape
    rb = min(ADD_ROWS, half)
    steps = half // rb

    def body(place_ref, o_ref, g_ref, out_ref):
        s = o_ref[0]
        for j in range(N_CHIPS - 1):
            s = s + g_ref[j].astype(F32)
        out_ref[...] = s

    return pl.pallas_call(
        body, name=name,
        grid_spec=pltpu.PrefetchScalarGridSpec(
            num_scalar_prefetch=1, grid=(steps,),
            in_specs=[pl.BlockSpec((1, rb, cols), lambda j, place: (place[1], j, 0)),
                      pl.BlockSpec((N_CHIPS - 1, rb, cols), lambda j, place: (0, j, 0))],
            out_specs=pl.BlockSpec((rb, cols), lambda j, place: (place[0] * steps + j, 0))),
        out_shape=jax.ShapeDtypeStruct((2 * half, cols), F32),
        compiler_params=_params("parallel"),
    )(place, chip_sum, got)


def _adam_math(w, g, m, v):
    m = ADAM_B1 * m + (1.0 - ADAM_B1) * g
    v = ADAM_B2 * v + (1.0 - ADAM_B2) * (g * g)
    m_hat = m / (1.0 - ADAM_B1 ** ADAM_STEP)
    v_hat = v / (1.0 - ADAM_B2 ** ADAM_STEP)
    delta = -ADAM_LR * (m_hat / (jnp.sqrt(v_hat) + ADAM_EPS) + ADAM_WD * w)
    return delta, m, v


def adamw(w, g, m, v, name):
    rows, cols = w.shape
    fits = [t for t in range(8, rows, 8) if rows % t == 0 and t * cols * 4 <= 2 ** 20]
    tile = max(fits) if fits else rows

    def body(w_ref, g_ref, m_ref, v_ref, d_ref, nm_ref, nv_ref):
        d, nm, nv = _adam_math(w_ref[...], g_ref[...], m_ref[...], v_ref[...])
        d_ref[...] = d
        nm_ref[...] = nm
        nv_ref[...] = nv

    spec = pl.BlockSpec((tile, cols), lambda i: (i, 0))
    shape = jax.ShapeDtypeStruct((rows, cols), F32)
    return pl.pallas_call(
        body, name=name, grid=(rows // tile,),
        out_shape=(shape, shape, shape),
        in_specs=[spec] * 4, out_specs=(spec, spec, spec),
        compiler_params=_params("parallel"),
    )(w, g, m, v)


def adamw_rows(w, g, m, v, name):
    rows, _, cols = w.shape
    tile = rows // 4

    def body(w_ref, g_ref, m_ref, v_ref, d_ref, nm_ref, nv_ref):
        d, nm, nv = _adam_math(w_ref[...], g_ref[...], m_ref[...], v_ref[...])
        d_ref[...] = d
        nm_ref[...] = nm
        nv_ref[...] = nv

    spec = pl.BlockSpec((tile, 1, cols), lambda i: (i, 0, 0))
    shape = jax.ShapeDtypeStruct(w.shape, F32)
    return pl.pallas_call(
        body, name=name, grid=(rows // tile,),
        out_shape=(shape, shape, shape),
        in_specs=[spec] * 4, out_specs=(spec, spec, spec),
        compiler_params=_params("parallel"),
    )(w, g, m, v)


def matmul_tn(a, b, name, tile_n=512, tile_s=512, by_column_tile=False):
    s, m = a.shape
    n = b.shape[1]
    tile_n = min(tile_n, n)
    steps = s // tile_s
    if by_column_tile:
        out_shape = jax.ShapeDtypeStruct((n // tile_n, m, tile_n), F32)
        out_spec = pl.BlockSpec((None, m, tile_n), lambda j, k: (j, 0, 0))
    else:
        out_shape = jax.ShapeDtypeStruct((m, n), F32)
        out_spec = pl.BlockSpec((m, tile_n), lambda j, k: (0, j))

    def body(a_ref, b_ref, out_ref):
        k = pl.program_id(1)

        @pl.when(k == 0)
        def _():
            out_ref[...] = jnp.zeros_like(out_ref)

        out_ref[...] += _tn(a_ref[...], b_ref[...])

    return pl.pallas_call(
        body, name=name, grid=(n // tile_n, steps),
        out_shape=out_shape,
        in_specs=[pl.BlockSpec((tile_s, m), lambda j, k: (k, 0)),
                  pl.BlockSpec((tile_s, tile_n), lambda j, k: (k, j))],
        out_specs=out_spec,
        compiler_params=_params("parallel", "arbitrary"),
    )(a, b)


ROW_TILE = 512


def _row_index(tile, rows):
    return tile * rows + lax.broadcasted_iota(jnp.int32, (rows, 1), 0)


def _inverse_counts(t_glob):
    return [1.0 / jnp.minimum(t_glob + 1, w).astype(F32) for w in POOL_WINDOWS]


def _sigmoid(z):
    return 1.0 / (1.0 + jnp.exp(-z))


def pool_forward(x, w0, wpi, gw, gb, scale, wpo):
    s = x.shape[0]
    ts = ROW_TILE
    nt = s // ts

    def body(x_ref, w0_ref, wpi_ref, gw_ref, gb_ref, sc_ref, wpo_ref,
             h1_ref, pooled_ref, gt_ref, n0_ref, ubuf):
        i = pl.program_id(0)
        xv = x_ref[...]
        r = lax.rsqrt(jnp.mean(xv * xv, axis=-1, keepdims=True) + EPS)
        n0 = _bf(xv * r * w0_ref[...])
        n0_ref[...] = n0
        u = jnp.concatenate([_nn(n0, wpi_ref[0]), _nn(n0, wpi_ref[1])], axis=-1)
        gt = jnp.concatenate([_nn(n0, wpi_ref[2]), _nn(n0, wpi_ref[3])], axis=-1)
        gt_ref[...] = gt

        @pl.when(i == 0)
        def _():
            ubuf[0:HALO, :] = jnp.zeros((HALO, D), F32)

        ubuf[HALO:HALO + ts, :] = u
        inv = _inverse_counts(_row_index(i, ts))
        mixed = []
        for g, w in enumerate(POOL_WINDOWS):
            cols = slice(g * GROUP_DIM, (g + 1) * GROUP_DIM)
            ug = u[:, cols]
            acc = ug
            for j in range(1, w):
                acc = acc + ubuf[HALO - j:HALO - j + ts, cols]
            pooled = acc * inv[g] - ug
            pooled_ref[:, cols] = pooled
            mixed.append(_nn(_bf(pooled), gw_ref[g]))
        ubuf[0:HALO, :] = ubuf[ts:ts + HALO, :]
        mixed = jnp.concatenate(mixed, axis=-1) + gb_ref[...]
        y = mixed * sc_ref[...] * (gt * _sigmoid(gt))
        h1_ref[...] = xv + _nn(_bf(y), wpo_ref[...])

    row = lambda cols: pl.BlockSpec((ts, cols), lambda i: (i, 0))
    return pl.pallas_call(
        body, name="pool_forward", grid=(nt,),
        out_shape=(jax.ShapeDtypeStruct((s, D), F32), jax.ShapeDtypeStruct((s, D), F32),
                   jax.ShapeDtypeStruct((s, D), F32), jax.ShapeDtypeStruct((s, D), BF16)),
        in_specs=[row(D), _full((1, D)), _full((N_CHIPS, D, D // 2)), _full((GROUPS, GROUP_DIM, GROUP_DIM)),
                  _full((1, D)), _full((1, D)), _full((D, D))],
        out_specs=(row(D), row(D), row(D), row(D)),
        scratch_shapes=[pltpu.VMEM((HALO + ts, D), F32)],
        compiler_params=_params("arbitrary"),
    )(x, w0, wpi, gw, gb, scale, wpo)


def pool_backward(x, dh1, pooled, gt, w0, wpi, gw, gb, scale, wpo):
    s = x.shape[0]
    ts = ROW_TILE
    nt = s // ts

    def body(x_ref, dh1_ref, pooled_ref, gt_ref, w0_ref, wpi_ref, gw_ref, gb_ref, sc_ref, wpo_ref,
             dx_ref, y_ref, dproj_ref, ggw_ref, small_ref, ebuf):
        i = pl.program_id(0)

        @pl.when(i == 0)
        def _():
            ggw_ref[...] = jnp.zeros_like(ggw_ref)
            small_ref[...] = jnp.zeros_like(small_ref)
            ebuf[ts:ts + HALO, :] = jnp.zeros((HALO, D), F32)

        dh1 = dh1_ref[...]
        gt = gt_ref[...]
        sc = sc_ref[...]
        dy = _nt(_bf(dh1), wpo_ref[...])
        pooled_bf = []
        mixed = []
        for g in range(GROUPS):
            cols = slice(g * GROUP_DIM, (g + 1) * GROUP_DIM)
            pb = _bf(pooled_ref[:, cols])
            pooled_bf.append(pb)
            mixed.append(_nn(pb, gw_ref[g]))
        mixed = jnp.concatenate(mixed, axis=-1) + gb_ref[...]
        sg = _sigmoid(gt)
        silu = gt * sg
        y_ref[...] = _bf(mixed * sc * silu)
        dmixed = dy * sc * silu
        dgt = dy * mixed * sc * (sg * (1.0 + gt * (1.0 - sg)))
        dproj_ref[:, D:] = _bf(dgt)
        small_ref[1:2, :] += jnp.sum(dy * mixed * silu, axis=0, keepdims=True)
        small_ref[2:3, :] += jnp.sum(dmixed, axis=0, keepdims=True)

        inv = _inverse_counts(_row_index(nt - 1 - i, ts))
        dpooled = []
        for g in range(GROUPS):
            cols = slice(g * GROUP_DIM, (g + 1) * GROUP_DIM)
            dm = _bf(dmixed[:, cols])
            ggw_ref[g] += _tn(pooled_bf[g], dm)
            dp = _nt(dm, gw_ref[g])
            dpooled.append(dp)
            ebuf[0:ts, cols] = dp * inv[g]
        du = []
        for g, w in enumerate(POOL_WINDOWS):
            cols = slice(g * GROUP_DIM, (g + 1) * GROUP_DIM)
            acc = -dpooled[g]
            for j in range(w):
                acc = acc + ebuf[j:j + ts, cols]
            du.append(acc)
        ebuf[ts:ts + HALO, :] = ebuf[0:HALO, :]
        du = _bf(jnp.concatenate(du, axis=-1))
        dproj_ref[:, :D] = du
        dgt_bf = _bf(dgt)
        half = D // 2
        dn0 = (_nt(du[:, :half], wpi_ref[0]) + _nt(du[:, half:], wpi_ref[1])
               + _nt(dgt_bf[:, :half], wpi_ref[2]) + _nt(dgt_bf[:, half:], wpi_ref[3]))

        xv = x_ref[...]
        r = lax.rsqrt(jnp.mean(xv * xv, axis=-1, keepdims=True) + EPS)
        xhat = xv * r
        small_ref[0:1, :] += jnp.sum(dn0 * xhat, axis=0, keepdims=True)
        dxh = dn0 * w0_ref[...]
        dx_ref[...] = dh1 + r * (dxh - xhat * jnp.mean(dxh * xhat, axis=-1, keepdims=True))

    row = lambda cols: pl.BlockSpec((ts, cols), lambda i: (nt - 1 - i, 0))
    return pl.pallas_call(
        body, name="pool_backward", grid=(nt,),
        out_shape=(jax.ShapeDtypeStruct((s, D), F32), jax.ShapeDtypeStruct((s, D), BF16),
                   jax.ShapeDtypeStruct((s, 2 * D), BF16),
                   jax.ShapeDtypeStruct((GROUPS, GROUP_DIM, GROUP_DIM), F32),
                   jax.ShapeDtypeStruct((8, D), F32)),
        in_specs=[row(D), row(D), row(D), row(D), _full((1, D)), _full((N_CHIPS, D, D // 2)),
                  _full((GROUPS, GROUP_DIM, GROUP_DIM)), _full((1, D)), _full((1, D)), _full((D, D))],
        out_specs=(row(D), row(D), row(2 * D), _full((GROUPS, GROUP_DIM, GROUP_DIM)), _full((8, D))),
        scratch_shapes=[pltpu.VMEM((ts + HALO, D), F32)],
        compiler_params=_params("arbitrary"),
    )(x, dh1, pooled, gt, w0, wpi, gw, gb, scale, wpo)


def gla_project(h1, w1, wgi, wlow, wgk, bgk):
    s = h1.shape[0]
    ts = ROW_TILE

    def body(h_ref, w1_ref, wgi_ref, wlow_ref, wgk_ref, bgk_ref, proj_ref, low_ref, cum_ref, n1_ref):
        hv = h_ref[...]
        r = lax.rsqrt(jnp.mean(hv * hv, axis=-1, keepdims=True) + EPS)
        n1 = _bf(hv * r * w1_ref[...])
        n1_ref[...] = n1
        proj_ref[...] = _nn(n1, wgi_ref[...])
        low = _nn(n1, wlow_ref[...])
        low_ref[...] = low
        z = _nn(_bf(low), wgk_ref[...]) + bgk_ref[...]
        lg = (jnp.minimum(z, 0.0) - jnp.log(1.0 + jnp.exp(-jnp.abs(z)))) / GATE_NORM
        lower_f = _chunk_masks()[0].astype(F32)
        for r0 in range(0, ts, CHUNK):
            cum_ref[r0:r0 + CHUNK, :] = _nn_exact(lower_f, lg[r0:r0 + CHUNK, :])

    row = lambda cols: pl.BlockSpec((ts, cols), lambda i: (i, 0))
    return pl.pallas_call(
        body, name="gla_project", grid=(s // ts,),
        out_shape=(jax.ShapeDtypeStruct((s, GLA_MAIN), F32), jax.ShapeDtypeStruct((s, RANK_PAD), F32),
                   jax.ShapeDtypeStruct((s, KEY_W), F32), jax.ShapeDtypeStruct((s, D), BF16)),
        in_specs=[row(D), _full((1, D)), _full((D, GLA_MAIN)), _full((D, RANK_PAD)),
                  _full((RANK_PAD, KEY_W)), _full((1, KEY_W))],
        out_specs=(row(GLA_MAIN), row(RANK_PAD), row(KEY_W), row(D)),
        compiler_params=_params("parallel"),
    )(h1, w1, wgi, wlow, wgk, bgk)


GLA_BLOCK = 512
CHUNKS_PER_BLOCK = GLA_BLOCK // CHUNK


def _chunk_masks():
    t = lax.broadcasted_iota(jnp.int32, (CHUNK, CHUNK), 0)
    u = lax.broadcasted_iota(jnp.int32, (CHUNK, CHUNK), 1)
    return t >= u, t <= u


def _gla_chunk_terms(q, cum):
    ep = jnp.exp(cum)
    en = jnp.exp(-cum)
    qs = q * (HEAD_K ** -0.5)
    last = cum[CHUNK - 1:CHUNK, :]
    ed = jnp.exp(last - cum)
    dec = jnp.exp(last)
    return ep, en, qs, ed, dec


def gla_forward(proj, cum):
    s = proj.shape[0]
    nb = s // GLA_BLOCK
    nc = s // CHUNK

    def body(q_ref, k_ref, v_ref, cum_ref, o_ref, st_ref, state):
        @pl.when(pl.program_id(0) == 0)
        def _():
            state[...] = jnp.zeros_like(state)

        lower, _ = _chunk_masks()

        def chunk(cc, carry):
            rows = pl.ds(pl.multiple_of(cc * CHUNK, CHUNK), CHUNK)
            for h in range(HEADS):
                kc = slice(h * HEAD_K, (h + 1) * HEAD_K)
                vc = slice(h * HEAD_V, (h + 1) * HEAD_V)
                q = q_ref[rows, kc]
                k = k_ref[rows, kc]
                v = _bf(v_ref[rows, vc])
                ep, en, qs, ed, dec = _gla_chunk_terms(q, cum_ref[rows, kc])
                a = _bf(qs * ep)
                fwd = _nt(a, _bf(k * en))
                bwd = _nt(_bf(qs * en), _bf(k * ep))
                scores = jnp.where(lower, fwd, bwd)
                st = state[h]
                st_ref[cc, h] = st
                o_ref[rows, vc] = _nn(_bf(scores), v) + _nt(a, _bf(st))
                state[h] = st * dec + _tn(v, _bf(k * ed))
            return carry

        lax.fori_loop(0, CHUNKS_PER_BLOCK, chunk, 0, unroll=4)

    return pl.pallas_call(
        body, name="gla_forward", grid=(nb,),
        out_shape=(jax.ShapeDtypeStruct((s, D), F32),
                   jax.ShapeDtypeStruct((nc, HEADS, HEAD_V, HEAD_K), F32)),
        in_specs=[pl.BlockSpec((GLA_BLOCK, KEY_W), lambda i: (i, 0)),
                  pl.BlockSpec((GLA_BLOCK, KEY_W), lambda i: (i, 1)),
                  pl.BlockSpec((GLA_BLOCK, D), lambda i: (i, 1)),
                  pl.BlockSpec((GLA_BLOCK, KEY_W), lambda i: (i, 0))],
        out_specs=(pl.BlockSpec((GLA_BLOCK, D), lambda i: (i, 0)),
                   pl.BlockSpec((CHUNKS_PER_BLOCK, HEADS, HEAD_V, HEAD_K), lambda i: (i, 0, 0, 0))),
        scratch_shapes=[pltpu.VMEM((HEADS, HEAD_V, HEAD_K), F32)],
        compiler_params=_params("arbitrary"),
    )(proj, proj, proj, cum)


def gla_backward(proj, cum, do, states):
    s = proj.shape[0]
    nb = s // GLA_BLOCK

    def body(q_ref, k_ref, v_ref, cum_ref, do_ref, st_ref, dq_ref, dk_ref, dv_ref, dcum_ref, dstate):
        @pl.when(pl.program_id(0) == 0)
        def _():
            dstate[...] = jnp.zeros_like(dstate)

        lower, _ = _chunk_masks()
        is_last = lax.broadcasted_iota(jnp.int32, (CHUNK, HEAD_K), 0) == CHUNK - 1

        def chunk(step, carry):
            cc = CHUNKS_PER_BLOCK - 1 - step
            rows = pl.ds(pl.multiple_of(cc * CHUNK, CHUNK), CHUNK)
            for h in range(HEADS):
                kc = slice(h * HEAD_K, (h + 1) * HEAD_K)
                vc = slice(h * HEAD_V, (h + 1) * HEAD_V)
                q = q_ref[rows, kc]
                k = k_ref[rows, kc]
                v = _bf(v_ref[rows, vc])
                do_c = _bf(do_ref[rows, vc])
                ep, en, qs, ed, dec = _gla_chunk_terms(q, cum_ref[rows, kc])
                a = _bf(qs * ep)
                b = _bf(k * en)
                c = _bf(qs * en)
                dk_dec = _bf(k * ep)
                kd = _bf(k * ed)
                scores = _bf(jnp.where(lower, _nt(a, b), _nt(c, dk_dec)))
                st = st_ref[cc, h]
                dst = dstate[h]
                dst_bf = _bf(dst)

                dscores = _nt(do_c, v)
                dfwd = _bf(jnp.where(lower, dscores, 0.0))
                dbwd = _bf(jnp.where(lower, 0.0, dscores))
                dv_ref[rows, vc] = _tn(scores, do_c) + _nt(kd, dst_bf)
                da = _nn(dfwd, b) + _nn(do_c, _bf(st))
                db = _tn(dfwd, a)
                dc = _nn(dbwd, dk_dec)
                ddk = _tn(dbwd, c)
                dkd = _nn(v, dst_bf)
                ddec = jnp.sum(dst * st, axis=0, keepdims=True)
                dstate[h] = dst * dec + _tn(do_c, a)

                m = dkd * k * ed
                dq_ref[rows, kc] = (da * ep + dc * en) * (HEAD_K ** -0.5)
                dk_ref[rows, kc] = db * en + ddk * ep + dkd * ed
                dcum = (da * qs + ddk * k) * ep - (db * k + dc * qs) * en - m
                dlast = jnp.sum(m, axis=0, keepdims=True) + ddec * dec
                dcum_ref[rows, kc] = dcum + jnp.where(is_last, dlast, 0.0)
            return carry

        lax.fori_loop(0, CHUNKS_PER_BLOCK, chunk, 0, unroll=4)

    rev = lambda cols, col_block: pl.BlockSpec((GLA_BLOCK, cols), lambda i: (nb - 1 - i, col_block))
    return pl.pallas_call(
        body, name="gla_backward", grid=(nb,),
        out_shape=(jax.ShapeDtypeStruct((s, KEY_W), F32), jax.ShapeDtypeStruct((s, KEY_W), F32),
                   jax.ShapeDtypeStruct((s, D), F32), jax.ShapeDtypeStruct((s, KEY_W), F32)),
        in_specs=[rev(KEY_W, 0), rev(KEY_W, 1), rev(D, 1), rev(KEY_W, 0), rev(D, 0),
                  pl.BlockSpec((CHUNKS_PER_BLOCK, HEADS, HEAD_V, HEAD_K), lambda i: (nb - 1 - i, 0, 0, 0))],
        out_specs=(rev(KEY_W, 0), rev(KEY_W, 0), rev(D, 0), rev(KEY_W, 0)),
        scratch_shapes=[pltpu.VMEM((HEADS, HEAD_V, HEAD_K), F32)],
        compiler_params=_params("arbitrary"),
    )(proj, proj, proj, cum, do, states)


def head_and_loss(o, proj, h1, target, hw, wgo, wf):
    s = o.shape[0]
    ts = ROW_TILE

    def body(o_ref, gate_ref, h1_ref, tgt_ref, hw_ref, wgo_ref, wf_ref,
             dh2_ref, dh2bf_ref, do_ref, dgate_ref, y2_ref, small_ref):
        @pl.when(pl.program_id(0) == 0)
        def _():
            small_ref[...] = jnp.zeros_like(small_ref)

        gate = gate_ref[...]
        hw = hw_ref[...]
        sg = _sigmoid(gate)
        silu = gate * sg
        ohat, ro = [], []
        for h in range(HEADS):
            oh = o_ref[:, h * HEAD_V:(h + 1) * HEAD_V]
            rh = lax.rsqrt(jnp.mean(oh * oh, axis=-1, keepdims=True) + EPS)
            ro.append(rh)
            ohat.append(oh * rh)
        ohat = jnp.concatenate(ohat, axis=-1)
        on = ohat * hw
        y2 = _bf(on * silu)
        y2_ref[...] = y2
        h2 = h1_ref[...] + _nn(y2, wgo_ref[...])
        rf = lax.rsqrt(jnp.mean(h2 * h2, axis=-1, keepdims=True) + EPS)
        h2hat = h2 * rf
        wf = wf_ref[...]
        diff = h2hat * wf - tgt_ref[...]
        small_ref[2:3, :] += jnp.zeros((1, D), F32) + 0.5 * jnp.sum(diff * diff) / D
        dout = diff / D
        small_ref[0:1, :] += jnp.sum(dout * h2hat, axis=0, keepdims=True)
        dxh = dout * wf
        dh2 = rf * (dxh - h2hat * jnp.mean(dxh * h2hat, axis=-1, keepdims=True))
        dh2_ref[...] = dh2
        dh2_bf = _bf(dh2)
        dh2bf_ref[...] = dh2_bf
        dy2 = _nt(dh2_bf, wgo_ref[...])
        don = dy2 * silu
        dgate_ref[...] = dy2 * on * (sg * (1.0 + gate * (1.0 - sg)))
        ghw = jnp.sum(don * ohat, axis=0, keepdims=True)
        small_ref[1:2, 0:HEAD_V] += sum(ghw[:, h * HEAD_V:(h + 1) * HEAD_V] for h in range(HEADS))
        dohat = don * hw
        for h in range(HEADS):
            cols = slice(h * HEAD_V, (h + 1) * HEAD_V)
            oh, dh = ohat[:, cols], dohat[:, cols]
            do_ref[:, cols] = ro[h] * (dh - oh * jnp.mean(dh * oh, axis=-1, keepdims=True))

    row = lambda cols: pl.BlockSpec((ts, cols), lambda i: (i, 0))
    act = jax.ShapeDtypeStruct((s, D), F32)
    act_bf = jax.ShapeDtypeStruct((s, D), BF16)
    return pl.pallas_call(
        body, name="head_and_loss", grid=(s // ts,),
        out_shape=(act, act_bf, act, act, act_bf, jax.ShapeDtypeStruct((8, D), F32)),
        in_specs=[row(D), pl.BlockSpec((ts, D), lambda i: (i, 2)), row(D), row(D),
                  _full((1, D)), _full((D, D)), _full((1, D))],
        out_specs=(row(D), row(D), row(D), row(D), row(D), _full((8, D))),
        compiler_params=_params("arbitrary"),
    )(o, proj, h1, target, hw, wgo, wf)


def gla_project_backward(dq, dk, dv, dgate, dcum, low, h1, dh2, w1, wgi, wlow, wgk, bgk):
    s = h1.shape[0]
    ts = ROW_TILE

    def body(dq_ref, dk_ref, dv_ref, dgate_ref, dcum_ref, low_ref, h1_ref, dh2_ref, w1_ref,
             wgi_ref, wlow_ref, wgk_ref, bgk_ref, dh1_ref, dh1bf_ref, dproj_ref, dlow_ref, ggk_ref,
             small_ref):
        @pl.when(pl.program_id(0) == 0)
        def _():
            ggk_ref[...] = jnp.zeros_like(ggk_ref)
            small_ref[...] = jnp.zeros_like(small_ref)

        low = _bf(low_ref[...])
        z = _nn(low, wgk_ref[...]) + bgk_ref[...]
        upper_f = _chunk_masks()[1].astype(F32)
        dlg = jnp.concatenate([_nn_exact(upper_f, dcum_ref[r0:r0 + CHUNK, :]) for r0 in range(0, ts, CHUNK)],
                              axis=0)
        dz = dlg * (1.0 / GATE_NORM) * _sigmoid(-z)
        dz_bf = _bf(dz)
        ggk_ref[...] += _tn(low, dz_bf)
        small_ref[1:2, 0:KEY_W] += jnp.sum(dz, axis=0, keepdims=True)
        dlow = _bf(_nt(dz_bf, wgk_ref[...]))
        dlow_ref[...] = dlow
        dn1 = _nt(dlow, wlow_ref[...])
        for ref, lo, hi in ((dq_ref, 0, KEY_W), (dk_ref, KEY_W, 2 * KEY_W),
                            (dv_ref, 2 * KEY_W, 2 * KEY_W + D), (dgate_ref, 2 * KEY_W + D, GLA_MAIN)):
            piece = _bf(ref[...])
            dproj_ref[:, lo:hi] = piece
            dn1 = dn1 + _nt(piece, wgi_ref[:, lo:hi])
        hv = h1_ref[...]
        r = lax.rsqrt(jnp.mean(hv * hv, axis=-1, keepdims=True) + EPS)
        hhat = hv * r
        small_ref[0:1, :] += jnp.sum(dn1 * hhat, axis=0, keepdims=True)
        dxh = dn1 * w1_ref[...]
        dh1 = dh2_ref[...] + r * (dxh - hhat * jnp.mean(dxh * hhat, axis=-1, keepdims=True))
        dh1_ref[...] = dh1
        dh1bf_ref[...] = _bf(dh1)

    row = lambda cols: pl.BlockSpec((ts, cols), lambda i: (i, 0))
    return pl.pallas_call(
        body, name="gla_project_backward", grid=(s // ts,),
        out_shape=(jax.ShapeDtypeStruct((s, D), F32), jax.ShapeDtypeStruct((s, D), BF16),
                   jax.ShapeDtypeStruct((s, GLA_MAIN), BF16),
                   jax.ShapeDtypeStruct((s, RANK_PAD), BF16), jax.ShapeDtypeStruct((RANK_PAD, KEY_W), F32),
                   jax.ShapeDtypeStruct((8, D), F32)),
        in_specs=[row(KEY_W), row(KEY_W), row(D), row(D), row(KEY_W), row(RANK_PAD), row(D), row(D),
                  _full((1, D)), _full((D, GLA_MAIN)), _full((D, RANK_PAD)), _full((RANK_PAD, KEY_W)),
                  _full((1, KEY_W))],
        out_specs=(row(D), row(D), row(GLA_MAIN), row(RANK_PAD), _full((RANK_PAD, KEY_W)), _full((8, D))),
        compiler_params=_params("arbitrary"),
    )(dq, dk, dv, dgate, dcum, low, h1, dh2, w1, wgi, wlow, wgk, bgk)


def _groups_from_quarters(a):
    return a.reshape(N_CHIPS, GROUPS, 64, GROUP_DIM).transpose(1, 0, 2, 3).reshape(GROUPS, GROUP_DIM, GROUP_DIM)


def _quarters_from_groups(a):
    return a.reshape(GROUPS, N_CHIPS, 64, GROUP_DIM).transpose(1, 0, 2, 3).reshape(N_CHIPS, GROUP_DIM, GROUP_DIM)


def _pad_row(*pieces):
    flat = jnp.concatenate([p.reshape(-1).astype(F32) for p in pieces])
    return jnp.pad(flat, (0, D - flat.shape[0])).reshape(1, D)


def local_gradients(xs, target, w0, w1, wf, wpi, gw, gb, scale, wpo, wgi, wlow, wgk, bgk, hw_tiled, wgo):
    h1, pooled, gt, n0 = pool_forward(xs, w0, wpi, gw, gb, scale, wpo)
    proj, low, cum, n1 = gla_project(h1, w1, wgi, wlow, wgk, bgk)
    o, states = gla_forward(proj, cum)

    dh2, dh2_bf, do, dgate, y2, small_top = head_and_loss(o, proj, h1, target, hw_tiled, wgo, wf)
    g_gla_out = matmul_tn(y2, dh2_bf, "grad_gla_out")
    dq, dk, dv, dcum = gla_backward(proj, cum, do, states)
    dh1, dh1_bf, dproj, dlow, g_gk_pad, small_gla = gla_project_backward(
        dq, dk, dv, dgate, dcum, low, h1, dh2, w1, wgi, wlow, wgk, bgk)
    g_gla_in = jnp.concatenate([matmul_tn(n1, dproj, "grad_gla_in"),
                                matmul_tn(n1, dlow, "grad_gla_low")[:, :GATE_RANK]], axis=1)
    dx, y, dpool, g_group_w, small_pool = pool_backward(
        xs, dh1, pooled, gt, w0, wpi, gw, gb, scale, wpo)
    g_pool_out = matmul_tn(y, dh1_bf, "grad_pool_out")
    g_pool_in = matmul_tn(n0, dpool, "grad_pool_in", by_column_tile=True)
    return (dx, g_pool_in, g_group_w, g_pool_out, g_gla_in, g_gla_out, g_gk_pad,
            small_top, small_gla, small_pool)


def kernel(x, norm_w, pool_in_w, pool_group_w, pool_group_b, pool_scale, pool_out_w, gla_in_w, gla_gk_w, gla_gk_b, gla_head_norm_w, gla_out_w, final_norm_w, loss_target, m_norm_w, m_pool_in_w, m_pool_group_w, m_pool_group_b, m_pool_scale, m_pool_out_w, m_gla_in_w, m_gla_gk_w, m_gla_gk_b, m_gla_head_norm_w, m_gla_out_w, m_final_norm_w, v_norm_w, v_pool_in_w, v_pool_group_w, v_pool_group_b, v_pool_scale, v_pool_out_w, v_gla_in_w, v_gla_gk_w, v_gla_gk_b, v_gla_head_norm_w, v_gla_out_w, v_final_norm_w):
    s = x.shape[1]
    xs = x[0]
    target = loss_target[0]
    q_chip = 2 * lax.axis_index("x") + lax.axis_index("y")
    place = jnp.stack([lax.axis_index("c"), q_chip]).astype(jnp.int32)

    wpi, gw_q, wpo_q, wgi_q, wgo_q = allgather_weights([
        pool_in_w[0], pool_group_w[0].reshape(GROUP_DIM, GROUP_DIM), pool_out_w[0], gla_in_w[0], gla_out_w[0]])
    gw = _groups_from_quarters(gw_q)
    wpo = wpo_q.reshape(D, D)
    wgo = wgo_q.reshape(D, D)
    wgi_all = jnp.concatenate([wgi_q[q] for q in range(N_CHIPS)], axis=1)
    wgi = wgi_all[:, :GLA_MAIN]
    wlow = jnp.pad(wgi_all[:, GLA_MAIN:], ((0, 0), (0, RANK_PAD - GATE_RANK)))

    small_in = jnp.concatenate([
        _pad_row(gla_gk_b[0], gla_head_norm_w[0], pool_group_b[0]),
        gla_gk_w[0].reshape(2, D),
        jnp.zeros((5, D), F32)], axis=0)
    small_all = gather_small(small_in, "gather_small_weights")[0::2]
    bgk = small_all[:, 0, 0:128].reshape(1, KEY_W)
    hw = small_all[:, 0, 128:192].reshape(1, HEAD_V)
    gb = jnp.concatenate([small_all[q, 0, 192:448].reshape(GROUPS, 64) for q in range(N_CHIPS)],
                         axis=1).reshape(1, D)
    wgk16 = jnp.concatenate([small_all[q, 1:3].reshape(GATE_RANK, 128) for q in range(N_CHIPS)], axis=1)
    wgk = _bf(jnp.pad(wgk16, ((0, RANK_PAD - GATE_RANK), (0, 0))))
    hw_tiled = jnp.tile(hw, (1, HEADS))

    w0 = norm_w[0:1]
    w1 = norm_w[1:2]
    wf = final_norm_w.reshape(1, D)

    (dx, g_pool_in, g_group_w, g_pool_out, g_gla_in, g_gla_out, g_gk_pad,
     small_top, small_gla, small_pool) = local_gradients(
        xs, target, w0, w1, wf, wpi, gw, gb, pool_scale, wpo, wgi, wlow, wgk, bgk, hw_tiled, wgo)

    names = ("pool_in", "group", "pool_out", "gla_in", "gla_out")
    grads = [g_pool_in, _quarters_from_groups(g_group_w), g_pool_out.reshape(N_CHIPS, D // N_CHIPS, D),
             jnp.stack([g_gla_in[:, GLA_IN_QUARTER * q:GLA_IN_QUARTER * (q + 1)] for q in range(N_CHIPS)]),
             g_gla_out.reshape(N_CHIPS, D // N_CHIPS, D)]
    theirs = exchange_with_sibling(grads)
    sums = [add_halves(g, t, place, "add_halves_" + n) for g, t, n in zip(grads, theirs, names)]
    got = scatter_to_owners([b for _, b in sums])
    reduced = join_halves([add_parts(f, g, place, "add_parts_" + n) for (f, _), g, n in zip(sums, got, names)])
    r_pool_in, r_group_w, r_pool_out, r_gla_in, r_gla_out = reduced
    r_group_w = r_group_w.reshape(GROUPS, 64, GROUP_DIM)

    small_out = jnp.concatenate([
        small_pool[0:1], small_gla[0:1],
        small_pool[1:2],
        small_top[0:1],
        small_top[2:3],
        _pad_row(small_gla[1, 0:KEY_W], small_top[1, 0:HEAD_V]),
        small_pool[2:3],
        g_gk_pad[:GATE_RANK].reshape(8, D),
        jnp.zeros((1, D), F32)], axis=0)
    total = gather_small(small_out, "allreduce_small_grads", reduce=True)
    loss = total[4, 0]
    g_norm = total[0:2]
    g_scale = total[2:3]
    g_final = total[3]
    pick = lambda full, width: lax.dynamic_slice_in_dim(full, q_chip * width, width, axis=-1)
    g_gk_b = pick(total[5:6, 0:KEY_W], 128)
    g_hnw = pick(total[5:6, KEY_W:KEY_W + HEAD_V], 64)
    g_group_b = pick(total[6].reshape(GROUPS, GROUP_DIM), 64)[None]
    g_gk_w = pick(total[7:15].reshape(GATE_RANK, KEY_W), 128)[None]

    def step_lane_rows(name, w, g, m, v):
        turn = lambda a: jnp.transpose(a, (2, 0, 1))
        back = lambda a: jnp.transpose(a, (1, 2, 0))
        g_t = turn(g)
        d, nm, nv = adamw_rows(turn(w), g_t, turn(m), turn(v), "adamw_" + name)
        return back(g_t), back(d), back(nm), back(nv)

    def step(name, w, g, m, v):
        shape = w.shape
        as2d = lambda a: a.reshape(-1, shape[-1])
        d, nm, nv = adamw(as2d(w), as2d(g), as2d(m), as2d(v), "adamw_" + name)
        return g.reshape(shape), d.reshape(shape), nm.reshape(shape), nv.reshape(shape)

    results = [
        step("norm_w", norm_w, g_norm, m_norm_w, v_norm_w),
        step("pool_in_w", pool_in_w, r_pool_in[None], m_pool_in_w, v_pool_in_w),
        step("pool_group_w", pool_group_w, r_group_w[None], m_pool_group_w, v_pool_group_w),
        step("pool_group_b", pool_group_b, g_group_b, m_pool_group_b, v_pool_group_b),
        step("pool_scale", pool_scale, g_scale, m_pool_scale, v_pool_scale),
        step("pool_out_w", pool_out_w, r_pool_out[None], m_pool_out_w, v_pool_out_w),
        step_lane_rows("gla_in_w", gla_in_w, r_gla_in[None], m_gla_in_w, v_gla_in_w),
        step("gla_gk_w", gla_gk_w, g_gk_w, m_gla_gk_w, v_gla_gk_w),
        step("gla_gk_b", gla_gk_b, g_gk_b, m_gla_gk_b, v_gla_gk_b),
        step("gla_head_norm_w", gla_head_norm_w, g_hnw, m_gla_head_norm_w, v_gla_head_norm_w),
        step("gla_out_w", gla_out_w, r_gla_out[None], m_gla_out_w, v_gla_out_w),
        step("final_norm_w", final_norm_w, g_final, m_final_norm_w, v_final_norm_w),
    ]
    grads, deltas, new_m, new_v = zip(*results)
    return (loss, dx[None], *grads, *deltas, *new_m, *new_v)
```

```python
import functools

import jax
import jax.numpy as jnp
from jax import lax
from jax.experimental import pallas as pl
from jax.experimental.pallas import tpu as pltpu

F32 = jnp.float32
BF16 = jnp.bfloat16
MESH = pl.DeviceIdType.MESH

D = 1024
POOL_WINDOWS = (2, 4, 8, 16)
GROUPS = 4
GROUP_DIM = 256
HEADS = 4
HEAD_K = 128
HEAD_V = 256
KEY_W = 512
CHUNK = 64
GATE_RANK = 16
GATE_NORM = 16.0
GLA_IN = 3088
GLA_MAIN = 3072
RANK_PAD = 128
EPS = 1e-6
HALO = 16

ADAM_LR = 0.001
ADAM_B1 = 0.9
ADAM_B2 = 0.999
ADAM_EPS = 1e-08
ADAM_WD = 0.01
ADAM_STEP = 10

N_CHIPS = 4
N_DEV = 8
GLA_IN_QUARTER = GLA_IN // N_CHIPS

VMEM_LIMIT = 56 * 1024 * 1024


def _nn(a, b):
    return lax.dot_general(a, b, (((1,), (0,)), ((), ())), preferred_element_type=F32)


def _nt(a, b):
    return lax.dot_general(a, b, (((1,), (1,)), ((), ())), preferred_element_type=F32)


def _tn(a, b):
    return lax.dot_general(a, b, (((0,), (0,)), ((), ())), preferred_element_type=F32)


def _nn_exact(a, b):
    return lax.dot_general(a, b, (((1,), (0,)), ((), ())), preferred_element_type=F32,
                           precision=lax.Precision.HIGHEST)


def _bf(a):
    return a.astype(BF16)


def _params(*sem):
    return pltpu.CompilerParams(dimension_semantics=sem, vmem_limit_bytes=VMEM_LIMIT)


def _full(shape):
    return pl.BlockSpec(shape, lambda i: (0,) * len(shape))


def _position():
    return lax.axis_index("x"), lax.axis_index("y"), lax.axis_index("c")


def gather_small(block, name, reduce=False):
    rows, cols = block.shape

    def body(in_ref, out_ref, *scratch):
        if reduce:
            all_ref, send_sems, recv_sems, local_sem = scratch
        else:
            all_ref = out_ref
            send_sems, recv_sems, local_sem = scratch
        x, y, c = _position()
        me = 4 * x + 2 * y + c
        mine = pltpu.make_async_copy(in_ref, all_ref.at[me], local_sem)
        mine.start()
        sends = []
        for k in range(N_DEV - 1):
            fx, fy, fc = (k + 1) >> 2 & 1, (k + 1) >> 1 & 1, (k + 1) & 1
            cp = pltpu.make_async_remote_copy(
                src_ref=in_ref, dst_ref=all_ref.at[me],
                send_sem=send_sems.at[k], recv_sem=recv_sems.at[k],
                device_id=(x ^ fx, y ^ fy, c ^ fc), device_id_type=MESH)
            cp.start()
            sends.append(cp)
        for k in range(N_DEV - 1):
            fx, fy, fc = (k + 1) >> 2 & 1, (k + 1) >> 1 & 1, (k + 1) & 1
            src_dev = 4 * (x ^ fx) + 2 * (y ^ fy) + (c ^ fc)
            pltpu.make_async_remote_copy(
                src_ref=in_ref, dst_ref=all_ref.at[src_dev],
                send_sem=send_sems.at[k], recv_sem=recv_sems.at[k],
                device_id=(x, y, c), device_id_type=MESH).wait_recv()
        for cp in sends:
            cp.wait_send()
        mine.wait()
        if reduce:
            total = all_ref[0]
            for dev in range(1, N_DEV):
                total = total + all_ref[dev]
            out_ref[...] = total

    sems = [pltpu.SemaphoreType.DMA((N_DEV - 1,)), pltpu.SemaphoreType.DMA((N_DEV - 1,)),
            pltpu.SemaphoreType.DMA]
    gathered = (N_DEV, rows, cols)
    return pl.pallas_call(
        body, name=name,
        out_shape=jax.ShapeDtypeStruct((rows, cols) if reduce else gathered, block.dtype),
        in_specs=[pl.BlockSpec(memory_space=pltpu.VMEM)],
        out_specs=pl.BlockSpec(memory_space=pltpu.VMEM),
        scratch_shapes=([pltpu.VMEM(gathered, block.dtype)] if reduce else []) + sems,
    )(block)


def _other_chips(x, y):
    return [(1 - x, y), (x, 1 - y), (1 - x, 1 - y)]


def _any_specs(n):
    return [pl.BlockSpec(memory_space=pl.ANY)] * n


def _halves(rows, c):
    half = rows // 2
    return pl.ds(c * half, half), pl.ds((1 - c) * half, half)


CAST_ROWS = 256


def _gather_copy(out_ref, send_sems, recv_sems, k, quarter, half, to, src=None):
    dst = out_ref.at[quarter, half]
    return pltpu.make_async_remote_copy(
        src_ref=dst if src is None else src, dst_ref=dst,
        send_sem=send_sems.at[k], recv_sem=recv_sems.at[k], device_id=to, device_id_type=MESH)


def allgather_weights(quarters, exchange):
    n = len(quarters)
    shapes = [w.shape for w in quarters]
    moved = [i for i in range(n) if exchange[i]]

    def body(*refs):
        w_refs, out_refs = refs[:n], refs[n:2 * n]
        f32_bufs, bf_bufs = refs[2 * n:3 * n], refs[3 * n:4 * n]
        send_sems, recv_sems, local_sems = refs[4 * n:]
        x, y, c = _position()
        q = 2 * x + y
        sibling = (x, y, 1 - c)
        chips = _other_chips(x, y)

        def copy(k, i, quarter, half, to, src=None):
            return _gather_copy(out_refs[i], send_sems, recv_sems, k * n + i, quarter, half, to, src)

        loads = [pltpu.make_async_copy(w_refs[i], f32_bufs[i], local_sems.at[i]) for i in range(n)]
        for cp in loads:
            cp.start()
        keeps, sends = [], []
        for i in range(n):
            loads[i].wait()
            for r0 in range(0, shapes[i][0], CAST_ROWS):
                bf_bufs[i][r0:r0 + CAST_ROWS, :] = _bf(f32_bufs[i][r0:r0 + CAST_ROWS, :])
            keep = pltpu.make_async_copy(bf_bufs[i], out_refs[i].at[q], local_sems.at[n + i])
            keep.start()
            keeps.append(keep)
            if not exchange[i]:
                continue
            mine, _ = _halves(shapes[i][0], c)
            for j, chip in enumerate(chips):
                cp = copy(j, i, q, mine, (*chip, c), src=bf_bufs[i].at[mine])
                cp.start()
                sends.append(cp)
        for j, chip in enumerate(chips):
            qj = 2 * chip[0] + chip[1]
            for i in moved:
                mine, _ = _halves(shapes[i][0], c)
                copy(j, i, qj, mine, (x, y, c)).wait_recv()
                cp = copy(3 + j, i, qj, mine, sibling)
                cp.start()
                sends.append(cp)
        for j, chip in enumerate(chips):
            qj = 2 * chip[0] + chip[1]
            for i in moved:
                _, other = _halves(shapes[i][0], c)
                copy(3 + j, i, qj, other, (x, y, c)).wait_recv()
        for cp in sends:
            cp.wait_send()
        for cp in keeps:
            cp.wait()

    return pl.pallas_call(
        body, name="allgather_weights",
        out_shape=[jax.ShapeDtypeStruct((N_CHIPS, *s), BF16) for s in shapes],
        in_specs=_any_specs(n), out_specs=_any_specs(n),
        scratch_shapes=([pltpu.VMEM(s, F32) for s in shapes] + [pltpu.VMEM(s, BF16) for s in shapes]
                        + [pltpu.SemaphoreType.DMA((6 * n,)), pltpu.SemaphoreType.DMA((6 * n,)),
                           pltpu.SemaphoreType.DMA((2 * n,))]),
        compiler_params=pltpu.CompilerParams(vmem_limit_bytes=VMEM_LIMIT),
    )(*quarters)


def exchange_with_sibling(grads):
    n = len(grads)

    def body(*refs):
        g_refs, theirs_refs = refs[:n], refs[n:2 * n]
        send_sems, recv_sems = refs[2 * n:]
        x, y, c = _position()
        copies = []
        for i in range(n):
            _, other = _halves(g_refs[i].shape[1], c)
            cp = pltpu.make_async_remote_copy(
                src_ref=g_refs[i].at[:, other], dst_ref=theirs_refs[i],
                send_sem=send_sems.at[i], recv_sem=recv_sems.at[i],
                device_id=(x, y, 1 - c), device_id_type=MESH)
            cp.start()
            copies.append(cp)
        for cp in copies:
            cp.wait()

    return pl.pallas_call(
        body, name="exchange_with_sibling",
        out_shape=[jax.ShapeDtypeStruct((N_CHIPS, g.shape[1] // 2, g.shape[2]), F32) for g in grads],
        in_specs=_any_specs(n), out_specs=_any_specs(n),
        scratch_shapes=[pltpu.SemaphoreType.DMA((n,)), pltpu.SemaphoreType.DMA((n,))],
    )(*grads)


def scatter_to_owners(chip_sums):
    n = len(chip_sums)

    def body(*refs):
        b_refs, got_refs = refs[:n], refs[n:2 * n]
        send_sems, recv_sems = refs[2 * n:]
        x, y, c = _position()
        copies = []
        for j, chip in enumerate(_other_chips(x, y)):
            qj = 2 * chip[0] + chip[1]
            for i in range(n):
                cp = pltpu.make_async_remote_copy(
                    src_ref=b_refs[i].at[qj], dst_ref=got_refs[i].at[j],
                    send_sem=send_sems.at[j * n + i], recv_sem=recv_sems.at[j * n + i],
                    device_id=(*chip, c), device_id_type=MESH)
                cp.start()
                copies.append(cp)
        for cp in copies:
            cp.wait()

    return pl.pallas_call(
        body, name="scatter_to_owners",
        out_shape=[jax.ShapeDtypeStruct((N_CHIPS - 1, *b.shape[1:]), BF16) for b in chip_sums],
        in_specs=_any_specs(n), out_specs=_any_specs(n),
        scratch_shapes=[pltpu.SemaphoreType.DMA((3 * n,)), pltpu.SemaphoreType.DMA((3 * n,))],
    )(*chip_sums)


def join_halves(reduced):
    n = len(reduced)

    def body(*refs):
        buf_refs = refs[n:2 * n]
        send_sems, recv_sems = refs[2 * n:]
        x, y, c = _position()
        copies = []
        for i in range(n):
            mine, _ = _halves(buf_refs[i].shape[0], c)
            cp = pltpu.make_async_remote_copy(
                src_ref=buf_refs[i].at[mine], dst_ref=buf_refs[i].at[mine],
                send_sem=send_sems.at[i], recv_sem=recv_sems.at[i],
                device_id=(x, y, 1 - c), device_id_type=MESH)
            cp.start()
            copies.append(cp)
        for cp in copies:
            cp.wait()

    return pl.pallas_call(
        body, name="join_halves",
        out_shape=[jax.ShapeDtypeStruct(r.shape, F32) for r in reduced],
        in_specs=_any_specs(n), out_specs=_any_specs(n),
        input_output_aliases={i: i for i in range(n)},
        scratch_shapes=[pltpu.SemaphoreType.DMA((n,)), pltpu.SemaphoreType.DMA((n,))],
    )(*reduced)


ADD_ROWS = 256


def add_halves(grad, theirs, place, name):
    _, half, cols = theirs.shape
    rb = min(ADD_ROWS, half)
    steps = half // rb

    def body(place_ref, a_ref, b_ref, f_ref, h_ref):
        s = a_ref[...] + b_ref[...]
        f_ref[...] = s
        h_ref[...] = _bf(s)

    spec = pl.BlockSpec((1, rb, cols), lambda i, j, place: (i, j, 0))
    return pl.pallas_call(
        body, name=name,
        grid_spec=pltpu.PrefetchScalarGridSpec(
            num_scalar_prefetch=1, grid=(N_CHIPS, steps),
            in_specs=[pl.BlockSpec((1, rb, cols), lambda i, j, place: (i, place[0] * steps + j, 0)), spec],
            out_specs=(spec, spec)),
        out_shape=(jax.ShapeDtypeStruct(theirs.shape, F32), jax.ShapeDtypeStruct(theirs.shape, BF16)),
        compiler_params=_params("parallel", "parallel"),
    )(place, grad, theirs)


def add_parts(chip_sum, got, place, name):
    _, half, cols = got.shape
    rb = min(ADD_ROWS, half)
    steps = half // rb

    def body(place_ref, o_ref, g_ref, out_ref):
        s = o_ref[0]
        for j in range(N_CHIPS - 1):
            s = s + g_ref[j].astype(F32)
        out_ref[...] = s

    return pl.pallas_call(
        body, name=name,
        grid_spec=pltpu.PrefetchScalarGridSpec(
            num_scalar_prefetch=1, grid=(steps,),
            in_specs=[pl.BlockSpec((1, rb, cols), lambda j, place: (place[1], j, 0)),
                      pl.BlockSpec((N_CHIPS - 1, rb, cols), lambda j, place: (0, j, 0))],
            out_specs=pl.BlockSpec((rb, cols), lambda j, place: (place[0] * steps + j, 0))),
        out_shape=jax.ShapeDtypeStruct((2 * half, cols), F32),
        compiler_params=_params("parallel"),
    )(place, chip_sum, got)


def _adam_math(w, g, m, v):
    m = ADAM_B1 * m + (1.0 - ADAM_B1) * g
    v = ADAM_B2 * v + (1.0 - ADAM_B2) * (g * g)
    m_hat = m / (1.0 - ADAM_B1 ** ADAM_STEP)
    v_hat = v / (1.0 - ADAM_B2 ** ADAM_STEP)
    delta = -ADAM_LR * (m_hat / (jnp.sqrt(v_hat) + ADAM_EPS) + ADAM_WD * w)
    return delta, m, v


def adamw(w, g, m, v, name):
    rows, cols = w.shape
    fits = [t for t in range(8, rows, 8) if rows % t == 0 and t * cols * 4 <= 2 ** 20]
    tile = max(fits) if fits else rows

    def body(w_ref, g_ref, m_ref, v_ref, d_ref, nm_ref, nv_ref):
        d, nm, nv = _adam_math(w_ref[...], g_ref[...], m_ref[...], v_ref[...])
        d_ref[...] = d
        nm_ref[...] = nm
        nv_ref[...] = nv

    spec = pl.BlockSpec((tile, cols), lambda i: (i, 0))
    shape = jax.ShapeDtypeStruct((rows, cols), F32)
    return pl.pallas_call(
        body, name=name, grid=(rows // tile,),
        out_shape=(shape, shape, shape),
        in_specs=[spec] * 4, out_specs=(spec, spec, spec),
        compiler_params=_params("parallel"),
    )(w, g, m, v)


def adamw_rows(w, g, m, v, name):
    rows, _, cols = w.shape
    tile = rows // 4

    def body(w_ref, g_ref, m_ref, v_ref, d_ref, nm_ref, nv_ref):
        d, nm, nv = _adam_math(w_ref[...], g_ref[...], m_ref[...], v_ref[...])
        d_ref[...] = d
        nm_ref[...] = nm
        nv_ref[...] = nv

    spec = pl.BlockSpec((tile, 1, cols), lambda i: (i, 0, 0))
    shape = jax.ShapeDtypeStruct(w.shape, F32)
    return pl.pallas_call(
        body, name=name, grid=(rows // tile,),
        out_shape=(shape, shape, shape),
        in_specs=[spec] * 4, out_specs=(spec, spec, spec),
        compiler_params=_params("parallel"),
    )(w, g, m, v)


def matmul_tn(a, b, name, tile_n=512, tile_s=2048, by_column_tile=False):
    s, m = a.shape
    n = b.shape[1]
    tile_n = min(tile_n, n)
    tile_s = min(tile_s, s)
    steps = s // tile_s
    if by_column_tile:
        out_shape = jax.ShapeDtypeStruct((n // tile_n, m, tile_n), F32)
        out_spec = pl.BlockSpec((None, m, tile_n), lambda j, k: (j, 0, 0))
    else:
        out_shape = jax.ShapeDtypeStruct((m, n), F32)
        out_spec = pl.BlockSpec((m, tile_n), lambda j, k: (0, j))

    def body(a_ref, b_ref, out_ref):
        k = pl.program_id(1)

        @pl.when(k == 0)
        def _():
            out_ref[...] = jnp.zeros_like(out_ref)

        out_ref[...] += _tn(a_ref[...], b_ref[...])

    return pl.pallas_call(
        body, name=name, grid=(n // tile_n, steps),
        out_shape=out_shape,
        in_specs=[pl.BlockSpec((tile_s, m), lambda j, k: (k, 0)),
                  pl.BlockSpec((tile_s, tile_n), lambda j, k: (k, j))],
        out_specs=out_spec,
        compiler_params=_params("parallel", "arbitrary"),
    )(a, b)


ROW_TILE = 512


def _row_index(tile, rows):
    return tile * rows + lax.broadcasted_iota(jnp.int32, (rows, 1), 0)


def _inverse_counts(t_glob):
    return [1.0 / jnp.minimum(t_glob + 1, w).astype(F32) for w in POOL_WINDOWS]


def _sigmoid(z):
    return 1.0 / (1.0 + jnp.exp(-z))


def gather_in_background(step, last, out_refs, send_sems, recv_sems):
    n = len(out_refs)
    x, y, c = _position()
    q = 2 * x + y
    chips = _other_chips(x, y)

    def copy(k, i, quarter, half, to):
        return _gather_copy(out_refs[i], send_sems, recv_sems, k * n + i, quarter, half, to)

    @pl.when(step == 0)
    def _():
        for i in range(n):
            mine, _ = _halves(out_refs[i].shape[1], c)
            for j, chip in enumerate(chips):
                copy(j, i, q, mine, (*chip, c)).start()

    @pl.when(step == last - 1)
    def _():
        for j, chip in enumerate(chips):
            qj = 2 * chip[0] + chip[1]
            for i in range(n):
                mine, _ = _halves(out_refs[i].shape[1], c)
                copy(j, i, qj, mine, (x, y, c)).wait_recv()
                copy(3 + j, i, qj, mine, (x, y, 1 - c)).start()

    @pl.when(step == last)
    def _():
        for j, chip in enumerate(chips):
            qj = 2 * chip[0] + chip[1]
            for i in range(n):
                mine, other = _halves(out_refs[i].shape[1], c)
                copy(3 + j, i, qj, other, (x, y, c)).wait_recv()
                copy(j, i, q, mine, (x, y, c)).wait_send()
                copy(3 + j, i, qj, mine, (x, y, c)).wait_send()


def pool_forward(x, w0, wpi, gw, gb, scale, wpo, later):
    s = x.shape[0]
    ts = ROW_TILE
    nt = s // ts
    assert nt >= 2
    n_later = len(later)

    def body(x_ref, w0_ref, wpi_ref, gw_ref, gb_ref, sc_ref, wpo_ref, *rest):
        rest = rest[n_later:]
        h1_ref, pooled_ref, gt_ref, n0_ref = rest[:4]
        later_refs = rest[4:4 + n_later]
        ubuf, send_sems, recv_sems = rest[4 + n_later:]
        i = pl.program_id(0)
        gather_in_background(i, nt - 1, later_refs, send_sems, recv_sems)
        xv = x_ref[...]
        r = lax.rsqrt(jnp.mean(xv * xv, axis=-1, keepdims=True) + EPS)
        n0 = _bf(xv * r * w0_ref[...])
        n0_ref[...] = n0
        u = jnp.concatenate([_nn(n0, wpi_ref[0]), _nn(n0, wpi_ref[1])], axis=-1)
        gt = jnp.concatenate([_nn(n0, wpi_ref[2]), _nn(n0, wpi_ref[3])], axis=-1)
        gt_ref[...] = gt

        @pl.when(i == 0)
        def _():
            ubuf[0:HALO, :] = jnp.zeros((HALO, D), F32)

        ubuf[HALO:HALO + ts, :] = u
        inv = _inverse_counts(_row_index(i, ts))
        mixed = []
        for g, w in enumerate(POOL_WINDOWS):
            cols = slice(g * GROUP_DIM, (g + 1) * GROUP_DIM)
            ug = u[:, cols]
            acc = ug
            for j in range(1, w):
                acc = acc + ubuf[HALO - j:HALO - j + ts, cols]
            pooled = acc * inv[g] - ug
            pooled_ref[:, cols] = pooled
            mixed.append(_nn(_bf(pooled), gw_ref[g]))
        ubuf[0:HALO, :] = ubuf[ts:ts + HALO, :]
        mixed = jnp.concatenate(mixed, axis=-1) + gb_ref[...]
        y = mixed * sc_ref[...] * (gt * _sigmoid(gt))
        h1_ref[...] = xv + _nn(_bf(y), wpo_ref[...])

    row = lambda cols: pl.BlockSpec((ts, cols), lambda i: (i, 0))
    outs = pl.pallas_call(
        body, name="pool_forward", grid=(nt,),
        out_shape=[jax.ShapeDtypeStruct((s, D), F32), jax.ShapeDtypeStruct((s, D), F32),
                   jax.ShapeDtypeStruct((s, D), F32), jax.ShapeDtypeStruct((s, D), BF16)]
                  + [jax.ShapeDtypeStruct(a.shape, a.dtype) for a in later],
        in_specs=[row(D), _full((1, D)), _full((N_CHIPS, D, D // 2)), _full((GROUPS, GROUP_DIM, GROUP_DIM)),
                  _full((1, D)), _full((1, D)), _full((D, D))] + _any_specs(n_later),
        out_specs=[row(D), row(D), row(D), row(D)] + _any_specs(n_later),
        input_output_aliases={7 + k: 4 + k for k in range(n_later)},
        scratch_shapes=[pltpu.VMEM((HALO + ts, D), F32),
                        pltpu.SemaphoreType.DMA((6 * n_later,)), pltpu.SemaphoreType.DMA((6 * n_later,))],
        compiler_params=_params("arbitrary"),
    )(x, w0, wpi, gw, gb, scale, wpo, *later)
    return outs[:4], outs[4:]


def pool_backward(x, dh1, pooled, gt, w0, wpi, gw, gb, scale, wpo):
    s = x.shape[0]
    ts = ROW_TILE
    nt = s // ts

    def body(x_ref, dh1_ref, pooled_ref, gt_ref, w0_ref, wpi_ref, gw_ref, gb_ref, sc_ref, wpo_ref,
             dx_ref, y_ref, dproj_ref, ggw_ref, small_ref, ebuf):
        i = pl.program_id(0)

        @pl.when(i == 0)
        def _():
            ggw_ref[...] = jnp.zeros_like(ggw_ref)
            small_ref[...] = jnp.zeros_like(small_ref)
            ebuf[ts:ts + HALO, :] = jnp.zeros((HALO, D), F32)

        dh1 = dh1_ref[...]
        gt = gt_ref[...]
        sc = sc_ref[...]
        dy = _nt(_bf(dh1), wpo_ref[...])
        pooled_bf = []
        mixed = []
        for g in range(GROUPS):
            cols = slice(g * GROUP_DIM, (g + 1) * GROUP_DIM)
            pb = _bf(pooled_ref[:, cols])
            pooled_bf.append(pb)
            mixed.append(_nn(pb, gw_ref[g]))
        mixed = jnp.concatenate(mixed, axis=-1) + gb_ref[...]
        sg = _sigmoid(gt)
        silu = gt * sg
        y_ref[...] = _bf(mixed * sc * silu)
        dmixed = dy * sc * silu
        dgt = dy * mixed * sc * (sg * (1.0 + gt * (1.0 - sg)))
        dproj_ref[:, D:] = _bf(dgt)
        small_ref[1:2, :] += jnp.sum(dy * mixed * silu, axis=0, keepdims=True)
        small_ref[2:3, :] += jnp.sum(dmixed, axis=0, keepdims=True)

        inv = _inverse_counts(_row_index(nt - 1 - i, ts))
        dpooled = []
        for g in range(GROUPS):
            cols = slice(g * GROUP_DIM, (g + 1) * GROUP_DIM)
            dm = _bf(dmixed[:, cols])
            ggw_ref[g] += _tn(pooled_bf[g], dm)
            dp = _nt(dm, gw_ref[g])
            dpooled.append(dp)
            ebuf[0:ts, cols] = dp * inv[g]
        du = []
        for g, w in enumerate(POOL_WINDOWS):
            cols = slice(g * GROUP_DIM, (g + 1) * GROUP_DIM)
            acc = -dpooled[g]
            for j in range(w):
                acc = acc + ebuf[j:j + ts, cols]
            du.append(acc)
        ebuf[ts:ts + HALO, :] = ebuf[0:HALO, :]
        du = _bf(jnp.concatenate(du, axis=-1))
        dproj_ref[:, :D] = du
        dgt_bf = _bf(dgt)
        half = D // 2
        dn0 = (_nt(du[:, :half], wpi_ref[0]) + _nt(du[:, half:], wpi_ref[1])
               + _nt(dgt_bf[:, :half], wpi_ref[2]) + _nt(dgt_bf[:, half:], wpi_ref[3]))

        xv = x_ref[...]
        r = lax.rsqrt(jnp.mean(xv * xv, axis=-1, keepdims=True) + EPS)
        xhat = xv * r
        small_ref[0:1, :] += jnp.sum(dn0 * xhat, axis=0, keepdims=True)
        dxh = dn0 * w0_ref[...]
        dx_ref[...] = dh1 + r * (dxh - xhat * jnp.mean(dxh * xhat, axis=-1, keepdims=True))

    row = lambda cols: pl.BlockSpec((ts, cols), lambda i: (nt - 1 - i, 0))
    return pl.pallas_call(
        body, name="pool_backward", grid=(nt,),
        out_shape=(jax.ShapeDtypeStruct((s, D), F32), jax.ShapeDtypeStruct((s, D), BF16),
                   jax.ShapeDtypeStruct((s, 2 * D), BF16),
                   jax.ShapeDtypeStruct((GROUPS, GROUP_DIM, GROUP_DIM), F32),
                   jax.ShapeDtypeStruct((8, D), F32)),
        in_specs=[row(D), row(D), row(D), row(D), _full((1, D)), _full((N_CHIPS, D, D // 2)),
                  _full((GROUPS, GROUP_DIM, GROUP_DIM)), _full((1, D)), _full((1, D)), _full((D, D))],
        out_specs=(row(D), row(D), row(2 * D), _full((GROUPS, GROUP_DIM, GROUP_DIM)), _full((8, D))),
        scratch_shapes=[pltpu.VMEM((ts + HALO, D), F32)],
        compiler_params=_params("arbitrary"),
    )(x, dh1, pooled, gt, w0, wpi, gw, gb, scale, wpo)


def gla_project(h1, w1, wgi, wlow, wgk, bgk):
    s = h1.shape[0]
    ts = ROW_TILE

    def body(h_ref, w1_ref, wgi_ref, wlow_ref, wgk_ref, bgk_ref, proj_ref, low_ref, cum_ref, n1_ref):
        hv = h_ref[...]
        r = lax.rsqrt(jnp.mean(hv * hv, axis=-1, keepdims=True) + EPS)
        n1 = _bf(hv * r * w1_ref[...])
        n1_ref[...] = n1
        proj_ref[...] = _nn(n1, wgi_ref[...])
        low = _nn(n1, wlow_ref[...])
        low_ref[...] = low
        z = _nn(_bf(low), wgk_ref[...]) + bgk_ref[...]
        lg = (jnp.minimum(z, 0.0) - jnp.log(1.0 + jnp.exp(-jnp.abs(z)))) / GATE_NORM
        lower_f = _chunk_masks()[0].astype(F32)
        for r0 in range(0, ts, CHUNK):
            cum_ref[r0:r0 + CHUNK, :] = _nn_exact(lower_f, lg[r0:r0 + CHUNK, :])

    row = lambda cols: pl.BlockSpec((ts, cols), lambda i: (i, 0))
    return pl.pallas_call(
        body, name="gla_project", grid=(s // ts,),
        out_shape=(jax.ShapeDtypeStruct((s, GLA_MAIN), F32), jax.ShapeDtypeStruct((s, RANK_PAD), F32),
                   jax.ShapeDtypeStruct((s, KEY_W), F32), jax.ShapeDtypeStruct((s, D), BF16)),
        in_specs=[row(D), _full((1, D)), _full((D, GLA_MAIN)), _full((D, RANK_PAD)),
                  _full((RANK_PAD, KEY_W)), _full((1, KEY_W))],
        out_specs=(row(GLA_MAIN), row(RANK_PAD), row(KEY_W), row(D)),
        compiler_params=_params("parallel"),
    )(h1, w1, wgi, wlow, wgk, bgk)


GLA_BLOCK = 512
CHUNKS_PER_BLOCK = GLA_BLOCK // CHUNK


def _chunk_masks():
    t = lax.broadcasted_iota(jnp.int32, (CHUNK, CHUNK), 0)
    u = lax.broadcasted_iota(jnp.int32, (CHUNK, CHUNK), 1)
    return t >= u, t <= u


def _gla_chunk_terms(q, cum):
    ep = jnp.exp(cum)
    en = jnp.exp(-cum)
    qs = q * (HEAD_K ** -0.5)
    last = cum[CHUNK - 1:CHUNK, :]
    ed = jnp.exp(last - cum)
    dec = jnp.exp(last)
    return ep, en, qs, ed, dec


def gla_forward(proj, cum):
    s = proj.shape[0]
    nb = s // GLA_BLOCK
    nc = s // CHUNK

    def body(q_ref, k_ref, v_ref, cum_ref, o_ref, st_ref, state):
        @pl.when(pl.program_id(0) == 0)
        def _():
            state[...] = jnp.zeros_like(state)

        lower, _ = _chunk_masks()

        def chunk(cc, carry):
            rows = pl.ds(pl.multiple_of(cc * CHUNK, CHUNK), CHUNK)
            for h in range(HEADS):
                kc = slice(h * HEAD_K, (h + 1) * HEAD_K)
                vc = slice(h * HEAD_V, (h + 1) * HEAD_V)
                q = q_ref[rows, kc]
                k = k_ref[rows, kc]
                v = _bf(v_ref[rows, vc])
                ep, en, qs, ed, dec = _gla_chunk_terms(q, cum_ref[rows, kc])
                a = _bf(qs * ep)
                fwd = _nt(a, _bf(k * en))
                bwd = _nt(_bf(qs * en), _bf(k * ep))
                scores = jnp.where(lower, fwd, bwd)
                st = state[h]
                st_ref[cc, h] = st
                o_ref[rows, vc] = _nn(_bf(scores), v) + _nt(a, _bf(st))
                state[h] = st * dec + _tn(v, _bf(k * ed))
            return carry

        lax.fori_loop(0, CHUNKS_PER_BLOCK, chunk, 0, unroll=4)

    return pl.pallas_call(
        body, name="gla_forward", grid=(nb,),
        out_shape=(jax.ShapeDtypeStruct((s, D), F32),
                   jax.ShapeDtypeStruct((nc, HEADS, HEAD_V, HEAD_K), F32)),
        in_specs=[pl.BlockSpec((GLA_BLOCK, KEY_W), lambda i: (i, 0)),
                  pl.BlockSpec((GLA_BLOCK, KEY_W), lambda i: (i, 1)),
                  pl.BlockSpec((GLA_BLOCK, D), lambda i: (i, 1)),
                  pl.BlockSpec((GLA_BLOCK, KEY_W), lambda i: (i, 0))],
        out_specs=(pl.BlockSpec((GLA_BLOCK, D), lambda i: (i, 0)),
                   pl.BlockSpec((CHUNKS_PER_BLOCK, HEADS, HEAD_V, HEAD_K), lambda i: (i, 0, 0, 0))),
        scratch_shapes=[pltpu.VMEM((HEADS, HEAD_V, HEAD_K), F32)],
        compiler_params=_params("arbitrary"),
    )(proj, proj, proj, cum)


def gla_backward(proj, cum, do, states):
    s = proj.shape[0]
    nb = s // GLA_BLOCK

    def body(q_ref, k_ref, v_ref, cum_ref, do_ref, st_ref, dq_ref, dk_ref, dv_ref, dcum_ref, dstate):
        @pl.when(pl.program_id(0) == 0)
        def _():
            dstate[...] = jnp.zeros_like(dstate)

        lower, _ = _chunk_masks()
        is_last = lax.broadcasted_iota(jnp.int32, (CHUNK, HEAD_K), 0) == CHUNK - 1

        def chunk(step, carry):
            cc = CHUNKS_PER_BLOCK - 1 - step
            rows = pl.ds(pl.multiple_of(cc * CHUNK, CHUNK), CHUNK)
            for h in range(HEADS):
                kc = slice(h * HEAD_K, (h + 1) * HEAD_K)
                vc = slice(h * HEAD_V, (h + 1) * HEAD_V)
                q = q_ref[rows, kc]
                k = k_ref[rows, kc]
                v = _bf(v_ref[rows, vc])
                do_c = _bf(do_ref[rows, vc])
                ep, en, qs, ed, dec = _gla_chunk_terms(q, cum_ref[rows, kc])
                a = _bf(qs * ep)
                b = _bf(k * en)
                c = _bf(qs * en)
                dk_dec = _bf(k * ep)
                kd = _bf(k * ed)
                scores = _bf(jnp.where(lower, _nt(a, b), _nt(c, dk_dec)))
                st = st_ref[cc, h]
                dst = dstate[h]
                dst_bf = _bf(dst)

                dscores = _nt(do_c, v)
                dfwd = _bf(jnp.where(lower, dscores, 0.0))
                dbwd = _bf(jnp.where(lower, 0.0, dscores))
                dv_ref[rows, vc] = _tn(scores, do_c) + _nt(kd, dst_bf)
                da = _nn(dfwd, b) + _nn(do_c, _bf(st))
                db = _tn(dfwd, a)
                dc = _nn(dbwd, dk_dec)
                ddk = _tn(dbwd, c)
                dkd = _nn(v, dst_bf)
                ddec = jnp.sum(dst * st, axis=0, keepdims=True)
                dstate[h] = dst * dec + _tn(do_c, a)

                m = dkd * k * ed
                dq_ref[rows, kc] = (da * ep + dc * en) * (HEAD_K ** -0.5)
                dk_ref[rows, kc] = db * en + ddk * ep + dkd * ed
                dcum = (da * qs + ddk * k) * ep - (db * k + dc * qs) * en - m
                dlast = jnp.sum(m, axis=0, keepdims=True) + ddec * dec
                dcum_ref[rows, kc] = dcum + jnp.where(is_last, dlast, 0.0)
            return carry

        lax.fori_loop(0, CHUNKS_PER_BLOCK, chunk, 0, unroll=4)

    rev = lambda cols, col_block: pl.BlockSpec((GLA_BLOCK, cols), lambda i: (nb - 1 - i, col_block))
    return pl.pallas_call(
        body, name="gla_backward", grid=(nb,),
        out_shape=(jax.ShapeDtypeStruct((s, KEY_W), F32), jax.ShapeDtypeStruct((s, KEY_W), F32),
                   jax.ShapeDtypeStruct((s, D), F32), jax.ShapeDtypeStruct((s, KEY_W), F32)),
        in_specs=[rev(KEY_W, 0), rev(KEY_W, 1), rev(D, 1), rev(KEY_W, 0), rev(D, 0),
                  pl.BlockSpec((CHUNKS_PER_BLOCK, HEADS, HEAD_V, HEAD_K), lambda i: (nb - 1 - i, 0, 0, 0))],
        out_specs=(rev(KEY_W, 0), rev(KEY_W, 0), rev(D, 0), rev(KEY_W, 0)),
        scratch_shapes=[pltpu.VMEM((HEADS, HEAD_V, HEAD_K), F32)],
        compiler_params=_params("arbitrary"),
    )(proj, proj, proj, cum, do, states)


def head_and_loss(o, proj, h1, target, hw, wgo, wf):
    s = o.shape[0]
    ts = ROW_TILE

    def body(o_ref, gate_ref, h1_ref, tgt_ref, hw_ref, wgo_ref, wf_ref,
             dh2_ref, dh2bf_ref, do_ref, dgate_ref, y2_ref, small_ref):
        @pl.when(pl.program_id(0) == 0)
        def _():
            small_ref[...] = jnp.zeros_like(small_ref)

        gate = gate_ref[...]
        hw = hw_ref[...]
        sg = _sigmoid(gate)
        silu = gate * sg
        ohat, ro = [], []
        for h in range(HEADS):
            oh = o_ref[:, h * HEAD_V:(h + 1) * HEAD_V]
            rh = lax.rsqrt(jnp.mean(oh * oh, axis=-1, keepdims=True) + EPS)
            ro.append(rh)
            ohat.append(oh * rh)
        ohat = jnp.concatenate(ohat, axis=-1)
        on = ohat * hw
        y2 = _bf(on * silu)
        y2_ref[...] = y2
        h2 = h1_ref[...] + _nn(y2, wgo_ref[...])
        rf = lax.rsqrt(jnp.mean(h2 * h2, axis=-1, keepdims=True) + EPS)
        h2hat = h2 * rf
        wf = wf_ref[...]
        diff = h2hat * wf - tgt_ref[...]
        small_ref[2:3, :] += jnp.zeros((1, D), F32) + 0.5 * jnp.sum(diff * diff) / D
        dout = diff / D
        small_ref[0:1, :] += jnp.sum(dout * h2hat, axis=0, keepdims=True)
        dxh = dout * wf
        dh2 = rf * (dxh - h2hat * jnp.mean(dxh * h2hat, axis=-1, keepdims=True))
        dh2_ref[...] = dh2
        dh2_bf = _bf(dh2)
        dh2bf_ref[...] = dh2_bf
        dy2 = _nt(dh2_bf, wgo_ref[...])
        don = dy2 * silu
        dgate_ref[...] = dy2 * on * (sg * (1.0 + gate * (1.0 - sg)))
        ghw = jnp.sum(don * ohat, axis=0, keepdims=True)
        small_ref[1:2, 0:HEAD_V] += sum(ghw[:, h * HEAD_V:(h + 1) * HEAD_V] for h in range(HEADS))
        dohat = don * hw
        for h in range(HEADS):
            cols = slice(h * HEAD_V, (h + 1) * HEAD_V)
            oh, dh = ohat[:, cols], dohat[:, cols]
            do_ref[:, cols] = ro[h] * (dh - oh * jnp.mean(dh * oh, axis=-1, keepdims=True))

    row = lambda cols: pl.BlockSpec((ts, cols), lambda i: (i, 0))
    act = jax.ShapeDtypeStruct((s, D), F32)
    act_bf = jax.ShapeDtypeStruct((s, D), BF16)
    return pl.pallas_call(
        body, name="head_and_loss", grid=(s // ts,),
        out_shape=(act, act_bf, act, act, act_bf, jax.ShapeDtypeStruct((8, D), F32)),
        in_specs=[row(D), pl.BlockSpec((ts, D), lambda i: (i, 2)), row(D), row(D),
                  _full((1, D)), _full((D, D)), _full((1, D))],
        out_specs=(row(D), row(D), row(D), row(D), row(D), _full((8, D))),
        compiler_params=_params("arbitrary"),
    )(o, proj, h1, target, hw, wgo, wf)


def gla_project_backward(dq, dk, dv, dgate, dcum, low, h1, dh2, w1, wgi, wlow, wgk, bgk):
    s = h1.shape[0]
    ts = ROW_TILE

    def body(dq_ref, dk_ref, dv_ref, dgate_ref, dcum_ref, low_ref, h1_ref, dh2_ref, w1_ref,
             wgi_ref, wlow_ref, wgk_ref, bgk_ref, dh1_ref, dh1bf_ref, dproj_ref, dlow_ref, ggk_ref,
             small_ref):
        @pl.when(pl.program_id(0) == 0)
        def _():
            ggk_ref[...] = jnp.zeros_like(ggk_ref)
            small_ref[...] = jnp.zeros_like(small_ref)

        low = _bf(low_ref[...])
        z = _nn(low, wgk_ref[...]) + bgk_ref[...]
        upper_f = _chunk_masks()[1].astype(F32)
        dlg = jnp.concatenate([_nn_exact(upper_f, dcum_ref[r0:r0 + CHUNK, :]) for r0 in range(0, ts, CHUNK)],
                              axis=0)
        dz = dlg * (1.0 / GATE_NORM) * _sigmoid(-z)
        dz_bf = _bf(dz)
        ggk_ref[...] += _tn(low, dz_bf)
        small_ref[1:2, 0:KEY_W] += jnp.sum(dz, axis=0, keepdims=True)
        dlow = _bf(_nt(dz_bf, wgk_ref[...]))
        dlow_ref[...] = dlow
        dn1 = _nt(dlow, wlow_ref[...])
        for ref, lo, hi in ((dq_ref, 0, KEY_W), (dk_ref, KEY_W, 2 * KEY_W),
                            (dv_ref, 2 * KEY_W, 2 * KEY_W + D), (dgate_ref, 2 * KEY_W + D, GLA_MAIN)):
            piece = _bf(ref[...])
            dproj_ref[:, lo:hi] = piece
            dn1 = dn1 + _nt(piece, wgi_ref[:, lo:hi])
        hv = h1_ref[...]
        r = lax.rsqrt(jnp.mean(hv * hv, axis=-1, keepdims=True) + EPS)
        hhat = hv * r
        small_ref[0:1, :] += jnp.sum(dn1 * hhat, axis=0, keepdims=True)
        dxh = dn1 * w1_ref[...]
        dh1 = dh2_ref[...] + r * (dxh - hhat * jnp.mean(dxh * hhat, axis=-1, keepdims=True))
        dh1_ref[...] = dh1
        dh1bf_ref[...] = _bf(dh1)

    row = lambda cols: pl.BlockSpec((ts, cols), lambda i: (i, 0))
    return pl.pallas_call(
        body, name="gla_project_backward", grid=(s // ts,),
        out_shape=(jax.ShapeDtypeStruct((s, D), F32), jax.ShapeDtypeStruct((s, D), BF16),
                   jax.ShapeDtypeStruct((s, GLA_MAIN), BF16),
                   jax.ShapeDtypeStruct((s, RANK_PAD), BF16), jax.ShapeDtypeStruct((RANK_PAD, KEY_W), F32),
                   jax.ShapeDtypeStruct((8, D), F32)),
        in_specs=[row(KEY_W), row(KEY_W), row(D), row(D), row(KEY_W), row(RANK_PAD), row(D), row(D),
                  _full((1, D)), _full((D, GLA_MAIN)), _full((D, RANK_PAD)), _full((RANK_PAD, KEY_W)),
                  _full((1, KEY_W))],
        out_specs=(row(D), row(D), row(GLA_MAIN), row(RANK_PAD), _full((RANK_PAD, KEY_W)), _full((8, D))),
        compiler_params=_params("arbitrary"),
    )(dq, dk, dv, dgate, dcum, low, h1, dh2, w1, wgi, wlow, wgk, bgk)


def _groups_from_quarters(a):
    return a.reshape(N_CHIPS, GROUPS, 64, GROUP_DIM).transpose(1, 0, 2, 3).reshape(GROUPS, GROUP_DIM, GROUP_DIM)


def _quarters_from_groups(a):
    return a.reshape(GROUPS, N_CHIPS, 64, GROUP_DIM).transpose(1, 0, 2, 3).reshape(N_CHIPS, GROUP_DIM, GROUP_DIM)


def _pad_row(*pieces):
    flat = jnp.concatenate([p.reshape(-1).astype(F32) for p in pieces])
    return jnp.pad(flat, (0, D - flat.shape[0])).reshape(1, D)


def _gla_weights(wgi_q, wgo_q):
    wgi_all = jnp.concatenate([wgi_q[q] for q in range(N_CHIPS)], axis=1)
    wlow = jnp.pad(wgi_all[:, GLA_MAIN:], ((0, 0), (0, RANK_PAD - GATE_RANK)))
    return wgi_all[:, :GLA_MAIN], wlow, wgo_q.reshape(D, D)


def local_gradients(xs, target, w0, w1, wf, wpi, gw, gb, scale, wpo, gla_quarters, wgk, bgk, hw_tiled):
    (h1, pooled, gt, n0), gla_all = pool_forward(xs, w0, wpi, gw, gb, scale, wpo, gla_quarters)
    wgi, wlow, wgo = _gla_weights(*gla_all)
    proj, low, cum, n1 = gla_project(h1, w1, wgi, wlow, wgk, bgk)
    o, states = gla_forward(proj, cum)

    dh2, dh2_bf, do, dgate, y2, small_top = head_and_loss(o, proj, h1, target, hw_tiled, wgo, wf)
    g_gla_out = matmul_tn(y2, dh2_bf, "grad_gla_out")
    dq, dk, dv, dcum = gla_backward(proj, cum, do, states)
    dh1, dh1_bf, dproj, dlow, g_gk_pad, small_gla = gla_project_backward(
        dq, dk, dv, dgate, dcum, low, h1, dh2, w1, wgi, wlow, wgk, bgk)
    g_gla_in = jnp.concatenate([matmul_tn(n1, dproj, "grad_gla_in"),
                                matmul_tn(n1, dlow, "grad_gla_low")[:, :GATE_RANK]], axis=1)
    dx, y, dpool, g_group_w, small_pool = pool_backward(
        xs, dh1, pooled, gt, w0, wpi, gw, gb, scale, wpo)
    g_pool_out = matmul_tn(y, dh1_bf, "grad_pool_out")
    g_pool_in = matmul_tn(n0, dpool, "grad_pool_in", by_column_tile=True)
    return (dx, g_pool_in, g_group_w, g_pool_out, g_gla_in, g_gla_out, g_gk_pad,
            small_top, small_gla, small_pool)


def kernel(x, norm_w, pool_in_w, pool_group_w, pool_group_b, pool_scale, pool_out_w, gla_in_w, gla_gk_w, gla_gk_b, gla_head_norm_w, gla_out_w, final_norm_w, loss_target, m_norm_w, m_pool_in_w, m_pool_group_w, m_pool_group_b, m_pool_scale, m_pool_out_w, m_gla_in_w, m_gla_gk_w, m_gla_gk_b, m_gla_head_norm_w, m_gla_out_w, m_final_norm_w, v_norm_w, v_pool_in_w, v_pool_group_w, v_pool_group_b, v_pool_scale, v_pool_out_w, v_gla_in_w, v_gla_gk_w, v_gla_gk_b, v_gla_head_norm_w, v_gla_out_w, v_final_norm_w):
    s = x.shape[1]
    xs = x[0]
    target = loss_target[0]
    q_chip = 2 * lax.axis_index("x") + lax.axis_index("y")
    place = jnp.stack([lax.axis_index("c"), q_chip]).astype(jnp.int32)

    wpi, gw_q, wpo_q, wgi_q, wgo_q = allgather_weights(
        [pool_in_w[0], pool_group_w[0].reshape(GROUP_DIM, GROUP_DIM), pool_out_w[0], gla_in_w[0], gla_out_w[0]],
        exchange=(True, True, True, False, False))
    gw = _groups_from_quarters(gw_q)
    wpo = wpo_q.reshape(D, D)

    small_in = jnp.concatenate([
        _pad_row(gla_gk_b[0], gla_head_norm_w[0], pool_group_b[0]),
        gla_gk_w[0].reshape(2, D),
        jnp.zeros((5, D), F32)], axis=0)
    small_all = gather_small(small_in, "gather_small_weights")[0::2]
    bgk = small_all[:, 0, 0:128].reshape(1, KEY_W)
    hw = small_all[:, 0, 128:192].reshape(1, HEAD_V)
    gb = jnp.concatenate([small_all[q, 0, 192:448].reshape(GROUPS, 64) for q in range(N_CHIPS)],
                         axis=1).reshape(1, D)
    wgk16 = jnp.concatenate([small_all[q, 1:3].reshape(GATE_RANK, 128) for q in range(N_CHIPS)], axis=1)
    wgk = _bf(jnp.pad(wgk16, ((0, RANK_PAD - GATE_RANK), (0, 0))))
    hw_tiled = jnp.tile(hw, (1, HEADS))

    w0 = norm_w[0:1]
    w1 = norm_w[1:2]
    wf = final_norm_w.reshape(1, D)

    (dx, g_pool_in, g_group_w, g_pool_out, g_gla_in, g_gla_out, g_gk_pad,
     small_top, small_gla, small_pool) = local_gradients(
        xs, target, w0, w1, wf, wpi, gw, gb, pool_scale, wpo, [wgi_q, wgo_q], wgk, bgk, hw_tiled)

    names = ("pool_in", "group", "pool_out", "gla_in", "gla_out")
    grads = [g_pool_in, _quarters_from_groups(g_group_w), g_pool_out.reshape(N_CHIPS, D // N_CHIPS, D),
             jnp.stack([g_gla_in[:, GLA_IN_QUARTER * q:GLA_IN_QUARTER * (q + 1)] for q in range(N_CHIPS)]),
             g_gla_out.reshape(N_CHIPS, D // N_CHIPS, D)]
    theirs = exchange_with_sibling(grads)
    sums = [add_halves(g, t, place, "add_halves_" + n) for g, t, n in zip(grads, theirs, names)]
    got = scatter_to_owners([b for _, b in sums])
    reduced = join_halves([add_parts(f, g, place, "add_parts_" + n) for (f, _), g, n in zip(sums, got, names)])
    r_pool_in, r_group_w, r_pool_out, r_gla_in, r_gla_out = reduced
    r_group_w = r_group_w.reshape(GROUPS, 64, GROUP_DIM)

    small_out = jnp.concatenate([
        small_pool[0:1], small_gla[0:1],
        small_pool[1:2],
        small_top[0:1],
        small_top[2:3],
        _pad_row(small_gla[1, 0:KEY_W], small_top[1, 0:HEAD_V]),
        small_pool[2:3],
        g_gk_pad[:GATE_RANK].reshape(8, D),
        jnp.zeros((1, D), F32)], axis=0)
    total = gather_small(small_out, "allreduce_small_grads", reduce=True)
    loss = total[4, 0]
    g_norm = total[0:2]
    g_scale = total[2:3]
    g_final = total[3]
    pick = lambda full, width: lax.dynamic_slice_in_dim(full, q_chip * width, width, axis=-1)
    g_gk_b = pick(total[5:6, 0:KEY_W], 128)
    g_hnw = pick(total[5:6, KEY_W:KEY_W + HEAD_V], 64)
    g_group_b = pick(total[6].reshape(GROUPS, GROUP_DIM), 64)[None]
    g_gk_w = pick(total[7:15].reshape(GATE_RANK, KEY_W), 128)[None]

    def step_lane_rows(name, w, g, m, v):
        turn = lambda a: jnp.transpose(a, (2, 0, 1))
        back = lambda a: jnp.transpose(a, (1, 2, 0))
        g_t = turn(g)
        d, nm, nv = adamw_rows(turn(w), g_t, turn(m), turn(v), "adamw_" + name)
        return back(g_t), back(d), back(nm), back(nv)

    def step(name, w, g, m, v):
        shape = w.shape
        as2d = lambda a: a.reshape(-1, shape[-1])
        d, nm, nv = adamw(as2d(w), as2d(g), as2d(m), as2d(v), "adamw_" + name)
        return g.reshape(shape), d.reshape(shape), nm.reshape(shape), nv.reshape(shape)

    results = [
        step("norm_w", norm_w, g_norm, m_norm_w, v_norm_w),
        step("pool_in_w", pool_in_w, r_pool_in[None], m_pool_in_w, v_pool_in_w),
        step("pool_group_w", pool_group_w, r_group_w[None], m_pool_group_w, v_pool_group_w),
        step("pool_group_b", pool_group_b, g_group_b, m_pool_group_b, v_pool_group_b),
        step("pool_scale", pool_scale, g_scale, m_pool_scale, v_pool_scale),
        step("pool_out_w", pool_out_w, r_pool_out[None], m_pool_out_w, v_pool_out_w),
        step_lane_rows("gla_in_w", gla_in_w, r_gla_in[None], m_gla_in_w, v_gla_in_w),
        step("gla_gk_w", gla_gk_w, g_gk_w, m_gla_gk_w, v_gla_gk_w),
        step("gla_gk_b", gla_gk_b, g_gk_b, m_gla_gk_b, v_gla_gk_b),
        step("gla_head_norm_w", gla_head_norm_w, g_hnw, m_gla_head_norm_w, v_gla_head_norm_w),
        step("gla_out_w", gla_out_w, r_gla_out[None], m_gla_out_w, v_gla_out_w),
        step("final_norm_w", final_norm_w, g_final, m_final_norm_w, v_final_norm_w),
    ]
    grads, deltas, new_m, new_v = zip(*results)
    return (loss, dx[None], *grads, *deltas, *new_m, *new_v)
```

```python
import functools

import jax
import jax.numpy as jnp
from jax import lax
from jax.experimental import pallas as pl
from jax.experimental.pallas import tpu as pltpu

F32 = jnp.float32
BF16 = jnp.bfloat16
MESH = pl.DeviceIdType.MESH

D = 1024
POOL_WINDOWS = (2, 4, 8, 16)
GROUPS = 4
GROUP_DIM = 256
HEADS = 4
HEAD_K = 128
HEAD_V = 256
KEY_W = 512
CHUNK = 64
GATE_RANK = 16
GATE_NORM = 16.0
GLA_IN = 3088
GLA_MAIN = 3072
RANK_PAD = 128
EPS = 1e-6
HALO = 16

ADAM_LR = 0.001
ADAM_B1 = 0.9
ADAM_B2 = 0.999
ADAM_EPS = 1e-08
ADAM_WD = 0.01
ADAM_STEP = 10

N_CHIPS = 4
N_DEV = 8
GLA_IN_QUARTER = GLA_IN // N_CHIPS

VMEM_LIMIT = 56 * 1024 * 1024


def _nn(a, b):
    return lax.dot_general(a, b, (((1,), (0,)), ((), ())), preferred_element_type=F32)


def _nt(a, b):
    return lax.dot_general(a, b, (((1,), (1,)), ((), ())), preferred_element_type=F32)


def _tn(a, b):
    return lax.dot_general(a, b, (((0,), (0,)), ((), ())), preferred_element_type=F32)


def _nn_exact(a, b):
    return lax.dot_general(a, b, (((1,), (0,)), ((), ())), preferred_element_type=F32,
                           precision=lax.Precision.HIGHEST)


def _bf(a):
    return a.astype(BF16)


def _params(*sem):
    return pltpu.CompilerParams(dimension_semantics=sem, vmem_limit_bytes=VMEM_LIMIT)


def _full(shape):
    return pl.BlockSpec(shape, lambda i: (0,) * len(shape))


def _position():
    return lax.axis_index("x"), lax.axis_index("y"), lax.axis_index("c")


def gather_small(block, name, reduce=False):
    rows, cols = block.shape

    def body(in_ref, out_ref, *scratch):
        if reduce:
            all_ref, send_sems, recv_sems, local_sem = scratch
        else:
            all_ref = out_ref
            send_sems, recv_sems, local_sem = scratch
        x, y, c = _position()
        me = 4 * x + 2 * y + c
        mine = pltpu.make_async_copy(in_ref, all_ref.at[me], local_sem)
        mine.start()
        sends = []
        for k in range(N_DEV - 1):
            fx, fy, fc = (k + 1) >> 2 & 1, (k + 1) >> 1 & 1, (k + 1) & 1
            cp = pltpu.make_async_remote_copy(
                src_ref=in_ref, dst_ref=all_ref.at[me],
                send_sem=send_sems.at[k], recv_sem=recv_sems.at[k],
                device_id=(x ^ fx, y ^ fy, c ^ fc), device_id_type=MESH)
            cp.start()
            sends.append(cp)
        for k in range(N_DEV - 1):
            fx, fy, fc = (k + 1) >> 2 & 1, (k + 1) >> 1 & 1, (k + 1) & 1
            src_dev = 4 * (x ^ fx) + 2 * (y ^ fy) + (c ^ fc)
            pltpu.make_async_remote_copy(
                src_ref=in_ref, dst_ref=all_ref.at[src_dev],
                send_sem=send_sems.at[k], recv_sem=recv_sems.at[k],
                device_id=(x, y, c), device_id_type=MESH).wait_recv()
        for cp in sends:
            cp.wait_send()
        mine.wait()
        if reduce:
            total = all_ref[0]
            for dev in range(1, N_DEV):
                total = total + all_ref[dev]
            out_ref[...] = total

    sems = [pltpu.SemaphoreType.DMA((N_DEV - 1,)), pltpu.SemaphoreType.DMA((N_DEV - 1,)),
            pltpu.SemaphoreType.DMA]
    gathered = (N_DEV, rows, cols)
    return pl.pallas_call(
        body, name=name,
        out_shape=jax.ShapeDtypeStruct((rows, cols) if reduce else gathered, block.dtype),
        in_specs=[pl.BlockSpec(memory_space=pltpu.VMEM)],
        out_specs=pl.BlockSpec(memory_space=pltpu.VMEM),
        scratch_shapes=([pltpu.VMEM(gathered, block.dtype)] if reduce else []) + sems,
    )(block)


def _other_chips(x, y):
    return [(1 - x, y), (x, 1 - y), (1 - x, 1 - y)]


def _any_specs(n):
    return [pl.BlockSpec(memory_space=pl.ANY)] * n


def _halves(rows, c):
    half = rows // 2
    return pl.ds(c * half, half), pl.ds((1 - c) * half, half)


CAST_ROWS = 256


def _gather_copy(out_ref, send_sems, recv_sems, k, quarter, half, to, src=None):
    dst = out_ref.at[quarter, half]
    return pltpu.make_async_remote_copy(
        src_ref=dst if src is None else src, dst_ref=dst,
        send_sem=send_sems.at[k], recv_sem=recv_sems.at[k], device_id=to, device_id_type=MESH)


def allgather_weights(quarters, exchange):
    n = len(quarters)
    shapes = [w.shape for w in quarters]
    moved = [i for i in range(n) if exchange[i]]

    def body(*refs):
        w_refs, out_refs = refs[:n], refs[n:2 * n]
        f32_bufs, bf_bufs = refs[2 * n:3 * n], refs[3 * n:4 * n]
        send_sems, recv_sems, local_sems = refs[4 * n:]
        x, y, c = _position()
        q = 2 * x + y
        sibling = (x, y, 1 - c)
        chips = _other_chips(x, y)

        def copy(k, i, quarter, half, to, src=None):
            return _gather_copy(out_refs[i], send_sems, recv_sems, k * n + i, quarter, half, to, src)

        loads = [pltpu.make_async_copy(w_refs[i], f32_bufs[i], local_sems.at[i]) for i in range(n)]
        for cp in loads:
            cp.start()
        keeps, sends = [], []
        for i in range(n):
            loads[i].wait()
            for r0 in range(0, shapes[i][0], CAST_ROWS):
                bf_bufs[i][r0:r0 + CAST_ROWS, :] = _bf(f32_bufs[i][r0:r0 + CAST_ROWS, :])
            keep = pltpu.make_async_copy(bf_bufs[i], out_refs[i].at[q], local_sems.at[n + i])
            keep.start()
            keeps.append(keep)
            if not exchange[i]:
                continue
            mine, _ = _halves(shapes[i][0], c)
            for j, chip in enumerate(chips):
                cp = copy(j, i, q, mine, (*chip, c), src=bf_bufs[i].at[mine])
                cp.start()
                sends.append(cp)
        for j, chip in enumerate(chips):
            qj = 2 * chip[0] + chip[1]
            for i in moved:
                mine, _ = _halves(shapes[i][0], c)
                copy(j, i, qj, mine, (x, y, c)).wait_recv()
                cp = copy(3 + j, i, qj, mine, sibling)
                cp.start()
                sends.append(cp)
        for j, chip in enumerate(chips):
            qj = 2 * chip[0] + chip[1]
            for i in moved:
                _, other = _halves(shapes[i][0], c)
                copy(3 + j, i, qj, other, (x, y, c)).wait_recv()
        for cp in sends:
            cp.wait_send()
        for cp in keeps:
            cp.wait()

    return pl.pallas_call(
        body, name="allgather_weights",
        out_shape=[jax.ShapeDtypeStruct((N_CHIPS, *s), BF16) for s in shapes],
        in_specs=_any_specs(n), out_specs=_any_specs(n),
        scratch_shapes=([pltpu.VMEM(s, F32) for s in shapes] + [pltpu.VMEM(s, BF16) for s in shapes]
                        + [pltpu.SemaphoreType.DMA((6 * n,)), pltpu.SemaphoreType.DMA((6 * n,)),
                           pltpu.SemaphoreType.DMA((2 * n,))]),
        compiler_params=pltpu.CompilerParams(vmem_limit_bytes=VMEM_LIMIT),
    )(*quarters)


def exchange_with_sibling(grads, name):
    n = len(grads)

    def body(*refs):
        g_refs, theirs_refs = refs[:n], refs[n:2 * n]
        send_sems, recv_sems = refs[2 * n:]
        x, y, c = _position()
        copies = []
        for i in range(n):
            _, other = _halves(g_refs[i].shape[1], c)
            cp = pltpu.make_async_remote_copy(
                src_ref=g_refs[i].at[:, other], dst_ref=theirs_refs[i],
                send_sem=send_sems.at[i], recv_sem=recv_sems.at[i],
                device_id=(x, y, 1 - c), device_id_type=MESH)
            cp.start()
            copies.append(cp)
        for cp in copies:
            cp.wait()

    return pl.pallas_call(
        body, name=name,
        out_shape=[jax.ShapeDtypeStruct((N_CHIPS, g.shape[1] // 2, g.shape[2]), F32) for g in grads],
        in_specs=_any_specs(n), out_specs=_any_specs(n),
        scratch_shapes=[pltpu.SemaphoreType.DMA((n,)), pltpu.SemaphoreType.DMA((n,))],
    )(*grads)


def _scatter_copies(b_refs, got_refs, send_sems, recv_sems):
    n = len(b_refs)
    x, y, c = _position()
    copies = []
    for j, chip in enumerate(_other_chips(x, y)):
        qj = 2 * chip[0] + chip[1]
        for i in range(n):
            copies.append(pltpu.make_async_remote_copy(
                src_ref=b_refs[i].at[qj], dst_ref=got_refs[i].at[j],
                send_sem=send_sems.at[j * n + i], recv_sem=recv_sems.at[j * n + i],
                device_id=(*chip, c), device_id_type=MESH))
    return copies


def _scatter_shapes(chip_sums):
    return [jax.ShapeDtypeStruct((N_CHIPS - 1, *b.shape[1:]), BF16) for b in chip_sums]


def scatter_to_owners(chip_sums, name):
    n = len(chip_sums)

    def body(*refs):
        copies = _scatter_copies(refs[:n], refs[n:2 * n], *refs[2 * n:])
        for cp in copies:
            cp.start()
        for cp in copies:
            cp.wait()

    return pl.pallas_call(
        body, name=name,
        out_shape=_scatter_shapes(chip_sums),
        in_specs=_any_specs(n), out_specs=_any_specs(n),
        scratch_shapes=[pltpu.SemaphoreType.DMA((3 * n,)), pltpu.SemaphoreType.DMA((3 * n,))],
    )(*chip_sums)


def join_halves(reduced):
    n = len(reduced)

    def body(*refs):
        buf_refs = refs[n:2 * n]
        send_sems, recv_sems = refs[2 * n:]
        x, y, c = _position()
        copies = []
        for i in range(n):
            mine, _ = _halves(buf_refs[i].shape[0], c)
            cp = pltpu.make_async_remote_copy(
                src_ref=buf_refs[i].at[mine], dst_ref=buf_refs[i].at[mine],
                send_sem=send_sems.at[i], recv_sem=recv_sems.at[i],
                device_id=(x, y, 1 - c), device_id_type=MESH)
            cp.start()
            copies.append(cp)
        for cp in copies:
            cp.wait()

    return pl.pallas_call(
        body, name="join_halves",
        out_shape=[jax.ShapeDtypeStruct(r.shape, F32) for r in reduced],
        in_specs=_any_specs(n), out_specs=_any_specs(n),
        input_output_aliases={i: i for i in range(n)},
        scratch_shapes=[pltpu.SemaphoreType.DMA((n,)), pltpu.SemaphoreType.DMA((n,))],
    )(*reduced)


ADD_ROWS = 256


def add_halves(grad, theirs, place, name):
    _, half, cols = theirs.shape
    rb = min(ADD_ROWS, half)
    steps = half // rb

    def body(place_ref, a_ref, b_ref, f_ref, h_ref):
        s = a_ref[...] + b_ref[...]
        f_ref[...] = s
        h_ref[...] = _bf(s)

    spec = pl.BlockSpec((1, rb, cols), lambda i, j, place: (i, j, 0))
    return pl.pallas_call(
        body, name=name,
        grid_spec=pltpu.PrefetchScalarGridSpec(
            num_scalar_prefetch=1, grid=(N_CHIPS, steps),
            in_specs=[pl.BlockSpec((1, rb, cols), lambda i, j, place: (i, place[0] * steps + j, 0)), spec],
            out_specs=(spec, spec)),
        out_shape=(jax.ShapeDtypeStruct(theirs.shape, F32), jax.ShapeDtypeStruct(theirs.shape, BF16)),
        compiler_params=_params("parallel", "parallel"),
    )(place, grad, theirs)


def add_parts(chip_sum, got, place, name):
    _, half, cols = got.shape
    rb = min(ADD_ROWS, half)
    steps = half // rb

    def body(place_ref, o_ref, g_ref, out_ref):
        s = o_ref[0]
        for j in range(N_CHIPS - 1):
            s = s + g_ref[j].astype(F32)
        out_ref[...] = s

    return pl.pallas_call(
        body, name=name,
        grid_spec=pltpu.PrefetchScalarGridSpec(
            num_scalar_prefetch=1, grid=(steps,),
            in_specs=[pl.BlockSpec((1, rb, cols), lambda j, place: (place[1], j, 0)),
                      pl.BlockSpec((N_CHIPS - 1, rb, cols), lambda j, place: (0, j, 0))],
            out_specs=pl.BlockSpec((rb, cols), lambda j, place: (place[0] * steps + j, 0))),
        out_shape=jax.ShapeDtypeStruct((2 * half, cols), F32),
        compiler_params=_params("parallel"),
    )(place, chip_sum, got)


def _adam_math(w, g, m, v):
    m = ADAM_B1 * m + (1.0 - ADAM_B1) * g
    v = ADAM_B2 * v + (1.0 - ADAM_B2) * (g * g)
    m_hat = m / (1.0 - ADAM_B1 ** ADAM_STEP)
    v_hat = v / (1.0 - ADAM_B2 ** ADAM_STEP)
    delta = -ADAM_LR * (m_hat / (jnp.sqrt(v_hat) + ADAM_EPS) + ADAM_WD * w)
    return delta, m, v


def adamw(w, g, m, v, name):
    rows, cols = w.shape
    fits = [t for t in range(8, rows, 8) if rows % t == 0 and t * cols * 4 <= 2 ** 20]
    tile = max(fits) if fits else rows

    def body(w_ref, g_ref, m_ref, v_ref, d_ref, nm_ref, nv_ref):
        d, nm, nv = _adam_math(w_ref[...], g_ref[...], m_ref[...], v_ref[...])
        d_ref[...] = d
        nm_ref[...] = nm
        nv_ref[...] = nv

    spec = pl.BlockSpec((tile, cols), lambda i: (i, 0))
    shape = jax.ShapeDtypeStruct((rows, cols), F32)
    return pl.pallas_call(
        body, name=name, grid=(rows // tile,),
        out_shape=(shape, shape, shape),
        in_specs=[spec] * 4, out_specs=(spec, spec, spec),
        compiler_params=_params("parallel"),
    )(w, g, m, v)


def adamw_rows(w, g, m, v, name):
    rows, _, cols = w.shape
    tile = rows // 4

    def body(w_ref, g_ref, m_ref, v_ref, d_ref, nm_ref, nv_ref):
        d, nm, nv = _adam_math(w_ref[...], g_ref[...], m_ref[...], v_ref[...])
        d_ref[...] = d
        nm_ref[...] = nm
        nv_ref[...] = nv

    spec = pl.BlockSpec((tile, 1, cols), lambda i: (i, 0, 0))
    shape = jax.ShapeDtypeStruct(w.shape, F32)
    return pl.pallas_call(
        body, name=name, grid=(rows // tile,),
        out_shape=(shape, shape, shape),
        in_specs=[spec] * 4, out_specs=(spec, spec, spec),
        compiler_params=_params("parallel"),
    )(w, g, m, v)


def matmul_tn(a, b, name, tile_n=512, tile_s=2048, by_column_tile=False):
    s, m = a.shape
    n = b.shape[1]
    tile_n = min(tile_n, n)
    tile_s = min(tile_s, s)
    steps = s // tile_s
    if by_column_tile:
        out_shape = jax.ShapeDtypeStruct((n // tile_n, m, tile_n), F32)
        out_spec = pl.BlockSpec((None, m, tile_n), lambda j, k: (j, 0, 0))
    else:
        out_shape = jax.ShapeDtypeStruct((m, n), F32)
        out_spec = pl.BlockSpec((m, tile_n), lambda j, k: (0, j))

    def body(a_ref, b_ref, out_ref):
        k = pl.program_id(1)

        @pl.when(k == 0)
        def _():
            out_ref[...] = jnp.zeros_like(out_ref)

        out_ref[...] += _tn(a_ref[...], b_ref[...])

    return pl.pallas_call(
        body, name=name, grid=(n // tile_n, steps),
        out_shape=out_shape,
        in_specs=[pl.BlockSpec((tile_s, m), lambda j, k: (k, 0)),
                  pl.BlockSpec((tile_s, tile_n), lambda j, k: (k, j))],
        out_specs=out_spec,
        compiler_params=_params("parallel", "arbitrary"),
    )(a, b)


ROW_TILE = 512


def _row_index(tile, rows):
    return tile * rows + lax.broadcasted_iota(jnp.int32, (rows, 1), 0)


def _inverse_counts(t_glob):
    return [1.0 / jnp.minimum(t_glob + 1, w).astype(F32) for w in POOL_WINDOWS]


def _sigmoid(z):
    return 1.0 / (1.0 + jnp.exp(-z))


def gather_in_background(step, last, out_refs, send_sems, recv_sems):
    n = len(out_refs)
    x, y, c = _position()
    q = 2 * x + y
    chips = _other_chips(x, y)

    def copy(k, i, quarter, half, to):
        return _gather_copy(out_refs[i], send_sems, recv_sems, k * n + i, quarter, half, to)

    @pl.when(step == 0)
    def _():
        for i in range(n):
            mine, _ = _halves(out_refs[i].shape[1], c)
            for j, chip in enumerate(chips):
                copy(j, i, q, mine, (*chip, c)).start()

    @pl.when(step == last - 1)
    def _():
        for j, chip in enumerate(chips):
            qj = 2 * chip[0] + chip[1]
            for i in range(n):
                mine, _ = _halves(out_refs[i].shape[1], c)
                copy(j, i, qj, mine, (x, y, c)).wait_recv()
                copy(3 + j, i, qj, mine, (x, y, 1 - c)).start()

    @pl.when(step == last)
    def _():
        for j, chip in enumerate(chips):
            qj = 2 * chip[0] + chip[1]
            for i in range(n):
                mine, other = _halves(out_refs[i].shape[1], c)
                copy(3 + j, i, qj, other, (x, y, c)).wait_recv()
                copy(j, i, q, mine, (x, y, c)).wait_send()
                copy(3 + j, i, qj, mine, (x, y, c)).wait_send()


def pool_forward(x, w0, wpi, gw, gb, scale, wpo, later):
    s = x.shape[0]
    ts = ROW_TILE
    nt = s // ts
    assert nt >= 2
    n_later = len(later)

    def body(x_ref, w0_ref, wpi_ref, gw_ref, gb_ref, sc_ref, wpo_ref, *rest):
        rest = rest[n_later:]
        h1_ref, pooled_ref, gt_ref, n0_ref = rest[:4]
        later_refs = rest[4:4 + n_later]
        ubuf, send_sems, recv_sems = rest[4 + n_later:]
        i = pl.program_id(0)
        gather_in_background(i, nt - 1, later_refs, send_sems, recv_sems)
        xv = x_ref[...]
        r = lax.rsqrt(jnp.mean(xv * xv, axis=-1, keepdims=True) + EPS)
        n0 = _bf(xv * r * w0_ref[...])
        n0_ref[...] = n0
        u = jnp.concatenate([_nn(n0, wpi_ref[0]), _nn(n0, wpi_ref[1])], axis=-1)
        gt = jnp.concatenate([_nn(n0, wpi_ref[2]), _nn(n0, wpi_ref[3])], axis=-1)
        gt_ref[...] = gt

        @pl.when(i == 0)
        def _():
            ubuf[0:HALO, :] = jnp.zeros((HALO, D), F32)

        ubuf[HALO:HALO + ts, :] = u
        inv = _inverse_counts(_row_index(i, ts))
        mixed = []
        for g, w in enumerate(POOL_WINDOWS):
            cols = slice(g * GROUP_DIM, (g + 1) * GROUP_DIM)
            ug = u[:, cols]
            acc = ug
            for j in range(1, w):
                acc = acc + ubuf[HALO - j:HALO - j + ts, cols]
            pooled = _bf(acc * inv[g] - ug)
            pooled_ref[:, cols] = pooled
            mixed.append(_nn(pooled, gw_ref[g]))
        ubuf[0:HALO, :] = ubuf[ts:ts + HALO, :]
        mixed = jnp.concatenate(mixed, axis=-1) + gb_ref[...]
        y = mixed * sc_ref[...] * (gt * _sigmoid(gt))
        h1_ref[...] = xv + _nn(_bf(y), wpo_ref[...])

    row = lambda cols: pl.BlockSpec((ts, cols), lambda i: (i, 0))
    outs = pl.pallas_call(
        body, name="pool_forward", grid=(nt,),
        out_shape=[jax.ShapeDtypeStruct((s, D), F32), jax.ShapeDtypeStruct((s, D), BF16),
                   jax.ShapeDtypeStruct((s, D), F32), jax.ShapeDtypeStruct((s, D), BF16)]
                  + [jax.ShapeDtypeStruct(a.shape, a.dtype) for a in later],
        in_specs=[row(D), _full((1, D)), _full((N_CHIPS, D, D // 2)), _full((GROUPS, GROUP_DIM, GROUP_DIM)),
                  _full((1, D)), _full((1, D)), _full((D, D))] + _any_specs(n_later),
        out_specs=[row(D), row(D), row(D), row(D)] + _any_specs(n_later),
        input_output_aliases={7 + k: 4 + k for k in range(n_later)},
        scratch_shapes=[pltpu.VMEM((HALO + ts, D), F32),
                        pltpu.SemaphoreType.DMA((6 * n_later,)), pltpu.SemaphoreType.DMA((6 * n_later,))],
        compiler_params=_params("arbitrary"),
    )(x, w0, wpi, gw, gb, scale, wpo, *later)
    return outs[:4], outs[4:]


def pool_backward(x, dh1, pooled, gt, w0, wpi, gw, gb, scale, wpo, chip_sums):
    s = x.shape[0]
    ts = ROW_TILE
    nt = s // ts
    n_sums = len(chip_sums)

    def body(x_ref, dh1_ref, pooled_ref, gt_ref, w0_ref, wpi_ref, gw_ref, gb_ref, sc_ref, wpo_ref, *rest):
        sum_refs, rest = rest[:n_sums], rest[n_sums:]
        dx_ref, y_ref, dproj_ref, ggw_ref, small_ref = rest[:5]
        got_refs = rest[5:5 + n_sums]
        ebuf, send_sems, recv_sems = rest[5 + n_sums:]
        i = pl.program_id(0)
        copies = _scatter_copies(sum_refs, got_refs, send_sems, recv_sems)

        @pl.when(i == 0)
        def _():
            for cp in copies:
                cp.start()

        @pl.when(i == nt - 1)
        def _():
            for cp in copies:
                cp.wait()

        @pl.when(i == 0)
        def _():
            ggw_ref[...] = jnp.zeros_like(ggw_ref)
            small_ref[...] = jnp.zeros_like(small_ref)
            ebuf[ts:ts + HALO, :] = jnp.zeros((HALO, D), F32)

        dh1 = dh1_ref[...]
        gt = gt_ref[...]
        sc = sc_ref[...]
        dy = _nt(_bf(dh1), wpo_ref[...])
        pooled_bf = []
        mixed = []
        for g in range(GROUPS):
            cols = slice(g * GROUP_DIM, (g + 1) * GROUP_DIM)
            pb = pooled_ref[:, cols]
            pooled_bf.append(pb)
            mixed.append(_nn(pb, gw_ref[g]))
        mixed = jnp.concatenate(mixed, axis=-1) + gb_ref[...]
        sg = _sigmoid(gt)
        silu = gt * sg
        y_ref[...] = _bf(mixed * sc * silu)
        dmixed = dy * sc * silu
        dgt = dy * mixed * sc * (sg * (1.0 + gt * (1.0 - sg)))
        dproj_ref[:, D:] = _bf(dgt)
        small_ref[1:2, :] += jnp.sum(dy * mixed * silu, axis=0, keepdims=True)
        small_ref[2:3, :] += jnp.sum(dmixed, axis=0, keepdims=True)

        inv = _inverse_counts(_row_index(nt - 1 - i, ts))
        dpooled = []
        for g in range(GROUPS):
            cols = slice(g * GROUP_DIM, (g + 1) * GROUP_DIM)
            dm = _bf(dmixed[:, cols])
            ggw_ref[g] += _tn(pooled_bf[g], dm)
            dp = _nt(dm, gw_ref[g])
            dpooled.append(dp)
            ebuf[0:ts, cols] = dp * inv[g]
        du = []
        for g, w in enumerate(POOL_WINDOWS):
            cols = slice(g * GROUP_DIM, (g + 1) * GROUP_DIM)
            acc = -dpooled[g]
            for j in range(w):
                acc = acc + ebuf[j:j + ts, cols]
            du.append(acc)
        ebuf[ts:ts + HALO, :] = ebuf[0:HALO, :]
        du = _bf(jnp.concatenate(du, axis=-1))
        dproj_ref[:, :D] = du
        dgt_bf = _bf(dgt)
        half = D // 2
        dn0 = (_nt(du[:, :half], wpi_ref[0]) + _nt(du[:, half:], wpi_ref[1])
               + _nt(dgt_bf[:, :half], wpi_ref[2]) + _nt(dgt_bf[:, half:], wpi_ref[3]))

        xv = x_ref[...]
        r = lax.rsqrt(jnp.mean(xv * xv, axis=-1, keepdims=True) + EPS)
        xhat = xv * r
        small_ref[0:1, :] += jnp.sum(dn0 * xhat, axis=0, keepdims=True)
        dxh = dn0 * w0_ref[...]
        dx_ref[...] = dh1 + r * (dxh - xhat * jnp.mean(dxh * xhat, axis=-1, keepdims=True))

    row = lambda cols: pl.BlockSpec((ts, cols), lambda i: (nt - 1 - i, 0))
    outs = pl.pallas_call(
        body, name="pool_backward", grid=(nt,),
        out_shape=[jax.ShapeDtypeStruct((s, D), F32), jax.ShapeDtypeStruct((s, D), BF16),
                   jax.ShapeDtypeStruct((s, 2 * D), BF16),
                   jax.ShapeDtypeStruct((GROUPS, GROUP_DIM, GROUP_DIM), F32),
                   jax.ShapeDtypeStruct((8, D), F32)] + _scatter_shapes(chip_sums),
        in_specs=[row(D), row(D), row(D), row(D), _full((1, D)), _full((N_CHIPS, D, D // 2)),
                  _full((GROUPS, GROUP_DIM, GROUP_DIM)), _full((1, D)), _full((1, D)), _full((D, D))]
                 + _any_specs(n_sums),
        out_specs=[row(D), row(D), row(2 * D), _full((GROUPS, GROUP_DIM, GROUP_DIM)), _full((8, D))]
                  + _any_specs(n_sums),
        scratch_shapes=[pltpu.VMEM((ts + HALO, D), F32),
                        pltpu.SemaphoreType.DMA((3 * n_sums,)), pltpu.SemaphoreType.DMA((3 * n_sums,))],
        compiler_params=_params("arbitrary"),
    )(x, dh1, pooled, gt, w0, wpi, gw, gb, scale, wpo, *chip_sums)
    return outs[:5], outs[5:]


def gla_project(h1, w1, wgi, wlow, wgk, bgk):
    s = h1.shape[0]
    ts = ROW_TILE

    def body(h_ref, w1_ref, wgi_ref, wlow_ref, wgk_ref, bgk_ref, qk_ref, v_ref, gate_ref, low_ref, cum_ref,
             n1_ref):
        hv = h_ref[...]
        r = lax.rsqrt(jnp.mean(hv * hv, axis=-1, keepdims=True) + EPS)
        n1 = _bf(hv * r * w1_ref[...])
        n1_ref[...] = n1
        qk_ref[...] = _nn(n1, wgi_ref[:, 0:2 * KEY_W])
        v_ref[...] = _bf(_nn(n1, wgi_ref[:, 2 * KEY_W:2 * KEY_W + D]))
        gate_ref[...] = _nn(n1, wgi_ref[:, 2 * KEY_W + D:GLA_MAIN])
        low = _bf(_nn(n1, wlow_ref[...]))
        low_ref[...] = low
        z = _nn(low, wgk_ref[...]) + bgk_ref[...]
        lg = (jnp.minimum(z, 0.0) - jnp.log(1.0 + jnp.exp(-jnp.abs(z)))) / GATE_NORM
        lower_f = _chunk_masks()[0].astype(F32)
        for r0 in range(0, ts, CHUNK):
            cum_ref[r0:r0 + CHUNK, :] = _nn_exact(lower_f, lg[r0:r0 + CHUNK, :])

    row = lambda cols: pl.BlockSpec((ts, cols), lambda i: (i, 0))
    return pl.pallas_call(
        body, name="gla_project", grid=(s // ts,),
        out_shape=(jax.ShapeDtypeStruct((s, D), F32), jax.ShapeDtypeStruct((s, D), BF16),
                   jax.ShapeDtypeStruct((s, D), F32), jax.ShapeDtypeStruct((s, RANK_PAD), BF16),
                   jax.ShapeDtypeStruct((s, KEY_W), F32), jax.ShapeDtypeStruct((s, D), BF16)),
        in_specs=[row(D), _full((1, D)), _full((D, GLA_MAIN)), _full((D, RANK_PAD)),
                  _full((RANK_PAD, KEY_W)), _full((1, KEY_W))],
        out_specs=(row(D), row(D), row(D), row(RANK_PAD), row(KEY_W), row(D)),
        compiler_params=_params("parallel"),
    )(h1, w1, wgi, wlow, wgk, bgk)


GLA_BLOCK = 512
CHUNKS_PER_BLOCK = GLA_BLOCK // CHUNK


def _chunk_masks():
    t = lax.broadcasted_iota(jnp.int32, (CHUNK, CHUNK), 0)
    u = lax.broadcasted_iota(jnp.int32, (CHUNK, CHUNK), 1)
    return t >= u, t <= u


def _gla_chunk_terms(q, cum):
    ep = jnp.exp(cum)
    en = jnp.exp(-cum)
    qs = q * (HEAD_K ** -0.5)
    last = cum[CHUNK - 1:CHUNK, :]
    ed = jnp.exp(last - cum)
    dec = jnp.exp(last)
    return ep, en, qs, ed, dec


def gla_forward(qk, v, cum):
    s = qk.shape[0]
    nb = s // GLA_BLOCK
    nc = s // CHUNK

    def body(q_ref, k_ref, v_ref, cum_ref, o_ref, st_ref, state):
        @pl.when(pl.program_id(0) == 0)
        def _():
            state[...] = jnp.zeros_like(state)

        lower, _ = _chunk_masks()

        def chunk(cc, carry):
            rows = pl.ds(pl.multiple_of(cc * CHUNK, CHUNK), CHUNK)
            for h in range(HEADS):
                kc = slice(h * HEAD_K, (h + 1) * HEAD_K)
                vc = slice(h * HEAD_V, (h + 1) * HEAD_V)
                q = q_ref[rows, kc]
                k = k_ref[rows, kc]
                v = v_ref[rows, vc]
                ep, en, qs, ed, dec = _gla_chunk_terms(q, cum_ref[rows, kc])
                a = _bf(qs * ep)
                fwd = _nt(a, _bf(k * en))
                bwd = _nt(_bf(qs * en), _bf(k * ep))
                scores = jnp.where(lower, fwd, bwd)
                st = state[h]
                st_ref[cc, h] = st
                o_ref[rows, vc] = _nn(_bf(scores), v) + _nt(a, _bf(st))
                state[h] = st * dec + _tn(v, _bf(k * ed))
            return carry

        lax.fori_loop(0, CHUNKS_PER_BLOCK, chunk, 0, unroll=4)

    return pl.pallas_call(
        body, name="gla_forward", grid=(nb,),
        out_shape=(jax.ShapeDtypeStruct((s, D), F32),
                   jax.ShapeDtypeStruct((nc, HEADS, HEAD_V, HEAD_K), F32)),
        in_specs=[pl.BlockSpec((GLA_BLOCK, KEY_W), lambda i: (i, 0)),
                  pl.BlockSpec((GLA_BLOCK, KEY_W), lambda i: (i, 1)),
                  pl.BlockSpec((GLA_BLOCK, D), lambda i: (i, 0)),
                  pl.BlockSpec((GLA_BLOCK, KEY_W), lambda i: (i, 0))],
        out_specs=(pl.BlockSpec((GLA_BLOCK, D), lambda i: (i, 0)),
                   pl.BlockSpec((CHUNKS_PER_BLOCK, HEADS, HEAD_V, HEAD_K), lambda i: (i, 0, 0, 0))),
        scratch_shapes=[pltpu.VMEM((HEADS, HEAD_V, HEAD_K), F32)],
        compiler_params=_params("arbitrary"),
    )(qk, qk, v, cum)


def gla_backward(qk, v, cum, do, states):
    s = qk.shape[0]
    nb = s // GLA_BLOCK

    def body(q_ref, k_ref, v_ref, cum_ref, do_ref, st_ref, dq_ref, dk_ref, dv_ref, dcum_ref, dstate):
        @pl.when(pl.program_id(0) == 0)
        def _():
            dstate[...] = jnp.zeros_like(dstate)

        lower, _ = _chunk_masks()
        is_last = lax.broadcasted_iota(jnp.int32, (CHUNK, HEAD_K), 0) == CHUNK - 1

        def chunk(step, carry):
            cc = CHUNKS_PER_BLOCK - 1 - step
            rows = pl.ds(pl.multiple_of(cc * CHUNK, CHUNK), CHUNK)
            for h in range(HEADS):
                kc = slice(h * HEAD_K, (h + 1) * HEAD_K)
                vc = slice(h * HEAD_V, (h + 1) * HEAD_V)
                q = q_ref[rows, kc]
                k = k_ref[rows, kc]
                v = v_ref[rows, vc]
                do_c = do_ref[rows, vc]
                ep, en, qs, ed, dec = _gla_chunk_terms(q, cum_ref[rows, kc])
                a = _bf(qs * ep)
                b = _bf(k * en)
                c = _bf(qs * en)
                dk_dec = _bf(k * ep)
                kd = _bf(k * ed)
                scores = _bf(jnp.where(lower, _nt(a, b), _nt(c, dk_dec)))
                st = st_ref[cc, h]
                dst = dstate[h]
                dst_bf = _bf(dst)

                dscores = _nt(do_c, v)
                dfwd = _bf(jnp.where(lower, dscores, 0.0))
                dbwd = _bf(jnp.where(lower, 0.0, dscores))
                dv_ref[rows, vc] = _bf(_tn(scores, do_c) + _nt(kd, dst_bf))
                da = _nn(dfwd, b) + _nn(do_c, _bf(st))
                db = _tn(dfwd, a)
                dc = _nn(dbwd, dk_dec)
                ddk = _tn(dbwd, c)
                dkd = _nn(v, dst_bf)
                ddec = jnp.sum(dst * st, axis=0, keepdims=True)
                dstate[h] = dst * dec + _tn(do_c, a)

                m = dkd * k * ed
                dq_ref[rows, kc] = _bf((da * ep + dc * en) * (HEAD_K ** -0.5))
                dk_ref[rows, kc] = _bf(db * en + ddk * ep + dkd * ed)
                dcum = (da * qs + ddk * k) * ep - (db * k + dc * qs) * en - m
                dlast = jnp.sum(m, axis=0, keepdims=True) + ddec * dec
                dcum_ref[rows, kc] = dcum + jnp.where(is_last, dlast, 0.0)
            return carry

        lax.fori_loop(0, CHUNKS_PER_BLOCK, chunk, 0, unroll=4)

    rev = lambda cols, col_block: pl.BlockSpec((GLA_BLOCK, cols), lambda i: (nb - 1 - i, col_block))
    return pl.pallas_call(
        body, name="gla_backward", grid=(nb,),
        out_shape=(jax.ShapeDtypeStruct((s, KEY_W), BF16), jax.ShapeDtypeStruct((s, KEY_W), BF16),
                   jax.ShapeDtypeStruct((s, D), BF16), jax.ShapeDtypeStruct((s, KEY_W), F32)),
        in_specs=[rev(KEY_W, 0), rev(KEY_W, 1), rev(D, 0), rev(KEY_W, 0), rev(D, 0),
                  pl.BlockSpec((CHUNKS_PER_BLOCK, HEADS, HEAD_V, HEAD_K), lambda i: (nb - 1 - i, 0, 0, 0))],
        out_specs=(rev(KEY_W, 0), rev(KEY_W, 0), rev(D, 0), rev(KEY_W, 0)),
        scratch_shapes=[pltpu.VMEM((HEADS, HEAD_V, HEAD_K), F32)],
        compiler_params=_params("arbitrary"),
    )(qk, qk, v, cum, do, states)


def head_and_loss(o, gate, h1, target, hw, wgo, wf):
    s = o.shape[0]
    ts = ROW_TILE

    def body(o_ref, gate_ref, h1_ref, tgt_ref, hw_ref, wgo_ref, wf_ref,
             dh2_ref, dh2bf_ref, do_ref, dgate_ref, y2_ref, small_ref):
        @pl.when(pl.program_id(0) == 0)
        def _():
            small_ref[...] = jnp.zeros_like(small_ref)

        gate = gate_ref[...]
        hw = hw_ref[...]
        sg = _sigmoid(gate)
        silu = gate * sg
        ohat, ro = [], []
        for h in range(HEADS):
            oh = o_ref[:, h * HEAD_V:(h + 1) * HEAD_V]
            rh = lax.rsqrt(jnp.mean(oh * oh, axis=-1, keepdims=True) + EPS)
            ro.append(rh)
            ohat.append(oh * rh)
        ohat = jnp.concatenate(ohat, axis=-1)
        on = ohat * hw
        y2 = _bf(on * silu)
        y2_ref[...] = y2
        h2 = h1_ref[...] + _nn(y2, wgo_ref[...])
        rf = lax.rsqrt(jnp.mean(h2 * h2, axis=-1, keepdims=True) + EPS)
        h2hat = h2 * rf
        wf = wf_ref[...]
        diff = h2hat * wf - tgt_ref[...]
        small_ref[2:3, :] += jnp.zeros((1, D), F32) + 0.5 * jnp.sum(diff * diff) / D
        dout = diff / D
        small_ref[0:1, :] += jnp.sum(dout * h2hat, axis=0, keepdims=True)
        dxh = dout * wf
        dh2 = rf * (dxh - h2hat * jnp.mean(dxh * h2hat, axis=-1, keepdims=True))
        dh2_ref[...] = dh2
        dh2_bf = _bf(dh2)
        dh2bf_ref[...] = dh2_bf
        dy2 = _nt(dh2_bf, wgo_ref[...])
        don = dy2 * silu
        dgate_ref[...] = _bf(dy2 * on * (sg * (1.0 + gate * (1.0 - sg))))
        ghw = jnp.sum(don * ohat, axis=0, keepdims=True)
        small_ref[1:2, 0:HEAD_V] += sum(ghw[:, h * HEAD_V:(h + 1) * HEAD_V] for h in range(HEADS))
        dohat = don * hw
        for h in range(HEADS):
            cols = slice(h * HEAD_V, (h + 1) * HEAD_V)
            oh, dh = ohat[:, cols], dohat[:, cols]
            do_ref[:, cols] = _bf(ro[h] * (dh - oh * jnp.mean(dh * oh, axis=-1, keepdims=True)))

    row = lambda cols: pl.BlockSpec((ts, cols), lambda i: (i, 0))
    act = jax.ShapeDtypeStruct((s, D), F32)
    act_bf = jax.ShapeDtypeStruct((s, D), BF16)
    return pl.pallas_call(
        body, name="head_and_loss", grid=(s // ts,),
        out_shape=(act, act_bf, act_bf, act_bf, act_bf, jax.ShapeDtypeStruct((8, D), F32)),
        in_specs=[row(D), row(D), row(D), row(D),
                  _full((1, D)), _full((D, D)), _full((1, D))],
        out_specs=(row(D), row(D), row(D), row(D), row(D), _full((8, D))),
        compiler_params=_params("arbitrary"),
    )(o, gate, h1, target, hw, wgo, wf)


def gla_project_backward(dq, dk, dv, dgate, dcum, low, h1, dh2, w1, wgi, wlow, wgk, bgk):
    s = h1.shape[0]
    ts = ROW_TILE

    def body(dq_ref, dk_ref, dv_ref, dgate_ref, dcum_ref, low_ref, h1_ref, dh2_ref, w1_ref,
             wgi_ref, wlow_ref, wgk_ref, bgk_ref, dh1_ref, dh1bf_ref, dproj_ref, dlow_ref, ggk_ref,
             small_ref):
        @pl.when(pl.program_id(0) == 0)
        def _():
            ggk_ref[...] = jnp.zeros_like(ggk_ref)
            small_ref[...] = jnp.zeros_like(small_ref)

        low = low_ref[...]
        z = _nn(low, wgk_ref[...]) + bgk_ref[...]
        upper_f = _chunk_masks()[1].astype(F32)
        dlg = jnp.concatenate([_nn_exact(upper_f, dcum_ref[r0:r0 + CHUNK, :]) for r0 in range(0, ts, CHUNK)],
                              axis=0)
        dz = dlg * (1.0 / GATE_NORM) * _sigmoid(-z)
        dz_bf = _bf(dz)
        ggk_ref[...] += _tn(low, dz_bf)
        small_ref[1:2, 0:KEY_W] += jnp.sum(dz, axis=0, keepdims=True)
        dlow = _bf(_nt(dz_bf, wgk_ref[...]))
        dlow_ref[...] = dlow
        dn1 = _nt(dlow, wlow_ref[...])
        for ref, lo, hi in ((dq_ref, 0, KEY_W), (dk_ref, KEY_W, 2 * KEY_W),
                            (dv_ref, 2 * KEY_W, 2 * KEY_W + D), (dgate_ref, 2 * KEY_W + D, GLA_MAIN)):
            piece = ref[...]
            dproj_ref[:, lo:hi] = piece
            dn1 = dn1 + _nt(piece, wgi_ref[:, lo:hi])
        hv = h1_ref[...]
        r = lax.rsqrt(jnp.mean(hv * hv, axis=-1, keepdims=True) + EPS)
        hhat = hv * r
        small_ref[0:1, :] += jnp.sum(dn1 * hhat, axis=0, keepdims=True)
        dxh = dn1 * w1_ref[...]
        dh1 = dh2_ref[...] + r * (dxh - hhat * jnp.mean(dxh * hhat, axis=-1, keepdims=True))
        dh1_ref[...] = dh1
        dh1bf_ref[...] = _bf(dh1)

    row = lambda cols: pl.BlockSpec((ts, cols), lambda i: (i, 0))
    return pl.pallas_call(
        body, name="gla_project_backward", grid=(s // ts,),
        out_shape=(jax.ShapeDtypeStruct((s, D), F32), jax.ShapeDtypeStruct((s, D), BF16),
                   jax.ShapeDtypeStruct((s, GLA_MAIN), BF16),
                   jax.ShapeDtypeStruct((s, RANK_PAD), BF16), jax.ShapeDtypeStruct((RANK_PAD, KEY_W), F32),
                   jax.ShapeDtypeStruct((8, D), F32)),
        in_specs=[row(KEY_W), row(KEY_W), row(D), row(D), row(KEY_W), row(RANK_PAD), row(D), row(D),
                  _full((1, D)), _full((D, GLA_MAIN)), _full((D, RANK_PAD)), _full((RANK_PAD, KEY_W)),
                  _full((1, KEY_W))],
        out_specs=(row(D), row(D), row(GLA_MAIN), row(RANK_PAD), _full((RANK_PAD, KEY_W)), _full((8, D))),
        compiler_params=_params("arbitrary"),
    )(dq, dk, dv, dgate, dcum, low, h1, dh2, w1, wgi, wlow, wgk, bgk)


def _groups_from_quarters(a):
    return a.reshape(N_CHIPS, GROUPS, 64, GROUP_DIM).transpose(1, 0, 2, 3).reshape(GROUPS, GROUP_DIM, GROUP_DIM)


def _quarters_from_groups(a):
    return a.reshape(GROUPS, N_CHIPS, 64, GROUP_DIM).transpose(1, 0, 2, 3).reshape(N_CHIPS, GROUP_DIM, GROUP_DIM)


def _pad_row(*pieces):
    flat = jnp.concatenate([p.reshape(-1).astype(F32) for p in pieces])
    return jnp.pad(flat, (0, D - flat.shape[0])).reshape(1, D)


def _gla_weights(wgi_q, wgo_q):
    wgi_all = jnp.concatenate([wgi_q[q] for q in range(N_CHIPS)], axis=1)
    wlow = jnp.pad(wgi_all[:, GLA_MAIN:], ((0, 0), (0, RANK_PAD - GATE_RANK)))
    return wgi_all[:, :GLA_MAIN], wlow, wgo_q.reshape(D, D)


def local_gradients(xs, target, w0, w1, wf, wpi, gw, gb, scale, wpo, gla_quarters, wgk, bgk, hw_tiled, place):
    (h1, pooled, gt, n0), gla_all = pool_forward(xs, w0, wpi, gw, gb, scale, wpo, gla_quarters)
    wgi, wlow, wgo = _gla_weights(*gla_all)
    qk, v, gate, low, cum, n1 = gla_project(h1, w1, wgi, wlow, wgk, bgk)
    o, states = gla_forward(qk, v, cum)

    dh2, dh2_bf, do, dgate, y2, small_top = head_and_loss(o, gate, h1, target, hw_tiled, wgo, wf)
    g_gla_out = matmul_tn(y2, dh2_bf, "grad_gla_out")
    dq, dk, dv, dcum = gla_backward(qk, v, cum, do, states)
    dh1, dh1_bf, dproj, dlow, g_gk_pad, small_gla = gla_project_backward(
        dq, dk, dv, dgate, dcum, low, h1, dh2, w1, wgi, wlow, wgk, bgk)
    g_gla_in = jnp.concatenate([matmul_tn(n1, dproj, "grad_gla_in"),
                                matmul_tn(n1, dlow, "grad_gla_low")[:, :GATE_RANK]], axis=1)

    def chip_sums(grads, names, tag):
        theirs = exchange_with_sibling(grads, "exchange_with_sibling_" + tag)
        return [add_halves(g, t, place, "add_halves_" + n) for g, t, n in zip(grads, theirs, names)]

    gla_names = ("gla_in", "gla_out")
    gla_sums = chip_sums(
        [jnp.stack([g_gla_in[:, GLA_IN_QUARTER * q:GLA_IN_QUARTER * (q + 1)] for q in range(N_CHIPS)]),
         g_gla_out.reshape(N_CHIPS, D // N_CHIPS, D)], gla_names, "gla")
    (dx, y, dpool, g_group_w, small_pool), gla_got = pool_backward(
        xs, dh1, pooled, gt, w0, wpi, gw, gb, scale, wpo, [b for _, b in gla_sums])
    g_pool_out = matmul_tn(y, dh1_bf, "grad_pool_out")
    g_pool_in = matmul_tn(n0, dpool, "grad_pool_in", by_column_tile=True)

    pool_names = ("pool_in", "group", "pool_out")
    pool_sums = chip_sums(
        [g_pool_in, _quarters_from_groups(g_group_w), g_pool_out.reshape(N_CHIPS, D // N_CHIPS, D)],
        pool_names, "pool")
    pool_got = scatter_to_owners([b for _, b in pool_sums], "scatter_to_owners_pool")
    reduced = join_halves([add_parts(f, g, place, "add_parts_" + n) for (f, _), g, n in
                           zip(pool_sums + gla_sums, list(pool_got) + list(gla_got), pool_names + gla_names)])
    return dx, reduced, g_gk_pad, small_top, small_gla, small_pool


def kernel(x, norm_w, pool_in_w, pool_group_w, pool_group_b, pool_scale, pool_out_w, gla_in_w, gla_gk_w, gla_gk_b, gla_head_norm_w, gla_out_w, final_norm_w, loss_target, m_norm_w, m_pool_in_w, m_pool_group_w, m_pool_group_b, m_pool_scale, m_pool_out_w, m_gla_in_w, m_gla_gk_w, m_gla_gk_b, m_gla_head_norm_w, m_gla_out_w, m_final_norm_w, v_norm_w, v_pool_in_w, v_pool_group_w, v_pool_group_b, v_pool_scale, v_pool_out_w, v_gla_in_w, v_gla_gk_w, v_gla_gk_b, v_gla_head_norm_w, v_gla_out_w, v_final_norm_w):
    s = x.shape[1]
    xs = x[0]
    target = loss_target[0]
    q_chip = 2 * lax.axis_index("x") + lax.axis_index("y")
    place = jnp.stack([lax.axis_index("c"), q_chip]).astype(jnp.int32)

    wpi, gw_q, wpo_q, wgi_q, wgo_q = allgather_weights(
        [pool_in_w[0], pool_group_w[0].reshape(GROUP_DIM, GROUP_DIM), pool_out_w[0], gla_in_w[0], gla_out_w[0]],
        exchange=(True, True, True, False, False))
    gw = _groups_from_quarters(gw_q)
    wpo = wpo_q.reshape(D, D)

    small_in = jnp.concatenate([
        _pad_row(gla_gk_b[0], gla_head_norm_w[0], pool_group_b[0]),
        gla_gk_w[0].reshape(2, D),
        jnp.zeros((5, D), F32)], axis=0)
    small_all = gather_small(small_in, "gather_small_weights")[0::2]
    bgk = small_all[:, 0, 0:128].reshape(1, KEY_W)
    hw = small_all[:, 0, 128:192].reshape(1, HEAD_V)
    gb = jnp.concatenate([small_all[q, 0, 192:448].reshape(GROUPS, 64) for q in range(N_CHIPS)],
                         axis=1).reshape(1, D)
    wgk16 = jnp.concatenate([small_all[q, 1:3].reshape(GATE_RANK, 128) for q in range(N_CHIPS)], axis=1)
    wgk = _bf(jnp.pad(wgk16, ((0, RANK_PAD - GATE_RANK), (0, 0))))
    hw_tiled = jnp.tile(hw, (1, HEADS))

    w0 = norm_w[0:1]
    w1 = norm_w[1:2]
    wf = final_norm_w.reshape(1, D)

    dx, reduced, g_gk_pad, small_top, small_gla, small_pool = local_gradients(
        xs, target, w0, w1, wf, wpi, gw, gb, pool_scale, wpo, [wgi_q, wgo_q], wgk, bgk, hw_tiled, place)
    r_pool_in, r_group_w, r_pool_out, r_gla_in, r_gla_out = reduced
    r_group_w = r_group_w.reshape(GROUPS, 64, GROUP_DIM)

    small_out = jnp.concatenate([
        small_pool[0:1], small_gla[0:1],
        small_pool[1:2],
        small_top[0:1],
        small_top[2:3],
        _pad_row(small_gla[1, 0:KEY_W], small_top[1, 0:HEAD_V]),
        small_pool[2:3],
        g_gk_pad[:GATE_RANK].reshape(8, D),
        jnp.zeros((1, D), F32)], axis=0)
    total = gather_small(small_out, "allreduce_small_grads", reduce=True)
    loss = total[4, 0]
    g_norm = total[0:2]
    g_scale = total[2:3]
    g_final = total[3]
    pick = lambda full, width: lax.dynamic_slice_in_dim(full, q_chip * width, width, axis=-1)
    g_gk_b = pick(total[5:6, 0:KEY_W], 128)
    g_hnw = pick(total[5:6, KEY_W:KEY_W + HEAD_V], 64)
    g_group_b = pick(total[6].reshape(GROUPS, GROUP_DIM), 64)[None]
    g_gk_w = pick(total[7:15].reshape(GATE_RANK, KEY_W), 128)[None]

    def step_lane_rows(name, w, g, m, v):
        turn = lambda a: jnp.transpose(a, (2, 0, 1))
        back = lambda a: jnp.transpose(a, (1, 2, 0))
        g_t = turn(g)
        d, nm, nv = adamw_rows(turn(w), g_t, turn(m), turn(v), "adamw_" + name)
        return back(g_t), back(d), back(nm), back(nv)

    def step(name, w, g, m, v):
        shape = w.shape
        as2d = lambda a: a.reshape(-1, shape[-1])
        d, nm, nv = adamw(as2d(w), as2d(g), as2d(m), as2d(v), "adamw_" + name)
        return g.reshape(shape), d.reshape(shape), nm.reshape(shape), nv.reshape(shape)

    results = [
        step("norm_w", norm_w, g_norm, m_norm_w, v_norm_w),
        step("pool_in_w", pool_in_w, r_pool_in[None], m_pool_in_w, v_pool_in_w),
        step("pool_group_w", pool_group_w, r_group_w[None], m_pool_group_w, v_pool_group_w),
        step("pool_group_b", pool_group_b, g_group_b, m_pool_group_b, v_pool_group_b),
        step("pool_scale", pool_scale, g_scale, m_pool_scale, v_pool_scale),
        step("pool_out_w", pool_out_w, r_pool_out[None], m_pool_out_w, v_pool_out_w),
        step_lane_rows("gla_in_w", gla_in_w, r_gla_in[None], m_gla_in_w, v_gla_in_w),
        step("gla_gk_w", gla_gk_w, g_gk_w, m_gla_gk_w, v_gla_gk_w),
        step("gla_gk_b", gla_gk_b, g_gk_b, m_gla_gk_b, v_gla_gk_b),
        step("gla_head_norm_w", gla_head_norm_w, g_hnw, m_gla_head_norm_w, v_gla_head_norm_w),
        step("gla_out_w", gla_out_w, r_gla_out[None], m_gla_out_w, v_gla_out_w),
        step("final_norm_w", final_norm_w, g_final, m_final_norm_w, v_final_norm_w),
    ]
    grads, deltas, new_m, new_v = zip(*results)
    return (loss, dx[None], *grads, *deltas, *new_m, *new_v)
```

```python
import functools

import jax
import jax.numpy as jnp
from jax import lax
from jax.experimental import pallas as pl
from jax.experimental.pallas import tpu as pltpu

F32 = jnp.float32
BF16 = jnp.bfloat16
MESH = pl.DeviceIdType.MESH

D = 1024
POOL_WINDOWS = (2, 4, 8, 16)
GROUPS = 4
GROUP_DIM = 256
HEADS = 4
HEAD_K = 128
HEAD_V = 256
KEY_W = 512
CHUNK = 64
GATE_RANK = 16
GATE_NORM = 16.0
GLA_IN = 3088
GLA_MAIN = 3072
RANK_PAD = 128
EPS = 1e-6
HALO = 16

ADAM_LR = 0.001
ADAM_B1 = 0.9
ADAM_B2 = 0.999
ADAM_EPS = 1e-08
ADAM_WD = 0.01
ADAM_STEP = 10

N_CHIPS = 4
N_DEV = 8
GLA_IN_QUARTER = GLA_IN // N_CHIPS

VMEM_LIMIT = 56 * 1024 * 1024


def _nn(a, b):
    return lax.dot_general(a, b, (((1,), (0,)), ((), ())), preferred_element_type=F32)


def _nt(a, b):
    return lax.dot_general(a, b, (((1,), (1,)), ((), ())), preferred_element_type=F32)


def _tn(a, b):
    return lax.dot_general(a, b, (((0,), (0,)), ((), ())), preferred_element_type=F32)


def _nn_exact(a, b):
    return lax.dot_general(a, b, (((1,), (0,)), ((), ())), preferred_element_type=F32,
                           precision=lax.Precision.HIGHEST)


def _bf(a):
    return a.astype(BF16)


def _params(*sem):
    return pltpu.CompilerParams(dimension_semantics=sem, vmem_limit_bytes=VMEM_LIMIT)


def _full(shape):
    return pl.BlockSpec(shape, lambda i: (0,) * len(shape))


def _position():
    return lax.axis_index("x"), lax.axis_index("y"), lax.axis_index("c")


def _gather_small(in_ref, all_ref, send_sems, recv_sems, local_sem):
    x, y, c = _position()
    me = 4 * x + 2 * y + c
    mine = pltpu.make_async_copy(in_ref, all_ref.at[me], local_sem)
    mine.start()
    sends = []
    for k in range(N_DEV - 1):
        fx, fy, fc = (k + 1) >> 2 & 1, (k + 1) >> 1 & 1, (k + 1) & 1
        cp = pltpu.make_async_remote_copy(
            src_ref=in_ref, dst_ref=all_ref.at[me],
            send_sem=send_sems.at[k], recv_sem=recv_sems.at[k],
            device_id=(x ^ fx, y ^ fy, c ^ fc), device_id_type=MESH)
        cp.start()
        sends.append(cp)
    for k in range(N_DEV - 1):
        fx, fy, fc = (k + 1) >> 2 & 1, (k + 1) >> 1 & 1, (k + 1) & 1
        src_dev = 4 * (x ^ fx) + 2 * (y ^ fy) + (c ^ fc)
        pltpu.make_async_remote_copy(
            src_ref=in_ref, dst_ref=all_ref.at[src_dev],
            send_sem=send_sems.at[k], recv_sem=recv_sems.at[k],
            device_id=(x, y, c), device_id_type=MESH).wait_recv()
    for cp in sends:
        cp.wait_send()
    mine.wait()


SMALL_SEMS = [pltpu.SemaphoreType.DMA((N_DEV - 1,)), pltpu.SemaphoreType.DMA((N_DEV - 1,)),
              pltpu.SemaphoreType.DMA]
VMEM_SPEC = pl.BlockSpec(memory_space=pltpu.VMEM)


def _other_chips(x, y):
    return [(1 - x, y), (x, 1 - y), (1 - x, 1 - y)]


def _any_specs(n):
    return [pl.BlockSpec(memory_space=pl.ANY)] * n


def _halves(rows, c):
    half = rows // 2
    return pl.ds(c * half, half), pl.ds((1 - c) * half, half)


CAST_ROWS = 256


def _gather_copy(out_ref, send_sems, recv_sems, k, quarter, half, to, src=None):
    dst = out_ref.at[quarter, half]
    return pltpu.make_async_remote_copy(
        src_ref=dst if src is None else src, dst_ref=dst,
        send_sem=send_sems.at[k], recv_sem=recv_sems.at[k], device_id=to, device_id_type=MESH)


def allgather_weights(quarters, exchange, small):
    n = len(quarters)
    shapes = [w.shape for w in quarters]
    moved = [i for i in range(n) if exchange[i]]

    def body(*refs):
        w_refs, small_ref = refs[:n], refs[n]
        out_refs, small_all_ref = refs[n + 1:2 * n + 1], refs[2 * n + 1]
        refs = refs[2 * n + 2:]
        f32_bufs, bf_bufs = refs[:n], refs[n:2 * n]
        send_sems, recv_sems, local_sems = refs[2 * n:2 * n + 3]
        x, y, c = _position()
        q = 2 * x + y
        sibling = (x, y, 1 - c)
        chips = _other_chips(x, y)

        def copy(k, i, quarter, half, to, src=None):
            return _gather_copy(out_refs[i], send_sems, recv_sems, k * n + i, quarter, half, to, src)

        loads = [pltpu.make_async_copy(w_refs[i], f32_bufs[i], local_sems.at[i]) for i in range(n)]
        for cp in loads:
            cp.start()
        keeps, sends = [], []
        for i in range(n):
            loads[i].wait()
            for r0 in range(0, shapes[i][0], CAST_ROWS):
                bf_bufs[i][r0:r0 + CAST_ROWS, :] = _bf(f32_bufs[i][r0:r0 + CAST_ROWS, :])
            keep = pltpu.make_async_copy(bf_bufs[i], out_refs[i].at[q], local_sems.at[n + i])
            keep.start()
            keeps.append(keep)
            if not exchange[i]:
                continue
            mine, _ = _halves(shapes[i][0], c)
            for j, chip in enumerate(chips):
                cp = copy(j, i, q, mine, (*chip, c), src=bf_bufs[i].at[mine])
                cp.start()
                sends.append(cp)
        for j, chip in enumerate(chips):
            qj = 2 * chip[0] + chip[1]
            for i in moved:
                mine, _ = _halves(shapes[i][0], c)
                copy(j, i, qj, mine, (x, y, c)).wait_recv()
                cp = copy(3 + j, i, qj, mine, sibling)
                cp.start()
                sends.append(cp)
        for j, chip in enumerate(chips):
            qj = 2 * chip[0] + chip[1]
            for i in moved:
                _, other = _halves(shapes[i][0], c)
                copy(3 + j, i, qj, other, (x, y, c)).wait_recv()
        _gather_small(small_ref, small_all_ref, *refs[2 * n + 3:])
        for cp in sends:
            cp.wait_send()
        for cp in keeps:
            cp.wait()

    outs = pl.pallas_call(
        body, name="allgather_weights",
        out_shape=[jax.ShapeDtypeStruct((N_CHIPS, *s), BF16) for s in shapes]
                  + [jax.ShapeDtypeStruct((N_DEV, *small.shape), small.dtype)],
        in_specs=_any_specs(n) + [VMEM_SPEC], out_specs=_any_specs(n) + [VMEM_SPEC],
        scratch_shapes=([pltpu.VMEM(s, F32) for s in shapes] + [pltpu.VMEM(s, BF16) for s in shapes]
                        + [pltpu.SemaphoreType.DMA((6 * n,)), pltpu.SemaphoreType.DMA((6 * n,)),
                           pltpu.SemaphoreType.DMA((2 * n,))] + SMALL_SEMS),
        compiler_params=pltpu.CompilerParams(vmem_limit_bytes=VMEM_LIMIT),
    )(*quarters, small)
    return outs[:n], outs[n]


def exchange_with_sibling(grads, name):
    n = len(grads)

    def body(*refs):
        g_refs, theirs_refs = refs[:n], refs[n:2 * n]
        send_sems, recv_sems = refs[2 * n:]
        x, y, c = _position()
        copies = []
        for i in range(n):
            _, other = _halves(g_refs[i].shape[1], c)
            cp = pltpu.make_async_remote_copy(
                src_ref=g_refs[i].at[:, other], dst_ref=theirs_refs[i],
                send_sem=send_sems.at[i], recv_sem=recv_sems.at[i],
                device_id=(x, y, 1 - c), device_id_type=MESH)
            cp.start()
            copies.append(cp)
        for cp in copies:
            cp.wait()

    return pl.pallas_call(
        body, name=name,
        out_shape=[jax.ShapeDtypeStruct((N_CHIPS, g.shape[1] // 2, g.shape[2]), F32) for g in grads],
        in_specs=_any_specs(n), out_specs=_any_specs(n),
        scratch_shapes=[pltpu.SemaphoreType.DMA((n,)), pltpu.SemaphoreType.DMA((n,))],
    )(*grads)


def _scatter_copies(b_refs, got_refs, send_sems, recv_sems):
    n = len(b_refs)
    x, y, c = _position()
    copies = []
    for j, chip in enumerate(_other_chips(x, y)):
        qj = 2 * chip[0] + chip[1]
        for i in range(n):
            copies.append(pltpu.make_async_remote_copy(
                src_ref=b_refs[i].at[qj], dst_ref=got_refs[i].at[j],
                send_sem=send_sems.at[j * n + i], recv_sem=recv_sems.at[j * n + i],
                device_id=(*chip, c), device_id_type=MESH))
    return copies


def _scatter_shapes(chip_sums):
    return [jax.ShapeDtypeStruct((N_CHIPS - 1, *b.shape[1:]), BF16) for b in chip_sums]


def scatter_to_owners(chip_sums, name):
    n = len(chip_sums)

    def body(*refs):
        copies = _scatter_copies(refs[:n], refs[n:2 * n], *refs[2 * n:])
        for cp in copies:
            cp.start()
        for cp in copies:
            cp.wait()

    return pl.pallas_call(
        body, name=name,
        out_shape=_scatter_shapes(chip_sums),
        in_specs=_any_specs(n), out_specs=_any_specs(n),
        scratch_shapes=[pltpu.SemaphoreType.DMA((3 * n,)), pltpu.SemaphoreType.DMA((3 * n,))],
    )(*chip_sums)


def join_halves(reduced, small):
    n = len(reduced)

    def body(*refs):
        small_ref = refs[n]
        buf_refs, total_ref = refs[n + 1:2 * n + 1], refs[2 * n + 1]
        send_sems, recv_sems, all_ref = refs[2 * n + 2:2 * n + 5]
        x, y, c = _position()
        copies = []
        for i in range(n):
            mine, _ = _halves(buf_refs[i].shape[0], c)
            cp = pltpu.make_async_remote_copy(
                src_ref=buf_refs[i].at[mine], dst_ref=buf_refs[i].at[mine],
                send_sem=send_sems.at[i], recv_sem=recv_sems.at[i],
                device_id=(x, y, 1 - c), device_id_type=MESH)
            cp.start()
            copies.append(cp)
        _gather_small(small_ref, all_ref, *refs[2 * n + 5:])
        total = all_ref[0]
        for dev in range(1, N_DEV):
            total = total + all_ref[dev]
        total_ref[...] = total
        for cp in copies:
            cp.wait()

    outs = pl.pallas_call(
        body, name="join_halves",
        out_shape=[jax.ShapeDtypeStruct(r.shape, F32) for r in reduced]
                  + [jax.ShapeDtypeStruct(small.shape, small.dtype)],
        in_specs=_any_specs(n) + [VMEM_SPEC], out_specs=_any_specs(n) + [VMEM_SPEC],
        input_output_aliases={i: i for i in range(n)},
        scratch_shapes=[pltpu.SemaphoreType.DMA((n,)), pltpu.SemaphoreType.DMA((n,)),
                        pltpu.VMEM((N_DEV, *small.shape), small.dtype)] + SMALL_SEMS,
    )(*reduced, small)
    return outs[:n], outs[n]


ADD_ROWS = 256


def add_halves(grad, theirs, place, name):
    _, half, cols = theirs.shape
    rb = min(ADD_ROWS, half)
    steps = half // rb

    def body(place_ref, a_ref, b_ref, f_ref, h_ref):
        s = a_ref[...] + b_ref[...]
        f_ref[...] = s
        h_ref[...] = _bf(s)

    spec = pl.BlockSpec((1, rb, cols), lambda i, j, place: (i, j, 0))
    return pl.pallas_call(
        body, name=name,
        grid_spec=pltpu.PrefetchScalarGridSpec(
            num_scalar_prefetch=1, grid=(N_CHIPS, steps),
            in_specs=[pl.BlockSpec((1, rb, cols), lambda i, j, place: (i, place[0] * steps + j, 0)), spec],
            out_specs=(spec, spec)),
        out_shape=(jax.ShapeDtypeStruct(theirs.shape, F32), jax.ShapeDtypeStruct(theirs.shape, BF16)),
        compiler_params=_params("parallel", "parallel"),
    )(place, grad, theirs)


def add_parts(chip_sum, got, place, name):
    _, half, cols = got.shape
    rb = min(ADD_ROWS, half)
    steps = half // rb

    def body(place_ref, o_ref, g_ref, out_ref):
        s = o_ref[0]
        for j in range(N_CHIPS - 1):
            s = s + g_ref[j].astype(F32)
        out_ref[...] = s

    return pl.pallas_call(
        body, name=name,
        grid_spec=pltpu.PrefetchScalarGridSpec(
            num_scalar_prefetch=1, grid=(steps,),
            in_specs=[pl.BlockSpec((1, rb, cols), lambda j, place: (place[1], j, 0)),
                      pl.BlockSpec((N_CHIPS - 1, rb, cols), lambda j, place: (0, j, 0))],
            out_specs=pl.BlockSpec((rb, cols), lambda j, place: (place[0] * steps + j, 0))),
        out_shape=jax.ShapeDtypeStruct((2 * half, cols), F32),
        compiler_params=_params("parallel"),
    )(place, chip_sum, got)


def _adam_math(w, g, m, v):
    m = ADAM_B1 * m + (1.0 - ADAM_B1) * g
    v = ADAM_B2 * v + (1.0 - ADAM_B2) * (g * g)
    m_hat = m / (1.0 - ADAM_B1 ** ADAM_STEP)
    v_hat = v / (1.0 - ADAM_B2 ** ADAM_STEP)
    delta = -ADAM_LR * (m_hat / (jnp.sqrt(v_hat) + ADAM_EPS) + ADAM_WD * w)
    return delta, m, v


def adamw(w, g, m, v, name):
    rows, cols = w.shape
    fits = [t for t in range(8, rows, 8) if rows % t == 0 and t * cols * 4 <= 2 ** 20]
    tile = max(fits) if fits else rows

    def body(w_ref, g_ref, m_ref, v_ref, d_ref, nm_ref, nv_ref):
        d, nm, nv = _adam_math(w_ref[...], g_ref[...], m_ref[...], v_ref[...])
        d_ref[...] = d
        nm_ref[...] = nm
        nv_ref[...] = nv

    spec = pl.BlockSpec((tile, cols), lambda i: (i, 0))
    shape = jax.ShapeDtypeStruct((rows, cols), F32)
    return pl.pallas_call(
        body, name=name, grid=(rows // tile,),
        out_shape=(shape, shape, shape),
        in_specs=[spec] * 4, out_specs=(spec, spec, spec),
        compiler_params=_params("parallel"),
    )(w, g, m, v)


def adamw_small(params):
    n = len(params)

    def body(*refs):
        ins, outs = refs[:4 * n], refs[4 * n:]
        for k in range(n):
            w_ref, g_ref, m_ref, v_ref = ins[4 * k:4 * k + 4]
            d, nm, nv = _adam_math(w_ref[...], g_ref[...], m_ref[...], v_ref[...])
            outs[3 * k][...] = d
            outs[3 * k + 1][...] = nm
            outs[3 * k + 2][...] = nv

    flat = [a for p in params for a in p]
    outs = pl.pallas_call(
        body, name="adamw_small",
        out_shape=[jax.ShapeDtypeStruct(p[0].shape, F32) for p in params for _ in range(3)],
        in_specs=[VMEM_SPEC] * (4 * n), out_specs=[VMEM_SPEC] * (3 * n),
    )(*flat)
    return [tuple(outs[3 * k:3 * k + 3]) for k in range(n)]


def adamw_rows(w, g, m, v, name):
    rows, _, cols = w.shape
    tile = rows // 4

    def body(w_ref, g_ref, m_ref, v_ref, d_ref, nm_ref, nv_ref):
        d, nm, nv = _adam_math(w_ref[...], g_ref[...], m_ref[...], v_ref[...])
        d_ref[...] = d
        nm_ref[...] = nm
        nv_ref[...] = nv

    spec = pl.BlockSpec((tile, 1, cols), lambda i: (i, 0, 0))
    shape = jax.ShapeDtypeStruct(w.shape, F32)
    return pl.pallas_call(
        body, name=name, grid=(rows // tile,),
        out_shape=(shape, shape, shape),
        in_specs=[spec] * 4, out_specs=(spec, spec, spec),
        compiler_params=_params("parallel"),
    )(w, g, m, v)


def matmul_tn(a, b, name, tile_n=512, tile_s=2048, by_column_tile=False):
    s, m = a.shape
    n = b.shape[1]
    tile_n = min(tile_n, n)
    tile_s = min(tile_s, s)
    steps = s // tile_s
    if by_column_tile:
        out_shape = jax.ShapeDtypeStruct((n // tile_n, m, tile_n), F32)
        out_spec = pl.BlockSpec((None, m, tile_n), lambda j, k: (j, 0, 0))
    else:
        out_shape = jax.ShapeDtypeStruct((m, n), F32)
        out_spec = pl.BlockSpec((m, tile_n), lambda j, k: (0, j))

    def body(a_ref, b_ref, out_ref):
        k = pl.program_id(1)

        @pl.when(k == 0)
        def _():
            out_ref[...] = jnp.zeros_like(out_ref)

        out_ref[...] += _tn(a_ref[...], b_ref[...])

    return pl.pallas_call(
        body, name=name, grid=(n // tile_n, steps),
        out_shape=out_shape,
        in_specs=[pl.BlockSpec((tile_s, m), lambda j, k: (k, 0)),
                  pl.BlockSpec((tile_s, tile_n), lambda j, k: (k, j))],
        out_specs=out_spec,
        compiler_params=_params("parallel", "arbitrary"),
    )(a, b)


ROW_TILE = 512


def _row_index(tile, rows):
    return tile * rows + lax.broadcasted_iota(jnp.int32, (rows, 1), 0)


def _inverse_counts(t_glob):
    return [1.0 / jnp.minimum(t_glob + 1, w).astype(F32) for w in POOL_WINDOWS]


def _sigmoid(z):
    return 1.0 / (1.0 + jnp.exp(-z))


def gather_in_background(step, last, out_refs, send_sems, recv_sems):
    n = len(out_refs)
    x, y, c = _position()
    q = 2 * x + y
    chips = _other_chips(x, y)

    def copy(k, i, quarter, half, to):
        return _gather_copy(out_refs[i], send_sems, recv_sems, k * n + i, quarter, half, to)

    @pl.when(step == 0)
    def _():
        for i in range(n):
            mine, _ = _halves(out_refs[i].shape[1], c)
            for j, chip in enumerate(chips):
                copy(j, i, q, mine, (*chip, c)).start()

    @pl.when(step == last - 1)
    def _():
        for j, chip in enumerate(chips):
            qj = 2 * chip[0] + chip[1]
            for i in range(n):
                mine, _ = _halves(out_refs[i].shape[1], c)
                copy(j, i, qj, mine, (x, y, c)).wait_recv()
                copy(3 + j, i, qj, mine, (x, y, 1 - c)).start()

    @pl.when(step == last)
    def _():
        for j, chip in enumerate(chips):
            qj = 2 * chip[0] + chip[1]
            for i in range(n):
                mine, other = _halves(out_refs[i].shape[1], c)
                copy(3 + j, i, qj, other, (x, y, c)).wait_recv()
                copy(j, i, q, mine, (x, y, c)).wait_send()
                copy(3 + j, i, qj, mine, (x, y, c)).wait_send()


def pool_forward(x, w0, wpi, gw, gb, scale, wpo, later):
    s = x.shape[0]
    ts = ROW_TILE
    nt = s // ts
    assert nt >= 2
    n_later = len(later)

    def body(x_ref, w0_ref, wpi_ref, gw_ref, gb_ref, sc_ref, wpo_ref, *rest):
        rest = rest[n_later:]
        h1_ref, pooled_ref, gt_ref, n0_ref = rest[:4]
        later_refs = rest[4:4 + n_later]
        ubuf, send_sems, recv_sems = rest[4 + n_later:]
        i = pl.program_id(0)
        gather_in_background(i, nt - 1, later_refs, send_sems, recv_sems)
        xv = x_ref[...]
        r = lax.rsqrt(jnp.mean(xv * xv, axis=-1, keepdims=True) + EPS)
        n0 = _bf(xv * r * w0_ref[...])
        n0_ref[...] = n0
        u = jnp.concatenate([_nn(n0, wpi_ref[0]), _nn(n0, wpi_ref[1])], axis=-1)
        gt = jnp.concatenate([_nn(n0, wpi_ref[2]), _nn(n0, wpi_ref[3])], axis=-1)
        gt_ref[...] = gt

        @pl.when(i == 0)
        def _():
            ubuf[0:HALO, :] = jnp.zeros((HALO, D), F32)

        ubuf[HALO:HALO + ts, :] = u
        inv = _inverse_counts(_row_index(i, ts))
        mixed = []
        for g, w in enumerate(POOL_WINDOWS):
            cols = slice(g * GROUP_DIM, (g + 1) * GROUP_DIM)
            ug = u[:, cols]
            acc = ug
            for j in range(1, w):
                acc = acc + ubuf[HALO - j:HALO - j + ts, cols]
            pooled = _bf(acc * inv[g] - ug)
            pooled_ref[:, cols] = pooled
            mixed.append(_nn(pooled, gw_ref[g]))
        ubuf[0:HALO, :] = ubuf[ts:ts + HALO, :]
        mixed = jnp.concatenate(mixed, axis=-1) + gb_ref[...]
        y = mixed * sc_ref[...] * (gt * _sigmoid(gt))
        h1_ref[...] = xv + _nn(_bf(y), wpo_ref[...])

    row = lambda cols: pl.BlockSpec((ts, cols), lambda i: (i, 0))
    outs = pl.pallas_call(
        body, name="pool_forward", grid=(nt,),
        out_shape=[jax.ShapeDtypeStruct((s, D), F32), jax.ShapeDtypeStruct((s, D), BF16),
                   jax.ShapeDtypeStruct((s, D), F32), jax.ShapeDtypeStruct((s, D), BF16)]
                  + [jax.ShapeDtypeStruct(a.shape, a.dtype) for a in later],
        in_specs=[row(D), _full((1, D)), _full((N_CHIPS, D, D // 2)), _full((GROUPS, GROUP_DIM, GROUP_DIM)),
                  _full((1, D)), _full((1, D)), _full((D, D))] + _any_specs(n_later),
        out_specs=[row(D), row(D), row(D), row(D)] + _any_specs(n_later),
        input_output_aliases={7 + k: 4 + k for k in range(n_later)},
        scratch_shapes=[pltpu.VMEM((HALO + ts, D), F32),
                        pltpu.SemaphoreType.DMA((6 * n_later,)), pltpu.SemaphoreType.DMA((6 * n_later,))],
        compiler_params=_params("arbitrary"),
    )(x, w0, wpi, gw, gb, scale, wpo, *later)
    return outs[:4], outs[4:]


def pool_backward(x, dh1, pooled, gt, w0, wpi, gw, gb, scale, wpo, chip_sums):
    s = x.shape[0]
    ts = ROW_TILE
    nt = s // ts
    n_sums = len(chip_sums)

    def body(x_ref, dh1_ref, pooled_ref, gt_ref, w0_ref, wpi_ref, gw_ref, gb_ref, sc_ref, wpo_ref, *rest):
        sum_refs, rest = rest[:n_sums], rest[n_sums:]
        dx_ref, dproj_ref, gpo_ref, ggw_ref, small_ref = rest[:5]
        got_refs = rest[5:5 + n_sums]
        ebuf, send_sems, recv_sems = rest[5 + n_sums:]
        i = pl.program_id(0)
        copies = _scatter_copies(sum_refs, got_refs, send_sems, recv_sems)

        @pl.when(i == 0)
        def _():
            for cp in copies:
                cp.start()

        @pl.when(i == nt - 1)
        def _():
            for cp in copies:
                cp.wait()

        @pl.when(i == 0)
        def _():
            gpo_ref[...] = jnp.zeros_like(gpo_ref)
            ggw_ref[...] = jnp.zeros_like(ggw_ref)
            small_ref[...] = jnp.zeros_like(small_ref)
            ebuf[ts:ts + HALO, :] = jnp.zeros((HALO, D), F32)

        dh1 = dh1_ref[...]
        dh1_bf = _bf(dh1)
        gt = gt_ref[...]
        sc = sc_ref[...]
        dy = _nt(dh1_bf, wpo_ref[...])
        pooled_bf = []
        mixed = []
        for g in range(GROUPS):
            cols = slice(g * GROUP_DIM, (g + 1) * GROUP_DIM)
            pb = pooled_ref[:, cols]
            pooled_bf.append(pb)
            mixed.append(_nn(pb, gw_ref[g]))
        mixed = jnp.concatenate(mixed, axis=-1) + gb_ref[...]
        sg = _sigmoid(gt)
        silu = gt * sg
        gpo_ref[...] += _tn(_bf(mixed * sc * silu), dh1_bf)
        dmixed = dy * sc * silu
        dgt = dy * mixed * sc * (sg * (1.0 + gt * (1.0 - sg)))
        dproj_ref[:, D:] = _bf(dgt)
        small_ref[1:2, :] += jnp.sum(dy * mixed * silu, axis=0, keepdims=True)
        small_ref[2:3, :] += jnp.sum(dmixed, axis=0, keepdims=True)

        inv = _inverse_counts(_row_index(nt - 1 - i, ts))
        dpooled = []
        for g in range(GROUPS):
            cols = slice(g * GROUP_DIM, (g + 1) * GROUP_DIM)
            dm = _bf(dmixed[:, cols])
            ggw_ref[g] += _tn(pooled_bf[g], dm)
            dp = _nt(dm, gw_ref[g])
            dpooled.append(dp)
            ebuf[0:ts, cols] = dp * inv[g]
        du = []
        for g, w in enumerate(POOL_WINDOWS):
            cols = slice(g * GROUP_DIM, (g + 1) * GROUP_DIM)
            acc = -dpooled[g]
            for j in range(w):
                acc = acc + ebuf[j:j + ts, cols]
            du.append(acc)
        ebuf[ts:ts + HALO, :] = ebuf[0:HALO, :]
        du = _bf(jnp.concatenate(du, axis=-1))
        dproj_ref[:, :D] = du
        dgt_bf = _bf(dgt)
        half = D // 2
        dn0 = (_nt(du[:, :half], wpi_ref[0]) + _nt(du[:, half:], wpi_ref[1])
               + _nt(dgt_bf[:, :half], wpi_ref[2]) + _nt(dgt_bf[:, half:], wpi_ref[3]))

        xv = x_ref[...]
        r = lax.rsqrt(jnp.mean(xv * xv, axis=-1, keepdims=True) + EPS)
        xhat = xv * r
        small_ref[0:1, :] += jnp.sum(dn0 * xhat, axis=0, keepdims=True)
        dxh = dn0 * w0_ref[...]
        dx_ref[...] = dh1 + r * (dxh - xhat * jnp.mean(dxh * xhat, axis=-1, keepdims=True))

    row = lambda cols: pl.BlockSpec((ts, cols), lambda i: (nt - 1 - i, 0))
    outs = pl.pallas_call(
        body, name="pool_backward", grid=(nt,),
        out_shape=[jax.ShapeDtypeStruct((s, D), F32), jax.ShapeDtypeStruct((s, 2 * D), BF16),
                   jax.ShapeDtypeStruct((D, D), F32),
                   jax.ShapeDtypeStruct((GROUPS, GROUP_DIM, GROUP_DIM), F32),
                   jax.ShapeDtypeStruct((8, D), F32)] + _scatter_shapes(chip_sums),
        in_specs=[row(D), row(D), row(D), row(D), _full((1, D)), _full((N_CHIPS, D, D // 2)),
                  _full((GROUPS, GROUP_DIM, GROUP_DIM)), _full((1, D)), _full((1, D)), _full((D, D))]
                 + _any_specs(n_sums),
        out_specs=[row(D), row(2 * D), _full((D, D)), _full((GROUPS, GROUP_DIM, GROUP_DIM)), _full((8, D))]
                  + _any_specs(n_sums),
        scratch_shapes=[pltpu.VMEM((ts + HALO, D), F32),
                        pltpu.SemaphoreType.DMA((3 * n_sums,)), pltpu.SemaphoreType.DMA((3 * n_sums,))],
        compiler_params=_params("arbitrary"),
    )(x, dh1, pooled, gt, w0, wpi, gw, gb, scale, wpo, *chip_sums)
    return outs[:5], outs[5:]


def gla_project(h1, w1, wgi, wlow, wgk, bgk):
    s = h1.shape[0]
    ts = ROW_TILE

    def body(h_ref, w1_ref, wgi_ref, wlow_ref, wgk_ref, bgk_ref, qk_ref, v_ref, gate_ref, low_ref, cum_ref,
             n1_ref):
        hv = h_ref[...]
        r = lax.rsqrt(jnp.mean(hv * hv, axis=-1, keepdims=True) + EPS)
        n1 = _bf(hv * r * w1_ref[...])
        n1_ref[...] = n1
        qk_ref[...] = _nn(n1, wgi_ref[:, 0:2 * KEY_W])
        v_ref[...] = _bf(_nn(n1, wgi_ref[:, 2 * KEY_W:2 * KEY_W + D]))
        gate_ref[...] = _nn(n1, wgi_ref[:, 2 * KEY_W + D:GLA_MAIN])
        low = _bf(_nn(n1, wlow_ref[...]))
        low_ref[...] = low
        z = _nn(low, wgk_ref[...]) + bgk_ref[...]
        lg = (jnp.minimum(z, 0.0) - jnp.log(1.0 + jnp.exp(-jnp.abs(z)))) / GATE_NORM
        lower_f = _chunk_masks()[0].astype(F32)
        for r0 in range(0, ts, CHUNK):
            cum_ref[r0:r0 + CHUNK, :] = _nn_exact(lower_f, lg[r0:r0 + CHUNK, :])

    row = lambda cols: pl.BlockSpec((ts, cols), lambda i: (i, 0))
    return pl.pallas_call(
        body, name="gla_project", grid=(s // ts,),
        out_shape=(jax.ShapeDtypeStruct((s, D), F32), jax.ShapeDtypeStruct((s, D), BF16),
                   jax.ShapeDtypeStruct((s, D), F32), jax.ShapeDtypeStruct((s, RANK_PAD), BF16),
                   jax.ShapeDtypeStruct((s, KEY_W), F32), jax.ShapeDtypeStruct((s, D), BF16)),
        in_specs=[row(D), _full((1, D)), _full((D, GLA_MAIN)), _full((D, RANK_PAD)),
                  _full((RANK_PAD, KEY_W)), _full((1, KEY_W))],
        out_specs=(row(D), row(D), row(D), row(RANK_PAD), row(KEY_W), row(D)),
        compiler_params=_params("parallel"),
    )(h1, w1, wgi, wlow, wgk, bgk)


GLA_BLOCK = 512
CHUNKS_PER_BLOCK = GLA_BLOCK // CHUNK


def _chunk_masks():
    t = lax.broadcasted_iota(jnp.int32, (CHUNK, CHUNK), 0)
    u = lax.broadcasted_iota(jnp.int32, (CHUNK, CHUNK), 1)
    return t >= u, t <= u


def _gla_chunk_terms(q, cum):
    ep = jnp.exp(cum)
    en = jnp.exp(-cum)
    qs = q * (HEAD_K ** -0.5)
    last = cum[CHUNK - 1:CHUNK, :]
    ed = jnp.exp(last - cum)
    dec = jnp.exp(last)
    return ep, en, qs, ed, dec


def gla_forward(qk, v, cum):
    s = qk.shape[0]
    nb = s // GLA_BLOCK
    nc = s // CHUNK

    def body(q_ref, k_ref, v_ref, cum_ref, o_ref, st_ref, state):
        @pl.when(pl.program_id(0) == 0)
        def _():
            state[...] = jnp.zeros_like(state)

        lower, _ = _chunk_masks()

        def chunk(cc, carry):
            rows = pl.ds(pl.multiple_of(cc * CHUNK, CHUNK), CHUNK)
            for h in range(HEADS):
                kc = slice(h * HEAD_K, (h + 1) * HEAD_K)
                vc = slice(h * HEAD_V, (h + 1) * HEAD_V)
                q = q_ref[rows, kc]
                k = k_ref[rows, kc]
                v = v_ref[rows, vc]
                ep, en, qs, ed, dec = _gla_chunk_terms(q, cum_ref[rows, kc])
                a = _bf(qs * ep)
                fwd = _nt(a, _bf(k * en))
                bwd = _nt(_bf(qs * en), _bf(k * ep))
                scores = jnp.where(lower, fwd, bwd)
                st = state[h]
                st_ref[cc, h] = st
                o_ref[rows, vc] = _nn(_bf(scores), v) + _nt(a, _bf(st))
                state[h] = st * dec + _tn(v, _bf(k * ed))
            return carry

        lax.fori_loop(0, CHUNKS_PER_BLOCK, chunk, 0, unroll=4)

    return pl.pallas_call(
        body, name="gla_forward", grid=(nb,),
        out_shape=(jax.ShapeDtypeStruct((s, D), F32),
                   jax.ShapeDtypeStruct((nc, HEADS, HEAD_V, HEAD_K), F32)),
        in_specs=[pl.BlockSpec((GLA_BLOCK, KEY_W), lambda i: (i, 0)),
                  pl.BlockSpec((GLA_BLOCK, KEY_W), lambda i: (i, 1)),
                  pl.BlockSpec((GLA_BLOCK, D), lambda i: (i, 0)),
                  pl.BlockSpec((GLA_BLOCK, KEY_W), lambda i: (i, 0))],
        out_specs=(pl.BlockSpec((GLA_BLOCK, D), lambda i: (i, 0)),
                   pl.BlockSpec((CHUNKS_PER_BLOCK, HEADS, HEAD_V, HEAD_K), lambda i: (i, 0, 0, 0))),
        scratch_shapes=[pltpu.VMEM((HEADS, HEAD_V, HEAD_K), F32)],
        compiler_params=_params("arbitrary"),
    )(qk, qk, v, cum)


def gla_backward(qk, v, cum, do, states):
    s = qk.shape[0]
    nb = s // GLA_BLOCK

    def body(q_ref, k_ref, v_ref, cum_ref, do_ref, st_ref, dq_ref, dk_ref, dv_ref, dcum_ref, dstate):
        @pl.when(pl.program_id(0) == 0)
        def _():
            dstate[...] = jnp.zeros_like(dstate)

        lower, _ = _chunk_masks()
        is_last = lax.broadcasted_iota(jnp.int32, (CHUNK, HEAD_K), 0) == CHUNK - 1

        def chunk(step, carry):
            cc = CHUNKS_PER_BLOCK - 1 - step
            rows = pl.ds(pl.multiple_of(cc * CHUNK, CHUNK), CHUNK)
            for h in range(HEADS):
                kc = slice(h * HEAD_K, (h + 1) * HEAD_K)
                vc = slice(h * HEAD_V, (h + 1) * HEAD_V)
                q = q_ref[rows, kc]
                k = k_ref[rows, kc]
                v = v_ref[rows, vc]
                do_c = do_ref[rows, vc]
                ep, en, qs, ed, dec = _gla_chunk_terms(q, cum_ref[rows, kc])
                a = _bf(qs * ep)
                b = _bf(k * en)
                c = _bf(qs * en)
                dk_dec = _bf(k * ep)
                kd = _bf(k * ed)
                scores = _bf(jnp.where(lower, _nt(a, b), _nt(c, dk_dec)))
                st = st_ref[cc, h]
                dst = dstate[h]
                dst_bf = _bf(dst)

                dscores = _nt(do_c, v)
                dfwd = _bf(jnp.where(lower, dscores, 0.0))
                dbwd = _bf(jnp.where(lower, 0.0, dscores))
                dv_ref[rows, vc] = _bf(_tn(scores, do_c) + _nt(kd, dst_bf))
                da = _nn(dfwd, b) + _nn(do_c, _bf(st))
                db = _tn(dfwd, a)
                dc = _nn(dbwd, dk_dec)
                ddk = _tn(dbwd, c)
                dkd = _nn(v, dst_bf)
                ddec = jnp.sum(dst * st, axis=0, keepdims=True)
                dstate[h] = dst * dec + _tn(do_c, a)

                m = dkd * k * ed
                dq_ref[rows, kc] = _bf((da * ep + dc * en) * (HEAD_K ** -0.5))
                dk_ref[rows, kc] = _bf(db * en + ddk * ep + dkd * ed)
                dcum = (da * qs + ddk * k) * ep - (db * k + dc * qs) * en - m
                dlast = jnp.sum(m, axis=0, keepdims=True) + ddec * dec
                dcum_ref[rows, kc] = dcum + jnp.where(is_last, dlast, 0.0)
            return carry

        lax.fori_loop(0, CHUNKS_PER_BLOCK, chunk, 0, unroll=4)

    rev = lambda cols, col_block: pl.BlockSpec((GLA_BLOCK, cols), lambda i: (nb - 1 - i, col_block))
    return pl.pallas_call(
        body, name="gla_backward", grid=(nb,),
        out_shape=(jax.ShapeDtypeStruct((s, KEY_W), BF16), jax.ShapeDtypeStruct((s, KEY_W), BF16),
                   jax.ShapeDtypeStruct((s, D), BF16), jax.ShapeDtypeStruct((s, KEY_W), F32)),
        in_specs=[rev(KEY_W, 0), rev(KEY_W, 1), rev(D, 0), rev(KEY_W, 0), rev(D, 0),
                  pl.BlockSpec((CHUNKS_PER_BLOCK, HEADS, HEAD_V, HEAD_K), lambda i: (nb - 1 - i, 0, 0, 0))],
        out_specs=(rev(KEY_W, 0), rev(KEY_W, 0), rev(D, 0), rev(KEY_W, 0)),
        scratch_shapes=[pltpu.VMEM((HEADS, HEAD_V, HEAD_K), F32)],
        compiler_params=_params("arbitrary"),
    )(qk, qk, v, cum, do, states)


def head_and_loss(o, gate, h1, target, hw, wgo, wf):
    s = o.shape[0]
    ts = ROW_TILE

    def body(o_ref, gate_ref, h1_ref, tgt_ref, hw_ref, wgo_ref, wf_ref,
             dh2_ref, do_ref, dgate_ref, ggo_ref, small_ref):
        @pl.when(pl.program_id(0) == 0)
        def _():
            ggo_ref[...] = jnp.zeros_like(ggo_ref)
            small_ref[...] = jnp.zeros_like(small_ref)

        gate = gate_ref[...]
        hw = hw_ref[...]
        sg = _sigmoid(gate)
        silu = gate * sg
        ohat, ro = [], []
        for h in range(HEADS):
            oh = o_ref[:, h * HEAD_V:(h + 1) * HEAD_V]
            rh = lax.rsqrt(jnp.mean(oh * oh, axis=-1, keepdims=True) + EPS)
            ro.append(rh)
            ohat.append(oh * rh)
        ohat = jnp.concatenate(ohat, axis=-1)
        on = ohat * hw
        y2 = _bf(on * silu)
        h2 = h1_ref[...] + _nn(y2, wgo_ref[...])
        rf = lax.rsqrt(jnp.mean(h2 * h2, axis=-1, keepdims=True) + EPS)
        h2hat = h2 * rf
        wf = wf_ref[...]
        diff = h2hat * wf - tgt_ref[...]
        small_ref[2:3, :] += jnp.zeros((1, D), F32) + 0.5 * jnp.sum(diff * diff) / D
        dout = diff / D
        small_ref[0:1, :] += jnp.sum(dout * h2hat, axis=0, keepdims=True)
        dxh = dout * wf
        dh2 = rf * (dxh - h2hat * jnp.mean(dxh * h2hat, axis=-1, keepdims=True))
        dh2_ref[...] = dh2
        dh2_bf = _bf(dh2)
        ggo_ref[...] += _tn(y2, dh2_bf)
        dy2 = _nt(dh2_bf, wgo_ref[...])
        don = dy2 * silu
        dgate_ref[...] = _bf(dy2 * on * (sg * (1.0 + gate * (1.0 - sg))))
        ghw = jnp.sum(don * ohat, axis=0, keepdims=True)
        small_ref[1:2, 0:HEAD_V] += sum(ghw[:, h * HEAD_V:(h + 1) * HEAD_V] for h in range(HEADS))
        dohat = don * hw
        for h in range(HEADS):
            cols = slice(h * HEAD_V, (h + 1) * HEAD_V)
            oh, dh = ohat[:, cols], dohat[:, cols]
            do_ref[:, cols] = _bf(ro[h] * (dh - oh * jnp.mean(dh * oh, axis=-1, keepdims=True)))

    row = lambda cols: pl.BlockSpec((ts, cols), lambda i: (i, 0))
    act = jax.ShapeDtypeStruct((s, D), F32)
    act_bf = jax.ShapeDtypeStruct((s, D), BF16)
    return pl.pallas_call(
        body, name="head_and_loss", grid=(s // ts,),
        out_shape=(act, act_bf, act_bf, jax.ShapeDtypeStruct((D, D), F32), jax.ShapeDtypeStruct((8, D), F32)),
        in_specs=[row(D), row(D), row(D), row(D),
                  _full((1, D)), _full((D, D)), _full((1, D))],
        out_specs=(row(D), row(D), row(D), _full((D, D)), _full((8, D))),
        compiler_params=_params("arbitrary"),
    )(o, gate, h1, target, hw, wgo, wf)


def gla_project_backward(dq, dk, dv, dgate, dcum, low, h1, dh2, w1, wgi, wlow, wgk, bgk):
    s = h1.shape[0]
    ts = ROW_TILE

    def body(dq_ref, dk_ref, dv_ref, dgate_ref, dcum_ref, low_ref, h1_ref, dh2_ref, w1_ref,
             wgi_ref, wlow_ref, wgk_ref, bgk_ref, dh1_ref, dproj_ref, dlow_ref, ggk_ref, small_ref):
        @pl.when(pl.program_id(0) == 0)
        def _():
            ggk_ref[...] = jnp.zeros_like(ggk_ref)
            small_ref[...] = jnp.zeros_like(small_ref)

        low = low_ref[...]
        z = _nn(low, wgk_ref[...]) + bgk_ref[...]
        upper_f = _chunk_masks()[1].astype(F32)
        dlg = jnp.concatenate([_nn_exact(upper_f, dcum_ref[r0:r0 + CHUNK, :]) for r0 in range(0, ts, CHUNK)],
                              axis=0)
        dz = dlg * (1.0 / GATE_NORM) * _sigmoid(-z)
        dz_bf = _bf(dz)
        ggk_ref[...] += _tn(low, dz_bf)
        small_ref[1:2, 0:KEY_W] += jnp.sum(dz, axis=0, keepdims=True)
        dlow = _bf(_nt(dz_bf, wgk_ref[...]))
        dlow_ref[...] = dlow
        dn1 = _nt(dlow, wlow_ref[...])
        for ref, lo, hi in ((dq_ref, 0, KEY_W), (dk_ref, KEY_W, 2 * KEY_W),
                            (dv_ref, 2 * KEY_W, 2 * KEY_W + D), (dgate_ref, 2 * KEY_W + D, GLA_MAIN)):
            piece = ref[...]
            dproj_ref[:, lo:hi] = piece
            dn1 = dn1 + _nt(piece, wgi_ref[:, lo:hi])
        hv = h1_ref[...]
        r = lax.rsqrt(jnp.mean(hv * hv, axis=-1, keepdims=True) + EPS)
        hhat = hv * r
        small_ref[0:1, :] += jnp.sum(dn1 * hhat, axis=0, keepdims=True)
        dxh = dn1 * w1_ref[...]
        dh1_ref[...] = dh2_ref[...] + r * (dxh - hhat * jnp.mean(dxh * hhat, axis=-1, keepdims=True))

    row = lambda cols: pl.BlockSpec((ts, cols), lambda i: (i, 0))
    return pl.pallas_call(
        body, name="gla_project_backward", grid=(s // ts,),
        out_shape=(jax.ShapeDtypeStruct((s, D), F32), jax.ShapeDtypeStruct((s, GLA_MAIN), BF16),
                   jax.ShapeDtypeStruct((s, RANK_PAD), BF16), jax.ShapeDtypeStruct((RANK_PAD, KEY_W), F32),
                   jax.ShapeDtypeStruct((8, D), F32)),
        in_specs=[row(KEY_W), row(KEY_W), row(D), row(D), row(KEY_W), row(RANK_PAD), row(D), row(D),
                  _full((1, D)), _full((D, GLA_MAIN)), _full((D, RANK_PAD)), _full((RANK_PAD, KEY_W)),
                  _full((1, KEY_W))],
        out_specs=(row(D), row(GLA_MAIN), row(RANK_PAD), _full((RANK_PAD, KEY_W)), _full((8, D))),
        compiler_params=_params("arbitrary"),
    )(dq, dk, dv, dgate, dcum, low, h1, dh2, w1, wgi, wlow, wgk, bgk)


def _groups_from_quarters(a):
    return a.reshape(N_CHIPS, GROUPS, 64, GROUP_DIM).transpose(1, 0, 2, 3).reshape(GROUPS, GROUP_DIM, GROUP_DIM)


def _quarters_from_groups(a):
    return a.reshape(GROUPS, N_CHIPS, 64, GROUP_DIM).transpose(1, 0, 2, 3).reshape(N_CHIPS, GROUP_DIM, GROUP_DIM)


def _pad_row(*pieces):
    flat = jnp.concatenate([p.reshape(-1).astype(F32) for p in pieces])
    return jnp.pad(flat, (0, D - flat.shape[0])).reshape(1, D)


def _gla_weights(wgi_q, wgo_q):
    wgi_all = jnp.concatenate([wgi_q[q] for q in range(N_CHIPS)], axis=1)
    wlow = jnp.pad(wgi_all[:, GLA_MAIN:], ((0, 0), (0, RANK_PAD - GATE_RANK)))
    return wgi_all[:, :GLA_MAIN], wlow, wgo_q.reshape(D, D)


def _small_sums(small_top, small_gla, small_pool, g_gk_pad):
    return jnp.concatenate([
        small_pool[0:1], small_gla[0:1],
        small_pool[1:2],
        small_top[0:1],
        small_top[2:3],
        _pad_row(small_gla[1, 0:KEY_W], small_top[1, 0:HEAD_V]),
        small_pool[2:3],
        g_gk_pad[:GATE_RANK].reshape(8, D),
        jnp.zeros((1, D), F32)], axis=0)


def local_gradients(xs, target, w0, w1, wf, wpi, gw, gb, scale, wpo, gla_quarters, wgk, bgk, hw_tiled, place):
    (h1, pooled, gt, n0), gla_all = pool_forward(xs, w0, wpi, gw, gb, scale, wpo, gla_quarters)
    wgi, wlow, wgo = _gla_weights(*gla_all)
    qk, v, gate, low, cum, n1 = gla_project(h1, w1, wgi, wlow, wgk, bgk)
    o, states = gla_forward(qk, v, cum)

    dh2, do, dgate, g_gla_out, small_top = head_and_loss(o, gate, h1, target, hw_tiled, wgo, wf)
    dq, dk, dv, dcum = gla_backward(qk, v, cum, do, states)
    dh1, dproj, dlow, g_gk_pad, small_gla = gla_project_backward(
        dq, dk, dv, dgate, dcum, low, h1, dh2, w1, wgi, wlow, wgk, bgk)
    g_gla_in = jnp.concatenate([matmul_tn(n1, dproj, "grad_gla_in"),
                                matmul_tn(n1, dlow, "grad_gla_low")[:, :GATE_RANK]], axis=1)

    def chip_sums(grads, names, tag):
        theirs = exchange_with_sibling(grads, "exchange_with_sibling_" + tag)
        return [add_halves(g, t, place, "add_halves_" + n) for g, t, n in zip(grads, theirs, names)]

    gla_names = ("gla_in", "gla_out")
    gla_sums = chip_sums(
        [jnp.stack([g_gla_in[:, GLA_IN_QUARTER * q:GLA_IN_QUARTER * (q + 1)] for q in range(N_CHIPS)]),
         g_gla_out.reshape(N_CHIPS, D // N_CHIPS, D)], gla_names, "gla")
    (dx, dpool, g_pool_out, g_group_w, small_pool), gla_got = pool_backward(
        xs, dh1, pooled, gt, w0, wpi, gw, gb, scale, wpo, [b for _, b in gla_sums])
    g_pool_in = matmul_tn(n0, dpool, "grad_pool_in", by_column_tile=True)

    pool_names = ("pool_in", "group", "pool_out")
    pool_sums = chip_sums(
        [g_pool_in, _quarters_from_groups(g_group_w), g_pool_out.reshape(N_CHIPS, D // N_CHIPS, D)],
        pool_names, "pool")
    pool_got = scatter_to_owners([b for _, b in pool_sums], "scatter_to_owners_pool")
    reduced, total = join_halves(
        [add_parts(f, g, place, "add_parts_" + n) for (f, _), g, n in
         zip(pool_sums + gla_sums, list(pool_got) + list(gla_got), pool_names + gla_names)],
        _small_sums(small_top, small_gla, small_pool, g_gk_pad))
    return dx, reduced, total


def kernel(x, norm_w, pool_in_w, pool_group_w, pool_group_b, pool_scale, pool_out_w, gla_in_w, gla_gk_w, gla_gk_b, gla_head_norm_w, gla_out_w, final_norm_w, loss_target, m_norm_w, m_pool_in_w, m_pool_group_w, m_pool_group_b, m_pool_scale, m_pool_out_w, m_gla_in_w, m_gla_gk_w, m_gla_gk_b, m_gla_head_norm_w, m_gla_out_w, m_final_norm_w, v_norm_w, v_pool_in_w, v_pool_group_w, v_pool_group_b, v_pool_scale, v_pool_out_w, v_gla_in_w, v_gla_gk_w, v_gla_gk_b, v_gla_head_norm_w, v_gla_out_w, v_final_norm_w):
    s = x.shape[1]
    xs = x[0]
    target = loss_target[0]
    q_chip = 2 * lax.axis_index("x") + lax.axis_index("y")
    place = jnp.stack([lax.axis_index("c"), q_chip]).astype(jnp.int32)

    small_in = jnp.concatenate([
        _pad_row(gla_gk_b[0], gla_head_norm_w[0], pool_group_b[0]),
        gla_gk_w[0].reshape(2, D),
        jnp.zeros((5, D), F32)], axis=0)
    (wpi, gw_q, wpo_q, wgi_q, wgo_q), small_all = allgather_weights(
        [pool_in_w[0], pool_group_w[0].reshape(GROUP_DIM, GROUP_DIM), pool_out_w[0], gla_in_w[0], gla_out_w[0]],
        exchange=(True, True, True, False, False), small=small_in)
    gw = _groups_from_quarters(gw_q)
    wpo = wpo_q.reshape(D, D)
    small_all = small_all[0::2]
    bgk = small_all[:, 0, 0:128].reshape(1, KEY_W)
    hw = small_all[:, 0, 128:192].reshape(1, HEAD_V)
    gb = jnp.concatenate([small_all[q, 0, 192:448].reshape(GROUPS, 64) for q in range(N_CHIPS)],
                         axis=1).reshape(1, D)
    wgk16 = jnp.concatenate([small_all[q, 1:3].reshape(GATE_RANK, 128) for q in range(N_CHIPS)], axis=1)
    wgk = _bf(jnp.pad(wgk16, ((0, RANK_PAD - GATE_RANK), (0, 0))))
    hw_tiled = jnp.tile(hw, (1, HEADS))

    w0 = norm_w[0:1]
    w1 = norm_w[1:2]
    wf = final_norm_w.reshape(1, D)

    dx, reduced, total = local_gradients(
        xs, target, w0, w1, wf, wpi, gw, gb, pool_scale, wpo, [wgi_q, wgo_q], wgk, bgk, hw_tiled, place)
    r_pool_in, r_group_w, r_pool_out, r_gla_in, r_gla_out = reduced
    r_group_w = r_group_w.reshape(GROUPS, 64, GROUP_DIM)

    loss = total[4, 0]
    g_norm = total[0:2]
    g_scale = total[2:3]
    g_final = total[3]
    pick = lambda full, width: lax.dynamic_slice_in_dim(full, q_chip * width, width, axis=-1)
    g_gk_b = pick(total[5:6, 0:KEY_W], 128)
    g_hnw = pick(total[5:6, KEY_W:KEY_W + HEAD_V], 64)
    g_group_b = pick(total[6].reshape(GROUPS, GROUP_DIM), 64)[None]
    g_gk_w = pick(total[7:15].reshape(GATE_RANK, KEY_W), 128)[None]

    def step_lane_rows(name, w, g, m, v):
        turn = lambda a: jnp.transpose(a, (2, 0, 1))
        back = lambda a: jnp.transpose(a, (1, 2, 0))
        g_t = turn(g)
        d, nm, nv = adamw_rows(turn(w), g_t, turn(m), turn(v), "adamw_" + name)
        return back(g_t), back(d), back(nm), back(nv)

    def step(name, w, g, m, v):
        shape = w.shape
        as2d = lambda a: a.reshape(-1, shape[-1])
        d, nm, nv = adamw(as2d(w), as2d(g), as2d(m), as2d(v), "adamw_" + name)
        return g.reshape(shape), d.reshape(shape), nm.reshape(shape), nv.reshape(shape)

    small_names = ("norm_w", "pool_group_b", "pool_scale", "gla_gk_w", "gla_gk_b", "gla_head_norm_w",
                   "final_norm_w")
    small_args = [(norm_w, g_norm, m_norm_w, v_norm_w),
                  (pool_group_b, g_group_b, m_pool_group_b, v_pool_group_b),
                  (pool_scale, g_scale, m_pool_scale, v_pool_scale),
                  (gla_gk_w, g_gk_w, m_gla_gk_w, v_gla_gk_w),
                  (gla_gk_b, g_gk_b, m_gla_gk_b, v_gla_gk_b),
                  (gla_head_norm_w, g_hnw, m_gla_head_norm_w, v_gla_head_norm_w),
                  (final_norm_w, g_final, m_final_norm_w, v_final_norm_w)]
    as2d = lambda a, w: a.reshape(-1, w.shape[-1])
    small_out = adamw_small([tuple(as2d(a, p[0]) for a in p) for p in small_args])
    small = {n: (p[1].reshape(p[0].shape),) + tuple(o.reshape(p[0].shape) for o in out)
             for n, p, out in zip(small_names, small_args, small_out)}
    results = [
        small["norm_w"],
        step("pool_in_w", pool_in_w, r_pool_in[None], m_pool_in_w, v_pool_in_w),
        step("pool_group_w", pool_group_w, r_group_w[None], m_pool_group_w, v_pool_group_w),
        small["pool_group_b"],
        small["pool_scale"],
        step("pool_out_w", pool_out_w, r_pool_out[None], m_pool_out_w, v_pool_out_w),
        step_lane_rows("gla_in_w", gla_in_w, r_gla_in[None], m_gla_in_w, v_gla_in_w),
        small["gla_gk_w"],
        small["gla_gk_b"],
        small["gla_head_norm_w"],
        step("gla_out_w", gla_out_w, r_gla_out[None], m_gla_out_w, v_gla_out_w),
        small["final_norm_w"],
    ]
    grads, deltas, new_m, new_v = zip(*results)
    return (loss, dx[None], *grads, *deltas, *new_m, *new_v)
```

```python
import functools

import jax
import jax.numpy as jnp
from jax import lax
from jax.experimental import pallas as pl
from jax.experimental.pallas import tpu as pltpu

F32 = jnp.float32
BF16 = jnp.bfloat16
MESH = pl.DeviceIdType.MESH

D = 1024
POOL_WINDOWS = (2, 4, 8, 16)
GROUPS = 4
GROUP_DIM = 256
HEADS = 4
HEAD_K = 128
HEAD_V = 256
KEY_W = 512
CHUNK = 64
GATE_RANK = 16
GATE_NORM = 16.0
GLA_IN = 3088
GLA_MAIN = 3072
RANK_PAD = 128
EPS = 1e-6
HALO = 16

ADAM_LR = 0.001
ADAM_B1 = 0.9
ADAM_B2 = 0.999
ADAM_EPS = 1e-08
ADAM_WD = 0.01
ADAM_STEP = 10

N_CHIPS = 4
N_DEV = 8
GLA_IN_QUARTER = GLA_IN // N_CHIPS

VMEM_LIMIT = 56 * 1024 * 1024


def _nn(a, b):
    return lax.dot_general(a, b, (((1,), (0,)), ((), ())), preferred_element_type=F32)


def _nt(a, b):
    return lax.dot_general(a, b, (((1,), (1,)), ((), ())), preferred_element_type=F32)


def _tn(a, b):
    return lax.dot_general(a, b, (((0,), (0,)), ((), ())), preferred_element_type=F32)


def _nn_exact(a, b):
    return lax.dot_general(a, b, (((1,), (0,)), ((), ())), preferred_element_type=F32,
                           precision=lax.Precision.HIGHEST)


def _bf(a):
    return a.astype(BF16)


def _params(*sem):
    return pltpu.CompilerParams(dimension_semantics=sem, vmem_limit_bytes=VMEM_LIMIT)


def _full(shape):
    return pl.BlockSpec(shape, lambda i: (0,) * len(shape))


def _position():
    return lax.axis_index("x"), lax.axis_index("y"), lax.axis_index("c")


def _gather_small(in_ref, all_ref, send_sems, recv_sems, local_sem):
    x, y, c = _position()
    me = 4 * x + 2 * y + c
    mine = pltpu.make_async_copy(in_ref, all_ref.at[me], local_sem)
    mine.start()
    sends = []
    for k in range(N_DEV - 1):
        fx, fy, fc = (k + 1) >> 2 & 1, (k + 1) >> 1 & 1, (k + 1) & 1
        cp = pltpu.make_async_remote_copy(
            src_ref=in_ref, dst_ref=all_ref.at[me],
            send_sem=send_sems.at[k], recv_sem=recv_sems.at[k],
            device_id=(x ^ fx, y ^ fy, c ^ fc), device_id_type=MESH)
        cp.start()
        sends.append(cp)
    for k in range(N_DEV - 1):
        fx, fy, fc = (k + 1) >> 2 & 1, (k + 1) >> 1 & 1, (k + 1) & 1
        src_dev = 4 * (x ^ fx) + 2 * (y ^ fy) + (c ^ fc)
        pltpu.make_async_remote_copy(
            src_ref=in_ref, dst_ref=all_ref.at[src_dev],
            send_sem=send_sems.at[k], recv_sem=recv_sems.at[k],
            device_id=(x, y, c), device_id_type=MESH).wait_recv()
    for cp in sends:
        cp.wait_send()
    mine.wait()


SMALL_SEMS = [pltpu.SemaphoreType.DMA((N_DEV - 1,)), pltpu.SemaphoreType.DMA((N_DEV - 1,)),
              pltpu.SemaphoreType.DMA]
VMEM_SPEC = pl.BlockSpec(memory_space=pltpu.VMEM)


def _other_chips(x, y):
    return [(1 - x, y), (x, 1 - y), (1 - x, 1 - y)]


def _any_specs(n):
    return [pl.BlockSpec(memory_space=pl.ANY)] * n


def _halves(rows, c):
    half = rows // 2
    return pl.ds(c * half, half), pl.ds((1 - c) * half, half)


CAST_ROWS = 256


def _gather_copy(out_ref, send_sems, recv_sems, k, quarter, half, to, src=None):
    dst = out_ref.at[quarter, half]
    return pltpu.make_async_remote_copy(
        src_ref=dst if src is None else src, dst_ref=dst,
        send_sem=send_sems.at[k], recv_sem=recv_sems.at[k], device_id=to, device_id_type=MESH)


def allgather_weights(quarters, exchange, small):
    n = len(quarters)
    shapes = [w.shape for w in quarters]
    moved = [i for i in range(n) if exchange[i]]

    def body(*refs):
        w_refs, small_ref = refs[:n], refs[n]
        out_refs, small_all_ref = refs[n + 1:2 * n + 1], refs[2 * n + 1]
        refs = refs[2 * n + 2:]
        f32_bufs, bf_bufs = refs[:n], refs[n:2 * n]
        send_sems, recv_sems, local_sems = refs[2 * n:2 * n + 3]
        x, y, c = _position()
        q = 2 * x + y
        sibling = (x, y, 1 - c)
        chips = _other_chips(x, y)

        def copy(k, i, quarter, half, to, src=None):
            return _gather_copy(out_refs[i], send_sems, recv_sems, k * n + i, quarter, half, to, src)

        loads = [pltpu.make_async_copy(w_refs[i], f32_bufs[i], local_sems.at[i]) for i in range(n)]
        for cp in loads:
            cp.start()
        keeps, sends = [], []
        for i in range(n):
            loads[i].wait()
            for r0 in range(0, shapes[i][0], CAST_ROWS):
                bf_bufs[i][r0:r0 + CAST_ROWS, :] = _bf(f32_bufs[i][r0:r0 + CAST_ROWS, :])
            keep = pltpu.make_async_copy(bf_bufs[i], out_refs[i].at[q], local_sems.at[n + i])
            keep.start()
            keeps.append(keep)
            if not exchange[i]:
                continue
            mine, _ = _halves(shapes[i][0], c)
            for j, chip in enumerate(chips):
                cp = copy(j, i, q, mine, (*chip, c), src=bf_bufs[i].at[mine])
                cp.start()
                sends.append(cp)
        for j, chip in enumerate(chips):
            qj = 2 * chip[0] + chip[1]
            for i in moved:
                mine, _ = _halves(shapes[i][0], c)
                copy(j, i, qj, mine, (x, y, c)).wait_recv()
                cp = copy(3 + j, i, qj, mine, sibling)
                cp.start()
                sends.append(cp)
        for j, chip in enumerate(chips):
            qj = 2 * chip[0] + chip[1]
            for i in moved:
                _, other = _halves(shapes[i][0], c)
                copy(3 + j, i, qj, other, (x, y, c)).wait_recv()
        _gather_small(small_ref, small_all_ref, *refs[2 * n + 3:])
        for cp in sends:
            cp.wait_send()
        for cp in keeps:
            cp.wait()

    outs = pl.pallas_call(
        body, name="allgather_weights",
        out_shape=[jax.ShapeDtypeStruct((N_CHIPS, *s), BF16) for s in shapes]
                  + [jax.ShapeDtypeStruct((N_DEV, *small.shape), small.dtype)],
        in_specs=_any_specs(n) + [VMEM_SPEC], out_specs=_any_specs(n) + [VMEM_SPEC],
        scratch_shapes=([pltpu.VMEM(s, F32) for s in shapes] + [pltpu.VMEM(s, BF16) for s in shapes]
                        + [pltpu.SemaphoreType.DMA((6 * n,)), pltpu.SemaphoreType.DMA((6 * n,)),
                           pltpu.SemaphoreType.DMA((2 * n,))] + SMALL_SEMS),
        compiler_params=pltpu.CompilerParams(vmem_limit_bytes=VMEM_LIMIT),
    )(*quarters, small)
    return outs[:n], outs[n]


def exchange_with_sibling(grads, name):
    n = len(grads)

    def body(*refs):
        g_refs, theirs_refs = refs[:n], refs[n:2 * n]
        send_sems, recv_sems = refs[2 * n:]
        x, y, c = _position()
        copies = []
        for i in range(n):
            _, other = _halves(g_refs[i].shape[1], c)
            cp = pltpu.make_async_remote_copy(
                src_ref=g_refs[i].at[:, other], dst_ref=theirs_refs[i],
                send_sem=send_sems.at[i], recv_sem=recv_sems.at[i],
                device_id=(x, y, 1 - c), device_id_type=MESH)
            cp.start()
            copies.append(cp)
        for cp in copies:
            cp.wait()

    return pl.pallas_call(
        body, name=name,
        out_shape=[jax.ShapeDtypeStruct((N_CHIPS, g.shape[1] // 2, g.shape[2]), F32) for g in grads],
        in_specs=_any_specs(n), out_specs=_any_specs(n),
        scratch_shapes=[pltpu.SemaphoreType.DMA((n,)), pltpu.SemaphoreType.DMA((n,))],
    )(*grads)


def _scatter_copies(b_refs, got_refs, send_sems, recv_sems):
    n = len(b_refs)
    x, y, c = _position()
    copies = []
    for j, chip in enumerate(_other_chips(x, y)):
        qj = 2 * chip[0] + chip[1]
        for i in range(n):
            copies.append(pltpu.make_async_remote_copy(
                src_ref=b_refs[i].at[qj], dst_ref=got_refs[i].at[j],
                send_sem=send_sems.at[j * n + i], recv_sem=recv_sems.at[j * n + i],
                device_id=(*chip, c), device_id_type=MESH))
    return copies


def _scatter_shapes(chip_sums):
    return [jax.ShapeDtypeStruct((N_CHIPS - 1, *b.shape[1:]), BF16) for b in chip_sums]


def scatter_to_owners(chip_sums, name):
    n = len(chip_sums)

    def body(*refs):
        copies = _scatter_copies(refs[:n], refs[n:2 * n], *refs[2 * n:])
        for cp in copies:
            cp.start()
        for cp in copies:
            cp.wait()

    return pl.pallas_call(
        body, name=name,
        out_shape=_scatter_shapes(chip_sums),
        in_specs=_any_specs(n), out_specs=_any_specs(n),
        scratch_shapes=[pltpu.SemaphoreType.DMA((3 * n,)), pltpu.SemaphoreType.DMA((3 * n,))],
    )(*chip_sums)


def join_halves(reduced, small):
    n = len(reduced)

    def body(*refs):
        small_ref = refs[n]
        buf_refs, total_ref = refs[n + 1:2 * n + 1], refs[2 * n + 1]
        send_sems, recv_sems, all_ref = refs[2 * n + 2:2 * n + 5]
        x, y, c = _position()
        copies = []
        for i in range(n):
            mine, _ = _halves(buf_refs[i].shape[0], c)
            cp = pltpu.make_async_remote_copy(
                src_ref=buf_refs[i].at[mine], dst_ref=buf_refs[i].at[mine],
                send_sem=send_sems.at[i], recv_sem=recv_sems.at[i],
                device_id=(x, y, 1 - c), device_id_type=MESH)
            cp.start()
            copies.append(cp)
        _gather_small(small_ref, all_ref, *refs[2 * n + 5:])
        total = all_ref[0]
        for dev in range(1, N_DEV):
            total = total + all_ref[dev]
        total_ref[...] = total
        for cp in copies:
            cp.wait()

    outs = pl.pallas_call(
        body, name="join_halves",
        out_shape=[jax.ShapeDtypeStruct(r.shape, F32) for r in reduced]
                  + [jax.ShapeDtypeStruct(small.shape, small.dtype)],
        in_specs=_any_specs(n) + [VMEM_SPEC], out_specs=_any_specs(n) + [VMEM_SPEC],
        input_output_aliases={i: i for i in range(n)},
        scratch_shapes=[pltpu.SemaphoreType.DMA((n,)), pltpu.SemaphoreType.DMA((n,)),
                        pltpu.VMEM((N_DEV, *small.shape), small.dtype)] + SMALL_SEMS,
    )(*reduced, small)
    return outs[:n], outs[n]


ADD_ROWS = 256


def add_halves(grad, theirs, place, name):
    _, half, cols = theirs.shape
    rb = min(ADD_ROWS, half)
    steps = half // rb

    def body(place_ref, a_ref, b_ref, f_ref, h_ref):
        s = a_ref[...] + b_ref[...]
        f_ref[...] = s
        h_ref[...] = _bf(s)

    spec = pl.BlockSpec((1, rb, cols), lambda i, j, place: (i, j, 0))
    return pl.pallas_call(
        body, name=name,
        grid_spec=pltpu.PrefetchScalarGridSpec(
            num_scalar_prefetch=1, grid=(N_CHIPS, steps),
            in_specs=[pl.BlockSpec((1, rb, cols), lambda i, j, place: (i, place[0] * steps + j, 0)), spec],
            out_specs=(spec, spec)),
        out_shape=(jax.ShapeDtypeStruct(theirs.shape, F32), jax.ShapeDtypeStruct(theirs.shape, BF16)),
        compiler_params=_params("parallel", "parallel"),
    )(place, grad, theirs)


def add_parts(chip_sum, got, place, name):
    _, half, cols = got.shape
    rb = min(ADD_ROWS, half)
    steps = half // rb

    def body(place_ref, o_ref, g_ref, out_ref):
        s = o_ref[0]
        for j in range(N_CHIPS - 1):
            s = s + g_ref[j].astype(F32)
        out_ref[...] = s

    return pl.pallas_call(
        body, name=name,
        grid_spec=pltpu.PrefetchScalarGridSpec(
            num_scalar_prefetch=1, grid=(steps,),
            in_specs=[pl.BlockSpec((1, rb, cols), lambda j, place: (place[1], j, 0)),
                      pl.BlockSpec((N_CHIPS - 1, rb, cols), lambda j, place: (0, j, 0))],
            out_specs=pl.BlockSpec((rb, cols), lambda j, place: (place[0] * steps + j, 0))),
        out_shape=jax.ShapeDtypeStruct((2 * half, cols), F32),
        compiler_params=_params("parallel"),
    )(place, chip_sum, got)


def _adam_math(w, g, m, v):
    m = ADAM_B1 * m + (1.0 - ADAM_B1) * g
    v = ADAM_B2 * v + (1.0 - ADAM_B2) * (g * g)
    m_hat = m / (1.0 - ADAM_B1 ** ADAM_STEP)
    v_hat = v / (1.0 - ADAM_B2 ** ADAM_STEP)
    delta = -ADAM_LR * (m_hat / (jnp.sqrt(v_hat) + ADAM_EPS) + ADAM_WD * w)
    return delta, m, v


def adamw(w, g, m, v, name):
    rows, cols = w.shape
    fits = [t for t in range(8, rows, 8) if rows % t == 0 and t * cols * 4 <= 2 ** 20]
    tile = max(fits) if fits else rows

    def body(w_ref, g_ref, m_ref, v_ref, d_ref, nm_ref, nv_ref):
        d, nm, nv = _adam_math(w_ref[...], g_ref[...], m_ref[...], v_ref[...])
        d_ref[...] = d
        nm_ref[...] = nm
        nv_ref[...] = nv

    spec = pl.BlockSpec((tile, cols), lambda i: (i, 0))
    shape = jax.ShapeDtypeStruct((rows, cols), F32)
    return pl.pallas_call(
        body, name=name, grid=(rows // tile,),
        out_shape=(shape, shape, shape),
        in_specs=[spec] * 4, out_specs=(spec, spec, spec),
        compiler_params=_params("parallel"),
    )(w, g, m, v)


def adamw_small(params):
    n = len(params)

    def body(*refs):
        ins, outs = refs[:4 * n], refs[4 * n:]
        for k in range(n):
            w_ref, g_ref, m_ref, v_ref = ins[4 * k:4 * k + 4]
            d, nm, nv = _adam_math(w_ref[...], g_ref[...], m_ref[...], v_ref[...])
            outs[3 * k][...] = d
            outs[3 * k + 1][...] = nm
            outs[3 * k + 2][...] = nv

    flat = [a for p in params for a in p]
    outs = pl.pallas_call(
        body, name="adamw_small",
        out_shape=[jax.ShapeDtypeStruct(p[0].shape, F32) for p in params for _ in range(3)],
        in_specs=[VMEM_SPEC] * (4 * n), out_specs=[VMEM_SPEC] * (3 * n),
    )(*flat)
    return [tuple(outs[3 * k:3 * k + 3]) for k in range(n)]


def adamw_rows(w, g, m, v, name):
    rows, _, cols = w.shape
    tile = rows // 4

    def body(w_ref, g_ref, m_ref, v_ref, d_ref, nm_ref, nv_ref):
        d, nm, nv = _adam_math(w_ref[...], g_ref[...], m_ref[...], v_ref[...])
        d_ref[...] = d
        nm_ref[...] = nm
        nv_ref[...] = nv

    spec = pl.BlockSpec((tile, 1, cols), lambda i: (i, 0, 0))
    shape = jax.ShapeDtypeStruct(w.shape, F32)
    return pl.pallas_call(
        body, name=name, grid=(rows // tile,),
        out_shape=(shape, shape, shape),
        in_specs=[spec] * 4, out_specs=(spec, spec, spec),
        compiler_params=_params("parallel"),
    )(w, g, m, v)


def matmul_tn(a, b, name, tile_n=512, tile_s=2048, by_column_tile=False):
    s, m = a.shape
    n = b.shape[1]
    tile_n = min(tile_n, n)
    tile_s = min(tile_s, s)
    steps = s // tile_s
    if by_column_tile:
        out_shape = jax.ShapeDtypeStruct((n // tile_n, m, tile_n), F32)
        out_spec = pl.BlockSpec((None, m, tile_n), lambda j, k: (j, 0, 0))
    else:
        out_shape = jax.ShapeDtypeStruct((m, n), F32)
        out_spec = pl.BlockSpec((m, tile_n), lambda j, k: (0, j))

    def body(a_ref, b_ref, out_ref):
        k = pl.program_id(1)

        @pl.when(k == 0)
        def _():
            out_ref[...] = jnp.zeros_like(out_ref)

        out_ref[...] += _tn(a_ref[...], b_ref[...])

    return pl.pallas_call(
        body, name=name, grid=(n // tile_n, steps),
        out_shape=out_shape,
        in_specs=[pl.BlockSpec((tile_s, m), lambda j, k: (k, 0)),
                  pl.BlockSpec((tile_s, tile_n), lambda j, k: (k, j))],
        out_specs=out_spec,
        compiler_params=_params("parallel", "arbitrary"),
    )(a, b)


ROW_TILE = 512


def _row_index(tile, rows):
    return tile * rows + lax.broadcasted_iota(jnp.int32, (rows, 1), 0)


def _inverse_counts(t_glob):
    return [1.0 / jnp.minimum(t_glob + 1, w).astype(F32) for w in POOL_WINDOWS]


def _sigmoid(z):
    return 1.0 / (1.0 + jnp.exp(-z))


def gather_in_background(step, last, out_refs, send_sems, recv_sems):
    n = len(out_refs)
    x, y, c = _position()
    q = 2 * x + y
    chips = _other_chips(x, y)

    def copy(k, i, quarter, half, to):
        return _gather_copy(out_refs[i], send_sems, recv_sems, k * n + i, quarter, half, to)

    @pl.when(step == 0)
    def _():
        for i in range(n):
            mine, _ = _halves(out_refs[i].shape[1], c)
            for j, chip in enumerate(chips):
                copy(j, i, q, mine, (*chip, c)).start()

    @pl.when(step == last - 1)
    def _():
        for j, chip in enumerate(chips):
            qj = 2 * chip[0] + chip[1]
            for i in range(n):
                mine, _ = _halves(out_refs[i].shape[1], c)
                copy(j, i, qj, mine, (x, y, c)).wait_recv()
                copy(3 + j, i, qj, mine, (x, y, 1 - c)).start()

    @pl.when(step == last)
    def _():
        for j, chip in enumerate(chips):
            qj = 2 * chip[0] + chip[1]
            for i in range(n):
                mine, other = _halves(out_refs[i].shape[1], c)
                copy(3 + j, i, qj, other, (x, y, c)).wait_recv()
                copy(j, i, q, mine, (x, y, c)).wait_send()
                copy(3 + j, i, qj, mine, (x, y, c)).wait_send()


def pool_forward(x, w0, wpi, gw, gb, scale, wpo, later):
    s = x.shape[0]
    ts = ROW_TILE
    nt = s // ts
    assert nt >= 2
    n_later = len(later)

    def body(x_ref, w0_ref, wpi_ref, gw_ref, gb_ref, sc_ref, wpo_ref, *rest):
        rest = rest[n_later:]
        h1_ref, pooled_ref, gt_ref, n0_ref = rest[:4]
        later_refs = rest[4:4 + n_later]
        ubuf, send_sems, recv_sems = rest[4 + n_later:]
        i = pl.program_id(0)
        gather_in_background(i, nt - 1, later_refs, send_sems, recv_sems)
        xv = x_ref[...]
        r = lax.rsqrt(jnp.mean(xv * xv, axis=-1, keepdims=True) + EPS)
        n0 = _bf(xv * r * w0_ref[...])
        n0_ref[...] = n0
        u = jnp.concatenate([_nn(n0, wpi_ref[0]), _nn(n0, wpi_ref[1])], axis=-1)
        gt = jnp.concatenate([_nn(n0, wpi_ref[2]), _nn(n0, wpi_ref[3])], axis=-1)
        gt_ref[...] = gt

        @pl.when(i == 0)
        def _():
            ubuf[0:HALO, :] = jnp.zeros((HALO, D), F32)

        ubuf[HALO:HALO + ts, :] = u
        inv = _inverse_counts(_row_index(i, ts))
        mixed = []
        for g, w in enumerate(POOL_WINDOWS):
            cols = slice(g * GROUP_DIM, (g + 1) * GROUP_DIM)
            ug = u[:, cols]
            acc = ug
            for j in range(1, w):
                acc = acc + ubuf[HALO - j:HALO - j + ts, cols]
            pooled = _bf(acc * inv[g] - ug)
            pooled_ref[:, cols] = pooled
            mixed.append(_nn(pooled, gw_ref[g]))
        ubuf[0:HALO, :] = ubuf[ts:ts + HALO, :]
        mixed = jnp.concatenate(mixed, axis=-1) + gb_ref[...]
        y = mixed * sc_ref[...] * (gt * _sigmoid(gt))
        h1_ref[...] = xv + _nn(_bf(y), wpo_ref[...])

    row = lambda cols: pl.BlockSpec((ts, cols), lambda i: (i, 0))
    outs = pl.pallas_call(
        body, name="pool_forward", grid=(nt,),
        out_shape=[jax.ShapeDtypeStruct((s, D), F32), jax.ShapeDtypeStruct((s, D), BF16),
                   jax.ShapeDtypeStruct((s, D), F32), jax.ShapeDtypeStruct((s, D), BF16)]
                  + [jax.ShapeDtypeStruct(a.shape, a.dtype) for a in later],
        in_specs=[row(D), _full((1, D)), _full((N_CHIPS, D, D // 2)), _full((GROUPS, GROUP_DIM, GROUP_DIM)),
                  _full((1, D)), _full((1, D)), _full((D, D))] + _any_specs(n_later),
        out_specs=[row(D), row(D), row(D), row(D)] + _any_specs(n_later),
        input_output_aliases={7 + k: 4 + k for k in range(n_later)},
        scratch_shapes=[pltpu.VMEM((HALO + ts, D), F32),
                        pltpu.SemaphoreType.DMA((6 * n_later,)), pltpu.SemaphoreType.DMA((6 * n_later,))],
        compiler_params=_params("arbitrary"),
    )(x, w0, wpi, gw, gb, scale, wpo, *later)
    return outs[:4], outs[4:]


def pool_backward(x, dh1, pooled, gt, w0, wpi, gw, gb, scale, wpo, chip_sums):
    s = x.shape[0]
    ts = ROW_TILE
    nt = s // ts
    n_sums = len(chip_sums)

    def body(x_ref, dh1_ref, pooled_ref, gt_ref, w0_ref, wpi_ref, gw_ref, gb_ref, sc_ref, wpo_ref, *rest):
        sum_refs, rest = rest[:n_sums], rest[n_sums:]
        dx_ref, dproj_ref, gpo_ref, ggw_ref, small_ref = rest[:5]
        got_refs = rest[5:5 + n_sums]
        ebuf, send_sems, recv_sems = rest[5 + n_sums:]
        i = pl.program_id(0)
        copies = _scatter_copies(sum_refs, got_refs, send_sems, recv_sems)

        @pl.when(i == 0)
        def _():
            for cp in copies:
                cp.start()

        @pl.when(i == nt - 1)
        def _():
            for cp in copies:
                cp.wait()

        @pl.when(i == 0)
        def _():
            gpo_ref[...] = jnp.zeros_like(gpo_ref)
            ggw_ref[...] = jnp.zeros_like(ggw_ref)
            small_ref[...] = jnp.zeros_like(small_ref)
            ebuf[ts:ts + HALO, :] = jnp.zeros((HALO, D), F32)

        dh1 = dh1_ref[...]
        dh1_bf = _bf(dh1)
        gt = gt_ref[...]
        sc = sc_ref[...]
        dy = _nt(dh1_bf, wpo_ref[...])
        pooled_bf = []
        mixed = []
        for g in range(GROUPS):
            cols = slice(g * GROUP_DIM, (g + 1) * GROUP_DIM)
            pb = pooled_ref[:, cols]
            pooled_bf.append(pb)
            mixed.append(_nn(pb, gw_ref[g]))
        mixed = jnp.concatenate(mixed, axis=-1) + gb_ref[...]
        sg = _sigmoid(gt)
        silu = gt * sg
        gpo_ref[...] += _tn(_bf(mixed * sc * silu), dh1_bf)
        dmixed = dy * sc * silu
        dgt = dy * mixed * sc * (sg * (1.0 + gt * (1.0 - sg)))
        dproj_ref[:, D:] = _bf(dgt)
        small_ref[1:2, :] += jnp.sum(dy * mixed * silu, axis=0, keepdims=True)
        small_ref[2:3, :] += jnp.sum(dmixed, axis=0, keepdims=True)

        inv = _inverse_counts(_row_index(nt - 1 - i, ts))
        dpooled = []
        for g in range(GROUPS):
            cols = slice(g * GROUP_DIM, (g + 1) * GROUP_DIM)
            dm = _bf(dmixed[:, cols])
            ggw_ref[g] += _tn(pooled_bf[g], dm)
            dp = _nt(dm, gw_ref[g])
            dpooled.append(dp)
            ebuf[0:ts, cols] = dp * inv[g]
        du = []
        for g, w in enumerate(POOL_WINDOWS):
            cols = slice(g * GROUP_DIM, (g + 1) * GROUP_DIM)
            acc = -dpooled[g]
            for j in range(w):
                acc = acc + ebuf[j:j + ts, cols]
            du.append(acc)
        ebuf[ts:ts + HALO, :] = ebuf[0:HALO, :]
        du = _bf(jnp.concatenate(du, axis=-1))
        dproj_ref[:, :D] = du
        dgt_bf = _bf(dgt)
        half = D // 2
        dn0 = (_nt(du[:, :half], wpi_ref[0]) + _nt(du[:, half:], wpi_ref[1])
               + _nt(dgt_bf[:, :half], wpi_ref[2]) + _nt(dgt_bf[:, half:], wpi_ref[3]))

        xv = x_ref[...]
        r = lax.rsqrt(jnp.mean(xv * xv, axis=-1, keepdims=True) + EPS)
        xhat = xv * r
        small_ref[0:1, :] += jnp.sum(dn0 * xhat, axis=0, keepdims=True)
        dxh = dn0 * w0_ref[...]
        dx_ref[...] = dh1 + r * (dxh - xhat * jnp.mean(dxh * xhat, axis=-1, keepdims=True))

    row = lambda cols: pl.BlockSpec((ts, cols), lambda i: (nt - 1 - i, 0))
    outs = pl.pallas_call(
        body, name="pool_backward", grid=(nt,),
        out_shape=[jax.ShapeDtypeStruct((s, D), F32), jax.ShapeDtypeStruct((s, 2 * D), BF16),
                   jax.ShapeDtypeStruct((D, D), F32),
                   jax.ShapeDtypeStruct((GROUPS, GROUP_DIM, GROUP_DIM), F32),
                   jax.ShapeDtypeStruct((8, D), F32)] + _scatter_shapes(chip_sums),
        in_specs=[row(D), row(D), row(D), row(D), _full((1, D)), _full((N_CHIPS, D, D // 2)),
                  _full((GROUPS, GROUP_DIM, GROUP_DIM)), _full((1, D)), _full((1, D)), _full((D, D))]
                 + _any_specs(n_sums),
        out_specs=[row(D), row(2 * D), _full((D, D)), _full((GROUPS, GROUP_DIM, GROUP_DIM)), _full((8, D))]
                  + _any_specs(n_sums),
        scratch_shapes=[pltpu.VMEM((ts + HALO, D), F32),
                        pltpu.SemaphoreType.DMA((3 * n_sums,)), pltpu.SemaphoreType.DMA((3 * n_sums,))],
        compiler_params=_params("arbitrary"),
    )(x, dh1, pooled, gt, w0, wpi, gw, gb, scale, wpo, *chip_sums)
    return outs[:5], outs[5:]


def gla_project(h1, w1, wgi, wlow, wgk, bgk):
    s = h1.shape[0]
    ts = ROW_TILE

    def body(h_ref, w1_ref, wgi_ref, wlow_ref, wgk_ref, bgk_ref, qk_ref, v_ref, gate_ref, low_ref, cum_ref,
             n1_ref):
        hv = h_ref[...]
        r = lax.rsqrt(jnp.mean(hv * hv, axis=-1, keepdims=True) + EPS)
        n1 = _bf(hv * r * w1_ref[...])
        n1_ref[...] = n1
        qk_ref[...] = _nn(n1, wgi_ref[:, 0:2 * KEY_W])
        v_ref[...] = _bf(_nn(n1, wgi_ref[:, 2 * KEY_W:2 * KEY_W + D]))
        gate_ref[...] = _nn(n1, wgi_ref[:, 2 * KEY_W + D:GLA_MAIN])
        low = _bf(_nn(n1, wlow_ref[...]))
        low_ref[...] = low
        z = _nn(low, wgk_ref[...]) + bgk_ref[...]
        lg = (jnp.minimum(z, 0.0) - jnp.log(1.0 + jnp.exp(-jnp.abs(z)))) / GATE_NORM
        lower_f = _chunk_masks()[0].astype(F32)
        for r0 in range(0, ts, CHUNK):
            cum_ref[r0:r0 + CHUNK, :] = _nn_exact(lower_f, lg[r0:r0 + CHUNK, :])

    row = lambda cols: pl.BlockSpec((ts, cols), lambda i: (i, 0))
    return pl.pallas_call(
        body, name="gla_project", grid=(s // ts,),
        out_shape=(jax.ShapeDtypeStruct((s, D), F32), jax.ShapeDtypeStruct((s, D), BF16),
                   jax.ShapeDtypeStruct((s, D), F32), jax.ShapeDtypeStruct((s, RANK_PAD), BF16),
                   jax.ShapeDtypeStruct((s, KEY_W), F32), jax.ShapeDtypeStruct((s, D), BF16)),
        in_specs=[row(D), _full((1, D)), _full((D, GLA_MAIN)), _full((D, RANK_PAD)),
                  _full((RANK_PAD, KEY_W)), _full((1, KEY_W))],
        out_specs=(row(D), row(D), row(D), row(RANK_PAD), row(KEY_W), row(D)),
        compiler_params=_params("parallel"),
    )(h1, w1, wgi, wlow, wgk, bgk)


GLA_BLOCK = 512
CHUNKS_PER_BLOCK = GLA_BLOCK // CHUNK


def _chunk_masks():
    t = lax.broadcasted_iota(jnp.int32, (CHUNK, CHUNK), 0)
    u = lax.broadcasted_iota(jnp.int32, (CHUNK, CHUNK), 1)
    return t >= u, t <= u


def _gla_chunk_terms(q, cum):
    ep = jnp.exp(cum)
    en = jnp.exp(-cum)
    qs = q * (HEAD_K ** -0.5)
    last = cum[CHUNK - 1:CHUNK, :]
    ed = jnp.exp(last - cum)
    dec = jnp.exp(last)
    return ep, en, qs, ed, dec


def gla_forward(qk, v, cum):
    s = qk.shape[0]
    nb = s // GLA_BLOCK
    nc = s // CHUNK

    def body(q_ref, k_ref, v_ref, cum_ref, o_ref, st_ref, sc_ref, state):
        @pl.when(pl.program_id(0) == 0)
        def _():
            state[...] = jnp.zeros_like(state)

        lower, _ = _chunk_masks()

        def chunk(cc, carry):
            rows = pl.ds(pl.multiple_of(cc * CHUNK, CHUNK), CHUNK)
            for h in range(HEADS):
                kc = slice(h * HEAD_K, (h + 1) * HEAD_K)
                vc = slice(h * HEAD_V, (h + 1) * HEAD_V)
                q = q_ref[rows, kc]
                k = k_ref[rows, kc]
                v = v_ref[rows, vc]
                ep, en, qs, ed, dec = _gla_chunk_terms(q, cum_ref[rows, kc])
                a = _bf(qs * ep)
                fwd = _nt(a, _bf(k * en))
                bwd = _nt(_bf(qs * en), _bf(k * ep))
                scores = _bf(jnp.where(lower, fwd, bwd))
                sc_ref[rows, h * CHUNK:(h + 1) * CHUNK] = scores
                st = state[h]
                st_ref[cc, h] = st
                o_ref[rows, vc] = _nn(scores, v) + _nt(a, _bf(st))
                state[h] = st * dec + _tn(v, _bf(k * ed))
            return carry

        lax.fori_loop(0, CHUNKS_PER_BLOCK, chunk, 0, unroll=4)

    return pl.pallas_call(
        body, name="gla_forward", grid=(nb,),
        out_shape=(jax.ShapeDtypeStruct((s, D), F32),
                   jax.ShapeDtypeStruct((nc, HEADS, HEAD_V, HEAD_K), F32),
                   jax.ShapeDtypeStruct((s, HEADS * CHUNK), BF16)),
        in_specs=[pl.BlockSpec((GLA_BLOCK, KEY_W), lambda i: (i, 0)),
                  pl.BlockSpec((GLA_BLOCK, KEY_W), lambda i: (i, 1)),
                  pl.BlockSpec((GLA_BLOCK, D), lambda i: (i, 0)),
                  pl.BlockSpec((GLA_BLOCK, KEY_W), lambda i: (i, 0))],
        out_specs=(pl.BlockSpec((GLA_BLOCK, D), lambda i: (i, 0)),
                   pl.BlockSpec((CHUNKS_PER_BLOCK, HEADS, HEAD_V, HEAD_K), lambda i: (i, 0, 0, 0)),
                   pl.BlockSpec((GLA_BLOCK, HEADS * CHUNK), lambda i: (i, 0))),
        scratch_shapes=[pltpu.VMEM((HEADS, HEAD_V, HEAD_K), F32)],
        compiler_params=_params("arbitrary"),
    )(qk, qk, v, cum)


def gla_backward(qk, v, cum, do, states, scores):
    s = qk.shape[0]
    nb = s // GLA_BLOCK

    def body(q_ref, k_ref, v_ref, cum_ref, do_ref, st_ref, sc_ref, dq_ref, dk_ref, dv_ref, dcum_ref, dstate):
        @pl.when(pl.program_id(0) == 0)
        def _():
            dstate[...] = jnp.zeros_like(dstate)

        lower, _ = _chunk_masks()
        is_last = lax.broadcasted_iota(jnp.int32, (CHUNK, HEAD_K), 0) == CHUNK - 1

        def chunk(step, carry):
            cc = CHUNKS_PER_BLOCK - 1 - step
            rows = pl.ds(pl.multiple_of(cc * CHUNK, CHUNK), CHUNK)
            for h in range(HEADS):
                kc = slice(h * HEAD_K, (h + 1) * HEAD_K)
                vc = slice(h * HEAD_V, (h + 1) * HEAD_V)
                q = q_ref[rows, kc]
                k = k_ref[rows, kc]
                v = v_ref[rows, vc]
                do_c = do_ref[rows, vc]
                ep, en, qs, ed, dec = _gla_chunk_terms(q, cum_ref[rows, kc])
                a = _bf(qs * ep)
                b = _bf(k * en)
                c = _bf(qs * en)
                dk_dec = _bf(k * ep)
                kd = _bf(k * ed)
                scores = sc_ref[rows, h * CHUNK:(h + 1) * CHUNK]
                st = st_ref[cc, h]
                dst = dstate[h]
                dst_bf = _bf(dst)

                dscores = _nt(do_c, v)
                dfwd = _bf(jnp.where(lower, dscores, 0.0))
                dbwd = _bf(jnp.where(lower, 0.0, dscores))
                dv_ref[rows, vc] = _bf(_tn(scores, do_c) + _nt(kd, dst_bf))
                da = _nn(dfwd, b) + _nn(do_c, _bf(st))
                db = _tn(dfwd, a)
                dc = _nn(dbwd, dk_dec)
                ddk = _tn(dbwd, c)
                dkd = _nn(v, dst_bf)
                ddec = jnp.sum(dst * st, axis=0, keepdims=True)
                dstate[h] = dst * dec + _tn(do_c, a)

                m = dkd * k * ed
                dq_ref[rows, kc] = _bf((da * ep + dc * en) * (HEAD_K ** -0.5))
                dk_ref[rows, kc] = _bf(db * en + ddk * ep + dkd * ed)
                dcum = (da * qs + ddk * k) * ep - (db * k + dc * qs) * en - m
                dlast = jnp.sum(m, axis=0, keepdims=True) + ddec * dec
                dcum_ref[rows, kc] = dcum + jnp.where(is_last, dlast, 0.0)
            return carry

        lax.fori_loop(0, CHUNKS_PER_BLOCK, chunk, 0, unroll=4)

    rev = lambda cols, col_block: pl.BlockSpec((GLA_BLOCK, cols), lambda i: (nb - 1 - i, col_block))
    return pl.pallas_call(
        body, name="gla_backward", grid=(nb,),
        out_shape=(jax.ShapeDtypeStruct((s, KEY_W), BF16), jax.ShapeDtypeStruct((s, KEY_W), BF16),
                   jax.ShapeDtypeStruct((s, D), BF16), jax.ShapeDtypeStruct((s, KEY_W), F32)),
        in_specs=[rev(KEY_W, 0), rev(KEY_W, 1), rev(D, 0), rev(KEY_W, 0), rev(D, 0),
                  pl.BlockSpec((CHUNKS_PER_BLOCK, HEADS, HEAD_V, HEAD_K), lambda i: (nb - 1 - i, 0, 0, 0)),
                  rev(HEADS * CHUNK, 0)],
        out_specs=(rev(KEY_W, 0), rev(KEY_W, 0), rev(D, 0), rev(KEY_W, 0)),
        scratch_shapes=[pltpu.VMEM((HEADS, HEAD_V, HEAD_K), F32)],
        compiler_params=_params("arbitrary"),
    )(qk, qk, v, cum, do, states, scores)


def head_and_loss(o, gate, h1, target, hw, wgo, wf):
    s = o.shape[0]
    ts = ROW_TILE

    def body(o_ref, gate_ref, h1_ref, tgt_ref, hw_ref, wgo_ref, wf_ref,
             dh2_ref, do_ref, dgate_ref, ggo_ref, small_ref):
        @pl.when(pl.program_id(0) == 0)
        def _():
            ggo_ref[...] = jnp.zeros_like(ggo_ref)
            small_ref[...] = jnp.zeros_like(small_ref)

        gate = gate_ref[...]
        hw = hw_ref[...]
        sg = _sigmoid(gate)
        silu = gate * sg
        ohat, ro = [], []
        for h in range(HEADS):
            oh = o_ref[:, h * HEAD_V:(h + 1) * HEAD_V]
            rh = lax.rsqrt(jnp.mean(oh * oh, axis=-1, keepdims=True) + EPS)
            ro.append(rh)
            ohat.append(oh * rh)
        ohat = jnp.concatenate(ohat, axis=-1)
        on = ohat * hw
        y2 = _bf(on * silu)
        h2 = h1_ref[...] + _nn(y2, wgo_ref[...])
        rf = lax.rsqrt(jnp.mean(h2 * h2, axis=-1, keepdims=True) + EPS)
        h2hat = h2 * rf
        wf = wf_ref[...]
        diff = h2hat * wf - tgt_ref[...]
        small_ref[2:3, :] += jnp.zeros((1, D), F32) + 0.5 * jnp.sum(diff * diff) / D
        dout = diff / D
        small_ref[0:1, :] += jnp.sum(dout * h2hat, axis=0, keepdims=True)
        dxh = dout * wf
        dh2 = rf * (dxh - h2hat * jnp.mean(dxh * h2hat, axis=-1, keepdims=True))
        dh2_ref[...] = dh2
        dh2_bf = _bf(dh2)
        ggo_ref[...] += _tn(y2, dh2_bf)
        dy2 = _nt(dh2_bf, wgo_ref[...])
        don = dy2 * silu
        dgate_ref[...] = _bf(dy2 * on * (sg * (1.0 + gate * (1.0 - sg))))
        ghw = jnp.sum(don * ohat, axis=0, keepdims=True)
        small_ref[1:2, 0:HEAD_V] += sum(ghw[:, h * HEAD_V:(h + 1) * HEAD_V] for h in range(HEADS))
        dohat = don * hw
        for h in range(HEADS):
            cols = slice(h * HEAD_V, (h + 1) * HEAD_V)
            oh, dh = ohat[:, cols], dohat[:, cols]
            do_ref[:, cols] = _bf(ro[h] * (dh - oh * jnp.mean(dh * oh, axis=-1, keepdims=True)))

    row = lambda cols: pl.BlockSpec((ts, cols), lambda i: (i, 0))
    act = jax.ShapeDtypeStruct((s, D), F32)
    act_bf = jax.ShapeDtypeStruct((s, D), BF16)
    return pl.pallas_call(
        body, name="head_and_loss", grid=(s // ts,),
        out_shape=(act, act_bf, act_bf, jax.ShapeDtypeStruct((D, D), F32), jax.ShapeDtypeStruct((8, D), F32)),
        in_specs=[row(D), row(D), row(D), row(D),
                  _full((1, D)), _full((D, D)), _full((1, D))],
        out_specs=(row(D), row(D), row(D), _full((D, D)), _full((8, D))),
        compiler_params=_params("arbitrary"),
    )(o, gate, h1, target, hw, wgo, wf)


def gla_project_backward(dq, dk, dv, dgate, dcum, low, h1, dh2, w1, wgi, wlow, wgk, bgk):
    s = h1.shape[0]
    ts = ROW_TILE

    def body(dq_ref, dk_ref, dv_ref, dgate_ref, dcum_ref, low_ref, h1_ref, dh2_ref, w1_ref,
             wgi_ref, wlow_ref, wgk_ref, bgk_ref, dh1_ref, dproj_ref, dlow_ref, ggk_ref, small_ref):
        @pl.when(pl.program_id(0) == 0)
        def _():
            ggk_ref[...] = jnp.zeros_like(ggk_ref)
            small_ref[...] = jnp.zeros_like(small_ref)

        low = low_ref[...]
        z = _nn(low, wgk_ref[...]) + bgk_ref[...]
        upper_f = _chunk_masks()[1].astype(F32)
        dlg = jnp.concatenate([_nn_exact(upper_f, dcum_ref[r0:r0 + CHUNK, :]) for r0 in range(0, ts, CHUNK)],
                              axis=0)
        dz = dlg * (1.0 / GATE_NORM) * _sigmoid(-z)
        dz_bf = _bf(dz)
        ggk_ref[...] += _tn(low, dz_bf)
        small_ref[1:2, 0:KEY_W] += jnp.sum(dz, axis=0, keepdims=True)
        dlow = _bf(_nt(dz_bf, wgk_ref[...]))
        dlow_ref[...] = dlow
        dn1 = _nt(dlow, wlow_ref[...])
        for ref, lo, hi in ((dq_ref, 0, KEY_W), (dk_ref, KEY_W, 2 * KEY_W),
                            (dv_ref, 2 * KEY_W, 2 * KEY_W + D), (dgate_ref, 2 * KEY_W + D, GLA_MAIN)):
            piece = ref[...]
            dproj_ref[:, lo:hi] = piece
            dn1 = dn1 + _nt(piece, wgi_ref[:, lo:hi])
        hv = h1_ref[...]
        r = lax.rsqrt(jnp.mean(hv * hv, axis=-1, keepdims=True) + EPS)
        hhat = hv * r
        small_ref[0:1, :] += jnp.sum(dn1 * hhat, axis=0, keepdims=True)
        dxh = dn1 * w1_ref[...]
        dh1_ref[...] = dh2_ref[...] + r * (dxh - hhat * jnp.mean(dxh * hhat, axis=-1, keepdims=True))

    row = lambda cols: pl.BlockSpec((ts, cols), lambda i: (i, 0))
    return pl.pallas_call(
        body, name="gla_project_backward", grid=(s // ts,),
        out_shape=(jax.ShapeDtypeStruct((s, D), F32), jax.ShapeDtypeStruct((s, GLA_MAIN), BF16),
                   jax.ShapeDtypeStruct((s, RANK_PAD), BF16), jax.ShapeDtypeStruct((RANK_PAD, KEY_W), F32),
                   jax.ShapeDtypeStruct((8, D), F32)),
        in_specs=[row(KEY_W), row(KEY_W), row(D), row(D), row(KEY_W), row(RANK_PAD), row(D), row(D),
                  _full((1, D)), _full((D, GLA_MAIN)), _full((D, RANK_PAD)), _full((RANK_PAD, KEY_W)),
                  _full((1, KEY_W))],
        out_specs=(row(D), row(GLA_MAIN), row(RANK_PAD), _full((RANK_PAD, KEY_W)), _full((8, D))),
        compiler_params=_params("arbitrary"),
    )(dq, dk, dv, dgate, dcum, low, h1, dh2, w1, wgi, wlow, wgk, bgk)


def _groups_from_quarters(a):
    return a.reshape(N_CHIPS, GROUPS, 64, GROUP_DIM).transpose(1, 0, 2, 3).reshape(GROUPS, GROUP_DIM, GROUP_DIM)


def _quarters_from_groups(a):
    return a.reshape(GROUPS, N_CHIPS, 64, GROUP_DIM).transpose(1, 0, 2, 3).reshape(N_CHIPS, GROUP_DIM, GROUP_DIM)


def _pad_row(*pieces):
    flat = jnp.concatenate([p.reshape(-1).astype(F32) for p in pieces])
    return jnp.pad(flat, (0, D - flat.shape[0])).reshape(1, D)


def _gla_weights(wgi_q, wgo_q):
    wgi_all = jnp.concatenate([wgi_q[q] for q in range(N_CHIPS)], axis=1)
    wlow = jnp.pad(wgi_all[:, GLA_MAIN:], ((0, 0), (0, RANK_PAD - GATE_RANK)))
    return wgi_all, wlow, wgo_q.reshape(D, D)


def _small_sums(small_top, small_gla, small_pool, g_gk_pad):
    return jnp.concatenate([
        small_pool[0:1], small_gla[0:1],
        small_pool[1:2],
        small_top[0:1],
        small_top[2:3],
        _pad_row(small_gla[1, 0:KEY_W], small_top[1, 0:HEAD_V]),
        small_pool[2:3],
        g_gk_pad[:GATE_RANK].reshape(8, D),
        jnp.zeros((1, D), F32)], axis=0)


def local_gradients(xs, target, w0, w1, wf, wpi, gw, gb, scale, wpo, gla_quarters, wgk, bgk, hw_tiled, place):
    (h1, pooled, gt, n0), gla_all = pool_forward(xs, w0, wpi, gw, gb, scale, wpo, gla_quarters)
    wgi, wlow, wgo = _gla_weights(*gla_all)
    qk, v, gate, low, cum, n1 = gla_project(h1, w1, wgi, wlow, wgk, bgk)
    o, states, scores = gla_forward(qk, v, cum)

    dh2, do, dgate, g_gla_out, small_top = head_and_loss(o, gate, h1, target, hw_tiled, wgo, wf)
    dq, dk, dv, dcum = gla_backward(qk, v, cum, do, states, scores)
    dh1, dproj, dlow, g_gk_pad, small_gla = gla_project_backward(
        dq, dk, dv, dgate, dcum, low, h1, dh2, w1, wgi, wlow, wgk, bgk)
    g_gla_in = jnp.concatenate([matmul_tn(n1, dproj, "grad_gla_in"),
                                matmul_tn(n1, dlow, "grad_gla_low")[:, :GATE_RANK]], axis=1)

    def chip_sums(grads, names, tag):
        theirs = exchange_with_sibling(grads, "exchange_with_sibling_" + tag)
        return [add_halves(g, t, place, "add_halves_" + n) for g, t, n in zip(grads, theirs, names)]

    gla_names = ("gla_in", "gla_out")
    gla_sums = chip_sums(
        [jnp.stack([g_gla_in[:, GLA_IN_QUARTER * q:GLA_IN_QUARTER * (q + 1)] for q in range(N_CHIPS)]),
         g_gla_out.reshape(N_CHIPS, D // N_CHIPS, D)], gla_names, "gla")
    (dx, dpool, g_pool_out, g_group_w, small_pool), gla_got = pool_backward(
        xs, dh1, pooled, gt, w0, wpi, gw, gb, scale, wpo, [b for _, b in gla_sums])
    g_pool_in = matmul_tn(n0, dpool, "grad_pool_in", by_column_tile=True)

    pool_names = ("pool_in", "group", "pool_out")
    pool_sums = chip_sums(
        [g_pool_in, _quarters_from_groups(g_group_w), g_pool_out.reshape(N_CHIPS, D // N_CHIPS, D)],
        pool_names, "pool")
    pool_got = scatter_to_owners([b for _, b in pool_sums], "scatter_to_owners_pool")
    reduced, total = join_halves(
        [add_parts(f, g, place, "add_parts_" + n) for (f, _), g, n in
         zip(pool_sums + gla_sums, list(pool_got) + list(gla_got), pool_names + gla_names)],
        _small_sums(small_top, small_gla, small_pool, g_gk_pad))
    return dx, reduced, total


def kernel(x, norm_w, pool_in_w, pool_group_w, pool_group_b, pool_scale, pool_out_w, gla_in_w, gla_gk_w, gla_gk_b, gla_head_norm_w, gla_out_w, final_norm_w, loss_target, m_norm_w, m_pool_in_w, m_pool_group_w, m_pool_group_b, m_pool_scale, m_pool_out_w, m_gla_in_w, m_gla_gk_w, m_gla_gk_b, m_gla_head_norm_w, m_gla_out_w, m_final_norm_w, v_norm_w, v_pool_in_w, v_pool_group_w, v_pool_group_b, v_pool_scale, v_pool_out_w, v_gla_in_w, v_gla_gk_w, v_gla_gk_b, v_gla_head_norm_w, v_gla_out_w, v_final_norm_w):
    s = x.shape[1]
    xs = x[0]
    target = loss_target[0]
    q_chip = 2 * lax.axis_index("x") + lax.axis_index("y")
    place = jnp.stack([lax.axis_index("c"), q_chip]).astype(jnp.int32)

    small_in = jnp.concatenate([
        _pad_row(gla_gk_b[0], gla_head_norm_w[0], pool_group_b[0]),
        gla_gk_w[0].reshape(2, D),
        jnp.zeros((5, D), F32)], axis=0)
    (wpi, gw_q, wpo_q, wgi_q, wgo_q), small_all = allgather_weights(
        [pool_in_w[0], pool_group_w[0].reshape(GROUP_DIM, GROUP_DIM), pool_out_w[0], gla_in_w[0], gla_out_w[0]],
        exchange=(True, True, True, False, False), small=small_in)
    gw = _groups_from_quarters(gw_q)
    wpo = wpo_q.reshape(D, D)
    small_all = small_all[0::2]
    bgk = small_all[:, 0, 0:128].reshape(1, KEY_W)
    hw = small_all[:, 0, 128:192].reshape(1, HEAD_V)
    gb = jnp.concatenate([small_all[q, 0, 192:448].reshape(GROUPS, 64) for q in range(N_CHIPS)],
                         axis=1).reshape(1, D)
    wgk16 = jnp.concatenate([small_all[q, 1:3].reshape(GATE_RANK, 128) for q in range(N_CHIPS)], axis=1)
    wgk = _bf(jnp.pad(wgk16, ((0, RANK_PAD - GATE_RANK), (0, 0))))
    hw_tiled = jnp.tile(hw, (1, HEADS))

    w0 = norm_w[0:1]
    w1 = norm_w[1:2]
    wf = final_norm_w.reshape(1, D)

    dx, reduced, total = local_gradients(
        xs, target, w0, w1, wf, wpi, gw, gb, pool_scale, wpo, [wgi_q, wgo_q], wgk, bgk, hw_tiled, place)
    r_pool_in, r_group_w, r_pool_out, r_gla_in, r_gla_out = reduced
    r_group_w = r_group_w.reshape(GROUPS, 64, GROUP_DIM)

    loss = total[4, 0]
    g_norm = total[0:2]
    g_scale = total[2:3]
    g_final = total[3]
    pick = lambda full, width: lax.dynamic_slice_in_dim(full, q_chip * width, width, axis=-1)
    g_gk_b = pick(total[5:6, 0:KEY_W], 128)
    g_hnw = pick(total[5:6, KEY_W:KEY_W + HEAD_V], 64)
    g_group_b = pick(total[6].reshape(GROUPS, GROUP_DIM), 64)[None]
    g_gk_w = pick(total[7:15].reshape(GATE_RANK, KEY_W), 128)[None]

    def step_lane_rows(name, w, g, m, v):
        turn = lambda a: jnp.transpose(a, (2, 0, 1))
        back = lambda a: jnp.transpose(a, (1, 2, 0))
        g_t = turn(g)
        d, nm, nv = adamw_rows(turn(w), g_t, turn(m), turn(v), "adamw_" + name)
        return back(g_t), back(d), back(nm), back(nv)

    def step(name, w, g, m, v):
        shape = w.shape
        as2d = lambda a: a.reshape(-1, shape[-1])
        d, nm, nv = adamw(as2d(w), as2d(g), as2d(m), as2d(v), "adamw_" + name)
        return g.reshape(shape), d.reshape(shape), nm.reshape(shape), nv.reshape(shape)

    small_names = ("norm_w", "pool_group_b", "pool_scale", "gla_gk_w", "gla_gk_b", "gla_head_norm_w",
                   "final_norm_w")
    small_args = [(norm_w, g_norm, m_norm_w, v_norm_w),
                  (pool_group_b, g_group_b, m_pool_group_b, v_pool_group_b),
                  (pool_scale, g_scale, m_pool_scale, v_pool_scale),
                  (gla_gk_w, g_gk_w, m_gla_gk_w, v_gla_gk_w),
                  (gla_gk_b, g_gk_b, m_gla_gk_b, v_gla_gk_b),
                  (gla_head_norm_w, g_hnw, m_gla_head_norm_w, v_gla_head_norm_w),
                  (final_norm_w, g_final, m_final_norm_w, v_final_norm_w)]
    as2d = lambda a, w: a.reshape(-1, w.shape[-1])
    small_out = adamw_small([tuple(as2d(a, p[0]) for a in p) for p in small_args])
    small = {n: (p[1].reshape(p[0].shape),) + tuple(o.reshape(p[0].shape) for o in out)
             for n, p, out in zip(small_names, small_args, small_out)}
    results = [
        small["norm_w"],
        step("pool_in_w", pool_in_w, r_pool_in[None], m_pool_in_w, v_pool_in_w),
        step("pool_group_w", pool_group_w, r_group_w[None], m_pool_group_w, v_pool_group_w),
        small["pool_group_b"],
        small["pool_scale"],
        step("pool_out_w", pool_out_w, r_pool_out[None], m_pool_out_w, v_pool_out_w),
        step_lane_rows("gla_in_w", gla_in_w, r_gla_in[None], m_gla_in_w, v_gla_in_w),
        small["gla_gk_w"],
        small["gla_gk_b"],
        small["gla_head_norm_w"],
        step("gla_out_w", gla_out_w, r_gla_out[None], m_gla_out_w, v_gla_out_w),
        small["final_norm_w"],
    ]
    grads, deltas, new_m, new_v = zip(*results)
    return (loss, dx[None], *grads, *deltas, *new_m, *new_v)
```

```python
import functools

import jax
import jax.numpy as jnp
from jax import lax
from jax.experimental import pallas as pl
from jax.experimental.pallas import tpu as pltpu

F32 = jnp.float32
BF16 = jnp.bfloat16
MESH = pl.DeviceIdType.MESH

D = 1024
POOL_WINDOWS = (2, 4, 8, 16)
GROUPS = 4
GROUP_DIM = 256
HEADS = 4
HEAD_K = 128
HEAD_V = 256
KEY_W = 512
CHUNK = 64
GATE_RANK = 16
GATE_NORM = 16.0
GLA_IN = 3088
GLA_MAIN = 3072
RANK_PAD = 128
EPS = 1e-6
HALO = 16

ADAM_LR = 0.001
ADAM_B1 = 0.9
ADAM_B2 = 0.999
ADAM_EPS = 1e-08
ADAM_WD = 0.01
ADAM_STEP = 10

N_CHIPS = 4
N_DEV = 8
GLA_IN_QUARTER = GLA_IN // N_CHIPS

VMEM_LIMIT = 56 * 1024 * 1024


def _nn(a, b):
    return lax.dot_general(a, b, (((1,), (0,)), ((), ())), preferred_element_type=F32)


def _nt(a, b):
    return lax.dot_general(a, b, (((1,), (1,)), ((), ())), preferred_element_type=F32)


def _tn(a, b):
    return lax.dot_general(a, b, (((0,), (0,)), ((), ())), preferred_element_type=F32)


def _nn_exact(a, b):
    return lax.dot_general(a, b, (((1,), (0,)), ((), ())), preferred_element_type=F32,
                           precision=lax.Precision.HIGHEST)


def _bf(a):
    return a.astype(BF16)


def _params(*sem):
    return pltpu.CompilerParams(dimension_semantics=sem, vmem_limit_bytes=VMEM_LIMIT)


def _full(shape):
    return pl.BlockSpec(shape, lambda i: (0,) * len(shape))


def _position():
    return lax.axis_index("x"), lax.axis_index("y"), lax.axis_index("c")


def _gather_small(in_ref, all_ref, send_sems, recv_sems, local_sem):
    x, y, c = _position()
    me = 4 * x + 2 * y + c
    mine = pltpu.make_async_copy(in_ref, all_ref.at[me], local_sem)
    mine.start()
    sends = []
    for k in range(N_DEV - 1):
        fx, fy, fc = (k + 1) >> 2 & 1, (k + 1) >> 1 & 1, (k + 1) & 1
        cp = pltpu.make_async_remote_copy(
            src_ref=in_ref, dst_ref=all_ref.at[me],
            send_sem=send_sems.at[k], recv_sem=recv_sems.at[k],
            device_id=(x ^ fx, y ^ fy, c ^ fc), device_id_type=MESH)
        cp.start()
        sends.append(cp)
    for k in range(N_DEV - 1):
        fx, fy, fc = (k + 1) >> 2 & 1, (k + 1) >> 1 & 1, (k + 1) & 1
        src_dev = 4 * (x ^ fx) + 2 * (y ^ fy) + (c ^ fc)
        pltpu.make_async_remote_copy(
            src_ref=in_ref, dst_ref=all_ref.at[src_dev],
            send_sem=send_sems.at[k], recv_sem=recv_sems.at[k],
            device_id=(x, y, c), device_id_type=MESH).wait_recv()
    for cp in sends:
        cp.wait_send()
    mine.wait()


SMALL_SEMS = [pltpu.SemaphoreType.DMA((N_DEV - 1,)), pltpu.SemaphoreType.DMA((N_DEV - 1,)),
              pltpu.SemaphoreType.DMA]
VMEM_SPEC = pl.BlockSpec(memory_space=pltpu.VMEM)


def _other_chips(x, y):
    return [(1 - x, y), (x, 1 - y), (1 - x, 1 - y)]


def _any_specs(n):
    return [pl.BlockSpec(memory_space=pl.ANY)] * n


def _halves(rows, c):
    half = rows // 2
    return pl.ds(c * half, half), pl.ds((1 - c) * half, half)


CAST_ROWS = 256


def _gather_copy(out_ref, send_sems, recv_sems, k, quarter, half, to, src=None):
    dst = out_ref.at[quarter, half]
    return pltpu.make_async_remote_copy(
        src_ref=dst if src is None else src, dst_ref=dst,
        send_sem=send_sems.at[k], recv_sem=recv_sems.at[k], device_id=to, device_id_type=MESH)


def allgather_weights(quarters, exchange, small):
    n = len(quarters)
    shapes = [w.shape for w in quarters]
    moved = [i for i in range(n) if exchange[i]]

    def body(*refs):
        w_refs, small_ref = refs[:n], refs[n]
        out_refs, small_all_ref = refs[n + 1:2 * n + 1], refs[2 * n + 1]
        refs = refs[2 * n + 2:]
        f32_bufs, bf_bufs = refs[:n], refs[n:2 * n]
        send_sems, recv_sems, local_sems = refs[2 * n:2 * n + 3]
        x, y, c = _position()
        q = 2 * x + y
        sibling = (x, y, 1 - c)
        chips = _other_chips(x, y)

        def copy(k, i, quarter, half, to, src=None):
            return _gather_copy(out_refs[i], send_sems, recv_sems, k * n + i, quarter, half, to, src)

        loads = [pltpu.make_async_copy(w_refs[i], f32_bufs[i], local_sems.at[i]) for i in range(n)]
        for cp in loads:
            cp.start()
        keeps, sends = [], []
        for i in range(n):
            loads[i].wait()
            for r0 in range(0, shapes[i][0], CAST_ROWS):
                bf_bufs[i][r0:r0 + CAST_ROWS, :] = _bf(f32_bufs[i][r0:r0 + CAST_ROWS, :])
            keep = pltpu.make_async_copy(bf_bufs[i], out_refs[i].at[q], local_sems.at[n + i])
            keep.start()
            keeps.append(keep)
            if not exchange[i]:
                continue
            mine, _ = _halves(shapes[i][0], c)
            for j, chip in enumerate(chips):
                cp = copy(j, i, q, mine, (*chip, c), src=bf_bufs[i].at[mine])
                cp.start()
                sends.append(cp)
        for j, chip in enumerate(chips):
            qj = 2 * chip[0] + chip[1]
            for i in moved:
                mine, _ = _halves(shapes[i][0], c)
                copy(j, i, qj, mine, (x, y, c)).wait_recv()
                cp = copy(3 + j, i, qj, mine, sibling)
                cp.start()
                sends.append(cp)
        for j, chip in enumerate(chips):
            qj = 2 * chip[0] + chip[1]
            for i in moved:
                _, other = _halves(shapes[i][0], c)
                copy(3 + j, i, qj, other, (x, y, c)).wait_recv()
        _gather_small(small_ref, small_all_ref, *refs[2 * n + 3:])
        for cp in sends:
            cp.wait_send()
        for cp in keeps:
            cp.wait()

    outs = pl.pallas_call(
        body, name="allgather_weights",
        out_shape=[jax.ShapeDtypeStruct((N_CHIPS, *s), BF16) for s in shapes]
                  + [jax.ShapeDtypeStruct((N_DEV, *small.shape), small.dtype)],
        in_specs=_any_specs(n) + [VMEM_SPEC], out_specs=_any_specs(n) + [VMEM_SPEC],
        scratch_shapes=([pltpu.VMEM(s, F32) for s in shapes] + [pltpu.VMEM(s, BF16) for s in shapes]
                        + [pltpu.SemaphoreType.DMA((6 * n,)), pltpu.SemaphoreType.DMA((6 * n,)),
                           pltpu.SemaphoreType.DMA((2 * n,))] + SMALL_SEMS),
        compiler_params=pltpu.CompilerParams(vmem_limit_bytes=VMEM_LIMIT),
    )(*quarters, small)
    return outs[:n], outs[n]


def exchange_with_sibling(grads, name):
    n = len(grads)

    def body(*refs):
        g_refs, theirs_refs = refs[:n], refs[n:2 * n]
        send_sems, recv_sems = refs[2 * n:]
        x, y, c = _position()
        copies = []
        for i in range(n):
            _, other = _halves(g_refs[i].shape[1], c)
            cp = pltpu.make_async_remote_copy(
                src_ref=g_refs[i].at[:, other], dst_ref=theirs_refs[i],
                send_sem=send_sems.at[i], recv_sem=recv_sems.at[i],
                device_id=(x, y, 1 - c), device_id_type=MESH)
            cp.start()
            copies.append(cp)
        for cp in copies:
            cp.wait()

    return pl.pallas_call(
        body, name=name,
        out_shape=[jax.ShapeDtypeStruct((N_CHIPS, g.shape[1] // 2, g.shape[2]), F32) for g in grads],
        in_specs=_any_specs(n), out_specs=_any_specs(n),
        scratch_shapes=[pltpu.SemaphoreType.DMA((n,)), pltpu.SemaphoreType.DMA((n,))],
    )(*grads)


def _scatter_copies(b_refs, got_refs, send_sems, recv_sems):
    n = len(b_refs)
    x, y, c = _position()
    copies = []
    for j, chip in enumerate(_other_chips(x, y)):
        qj = 2 * chip[0] + chip[1]
        for i in range(n):
            copies.append(pltpu.make_async_remote_copy(
                src_ref=b_refs[i].at[qj], dst_ref=got_refs[i].at[j],
                send_sem=send_sems.at[j * n + i], recv_sem=recv_sems.at[j * n + i],
                device_id=(*chip, c), device_id_type=MESH))
    return copies


def _scatter_shapes(chip_sums):
    return [jax.ShapeDtypeStruct((N_CHIPS - 1, *b.shape[1:]), BF16) for b in chip_sums]


def scatter_to_owners(chip_sums, name):
    n = len(chip_sums)

    def body(*refs):
        copies = _scatter_copies(refs[:n], refs[n:2 * n], *refs[2 * n:])
        for cp in copies:
            cp.start()
        for cp in copies:
            cp.wait()

    return pl.pallas_call(
        body, name=name,
        out_shape=_scatter_shapes(chip_sums),
        in_specs=_any_specs(n), out_specs=_any_specs(n),
        scratch_shapes=[pltpu.SemaphoreType.DMA((3 * n,)), pltpu.SemaphoreType.DMA((3 * n,))],
    )(*chip_sums)


def join_halves(reduced, small):
    n = len(reduced)

    def body(*refs):
        small_ref = refs[n]
        buf_refs, total_ref = refs[n + 1:2 * n + 1], refs[2 * n + 1]
        send_sems, recv_sems, all_ref = refs[2 * n + 2:2 * n + 5]
        x, y, c = _position()
        copies = []
        for i in range(n):
            mine, _ = _halves(buf_refs[i].shape[0], c)
            cp = pltpu.make_async_remote_copy(
                src_ref=buf_refs[i].at[mine], dst_ref=buf_refs[i].at[mine],
                send_sem=send_sems.at[i], recv_sem=recv_sems.at[i],
                device_id=(x, y, 1 - c), device_id_type=MESH)
            cp.start()
            copies.append(cp)
        _gather_small(small_ref, all_ref, *refs[2 * n + 5:])
        total = all_ref[0]
        for dev in range(1, N_DEV):
            total = total + all_ref[dev]
        total_ref[...] = total
        for cp in copies:
            cp.wait()

    outs = pl.pallas_call(
        body, name="join_halves",
        out_shape=[jax.ShapeDtypeStruct(r.shape, F32) for r in reduced]
                  + [jax.ShapeDtypeStruct(small.shape, small.dtype)],
        in_specs=_any_specs(n) + [VMEM_SPEC], out_specs=_any_specs(n) + [VMEM_SPEC],
        input_output_aliases={i: i for i in range(n)},
        scratch_shapes=[pltpu.SemaphoreType.DMA((n,)), pltpu.SemaphoreType.DMA((n,)),
                        pltpu.VMEM((N_DEV, *small.shape), small.dtype)] + SMALL_SEMS,
    )(*reduced, small)
    return outs[:n], outs[n]


ADD_ROWS = 256


def add_halves(grad, theirs, place, name):
    _, half, cols = theirs.shape
    rb = min(ADD_ROWS, half)
    steps = half // rb

    def body(place_ref, a_ref, b_ref, f_ref, h_ref):
        s = a_ref[...] + b_ref[...]
        f_ref[...] = s
        h_ref[...] = _bf(s)

    spec = pl.BlockSpec((1, rb, cols), lambda i, j, place: (i, j, 0))
    return pl.pallas_call(
        body, name=name,
        grid_spec=pltpu.PrefetchScalarGridSpec(
            num_scalar_prefetch=1, grid=(N_CHIPS, steps),
            in_specs=[pl.BlockSpec((1, rb, cols), lambda i, j, place: (i, place[0] * steps + j, 0)), spec],
            out_specs=(spec, spec)),
        out_shape=(jax.ShapeDtypeStruct(theirs.shape, F32), jax.ShapeDtypeStruct(theirs.shape, BF16)),
        compiler_params=_params("parallel", "parallel"),
    )(place, grad, theirs)


def add_parts(chip_sum, got, place, name):
    _, half, cols = got.shape
    rb = min(ADD_ROWS, half)
    steps = half // rb

    def body(place_ref, o_ref, g_ref, out_ref):
        s = o_ref[0]
        for j in range(N_CHIPS - 1):
            s = s + g_ref[j].astype(F32)
        out_ref[...] = s

    return pl.pallas_call(
        body, name=name,
        grid_spec=pltpu.PrefetchScalarGridSpec(
            num_scalar_prefetch=1, grid=(steps,),
            in_specs=[pl.BlockSpec((1, rb, cols), lambda j, place: (place[1], j, 0)),
                      pl.BlockSpec((N_CHIPS - 1, rb, cols), lambda j, place: (0, j, 0))],
            out_specs=pl.BlockSpec((rb, cols), lambda j, place: (place[0] * steps + j, 0))),
        out_shape=jax.ShapeDtypeStruct((2 * half, cols), F32),
        compiler_params=_params("parallel"),
    )(place, chip_sum, got)


def _adam_math(w, g, m, v):
    m = ADAM_B1 * m + (1.0 - ADAM_B1) * g
    v = ADAM_B2 * v + (1.0 - ADAM_B2) * (g * g)
    m_hat = m / (1.0 - ADAM_B1 ** ADAM_STEP)
    v_hat = v / (1.0 - ADAM_B2 ** ADAM_STEP)
    delta = -ADAM_LR * (m_hat / (jnp.sqrt(v_hat) + ADAM_EPS) + ADAM_WD * w)
    return delta, m, v


def adamw(w, g, m, v, name):
    rows, cols = w.shape
    fits = [t for t in range(8, rows, 8) if rows % t == 0 and t * cols * 4 <= 2 ** 20]
    tile = max(fits) if fits else rows

    def body(w_ref, g_ref, m_ref, v_ref, d_ref, nm_ref, nv_ref):
        d, nm, nv = _adam_math(w_ref[...], g_ref[...], m_ref[...], v_ref[...])
        d_ref[...] = d
        nm_ref[...] = nm
        nv_ref[...] = nv

    spec = pl.BlockSpec((tile, cols), lambda i: (i, 0))
    shape = jax.ShapeDtypeStruct((rows, cols), F32)
    return pl.pallas_call(
        body, name=name, grid=(rows // tile,),
        out_shape=(shape, shape, shape),
        in_specs=[spec] * 4, out_specs=(spec, spec, spec),
        compiler_params=_params("parallel"),
    )(w, g, m, v)


def adamw_small(params):
    n = len(params)

    def body(*refs):
        ins, outs = refs[:4 * n], refs[4 * n:]
        for k in range(n):
            w_ref, g_ref, m_ref, v_ref = ins[4 * k:4 * k + 4]
            d, nm, nv = _adam_math(w_ref[...], g_ref[...], m_ref[...], v_ref[...])
            outs[3 * k][...] = d
            outs[3 * k + 1][...] = nm
            outs[3 * k + 2][...] = nv

    flat = [a for p in params for a in p]
    outs = pl.pallas_call(
        body, name="adamw_small",
        out_shape=[jax.ShapeDtypeStruct(p[0].shape, F32) for p in params for _ in range(3)],
        in_specs=[VMEM_SPEC] * (4 * n), out_specs=[VMEM_SPEC] * (3 * n),
    )(*flat)
    return [tuple(outs[3 * k:3 * k + 3]) for k in range(n)]


def adamw_rows(w, g, m, v, name):
    rows, _, cols = w.shape
    tile = rows // 4

    def body(w_ref, g_ref, m_ref, v_ref, d_ref, nm_ref, nv_ref):
        d, nm, nv = _adam_math(w_ref[...], g_ref[...], m_ref[...], v_ref[...])
        d_ref[...] = d
        nm_ref[...] = nm
        nv_ref[...] = nv

    spec = pl.BlockSpec((tile, 1, cols), lambda i: (i, 0, 0))
    shape = jax.ShapeDtypeStruct(w.shape, F32)
    return pl.pallas_call(
        body, name=name, grid=(rows // tile,),
        out_shape=(shape, shape, shape),
        in_specs=[spec] * 4, out_specs=(spec, spec, spec),
        compiler_params=_params("parallel"),
    )(w, g, m, v)


def matmul_tn(a, b, name, tile_n=512, tile_s=2048, by_column_tile=False):
    s, m = a.shape
    n = b.shape[1]
    tile_n = min(tile_n, n)
    tile_s = min(tile_s, s)
    steps = s // tile_s
    if by_column_tile:
        out_shape = jax.ShapeDtypeStruct((n // tile_n, m, tile_n), F32)
        out_spec = pl.BlockSpec((None, m, tile_n), lambda j, k: (j, 0, 0))
    else:
        out_shape = jax.ShapeDtypeStruct((m, n), F32)
        out_spec = pl.BlockSpec((m, tile_n), lambda j, k: (0, j))

    def body(a_ref, b_ref, out_ref):
        k = pl.program_id(1)

        @pl.when(k == 0)
        def _():
            out_ref[...] = jnp.zeros_like(out_ref)

        out_ref[...] += _tn(a_ref[...], b_ref[...])

    return pl.pallas_call(
        body, name=name, grid=(n // tile_n, steps),
        out_shape=out_shape,
        in_specs=[pl.BlockSpec((tile_s, m), lambda j, k: (k, 0)),
                  pl.BlockSpec((tile_s, tile_n), lambda j, k: (k, j))],
        out_specs=out_spec,
        compiler_params=_params("parallel", "arbitrary"),
    )(a, b)


ROW_TILE = 512


def _row_index(tile, rows):
    return tile * rows + lax.broadcasted_iota(jnp.int32, (rows, 1), 0)


def _inverse_counts(t_glob):
    return [1.0 / jnp.minimum(t_glob + 1, w).astype(F32) for w in POOL_WINDOWS]


def _sigmoid(z):
    return 1.0 / (1.0 + jnp.exp(-z))


def gather_in_background(step, last, out_refs, send_sems, recv_sems):
    n = len(out_refs)
    x, y, c = _position()
    q = 2 * x + y
    chips = _other_chips(x, y)

    def copy(k, i, quarter, half, to):
        return _gather_copy(out_refs[i], send_sems, recv_sems, k * n + i, quarter, half, to)

    @pl.when(step == 0)
    def _():
        for i in range(n):
            mine, _ = _halves(out_refs[i].shape[1], c)
            for j, chip in enumerate(chips):
                copy(j, i, q, mine, (*chip, c)).start()

    @pl.when(step == last - 1)
    def _():
        for j, chip in enumerate(chips):
            qj = 2 * chip[0] + chip[1]
            for i in range(n):
                mine, _ = _halves(out_refs[i].shape[1], c)
                copy(j, i, qj, mine, (x, y, c)).wait_recv()
                copy(3 + j, i, qj, mine, (x, y, 1 - c)).start()

    @pl.when(step == last)
    def _():
        for j, chip in enumerate(chips):
            qj = 2 * chip[0] + chip[1]
            for i in range(n):
                mine, other = _halves(out_refs[i].shape[1], c)
                copy(3 + j, i, qj, other, (x, y, c)).wait_recv()
                copy(j, i, q, mine, (x, y, c)).wait_send()
                copy(3 + j, i, qj, mine, (x, y, c)).wait_send()


def pool_forward(x, w0, wpi, gw, gb, scale, wpo, later):
    s = x.shape[0]
    ts = ROW_TILE
    nt = s // ts
    assert nt >= 2
    n_later = len(later)

    def body(x_ref, w0_ref, wpi_ref, gw_ref, gb_ref, sc_ref, wpo_ref, *rest):
        rest = rest[n_later:]
        h1_ref, pooled_ref, gt_ref, n0_ref = rest[:4]
        later_refs = rest[4:4 + n_later]
        ubuf, send_sems, recv_sems = rest[4 + n_later:]
        i = pl.program_id(0)
        gather_in_background(i, nt - 1, later_refs, send_sems, recv_sems)
        xv = x_ref[...]
        r = lax.rsqrt(jnp.mean(xv * xv, axis=-1, keepdims=True) + EPS)
        n0 = _bf(xv * r * w0_ref[...])
        n0_ref[...] = n0
        u = jnp.concatenate([_nn(n0, wpi_ref[0]), _nn(n0, wpi_ref[1])], axis=-1)
        gt = jnp.concatenate([_nn(n0, wpi_ref[2]), _nn(n0, wpi_ref[3])], axis=-1)
        gt_ref[...] = gt

        @pl.when(i == 0)
        def _():
            ubuf[0:HALO, :] = jnp.zeros((HALO, D), F32)

        ubuf[HALO:HALO + ts, :] = u
        inv = _inverse_counts(_row_index(i, ts))
        mixed = []
        for g, w in enumerate(POOL_WINDOWS):
            cols = slice(g * GROUP_DIM, (g + 1) * GROUP_DIM)
            ug = u[:, cols]
            acc = ug
            for j in range(1, w):
                acc = acc + ubuf[HALO - j:HALO - j + ts, cols]
            pooled = _bf(acc * inv[g] - ug)
            pooled_ref[:, cols] = pooled
            mixed.append(_nn(pooled, gw_ref[g]))
        ubuf[0:HALO, :] = ubuf[ts:ts + HALO, :]
        mixed = jnp.concatenate(mixed, axis=-1) + gb_ref[...]
        y = mixed * sc_ref[...] * (gt * _sigmoid(gt))
        h1_ref[...] = xv + _nn(_bf(y), wpo_ref[...])

    row = lambda cols: pl.BlockSpec((ts, cols), lambda i: (i, 0))
    outs = pl.pallas_call(
        body, name="pool_forward", grid=(nt,),
        out_shape=[jax.ShapeDtypeStruct((s, D), F32), jax.ShapeDtypeStruct((s, D), BF16),
                   jax.ShapeDtypeStruct((s, D), F32), jax.ShapeDtypeStruct((s, D), BF16)]
                  + [jax.ShapeDtypeStruct(a.shape, a.dtype) for a in later],
        in_specs=[row(D), _full((1, D)), _full((N_CHIPS, D, D // 2)), _full((GROUPS, GROUP_DIM, GROUP_DIM)),
                  _full((1, D)), _full((1, D)), _full((D, D))] + _any_specs(n_later),
        out_specs=[row(D), row(D), row(D), row(D)] + _any_specs(n_later),
        input_output_aliases={7 + k: 4 + k for k in range(n_later)},
        scratch_shapes=[pltpu.VMEM((HALO + ts, D), F32),
                        pltpu.SemaphoreType.DMA((6 * n_later,)), pltpu.SemaphoreType.DMA((6 * n_later,))],
        compiler_params=_params("arbitrary"),
    )(x, w0, wpi, gw, gb, scale, wpo, *later)
    return outs[:4], outs[4:]


def pool_backward(x, dh1, pooled, gt, w0, wpi, gw, gb, scale, wpo, chip_sums):
    s = x.shape[0]
    ts = ROW_TILE
    nt = s // ts
    n_sums = len(chip_sums)

    def body(x_ref, dh1_ref, pooled_ref, gt_ref, w0_ref, wpi_ref, gw_ref, gb_ref, sc_ref, wpo_ref, *rest):
        sum_refs, rest = rest[:n_sums], rest[n_sums:]
        dx_ref, dproj_ref, gpo_ref, ggw_ref, small_ref = rest[:5]
        got_refs = rest[5:5 + n_sums]
        ebuf, send_sems, recv_sems = rest[5 + n_sums:]
        i = pl.program_id(0)
        copies = _scatter_copies(sum_refs, got_refs, send_sems, recv_sems)

        @pl.when(i == 0)
        def _():
            for cp in copies:
                cp.start()

        @pl.when(i == nt - 1)
        def _():
            for cp in copies:
                cp.wait()

        @pl.when(i == 0)
        def _():
            gpo_ref[...] = jnp.zeros_like(gpo_ref)
            ggw_ref[...] = jnp.zeros_like(ggw_ref)
            small_ref[...] = jnp.zeros_like(small_ref)
            ebuf[ts:ts + HALO, :] = jnp.zeros((HALO, D), F32)

        dh1 = dh1_ref[...]
        dh1_bf = _bf(dh1)
        gt = gt_ref[...]
        sc = sc_ref[...]
        dy = _nt(dh1_bf, wpo_ref[...])
        pooled_bf = []
        mixed = []
        for g in range(GROUPS):
            cols = slice(g * GROUP_DIM, (g + 1) * GROUP_DIM)
            pb = pooled_ref[:, cols]
            pooled_bf.append(pb)
            mixed.append(_nn(pb, gw_ref[g]))
        mixed = jnp.concatenate(mixed, axis=-1) + gb_ref[...]
        sg = _sigmoid(gt)
        silu = gt * sg
        gpo_ref[...] += _tn(_bf(mixed * sc * silu), dh1_bf)
        dmixed = dy * sc * silu
        dgt = dy * mixed * sc * (sg * (1.0 + gt * (1.0 - sg)))
        dproj_ref[:, D:] = _bf(dgt)
        small_ref[1:2, :] += jnp.sum(dy * mixed * silu, axis=0, keepdims=True)
        small_ref[2:3, :] += jnp.sum(dmixed, axis=0, keepdims=True)

        inv = _inverse_counts(_row_index(nt - 1 - i, ts))
        dpooled = []
        for g in range(GROUPS):
            cols = slice(g * GROUP_DIM, (g + 1) * GROUP_DIM)
            dm = _bf(dmixed[:, cols])
            ggw_ref[g] += _tn(pooled_bf[g], dm)
            dp = _nt(dm, gw_ref[g])
            dpooled.append(dp)
            ebuf[0:ts, cols] = dp * inv[g]
        du = []
        for g, w in enumerate(POOL_WINDOWS):
            cols = slice(g * GROUP_DIM, (g + 1) * GROUP_DIM)
            acc = -dpooled[g]
            for j in range(w):
                acc = acc + ebuf[j:j + ts, cols]
            du.append(acc)
        ebuf[ts:ts + HALO, :] = ebuf[0:HALO, :]
        du = _bf(jnp.concatenate(du, axis=-1))
        dproj_ref[:, :D] = du
        dgt_bf = _bf(dgt)
        half = D // 2
        dn0 = (_nt(du[:, :half], wpi_ref[0]) + _nt(du[:, half:], wpi_ref[1])
               + _nt(dgt_bf[:, :half], wpi_ref[2]) + _nt(dgt_bf[:, half:], wpi_ref[3]))

        xv = x_ref[...]
        r = lax.rsqrt(jnp.mean(xv * xv, axis=-1, keepdims=True) + EPS)
        xhat = xv * r
        small_ref[0:1, :] += jnp.sum(dn0 * xhat, axis=0, keepdims=True)
        dxh = dn0 * w0_ref[...]
        dx_ref[...] = dh1 + r * (dxh - xhat * jnp.mean(dxh * xhat, axis=-1, keepdims=True))

    row = lambda cols: pl.BlockSpec((ts, cols), lambda i: (nt - 1 - i, 0))
    outs = pl.pallas_call(
        body, name="pool_backward", grid=(nt,),
        out_shape=[jax.ShapeDtypeStruct((s, D), F32), jax.ShapeDtypeStruct((s, 2 * D), BF16),
                   jax.ShapeDtypeStruct((D, D), F32),
                   jax.ShapeDtypeStruct((GROUPS, GROUP_DIM, GROUP_DIM), F32),
                   jax.ShapeDtypeStruct((8, D), F32)] + _scatter_shapes(chip_sums),
        in_specs=[row(D), row(D), row(D), row(D), _full((1, D)), _full((N_CHIPS, D, D // 2)),
                  _full((GROUPS, GROUP_DIM, GROUP_DIM)), _full((1, D)), _full((1, D)), _full((D, D))]
                 + _any_specs(n_sums),
        out_specs=[row(D), row(2 * D), _full((D, D)), _full((GROUPS, GROUP_DIM, GROUP_DIM)), _full((8, D))]
                  + _any_specs(n_sums),
        scratch_shapes=[pltpu.VMEM((ts + HALO, D), F32),
                        pltpu.SemaphoreType.DMA((3 * n_sums,)), pltpu.SemaphoreType.DMA((3 * n_sums,))],
        compiler_params=_params("arbitrary"),
    )(x, dh1, pooled, gt, w0, wpi, gw, gb, scale, wpo, *chip_sums)
    return outs[:5], outs[5:]


def gla_project(h1, w1, wgi, wlow, wgk, bgk, later):
    s = h1.shape[0]
    ts = ROW_TILE
    nt = s // ts
    assert nt >= 2
    n_later = len(later)

    def body(h_ref, w1_ref, wgi_ref, wlow_ref, wgk_ref, bgk_ref, *rest):
        rest = rest[n_later:]
        qk_ref, v_ref, gate_ref, low_ref, cum_ref, n1_ref = rest[:6]
        later_refs = rest[6:6 + n_later]
        send_sems, recv_sems = rest[6 + n_later:]
        gather_in_background(pl.program_id(0), nt - 1, later_refs, send_sems, recv_sems)
        hv = h_ref[...]
        r = lax.rsqrt(jnp.mean(hv * hv, axis=-1, keepdims=True) + EPS)
        n1 = _bf(hv * r * w1_ref[...])
        n1_ref[...] = n1
        qk_ref[...] = _nn(n1, wgi_ref[:, 0:2 * KEY_W])
        v_ref[...] = _bf(_nn(n1, wgi_ref[:, 2 * KEY_W:2 * KEY_W + D]))
        gate_ref[...] = _nn(n1, wgi_ref[:, 2 * KEY_W + D:GLA_MAIN])
        low = _bf(_nn(n1, wlow_ref[...]))
        low_ref[...] = low
        z = _nn(low, wgk_ref[...]) + bgk_ref[...]
        lg = (jnp.minimum(z, 0.0) - jnp.log(1.0 + jnp.exp(-jnp.abs(z)))) / GATE_NORM
        lower_f = _chunk_masks()[0].astype(F32)
        for r0 in range(0, ts, CHUNK):
            cum_ref[r0:r0 + CHUNK, :] = _nn_exact(lower_f, lg[r0:r0 + CHUNK, :])

    row = lambda cols: pl.BlockSpec((ts, cols), lambda i: (i, 0))
    outs = pl.pallas_call(
        body, name="gla_project", grid=(nt,),
        out_shape=[jax.ShapeDtypeStruct((s, D), F32), jax.ShapeDtypeStruct((s, D), BF16),
                   jax.ShapeDtypeStruct((s, D), F32), jax.ShapeDtypeStruct((s, RANK_PAD), BF16),
                   jax.ShapeDtypeStruct((s, KEY_W), F32), jax.ShapeDtypeStruct((s, D), BF16)]
                  + [jax.ShapeDtypeStruct(a.shape, a.dtype) for a in later],
        in_specs=[row(D), _full((1, D)), _full((D, GLA_MAIN)), _full((D, RANK_PAD)),
                  _full((RANK_PAD, KEY_W)), _full((1, KEY_W))] + _any_specs(n_later),
        out_specs=[row(D), row(D), row(D), row(RANK_PAD), row(KEY_W), row(D)] + _any_specs(n_later),
        input_output_aliases={6 + k: 6 + k for k in range(n_later)},
        scratch_shapes=[pltpu.SemaphoreType.DMA((6 * n_later,)), pltpu.SemaphoreType.DMA((6 * n_later,))],
        compiler_params=_params("arbitrary"),
    )(h1, w1, wgi, wlow, wgk, bgk, *later)
    return outs[:6], outs[6:]


GLA_BLOCK = 512
CHUNKS_PER_BLOCK = GLA_BLOCK // CHUNK


def _chunk_masks():
    t = lax.broadcasted_iota(jnp.int32, (CHUNK, CHUNK), 0)
    u = lax.broadcasted_iota(jnp.int32, (CHUNK, CHUNK), 1)
    return t >= u, t <= u


def _gla_chunk_terms(q, cum):
    ep = jnp.exp(cum)
    en = jnp.exp(-cum)
    qs = q * (HEAD_K ** -0.5)
    last = cum[CHUNK - 1:CHUNK, :]
    ed = jnp.exp(last - cum)
    dec = jnp.exp(last)
    return ep, en, qs, ed, dec


def gla_forward(qk, v, cum):
    s = qk.shape[0]
    nb = s // GLA_BLOCK
    nc = s // CHUNK

    def body(q_ref, k_ref, v_ref, cum_ref, o_ref, st_ref, sc_ref, state):
        @pl.when(pl.program_id(0) == 0)
        def _():
            state[...] = jnp.zeros_like(state)

        lower, _ = _chunk_masks()

        def chunk(cc, carry):
            rows = pl.ds(pl.multiple_of(cc * CHUNK, CHUNK), CHUNK)
            for h in range(HEADS):
                kc = slice(h * HEAD_K, (h + 1) * HEAD_K)
                vc = slice(h * HEAD_V, (h + 1) * HEAD_V)
                q = q_ref[rows, kc]
                k = k_ref[rows, kc]
                v = v_ref[rows, vc]
                ep, en, qs, ed, dec = _gla_chunk_terms(q, cum_ref[rows, kc])
                a = _bf(qs * ep)
                fwd = _nt(a, _bf(k * en))
                bwd = _nt(_bf(qs * en), _bf(k * ep))
                scores = _bf(jnp.where(lower, fwd, bwd))
                sc_ref[rows, h * CHUNK:(h + 1) * CHUNK] = scores
                st = state[h]
                st_ref[cc, h] = st
                o_ref[rows, vc] = _nn(scores, v) + _nt(a, _bf(st))
                state[h] = st * dec + _tn(v, _bf(k * ed))
            return carry

        lax.fori_loop(0, CHUNKS_PER_BLOCK, chunk, 0, unroll=4)

    return pl.pallas_call(
        body, name="gla_forward", grid=(nb,),
        out_shape=(jax.ShapeDtypeStruct((s, D), F32),
                   jax.ShapeDtypeStruct((nc, HEADS, HEAD_V, HEAD_K), F32),
                   jax.ShapeDtypeStruct((s, HEADS * CHUNK), BF16)),
        in_specs=[pl.BlockSpec((GLA_BLOCK, KEY_W), lambda i: (i, 0)),
                  pl.BlockSpec((GLA_BLOCK, KEY_W), lambda i: (i, 1)),
                  pl.BlockSpec((GLA_BLOCK, D), lambda i: (i, 0)),
                  pl.BlockSpec((GLA_BLOCK, KEY_W), lambda i: (i, 0))],
        out_specs=(pl.BlockSpec((GLA_BLOCK, D), lambda i: (i, 0)),
                   pl.BlockSpec((CHUNKS_PER_BLOCK, HEADS, HEAD_V, HEAD_K), lambda i: (i, 0, 0, 0)),
                   pl.BlockSpec((GLA_BLOCK, HEADS * CHUNK), lambda i: (i, 0))),
        scratch_shapes=[pltpu.VMEM((HEADS, HEAD_V, HEAD_K), F32)],
        compiler_params=_params("arbitrary"),
    )(qk, qk, v, cum)


def gla_backward(qk, v, cum, do, states, scores):
    s = qk.shape[0]
    nb = s // GLA_BLOCK

    def body(q_ref, k_ref, v_ref, cum_ref, do_ref, st_ref, sc_ref, dq_ref, dk_ref, dv_ref, dcum_ref, dstate):
        @pl.when(pl.program_id(0) == 0)
        def _():
            dstate[...] = jnp.zeros_like(dstate)

        lower, _ = _chunk_masks()
        is_last = lax.broadcasted_iota(jnp.int32, (CHUNK, HEAD_K), 0) == CHUNK - 1

        def chunk(step, carry):
            cc = CHUNKS_PER_BLOCK - 1 - step
            rows = pl.ds(pl.multiple_of(cc * CHUNK, CHUNK), CHUNK)
            for h in range(HEADS):
                kc = slice(h * HEAD_K, (h + 1) * HEAD_K)
                vc = slice(h * HEAD_V, (h + 1) * HEAD_V)
                q = q_ref[rows, kc]
                k = k_ref[rows, kc]
                v = v_ref[rows, vc]
                do_c = do_ref[rows, vc]
                ep, en, qs, ed, dec = _gla_chunk_terms(q, cum_ref[rows, kc])
                a = _bf(qs * ep)
                b = _bf(k * en)
                c = _bf(qs * en)
                dk_dec = _bf(k * ep)
                kd = _bf(k * ed)
                scores = sc_ref[rows, h * CHUNK:(h + 1) * CHUNK]
                st = st_ref[cc, h]
                dst = dstate[h]
                dst_bf = _bf(dst)

                dscores = _nt(do_c, v)
                dfwd = _bf(jnp.where(lower, dscores, 0.0))
                dbwd = _bf(jnp.where(lower, 0.0, dscores))
                dv_ref[rows, vc] = _bf(_tn(scores, do_c) + _nt(kd, dst_bf))
                da = _nn(dfwd, b) + _nn(do_c, _bf(st))
                db = _tn(dfwd, a)
                dc = _nn(dbwd, dk_dec)
                ddk = _tn(dbwd, c)
                dkd = _nn(v, dst_bf)
                ddec = jnp.sum(dst * st, axis=0, keepdims=True)
                dstate[h] = dst * dec + _tn(do_c, a)

                m = dkd * k * ed
                dq_ref[rows, kc] = _bf((da * ep + dc * en) * (HEAD_K ** -0.5))
                dk_ref[rows, kc] = _bf(db * en + ddk * ep + dkd * ed)
                dcum = (da * qs + ddk * k) * ep - (db * k + dc * qs) * en - m
                dlast = jnp.sum(m, axis=0, keepdims=True) + ddec * dec
                dcum_ref[rows, kc] = dcum + jnp.where(is_last, dlast, 0.0)
            return carry

        lax.fori_loop(0, CHUNKS_PER_BLOCK, chunk, 0, unroll=4)

    rev = lambda cols, col_block: pl.BlockSpec((GLA_BLOCK, cols), lambda i: (nb - 1 - i, col_block))
    return pl.pallas_call(
        body, name="gla_backward", grid=(nb,),
        out_shape=(jax.ShapeDtypeStruct((s, KEY_W), BF16), jax.ShapeDtypeStruct((s, KEY_W), BF16),
                   jax.ShapeDtypeStruct((s, D), BF16), jax.ShapeDtypeStruct((s, KEY_W), F32)),
        in_specs=[rev(KEY_W, 0), rev(KEY_W, 1), rev(D, 0), rev(KEY_W, 0), rev(D, 0),
                  pl.BlockSpec((CHUNKS_PER_BLOCK, HEADS, HEAD_V, HEAD_K), lambda i: (nb - 1 - i, 0, 0, 0)),
                  rev(HEADS * CHUNK, 0)],
        out_specs=(rev(KEY_W, 0), rev(KEY_W, 0), rev(D, 0), rev(KEY_W, 0)),
        scratch_shapes=[pltpu.VMEM((HEADS, HEAD_V, HEAD_K), F32)],
        compiler_params=_params("arbitrary"),
    )(qk, qk, v, cum, do, states, scores)


def head_and_loss(o, gate, h1, target, hw, wgo, wf):
    s = o.shape[0]
    ts = ROW_TILE

    def body(o_ref, gate_ref, h1_ref, tgt_ref, hw_ref, wgo_ref, wf_ref,
             dh2_ref, do_ref, dgate_ref, ggo_ref, small_ref):
        @pl.when(pl.program_id(0) == 0)
        def _():
            ggo_ref[...] = jnp.zeros_like(ggo_ref)
            small_ref[...] = jnp.zeros_like(small_ref)

        gate = gate_ref[...]
        hw = hw_ref[...]
        sg = _sigmoid(gate)
        silu = gate * sg
        ohat, ro = [], []
        for h in range(HEADS):
            oh = o_ref[:, h * HEAD_V:(h + 1) * HEAD_V]
            rh = lax.rsqrt(jnp.mean(oh * oh, axis=-1, keepdims=True) + EPS)
            ro.append(rh)
            ohat.append(oh * rh)
        ohat = jnp.concatenate(ohat, axis=-1)
        on = ohat * hw
        y2 = _bf(on * silu)
        h2 = h1_ref[...] + _nn(y2, wgo_ref[...])
        rf = lax.rsqrt(jnp.mean(h2 * h2, axis=-1, keepdims=True) + EPS)
        h2hat = h2 * rf
        wf = wf_ref[...]
        diff = h2hat * wf - tgt_ref[...]
        small_ref[2:3, :] += jnp.zeros((1, D), F32) + 0.5 * jnp.sum(diff * diff) / D
        dout = diff / D
        small_ref[0:1, :] += jnp.sum(dout * h2hat, axis=0, keepdims=True)
        dxh = dout * wf
        dh2 = rf * (dxh - h2hat * jnp.mean(dxh * h2hat, axis=-1, keepdims=True))
        dh2_ref[...] = dh2
        dh2_bf = _bf(dh2)
        ggo_ref[...] += _tn(y2, dh2_bf)
        dy2 = _nt(dh2_bf, wgo_ref[...])
        don = dy2 * silu
        dgate_ref[...] = _bf(dy2 * on * (sg * (1.0 + gate * (1.0 - sg))))
        ghw = jnp.sum(don * ohat, axis=0, keepdims=True)
        small_ref[1:2, 0:HEAD_V] += sum(ghw[:, h * HEAD_V:(h + 1) * HEAD_V] for h in range(HEADS))
        dohat = don * hw
        for h in range(HEADS):
            cols = slice(h * HEAD_V, (h + 1) * HEAD_V)
            oh, dh = ohat[:, cols], dohat[:, cols]
            do_ref[:, cols] = _bf(ro[h] * (dh - oh * jnp.mean(dh * oh, axis=-1, keepdims=True)))

    row = lambda cols: pl.BlockSpec((ts, cols), lambda i: (i, 0))
    act = jax.ShapeDtypeStruct((s, D), F32)
    act_bf = jax.ShapeDtypeStruct((s, D), BF16)
    return pl.pallas_call(
        body, name="head_and_loss", grid=(s // ts,),
        out_shape=(act, act_bf, act_bf, jax.ShapeDtypeStruct((D, D), F32), jax.ShapeDtypeStruct((8, D), F32)),
        in_specs=[row(D), row(D), row(D), row(D),
                  _full((1, D)), _full((D, D)), _full((1, D))],
        out_specs=(row(D), row(D), row(D), _full((D, D)), _full((8, D))),
        compiler_params=_params("arbitrary"),
    )(o, gate, h1, target, hw, wgo, wf)


def gla_project_backward(dq, dk, dv, dgate, dcum, low, h1, dh2, w1, wgi, wlow, wgk, bgk):
    s = h1.shape[0]
    ts = ROW_TILE

    def body(dq_ref, dk_ref, dv_ref, dgate_ref, dcum_ref, low_ref, h1_ref, dh2_ref, w1_ref,
             wgi_ref, wlow_ref, wgk_ref, bgk_ref, dh1_ref, dproj_ref, dlow_ref, ggk_ref, small_ref):
        @pl.when(pl.program_id(0) == 0)
        def _():
            ggk_ref[...] = jnp.zeros_like(ggk_ref)
            small_ref[...] = jnp.zeros_like(small_ref)

        low = low_ref[...]
        z = _nn(low, wgk_ref[...]) + bgk_ref[...]
        upper_f = _chunk_masks()[1].astype(F32)
        dlg = jnp.concatenate([_nn_exact(upper_f, dcum_ref[r0:r0 + CHUNK, :]) for r0 in range(0, ts, CHUNK)],
                              axis=0)
        dz = dlg * (1.0 / GATE_NORM) * _sigmoid(-z)
        dz_bf = _bf(dz)
        ggk_ref[...] += _tn(low, dz_bf)
        small_ref[1:2, 0:KEY_W] += jnp.sum(dz, axis=0, keepdims=True)
        dlow = _bf(_nt(dz_bf, wgk_ref[...]))
        dlow_ref[...] = dlow
        dn1 = _nt(dlow, wlow_ref[...])
        for ref, lo, hi in ((dq_ref, 0, KEY_W), (dk_ref, KEY_W, 2 * KEY_W),
                            (dv_ref, 2 * KEY_W, 2 * KEY_W + D), (dgate_ref, 2 * KEY_W + D, GLA_MAIN)):
            piece = ref[...]
            dproj_ref[:, lo:hi] = piece
            dn1 = dn1 + _nt(piece, wgi_ref[:, lo:hi])
        hv = h1_ref[...]
        r = lax.rsqrt(jnp.mean(hv * hv, axis=-1, keepdims=True) + EPS)
        hhat = hv * r
        small_ref[0:1, :] += jnp.sum(dn1 * hhat, axis=0, keepdims=True)
        dxh = dn1 * w1_ref[...]
        dh1_ref[...] = dh2_ref[...] + r * (dxh - hhat * jnp.mean(dxh * hhat, axis=-1, keepdims=True))

    row = lambda cols: pl.BlockSpec((ts, cols), lambda i: (i, 0))
    return pl.pallas_call(
        body, name="gla_project_backward", grid=(s // ts,),
        out_shape=(jax.ShapeDtypeStruct((s, D), F32), jax.ShapeDtypeStruct((s, GLA_MAIN), BF16),
                   jax.ShapeDtypeStruct((s, RANK_PAD), BF16), jax.ShapeDtypeStruct((RANK_PAD, KEY_W), F32),
                   jax.ShapeDtypeStruct((8, D), F32)),
        in_specs=[row(KEY_W), row(KEY_W), row(D), row(D), row(KEY_W), row(RANK_PAD), row(D), row(D),
                  _full((1, D)), _full((D, GLA_MAIN)), _full((D, RANK_PAD)), _full((RANK_PAD, KEY_W)),
                  _full((1, KEY_W))],
        out_specs=(row(D), row(GLA_MAIN), row(RANK_PAD), _full((RANK_PAD, KEY_W)), _full((8, D))),
        compiler_params=_params("arbitrary"),
    )(dq, dk, dv, dgate, dcum, low, h1, dh2, w1, wgi, wlow, wgk, bgk)


def _groups_from_quarters(a):
    return a.reshape(N_CHIPS, GROUPS, 64, GROUP_DIM).transpose(1, 0, 2, 3).reshape(GROUPS, GROUP_DIM, GROUP_DIM)


def _quarters_from_groups(a):
    return a.reshape(GROUPS, N_CHIPS, 64, GROUP_DIM).transpose(1, 0, 2, 3).reshape(N_CHIPS, GROUP_DIM, GROUP_DIM)


def _pad_row(*pieces):
    flat = jnp.concatenate([p.reshape(-1).astype(F32) for p in pieces])
    return jnp.pad(flat, (0, D - flat.shape[0])).reshape(1, D)


def _gla_in_weights(wgi_q):
    wgi_all = jnp.concatenate([wgi_q[q] for q in range(N_CHIPS)], axis=1)
    wlow = jnp.pad(wgi_all[:, GLA_MAIN:], ((0, 0), (0, RANK_PAD - GATE_RANK)))
    return wgi_all, wlow


def _small_sums(small_top, small_gla, small_pool, g_gk_pad):
    return jnp.concatenate([
        small_pool[0:1], small_gla[0:1],
        small_pool[1:2],
        small_top[0:1],
        small_top[2:3],
        _pad_row(small_gla[1, 0:KEY_W], small_top[1, 0:HEAD_V]),
        small_pool[2:3],
        g_gk_pad[:GATE_RANK].reshape(8, D),
        jnp.zeros((1, D), F32)], axis=0)


def local_gradients(xs, target, w0, w1, wf, wpi, gw, gb, scale, wpo, gla_quarters, wgk, bgk, hw_tiled, place):
    wgi_q, wgo_q = gla_quarters
    (h1, pooled, gt, n0), (wgi_q,) = pool_forward(xs, w0, wpi, gw, gb, scale, wpo, [wgi_q])
    wgi, wlow = _gla_in_weights(wgi_q)
    (qk, v, gate, low, cum, n1), (wgo_q,) = gla_project(h1, w1, wgi, wlow, wgk, bgk, [wgo_q])
    wgo = wgo_q.reshape(D, D)
    o, states, scores = gla_forward(qk, v, cum)

    dh2, do, dgate, g_gla_out, small_top = head_and_loss(o, gate, h1, target, hw_tiled, wgo, wf)
    dq, dk, dv, dcum = gla_backward(qk, v, cum, do, states, scores)
    dh1, dproj, dlow, g_gk_pad, small_gla = gla_project_backward(
        dq, dk, dv, dgate, dcum, low, h1, dh2, w1, wgi, wlow, wgk, bgk)
    g_gla_in = jnp.concatenate([matmul_tn(n1, dproj, "grad_gla_in"),
                                matmul_tn(n1, dlow, "grad_gla_low")[:, :GATE_RANK]], axis=1)

    def chip_sums(grads, names, tag):
        theirs = exchange_with_sibling(grads, "exchange_with_sibling_" + tag)
        return [add_halves(g, t, place, "add_halves_" + n) for g, t, n in zip(grads, theirs, names)]

    gla_names = ("gla_in", "gla_out")
    gla_sums = chip_sums(
        [jnp.stack([g_gla_in[:, GLA_IN_QUARTER * q:GLA_IN_QUARTER * (q + 1)] for q in range(N_CHIPS)]),
         g_gla_out.reshape(N_CHIPS, D // N_CHIPS, D)], gla_names, "gla")
    (dx, dpool, g_pool_out, g_group_w, small_pool), gla_got = pool_backward(
        xs, dh1, pooled, gt, w0, wpi, gw, gb, scale, wpo, [b for _, b in gla_sums])
    g_pool_in = matmul_tn(n0, dpool, "grad_pool_in", by_column_tile=True)

    pool_names = ("pool_in", "group", "pool_out")
    pool_sums = chip_sums(
        [g_pool_in, _quarters_from_groups(g_group_w), g_pool_out.reshape(N_CHIPS, D // N_CHIPS, D)],
        pool_names, "pool")
    pool_got = scatter_to_owners([b for _, b in pool_sums], "scatter_to_owners_pool")
    reduced, total = join_halves(
        [add_parts(f, g, place, "add_parts_" + n) for (f, _), g, n in
         zip(pool_sums + gla_sums, list(pool_got) + list(gla_got), pool_names + gla_names)],
        _small_sums(small_top, small_gla, small_pool, g_gk_pad))
    return dx, reduced, total


def kernel(x, norm_w, pool_in_w, pool_group_w, pool_group_b, pool_scale, pool_out_w, gla_in_w, gla_gk_w, gla_gk_b, gla_head_norm_w, gla_out_w, final_norm_w, loss_target, m_norm_w, m_pool_in_w, m_pool_group_w, m_pool_group_b, m_pool_scale, m_pool_out_w, m_gla_in_w, m_gla_gk_w, m_gla_gk_b, m_gla_head_norm_w, m_gla_out_w, m_final_norm_w, v_norm_w, v_pool_in_w, v_pool_group_w, v_pool_group_b, v_pool_scale, v_pool_out_w, v_gla_in_w, v_gla_gk_w, v_gla_gk_b, v_gla_head_norm_w, v_gla_out_w, v_final_norm_w):
    s = x.shape[1]
    xs = x[0]
    target = loss_target[0]
    q_chip = 2 * lax.axis_index("x") + lax.axis_index("y")
    place = jnp.stack([lax.axis_index("c"), q_chip]).astype(jnp.int32)

    small_in = jnp.concatenate([
        _pad_row(gla_gk_b[0], gla_head_norm_w[0], pool_group_b[0]),
        gla_gk_w[0].reshape(2, D),
        jnp.zeros((5, D), F32)], axis=0)
    (wpi, gw_q, wpo_q, wgi_q, wgo_q), small_all = allgather_weights(
        [pool_in_w[0], pool_group_w[0].reshape(GROUP_DIM, GROUP_DIM), pool_out_w[0], gla_in_w[0], gla_out_w[0]],
        exchange=(True, True, True, False, False), small=small_in)
    gw = _groups_from_quarters(gw_q)
    wpo = wpo_q.reshape(D, D)
    small_all = small_all[0::2]
    bgk = small_all[:, 0, 0:128].reshape(1, KEY_W)
    hw = small_all[:, 0, 128:192].reshape(1, HEAD_V)
    gb = jnp.concatenate([small_all[q, 0, 192:448].reshape(GROUPS, 64) for q in range(N_CHIPS)],
                         axis=1).reshape(1, D)
    wgk16 = jnp.concatenate([small_all[q, 1:3].reshape(GATE_RANK, 128) for q in range(N_CHIPS)], axis=1)
    wgk = _bf(jnp.pad(wgk16, ((0, RANK_PAD - GATE_RANK), (0, 0))))
    hw_tiled = jnp.tile(hw, (1, HEADS))

    w0 = norm_w[0:1]
    w1 = norm_w[1:2]
    wf = final_norm_w.reshape(1, D)

    dx, reduced, total = local_gradients(
        xs, target, w0, w1, wf, wpi, gw, gb, pool_scale, wpo, [wgi_q, wgo_q], wgk, bgk, hw_tiled, place)
    r_pool_in, r_group_w, r_pool_out, r_gla_in, r_gla_out = reduced
    r_group_w = r_group_w.reshape(GROUPS, 64, GROUP_DIM)

    loss = total[4, 0]
    g_norm = total[0:2]
    g_scale = total[2:3]
    g_final = total[3]
    pick = lambda full, width: lax.dynamic_slice_in_dim(full, q_chip * width, width, axis=-1)
    g_gk_b = pick(total[5:6, 0:KEY_W], 128)
    g_hnw = pick(total[5:6, KEY_W:KEY_W + HEAD_V], 64)
    g_group_b = pick(total[6].reshape(GROUPS, GROUP_DIM), 64)[None]
    g_gk_w = pick(total[7:15].reshape(GATE_RANK, KEY_W), 128)[None]

    def step_lane_rows(name, w, g, m, v):
        turn = lambda a: jnp.transpose(a, (2, 0, 1))
        back = lambda a: jnp.transpose(a, (1, 2, 0))
        g_t = turn(g)
        d, nm, nv = adamw_rows(turn(w), g_t, turn(m), turn(v), "adamw_" + name)
        return back(g_t), back(d), back(nm), back(nv)

    def step(name, w, g, m, v):
        shape = w.shape
        as2d = lambda a: a.reshape(-1, shape[-1])
        d, nm, nv = adamw(as2d(w), as2d(g), as2d(m), as2d(v), "adamw_" + name)
        return g.reshape(shape), d.reshape(shape), nm.reshape(shape), nv.reshape(shape)

    small_names = ("norm_w", "pool_group_b", "pool_scale", "gla_gk_w", "gla_gk_b", "gla_head_norm_w",
                   "final_norm_w")
    small_args = [(norm_w, g_norm, m_norm_w, v_norm_w),
                  (pool_group_b, g_group_b, m_pool_group_b, v_pool_group_b),
                  (pool_scale, g_scale, m_pool_scale, v_pool_scale),
                  (gla_gk_w, g_gk_w, m_gla_gk_w, v_gla_gk_w),
                  (gla_gk_b, g_gk_b, m_gla_gk_b, v_gla_gk_b),
                  (gla_head_norm_w, g_hnw, m_gla_head_norm_w, v_gla_head_norm_w),
                  (final_norm_w, g_final, m_final_norm_w, v_final_norm_w)]
    as2d = lambda a, w: a.reshape(-1, w.shape[-1])
    small_out = adamw_small([tuple(as2d(a, p[0]) for a in p) for p in small_args])
    small = {n: (p[1].reshape(p[0].shape),) + tuple(o.reshape(p[0].shape) for o in out)
             for n, p, out in zip(small_names, small_args, small_out)}
    results = [
        small["norm_w"],
        step("pool_in_w", pool_in_w, r_pool_in[None], m_pool_in_w, v_pool_in_w),
        step("pool_group_w", pool_group_w, r_group_w[None], m_pool_group_w, v_pool_group_w),
        small["pool_group_b"],
        small["pool_scale"],
        step("pool_out_w", pool_out_w, r_pool_out[None], m_pool_out_w, v_pool_out_w),
        step_lane_rows("gla_in_w", gla_in_w, r_gla_in[None], m_gla_in_w, v_gla_in_w),
        small["gla_gk_w"],
        small["gla_gk_b"],
        small["gla_head_norm_w"],
        step("gla_out_w", gla_out_w, r_gla_out[None], m_gla_out_w, v_gla_out_w),
        small["final_norm_w"],
    ]
    grads, deltas, new_m, new_v = zip(*results)
    return (loss, dx[None], *grads, *deltas, *new_m, *new_v)
```

```python
import functools

import jax
import jax.numpy as jnp
from jax import lax
from jax.experimental import pallas as pl
from jax.experimental.pallas import tpu as pltpu

F32 = jnp.float32
BF16 = jnp.bfloat16
MESH = pl.DeviceIdType.MESH

D = 1024
POOL_WINDOWS = (2, 4, 8, 16)
GROUPS = 4
GROUP_DIM = 256
HEADS = 4
HEAD_K = 128
HEAD_V = 256
KEY_W = 512
CHUNK = 64
GATE_RANK = 16
GATE_NORM = 16.0
GLA_IN = 3088
GLA_MAIN = 3072
RANK_PAD = 128
EPS = 1e-6
HALO = 16

ADAM_LR = 0.001
ADAM_B1 = 0.9
ADAM_B2 = 0.999
ADAM_EPS = 1e-08
ADAM_WD = 0.01
ADAM_STEP = 10

N_CHIPS = 4
N_DEV = 8
GLA_IN_QUARTER = GLA_IN // N_CHIPS

VMEM_LIMIT = 56 * 1024 * 1024


def _nn(a, b):
    return lax.dot_general(a, b, (((1,), (0,)), ((), ())), preferred_element_type=F32)


def _nt(a, b):
    return lax.dot_general(a, b, (((1,), (1,)), ((), ())), preferred_element_type=F32)


def _tn(a, b):
    return lax.dot_general(a, b, (((0,), (0,)), ((), ())), preferred_element_type=F32)


def _nn_exact(a, b):
    return lax.dot_general(a, b, (((1,), (0,)), ((), ())), preferred_element_type=F32,
                           precision=lax.Precision.HIGHEST)


def _bf(a):
    return a.astype(BF16)


def _params(*sem):
    return pltpu.CompilerParams(dimension_semantics=sem, vmem_limit_bytes=VMEM_LIMIT)


def _full(shape):
    return pl.BlockSpec(shape, lambda i: (0,) * len(shape))


def _position():
    return lax.axis_index("x"), lax.axis_index("y"), lax.axis_index("c")


def _gather_small(in_ref, all_ref, send_sems, recv_sems, local_sem):
    x, y, c = _position()
    me = 4 * x + 2 * y + c
    mine = pltpu.make_async_copy(in_ref, all_ref.at[me], local_sem)
    mine.start()
    sends = []
    for k in range(N_DEV - 1):
        fx, fy, fc = (k + 1) >> 2 & 1, (k + 1) >> 1 & 1, (k + 1) & 1
        cp = pltpu.make_async_remote_copy(
            src_ref=in_ref, dst_ref=all_ref.at[me],
            send_sem=send_sems.at[k], recv_sem=recv_sems.at[k],
            device_id=(x ^ fx, y ^ fy, c ^ fc), device_id_type=MESH)
        cp.start()
        sends.append(cp)
    for k in range(N_DEV - 1):
        fx, fy, fc = (k + 1) >> 2 & 1, (k + 1) >> 1 & 1, (k + 1) & 1
        src_dev = 4 * (x ^ fx) + 2 * (y ^ fy) + (c ^ fc)
        pltpu.make_async_remote_copy(
            src_ref=in_ref, dst_ref=all_ref.at[src_dev],
            send_sem=send_sems.at[k], recv_sem=recv_sems.at[k],
            device_id=(x, y, c), device_id_type=MESH).wait_recv()
    for cp in sends:
        cp.wait_send()
    mine.wait()


SMALL_SEMS = [pltpu.SemaphoreType.DMA((N_DEV - 1,)), pltpu.SemaphoreType.DMA((N_DEV - 1,)),
              pltpu.SemaphoreType.DMA]
VMEM_SPEC = pl.BlockSpec(memory_space=pltpu.VMEM)


def _other_chips(x, y):
    return [(1 - x, y), (x, 1 - y), (1 - x, 1 - y)]


def _any_specs(n):
    return [pl.BlockSpec(memory_space=pl.ANY)] * n


def _halves(rows, c):
    half = rows // 2
    return pl.ds(c * half, half), pl.ds((1 - c) * half, half)


CAST_ROWS = 256


def _gather_copy(out_ref, send_sems, recv_sems, k, quarter, half, to, src=None):
    dst = out_ref.at[quarter, half]
    return pltpu.make_async_remote_copy(
        src_ref=dst if src is None else src, dst_ref=dst,
        send_sem=send_sems.at[k], recv_sem=recv_sems.at[k], device_id=to, device_id_type=MESH)


def allgather_weights(quarters, exchange, small):
    n = len(quarters)
    shapes = [w.shape for w in quarters]
    moved = [i for i in range(n) if exchange[i]]

    def body(*refs):
        w_refs, small_ref = refs[:n], refs[n]
        out_refs, small_all_ref = refs[n + 1:2 * n + 1], refs[2 * n + 1]
        refs = refs[2 * n + 2:]
        f32_bufs, bf_bufs = refs[:n], refs[n:2 * n]
        send_sems, recv_sems, local_sems = refs[2 * n:2 * n + 3]
        x, y, c = _position()
        q = 2 * x + y
        sibling = (x, y, 1 - c)
        chips = _other_chips(x, y)

        def copy(k, i, quarter, half, to, src=None):
            return _gather_copy(out_refs[i], send_sems, recv_sems, k * n + i, quarter, half, to, src)

        loads = [pltpu.make_async_copy(w_refs[i], f32_bufs[i], local_sems.at[i]) for i in range(n)]
        for cp in loads:
            cp.start()
        keeps, sends = [], []
        for i in range(n):
            loads[i].wait()
            for r0 in range(0, shapes[i][0], CAST_ROWS):
                bf_bufs[i][r0:r0 + CAST_ROWS, :] = _bf(f32_bufs[i][r0:r0 + CAST_ROWS, :])
            keep = pltpu.make_async_copy(bf_bufs[i], out_refs[i].at[q], local_sems.at[n + i])
            keep.start()
            keeps.append(keep)
            if not exchange[i]:
                continue
            mine, _ = _halves(shapes[i][0], c)
            for j, chip in enumerate(chips):
                cp = copy(j, i, q, mine, (*chip, c), src=bf_bufs[i].at[mine])
                cp.start()
                sends.append(cp)
        for j, chip in enumerate(chips):
            qj = 2 * chip[0] + chip[1]
            for i in moved:
                mine, _ = _halves(shapes[i][0], c)
                copy(j, i, qj, mine, (x, y, c)).wait_recv()
                cp = copy(3 + j, i, qj, mine, sibling)
                cp.start()
                sends.append(cp)
        for j, chip in enumerate(chips):
            qj = 2 * chip[0] + chip[1]
            for i in moved:
                _, other = _halves(shapes[i][0], c)
                copy(3 + j, i, qj, other, (x, y, c)).wait_recv()
        _gather_small(small_ref, small_all_ref, *refs[2 * n + 3:])
        for cp in sends:
            cp.wait_send()
        for cp in keeps:
            cp.wait()

    outs = pl.pallas_call(
        body, name="allgather_weights",
        out_shape=[jax.ShapeDtypeStruct((N_CHIPS, *s), BF16) for s in shapes]
                  + [jax.ShapeDtypeStruct((N_DEV, *small.shape), small.dtype)],
        in_specs=_any_specs(n) + [VMEM_SPEC], out_specs=_any_specs(n) + [VMEM_SPEC],
        scratch_shapes=([pltpu.VMEM(s, F32) for s in shapes] + [pltpu.VMEM(s, BF16) for s in shapes]
                        + [pltpu.SemaphoreType.DMA((6 * n,)), pltpu.SemaphoreType.DMA((6 * n,)),
                           pltpu.SemaphoreType.DMA((2 * n,))] + SMALL_SEMS),
        compiler_params=pltpu.CompilerParams(vmem_limit_bytes=VMEM_LIMIT),
    )(*quarters, small)
    return outs[:n], outs[n]


def exchange_with_sibling(grads, name):
    n = len(grads)

    def body(*refs):
        g_refs, theirs_refs = refs[:n], refs[n:2 * n]
        send_sems, recv_sems = refs[2 * n:]
        x, y, c = _position()
        copies = []
        for i in range(n):
            _, other = _halves(g_refs[i].shape[1], c)
            cp = pltpu.make_async_remote_copy(
                src_ref=g_refs[i].at[:, other], dst_ref=theirs_refs[i],
                send_sem=send_sems.at[i], recv_sem=recv_sems.at[i],
                device_id=(x, y, 1 - c), device_id_type=MESH)
            cp.start()
            copies.append(cp)
        for cp in copies:
            cp.wait()

    return pl.pallas_call(
        body, name=name,
        out_shape=[jax.ShapeDtypeStruct((N_CHIPS, g.shape[1] // 2, g.shape[2]), F32) for g in grads],
        in_specs=_any_specs(n), out_specs=_any_specs(n),
        scratch_shapes=[pltpu.SemaphoreType.DMA((n,)), pltpu.SemaphoreType.DMA((n,))],
    )(*grads)


def _scatter_copies(b_refs, got_refs, send_sems, recv_sems):
    n = len(b_refs)
    x, y, c = _position()
    copies = []
    for j, chip in enumerate(_other_chips(x, y)):
        qj = 2 * chip[0] + chip[1]
        for i in range(n):
            copies.append(pltpu.make_async_remote_copy(
                src_ref=b_refs[i].at[qj], dst_ref=got_refs[i].at[j],
                send_sem=send_sems.at[j * n + i], recv_sem=recv_sems.at[j * n + i],
                device_id=(*chip, c), device_id_type=MESH))
    return copies


def _scatter_shapes(chip_sums):
    return [jax.ShapeDtypeStruct((N_CHIPS - 1, *b.shape[1:]), BF16) for b in chip_sums]


def scatter_to_owners(chip_sums, name):
    n = len(chip_sums)

    def body(*refs):
        copies = _scatter_copies(refs[:n], refs[n:2 * n], *refs[2 * n:])
        for cp in copies:
            cp.start()
        for cp in copies:
            cp.wait()

    return pl.pallas_call(
        body, name=name,
        out_shape=_scatter_shapes(chip_sums),
        in_specs=_any_specs(n), out_specs=_any_specs(n),
        scratch_shapes=[pltpu.SemaphoreType.DMA((3 * n,)), pltpu.SemaphoreType.DMA((3 * n,))],
    )(*chip_sums)


def join_halves(reduced, small):
    n = len(reduced)

    def body(*refs):
        small_ref = refs[n]
        buf_refs, total_ref = refs[n + 1:2 * n + 1], refs[2 * n + 1]
        send_sems, recv_sems, all_ref = refs[2 * n + 2:2 * n + 5]
        x, y, c = _position()
        copies = []
        for i in range(n):
            mine, _ = _halves(buf_refs[i].shape[0], c)
            cp = pltpu.make_async_remote_copy(
                src_ref=buf_refs[i].at[mine], dst_ref=buf_refs[i].at[mine],
                send_sem=send_sems.at[i], recv_sem=recv_sems.at[i],
                device_id=(x, y, 1 - c), device_id_type=MESH)
            cp.start()
            copies.append(cp)
        _gather_small(small_ref, all_ref, *refs[2 * n + 5:])
        total = all_ref[0]
        for dev in range(1, N_DEV):
            total = total + all_ref[dev]
        total_ref[...] = total
        for cp in copies:
            cp.wait()

    outs = pl.pallas_call(
        body, name="join_halves",
        out_shape=[jax.ShapeDtypeStruct(r.shape, F32) for r in reduced]
                  + [jax.ShapeDtypeStruct(small.shape, small.dtype)],
        in_specs=_any_specs(n) + [VMEM_SPEC], out_specs=_any_specs(n) + [VMEM_SPEC],
        input_output_aliases={i: i for i in range(n)},
        scratch_shapes=[pltpu.SemaphoreType.DMA((n,)), pltpu.SemaphoreType.DMA((n,)),
                        pltpu.VMEM((N_DEV, *small.shape), small.dtype)] + SMALL_SEMS,
    )(*reduced, small)
    return outs[:n], outs[n]


ADD_ROWS = 512


def add_halves(grad, theirs, place, name):
    _, half, cols = theirs.shape
    rb = min(ADD_ROWS, half)
    steps = half // rb

    def body(place_ref, a_ref, b_ref, f_ref, h_ref):
        s = a_ref[0] + b_ref[0]
        h_ref[0] = _bf(s)

        @pl.when(pl.program_id(1) == place_ref[1])
        def _():
            f_ref[...] = s

    spec = pl.BlockSpec((1, rb, cols), lambda j, i, place: (i, j, 0))
    return pl.pallas_call(
        body, name=name,
        grid_spec=pltpu.PrefetchScalarGridSpec(
            num_scalar_prefetch=1, grid=(steps, N_CHIPS),
            in_specs=[pl.BlockSpec((1, rb, cols), lambda j, i, place: (i, place[0] * steps + j, 0)), spec],
            out_specs=(pl.BlockSpec((rb, cols), lambda j, i, place: (j, 0)), spec)),
        out_shape=(jax.ShapeDtypeStruct((half, cols), F32), jax.ShapeDtypeStruct(theirs.shape, BF16)),
        compiler_params=_params("parallel", "arbitrary"),
    )(place, grad, theirs)


def add_parts(own, got, place, name):
    _, half, cols = got.shape
    rb = min(ADD_ROWS, half)
    steps = half // rb

    def body(place_ref, o_ref, g_ref, out_ref):
        s = o_ref[...]
        for j in range(N_CHIPS - 1):
            s = s + g_ref[j].astype(F32)
        out_ref[...] = s

    return pl.pallas_call(
        body, name=name,
        grid_spec=pltpu.PrefetchScalarGridSpec(
            num_scalar_prefetch=1, grid=(steps,),
            in_specs=[pl.BlockSpec((rb, cols), lambda j, place: (j, 0)),
                      pl.BlockSpec((N_CHIPS - 1, rb, cols), lambda j, place: (0, j, 0))],
            out_specs=pl.BlockSpec((rb, cols), lambda j, place: (place[0] * steps + j, 0))),
        out_shape=jax.ShapeDtypeStruct((2 * half, cols), F32),
        compiler_params=_params("parallel"),
    )(place, own, got)


def _adam_math(w, g, m, v):
    m = ADAM_B1 * m + (1.0 - ADAM_B1) * g
    v = ADAM_B2 * v + (1.0 - ADAM_B2) * (g * g)
    m_hat = m / (1.0 - ADAM_B1 ** ADAM_STEP)
    v_hat = v / (1.0 - ADAM_B2 ** ADAM_STEP)
    delta = -ADAM_LR * (m_hat / (jnp.sqrt(v_hat) + ADAM_EPS) + ADAM_WD * w)
    return delta, m, v


def adamw(w, g, m, v, name):
    rows, cols = w.shape
    fits = [t for t in range(8, rows, 8) if rows % t == 0 and t * cols * 4 <= 2 ** 20]
    tile = max(fits) if fits else rows

    def body(w_ref, g_ref, m_ref, v_ref, d_ref, nm_ref, nv_ref):
        d, nm, nv = _adam_math(w_ref[...], g_ref[...], m_ref[...], v_ref[...])
        d_ref[...] = d
        nm_ref[...] = nm
        nv_ref[...] = nv

    spec = pl.BlockSpec((tile, cols), lambda i: (i, 0))
    shape = jax.ShapeDtypeStruct((rows, cols), F32)
    return pl.pallas_call(
        body, name=name, grid=(rows // tile,),
        out_shape=(shape, shape, shape),
        in_specs=[spec] * 4, out_specs=(spec, spec, spec),
        compiler_params=_params("parallel"),
    )(w, g, m, v)


def adamw_small(params):
    n = len(params)

    def body(*refs):
        ins, outs = refs[:4 * n], refs[4 * n:]
        for k in range(n):
            w_ref, g_ref, m_ref, v_ref = ins[4 * k:4 * k + 4]
            d, nm, nv = _adam_math(w_ref[...], g_ref[...], m_ref[...], v_ref[...])
            outs[3 * k][...] = d
            outs[3 * k + 1][...] = nm
            outs[3 * k + 2][...] = nv

    flat = [a for p in params for a in p]
    outs = pl.pallas_call(
        body, name="adamw_small",
        out_shape=[jax.ShapeDtypeStruct(p[0].shape, F32) for p in params for _ in range(3)],
        in_specs=[VMEM_SPEC] * (4 * n), out_specs=[VMEM_SPEC] * (3 * n),
    )(*flat)
    return [tuple(outs[3 * k:3 * k + 3]) for k in range(n)]


def adamw_rows(w, g, m, v, name):
    rows, _, cols = w.shape
    tile = rows // 4

    def body(w_ref, g_ref, m_ref, v_ref, d_ref, nm_ref, nv_ref):
        d, nm, nv = _adam_math(w_ref[...], g_ref[...], m_ref[...], v_ref[...])
        d_ref[...] = d
        nm_ref[...] = nm
        nv_ref[...] = nv

    spec = pl.BlockSpec((tile, 1, cols), lambda i: (i, 0, 0))
    shape = jax.ShapeDtypeStruct(w.shape, F32)
    return pl.pallas_call(
        body, name=name, grid=(rows // tile,),
        out_shape=(shape, shape, shape),
        in_specs=[spec] * 4, out_specs=(spec, spec, spec),
        compiler_params=_params("parallel"),
    )(w, g, m, v)


def matmul_tn(a, b, name, tile_n=512, tile_s=2048, by_column_tile=False):
    s, m = a.shape
    n = b.shape[1]
    tile_n = min(tile_n, n)
    tile_s = min(tile_s, s)
    steps = s // tile_s
    if by_column_tile:
        out_shape = jax.ShapeDtypeStruct((n // tile_n, m, tile_n), F32)
        out_spec = pl.BlockSpec((None, m, tile_n), lambda j, k: (j, 0, 0))
    else:
        out_shape = jax.ShapeDtypeStruct((m, n), F32)
        out_spec = pl.BlockSpec((m, tile_n), lambda j, k: (0, j))

    def body(a_ref, b_ref, out_ref):
        k = pl.program_id(1)

        @pl.when(k == 0)
        def _():
            out_ref[...] = jnp.zeros_like(out_ref)

        out_ref[...] += _tn(a_ref[...], b_ref[...])

    return pl.pallas_call(
        body, name=name, grid=(n // tile_n, steps),
        out_shape=out_shape,
        in_specs=[pl.BlockSpec((tile_s, m), lambda j, k: (k, 0)),
                  pl.BlockSpec((tile_s, tile_n), lambda j, k: (k, j))],
        out_specs=out_spec,
        compiler_params=_params("parallel", "arbitrary"),
    )(a, b)


ROW_TILE = 512


def _row_index(tile, rows):
    return tile * rows + lax.broadcasted_iota(jnp.int32, (rows, 1), 0)


def _inverse_counts(t_glob):
    return [1.0 / jnp.minimum(t_glob + 1, w).astype(F32) for w in POOL_WINDOWS]


def _sigmoid(z):
    return 1.0 / (1.0 + jnp.exp(-z))


def gather_in_background(step, last, out_refs, send_sems, recv_sems, finish):
    n = len(out_refs)
    x, y, c = _position()
    q = 2 * x + y
    chips = _other_chips(x, y)

    def copy(k, i, quarter, half, to):
        return _gather_copy(out_refs[i], send_sems, recv_sems, k * n + i, quarter, half, to)

    if not finish:
        @pl.when(step == 0)
        def _():
            for i in range(n):
                mine, _ = _halves(out_refs[i].shape[1], c)
                for j, chip in enumerate(chips):
                    copy(j, i, q, mine, (*chip, c)).start()

        @pl.when(step == last)
        def _():
            for j, chip in enumerate(chips):
                qj = 2 * chip[0] + chip[1]
                for i in range(n):
                    mine, _ = _halves(out_refs[i].shape[1], c)
                    copy(j, i, qj, mine, (x, y, c)).wait_recv()
                    copy(3 + j, i, qj, mine, (x, y, 1 - c)).start()
        return

    @pl.when(step == last)
    def _():
        for j, chip in enumerate(chips):
            qj = 2 * chip[0] + chip[1]
            for i in range(n):
                mine, other = _halves(out_refs[i].shape[1], c)
                copy(3 + j, i, qj, other, (x, y, c)).wait_recv()
                copy(j, i, q, mine, (x, y, c)).wait_send()
                copy(3 + j, i, qj, mine, (x, y, c)).wait_send()


def pool_forward(x, w0, wpi, gw, gb, scale, wpo, later):
    s = x.shape[0]
    ts = ROW_TILE
    nt = s // ts
    assert nt >= 2
    n_later = len(later)

    def body(x_ref, w0_ref, wpi_ref, gw_ref, gb_ref, sc_ref, wpo_ref, *rest):
        rest = rest[n_later:]
        h1_ref, pooled_ref, gt_ref, n0_ref = rest[:4]
        later_refs = rest[4:4 + n_later]
        ubuf, send_sems, recv_sems = rest[4 + n_later:]
        i = pl.program_id(0)
        gather_in_background(i, nt - 1, later_refs, send_sems, recv_sems, finish=False)
        xv = x_ref[...]
        r = lax.rsqrt(jnp.mean(xv * xv, axis=-1, keepdims=True) + EPS)
        n0 = _bf(xv * r * w0_ref[...])
        n0_ref[...] = n0
        u = jnp.concatenate([_nn(n0, wpi_ref[0]), _nn(n0, wpi_ref[1])], axis=-1)
        gt = jnp.concatenate([_nn(n0, wpi_ref[2]), _nn(n0, wpi_ref[3])], axis=-1)
        gt_ref[...] = gt

        @pl.when(i == 0)
        def _():
            ubuf[0:HALO, :] = jnp.zeros((HALO, D), F32)

        ubuf[HALO:HALO + ts, :] = u
        inv = _inverse_counts(_row_index(i, ts))
        mixed = []
        for g, w in enumerate(POOL_WINDOWS):
            cols = slice(g * GROUP_DIM, (g + 1) * GROUP_DIM)
            ug = u[:, cols]
            acc = ug
            for j in range(1, w):
                acc = acc + ubuf[HALO - j:HALO - j + ts, cols]
            pooled = _bf(acc * inv[g] - ug)
            pooled_ref[:, cols] = pooled
            mixed.append(_nn(pooled, gw_ref[g]))
        ubuf[0:HALO, :] = ubuf[ts:ts + HALO, :]
        mixed = jnp.concatenate(mixed, axis=-1) + gb_ref[...]
        y = mixed * sc_ref[...] * (gt * _sigmoid(gt))
        h1_ref[...] = xv + _nn(_bf(y), wpo_ref[...])
        gather_in_background(i, nt - 1, later_refs, send_sems, recv_sems, finish=True)

    row = lambda cols: pl.BlockSpec((ts, cols), lambda i: (i, 0))
    outs = pl.pallas_call(
        body, name="pool_forward", grid=(nt,),
        out_shape=[jax.ShapeDtypeStruct((s, D), F32), jax.ShapeDtypeStruct((s, D), BF16),
                   jax.ShapeDtypeStruct((s, D), F32), jax.ShapeDtypeStruct((s, D), BF16)]
                  + [jax.ShapeDtypeStruct(a.shape, a.dtype) for a in later],
        in_specs=[row(D), _full((1, D)), _full((N_CHIPS, D, D // 2)), _full((GROUPS, GROUP_DIM, GROUP_DIM)),
                  _full((1, D)), _full((1, D)), _full((D, D))] + _any_specs(n_later),
        out_specs=[row(D), row(D), row(D), row(D)] + _any_specs(n_later),
        input_output_aliases={7 + k: 4 + k for k in range(n_later)},
        scratch_shapes=[pltpu.VMEM((HALO + ts, D), F32),
                        pltpu.SemaphoreType.DMA((6 * n_later,)), pltpu.SemaphoreType.DMA((6 * n_later,))],
        compiler_params=_params("arbitrary"),
    )(x, w0, wpi, gw, gb, scale, wpo, *later)
    return outs[:4], outs[4:]


def pool_backward(x, dh1, pooled, gt, w0, wpi, gw, gb, scale, wpo, chip_sums):
    s = x.shape[0]
    ts = ROW_TILE
    nt = s // ts
    n_sums = len(chip_sums)

    def body(x_ref, dh1_ref, pooled_ref, gt_ref, w0_ref, wpi_ref, gw_ref, gb_ref, sc_ref, wpo_ref, *rest):
        sum_refs, rest = rest[:n_sums], rest[n_sums:]
        dx_ref, dproj_ref, gpo_ref, ggw_ref, small_ref = rest[:5]
        got_refs = rest[5:5 + n_sums]
        ebuf, send_sems, recv_sems = rest[5 + n_sums:]
        i = pl.program_id(0)
        copies = _scatter_copies(sum_refs, got_refs, send_sems, recv_sems)

        @pl.when(i == 0)
        def _():
            for cp in copies:
                cp.start()

        @pl.when(i == 0)
        def _():
            gpo_ref[...] = jnp.zeros_like(gpo_ref)
            ggw_ref[...] = jnp.zeros_like(ggw_ref)
            small_ref[...] = jnp.zeros_like(small_ref)
            ebuf[ts:ts + HALO, :] = jnp.zeros((HALO, D), F32)

        dh1 = dh1_ref[...]
        dh1_bf = _bf(dh1)
        gt = gt_ref[...]
        sc = sc_ref[...]
        dy = _nt(dh1_bf, wpo_ref[...])
        pooled_bf = []
        mixed = []
        for g in range(GROUPS):
            cols = slice(g * GROUP_DIM, (g + 1) * GROUP_DIM)
            pb = pooled_ref[:, cols]
            pooled_bf.append(pb)
            mixed.append(_nn(pb, gw_ref[g]))
        mixed = jnp.concatenate(mixed, axis=-1) + gb_ref[...]
        sg = _sigmoid(gt)
        silu = gt * sg
        gpo_ref[...] += _tn(_bf(mixed * sc * silu), dh1_bf)
        dmixed = dy * sc * silu
        dgt = dy * mixed * sc * (sg * (1.0 + gt * (1.0 - sg)))
        dproj_ref[:, D:] = _bf(dgt)
        small_ref[1:2, :] += jnp.sum(dy * mixed * silu, axis=0, keepdims=True)
        small_ref[2:3, :] += jnp.sum(dmixed, axis=0, keepdims=True)

        inv = _inverse_counts(_row_index(nt - 1 - i, ts))
        dpooled = []
        for g in range(GROUPS):
            cols = slice(g * GROUP_DIM, (g + 1) * GROUP_DIM)
            dm = _bf(dmixed[:, cols])
            ggw_ref[g] += _tn(pooled_bf[g], dm)
            dp = _nt(dm, gw_ref[g])
            dpooled.append(dp)
            ebuf[0:ts, cols] = dp * inv[g]
        du = []
        for g, w in enumerate(POOL_WINDOWS):
            cols = slice(g * GROUP_DIM, (g + 1) * GROUP_DIM)
            acc = -dpooled[g]
            for j in range(w):
                acc = acc + ebuf[j:j + ts, cols]
            du.append(acc)
        ebuf[ts:ts + HALO, :] = ebuf[0:HALO, :]
        du = _bf(jnp.concatenate(du, axis=-1))
        dproj_ref[:, :D] = du
        dgt_bf = _bf(dgt)
        half = D // 2
        dn0 = (_nt(du[:, :half], wpi_ref[0]) + _nt(du[:, half:], wpi_ref[1])
               + _nt(dgt_bf[:, :half], wpi_ref[2]) + _nt(dgt_bf[:, half:], wpi_ref[3]))

        xv = x_ref[...]
        r = lax.rsqrt(jnp.mean(xv * xv, axis=-1, keepdims=True) + EPS)
        xhat = xv * r
        small_ref[0:1, :] += jnp.sum(dn0 * xhat, axis=0, keepdims=True)
        dxh = dn0 * w0_ref[...]
        dx_ref[...] = dh1 + r * (dxh - xhat * jnp.mean(dxh * xhat, axis=-1, keepdims=True))

        @pl.when(i == nt - 1)
        def _():
            for cp in copies:
                cp.wait()

    row = lambda cols: pl.BlockSpec((ts, cols), lambda i: (nt - 1 - i, 0))
    outs = pl.pallas_call(
        body, name="pool_backward", grid=(nt,),
        out_shape=[jax.ShapeDtypeStruct((s, D), F32), jax.ShapeDtypeStruct((s, 2 * D), BF16),
                   jax.ShapeDtypeStruct((D, D), F32),
                   jax.ShapeDtypeStruct((GROUPS, GROUP_DIM, GROUP_DIM), F32),
                   jax.ShapeDtypeStruct((8, D), F32)] + _scatter_shapes(chip_sums),
        in_specs=[row(D), row(D), row(D), row(D), _full((1, D)), _full((N_CHIPS, D, D // 2)),
                  _full((GROUPS, GROUP_DIM, GROUP_DIM)), _full((1, D)), _full((1, D)), _full((D, D))]
                 + _any_specs(n_sums),
        out_specs=[row(D), row(2 * D), _full((D, D)), _full((GROUPS, GROUP_DIM, GROUP_DIM)), _full((8, D))]
                  + _any_specs(n_sums),
        scratch_shapes=[pltpu.VMEM((ts + HALO, D), F32),
                        pltpu.SemaphoreType.DMA((3 * n_sums,)), pltpu.SemaphoreType.DMA((3 * n_sums,))],
        compiler_params=_params("arbitrary"),
    )(x, dh1, pooled, gt, w0, wpi, gw, gb, scale, wpo, *chip_sums)
    return outs[:5], outs[5:]


def gla_project(h1, w1, wgi, wlow, wgk, bgk, later):
    s = h1.shape[0]
    ts = ROW_TILE
    nt = s // ts
    assert nt >= 2
    n_later = len(later)

    def body(h_ref, w1_ref, wgi_ref, wlow_ref, wgk_ref, bgk_ref, *rest):
        rest = rest[n_later:]
        qk_ref, v_ref, gate_ref, low_ref, cum_ref, n1_ref = rest[:6]
        later_refs = rest[6:6 + n_later]
        send_sems, recv_sems = rest[6 + n_later:]
        gather_in_background(pl.program_id(0), nt - 1, later_refs, send_sems, recv_sems, finish=False)
        hv = h_ref[...]
        r = lax.rsqrt(jnp.mean(hv * hv, axis=-1, keepdims=True) + EPS)
        n1 = _bf(hv * r * w1_ref[...])
        n1_ref[...] = n1
        qk_ref[...] = _nn(n1, wgi_ref[:, 0:2 * KEY_W])
        v_ref[...] = _bf(_nn(n1, wgi_ref[:, 2 * KEY_W:2 * KEY_W + D]))
        gate_ref[...] = _nn(n1, wgi_ref[:, 2 * KEY_W + D:GLA_MAIN])
        low = _bf(_nn(n1, wlow_ref[...]))
        low_ref[...] = low
        z = _nn(low, wgk_ref[...]) + bgk_ref[...]
        lg = (jnp.minimum(z, 0.0) - jnp.log(1.0 + jnp.exp(-jnp.abs(z)))) / GATE_NORM
        lower_f = _chunk_masks()[0].astype(F32)
        for r0 in range(0, ts, CHUNK):
            cum_ref[r0:r0 + CHUNK, :] = _nn_exact(lower_f, lg[r0:r0 + CHUNK, :])
        gather_in_background(pl.program_id(0), nt - 1, later_refs, send_sems, recv_sems, finish=True)

    row = lambda cols: pl.BlockSpec((ts, cols), lambda i: (i, 0))
    outs = pl.pallas_call(
        body, name="gla_project", grid=(nt,),
        out_shape=[jax.ShapeDtypeStruct((s, D), F32), jax.ShapeDtypeStruct((s, D), BF16),
                   jax.ShapeDtypeStruct((s, D), F32), jax.ShapeDtypeStruct((s, RANK_PAD), BF16),
                   jax.ShapeDtypeStruct((s, KEY_W), F32), jax.ShapeDtypeStruct((s, D), BF16)]
                  + [jax.ShapeDtypeStruct(a.shape, a.dtype) for a in later],
        in_specs=[row(D), _full((1, D)), _full((D, GLA_MAIN)), _full((D, RANK_PAD)),
                  _full((RANK_PAD, KEY_W)), _full((1, KEY_W))] + _any_specs(n_later),
        out_specs=[row(D), row(D), row(D), row(RANK_PAD), row(KEY_W), row(D)] + _any_specs(n_later),
        input_output_aliases={6 + k: 6 + k for k in range(n_later)},
        scratch_shapes=[pltpu.SemaphoreType.DMA((6 * n_later,)), pltpu.SemaphoreType.DMA((6 * n_later,))],
        compiler_params=_params("arbitrary"),
    )(h1, w1, wgi, wlow, wgk, bgk, *later)
    return outs[:6], outs[6:]


GLA_BLOCK = 512
CHUNKS_PER_BLOCK = GLA_BLOCK // CHUNK


def _chunk_masks():
    t = lax.broadcasted_iota(jnp.int32, (CHUNK, CHUNK), 0)
    u = lax.broadcasted_iota(jnp.int32, (CHUNK, CHUNK), 1)
    return t >= u, t <= u


def _gla_chunk_terms(q, cum):
    ep = jnp.exp(cum)
    en = jnp.exp(-cum)
    qs = q * (HEAD_K ** -0.5)
    last = cum[CHUNK - 1:CHUNK, :]
    ed = jnp.exp(last - cum)
    dec = jnp.exp(last)
    return ep, en, qs, ed, dec


def gla_forward(qk, v, cum):
    s = qk.shape[0]
    nb = s // GLA_BLOCK
    nc = s // CHUNK

    def body(q_ref, k_ref, v_ref, cum_ref, o_ref, st_ref, sc_ref, state):
        @pl.when(pl.program_id(0) == 0)
        def _():
            state[...] = jnp.zeros_like(state)

        lower, _ = _chunk_masks()

        def chunk(cc, carry):
            rows = pl.ds(pl.multiple_of(cc * CHUNK, CHUNK), CHUNK)
            for h in range(HEADS):
                kc = slice(h * HEAD_K, (h + 1) * HEAD_K)
                vc = slice(h * HEAD_V, (h + 1) * HEAD_V)
                q = q_ref[rows, kc]
                k = k_ref[rows, kc]
                v = v_ref[rows, vc]
                ep, en, qs, ed, dec = _gla_chunk_terms(q, cum_ref[rows, kc])
                a = _bf(qs * ep)
                fwd = _nt(a, _bf(k * en))
                bwd = _nt(_bf(qs * en), _bf(k * ep))
                scores = _bf(jnp.where(lower, fwd, bwd))
                sc_ref[rows, h * CHUNK:(h + 1) * CHUNK] = scores
                st = state[h]
                st_ref[cc, h] = st
                o_ref[rows, vc] = _nn(scores, v) + _nt(a, _bf(st))
                state[h] = st * dec + _tn(v, _bf(k * ed))
            return carry

        lax.fori_loop(0, CHUNKS_PER_BLOCK, chunk, 0, unroll=4)

    return pl.pallas_call(
        body, name="gla_forward", grid=(nb,),
        out_shape=(jax.ShapeDtypeStruct((s, D), F32),
                   jax.ShapeDtypeStruct((nc, HEADS, HEAD_V, HEAD_K), F32),
                   jax.ShapeDtypeStruct((s, HEADS * CHUNK), BF16)),
        in_specs=[pl.BlockSpec((GLA_BLOCK, KEY_W), lambda i: (i, 0)),
                  pl.BlockSpec((GLA_BLOCK, KEY_W), lambda i: (i, 1)),
                  pl.BlockSpec((GLA_BLOCK, D), lambda i: (i, 0)),
                  pl.BlockSpec((GLA_BLOCK, KEY_W), lambda i: (i, 0))],
        out_specs=(pl.BlockSpec((GLA_BLOCK, D), lambda i: (i, 0)),
                   pl.BlockSpec((CHUNKS_PER_BLOCK, HEADS, HEAD_V, HEAD_K), lambda i: (i, 0, 0, 0)),
                   pl.BlockSpec((GLA_BLOCK, HEADS * CHUNK), lambda i: (i, 0))),
        scratch_shapes=[pltpu.VMEM((HEADS, HEAD_V, HEAD_K), F32)],
        compiler_params=_params("arbitrary"),
    )(qk, qk, v, cum)


def gla_backward(qk, v, cum, do, states, scores):
    s = qk.shape[0]
    nb = s // GLA_BLOCK

    def body(q_ref, k_ref, v_ref, cum_ref, do_ref, st_ref, sc_ref, dq_ref, dk_ref, dv_ref, dcum_ref, dstate):
        @pl.when(pl.program_id(0) == 0)
        def _():
            dstate[...] = jnp.zeros_like(dstate)

        lower, _ = _chunk_masks()
        is_last = lax.broadcasted_iota(jnp.int32, (CHUNK, HEAD_K), 0) == CHUNK - 1

        def chunk(step, carry):
            cc = CHUNKS_PER_BLOCK - 1 - step
            rows = pl.ds(pl.multiple_of(cc * CHUNK, CHUNK), CHUNK)
            for h in range(HEADS):
                kc = slice(h * HEAD_K, (h + 1) * HEAD_K)
                vc = slice(h * HEAD_V, (h + 1) * HEAD_V)
                q = q_ref[rows, kc]
                k = k_ref[rows, kc]
                v = v_ref[rows, vc]
                do_c = do_ref[rows, vc]
                ep, en, qs, ed, dec = _gla_chunk_terms(q, cum_ref[rows, kc])
                a = _bf(qs * ep)
                b = _bf(k * en)
                c = _bf(qs * en)
                dk_dec = _bf(k * ep)
                kd = _bf(k * ed)
                scores = sc_ref[rows, h * CHUNK:(h + 1) * CHUNK]
                st = st_ref[cc, h]
                dst = dstate[h]
                dst_bf = _bf(dst)

                dscores = _nt(do_c, v)
                dfwd = _bf(jnp.where(lower, dscores, 0.0))
                dbwd = _bf(jnp.where(lower, 0.0, dscores))
                dv_ref[rows, vc] = _bf(_tn(scores, do_c) + _nt(kd, dst_bf))
                da = _nn(dfwd, b) + _nn(do_c, _bf(st))
                db = _tn(dfwd, a)
                dc = _nn(dbwd, dk_dec)
                ddk = _tn(dbwd, c)
                dkd = _nn(v, dst_bf)
                ddec = jnp.sum(dst * st, axis=0, keepdims=True)
                dstate[h] = dst * dec + _tn(do_c, a)

                m = dkd * k * ed
                dq_ref[rows, kc] = _bf((da * ep + dc * en) * (HEAD_K ** -0.5))
                dk_ref[rows, kc] = _bf(db * en + ddk * ep + dkd * ed)
                dcum = (da * qs + ddk * k) * ep - (db * k + dc * qs) * en - m
                dlast = jnp.sum(m, axis=0, keepdims=True) + ddec * dec
                dcum_ref[rows, kc] = dcum + jnp.where(is_last, dlast, 0.0)
            return carry

        lax.fori_loop(0, CHUNKS_PER_BLOCK, chunk, 0, unroll=4)

    rev = lambda cols, col_block: pl.BlockSpec((GLA_BLOCK, cols), lambda i: (nb - 1 - i, col_block))
    return pl.pallas_call(
        body, name="gla_backward", grid=(nb,),
        out_shape=(jax.ShapeDtypeStruct((s, KEY_W), BF16), jax.ShapeDtypeStruct((s, KEY_W), BF16),
                   jax.ShapeDtypeStruct((s, D), BF16), jax.ShapeDtypeStruct((s, KEY_W), F32)),
        in_specs=[rev(KEY_W, 0), rev(KEY_W, 1), rev(D, 0), rev(KEY_W, 0), rev(D, 0),
                  pl.BlockSpec((CHUNKS_PER_BLOCK, HEADS, HEAD_V, HEAD_K), lambda i: (nb - 1 - i, 0, 0, 0)),
                  rev(HEADS * CHUNK, 0)],
        out_specs=(rev(KEY_W, 0), rev(KEY_W, 0), rev(D, 0), rev(KEY_W, 0)),
        scratch_shapes=[pltpu.VMEM((HEADS, HEAD_V, HEAD_K), F32)],
        compiler_params=_params("arbitrary"),
    )(qk, qk, v, cum, do, states, scores)


def head_and_loss(o, gate, h1, target, hw, wgo, wf):
    s = o.shape[0]
    ts = ROW_TILE

    def body(o_ref, gate_ref, h1_ref, tgt_ref, hw_ref, wgo_ref, wf_ref,
             dh2_ref, do_ref, dgate_ref, ggo_ref, small_ref):
        @pl.when(pl.program_id(0) == 0)
        def _():
            ggo_ref[...] = jnp.zeros_like(ggo_ref)
            small_ref[...] = jnp.zeros_like(small_ref)

        gate = gate_ref[...]
        hw = hw_ref[...]
        sg = _sigmoid(gate)
        silu = gate * sg
        ohat, ro = [], []
        for h in range(HEADS):
            oh = o_ref[:, h * HEAD_V:(h + 1) * HEAD_V]
            rh = lax.rsqrt(jnp.mean(oh * oh, axis=-1, keepdims=True) + EPS)
            ro.append(rh)
            ohat.append(oh * rh)
        ohat = jnp.concatenate(ohat, axis=-1)
        on = ohat * hw
        y2 = _bf(on * silu)
        h2 = h1_ref[...] + _nn(y2, wgo_ref[...])
        rf = lax.rsqrt(jnp.mean(h2 * h2, axis=-1, keepdims=True) + EPS)
        h2hat = h2 * rf
        wf = wf_ref[...]
        diff = h2hat * wf - tgt_ref[...]
        small_ref[2:3, :] += jnp.zeros((1, D), F32) + 0.5 * jnp.sum(diff * diff) / D
        dout = diff / D
        small_ref[0:1, :] += jnp.sum(dout * h2hat, axis=0, keepdims=True)
        dxh = dout * wf
        dh2 = rf * (dxh - h2hat * jnp.mean(dxh * h2hat, axis=-1, keepdims=True))
        dh2_ref[...] = dh2
        dh2_bf = _bf(dh2)
        ggo_ref[...] += _tn(y2, dh2_bf)
        dy2 = _nt(dh2_bf, wgo_ref[...])
        don = dy2 * silu
        dgate_ref[...] = _bf(dy2 * on * (sg * (1.0 + gate * (1.0 - sg))))
        ghw = jnp.sum(don * ohat, axis=0, keepdims=True)
        small_ref[1:2, 0:HEAD_V] += sum(ghw[:, h * HEAD_V:(h + 1) * HEAD_V] for h in range(HEADS))
        dohat = don * hw
        for h in range(HEADS):
            cols = slice(h * HEAD_V, (h + 1) * HEAD_V)
            oh, dh = ohat[:, cols], dohat[:, cols]
            do_ref[:, cols] = _bf(ro[h] * (dh - oh * jnp.mean(dh * oh, axis=-1, keepdims=True)))

    row = lambda cols: pl.BlockSpec((ts, cols), lambda i: (i, 0))
    act = jax.ShapeDtypeStruct((s, D), F32)
    act_bf = jax.ShapeDtypeStruct((s, D), BF16)
    return pl.pallas_call(
        body, name="head_and_loss", grid=(s // ts,),
        out_shape=(act, act_bf, act_bf, jax.ShapeDtypeStruct((D, D), F32), jax.ShapeDtypeStruct((8, D), F32)),
        in_specs=[row(D), row(D), row(D), row(D),
                  _full((1, D)), _full((D, D)), _full((1, D))],
        out_specs=(row(D), row(D), row(D), _full((D, D)), _full((8, D))),
        compiler_params=_params("arbitrary"),
    )(o, gate, h1, target, hw, wgo, wf)


def gla_project_backward(dq, dk, dv, dgate, dcum, low, h1, dh2, w1, wgi, wlow, wgk, bgk):
    s = h1.shape[0]
    ts = ROW_TILE

    def body(dq_ref, dk_ref, dv_ref, dgate_ref, dcum_ref, low_ref, h1_ref, dh2_ref, w1_ref,
             wgi_ref, wlow_ref, wgk_ref, bgk_ref, dh1_ref, dproj_ref, dlow_ref, ggk_ref, small_ref):
        @pl.when(pl.program_id(0) == 0)
        def _():
            ggk_ref[...] = jnp.zeros_like(ggk_ref)
            small_ref[...] = jnp.zeros_like(small_ref)

        low = low_ref[...]
        z = _nn(low, wgk_ref[...]) + bgk_ref[...]
        upper_f = _chunk_masks()[1].astype(F32)
        dlg = jnp.concatenate([_nn_exact(upper_f, dcum_ref[r0:r0 + CHUNK, :]) for r0 in range(0, ts, CHUNK)],
                              axis=0)
        dz = dlg * (1.0 / GATE_NORM) * _sigmoid(-z)
        dz_bf = _bf(dz)
        ggk_ref[...] += _tn(low, dz_bf)
        small_ref[1:2, 0:KEY_W] += jnp.sum(dz, axis=0, keepdims=True)
        dlow = _bf(_nt(dz_bf, wgk_ref[...]))
        dlow_ref[...] = dlow
        dn1 = _nt(dlow, wlow_ref[...])
        for ref, lo, hi in ((dq_ref, 0, KEY_W), (dk_ref, KEY_W, 2 * KEY_W),
                            (dv_ref, 2 * KEY_W, 2 * KEY_W + D), (dgate_ref, 2 * KEY_W + D, GLA_MAIN)):
            piece = ref[...]
            dproj_ref[:, lo:hi] = piece
            dn1 = dn1 + _nt(piece, wgi_ref[:, lo:hi])
        hv = h1_ref[...]
        r = lax.rsqrt(jnp.mean(hv * hv, axis=-1, keepdims=True) + EPS)
        hhat = hv * r
        small_ref[0:1, :] += jnp.sum(dn1 * hhat, axis=0, keepdims=True)
        dxh = dn1 * w1_ref[...]
        dh1_ref[...] = dh2_ref[...] + r * (dxh - hhat * jnp.mean(dxh * hhat, axis=-1, keepdims=True))

    row = lambda cols: pl.BlockSpec((ts, cols), lambda i: (i, 0))
    return pl.pallas_call(
        body, name="gla_project_backward", grid=(s // ts,),
        out_shape=(jax.ShapeDtypeStruct((s, D), F32), jax.ShapeDtypeStruct((s, GLA_MAIN), BF16),
                   jax.ShapeDtypeStruct((s, RANK_PAD), BF16), jax.ShapeDtypeStruct((RANK_PAD, KEY_W), F32),
                   jax.ShapeDtypeStruct((8, D), F32)),
        in_specs=[row(KEY_W), row(KEY_W), row(D), row(D), row(KEY_W), row(RANK_PAD), row(D), row(D),
                  _full((1, D)), _full((D, GLA_MAIN)), _full((D, RANK_PAD)), _full((RANK_PAD, KEY_W)),
                  _full((1, KEY_W))],
        out_specs=(row(D), row(GLA_MAIN), row(RANK_PAD), _full((RANK_PAD, KEY_W)), _full((8, D))),
        compiler_params=_params("arbitrary"),
    )(dq, dk, dv, dgate, dcum, low, h1, dh2, w1, wgi, wlow, wgk, bgk)


def _groups_from_quarters(a):
    return a.reshape(N_CHIPS, GROUPS, 64, GROUP_DIM).transpose(1, 0, 2, 3).reshape(GROUPS, GROUP_DIM, GROUP_DIM)


def _quarters_from_groups(a):
    return a.reshape(GROUPS, N_CHIPS, 64, GROUP_DIM).transpose(1, 0, 2, 3).reshape(N_CHIPS, GROUP_DIM, GROUP_DIM)


def _pad_row(*pieces):
    flat = jnp.concatenate([p.reshape(-1).astype(F32) for p in pieces])
    return jnp.pad(flat, (0, D - flat.shape[0])).reshape(1, D)


def _gla_in_weights(wgi_q):
    wgi_all = jnp.concatenate([wgi_q[q] for q in range(N_CHIPS)], axis=1)
    wlow = jnp.pad(wgi_all[:, GLA_MAIN:], ((0, 0), (0, RANK_PAD - GATE_RANK)))
    return wgi_all, wlow


def _small_sums(small_top, small_gla, small_pool, g_gk_pad):
    return jnp.concatenate([
        small_pool[0:1], small_gla[0:1],
        small_pool[1:2],
        small_top[0:1],
        small_top[2:3],
        _pad_row(small_gla[1, 0:KEY_W], small_top[1, 0:HEAD_V]),
        small_pool[2:3],
        g_gk_pad[:GATE_RANK].reshape(8, D),
        jnp.zeros((1, D), F32)], axis=0)


def local_gradients(xs, target, w0, w1, wf, wpi, gw, gb, scale, wpo, gla_quarters, wgk, bgk, hw_tiled, place):
    wgi_q, wgo_q = gla_quarters
    (h1, pooled, gt, n0), (wgi_q,) = pool_forward(xs, w0, wpi, gw, gb, scale, wpo, [wgi_q])
    wgi, wlow = _gla_in_weights(wgi_q)
    (qk, v, gate, low, cum, n1), (wgo_q,) = gla_project(h1, w1, wgi, wlow, wgk, bgk, [wgo_q])
    wgo = wgo_q.reshape(D, D)
    o, states, scores = gla_forward(qk, v, cum)

    dh2, do, dgate, g_gla_out, small_top = head_and_loss(o, gate, h1, target, hw_tiled, wgo, wf)
    dq, dk, dv, dcum = gla_backward(qk, v, cum, do, states, scores)
    dh1, dproj, dlow, g_gk_pad, small_gla = gla_project_backward(
        dq, dk, dv, dgate, dcum, low, h1, dh2, w1, wgi, wlow, wgk, bgk)
    g_gla_in = jnp.concatenate([matmul_tn(n1, dproj, "grad_gla_in"),
                                matmul_tn(n1, dlow, "grad_gla_low")[:, :GATE_RANK]], axis=1)

    def chip_sums(grads, names, tag):
        theirs = exchange_with_sibling(grads, "exchange_with_sibling_" + tag)
        return [add_halves(g, t, place, "add_halves_" + n) for g, t, n in zip(grads, theirs, names)]

    gla_names = ("gla_in", "gla_out")
    gla_sums = chip_sums(
        [jnp.stack([g_gla_in[:, GLA_IN_QUARTER * q:GLA_IN_QUARTER * (q + 1)] for q in range(N_CHIPS)]),
         g_gla_out.reshape(N_CHIPS, D // N_CHIPS, D)], gla_names, "gla")
    (dx, dpool, g_pool_out, g_group_w, small_pool), gla_got = pool_backward(
        xs, dh1, pooled, gt, w0, wpi, gw, gb, scale, wpo, [b for _, b in gla_sums])
    g_pool_in = matmul_tn(n0, dpool, "grad_pool_in", by_column_tile=True)

    pool_names = ("pool_in", "group", "pool_out")
    pool_sums = chip_sums(
        [g_pool_in, _quarters_from_groups(g_group_w), g_pool_out.reshape(N_CHIPS, D // N_CHIPS, D)],
        pool_names, "pool")
    pool_got = scatter_to_owners([b for _, b in pool_sums], "scatter_to_owners_pool")
    reduced, total = join_halves(
        [add_parts(f, g, place, "add_parts_" + n) for (f, _), g, n in
         zip(pool_sums + gla_sums, list(pool_got) + list(gla_got), pool_names + gla_names)],
        _small_sums(small_top, small_gla, small_pool, g_gk_pad))
    return dx, reduced, total


def kernel(x, norm_w, pool_in_w, pool_group_w, pool_group_b, pool_scale, pool_out_w, gla_in_w, gla_gk_w, gla_gk_b, gla_head_norm_w, gla_out_w, final_norm_w, loss_target, m_norm_w, m_pool_in_w, m_pool_group_w, m_pool_group_b, m_pool_scale, m_pool_out_w, m_gla_in_w, m_gla_gk_w, m_gla_gk_b, m_gla_head_norm_w, m_gla_out_w, m_final_norm_w, v_norm_w, v_pool_in_w, v_pool_group_w, v_pool_group_b, v_pool_scale, v_pool_out_w, v_gla_in_w, v_gla_gk_w, v_gla_gk_b, v_gla_head_norm_w, v_gla_out_w, v_final_norm_w):
    s = x.shape[1]
    xs = x[0]
    target = loss_target[0]
    q_chip = 2 * lax.axis_index("x") + lax.axis_index("y")
    place = jnp.stack([lax.axis_index("c"), q_chip]).astype(jnp.int32)

    small_in = jnp.concatenate([
        _pad_row(gla_gk_b[0], gla_head_norm_w[0], pool_group_b[0]),
        gla_gk_w[0].reshape(2, D),
        jnp.zeros((5, D), F32)], axis=0)
    (wpi, gw_q, wpo_q, wgi_q, wgo_q), small_all = allgather_weights(
        [pool_in_w[0], pool_group_w[0].reshape(GROUP_DIM, GROUP_DIM), pool_out_w[0], gla_in_w[0], gla_out_w[0]],
        exchange=(True, True, True, False, False), small=small_in)
    gw = _groups_from_quarters(gw_q)
    wpo = wpo_q.reshape(D, D)
    small_all = small_all[0::2]
    bgk = small_all[:, 0, 0:128].reshape(1, KEY_W)
    hw = small_all[:, 0, 128:192].reshape(1, HEAD_V)
    gb = jnp.concatenate([small_all[q, 0, 192:448].reshape(GROUPS, 64) for q in range(N_CHIPS)],
                         axis=1).reshape(1, D)
    wgk16 = jnp.concatenate([small_all[q, 1:3].reshape(GATE_RANK, 128) for q in range(N_CHIPS)], axis=1)
    wgk = _bf(jnp.pad(wgk16, ((0, RANK_PAD - GATE_RANK), (0, 0))))
    hw_tiled = jnp.tile(hw, (1, HEADS))

    w0 = norm_w[0:1]
    w1 = norm_w[1:2]
    wf = final_norm_w.reshape(1, D)

    dx, reduced, total = local_gradients(
        xs, target, w0, w1, wf, wpi, gw, gb, pool_scale, wpo, [wgi_q, wgo_q], wgk, bgk, hw_tiled, place)
    r_pool_in, r_group_w, r_pool_out, r_gla_in, r_gla_out = reduced
    r_group_w = r_group_w.reshape(GROUPS, 64, GROUP_DIM)

    loss = total[4, 0]
    g_norm = total[0:2]
    g_scale = total[2:3]
    g_final = total[3]
    pick = lambda full, width: lax.dynamic_slice_in_dim(full, q_chip * width, width, axis=-1)
    g_gk_b = pick(total[5:6, 0:KEY_W], 128)
    g_hnw = pick(total[5:6, KEY_W:KEY_W + HEAD_V], 64)
    g_group_b = pick(total[6].reshape(GROUPS, GROUP_DIM), 64)[None]
    g_gk_w = pick(total[7:15].reshape(GATE_RANK, KEY_W), 128)[None]

    def step_lane_rows(name, w, g, m, v):
        turn = lambda a: jnp.transpose(a, (2, 0, 1))
        back = lambda a: jnp.transpose(a, (1, 2, 0))
        g_t = turn(g)
        d, nm, nv = adamw_rows(turn(w), g_t, turn(m), turn(v), "adamw_" + name)
        return back(g_t), back(d), back(nm), back(nv)

    def step(name, w, g, m, v):
        shape = w.shape
        as2d = lambda a: a.reshape(-1, shape[-1])
        d, nm, nv = adamw(as2d(w), as2d(g), as2d(m), as2d(v), "adamw_" + name)
        return g.reshape(shape), d.reshape(shape), nm.reshape(shape), nv.reshape(shape)

    small_names = ("norm_w", "pool_group_b", "pool_scale", "gla_gk_w", "gla_gk_b", "gla_head_norm_w",
                   "final_norm_w")
    small_args = [(norm_w, g_norm, m_norm_w, v_norm_w),
                  (pool_group_b, g_group_b, m_pool_group_b, v_pool_group_b),
                  (pool_scale, g_scale, m_pool_scale, v_pool_scale),
                  (gla_gk_w, g_gk_w, m_gla_gk_w, v_gla_gk_w),
                  (gla_gk_b, g_gk_b, m_gla_gk_b, v_gla_gk_b),
                  (gla_head_norm_w, g_hnw, m_gla_head_norm_w, v_gla_head_norm_w),
                  (final_norm_w, g_final, m_final_norm_w, v_final_norm_w)]
    as2d = lambda a, w: a.reshape(-1, w.shape[-1])
    small_out = adamw_small([tuple(as2d(a, p[0]) for a in p) for p in small_args])
    small = {n: (p[1].reshape(p[0].shape),) + tuple(o.reshape(p[0].shape) for o in out)
             for n, p, out in zip(small_names, small_args, small_out)}
    results = [
        small["norm_w"],
        step("pool_in_w", pool_in_w, r_pool_in[None], m_pool_in_w, v_pool_in_w),
        step("pool_group_w", pool_group_w, r_group_w[None], m_pool_group_w, v_pool_group_w),
        small["pool_group_b"],
        small["pool_scale"],
        step("pool_out_w", pool_out_w, r_pool_out[None], m_pool_out_w, v_pool_out_w),
        step_lane_rows("gla_in_w", gla_in_w, r_gla_in[None], m_gla_in_w, v_gla_in_w),
        small["gla_gk_w"],
        small["gla_gk_b"],
        small["gla_head_norm_w"],
        step("gla_out_w", gla_out_w, r_gla_out[None], m_gla_out_w, v_gla_out_w),
        small["final_norm_w"],
    ]
    grads, deltas, new_m, new_v = zip(*results)
    return (loss, dx[None], *grads, *deltas, *new_m, *new_v)
```

```python
import functools

import jax
import jax.numpy as jnp
from jax import lax
from jax.experimental import pallas as pl
from jax.experimental.pallas import tpu as pltpu

F32 = jnp.float32
BF16 = jnp.bfloat16
MESH = pl.DeviceIdType.MESH

D = 1024
POOL_WINDOWS = (2, 4, 8, 16)
GROUPS = 4
GROUP_DIM = 256
HEADS = 4
HEAD_K = 128
HEAD_V = 256
KEY_W = 512
CHUNK = 64
GATE_RANK = 16
GATE_NORM = 16.0
GLA_IN = 3088
GLA_MAIN = 3072
RANK_PAD = 128
EPS = 1e-6
HALO = 16

ADAM_LR = 0.001
ADAM_B1 = 0.9
ADAM_B2 = 0.999
ADAM_EPS = 1e-08
ADAM_WD = 0.01
ADAM_STEP = 10

N_CHIPS = 4
N_DEV = 8
GLA_IN_QUARTER = GLA_IN // N_CHIPS

VMEM_LIMIT = 56 * 1024 * 1024


def _nn(a, b):
    return lax.dot_general(a, b, (((1,), (0,)), ((), ())), preferred_element_type=F32)


def _nt(a, b):
    return lax.dot_general(a, b, (((1,), (1,)), ((), ())), preferred_element_type=F32)


def _tn(a, b):
    return lax.dot_general(a, b, (((0,), (0,)), ((), ())), preferred_element_type=F32)


def _nn_exact(a, b):
    return lax.dot_general(a, b, (((1,), (0,)), ((), ())), preferred_element_type=F32,
                           precision=lax.Precision.HIGHEST)


def _bf(a):
    return a.astype(BF16)


def _params(*sem):
    return pltpu.CompilerParams(dimension_semantics=sem, vmem_limit_bytes=VMEM_LIMIT)


def _full(shape):
    return pl.BlockSpec(shape, lambda i: (0,) * len(shape))


def _position():
    return lax.axis_index("x"), lax.axis_index("y"), lax.axis_index("c")


def _gather_small(in_ref, all_ref, send_sems, recv_sems, local_sem):
    x, y, c = _position()
    me = 4 * x + 2 * y + c
    mine = pltpu.make_async_copy(in_ref, all_ref.at[me], local_sem)
    mine.start()
    sends = []
    for k in range(N_DEV - 1):
        fx, fy, fc = (k + 1) >> 2 & 1, (k + 1) >> 1 & 1, (k + 1) & 1
        cp = pltpu.make_async_remote_copy(
            src_ref=in_ref, dst_ref=all_ref.at[me],
            send_sem=send_sems.at[k], recv_sem=recv_sems.at[k],
            device_id=(x ^ fx, y ^ fy, c ^ fc), device_id_type=MESH)
        cp.start()
        sends.append(cp)
    for k in range(N_DEV - 1):
        fx, fy, fc = (k + 1) >> 2 & 1, (k + 1) >> 1 & 1, (k + 1) & 1
        src_dev = 4 * (x ^ fx) + 2 * (y ^ fy) + (c ^ fc)
        pltpu.make_async_remote_copy(
            src_ref=in_ref, dst_ref=all_ref.at[src_dev],
            send_sem=send_sems.at[k], recv_sem=recv_sems.at[k],
            device_id=(x, y, c), device_id_type=MESH).wait_recv()
    for cp in sends:
        cp.wait_send()
    mine.wait()


SMALL_SEMS = [pltpu.SemaphoreType.DMA((N_DEV - 1,)), pltpu.SemaphoreType.DMA((N_DEV - 1,)),
              pltpu.SemaphoreType.DMA]
VMEM_SPEC = pl.BlockSpec(memory_space=pltpu.VMEM)


def _other_chips(x, y):
    return [(1 - x, y), (x, 1 - y), (1 - x, 1 - y)]


def _any_specs(n):
    return [pl.BlockSpec(memory_space=pl.ANY)] * n


def _halves(rows, c):
    half = rows // 2
    return pl.ds(c * half, half), pl.ds((1 - c) * half, half)


CAST_ROWS = 256


def _gather_copy(out_ref, send_sems, recv_sems, k, quarter, half, to, src=None):
    dst = out_ref.at[quarter, half]
    return pltpu.make_async_remote_copy(
        src_ref=dst if src is None else src, dst_ref=dst,
        send_sem=send_sems.at[k], recv_sem=recv_sems.at[k], device_id=to, device_id_type=MESH)


def allgather_weights(quarters, exchange, small):
    n = len(quarters)
    shapes = [w.shape for w in quarters]
    moved = [i for i in range(n) if exchange[i]]

    def body(*refs):
        w_refs, small_ref = refs[:n], refs[n]
        out_refs, small_all_ref = refs[n + 1:2 * n + 1], refs[2 * n + 1]
        refs = refs[2 * n + 2:]
        f32_bufs, bf_bufs = refs[:n], refs[n:2 * n]
        send_sems, recv_sems, local_sems = refs[2 * n:2 * n + 3]
        x, y, c = _position()
        q = 2 * x + y
        sibling = (x, y, 1 - c)
        chips = _other_chips(x, y)

        def copy(k, i, quarter, half, to, src=None):
            return _gather_copy(out_refs[i], send_sems, recv_sems, k * n + i, quarter, half, to, src)

        loads = [pltpu.make_async_copy(w_refs[i], f32_bufs[i], local_sems.at[i]) for i in range(n)]
        for cp in loads:
            cp.start()
        keeps, sends = [], []
        for i in range(n):
            loads[i].wait()
            for r0 in range(0, shapes[i][0], CAST_ROWS):
                bf_bufs[i][r0:r0 + CAST_ROWS, :] = _bf(f32_bufs[i][r0:r0 + CAST_ROWS, :])
            keep = pltpu.make_async_copy(bf_bufs[i], out_refs[i].at[q], local_sems.at[n + i])
            keep.start()
            keeps.append(keep)
            if not exchange[i]:
                continue
            mine, _ = _halves(shapes[i][0], c)
            for j, chip in enumerate(chips):
                cp = copy(j, i, q, mine, (*chip, c), src=bf_bufs[i].at[mine])
                cp.start()
                sends.append(cp)
        for j, chip in enumerate(chips):
            qj = 2 * chip[0] + chip[1]
            for i in moved:
                mine, _ = _halves(shapes[i][0], c)
                copy(j, i, qj, mine, (x, y, c)).wait_recv()
                cp = copy(3 + j, i, qj, mine, sibling)
                cp.start()
                sends.append(cp)
        for j, chip in enumerate(chips):
            qj = 2 * chip[0] + chip[1]
            for i in moved:
                _, other = _halves(shapes[i][0], c)
                copy(3 + j, i, qj, other, (x, y, c)).wait_recv()
        _gather_small(small_ref, small_all_ref, *refs[2 * n + 3:])
        for cp in sends:
            cp.wait_send()
        for cp in keeps:
            cp.wait()

    outs = pl.pallas_call(
        body, name="allgather_weights",
        out_shape=[jax.ShapeDtypeStruct((N_CHIPS, *s), BF16) for s in shapes]
                  + [jax.ShapeDtypeStruct((N_DEV, *small.shape), small.dtype)],
        in_specs=_any_specs(n) + [VMEM_SPEC], out_specs=_any_specs(n) + [VMEM_SPEC],
        scratch_shapes=([pltpu.VMEM(s, F32) for s in shapes] + [pltpu.VMEM(s, BF16) for s in shapes]
                        + [pltpu.SemaphoreType.DMA((6 * n,)), pltpu.SemaphoreType.DMA((6 * n,)),
                           pltpu.SemaphoreType.DMA((2 * n,))] + SMALL_SEMS),
        compiler_params=pltpu.CompilerParams(vmem_limit_bytes=VMEM_LIMIT),
    )(*quarters, small)
    return outs[:n], outs[n]


def exchange_with_sibling(grads, name):
    n = len(grads)

    def body(*refs):
        g_refs, theirs_refs = refs[:n], refs[n:2 * n]
        send_sems, recv_sems = refs[2 * n:]
        x, y, c = _position()
        copies = []
        for i in range(n):
            _, other = _halves(g_refs[i].shape[1], c)
            cp = pltpu.make_async_remote_copy(
                src_ref=g_refs[i].at[:, other], dst_ref=theirs_refs[i],
                send_sem=send_sems.at[i], recv_sem=recv_sems.at[i],
                device_id=(x, y, 1 - c), device_id_type=MESH)
            cp.start()
            copies.append(cp)
        for cp in copies:
            cp.wait()

    return pl.pallas_call(
        body, name=name,
        out_shape=[jax.ShapeDtypeStruct((N_CHIPS, g.shape[1] // 2, g.shape[2]), BF16) for g in grads],
        in_specs=_any_specs(n), out_specs=_any_specs(n),
        scratch_shapes=[pltpu.SemaphoreType.DMA((n,)), pltpu.SemaphoreType.DMA((n,))],
    )(*grads)


def _scatter_copies(b_refs, got_refs, send_sems, recv_sems):
    n = len(b_refs)
    x, y, c = _position()
    copies = []
    for j, chip in enumerate(_other_chips(x, y)):
        qj = 2 * chip[0] + chip[1]
        for i in range(n):
            copies.append(pltpu.make_async_remote_copy(
                src_ref=b_refs[i].at[qj], dst_ref=got_refs[i].at[j],
                send_sem=send_sems.at[j * n + i], recv_sem=recv_sems.at[j * n + i],
                device_id=(*chip, c), device_id_type=MESH))
    return copies


def _scatter_shapes(chip_sums):
    return [jax.ShapeDtypeStruct((N_CHIPS - 1, *b.shape[1:]), BF16) for b in chip_sums]


def scatter_to_owners(chip_sums, name):
    n = len(chip_sums)

    def body(*refs):
        copies = _scatter_copies(refs[:n], refs[n:2 * n], *refs[2 * n:])
        for cp in copies:
            cp.start()
        for cp in copies:
            cp.wait()

    return pl.pallas_call(
        body, name=name,
        out_shape=_scatter_shapes(chip_sums),
        in_specs=_any_specs(n), out_specs=_any_specs(n),
        scratch_shapes=[pltpu.SemaphoreType.DMA((3 * n,)), pltpu.SemaphoreType.DMA((3 * n,))],
    )(*chip_sums)


def join_halves(reduced, small):
    n = len(reduced)

    def body(*refs):
        small_ref = refs[n]
        buf_refs, total_ref = refs[n + 1:2 * n + 1], refs[2 * n + 1]
        send_sems, recv_sems, all_ref = refs[2 * n + 2:2 * n + 5]
        x, y, c = _position()
        copies = []
        for i in range(n):
            mine, _ = _halves(buf_refs[i].shape[0], c)
            cp = pltpu.make_async_remote_copy(
                src_ref=buf_refs[i].at[mine], dst_ref=buf_refs[i].at[mine],
                send_sem=send_sems.at[i], recv_sem=recv_sems.at[i],
                device_id=(x, y, 1 - c), device_id_type=MESH)
            cp.start()
            copies.append(cp)
        _gather_small(small_ref, all_ref, *refs[2 * n + 5:])
        total = all_ref[0]
        for dev in range(1, N_DEV):
            total = total + all_ref[dev]
        total_ref[...] = total
        for cp in copies:
            cp.wait()

    outs = pl.pallas_call(
        body, name="join_halves",
        out_shape=[jax.ShapeDtypeStruct(r.shape, F32) for r in reduced]
                  + [jax.ShapeDtypeStruct(small.shape, small.dtype)],
        in_specs=_any_specs(n) + [VMEM_SPEC], out_specs=_any_specs(n) + [VMEM_SPEC],
        input_output_aliases={i: i for i in range(n)},
        scratch_shapes=[pltpu.SemaphoreType.DMA((n,)), pltpu.SemaphoreType.DMA((n,)),
                        pltpu.VMEM((N_DEV, *small.shape), small.dtype)] + SMALL_SEMS,
    )(*reduced, small)
    return outs[:n], outs[n]


ADD_ROWS = 512


def add_halves(grad, theirs, place, name):
    _, half, cols = theirs.shape
    rb = min(ADD_ROWS, half)
    steps = half // rb

    def body(place_ref, a_ref, b_ref, f_ref, h_ref):
        s = a_ref[0] + b_ref[0].astype(F32)
        h_ref[0] = _bf(s)

        @pl.when(pl.program_id(1) == place_ref[1])
        def _():
            f_ref[...] = s

    spec = pl.BlockSpec((1, rb, cols), lambda j, i, place: (i, j, 0))
    return pl.pallas_call(
        body, name=name,
        grid_spec=pltpu.PrefetchScalarGridSpec(
            num_scalar_prefetch=1, grid=(steps, N_CHIPS),
            in_specs=[pl.BlockSpec((1, rb, cols), lambda j, i, place: (i, place[0] * steps + j, 0)), spec],
            out_specs=(pl.BlockSpec((rb, cols), lambda j, i, place: (j, 0)), spec)),
        out_shape=(jax.ShapeDtypeStruct((half, cols), F32), jax.ShapeDtypeStruct(theirs.shape, BF16)),
        compiler_params=_params("parallel", "arbitrary"),
    )(place, grad, theirs)


def add_parts(own, got, place, name):
    _, half, cols = got.shape
    rb = min(ADD_ROWS, half)
    steps = half // rb

    def body(place_ref, o_ref, g_ref, out_ref):
        s = o_ref[...]
        for j in range(N_CHIPS - 1):
            s = s + g_ref[j].astype(F32)
        out_ref[...] = s

    return pl.pallas_call(
        body, name=name,
        grid_spec=pltpu.PrefetchScalarGridSpec(
            num_scalar_prefetch=1, grid=(steps,),
            in_specs=[pl.BlockSpec((rb, cols), lambda j, place: (j, 0)),
                      pl.BlockSpec((N_CHIPS - 1, rb, cols), lambda j, place: (0, j, 0))],
            out_specs=pl.BlockSpec((rb, cols), lambda j, place: (place[0] * steps + j, 0))),
        out_shape=jax.ShapeDtypeStruct((2 * half, cols), F32),
        compiler_params=_params("parallel"),
    )(place, own, got)


def _adam_math(w, g, m, v):
    m = ADAM_B1 * m + (1.0 - ADAM_B1) * g
    v = ADAM_B2 * v + (1.0 - ADAM_B2) * (g * g)
    m_hat = m / (1.0 - ADAM_B1 ** ADAM_STEP)
    v_hat = v / (1.0 - ADAM_B2 ** ADAM_STEP)
    delta = -ADAM_LR * (m_hat / (jnp.sqrt(v_hat) + ADAM_EPS) + ADAM_WD * w)
    return delta, m, v


def adamw(w, g, m, v, name):
    rows, cols = w.shape
    fits = [t for t in range(8, rows, 8) if rows % t == 0 and t * cols * 4 <= 2 ** 20]
    tile = max(fits) if fits else rows

    def body(w_ref, g_ref, m_ref, v_ref, d_ref, nm_ref, nv_ref):
        d, nm, nv = _adam_math(w_ref[...], g_ref[...], m_ref[...], v_ref[...])
        d_ref[...] = d
        nm_ref[...] = nm
        nv_ref[...] = nv

    spec = pl.BlockSpec((tile, cols), lambda i: (i, 0))
    shape = jax.ShapeDtypeStruct((rows, cols), F32)
    return pl.pallas_call(
        body, name=name, grid=(rows // tile,),
        out_shape=(shape, shape, shape),
        in_specs=[spec] * 4, out_specs=(spec, spec, spec),
        compiler_params=_params("parallel"),
    )(w, g, m, v)


def adamw_small(params):
    n = len(params)

    def body(*refs):
        ins, outs = refs[:4 * n], refs[4 * n:]
        for k in range(n):
            w_ref, g_ref, m_ref, v_ref = ins[4 * k:4 * k + 4]
            d, nm, nv = _adam_math(w_ref[...], g_ref[...], m_ref[...], v_ref[...])
            outs[3 * k][...] = d
            outs[3 * k + 1][...] = nm
            outs[3 * k + 2][...] = nv

    flat = [a for p in params for a in p]
    outs = pl.pallas_call(
        body, name="adamw_small",
        out_shape=[jax.ShapeDtypeStruct(p[0].shape, F32) for p in params for _ in range(3)],
        in_specs=[VMEM_SPEC] * (4 * n), out_specs=[VMEM_SPEC] * (3 * n),
    )(*flat)
    return [tuple(outs[3 * k:3 * k + 3]) for k in range(n)]


def adamw_rows(w, g, m, v, name):
    rows, _, cols = w.shape
    tile = rows // 4

    def body(w_ref, g_ref, m_ref, v_ref, d_ref, nm_ref, nv_ref):
        d, nm, nv = _adam_math(w_ref[...], g_ref[...], m_ref[...], v_ref[...])
        d_ref[...] = d
        nm_ref[...] = nm
        nv_ref[...] = nv

    spec = pl.BlockSpec((tile, 1, cols), lambda i: (i, 0, 0))
    shape = jax.ShapeDtypeStruct(w.shape, F32)
    return pl.pallas_call(
        body, name=name, grid=(rows // tile,),
        out_shape=(shape, shape, shape),
        in_specs=[spec] * 4, out_specs=(spec, spec, spec),
        compiler_params=_params("parallel"),
    )(w, g, m, v)


def matmul_tn(a, b, name, tile_n=512, tile_s=2048, by_column_tile=False):
    s, m = a.shape
    n = b.shape[1]
    tile_n = min(tile_n, n)
    tile_s = min(tile_s, s)
    steps = s // tile_s
    if by_column_tile:
        shape = (n // tile_n, m, tile_n)
        out_spec = pl.BlockSpec((None, m, tile_n), lambda j, k: (j, 0, 0))
    else:
        shape = (m, n)
        out_spec = pl.BlockSpec((m, tile_n), lambda j, k: (0, j))

    def body(a_ref, b_ref, out_ref, bf_ref):
        k = pl.program_id(1)

        @pl.when(k == 0)
        def _():
            out_ref[...] = jnp.zeros_like(out_ref)

        out_ref[...] += _tn(a_ref[...], b_ref[...])

        @pl.when(k == steps - 1)
        def _():
            bf_ref[...] = _bf(out_ref[...])

    return pl.pallas_call(
        body, name=name, grid=(n // tile_n, steps),
        out_shape=(jax.ShapeDtypeStruct(shape, F32), jax.ShapeDtypeStruct(shape, BF16)),
        in_specs=[pl.BlockSpec((tile_s, m), lambda j, k: (k, 0)),
                  pl.BlockSpec((tile_s, tile_n), lambda j, k: (k, j))],
        out_specs=(out_spec, out_spec),
        compiler_params=_params("parallel", "arbitrary"),
    )(a, b)


ROW_TILE = 512


def _row_index(tile, rows):
    return tile * rows + lax.broadcasted_iota(jnp.int32, (rows, 1), 0)


def _inverse_counts(t_glob):
    return [1.0 / jnp.minimum(t_glob + 1, w).astype(F32) for w in POOL_WINDOWS]


def _sigmoid(z):
    return 1.0 / (1.0 + jnp.exp(-z))


def gather_in_background(step, last, out_refs, send_sems, recv_sems, finish):
    n = len(out_refs)
    x, y, c = _position()
    q = 2 * x + y
    chips = _other_chips(x, y)

    def copy(k, i, quarter, half, to):
        return _gather_copy(out_refs[i], send_sems, recv_sems, k * n + i, quarter, half, to)

    if not finish:
        @pl.when(step == 0)
        def _():
            for i in range(n):
                mine, _ = _halves(out_refs[i].shape[1], c)
                for j, chip in enumerate(chips):
                    copy(j, i, q, mine, (*chip, c)).start()

        @pl.when(step == last)
        def _():
            for j, chip in enumerate(chips):
                qj = 2 * chip[0] + chip[1]
                for i in range(n):
                    mine, _ = _halves(out_refs[i].shape[1], c)
                    copy(j, i, qj, mine, (x, y, c)).wait_recv()
                    copy(3 + j, i, qj, mine, (x, y, 1 - c)).start()
        return

    @pl.when(step == last)
    def _():
        for j, chip in enumerate(chips):
            qj = 2 * chip[0] + chip[1]
            for i in range(n):
                mine, other = _halves(out_refs[i].shape[1], c)
                copy(3 + j, i, qj, other, (x, y, c)).wait_recv()
                copy(j, i, q, mine, (x, y, c)).wait_send()
                copy(3 + j, i, qj, mine, (x, y, c)).wait_send()


def pool_forward(x, w0, wpi, gw, gb, scale, wpo, later):
    s = x.shape[0]
    ts = ROW_TILE
    nt = s // ts
    assert nt >= 2
    n_later = len(later)

    def body(x_ref, w0_ref, wpi_ref, gw_ref, gb_ref, sc_ref, wpo_ref, *rest):
        rest = rest[n_later:]
        h1_ref, pooled_ref, gt_ref, n0_ref = rest[:4]
        later_refs = rest[4:4 + n_later]
        ubuf, send_sems, recv_sems = rest[4 + n_later:]
        i = pl.program_id(0)
        gather_in_background(i, nt - 1, later_refs, send_sems, recv_sems, finish=False)
        xv = x_ref[...]
        r = lax.rsqrt(jnp.mean(xv * xv, axis=-1, keepdims=True) + EPS)
        n0 = _bf(xv * r * w0_ref[...])
        n0_ref[...] = n0
        u = jnp.concatenate([_nn(n0, wpi_ref[0]), _nn(n0, wpi_ref[1])], axis=-1)
        gt = jnp.concatenate([_nn(n0, wpi_ref[2]), _nn(n0, wpi_ref[3])], axis=-1)
        gt_ref[...] = gt

        @pl.when(i == 0)
        def _():
            ubuf[0:HALO, :] = jnp.zeros((HALO, D), F32)

        ubuf[HALO:HALO + ts, :] = u
        inv = _inverse_counts(_row_index(i, ts))
        mixed = []
        for g, w in enumerate(POOL_WINDOWS):
            cols = slice(g * GROUP_DIM, (g + 1) * GROUP_DIM)
            ug = u[:, cols]
            acc = ug
            for j in range(1, w):
                acc = acc + ubuf[HALO - j:HALO - j + ts, cols]
            pooled = _bf(acc * inv[g] - ug)
            pooled_ref[:, cols] = pooled
            mixed.append(_nn(pooled, gw_ref[g]))
        ubuf[0:HALO, :] = ubuf[ts:ts + HALO, :]
        mixed = jnp.concatenate(mixed, axis=-1) + gb_ref[...]
        y = mixed * sc_ref[...] * (gt * _sigmoid(gt))
        h1_ref[...] = xv + _nn(_bf(y), wpo_ref[...])
        gather_in_background(i, nt - 1, later_refs, send_sems, recv_sems, finish=True)

    row = lambda cols: pl.BlockSpec((ts, cols), lambda i: (i, 0))
    outs = pl.pallas_call(
        body, name="pool_forward", grid=(nt,),
        out_shape=[jax.ShapeDtypeStruct((s, D), F32), jax.ShapeDtypeStruct((s, D), BF16),
                   jax.ShapeDtypeStruct((s, D), F32), jax.ShapeDtypeStruct((s, D), BF16)]
                  + [jax.ShapeDtypeStruct(a.shape, a.dtype) for a in later],
        in_specs=[row(D), _full((1, D)), _full((N_CHIPS, D, D // 2)), _full((GROUPS, GROUP_DIM, GROUP_DIM)),
                  _full((1, D)), _full((1, D)), _full((D, D))] + _any_specs(n_later),
        out_specs=[row(D), row(D), row(D), row(D)] + _any_specs(n_later),
        input_output_aliases={7 + k: 4 + k for k in range(n_later)},
        scratch_shapes=[pltpu.VMEM((HALO + ts, D), F32),
                        pltpu.SemaphoreType.DMA((6 * n_later,)), pltpu.SemaphoreType.DMA((6 * n_later,))],
        compiler_params=_params("arbitrary"),
    )(x, w0, wpi, gw, gb, scale, wpo, *later)
    return outs[:4], outs[4:]


def pool_backward(x, dh1, pooled, gt, w0, wpi, gw, gb, scale, wpo, chip_sums):
    s = x.shape[0]
    ts = ROW_TILE
    nt = s // ts
    n_sums = len(chip_sums)

    def body(x_ref, dh1_ref, pooled_ref, gt_ref, w0_ref, wpi_ref, gw_ref, gb_ref, sc_ref, wpo_ref, *rest):
        sum_refs, rest = rest[:n_sums], rest[n_sums:]
        dx_ref, dproj_ref, gpo_ref, gpo_bf_ref, ggw_ref, small_ref = rest[:6]
        got_refs = rest[6:6 + n_sums]
        ebuf, send_sems, recv_sems = rest[6 + n_sums:]
        i = pl.program_id(0)
        copies = _scatter_copies(sum_refs, got_refs, send_sems, recv_sems)

        @pl.when(i == 0)
        def _():
            for cp in copies:
                cp.start()

        @pl.when(i == 0)
        def _():
            gpo_ref[...] = jnp.zeros_like(gpo_ref)
            ggw_ref[...] = jnp.zeros_like(ggw_ref)
            small_ref[...] = jnp.zeros_like(small_ref)
            ebuf[ts:ts + HALO, :] = jnp.zeros((HALO, D), F32)

        dh1 = dh1_ref[...]
        dh1_bf = _bf(dh1)
        gt = gt_ref[...]
        sc = sc_ref[...]
        dy = _nt(dh1_bf, wpo_ref[...])
        pooled_bf = []
        mixed = []
        for g in range(GROUPS):
            cols = slice(g * GROUP_DIM, (g + 1) * GROUP_DIM)
            pb = pooled_ref[:, cols]
            pooled_bf.append(pb)
            mixed.append(_nn(pb, gw_ref[g]))
        mixed = jnp.concatenate(mixed, axis=-1) + gb_ref[...]
        sg = _sigmoid(gt)
        silu = gt * sg
        gpo_ref[...] += _tn(_bf(mixed * sc * silu), dh1_bf)
        dmixed = dy * sc * silu
        dgt = dy * mixed * sc * (sg * (1.0 + gt * (1.0 - sg)))
        dproj_ref[:, D:] = _bf(dgt)
        small_ref[1:2, :] += jnp.sum(dy * mixed * silu, axis=0, keepdims=True)
        small_ref[2:3, :] += jnp.sum(dmixed, axis=0, keepdims=True)

        inv = _inverse_counts(_row_index(nt - 1 - i, ts))
        dpooled = []
        for g in range(GROUPS):
            cols = slice(g * GROUP_DIM, (g + 1) * GROUP_DIM)
            dm = _bf(dmixed[:, cols])
            ggw_ref[g] += _tn(pooled_bf[g], dm)
            dp = _nt(dm, gw_ref[g])
            dpooled.append(dp)
            ebuf[0:ts, cols] = dp * inv[g]
        du = []
        for g, w in enumerate(POOL_WINDOWS):
            cols = slice(g * GROUP_DIM, (g + 1) * GROUP_DIM)
            acc = -dpooled[g]
            for j in range(w):
                acc = acc + ebuf[j:j + ts, cols]
            du.append(acc)
        ebuf[ts:ts + HALO, :] = ebuf[0:HALO, :]
        du = _bf(jnp.concatenate(du, axis=-1))
        dproj_ref[:, :D] = du
        dgt_bf = _bf(dgt)
        half = D // 2
        dn0 = (_nt(du[:, :half], wpi_ref[0]) + _nt(du[:, half:], wpi_ref[1])
               + _nt(dgt_bf[:, :half], wpi_ref[2]) + _nt(dgt_bf[:, half:], wpi_ref[3]))

        xv = x_ref[...]
        r = lax.rsqrt(jnp.mean(xv * xv, axis=-1, keepdims=True) + EPS)
        xhat = xv * r
        small_ref[0:1, :] += jnp.sum(dn0 * xhat, axis=0, keepdims=True)
        dxh = dn0 * w0_ref[...]
        dx_ref[...] = dh1 + r * (dxh - xhat * jnp.mean(dxh * xhat, axis=-1, keepdims=True))

        @pl.when(i == nt - 1)
        def _():
            gpo_bf_ref[...] = _bf(gpo_ref[...])
            for cp in copies:
                cp.wait()

    row = lambda cols: pl.BlockSpec((ts, cols), lambda i: (nt - 1 - i, 0))
    outs = pl.pallas_call(
        body, name="pool_backward", grid=(nt,),
        out_shape=[jax.ShapeDtypeStruct((s, D), F32), jax.ShapeDtypeStruct((s, 2 * D), BF16),
                   jax.ShapeDtypeStruct((D, D), F32), jax.ShapeDtypeStruct((D, D), BF16),
                   jax.ShapeDtypeStruct((GROUPS, GROUP_DIM, GROUP_DIM), F32),
                   jax.ShapeDtypeStruct((8, D), F32)] + _scatter_shapes(chip_sums),
        in_specs=[row(D), row(D), row(D), row(D), _full((1, D)), _full((N_CHIPS, D, D // 2)),
                  _full((GROUPS, GROUP_DIM, GROUP_DIM)), _full((1, D)), _full((1, D)), _full((D, D))]
                 + _any_specs(n_sums),
        out_specs=[row(D), row(2 * D), _full((D, D)), _full((D, D)), _full((GROUPS, GROUP_DIM, GROUP_DIM)),
                   _full((8, D))] + _any_specs(n_sums),
        scratch_shapes=[pltpu.VMEM((ts + HALO, D), F32),
                        pltpu.SemaphoreType.DMA((3 * n_sums,)), pltpu.SemaphoreType.DMA((3 * n_sums,))],
        compiler_params=_params("arbitrary"),
    )(x, dh1, pooled, gt, w0, wpi, gw, gb, scale, wpo, *chip_sums)
    return outs[:6], outs[6:]


def gla_project(h1, w1, wgi, wlow, wgk, bgk, later):
    s = h1.shape[0]
    ts = ROW_TILE
    nt = s // ts
    assert nt >= 2
    n_later = len(later)

    def body(h_ref, w1_ref, wgi_ref, wlow_ref, wgk_ref, bgk_ref, *rest):
        rest = rest[n_later:]
        qk_ref, v_ref, gate_ref, low_ref, cum_ref, n1_ref = rest[:6]
        later_refs = rest[6:6 + n_later]
        send_sems, recv_sems = rest[6 + n_later:]
        gather_in_background(pl.program_id(0), nt - 1, later_refs, send_sems, recv_sems, finish=False)
        hv = h_ref[...]
        r = lax.rsqrt(jnp.mean(hv * hv, axis=-1, keepdims=True) + EPS)
        n1 = _bf(hv * r * w1_ref[...])
        n1_ref[...] = n1
        qk_ref[...] = _nn(n1, wgi_ref[:, 0:2 * KEY_W])
        v_ref[...] = _bf(_nn(n1, wgi_ref[:, 2 * KEY_W:2 * KEY_W + D]))
        gate_ref[...] = _nn(n1, wgi_ref[:, 2 * KEY_W + D:GLA_MAIN])
        low = _bf(_nn(n1, wlow_ref[...]))
        low_ref[...] = low
        z = _nn(low, wgk_ref[...]) + bgk_ref[...]
        lg = (jnp.minimum(z, 0.0) - jnp.log(1.0 + jnp.exp(-jnp.abs(z)))) / GATE_NORM
        lower_f = _chunk_masks()[0].astype(F32)
        for r0 in range(0, ts, CHUNK):
            cum_ref[r0:r0 + CHUNK, :] = _nn_exact(lower_f, lg[r0:r0 + CHUNK, :])
        gather_in_background(pl.program_id(0), nt - 1, later_refs, send_sems, recv_sems, finish=True)

    row = lambda cols: pl.BlockSpec((ts, cols), lambda i: (i, 0))
    outs = pl.pallas_call(
        body, name="gla_project", grid=(nt,),
        out_shape=[jax.ShapeDtypeStruct((s, D), F32), jax.ShapeDtypeStruct((s, D), BF16),
                   jax.ShapeDtypeStruct((s, D), F32), jax.ShapeDtypeStruct((s, RANK_PAD), BF16),
                   jax.ShapeDtypeStruct((s, KEY_W), F32), jax.ShapeDtypeStruct((s, D), BF16)]
                  + [jax.ShapeDtypeStruct(a.shape, a.dtype) for a in later],
        in_specs=[row(D), _full((1, D)), _full((D, GLA_MAIN)), _full((D, RANK_PAD)),
                  _full((RANK_PAD, KEY_W)), _full((1, KEY_W))] + _any_specs(n_later),
        out_specs=[row(D), row(D), row(D), row(RANK_PAD), row(KEY_W), row(D)] + _any_specs(n_later),
        input_output_aliases={6 + k: 6 + k for k in range(n_later)},
        scratch_shapes=[pltpu.SemaphoreType.DMA((6 * n_later,)), pltpu.SemaphoreType.DMA((6 * n_later,))],
        compiler_params=_params("arbitrary"),
    )(h1, w1, wgi, wlow, wgk, bgk, *later)
    return outs[:6], outs[6:]


GLA_BLOCK = 512
CHUNKS_PER_BLOCK = GLA_BLOCK // CHUNK


def _chunk_masks():
    t = lax.broadcasted_iota(jnp.int32, (CHUNK, CHUNK), 0)
    u = lax.broadcasted_iota(jnp.int32, (CHUNK, CHUNK), 1)
    return t >= u, t <= u


def _gla_chunk_terms(q, cum):
    ep = jnp.exp(cum)
    en = jnp.exp(-cum)
    qs = q * (HEAD_K ** -0.5)
    last = cum[CHUNK - 1:CHUNK, :]
    ed = jnp.exp(last - cum)
    dec = jnp.exp(last)
    return ep, en, qs, ed, dec


def gla_forward(qk, v, cum):
    s = qk.shape[0]
    nb = s // GLA_BLOCK
    nc = s // CHUNK

    def body(q_ref, k_ref, v_ref, cum_ref, o_ref, st_ref, sc_ref, state):
        @pl.when(pl.program_id(0) == 0)
        def _():
            state[...] = jnp.zeros_like(state)

        lower, _ = _chunk_masks()

        def chunk(cc, carry):
            rows = pl.ds(pl.multiple_of(cc * CHUNK, CHUNK), CHUNK)
            for h in range(HEADS):
                kc = slice(h * HEAD_K, (h + 1) * HEAD_K)
                vc = slice(h * HEAD_V, (h + 1) * HEAD_V)
                q = q_ref[rows, kc]
                k = k_ref[rows, kc]
                v = v_ref[rows, vc]
                ep, en, qs, ed, dec = _gla_chunk_terms(q, cum_ref[rows, kc])
                a = _bf(qs * ep)
                fwd = _nt(a, _bf(k * en))
                bwd = _nt(_bf(qs * en), _bf(k * ep))
                scores = _bf(jnp.where(lower, fwd, bwd))
                sc_ref[rows, h * CHUNK:(h + 1) * CHUNK] = scores
                st = state[h]
                st_ref[cc, h] = st
                o_ref[rows, vc] = _nn(scores, v) + _nt(a, _bf(st))
                state[h] = st * dec + _tn(v, _bf(k * ed))
            return carry

        lax.fori_loop(0, CHUNKS_PER_BLOCK, chunk, 0, unroll=4)

    return pl.pallas_call(
        body, name="gla_forward", grid=(nb,),
        out_shape=(jax.ShapeDtypeStruct((s, D), F32),
                   jax.ShapeDtypeStruct((nc, HEADS, HEAD_V, HEAD_K), F32),
                   jax.ShapeDtypeStruct((s, HEADS * CHUNK), BF16)),
        in_specs=[pl.BlockSpec((GLA_BLOCK, KEY_W), lambda i: (i, 0)),
                  pl.BlockSpec((GLA_BLOCK, KEY_W), lambda i: (i, 1)),
                  pl.BlockSpec((GLA_BLOCK, D), lambda i: (i, 0)),
                  pl.BlockSpec((GLA_BLOCK, KEY_W), lambda i: (i, 0))],
        out_specs=(pl.BlockSpec((GLA_BLOCK, D), lambda i: (i, 0)),
                   pl.BlockSpec((CHUNKS_PER_BLOCK, HEADS, HEAD_V, HEAD_K), lambda i: (i, 0, 0, 0)),
                   pl.BlockSpec((GLA_BLOCK, HEADS * CHUNK), lambda i: (i, 0))),
        scratch_shapes=[pltpu.VMEM((HEADS, HEAD_V, HEAD_K), F32)],
        compiler_params=_params("arbitrary"),
    )(qk, qk, v, cum)


def gla_backward(qk, v, cum, do, states, scores):
    s = qk.shape[0]
    nb = s // GLA_BLOCK

    def body(q_ref, k_ref, v_ref, cum_ref, do_ref, st_ref, sc_ref, dq_ref, dk_ref, dv_ref, dcum_ref, dstate):
        @pl.when(pl.program_id(0) == 0)
        def _():
            dstate[...] = jnp.zeros_like(dstate)

        lower, _ = _chunk_masks()
        is_last = lax.broadcasted_iota(jnp.int32, (CHUNK, HEAD_K), 0) == CHUNK - 1

        def chunk(step, carry):
            cc = CHUNKS_PER_BLOCK - 1 - step
            rows = pl.ds(pl.multiple_of(cc * CHUNK, CHUNK), CHUNK)
            for h in range(HEADS):
                kc = slice(h * HEAD_K, (h + 1) * HEAD_K)
                vc = slice(h * HEAD_V, (h + 1) * HEAD_V)
                q = q_ref[rows, kc]
                k = k_ref[rows, kc]
                v = v_ref[rows, vc]
                do_c = do_ref[rows, vc]
                ep, en, qs, ed, dec = _gla_chunk_terms(q, cum_ref[rows, kc])
                a = _bf(qs * ep)
                b = _bf(k * en)
                c = _bf(qs * en)
                dk_dec = _bf(k * ep)
                kd = _bf(k * ed)
                scores = sc_ref[rows, h * CHUNK:(h + 1) * CHUNK]
                st = st_ref[cc, h]
                dst = dstate[h]
                dst_bf = _bf(dst)

                dscores = _nt(do_c, v)
                dfwd = _bf(jnp.where(lower, dscores, 0.0))
                dbwd = _bf(jnp.where(lower, 0.0, dscores))
                dv_ref[rows, vc] = _bf(_tn(scores, do_c) + _nt(kd, dst_bf))
                da = _nn(dfwd, b) + _nn(do_c, _bf(st))
                db = _tn(dfwd, a)
                dc = _nn(dbwd, dk_dec)
                ddk = _tn(dbwd, c)
                dkd = _nn(v, dst_bf)
                ddec = jnp.sum(dst * st, axis=0, keepdims=True)
                dstate[h] = dst * dec + _tn(do_c, a)

                m = dkd * k * ed
                dq_ref[rows, kc] = _bf((da * ep + dc * en) * (HEAD_K ** -0.5))
                dk_ref[rows, kc] = _bf(db * en + ddk * ep + dkd * ed)
                dcum = (da * qs + ddk * k) * ep - (db * k + dc * qs) * en - m
                dlast = jnp.sum(m, axis=0, keepdims=True) + ddec * dec
                dcum_ref[rows, kc] = dcum + jnp.where(is_last, dlast, 0.0)
            return carry

        lax.fori_loop(0, CHUNKS_PER_BLOCK, chunk, 0, unroll=4)

    rev = lambda cols, col_block: pl.BlockSpec((GLA_BLOCK, cols), lambda i: (nb - 1 - i, col_block))
    return pl.pallas_call(
        body, name="gla_backward", grid=(nb,),
        out_shape=(jax.ShapeDtypeStruct((s, KEY_W), BF16), jax.ShapeDtypeStruct((s, KEY_W), BF16),
                   jax.ShapeDtypeStruct((s, D), BF16), jax.ShapeDtypeStruct((s, KEY_W), F32)),
        in_specs=[rev(KEY_W, 0), rev(KEY_W, 1), rev(D, 0), rev(KEY_W, 0), rev(D, 0),
                  pl.BlockSpec((CHUNKS_PER_BLOCK, HEADS, HEAD_V, HEAD_K), lambda i: (nb - 1 - i, 0, 0, 0)),
                  rev(HEADS * CHUNK, 0)],
        out_specs=(rev(KEY_W, 0), rev(KEY_W, 0), rev(D, 0), rev(KEY_W, 0)),
        scratch_shapes=[pltpu.VMEM((HEADS, HEAD_V, HEAD_K), F32)],
        compiler_params=_params("arbitrary"),
    )(qk, qk, v, cum, do, states, scores)


def head_and_loss(o, gate, h1, target, hw, wgo, wf):
    s = o.shape[0]
    ts = ROW_TILE
    nt = s // ts

    def body(o_ref, gate_ref, h1_ref, tgt_ref, hw_ref, wgo_ref, wf_ref,
             dh2_ref, do_ref, dgate_ref, ggo_ref, ggo_bf_ref, small_ref):
        @pl.when(pl.program_id(0) == 0)
        def _():
            ggo_ref[...] = jnp.zeros_like(ggo_ref)
            small_ref[...] = jnp.zeros_like(small_ref)

        gate = gate_ref[...]
        hw = hw_ref[...]
        sg = _sigmoid(gate)
        silu = gate * sg
        ohat, ro = [], []
        for h in range(HEADS):
            oh = o_ref[:, h * HEAD_V:(h + 1) * HEAD_V]
            rh = lax.rsqrt(jnp.mean(oh * oh, axis=-1, keepdims=True) + EPS)
            ro.append(rh)
            ohat.append(oh * rh)
        ohat = jnp.concatenate(ohat, axis=-1)
        on = ohat * hw
        y2 = _bf(on * silu)
        h2 = h1_ref[...] + _nn(y2, wgo_ref[...])
        rf = lax.rsqrt(jnp.mean(h2 * h2, axis=-1, keepdims=True) + EPS)
        h2hat = h2 * rf
        wf = wf_ref[...]
        diff = h2hat * wf - tgt_ref[...]
        small_ref[2:3, :] += jnp.zeros((1, D), F32) + 0.5 * jnp.sum(diff * diff) / D
        dout = diff / D
        small_ref[0:1, :] += jnp.sum(dout * h2hat, axis=0, keepdims=True)
        dxh = dout * wf
        dh2 = rf * (dxh - h2hat * jnp.mean(dxh * h2hat, axis=-1, keepdims=True))
        dh2_ref[...] = dh2
        dh2_bf = _bf(dh2)
        ggo_ref[...] += _tn(y2, dh2_bf)
        dy2 = _nt(dh2_bf, wgo_ref[...])
        don = dy2 * silu
        dgate_ref[...] = _bf(dy2 * on * (sg * (1.0 + gate * (1.0 - sg))))
        ghw = jnp.sum(don * ohat, axis=0, keepdims=True)
        small_ref[1:2, 0:HEAD_V] += sum(ghw[:, h * HEAD_V:(h + 1) * HEAD_V] for h in range(HEADS))
        dohat = don * hw
        for h in range(HEADS):
            cols = slice(h * HEAD_V, (h + 1) * HEAD_V)
            oh, dh = ohat[:, cols], dohat[:, cols]
            do_ref[:, cols] = _bf(ro[h] * (dh - oh * jnp.mean(dh * oh, axis=-1, keepdims=True)))

        @pl.when(pl.program_id(0) == nt - 1)
        def _():
            ggo_bf_ref[...] = _bf(ggo_ref[...])

    row = lambda cols: pl.BlockSpec((ts, cols), lambda i: (i, 0))
    act = jax.ShapeDtypeStruct((s, D), F32)
    act_bf = jax.ShapeDtypeStruct((s, D), BF16)
    return pl.pallas_call(
        body, name="head_and_loss", grid=(nt,),
        out_shape=(act, act_bf, act_bf, jax.ShapeDtypeStruct((D, D), F32), jax.ShapeDtypeStruct((D, D), BF16),
                   jax.ShapeDtypeStruct((8, D), F32)),
        in_specs=[row(D), row(D), row(D), row(D),
                  _full((1, D)), _full((D, D)), _full((1, D))],
        out_specs=(row(D), row(D), row(D), _full((D, D)), _full((D, D)), _full((8, D))),
        compiler_params=_params("arbitrary"),
    )(o, gate, h1, target, hw, wgo, wf)


def gla_project_backward(dq, dk, dv, dgate, dcum, low, h1, dh2, w1, wgi, wlow, wgk, bgk):
    s = h1.shape[0]
    ts = ROW_TILE

    def body(dq_ref, dk_ref, dv_ref, dgate_ref, dcum_ref, low_ref, h1_ref, dh2_ref, w1_ref,
             wgi_ref, wlow_ref, wgk_ref, bgk_ref, dh1_ref, dproj_ref, ggk_ref, small_ref):
        @pl.when(pl.program_id(0) == 0)
        def _():
            ggk_ref[...] = jnp.zeros_like(ggk_ref)
            small_ref[...] = jnp.zeros_like(small_ref)

        low = low_ref[...]
        z = _nn(low, wgk_ref[...]) + bgk_ref[...]
        upper_f = _chunk_masks()[1].astype(F32)
        dlg = jnp.concatenate([_nn_exact(upper_f, dcum_ref[r0:r0 + CHUNK, :]) for r0 in range(0, ts, CHUNK)],
                              axis=0)
        dz = dlg * (1.0 / GATE_NORM) * _sigmoid(-z)
        dz_bf = _bf(dz)
        ggk_ref[...] += _tn(low, dz_bf)
        small_ref[1:2, 0:KEY_W] += jnp.sum(dz, axis=0, keepdims=True)
        dlow = _bf(_nt(dz_bf, wgk_ref[...]))
        dproj_ref[:, GLA_MAIN:] = dlow
        dn1 = _nt(dlow, wlow_ref[...])
        for ref, lo, hi in ((dq_ref, 0, KEY_W), (dk_ref, KEY_W, 2 * KEY_W),
                            (dv_ref, 2 * KEY_W, 2 * KEY_W + D), (dgate_ref, 2 * KEY_W + D, GLA_MAIN)):
            piece = ref[...]
            dproj_ref[:, lo:hi] = piece
            dn1 = dn1 + _nt(piece, wgi_ref[:, lo:hi])
        hv = h1_ref[...]
        r = lax.rsqrt(jnp.mean(hv * hv, axis=-1, keepdims=True) + EPS)
        hhat = hv * r
        small_ref[0:1, :] += jnp.sum(dn1 * hhat, axis=0, keepdims=True)
        dxh = dn1 * w1_ref[...]
        dh1_ref[...] = dh2_ref[...] + r * (dxh - hhat * jnp.mean(dxh * hhat, axis=-1, keepdims=True))

    row = lambda cols: pl.BlockSpec((ts, cols), lambda i: (i, 0))
    return pl.pallas_call(
        body, name="gla_project_backward", grid=(s // ts,),
        out_shape=(jax.ShapeDtypeStruct((s, D), F32), jax.ShapeDtypeStruct((s, GLA_MAIN + RANK_PAD), BF16),
                   jax.ShapeDtypeStruct((RANK_PAD, KEY_W), F32),
                   jax.ShapeDtypeStruct((8, D), F32)),
        in_specs=[row(KEY_W), row(KEY_W), row(D), row(D), row(KEY_W), row(RANK_PAD), row(D), row(D),
                  _full((1, D)), _full((D, GLA_MAIN)), _full((D, RANK_PAD)), _full((RANK_PAD, KEY_W)),
                  _full((1, KEY_W))],
        out_specs=(row(D), row(GLA_MAIN + RANK_PAD), _full((RANK_PAD, KEY_W)), _full((8, D))),
        compiler_params=_params("arbitrary"),
    )(dq, dk, dv, dgate, dcum, low, h1, dh2, w1, wgi, wlow, wgk, bgk)


def _groups_from_quarters(a):
    return a.reshape(N_CHIPS, GROUPS, 64, GROUP_DIM).transpose(1, 0, 2, 3).reshape(GROUPS, GROUP_DIM, GROUP_DIM)


def _quarters_from_groups(a):
    return a.reshape(GROUPS, N_CHIPS, 64, GROUP_DIM).transpose(1, 0, 2, 3).reshape(N_CHIPS, GROUP_DIM, GROUP_DIM)


def _pad_row(*pieces):
    flat = jnp.concatenate([p.reshape(-1).astype(F32) for p in pieces])
    return jnp.pad(flat, (0, D - flat.shape[0])).reshape(1, D)


def _gla_in_weights(wgi_q):
    wgi_all = jnp.concatenate([wgi_q[q] for q in range(N_CHIPS)], axis=1)
    wlow = jnp.pad(wgi_all[:, GLA_MAIN:], ((0, 0), (0, RANK_PAD - GATE_RANK)))
    return wgi_all, wlow


def _small_sums(small_top, small_gla, small_pool, g_gk_pad):
    return jnp.concatenate([
        small_pool[0:1], small_gla[0:1],
        small_pool[1:2],
        small_top[0:1],
        small_top[2:3],
        _pad_row(small_gla[1, 0:KEY_W], small_top[1, 0:HEAD_V]),
        small_pool[2:3],
        g_gk_pad[:GATE_RANK].reshape(8, D),
        jnp.zeros((1, D), F32)], axis=0)


def local_gradients(xs, target, w0, w1, wf, wpi, gw, gb, scale, wpo, gla_quarters, wgk, bgk, hw_tiled, place):
    wgi_q, wgo_q = gla_quarters
    (h1, pooled, gt, n0), (wgi_q,) = pool_forward(xs, w0, wpi, gw, gb, scale, wpo, [wgi_q])
    wgi, wlow = _gla_in_weights(wgi_q)
    (qk, v, gate, low, cum, n1), (wgo_q,) = gla_project(h1, w1, wgi, wlow, wgk, bgk, [wgo_q])
    wgo = wgo_q.reshape(D, D)
    o, states, scores = gla_forward(qk, v, cum)

    dh2, do, dgate, g_gla_out, g_gla_out_bf, small_top = head_and_loss(o, gate, h1, target, hw_tiled, wgo, wf)
    dq, dk, dv, dcum = gla_backward(qk, v, cum, do, states, scores)
    dh1, dproj, g_gk_pad, small_gla = gla_project_backward(
        dq, dk, dv, dgate, dcum, low, h1, dh2, w1, wgi, wlow, wgk, bgk)
    g_gla_in, g_gla_in_bf = matmul_tn(n1, dproj, "grad_gla_in", tile_n=640)

    def chip_sums(grads, grads_bf, names, tag):
        theirs = exchange_with_sibling(grads_bf, "exchange_with_sibling_" + tag)
        return [add_halves(g, t, place, "add_halves_" + n) for g, t, n in zip(grads, theirs, names)]

    by_quarter = lambda g: jnp.stack([g[:, GLA_IN_QUARTER * q:GLA_IN_QUARTER * (q + 1)] for q in range(N_CHIPS)])
    rows_by_quarter = lambda g: g.reshape(N_CHIPS, D // N_CHIPS, D)
    gla_names = ("gla_in", "gla_out")
    gla_sums = chip_sums([by_quarter(g_gla_in), rows_by_quarter(g_gla_out)],
                         [by_quarter(g_gla_in_bf), rows_by_quarter(g_gla_out_bf)], gla_names, "gla")
    (dx, dpool, g_pool_out, g_pool_out_bf, g_group_w, small_pool), gla_got = pool_backward(
        xs, dh1, pooled, gt, w0, wpi, gw, gb, scale, wpo, [b for _, b in gla_sums])
    g_pool_in, g_pool_in_bf = matmul_tn(n0, dpool, "grad_pool_in", by_column_tile=True)

    pool_names = ("pool_in", "group", "pool_out")
    g_group_q = _quarters_from_groups(g_group_w)
    pool_sums = chip_sums([g_pool_in, g_group_q, rows_by_quarter(g_pool_out)],
                          [g_pool_in_bf, _bf(g_group_q), rows_by_quarter(g_pool_out_bf)], pool_names, "pool")
    pool_got = scatter_to_owners([b for _, b in pool_sums], "scatter_to_owners_pool")
    reduced, total = join_halves(
        [add_parts(f, g, place, "add_parts_" + n) for (f, _), g, n in
         zip(pool_sums + gla_sums, list(pool_got) + list(gla_got), pool_names + gla_names)],
        _small_sums(small_top, small_gla, small_pool, g_gk_pad))
    return dx, reduced, total


def kernel(x, norm_w, pool_in_w, pool_group_w, pool_group_b, pool_scale, pool_out_w, gla_in_w, gla_gk_w, gla_gk_b, gla_head_norm_w, gla_out_w, final_norm_w, loss_target, m_norm_w, m_pool_in_w, m_pool_group_w, m_pool_group_b, m_pool_scale, m_pool_out_w, m_gla_in_w, m_gla_gk_w, m_gla_gk_b, m_gla_head_norm_w, m_gla_out_w, m_final_norm_w, v_norm_w, v_pool_in_w, v_pool_group_w, v_pool_group_b, v_pool_scale, v_pool_out_w, v_gla_in_w, v_gla_gk_w, v_gla_gk_b, v_gla_head_norm_w, v_gla_out_w, v_final_norm_w):
    s = x.shape[1]
    xs = x[0]
    target = loss_target[0]
    q_chip = 2 * lax.axis_index("x") + lax.axis_index("y")
    place = jnp.stack([lax.axis_index("c"), q_chip]).astype(jnp.int32)

    small_in = jnp.concatenate([
        _pad_row(gla_gk_b[0], gla_head_norm_w[0], pool_group_b[0]),
        gla_gk_w[0].reshape(2, D),
        jnp.zeros((5, D), F32)], axis=0)
    (wpi, gw_q, wpo_q, wgi_q, wgo_q), small_all = allgather_weights(
        [pool_in_w[0], pool_group_w[0].reshape(GROUP_DIM, GROUP_DIM), pool_out_w[0], gla_in_w[0], gla_out_w[0]],
        exchange=(True, True, True, False, False), small=small_in)
    gw = _groups_from_quarters(gw_q)
    wpo = wpo_q.reshape(D, D)
    small_all = small_all[0::2]
    bgk = small_all[:, 0, 0:128].reshape(1, KEY_W)
    hw = small_all[:, 0, 128:192].reshape(1, HEAD_V)
    gb = jnp.concatenate([small_all[q, 0, 192:448].reshape(GROUPS, 64) for q in range(N_CHIPS)],
                         axis=1).reshape(1, D)
    wgk16 = jnp.concatenate([small_all[q, 1:3].reshape(GATE_RANK, 128) for q in range(N_CHIPS)], axis=1)
    wgk = _bf(jnp.pad(wgk16, ((0, RANK_PAD - GATE_RANK), (0, 0))))
    hw_tiled = jnp.tile(hw, (1, HEADS))

    w0 = norm_w[0:1]
    w1 = norm_w[1:2]
    wf = final_norm_w.reshape(1, D)

    dx, reduced, total = local_gradients(
        xs, target, w0, w1, wf, wpi, gw, gb, pool_scale, wpo, [wgi_q, wgo_q], wgk, bgk, hw_tiled, place)
    r_pool_in, r_group_w, r_pool_out, r_gla_in, r_gla_out = reduced
    r_group_w = r_group_w.reshape(GROUPS, 64, GROUP_DIM)

    loss = total[4, 0]
    g_norm = total[0:2]
    g_scale = total[2:3]
    g_final = total[3]
    pick = lambda full, width: lax.dynamic_slice_in_dim(full, q_chip * width, width, axis=-1)
    g_gk_b = pick(total[5:6, 0:KEY_W], 128)
    g_hnw = pick(total[5:6, KEY_W:KEY_W + HEAD_V], 64)
    g_group_b = pick(total[6].reshape(GROUPS, GROUP_DIM), 64)[None]
    g_gk_w = pick(total[7:15].reshape(GATE_RANK, KEY_W), 128)[None]

    def step_lane_rows(name, w, g, m, v):
        turn = lambda a: jnp.transpose(a, (2, 0, 1))
        back = lambda a: jnp.transpose(a, (1, 2, 0))
        g_t = turn(g)
        d, nm, nv = adamw_rows(turn(w), g_t, turn(m), turn(v), "adamw_" + name)
        return back(g_t), back(d), back(nm), back(nv)

    def step(name, w, g, m, v):
        shape = w.shape
        as2d = lambda a: a.reshape(-1, shape[-1])
        d, nm, nv = adamw(as2d(w), as2d(g), as2d(m), as2d(v), "adamw_" + name)
        return g.reshape(shape), d.reshape(shape), nm.reshape(shape), nv.reshape(shape)

    small_names = ("norm_w", "pool_group_b", "pool_scale", "gla_gk_w", "gla_gk_b", "gla_head_norm_w",
                   "final_norm_w")
    small_args = [(norm_w, g_norm, m_norm_w, v_norm_w),
                  (pool_group_b, g_group_b, m_pool_group_b, v_pool_group_b),
                  (pool_scale, g_scale, m_pool_scale, v_pool_scale),
                  (gla_gk_w, g_gk_w, m_gla_gk_w, v_gla_gk_w),
                  (gla_gk_b, g_gk_b, m_gla_gk_b, v_gla_gk_b),
                  (gla_head_norm_w, g_hnw, m_gla_head_norm_w, v_gla_head_norm_w),
                  (final_norm_w, g_final, m_final_norm_w, v_final_norm_w)]
    as2d = lambda a, w: a.reshape(-1, w.shape[-1])
    small_out = adamw_small([tuple(as2d(a, p[0]) for a in p) for p in small_args])
    small = {n: (p[1].reshape(p[0].shape),) + tuple(o.reshape(p[0].shape) for o in out)
             for n, p, out in zip(small_names, small_args, small_out)}
    results = [
        small["norm_w"],
        step("pool_in_w", pool_in_w, r_pool_in[None], m_pool_in_w, v_pool_in_w),
        step("pool_group_w", pool_group_w, r_group_w[None], m_pool_group_w, v_pool_group_w),
        small["pool_group_b"],
        small["pool_scale"],
        step("pool_out_w", pool_out_w, r_pool_out[None], m_pool_out_w, v_pool_out_w),
        step_lane_rows("gla_in_w", gla_in_w, r_gla_in[None], m_gla_in_w, v_gla_in_w),
        small["gla_gk_w"],
        small["gla_gk_b"],
        small["gla_head_norm_w"],
        step("gla_out_w", gla_out_w, r_gla_out[None], m_gla_out_w, v_gla_out_w),
        small["final_norm_w"],
    ]
    grads, deltas, new_m, new_v = zip(*results)
    return (loss, dx[None], *grads, *deltas, *new_m, *new_v)
```

```python
import functools

import jax
import jax.numpy as jnp
from jax import lax
from jax.experimental import pallas as pl
from jax.experimental.pallas import tpu as pltpu

F32 = jnp.float32
BF16 = jnp.bfloat16
MESH = pl.DeviceIdType.MESH

D = 1024
POOL_WINDOWS = (2, 4, 8, 16)
GROUPS = 4
GROUP_DIM = 256
HEADS = 4
HEAD_K = 128
HEAD_V = 256
KEY_W = 512
CHUNK = 64
GATE_RANK = 16
GATE_NORM = 16.0
GLA_IN = 3088
GLA_MAIN = 3072
RANK_PAD = 128
EPS = 1e-6
HALO = 32

ADAM_LR = 0.001
ADAM_B1 = 0.9
ADAM_B2 = 0.999
ADAM_EPS = 1e-08
ADAM_WD = 0.01
ADAM_STEP = 10

N_CHIPS = 4
N_DEV = 8
GLA_IN_QUARTER = GLA_IN // N_CHIPS

VMEM_LIMIT = 56 * 1024 * 1024


def _nn(a, b):
    return lax.dot_general(a, b, (((1,), (0,)), ((), ())), preferred_element_type=F32)


def _nt(a, b):
    return lax.dot_general(a, b, (((1,), (1,)), ((), ())), preferred_element_type=F32)


def _tn(a, b):
    return lax.dot_general(a, b, (((0,), (0,)), ((), ())), preferred_element_type=F32)


def _nn_exact(a, b):
    return lax.dot_general(a, b, (((1,), (0,)), ((), ())), preferred_element_type=F32,
                           precision=lax.Precision.HIGHEST)


def _bf(a):
    return a.astype(BF16)


def _params(*sem):
    return pltpu.CompilerParams(dimension_semantics=sem, vmem_limit_bytes=VMEM_LIMIT)


def _full(shape):
    return pl.BlockSpec(shape, lambda i: (0,) * len(shape))


def _position():
    return lax.axis_index("x"), lax.axis_index("y"), lax.axis_index("c")


def _gather_small(in_ref, all_ref, send_sems, recv_sems, local_sem):
    x, y, c = _position()
    me = 4 * x + 2 * y + c
    mine = pltpu.make_async_copy(in_ref, all_ref.at[me], local_sem)
    mine.start()
    sends = []
    for k in range(N_DEV - 1):
        fx, fy, fc = (k + 1) >> 2 & 1, (k + 1) >> 1 & 1, (k + 1) & 1
        cp = pltpu.make_async_remote_copy(
            src_ref=in_ref, dst_ref=all_ref.at[me],
            send_sem=send_sems.at[k], recv_sem=recv_sems.at[k],
            device_id=(x ^ fx, y ^ fy, c ^ fc), device_id_type=MESH)
        cp.start()
        sends.append(cp)
    for k in range(N_DEV - 1):
        fx, fy, fc = (k + 1) >> 2 & 1, (k + 1) >> 1 & 1, (k + 1) & 1
        src_dev = 4 * (x ^ fx) + 2 * (y ^ fy) + (c ^ fc)
        pltpu.make_async_remote_copy(
            src_ref=in_ref, dst_ref=all_ref.at[src_dev],
            send_sem=send_sems.at[k], recv_sem=recv_sems.at[k],
            device_id=(x, y, c), device_id_type=MESH).wait_recv()
    for cp in sends:
        cp.wait_send()
    mine.wait()


SMALL_SEMS = [pltpu.SemaphoreType.DMA((N_DEV - 1,)), pltpu.SemaphoreType.DMA((N_DEV - 1,)),
              pltpu.SemaphoreType.DMA]
VMEM_SPEC = pl.BlockSpec(memory_space=pltpu.VMEM)


def _other_chips(x, y):
    return [(1 - x, y), (x, 1 - y), (1 - x, 1 - y)]


def _any_specs(n):
    return [pl.BlockSpec(memory_space=pl.ANY)] * n


def _halves(rows, c):
    half = rows // 2
    return pl.ds(c * half, half), pl.ds((1 - c) * half, half)


CAST_ROWS = 256


def _gather_copy(out_ref, send_sems, recv_sems, k, quarter, half, to, src=None):
    dst = out_ref.at[quarter, half]
    return pltpu.make_async_remote_copy(
        src_ref=dst if src is None else src, dst_ref=dst,
        send_sem=send_sems.at[k], recv_sem=recv_sems.at[k], device_id=to, device_id_type=MESH)


def allgather_weights(quarters, exchange, small):
    n = len(quarters)
    shapes = [w.shape for w in quarters]
    moved = [i for i in range(n) if exchange[i]]

    def body(*refs):
        w_refs, small_ref = refs[:n], refs[n]
        out_refs, small_all_ref = refs[n + 1:2 * n + 1], refs[2 * n + 1]
        refs = refs[2 * n + 2:]
        f32_bufs, bf_bufs = refs[:n], refs[n:2 * n]
        send_sems, recv_sems, local_sems = refs[2 * n:2 * n + 3]
        x, y, c = _position()
        q = 2 * x + y
        sibling = (x, y, 1 - c)
        chips = _other_chips(x, y)

        def copy(k, i, quarter, half, to, src=None):
            return _gather_copy(out_refs[i], send_sems, recv_sems, k * n + i, quarter, half, to, src)

        loads = [pltpu.make_async_copy(w_refs[i], f32_bufs[i], local_sems.at[i]) for i in range(n)]
        for cp in loads:
            cp.start()
        keeps, sends = [], []
        for i in range(n):
            loads[i].wait()
            for r0 in range(0, shapes[i][0], CAST_ROWS):
                bf_bufs[i][r0:r0 + CAST_ROWS, :] = _bf(f32_bufs[i][r0:r0 + CAST_ROWS, :])
            keep = pltpu.make_async_copy(bf_bufs[i], out_refs[i].at[q], local_sems.at[n + i])
            keep.start()
            keeps.append(keep)
            if not exchange[i]:
                continue
            mine, _ = _halves(shapes[i][0], c)
            for j, chip in enumerate(chips):
                cp = copy(j, i, q, mine, (*chip, c), src=bf_bufs[i].at[mine])
                cp.start()
                sends.append(cp)
        for j, chip in enumerate(chips):
            qj = 2 * chip[0] + chip[1]
            for i in moved:
                mine, _ = _halves(shapes[i][0], c)
                copy(j, i, qj, mine, (x, y, c)).wait_recv()
                cp = copy(3 + j, i, qj, mine, sibling)
                cp.start()
                sends.append(cp)
        for j, chip in enumerate(chips):
            qj = 2 * chip[0] + chip[1]
            for i in moved:
                _, other = _halves(shapes[i][0], c)
                copy(3 + j, i, qj, other, (x, y, c)).wait_recv()
        _gather_small(small_ref, small_all_ref, *refs[2 * n + 3:])
        for cp in sends:
            cp.wait_send()
        for cp in keeps:
            cp.wait()

    outs = pl.pallas_call(
        body, name="allgather_weights",
        out_shape=[jax.ShapeDtypeStruct((N_CHIPS, *s), BF16) for s in shapes]
                  + [jax.ShapeDtypeStruct((N_DEV, *small.shape), small.dtype)],
        in_specs=_any_specs(n) + [VMEM_SPEC], out_specs=_any_specs(n) + [VMEM_SPEC],
        scratch_shapes=([pltpu.VMEM(s, F32) for s in shapes] + [pltpu.VMEM(s, BF16) for s in shapes]
                        + [pltpu.SemaphoreType.DMA((6 * n,)), pltpu.SemaphoreType.DMA((6 * n,)),
                           pltpu.SemaphoreType.DMA((2 * n,))] + SMALL_SEMS),
        compiler_params=pltpu.CompilerParams(vmem_limit_bytes=VMEM_LIMIT),
    )(*quarters, small)
    return outs[:n], outs[n]


def exchange_with_sibling(grads, name):
    n = len(grads)

    def body(*refs):
        g_refs, theirs_refs = refs[:n], refs[n:2 * n]
        send_sems, recv_sems = refs[2 * n:]
        x, y, c = _position()
        copies = []
        for i in range(n):
            _, other = _halves(g_refs[i].shape[1], c)
            cp = pltpu.make_async_remote_copy(
                src_ref=g_refs[i].at[:, other], dst_ref=theirs_refs[i],
                send_sem=send_sems.at[i], recv_sem=recv_sems.at[i],
                device_id=(x, y, 1 - c), device_id_type=MESH)
            cp.start()
            copies.append(cp)
        for cp in copies:
            cp.wait()

    return pl.pallas_call(
        body, name=name,
        out_shape=[jax.ShapeDtypeStruct((N_CHIPS, g.shape[1] // 2, g.shape[2]), F32) for g in grads],
        in_specs=_any_specs(n), out_specs=_any_specs(n),
        scratch_shapes=[pltpu.SemaphoreType.DMA((n,)), pltpu.SemaphoreType.DMA((n,))],
    )(*grads)


def _scatter_copies(b_refs, got_refs, send_sems, recv_sems):
    n = len(b_refs)
    x, y, c = _position()
    copies = []
    for j, chip in enumerate(_other_chips(x, y)):
        qj = 2 * chip[0] + chip[1]
        for i in range(n):
            copies.append(pltpu.make_async_remote_copy(
                src_ref=b_refs[i].at[qj], dst_ref=got_refs[i].at[j],
                send_sem=send_sems.at[j * n + i], recv_sem=recv_sems.at[j * n + i],
                device_id=(*chip, c), device_id_type=MESH))
    return copies


def _scatter_shapes(chip_sums):
    return [jax.ShapeDtypeStruct((N_CHIPS - 1, *b.shape[1:]), BF16) for b in chip_sums]


def scatter_to_owners(chip_sums, name):
    n = len(chip_sums)

    def body(*refs):
        copies = _scatter_copies(refs[:n], refs[n:2 * n], *refs[2 * n:])
        for cp in copies:
            cp.start()
        for cp in copies:
            cp.wait()

    return pl.pallas_call(
        body, name=name,
        out_shape=_scatter_shapes(chip_sums),
        in_specs=_any_specs(n), out_specs=_any_specs(n),
        scratch_shapes=[pltpu.SemaphoreType.DMA((3 * n,)), pltpu.SemaphoreType.DMA((3 * n,))],
    )(*chip_sums)


def join_halves(reduced, small):
    n = len(reduced)

    def body(*refs):
        small_ref = refs[n]
        buf_refs, total_ref = refs[n + 1:2 * n + 1], refs[2 * n + 1]
        send_sems, recv_sems, all_ref = refs[2 * n + 2:2 * n + 5]
        x, y, c = _position()
        copies = []
        for i in range(n):
            mine, _ = _halves(buf_refs[i].shape[0], c)
            cp = pltpu.make_async_remote_copy(
                src_ref=buf_refs[i].at[mine], dst_ref=buf_refs[i].at[mine],
                send_sem=send_sems.at[i], recv_sem=recv_sems.at[i],
                device_id=(x, y, 1 - c), device_id_type=MESH)
            cp.start()
            copies.append(cp)
        _gather_small(small_ref, all_ref, *refs[2 * n + 5:])
        total = all_ref[0]
        for dev in range(1, N_DEV):
            total = total + all_ref[dev]
        total_ref[...] = total
        for cp in copies:
            cp.wait()

    outs = pl.pallas_call(
        body, name="join_halves",
        out_shape=[jax.ShapeDtypeStruct(r.shape, F32) for r in reduced]
                  + [jax.ShapeDtypeStruct(small.shape, small.dtype)],
        in_specs=_any_specs(n) + [VMEM_SPEC], out_specs=_any_specs(n) + [VMEM_SPEC],
        input_output_aliases={i: i for i in range(n)},
        scratch_shapes=[pltpu.SemaphoreType.DMA((n,)), pltpu.SemaphoreType.DMA((n,)),
                        pltpu.VMEM((N_DEV, *small.shape), small.dtype)] + SMALL_SEMS,
    )(*reduced, small)
    return outs[:n], outs[n]


ADD_ROWS = 512


def add_halves(grad, theirs, place, name):
    _, half, cols = theirs.shape
    rb = min(ADD_ROWS, half)
    steps = half // rb

    def body(place_ref, a_ref, b_ref, f_ref, h_ref):
        s = a_ref[0] + b_ref[0]
        h_ref[0] = _bf(s)

        @pl.when(pl.program_id(1) == place_ref[1])
        def _():
            f_ref[...] = s

    spec = pl.BlockSpec((1, rb, cols), lambda j, i, place: (i, j, 0))
    return pl.pallas_call(
        body, name=name,
        grid_spec=pltpu.PrefetchScalarGridSpec(
            num_scalar_prefetch=1, grid=(steps, N_CHIPS),
            in_specs=[pl.BlockSpec((1, rb, cols), lambda j, i, place: (i, place[0] * steps + j, 0)), spec],
            out_specs=(pl.BlockSpec((rb, cols), lambda j, i, place: (j, 0)), spec)),
        out_shape=(jax.ShapeDtypeStruct((half, cols), F32), jax.ShapeDtypeStruct(theirs.shape, BF16)),
        compiler_params=_params("parallel", "arbitrary"),
    )(place, grad, theirs)


def add_parts(own, got, place, name):
    _, half, cols = got.shape
    rb = min(ADD_ROWS, half)
    steps = half // rb

    def body(place_ref, o_ref, g_ref, out_ref):
        s = o_ref[...]
        for j in range(N_CHIPS - 1):
            s = s + g_ref[j].astype(F32)
        out_ref[...] = s

    return pl.pallas_call(
        body, name=name,
        grid_spec=pltpu.PrefetchScalarGridSpec(
            num_scalar_prefetch=1, grid=(steps,),
            in_specs=[pl.BlockSpec((rb, cols), lambda j, place: (j, 0)),
                      pl.BlockSpec((N_CHIPS - 1, rb, cols), lambda j, place: (0, j, 0))],
            out_specs=pl.BlockSpec((rb, cols), lambda j, place: (place[0] * steps + j, 0))),
        out_shape=jax.ShapeDtypeStruct((2 * half, cols), F32),
        compiler_params=_params("parallel"),
    )(place, own, got)


def _adam_math(w, g, m, v):
    m = ADAM_B1 * m + (1.0 - ADAM_B1) * g
    v = ADAM_B2 * v + (1.0 - ADAM_B2) * (g * g)
    m_hat = m / (1.0 - ADAM_B1 ** ADAM_STEP)
    v_hat = v / (1.0 - ADAM_B2 ** ADAM_STEP)
    delta = -ADAM_LR * (m_hat / (jnp.sqrt(v_hat) + ADAM_EPS) + ADAM_WD * w)
    return delta, m, v


def adamw(w, g, m, v, name):
    rows, cols = w.shape
    fits = [t for t in range(8, rows, 8) if rows % t == 0 and t * cols * 4 <= 2 ** 20]
    tile = max(fits) if fits else rows

    def body(w_ref, g_ref, m_ref, v_ref, d_ref, nm_ref, nv_ref):
        d, nm, nv = _adam_math(w_ref[...], g_ref[...], m_ref[...], v_ref[...])
        d_ref[...] = d
        nm_ref[...] = nm
        nv_ref[...] = nv

    spec = pl.BlockSpec((tile, cols), lambda i: (i, 0))
    shape = jax.ShapeDtypeStruct((rows, cols), F32)
    return pl.pallas_call(
        body, name=name, grid=(rows // tile,),
        out_shape=(shape, shape, shape),
        in_specs=[spec] * 4, out_specs=(spec, spec, spec),
        compiler_params=_params("parallel"),
    )(w, g, m, v)


def adamw_small(params):
    n = len(params)

    def body(*refs):
        ins, outs = refs[:4 * n], refs[4 * n:]
        for k in range(n):
            w_ref, g_ref, m_ref, v_ref = ins[4 * k:4 * k + 4]
            d, nm, nv = _adam_math(w_ref[...], g_ref[...], m_ref[...], v_ref[...])
            outs[3 * k][...] = d
            outs[3 * k + 1][...] = nm
            outs[3 * k + 2][...] = nv

    flat = [a for p in params for a in p]
    outs = pl.pallas_call(
        body, name="adamw_small",
        out_shape=[jax.ShapeDtypeStruct(p[0].shape, F32) for p in params for _ in range(3)],
        in_specs=[VMEM_SPEC] * (4 * n), out_specs=[VMEM_SPEC] * (3 * n),
    )(*flat)
    return [tuple(outs[3 * k:3 * k + 3]) for k in range(n)]


def adamw_rows(w, g, m, v, name):
    rows, _, cols = w.shape
    tile = rows // 4

    def body(w_ref, g_ref, m_ref, v_ref, d_ref, nm_ref, nv_ref):
        d, nm, nv = _adam_math(w_ref[...], g_ref[...], m_ref[...], v_ref[...])
        d_ref[...] = d
        nm_ref[...] = nm
        nv_ref[...] = nv

    spec = pl.BlockSpec((tile, 1, cols), lambda i: (i, 0, 0))
    shape = jax.ShapeDtypeStruct(w.shape, F32)
    return pl.pallas_call(
        body, name=name, grid=(rows // tile,),
        out_shape=(shape, shape, shape),
        in_specs=[spec] * 4, out_specs=(spec, spec, spec),
        compiler_params=_params("parallel"),
    )(w, g, m, v)


def matmul_tn(a, b, name, tile_n=512, tile_s=2048, by_column_tile=False):
    s, m = a.shape
    n = b.shape[1]
    tile_n = min(tile_n, n)
    tile_s = min(tile_s, s)
    steps = s // tile_s
    if by_column_tile:
        out_shape = jax.ShapeDtypeStruct((n // tile_n, m, tile_n), F32)
        out_spec = pl.BlockSpec((None, m, tile_n), lambda j, k: (j, 0, 0))
    else:
        out_shape = jax.ShapeDtypeStruct((m, n), F32)
        out_spec = pl.BlockSpec((m, tile_n), lambda j, k: (0, j))

    def body(a_ref, b_ref, out_ref):
        k = pl.program_id(1)

        @pl.when(k == 0)
        def _():
            out_ref[...] = jnp.zeros_like(out_ref)

        out_ref[...] += _tn(a_ref[...], b_ref[...])

    return pl.pallas_call(
        body, name=name, grid=(n // tile_n, steps),
        out_shape=out_shape,
        in_specs=[pl.BlockSpec((tile_s, m), lambda j, k: (k, 0)),
                  pl.BlockSpec((tile_s, tile_n), lambda j, k: (k, j))],
        out_specs=out_spec,
        compiler_params=_params("parallel", "arbitrary"),
    )(a, b)


ROW_TILE = 512


def _row_index(tile, rows):
    return tile * rows + lax.broadcasted_iota(jnp.int32, (rows, 1), 0)


def _inverse_counts(t_glob):
    return [1.0 / jnp.minimum(t_glob + 1, w).astype(F32) for w in POOL_WINDOWS]


def _sigmoid(z):
    return 1.0 / (1.0 + jnp.exp(-z))


def _trailing_sums(src, tmp, cols, window, rows):
    bufs = (src, tmp)
    span, level, start = 1, 0, 0
    while span < window:
        start += 8
        a, b = bufs[level % 2], bufs[(level + 1) % 2]
        n = HALO + rows - start
        b[start:start + n, cols] = a[start:start + n, cols] + a[start - span:start - span + n, cols]
        span, level = 2 * span, level + 1
    return bufs[level % 2][HALO:HALO + rows, cols]


def _leading_sums(src, tmp, cols, window, rows):
    bufs = (src, tmp)
    span, level, n = 1, 0, rows + HALO
    while span < window:
        n -= 8
        a, b = bufs[level % 2], bufs[(level + 1) % 2]
        b[0:n, cols] = a[0:n, cols] + a[span:span + n, cols]
        span, level = 2 * span, level + 1
    return bufs[level % 2][0:rows, cols]


def gather_in_background(step, last, out_refs, send_sems, recv_sems, finish):
    n = len(out_refs)
    x, y, c = _position()
    q = 2 * x + y
    chips = _other_chips(x, y)

    def copy(k, i, quarter, half, to):
        return _gather_copy(out_refs[i], send_sems, recv_sems, k * n + i, quarter, half, to)

    if not finish:
        @pl.when(step == 0)
        def _():
            for i in range(n):
                mine, _ = _halves(out_refs[i].shape[1], c)
                for j, chip in enumerate(chips):
                    copy(j, i, q, mine, (*chip, c)).start()

        @pl.when(step == last)
        def _():
            for j, chip in enumerate(chips):
                qj = 2 * chip[0] + chip[1]
                for i in range(n):
                    mine, _ = _halves(out_refs[i].shape[1], c)
                    copy(j, i, qj, mine, (x, y, c)).wait_recv()
                    copy(3 + j, i, qj, mine, (x, y, 1 - c)).start()
        return

    @pl.when(step == last)
    def _():
        for j, chip in enumerate(chips):
            qj = 2 * chip[0] + chip[1]
            for i in range(n):
                mine, other = _halves(out_refs[i].shape[1], c)
                copy(3 + j, i, qj, other, (x, y, c)).wait_recv()
                copy(j, i, q, mine, (x, y, c)).wait_send()
                copy(3 + j, i, qj, mine, (x, y, c)).wait_send()


def pool_forward(x, w0, wpi, gw, gb, scale, wpo, later):
    s = x.shape[0]
    ts = ROW_TILE
    nt = s // ts
    assert nt >= 2
    n_later = len(later)

    def body(x_ref, w0_ref, wpi_ref, gw_ref, gb_ref, sc_ref, wpo_ref, *rest):
        rest = rest[n_later:]
        h1_ref, pooled_ref, gt_ref, n0_ref = rest[:4]
        later_refs = rest[4:4 + n_later]
        ubuf, tbuf, hist, send_sems, recv_sems = rest[4 + n_later:]
        i = pl.program_id(0)
        gather_in_background(i, nt - 1, later_refs, send_sems, recv_sems, finish=False)
        xv = x_ref[...]
        r = lax.rsqrt(jnp.mean(xv * xv, axis=-1, keepdims=True) + EPS)
        n0 = _bf(xv * r * w0_ref[...])
        n0_ref[...] = n0
        u = jnp.concatenate([_nn(n0, wpi_ref[0]), _nn(n0, wpi_ref[1])], axis=-1)
        gt = jnp.concatenate([_nn(n0, wpi_ref[2]), _nn(n0, wpi_ref[3])], axis=-1)
        gt_ref[...] = gt

        @pl.when(i == 0)
        def _():
            hist[...] = jnp.zeros_like(hist)

        ubuf[0:HALO, :] = hist[...]
        ubuf[HALO:HALO + ts, :] = u
        hist[...] = u[ts - HALO:, :]
        inv = _inverse_counts(_row_index(i, ts))
        mixed = []
        for g, w in enumerate(POOL_WINDOWS):
            cols = slice(g * GROUP_DIM, (g + 1) * GROUP_DIM)
            pooled = _bf(_trailing_sums(ubuf, tbuf, cols, w, ts) * inv[g] - u[:, cols])
            pooled_ref[:, cols] = pooled
            mixed.append(_nn(pooled, gw_ref[g]))
        mixed = jnp.concatenate(mixed, axis=-1) + gb_ref[...]
        y = mixed * sc_ref[...] * (gt * _sigmoid(gt))
        h1_ref[...] = xv + _nn(_bf(y), wpo_ref[...])
        gather_in_background(i, nt - 1, later_refs, send_sems, recv_sems, finish=True)

    row = lambda cols: pl.BlockSpec((ts, cols), lambda i: (i, 0))
    outs = pl.pallas_call(
        body, name="pool_forward", grid=(nt,),
        out_shape=[jax.ShapeDtypeStruct((s, D), F32), jax.ShapeDtypeStruct((s, D), BF16),
                   jax.ShapeDtypeStruct((s, D), F32), jax.ShapeDtypeStruct((s, D), BF16)]
                  + [jax.ShapeDtypeStruct(a.shape, a.dtype) for a in later],
        in_specs=[row(D), _full((1, D)), _full((N_CHIPS, D, D // 2)), _full((GROUPS, GROUP_DIM, GROUP_DIM)),
                  _full((1, D)), _full((1, D)), _full((D, D))] + _any_specs(n_later),
        out_specs=[row(D), row(D), row(D), row(D)] + _any_specs(n_later),
        input_output_aliases={7 + k: 4 + k for k in range(n_later)},
        scratch_shapes=[pltpu.VMEM((HALO + ts, D), F32), pltpu.VMEM((HALO + ts, D), F32),
                        pltpu.VMEM((HALO, D), F32),
                        pltpu.SemaphoreType.DMA((6 * n_later,)), pltpu.SemaphoreType.DMA((6 * n_later,))],
        compiler_params=_params("arbitrary"),
    )(x, w0, wpi, gw, gb, scale, wpo, *later)
    return outs[:4], outs[4:]


def pool_backward(x, dh1, pooled, gt, w0, wpi, gw, gb, scale, wpo, chip_sums):
    s = x.shape[0]
    ts = ROW_TILE
    nt = s // ts
    n_sums = len(chip_sums)

    def body(x_ref, dh1_ref, pooled_ref, gt_ref, w0_ref, wpi_ref, gw_ref, gb_ref, sc_ref, wpo_ref, *rest):
        sum_refs, rest = rest[:n_sums], rest[n_sums:]
        dx_ref, dproj_ref, gpo_ref, ggw_ref, small_ref = rest[:5]
        got_refs = rest[5:5 + n_sums]
        ebuf, tbuf, ahead, send_sems, recv_sems = rest[5 + n_sums:]
        i = pl.program_id(0)
        copies = _scatter_copies(sum_refs, got_refs, send_sems, recv_sems)

        @pl.when(i == 0)
        def _():
            for cp in copies:
                cp.start()

        @pl.when(i == 0)
        def _():
            gpo_ref[...] = jnp.zeros_like(gpo_ref)
            ggw_ref[...] = jnp.zeros_like(ggw_ref)
            small_ref[...] = jnp.zeros_like(small_ref)
            ahead[...] = jnp.zeros_like(ahead)

        dh1 = dh1_ref[...]
        dh1_bf = _bf(dh1)
        gt = gt_ref[...]
        sc = sc_ref[...]
        dy = _nt(dh1_bf, wpo_ref[...])
        pooled_bf = []
        mixed = []
        for g in range(GROUPS):
            cols = slice(g * GROUP_DIM, (g + 1) * GROUP_DIM)
            pb = pooled_ref[:, cols]
            pooled_bf.append(pb)
            mixed.append(_nn(pb, gw_ref[g]))
        mixed = jnp.concatenate(mixed, axis=-1) + gb_ref[...]
        sg = _sigmoid(gt)
        silu = gt * sg
        gpo_ref[...] += _tn(_bf(mixed * sc * silu), dh1_bf)
        dmixed = dy * sc * silu
        dgt = dy * mixed * sc * (sg * (1.0 + gt * (1.0 - sg)))
        dproj_ref[:, D:] = _bf(dgt)
        small_ref[1:2, :] += jnp.sum(dy * mixed * silu, axis=0, keepdims=True)
        small_ref[2:3, :] += jnp.sum(dmixed, axis=0, keepdims=True)

        inv = _inverse_counts(_row_index(nt - 1 - i, ts))
        ebuf[ts:ts + HALO, :] = ahead[...]
        dpooled = []
        for g in range(GROUPS):
            cols = slice(g * GROUP_DIM, (g + 1) * GROUP_DIM)
            dm = _bf(dmixed[:, cols])
            ggw_ref[g] += _tn(pooled_bf[g], dm)
            dp = _nt(dm, gw_ref[g])
            dpooled.append(dp)
            ebuf[0:ts, cols] = dp * inv[g]
        ahead[...] = ebuf[0:HALO, :]
        du = []
        for g, w in enumerate(POOL_WINDOWS):
            cols = slice(g * GROUP_DIM, (g + 1) * GROUP_DIM)
            du.append(_leading_sums(ebuf, tbuf, cols, w, ts) - dpooled[g])
        du = _bf(jnp.concatenate(du, axis=-1))
        dproj_ref[:, :D] = du
        dgt_bf = _bf(dgt)
        half = D // 2
        dn0 = (_nt(du[:, :half], wpi_ref[0]) + _nt(du[:, half:], wpi_ref[1])
               + _nt(dgt_bf[:, :half], wpi_ref[2]) + _nt(dgt_bf[:, half:], wpi_ref[3]))

        xv = x_ref[...]
        r = lax.rsqrt(jnp.mean(xv * xv, axis=-1, keepdims=True) + EPS)
        xhat = xv * r
        small_ref[0:1, :] += jnp.sum(dn0 * xhat, axis=0, keepdims=True)
        dxh = dn0 * w0_ref[...]
        dx_ref[...] = dh1 + r * (dxh - xhat * jnp.mean(dxh * xhat, axis=-1, keepdims=True))

        @pl.when(i == nt - 1)
        def _():
            for cp in copies:
                cp.wait()

    row = lambda cols: pl.BlockSpec((ts, cols), lambda i: (nt - 1 - i, 0))
    outs = pl.pallas_call(
        body, name="pool_backward", grid=(nt,),
        out_shape=[jax.ShapeDtypeStruct((s, D), F32), jax.ShapeDtypeStruct((s, 2 * D), BF16),
                   jax.ShapeDtypeStruct((D, D), F32),
                   jax.ShapeDtypeStruct((GROUPS, GROUP_DIM, GROUP_DIM), F32),
                   jax.ShapeDtypeStruct((8, D), F32)] + _scatter_shapes(chip_sums),
        in_specs=[row(D), row(D), row(D), row(D), _full((1, D)), _full((N_CHIPS, D, D // 2)),
                  _full((GROUPS, GROUP_DIM, GROUP_DIM)), _full((1, D)), _full((1, D)), _full((D, D))]
                 + _any_specs(n_sums),
        out_specs=[row(D), row(2 * D), _full((D, D)), _full((GROUPS, GROUP_DIM, GROUP_DIM)), _full((8, D))]
                  + _any_specs(n_sums),
        scratch_shapes=[pltpu.VMEM((ts + HALO, D), F32), pltpu.VMEM((ts + HALO, D), F32),
                        pltpu.VMEM((HALO, D), F32),
                        pltpu.SemaphoreType.DMA((3 * n_sums,)), pltpu.SemaphoreType.DMA((3 * n_sums,))],
        compiler_params=_params("arbitrary"),
    )(x, dh1, pooled, gt, w0, wpi, gw, gb, scale, wpo, *chip_sums)
    return outs[:5], outs[5:]


def gla_project(h1, w1, wgi, wlow, wgk, bgk, later):
    s = h1.shape[0]
    ts = ROW_TILE
    nt = s // ts
    assert nt >= 2
    n_later = len(later)

    def body(h_ref, w1_ref, wgi_ref, wlow_ref, wgk_ref, bgk_ref, *rest):
        rest = rest[n_later:]
        qk_ref, v_ref, gate_ref, low_ref, cum_ref, n1_ref = rest[:6]
        later_refs = rest[6:6 + n_later]
        send_sems, recv_sems = rest[6 + n_later:]
        gather_in_background(pl.program_id(0), nt - 1, later_refs, send_sems, recv_sems, finish=False)
        hv = h_ref[...]
        r = lax.rsqrt(jnp.mean(hv * hv, axis=-1, keepdims=True) + EPS)
        n1 = _bf(hv * r * w1_ref[...])
        n1_ref[...] = n1
        qk_ref[...] = _nn(n1, wgi_ref[:, 0:2 * KEY_W])
        v_ref[...] = _bf(_nn(n1, wgi_ref[:, 2 * KEY_W:2 * KEY_W + D]))
        gate_ref[...] = _nn(n1, wgi_ref[:, 2 * KEY_W + D:GLA_MAIN])
        low = _bf(_nn(n1, wlow_ref[...]))
        low_ref[...] = low
        z = _nn(low, wgk_ref[...]) + bgk_ref[...]
        lg = (jnp.minimum(z, 0.0) - jnp.log(1.0 + jnp.exp(-jnp.abs(z)))) / GATE_NORM
        lower_f = _chunk_masks()[0].astype(F32)
        for r0 in range(0, ts, CHUNK):
            cum_ref[r0:r0 + CHUNK, :] = _nn_exact(lower_f, lg[r0:r0 + CHUNK, :])
        gather_in_background(pl.program_id(0), nt - 1, later_refs, send_sems, recv_sems, finish=True)

    row = lambda cols: pl.BlockSpec((ts, cols), lambda i: (i, 0))
    outs = pl.pallas_call(
        body, name="gla_project", grid=(nt,),
        out_shape=[jax.ShapeDtypeStruct((s, D), F32), jax.ShapeDtypeStruct((s, D), BF16),
                   jax.ShapeDtypeStruct((s, D), F32), jax.ShapeDtypeStruct((s, RANK_PAD), BF16),
                   jax.ShapeDtypeStruct((s, KEY_W), F32), jax.ShapeDtypeStruct((s, D), BF16)]
                  + [jax.ShapeDtypeStruct(a.shape, a.dtype) for a in later],
        in_specs=[row(D), _full((1, D)), _full((D, GLA_MAIN)), _full((D, RANK_PAD)),
                  _full((RANK_PAD, KEY_W)), _full((1, KEY_W))] + _any_specs(n_later),
        out_specs=[row(D), row(D), row(D), row(RANK_PAD), row(KEY_W), row(D)] + _any_specs(n_later),
        input_output_aliases={6 + k: 6 + k for k in range(n_later)},
        scratch_shapes=[pltpu.SemaphoreType.DMA((6 * n_later,)), pltpu.SemaphoreType.DMA((6 * n_later,))],
        compiler_params=_params("arbitrary"),
    )(h1, w1, wgi, wlow, wgk, bgk, *later)
    return outs[:6], outs[6:]


GLA_BLOCK = 512
CHUNKS_PER_BLOCK = GLA_BLOCK // CHUNK


def _chunk_masks():
    t = lax.broadcasted_iota(jnp.int32, (CHUNK, CHUNK), 0)
    u = lax.broadcasted_iota(jnp.int32, (CHUNK, CHUNK), 1)
    return t >= u, t <= u


def _gla_chunk_terms(q, cum):
    ep = jnp.exp(cum)
    en = jnp.exp(-cum)
    qs = q * (HEAD_K ** -0.5)
    last = cum[CHUNK - 1:CHUNK, :]
    ed = jnp.exp(last - cum)
    dec = jnp.exp(last)
    return ep, en, qs, ed, dec


def gla_forward(qk, v, cum):
    s = qk.shape[0]
    nb = s // GLA_BLOCK
    nc = s // CHUNK

    def body(q_ref, k_ref, v_ref, cum_ref, o_ref, st_ref, sc_ref, state):
        @pl.when(pl.program_id(0) == 0)
        def _():
            state[...] = jnp.zeros_like(state)

        lower, _ = _chunk_masks()

        def chunk(cc, carry):
            rows = pl.ds(pl.multiple_of(cc * CHUNK, CHUNK), CHUNK)
            for h in range(HEADS):
                kc = slice(h * HEAD_K, (h + 1) * HEAD_K)
                vc = slice(h * HEAD_V, (h + 1) * HEAD_V)
                q = q_ref[rows, kc]
                k = k_ref[rows, kc]
                v = v_ref[rows, vc]
                ep, en, qs, ed, dec = _gla_chunk_terms(q, cum_ref[rows, kc])
                a = _bf(qs * ep)
                fwd = _nt(a, _bf(k * en))
                bwd = _nt(_bf(qs * en), _bf(k * ep))
                scores = _bf(jnp.where(lower, fwd, bwd))
                sc_ref[rows, h * CHUNK:(h + 1) * CHUNK] = scores
                st = state[h]
                st_ref[cc, h] = st
                o_ref[rows, vc] = _nn(scores, v) + _nt(a, _bf(st))
                state[h] = st * dec + _tn(v, _bf(k * ed))
            return carry

        lax.fori_loop(0, CHUNKS_PER_BLOCK, chunk, 0, unroll=4)

    return pl.pallas_call(
        body, name="gla_forward", grid=(nb,),
        out_shape=(jax.ShapeDtypeStruct((s, D), F32),
                   jax.ShapeDtypeStruct((nc, HEADS, HEAD_V, HEAD_K), F32),
                   jax.ShapeDtypeStruct((s, HEADS * CHUNK), BF16)),
        in_specs=[pl.BlockSpec((GLA_BLOCK, KEY_W), lambda i: (i, 0)),
                  pl.BlockSpec((GLA_BLOCK, KEY_W), lambda i: (i, 1)),
                  pl.BlockSpec((GLA_BLOCK, D), lambda i: (i, 0)),
                  pl.BlockSpec((GLA_BLOCK, KEY_W), lambda i: (i, 0))],
        out_specs=(pl.BlockSpec((GLA_BLOCK, D), lambda i: (i, 0)),
                   pl.BlockSpec((CHUNKS_PER_BLOCK, HEADS, HEAD_V, HEAD_K), lambda i: (i, 0, 0, 0)),
                   pl.BlockSpec((GLA_BLOCK, HEADS * CHUNK), lambda i: (i, 0))),
        scratch_shapes=[pltpu.VMEM((HEADS, HEAD_V, HEAD_K), F32)],
        compiler_params=_params("arbitrary"),
    )(qk, qk, v, cum)


def gla_backward(qk, v, cum, do, states, scores):
    s = qk.shape[0]
    nb = s // GLA_BLOCK

    def body(q_ref, k_ref, v_ref, cum_ref, do_ref, st_ref, sc_ref, dq_ref, dk_ref, dv_ref, dcum_ref, dstate):
        @pl.when(pl.program_id(0) == 0)
        def _():
            dstate[...] = jnp.zeros_like(dstate)

        lower, _ = _chunk_masks()
        is_last = lax.broadcasted_iota(jnp.int32, (CHUNK, HEAD_K), 0) == CHUNK - 1

        def chunk(step, carry):
            cc = CHUNKS_PER_BLOCK - 1 - step
            rows = pl.ds(pl.multiple_of(cc * CHUNK, CHUNK), CHUNK)
            for h in range(HEADS):
                kc = slice(h * HEAD_K, (h + 1) * HEAD_K)
                vc = slice(h * HEAD_V, (h + 1) * HEAD_V)
                q = q_ref[rows, kc]
                k = k_ref[rows, kc]
                v = v_ref[rows, vc]
                do_c = do_ref[rows, vc]
                ep, en, qs, ed, dec = _gla_chunk_terms(q, cum_ref[rows, kc])
                a = _bf(qs * ep)
                b = _bf(k * en)
                c = _bf(qs * en)
                dk_dec = _bf(k * ep)
                kd = _bf(k * ed)
                scores = sc_ref[rows, h * CHUNK:(h + 1) * CHUNK]
                st = st_ref[cc, h]
                dst = dstate[h]
                dst_bf = _bf(dst)

                dscores = _nt(do_c, v)
                dfwd = _bf(jnp.where(lower, dscores, 0.0))
                dbwd = _bf(jnp.where(lower, 0.0, dscores))
                dv_ref[rows, vc] = _bf(_tn(scores, do_c) + _nt(kd, dst_bf))
                da = _nn(dfwd, b) + _nn(do_c, _bf(st))
                db = _tn(dfwd, a)
                dc = _nn(dbwd, dk_dec)
                ddk = _tn(dbwd, c)
                dkd = _nn(v, dst_bf)
                ddec = jnp.sum(dst * st, axis=0, keepdims=True)
                dstate[h] = dst * dec + _tn(do_c, a)

                m = dkd * k * ed
                dq_ref[rows, kc] = _bf((da * ep + dc * en) * (HEAD_K ** -0.5))
                dk_ref[rows, kc] = _bf(db * en + ddk * ep + dkd * ed)
                dcum = (da * qs + ddk * k) * ep - (db * k + dc * qs) * en - m
                dlast = jnp.sum(m, axis=0, keepdims=True) + ddec * dec
                dcum_ref[rows, kc] = dcum + jnp.where(is_last, dlast, 0.0)
            return carry

        lax.fori_loop(0, CHUNKS_PER_BLOCK, chunk, 0, unroll=4)

    rev = lambda cols, col_block: pl.BlockSpec((GLA_BLOCK, cols), lambda i: (nb - 1 - i, col_block))
    return pl.pallas_call(
        body, name="gla_backward", grid=(nb,),
        out_shape=(jax.ShapeDtypeStruct((s, KEY_W), BF16), jax.ShapeDtypeStruct((s, KEY_W), BF16),
                   jax.ShapeDtypeStruct((s, D), BF16), jax.ShapeDtypeStruct((s, KEY_W), F32)),
        in_specs=[rev(KEY_W, 0), rev(KEY_W, 1), rev(D, 0), rev(KEY_W, 0), rev(D, 0),
                  pl.BlockSpec((CHUNKS_PER_BLOCK, HEADS, HEAD_V, HEAD_K), lambda i: (nb - 1 - i, 0, 0, 0)),
                  rev(HEADS * CHUNK, 0)],
        out_specs=(rev(KEY_W, 0), rev(KEY_W, 0), rev(D, 0), rev(KEY_W, 0)),
        scratch_shapes=[pltpu.VMEM((HEADS, HEAD_V, HEAD_K), F32)],
        compiler_params=_params("arbitrary"),
    )(qk, qk, v, cum, do, states, scores)


def head_and_loss(o, gate, h1, target, hw, wgo, wf):
    s = o.shape[0]
    ts = ROW_TILE

    def body(o_ref, gate_ref, h1_ref, tgt_ref, hw_ref, wgo_ref, wf_ref,
             dh2_ref, do_ref, dgate_ref, ggo_ref, small_ref):
        @pl.when(pl.program_id(0) == 0)
        def _():
            ggo_ref[...] = jnp.zeros_like(ggo_ref)
            small_ref[...] = jnp.zeros_like(small_ref)

        gate = gate_ref[...]
        hw = hw_ref[...]
        sg = _sigmoid(gate)
        silu = gate * sg
        ohat, ro = [], []
        for h in range(HEADS):
            oh = o_ref[:, h * HEAD_V:(h + 1) * HEAD_V]
            rh = lax.rsqrt(jnp.mean(oh * oh, axis=-1, keepdims=True) + EPS)
            ro.append(rh)
            ohat.append(oh * rh)
        ohat = jnp.concatenate(ohat, axis=-1)
        on = ohat * hw
        y2 = _bf(on * silu)
        h2 = h1_ref[...] + _nn(y2, wgo_ref[...])
        rf = lax.rsqrt(jnp.mean(h2 * h2, axis=-1, keepdims=True) + EPS)
        h2hat = h2 * rf
        wf = wf_ref[...]
        diff = h2hat * wf - tgt_ref[...]
        small_ref[2:3, :] += jnp.zeros((1, D), F32) + 0.5 * jnp.sum(diff * diff) / D
        dout = diff / D
        small_ref[0:1, :] += jnp.sum(dout * h2hat, axis=0, keepdims=True)
        dxh = dout * wf
        dh2 = rf * (dxh - h2hat * jnp.mean(dxh * h2hat, axis=-1, keepdims=True))
        dh2_ref[...] = dh2
        dh2_bf = _bf(dh2)
        ggo_ref[...] += _tn(y2, dh2_bf)
        dy2 = _nt(dh2_bf, wgo_ref[...])
        don = dy2 * silu
        dgate_ref[...] = _bf(dy2 * on * (sg * (1.0 + gate * (1.0 - sg))))
        ghw = jnp.sum(don * ohat, axis=0, keepdims=True)
        small_ref[1:2, 0:HEAD_V] += sum(ghw[:, h * HEAD_V:(h + 1) * HEAD_V] for h in range(HEADS))
        dohat = don * hw
        for h in range(HEADS):
            cols = slice(h * HEAD_V, (h + 1) * HEAD_V)
            oh, dh = ohat[:, cols], dohat[:, cols]
            do_ref[:, cols] = _bf(ro[h] * (dh - oh * jnp.mean(dh * oh, axis=-1, keepdims=True)))

    row = lambda cols: pl.BlockSpec((ts, cols), lambda i: (i, 0))
    act = jax.ShapeDtypeStruct((s, D), F32)
    act_bf = jax.ShapeDtypeStruct((s, D), BF16)
    return pl.pallas_call(
        body, name="head_and_loss", grid=(s // ts,),
        out_shape=(act, act_bf, act_bf, jax.ShapeDtypeStruct((D, D), F32), jax.ShapeDtypeStruct((8, D), F32)),
        in_specs=[row(D), row(D), row(D), row(D),
                  _full((1, D)), _full((D, D)), _full((1, D))],
        out_specs=(row(D), row(D), row(D), _full((D, D)), _full((8, D))),
        compiler_params=_params("arbitrary"),
    )(o, gate, h1, target, hw, wgo, wf)


def gla_project_backward(dq, dk, dv, dgate, dcum, low, h1, dh2, w1, wgi, wlow, wgk, bgk):
    s = h1.shape[0]
    ts = ROW_TILE

    def body(dq_ref, dk_ref, dv_ref, dgate_ref, dcum_ref, low_ref, h1_ref, dh2_ref, w1_ref,
             wgi_ref, wlow_ref, wgk_ref, bgk_ref, dh1_ref, dproj_ref, dlow_ref, ggk_ref, small_ref):
        @pl.when(pl.program_id(0) == 0)
        def _():
            ggk_ref[...] = jnp.zeros_like(ggk_ref)
            small_ref[...] = jnp.zeros_like(small_ref)

        low = low_ref[...]
        z = _nn(low, wgk_ref[...]) + bgk_ref[...]
        upper_f = _chunk_masks()[1].astype(F32)
        dlg = jnp.concatenate([_nn_exact(upper_f, dcum_ref[r0:r0 + CHUNK, :]) for r0 in range(0, ts, CHUNK)],
                              axis=0)
        dz = dlg * (1.0 / GATE_NORM) * _sigmoid(-z)
        dz_bf = _bf(dz)
        ggk_ref[...] += _tn(low, dz_bf)
        small_ref[1:2, 0:KEY_W] += jnp.sum(dz, axis=0, keepdims=True)
        dlow = _bf(_nt(dz_bf, wgk_ref[...]))
        dlow_ref[...] = dlow
        dn1 = _nt(dlow, wlow_ref[...])
        for ref, lo, hi in ((dq_ref, 0, KEY_W), (dk_ref, KEY_W, 2 * KEY_W),
                            (dv_ref, 2 * KEY_W, 2 * KEY_W + D), (dgate_ref, 2 * KEY_W + D, GLA_MAIN)):
            piece = ref[...]
            dproj_ref[:, lo:hi] = piece
            dn1 = dn1 + _nt(piece, wgi_ref[:, lo:hi])
        hv = h1_ref[...]
        r = lax.rsqrt(jnp.mean(hv * hv, axis=-1, keepdims=True) + EPS)
        hhat = hv * r
        small_ref[0:1, :] += jnp.sum(dn1 * hhat, axis=0, keepdims=True)
        dxh = dn1 * w1_ref[...]
        dh1_ref[...] = dh2_ref[...] + r * (dxh - hhat * jnp.mean(dxh * hhat, axis=-1, keepdims=True))

    row = lambda cols: pl.BlockSpec((ts, cols), lambda i: (i, 0))
    return pl.pallas_call(
        body, name="gla_project_backward", grid=(s // ts,),
        out_shape=(jax.ShapeDtypeStruct((s, D), F32), jax.ShapeDtypeStruct((s, GLA_MAIN), BF16),
                   jax.ShapeDtypeStruct((s, RANK_PAD), BF16), jax.ShapeDtypeStruct((RANK_PAD, KEY_W), F32),
                   jax.ShapeDtypeStruct((8, D), F32)),
        in_specs=[row(KEY_W), row(KEY_W), row(D), row(D), row(KEY_W), row(RANK_PAD), row(D), row(D),
                  _full((1, D)), _full((D, GLA_MAIN)), _full((D, RANK_PAD)), _full((RANK_PAD, KEY_W)),
                  _full((1, KEY_W))],
        out_specs=(row(D), row(GLA_MAIN), row(RANK_PAD), _full((RANK_PAD, KEY_W)), _full((8, D))),
        compiler_params=_params("arbitrary"),
    )(dq, dk, dv, dgate, dcum, low, h1, dh2, w1, wgi, wlow, wgk, bgk)


def _groups_from_quarters(a):
    return a.reshape(N_CHIPS, GROUPS, 64, GROUP_DIM).transpose(1, 0, 2, 3).reshape(GROUPS, GROUP_DIM, GROUP_DIM)


def _quarters_from_groups(a):
    return a.reshape(GROUPS, N_CHIPS, 64, GROUP_DIM).transpose(1, 0, 2, 3).reshape(N_CHIPS, GROUP_DIM, GROUP_DIM)


def _pad_row(*pieces):
    flat = jnp.concatenate([p.reshape(-1).astype(F32) for p in pieces])
    return jnp.pad(flat, (0, D - flat.shape[0])).reshape(1, D)


def _gla_in_weights(wgi_q):
    wgi_all = jnp.concatenate([wgi_q[q] for q in range(N_CHIPS)], axis=1)
    wlow = jnp.pad(wgi_all[:, GLA_MAIN:], ((0, 0), (0, RANK_PAD - GATE_RANK)))
    return wgi_all, wlow


def _small_sums(small_top, small_gla, small_pool, g_gk_pad):
    return jnp.concatenate([
        small_pool[0:1], small_gla[0:1],
        small_pool[1:2],
        small_top[0:1],
        small_top[2:3],
        _pad_row(small_gla[1, 0:KEY_W], small_top[1, 0:HEAD_V]),
        small_pool[2:3],
        g_gk_pad[:GATE_RANK].reshape(8, D),
        jnp.zeros((1, D), F32)], axis=0)


def local_gradients(xs, target, w0, w1, wf, wpi, gw, gb, scale, wpo, gla_quarters, wgk, bgk, hw_tiled, place):
    wgi_q, wgo_q = gla_quarters
    (h1, pooled, gt, n0), (wgi_q,) = pool_forward(xs, w0, wpi, gw, gb, scale, wpo, [wgi_q])
    wgi, wlow = _gla_in_weights(wgi_q)
    (qk, v, gate, low, cum, n1), (wgo_q,) = gla_project(h1, w1, wgi, wlow, wgk, bgk, [wgo_q])
    wgo = wgo_q.reshape(D, D)
    o, states, scores = gla_forward(qk, v, cum)

    dh2, do, dgate, g_gla_out, small_top = head_and_loss(o, gate, h1, target, hw_tiled, wgo, wf)
    dq, dk, dv, dcum = gla_backward(qk, v, cum, do, states, scores)
    dh1, dproj, dlow, g_gk_pad, small_gla = gla_project_backward(
        dq, dk, dv, dgate, dcum, low, h1, dh2, w1, wgi, wlow, wgk, bgk)
    g_gla_in = jnp.concatenate([matmul_tn(n1, dproj, "grad_gla_in"),
                                matmul_tn(n1, dlow, "grad_gla_low")[:, :GATE_RANK]], axis=1)

    def chip_sums(grads, names, tag):
        theirs = exchange_with_sibling(grads, "exchange_with_sibling_" + tag)
        return [add_halves(g, t, place, "add_halves_" + n) for g, t, n in zip(grads, theirs, names)]

    gla_names = ("gla_in", "gla_out")
    gla_sums = chip_sums(
        [jnp.stack([g_gla_in[:, GLA_IN_QUARTER * q:GLA_IN_QUARTER * (q + 1)] for q in range(N_CHIPS)]),
         g_gla_out.reshape(N_CHIPS, D // N_CHIPS, D)], gla_names, "gla")
    (dx, dpool, g_pool_out, g_group_w, small_pool), gla_got = pool_backward(
        xs, dh1, pooled, gt, w0, wpi, gw, gb, scale, wpo, [b for _, b in gla_sums])
    g_pool_in = matmul_tn(n0, dpool, "grad_pool_in", by_column_tile=True)

    pool_names = ("pool_in", "group", "pool_out")
    pool_sums = chip_sums(
        [g_pool_in, _quarters_from_groups(g_group_w), g_pool_out.reshape(N_CHIPS, D // N_CHIPS, D)],
        pool_names, "pool")
    pool_got = scatter_to_owners([b for _, b in pool_sums], "scatter_to_owners_pool")
    reduced, total = join_halves(
        [add_parts(f, g, place, "add_parts_" + n) for (f, _), g, n in
         zip(pool_sums + gla_sums, list(pool_got) + list(gla_got), pool_names + gla_names)],
        _small_sums(small_top, small_gla, small_pool, g_gk_pad))
    return dx, reduced, total


def kernel(x, norm_w, pool_in_w, pool_group_w, pool_group_b, pool_scale, pool_out_w, gla_in_w, gla_gk_w, gla_gk_b, gla_head_norm_w, gla_out_w, final_norm_w, loss_target, m_norm_w, m_pool_in_w, m_pool_group_w, m_pool_group_b, m_pool_scale, m_pool_out_w, m_gla_in_w, m_gla_gk_w, m_gla_gk_b, m_gla_head_norm_w, m_gla_out_w, m_final_norm_w, v_norm_w, v_pool_in_w, v_pool_group_w, v_pool_group_b, v_pool_scale, v_pool_out_w, v_gla_in_w, v_gla_gk_w, v_gla_gk_b, v_gla_head_norm_w, v_gla_out_w, v_final_norm_w):
    s = x.shape[1]
    xs = x[0]
    target = loss_target[0]
    q_chip = 2 * lax.axis_index("x") + lax.axis_index("y")
    place = jnp.stack([lax.axis_index("c"), q_chip]).astype(jnp.int32)

    small_in = jnp.concatenate([
        _pad_row(gla_gk_b[0], gla_head_norm_w[0], pool_group_b[0]),
        gla_gk_w[0].reshape(2, D),
        jnp.zeros((5, D), F32)], axis=0)
    (wpi, gw_q, wpo_q, wgi_q, wgo_q), small_all = allgather_weights(
        [pool_in_w[0], pool_group_w[0].reshape(GROUP_DIM, GROUP_DIM), pool_out_w[0], gla_in_w[0], gla_out_w[0]],
        exchange=(True, True, True, False, False), small=small_in)
    gw = _groups_from_quarters(gw_q)
    wpo = wpo_q.reshape(D, D)
    small_all = small_all[0::2]
    bgk = small_all[:, 0, 0:128].reshape(1, KEY_W)
    hw = small_all[:, 0, 128:192].reshape(1, HEAD_V)
    gb = jnp.concatenate([small_all[q, 0, 192:448].reshape(GROUPS, 64) for q in range(N_CHIPS)],
                         axis=1).reshape(1, D)
    wgk16 = jnp.concatenate([small_all[q, 1:3].reshape(GATE_RANK, 128) for q in range(N_CHIPS)], axis=1)
    wgk = _bf(jnp.pad(wgk16, ((0, RANK_PAD - GATE_RANK), (0, 0))))
    hw_tiled = jnp.tile(hw, (1, HEADS))

    w0 = norm_w[0:1]
    w1 = norm_w[1:2]
    wf = final_norm_w.reshape(1, D)

    dx, reduced, total = local_gradients(
        xs, target, w0, w1, wf, wpi, gw, gb, pool_scale, wpo, [wgi_q, wgo_q], wgk, bgk, hw_tiled, place)
    r_pool_in, r_group_w, r_pool_out, r_gla_in, r_gla_out = reduced
    r_group_w = r_group_w.reshape(GROUPS, 64, GROUP_DIM)

    loss = total[4, 0]
    g_norm = total[0:2]
    g_scale = total[2:3]
    g_final = total[3]
    pick = lambda full, width: lax.dynamic_slice_in_dim(full, q_chip * width, width, axis=-1)
    g_gk_b = pick(total[5:6, 0:KEY_W], 128)
    g_hnw = pick(total[5:6, KEY_W:KEY_W + HEAD_V], 64)
    g_group_b = pick(total[6].reshape(GROUPS, GROUP_DIM), 64)[None]
    g_gk_w = pick(total[7:15].reshape(GATE_RANK, KEY_W), 128)[None]

    def step_lane_rows(name, w, g, m, v):
        turn = lambda a: jnp.transpose(a, (2, 0, 1))
        back = lambda a: jnp.transpose(a, (1, 2, 0))
        g_t = turn(g)
        d, nm, nv = adamw_rows(turn(w), g_t, turn(m), turn(v), "adamw_" + name)
        return back(g_t), back(d), back(nm), back(nv)

    def step(name, w, g, m, v):
        shape = w.shape
        as2d = lambda a: a.reshape(-1, shape[-1])
        d, nm, nv = adamw(as2d(w), as2d(g), as2d(m), as2d(v), "adamw_" + name)
        return g.reshape(shape), d.reshape(shape), nm.reshape(shape), nv.reshape(shape)

    small_names = ("norm_w", "pool_group_b", "pool_scale", "gla_gk_w", "gla_gk_b", "gla_head_norm_w",
                   "final_norm_w")
    small_args = [(norm_w, g_norm, m_norm_w, v_norm_w),
                  (pool_group_b, g_group_b, m_pool_group_b, v_pool_group_b),
                  (pool_scale, g_scale, m_pool_scale, v_pool_scale),
                  (gla_gk_w, g_gk_w, m_gla_gk_w, v_gla_gk_w),
                  (gla_gk_b, g_gk_b, m_gla_gk_b, v_gla_gk_b),
                  (gla_head_norm_w, g_hnw, m_gla_head_norm_w, v_gla_head_norm_w),
                  (final_norm_w, g_final, m_final_norm_w, v_final_norm_w)]
    as2d = lambda a, w: a.reshape(-1, w.shape[-1])
    small_out = adamw_small([tuple(as2d(a, p[0]) for a in p) for p in small_args])
    small = {n: (p[1].reshape(p[0].shape),) + tuple(o.reshape(p[0].shape) for o in out)
             for n, p, out in zip(small_names, small_args, small_out)}
    results = [
        small["norm_w"],
        step("pool_in_w", pool_in_w, r_pool_in[None], m_pool_in_w, v_pool_in_w),
        step("pool_group_w", pool_group_w, r_group_w[None], m_pool_group_w, v_pool_group_w),
        small["pool_group_b"],
        small["pool_scale"],
        step("pool_out_w", pool_out_w, r_pool_out[None], m_pool_out_w, v_pool_out_w),
        step_lane_rows("gla_in_w", gla_in_w, r_gla_in[None], m_gla_in_w, v_gla_in_w),
        small["gla_gk_w"],
        small["gla_gk_b"],
        small["gla_head_norm_w"],
        step("gla_out_w", gla_out_w, r_gla_out[None], m_gla_out_w, v_gla_out_w),
        small["final_norm_w"],
    ]
    grads, deltas, new_m, new_v = zip(*results)
    return (loss, dx[None], *grads, *deltas, *new_m, *new_v)
```

```python
import functools

import jax
import jax.numpy as jnp
from jax import lax
from jax.experimental import pallas as pl
from jax.experimental.pallas import tpu as pltpu

F32 = jnp.float32
BF16 = jnp.bfloat16
MESH = pl.DeviceIdType.MESH

D = 1024
POOL_WINDOWS = (2, 4, 8, 16)
GROUPS = 4
GROUP_DIM = 256
HEADS = 4
HEAD_K = 128
HEAD_V = 256
KEY_W = 512
CHUNK = 64
GATE_RANK = 16
GATE_NORM = 16.0
GLA_IN = 3088
GLA_MAIN = 3072
RANK_PAD = 128
EPS = 1e-6
HALO = 32

ADAM_LR = 0.001
ADAM_B1 = 0.9
ADAM_B2 = 0.999
ADAM_EPS = 1e-08
ADAM_WD = 0.01
ADAM_STEP = 10

N_CHIPS = 4
N_DEV = 8
GLA_IN_QUARTER = GLA_IN // N_CHIPS

VMEM_LIMIT = 56 * 1024 * 1024


def _nn(a, b):
    return lax.dot_general(a, b, (((1,), (0,)), ((), ())), preferred_element_type=F32)


def _nt(a, b):
    return lax.dot_general(a, b, (((1,), (1,)), ((), ())), preferred_element_type=F32)


def _tn(a, b):
    return lax.dot_general(a, b, (((0,), (0,)), ((), ())), preferred_element_type=F32)


def _nn_exact(a, b):
    return lax.dot_general(a, b, (((1,), (0,)), ((), ())), preferred_element_type=F32,
                           precision=lax.Precision.HIGHEST)


def _bf(a):
    return a.astype(BF16)


def _params(*sem):
    return pltpu.CompilerParams(dimension_semantics=sem, vmem_limit_bytes=VMEM_LIMIT)


def _full(shape):
    return pl.BlockSpec(shape, lambda i: (0,) * len(shape))


def _position():
    return lax.axis_index("x"), lax.axis_index("y"), lax.axis_index("c")


def _gather_small(in_ref, all_ref, send_sems, recv_sems, local_sem):
    x, y, c = _position()
    me = 4 * x + 2 * y + c
    mine = pltpu.make_async_copy(in_ref, all_ref.at[me], local_sem)
    mine.start()
    sends = []
    for k in range(N_DEV - 1):
        fx, fy, fc = (k + 1) >> 2 & 1, (k + 1) >> 1 & 1, (k + 1) & 1
        cp = pltpu.make_async_remote_copy(
            src_ref=in_ref, dst_ref=all_ref.at[me],
            send_sem=send_sems.at[k], recv_sem=recv_sems.at[k],
            device_id=(x ^ fx, y ^ fy, c ^ fc), device_id_type=MESH)
        cp.start()
        sends.append(cp)
    for k in range(N_DEV - 1):
        fx, fy, fc = (k + 1) >> 2 & 1, (k + 1) >> 1 & 1, (k + 1) & 1
        src_dev = 4 * (x ^ fx) + 2 * (y ^ fy) + (c ^ fc)
        pltpu.make_async_remote_copy(
            src_ref=in_ref, dst_ref=all_ref.at[src_dev],
            send_sem=send_sems.at[k], recv_sem=recv_sems.at[k],
            device_id=(x, y, c), device_id_type=MESH).wait_recv()
    for cp in sends:
        cp.wait_send()
    mine.wait()


SMALL_SEMS = [pltpu.SemaphoreType.DMA((N_DEV - 1,)), pltpu.SemaphoreType.DMA((N_DEV - 1,)),
              pltpu.SemaphoreType.DMA]
VMEM_SPEC = pl.BlockSpec(memory_space=pltpu.VMEM)


def _other_chips(x, y):
    return [(1 - x, y), (x, 1 - y), (1 - x, 1 - y)]


def _any_specs(n):
    return [pl.BlockSpec(memory_space=pl.ANY)] * n


def _halves(rows, c):
    half = rows // 2
    return pl.ds(c * half, half), pl.ds((1 - c) * half, half)


CAST_ROWS = 256


def _gather_copy(out_ref, send_sems, recv_sems, k, quarter, half, to, src=None):
    dst = out_ref.at[quarter, half]
    return pltpu.make_async_remote_copy(
        src_ref=dst if src is None else src, dst_ref=dst,
        send_sem=send_sems.at[k], recv_sem=recv_sems.at[k], device_id=to, device_id_type=MESH)


def allgather_weights(quarters, exchange, small):
    n = len(quarters)
    shapes = [w.shape for w in quarters]
    moved = [i for i in range(n) if exchange[i]]

    def body(*refs):
        w_refs, small_ref = refs[:n], refs[n]
        out_refs, small_all_ref = refs[n + 1:2 * n + 1], refs[2 * n + 1]
        refs = refs[2 * n + 2:]
        f32_bufs, bf_bufs = refs[:n], refs[n:2 * n]
        send_sems, recv_sems, local_sems = refs[2 * n:2 * n + 3]
        x, y, c = _position()
        q = 2 * x + y
        sibling = (x, y, 1 - c)
        chips = _other_chips(x, y)

        def copy(k, i, quarter, half, to, src=None):
            return _gather_copy(out_refs[i], send_sems, recv_sems, k * n + i, quarter, half, to, src)

        loads = [pltpu.make_async_copy(w_refs[i], f32_bufs[i], local_sems.at[i]) for i in range(n)]
        for cp in loads:
            cp.start()
        keeps, sends = [], []
        for i in range(n):
            loads[i].wait()
            for r0 in range(0, shapes[i][0], CAST_ROWS):
                bf_bufs[i][r0:r0 + CAST_ROWS, :] = _bf(f32_bufs[i][r0:r0 + CAST_ROWS, :])
            keep = pltpu.make_async_copy(bf_bufs[i], out_refs[i].at[q], local_sems.at[n + i])
            keep.start()
            keeps.append(keep)
            if not exchange[i]:
                continue
            mine, _ = _halves(shapes[i][0], c)
            for j, chip in enumerate(chips):
                cp = copy(j, i, q, mine, (*chip, c), src=bf_bufs[i].at[mine])
                cp.start()
                sends.append(cp)
        for j, chip in enumerate(chips):
            qj = 2 * chip[0] + chip[1]
            for i in moved:
                mine, _ = _halves(shapes[i][0], c)
                copy(j, i, qj, mine, (x, y, c)).wait_recv()
                cp = copy(3 + j, i, qj, mine, sibling)
                cp.start()
                sends.append(cp)
        for j, chip in enumerate(chips):
            qj = 2 * chip[0] + chip[1]
            for i in moved:
                _, other = _halves(shapes[i][0], c)
                copy(3 + j, i, qj, other, (x, y, c)).wait_recv()
        _gather_small(small_ref, small_all_ref, *refs[2 * n + 3:])
        for cp in sends:
            cp.wait_send()
        for cp in keeps:
            cp.wait()

    outs = pl.pallas_call(
        body, name="allgather_weights",
        out_shape=[jax.ShapeDtypeStruct((N_CHIPS, *s), BF16) for s in shapes]
                  + [jax.ShapeDtypeStruct((N_DEV, *small.shape), small.dtype)],
        in_specs=_any_specs(n) + [VMEM_SPEC], out_specs=_any_specs(n) + [VMEM_SPEC],
        scratch_shapes=([pltpu.VMEM(s, F32) for s in shapes] + [pltpu.VMEM(s, BF16) for s in shapes]
                        + [pltpu.SemaphoreType.DMA((6 * n,)), pltpu.SemaphoreType.DMA((6 * n,)),
                           pltpu.SemaphoreType.DMA((2 * n,))] + SMALL_SEMS),
        compiler_params=pltpu.CompilerParams(vmem_limit_bytes=VMEM_LIMIT),
    )(*quarters, small)
    return outs[:n], outs[n]


def exchange_with_sibling(grads, name):
    n = len(grads)

    def body(*refs):
        g_refs, theirs_refs = refs[:n], refs[n:2 * n]
        send_sems, recv_sems = refs[2 * n:]
        x, y, c = _position()
        copies = []
        for i in range(n):
            _, other = _halves(g_refs[i].shape[1], c)
            cp = pltpu.make_async_remote_copy(
                src_ref=g_refs[i].at[:, other], dst_ref=theirs_refs[i],
                send_sem=send_sems.at[i], recv_sem=recv_sems.at[i],
                device_id=(x, y, 1 - c), device_id_type=MESH)
            cp.start()
            copies.append(cp)
        for cp in copies:
            cp.wait()

    return pl.pallas_call(
        body, name=name,
        out_shape=[jax.ShapeDtypeStruct((N_CHIPS, g.shape[1] // 2, g.shape[2]), F32) for g in grads],
        in_specs=_any_specs(n), out_specs=_any_specs(n),
        scratch_shapes=[pltpu.SemaphoreType.DMA((n,)), pltpu.SemaphoreType.DMA((n,))],
    )(*grads)


def _scatter_copies(b_refs, got_refs, send_sems, recv_sems):
    n = len(b_refs)
    x, y, c = _position()
    copies = []
    for j, chip in enumerate(_other_chips(x, y)):
        qj = 2 * chip[0] + chip[1]
        for i in range(n):
            copies.append(pltpu.make_async_remote_copy(
                src_ref=b_refs[i].at[qj], dst_ref=got_refs[i].at[j],
                send_sem=send_sems.at[j * n + i], recv_sem=recv_sems.at[j * n + i],
                device_id=(*chip, c), device_id_type=MESH))
    return copies


def _scatter_shapes(chip_sums):
    return [jax.ShapeDtypeStruct((N_CHIPS - 1, *b.shape[1:]), BF16) for b in chip_sums]


def scatter_to_owners(chip_sums, name):
    n = len(chip_sums)

    def body(*refs):
        copies = _scatter_copies(refs[:n], refs[n:2 * n], *refs[2 * n:])
        for cp in copies:
            cp.start()
        for cp in copies:
            cp.wait()

    return pl.pallas_call(
        body, name=name,
        out_shape=_scatter_shapes(chip_sums),
        in_specs=_any_specs(n), out_specs=_any_specs(n),
        scratch_shapes=[pltpu.SemaphoreType.DMA((3 * n,)), pltpu.SemaphoreType.DMA((3 * n,))],
    )(*chip_sums)


def join_halves(reduced, small):
    n = len(reduced)

    def body(*refs):
        small_ref = refs[n]
        buf_refs, total_ref = refs[n + 1:2 * n + 1], refs[2 * n + 1]
        send_sems, recv_sems, all_ref = refs[2 * n + 2:2 * n + 5]
        x, y, c = _position()
        copies = []
        for i in range(n):
            mine, _ = _halves(buf_refs[i].shape[0], c)
            cp = pltpu.make_async_remote_copy(
                src_ref=buf_refs[i].at[mine], dst_ref=buf_refs[i].at[mine],
                send_sem=send_sems.at[i], recv_sem=recv_sems.at[i],
                device_id=(x, y, 1 - c), device_id_type=MESH)
            cp.start()
            copies.append(cp)
        _gather_small(small_ref, all_ref, *refs[2 * n + 5:])
        total = all_ref[0]
        for dev in range(1, N_DEV):
            total = total + all_ref[dev]
        total_ref[...] = total
        for cp in copies:
            cp.wait()

    outs = pl.pallas_call(
        body, name="join_halves",
        out_shape=[jax.ShapeDtypeStruct(r.shape, F32) for r in reduced]
                  + [jax.ShapeDtypeStruct(small.shape, small.dtype)],
        in_specs=_any_specs(n) + [VMEM_SPEC], out_specs=_any_specs(n) + [VMEM_SPEC],
        input_output_aliases={i: i for i in range(n)},
        scratch_shapes=[pltpu.SemaphoreType.DMA((n,)), pltpu.SemaphoreType.DMA((n,)),
                        pltpu.VMEM((N_DEV, *small.shape), small.dtype)] + SMALL_SEMS,
    )(*reduced, small)
    return outs[:n], outs[n]


ADD_ROWS = 512


def add_halves(grad, theirs, place, name):
    _, half, cols = theirs.shape
    rb = min(ADD_ROWS, half)
    steps = half // rb

    def body(place_ref, a_ref, b_ref, f_ref, h_ref):
        s = a_ref[0] + b_ref[0]
        h_ref[0] = _bf(s)

        @pl.when(pl.program_id(1) == place_ref[1])
        def _():
            f_ref[...] = s

    spec = pl.BlockSpec((1, rb, cols), lambda j, i, place: (i, j, 0))
    return pl.pallas_call(
        body, name=name,
        grid_spec=pltpu.PrefetchScalarGridSpec(
            num_scalar_prefetch=1, grid=(steps, N_CHIPS),
            in_specs=[pl.BlockSpec((1, rb, cols), lambda j, i, place: (i, place[0] * steps + j, 0)), spec],
            out_specs=(pl.BlockSpec((rb, cols), lambda j, i, place: (j, 0)), spec)),
        out_shape=(jax.ShapeDtypeStruct((half, cols), F32), jax.ShapeDtypeStruct(theirs.shape, BF16)),
        compiler_params=_params("parallel", "arbitrary"),
    )(place, grad, theirs)


def add_parts(own, got, place, name):
    _, half, cols = got.shape
    rb = min(ADD_ROWS, half)
    steps = half // rb

    def body(place_ref, o_ref, g_ref, out_ref):
        s = o_ref[...]
        for j in range(N_CHIPS - 1):
            s = s + g_ref[j].astype(F32)
        out_ref[...] = s

    return pl.pallas_call(
        body, name=name,
        grid_spec=pltpu.PrefetchScalarGridSpec(
            num_scalar_prefetch=1, grid=(steps,),
            in_specs=[pl.BlockSpec((rb, cols), lambda j, place: (j, 0)),
                      pl.BlockSpec((N_CHIPS - 1, rb, cols), lambda j, place: (0, j, 0))],
            out_specs=pl.BlockSpec((rb, cols), lambda j, place: (place[0] * steps + j, 0))),
        out_shape=jax.ShapeDtypeStruct((2 * half, cols), F32),
        compiler_params=_params("parallel"),
    )(place, own, got)


def _adam_math(w, g, m, v):
    m = ADAM_B1 * m + (1.0 - ADAM_B1) * g
    v = ADAM_B2 * v + (1.0 - ADAM_B2) * (g * g)
    m_hat = m / (1.0 - ADAM_B1 ** ADAM_STEP)
    v_hat = v / (1.0 - ADAM_B2 ** ADAM_STEP)
    delta = -ADAM_LR * (m_hat / (jnp.sqrt(v_hat) + ADAM_EPS) + ADAM_WD * w)
    return delta, m, v


def adamw(w, g, m, v, name):
    rows, cols = w.shape
    fits = [t for t in range(8, rows, 8) if rows % t == 0 and t * cols * 4 <= 2 ** 20]
    tile = max(fits) if fits else rows

    def body(w_ref, g_ref, m_ref, v_ref, d_ref, nm_ref, nv_ref):
        d, nm, nv = _adam_math(w_ref[...], g_ref[...], m_ref[...], v_ref[...])
        d_ref[...] = d
        nm_ref[...] = nm
        nv_ref[...] = nv

    spec = pl.BlockSpec((tile, cols), lambda i: (i, 0))
    shape = jax.ShapeDtypeStruct((rows, cols), F32)
    return pl.pallas_call(
        body, name=name, grid=(rows // tile,),
        out_shape=(shape, shape, shape),
        in_specs=[spec] * 4, out_specs=(spec, spec, spec),
        compiler_params=_params("parallel"),
    )(w, g, m, v)


def adamw_small(params):
    n = len(params)

    def body(*refs):
        ins, outs = refs[:4 * n], refs[4 * n:]
        for k in range(n):
            w_ref, g_ref, m_ref, v_ref = ins[4 * k:4 * k + 4]
            d, nm, nv = _adam_math(w_ref[...], g_ref[...], m_ref[...], v_ref[...])
            outs[3 * k][...] = d
            outs[3 * k + 1][...] = nm
            outs[3 * k + 2][...] = nv

    flat = [a for p in params for a in p]
    outs = pl.pallas_call(
        body, name="adamw_small",
        out_shape=[jax.ShapeDtypeStruct(p[0].shape, F32) for p in params for _ in range(3)],
        in_specs=[VMEM_SPEC] * (4 * n), out_specs=[VMEM_SPEC] * (3 * n),
    )(*flat)
    return [tuple(outs[3 * k:3 * k + 3]) for k in range(n)]


def adamw_rows(w, g, m, v, name):
    rows, _, cols = w.shape
    tile = rows // 4

    def body(w_ref, g_ref, m_ref, v_ref, d_ref, nm_ref, nv_ref):
        d, nm, nv = _adam_math(w_ref[...], g_ref[...], m_ref[...], v_ref[...])
        d_ref[...] = d
        nm_ref[...] = nm
        nv_ref[...] = nv

    spec = pl.BlockSpec((tile, 1, cols), lambda i: (i, 0, 0))
    shape = jax.ShapeDtypeStruct(w.shape, F32)
    return pl.pallas_call(
        body, name=name, grid=(rows // tile,),
        out_shape=(shape, shape, shape),
        in_specs=[spec] * 4, out_specs=(spec, spec, spec),
        compiler_params=_params("parallel"),
    )(w, g, m, v)


def matmul_tn(a, b, name, tile_n=512, by_column_tile=False):
    s, m = a.shape
    n = b.shape[1]
    tile_n = min(tile_n, n)
    if by_column_tile:
        out_shape = jax.ShapeDtypeStruct((n // tile_n, m, tile_n), F32)
        out_spec = pl.BlockSpec((None, m, tile_n), lambda j: (j, 0, 0))
    else:
        out_shape = jax.ShapeDtypeStruct((m, n), F32)
        out_spec = pl.BlockSpec((m, tile_n), lambda j: (0, j))

    def body(a_ref, b_ref, out_ref):
        out_ref[...] = _tn(a_ref[...], b_ref[...])

    return pl.pallas_call(
        body, name=name, grid=(n // tile_n,),
        out_shape=out_shape,
        in_specs=[_full((s, m)), pl.BlockSpec((s, tile_n), lambda j: (0, j))],
        out_specs=out_spec,
        compiler_params=_params("parallel"),
    )(a, b)


ROW_TILE = 512


def _row_index(tile, rows):
    return tile * rows + lax.broadcasted_iota(jnp.int32, (rows, 1), 0)


def _inverse_counts(t_glob):
    return [1.0 / jnp.minimum(t_glob + 1, w).astype(F32) for w in POOL_WINDOWS]


def _sigmoid(z):
    return 1.0 / (1.0 + jnp.exp(-z))


def _trailing_sums(src, tmp, cols, window, rows):
    bufs = (src, tmp)
    span, level, start = 1, 0, 0
    while span < window:
        start += 8
        a, b = bufs[level % 2], bufs[(level + 1) % 2]
        n = HALO + rows - start
        b[start:start + n, cols] = a[start:start + n, cols] + a[start - span:start - span + n, cols]
        span, level = 2 * span, level + 1
    return bufs[level % 2][HALO:HALO + rows, cols]


def _leading_sums(src, tmp, cols, window, rows):
    bufs = (src, tmp)
    span, level, n = 1, 0, rows + HALO
    while span < window:
        n -= 8
        a, b = bufs[level % 2], bufs[(level + 1) % 2]
        b[0:n, cols] = a[0:n, cols] + a[span:span + n, cols]
        span, level = 2 * span, level + 1
    return bufs[level % 2][0:rows, cols]


def gather_in_background(step, last, out_refs, send_sems, recv_sems, finish):
    n = len(out_refs)
    x, y, c = _position()
    q = 2 * x + y
    chips = _other_chips(x, y)

    def copy(k, i, quarter, half, to):
        return _gather_copy(out_refs[i], send_sems, recv_sems, k * n + i, quarter, half, to)

    if not finish:
        @pl.when(step == 0)
        def _():
            for i in range(n):
                mine, _ = _halves(out_refs[i].shape[1], c)
                for j, chip in enumerate(chips):
                    copy(j, i, q, mine, (*chip, c)).start()

        @pl.when(step == last)
        def _():
            for j, chip in enumerate(chips):
                qj = 2 * chip[0] + chip[1]
                for i in range(n):
                    mine, _ = _halves(out_refs[i].shape[1], c)
                    copy(j, i, qj, mine, (x, y, c)).wait_recv()
                    copy(3 + j, i, qj, mine, (x, y, 1 - c)).start()
        return

    @pl.when(step == last)
    def _():
        for j, chip in enumerate(chips):
            qj = 2 * chip[0] + chip[1]
            for i in range(n):
                mine, other = _halves(out_refs[i].shape[1], c)
                copy(3 + j, i, qj, other, (x, y, c)).wait_recv()
                copy(j, i, q, mine, (x, y, c)).wait_send()
                copy(3 + j, i, qj, mine, (x, y, c)).wait_send()


def pool_forward(x, w0, wpi, gw, gb, scale, wpo, later):
    s = x.shape[0]
    ts = ROW_TILE
    nt = s // ts
    assert nt >= 2
    n_later = len(later)

    def body(x_ref, w0_ref, wpi_ref, gw_ref, gb_ref, sc_ref, wpo_ref, *rest):
        rest = rest[n_later:]
        h1_ref, pooled_ref, gt_ref, n0_ref = rest[:4]
        later_refs = rest[4:4 + n_later]
        ubuf, tbuf, hist, send_sems, recv_sems = rest[4 + n_later:]
        i = pl.program_id(0)
        gather_in_background(i, nt - 1, later_refs, send_sems, recv_sems, finish=False)
        xv = x_ref[...]
        r = lax.rsqrt(jnp.mean(xv * xv, axis=-1, keepdims=True) + EPS)
        n0 = _bf(xv * r * w0_ref[...])
        n0_ref[...] = n0
        u = jnp.concatenate([_nn(n0, wpi_ref[0]), _nn(n0, wpi_ref[1])], axis=-1)
        gt = jnp.concatenate([_nn(n0, wpi_ref[2]), _nn(n0, wpi_ref[3])], axis=-1)
        gt_ref[...] = gt

        @pl.when(i == 0)
        def _():
            hist[...] = jnp.zeros_like(hist)

        ubuf[0:HALO, :] = hist[...]
        ubuf[HALO:HALO + ts, :] = u
        hist[...] = u[ts - HALO:, :]
        inv = _inverse_counts(_row_index(i, ts))
        mixed = []
        for g, w in enumerate(POOL_WINDOWS):
            cols = slice(g * GROUP_DIM, (g + 1) * GROUP_DIM)
            pooled = _bf(_trailing_sums(ubuf, tbuf, cols, w, ts) * inv[g] - u[:, cols])
            pooled_ref[:, cols] = pooled
            mixed.append(_nn(pooled, gw_ref[g]))
        mixed = jnp.concatenate(mixed, axis=-1) + gb_ref[...]
        y = mixed * sc_ref[...] * (gt * _sigmoid(gt))
        h1_ref[...] = xv + _nn(_bf(y), wpo_ref[...])
        gather_in_background(i, nt - 1, later_refs, send_sems, recv_sems, finish=True)

    row = lambda cols: pl.BlockSpec((ts, cols), lambda i: (i, 0))
    outs = pl.pallas_call(
        body, name="pool_forward", grid=(nt,),
        out_shape=[jax.ShapeDtypeStruct((s, D), F32), jax.ShapeDtypeStruct((s, D), BF16),
                   jax.ShapeDtypeStruct((s, D), F32), jax.ShapeDtypeStruct((s, D), BF16)]
                  + [jax.ShapeDtypeStruct(a.shape, a.dtype) for a in later],
        in_specs=[row(D), _full((1, D)), _full((N_CHIPS, D, D // 2)), _full((GROUPS, GROUP_DIM, GROUP_DIM)),
                  _full((1, D)), _full((1, D)), _full((D, D))] + _any_specs(n_later),
        out_specs=[row(D), row(D), row(D), row(D)] + _any_specs(n_later),
        input_output_aliases={7 + k: 4 + k for k in range(n_later)},
        scratch_shapes=[pltpu.VMEM((HALO + ts, D), F32), pltpu.VMEM((HALO + ts, D), F32),
                        pltpu.VMEM((HALO, D), F32),
                        pltpu.SemaphoreType.DMA((6 * n_later,)), pltpu.SemaphoreType.DMA((6 * n_later,))],
        compiler_params=_params("arbitrary"),
    )(x, w0, wpi, gw, gb, scale, wpo, *later)
    return outs[:4], outs[4:]


def pool_backward(x, dh1, pooled, gt, w0, wpi, gw, gb, scale, wpo, chip_sums):
    s = x.shape[0]
    ts = ROW_TILE
    nt = s // ts
    n_sums = len(chip_sums)

    def body(x_ref, dh1_ref, pooled_ref, gt_ref, w0_ref, wpi_ref, gw_ref, gb_ref, sc_ref, wpo_ref, *rest):
        sum_refs, rest = rest[:n_sums], rest[n_sums:]
        dx_ref, dproj_ref, gpo_ref, ggw_ref, small_ref = rest[:5]
        got_refs = rest[5:5 + n_sums]
        ebuf, tbuf, ahead, send_sems, recv_sems = rest[5 + n_sums:]
        i = pl.program_id(0)
        copies = _scatter_copies(sum_refs, got_refs, send_sems, recv_sems)

        @pl.when(i == 0)
        def _():
            for cp in copies:
                cp.start()

        @pl.when(i == 0)
        def _():
            gpo_ref[...] = jnp.zeros_like(gpo_ref)
            ggw_ref[...] = jnp.zeros_like(ggw_ref)
            small_ref[...] = jnp.zeros_like(small_ref)
            ahead[...] = jnp.zeros_like(ahead)

        dh1 = dh1_ref[...]
        dh1_bf = _bf(dh1)
        gt = gt_ref[...]
        sc = sc_ref[...]
        dy = _nt(dh1_bf, wpo_ref[...])
        pooled_bf = []
        mixed = []
        for g in range(GROUPS):
            cols = slice(g * GROUP_DIM, (g + 1) * GROUP_DIM)
            pb = pooled_ref[:, cols]
            pooled_bf.append(pb)
            mixed.append(_nn(pb, gw_ref[g]))
        mixed = jnp.concatenate(mixed, axis=-1) + gb_ref[...]
        sg = _sigmoid(gt)
        silu = gt * sg
        gpo_ref[...] += _tn(_bf(mixed * sc * silu), dh1_bf)
        dmixed = dy * sc * silu
        dgt = dy * mixed * sc * (sg * (1.0 + gt * (1.0 - sg)))
        dproj_ref[:, D:] = _bf(dgt)
        small_ref[1:2, :] += jnp.sum(dy * mixed * silu, axis=0, keepdims=True)
        small_ref[2:3, :] += jnp.sum(dmixed, axis=0, keepdims=True)

        inv = _inverse_counts(_row_index(nt - 1 - i, ts))
        ebuf[ts:ts + HALO, :] = ahead[...]
        dpooled = []
        for g in range(GROUPS):
            cols = slice(g * GROUP_DIM, (g + 1) * GROUP_DIM)
            dm = _bf(dmixed[:, cols])
            ggw_ref[g] += _tn(pooled_bf[g], dm)
            dp = _nt(dm, gw_ref[g])
            dpooled.append(dp)
            ebuf[0:ts, cols] = dp * inv[g]
        ahead[...] = ebuf[0:HALO, :]
        du = []
        for g, w in enumerate(POOL_WINDOWS):
            cols = slice(g * GROUP_DIM, (g + 1) * GROUP_DIM)
            du.append(_leading_sums(ebuf, tbuf, cols, w, ts) - dpooled[g])
        du = _bf(jnp.concatenate(du, axis=-1))
        dproj_ref[:, :D] = du
        dgt_bf = _bf(dgt)
        half = D // 2
        dn0 = (_nt(du[:, :half], wpi_ref[0]) + _nt(du[:, half:], wpi_ref[1])
               + _nt(dgt_bf[:, :half], wpi_ref[2]) + _nt(dgt_bf[:, half:], wpi_ref[3]))

        xv = x_ref[...]
        r = lax.rsqrt(jnp.mean(xv * xv, axis=-1, keepdims=True) + EPS)
        xhat = xv * r
        small_ref[0:1, :] += jnp.sum(dn0 * xhat, axis=0, keepdims=True)
        dxh = dn0 * w0_ref[...]
        dx_ref[...] = dh1 + r * (dxh - xhat * jnp.mean(dxh * xhat, axis=-1, keepdims=True))

        @pl.when(i == nt - 1)
        def _():
            for cp in copies:
                cp.wait()

    row = lambda cols: pl.BlockSpec((ts, cols), lambda i: (nt - 1 - i, 0))
    outs = pl.pallas_call(
        body, name="pool_backward", grid=(nt,),
        out_shape=[jax.ShapeDtypeStruct((s, D), F32), jax.ShapeDtypeStruct((s, 2 * D), BF16),
                   jax.ShapeDtypeStruct((D, D), F32),
                   jax.ShapeDtypeStruct((GROUPS, GROUP_DIM, GROUP_DIM), F32),
                   jax.ShapeDtypeStruct((8, D), F32)] + _scatter_shapes(chip_sums),
        in_specs=[row(D), row(D), row(D), row(D), _full((1, D)), _full((N_CHIPS, D, D // 2)),
                  _full((GROUPS, GROUP_DIM, GROUP_DIM)), _full((1, D)), _full((1, D)), _full((D, D))]
                 + _any_specs(n_sums),
        out_specs=[row(D), row(2 * D), _full((D, D)), _full((GROUPS, GROUP_DIM, GROUP_DIM)), _full((8, D))]
                  + _any_specs(n_sums),
        scratch_shapes=[pltpu.VMEM((ts + HALO, D), F32), pltpu.VMEM((ts + HALO, D), F32),
                        pltpu.VMEM((HALO, D), F32),
                        pltpu.SemaphoreType.DMA((3 * n_sums,)), pltpu.SemaphoreType.DMA((3 * n_sums,))],
        compiler_params=_params("arbitrary"),
    )(x, dh1, pooled, gt, w0, wpi, gw, gb, scale, wpo, *chip_sums)
    return outs[:5], outs[5:]


def gla_project(h1, w1, wgi, wlow, wgk, bgk, later):
    s = h1.shape[0]
    ts = ROW_TILE
    nt = s // ts
    assert nt >= 2
    n_later = len(later)

    def body(h_ref, w1_ref, wgi_ref, wlow_ref, wgk_ref, bgk_ref, *rest):
        rest = rest[n_later:]
        qk_ref, v_ref, gate_ref, low_ref, cum_ref, n1_ref = rest[:6]
        later_refs = rest[6:6 + n_later]
        send_sems, recv_sems = rest[6 + n_later:]
        gather_in_background(pl.program_id(0), nt - 1, later_refs, send_sems, recv_sems, finish=False)
        hv = h_ref[...]
        r = lax.rsqrt(jnp.mean(hv * hv, axis=-1, keepdims=True) + EPS)
        n1 = _bf(hv * r * w1_ref[...])
        n1_ref[...] = n1
        qk_ref[...] = _nn(n1, wgi_ref[:, 0:2 * KEY_W])
        v_ref[...] = _bf(_nn(n1, wgi_ref[:, 2 * KEY_W:2 * KEY_W + D]))
        gate_ref[...] = _nn(n1, wgi_ref[:, 2 * KEY_W + D:GLA_MAIN])
        low = _bf(_nn(n1, wlow_ref[...]))
        low_ref[...] = low
        z = _nn(low, wgk_ref[...]) + bgk_ref[...]
        lg = (jnp.minimum(z, 0.0) - jnp.log(1.0 + jnp.exp(-jnp.abs(z)))) / GATE_NORM
        lower_f = _chunk_masks()[0].astype(F32)
        for r0 in range(0, ts, CHUNK):
            cum_ref[r0:r0 + CHUNK, :] = _nn_exact(lower_f, lg[r0:r0 + CHUNK, :])
        gather_in_background(pl.program_id(0), nt - 1, later_refs, send_sems, recv_sems, finish=True)

    row = lambda cols: pl.BlockSpec((ts, cols), lambda i: (i, 0))
    outs = pl.pallas_call(
        body, name="gla_project", grid=(nt,),
        out_shape=[jax.ShapeDtypeStruct((s, D), F32), jax.ShapeDtypeStruct((s, D), BF16),
                   jax.ShapeDtypeStruct((s, D), F32), jax.ShapeDtypeStruct((s, RANK_PAD), BF16),
                   jax.ShapeDtypeStruct((s, KEY_W), F32), jax.ShapeDtypeStruct((s, D), BF16)]
                  + [jax.ShapeDtypeStruct(a.shape, a.dtype) for a in later],
        in_specs=[row(D), _full((1, D)), _full((D, GLA_MAIN)), _full((D, RANK_PAD)),
                  _full((RANK_PAD, KEY_W)), _full((1, KEY_W))] + _any_specs(n_later),
        out_specs=[row(D), row(D), row(D), row(RANK_PAD), row(KEY_W), row(D)] + _any_specs(n_later),
        input_output_aliases={6 + k: 6 + k for k in range(n_later)},
        scratch_shapes=[pltpu.SemaphoreType.DMA((6 * n_later,)), pltpu.SemaphoreType.DMA((6 * n_later,))],
        compiler_params=_params("arbitrary"),
    )(h1, w1, wgi, wlow, wgk, bgk, *later)
    return outs[:6], outs[6:]


GLA_BLOCK = 512
CHUNKS_PER_BLOCK = GLA_BLOCK // CHUNK


def _chunk_masks():
    t = lax.broadcasted_iota(jnp.int32, (CHUNK, CHUNK), 0)
    u = lax.broadcasted_iota(jnp.int32, (CHUNK, CHUNK), 1)
    return t >= u, t <= u


def _gla_chunk_terms(q, cum):
    ep = jnp.exp(cum)
    en = jnp.exp(-cum)
    qs = q * (HEAD_K ** -0.5)
    last = cum[CHUNK - 1:CHUNK, :]
    ed = jnp.exp(last - cum)
    dec = jnp.exp(last)
    return ep, en, qs, ed, dec


def gla_forward(qk, v, cum):
    s = qk.shape[0]
    nb = s // GLA_BLOCK
    nc = s // CHUNK

    def body(q_ref, k_ref, v_ref, cum_ref, o_ref, st_ref, sc_ref, state):
        @pl.when(pl.program_id(0) == 0)
        def _():
            state[...] = jnp.zeros_like(state)

        lower, _ = _chunk_masks()

        def chunk(cc, carry):
            rows = pl.ds(pl.multiple_of(cc * CHUNK, CHUNK), CHUNK)
            for h in range(HEADS):
                kc = slice(h * HEAD_K, (h + 1) * HEAD_K)
                vc = slice(h * HEAD_V, (h + 1) * HEAD_V)
                q = q_ref[rows, kc]
                k = k_ref[rows, kc]
                v = v_ref[rows, vc]
                ep, en, qs, ed, dec = _gla_chunk_terms(q, cum_ref[rows, kc])
                a = _bf(qs * ep)
                fwd = _nt(a, _bf(k * en))
                bwd = _nt(_bf(qs * en), _bf(k * ep))
                scores = _bf(jnp.where(lower, fwd, bwd))
                sc_ref[rows, h * CHUNK:(h + 1) * CHUNK] = scores
                st = state[h]
                st_ref[cc, h] = st
                o_ref[rows, vc] = _nn(scores, v) + _nt(a, _bf(st))
                state[h] = st * dec + _tn(v, _bf(k * ed))
            return carry

        lax.fori_loop(0, CHUNKS_PER_BLOCK, chunk, 0, unroll=4)

    return pl.pallas_call(
        body, name="gla_forward", grid=(nb,),
        out_shape=(jax.ShapeDtypeStruct((s, D), F32),
                   jax.ShapeDtypeStruct((nc, HEADS, HEAD_V, HEAD_K), F32),
                   jax.ShapeDtypeStruct((s, HEADS * CHUNK), BF16)),
        in_specs=[pl.BlockSpec((GLA_BLOCK, KEY_W), lambda i: (i, 0)),
                  pl.BlockSpec((GLA_BLOCK, KEY_W), lambda i: (i, 1)),
                  pl.BlockSpec((GLA_BLOCK, D), lambda i: (i, 0)),
                  pl.BlockSpec((GLA_BLOCK, KEY_W), lambda i: (i, 0))],
        out_specs=(pl.BlockSpec((GLA_BLOCK, D), lambda i: (i, 0)),
                   pl.BlockSpec((CHUNKS_PER_BLOCK, HEADS, HEAD_V, HEAD_K), lambda i: (i, 0, 0, 0)),
                   pl.BlockSpec((GLA_BLOCK, HEADS * CHUNK), lambda i: (i, 0))),
        scratch_shapes=[pltpu.VMEM((HEADS, HEAD_V, HEAD_K), F32)],
        compiler_params=_params("arbitrary"),
    )(qk, qk, v, cum)


def gla_backward(qk, v, cum, do, states, scores):
    s = qk.shape[0]
    nb = s // GLA_BLOCK

    def body(q_ref, k_ref, v_ref, cum_ref, do_ref, st_ref, sc_ref, dq_ref, dk_ref, dv_ref, dcum_ref, dstate):
        @pl.when(pl.program_id(0) == 0)
        def _():
            dstate[...] = jnp.zeros_like(dstate)

        lower, _ = _chunk_masks()
        is_last = lax.broadcasted_iota(jnp.int32, (CHUNK, HEAD_K), 0) == CHUNK - 1

        def chunk(step, carry):
            cc = CHUNKS_PER_BLOCK - 1 - step
            rows = pl.ds(pl.multiple_of(cc * CHUNK, CHUNK), CHUNK)
            for h in range(HEADS):
                kc = slice(h * HEAD_K, (h + 1) * HEAD_K)
                vc = slice(h * HEAD_V, (h + 1) * HEAD_V)
                q = q_ref[rows, kc]
                k = k_ref[rows, kc]
                v = v_ref[rows, vc]
                do_c = do_ref[rows, vc]
                ep, en, qs, ed, dec = _gla_chunk_terms(q, cum_ref[rows, kc])
                a = _bf(qs * ep)
                b = _bf(k * en)
                c = _bf(qs * en)
                dk_dec = _bf(k * ep)
                kd = _bf(k * ed)
                scores = sc_ref[rows, h * CHUNK:(h + 1) * CHUNK]
                st = st_ref[cc, h]
                dst = dstate[h]
                dst_bf = _bf(dst)

                dscores = _nt(do_c, v)
                dfwd = _bf(jnp.where(lower, dscores, 0.0))
                dbwd = _bf(jnp.where(lower, 0.0, dscores))
                dv_ref[rows, vc] = _bf(_tn(scores, do_c) + _nt(kd, dst_bf))
                da = _nn(dfwd, b) + _nn(do_c, _bf(st))
                db = _tn(dfwd, a)
                dc = _nn(dbwd, dk_dec)
                ddk = _tn(dbwd, c)
                dkd = _nn(v, dst_bf)
                ddec = jnp.sum(dst * st, axis=0, keepdims=True)
                dstate[h] = dst * dec + _tn(do_c, a)

                m = dkd * k * ed
                dq_ref[rows, kc] = _bf((da * ep + dc * en) * (HEAD_K ** -0.5))
                dk_ref[rows, kc] = _bf(db * en + ddk * ep + dkd * ed)
                dcum = (da * qs + ddk * k) * ep - (db * k + dc * qs) * en - m
                dlast = jnp.sum(m, axis=0, keepdims=True) + ddec * dec
                dcum_ref[rows, kc] = dcum + jnp.where(is_last, dlast, 0.0)
            return carry

        lax.fori_loop(0, CHUNKS_PER_BLOCK, chunk, 0, unroll=4)

    rev = lambda cols, col_block: pl.BlockSpec((GLA_BLOCK, cols), lambda i: (nb - 1 - i, col_block))
    return pl.pallas_call(
        body, name="gla_backward", grid=(nb,),
        out_shape=(jax.ShapeDtypeStruct((s, KEY_W), BF16), jax.ShapeDtypeStruct((s, KEY_W), BF16),
                   jax.ShapeDtypeStruct((s, D), BF16), jax.ShapeDtypeStruct((s, KEY_W), F32)),
        in_specs=[rev(KEY_W, 0), rev(KEY_W, 1), rev(D, 0), rev(KEY_W, 0), rev(D, 0),
                  pl.BlockSpec((CHUNKS_PER_BLOCK, HEADS, HEAD_V, HEAD_K), lambda i: (nb - 1 - i, 0, 0, 0)),
                  rev(HEADS * CHUNK, 0)],
        out_specs=(rev(KEY_W, 0), rev(KEY_W, 0), rev(D, 0), rev(KEY_W, 0)),
        scratch_shapes=[pltpu.VMEM((HEADS, HEAD_V, HEAD_K), F32)],
        compiler_params=_params("arbitrary"),
    )(qk, qk, v, cum, do, states, scores)


def head_and_loss(o, gate, h1, target, hw, wgo, wf):
    s = o.shape[0]
    ts = ROW_TILE

    def body(o_ref, gate_ref, h1_ref, tgt_ref, hw_ref, wgo_ref, wf_ref,
             dh2_ref, do_ref, dgate_ref, ggo_ref, small_ref):
        @pl.when(pl.program_id(0) == 0)
        def _():
            ggo_ref[...] = jnp.zeros_like(ggo_ref)
            small_ref[...] = jnp.zeros_like(small_ref)

        gate = gate_ref[...]
        hw = hw_ref[...]
        sg = _sigmoid(gate)
        silu = gate * sg
        ohat, ro = [], []
        for h in range(HEADS):
            oh = o_ref[:, h * HEAD_V:(h + 1) * HEAD_V]
            rh = lax.rsqrt(jnp.mean(oh * oh, axis=-1, keepdims=True) + EPS)
            ro.append(rh)
            ohat.append(oh * rh)
        ohat = jnp.concatenate(ohat, axis=-1)
        on = ohat * hw
        y2 = _bf(on * silu)
        h2 = h1_ref[...] + _nn(y2, wgo_ref[...])
        rf = lax.rsqrt(jnp.mean(h2 * h2, axis=-1, keepdims=True) + EPS)
        h2hat = h2 * rf
        wf = wf_ref[...]
        diff = h2hat * wf - tgt_ref[...]
        small_ref[2:3, :] += jnp.zeros((1, D), F32) + 0.5 * jnp.sum(diff * diff) / D
        dout = diff / D
        small_ref[0:1, :] += jnp.sum(dout * h2hat, axis=0, keepdims=True)
        dxh = dout * wf
        dh2 = rf * (dxh - h2hat * jnp.mean(dxh * h2hat, axis=-1, keepdims=True))
        dh2_ref[...] = dh2
        dh2_bf = _bf(dh2)
        ggo_ref[...] += _tn(y2, dh2_bf)
        dy2 = _nt(dh2_bf, wgo_ref[...])
        don = dy2 * silu
        dgate_ref[...] = _bf(dy2 * on * (sg * (1.0 + gate * (1.0 - sg))))
        ghw = jnp.sum(don * ohat, axis=0, keepdims=True)
        small_ref[1:2, 0:HEAD_V] += sum(ghw[:, h * HEAD_V:(h + 1) * HEAD_V] for h in range(HEADS))
        dohat = don * hw
        for h in range(HEADS):
            cols = slice(h * HEAD_V, (h + 1) * HEAD_V)
            oh, dh = ohat[:, cols], dohat[:, cols]
            do_ref[:, cols] = _bf(ro[h] * (dh - oh * jnp.mean(dh * oh, axis=-1, keepdims=True)))

    row = lambda cols: pl.BlockSpec((ts, cols), lambda i: (i, 0))
    act = jax.ShapeDtypeStruct((s, D), F32)
    act_bf = jax.ShapeDtypeStruct((s, D), BF16)
    return pl.pallas_call(
        body, name="head_and_loss", grid=(s // ts,),
        out_shape=(act, act_bf, act_bf, jax.ShapeDtypeStruct((D, D), F32), jax.ShapeDtypeStruct((8, D), F32)),
        in_specs=[row(D), row(D), row(D), row(D),
                  _full((1, D)), _full((D, D)), _full((1, D))],
        out_specs=(row(D), row(D), row(D), _full((D, D)), _full((8, D))),
        compiler_params=_params("arbitrary"),
    )(o, gate, h1, target, hw, wgo, wf)


def gla_project_backward(dq, dk, dv, dgate, dcum, low, h1, dh2, w1, wgi, wlow, wgk, bgk):
    s = h1.shape[0]
    ts = ROW_TILE

    def body(dq_ref, dk_ref, dv_ref, dgate_ref, dcum_ref, low_ref, h1_ref, dh2_ref, w1_ref,
             wgi_ref, wlow_ref, wgk_ref, bgk_ref, dh1_ref, dproj_ref, dlow_ref, ggk_ref, small_ref):
        @pl.when(pl.program_id(0) == 0)
        def _():
            ggk_ref[...] = jnp.zeros_like(ggk_ref)
            small_ref[...] = jnp.zeros_like(small_ref)

        low = low_ref[...]
        z = _nn(low, wgk_ref[...]) + bgk_ref[...]
        upper_f = _chunk_masks()[1].astype(F32)
        dlg = jnp.concatenate([_nn_exact(upper_f, dcum_ref[r0:r0 + CHUNK, :]) for r0 in range(0, ts, CHUNK)],
                              axis=0)
        dz = dlg * (1.0 / GATE_NORM) * _sigmoid(-z)
        dz_bf = _bf(dz)
        ggk_ref[...] += _tn(low, dz_bf)
        small_ref[1:2, 0:KEY_W] += jnp.sum(dz, axis=0, keepdims=True)
        dlow = _bf(_nt(dz_bf, wgk_ref[...]))
        dlow_ref[...] = dlow
        dn1 = _nt(dlow, wlow_ref[...])
        for ref, lo, hi in ((dq_ref, 0, KEY_W), (dk_ref, KEY_W, 2 * KEY_W),
                            (dv_ref, 2 * KEY_W, 2 * KEY_W + D), (dgate_ref, 2 * KEY_W + D, GLA_MAIN)):
            piece = ref[...]
            dproj_ref[:, lo:hi] = piece
            dn1 = dn1 + _nt(piece, wgi_ref[:, lo:hi])
        hv = h1_ref[...]
        r = lax.rsqrt(jnp.mean(hv * hv, axis=-1, keepdims=True) + EPS)
        hhat = hv * r
        small_ref[0:1, :] += jnp.sum(dn1 * hhat, axis=0, keepdims=True)
        dxh = dn1 * w1_ref[...]
        dh1_ref[...] = dh2_ref[...] + r * (dxh - hhat * jnp.mean(dxh * hhat, axis=-1, keepdims=True))

    row = lambda cols: pl.BlockSpec((ts, cols), lambda i: (i, 0))
    return pl.pallas_call(
        body, name="gla_project_backward", grid=(s // ts,),
        out_shape=(jax.ShapeDtypeStruct((s, D), F32), jax.ShapeDtypeStruct((s, GLA_MAIN), BF16),
                   jax.ShapeDtypeStruct((s, RANK_PAD), BF16), jax.ShapeDtypeStruct((RANK_PAD, KEY_W), F32),
                   jax.ShapeDtypeStruct((8, D), F32)),
        in_specs=[row(KEY_W), row(KEY_W), row(D), row(D), row(KEY_W), row(RANK_PAD), row(D), row(D),
                  _full((1, D)), _full((D, GLA_MAIN)), _full((D, RANK_PAD)), _full((RANK_PAD, KEY_W)),
                  _full((1, KEY_W))],
        out_specs=(row(D), row(GLA_MAIN), row(RANK_PAD), _full((RANK_PAD, KEY_W)), _full((8, D))),
        compiler_params=_params("arbitrary"),
    )(dq, dk, dv, dgate, dcum, low, h1, dh2, w1, wgi, wlow, wgk, bgk)


def _groups_from_quarters(a):
    return a.reshape(N_CHIPS, GROUPS, 64, GROUP_DIM).transpose(1, 0, 2, 3).reshape(GROUPS, GROUP_DIM, GROUP_DIM)


def _quarters_from_groups(a):
    return a.reshape(GROUPS, N_CHIPS, 64, GROUP_DIM).transpose(1, 0, 2, 3).reshape(N_CHIPS, GROUP_DIM, GROUP_DIM)


def _pad_row(*pieces):
    flat = jnp.concatenate([p.reshape(-1).astype(F32) for p in pieces])
    return jnp.pad(flat, (0, D - flat.shape[0])).reshape(1, D)


def _gla_in_weights(wgi_q):
    wgi_all = jnp.concatenate([wgi_q[q] for q in range(N_CHIPS)], axis=1)
    wlow = jnp.pad(wgi_all[:, GLA_MAIN:], ((0, 0), (0, RANK_PAD - GATE_RANK)))
    return wgi_all, wlow


def _small_sums(small_top, small_gla, small_pool, g_gk_pad):
    return jnp.concatenate([
        small_pool[0:1], small_gla[0:1],
        small_pool[1:2],
        small_top[0:1],
        small_top[2:3],
        _pad_row(small_gla[1, 0:KEY_W], small_top[1, 0:HEAD_V]),
        small_pool[2:3],
        g_gk_pad[:GATE_RANK].reshape(8, D),
        jnp.zeros((1, D), F32)], axis=0)


def local_gradients(xs, target, w0, w1, wf, wpi, gw, gb, scale, wpo, gla_quarters, wgk, bgk, hw_tiled, place):
    wgi_q, wgo_q = gla_quarters
    (h1, pooled, gt, n0), (wgi_q,) = pool_forward(xs, w0, wpi, gw, gb, scale, wpo, [wgi_q])
    wgi, wlow = _gla_in_weights(wgi_q)
    (qk, v, gate, low, cum, n1), (wgo_q,) = gla_project(h1, w1, wgi, wlow, wgk, bgk, [wgo_q])
    wgo = wgo_q.reshape(D, D)
    o, states, scores = gla_forward(qk, v, cum)

    dh2, do, dgate, g_gla_out, small_top = head_and_loss(o, gate, h1, target, hw_tiled, wgo, wf)
    dq, dk, dv, dcum = gla_backward(qk, v, cum, do, states, scores)
    dh1, dproj, dlow, g_gk_pad, small_gla = gla_project_backward(
        dq, dk, dv, dgate, dcum, low, h1, dh2, w1, wgi, wlow, wgk, bgk)
    g_gla_in = jnp.concatenate([matmul_tn(n1, dproj, "grad_gla_in"),
                                matmul_tn(n1, dlow, "grad_gla_low")[:, :GATE_RANK]], axis=1)

    def chip_sums(grads, names, tag):
        theirs = exchange_with_sibling(grads, "exchange_with_sibling_" + tag)
        return [add_halves(g, t, place, "add_halves_" + n) for g, t, n in zip(grads, theirs, names)]

    gla_names = ("gla_in", "gla_out")
    gla_sums = chip_sums(
        [jnp.stack([g_gla_in[:, GLA_IN_QUARTER * q:GLA_IN_QUARTER * (q + 1)] for q in range(N_CHIPS)]),
         g_gla_out.reshape(N_CHIPS, D // N_CHIPS, D)], gla_names, "gla")
    (dx, dpool, g_pool_out, g_group_w, small_pool), gla_got = pool_backward(
        xs, dh1, pooled, gt, w0, wpi, gw, gb, scale, wpo, [b for _, b in gla_sums])
    g_pool_in = matmul_tn(n0, dpool, "grad_pool_in", by_column_tile=True)

    pool_names = ("pool_in", "group", "pool_out")
    pool_sums = chip_sums(
        [g_pool_in, _quarters_from_groups(g_group_w), g_pool_out.reshape(N_CHIPS, D // N_CHIPS, D)],
        pool_names, "pool")
    pool_got = scatter_to_owners([b for _, b in pool_sums], "scatter_to_owners_pool")
    reduced, total = join_halves(
        [add_parts(f, g, place, "add_parts_" + n) for (f, _), g, n in
         zip(pool_sums + gla_sums, list(pool_got) + list(gla_got), pool_names + gla_names)],
        _small_sums(small_top, small_gla, small_pool, g_gk_pad))
    return dx, reduced, total


def kernel(x, norm_w, pool_in_w, pool_group_w, pool_group_b, pool_scale, pool_out_w, gla_in_w, gla_gk_w, gla_gk_b, gla_head_norm_w, gla_out_w, final_norm_w, loss_target, m_norm_w, m_pool_in_w, m_pool_group_w, m_pool_group_b, m_pool_scale, m_pool_out_w, m_gla_in_w, m_gla_gk_w, m_gla_gk_b, m_gla_head_norm_w, m_gla_out_w, m_final_norm_w, v_norm_w, v_pool_in_w, v_pool_group_w, v_pool_group_b, v_pool_scale, v_pool_out_w, v_gla_in_w, v_gla_gk_w, v_gla_gk_b, v_gla_head_norm_w, v_gla_out_w, v_final_norm_w):
    s = x.shape[1]
    xs = x[0]
    target = loss_target[0]
    q_chip = 2 * lax.axis_index("x") + lax.axis_index("y")
    place = jnp.stack([lax.axis_index("c"), q_chip]).astype(jnp.int32)

    small_in = jnp.concatenate([
        _pad_row(gla_gk_b[0], gla_head_norm_w[0], pool_group_b[0]),
        gla_gk_w[0].reshape(2, D),
        jnp.zeros((5, D), F32)], axis=0)
    (wpi, gw_q, wpo_q, wgi_q, wgo_q), small_all = allgather_weights(
        [pool_in_w[0], pool_group_w[0].reshape(GROUP_DIM, GROUP_DIM), pool_out_w[0], gla_in_w[0], gla_out_w[0]],
        exchange=(True, True, True, False, False), small=small_in)
    gw = _groups_from_quarters(gw_q)
    wpo = wpo_q.reshape(D, D)
    small_all = small_all[0::2]
    bgk = small_all[:, 0, 0:128].reshape(1, KEY_W)
    hw = small_all[:, 0, 128:192].reshape(1, HEAD_V)
    gb = jnp.concatenate([small_all[q, 0, 192:448].reshape(GROUPS, 64) for q in range(N_CHIPS)],
                         axis=1).reshape(1, D)
    wgk16 = jnp.concatenate([small_all[q, 1:3].reshape(GATE_RANK, 128) for q in range(N_CHIPS)], axis=1)
    wgk = _bf(jnp.pad(wgk16, ((0, RANK_PAD - GATE_RANK), (0, 0))))
    hw_tiled = jnp.tile(hw, (1, HEADS))

    w0 = norm_w[0:1]
    w1 = norm_w[1:2]
    wf = final_norm_w.reshape(1, D)

    dx, reduced, total = local_gradients(
        xs, target, w0, w1, wf, wpi, gw, gb, pool_scale, wpo, [wgi_q, wgo_q], wgk, bgk, hw_tiled, place)
    r_pool_in, r_group_w, r_pool_out, r_gla_in, r_gla_out = reduced
    r_group_w = r_group_w.reshape(GROUPS, 64, GROUP_DIM)

    loss = total[4, 0]
    g_norm = total[0:2]
    g_scale = total[2:3]
    g_final = total[3]
    pick = lambda full, width: lax.dynamic_slice_in_dim(full, q_chip * width, width, axis=-1)
    g_gk_b = pick(total[5:6, 0:KEY_W], 128)
    g_hnw = pick(total[5:6, KEY_W:KEY_W + HEAD_V], 64)
    g_group_b = pick(total[6].reshape(GROUPS, GROUP_DIM), 64)[None]
    g_gk_w = pick(total[7:15].reshape(GATE_RANK, KEY_W), 128)[None]

    def step_lane_rows(name, w, g, m, v):
        turn = lambda a: jnp.transpose(a, (2, 0, 1))
        back = lambda a: jnp.transpose(a, (1, 2, 0))
        g_t = turn(g)
        d, nm, nv = adamw_rows(turn(w), g_t, turn(m), turn(v), "adamw_" + name)
        return back(g_t), back(d), back(nm), back(nv)

    def step(name, w, g, m, v):
        shape = w.shape
        as2d = lambda a: a.reshape(-1, shape[-1])
        d, nm, nv = adamw(as2d(w), as2d(g), as2d(m), as2d(v), "adamw_" + name)
        return g.reshape(shape), d.reshape(shape), nm.reshape(shape), nv.reshape(shape)

    small_names = ("norm_w", "pool_group_b", "pool_scale", "gla_gk_w", "gla_gk_b", "gla_head_norm_w",
                   "final_norm_w")
    small_args = [(norm_w, g_norm, m_norm_w, v_norm_w),
                  (pool_group_b, g_group_b, m_pool_group_b, v_pool_group_b),
                  (pool_scale, g_scale, m_pool_scale, v_pool_scale),
                  (gla_gk_w, g_gk_w, m_gla_gk_w, v_gla_gk_w),
                  (gla_gk_b, g_gk_b, m_gla_gk_b, v_gla_gk_b),
                  (gla_head_norm_w, g_hnw, m_gla_head_norm_w, v_gla_head_norm_w),
                  (final_norm_w, g_final, m_final_norm_w, v_final_norm_w)]
    as2d = lambda a, w: a.reshape(-1, w.shape[-1])
    small_out = adamw_small([tuple(as2d(a, p[0]) for a in p) for p in small_args])
    small = {n: (p[1].reshape(p[0].shape),) + tuple(o.reshape(p[0].shape) for o in out)
             for n, p, out in zip(small_names, small_args, small_out)}
    results = [
        small["norm_w"],
        step("pool_in_w", pool_in_w, r_pool_in[None], m_pool_in_w, v_pool_in_w),
        step("pool_group_w", pool_group_w, r_group_w[None], m_pool_group_w, v_pool_group_w),
        small["pool_group_b"],
        small["pool_scale"],
        step("pool_out_w", pool_out_w, r_pool_out[None], m_pool_out_w, v_pool_out_w),
        step_lane_rows("gla_in_w", gla_in_w, r_gla_in[None], m_gla_in_w, v_gla_in_w),
        small["gla_gk_w"],
        small["gla_gk_b"],
        small["gla_head_norm_w"],
        step("gla_out_w", gla_out_w, r_gla_out[None], m_gla_out_w, v_gla_out_w),
        small["final_norm_w"],
    ]
    grads, deltas, new_m, new_v = zip(*results)
    return (loss, dx[None], *grads, *deltas, *new_m, *new_v)
```

```python
import functools

import jax
import jax.numpy as jnp
from jax import lax
from jax.experimental import pallas as pl
from jax.experimental.pallas import tpu as pltpu

F32 = jnp.float32
BF16 = jnp.bfloat16
MESH = pl.DeviceIdType.MESH

D = 1024
POOL_WINDOWS = (2, 4, 8, 16)
GROUPS = 4
GROUP_DIM = 256
HEADS = 4
HEAD_K = 128
HEAD_V = 256
KEY_W = 512
CHUNK = 64
GATE_RANK = 16
GATE_NORM = 16.0
GLA_IN = 3088
GLA_MAIN = 3072
RANK_PAD = 128
EPS = 1e-6
HALO = 32

ADAM_LR = 0.001
ADAM_B1 = 0.9
ADAM_B2 = 0.999
ADAM_EPS = 1e-08
ADAM_WD = 0.01
ADAM_STEP = 10

N_CHIPS = 4
N_DEV = 8
GLA_IN_QUARTER = GLA_IN // N_CHIPS

VMEM_LIMIT = 56 * 1024 * 1024


def _nn(a, b):
    return lax.dot_general(a, b, (((1,), (0,)), ((), ())), preferred_element_type=F32)


def _nt(a, b):
    return lax.dot_general(a, b, (((1,), (1,)), ((), ())), preferred_element_type=F32)


def _tn(a, b):
    return lax.dot_general(a, b, (((0,), (0,)), ((), ())), preferred_element_type=F32)


def _nn_exact(a, b):
    return lax.dot_general(a, b, (((1,), (0,)), ((), ())), preferred_element_type=F32,
                           precision=lax.Precision.HIGHEST)


def _bf(a):
    return a.astype(BF16)


def _params(*sem):
    return pltpu.CompilerParams(dimension_semantics=sem, vmem_limit_bytes=VMEM_LIMIT)


def _full(shape):
    return pl.BlockSpec(shape, lambda i: (0,) * len(shape))


def _position():
    return lax.axis_index("x"), lax.axis_index("y"), lax.axis_index("c")


def _gather_small(in_ref, all_ref, send_sems, recv_sems, local_sem):
    x, y, c = _position()
    me = 4 * x + 2 * y + c
    mine = pltpu.make_async_copy(in_ref, all_ref.at[me], local_sem)
    mine.start()
    sends = []
    for k in range(N_DEV - 1):
        fx, fy, fc = (k + 1) >> 2 & 1, (k + 1) >> 1 & 1, (k + 1) & 1
        cp = pltpu.make_async_remote_copy(
            src_ref=in_ref, dst_ref=all_ref.at[me],
            send_sem=send_sems.at[k], recv_sem=recv_sems.at[k],
            device_id=(x ^ fx, y ^ fy, c ^ fc), device_id_type=MESH)
        cp.start()
        sends.append(cp)
    for k in range(N_DEV - 1):
        fx, fy, fc = (k + 1) >> 2 & 1, (k + 1) >> 1 & 1, (k + 1) & 1
        src_dev = 4 * (x ^ fx) + 2 * (y ^ fy) + (c ^ fc)
        pltpu.make_async_remote_copy(
            src_ref=in_ref, dst_ref=all_ref.at[src_dev],
            send_sem=send_sems.at[k], recv_sem=recv_sems.at[k],
            device_id=(x, y, c), device_id_type=MESH).wait_recv()
    for cp in sends:
        cp.wait_send()
    mine.wait()


SMALL_SEMS = [pltpu.SemaphoreType.DMA((N_DEV - 1,)), pltpu.SemaphoreType.DMA((N_DEV - 1,)),
              pltpu.SemaphoreType.DMA]
VMEM_SPEC = pl.BlockSpec(memory_space=pltpu.VMEM)


def _other_chips(x, y):
    return [(1 - x, y), (x, 1 - y), (1 - x, 1 - y)]


def _any_specs(n):
    return [pl.BlockSpec(memory_space=pl.ANY)] * n


def _halves(rows, c):
    half = rows // 2
    return pl.ds(c * half, half), pl.ds((1 - c) * half, half)


CAST_ROWS = 256


def _gather_copy(out_ref, send_sems, recv_sems, k, quarter, half, to, src=None):
    dst = out_ref.at[quarter, half]
    return pltpu.make_async_remote_copy(
        src_ref=dst if src is None else src, dst_ref=dst,
        send_sem=send_sems.at[k], recv_sem=recv_sems.at[k], device_id=to, device_id_type=MESH)


SMALL_IN_ROWS = 24


def allgather_weights(quarters, exchange, smalls):
    n = len(quarters)
    shapes = [w.shape for w in quarters]
    moved = [i for i in range(n) if exchange[i]]

    def body(*refs):
        w_refs, (gkb_ref, hnw_ref, gb_ref, gkw_ref) = refs[:n], refs[n:n + 4]
        out_refs, small_all_ref = refs[n + 4:2 * n + 4], refs[2 * n + 4]
        refs = refs[2 * n + 5:]
        f32_bufs, bf_bufs = refs[:n], refs[n:2 * n]
        send_sems, recv_sems, local_sems, small_ref = refs[2 * n:2 * n + 4]
        small_ref[...] = jnp.zeros_like(small_ref)
        small_ref[0:1, :] = gkb_ref[...]
        small_ref[1:2, 0:64] = hnw_ref[...]
        small_ref[2:2 + GROUPS, 0:64] = gb_ref[...]
        small_ref[8:8 + GATE_RANK, :] = gkw_ref[...]
        x, y, c = _position()
        q = 2 * x + y
        sibling = (x, y, 1 - c)
        chips = _other_chips(x, y)

        def copy(k, i, quarter, half, to, src=None):
            return _gather_copy(out_refs[i], send_sems, recv_sems, k * n + i, quarter, half, to, src)

        loads = [pltpu.make_async_copy(w_refs[i], f32_bufs[i], local_sems.at[i]) for i in range(n)]
        for cp in loads:
            cp.start()
        keeps, sends = [], []
        for i in range(n):
            loads[i].wait()
            for r0 in range(0, shapes[i][0], CAST_ROWS):
                bf_bufs[i][r0:r0 + CAST_ROWS, :] = _bf(f32_bufs[i][r0:r0 + CAST_ROWS, :])
            keep = pltpu.make_async_copy(bf_bufs[i], out_refs[i].at[q], local_sems.at[n + i])
            keep.start()
            keeps.append(keep)
            if not exchange[i]:
                continue
            mine, _ = _halves(shapes[i][0], c)
            for j, chip in enumerate(chips):
                cp = copy(j, i, q, mine, (*chip, c), src=bf_bufs[i].at[mine])
                cp.start()
                sends.append(cp)
        for j, chip in enumerate(chips):
            qj = 2 * chip[0] + chip[1]
            for i in moved:
                mine, _ = _halves(shapes[i][0], c)
                copy(j, i, qj, mine, (x, y, c)).wait_recv()
                cp = copy(3 + j, i, qj, mine, sibling)
                cp.start()
                sends.append(cp)
        for j, chip in enumerate(chips):
            qj = 2 * chip[0] + chip[1]
            for i in moved:
                _, other = _halves(shapes[i][0], c)
                copy(3 + j, i, qj, other, (x, y, c)).wait_recv()
        _gather_small(small_ref, small_all_ref, *refs[2 * n + 4:])
        for cp in sends:
            cp.wait_send()
        for cp in keeps:
            cp.wait()

    outs = pl.pallas_call(
        body, name="allgather_weights",
        out_shape=[jax.ShapeDtypeStruct((N_CHIPS, *s), BF16) for s in shapes]
                  + [jax.ShapeDtypeStruct((N_DEV, SMALL_IN_ROWS, 128), F32)],
        in_specs=_any_specs(n) + [VMEM_SPEC] * 4, out_specs=_any_specs(n) + [VMEM_SPEC],
        scratch_shapes=([pltpu.VMEM(s, F32) for s in shapes] + [pltpu.VMEM(s, BF16) for s in shapes]
                        + [pltpu.SemaphoreType.DMA((6 * n,)), pltpu.SemaphoreType.DMA((6 * n,)),
                           pltpu.SemaphoreType.DMA((2 * n,)), pltpu.VMEM((SMALL_IN_ROWS, 128), F32)] + SMALL_SEMS),
        compiler_params=pltpu.CompilerParams(vmem_limit_bytes=VMEM_LIMIT),
    )(*quarters, *smalls)
    return outs[:n], outs[n]


def exchange_with_sibling(grads, name):
    n = len(grads)

    def body(*refs):
        g_refs, theirs_refs = refs[:n], refs[n:2 * n]
        send_sems, recv_sems = refs[2 * n:]
        x, y, c = _position()
        copies = []
        for i in range(n):
            _, other = _halves(g_refs[i].shape[1], c)
            cp = pltpu.make_async_remote_copy(
                src_ref=g_refs[i].at[:, other], dst_ref=theirs_refs[i],
                send_sem=send_sems.at[i], recv_sem=recv_sems.at[i],
                device_id=(x, y, 1 - c), device_id_type=MESH)
            cp.start()
            copies.append(cp)
        for cp in copies:
            cp.wait()

    return pl.pallas_call(
        body, name=name,
        out_shape=[jax.ShapeDtypeStruct((N_CHIPS, g.shape[1] // 2, g.shape[2]), F32) for g in grads],
        in_specs=_any_specs(n), out_specs=_any_specs(n),
        scratch_shapes=[pltpu.SemaphoreType.DMA((n,)), pltpu.SemaphoreType.DMA((n,))],
    )(*grads)


def _scatter_copies(b_refs, got_refs, send_sems, recv_sems):
    n = len(b_refs)
    x, y, c = _position()
    copies = []
    for j, chip in enumerate(_other_chips(x, y)):
        qj = 2 * chip[0] + chip[1]
        for i in range(n):
            copies.append(pltpu.make_async_remote_copy(
                src_ref=b_refs[i].at[qj], dst_ref=got_refs[i].at[j],
                send_sem=send_sems.at[j * n + i], recv_sem=recv_sems.at[j * n + i],
                device_id=(*chip, c), device_id_type=MESH))
    return copies


def _scatter_shapes(chip_sums):
    return [jax.ShapeDtypeStruct((N_CHIPS - 1, *b.shape[1:]), BF16) for b in chip_sums]


def scatter_to_owners(chip_sums, name):
    n = len(chip_sums)

    def body(*refs):
        copies = _scatter_copies(refs[:n], refs[n:2 * n], *refs[2 * n:])
        for cp in copies:
            cp.start()
        for cp in copies:
            cp.wait()

    return pl.pallas_call(
        body, name=name,
        out_shape=_scatter_shapes(chip_sums),
        in_specs=_any_specs(n), out_specs=_any_specs(n),
        scratch_shapes=[pltpu.SemaphoreType.DMA((3 * n,)), pltpu.SemaphoreType.DMA((3 * n,))],
    )(*chip_sums)


SMALL_SUM_ROWS = 16


def join_halves(reduced, small_pool, small_gla, small_top, g_gk_pad):
    n = len(reduced)

    def body(*refs):
        pool_ref, gla_ref, top_ref, gk_ref = refs[n:n + 4]
        buf_refs, total_ref = refs[n + 4:2 * n + 4], refs[2 * n + 4]
        send_sems, recv_sems, all_ref, small_ref = refs[2 * n + 5:2 * n + 9]
        small_ref[0:3, :] = pool_ref[0:3, :]
        small_ref[3:5, :] = gla_ref[0:2, :]
        small_ref[5:8, :] = top_ref[0:3, :]
        for r in range(GATE_RANK):
            small_ref[8 + r // 2:9 + r // 2, (r % 2) * KEY_W:(r % 2 + 1) * KEY_W] = gk_ref[r:r + 1, :]
        x, y, c = _position()
        copies = []
        for i in range(n):
            mine, _ = _halves(buf_refs[i].shape[0], c)
            cp = pltpu.make_async_remote_copy(
                src_ref=buf_refs[i].at[mine], dst_ref=buf_refs[i].at[mine],
                send_sem=send_sems.at[i], recv_sem=recv_sems.at[i],
                device_id=(x, y, 1 - c), device_id_type=MESH)
            cp.start()
            copies.append(cp)
        _gather_small(small_ref, all_ref, *refs[2 * n + 9:])
        total = all_ref[0]
        for dev in range(1, N_DEV):
            total = total + all_ref[dev]
        total_ref[...] = total
        for cp in copies:
            cp.wait()

    outs = pl.pallas_call(
        body, name="join_halves",
        out_shape=[jax.ShapeDtypeStruct(r.shape, F32) for r in reduced]
                  + [jax.ShapeDtypeStruct((SMALL_SUM_ROWS, D), F32)],
        in_specs=_any_specs(n) + [VMEM_SPEC] * 4, out_specs=_any_specs(n) + [VMEM_SPEC],
        input_output_aliases={i: i for i in range(n)},
        scratch_shapes=[pltpu.SemaphoreType.DMA((n,)), pltpu.SemaphoreType.DMA((n,)),
                        pltpu.VMEM((N_DEV, SMALL_SUM_ROWS, D), F32), pltpu.VMEM((SMALL_SUM_ROWS, D), F32)]
                       + SMALL_SEMS,
    )(*reduced, small_pool, small_gla, small_top, g_gk_pad)
    return outs[:n], outs[n]


ADD_ROWS = 512


def add_halves(grad, theirs, place, name):
    _, half, cols = theirs.shape
    rb = min(ADD_ROWS, half)
    steps = half // rb

    def body(place_ref, a_ref, b_ref, f_ref, h_ref):
        s = a_ref[0] + b_ref[0]
        h_ref[0] = _bf(s)

        @pl.when(pl.program_id(1) == place_ref[1])
        def _():
            f_ref[...] = s

    spec = pl.BlockSpec((1, rb, cols), lambda j, i, place: (i, j, 0))
    return pl.pallas_call(
        body, name=name,
        grid_spec=pltpu.PrefetchScalarGridSpec(
            num_scalar_prefetch=1, grid=(steps, N_CHIPS),
            in_specs=[pl.BlockSpec((1, rb, cols), lambda j, i, place: (i, place[0] * steps + j, 0)), spec],
            out_specs=(pl.BlockSpec((rb, cols), lambda j, i, place: (j, 0)), spec)),
        out_shape=(jax.ShapeDtypeStruct((half, cols), F32), jax.ShapeDtypeStruct(theirs.shape, BF16)),
        compiler_params=_params("parallel", "arbitrary"),
    )(place, grad, theirs)


def add_parts(own, got, place, name):
    _, half, cols = got.shape
    rb = min(ADD_ROWS, half)
    steps = half // rb

    def body(place_ref, o_ref, g_ref, out_ref):
        s = o_ref[...]
        for j in range(N_CHIPS - 1):
            s = s + g_ref[j].astype(F32)
        out_ref[...] = s

    return pl.pallas_call(
        body, name=name,
        grid_spec=pltpu.PrefetchScalarGridSpec(
            num_scalar_prefetch=1, grid=(steps,),
            in_specs=[pl.BlockSpec((rb, cols), lambda j, place: (j, 0)),
                      pl.BlockSpec((N_CHIPS - 1, rb, cols), lambda j, place: (0, j, 0))],
            out_specs=pl.BlockSpec((rb, cols), lambda j, place: (place[0] * steps + j, 0))),
        out_shape=jax.ShapeDtypeStruct((2 * half, cols), F32),
        compiler_params=_params("parallel"),
    )(place, own, got)


def _adam_math(w, g, m, v):
    m = ADAM_B1 * m + (1.0 - ADAM_B1) * g
    v = ADAM_B2 * v + (1.0 - ADAM_B2) * (g * g)
    m_hat = m / (1.0 - ADAM_B1 ** ADAM_STEP)
    v_hat = v / (1.0 - ADAM_B2 ** ADAM_STEP)
    delta = -ADAM_LR * (m_hat / (jnp.sqrt(v_hat) + ADAM_EPS) + ADAM_WD * w)
    return delta, m, v


def adamw(w, g, m, v, name):
    rows, cols = w.shape
    fits = [t for t in range(8, rows, 8) if rows % t == 0 and t * cols * 4 <= 2 ** 20]
    tile = max(fits) if fits else rows

    def body(w_ref, g_ref, m_ref, v_ref, d_ref, nm_ref, nv_ref):
        d, nm, nv = _adam_math(w_ref[...], g_ref[...], m_ref[...], v_ref[...])
        d_ref[...] = d
        nm_ref[...] = nm
        nv_ref[...] = nv

    spec = pl.BlockSpec((tile, cols), lambda i: (i, 0))
    shape = jax.ShapeDtypeStruct((rows, cols), F32)
    return pl.pallas_call(
        body, name=name, grid=(rows // tile,),
        out_shape=(shape, shape, shape),
        in_specs=[spec] * 4, out_specs=(spec, spec, spec),
        compiler_params=_params("parallel"),
    )(w, g, m, v)


def adamw_small(params):
    n = len(params)

    def body(*refs):
        ins, outs = refs[:4 * n], refs[4 * n:]
        for k in range(n):
            w_ref, g_ref, m_ref, v_ref = ins[4 * k:4 * k + 4]
            d, nm, nv = _adam_math(w_ref[...], g_ref[...], m_ref[...], v_ref[...])
            outs[3 * k][...] = d
            outs[3 * k + 1][...] = nm
            outs[3 * k + 2][...] = nv

    flat = [a for p in params for a in p]
    outs = pl.pallas_call(
        body, name="adamw_small",
        out_shape=[jax.ShapeDtypeStruct(p[0].shape, F32) for p in params for _ in range(3)],
        in_specs=[VMEM_SPEC] * (4 * n), out_specs=[VMEM_SPEC] * (3 * n),
    )(*flat)
    return [tuple(outs[3 * k:3 * k + 3]) for k in range(n)]


def adamw_rows(w, g, m, v, name):
    rows, _, cols = w.shape
    tile = rows // 4

    def body(w_ref, g_ref, m_ref, v_ref, d_ref, nm_ref, nv_ref):
        d, nm, nv = _adam_math(w_ref[...], g_ref[...], m_ref[...], v_ref[...])
        d_ref[...] = d
        nm_ref[...] = nm
        nv_ref[...] = nv

    spec = pl.BlockSpec((tile, 1, cols), lambda i: (i, 0, 0))
    shape = jax.ShapeDtypeStruct(w.shape, F32)
    return pl.pallas_call(
        body, name=name, grid=(rows // tile,),
        out_shape=(shape, shape, shape),
        in_specs=[spec] * 4, out_specs=(spec, spec, spec),
        compiler_params=_params("parallel"),
    )(w, g, m, v)


def matmul_tn(a, b, name, tile_n=512, by_column_tile=False, narrow=None):
    s, m = a.shape
    n = b.shape[1]
    tile_n = min(tile_n, n)
    if by_column_tile:
        out_shape = jax.ShapeDtypeStruct((n // tile_n, m, tile_n), F32)
        out_spec = pl.BlockSpec((None, m, tile_n), lambda j: (j, 0, 0))
    else:
        out_shape = jax.ShapeDtypeStruct((m, n), F32)
        out_spec = pl.BlockSpec((m, tile_n), lambda j: (0, j))

    if narrow is None:
        def body(a_ref, b_ref, out_ref):
            out_ref[...] = _tn(a_ref[...], b_ref[...])

        return pl.pallas_call(
            body, name=name, grid=(n // tile_n,),
            out_shape=out_shape,
            in_specs=[_full((s, m)), pl.BlockSpec((s, tile_n), lambda j: (0, j))],
            out_specs=out_spec,
            compiler_params=_params("parallel"),
        )(a, b)

    def body_with_narrow(a_ref, b_ref, c_ref, out_ref, out_c_ref):
        out_ref[...] = _tn(a_ref[...], b_ref[...])

        @pl.when(pl.program_id(0) == 0)
        def _():
            out_c_ref[...] = _tn(a_ref[...], c_ref[...])

    return pl.pallas_call(
        body_with_narrow, name=name, grid=(n // tile_n,),
        out_shape=(out_shape, jax.ShapeDtypeStruct((m, narrow.shape[1]), F32)),
        in_specs=[_full((s, m)), pl.BlockSpec((s, tile_n), lambda j: (0, j)), _full(narrow.shape)],
        out_specs=(out_spec, _full((m, narrow.shape[1]))),
        compiler_params=_params("arbitrary"),
    )(a, b, narrow)


ROW_TILE = 512


def _row_index(tile, rows):
    return tile * rows + lax.broadcasted_iota(jnp.int32, (rows, 1), 0)


def _inverse_counts(t_glob):
    return [1.0 / jnp.minimum(t_glob + 1, w).astype(F32) for w in POOL_WINDOWS]


def _sigmoid(z):
    return 1.0 / (1.0 + jnp.exp(-z))


def _trailing_sums(src, tmp, cols, window, rows):
    bufs = (src, tmp)
    span, level, start = 1, 0, 0
    while span < window:
        start += 8
        a, b = bufs[level % 2], bufs[(level + 1) % 2]
        n = HALO + rows - start
        b[start:start + n, cols] = a[start:start + n, cols] + a[start - span:start - span + n, cols]
        span, level = 2 * span, level + 1
    return bufs[level % 2][HALO:HALO + rows, cols]


def _leading_sums(src, tmp, cols, window, rows):
    bufs = (src, tmp)
    span, level, n = 1, 0, rows + HALO
    while span < window:
        n -= 8
        a, b = bufs[level % 2], bufs[(level + 1) % 2]
        b[0:n, cols] = a[0:n, cols] + a[span:span + n, cols]
        span, level = 2 * span, level + 1
    return bufs[level % 2][0:rows, cols]


def gather_in_background(step, last, out_refs, send_sems, recv_sems, finish):
    n = len(out_refs)
    x, y, c = _position()
    q = 2 * x + y
    chips = _other_chips(x, y)

    def copy(k, i, quarter, half, to):
        return _gather_copy(out_refs[i], send_sems, recv_sems, k * n + i, quarter, half, to)

    if not finish:
        @pl.when(step == 0)
        def _():
            for i in range(n):
                mine, _ = _halves(out_refs[i].shape[1], c)
                for j, chip in enumerate(chips):
                    copy(j, i, q, mine, (*chip, c)).start()

        @pl.when(step == last)
        def _():
            for j, chip in enumerate(chips):
                qj = 2 * chip[0] + chip[1]
                for i in range(n):
                    mine, _ = _halves(out_refs[i].shape[1], c)
                    copy(j, i, qj, mine, (x, y, c)).wait_recv()
                    copy(3 + j, i, qj, mine, (x, y, 1 - c)).start()
        return

    @pl.when(step == last)
    def _():
        for j, chip in enumerate(chips):
            qj = 2 * chip[0] + chip[1]
            for i in range(n):
                mine, other = _halves(out_refs[i].shape[1], c)
                copy(3 + j, i, qj, other, (x, y, c)).wait_recv()
                copy(j, i, q, mine, (x, y, c)).wait_send()
                copy(3 + j, i, qj, mine, (x, y, c)).wait_send()


def pool_forward(x, w0, wpi, gw, gb, scale, wpo, later):
    s = x.shape[0]
    ts = ROW_TILE
    nt = s // ts
    assert nt >= 2
    n_later = len(later)

    def body(x_ref, w0_ref, wpi_ref, gw_ref, gb_ref, sc_ref, wpo_ref, *rest):
        rest = rest[n_later:]
        h1_ref, pooled_ref, gt_ref, n0_ref = rest[:4]
        later_refs = rest[4:4 + n_later]
        ubuf, tbuf, hist, send_sems, recv_sems = rest[4 + n_later:]
        i = pl.program_id(0)
        gather_in_background(i, nt - 1, later_refs, send_sems, recv_sems, finish=False)
        xv = x_ref[...]
        r = lax.rsqrt(jnp.mean(xv * xv, axis=-1, keepdims=True) + EPS)
        n0 = _bf(xv * r * w0_ref[...])
        n0_ref[...] = n0
        u = jnp.concatenate([_nn(n0, wpi_ref[0]), _nn(n0, wpi_ref[1])], axis=-1)
        gt = jnp.concatenate([_nn(n0, wpi_ref[2]), _nn(n0, wpi_ref[3])], axis=-1)
        gt_ref[...] = gt

        @pl.when(i == 0)
        def _():
            hist[...] = jnp.zeros_like(hist)

        ubuf[0:HALO, :] = hist[...]
        ubuf[HALO:HALO + ts, :] = u
        hist[...] = u[ts - HALO:, :]
        inv = _inverse_counts(_row_index(i, ts))
        mixed = []
        for g, w in enumerate(POOL_WINDOWS):
            cols = slice(g * GROUP_DIM, (g + 1) * GROUP_DIM)
            pooled = _bf(_trailing_sums(ubuf, tbuf, cols, w, ts) * inv[g] - u[:, cols])
            pooled_ref[:, cols] = pooled
            mixed.append(_nn(pooled, gw_ref[g]))
        mixed = jnp.concatenate(mixed, axis=-1) + gb_ref[...]
        y = mixed * sc_ref[...] * (gt * _sigmoid(gt))
        h1_ref[...] = xv + _nn(_bf(y), wpo_ref[...])
        gather_in_background(i, nt - 1, later_refs, send_sems, recv_sems, finish=True)

    row = lambda cols: pl.BlockSpec((ts, cols), lambda i: (i, 0))
    outs = pl.pallas_call(
        body, name="pool_forward", grid=(nt,),
        out_shape=[jax.ShapeDtypeStruct((s, D), F32), jax.ShapeDtypeStruct((s, D), BF16),
                   jax.ShapeDtypeStruct((s, D), F32), jax.ShapeDtypeStruct((s, D), BF16)]
                  + [jax.ShapeDtypeStruct(a.shape, a.dtype) for a in later],
        in_specs=[row(D), _full((1, D)), _full((N_CHIPS, D, D // 2)), _full((GROUPS, GROUP_DIM, GROUP_DIM)),
                  _full((1, D)), _full((1, D)), _full((D, D))] + _any_specs(n_later),
        out_specs=[row(D), row(D), row(D), row(D)] + _any_specs(n_later),
        input_output_aliases={7 + k: 4 + k for k in range(n_later)},
        scratch_shapes=[pltpu.VMEM((HALO + ts, D), F32), pltpu.VMEM((HALO + ts, D), F32),
                        pltpu.VMEM((HALO, D), F32),
                        pltpu.SemaphoreType.DMA((6 * n_later,)), pltpu.SemaphoreType.DMA((6 * n_later,))],
        compiler_params=_params("arbitrary"),
    )(x, w0, wpi, gw, gb, scale, wpo, *later)
    return outs[:4], outs[4:]


def pool_backward(x, dh1, pooled, gt, w0, wpi, gw, gb, scale, wpo, chip_sums):
    s = x.shape[0]
    ts = ROW_TILE
    nt = s // ts
    n_sums = len(chip_sums)

    def body(x_ref, dh1_ref, pooled_ref, gt_ref, w0_ref, wpi_ref, gw_ref, gb_ref, sc_ref, wpo_ref, *rest):
        sum_refs, rest = rest[:n_sums], rest[n_sums:]
        dx_ref, dproj_ref, gpo_ref, ggw_ref, small_ref = rest[:5]
        got_refs = rest[5:5 + n_sums]
        ebuf, tbuf, ahead, send_sems, recv_sems = rest[5 + n_sums:]
        i = pl.program_id(0)
        copies = _scatter_copies(sum_refs, got_refs, send_sems, recv_sems)

        @pl.when(i == 0)
        def _():
            for cp in copies:
                cp.start()

        @pl.when(i == 0)
        def _():
            gpo_ref[...] = jnp.zeros_like(gpo_ref)
            ggw_ref[...] = jnp.zeros_like(ggw_ref)
            small_ref[...] = jnp.zeros_like(small_ref)
            ahead[...] = jnp.zeros_like(ahead)

        dh1 = dh1_ref[...]
        dh1_bf = _bf(dh1)
        gt = gt_ref[...]
        sc = sc_ref[...]
        dy = _nt(dh1_bf, wpo_ref[...])
        pooled_bf = []
        mixed = []
        for g in range(GROUPS):
            cols = slice(g * GROUP_DIM, (g + 1) * GROUP_DIM)
            pb = pooled_ref[:, cols]
            pooled_bf.append(pb)
            mixed.append(_nn(pb, gw_ref[g]))
        mixed = jnp.concatenate(mixed, axis=-1) + gb_ref[...]
        sg = _sigmoid(gt)
        silu = gt * sg
        gpo_ref[...] += _tn(_bf(mixed * sc * silu), dh1_bf)
        dmixed = dy * sc * silu
        dgt = dy * mixed * sc * (sg * (1.0 + gt * (1.0 - sg)))
        dproj_ref[:, D:] = _bf(dgt)
        small_ref[1:2, :] += jnp.sum(dy * mixed * silu, axis=0, keepdims=True)
        small_ref[2:3, :] += jnp.sum(dmixed, axis=0, keepdims=True)

        inv = _inverse_counts(_row_index(nt - 1 - i, ts))
        ebuf[ts:ts + HALO, :] = ahead[...]
        dpooled = []
        for g in range(GROUPS):
            cols = slice(g * GROUP_DIM, (g + 1) * GROUP_DIM)
            dm = _bf(dmixed[:, cols])
            ggw_ref[g] += _tn(pooled_bf[g], dm)
            dp = _nt(dm, gw_ref[g])
            dpooled.append(dp)
            ebuf[0:ts, cols] = dp * inv[g]
        ahead[...] = ebuf[0:HALO, :]
        du = []
        for g, w in enumerate(POOL_WINDOWS):
            cols = slice(g * GROUP_DIM, (g + 1) * GROUP_DIM)
            du.append(_leading_sums(ebuf, tbuf, cols, w, ts) - dpooled[g])
        du = _bf(jnp.concatenate(du, axis=-1))
        dproj_ref[:, :D] = du
        dgt_bf = _bf(dgt)
        half = D // 2
        dn0 = (_nt(du[:, :half], wpi_ref[0]) + _nt(du[:, half:], wpi_ref[1])
               + _nt(dgt_bf[:, :half], wpi_ref[2]) + _nt(dgt_bf[:, half:], wpi_ref[3]))

        xv = x_ref[...]
        r = lax.rsqrt(jnp.mean(xv * xv, axis=-1, keepdims=True) + EPS)
        xhat = xv * r
        small_ref[0:1, :] += jnp.sum(dn0 * xhat, axis=0, keepdims=True)
        dxh = dn0 * w0_ref[...]
        dx_ref[...] = dh1 + r * (dxh - xhat * jnp.mean(dxh * xhat, axis=-1, keepdims=True))

        @pl.when(i == nt - 1)
        def _():
            for cp in copies:
                cp.wait()

    row = lambda cols: pl.BlockSpec((ts, cols), lambda i: (nt - 1 - i, 0))
    outs = pl.pallas_call(
        body, name="pool_backward", grid=(nt,),
        out_shape=[jax.ShapeDtypeStruct((s, D), F32), jax.ShapeDtypeStruct((s, 2 * D), BF16),
                   jax.ShapeDtypeStruct((D, D), F32),
                   jax.ShapeDtypeStruct((GROUPS, GROUP_DIM, GROUP_DIM), F32),
                   jax.ShapeDtypeStruct((8, D), F32)] + _scatter_shapes(chip_sums),
        in_specs=[row(D), row(D), row(D), row(D), _full((1, D)), _full((N_CHIPS, D, D // 2)),
                  _full((GROUPS, GROUP_DIM, GROUP_DIM)), _full((1, D)), _full((1, D)), _full((D, D))]
                 + _any_specs(n_sums),
        out_specs=[row(D), row(2 * D), _full((D, D)), _full((GROUPS, GROUP_DIM, GROUP_DIM)), _full((8, D))]
                  + _any_specs(n_sums),
        scratch_shapes=[pltpu.VMEM((ts + HALO, D), F32), pltpu.VMEM((ts + HALO, D), F32),
                        pltpu.VMEM((HALO, D), F32),
                        pltpu.SemaphoreType.DMA((3 * n_sums,)), pltpu.SemaphoreType.DMA((3 * n_sums,))],
        compiler_params=_params("arbitrary"),
    )(x, dh1, pooled, gt, w0, wpi, gw, gb, scale, wpo, *chip_sums)
    return outs[:5], outs[5:]


def gla_project(h1, w1, wgi, wlow, wgk, bgk, later):
    s = h1.shape[0]
    ts = ROW_TILE
    nt = s // ts
    assert nt >= 2
    n_later = len(later)

    def body(h_ref, w1_ref, wgi_ref, wlow_ref, wgk_ref, bgk_ref, *rest):
        rest = rest[n_later:]
        qk_ref, v_ref, gate_ref, low_ref, cum_ref, n1_ref = rest[:6]
        later_refs = rest[6:6 + n_later]
        send_sems, recv_sems = rest[6 + n_later:]
        gather_in_background(pl.program_id(0), nt - 1, later_refs, send_sems, recv_sems, finish=False)
        hv = h_ref[...]
        r = lax.rsqrt(jnp.mean(hv * hv, axis=-1, keepdims=True) + EPS)
        n1 = _bf(hv * r * w1_ref[...])
        n1_ref[...] = n1
        qk_ref[...] = _nn(n1, wgi_ref[:, 0:2 * KEY_W])
        v_ref[...] = _bf(_nn(n1, wgi_ref[:, 2 * KEY_W:2 * KEY_W + D]))
        gate_ref[...] = _nn(n1, wgi_ref[:, 2 * KEY_W + D:GLA_MAIN])
        low = _bf(_nn(n1, wlow_ref[...]))
        low_ref[...] = low
        z = _nn(low, wgk_ref[...]) + bgk_ref[...]
        lg = (jnp.minimum(z, 0.0) - jnp.log(1.0 + jnp.exp(-jnp.abs(z)))) / GATE_NORM
        lower_f = _chunk_masks()[0].astype(F32)
        for r0 in range(0, ts, CHUNK):
            cum_ref[r0:r0 + CHUNK, :] = _nn_exact(lower_f, lg[r0:r0 + CHUNK, :])
        gather_in_background(pl.program_id(0), nt - 1, later_refs, send_sems, recv_sems, finish=True)

    row = lambda cols: pl.BlockSpec((ts, cols), lambda i: (i, 0))
    outs = pl.pallas_call(
        body, name="gla_project", grid=(nt,),
        out_shape=[jax.ShapeDtypeStruct((s, D), F32), jax.ShapeDtypeStruct((s, D), BF16),
                   jax.ShapeDtypeStruct((s, D), F32), jax.ShapeDtypeStruct((s, RANK_PAD), BF16),
                   jax.ShapeDtypeStruct((s, KEY_W), F32), jax.ShapeDtypeStruct((s, D), BF16)]
                  + [jax.ShapeDtypeStruct(a.shape, a.dtype) for a in later],
        in_specs=[row(D), _full((1, D)), _full((D, GLA_MAIN)), _full((D, RANK_PAD)),
                  _full((RANK_PAD, KEY_W)), _full((1, KEY_W))] + _any_specs(n_later),
        out_specs=[row(D), row(D), row(D), row(RANK_PAD), row(KEY_W), row(D)] + _any_specs(n_later),
        input_output_aliases={6 + k: 6 + k for k in range(n_later)},
        scratch_shapes=[pltpu.SemaphoreType.DMA((6 * n_later,)), pltpu.SemaphoreType.DMA((6 * n_later,))],
        compiler_params=_params("arbitrary"),
    )(h1, w1, wgi, wlow, wgk, bgk, *later)
    return outs[:6], outs[6:]


GLA_BLOCK = 512
CHUNKS_PER_BLOCK = GLA_BLOCK // CHUNK


def _chunk_masks():
    t = lax.broadcasted_iota(jnp.int32, (CHUNK, CHUNK), 0)
    u = lax.broadcasted_iota(jnp.int32, (CHUNK, CHUNK), 1)
    return t >= u, t <= u


def _gla_chunk_terms(q, cum):
    ep = jnp.exp(cum)
    en = jnp.exp(-cum)
    qs = q * (HEAD_K ** -0.5)
    last = cum[CHUNK - 1:CHUNK, :]
    ed = jnp.exp(last - cum)
    dec = jnp.exp(last)
    return ep, en, qs, ed, dec


def gla_forward(qk, v, cum):
    s = qk.shape[0]
    nb = s // GLA_BLOCK
    nc = s // CHUNK

    def body(q_ref, k_ref, v_ref, cum_ref, o_ref, st_ref, sc_ref, state):
        @pl.when(pl.program_id(0) == 0)
        def _():
            state[...] = jnp.zeros_like(state)

        lower, _ = _chunk_masks()

        def chunk(cc, carry):
            rows = pl.ds(pl.multiple_of(cc * CHUNK, CHUNK), CHUNK)
            for h in range(HEADS):
                kc = slice(h * HEAD_K, (h + 1) * HEAD_K)
                vc = slice(h * HEAD_V, (h + 1) * HEAD_V)
                q = q_ref[rows, kc]
                k = k_ref[rows, kc]
                v = v_ref[rows, vc]
                ep, en, qs, ed, dec = _gla_chunk_terms(q, cum_ref[rows, kc])
                a = _bf(qs * ep)
                fwd = _nt(a, _bf(k * en))
                bwd = _nt(_bf(qs * en), _bf(k * ep))
                scores = _bf(jnp.where(lower, fwd, bwd))
                sc_ref[rows, h * CHUNK:(h + 1) * CHUNK] = scores
                st = state[h]
                st_ref[cc, h] = st
                o_ref[rows, vc] = _nn(scores, v) + _nt(a, _bf(st))
                state[h] = st * dec + _tn(v, _bf(k * ed))
            return carry

        lax.fori_loop(0, CHUNKS_PER_BLOCK, chunk, 0, unroll=4)

    return pl.pallas_call(
        body, name="gla_forward", grid=(nb,),
        out_shape=(jax.ShapeDtypeStruct((s, D), F32),
                   jax.ShapeDtypeStruct((nc, HEADS, HEAD_V, HEAD_K), F32),
                   jax.ShapeDtypeStruct((s, HEADS * CHUNK), BF16)),
        in_specs=[pl.BlockSpec((GLA_BLOCK, KEY_W), lambda i: (i, 0)),
                  pl.BlockSpec((GLA_BLOCK, KEY_W), lambda i: (i, 1)),
                  pl.BlockSpec((GLA_BLOCK, D), lambda i: (i, 0)),
                  pl.BlockSpec((GLA_BLOCK, KEY_W), lambda i: (i, 0))],
        out_specs=(pl.BlockSpec((GLA_BLOCK, D), lambda i: (i, 0)),
                   pl.BlockSpec((CHUNKS_PER_BLOCK, HEADS, HEAD_V, HEAD_K), lambda i: (i, 0, 0, 0)),
                   pl.BlockSpec((GLA_BLOCK, HEADS * CHUNK), lambda i: (i, 0))),
        scratch_shapes=[pltpu.VMEM((HEADS, HEAD_V, HEAD_K), F32)],
        compiler_params=_params("arbitrary"),
    )(qk, qk, v, cum)


def gla_backward(qk, v, cum, do, states, scores):
    s = qk.shape[0]
    nb = s // GLA_BLOCK

    def body(q_ref, k_ref, v_ref, cum_ref, do_ref, st_ref, sc_ref, dq_ref, dk_ref, dv_ref, dcum_ref, dstate):
        @pl.when(pl.program_id(0) == 0)
        def _():
            dstate[...] = jnp.zeros_like(dstate)

        lower, _ = _chunk_masks()
        is_last = lax.broadcasted_iota(jnp.int32, (CHUNK, HEAD_K), 0) == CHUNK - 1

        def chunk(step, carry):
            cc = CHUNKS_PER_BLOCK - 1 - step
            rows = pl.ds(pl.multiple_of(cc * CHUNK, CHUNK), CHUNK)
            for h in range(HEADS):
                kc = slice(h * HEAD_K, (h + 1) * HEAD_K)
                vc = slice(h * HEAD_V, (h + 1) * HEAD_V)
                q = q_ref[rows, kc]
                k = k_ref[rows, kc]
                v = v_ref[rows, vc]
                do_c = do_ref[rows, vc]
                ep, en, qs, ed, dec = _gla_chunk_terms(q, cum_ref[rows, kc])
                a = _bf(qs * ep)
                b = _bf(k * en)
                c = _bf(qs * en)
                dk_dec = _bf(k * ep)
                kd = _bf(k * ed)
                scores = sc_ref[rows, h * CHUNK:(h + 1) * CHUNK]
                st = st_ref[cc, h]
                dst = dstate[h]
                dst_bf = _bf(dst)

                dscores = _nt(do_c, v)
                dfwd = _bf(jnp.where(lower, dscores, 0.0))
                dbwd = _bf(jnp.where(lower, 0.0, dscores))
                dv_ref[rows, vc] = _bf(_tn(scores, do_c) + _nt(kd, dst_bf))
                da = _nn(dfwd, b) + _nn(do_c, _bf(st))
                db = _tn(dfwd, a)
                dc = _nn(dbwd, dk_dec)
                ddk = _tn(dbwd, c)
                dkd = _nn(v, dst_bf)
                ddec = jnp.sum(dst * st, axis=0, keepdims=True)
                dstate[h] = dst * dec + _tn(do_c, a)

                m = dkd * k * ed
                dq_ref[rows, kc] = _bf((da * ep + dc * en) * (HEAD_K ** -0.5))
                dk_ref[rows, kc] = _bf(db * en + ddk * ep + dkd * ed)
                dcum = (da * qs + ddk * k) * ep - (db * k + dc * qs) * en - m
                dlast = jnp.sum(m, axis=0, keepdims=True) + ddec * dec
                dcum_ref[rows, kc] = dcum + jnp.where(is_last, dlast, 0.0)
            return carry

        lax.fori_loop(0, CHUNKS_PER_BLOCK, chunk, 0, unroll=4)

    rev = lambda cols, col_block: pl.BlockSpec((GLA_BLOCK, cols), lambda i: (nb - 1 - i, col_block))
    return pl.pallas_call(
        body, name="gla_backward", grid=(nb,),
        out_shape=(jax.ShapeDtypeStruct((s, KEY_W), BF16), jax.ShapeDtypeStruct((s, KEY_W), BF16),
                   jax.ShapeDtypeStruct((s, D), BF16), jax.ShapeDtypeStruct((s, KEY_W), F32)),
        in_specs=[rev(KEY_W, 0), rev(KEY_W, 1), rev(D, 0), rev(KEY_W, 0), rev(D, 0),
                  pl.BlockSpec((CHUNKS_PER_BLOCK, HEADS, HEAD_V, HEAD_K), lambda i: (nb - 1 - i, 0, 0, 0)),
                  rev(HEADS * CHUNK, 0)],
        out_specs=(rev(KEY_W, 0), rev(KEY_W, 0), rev(D, 0), rev(KEY_W, 0)),
        scratch_shapes=[pltpu.VMEM((HEADS, HEAD_V, HEAD_K), F32)],
        compiler_params=_params("arbitrary"),
    )(qk, qk, v, cum, do, states, scores)


def head_and_loss(o, gate, h1, target, hw, wgo, wf):
    s = o.shape[0]
    ts = ROW_TILE

    def body(o_ref, gate_ref, h1_ref, tgt_ref, hw_ref, wgo_ref, wf_ref,
             dh2_ref, do_ref, dgate_ref, ggo_ref, small_ref):
        @pl.when(pl.program_id(0) == 0)
        def _():
            ggo_ref[...] = jnp.zeros_like(ggo_ref)
            small_ref[...] = jnp.zeros_like(small_ref)

        gate = gate_ref[...]
        hw = hw_ref[...]
        sg = _sigmoid(gate)
        silu = gate * sg
        ohat, ro = [], []
        for h in range(HEADS):
            oh = o_ref[:, h * HEAD_V:(h + 1) * HEAD_V]
            rh = lax.rsqrt(jnp.mean(oh * oh, axis=-1, keepdims=True) + EPS)
            ro.append(rh)
            ohat.append(oh * rh)
        ohat = jnp.concatenate(ohat, axis=-1)
        on = ohat * hw
        y2 = _bf(on * silu)
        h2 = h1_ref[...] + _nn(y2, wgo_ref[...])
        rf = lax.rsqrt(jnp.mean(h2 * h2, axis=-1, keepdims=True) + EPS)
        h2hat = h2 * rf
        wf = wf_ref[...]
        diff = h2hat * wf - tgt_ref[...]
        small_ref[2:3, :] += jnp.zeros((1, D), F32) + 0.5 * jnp.sum(diff * diff) / D
        dout = diff / D
        small_ref[0:1, :] += jnp.sum(dout * h2hat, axis=0, keepdims=True)
        dxh = dout * wf
        dh2 = rf * (dxh - h2hat * jnp.mean(dxh * h2hat, axis=-1, keepdims=True))
        dh2_ref[...] = dh2
        dh2_bf = _bf(dh2)
        ggo_ref[...] += _tn(y2, dh2_bf)
        dy2 = _nt(dh2_bf, wgo_ref[...])
        don = dy2 * silu
        dgate_ref[...] = _bf(dy2 * on * (sg * (1.0 + gate * (1.0 - sg))))
        ghw = jnp.sum(don * ohat, axis=0, keepdims=True)
        small_ref[1:2, 0:HEAD_V] += sum(ghw[:, h * HEAD_V:(h + 1) * HEAD_V] for h in range(HEADS))
        dohat = don * hw
        for h in range(HEADS):
            cols = slice(h * HEAD_V, (h + 1) * HEAD_V)
            oh, dh = ohat[:, cols], dohat[:, cols]
            do_ref[:, cols] = _bf(ro[h] * (dh - oh * jnp.mean(dh * oh, axis=-1, keepdims=True)))

    row = lambda cols: pl.BlockSpec((ts, cols), lambda i: (i, 0))
    act = jax.ShapeDtypeStruct((s, D), F32)
    act_bf = jax.ShapeDtypeStruct((s, D), BF16)
    return pl.pallas_call(
        body, name="head_and_loss", grid=(s // ts,),
        out_shape=(act, act_bf, act_bf, jax.ShapeDtypeStruct((D, D), F32), jax.ShapeDtypeStruct((8, D), F32)),
        in_specs=[row(D), row(D), row(D), row(D),
                  _full((1, D)), _full((D, D)), _full((1, D))],
        out_specs=(row(D), row(D), row(D), _full((D, D)), _full((8, D))),
        compiler_params=_params("arbitrary"),
    )(o, gate, h1, target, hw, wgo, wf)


def gla_project_backward(dq, dk, dv, dgate, dcum, low, h1, dh2, w1, wgi, wlow, wgk, bgk):
    s = h1.shape[0]
    ts = ROW_TILE

    def body(dq_ref, dk_ref, dv_ref, dgate_ref, dcum_ref, low_ref, h1_ref, dh2_ref, w1_ref,
             wgi_ref, wlow_ref, wgk_ref, bgk_ref, dh1_ref, dproj_ref, dlow_ref, ggk_ref, small_ref):
        @pl.when(pl.program_id(0) == 0)
        def _():
            ggk_ref[...] = jnp.zeros_like(ggk_ref)
            small_ref[...] = jnp.zeros_like(small_ref)

        low = low_ref[...]
        z = _nn(low, wgk_ref[...]) + bgk_ref[...]
        upper_f = _chunk_masks()[1].astype(F32)
        dlg = jnp.concatenate([_nn_exact(upper_f, dcum_ref[r0:r0 + CHUNK, :]) for r0 in range(0, ts, CHUNK)],
                              axis=0)
        dz = dlg * (1.0 / GATE_NORM) * _sigmoid(-z)
        dz_bf = _bf(dz)
        ggk_ref[...] += _tn(low, dz_bf)
        small_ref[1:2, 0:KEY_W] += jnp.sum(dz, axis=0, keepdims=True)
        dlow = _bf(_nt(dz_bf, wgk_ref[...]))
        dlow_ref[...] = dlow
        dn1 = _nt(dlow, wlow_ref[...])
        for ref, lo, hi in ((dq_ref, 0, KEY_W), (dk_ref, KEY_W, 2 * KEY_W),
                            (dv_ref, 2 * KEY_W, 2 * KEY_W + D), (dgate_ref, 2 * KEY_W + D, GLA_MAIN)):
            piece = ref[...]
            dproj_ref[:, lo:hi] = piece
            dn1 = dn1 + _nt(piece, wgi_ref[:, lo:hi])
        hv = h1_ref[...]
        r = lax.rsqrt(jnp.mean(hv * hv, axis=-1, keepdims=True) + EPS)
        hhat = hv * r
        small_ref[0:1, :] += jnp.sum(dn1 * hhat, axis=0, keepdims=True)
        dxh = dn1 * w1_ref[...]
        dh1_ref[...] = dh2_ref[...] + r * (dxh - hhat * jnp.mean(dxh * hhat, axis=-1, keepdims=True))

    row = lambda cols: pl.BlockSpec((ts, cols), lambda i: (i, 0))
    return pl.pallas_call(
        body, name="gla_project_backward", grid=(s // ts,),
        out_shape=(jax.ShapeDtypeStruct((s, D), F32), jax.ShapeDtypeStruct((s, GLA_MAIN), BF16),
                   jax.ShapeDtypeStruct((s, RANK_PAD), BF16), jax.ShapeDtypeStruct((RANK_PAD, KEY_W), F32),
                   jax.ShapeDtypeStruct((8, D), F32)),
        in_specs=[row(KEY_W), row(KEY_W), row(D), row(D), row(KEY_W), row(RANK_PAD), row(D), row(D),
                  _full((1, D)), _full((D, GLA_MAIN)), _full((D, RANK_PAD)), _full((RANK_PAD, KEY_W)),
                  _full((1, KEY_W))],
        out_specs=(row(D), row(GLA_MAIN), row(RANK_PAD), _full((RANK_PAD, KEY_W)), _full((8, D))),
        compiler_params=_params("arbitrary"),
    )(dq, dk, dv, dgate, dcum, low, h1, dh2, w1, wgi, wlow, wgk, bgk)


def _groups_from_quarters(a):
    return a.reshape(N_CHIPS, GROUPS, 64, GROUP_DIM).transpose(1, 0, 2, 3).reshape(GROUPS, GROUP_DIM, GROUP_DIM)


def _quarters_from_groups(a):
    return a.reshape(GROUPS, N_CHIPS, 64, GROUP_DIM).transpose(1, 0, 2, 3).reshape(N_CHIPS, GROUP_DIM, GROUP_DIM)


def _gla_in_weights(wgi_q):
    wgi_all = jnp.concatenate([wgi_q[q] for q in range(N_CHIPS)], axis=1)
    wlow = jnp.pad(wgi_all[:, GLA_MAIN:], ((0, 0), (0, RANK_PAD - GATE_RANK)))
    return wgi_all, wlow


def local_gradients(xs, target, w0, w1, wf, wpi, gw, gb, scale, wpo, gla_quarters, wgk, bgk, hw_tiled, place):
    wgi_q, wgo_q = gla_quarters
    (h1, pooled, gt, n0), (wgi_q,) = pool_forward(xs, w0, wpi, gw, gb, scale, wpo, [wgi_q])
    wgi, wlow = _gla_in_weights(wgi_q)
    (qk, v, gate, low, cum, n1), (wgo_q,) = gla_project(h1, w1, wgi, wlow, wgk, bgk, [wgo_q])
    wgo = wgo_q.reshape(D, D)
    o, states, scores = gla_forward(qk, v, cum)

    dh2, do, dgate, g_gla_out, small_top = head_and_loss(o, gate, h1, target, hw_tiled, wgo, wf)
    dq, dk, dv, dcum = gla_backward(qk, v, cum, do, states, scores)
    dh1, dproj, dlow, g_gk_pad, small_gla = gla_project_backward(
        dq, dk, dv, dgate, dcum, low, h1, dh2, w1, wgi, wlow, wgk, bgk)
    g_gla_main, g_gla_low = matmul_tn(n1, dproj, "grad_gla_in", narrow=dlow)
    g_gla_in = jnp.concatenate([g_gla_main, g_gla_low[:, :GATE_RANK]], axis=1)

    def chip_sums(grads, names, tag):
        theirs = exchange_with_sibling(grads, "exchange_with_sibling_" + tag)
        return [add_halves(g, t, place, "add_halves_" + n) for g, t, n in zip(grads, theirs, names)]

    gla_names = ("gla_in", "gla_out")
    gla_sums = chip_sums(
        [jnp.stack([g_gla_in[:, GLA_IN_QUARTER * q:GLA_IN_QUARTER * (q + 1)] for q in range(N_CHIPS)]),
         g_gla_out.reshape(N_CHIPS, D // N_CHIPS, D)], gla_names, "gla")
    (dx, dpool, g_pool_out, g_group_w, small_pool), gla_got = pool_backward(
        xs, dh1, pooled, gt, w0, wpi, gw, gb, scale, wpo, [b for _, b in gla_sums])
    g_pool_in = matmul_tn(n0, dpool, "grad_pool_in", by_column_tile=True)

    pool_names = ("pool_in", "group", "pool_out")
    pool_sums = chip_sums(
        [g_pool_in, _quarters_from_groups(g_group_w), g_pool_out.reshape(N_CHIPS, D // N_CHIPS, D)],
        pool_names, "pool")
    pool_got = scatter_to_owners([b for _, b in pool_sums], "scatter_to_owners_pool")
    reduced, total = join_halves(
        [add_parts(f, g, place, "add_parts_" + n) for (f, _), g, n in
         zip(pool_sums + gla_sums, list(pool_got) + list(gla_got), pool_names + gla_names)],
        small_pool, small_gla, small_top, g_gk_pad)
    return dx, reduced, total


def kernel(x, norm_w, pool_in_w, pool_group_w, pool_group_b, pool_scale, pool_out_w, gla_in_w, gla_gk_w, gla_gk_b, gla_head_norm_w, gla_out_w, final_norm_w, loss_target, m_norm_w, m_pool_in_w, m_pool_group_w, m_pool_group_b, m_pool_scale, m_pool_out_w, m_gla_in_w, m_gla_gk_w, m_gla_gk_b, m_gla_head_norm_w, m_gla_out_w, m_final_norm_w, v_norm_w, v_pool_in_w, v_pool_group_w, v_pool_group_b, v_pool_scale, v_pool_out_w, v_gla_in_w, v_gla_gk_w, v_gla_gk_b, v_gla_head_norm_w, v_gla_out_w, v_final_norm_w):
    s = x.shape[1]
    xs = x[0]
    target = loss_target[0]
    q_chip = 2 * lax.axis_index("x") + lax.axis_index("y")
    place = jnp.stack([lax.axis_index("c"), q_chip]).astype(jnp.int32)

    (wpi, gw_q, wpo_q, wgi_q, wgo_q), small_all = allgather_weights(
        [pool_in_w[0], pool_group_w[0].reshape(GROUP_DIM, GROUP_DIM), pool_out_w[0], gla_in_w[0], gla_out_w[0]],
        exchange=(True, True, True, False, False),
        smalls=[gla_gk_b, gla_head_norm_w, pool_group_b[0], gla_gk_w[0]])
    gw = _groups_from_quarters(gw_q)
    wpo = wpo_q.reshape(D, D)
    small_all = small_all[0::2]
    bgk = small_all[:, 0, :].reshape(1, KEY_W)
    hw = small_all[:, 1, 0:64].reshape(1, HEAD_V)
    gb = small_all[:, 2:2 + GROUPS, 0:64].transpose(1, 0, 2).reshape(1, D)
    wgk16 = small_all[:, 8:8 + GATE_RANK, :].transpose(1, 0, 2).reshape(GATE_RANK, KEY_W)
    wgk = _bf(jnp.pad(wgk16, ((0, RANK_PAD - GATE_RANK), (0, 0))))
    hw_tiled = jnp.tile(hw, (1, HEADS))

    w0 = norm_w[0:1]
    w1 = norm_w[1:2]
    wf = final_norm_w.reshape(1, D)

    dx, reduced, total = local_gradients(
        xs, target, w0, w1, wf, wpi, gw, gb, pool_scale, wpo, [wgi_q, wgo_q], wgk, bgk, hw_tiled, place)
    r_pool_in, r_group_w, r_pool_out, r_gla_in, r_gla_out = reduced
    r_group_w = r_group_w.reshape(GROUPS, 64, GROUP_DIM)

    loss = total[7, 0]
    g_norm = jnp.stack([total[0], total[3]])
    g_scale = total[1:2]
    g_final = total[5]
    pick = lambda full, width: lax.dynamic_slice_in_dim(full, q_chip * width, width, axis=-1)
    g_gk_b = pick(total[4:5, 0:KEY_W], 128)
    g_hnw = pick(total[6:7, 0:HEAD_V], 64)
    g_group_b = pick(total[2].reshape(GROUPS, GROUP_DIM), 64)[None]
    g_gk_w = pick(total[8:16].reshape(GATE_RANK, KEY_W), 128)[None]

    def step_lane_rows(name, w, g, m, v):
        turn = lambda a: jnp.transpose(a, (2, 0, 1))
        back = lambda a: jnp.transpose(a, (1, 2, 0))
        g_t = turn(g)
        d, nm, nv = adamw_rows(turn(w), g_t, turn(m), turn(v), "adamw_" + name)
        return back(g_t), back(d), back(nm), back(nv)

    def step(name, w, g, m, v):
        shape = w.shape
        as2d = lambda a: a.reshape(-1, shape[-1])
        d, nm, nv = adamw(as2d(w), as2d(g), as2d(m), as2d(v), "adamw_" + name)
        return g.reshape(shape), d.reshape(shape), nm.reshape(shape), nv.reshape(shape)

    small_names = ("norm_w", "pool_group_b", "pool_scale", "gla_gk_w", "gla_gk_b", "gla_head_norm_w",
                   "final_norm_w")
    small_args = [(norm_w, g_norm, m_norm_w, v_norm_w),
                  (pool_group_b, g_group_b, m_pool_group_b, v_pool_group_b),
                  (pool_scale, g_scale, m_pool_scale, v_pool_scale),
                  (gla_gk_w, g_gk_w, m_gla_gk_w, v_gla_gk_w),
                  (gla_gk_b, g_gk_b, m_gla_gk_b, v_gla_gk_b),
                  (gla_head_norm_w, g_hnw, m_gla_head_norm_w, v_gla_head_norm_w),
                  (final_norm_w, g_final, m_final_norm_w, v_final_norm_w)]
    as2d = lambda a, w: a.reshape(-1, w.shape[-1])
    small_out = adamw_small([tuple(as2d(a, p[0]) for a in p) for p in small_args])
    small = {n: (p[1].reshape(p[0].shape),) + tuple(o.reshape(p[0].shape) for o in out)
             for n, p, out in zip(small_names, small_args, small_out)}
    results = [
        small["norm_w"],
        step("pool_in_w", pool_in_w, r_pool_in[None], m_pool_in_w, v_pool_in_w),
        step("pool_group_w", pool_group_w, r_group_w[None], m_pool_group_w, v_pool_group_w),
        small["pool_group_b"],
        small["pool_scale"],
        step("pool_out_w", pool_out_w, r_pool_out[None], m_pool_out_w, v_pool_out_w),
        step_lane_rows("gla_in_w", gla_in_w, r_gla_in[None], m_gla_in_w, v_gla_in_w),
        small["gla_gk_w"],
        small["gla_gk_b"],
        small["gla_head_norm_w"],
        step("gla_out_w", gla_out_w, r_gla_out[None], m_gla_out_w, v_gla_out_w),
        small["final_norm_w"],
    ]
    grads, deltas, new_m, new_v = zip(*results)
    return (loss, dx[None], *grads, *deltas, *new_m, *new_v)
```

```python
import functools

import jax
import jax.numpy as jnp
from jax import lax
from jax.experimental import pallas as pl
from jax.experimental.pallas import tpu as pltpu

F32 = jnp.float32
BF16 = jnp.bfloat16
MESH = pl.DeviceIdType.MESH

D = 1024
POOL_WINDOWS = (2, 4, 8, 16)
GROUPS = 4
GROUP_DIM = 256
HEADS = 4
HEAD_K = 128
HEAD_V = 256
KEY_W = 512
CHUNK = 64
GATE_RANK = 16
GATE_NORM = 16.0
GLA_IN = 3088
GLA_MAIN = 3072
RANK_PAD = 128
EPS = 1e-6
HALO = 32

ADAM_LR = 0.001
ADAM_B1 = 0.9
ADAM_B2 = 0.999
ADAM_EPS = 1e-08
ADAM_WD = 0.01
ADAM_STEP = 10

N_CHIPS = 4
N_DEV = 8
GLA_IN_QUARTER = GLA_IN // N_CHIPS

VMEM_LIMIT = 56 * 1024 * 1024


def _nn(a, b):
    return lax.dot_general(a, b, (((1,), (0,)), ((), ())), preferred_element_type=F32)


def _nt(a, b):
    return lax.dot_general(a, b, (((1,), (1,)), ((), ())), preferred_element_type=F32)


def _tn(a, b):
    return lax.dot_general(a, b, (((0,), (0,)), ((), ())), preferred_element_type=F32)


def _nn_exact(a, b):
    return lax.dot_general(a, b, (((1,), (0,)), ((), ())), preferred_element_type=F32,
                           precision=lax.Precision.HIGHEST)


def _bf(a):
    return a.astype(BF16)


def _params(*sem):
    return pltpu.CompilerParams(dimension_semantics=sem, vmem_limit_bytes=VMEM_LIMIT)


def _full(shape):
    return pl.BlockSpec(shape, lambda i: (0,) * len(shape))


def _position():
    return lax.axis_index("x"), lax.axis_index("y"), lax.axis_index("c")


def _gather_small(in_ref, all_ref, send_sems, recv_sems, local_sem):
    x, y, c = _position()
    me = 4 * x + 2 * y + c
    mine = pltpu.make_async_copy(in_ref, all_ref.at[me], local_sem)
    mine.start()
    sends = []
    for k in range(N_DEV - 1):
        fx, fy, fc = (k + 1) >> 2 & 1, (k + 1) >> 1 & 1, (k + 1) & 1
        cp = pltpu.make_async_remote_copy(
            src_ref=in_ref, dst_ref=all_ref.at[me],
            send_sem=send_sems.at[k], recv_sem=recv_sems.at[k],
            device_id=(x ^ fx, y ^ fy, c ^ fc), device_id_type=MESH)
        cp.start()
        sends.append(cp)
    for k in range(N_DEV - 1):
        fx, fy, fc = (k + 1) >> 2 & 1, (k + 1) >> 1 & 1, (k + 1) & 1
        src_dev = 4 * (x ^ fx) + 2 * (y ^ fy) + (c ^ fc)
        pltpu.make_async_remote_copy(
            src_ref=in_ref, dst_ref=all_ref.at[src_dev],
            send_sem=send_sems.at[k], recv_sem=recv_sems.at[k],
            device_id=(x, y, c), device_id_type=MESH).wait_recv()
    for cp in sends:
        cp.wait_send()
    mine.wait()


SMALL_SEMS = [pltpu.SemaphoreType.DMA((N_DEV - 1,)), pltpu.SemaphoreType.DMA((N_DEV - 1,)),
              pltpu.SemaphoreType.DMA]
VMEM_SPEC = pl.BlockSpec(memory_space=pltpu.VMEM)


def _other_chips(x, y):
    return [(1 - x, y), (x, 1 - y), (1 - x, 1 - y)]


def _any_specs(n):
    return [pl.BlockSpec(memory_space=pl.ANY)] * n


def _halves(rows, c):
    half = rows // 2
    return pl.ds(c * half, half), pl.ds((1 - c) * half, half)


CAST_ROWS = 256


def _gather_copy(out_ref, send_sems, recv_sems, k, quarter, half, to, src=None):
    dst = out_ref.at[quarter, half]
    return pltpu.make_async_remote_copy(
        src_ref=dst if src is None else src, dst_ref=dst,
        send_sem=send_sems.at[k], recv_sem=recv_sems.at[k], device_id=to, device_id_type=MESH)


SMALL_IN_ROWS = 24


def allgather_weights(quarters, exchange, smalls):
    n = len(quarters)
    shapes = [w.shape for w in quarters]
    moved = [i for i in range(n) if exchange[i]]

    def body(*refs):
        w_refs, (gkb_ref, hnw_ref, gb_ref, gkw_ref) = refs[:n], refs[n:n + 4]
        out_refs, small_all_ref = refs[n + 4:2 * n + 4], refs[2 * n + 4]
        refs = refs[2 * n + 5:]
        f32_bufs, bf_bufs = refs[:n], refs[n:2 * n]
        send_sems, recv_sems, local_sems, small_ref = refs[2 * n:2 * n + 4]
        small_ref[...] = jnp.zeros_like(small_ref)
        small_ref[0:1, :] = gkb_ref[...]
        small_ref[1:2, 0:64] = hnw_ref[...]
        small_ref[2:2 + GROUPS, 0:64] = gb_ref[...]
        small_ref[8:8 + GATE_RANK, :] = gkw_ref[...]
        x, y, c = _position()
        q = 2 * x + y
        sibling = (x, y, 1 - c)
        chips = _other_chips(x, y)

        def copy(k, i, quarter, half, to, src=None):
            return _gather_copy(out_refs[i], send_sems, recv_sems, k * n + i, quarter, half, to, src)

        loads = [pltpu.make_async_copy(w_refs[i], f32_bufs[i], local_sems.at[i]) for i in range(n)]
        for cp in loads:
            cp.start()
        keeps, sends = [], []
        for i in range(n):
            loads[i].wait()
            for r0 in range(0, shapes[i][0], CAST_ROWS):
                bf_bufs[i][r0:r0 + CAST_ROWS, :] = _bf(f32_bufs[i][r0:r0 + CAST_ROWS, :])
            keep = pltpu.make_async_copy(bf_bufs[i], out_refs[i].at[q], local_sems.at[n + i])
            keep.start()
            keeps.append(keep)
            if not exchange[i]:
                continue
            mine, _ = _halves(shapes[i][0], c)
            for j, chip in enumerate(chips):
                cp = copy(j, i, q, mine, (*chip, c), src=bf_bufs[i].at[mine])
                cp.start()
                sends.append(cp)
        for j, chip in enumerate(chips):
            qj = 2 * chip[0] + chip[1]
            for i in moved:
                mine, _ = _halves(shapes[i][0], c)
                copy(j, i, qj, mine, (x, y, c)).wait_recv()
                cp = copy(3 + j, i, qj, mine, sibling)
                cp.start()
                sends.append(cp)
        for j, chip in enumerate(chips):
            qj = 2 * chip[0] + chip[1]
            for i in moved:
                _, other = _halves(shapes[i][0], c)
                copy(3 + j, i, qj, other, (x, y, c)).wait_recv()
        _gather_small(small_ref, small_all_ref, *refs[2 * n + 4:])
        for cp in sends:
            cp.wait_send()
        for cp in keeps:
            cp.wait()

    outs = pl.pallas_call(
        body, name="allgather_weights",
        out_shape=[jax.ShapeDtypeStruct((N_CHIPS, *s), BF16) for s in shapes]
                  + [jax.ShapeDtypeStruct((N_DEV, SMALL_IN_ROWS, 128), F32)],
        in_specs=_any_specs(n) + [VMEM_SPEC] * 4, out_specs=_any_specs(n) + [VMEM_SPEC],
        scratch_shapes=([pltpu.VMEM(s, F32) for s in shapes] + [pltpu.VMEM(s, BF16) for s in shapes]
                        + [pltpu.SemaphoreType.DMA((6 * n,)), pltpu.SemaphoreType.DMA((6 * n,)),
                           pltpu.SemaphoreType.DMA((2 * n,)), pltpu.VMEM((SMALL_IN_ROWS, 128), F32)] + SMALL_SEMS),
        compiler_params=pltpu.CompilerParams(vmem_limit_bytes=VMEM_LIMIT),
    )(*quarters, *smalls)
    return outs[:n], outs[n]


def exchange_with_sibling(grads, name):
    n = len(grads)

    def body(*refs):
        g_refs, theirs_refs = refs[:n], refs[n:2 * n]
        send_sems, recv_sems = refs[2 * n:]
        x, y, c = _position()
        copies = []
        for i in range(n):
            _, other = _halves(g_refs[i].shape[1], c)
            cp = pltpu.make_async_remote_copy(
                src_ref=g_refs[i].at[:, other], dst_ref=theirs_refs[i],
                send_sem=send_sems.at[i], recv_sem=recv_sems.at[i],
                device_id=(x, y, 1 - c), device_id_type=MESH)
            cp.start()
            copies.append(cp)
        for cp in copies:
            cp.wait()

    return pl.pallas_call(
        body, name=name,
        out_shape=[jax.ShapeDtypeStruct((N_CHIPS, g.shape[1] // 2, g.shape[2]), F32) for g in grads],
        in_specs=_any_specs(n), out_specs=_any_specs(n),
        scratch_shapes=[pltpu.SemaphoreType.DMA((n,)), pltpu.SemaphoreType.DMA((n,))],
    )(*grads)


def _scatter_copies(b_refs, got_refs, send_sems, recv_sems):
    n = len(b_refs)
    x, y, c = _position()
    copies = []
    for j, chip in enumerate(_other_chips(x, y)):
        qj = 2 * chip[0] + chip[1]
        for i in range(n):
            copies.append(pltpu.make_async_remote_copy(
                src_ref=b_refs[i].at[qj], dst_ref=got_refs[i].at[j],
                send_sem=send_sems.at[j * n + i], recv_sem=recv_sems.at[j * n + i],
                device_id=(*chip, c), device_id_type=MESH))
    return copies


def _scatter_shapes(chip_sums):
    return [jax.ShapeDtypeStruct((N_CHIPS - 1, *b.shape[1:]), BF16) for b in chip_sums]


def scatter_to_owners(chip_sums, name):
    n = len(chip_sums)

    def body(*refs):
        copies = _scatter_copies(refs[:n], refs[n:2 * n], *refs[2 * n:])
        for cp in copies:
            cp.start()
        for cp in copies:
            cp.wait()

    return pl.pallas_call(
        body, name=name,
        out_shape=_scatter_shapes(chip_sums),
        in_specs=_any_specs(n), out_specs=_any_specs(n),
        scratch_shapes=[pltpu.SemaphoreType.DMA((3 * n,)), pltpu.SemaphoreType.DMA((3 * n,))],
    )(*chip_sums)


SMALL_SUM_ROWS = 16


def join_halves(reduced, small_pool, small_gla, small_top, g_gk_pad):
    n = len(reduced)

    def body(*refs):
        pool_ref, gla_ref, top_ref, gk_ref = refs[n:n + 4]
        buf_refs, total_ref = refs[n + 4:2 * n + 4], refs[2 * n + 4]
        send_sems, recv_sems, all_ref, small_ref = refs[2 * n + 5:2 * n + 9]
        small_ref[0:3, :] = pool_ref[0:3, :]
        small_ref[3:5, :] = gla_ref[0:2, :]
        small_ref[5:8, :] = top_ref[0:3, :]
        for r in range(GATE_RANK):
            small_ref[8 + r // 2:9 + r // 2, (r % 2) * KEY_W:(r % 2 + 1) * KEY_W] = gk_ref[r:r + 1, :]
        x, y, c = _position()
        copies = []
        for i in range(n):
            mine, _ = _halves(buf_refs[i].shape[0], c)
            cp = pltpu.make_async_remote_copy(
                src_ref=buf_refs[i].at[mine], dst_ref=buf_refs[i].at[mine],
                send_sem=send_sems.at[i], recv_sem=recv_sems.at[i],
                device_id=(x, y, 1 - c), device_id_type=MESH)
            cp.start()
            copies.append(cp)
        _gather_small(small_ref, all_ref, *refs[2 * n + 9:])
        total = all_ref[0]
        for dev in range(1, N_DEV):
            total = total + all_ref[dev]
        total_ref[...] = total
        for cp in copies:
            cp.wait()

    outs = pl.pallas_call(
        body, name="join_halves",
        out_shape=[jax.ShapeDtypeStruct(r.shape, F32) for r in reduced]
                  + [jax.ShapeDtypeStruct((SMALL_SUM_ROWS, D), F32)],
        in_specs=_any_specs(n) + [VMEM_SPEC] * 4, out_specs=_any_specs(n) + [VMEM_SPEC],
        input_output_aliases={i: i for i in range(n)},
        scratch_shapes=[pltpu.SemaphoreType.DMA((n,)), pltpu.SemaphoreType.DMA((n,)),
                        pltpu.VMEM((N_DEV, SMALL_SUM_ROWS, D), F32), pltpu.VMEM((SMALL_SUM_ROWS, D), F32)]
                       + SMALL_SEMS,
    )(*reduced, small_pool, small_gla, small_top, g_gk_pad)
    return outs[:n], outs[n]


ADD_ROWS = 512


def _spans(counts):
    starts, total = [], 0
    for count in counts:
        starts.append(total)
        total += count
    return starts, total


def _local_step(t, start, count):
    return jnp.clip(t - start, 0, count - 1)


def add_halves(grads, theirs, place, name):
    n = len(grads)
    shapes = [t.shape for t in theirs]
    rbs = [min(ADD_ROWS, sh[1]) for sh in shapes]
    steps = [sh[1] // rb for sh, rb in zip(shapes, rbs)]
    counts = [N_CHIPS * st for st in steps]
    starts, total = _spans(counts)

    def body(place_ref, *refs):
        a_refs, b_refs = refs[:n], refs[n:2 * n]
        f_refs, h_refs = refs[2 * n:3 * n], refs[3 * n:4 * n]
        t = pl.program_id(0)
        for i in range(n):
            @pl.when((t >= starts[i]) & (t < starts[i] + counts[i]))
            def _(i=i):
                total_i = a_refs[i][0] + b_refs[i][0]
                h_refs[i][0] = _bf(total_i)

                @pl.when((t - starts[i]) % N_CHIPS == place_ref[1])
                def _():
                    f_refs[i][...] = total_i

    def specs(i):
        block = (1, rbs[i], shapes[i][2])

        def split(t):
            local = _local_step(t, starts[i], counts[i])
            return local // N_CHIPS, local % N_CHIPS

        mine = pl.BlockSpec(block, lambda t, place: (split(t)[1], place[0] * steps[i] + split(t)[0], 0))
        same = pl.BlockSpec(block, lambda t, place: (split(t)[1], split(t)[0], 0))
        own = pl.BlockSpec(block[1:], lambda t, place: (split(t)[0], 0))
        return mine, same, own

    all_specs = [specs(i) for i in range(n)]
    outs = pl.pallas_call(
        body, name=name,
        grid_spec=pltpu.PrefetchScalarGridSpec(
            num_scalar_prefetch=1, grid=(total,),
            in_specs=[sp[0] for sp in all_specs] + [sp[1] for sp in all_specs],
            out_specs=[sp[2] for sp in all_specs] + [sp[1] for sp in all_specs]),
        out_shape=[jax.ShapeDtypeStruct(sh[1:], F32) for sh in shapes]
                  + [jax.ShapeDtypeStruct(sh, BF16) for sh in shapes],
        compiler_params=_params("arbitrary"),
    )(place, *grads, *theirs)
    return list(zip(outs[:n], outs[n:]))


def add_parts(owns, gots, place, name):
    n = len(owns)
    shapes = [g.shape for g in gots]
    rbs = [min(ADD_ROWS, sh[1]) for sh in shapes]
    counts = [sh[1] // rb for sh, rb in zip(shapes, rbs)]
    starts, total = _spans(counts)

    def body(place_ref, *refs):
        o_refs, g_refs, out_refs = refs[:n], refs[n:2 * n], refs[2 * n:]
        t = pl.program_id(0)
        for i in range(n):
            @pl.when((t >= starts[i]) & (t < starts[i] + counts[i]))
            def _(i=i):
                total_i = o_refs[i][...]
                for j in range(N_CHIPS - 1):
                    total_i = total_i + g_refs[i][j].astype(F32)
                out_refs[i][...] = total_i

    def specs(i):
        rb, cols = rbs[i], shapes[i][2]
        step = lambda t: _local_step(t, starts[i], counts[i])
        return (pl.BlockSpec((rb, cols), lambda t, place: (step(t), 0)),
                pl.BlockSpec((N_CHIPS - 1, rb, cols), lambda t, place: (0, step(t), 0)),
                pl.BlockSpec((rb, cols), lambda t, place: (place[0] * counts[i] + step(t), 0)))

    all_specs = [specs(i) for i in range(n)]
    return pl.pallas_call(
        body, name=name,
        grid_spec=pltpu.PrefetchScalarGridSpec(
            num_scalar_prefetch=1, grid=(total,),
            in_specs=[sp[0] for sp in all_specs] + [sp[1] for sp in all_specs],
            out_specs=[sp[2] for sp in all_specs]),
        out_shape=[jax.ShapeDtypeStruct((2 * sh[1], sh[2]), F32) for sh in shapes],
        compiler_params=_params("arbitrary"),
    )(place, *owns, *gots)


def _adam_math(w, g, m, v):
    m = ADAM_B1 * m + (1.0 - ADAM_B1) * g
    v = ADAM_B2 * v + (1.0 - ADAM_B2) * (g * g)
    m_hat = m / (1.0 - ADAM_B1 ** ADAM_STEP)
    v_hat = v / (1.0 - ADAM_B2 ** ADAM_STEP)
    delta = -ADAM_LR * (m_hat / (jnp.sqrt(v_hat) + ADAM_EPS) + ADAM_WD * w)
    return delta, m, v


ADAM_BLOCK_BYTES = 2 ** 19
ADAM_MOST_STEPS = 8


def adamw(params, name):
    n = len(params)
    shapes = [p[0].shape for p in params]

    def tile_rows(shape):
        rows, cols = shape[0], shape[-1]
        aligned = 1 if len(shape) == 3 else 8
        divisors = [t for t in range(aligned, rows + 1, aligned) if rows % t == 0]
        tile = max(t for t in divisors if t * cols * 4 <= ADAM_BLOCK_BYTES)
        if rows // tile > ADAM_MOST_STEPS:
            tile = min(t for t in divisors if rows // t <= ADAM_MOST_STEPS)
        return tile

    tiles = [tile_rows(sh) for sh in shapes]
    counts = [sh[0] // tl for sh, tl in zip(shapes, tiles)]
    starts, total = _spans(counts)

    def body(*refs):
        ins, outs = refs[:4 * n], refs[4 * n:]
        t = pl.program_id(0)
        for i in range(n):
            @pl.when((t >= starts[i]) & (t < starts[i] + counts[i]))
            def _(i=i):
                w_ref, g_ref, m_ref, v_ref = ins[4 * i:4 * i + 4]
                d, nm, nv = _adam_math(w_ref[...], g_ref[...], m_ref[...], v_ref[...])
                outs[3 * i][...] = d
                outs[3 * i + 1][...] = nm
                outs[3 * i + 2][...] = nv

    def spec(i):
        block = (tiles[i],) + shapes[i][1:]
        zeros = (0,) * (len(block) - 1)
        return pl.BlockSpec(block, lambda t: (_local_step(t, starts[i], counts[i]),) + zeros)

    outs = pl.pallas_call(
        body, name=name, grid=(total,),
        out_shape=[jax.ShapeDtypeStruct(sh, F32) for sh in shapes for _ in range(3)],
        in_specs=[spec(i) for i in range(n) for _ in range(4)],
        out_specs=[spec(i) for i in range(n) for _ in range(3)],
        compiler_params=_params("arbitrary"),
    )(*[a for p in params for a in p])
    return [tuple(outs[3 * i:3 * i + 3]) for i in range(n)]


def adamw_small(params):
    n = len(params)

    def body(*refs):
        ins, outs = refs[:4 * n], refs[4 * n:]
        for k in range(n):
            w_ref, g_ref, m_ref, v_ref = ins[4 * k:4 * k + 4]
            d, nm, nv = _adam_math(w_ref[...], g_ref[...], m_ref[...], v_ref[...])
            outs[3 * k][...] = d
            outs[3 * k + 1][...] = nm
            outs[3 * k + 2][...] = nv

    flat = [a for p in params for a in p]
    outs = pl.pallas_call(
        body, name="adamw_small",
        out_shape=[jax.ShapeDtypeStruct(p[0].shape, F32) for p in params for _ in range(3)],
        in_specs=[VMEM_SPEC] * (4 * n), out_specs=[VMEM_SPEC] * (3 * n),
    )(*flat)
    return [tuple(outs[3 * k:3 * k + 3]) for k in range(n)]


def matmul_tn(a, b, name, tile_n=512, by_column_tile=False, narrow=None):
    s, m = a.shape
    n = b.shape[1]
    tile_n = min(tile_n, n)
    if by_column_tile:
        out_shape = jax.ShapeDtypeStruct((n // tile_n, m, tile_n), F32)
        out_spec = pl.BlockSpec((None, m, tile_n), lambda j: (j, 0, 0))
    else:
        out_shape = jax.ShapeDtypeStruct((m, n), F32)
        out_spec = pl.BlockSpec((m, tile_n), lambda j: (0, j))

    if narrow is None:
        def body(a_ref, b_ref, out_ref):
            out_ref[...] = _tn(a_ref[...], b_ref[...])

        return pl.pallas_call(
            body, name=name, grid=(n // tile_n,),
            out_shape=out_shape,
            in_specs=[_full((s, m)), pl.BlockSpec((s, tile_n), lambda j: (0, j))],
            out_specs=out_spec,
            compiler_params=_params("parallel"),
        )(a, b)

    def body_with_narrow(a_ref, b_ref, c_ref, out_ref, out_c_ref):
        out_ref[...] = _tn(a_ref[...], b_ref[...])

        @pl.when(pl.program_id(0) == 0)
        def _():
            out_c_ref[...] = _tn(a_ref[...], c_ref[...])

    return pl.pallas_call(
        body_with_narrow, name=name, grid=(n // tile_n,),
        out_shape=(out_shape, jax.ShapeDtypeStruct((m, narrow.shape[1]), F32)),
        in_specs=[_full((s, m)), pl.BlockSpec((s, tile_n), lambda j: (0, j)), _full(narrow.shape)],
        out_specs=(out_spec, _full((m, narrow.shape[1]))),
        compiler_params=_params("arbitrary"),
    )(a, b, narrow)


ROW_TILE = 512


def _row_index(tile, rows):
    return tile * rows + lax.broadcasted_iota(jnp.int32, (rows, 1), 0)


def _inverse_counts(t_glob):
    return [1.0 / jnp.minimum(t_glob + 1, w).astype(F32) for w in POOL_WINDOWS]


def _sigmoid(z):
    return 1.0 / (1.0 + jnp.exp(-z))


def _trailing_sums(src, tmp, cols, window, rows):
    bufs = (src, tmp)
    span, level, start = 1, 0, 0
    while span < window:
        start += 8
        a, b = bufs[level % 2], bufs[(level + 1) % 2]
        n = HALO + rows - start
        b[start:start + n, cols] = a[start:start + n, cols] + a[start - span:start - span + n, cols]
        span, level = 2 * span, level + 1
    return bufs[level % 2][HALO:HALO + rows, cols]


def _leading_sums(src, tmp, cols, window, rows):
    bufs = (src, tmp)
    span, level, n = 1, 0, rows + HALO
    while span < window:
        n -= 8
        a, b = bufs[level % 2], bufs[(level + 1) % 2]
        b[0:n, cols] = a[0:n, cols] + a[span:span + n, cols]
        span, level = 2 * span, level + 1
    return bufs[level % 2][0:rows, cols]


def gather_in_background(step, last, out_refs, send_sems, recv_sems, finish):
    n = len(out_refs)
    x, y, c = _position()
    q = 2 * x + y
    chips = _other_chips(x, y)

    def copy(k, i, quarter, half, to):
        return _gather_copy(out_refs[i], send_sems, recv_sems, k * n + i, quarter, half, to)

    if not finish:
        @pl.when(step == 0)
        def _():
            for i in range(n):
                mine, _ = _halves(out_refs[i].shape[1], c)
                for j, chip in enumerate(chips):
                    copy(j, i, q, mine, (*chip, c)).start()

        @pl.when(step == last)
        def _():
            for j, chip in enumerate(chips):
                qj = 2 * chip[0] + chip[1]
                for i in range(n):
                    mine, _ = _halves(out_refs[i].shape[1], c)
                    copy(j, i, qj, mine, (x, y, c)).wait_recv()
                    copy(3 + j, i, qj, mine, (x, y, 1 - c)).start()
        return

    @pl.when(step == last)
    def _():
        for j, chip in enumerate(chips):
            qj = 2 * chip[0] + chip[1]
            for i in range(n):
                mine, other = _halves(out_refs[i].shape[1], c)
                copy(3 + j, i, qj, other, (x, y, c)).wait_recv()
                copy(j, i, q, mine, (x, y, c)).wait_send()
                copy(3 + j, i, qj, mine, (x, y, c)).wait_send()


def pool_forward(x, w0, wpi, gw, gb, scale, wpo, later):
    s = x.shape[0]
    ts = ROW_TILE
    nt = s // ts
    assert nt >= 2
    n_later = len(later)

    def body(x_ref, w0_ref, wpi_ref, gw_ref, gb_ref, sc_ref, wpo_ref, *rest):
        rest = rest[n_later:]
        h1_ref, pooled_ref, gt_ref, n0_ref = rest[:4]
        later_refs = rest[4:4 + n_later]
        ubuf, tbuf, hist, send_sems, recv_sems = rest[4 + n_later:]
        i = pl.program_id(0)
        gather_in_background(i, nt - 1, later_refs, send_sems, recv_sems, finish=False)
        xv = x_ref[...]
        r = lax.rsqrt(jnp.mean(xv * xv, axis=-1, keepdims=True) + EPS)
        n0 = _bf(xv * r * w0_ref[...])
        n0_ref[...] = n0
        u = jnp.concatenate([_nn(n0, wpi_ref[0]), _nn(n0, wpi_ref[1])], axis=-1)
        gt = jnp.concatenate([_nn(n0, wpi_ref[2]), _nn(n0, wpi_ref[3])], axis=-1)
        gt_ref[...] = gt

        @pl.when(i == 0)
        def _():
            hist[...] = jnp.zeros_like(hist)

        ubuf[0:HALO, :] = hist[...]
        ubuf[HALO:HALO + ts, :] = u
        hist[...] = u[ts - HALO:, :]
        inv = _inverse_counts(_row_index(i, ts))
        mixed = []
        for g, w in enumerate(POOL_WINDOWS):
            cols = slice(g * GROUP_DIM, (g + 1) * GROUP_DIM)
            pooled = _bf(_trailing_sums(ubuf, tbuf, cols, w, ts) * inv[g] - u[:, cols])
            pooled_ref[:, cols] = pooled
            mixed.append(_nn(pooled, gw_ref[g]))
        mixed = jnp.concatenate(mixed, axis=-1) + gb_ref[...]
        y = mixed * sc_ref[...] * (gt * _sigmoid(gt))
        h1_ref[...] = xv + _nn(_bf(y), wpo_ref[...])
        gather_in_background(i, nt - 1, later_refs, send_sems, recv_sems, finish=True)

    row = lambda cols: pl.BlockSpec((ts, cols), lambda i: (i, 0))
    outs = pl.pallas_call(
        body, name="pool_forward", grid=(nt,),
        out_shape=[jax.ShapeDtypeStruct((s, D), F32), jax.ShapeDtypeStruct((s, D), BF16),
                   jax.ShapeDtypeStruct((s, D), F32), jax.ShapeDtypeStruct((s, D), BF16)]
                  + [jax.ShapeDtypeStruct(a.shape, a.dtype) for a in later],
        in_specs=[row(D), _full((1, D)), _full((N_CHIPS, D, D // 2)), _full((GROUPS, GROUP_DIM, GROUP_DIM)),
                  _full((1, D)), _full((1, D)), _full((D, D))] + _any_specs(n_later),
        out_specs=[row(D), row(D), row(D), row(D)] + _any_specs(n_later),
        input_output_aliases={7 + k: 4 + k for k in range(n_later)},
        scratch_shapes=[pltpu.VMEM((HALO + ts, D), F32), pltpu.VMEM((HALO + ts, D), F32),
                        pltpu.VMEM((HALO, D), F32),
                        pltpu.SemaphoreType.DMA((6 * n_later,)), pltpu.SemaphoreType.DMA((6 * n_later,))],
        compiler_params=_params("arbitrary"),
    )(x, w0, wpi, gw, gb, scale, wpo, *later)
    return outs[:4], outs[4:]


def pool_backward(x, dh1, pooled, gt, w0, wpi, gw, gb, scale, wpo, chip_sums):
    s = x.shape[0]
    ts = ROW_TILE
    nt = s // ts
    n_sums = len(chip_sums)

    def body(x_ref, dh1_ref, pooled_ref, gt_ref, w0_ref, wpi_ref, gw_ref, gb_ref, sc_ref, wpo_ref, *rest):
        sum_refs, rest = rest[:n_sums], rest[n_sums:]
        dx_ref, dproj_ref, gpo_ref, ggw_ref, small_ref = rest[:5]
        got_refs = rest[5:5 + n_sums]
        ebuf, tbuf, ahead, send_sems, recv_sems = rest[5 + n_sums:]
        i = pl.program_id(0)
        copies = _scatter_copies(sum_refs, got_refs, send_sems, recv_sems)

        @pl.when(i == 0)
        def _():
            for cp in copies:
                cp.start()

        @pl.when(i == 0)
        def _():
            gpo_ref[...] = jnp.zeros_like(gpo_ref)
            ggw_ref[...] = jnp.zeros_like(ggw_ref)
            small_ref[...] = jnp.zeros_like(small_ref)
            ahead[...] = jnp.zeros_like(ahead)

        dh1 = dh1_ref[...]
        dh1_bf = _bf(dh1)
        gt = gt_ref[...]
        sc = sc_ref[...]
        dy = _nt(dh1_bf, wpo_ref[...])
        pooled_bf = []
        mixed = []
        for g in range(GROUPS):
            cols = slice(g * GROUP_DIM, (g + 1) * GROUP_DIM)
            pb = pooled_ref[:, cols]
            pooled_bf.append(pb)
            mixed.append(_nn(pb, gw_ref[g]))
        mixed = jnp.concatenate(mixed, axis=-1) + gb_ref[...]
        sg = _sigmoid(gt)
        silu = gt * sg
        gpo_ref[...] += _tn(_bf(mixed * sc * silu), dh1_bf)
        dmixed = dy * sc * silu
        dgt = dy * mixed * sc * (sg * (1.0 + gt * (1.0 - sg)))
        dproj_ref[:, D:] = _bf(dgt)
        small_ref[1:2, :] += jnp.sum(dy * mixed * silu, axis=0, keepdims=True)
        small_ref[2:3, :] += jnp.sum(dmixed, axis=0, keepdims=True)

        inv = _inverse_counts(_row_index(nt - 1 - i, ts))
        ebuf[ts:ts + HALO, :] = ahead[...]
        dpooled = []
        for g in range(GROUPS):
            cols = slice(g * GROUP_DIM, (g + 1) * GROUP_DIM)
            dm = _bf(dmixed[:, cols])
            ggw_ref[g] += _tn(pooled_bf[g], dm)
            dp = _nt(dm, gw_ref[g])
            dpooled.append(dp)
            ebuf[0:ts, cols] = dp * inv[g]
        ahead[...] = ebuf[0:HALO, :]
        du = []
        for g, w in enumerate(POOL_WINDOWS):
            cols = slice(g * GROUP_DIM, (g + 1) * GROUP_DIM)
            du.append(_leading_sums(ebuf, tbuf, cols, w, ts) - dpooled[g])
        du = _bf(jnp.concatenate(du, axis=-1))
        dproj_ref[:, :D] = du
        dgt_bf = _bf(dgt)
        half = D // 2
        dn0 = (_nt(du[:, :half], wpi_ref[0]) + _nt(du[:, half:], wpi_ref[1])
               + _nt(dgt_bf[:, :half], wpi_ref[2]) + _nt(dgt_bf[:, half:], wpi_ref[3]))

        xv = x_ref[...]
        r = lax.rsqrt(jnp.mean(xv * xv, axis=-1, keepdims=True) + EPS)
        xhat = xv * r
        small_ref[0:1, :] += jnp.sum(dn0 * xhat, axis=0, keepdims=True)
        dxh = dn0 * w0_ref[...]
        dx_ref[...] = dh1 + r * (dxh - xhat * jnp.mean(dxh * xhat, axis=-1, keepdims=True))

        @pl.when(i == nt - 1)
        def _():
            for cp in copies:
                cp.wait()

    row = lambda cols: pl.BlockSpec((ts, cols), lambda i: (nt - 1 - i, 0))
    outs = pl.pallas_call(
        body, name="pool_backward", grid=(nt,),
        out_shape=[jax.ShapeDtypeStruct((s, D), F32), jax.ShapeDtypeStruct((s, 2 * D), BF16),
                   jax.ShapeDtypeStruct((D, D), F32),
                   jax.ShapeDtypeStruct((GROUPS, GROUP_DIM, GROUP_DIM), F32),
                   jax.ShapeDtypeStruct((8, D), F32)] + _scatter_shapes(chip_sums),
        in_specs=[row(D), row(D), row(D), row(D), _full((1, D)), _full((N_CHIPS, D, D // 2)),
                  _full((GROUPS, GROUP_DIM, GROUP_DIM)), _full((1, D)), _full((1, D)), _full((D, D))]
                 + _any_specs(n_sums),
        out_specs=[row(D), row(2 * D), _full((D, D)), _full((GROUPS, GROUP_DIM, GROUP_DIM)), _full((8, D))]
                  + _any_specs(n_sums),
        scratch_shapes=[pltpu.VMEM((ts + HALO, D), F32), pltpu.VMEM((ts + HALO, D), F32),
                        pltpu.VMEM((HALO, D), F32),
                        pltpu.SemaphoreType.DMA((3 * n_sums,)), pltpu.SemaphoreType.DMA((3 * n_sums,))],
        compiler_params=_params("arbitrary"),
    )(x, dh1, pooled, gt, w0, wpi, gw, gb, scale, wpo, *chip_sums)
    return outs[:5], outs[5:]


def gla_project(h1, w1, wgi, wlow, wgk, bgk, later):
    s = h1.shape[0]
    ts = ROW_TILE
    nt = s // ts
    assert nt >= 2
    n_later = len(later)

    def body(h_ref, w1_ref, wgi_ref, wlow_ref, wgk_ref, bgk_ref, *rest):
        rest = rest[n_later:]
        qk_ref, v_ref, gate_ref, low_ref, cum_ref, n1_ref = rest[:6]
        later_refs = rest[6:6 + n_later]
        send_sems, recv_sems = rest[6 + n_later:]
        gather_in_background(pl.program_id(0), nt - 1, later_refs, send_sems, recv_sems, finish=False)
        hv = h_ref[...]
        r = lax.rsqrt(jnp.mean(hv * hv, axis=-1, keepdims=True) + EPS)
        n1 = _bf(hv * r * w1_ref[...])
        n1_ref[...] = n1
        qk_ref[...] = _nn(n1, wgi_ref[:, 0:2 * KEY_W])
        v_ref[...] = _bf(_nn(n1, wgi_ref[:, 2 * KEY_W:2 * KEY_W + D]))
        gate_ref[...] = _nn(n1, wgi_ref[:, 2 * KEY_W + D:GLA_MAIN])
        low = _bf(_nn(n1, wlow_ref[...]))
        low_ref[...] = low
        z = _nn(low, wgk_ref[...]) + bgk_ref[...]
        lg = (jnp.minimum(z, 0.0) - jnp.log(1.0 + jnp.exp(-jnp.abs(z)))) / GATE_NORM
        lower_f = _chunk_masks()[0].astype(F32)
        for r0 in range(0, ts, CHUNK):
            cum_ref[r0:r0 + CHUNK, :] = _nn_exact(lower_f, lg[r0:r0 + CHUNK, :])
        gather_in_background(pl.program_id(0), nt - 1, later_refs, send_sems, recv_sems, finish=True)

    row = lambda cols: pl.BlockSpec((ts, cols), lambda i: (i, 0))
    outs = pl.pallas_call(
        body, name="gla_project", grid=(nt,),
        out_shape=[jax.ShapeDtypeStruct((s, D), F32), jax.ShapeDtypeStruct((s, D), BF16),
                   jax.ShapeDtypeStruct((s, D), F32), jax.ShapeDtypeStruct((s, RANK_PAD), BF16),
                   jax.ShapeDtypeStruct((s, KEY_W), F32), jax.ShapeDtypeStruct((s, D), BF16)]
                  + [jax.ShapeDtypeStruct(a.shape, a.dtype) for a in later],
        in_specs=[row(D), _full((1, D)), _full((D, GLA_MAIN)), _full((D, RANK_PAD)),
                  _full((RANK_PAD, KEY_W)), _full((1, KEY_W))] + _any_specs(n_later),
        out_specs=[row(D), row(D), row(D), row(RANK_PAD), row(KEY_W), row(D)] + _any_specs(n_later),
        input_output_aliases={6 + k: 6 + k for k in range(n_later)},
        scratch_shapes=[pltpu.SemaphoreType.DMA((6 * n_later,)), pltpu.SemaphoreType.DMA((6 * n_later,))],
        compiler_params=_params("arbitrary"),
    )(h1, w1, wgi, wlow, wgk, bgk, *later)
    return outs[:6], outs[6:]


GLA_BLOCK = 512
CHUNKS_PER_BLOCK = GLA_BLOCK // CHUNK


def _chunk_masks():
    t = lax.broadcasted_iota(jnp.int32, (CHUNK, CHUNK), 0)
    u = lax.broadcasted_iota(jnp.int32, (CHUNK, CHUNK), 1)
    return t >= u, t <= u


def _gla_chunk_terms(q, cum):
    ep = jnp.exp(cum)
    en = jnp.exp(-cum)
    qs = q * (HEAD_K ** -0.5)
    last = cum[CHUNK - 1:CHUNK, :]
    ed = jnp.exp(last - cum)
    dec = jnp.exp(last)
    return ep, en, qs, ed, dec


def gla_forward(qk, v, cum):
    s = qk.shape[0]
    nb = s // GLA_BLOCK
    nc = s // CHUNK

    def body(q_ref, k_ref, v_ref, cum_ref, o_ref, st_ref, sc_ref, state):
        @pl.when(pl.program_id(0) == 0)
        def _():
            state[...] = jnp.zeros_like(state)

        lower, _ = _chunk_masks()

        def chunk(cc, carry):
            rows = pl.ds(pl.multiple_of(cc * CHUNK, CHUNK), CHUNK)
            for h in range(HEADS):
                kc = slice(h * HEAD_K, (h + 1) * HEAD_K)
                vc = slice(h * HEAD_V, (h + 1) * HEAD_V)
                q = q_ref[rows, kc]
                k = k_ref[rows, kc]
                v = v_ref[rows, vc]
                ep, en, qs, ed, dec = _gla_chunk_terms(q, cum_ref[rows, kc])
                a = _bf(qs * ep)
                fwd = _nt(a, _bf(k * en))
                bwd = _nt(_bf(qs * en), _bf(k * ep))
                scores = _bf(jnp.where(lower, fwd, bwd))
                sc_ref[rows, h * CHUNK:(h + 1) * CHUNK] = scores
                st = state[h]
                st_ref[cc, h] = st
                o_ref[rows, vc] = _nn(scores, v) + _nt(a, _bf(st))
                state[h] = st * dec + _tn(v, _bf(k * ed))
            return carry

        lax.fori_loop(0, CHUNKS_PER_BLOCK, chunk, 0, unroll=4)

    return pl.pallas_call(
        body, name="gla_forward", grid=(nb,),
        out_shape=(jax.ShapeDtypeStruct((s, D), F32),
                   jax.ShapeDtypeStruct((nc, HEADS, HEAD_V, HEAD_K), F32),
                   jax.ShapeDtypeStruct((s, HEADS * CHUNK), BF16)),
        in_specs=[pl.BlockSpec((GLA_BLOCK, KEY_W), lambda i: (i, 0)),
                  pl.BlockSpec((GLA_BLOCK, KEY_W), lambda i: (i, 1)),
                  pl.BlockSpec((GLA_BLOCK, D), lambda i: (i, 0)),
                  pl.BlockSpec((GLA_BLOCK, KEY_W), lambda i: (i, 0))],
        out_specs=(pl.BlockSpec((GLA_BLOCK, D), lambda i: (i, 0)),
                   pl.BlockSpec((CHUNKS_PER_BLOCK, HEADS, HEAD_V, HEAD_K), lambda i: (i, 0, 0, 0)),
                   pl.BlockSpec((GLA_BLOCK, HEADS * CHUNK), lambda i: (i, 0))),
        scratch_shapes=[pltpu.VMEM((HEADS, HEAD_V, HEAD_K), F32)],
        compiler_params=_params("arbitrary"),
    )(qk, qk, v, cum)


def gla_backward(qk, v, cum, do, states, scores):
    s = qk.shape[0]
    nb = s // GLA_BLOCK

    def body(q_ref, k_ref, v_ref, cum_ref, do_ref, st_ref, sc_ref, dq_ref, dk_ref, dv_ref, dcum_ref, dstate):
        @pl.when(pl.program_id(0) == 0)
        def _():
            dstate[...] = jnp.zeros_like(dstate)

        lower, _ = _chunk_masks()
        is_last = lax.broadcasted_iota(jnp.int32, (CHUNK, HEAD_K), 0) == CHUNK - 1

        def chunk(step, carry):
            cc = CHUNKS_PER_BLOCK - 1 - step
            rows = pl.ds(pl.multiple_of(cc * CHUNK, CHUNK), CHUNK)
            for h in range(HEADS):
                kc = slice(h * HEAD_K, (h + 1) * HEAD_K)
                vc = slice(h * HEAD_V, (h + 1) * HEAD_V)
                q = q_ref[rows, kc]
                k = k_ref[rows, kc]
                v = v_ref[rows, vc]
                do_c = do_ref[rows, vc]
                ep, en, qs, ed, dec = _gla_chunk_terms(q, cum_ref[rows, kc])
                a = _bf(qs * ep)
                b = _bf(k * en)
                c = _bf(qs * en)
                dk_dec = _bf(k * ep)
                kd = _bf(k * ed)
                scores = sc_ref[rows, h * CHUNK:(h + 1) * CHUNK]
                st = st_ref[cc, h]
                dst = dstate[h]
                dst_bf = _bf(dst)

                dscores = _nt(do_c, v)
                dfwd = _bf(jnp.where(lower, dscores, 0.0))
                dbwd = _bf(jnp.where(lower, 0.0, dscores))
                dv_ref[rows, vc] = _bf(_tn(scores, do_c) + _nt(kd, dst_bf))
                da = _nn(dfwd, b) + _nn(do_c, _bf(st))
                db = _tn(dfwd, a)
                dc = _nn(dbwd, dk_dec)
                ddk = _tn(dbwd, c)
                dkd = _nn(v, dst_bf)
                ddec = jnp.sum(dst * st, axis=0, keepdims=True)
                dstate[h] = dst * dec + _tn(do_c, a)

                m = dkd * k * ed
                dq_ref[rows, kc] = _bf((da * ep + dc * en) * (HEAD_K ** -0.5))
                dk_ref[rows, kc] = _bf(db * en + ddk * ep + dkd * ed)
                dcum = (da * qs + ddk * k) * ep - (db * k + dc * qs) * en - m
                dlast = jnp.sum(m, axis=0, keepdims=True) + ddec * dec
                dcum_ref[rows, kc] = dcum + jnp.where(is_last, dlast, 0.0)
            return carry

        lax.fori_loop(0, CHUNKS_PER_BLOCK, chunk, 0, unroll=4)

    rev = lambda cols, col_block: pl.BlockSpec((GLA_BLOCK, cols), lambda i: (nb - 1 - i, col_block))
    return pl.pallas_call(
        body, name="gla_backward", grid=(nb,),
        out_shape=(jax.ShapeDtypeStruct((s, KEY_W), BF16), jax.ShapeDtypeStruct((s, KEY_W), BF16),
                   jax.ShapeDtypeStruct((s, D), BF16), jax.ShapeDtypeStruct((s, KEY_W), F32)),
        in_specs=[rev(KEY_W, 0), rev(KEY_W, 1), rev(D, 0), rev(KEY_W, 0), rev(D, 0),
                  pl.BlockSpec((CHUNKS_PER_BLOCK, HEADS, HEAD_V, HEAD_K), lambda i: (nb - 1 - i, 0, 0, 0)),
                  rev(HEADS * CHUNK, 0)],
        out_specs=(rev(KEY_W, 0), rev(KEY_W, 0), rev(D, 0), rev(KEY_W, 0)),
        scratch_shapes=[pltpu.VMEM((HEADS, HEAD_V, HEAD_K), F32)],
        compiler_params=_params("arbitrary"),
    )(qk, qk, v, cum, do, states, scores)


def head_and_loss(o, gate, h1, target, hw, wgo, wf):
    s = o.shape[0]
    ts = ROW_TILE

    def body(o_ref, gate_ref, h1_ref, tgt_ref, hw_ref, wgo_ref, wf_ref,
             dh2_ref, do_ref, dgate_ref, ggo_ref, small_ref):
        @pl.when(pl.program_id(0) == 0)
        def _():
            ggo_ref[...] = jnp.zeros_like(ggo_ref)
            small_ref[...] = jnp.zeros_like(small_ref)

        gate = gate_ref[...]
        hw = hw_ref[...]
        sg = _sigmoid(gate)
        silu = gate * sg
        ohat, ro = [], []
        for h in range(HEADS):
            oh = o_ref[:, h * HEAD_V:(h + 1) * HEAD_V]
            rh = lax.rsqrt(jnp.mean(oh * oh, axis=-1, keepdims=True) + EPS)
            ro.append(rh)
            ohat.append(oh * rh)
        ohat = jnp.concatenate(ohat, axis=-1)
        on = ohat * hw
        y2 = _bf(on * silu)
        h2 = h1_ref[...] + _nn(y2, wgo_ref[...])
        rf = lax.rsqrt(jnp.mean(h2 * h2, axis=-1, keepdims=True) + EPS)
        h2hat = h2 * rf
        wf = wf_ref[...]
        diff = h2hat * wf - tgt_ref[...]
        small_ref[2:3, :] += jnp.zeros((1, D), F32) + 0.5 * jnp.sum(diff * diff) / D
        dout = diff / D
        small_ref[0:1, :] += jnp.sum(dout * h2hat, axis=0, keepdims=True)
        dxh = dout * wf
        dh2 = rf * (dxh - h2hat * jnp.mean(dxh * h2hat, axis=-1, keepdims=True))
        dh2_ref[...] = dh2
        dh2_bf = _bf(dh2)
        ggo_ref[...] += _tn(y2, dh2_bf)
        dy2 = _nt(dh2_bf, wgo_ref[...])
        don = dy2 * silu
        dgate_ref[...] = _bf(dy2 * on * (sg * (1.0 + gate * (1.0 - sg))))
        ghw = jnp.sum(don * ohat, axis=0, keepdims=True)
        small_ref[1:2, 0:HEAD_V] += sum(ghw[:, h * HEAD_V:(h + 1) * HEAD_V] for h in range(HEADS))
        dohat = don * hw
        for h in range(HEADS):
            cols = slice(h * HEAD_V, (h + 1) * HEAD_V)
            oh, dh = ohat[:, cols], dohat[:, cols]
            do_ref[:, cols] = _bf(ro[h] * (dh - oh * jnp.mean(dh * oh, axis=-1, keepdims=True)))

    row = lambda cols: pl.BlockSpec((ts, cols), lambda i: (i, 0))
    act = jax.ShapeDtypeStruct((s, D), F32)
    act_bf = jax.ShapeDtypeStruct((s, D), BF16)
    return pl.pallas_call(
        body, name="head_and_loss", grid=(s // ts,),
        out_shape=(act, act_bf, act_bf, jax.ShapeDtypeStruct((D, D), F32), jax.ShapeDtypeStruct((8, D), F32)),
        in_specs=[row(D), row(D), row(D), row(D),
                  _full((1, D)), _full((D, D)), _full((1, D))],
        out_specs=(row(D), row(D), row(D), _full((D, D)), _full((8, D))),
        compiler_params=_params("arbitrary"),
    )(o, gate, h1, target, hw, wgo, wf)


def gla_project_backward(dq, dk, dv, dgate, dcum, low, h1, dh2, w1, wgi, wlow, wgk, bgk):
    s = h1.shape[0]
    ts = ROW_TILE

    def body(dq_ref, dk_ref, dv_ref, dgate_ref, dcum_ref, low_ref, h1_ref, dh2_ref, w1_ref,
             wgi_ref, wlow_ref, wgk_ref, bgk_ref, dh1_ref, dproj_ref, dlow_ref, ggk_ref, small_ref):
        @pl.when(pl.program_id(0) == 0)
        def _():
            ggk_ref[...] = jnp.zeros_like(ggk_ref)
            small_ref[...] = jnp.zeros_like(small_ref)

        low = low_ref[...]
        z = _nn(low, wgk_ref[...]) + bgk_ref[...]
        upper_f = _chunk_masks()[1].astype(F32)
        dlg = jnp.concatenate([_nn_exact(upper_f, dcum_ref[r0:r0 + CHUNK, :]) for r0 in range(0, ts, CHUNK)],
                              axis=0)
        dz = dlg * (1.0 / GATE_NORM) * _sigmoid(-z)
        dz_bf = _bf(dz)
        ggk_ref[...] += _tn(low, dz_bf)
        small_ref[1:2, 0:KEY_W] += jnp.sum(dz, axis=0, keepdims=True)
        dlow = _bf(_nt(dz_bf, wgk_ref[...]))
        dlow_ref[...] = dlow
        dn1 = _nt(dlow, wlow_ref[...])
        for ref, lo, hi in ((dq_ref, 0, KEY_W), (dk_ref, KEY_W, 2 * KEY_W),
                            (dv_ref, 2 * KEY_W, 2 * KEY_W + D), (dgate_ref, 2 * KEY_W + D, GLA_MAIN)):
            piece = ref[...]
            dproj_ref[:, lo:hi] = piece
            dn1 = dn1 + _nt(piece, wgi_ref[:, lo:hi])
        hv = h1_ref[...]
        r = lax.rsqrt(jnp.mean(hv * hv, axis=-1, keepdims=True) + EPS)
        hhat = hv * r
        small_ref[0:1, :] += jnp.sum(dn1 * hhat, axis=0, keepdims=True)
        dxh = dn1 * w1_ref[...]
        dh1_ref[...] = dh2_ref[...] + r * (dxh - hhat * jnp.mean(dxh * hhat, axis=-1, keepdims=True))

    row = lambda cols: pl.BlockSpec((ts, cols), lambda i: (i, 0))
    return pl.pallas_call(
        body, name="gla_project_backward", grid=(s // ts,),
        out_shape=(jax.ShapeDtypeStruct((s, D), F32), jax.ShapeDtypeStruct((s, GLA_MAIN), BF16),
                   jax.ShapeDtypeStruct((s, RANK_PAD), BF16), jax.ShapeDtypeStruct((RANK_PAD, KEY_W), F32),
                   jax.ShapeDtypeStruct((8, D), F32)),
        in_specs=[row(KEY_W), row(KEY_W), row(D), row(D), row(KEY_W), row(RANK_PAD), row(D), row(D),
                  _full((1, D)), _full((D, GLA_MAIN)), _full((D, RANK_PAD)), _full((RANK_PAD, KEY_W)),
                  _full((1, KEY_W))],
        out_specs=(row(D), row(GLA_MAIN), row(RANK_PAD), _full((RANK_PAD, KEY_W)), _full((8, D))),
        compiler_params=_params("arbitrary"),
    )(dq, dk, dv, dgate, dcum, low, h1, dh2, w1, wgi, wlow, wgk, bgk)


def _groups_from_quarters(a):
    return a.reshape(N_CHIPS, GROUPS, 64, GROUP_DIM).transpose(1, 0, 2, 3).reshape(GROUPS, GROUP_DIM, GROUP_DIM)


def _quarters_from_groups(a):
    return a.reshape(GROUPS, N_CHIPS, 64, GROUP_DIM).transpose(1, 0, 2, 3).reshape(N_CHIPS, GROUP_DIM, GROUP_DIM)


def _gla_in_weights(wgi_q):
    wgi_all = jnp.concatenate([wgi_q[q] for q in range(N_CHIPS)], axis=1)
    wlow = jnp.pad(wgi_all[:, GLA_MAIN:], ((0, 0), (0, RANK_PAD - GATE_RANK)))
    return wgi_all, wlow


def local_gradients(xs, target, w0, w1, wf, wpi, gw, gb, scale, wpo, gla_quarters, wgk, bgk, hw_tiled, place):
    wgi_q, wgo_q = gla_quarters
    (h1, pooled, gt, n0), (wgi_q,) = pool_forward(xs, w0, wpi, gw, gb, scale, wpo, [wgi_q])
    wgi, wlow = _gla_in_weights(wgi_q)
    (qk, v, gate, low, cum, n1), (wgo_q,) = gla_project(h1, w1, wgi, wlow, wgk, bgk, [wgo_q])
    wgo = wgo_q.reshape(D, D)
    o, states, scores = gla_forward(qk, v, cum)

    dh2, do, dgate, g_gla_out, small_top = head_and_loss(o, gate, h1, target, hw_tiled, wgo, wf)
    dq, dk, dv, dcum = gla_backward(qk, v, cum, do, states, scores)
    dh1, dproj, dlow, g_gk_pad, small_gla = gla_project_backward(
        dq, dk, dv, dgate, dcum, low, h1, dh2, w1, wgi, wlow, wgk, bgk)
    g_gla_main, g_gla_low = matmul_tn(n1, dproj, "grad_gla_in", narrow=dlow)
    g_gla_in = jnp.concatenate([g_gla_main, g_gla_low[:, :GATE_RANK]], axis=1)

    def chip_sums(grads, tag):
        theirs = exchange_with_sibling(grads, "exchange_with_sibling_" + tag)
        return add_halves(grads, theirs, place, "add_halves_" + tag)

    gla_sums = chip_sums(
        [jnp.stack([g_gla_in[:, GLA_IN_QUARTER * q:GLA_IN_QUARTER * (q + 1)] for q in range(N_CHIPS)]),
         g_gla_out.reshape(N_CHIPS, D // N_CHIPS, D)], "gla")
    (dx, dpool, g_pool_out, g_group_w, small_pool), gla_got = pool_backward(
        xs, dh1, pooled, gt, w0, wpi, gw, gb, scale, wpo, [b for _, b in gla_sums])
    g_pool_in = matmul_tn(n0, dpool, "grad_pool_in", by_column_tile=True)

    pool_sums = chip_sums(
        [g_pool_in, _quarters_from_groups(g_group_w), g_pool_out.reshape(N_CHIPS, D // N_CHIPS, D)], "pool")
    pool_got = scatter_to_owners([b for _, b in pool_sums], "scatter_to_owners_pool")
    reduced, total = join_halves(
        add_parts([f for f, _ in pool_sums + gla_sums], list(pool_got) + list(gla_got), place, "add_parts"),
        small_pool, small_gla, small_top, g_gk_pad)
    return dx, reduced, total


def kernel(x, norm_w, pool_in_w, pool_group_w, pool_group_b, pool_scale, pool_out_w, gla_in_w, gla_gk_w, gla_gk_b, gla_head_norm_w, gla_out_w, final_norm_w, loss_target, m_norm_w, m_pool_in_w, m_pool_group_w, m_pool_group_b, m_pool_scale, m_pool_out_w, m_gla_in_w, m_gla_gk_w, m_gla_gk_b, m_gla_head_norm_w, m_gla_out_w, m_final_norm_w, v_norm_w, v_pool_in_w, v_pool_group_w, v_pool_group_b, v_pool_scale, v_pool_out_w, v_gla_in_w, v_gla_gk_w, v_gla_gk_b, v_gla_head_norm_w, v_gla_out_w, v_final_norm_w):
    s = x.shape[1]
    xs = x[0]
    target = loss_target[0]
    q_chip = 2 * lax.axis_index("x") + lax.axis_index("y")
    place = jnp.stack([lax.axis_index("c"), q_chip]).astype(jnp.int32)

    (wpi, gw_q, wpo_q, wgi_q, wgo_q), small_all = allgather_weights(
        [pool_in_w[0], pool_group_w[0].reshape(GROUP_DIM, GROUP_DIM), pool_out_w[0], gla_in_w[0], gla_out_w[0]],
        exchange=(True, True, True, False, False),
        smalls=[gla_gk_b, gla_head_norm_w, pool_group_b[0], gla_gk_w[0]])
    gw = _groups_from_quarters(gw_q)
    wpo = wpo_q.reshape(D, D)
    small_all = small_all[0::2]
    bgk = small_all[:, 0, :].reshape(1, KEY_W)
    hw = small_all[:, 1, 0:64].reshape(1, HEAD_V)
    gb = small_all[:, 2:2 + GROUPS, 0:64].transpose(1, 0, 2).reshape(1, D)
    wgk16 = small_all[:, 8:8 + GATE_RANK, :].transpose(1, 0, 2).reshape(GATE_RANK, KEY_W)
    wgk = _bf(jnp.pad(wgk16, ((0, RANK_PAD - GATE_RANK), (0, 0))))
    hw_tiled = jnp.tile(hw, (1, HEADS))

    w0 = norm_w[0:1]
    w1 = norm_w[1:2]
    wf = final_norm_w.reshape(1, D)

    dx, reduced, total = local_gradients(
        xs, target, w0, w1, wf, wpi, gw, gb, pool_scale, wpo, [wgi_q, wgo_q], wgk, bgk, hw_tiled, place)
    r_pool_in, r_group_w, r_pool_out, r_gla_in, r_gla_out = reduced
    r_group_w = r_group_w.reshape(GROUPS, 64, GROUP_DIM)

    loss = total[7, 0]
    g_norm = jnp.stack([total[0], total[3]])
    g_scale = total[1:2]
    g_final = total[5]
    pick = lambda full, width: lax.dynamic_slice_in_dim(full, q_chip * width, width, axis=-1)
    g_gk_b = pick(total[4:5, 0:KEY_W], 128)
    g_hnw = pick(total[6:7, 0:HEAD_V], 64)
    g_group_b = pick(total[2].reshape(GROUPS, GROUP_DIM), 64)[None]
    g_gk_w = pick(total[8:16].reshape(GATE_RANK, KEY_W), 128)[None]

    turn = lambda a: jnp.transpose(a, (2, 0, 1))
    back = lambda a: jnp.transpose(a, (1, 2, 0))
    as2d = lambda a, w: a.reshape(-1, w.shape[-1])
    big_names = ("pool_in_w", "pool_group_w", "pool_out_w", "gla_in_w", "gla_out_w")
    big_args = [(pool_in_w, r_pool_in[None], m_pool_in_w, v_pool_in_w),
                (pool_group_w, r_group_w[None], m_pool_group_w, v_pool_group_w),
                (pool_out_w, r_pool_out[None], m_pool_out_w, v_pool_out_w),
                (gla_in_w, r_gla_in[None], m_gla_in_w, v_gla_in_w),
                (gla_out_w, r_gla_out[None], m_gla_out_w, v_gla_out_w)]
    to_kernel = lambda n, a, w: turn(a) if n == "gla_in_w" else as2d(a, w)
    from_kernel = lambda n, a, w: back(a) if n == "gla_in_w" else a.reshape(w.shape)
    big_in = [tuple(to_kernel(n, a, p[0]) for a in p) for n, p in zip(big_names, big_args)]
    big_out = adamw(big_in, "adamw")
    big = {n: (from_kernel(n, i[1], p[0]),) + tuple(from_kernel(n, o, p[0]) for o in out)
           for n, p, i, out in zip(big_names, big_args, big_in, big_out)}

    small_names = ("norm_w", "pool_group_b", "pool_scale", "gla_gk_w", "gla_gk_b", "gla_head_norm_w",
                   "final_norm_w")
    small_args = [(norm_w, g_norm, m_norm_w, v_norm_w),
                  (pool_group_b, g_group_b, m_pool_group_b, v_pool_group_b),
                  (pool_scale, g_scale, m_pool_scale, v_pool_scale),
                  (gla_gk_w, g_gk_w, m_gla_gk_w, v_gla_gk_w),
                  (gla_gk_b, g_gk_b, m_gla_gk_b, v_gla_gk_b),
                  (gla_head_norm_w, g_hnw, m_gla_head_norm_w, v_gla_head_norm_w),
                  (final_norm_w, g_final, m_final_norm_w, v_final_norm_w)]
    small_out = adamw_small([tuple(as2d(a, p[0]) for a in p) for p in small_args])
    small = {n: (p[1].reshape(p[0].shape),) + tuple(o.reshape(p[0].shape) for o in out)
             for n, p, out in zip(small_names, small_args, small_out)}
    results = [
        small["norm_w"],
        big["pool_in_w"],
        big["pool_group_w"],
        small["pool_group_b"],
        small["pool_scale"],
        big["pool_out_w"],
        big["gla_in_w"],
        small["gla_gk_w"],
        small["gla_gk_b"],
        small["gla_head_norm_w"],
        big["gla_out_w"],
        small["final_norm_w"],
    ]
    grads, deltas, new_m, new_v = zip(*results)
    return (loss, dx[None], *grads, *deltas, *new_m, *new_v)
```

```python
import jax
import jax.numpy as jnp
from jax import lax
from jax.experimental import pallas as pl
from jax.experimental.pallas import tpu as pltpu

F32 = jnp.float32
BF16 = jnp.bfloat16
MESH = pl.DeviceIdType.MESH

D = 1024
POOL_WINDOWS = (2, 4, 8, 16)
GROUPS = 4
GROUP_DIM = 256
HEADS = 4
HEAD_K = 128
HEAD_V = 256
KEY_W = 512
CHUNK = 64
GATE_RANK = 16
GATE_NORM = 16.0
GLA_IN = 3088
GLA_MAIN = 3072
RANK_PAD = 128
EPS = 1e-6
HALO = 32

ADAM_LR = 0.001
ADAM_B1 = 0.9
ADAM_B2 = 0.999
ADAM_EPS = 1e-08
ADAM_WD = 0.01
ADAM_STEP = 10

N_CHIPS = 4
N_DEV = 8
GLA_IN_QUARTER = GLA_IN // N_CHIPS

VMEM_LIMIT = 56 * 1024 * 1024


def _nn(a, b):
    return lax.dot_general(a, b, (((1,), (0,)), ((), ())), preferred_element_type=F32)


def _nt(a, b):
    return lax.dot_general(a, b, (((1,), (1,)), ((), ())), preferred_element_type=F32)


def _tn(a, b):
    return lax.dot_general(a, b, (((0,), (0,)), ((), ())), preferred_element_type=F32)


def _nn_exact(a, b):
    return lax.dot_general(a, b, (((1,), (0,)), ((), ())), preferred_element_type=F32,
                           precision=lax.Precision.HIGHEST)


def _bf(a):
    return a.astype(BF16)


def _params(*sem):
    return pltpu.CompilerParams(dimension_semantics=sem, vmem_limit_bytes=VMEM_LIMIT)


def _full(shape):
    return pl.BlockSpec(shape, lambda i: (0,) * len(shape))


def _position():
    return lax.axis_index("x"), lax.axis_index("y"), lax.axis_index("c")


def _gather_small(in_ref, all_ref, send_sems, recv_sems, local_sem):
    x, y, c = _position()
    me = 4 * x + 2 * y + c
    mine = pltpu.make_async_copy(in_ref, all_ref.at[me], local_sem)
    mine.start()
    sends = []
    for k in range(N_DEV - 1):
        fx, fy, fc = (k + 1) >> 2 & 1, (k + 1) >> 1 & 1, (k + 1) & 1
        cp = pltpu.make_async_remote_copy(
            src_ref=in_ref, dst_ref=all_ref.at[me],
            send_sem=send_sems.at[k], recv_sem=recv_sems.at[k],
            device_id=(x ^ fx, y ^ fy, c ^ fc), device_id_type=MESH)
        cp.start()
        sends.append(cp)
    def wait():
        for k in range(N_DEV - 1):
            fx, fy, fc = (k + 1) >> 2 & 1, (k + 1) >> 1 & 1, (k + 1) & 1
            src_dev = 4 * (x ^ fx) + 2 * (y ^ fy) + (c ^ fc)
            pltpu.make_async_remote_copy(
                src_ref=in_ref, dst_ref=all_ref.at[src_dev],
                send_sem=send_sems.at[k], recv_sem=recv_sems.at[k],
                device_id=(x, y, c), device_id_type=MESH).wait_recv()
        for cp in sends:
            cp.wait_send()
        mine.wait()

    return wait


SMALL_SEMS = [pltpu.SemaphoreType.DMA((N_DEV - 1,)), pltpu.SemaphoreType.DMA((N_DEV - 1,)),
              pltpu.SemaphoreType.DMA]
VMEM_SPEC = pl.BlockSpec(memory_space=pltpu.VMEM)


def _other_chips(x, y):
    return [(1 - x, y), (x, 1 - y), (1 - x, 1 - y)]


def _any_specs(n):
    return [pl.BlockSpec(memory_space=pl.ANY)] * n


def _halves(rows, c):
    half = rows // 2
    return pl.ds(c * half, half), pl.ds((1 - c) * half, half)


CAST_ROWS = 256


def _gather_copy(out_ref, send_sems, recv_sems, k, quarter, half, to, src=None):
    dst = out_ref.at[quarter, half]
    return pltpu.make_async_remote_copy(
        src_ref=dst if src is None else src, dst_ref=dst,
        send_sem=send_sems.at[k], recv_sem=recv_sems.at[k], device_id=to, device_id_type=MESH)


SMALL_IN_ROWS = 24


def allgather_weights(quarters, exchange, smalls):
    n = len(quarters)
    shapes = [w.shape for w in quarters]
    moved = [i for i in range(n) if exchange[i]]

    def body(*refs):
        w_refs, (gkb_ref, hnw_ref, gb_ref, gkw_ref) = refs[:n], refs[n:n + 4]
        out_refs, small_all_ref = refs[n + 4:2 * n + 4], refs[2 * n + 4]
        refs = refs[2 * n + 5:]
        f32_bufs, bf_bufs = refs[:n], refs[n:2 * n]
        send_sems, recv_sems, local_sems, small_ref = refs[2 * n:2 * n + 4]
        small_ref[...] = jnp.zeros_like(small_ref)
        small_ref[0:1, :] = gkb_ref[...]
        small_ref[1:2, 0:64] = hnw_ref[...]
        small_ref[2:2 + GROUPS, 0:64] = gb_ref[...]
        small_ref[8:8 + GATE_RANK, :] = gkw_ref[...]
        wait_small = _gather_small(small_ref, small_all_ref, *refs[2 * n + 4:])
        x, y, c = _position()
        q = 2 * x + y
        sibling = (x, y, 1 - c)
        chips = _other_chips(x, y)

        def copy(k, i, quarter, half, to, src=None):
            return _gather_copy(out_refs[i], send_sems, recv_sems, k * n + i, quarter, half, to, src)

        loads = [pltpu.make_async_copy(w_refs[i], f32_bufs[i], local_sems.at[i]) for i in range(n)]
        for cp in loads:
            cp.start()
        keeps, sends = [], []
        for i in range(n):
            loads[i].wait()
            for r0 in range(0, shapes[i][0], CAST_ROWS):
                bf_bufs[i][r0:r0 + CAST_ROWS, :] = _bf(f32_bufs[i][r0:r0 + CAST_ROWS, :])
            keep = pltpu.make_async_copy(bf_bufs[i], out_refs[i].at[q], local_sems.at[n + i])
            keep.start()
            keeps.append(keep)
            if not exchange[i]:
                continue
            mine, _ = _halves(shapes[i][0], c)
            for j, chip in enumerate(chips):
                cp = copy(j, i, q, mine, (*chip, c), src=bf_bufs[i].at[mine])
                cp.start()
                sends.append(cp)
        for j, chip in enumerate(chips):
            qj = 2 * chip[0] + chip[1]
            for i in moved:
                mine, _ = _halves(shapes[i][0], c)
                copy(j, i, qj, mine, (x, y, c)).wait_recv()
                cp = copy(3 + j, i, qj, mine, sibling)
                cp.start()
                sends.append(cp)
        for j, chip in enumerate(chips):
            qj = 2 * chip[0] + chip[1]
            for i in moved:
                _, other = _halves(shapes[i][0], c)
                copy(3 + j, i, qj, other, (x, y, c)).wait_recv()
        wait_small()
        for cp in sends:
            cp.wait_send()
        for cp in keeps:
            cp.wait()

    outs = pl.pallas_call(
        body, name="allgather_weights",
        out_shape=[jax.ShapeDtypeStruct((N_CHIPS, *s), BF16) for s in shapes]
                  + [jax.ShapeDtypeStruct((N_DEV, SMALL_IN_ROWS, 128), F32)],
        in_specs=_any_specs(n) + [VMEM_SPEC] * 4, out_specs=_any_specs(n) + [VMEM_SPEC],
        scratch_shapes=([pltpu.VMEM(s, F32) for s in shapes] + [pltpu.VMEM(s, BF16) for s in shapes]
                        + [pltpu.SemaphoreType.DMA((6 * n,)), pltpu.SemaphoreType.DMA((6 * n,)),
                           pltpu.SemaphoreType.DMA((2 * n,)), pltpu.VMEM((SMALL_IN_ROWS, 128), F32)] + SMALL_SEMS),
        compiler_params=pltpu.CompilerParams(vmem_limit_bytes=VMEM_LIMIT),
    )(*quarters, *smalls)
    return outs[:n], outs[n]


def exchange_with_sibling(grads, name):
    n = len(grads)

    def body(*refs):
        g_refs, theirs_refs = refs[:n], refs[n:2 * n]
        send_sems, recv_sems = refs[2 * n:]
        x, y, c = _position()
        copies = []
        for i in range(n):
            _, other = _halves(g_refs[i].shape[1], c)
            cp = pltpu.make_async_remote_copy(
                src_ref=g_refs[i].at[:, other], dst_ref=theirs_refs[i],
                send_sem=send_sems.at[i], recv_sem=recv_sems.at[i],
                device_id=(x, y, 1 - c), device_id_type=MESH)
            cp.start()
            copies.append(cp)
        for cp in copies:
            cp.wait()

    return pl.pallas_call(
        body, name=name,
        out_shape=[jax.ShapeDtypeStruct((N_CHIPS, g.shape[1] // 2, g.shape[2]), F32) for g in grads],
        in_specs=_any_specs(n), out_specs=_any_specs(n),
        scratch_shapes=[pltpu.SemaphoreType.DMA((n,)), pltpu.SemaphoreType.DMA((n,))],
    )(*grads)


def _scatter_copies(b_refs, got_refs, send_sems, recv_sems):
    n = len(b_refs)
    x, y, c = _position()
    copies = []
    for j, chip in enumerate(_other_chips(x, y)):
        qj = 2 * chip[0] + chip[1]
        for i in range(n):
            copies.append(pltpu.make_async_remote_copy(
                src_ref=b_refs[i].at[qj], dst_ref=got_refs[i].at[j],
                send_sem=send_sems.at[j * n + i], recv_sem=recv_sems.at[j * n + i],
                device_id=(*chip, c), device_id_type=MESH))
    return copies


def _scatter_shapes(chip_sums):
    return [jax.ShapeDtypeStruct((N_CHIPS - 1, *b.shape[1:]), BF16) for b in chip_sums]


def scatter_to_owners(chip_sums, name):
    n = len(chip_sums)

    def body(*refs):
        copies = _scatter_copies(refs[:n], refs[n:2 * n], *refs[2 * n:])
        for cp in copies:
            cp.start()
        for cp in copies:
            cp.wait()

    return pl.pallas_call(
        body, name=name,
        out_shape=_scatter_shapes(chip_sums),
        in_specs=_any_specs(n), out_specs=_any_specs(n),
        scratch_shapes=[pltpu.SemaphoreType.DMA((3 * n,)), pltpu.SemaphoreType.DMA((3 * n,))],
    )(*chip_sums)


SMALL_SUM_ROWS = 16


def join_halves(reduced, small_pool, small_gla, small_top, g_gk_pad):
    n = len(reduced)

    def body(*refs):
        pool_ref, gla_ref, top_ref, gk_ref = refs[n:n + 4]
        buf_refs, total_ref = refs[n + 4:2 * n + 4], refs[2 * n + 4]
        send_sems, recv_sems, all_ref, small_ref = refs[2 * n + 5:2 * n + 9]
        small_ref[0:3, :] = pool_ref[0:3, :]
        small_ref[3:5, :] = gla_ref[0:2, :]
        small_ref[5:8, :] = top_ref[0:3, :]
        for r in range(GATE_RANK):
            small_ref[8 + r // 2:9 + r // 2, (r % 2) * KEY_W:(r % 2 + 1) * KEY_W] = gk_ref[r:r + 1, :]
        x, y, c = _position()
        copies = []
        for i in range(n):
            mine, _ = _halves(buf_refs[i].shape[0], c)
            cp = pltpu.make_async_remote_copy(
                src_ref=buf_refs[i].at[mine], dst_ref=buf_refs[i].at[mine],
                send_sem=send_sems.at[i], recv_sem=recv_sems.at[i],
                device_id=(x, y, 1 - c), device_id_type=MESH)
            cp.start()
            copies.append(cp)
        _gather_small(small_ref, all_ref, *refs[2 * n + 9:])()
        total = all_ref[0]
        for dev in range(1, N_DEV):
            total = total + all_ref[dev]
        total_ref[...] = total
        for cp in copies:
            cp.wait()

    outs = pl.pallas_call(
        body, name="join_halves",
        out_shape=[jax.ShapeDtypeStruct(r.shape, F32) for r in reduced]
                  + [jax.ShapeDtypeStruct((SMALL_SUM_ROWS, D), F32)],
        in_specs=_any_specs(n) + [VMEM_SPEC] * 4, out_specs=_any_specs(n) + [VMEM_SPEC],
        input_output_aliases={i: i for i in range(n)},
        scratch_shapes=[pltpu.SemaphoreType.DMA((n,)), pltpu.SemaphoreType.DMA((n,)),
                        pltpu.VMEM((N_DEV, SMALL_SUM_ROWS, D), F32), pltpu.VMEM((SMALL_SUM_ROWS, D), F32)]
                       + SMALL_SEMS,
    )(*reduced, small_pool, small_gla, small_top, g_gk_pad)
    return outs[:n], outs[n]


ADD_ROWS = 512


def _spans(counts):
    starts, total = [], 0
    for count in counts:
        starts.append(total)
        total += count
    return starts, total


def _local_step(t, start, count):
    return jnp.clip(t - start, 0, count - 1)


def add_halves(grads, theirs, place, name):
    n = len(grads)
    shapes = [t.shape for t in theirs]
    rbs = [min(ADD_ROWS, sh[1]) for sh in shapes]
    steps = [sh[1] // rb for sh, rb in zip(shapes, rbs)]
    counts = [N_CHIPS * st for st in steps]
    starts, total = _spans(counts)

    def body(place_ref, *refs):
        a_refs, b_refs = refs[:n], refs[n:2 * n]
        f_refs, h_refs = refs[2 * n:3 * n], refs[3 * n:4 * n]
        t = pl.program_id(0)
        for i in range(n):
            @pl.when((t >= starts[i]) & (t < starts[i] + counts[i]))
            def _(i=i):
                total_i = a_refs[i][0] + b_refs[i][0]
                h_refs[i][0] = _bf(total_i)

                @pl.when((t - starts[i]) % N_CHIPS == place_ref[1])
                def _():
                    f_refs[i][...] = total_i

    def specs(i):
        block = (1, rbs[i], shapes[i][2])

        def split(t):
            local = _local_step(t, starts[i], counts[i])
            return local // N_CHIPS, local % N_CHIPS

        mine = pl.BlockSpec(block, lambda t, place: (split(t)[1], place[0] * steps[i] + split(t)[0], 0))
        same = pl.BlockSpec(block, lambda t, place: (split(t)[1], split(t)[0], 0))
        own = pl.BlockSpec(block[1:], lambda t, place: (split(t)[0], 0))
        return mine, same, own

    all_specs = [specs(i) for i in range(n)]
    outs = pl.pallas_call(
        body, name=name,
        grid_spec=pltpu.PrefetchScalarGridSpec(
            num_scalar_prefetch=1, grid=(total,),
            in_specs=[sp[0] for sp in all_specs] + [sp[1] for sp in all_specs],
            out_specs=[sp[2] for sp in all_specs] + [sp[1] for sp in all_specs]),
        out_shape=[jax.ShapeDtypeStruct(sh[1:], F32) for sh in shapes]
                  + [jax.ShapeDtypeStruct(sh, BF16) for sh in shapes],
        compiler_params=_params("arbitrary"),
    )(place, *grads, *theirs)
    return list(zip(outs[:n], outs[n:]))


def add_parts(owns, gots, place, name):
    n = len(owns)
    shapes = [g.shape for g in gots]
    rbs = [min(ADD_ROWS, sh[1]) for sh in shapes]
    counts = [sh[1] // rb for sh, rb in zip(shapes, rbs)]
    starts, total = _spans(counts)

    def body(place_ref, *refs):
        o_refs, g_refs, out_refs = refs[:n], refs[n:2 * n], refs[2 * n:]
        t = pl.program_id(0)
        for i in range(n):
            @pl.when((t >= starts[i]) & (t < starts[i] + counts[i]))
            def _(i=i):
                total_i = o_refs[i][...]
                for j in range(N_CHIPS - 1):
                    total_i = total_i + g_refs[i][j].astype(F32)
                out_refs[i][...] = total_i

    def specs(i):
        rb, cols = rbs[i], shapes[i][2]
        step = lambda t: _local_step(t, starts[i], counts[i])
        return (pl.BlockSpec((rb, cols), lambda t, place: (step(t), 0)),
                pl.BlockSpec((N_CHIPS - 1, rb, cols), lambda t, place: (0, step(t), 0)),
                pl.BlockSpec((rb, cols), lambda t, place: (place[0] * counts[i] + step(t), 0)))

    all_specs = [specs(i) for i in range(n)]
    return pl.pallas_call(
        body, name=name,
        grid_spec=pltpu.PrefetchScalarGridSpec(
            num_scalar_prefetch=1, grid=(total,),
            in_specs=[sp[0] for sp in all_specs] + [sp[1] for sp in all_specs],
            out_specs=[sp[2] for sp in all_specs]),
        out_shape=[jax.ShapeDtypeStruct((2 * sh[1], sh[2]), F32) for sh in shapes],
        compiler_params=_params("arbitrary"),
    )(place, *owns, *gots)


def _adam_math(w, g, m, v):
    m = ADAM_B1 * m + (1.0 - ADAM_B1) * g
    v = ADAM_B2 * v + (1.0 - ADAM_B2) * (g * g)
    m_hat = m / (1.0 - ADAM_B1 ** ADAM_STEP)
    v_hat = v / (1.0 - ADAM_B2 ** ADAM_STEP)
    delta = -ADAM_LR * (m_hat / (jnp.sqrt(v_hat) + ADAM_EPS) + ADAM_WD * w)
    return delta, m, v


ADAM_BLOCK_BYTES = 2 ** 19
ADAM_MOST_STEPS = 8


def adamw(params, name):
    n = len(params)
    shapes = [p[0].shape for p in params]

    def tile_rows(shape):
        rows, cols = shape[0], shape[-1]
        aligned = 1 if len(shape) == 3 else 8
        divisors = [t for t in range(aligned, rows + 1, aligned) if rows % t == 0]
        tile = max(t for t in divisors if t * cols * 4 <= ADAM_BLOCK_BYTES)
        if rows // tile > ADAM_MOST_STEPS:
            tile = min(t for t in divisors if rows // t <= ADAM_MOST_STEPS)
        return tile

    tiles = [tile_rows(sh) for sh in shapes]
    counts = [sh[0] // tl for sh, tl in zip(shapes, tiles)]
    starts, total = _spans(counts)

    def body(*refs):
        ins, outs = refs[:4 * n], refs[4 * n:]
        t = pl.program_id(0)
        for i in range(n):
            @pl.when((t >= starts[i]) & (t < starts[i] + counts[i]))
            def _(i=i):
                w_ref, g_ref, m_ref, v_ref = ins[4 * i:4 * i + 4]
                d, nm, nv = _adam_math(w_ref[...], g_ref[...], m_ref[...], v_ref[...])
                outs[3 * i][...] = d
                outs[3 * i + 1][...] = nm
                outs[3 * i + 2][...] = nv

    def spec(i):
        block = (tiles[i],) + shapes[i][1:]
        zeros = (0,) * (len(block) - 1)
        return pl.BlockSpec(block, lambda t: (_local_step(t, starts[i], counts[i]),) + zeros)

    outs = pl.pallas_call(
        body, name=name, grid=(total,),
        out_shape=[jax.ShapeDtypeStruct(sh, F32) for sh in shapes for _ in range(3)],
        in_specs=[spec(i) for i in range(n) for _ in range(4)],
        out_specs=[spec(i) for i in range(n) for _ in range(3)],
        compiler_params=_params("arbitrary"),
    )(*[a for p in params for a in p])
    return [tuple(outs[3 * i:3 * i + 3]) for i in range(n)]


def adamw_small(params):
    n = len(params)

    def body(*refs):
        ins, outs = refs[:4 * n], refs[4 * n:]
        for k in range(n):
            w_ref, g_ref, m_ref, v_ref = ins[4 * k:4 * k + 4]
            d, nm, nv = _adam_math(w_ref[...], g_ref[...], m_ref[...], v_ref[...])
            outs[3 * k][...] = d
            outs[3 * k + 1][...] = nm
            outs[3 * k + 2][...] = nv

    flat = [a for p in params for a in p]
    outs = pl.pallas_call(
        body, name="adamw_small",
        out_shape=[jax.ShapeDtypeStruct(p[0].shape, F32) for p in params for _ in range(3)],
        in_specs=[VMEM_SPEC] * (4 * n), out_specs=[VMEM_SPEC] * (3 * n),
    )(*flat)
    return [tuple(outs[3 * k:3 * k + 3]) for k in range(n)]


def matmul_tn(a, b, name, tile_n=512, by_column_tile=False, narrow=None):
    s, m = a.shape
    n = b.shape[1]
    tile_n = min(tile_n, n)
    if by_column_tile:
        out_shape = jax.ShapeDtypeStruct((n // tile_n, m, tile_n), F32)
        out_spec = pl.BlockSpec((None, m, tile_n), lambda j: (j, 0, 0))
    else:
        out_shape = jax.ShapeDtypeStruct((m, n), F32)
        out_spec = pl.BlockSpec((m, tile_n), lambda j: (0, j))

    if narrow is None:
        def body(a_ref, b_ref, out_ref):
            out_ref[...] = _tn(a_ref[...], b_ref[...])

        return pl.pallas_call(
            body, name=name, grid=(n // tile_n,),
            out_shape=out_shape,
            in_specs=[_full((s, m)), pl.BlockSpec((s, tile_n), lambda j: (0, j))],
            out_specs=out_spec,
            compiler_params=_params("parallel"),
        )(a, b)

    def body_with_narrow(a_ref, b_ref, c_ref, out_ref, out_c_ref):
        out_ref[...] = _tn(a_ref[...], b_ref[...])

        @pl.when(pl.program_id(0) == 0)
        def _():
            out_c_ref[...] = _tn(a_ref[...], c_ref[...])

    return pl.pallas_call(
        body_with_narrow, name=name, grid=(n // tile_n,),
        out_shape=(out_shape, jax.ShapeDtypeStruct((m, narrow.shape[1]), F32)),
        in_specs=[_full((s, m)), pl.BlockSpec((s, tile_n), lambda j: (0, j)), _full(narrow.shape)],
        out_specs=(out_spec, _full((m, narrow.shape[1]))),
        compiler_params=_params("arbitrary"),
    )(a, b, narrow)


ROW_TILE = 512


def _row_index(tile, rows):
    return tile * rows + lax.broadcasted_iota(jnp.int32, (rows, 1), 0)


def _inverse_counts(t_glob):
    return [1.0 / jnp.minimum(t_glob + 1, w).astype(F32) for w in POOL_WINDOWS]


def _sigmoid(z):
    return 1.0 / (1.0 + jnp.exp(-z))


def _trailing_sums(src, tmp, cols, window, rows):
    bufs = (src, tmp)
    span, level, start = 1, 0, 0
    while span < window:
        start += 8
        a, b = bufs[level % 2], bufs[(level + 1) % 2]
        n = HALO + rows - start
        b[start:start + n, cols] = a[start:start + n, cols] + a[start - span:start - span + n, cols]
        span, level = 2 * span, level + 1
    return bufs[level % 2][HALO:HALO + rows, cols]


def _leading_sums(src, tmp, cols, window, rows):
    bufs = (src, tmp)
    span, level, n = 1, 0, rows + HALO
    while span < window:
        n -= 8
        a, b = bufs[level % 2], bufs[(level + 1) % 2]
        b[0:n, cols] = a[0:n, cols] + a[span:span + n, cols]
        span, level = 2 * span, level + 1
    return bufs[level % 2][0:rows, cols]


def gather_in_background(step, last, out_refs, send_sems, recv_sems, finish):
    n = len(out_refs)
    x, y, c = _position()
    q = 2 * x + y
    chips = _other_chips(x, y)

    def copy(k, i, quarter, half, to):
        return _gather_copy(out_refs[i], send_sems, recv_sems, k * n + i, quarter, half, to)

    if not finish:
        @pl.when(step == 0)
        def _():
            for i in range(n):
                mine, _ = _halves(out_refs[i].shape[1], c)
                for j, chip in enumerate(chips):
                    copy(j, i, q, mine, (*chip, c)).start()

        @pl.when(step == last)
        def _():
            for j, chip in enumerate(chips):
                qj = 2 * chip[0] + chip[1]
                for i in range(n):
                    mine, _ = _halves(out_refs[i].shape[1], c)
                    copy(j, i, qj, mine, (x, y, c)).wait_recv()
                    copy(3 + j, i, qj, mine, (x, y, 1 - c)).start()
        return

    @pl.when(step == last)
    def _():
        for j, chip in enumerate(chips):
            qj = 2 * chip[0] + chip[1]
            for i in range(n):
                mine, other = _halves(out_refs[i].shape[1], c)
                copy(3 + j, i, qj, other, (x, y, c)).wait_recv()
                copy(j, i, q, mine, (x, y, c)).wait_send()
                copy(3 + j, i, qj, mine, (x, y, c)).wait_send()


def pool_forward(x, w0, wpi, gw, gb, scale, wpo, later):
    s = x.shape[0]
    ts = ROW_TILE
    nt = s // ts
    assert nt >= 2
    n_later = len(later)

    def body(x_ref, w0_ref, wpi_ref, gw_ref, gb_ref, sc_ref, wpo_ref, *rest):
        rest = rest[n_later:]
        h1_ref, pooled_ref, gt_ref, n0_ref = rest[:4]
        later_refs = rest[4:4 + n_later]
        ubuf, tbuf, hist, send_sems, recv_sems = rest[4 + n_later:]
        i = pl.program_id(0)
        gather_in_background(i, nt - 1, later_refs, send_sems, recv_sems, finish=False)
        xv = x_ref[...]
        r = lax.rsqrt(jnp.mean(xv * xv, axis=-1, keepdims=True) + EPS)
        n0 = _bf(xv * r * w0_ref[...])
        n0_ref[...] = n0
        u = jnp.concatenate([_nn(n0, wpi_ref[0]), _nn(n0, wpi_ref[1])], axis=-1)
        gt = jnp.concatenate([_nn(n0, wpi_ref[2]), _nn(n0, wpi_ref[3])], axis=-1)
        gt_ref[...] = gt

        @pl.when(i == 0)
        def _():
            hist[...] = jnp.zeros_like(hist)

        ubuf[0:HALO, :] = hist[...]
        ubuf[HALO:HALO + ts, :] = u
        hist[...] = u[ts - HALO:, :]
        inv = _inverse_counts(_row_index(i, ts))
        mixed = []
        for g, w in enumerate(POOL_WINDOWS):
            cols = slice(g * GROUP_DIM, (g + 1) * GROUP_DIM)
            pooled = _bf(_trailing_sums(ubuf, tbuf, cols, w, ts) * inv[g] - u[:, cols])
            pooled_ref[:, cols] = pooled
            mixed.append(_nn(pooled, gw_ref[g]))
        mixed = jnp.concatenate(mixed, axis=-1) + gb_ref[...]
        y = mixed * sc_ref[...] * (gt * _sigmoid(gt))
        h1_ref[...] = xv + _nn(_bf(y), wpo_ref[...])
        gather_in_background(i, nt - 1, later_refs, send_sems, recv_sems, finish=True)

    row = lambda cols: pl.BlockSpec((ts, cols), lambda i: (i, 0))
    outs = pl.pallas_call(
        body, name="pool_forward", grid=(nt,),
        out_shape=[jax.ShapeDtypeStruct((s, D), F32), jax.ShapeDtypeStruct((s, D), BF16),
                   jax.ShapeDtypeStruct((s, D), F32), jax.ShapeDtypeStruct((s, D), BF16)]
                  + [jax.ShapeDtypeStruct(a.shape, a.dtype) for a in later],
        in_specs=[row(D), _full((1, D)), _full((N_CHIPS, D, D // 2)), _full((GROUPS, GROUP_DIM, GROUP_DIM)),
                  _full((1, D)), _full((1, D)), _full((D, D))] + _any_specs(n_later),
        out_specs=[row(D), row(D), row(D), row(D)] + _any_specs(n_later),
        input_output_aliases={7 + k: 4 + k for k in range(n_later)},
        scratch_shapes=[pltpu.VMEM((HALO + ts, D), F32), pltpu.VMEM((HALO + ts, D), F32),
                        pltpu.VMEM((HALO, D), F32),
                        pltpu.SemaphoreType.DMA((6 * n_later,)), pltpu.SemaphoreType.DMA((6 * n_later,))],
        compiler_params=_params("arbitrary"),
    )(x, w0, wpi, gw, gb, scale, wpo, *later)
    return outs[:4], outs[4:]


def pool_backward(x, dh1, pooled, gt, w0, wpi, gw, gb, scale, wpo, chip_sums):
    s = x.shape[0]
    ts = ROW_TILE
    nt = s // ts
    n_sums = len(chip_sums)

    def body(x_ref, dh1_ref, pooled_ref, gt_ref, w0_ref, wpi_ref, gw_ref, gb_ref, sc_ref, wpo_ref, *rest):
        sum_refs, rest = rest[:n_sums], rest[n_sums:]
        dx_ref, dproj_ref, gpo_ref, ggw_ref, small_ref = rest[:5]
        got_refs = rest[5:5 + n_sums]
        ebuf, tbuf, ahead, send_sems, recv_sems = rest[5 + n_sums:]
        i = pl.program_id(0)
        copies = _scatter_copies(sum_refs, got_refs, send_sems, recv_sems)

        @pl.when(i == 0)
        def _():
            for cp in copies:
                cp.start()

        @pl.when(i == 0)
        def _():
            gpo_ref[...] = jnp.zeros_like(gpo_ref)
            ggw_ref[...] = jnp.zeros_like(ggw_ref)
            small_ref[...] = jnp.zeros_like(small_ref)
            ahead[...] = jnp.zeros_like(ahead)

        dh1 = dh1_ref[...]
        dh1_bf = _bf(dh1)
        gt = gt_ref[...]
        sc = sc_ref[...]
        dy = _nt(dh1_bf, wpo_ref[...])
        pooled_bf = []
        mixed = []
        for g in range(GROUPS):
            cols = slice(g * GROUP_DIM, (g + 1) * GROUP_DIM)
            pb = pooled_ref[:, cols]
            pooled_bf.append(pb)
            mixed.append(_nn(pb, gw_ref[g]))
        mixed = jnp.concatenate(mixed, axis=-1) + gb_ref[...]
        sg = _sigmoid(gt)
        silu = gt * sg
        gpo_ref[...] += _tn(_bf(mixed * sc * silu), dh1_bf)
        dmixed = dy * sc * silu
        dgt = dy * mixed * sc * (sg * (1.0 + gt * (1.0 - sg)))
        dproj_ref[:, D:] = _bf(dgt)
        small_ref[1:2, :] += jnp.sum(dy * mixed * silu, axis=0, keepdims=True)
        small_ref[2:3, :] += jnp.sum(dmixed, axis=0, keepdims=True)

        inv = _inverse_counts(_row_index(nt - 1 - i, ts))
        ebuf[ts:ts + HALO, :] = ahead[...]
        dpooled = []
        for g in range(GROUPS):
            cols = slice(g * GROUP_DIM, (g + 1) * GROUP_DIM)
            dm = _bf(dmixed[:, cols])
            ggw_ref[g] += _tn(pooled_bf[g], dm)
            dp = _nt(dm, gw_ref[g])
            dpooled.append(dp)
            ebuf[0:ts, cols] = dp * inv[g]
        ahead[...] = ebuf[0:HALO, :]
        du = []
        for g, w in enumerate(POOL_WINDOWS):
            cols = slice(g * GROUP_DIM, (g + 1) * GROUP_DIM)
            du.append(_leading_sums(ebuf, tbuf, cols, w, ts) - dpooled[g])
        du = _bf(jnp.concatenate(du, axis=-1))
        dproj_ref[:, :D] = du
        dgt_bf = _bf(dgt)
        half = D // 2
        dn0 = (_nt(du[:, :half], wpi_ref[0]) + _nt(du[:, half:], wpi_ref[1])
               + _nt(dgt_bf[:, :half], wpi_ref[2]) + _nt(dgt_bf[:, half:], wpi_ref[3]))

        xv = x_ref[...]
        r = lax.rsqrt(jnp.mean(xv * xv, axis=-1, keepdims=True) + EPS)
        xhat = xv * r
        small_ref[0:1, :] += jnp.sum(dn0 * xhat, axis=0, keepdims=True)
        dxh = dn0 * w0_ref[...]
        dx_ref[...] = dh1 + r * (dxh - xhat * jnp.mean(dxh * xhat, axis=-1, keepdims=True))

        @pl.when(i == nt - 1)
        def _():
            for cp in copies:
                cp.wait()

    row = lambda cols: pl.BlockSpec((ts, cols), lambda i: (nt - 1 - i, 0))
    outs = pl.pallas_call(
        body, name="pool_backward", grid=(nt,),
        out_shape=[jax.ShapeDtypeStruct((s, D), F32), jax.ShapeDtypeStruct((s, 2 * D), BF16),
                   jax.ShapeDtypeStruct((D, D), F32),
                   jax.ShapeDtypeStruct((GROUPS, GROUP_DIM, GROUP_DIM), F32),
                   jax.ShapeDtypeStruct((8, D), F32)] + _scatter_shapes(chip_sums),
        in_specs=[row(D), row(D), row(D), row(D), _full((1, D)), _full((N_CHIPS, D, D // 2)),
                  _full((GROUPS, GROUP_DIM, GROUP_DIM)), _full((1, D)), _full((1, D)), _full((D, D))]
                 + _any_specs(n_sums),
        out_specs=[row(D), row(2 * D), _full((D, D)), _full((GROUPS, GROUP_DIM, GROUP_DIM)), _full((8, D))]
                  + _any_specs(n_sums),
        scratch_shapes=[pltpu.VMEM((ts + HALO, D), F32), pltpu.VMEM((ts + HALO, D), F32),
                        pltpu.VMEM((HALO, D), F32),
                        pltpu.SemaphoreType.DMA((3 * n_sums,)), pltpu.SemaphoreType.DMA((3 * n_sums,))],
        compiler_params=_params("arbitrary"),
    )(x, dh1, pooled, gt, w0, wpi, gw, gb, scale, wpo, *chip_sums)
    return outs[:5], outs[5:]


def gla_project(h1, w1, wgi, wlow, wgk, bgk, later):
    s = h1.shape[0]
    ts = ROW_TILE
    nt = s // ts
    assert nt >= 2
    n_later = len(later)

    def body(h_ref, w1_ref, wgi_ref, wlow_ref, wgk_ref, bgk_ref, *rest):
        rest = rest[n_later:]
        qk_ref, v_ref, gate_ref, low_ref, cum_ref, n1_ref = rest[:6]
        later_refs = rest[6:6 + n_later]
        send_sems, recv_sems = rest[6 + n_later:]
        gather_in_background(pl.program_id(0), nt - 1, later_refs, send_sems, recv_sems, finish=False)
        hv = h_ref[...]
        r = lax.rsqrt(jnp.mean(hv * hv, axis=-1, keepdims=True) + EPS)
        n1 = _bf(hv * r * w1_ref[...])
        n1_ref[...] = n1
        qk_ref[...] = _nn(n1, wgi_ref[:, 0:2 * KEY_W])
        v_ref[...] = _bf(_nn(n1, wgi_ref[:, 2 * KEY_W:2 * KEY_W + D]))
        gate_ref[...] = _nn(n1, wgi_ref[:, 2 * KEY_W + D:GLA_MAIN])
        low = _bf(_nn(n1, wlow_ref[...]))
        low_ref[...] = low
        z = _nn(low, wgk_ref[...]) + bgk_ref[...]
        lg = (jnp.minimum(z, 0.0) - jnp.log(1.0 + jnp.exp(-jnp.abs(z)))) / GATE_NORM
        lower_f = _chunk_masks()[0].astype(F32)
        for r0 in range(0, ts, CHUNK):
            cum_ref[r0:r0 + CHUNK, :] = _nn_exact(lower_f, lg[r0:r0 + CHUNK, :])
        gather_in_background(pl.program_id(0), nt - 1, later_refs, send_sems, recv_sems, finish=True)

    row = lambda cols: pl.BlockSpec((ts, cols), lambda i: (i, 0))
    outs = pl.pallas_call(
        body, name="gla_project", grid=(nt,),
        out_shape=[jax.ShapeDtypeStruct((s, D), F32), jax.ShapeDtypeStruct((s, D), BF16),
                   jax.ShapeDtypeStruct((s, D), F32), jax.ShapeDtypeStruct((s, RANK_PAD), BF16),
                   jax.ShapeDtypeStruct((s, KEY_W), F32), jax.ShapeDtypeStruct((s, D), BF16)]
                  + [jax.ShapeDtypeStruct(a.shape, a.dtype) for a in later],
        in_specs=[row(D), _full((1, D)), _full((D, GLA_MAIN)), _full((D, RANK_PAD)),
                  _full((RANK_PAD, KEY_W)), _full((1, KEY_W))] + _any_specs(n_later),
        out_specs=[row(D), row(D), row(D), row(RANK_PAD), row(KEY_W), row(D)] + _any_specs(n_later),
        input_output_aliases={6 + k: 6 + k for k in range(n_later)},
        scratch_shapes=[pltpu.SemaphoreType.DMA((6 * n_later,)), pltpu.SemaphoreType.DMA((6 * n_later,))],
        compiler_params=_params("arbitrary"),
    )(h1, w1, wgi, wlow, wgk, bgk, *later)
    return outs[:6], outs[6:]


GLA_BLOCK = 512
CHUNKS_PER_BLOCK = GLA_BLOCK // CHUNK


def _chunk_masks():
    t = lax.broadcasted_iota(jnp.int32, (CHUNK, CHUNK), 0)
    u = lax.broadcasted_iota(jnp.int32, (CHUNK, CHUNK), 1)
    return t >= u, t <= u


def _gla_chunk_terms(q, cum):
    ep = jnp.exp(cum)
    en = jnp.exp(-cum)
    qs = q * (HEAD_K ** -0.5)
    last = cum[CHUNK - 1:CHUNK, :]
    ed = jnp.exp(last - cum)
    dec = jnp.exp(last)
    return ep, en, qs, ed, dec


def gla_forward(qk, v, cum):
    s = qk.shape[0]
    nb = s // GLA_BLOCK
    nc = s // CHUNK

    def body(q_ref, k_ref, v_ref, cum_ref, o_ref, st_ref, sc_ref, state):
        @pl.when(pl.program_id(0) == 0)
        def _():
            state[...] = jnp.zeros_like(state)

        lower, _ = _chunk_masks()

        def chunk(cc, carry):
            rows = pl.ds(pl.multiple_of(cc * CHUNK, CHUNK), CHUNK)
            for h in range(HEADS):
                kc = slice(h * HEAD_K, (h + 1) * HEAD_K)
                vc = slice(h * HEAD_V, (h + 1) * HEAD_V)
                q = q_ref[rows, kc]
                k = k_ref[rows, kc]
                v = v_ref[rows, vc]
                ep, en, qs, ed, dec = _gla_chunk_terms(q, cum_ref[rows, kc])
                a = _bf(qs * ep)
                fwd = _nt(a, _bf(k * en))
                bwd = _nt(_bf(qs * en), _bf(k * ep))
                scores = _bf(jnp.where(lower, fwd, bwd))
                sc_ref[rows, h * CHUNK:(h + 1) * CHUNK] = scores
                st = state[h]
                st_ref[cc, h] = st
                o_ref[rows, vc] = _nn(scores, v) + _nt(a, _bf(st))
                state[h] = st * dec + _tn(v, _bf(k * ed))
            return carry

        lax.fori_loop(0, CHUNKS_PER_BLOCK, chunk, 0, unroll=4)

    return pl.pallas_call(
        body, name="gla_forward", grid=(nb,),
        out_shape=(jax.ShapeDtypeStruct((s, D), F32),
                   jax.ShapeDtypeStruct((nc, HEADS, HEAD_V, HEAD_K), F32),
                   jax.ShapeDtypeStruct((s, HEADS * CHUNK), BF16)),
        in_specs=[pl.BlockSpec((GLA_BLOCK, KEY_W), lambda i: (i, 0)),
                  pl.BlockSpec((GLA_BLOCK, KEY_W), lambda i: (i, 1)),
                  pl.BlockSpec((GLA_BLOCK, D), lambda i: (i, 0)),
                  pl.BlockSpec((GLA_BLOCK, KEY_W), lambda i: (i, 0))],
        out_specs=(pl.BlockSpec((GLA_BLOCK, D), lambda i: (i, 0)),
                   pl.BlockSpec((CHUNKS_PER_BLOCK, HEADS, HEAD_V, HEAD_K), lambda i: (i, 0, 0, 0)),
                   pl.BlockSpec((GLA_BLOCK, HEADS * CHUNK), lambda i: (i, 0))),
        scratch_shapes=[pltpu.VMEM((HEADS, HEAD_V, HEAD_K), F32)],
        compiler_params=_params("arbitrary"),
    )(qk, qk, v, cum)


def gla_backward(qk, v, cum, do, states, scores):
    s = qk.shape[0]
    nb = s // GLA_BLOCK

    def body(q_ref, k_ref, v_ref, cum_ref, do_ref, st_ref, sc_ref, dq_ref, dk_ref, dv_ref, dcum_ref, dstate):
        @pl.when(pl.program_id(0) == 0)
        def _():
            dstate[...] = jnp.zeros_like(dstate)

        lower, _ = _chunk_masks()
        is_last = lax.broadcasted_iota(jnp.int32, (CHUNK, HEAD_K), 0) == CHUNK - 1

        def chunk(step, carry):
            cc = CHUNKS_PER_BLOCK - 1 - step
            rows = pl.ds(pl.multiple_of(cc * CHUNK, CHUNK), CHUNK)
            for h in range(HEADS):
                kc = slice(h * HEAD_K, (h + 1) * HEAD_K)
                vc = slice(h * HEAD_V, (h + 1) * HEAD_V)
                q = q_ref[rows, kc]
                k = k_ref[rows, kc]
                v = v_ref[rows, vc]
                do_c = do_ref[rows, vc]
                ep, en, qs, ed, dec = _gla_chunk_terms(q, cum_ref[rows, kc])
                a = _bf(qs * ep)
                b = _bf(k * en)
                c = _bf(qs * en)
                dk_dec = _bf(k * ep)
                kd = _bf(k * ed)
                scores = sc_ref[rows, h * CHUNK:(h + 1) * CHUNK]
                st = st_ref[cc, h]
                dst = dstate[h]
                dst_bf = _bf(dst)

                dscores = _nt(do_c, v)
                dfwd = _bf(jnp.where(lower, dscores, 0.0))
                dbwd = _bf(jnp.where(lower, 0.0, dscores))
                dv_ref[rows, vc] = _bf(_tn(scores, do_c) + _nt(kd, dst_bf))
                da = _nn(dfwd, b) + _nn(do_c, _bf(st))
                db = _tn(dfwd, a)
                dc = _nn(dbwd, dk_dec)
                ddk = _tn(dbwd, c)
                dkd = _nn(v, dst_bf)
                ddec = jnp.sum(dst * st, axis=0, keepdims=True)
                dstate[h] = dst * dec + _tn(do_c, a)

                m = dkd * k * ed
                dq_ref[rows, kc] = _bf((da * ep + dc * en) * (HEAD_K ** -0.5))
                dk_ref[rows, kc] = _bf(db * en + ddk * ep + dkd * ed)
                dcum = (da * qs + ddk * k) * ep - (db * k + dc * qs) * en - m
                dlast = jnp.sum(m, axis=0, keepdims=True) + ddec * dec
                dcum_ref[rows, kc] = dcum + jnp.where(is_last, dlast, 0.0)
            return carry

        lax.fori_loop(0, CHUNKS_PER_BLOCK, chunk, 0, unroll=4)

    rev = lambda cols, col_block: pl.BlockSpec((GLA_BLOCK, cols), lambda i: (nb - 1 - i, col_block))
    return pl.pallas_call(
        body, name="gla_backward", grid=(nb,),
        out_shape=(jax.ShapeDtypeStruct((s, KEY_W), BF16), jax.ShapeDtypeStruct((s, KEY_W), BF16),
                   jax.ShapeDtypeStruct((s, D), BF16), jax.ShapeDtypeStruct((s, KEY_W), F32)),
        in_specs=[rev(KEY_W, 0), rev(KEY_W, 1), rev(D, 0), rev(KEY_W, 0), rev(D, 0),
                  pl.BlockSpec((CHUNKS_PER_BLOCK, HEADS, HEAD_V, HEAD_K), lambda i: (nb - 1 - i, 0, 0, 0)),
                  rev(HEADS * CHUNK, 0)],
        out_specs=(rev(KEY_W, 0), rev(KEY_W, 0), rev(D, 0), rev(KEY_W, 0)),
        scratch_shapes=[pltpu.VMEM((HEADS, HEAD_V, HEAD_K), F32)],
        compiler_params=_params("arbitrary"),
    )(qk, qk, v, cum, do, states, scores)


def head_and_loss(o, gate, h1, target, hw, wgo, wf):
    s = o.shape[0]
    ts = ROW_TILE

    def body(o_ref, gate_ref, h1_ref, tgt_ref, hw_ref, wgo_ref, wf_ref,
             dh2_ref, do_ref, dgate_ref, ggo_ref, small_ref):
        @pl.when(pl.program_id(0) == 0)
        def _():
            ggo_ref[...] = jnp.zeros_like(ggo_ref)
            small_ref[...] = jnp.zeros_like(small_ref)

        gate = gate_ref[...]
        hw = hw_ref[...]
        sg = _sigmoid(gate)
        silu = gate * sg
        ohat, ro = [], []
        for h in range(HEADS):
            oh = o_ref[:, h * HEAD_V:(h + 1) * HEAD_V]
            rh = lax.rsqrt(jnp.mean(oh * oh, axis=-1, keepdims=True) + EPS)
            ro.append(rh)
            ohat.append(oh * rh)
        ohat = jnp.concatenate(ohat, axis=-1)
        on = ohat * hw
        y2 = _bf(on * silu)
        h2 = h1_ref[...] + _nn(y2, wgo_ref[...])
        rf = lax.rsqrt(jnp.mean(h2 * h2, axis=-1, keepdims=True) + EPS)
        h2hat = h2 * rf
        wf = wf_ref[...]
        diff = h2hat * wf - tgt_ref[...]
        small_ref[2:3, :] += jnp.zeros((1, D), F32) + 0.5 * jnp.sum(diff * diff) / D
        dout = diff / D
        small_ref[0:1, :] += jnp.sum(dout * h2hat, axis=0, keepdims=True)
        dxh = dout * wf
        dh2 = rf * (dxh - h2hat * jnp.mean(dxh * h2hat, axis=-1, keepdims=True))
        dh2_ref[...] = dh2
        dh2_bf = _bf(dh2)
        ggo_ref[...] += _tn(y2, dh2_bf)
        dy2 = _nt(dh2_bf, wgo_ref[...])
        don = dy2 * silu
        dgate_ref[...] = _bf(dy2 * on * (sg * (1.0 + gate * (1.0 - sg))))
        ghw = jnp.sum(don * ohat, axis=0, keepdims=True)
        small_ref[1:2, 0:HEAD_V] += sum(ghw[:, h * HEAD_V:(h + 1) * HEAD_V] for h in range(HEADS))
        dohat = don * hw
        for h in range(HEADS):
            cols = slice(h * HEAD_V, (h + 1) * HEAD_V)
            oh, dh = ohat[:, cols], dohat[:, cols]
            do_ref[:, cols] = _bf(ro[h] * (dh - oh * jnp.mean(dh * oh, axis=-1, keepdims=True)))

    row = lambda cols: pl.BlockSpec((ts, cols), lambda i: (i, 0))
    act = jax.ShapeDtypeStruct((s, D), F32)
    act_bf = jax.ShapeDtypeStruct((s, D), BF16)
    return pl.pallas_call(
        body, name="head_and_loss", grid=(s // ts,),
        out_shape=(act, act_bf, act_bf, jax.ShapeDtypeStruct((D, D), F32), jax.ShapeDtypeStruct((8, D), F32)),
        in_specs=[row(D), row(D), row(D), row(D),
                  _full((1, D)), _full((D, D)), _full((1, D))],
        out_specs=(row(D), row(D), row(D), _full((D, D)), _full((8, D))),
        compiler_params=_params("arbitrary"),
    )(o, gate, h1, target, hw, wgo, wf)


def gla_project_backward(dq, dk, dv, dgate, dcum, low, h1, dh2, w1, wgi, wlow, wgk, bgk):
    s = h1.shape[0]
    ts = ROW_TILE

    def body(dq_ref, dk_ref, dv_ref, dgate_ref, dcum_ref, low_ref, h1_ref, dh2_ref, w1_ref,
             wgi_ref, wlow_ref, wgk_ref, bgk_ref, dh1_ref, dproj_ref, dlow_ref, ggk_ref, small_ref):
        @pl.when(pl.program_id(0) == 0)
        def _():
            ggk_ref[...] = jnp.zeros_like(ggk_ref)
            small_ref[...] = jnp.zeros_like(small_ref)

        low = low_ref[...]
        z = _nn(low, wgk_ref[...]) + bgk_ref[...]
        upper_f = _chunk_masks()[1].astype(F32)
        dlg = jnp.concatenate([_nn_exact(upper_f, dcum_ref[r0:r0 + CHUNK, :]) for r0 in range(0, ts, CHUNK)],
                              axis=0)
        dz = dlg * (1.0 / GATE_NORM) * _sigmoid(-z)
        dz_bf = _bf(dz)
        ggk_ref[...] += _tn(low, dz_bf)
        small_ref[1:2, 0:KEY_W] += jnp.sum(dz, axis=0, keepdims=True)
        dlow = _bf(_nt(dz_bf, wgk_ref[...]))
        dlow_ref[...] = dlow
        dn1 = _nt(dlow, wlow_ref[...])
        for ref, lo, hi in ((dq_ref, 0, KEY_W), (dk_ref, KEY_W, 2 * KEY_W),
                            (dv_ref, 2 * KEY_W, 2 * KEY_W + D), (dgate_ref, 2 * KEY_W + D, GLA_MAIN)):
            piece = ref[...]
            dproj_ref[:, lo:hi] = piece
            dn1 = dn1 + _nt(piece, wgi_ref[:, lo:hi])
        hv = h1_ref[...]
        r = lax.rsqrt(jnp.mean(hv * hv, axis=-1, keepdims=True) + EPS)
        hhat = hv * r
        small_ref[0:1, :] += jnp.sum(dn1 * hhat, axis=0, keepdims=True)
        dxh = dn1 * w1_ref[...]
        dh1_ref[...] = dh2_ref[...] + r * (dxh - hhat * jnp.mean(dxh * hhat, axis=-1, keepdims=True))

    row = lambda cols: pl.BlockSpec((ts, cols), lambda i: (i, 0))
    return pl.pallas_call(
        body, name="gla_project_backward", grid=(s // ts,),
        out_shape=(jax.ShapeDtypeStruct((s, D), F32), jax.ShapeDtypeStruct((s, GLA_MAIN), BF16),
                   jax.ShapeDtypeStruct((s, RANK_PAD), BF16), jax.ShapeDtypeStruct((RANK_PAD, KEY_W), F32),
                   jax.ShapeDtypeStruct((8, D), F32)),
        in_specs=[row(KEY_W), row(KEY_W), row(D), row(D), row(KEY_W), row(RANK_PAD), row(D), row(D),
                  _full((1, D)), _full((D, GLA_MAIN)), _full((D, RANK_PAD)), _full((RANK_PAD, KEY_W)),
                  _full((1, KEY_W))],
        out_specs=(row(D), row(GLA_MAIN), row(RANK_PAD), _full((RANK_PAD, KEY_W)), _full((8, D))),
        compiler_params=_params("arbitrary"),
    )(dq, dk, dv, dgate, dcum, low, h1, dh2, w1, wgi, wlow, wgk, bgk)


def _groups_from_quarters(a):
    return a.reshape(N_CHIPS, GROUPS, 64, GROUP_DIM).transpose(1, 0, 2, 3).reshape(GROUPS, GROUP_DIM, GROUP_DIM)


def _quarters_from_groups(a):
    return a.reshape(GROUPS, N_CHIPS, 64, GROUP_DIM).transpose(1, 0, 2, 3).reshape(N_CHIPS, GROUP_DIM, GROUP_DIM)


def _gla_in_weights(wgi_q):
    wgi_all = jnp.concatenate([wgi_q[q] for q in range(N_CHIPS)], axis=1)
    wlow = jnp.pad(wgi_all[:, GLA_MAIN:], ((0, 0), (0, RANK_PAD - GATE_RANK)))
    return wgi_all, wlow


def local_gradients(xs, target, w0, w1, wf, wpi, gw, gb, scale, wpo, gla_quarters, wgk, bgk, hw_tiled, place):
    wgi_q, wgo_q = gla_quarters
    (h1, pooled, gt, n0), (wgi_q,) = pool_forward(xs, w0, wpi, gw, gb, scale, wpo, [wgi_q])
    wgi, wlow = _gla_in_weights(wgi_q)
    (qk, v, gate, low, cum, n1), (wgo_q,) = gla_project(h1, w1, wgi, wlow, wgk, bgk, [wgo_q])
    wgo = wgo_q.reshape(D, D)
    o, states, scores = gla_forward(qk, v, cum)

    dh2, do, dgate, g_gla_out, small_top = head_and_loss(o, gate, h1, target, hw_tiled, wgo, wf)
    dq, dk, dv, dcum = gla_backward(qk, v, cum, do, states, scores)
    dh1, dproj, dlow, g_gk_pad, small_gla = gla_project_backward(
        dq, dk, dv, dgate, dcum, low, h1, dh2, w1, wgi, wlow, wgk, bgk)
    g_gla_main, g_gla_low = matmul_tn(n1, dproj, "grad_gla_in", narrow=dlow)
    g_gla_in = jnp.concatenate([g_gla_main, g_gla_low[:, :GATE_RANK]], axis=1)

    def chip_sums(grads, tag):
        theirs = exchange_with_sibling(grads, "exchange_with_sibling_" + tag)
        return add_halves(grads, theirs, place, "add_halves_" + tag)

    gla_sums = chip_sums(
        [jnp.stack([g_gla_in[:, GLA_IN_QUARTER * q:GLA_IN_QUARTER * (q + 1)] for q in range(N_CHIPS)]),
         g_gla_out.reshape(N_CHIPS, D // N_CHIPS, D)], "gla")
    (dx, dpool, g_pool_out, g_group_w, small_pool), gla_got = pool_backward(
        xs, dh1, pooled, gt, w0, wpi, gw, gb, scale, wpo, [b for _, b in gla_sums])
    g_pool_in = matmul_tn(n0, dpool, "grad_pool_in", by_column_tile=True)

    pool_sums = chip_sums(
        [g_pool_in, _quarters_from_groups(g_group_w), g_pool_out.reshape(N_CHIPS, D // N_CHIPS, D)], "pool")
    pool_got = scatter_to_owners([b for _, b in pool_sums], "scatter_to_owners_pool")
    reduced, total = join_halves(
        add_parts([f for f, _ in pool_sums + gla_sums], list(pool_got) + list(gla_got), place, "add_parts"),
        small_pool, small_gla, small_top, g_gk_pad)
    return dx, reduced, total


def kernel(x, norm_w, pool_in_w, pool_group_w, pool_group_b, pool_scale, pool_out_w, gla_in_w, gla_gk_w, gla_gk_b, gla_head_norm_w, gla_out_w, final_norm_w, loss_target, m_norm_w, m_pool_in_w, m_pool_group_w, m_pool_group_b, m_pool_scale, m_pool_out_w, m_gla_in_w, m_gla_gk_w, m_gla_gk_b, m_gla_head_norm_w, m_gla_out_w, m_final_norm_w, v_norm_w, v_pool_in_w, v_pool_group_w, v_pool_group_b, v_pool_scale, v_pool_out_w, v_gla_in_w, v_gla_gk_w, v_gla_gk_b, v_gla_head_norm_w, v_gla_out_w, v_final_norm_w):
    xs = x[0]
    target = loss_target[0]
    q_chip = 2 * lax.axis_index("x") + lax.axis_index("y")
    place = jnp.stack([lax.axis_index("c"), q_chip]).astype(jnp.int32)

    (wpi, gw_q, wpo_q, wgi_q, wgo_q), small_all = allgather_weights(
        [pool_in_w[0], pool_group_w[0].reshape(GROUP_DIM, GROUP_DIM), pool_out_w[0], gla_in_w[0], gla_out_w[0]],
        exchange=(True, True, True, False, False),
        smalls=[gla_gk_b, gla_head_norm_w, pool_group_b[0], gla_gk_w[0]])
    gw = _groups_from_quarters(gw_q)
    wpo = wpo_q.reshape(D, D)
    small_all = small_all[0::2]
    bgk = small_all[:, 0, :].reshape(1, KEY_W)
    hw = small_all[:, 1, 0:64].reshape(1, HEAD_V)
    gb = small_all[:, 2:2 + GROUPS, 0:64].transpose(1, 0, 2).reshape(1, D)
    wgk16 = small_all[:, 8:8 + GATE_RANK, :].transpose(1, 0, 2).reshape(GATE_RANK, KEY_W)
    wgk = _bf(jnp.pad(wgk16, ((0, RANK_PAD - GATE_RANK), (0, 0))))
    hw_tiled = jnp.tile(hw, (1, HEADS))

    w0 = norm_w[0:1]
    w1 = norm_w[1:2]
    wf = final_norm_w.reshape(1, D)

    dx, reduced, total = local_gradients(
        xs, target, w0, w1, wf, wpi, gw, gb, pool_scale, wpo, [wgi_q, wgo_q], wgk, bgk, hw_tiled, place)
    r_pool_in, r_group_w, r_pool_out, r_gla_in, r_gla_out = reduced
    r_group_w = r_group_w.reshape(GROUPS, 64, GROUP_DIM)

    loss = total[7, 0]
    g_norm = jnp.stack([total[0], total[3]])
    g_scale = total[1:2]
    g_final = total[5]
    pick = lambda full, width: lax.dynamic_slice_in_dim(full, q_chip * width, width, axis=-1)
    g_gk_b = pick(total[4:5, 0:KEY_W], 128)
    g_hnw = pick(total[6:7, 0:HEAD_V], 64)
    g_group_b = pick(total[2].reshape(GROUPS, GROUP_DIM), 64)[None]
    g_gk_w = pick(total[8:16].reshape(GATE_RANK, KEY_W), 128)[None]

    turn = lambda a: jnp.transpose(a, (2, 0, 1))
    back = lambda a: jnp.transpose(a, (1, 2, 0))
    as2d = lambda a, w: a.reshape(-1, w.shape[-1])
    big_names = ("pool_in_w", "pool_group_w", "pool_out_w", "gla_in_w", "gla_out_w")
    big_args = [(pool_in_w, r_pool_in[None], m_pool_in_w, v_pool_in_w),
                (pool_group_w, r_group_w[None], m_pool_group_w, v_pool_group_w),
                (pool_out_w, r_pool_out[None], m_pool_out_w, v_pool_out_w),
                (gla_in_w, r_gla_in[None], m_gla_in_w, v_gla_in_w),
                (gla_out_w, r_gla_out[None], m_gla_out_w, v_gla_out_w)]
    to_kernel = lambda n, a, w: turn(a) if n == "gla_in_w" else as2d(a, w)
    from_kernel = lambda n, a, w: back(a) if n == "gla_in_w" else a.reshape(w.shape)
    big_in = [tuple(to_kernel(n, a, p[0]) for a in p) for n, p in zip(big_names, big_args)]
    big_out = adamw(big_in, "adamw")
    big = {n: (from_kernel(n, i[1], p[0]),) + tuple(from_kernel(n, o, p[0]) for o in out)
           for n, p, i, out in zip(big_names, big_args, big_in, big_out)}

    small_names = ("norm_w", "pool_group_b", "pool_scale", "gla_gk_w", "gla_gk_b", "gla_head_norm_w",
                   "final_norm_w")
    small_args = [(norm_w, g_norm, m_norm_w, v_norm_w),
                  (pool_group_b, g_group_b, m_pool_group_b, v_pool_group_b),
                  (pool_scale, g_scale, m_pool_scale, v_pool_scale),
                  (gla_gk_w, g_gk_w, m_gla_gk_w, v_gla_gk_w),
                  (gla_gk_b, g_gk_b, m_gla_gk_b, v_gla_gk_b),
                  (gla_head_norm_w, g_hnw, m_gla_head_norm_w, v_gla_head_norm_w),
                  (final_norm_w, g_final, m_final_norm_w, v_final_norm_w)]
    small_out = adamw_small([tuple(as2d(a, p[0]) for a in p) for p in small_args])
    small = {n: (p[1].reshape(p[0].shape),) + tuple(o.reshape(p[0].shape) for o in out)
             for n, p, out in zip(small_names, small_args, small_out)}
    results = [
        small["norm_w"],
        big["pool_in_w"],
        big["pool_group_w"],
        small["pool_group_b"],
        small["pool_scale"],
        big["pool_out_w"],
        big["gla_in_w"],
        small["gla_gk_w"],
        small["gla_gk_b"],
        small["gla_head_norm_w"],
        big["gla_out_w"],
        small["final_norm_w"],
    ]
    grads, deltas, new_m, new_v = zip(*results)
    return (loss, dx[None], *grads, *deltas, *new_m, *new_v)
```

```python
import jax
import jax.numpy as jnp
from jax import lax
from jax.experimental import pallas as pl
from jax.experimental.pallas import tpu as pltpu

F32 = jnp.float32
BF16 = jnp.bfloat16
MESH = pl.DeviceIdType.MESH

D = 1024
POOL_WINDOWS = (2, 4, 8, 16)
GROUPS = 4
GROUP_DIM = 256
HEADS = 4
HEAD_K = 128
HEAD_V = 256
KEY_W = 512
CHUNK = 64
GATE_RANK = 16
GATE_NORM = 16.0
GLA_IN = 3088
GLA_MAIN = 3072
RANK_PAD = 128
EPS = 1e-6
HALO = 32

ADAM_LR = 0.001
ADAM_B1 = 0.9
ADAM_B2 = 0.999
ADAM_EPS = 1e-08
ADAM_WD = 0.01
ADAM_STEP = 10

N_CHIPS = 4
N_DEV = 8
GLA_IN_QUARTER = GLA_IN // N_CHIPS

VMEM_LIMIT = 56 * 1024 * 1024


def _nn(a, b):
    return lax.dot_general(a, b, (((1,), (0,)), ((), ())), preferred_element_type=F32)


def _nt(a, b):
    return lax.dot_general(a, b, (((1,), (1,)), ((), ())), preferred_element_type=F32)


def _tn(a, b):
    return lax.dot_general(a, b, (((0,), (0,)), ((), ())), preferred_element_type=F32)


def _nn_exact(a, b):
    return lax.dot_general(a, b, (((1,), (0,)), ((), ())), preferred_element_type=F32,
                           precision=lax.Precision.HIGHEST)


def _bf(a):
    return a.astype(BF16)


def _params(*sem):
    return pltpu.CompilerParams(dimension_semantics=sem, vmem_limit_bytes=VMEM_LIMIT)


def _full(shape):
    return pl.BlockSpec(shape, lambda i: (0,) * len(shape))


def _position():
    return lax.axis_index("x"), lax.axis_index("y"), lax.axis_index("c")


def _gather_small(in_ref, all_ref, send_sems, recv_sems, local_sem):
    x, y, c = _position()
    me = 4 * x + 2 * y + c
    mine = pltpu.make_async_copy(in_ref, all_ref.at[me], local_sem)
    mine.start()
    sends = []
    for k in range(N_DEV - 1):
        fx, fy, fc = (k + 1) >> 2 & 1, (k + 1) >> 1 & 1, (k + 1) & 1
        cp = pltpu.make_async_remote_copy(
            src_ref=in_ref, dst_ref=all_ref.at[me],
            send_sem=send_sems.at[k], recv_sem=recv_sems.at[k],
            device_id=(x ^ fx, y ^ fy, c ^ fc), device_id_type=MESH)
        cp.start()
        sends.append(cp)
    def wait():
        for k in range(N_DEV - 1):
            fx, fy, fc = (k + 1) >> 2 & 1, (k + 1) >> 1 & 1, (k + 1) & 1
            src_dev = 4 * (x ^ fx) + 2 * (y ^ fy) + (c ^ fc)
            pltpu.make_async_remote_copy(
                src_ref=in_ref, dst_ref=all_ref.at[src_dev],
                send_sem=send_sems.at[k], recv_sem=recv_sems.at[k],
                device_id=(x, y, c), device_id_type=MESH).wait_recv()
        for cp in sends:
            cp.wait_send()
        mine.wait()

    return wait


SMALL_SEMS = [pltpu.SemaphoreType.DMA((N_DEV - 1,)), pltpu.SemaphoreType.DMA((N_DEV - 1,)),
              pltpu.SemaphoreType.DMA]
VMEM_SPEC = pl.BlockSpec(memory_space=pltpu.VMEM)


def _other_chips(x, y):
    return [(1 - x, y), (x, 1 - y), (1 - x, 1 - y)]


def _any_specs(n):
    return [pl.BlockSpec(memory_space=pl.ANY)] * n


def _halves(rows, c):
    half = rows // 2
    return pl.ds(c * half, half), pl.ds((1 - c) * half, half)


CAST_ROWS = 256


def _gather_copy(out_ref, send_sems, recv_sems, k, quarter, half, to, src=None):
    dst = out_ref.at[quarter, half]
    return pltpu.make_async_remote_copy(
        src_ref=dst if src is None else src, dst_ref=dst,
        send_sem=send_sems.at[k], recv_sem=recv_sems.at[k], device_id=to, device_id_type=MESH)


SMALL_IN_ROWS = 24


def allgather_weights(quarters, exchange, smalls):
    n = len(quarters)
    shapes = [w.shape for w in quarters]
    moved = [i for i in range(n) if exchange[i]]

    def body(*refs):
        w_refs, (gkb_ref, hnw_ref, gb_ref, gkw_ref) = refs[:n], refs[n:n + 4]
        out_refs, small_all_ref = refs[n + 4:2 * n + 4], refs[2 * n + 4]
        refs = refs[2 * n + 5:]
        f32_bufs, bf_bufs = refs[:n], refs[n:2 * n]
        send_sems, recv_sems, local_sems, small_ref = refs[2 * n:2 * n + 4]
        small_ref[...] = jnp.zeros_like(small_ref)
        small_ref[0:1, :] = gkb_ref[...]
        small_ref[1:2, 0:64] = hnw_ref[...]
        small_ref[2:2 + GROUPS, 0:64] = gb_ref[...]
        small_ref[8:8 + GATE_RANK, :] = gkw_ref[...]
        wait_small = _gather_small(small_ref, small_all_ref, *refs[2 * n + 4:])
        x, y, c = _position()
        q = 2 * x + y
        sibling = (x, y, 1 - c)
        chips = _other_chips(x, y)

        def copy(k, i, quarter, half, to, src=None):
            return _gather_copy(out_refs[i], send_sems, recv_sems, k * n + i, quarter, half, to, src)

        loads = [pltpu.make_async_copy(w_refs[i], f32_bufs[i], local_sems.at[i]) for i in range(n)]
        for cp in loads:
            cp.start()
        keeps, sends = [], []
        for i in range(n):
            loads[i].wait()
            for r0 in range(0, shapes[i][0], CAST_ROWS):
                bf_bufs[i][r0:r0 + CAST_ROWS, :] = _bf(f32_bufs[i][r0:r0 + CAST_ROWS, :])
            keep = pltpu.make_async_copy(bf_bufs[i], out_refs[i].at[q], local_sems.at[n + i])
            keep.start()
            keeps.append(keep)
            if not exchange[i]:
                continue
            mine, _ = _halves(shapes[i][0], c)
            for j, chip in enumerate(chips):
                cp = copy(j, i, q, mine, (*chip, c), src=bf_bufs[i].at[mine])
                cp.start()
                sends.append(cp)
        for j, chip in enumerate(chips):
            qj = 2 * chip[0] + chip[1]
            for i in moved:
                mine, _ = _halves(shapes[i][0], c)
                copy(j, i, qj, mine, (x, y, c)).wait_recv()
                cp = copy(3 + j, i, qj, mine, sibling)
                cp.start()
                sends.append(cp)
        for j, chip in enumerate(chips):
            qj = 2 * chip[0] + chip[1]
            for i in moved:
                _, other = _halves(shapes[i][0], c)
                copy(3 + j, i, qj, other, (x, y, c)).wait_recv()
        wait_small()
        for cp in sends:
            cp.wait_send()
        for cp in keeps:
            cp.wait()

    outs = pl.pallas_call(
        body, name="allgather_weights",
        out_shape=[jax.ShapeDtypeStruct((N_CHIPS, *s), BF16) for s in shapes]
                  + [jax.ShapeDtypeStruct((N_DEV, SMALL_IN_ROWS, 128), F32)],
        in_specs=_any_specs(n) + [VMEM_SPEC] * 4, out_specs=_any_specs(n) + [VMEM_SPEC],
        scratch_shapes=([pltpu.VMEM(s, F32) for s in shapes] + [pltpu.VMEM(s, BF16) for s in shapes]
                        + [pltpu.SemaphoreType.DMA((6 * n,)), pltpu.SemaphoreType.DMA((6 * n,)),
                           pltpu.SemaphoreType.DMA((2 * n,)), pltpu.VMEM((SMALL_IN_ROWS, 128), F32)] + SMALL_SEMS),
        compiler_params=pltpu.CompilerParams(vmem_limit_bytes=VMEM_LIMIT),
    )(*quarters, *smalls)
    return outs[:n], outs[n]


def exchange_with_sibling(grads, name):
    n = len(grads)

    def body(*refs):
        g_refs, theirs_refs = refs[:n], refs[n:2 * n]
        send_sems, recv_sems = refs[2 * n:]
        x, y, c = _position()
        copies = []
        for i in range(n):
            _, other = _halves(g_refs[i].shape[1], c)
            cp = pltpu.make_async_remote_copy(
                src_ref=g_refs[i].at[:, other], dst_ref=theirs_refs[i],
                send_sem=send_sems.at[i], recv_sem=recv_sems.at[i],
                device_id=(x, y, 1 - c), device_id_type=MESH)
            cp.start()
            copies.append(cp)
        for cp in copies:
            cp.wait()

    return pl.pallas_call(
        body, name=name,
        out_shape=[jax.ShapeDtypeStruct((N_CHIPS, g.shape[1] // 2, g.shape[2]), F32) for g in grads],
        in_specs=_any_specs(n), out_specs=_any_specs(n),
        scratch_shapes=[pltpu.SemaphoreType.DMA((n,)), pltpu.SemaphoreType.DMA((n,))],
    )(*grads)


def _scatter_copies(b_refs, got_refs, send_sems, recv_sems):
    n = len(b_refs)
    x, y, c = _position()
    copies = []
    for j, chip in enumerate(_other_chips(x, y)):
        qj = 2 * chip[0] + chip[1]
        for i in range(n):
            copies.append(pltpu.make_async_remote_copy(
                src_ref=b_refs[i].at[qj], dst_ref=got_refs[i].at[j],
                send_sem=send_sems.at[j * n + i], recv_sem=recv_sems.at[j * n + i],
                device_id=(*chip, c), device_id_type=MESH))
    return copies


def _scatter_shapes(chip_sums):
    return [jax.ShapeDtypeStruct((N_CHIPS - 1, *b.shape[1:]), BF16) for b in chip_sums]


def scatter_to_owners(chip_sums, name):
    n = len(chip_sums)

    def body(*refs):
        copies = _scatter_copies(refs[:n], refs[n:2 * n], *refs[2 * n:])
        for cp in copies:
            cp.start()
        for cp in copies:
            cp.wait()

    return pl.pallas_call(
        body, name=name,
        out_shape=_scatter_shapes(chip_sums),
        in_specs=_any_specs(n), out_specs=_any_specs(n),
        scratch_shapes=[pltpu.SemaphoreType.DMA((3 * n,)), pltpu.SemaphoreType.DMA((3 * n,))],
    )(*chip_sums)


SMALL_SUM_ROWS = 16


def join_halves(reduced, small_pool, small_gla, small_top, g_gk_pad):
    n = len(reduced)

    def body(*refs):
        pool_ref, gla_ref, top_ref, gk_ref = refs[n:n + 4]
        buf_refs, total_ref = refs[n + 4:2 * n + 4], refs[2 * n + 4]
        send_sems, recv_sems, all_ref, small_ref = refs[2 * n + 5:2 * n + 9]
        small_ref[0:3, :] = pool_ref[0:3, :]
        small_ref[3:5, :] = gla_ref[0:2, :]
        small_ref[5:8, :] = top_ref[0:3, :]
        for r in range(GATE_RANK):
            small_ref[8 + r // 2:9 + r // 2, (r % 2) * KEY_W:(r % 2 + 1) * KEY_W] = gk_ref[r:r + 1, :]
        x, y, c = _position()
        copies = []
        for i in range(n):
            mine, _ = _halves(buf_refs[i].shape[0], c)
            cp = pltpu.make_async_remote_copy(
                src_ref=buf_refs[i].at[mine], dst_ref=buf_refs[i].at[mine],
                send_sem=send_sems.at[i], recv_sem=recv_sems.at[i],
                device_id=(x, y, 1 - c), device_id_type=MESH)
            cp.start()
            copies.append(cp)
        _gather_small(small_ref, all_ref, *refs[2 * n + 9:])()
        total = all_ref[0]
        for dev in range(1, N_DEV):
            total = total + all_ref[dev]
        total_ref[...] = total
        for cp in copies:
            cp.wait()

    outs = pl.pallas_call(
        body, name="join_halves",
        out_shape=[jax.ShapeDtypeStruct(r.shape, F32) for r in reduced]
                  + [jax.ShapeDtypeStruct((SMALL_SUM_ROWS, D), F32)],
        in_specs=_any_specs(n) + [VMEM_SPEC] * 4, out_specs=_any_specs(n) + [VMEM_SPEC],
        input_output_aliases={i: i for i in range(n)},
        scratch_shapes=[pltpu.SemaphoreType.DMA((n,)), pltpu.SemaphoreType.DMA((n,)),
                        pltpu.VMEM((N_DEV, SMALL_SUM_ROWS, D), F32), pltpu.VMEM((SMALL_SUM_ROWS, D), F32)]
                       + SMALL_SEMS,
    )(*reduced, small_pool, small_gla, small_top, g_gk_pad)
    return outs[:n], outs[n]


ADD_ROWS = 512
ADD_HALVES_ROWS = 256


def _spans(counts):
    starts, total = [], 0
    for count in counts:
        starts.append(total)
        total += count
    return starts, total


def _local_step(t, start, count):
    return jnp.clip(t - start, 0, count - 1)


def add_halves(grads, theirs, place, name):
    n = len(grads)
    shapes = [t.shape for t in theirs]
    rbs = [min(ADD_HALVES_ROWS, sh[1]) for sh in shapes]
    counts = [sh[1] // rb for sh, rb in zip(shapes, rbs)]
    starts, total = _spans(counts)

    def body(place_ref, *refs):
        a_refs, b_refs = refs[:n], refs[n:2 * n]
        f_refs, h_refs = refs[2 * n:3 * n], refs[3 * n:4 * n]
        t = pl.program_id(0)
        q = place_ref[1]
        for i in range(n):
            @pl.when((t >= starts[i]) & (t < starts[i] + counts[i]))
            def _(i=i):
                h_refs[i][...] = _bf(a_refs[i][...] + b_refs[i][...])
                f_refs[i][...] = a_refs[i][q] + b_refs[i][q]

    def specs(i):
        block = (N_CHIPS, rbs[i], shapes[i][2])
        step = lambda t: _local_step(t, starts[i], counts[i])
        mine = pl.BlockSpec(block, lambda t, place: (0, place[0] * counts[i] + step(t), 0))
        same = pl.BlockSpec(block, lambda t, place: (0, step(t), 0))
        own = pl.BlockSpec(block[1:], lambda t, place: (step(t), 0))
        return mine, same, own

    all_specs = [specs(i) for i in range(n)]
    outs = pl.pallas_call(
        body, name=name,
        grid_spec=pltpu.PrefetchScalarGridSpec(
            num_scalar_prefetch=1, grid=(total,),
            in_specs=[sp[0] for sp in all_specs] + [sp[1] for sp in all_specs],
            out_specs=[sp[2] for sp in all_specs] + [sp[1] for sp in all_specs]),
        out_shape=[jax.ShapeDtypeStruct(sh[1:], F32) for sh in shapes]
                  + [jax.ShapeDtypeStruct(sh, BF16) for sh in shapes],
        compiler_params=_params("arbitrary"),
    )(place, *grads, *theirs)
    return list(zip(outs[:n], outs[n:]))


def add_parts(owns, gots, place, name):
    n = len(owns)
    shapes = [g.shape for g in gots]
    rbs = [min(ADD_ROWS, sh[1]) for sh in shapes]
    counts = [sh[1] // rb for sh, rb in zip(shapes, rbs)]
    starts, total = _spans(counts)

    def body(place_ref, *refs):
        o_refs, g_refs, out_refs = refs[:n], refs[n:2 * n], refs[2 * n:]
        t = pl.program_id(0)
        for i in range(n):
            @pl.when((t >= starts[i]) & (t < starts[i] + counts[i]))
            def _(i=i):
                total_i = o_refs[i][...]
                for j in range(N_CHIPS - 1):
                    total_i = total_i + g_refs[i][j].astype(F32)
                out_refs[i][...] = total_i

    def specs(i):
        rb, cols = rbs[i], shapes[i][2]
        step = lambda t: _local_step(t, starts[i], counts[i])
        return (pl.BlockSpec((rb, cols), lambda t, place: (step(t), 0)),
                pl.BlockSpec((N_CHIPS - 1, rb, cols), lambda t, place: (0, step(t), 0)),
                pl.BlockSpec((rb, cols), lambda t, place: (place[0] * counts[i] + step(t), 0)))

    all_specs = [specs(i) for i in range(n)]
    return pl.pallas_call(
        body, name=name,
        grid_spec=pltpu.PrefetchScalarGridSpec(
            num_scalar_prefetch=1, grid=(total,),
            in_specs=[sp[0] for sp in all_specs] + [sp[1] for sp in all_specs],
            out_specs=[sp[2] for sp in all_specs]),
        out_shape=[jax.ShapeDtypeStruct((2 * sh[1], sh[2]), F32) for sh in shapes],
        compiler_params=_params("arbitrary"),
    )(place, *owns, *gots)


def _adam_math(w, g, m, v):
    m = ADAM_B1 * m + (1.0 - ADAM_B1) * g
    v = ADAM_B2 * v + (1.0 - ADAM_B2) * (g * g)
    m_hat = m / (1.0 - ADAM_B1 ** ADAM_STEP)
    v_hat = v / (1.0 - ADAM_B2 ** ADAM_STEP)
    delta = -ADAM_LR * (m_hat / (jnp.sqrt(v_hat) + ADAM_EPS) + ADAM_WD * w)
    return delta, m, v


ADAM_BLOCK_BYTES = 2 ** 19
ADAM_MOST_STEPS = 8


def adamw(params, name):
    n = len(params)
    shapes = [p[0].shape for p in params]

    def tile_rows(shape):
        rows, cols = shape[0], shape[-1]
        aligned = 1 if len(shape) == 3 else 8
        divisors = [t for t in range(aligned, rows + 1, aligned) if rows % t == 0]
        tile = max(t for t in divisors if t * cols * 4 <= ADAM_BLOCK_BYTES)
        if rows // tile > ADAM_MOST_STEPS:
            tile = min(t for t in divisors if rows // t <= ADAM_MOST_STEPS)
        return tile

    tiles = [tile_rows(sh) for sh in shapes]
    counts = [sh[0] // tl for sh, tl in zip(shapes, tiles)]
    starts, total = _spans(counts)

    def body(*refs):
        ins, outs = refs[:4 * n], refs[4 * n:]
        t = pl.program_id(0)
        for i in range(n):
            @pl.when((t >= starts[i]) & (t < starts[i] + counts[i]))
            def _(i=i):
                w_ref, g_ref, m_ref, v_ref = ins[4 * i:4 * i + 4]
                d, nm, nv = _adam_math(w_ref[...], g_ref[...], m_ref[...], v_ref[...])
                outs[3 * i][...] = d
                outs[3 * i + 1][...] = nm
                outs[3 * i + 2][...] = nv

    def spec(i):
        block = (tiles[i],) + shapes[i][1:]
        zeros = (0,) * (len(block) - 1)
        return pl.BlockSpec(block, lambda t: (_local_step(t, starts[i], counts[i]),) + zeros)

    outs = pl.pallas_call(
        body, name=name, grid=(total,),
        out_shape=[jax.ShapeDtypeStruct(sh, F32) for sh in shapes for _ in range(3)],
        in_specs=[spec(i) for i in range(n) for _ in range(4)],
        out_specs=[spec(i) for i in range(n) for _ in range(3)],
        compiler_params=_params("arbitrary"),
    )(*[a for p in params for a in p])
    return [tuple(outs[3 * i:3 * i + 3]) for i in range(n)]


def adamw_small(params):
    n = len(params)

    def body(*refs):
        ins, outs = refs[:4 * n], refs[4 * n:]
        for k in range(n):
            w_ref, g_ref, m_ref, v_ref = ins[4 * k:4 * k + 4]
            d, nm, nv = _adam_math(w_ref[...], g_ref[...], m_ref[...], v_ref[...])
            outs[3 * k][...] = d
            outs[3 * k + 1][...] = nm
            outs[3 * k + 2][...] = nv

    flat = [a for p in params for a in p]
    outs = pl.pallas_call(
        body, name="adamw_small",
        out_shape=[jax.ShapeDtypeStruct(p[0].shape, F32) for p in params for _ in range(3)],
        in_specs=[VMEM_SPEC] * (4 * n), out_specs=[VMEM_SPEC] * (3 * n),
    )(*flat)
    return [tuple(outs[3 * k:3 * k + 3]) for k in range(n)]


def matmul_tn(a, b, name, tile_n=512, by_column_tile=False, narrow=None):
    s, m = a.shape
    n = b.shape[1]
    tile_n = min(tile_n, n)
    if by_column_tile:
        out_shape = jax.ShapeDtypeStruct((n // tile_n, m, tile_n), F32)
        out_spec = pl.BlockSpec((None, m, tile_n), lambda j: (j, 0, 0))
    else:
        out_shape = jax.ShapeDtypeStruct((m, n), F32)
        out_spec = pl.BlockSpec((m, tile_n), lambda j: (0, j))

    if narrow is None:
        def body(a_ref, b_ref, out_ref):
            out_ref[...] = _tn(a_ref[...], b_ref[...])

        return pl.pallas_call(
            body, name=name, grid=(n // tile_n,),
            out_shape=out_shape,
            in_specs=[_full((s, m)), pl.BlockSpec((s, tile_n), lambda j: (0, j))],
            out_specs=out_spec,
            compiler_params=_params("parallel"),
        )(a, b)

    def body_with_narrow(a_ref, b_ref, c_ref, out_ref, out_c_ref):
        out_ref[...] = _tn(a_ref[...], b_ref[...])

        @pl.when(pl.program_id(0) == 0)
        def _():
            out_c_ref[...] = _tn(a_ref[...], c_ref[...])

    return pl.pallas_call(
        body_with_narrow, name=name, grid=(n // tile_n,),
        out_shape=(out_shape, jax.ShapeDtypeStruct((m, narrow.shape[1]), F32)),
        in_specs=[_full((s, m)), pl.BlockSpec((s, tile_n), lambda j: (0, j)), _full(narrow.shape)],
        out_specs=(out_spec, _full((m, narrow.shape[1]))),
        compiler_params=_params("arbitrary"),
    )(a, b, narrow)


ROW_TILE = 512


def _row_index(tile, rows):
    return tile * rows + lax.broadcasted_iota(jnp.int32, (rows, 1), 0)


def _inverse_counts(t_glob):
    return [1.0 / jnp.minimum(t_glob + 1, w).astype(F32) for w in POOL_WINDOWS]


def _sigmoid(z):
    return 1.0 / (1.0 + jnp.exp(-z))


def _trailing_sums(src, tmp, cols, window, rows):
    bufs = (src, tmp)
    span, level, start = 1, 0, 0
    while span < window:
        start += 8
        a, b = bufs[level % 2], bufs[(level + 1) % 2]
        n = HALO + rows - start
        b[start:start + n, cols] = a[start:start + n, cols] + a[start - span:start - span + n, cols]
        span, level = 2 * span, level + 1
    return bufs[level % 2][HALO:HALO + rows, cols]


def _leading_sums(src, tmp, cols, window, rows):
    bufs = (src, tmp)
    span, level, n = 1, 0, rows + HALO
    while span < window:
        n -= 8
        a, b = bufs[level % 2], bufs[(level + 1) % 2]
        b[0:n, cols] = a[0:n, cols] + a[span:span + n, cols]
        span, level = 2 * span, level + 1
    return bufs[level % 2][0:rows, cols]


def gather_in_background(step, last, out_refs, send_sems, recv_sems, finish):
    n = len(out_refs)
    x, y, c = _position()
    q = 2 * x + y
    chips = _other_chips(x, y)

    def copy(k, i, quarter, half, to):
        return _gather_copy(out_refs[i], send_sems, recv_sems, k * n + i, quarter, half, to)

    if not finish:
        @pl.when(step == 0)
        def _():
            for i in range(n):
                mine, _ = _halves(out_refs[i].shape[1], c)
                for j, chip in enumerate(chips):
                    copy(j, i, q, mine, (*chip, c)).start()

        @pl.when(step == last)
        def _():
            for j, chip in enumerate(chips):
                qj = 2 * chip[0] + chip[1]
                for i in range(n):
                    mine, _ = _halves(out_refs[i].shape[1], c)
                    copy(j, i, qj, mine, (x, y, c)).wait_recv()
                    copy(3 + j, i, qj, mine, (x, y, 1 - c)).start()
        return

    @pl.when(step == last)
    def _():
        for j, chip in enumerate(chips):
            qj = 2 * chip[0] + chip[1]
            for i in range(n):
                mine, other = _halves(out_refs[i].shape[1], c)
                copy(3 + j, i, qj, other, (x, y, c)).wait_recv()
                copy(j, i, q, mine, (x, y, c)).wait_send()
                copy(3 + j, i, qj, mine, (x, y, c)).wait_send()


def pool_forward(x, w0, wpi, gw, gb, scale, wpo, later):
    s = x.shape[0]
    ts = ROW_TILE
    nt = s // ts
    assert nt >= 2
    n_later = len(later)

    def body(x_ref, w0_ref, wpi_ref, gw_ref, gb_ref, sc_ref, wpo_ref, *rest):
        rest = rest[n_later:]
        h1_ref, pooled_ref, gt_ref, n0_ref = rest[:4]
        later_refs = rest[4:4 + n_later]
        ubuf, tbuf, hist, send_sems, recv_sems = rest[4 + n_later:]
        i = pl.program_id(0)
        gather_in_background(i, nt - 1, later_refs, send_sems, recv_sems, finish=False)
        xv = x_ref[...]
        r = lax.rsqrt(jnp.mean(xv * xv, axis=-1, keepdims=True) + EPS)
        n0 = _bf(xv * r * w0_ref[...])
        n0_ref[...] = n0
        u = jnp.concatenate([_nn(n0, wpi_ref[0]), _nn(n0, wpi_ref[1])], axis=-1)
        gt = jnp.concatenate([_nn(n0, wpi_ref[2]), _nn(n0, wpi_ref[3])], axis=-1)
        gt_ref[...] = gt

        @pl.when(i == 0)
        def _():
            hist[...] = jnp.zeros_like(hist)

        ubuf[0:HALO, :] = hist[...]
        ubuf[HALO:HALO + ts, :] = u
        hist[...] = u[ts - HALO:, :]
        inv = _inverse_counts(_row_index(i, ts))
        mixed = []
        for g, w in enumerate(POOL_WINDOWS):
            cols = slice(g * GROUP_DIM, (g + 1) * GROUP_DIM)
            pooled = _bf(_trailing_sums(ubuf, tbuf, cols, w, ts) * inv[g] - u[:, cols])
            pooled_ref[:, cols] = pooled
            mixed.append(_nn(pooled, gw_ref[g]))
        mixed = jnp.concatenate(mixed, axis=-1) + gb_ref[...]
        y = mixed * sc_ref[...] * (gt * _sigmoid(gt))
        h1_ref[...] = xv + _nn(_bf(y), wpo_ref[...])
        gather_in_background(i, nt - 1, later_refs, send_sems, recv_sems, finish=True)

    row = lambda cols: pl.BlockSpec((ts, cols), lambda i: (i, 0))
    outs = pl.pallas_call(
        body, name="pool_forward", grid=(nt,),
        out_shape=[jax.ShapeDtypeStruct((s, D), F32), jax.ShapeDtypeStruct((s, D), BF16),
                   jax.ShapeDtypeStruct((s, D), F32), jax.ShapeDtypeStruct((s, D), BF16)]
                  + [jax.ShapeDtypeStruct(a.shape, a.dtype) for a in later],
        in_specs=[row(D), _full((1, D)), _full((N_CHIPS, D, D // 2)), _full((GROUPS, GROUP_DIM, GROUP_DIM)),
                  _full((1, D)), _full((1, D)), _full((D, D))] + _any_specs(n_later),
        out_specs=[row(D), row(D), row(D), row(D)] + _any_specs(n_later),
        input_output_aliases={7 + k: 4 + k for k in range(n_later)},
        scratch_shapes=[pltpu.VMEM((HALO + ts, D), F32), pltpu.VMEM((HALO + ts, D), F32),
                        pltpu.VMEM((HALO, D), F32),
                        pltpu.SemaphoreType.DMA((6 * n_later,)), pltpu.SemaphoreType.DMA((6 * n_later,))],
        compiler_params=_params("arbitrary"),
    )(x, w0, wpi, gw, gb, scale, wpo, *later)
    return outs[:4], outs[4:]


def pool_backward(x, dh1, pooled, gt, w0, wpi, gw, gb, scale, wpo, chip_sums):
    s = x.shape[0]
    ts = ROW_TILE
    nt = s // ts
    n_sums = len(chip_sums)

    def body(x_ref, dh1_ref, pooled_ref, gt_ref, w0_ref, wpi_ref, gw_ref, gb_ref, sc_ref, wpo_ref, *rest):
        sum_refs, rest = rest[:n_sums], rest[n_sums:]
        dx_ref, dproj_ref, gpo_ref, ggw_ref, small_ref = rest[:5]
        got_refs = rest[5:5 + n_sums]
        ebuf, tbuf, ahead, send_sems, recv_sems = rest[5 + n_sums:]
        i = pl.program_id(0)
        copies = _scatter_copies(sum_refs, got_refs, send_sems, recv_sems)

        @pl.when(i == 0)
        def _():
            for cp in copies:
                cp.start()

        @pl.when(i == 0)
        def _():
            gpo_ref[...] = jnp.zeros_like(gpo_ref)
            ggw_ref[...] = jnp.zeros_like(ggw_ref)
            small_ref[...] = jnp.zeros_like(small_ref)
            ahead[...] = jnp.zeros_like(ahead)

        dh1 = dh1_ref[...]
        dh1_bf = _bf(dh1)
        gt = gt_ref[...]
        sc = sc_ref[...]
        dy = _nt(dh1_bf, wpo_ref[...])
        pooled_bf = []
        mixed = []
        for g in range(GROUPS):
            cols = slice(g * GROUP_DIM, (g + 1) * GROUP_DIM)
            pb = pooled_ref[:, cols]
            pooled_bf.append(pb)
            mixed.append(_nn(pb, gw_ref[g]))
        mixed = jnp.concatenate(mixed, axis=-1) + gb_ref[...]
        sg = _sigmoid(gt)
        silu = gt * sg
        gpo_ref[...] += _tn(_bf(mixed * sc * silu), dh1_bf)
        dmixed = dy * sc * silu
        dgt = dy * mixed * sc * (sg * (1.0 + gt * (1.0 - sg)))
        dproj_ref[:, D:] = _bf(dgt)
        small_ref[1:2, :] += jnp.sum(dy * mixed * silu, axis=0, keepdims=True)
        small_ref[2:3, :] += jnp.sum(dmixed, axis=0, keepdims=True)

        inv = _inverse_counts(_row_index(nt - 1 - i, ts))
        ebuf[ts:ts + HALO, :] = ahead[...]
        dpooled = []
        for g in range(GROUPS):
            cols = slice(g * GROUP_DIM, (g + 1) * GROUP_DIM)
            dm = _bf(dmixed[:, cols])
            ggw_ref[g] += _tn(pooled_bf[g], dm)
            dp = _nt(dm, gw_ref[g])
            dpooled.append(dp)
            ebuf[0:ts, cols] = dp * inv[g]
        ahead[...] = ebuf[0:HALO, :]
        du = []
        for g, w in enumerate(POOL_WINDOWS):
            cols = slice(g * GROUP_DIM, (g + 1) * GROUP_DIM)
            du.append(_leading_sums(ebuf, tbuf, cols, w, ts) - dpooled[g])
        du = _bf(jnp.concatenate(du, axis=-1))
        dproj_ref[:, :D] = du
        dgt_bf = _bf(dgt)
        half = D // 2
        dn0 = (_nt(du[:, :half], wpi_ref[0]) + _nt(du[:, half:], wpi_ref[1])
               + _nt(dgt_bf[:, :half], wpi_ref[2]) + _nt(dgt_bf[:, half:], wpi_ref[3]))

        xv = x_ref[...]
        r = lax.rsqrt(jnp.mean(xv * xv, axis=-1, keepdims=True) + EPS)
        xhat = xv * r
        small_ref[0:1, :] += jnp.sum(dn0 * xhat, axis=0, keepdims=True)
        dxh = dn0 * w0_ref[...]
        dx_ref[...] = dh1 + r * (dxh - xhat * jnp.mean(dxh * xhat, axis=-1, keepdims=True))

        @pl.when(i == nt - 1)
        def _():
            for cp in copies:
                cp.wait()

    row = lambda cols: pl.BlockSpec((ts, cols), lambda i: (nt - 1 - i, 0))
    outs = pl.pallas_call(
        body, name="pool_backward", grid=(nt,),
        out_shape=[jax.ShapeDtypeStruct((s, D), F32), jax.ShapeDtypeStruct((s, 2 * D), BF16),
                   jax.ShapeDtypeStruct((D, D), F32),
                   jax.ShapeDtypeStruct((GROUPS, GROUP_DIM, GROUP_DIM), F32),
                   jax.ShapeDtypeStruct((8, D), F32)] + _scatter_shapes(chip_sums),
        in_specs=[row(D), row(D), row(D), row(D), _full((1, D)), _full((N_CHIPS, D, D // 2)),
                  _full((GROUPS, GROUP_DIM, GROUP_DIM)), _full((1, D)), _full((1, D)), _full((D, D))]
                 + _any_specs(n_sums),
        out_specs=[row(D), row(2 * D), _full((D, D)), _full((GROUPS, GROUP_DIM, GROUP_DIM)), _full((8, D))]
                  + _any_specs(n_sums),
        scratch_shapes=[pltpu.VMEM((ts + HALO, D), F32), pltpu.VMEM((ts + HALO, D), F32),
                        pltpu.VMEM((HALO, D), F32),
                        pltpu.SemaphoreType.DMA((3 * n_sums,)), pltpu.SemaphoreType.DMA((3 * n_sums,))],
        compiler_params=_params("arbitrary"),
    )(x, dh1, pooled, gt, w0, wpi, gw, gb, scale, wpo, *chip_sums)
    return outs[:5], outs[5:]


def gla_project(h1, w1, wgi, wlow, wgk, bgk, later):
    s = h1.shape[0]
    ts = ROW_TILE
    nt = s // ts
    assert nt >= 2
    n_later = len(later)

    def body(h_ref, w1_ref, wgi_ref, wlow_ref, wgk_ref, bgk_ref, *rest):
        rest = rest[n_later:]
        qk_ref, v_ref, gate_ref, low_ref, cum_ref, n1_ref = rest[:6]
        later_refs = rest[6:6 + n_later]
        send_sems, recv_sems = rest[6 + n_later:]
        gather_in_background(pl.program_id(0), nt - 1, later_refs, send_sems, recv_sems, finish=False)
        hv = h_ref[...]
        r = lax.rsqrt(jnp.mean(hv * hv, axis=-1, keepdims=True) + EPS)
        n1 = _bf(hv * r * w1_ref[...])
        n1_ref[...] = n1
        qk_ref[...] = _nn(n1, wgi_ref[:, 0:2 * KEY_W])
        v_ref[...] = _bf(_nn(n1, wgi_ref[:, 2 * KEY_W:2 * KEY_W + D]))
        gate_ref[...] = _nn(n1, wgi_ref[:, 2 * KEY_W + D:GLA_MAIN])
        low = _bf(_nn(n1, wlow_ref[...]))
        low_ref[...] = low
        z = _nn(low, wgk_ref[...]) + bgk_ref[...]
        lg = (jnp.minimum(z, 0.0) - jnp.log(1.0 + jnp.exp(-jnp.abs(z)))) / GATE_NORM
        lower_f = _chunk_masks()[0].astype(F32)
        for r0 in range(0, ts, CHUNK):
            cum_ref[r0:r0 + CHUNK, :] = _nn_exact(lower_f, lg[r0:r0 + CHUNK, :])
        gather_in_background(pl.program_id(0), nt - 1, later_refs, send_sems, recv_sems, finish=True)

    row = lambda cols: pl.BlockSpec((ts, cols), lambda i: (i, 0))
    outs = pl.pallas_call(
        body, name="gla_project", grid=(nt,),
        out_shape=[jax.ShapeDtypeStruct((s, D), F32), jax.ShapeDtypeStruct((s, D), BF16),
                   jax.ShapeDtypeStruct((s, D), F32), jax.ShapeDtypeStruct((s, RANK_PAD), BF16),
                   jax.ShapeDtypeStruct((s, KEY_W), F32), jax.ShapeDtypeStruct((s, D), BF16)]
                  + [jax.ShapeDtypeStruct(a.shape, a.dtype) for a in later],
        in_specs=[row(D), _full((1, D)), _full((D, GLA_MAIN)), _full((D, RANK_PAD)),
                  _full((RANK_PAD, KEY_W)), _full((1, KEY_W))] + _any_specs(n_later),
        out_specs=[row(D), row(D), row(D), row(RANK_PAD), row(KEY_W), row(D)] + _any_specs(n_later),
        input_output_aliases={6 + k: 6 + k for k in range(n_later)},
        scratch_shapes=[pltpu.SemaphoreType.DMA((6 * n_later,)), pltpu.SemaphoreType.DMA((6 * n_later,))],
        compiler_params=_params("arbitrary"),
    )(h1, w1, wgi, wlow, wgk, bgk, *later)
    return outs[:6], outs[6:]


GLA_BLOCK = 512
CHUNKS_PER_BLOCK = GLA_BLOCK // CHUNK


def _chunk_masks():
    t = lax.broadcasted_iota(jnp.int32, (CHUNK, CHUNK), 0)
    u = lax.broadcasted_iota(jnp.int32, (CHUNK, CHUNK), 1)
    return t >= u, t <= u


def _gla_chunk_terms(q, cum):
    ep = jnp.exp(cum)
    en = jnp.exp(-cum)
    qs = q * (HEAD_K ** -0.5)
    last = cum[CHUNK - 1:CHUNK, :]
    ed = jnp.exp(last - cum)
    dec = jnp.exp(last)
    return ep, en, qs, ed, dec


def gla_forward(qk, v, cum):
    s = qk.shape[0]
    nb = s // GLA_BLOCK
    nc = s // CHUNK

    def body(q_ref, k_ref, v_ref, cum_ref, o_ref, st_ref, sc_ref, state):
        @pl.when(pl.program_id(0) == 0)
        def _():
            state[...] = jnp.zeros_like(state)

        lower, _ = _chunk_masks()

        def chunk(cc, carry):
            rows = pl.ds(pl.multiple_of(cc * CHUNK, CHUNK), CHUNK)
            for h in range(HEADS):
                kc = slice(h * HEAD_K, (h + 1) * HEAD_K)
                vc = slice(h * HEAD_V, (h + 1) * HEAD_V)
                q = q_ref[rows, kc]
                k = k_ref[rows, kc]
                v = v_ref[rows, vc]
                ep, en, qs, ed, dec = _gla_chunk_terms(q, cum_ref[rows, kc])
                a = _bf(qs * ep)
                fwd = _nt(a, _bf(k * en))
                bwd = _nt(_bf(qs * en), _bf(k * ep))
                scores = _bf(jnp.where(lower, fwd, bwd))
                sc_ref[rows, h * CHUNK:(h + 1) * CHUNK] = scores
                st = state[h]
                st_ref[cc, h] = st
                o_ref[rows, vc] = _nn(scores, v) + _nt(a, _bf(st))
                state[h] = st * dec + _tn(v, _bf(k * ed))
            return carry

        lax.fori_loop(0, CHUNKS_PER_BLOCK, chunk, 0, unroll=4)

    return pl.pallas_call(
        body, name="gla_forward", grid=(nb,),
        out_shape=(jax.ShapeDtypeStruct((s, D), F32),
                   jax.ShapeDtypeStruct((nc, HEADS, HEAD_V, HEAD_K), F32),
                   jax.ShapeDtypeStruct((s, HEADS * CHUNK), BF16)),
        in_specs=[pl.BlockSpec((GLA_BLOCK, KEY_W), lambda i: (i, 0)),
                  pl.BlockSpec((GLA_BLOCK, KEY_W), lambda i: (i, 1)),
                  pl.BlockSpec((GLA_BLOCK, D), lambda i: (i, 0)),
                  pl.BlockSpec((GLA_BLOCK, KEY_W), lambda i: (i, 0))],
        out_specs=(pl.BlockSpec((GLA_BLOCK, D), lambda i: (i, 0)),
                   pl.BlockSpec((CHUNKS_PER_BLOCK, HEADS, HEAD_V, HEAD_K), lambda i: (i, 0, 0, 0)),
                   pl.BlockSpec((GLA_BLOCK, HEADS * CHUNK), lambda i: (i, 0))),
        scratch_shapes=[pltpu.VMEM((HEADS, HEAD_V, HEAD_K), F32)],
        compiler_params=_params("arbitrary"),
    )(qk, qk, v, cum)


def gla_backward(qk, v, cum, do, states, scores):
    s = qk.shape[0]
    nb = s // GLA_BLOCK

    def body(q_ref, k_ref, v_ref, cum_ref, do_ref, st_ref, sc_ref, dq_ref, dk_ref, dv_ref, dcum_ref, dstate):
        @pl.when(pl.program_id(0) == 0)
        def _():
            dstate[...] = jnp.zeros_like(dstate)

        lower, _ = _chunk_masks()
        is_last = lax.broadcasted_iota(jnp.int32, (CHUNK, HEAD_K), 0) == CHUNK - 1

        def chunk(step, carry):
            cc = CHUNKS_PER_BLOCK - 1 - step
            rows = pl.ds(pl.multiple_of(cc * CHUNK, CHUNK), CHUNK)
            for h in range(HEADS):
                kc = slice(h * HEAD_K, (h + 1) * HEAD_K)
                vc = slice(h * HEAD_V, (h + 1) * HEAD_V)
                q = q_ref[rows, kc]
                k = k_ref[rows, kc]
                v = v_ref[rows, vc]
                do_c = do_ref[rows, vc]
                ep, en, qs, ed, dec = _gla_chunk_terms(q, cum_ref[rows, kc])
                a = _bf(qs * ep)
                b = _bf(k * en)
                c = _bf(qs * en)
                dk_dec = _bf(k * ep)
                kd = _bf(k * ed)
                scores = sc_ref[rows, h * CHUNK:(h + 1) * CHUNK]
                st = st_ref[cc, h]
                dst = dstate[h]
                dst_bf = _bf(dst)

                dscores = _nt(do_c, v)
                dfwd = _bf(jnp.where(lower, dscores, 0.0))
                dbwd = _bf(jnp.where(lower, 0.0, dscores))
                dv_ref[rows, vc] = _bf(_tn(scores, do_c) + _nt(kd, dst_bf))
                da = _nn(dfwd, b) + _nn(do_c, _bf(st))
                db = _tn(dfwd, a)
                dc = _nn(dbwd, dk_dec)
                ddk = _tn(dbwd, c)
                dkd = _nn(v, dst_bf)
                ddec = jnp.sum(dst * st, axis=0, keepdims=True)
                dstate[h] = dst * dec + _tn(do_c, a)

                m = dkd * k * ed
                dq_ref[rows, kc] = _bf((da * ep + dc * en) * (HEAD_K ** -0.5))
                dk_ref[rows, kc] = _bf(db * en + ddk * ep + dkd * ed)
                dcum = (da * qs + ddk * k) * ep - (db * k + dc * qs) * en - m
                dlast = jnp.sum(m, axis=0, keepdims=True) + ddec * dec
                dcum_ref[rows, kc] = dcum + jnp.where(is_last, dlast, 0.0)
            return carry

        lax.fori_loop(0, CHUNKS_PER_BLOCK, chunk, 0, unroll=4)

    rev = lambda cols, col_block: pl.BlockSpec((GLA_BLOCK, cols), lambda i: (nb - 1 - i, col_block))
    return pl.pallas_call(
        body, name="gla_backward", grid=(nb,),
        out_shape=(jax.ShapeDtypeStruct((s, KEY_W), BF16), jax.ShapeDtypeStruct((s, KEY_W), BF16),
                   jax.ShapeDtypeStruct((s, D), BF16), jax.ShapeDtypeStruct((s, KEY_W), F32)),
        in_specs=[rev(KEY_W, 0), rev(KEY_W, 1), rev(D, 0), rev(KEY_W, 0), rev(D, 0),
                  pl.BlockSpec((CHUNKS_PER_BLOCK, HEADS, HEAD_V, HEAD_K), lambda i: (nb - 1 - i, 0, 0, 0)),
                  rev(HEADS * CHUNK, 0)],
        out_specs=(rev(KEY_W, 0), rev(KEY_W, 0), rev(D, 0), rev(KEY_W, 0)),
        scratch_shapes=[pltpu.VMEM((HEADS, HEAD_V, HEAD_K), F32)],
        compiler_params=_params("arbitrary"),
    )(qk, qk, v, cum, do, states, scores)


def head_and_loss(o, gate, h1, target, hw, wgo, wf):
    s = o.shape[0]
    ts = ROW_TILE

    def body(o_ref, gate_ref, h1_ref, tgt_ref, hw_ref, wgo_ref, wf_ref,
             dh2_ref, do_ref, dgate_ref, ggo_ref, small_ref):
        @pl.when(pl.program_id(0) == 0)
        def _():
            ggo_ref[...] = jnp.zeros_like(ggo_ref)
            small_ref[...] = jnp.zeros_like(small_ref)

        gate = gate_ref[...]
        hw = hw_ref[...]
        sg = _sigmoid(gate)
        silu = gate * sg
        ohat, ro = [], []
        for h in range(HEADS):
            oh = o_ref[:, h * HEAD_V:(h + 1) * HEAD_V]
            rh = lax.rsqrt(jnp.mean(oh * oh, axis=-1, keepdims=True) + EPS)
            ro.append(rh)
            ohat.append(oh * rh)
        ohat = jnp.concatenate(ohat, axis=-1)
        on = ohat * hw
        y2 = _bf(on * silu)
        h2 = h1_ref[...] + _nn(y2, wgo_ref[...])
        rf = lax.rsqrt(jnp.mean(h2 * h2, axis=-1, keepdims=True) + EPS)
        h2hat = h2 * rf
        wf = wf_ref[...]
        diff = h2hat * wf - tgt_ref[...]
        small_ref[2:3, :] += jnp.zeros((1, D), F32) + 0.5 * jnp.sum(diff * diff) / D
        dout = diff / D
        small_ref[0:1, :] += jnp.sum(dout * h2hat, axis=0, keepdims=True)
        dxh = dout * wf
        dh2 = rf * (dxh - h2hat * jnp.mean(dxh * h2hat, axis=-1, keepdims=True))
        dh2_ref[...] = dh2
        dh2_bf = _bf(dh2)
        ggo_ref[...] += _tn(y2, dh2_bf)
        dy2 = _nt(dh2_bf, wgo_ref[...])
        don = dy2 * silu
        dgate_ref[...] = _bf(dy2 * on * (sg * (1.0 + gate * (1.0 - sg))))
        ghw = jnp.sum(don * ohat, axis=0, keepdims=True)
        small_ref[1:2, 0:HEAD_V] += sum(ghw[:, h * HEAD_V:(h + 1) * HEAD_V] for h in range(HEADS))
        dohat = don * hw
        for h in range(HEADS):
            cols = slice(h * HEAD_V, (h + 1) * HEAD_V)
            oh, dh = ohat[:, cols], dohat[:, cols]
            do_ref[:, cols] = _bf(ro[h] * (dh - oh * jnp.mean(dh * oh, axis=-1, keepdims=True)))

    row = lambda cols: pl.BlockSpec((ts, cols), lambda i: (i, 0))
    act = jax.ShapeDtypeStruct((s, D), F32)
    act_bf = jax.ShapeDtypeStruct((s, D), BF16)
    return pl.pallas_call(
        body, name="head_and_loss", grid=(s // ts,),
        out_shape=(act, act_bf, act_bf, jax.ShapeDtypeStruct((D, D), F32), jax.ShapeDtypeStruct((8, D), F32)),
        in_specs=[row(D), row(D), row(D), row(D),
                  _full((1, D)), _full((D, D)), _full((1, D))],
        out_specs=(row(D), row(D), row(D), _full((D, D)), _full((8, D))),
        compiler_params=_params("arbitrary"),
    )(o, gate, h1, target, hw, wgo, wf)


def gla_project_backward(dq, dk, dv, dgate, dcum, low, h1, dh2, w1, wgi, wlow, wgk, bgk):
    s = h1.shape[0]
    ts = ROW_TILE

    def body(dq_ref, dk_ref, dv_ref, dgate_ref, dcum_ref, low_ref, h1_ref, dh2_ref, w1_ref,
             wgi_ref, wlow_ref, wgk_ref, bgk_ref, dh1_ref, dproj_ref, dlow_ref, ggk_ref, small_ref):
        @pl.when(pl.program_id(0) == 0)
        def _():
            ggk_ref[...] = jnp.zeros_like(ggk_ref)
            small_ref[...] = jnp.zeros_like(small_ref)

        low = low_ref[...]
        z = _nn(low, wgk_ref[...]) + bgk_ref[...]
        upper_f = _chunk_masks()[1].astype(F32)
        dlg = jnp.concatenate([_nn_exact(upper_f, dcum_ref[r0:r0 + CHUNK, :]) for r0 in range(0, ts, CHUNK)],
                              axis=0)
        dz = dlg * (1.0 / GATE_NORM) * _sigmoid(-z)
        dz_bf = _bf(dz)
        ggk_ref[...] += _tn(low, dz_bf)
        small_ref[1:2, 0:KEY_W] += jnp.sum(dz, axis=0, keepdims=True)
        dlow = _bf(_nt(dz_bf, wgk_ref[...]))
        dlow_ref[...] = dlow
        dn1 = _nt(dlow, wlow_ref[...])
        for ref, lo, hi in ((dq_ref, 0, KEY_W), (dk_ref, KEY_W, 2 * KEY_W),
                            (dv_ref, 2 * KEY_W, 2 * KEY_W + D), (dgate_ref, 2 * KEY_W + D, GLA_MAIN)):
            piece = ref[...]
            dproj_ref[:, lo:hi] = piece
            dn1 = dn1 + _nt(piece, wgi_ref[:, lo:hi])
        hv = h1_ref[...]
        r = lax.rsqrt(jnp.mean(hv * hv, axis=-1, keepdims=True) + EPS)
        hhat = hv * r
        small_ref[0:1, :] += jnp.sum(dn1 * hhat, axis=0, keepdims=True)
        dxh = dn1 * w1_ref[...]
        dh1_ref[...] = dh2_ref[...] + r * (dxh - hhat * jnp.mean(dxh * hhat, axis=-1, keepdims=True))

    row = lambda cols: pl.BlockSpec((ts, cols), lambda i: (i, 0))
    return pl.pallas_call(
        body, name="gla_project_backward", grid=(s // ts,),
        out_shape=(jax.ShapeDtypeStruct((s, D), F32), jax.ShapeDtypeStruct((s, GLA_MAIN), BF16),
                   jax.ShapeDtypeStruct((s, RANK_PAD), BF16), jax.ShapeDtypeStruct((RANK_PAD, KEY_W), F32),
                   jax.ShapeDtypeStruct((8, D), F32)),
        in_specs=[row(KEY_W), row(KEY_W), row(D), row(D), row(KEY_W), row(RANK_PAD), row(D), row(D),
                  _full((1, D)), _full((D, GLA_MAIN)), _full((D, RANK_PAD)), _full((RANK_PAD, KEY_W)),
                  _full((1, KEY_W))],
        out_specs=(row(D), row(GLA_MAIN), row(RANK_PAD), _full((RANK_PAD, KEY_W)), _full((8, D))),
        compiler_params=_params("arbitrary"),
    )(dq, dk, dv, dgate, dcum, low, h1, dh2, w1, wgi, wlow, wgk, bgk)


def _groups_from_quarters(a):
    return a.reshape(N_CHIPS, GROUPS, 64, GROUP_DIM).transpose(1, 0, 2, 3).reshape(GROUPS, GROUP_DIM, GROUP_DIM)


def _quarters_from_groups(a):
    return a.reshape(GROUPS, N_CHIPS, 64, GROUP_DIM).transpose(1, 0, 2, 3).reshape(N_CHIPS, GROUP_DIM, GROUP_DIM)


def _gla_in_weights(wgi_q):
    wgi_all = jnp.concatenate([wgi_q[q] for q in range(N_CHIPS)], axis=1)
    wlow = jnp.pad(wgi_all[:, GLA_MAIN:], ((0, 0), (0, RANK_PAD - GATE_RANK)))
    return wgi_all, wlow


def local_gradients(xs, target, w0, w1, wf, wpi, gw, gb, scale, wpo, gla_quarters, wgk, bgk, hw_tiled, place):
    wgi_q, wgo_q = gla_quarters
    (h1, pooled, gt, n0), (wgi_q,) = pool_forward(xs, w0, wpi, gw, gb, scale, wpo, [wgi_q])
    wgi, wlow = _gla_in_weights(wgi_q)
    (qk, v, gate, low, cum, n1), (wgo_q,) = gla_project(h1, w1, wgi, wlow, wgk, bgk, [wgo_q])
    wgo = wgo_q.reshape(D, D)
    o, states, scores = gla_forward(qk, v, cum)

    dh2, do, dgate, g_gla_out, small_top = head_and_loss(o, gate, h1, target, hw_tiled, wgo, wf)
    dq, dk, dv, dcum = gla_backward(qk, v, cum, do, states, scores)
    dh1, dproj, dlow, g_gk_pad, small_gla = gla_project_backward(
        dq, dk, dv, dgate, dcum, low, h1, dh2, w1, wgi, wlow, wgk, bgk)
    g_gla_main, g_gla_low = matmul_tn(n1, dproj, "grad_gla_in", narrow=dlow)
    g_gla_in = jnp.concatenate([g_gla_main, g_gla_low[:, :GATE_RANK]], axis=1)

    def chip_sums(grads, tag):
        theirs = exchange_with_sibling(grads, "exchange_with_sibling_" + tag)
        return add_halves(grads, theirs, place, "add_halves_" + tag)

    gla_sums = chip_sums(
        [jnp.stack([g_gla_in[:, GLA_IN_QUARTER * q:GLA_IN_QUARTER * (q + 1)] for q in range(N_CHIPS)]),
         g_gla_out.reshape(N_CHIPS, D // N_CHIPS, D)], "gla")
    (dx, dpool, g_pool_out, g_group_w, small_pool), gla_got = pool_backward(
        xs, dh1, pooled, gt, w0, wpi, gw, gb, scale, wpo, [b for _, b in gla_sums])
    g_pool_in = matmul_tn(n0, dpool, "grad_pool_in", by_column_tile=True)

    pool_sums = chip_sums(
        [g_pool_in, _quarters_from_groups(g_group_w), g_pool_out.reshape(N_CHIPS, D // N_CHIPS, D)], "pool")
    pool_got = scatter_to_owners([b for _, b in pool_sums], "scatter_to_owners_pool")
    reduced, total = join_halves(
        add_parts([f for f, _ in pool_sums + gla_sums], list(pool_got) + list(gla_got), place, "add_parts"),
        small_pool, small_gla, small_top, g_gk_pad)
    return dx, reduced, total


def kernel(x, norm_w, pool_in_w, pool_group_w, pool_group_b, pool_scale, pool_out_w, gla_in_w, gla_gk_w, gla_gk_b, gla_head_norm_w, gla_out_w, final_norm_w, loss_target, m_norm_w, m_pool_in_w, m_pool_group_w, m_pool_group_b, m_pool_scale, m_pool_out_w, m_gla_in_w, m_gla_gk_w, m_gla_gk_b, m_gla_head_norm_w, m_gla_out_w, m_final_norm_w, v_norm_w, v_pool_in_w, v_pool_group_w, v_pool_group_b, v_pool_scale, v_pool_out_w, v_gla_in_w, v_gla_gk_w, v_gla_gk_b, v_gla_head_norm_w, v_gla_out_w, v_final_norm_w):
    xs = x[0]
    target = loss_target[0]
    q_chip = 2 * lax.axis_index("x") + lax.axis_index("y")
    place = jnp.stack([lax.axis_index("c"), q_chip]).astype(jnp.int32)

    (wpi, gw_q, wpo_q, wgi_q, wgo_q), small_all = allgather_weights(
        [pool_in_w[0], pool_group_w[0].reshape(GROUP_DIM, GROUP_DIM), pool_out_w[0], gla_in_w[0], gla_out_w[0]],
        exchange=(True, True, True, False, False),
        smalls=[gla_gk_b, gla_head_norm_w, pool_group_b[0], gla_gk_w[0]])
    gw = _groups_from_quarters(gw_q)
    wpo = wpo_q.reshape(D, D)
    small_all = small_all[0::2]
    bgk = small_all[:, 0, :].reshape(1, KEY_W)
    hw = small_all[:, 1, 0:64].reshape(1, HEAD_V)
    gb = small_all[:, 2:2 + GROUPS, 0:64].transpose(1, 0, 2).reshape(1, D)
    wgk16 = small_all[:, 8:8 + GATE_RANK, :].transpose(1, 0, 2).reshape(GATE_RANK, KEY_W)
    wgk = _bf(jnp.pad(wgk16, ((0, RANK_PAD - GATE_RANK), (0, 0))))
    hw_tiled = jnp.tile(hw, (1, HEADS))

    w0 = norm_w[0:1]
    w1 = norm_w[1:2]
    wf = final_norm_w.reshape(1, D)

    dx, reduced, total = local_gradients(
        xs, target, w0, w1, wf, wpi, gw, gb, pool_scale, wpo, [wgi_q, wgo_q], wgk, bgk, hw_tiled, place)
    r_pool_in, r_group_w, r_pool_out, r_gla_in, r_gla_out = reduced
    r_group_w = r_group_w.reshape(GROUPS, 64, GROUP_DIM)

    loss = total[7, 0]
    g_norm = jnp.stack([total[0], total[3]])
    g_scale = total[1:2]
    g_final = total[5]
    pick = lambda full, width: lax.dynamic_slice_in_dim(full, q_chip * width, width, axis=-1)
    g_gk_b = pick(total[4:5, 0:KEY_W], 128)
    g_hnw = pick(total[6:7, 0:HEAD_V], 64)
    g_group_b = pick(total[2].reshape(GROUPS, GROUP_DIM), 64)[None]
    g_gk_w = pick(total[8:16].reshape(GATE_RANK, KEY_W), 128)[None]

    turn = lambda a: jnp.transpose(a, (2, 0, 1))
    back = lambda a: jnp.transpose(a, (1, 2, 0))
    as2d = lambda a, w: a.reshape(-1, w.shape[-1])
    big_names = ("pool_in_w", "pool_group_w", "pool_out_w", "gla_in_w", "gla_out_w")
    big_args = [(pool_in_w, r_pool_in[None], m_pool_in_w, v_pool_in_w),
                (pool_group_w, r_group_w[None], m_pool_group_w, v_pool_group_w),
                (pool_out_w, r_pool_out[None], m_pool_out_w, v_pool_out_w),
                (gla_in_w, r_gla_in[None], m_gla_in_w, v_gla_in_w),
                (gla_out_w, r_gla_out[None], m_gla_out_w, v_gla_out_w)]
    to_kernel = lambda n, a, w: turn(a) if n == "gla_in_w" else as2d(a, w)
    from_kernel = lambda n, a, w: back(a) if n == "gla_in_w" else a.reshape(w.shape)
    big_in = [tuple(to_kernel(n, a, p[0]) for a in p) for n, p in zip(big_names, big_args)]
    big_out = adamw(big_in, "adamw")
    big = {n: (from_kernel(n, i[1], p[0]),) + tuple(from_kernel(n, o, p[0]) for o in out)
           for n, p, i, out in zip(big_names, big_args, big_in, big_out)}

    small_names = ("norm_w", "pool_group_b", "pool_scale", "gla_gk_w", "gla_gk_b", "gla_head_norm_w",
                   "final_norm_w")
    small_args = [(norm_w, g_norm, m_norm_w, v_norm_w),
                  (pool_group_b, g_group_b, m_pool_group_b, v_pool_group_b),
                  (pool_scale, g_scale, m_pool_scale, v_pool_scale),
                  (gla_gk_w, g_gk_w, m_gla_gk_w, v_gla_gk_w),
                  (gla_gk_b, g_gk_b, m_gla_gk_b, v_gla_gk_b),
                  (gla_head_norm_w, g_hnw, m_gla_head_norm_w, v_gla_head_norm_w),
                  (final_norm_w, g_final, m_final_norm_w, v_final_norm_w)]
    small_out = adamw_small([tuple(as2d(a, p[0]) for a in p) for p in small_args])
    small = {n: (p[1].reshape(p[0].shape),) + tuple(o.reshape(p[0].shape) for o in out)
             for n, p, out in zip(small_names, small_args, small_out)}
    results = [
        small["norm_w"],
        big["pool_in_w"],
        big["pool_group_w"],
        small["pool_group_b"],
        small["pool_scale"],
        big["pool_out_w"],
        big["gla_in_w"],
        small["gla_gk_w"],
        small["gla_gk_b"],
        small["gla_head_norm_w"],
        big["gla_out_w"],
        small["final_norm_w"],
    ]
    grads, deltas, new_m, new_v = zip(*results)
    return (loss, dx[None], *grads, *deltas, *new_m, *new_v)
```

```python
import jax
import jax.numpy as jnp
from jax import lax
from jax.experimental import pallas as pl
from jax.experimental.pallas import tpu as pltpu

F32 = jnp.float32
BF16 = jnp.bfloat16
MESH = pl.DeviceIdType.MESH

D = 1024
POOL_WINDOWS = (2, 4, 8, 16)
GROUPS = 4
GROUP_DIM = 256
HEADS = 4
HEAD_K = 128
HEAD_V = 256
KEY_W = 512
CHUNK = 64
GATE_RANK = 16
GATE_NORM = 16.0
GLA_IN = 3088
GLA_MAIN = 3072
RANK_PAD = 128
EPS = 1e-6
HALO = 32

ADAM_LR = 0.001
ADAM_B1 = 0.9
ADAM_B2 = 0.999
ADAM_EPS = 1e-08
ADAM_WD = 0.01
ADAM_STEP = 10

N_CHIPS = 4
N_DEV = 8
GLA_IN_QUARTER = GLA_IN // N_CHIPS

VMEM_LIMIT = 56 * 1024 * 1024


def _nn(a, b):
    return lax.dot_general(a, b, (((1,), (0,)), ((), ())), preferred_element_type=F32)


def _nt(a, b):
    return lax.dot_general(a, b, (((1,), (1,)), ((), ())), preferred_element_type=F32)


def _tn(a, b):
    return lax.dot_general(a, b, (((0,), (0,)), ((), ())), preferred_element_type=F32)


def _nn_exact(a, b):
    return lax.dot_general(a, b, (((1,), (0,)), ((), ())), preferred_element_type=F32,
                           precision=lax.Precision.HIGHEST)


def _bf(a):
    return a.astype(BF16)


def _params(*sem):
    return pltpu.CompilerParams(dimension_semantics=sem, vmem_limit_bytes=VMEM_LIMIT)


def _full(shape):
    return pl.BlockSpec(shape, lambda i: (0,) * len(shape))


def _position():
    return lax.axis_index("x"), lax.axis_index("y"), lax.axis_index("c")


def _gather_small(in_ref, all_ref, send_sems, recv_sems, local_sem):
    x, y, c = _position()
    me = 4 * x + 2 * y + c
    mine = pltpu.make_async_copy(in_ref, all_ref.at[me], local_sem)
    mine.start()
    sends = []
    for k in range(N_DEV - 1):
        fx, fy, fc = (k + 1) >> 2 & 1, (k + 1) >> 1 & 1, (k + 1) & 1
        cp = pltpu.make_async_remote_copy(
            src_ref=in_ref, dst_ref=all_ref.at[me],
            send_sem=send_sems.at[k], recv_sem=recv_sems.at[k],
            device_id=(x ^ fx, y ^ fy, c ^ fc), device_id_type=MESH)
        cp.start()
        sends.append(cp)
    def wait():
        for k in range(N_DEV - 1):
            fx, fy, fc = (k + 1) >> 2 & 1, (k + 1) >> 1 & 1, (k + 1) & 1
            src_dev = 4 * (x ^ fx) + 2 * (y ^ fy) + (c ^ fc)
            pltpu.make_async_remote_copy(
                src_ref=in_ref, dst_ref=all_ref.at[src_dev],
                send_sem=send_sems.at[k], recv_sem=recv_sems.at[k],
                device_id=(x, y, c), device_id_type=MESH).wait_recv()
        for cp in sends:
            cp.wait_send()
        mine.wait()

    return wait


SMALL_SEMS = [pltpu.SemaphoreType.DMA((N_DEV - 1,)), pltpu.SemaphoreType.DMA((N_DEV - 1,)),
              pltpu.SemaphoreType.DMA]
VMEM_SPEC = pl.BlockSpec(memory_space=pltpu.VMEM)


def _other_chips(x, y):
    return [(1 - x, y), (x, 1 - y), (1 - x, 1 - y)]


def _any_specs(n):
    return [pl.BlockSpec(memory_space=pl.ANY)] * n


def _halves(rows, c):
    half = rows // 2
    return pl.ds(c * half, half), pl.ds((1 - c) * half, half)


CAST_ROWS = 256


def _gather_copy(out_ref, send_sems, recv_sems, k, quarter, half, to, src=None):
    dst = out_ref.at[quarter, half]
    return pltpu.make_async_remote_copy(
        src_ref=dst if src is None else src, dst_ref=dst,
        send_sem=send_sems.at[k], recv_sem=recv_sems.at[k], device_id=to, device_id_type=MESH)


SMALL_IN_ROWS = 24


def allgather_weights(quarters, exchange, smalls):
    n = len(quarters)
    shapes = [w.shape for w in quarters]
    moved = [i for i in range(n) if exchange[i]]

    def body(*refs):
        w_refs, (gkb_ref, hnw_ref, gb_ref, gkw_ref) = refs[:n], refs[n:n + 4]
        out_refs, small_all_ref = refs[n + 4:2 * n + 4], refs[2 * n + 4]
        refs = refs[2 * n + 5:]
        f32_bufs, bf_bufs = refs[:n], refs[n:2 * n]
        send_sems, recv_sems, local_sems, small_ref = refs[2 * n:2 * n + 4]
        small_ref[...] = jnp.zeros_like(small_ref)
        small_ref[0:1, :] = gkb_ref[...]
        small_ref[1:2, 0:64] = hnw_ref[...]
        small_ref[2:2 + GROUPS, 0:64] = gb_ref[...]
        small_ref[8:8 + GATE_RANK, :] = gkw_ref[...]
        wait_small = _gather_small(small_ref, small_all_ref, *refs[2 * n + 4:])
        x, y, c = _position()
        q = 2 * x + y
        sibling = (x, y, 1 - c)
        chips = _other_chips(x, y)

        def copy(k, i, quarter, half, to, src=None):
            return _gather_copy(out_refs[i], send_sems, recv_sems, k * n + i, quarter, half, to, src)

        loads = [pltpu.make_async_copy(w_refs[i], f32_bufs[i], local_sems.at[i]) for i in range(n)]
        for cp in loads:
            cp.start()
        keeps, sends = [], []
        for i in range(n):
            loads[i].wait()
            for r0 in range(0, shapes[i][0], CAST_ROWS):
                bf_bufs[i][r0:r0 + CAST_ROWS, :] = _bf(f32_bufs[i][r0:r0 + CAST_ROWS, :])
            keep = pltpu.make_async_copy(bf_bufs[i], out_refs[i].at[q], local_sems.at[n + i])
            keep.start()
            keeps.append(keep)
            if not exchange[i]:
                continue
            mine, _ = _halves(shapes[i][0], c)
            for j, chip in enumerate(chips):
                cp = copy(j, i, q, mine, (*chip, c), src=bf_bufs[i].at[mine])
                cp.start()
                sends.append(cp)
        for j, chip in enumerate(chips):
            qj = 2 * chip[0] + chip[1]
            for i in moved:
                mine, _ = _halves(shapes[i][0], c)
                copy(j, i, qj, mine, (x, y, c)).wait_recv()
                cp = copy(3 + j, i, qj, mine, sibling)
                cp.start()
                sends.append(cp)
        for j, chip in enumerate(chips):
            qj = 2 * chip[0] + chip[1]
            for i in moved:
                _, other = _halves(shapes[i][0], c)
                copy(3 + j, i, qj, other, (x, y, c)).wait_recv()
        wait_small()
        for cp in sends:
            cp.wait_send()
        for cp in keeps:
            cp.wait()

    outs = pl.pallas_call(
        body, name="allgather_weights",
        out_shape=[jax.ShapeDtypeStruct((N_CHIPS, *s), BF16) for s in shapes]
                  + [jax.ShapeDtypeStruct((N_DEV, SMALL_IN_ROWS, 128), F32)],
        in_specs=_any_specs(n) + [VMEM_SPEC] * 4, out_specs=_any_specs(n) + [VMEM_SPEC],
        scratch_shapes=([pltpu.VMEM(s, F32) for s in shapes] + [pltpu.VMEM(s, BF16) for s in shapes]
                        + [pltpu.SemaphoreType.DMA((6 * n,)), pltpu.SemaphoreType.DMA((6 * n,)),
                           pltpu.SemaphoreType.DMA((2 * n,)), pltpu.VMEM((SMALL_IN_ROWS, 128), F32)] + SMALL_SEMS),
        compiler_params=pltpu.CompilerParams(vmem_limit_bytes=VMEM_LIMIT),
    )(*quarters, *smalls)
    return outs[:n], outs[n]


def exchange_with_sibling(grads, name):
    n = len(grads)

    def body(*refs):
        g_refs, theirs_refs = refs[:n], refs[n:2 * n]
        send_sems, recv_sems = refs[2 * n:]
        x, y, c = _position()
        copies = []
        for i in range(n):
            _, other = _halves(g_refs[i].shape[1], c)
            cp = pltpu.make_async_remote_copy(
                src_ref=g_refs[i].at[:, other], dst_ref=theirs_refs[i],
                send_sem=send_sems.at[i], recv_sem=recv_sems.at[i],
                device_id=(x, y, 1 - c), device_id_type=MESH)
            cp.start()
            copies.append(cp)
        for cp in copies:
            cp.wait()

    return pl.pallas_call(
        body, name=name,
        out_shape=[jax.ShapeDtypeStruct((N_CHIPS, g.shape[1] // 2, g.shape[2]), F32) for g in grads],
        in_specs=_any_specs(n), out_specs=_any_specs(n),
        scratch_shapes=[pltpu.SemaphoreType.DMA((n,)), pltpu.SemaphoreType.DMA((n,))],
    )(*grads)


def _scatter_copies(b_refs, got_refs, send_sems, recv_sems):
    n = len(b_refs)
    x, y, c = _position()
    copies = []
    for j, chip in enumerate(_other_chips(x, y)):
        qj = 2 * chip[0] + chip[1]
        for i in range(n):
            copies.append(pltpu.make_async_remote_copy(
                src_ref=b_refs[i].at[qj], dst_ref=got_refs[i].at[j],
                send_sem=send_sems.at[j * n + i], recv_sem=recv_sems.at[j * n + i],
                device_id=(*chip, c), device_id_type=MESH))
    return copies


def _scatter_shapes(chip_sums):
    return [jax.ShapeDtypeStruct((N_CHIPS - 1, *b.shape[1:]), BF16) for b in chip_sums]


def scatter_to_owners(chip_sums, name):
    n = len(chip_sums)

    def body(*refs):
        copies = _scatter_copies(refs[:n], refs[n:2 * n], *refs[2 * n:])
        for cp in copies:
            cp.start()
        for cp in copies:
            cp.wait()

    return pl.pallas_call(
        body, name=name,
        out_shape=_scatter_shapes(chip_sums),
        in_specs=_any_specs(n), out_specs=_any_specs(n),
        scratch_shapes=[pltpu.SemaphoreType.DMA((3 * n,)), pltpu.SemaphoreType.DMA((3 * n,))],
    )(*chip_sums)


SMALL_SUM_ROWS = 16


def join_halves(reduced, small_in, small_pool, small_gla, small_top, g_gk_pad):
    n = len(reduced)

    def body(*refs):
        in_ref, pool_ref, gla_ref, top_ref, gk_ref = refs[n:n + 5]
        buf_refs, total_ref = refs[n + 5:2 * n + 5], refs[2 * n + 5]
        send_sems, recv_sems, all_ref, small_ref = refs[2 * n + 6:2 * n + 10]
        small_ref[0:1, :] = in_ref[0:1, :]
        small_ref[1:3, :] = pool_ref[1:3, :]
        small_ref[3:5, :] = gla_ref[0:2, :]
        small_ref[5:8, :] = top_ref[0:3, :]
        for r in range(GATE_RANK):
            small_ref[8 + r // 2:9 + r // 2, (r % 2) * KEY_W:(r % 2 + 1) * KEY_W] = gk_ref[r:r + 1, :]
        x, y, c = _position()
        copies = []
        for i in range(n):
            mine, _ = _halves(buf_refs[i].shape[0], c)
            cp = pltpu.make_async_remote_copy(
                src_ref=buf_refs[i].at[mine], dst_ref=buf_refs[i].at[mine],
                send_sem=send_sems.at[i], recv_sem=recv_sems.at[i],
                device_id=(x, y, 1 - c), device_id_type=MESH)
            cp.start()
            copies.append(cp)
        _gather_small(small_ref, all_ref, *refs[2 * n + 10:])()
        total = all_ref[0]
        for dev in range(1, N_DEV):
            total = total + all_ref[dev]
        total_ref[...] = total
        for cp in copies:
            cp.wait()

    outs = pl.pallas_call(
        body, name="join_halves",
        out_shape=[jax.ShapeDtypeStruct(r.shape, F32) for r in reduced]
                  + [jax.ShapeDtypeStruct((SMALL_SUM_ROWS, D), F32)],
        in_specs=_any_specs(n) + [VMEM_SPEC] * 5, out_specs=_any_specs(n) + [VMEM_SPEC],
        input_output_aliases={i: i for i in range(n)},
        scratch_shapes=[pltpu.SemaphoreType.DMA((n,)), pltpu.SemaphoreType.DMA((n,)),
                        pltpu.VMEM((N_DEV, SMALL_SUM_ROWS, D), F32), pltpu.VMEM((SMALL_SUM_ROWS, D), F32)]
                       + SMALL_SEMS,
    )(*reduced, small_in, small_pool, small_gla, small_top, g_gk_pad)
    return outs[:n], outs[n]


ADD_ROWS = 512
ADD_HALVES_ROWS = 256


def _spans(counts):
    starts, total = [], 0
    for count in counts:
        starts.append(total)
        total += count
    return starts, total


def _local_step(t, start, count):
    return jnp.clip(t - start, 0, count - 1)


def add_halves(grads, theirs, place, name):
    n = len(grads)
    shapes = [t.shape for t in theirs]
    rbs = [min(ADD_HALVES_ROWS, sh[1]) for sh in shapes]
    counts = [sh[1] // rb for sh, rb in zip(shapes, rbs)]
    starts, total = _spans(counts)

    def body(place_ref, *refs):
        a_refs, b_refs = refs[:n], refs[n:2 * n]
        f_refs, h_refs = refs[2 * n:3 * n], refs[3 * n:4 * n]
        t = pl.program_id(0)
        q = place_ref[1]
        for i in range(n):
            @pl.when((t >= starts[i]) & (t < starts[i] + counts[i]))
            def _(i=i):
                h_refs[i][...] = _bf(a_refs[i][...] + b_refs[i][...])
                f_refs[i][...] = a_refs[i][q] + b_refs[i][q]

    def specs(i):
        block = (N_CHIPS, rbs[i], shapes[i][2])
        step = lambda t: _local_step(t, starts[i], counts[i])
        mine = pl.BlockSpec(block, lambda t, place: (0, place[0] * counts[i] + step(t), 0))
        same = pl.BlockSpec(block, lambda t, place: (0, step(t), 0))
        own = pl.BlockSpec(block[1:], lambda t, place: (step(t), 0))
        return mine, same, own

    all_specs = [specs(i) for i in range(n)]
    outs = pl.pallas_call(
        body, name=name,
        grid_spec=pltpu.PrefetchScalarGridSpec(
            num_scalar_prefetch=1, grid=(total,),
            in_specs=[sp[0] for sp in all_specs] + [sp[1] for sp in all_specs],
            out_specs=[sp[2] for sp in all_specs] + [sp[1] for sp in all_specs]),
        out_shape=[jax.ShapeDtypeStruct(sh[1:], F32) for sh in shapes]
                  + [jax.ShapeDtypeStruct(sh, BF16) for sh in shapes],
        compiler_params=_params("arbitrary"),
    )(place, *grads, *theirs)
    return list(zip(outs[:n], outs[n:]))


def add_parts(owns, gots, place, name):
    n = len(owns)
    shapes = [g.shape for g in gots]
    rbs = [min(ADD_ROWS, sh[1]) for sh in shapes]
    counts = [sh[1] // rb for sh, rb in zip(shapes, rbs)]
    starts, total = _spans(counts)

    def body(place_ref, *refs):
        o_refs, g_refs, out_refs = refs[:n], refs[n:2 * n], refs[2 * n:]
        t = pl.program_id(0)
        for i in range(n):
            @pl.when((t >= starts[i]) & (t < starts[i] + counts[i]))
            def _(i=i):
                total_i = o_refs[i][...]
                for j in range(N_CHIPS - 1):
                    total_i = total_i + g_refs[i][j].astype(F32)
                out_refs[i][...] = total_i

    def specs(i):
        rb, cols = rbs[i], shapes[i][2]
        step = lambda t: _local_step(t, starts[i], counts[i])
        return (pl.BlockSpec((rb, cols), lambda t, place: (step(t), 0)),
                pl.BlockSpec((N_CHIPS - 1, rb, cols), lambda t, place: (0, step(t), 0)),
                pl.BlockSpec((rb, cols), lambda t, place: (place[0] * counts[i] + step(t), 0)))

    all_specs = [specs(i) for i in range(n)]
    return pl.pallas_call(
        body, name=name,
        grid_spec=pltpu.PrefetchScalarGridSpec(
            num_scalar_prefetch=1, grid=(total,),
            in_specs=[sp[0] for sp in all_specs] + [sp[1] for sp in all_specs],
            out_specs=[sp[2] for sp in all_specs]),
        out_shape=[jax.ShapeDtypeStruct((2 * sh[1], sh[2]), F32) for sh in shapes],
        compiler_params=_params("arbitrary"),
    )(place, *owns, *gots)


def _adam_math(w, g, m, v):
    m = ADAM_B1 * m + (1.0 - ADAM_B1) * g
    v = ADAM_B2 * v + (1.0 - ADAM_B2) * (g * g)
    m_hat = m / (1.0 - ADAM_B1 ** ADAM_STEP)
    v_hat = v / (1.0 - ADAM_B2 ** ADAM_STEP)
    delta = -ADAM_LR * (m_hat / (jnp.sqrt(v_hat) + ADAM_EPS) + ADAM_WD * w)
    return delta, m, v


ADAM_BLOCK_BYTES = 2 ** 19
ADAM_MOST_STEPS = 8


def adamw(params, name):
    n = len(params)
    shapes = [p[0].shape for p in params]

    def tile_rows(shape):
        rows, cols = shape[0], shape[-1]
        aligned = 1 if len(shape) == 3 else 8
        divisors = [t for t in range(aligned, rows + 1, aligned) if rows % t == 0]
        tile = max(t for t in divisors if t * cols * 4 <= ADAM_BLOCK_BYTES)
        if rows // tile > ADAM_MOST_STEPS:
            tile = min(t for t in divisors if rows // t <= ADAM_MOST_STEPS)
        return tile

    tiles = [tile_rows(sh) for sh in shapes]
    counts = [sh[0] // tl for sh, tl in zip(shapes, tiles)]
    starts, total = _spans(counts)

    def body(*refs):
        ins, outs = refs[:4 * n], refs[4 * n:]
        t = pl.program_id(0)
        for i in range(n):
            @pl.when((t >= starts[i]) & (t < starts[i] + counts[i]))
            def _(i=i):
                w_ref, g_ref, m_ref, v_ref = ins[4 * i:4 * i + 4]
                d, nm, nv = _adam_math(w_ref[...], g_ref[...], m_ref[...], v_ref[...])
                outs[3 * i][...] = d
                outs[3 * i + 1][...] = nm
                outs[3 * i + 2][...] = nv

    def spec(i):
        block = (tiles[i],) + shapes[i][1:]
        zeros = (0,) * (len(block) - 1)
        return pl.BlockSpec(block, lambda t: (_local_step(t, starts[i], counts[i]),) + zeros)

    outs = pl.pallas_call(
        body, name=name, grid=(total,),
        out_shape=[jax.ShapeDtypeStruct(sh, F32) for sh in shapes for _ in range(3)],
        in_specs=[spec(i) for i in range(n) for _ in range(4)],
        out_specs=[spec(i) for i in range(n) for _ in range(3)],
        compiler_params=_params("arbitrary"),
    )(*[a for p in params for a in p])
    return [tuple(outs[3 * i:3 * i + 3]) for i in range(n)]


def adamw_small(params):
    n = len(params)

    def body(*refs):
        ins, outs = refs[:4 * n], refs[4 * n:]
        for k in range(n):
            w_ref, g_ref, m_ref, v_ref = ins[4 * k:4 * k + 4]
            d, nm, nv = _adam_math(w_ref[...], g_ref[...], m_ref[...], v_ref[...])
            outs[3 * k][...] = d
            outs[3 * k + 1][...] = nm
            outs[3 * k + 2][...] = nv

    flat = [a for p in params for a in p]
    outs = pl.pallas_call(
        body, name="adamw_small",
        out_shape=[jax.ShapeDtypeStruct(p[0].shape, F32) for p in params for _ in range(3)],
        in_specs=[VMEM_SPEC] * (4 * n), out_specs=[VMEM_SPEC] * (3 * n),
    )(*flat)
    return [tuple(outs[3 * k:3 * k + 3]) for k in range(n)]


def matmul_tn(a, b, narrow, name, tile_n=512):
    s, m = a.shape
    n = b.shape[1]

    def body(a_ref, b_ref, c_ref, out_ref, out_c_ref):
        out_ref[...] = _tn(a_ref[...], b_ref[...])

        @pl.when(pl.program_id(0) == 0)
        def _():
            out_c_ref[...] = _tn(a_ref[...], c_ref[...])

    return pl.pallas_call(
        body, name=name, grid=(n // tile_n,),
        out_shape=(jax.ShapeDtypeStruct((m, n), F32), jax.ShapeDtypeStruct((m, narrow.shape[1]), F32)),
        in_specs=[_full((s, m)), pl.BlockSpec((s, tile_n), lambda j: (0, j)), _full(narrow.shape)],
        out_specs=(pl.BlockSpec((m, tile_n), lambda j: (0, j)), _full((m, narrow.shape[1]))),
        compiler_params=_params("arbitrary"),
    )(a, b, narrow)


ROW_TILE = 512


def _row_index(tile, rows):
    return tile * rows + lax.broadcasted_iota(jnp.int32, (rows, 1), 0)


def _inverse_counts(t_glob):
    return [1.0 / jnp.minimum(t_glob + 1, w).astype(F32) for w in POOL_WINDOWS]


def _sigmoid(z):
    return 1.0 / (1.0 + jnp.exp(-z))


def _trailing_sums(src, tmp, cols, window, rows):
    bufs = (src, tmp)
    span, level, start = 1, 0, 0
    while span < window:
        start += 8
        a, b = bufs[level % 2], bufs[(level + 1) % 2]
        n = HALO + rows - start
        b[start:start + n, cols] = a[start:start + n, cols] + a[start - span:start - span + n, cols]
        span, level = 2 * span, level + 1
    return bufs[level % 2][HALO:HALO + rows, cols]


def _leading_sums(src, tmp, cols, window, rows):
    bufs = (src, tmp)
    span, level, n = 1, 0, rows + HALO
    while span < window:
        n -= 8
        a, b = bufs[level % 2], bufs[(level + 1) % 2]
        b[0:n, cols] = a[0:n, cols] + a[span:span + n, cols]
        span, level = 2 * span, level + 1
    return bufs[level % 2][0:rows, cols]


def gather_in_background(step, last, out_refs, send_sems, recv_sems, finish):
    n = len(out_refs)
    x, y, c = _position()
    q = 2 * x + y
    chips = _other_chips(x, y)

    def copy(k, i, quarter, half, to):
        return _gather_copy(out_refs[i], send_sems, recv_sems, k * n + i, quarter, half, to)

    if not finish:
        @pl.when(step == 0)
        def _():
            for i in range(n):
                mine, _ = _halves(out_refs[i].shape[1], c)
                for j, chip in enumerate(chips):
                    copy(j, i, q, mine, (*chip, c)).start()

        @pl.when(step == last)
        def _():
            for j, chip in enumerate(chips):
                qj = 2 * chip[0] + chip[1]
                for i in range(n):
                    mine, _ = _halves(out_refs[i].shape[1], c)
                    copy(j, i, qj, mine, (x, y, c)).wait_recv()
                    copy(3 + j, i, qj, mine, (x, y, 1 - c)).start()
        return

    @pl.when(step == last)
    def _():
        for j, chip in enumerate(chips):
            qj = 2 * chip[0] + chip[1]
            for i in range(n):
                mine, other = _halves(out_refs[i].shape[1], c)
                copy(3 + j, i, qj, other, (x, y, c)).wait_recv()
                copy(j, i, q, mine, (x, y, c)).wait_send()
                copy(3 + j, i, qj, mine, (x, y, c)).wait_send()


def pool_forward(x, w0, wpi, gw, gb, scale, wpo, later):
    s = x.shape[0]
    ts = ROW_TILE
    nt = s // ts
    assert nt >= 2
    n_later = len(later)

    def body(x_ref, w0_ref, wpi_ref, gw_ref, gb_ref, sc_ref, wpo_ref, *rest):
        rest = rest[n_later:]
        h1_ref, pooled_ref, gt_ref, n0_ref = rest[:4]
        later_refs = rest[4:4 + n_later]
        ubuf, tbuf, hist, send_sems, recv_sems = rest[4 + n_later:]
        i = pl.program_id(0)
        gather_in_background(i, nt - 1, later_refs, send_sems, recv_sems, finish=False)
        xv = x_ref[...]
        r = lax.rsqrt(jnp.mean(xv * xv, axis=-1, keepdims=True) + EPS)
        n0 = _bf(xv * r * w0_ref[...])
        n0_ref[...] = n0
        u = jnp.concatenate([_nn(n0, wpi_ref[0]), _nn(n0, wpi_ref[1])], axis=-1)
        gt = jnp.concatenate([_nn(n0, wpi_ref[2]), _nn(n0, wpi_ref[3])], axis=-1)
        gt_ref[...] = gt

        @pl.when(i == 0)
        def _():
            hist[...] = jnp.zeros_like(hist)

        ubuf[0:HALO, :] = hist[...]
        ubuf[HALO:HALO + ts, :] = u
        hist[...] = u[ts - HALO:, :]
        inv = _inverse_counts(_row_index(i, ts))
        mixed = []
        for g, w in enumerate(POOL_WINDOWS):
            cols = slice(g * GROUP_DIM, (g + 1) * GROUP_DIM)
            pooled = _bf(_trailing_sums(ubuf, tbuf, cols, w, ts) * inv[g] - u[:, cols])
            pooled_ref[:, cols] = pooled
            mixed.append(_nn(pooled, gw_ref[g]))
        mixed = jnp.concatenate(mixed, axis=-1) + gb_ref[...]
        y = mixed * sc_ref[...] * (gt * _sigmoid(gt))
        h1_ref[...] = xv + _nn(_bf(y), wpo_ref[...])
        gather_in_background(i, nt - 1, later_refs, send_sems, recv_sems, finish=True)

    row = lambda cols: pl.BlockSpec((ts, cols), lambda i: (i, 0))
    outs = pl.pallas_call(
        body, name="pool_forward", grid=(nt,),
        out_shape=[jax.ShapeDtypeStruct((s, D), F32), jax.ShapeDtypeStruct((s, D), BF16),
                   jax.ShapeDtypeStruct((s, D), F32), jax.ShapeDtypeStruct((s, D), BF16)]
                  + [jax.ShapeDtypeStruct(a.shape, a.dtype) for a in later],
        in_specs=[row(D), _full((1, D)), _full((N_CHIPS, D, D // 2)), _full((GROUPS, GROUP_DIM, GROUP_DIM)),
                  _full((1, D)), _full((1, D)), _full((D, D))] + _any_specs(n_later),
        out_specs=[row(D), row(D), row(D), row(D)] + _any_specs(n_later),
        input_output_aliases={7 + k: 4 + k for k in range(n_later)},
        scratch_shapes=[pltpu.VMEM((HALO + ts, D), F32), pltpu.VMEM((HALO + ts, D), F32),
                        pltpu.VMEM((HALO, D), F32),
                        pltpu.SemaphoreType.DMA((6 * n_later,)), pltpu.SemaphoreType.DMA((6 * n_later,))],
        compiler_params=_params("arbitrary"),
    )(x, w0, wpi, gw, gb, scale, wpo, *later)
    return outs[:4], outs[4:]


def pool_backward(dh1, pooled, gt, gw, gb, scale, wpo, chip_sums):
    s = dh1.shape[0]
    ts = ROW_TILE
    nt = s // ts
    n_sums = len(chip_sums)

    def body(dh1_ref, pooled_ref, gt_ref, gw_ref, gb_ref, sc_ref, wpo_ref, *rest):
        sum_refs, rest = rest[:n_sums], rest[n_sums:]
        dproj_ref, gpo_ref, ggw_ref, small_ref = rest[:4]
        got_refs = rest[4:4 + n_sums]
        ebuf, tbuf, ahead, send_sems, recv_sems = rest[4 + n_sums:]
        i = pl.program_id(0)
        copies = _scatter_copies(sum_refs, got_refs, send_sems, recv_sems)

        @pl.when(i == 0)
        def _():
            for cp in copies:
                cp.start()

        @pl.when(i == 0)
        def _():
            gpo_ref[...] = jnp.zeros_like(gpo_ref)
            ggw_ref[...] = jnp.zeros_like(ggw_ref)
            small_ref[...] = jnp.zeros_like(small_ref)
            ahead[...] = jnp.zeros_like(ahead)

        dh1_bf = _bf(dh1_ref[...])
        gt = gt_ref[...]
        sc = sc_ref[...]
        dy = _nt(dh1_bf, wpo_ref[...])
        pooled_bf = []
        mixed = []
        for g in range(GROUPS):
            cols = slice(g * GROUP_DIM, (g + 1) * GROUP_DIM)
            pb = pooled_ref[:, cols]
            pooled_bf.append(pb)
            mixed.append(_nn(pb, gw_ref[g]))
        mixed = jnp.concatenate(mixed, axis=-1) + gb_ref[...]
        sg = _sigmoid(gt)
        silu = gt * sg
        gpo_ref[...] += _tn(_bf(mixed * sc * silu), dh1_bf)
        dmixed = dy * sc * silu
        dgt = dy * mixed * sc * (sg * (1.0 + gt * (1.0 - sg)))
        dproj_ref[:, D:] = _bf(dgt)
        small_ref[1:2, :] += jnp.sum(dy * mixed * silu, axis=0, keepdims=True)
        small_ref[2:3, :] += jnp.sum(dmixed, axis=0, keepdims=True)

        inv = _inverse_counts(_row_index(nt - 1 - i, ts))
        ebuf[ts:ts + HALO, :] = ahead[...]
        dpooled = []
        for g in range(GROUPS):
            cols = slice(g * GROUP_DIM, (g + 1) * GROUP_DIM)
            dm = _bf(dmixed[:, cols])
            ggw_ref[g] += _tn(pooled_bf[g], dm)
            dp = _nt(dm, gw_ref[g])
            dpooled.append(dp)
            ebuf[0:ts, cols] = dp * inv[g]
        ahead[...] = ebuf[0:HALO, :]
        du = []
        for g, w in enumerate(POOL_WINDOWS):
            cols = slice(g * GROUP_DIM, (g + 1) * GROUP_DIM)
            du.append(_leading_sums(ebuf, tbuf, cols, w, ts) - dpooled[g])
        dproj_ref[:, :D] = _bf(jnp.concatenate(du, axis=-1))

        @pl.when(i == nt - 1)
        def _():
            for cp in copies:
                cp.wait()

    row = lambda cols: pl.BlockSpec((ts, cols), lambda i: (nt - 1 - i, 0))
    outs = pl.pallas_call(
        body, name="pool_backward", grid=(nt,),
        out_shape=[jax.ShapeDtypeStruct((s, 2 * D), BF16), jax.ShapeDtypeStruct((D, D), F32),
                   jax.ShapeDtypeStruct((GROUPS, GROUP_DIM, GROUP_DIM), F32),
                   jax.ShapeDtypeStruct((8, D), F32)] + _scatter_shapes(chip_sums),
        in_specs=[row(D), row(D), row(D), _full((GROUPS, GROUP_DIM, GROUP_DIM)), _full((1, D)), _full((1, D)),
                  _full((D, D))] + _any_specs(n_sums),
        out_specs=[row(2 * D), _full((D, D)), _full((GROUPS, GROUP_DIM, GROUP_DIM)), _full((8, D))]
                  + _any_specs(n_sums),
        scratch_shapes=[pltpu.VMEM((ts + HALO, D), F32), pltpu.VMEM((ts + HALO, D), F32),
                        pltpu.VMEM((HALO, D), F32),
                        pltpu.SemaphoreType.DMA((3 * n_sums,)), pltpu.SemaphoreType.DMA((3 * n_sums,))],
        compiler_params=_params("arbitrary"),
    )(dh1, pooled, gt, gw, gb, scale, wpo, *chip_sums)
    return outs[:4], outs[4:]


def pool_in_backward(x, dh1, dproj, n0, w0, wpi):
    s = x.shape[0]
    ts = ROW_TILE
    half = D // 2

    def body(x_ref, dh1_ref, dproj_ref, n0_ref, w0_ref, wpi_ref, dx_ref, gpi_ref, small_ref):
        @pl.when(pl.program_id(0) == 0)
        def _():
            gpi_ref[...] = jnp.zeros_like(gpi_ref)
            small_ref[...] = jnp.zeros_like(small_ref)

        n0 = n0_ref[...]
        dn0 = jnp.zeros((ts, D), F32)
        for q in range(N_CHIPS):
            piece = dproj_ref[:, q * half:(q + 1) * half]
            dn0 = dn0 + _nt(piece, wpi_ref[q])
            gpi_ref[q] += _tn(n0, piece)
        xv = x_ref[...]
        r = lax.rsqrt(jnp.mean(xv * xv, axis=-1, keepdims=True) + EPS)
        xhat = xv * r
        small_ref[0:1, :] += jnp.sum(dn0 * xhat, axis=0, keepdims=True)
        dxh = dn0 * w0_ref[...]
        dx_ref[...] = dh1_ref[...] + r * (dxh - xhat * jnp.mean(dxh * xhat, axis=-1, keepdims=True))

    row = lambda cols: pl.BlockSpec((ts, cols), lambda i: (i, 0))
    return pl.pallas_call(
        body, name="pool_in_backward", grid=(s // ts,),
        out_shape=(jax.ShapeDtypeStruct((s, D), F32), jax.ShapeDtypeStruct((N_CHIPS, D, half), F32),
                   jax.ShapeDtypeStruct((8, D), F32)),
        in_specs=[row(D), row(D), row(2 * D), row(D), _full((1, D)), _full((N_CHIPS, D, half))],
        out_specs=(row(D), _full((N_CHIPS, D, half)), _full((8, D))),
        compiler_params=_params("arbitrary"),
    )(x, dh1, dproj, n0, w0, wpi)


def gla_project(h1, w1, wgi, wlow, wgk, bgk, later):
    s = h1.shape[0]
    ts = ROW_TILE
    nt = s // ts
    assert nt >= 2
    n_later = len(later)

    def body(h_ref, w1_ref, wgi_ref, wlow_ref, wgk_ref, bgk_ref, *rest):
        rest = rest[n_later:]
        qk_ref, v_ref, gate_ref, low_ref, cum_ref, n1_ref = rest[:6]
        later_refs = rest[6:6 + n_later]
        send_sems, recv_sems = rest[6 + n_later:]
        gather_in_background(pl.program_id(0), nt - 1, later_refs, send_sems, recv_sems, finish=False)
        hv = h_ref[...]
        r = lax.rsqrt(jnp.mean(hv * hv, axis=-1, keepdims=True) + EPS)
        n1 = _bf(hv * r * w1_ref[...])
        n1_ref[...] = n1
        qk_ref[...] = _nn(n1, wgi_ref[:, 0:2 * KEY_W])
        v_ref[...] = _bf(_nn(n1, wgi_ref[:, 2 * KEY_W:2 * KEY_W + D]))
        gate_ref[...] = _nn(n1, wgi_ref[:, 2 * KEY_W + D:GLA_MAIN])
        low = _bf(_nn(n1, wlow_ref[...]))
        low_ref[...] = low
        z = _nn(low, wgk_ref[...]) + bgk_ref[...]
        lg = (jnp.minimum(z, 0.0) - jnp.log(1.0 + jnp.exp(-jnp.abs(z)))) / GATE_NORM
        lower_f = _chunk_masks()[0].astype(F32)
        for r0 in range(0, ts, CHUNK):
            cum_ref[r0:r0 + CHUNK, :] = _nn_exact(lower_f, lg[r0:r0 + CHUNK, :])
        gather_in_background(pl.program_id(0), nt - 1, later_refs, send_sems, recv_sems, finish=True)

    row = lambda cols: pl.BlockSpec((ts, cols), lambda i: (i, 0))
    outs = pl.pallas_call(
        body, name="gla_project", grid=(nt,),
        out_shape=[jax.ShapeDtypeStruct((s, D), F32), jax.ShapeDtypeStruct((s, D), BF16),
                   jax.ShapeDtypeStruct((s, D), F32), jax.ShapeDtypeStruct((s, RANK_PAD), BF16),
                   jax.ShapeDtypeStruct((s, KEY_W), F32), jax.ShapeDtypeStruct((s, D), BF16)]
                  + [jax.ShapeDtypeStruct(a.shape, a.dtype) for a in later],
        in_specs=[row(D), _full((1, D)), _full((D, GLA_MAIN)), _full((D, RANK_PAD)),
                  _full((RANK_PAD, KEY_W)), _full((1, KEY_W))] + _any_specs(n_later),
        out_specs=[row(D), row(D), row(D), row(RANK_PAD), row(KEY_W), row(D)] + _any_specs(n_later),
        input_output_aliases={6 + k: 6 + k for k in range(n_later)},
        scratch_shapes=[pltpu.SemaphoreType.DMA((6 * n_later,)), pltpu.SemaphoreType.DMA((6 * n_later,))],
        compiler_params=_params("arbitrary"),
    )(h1, w1, wgi, wlow, wgk, bgk, *later)
    return outs[:6], outs[6:]


GLA_BLOCK = 512
CHUNKS_PER_BLOCK = GLA_BLOCK // CHUNK


def _chunk_masks():
    t = lax.broadcasted_iota(jnp.int32, (CHUNK, CHUNK), 0)
    u = lax.broadcasted_iota(jnp.int32, (CHUNK, CHUNK), 1)
    return t >= u, t <= u


def _gla_chunk_terms(q, cum):
    ep = jnp.exp(cum)
    en = jnp.exp(-cum)
    qs = q * (HEAD_K ** -0.5)
    last = cum[CHUNK - 1:CHUNK, :]
    ed = jnp.exp(last - cum)
    dec = jnp.exp(last)
    return ep, en, qs, ed, dec


def gla_forward(qk, v, cum):
    s = qk.shape[0]
    nb = s // GLA_BLOCK
    nc = s // CHUNK

    def body(q_ref, k_ref, v_ref, cum_ref, o_ref, st_ref, sc_ref, state):
        @pl.when(pl.program_id(0) == 0)
        def _():
            state[...] = jnp.zeros_like(state)

        lower, _ = _chunk_masks()

        def chunk(cc, carry):
            rows = pl.ds(pl.multiple_of(cc * CHUNK, CHUNK), CHUNK)
            for h in range(HEADS):
                kc = slice(h * HEAD_K, (h + 1) * HEAD_K)
                vc = slice(h * HEAD_V, (h + 1) * HEAD_V)
                q = q_ref[rows, kc]
                k = k_ref[rows, kc]
                v = v_ref[rows, vc]
                ep, en, qs, ed, dec = _gla_chunk_terms(q, cum_ref[rows, kc])
                a = _bf(qs * ep)
                fwd = _nt(a, _bf(k * en))
                bwd = _nt(_bf(qs * en), _bf(k * ep))
                scores = _bf(jnp.where(lower, fwd, bwd))
                sc_ref[rows, h * CHUNK:(h + 1) * CHUNK] = scores
                st = state[h]
                st_ref[cc, h] = st
                o_ref[rows, vc] = _nn(scores, v) + _nt(a, _bf(st))
                state[h] = st * dec + _tn(v, _bf(k * ed))
            return carry

        lax.fori_loop(0, CHUNKS_PER_BLOCK, chunk, 0, unroll=4)

    return pl.pallas_call(
        body, name="gla_forward", grid=(nb,),
        out_shape=(jax.ShapeDtypeStruct((s, D), F32),
                   jax.ShapeDtypeStruct((nc, HEADS, HEAD_V, HEAD_K), F32),
                   jax.ShapeDtypeStruct((s, HEADS * CHUNK), BF16)),
        in_specs=[pl.BlockSpec((GLA_BLOCK, KEY_W), lambda i: (i, 0)),
                  pl.BlockSpec((GLA_BLOCK, KEY_W), lambda i: (i, 1)),
                  pl.BlockSpec((GLA_BLOCK, D), lambda i: (i, 0)),
                  pl.BlockSpec((GLA_BLOCK, KEY_W), lambda i: (i, 0))],
        out_specs=(pl.BlockSpec((GLA_BLOCK, D), lambda i: (i, 0)),
                   pl.BlockSpec((CHUNKS_PER_BLOCK, HEADS, HEAD_V, HEAD_K), lambda i: (i, 0, 0, 0)),
                   pl.BlockSpec((GLA_BLOCK, HEADS * CHUNK), lambda i: (i, 0))),
        scratch_shapes=[pltpu.VMEM((HEADS, HEAD_V, HEAD_K), F32)],
        compiler_params=_params("arbitrary"),
    )(qk, qk, v, cum)


def gla_backward(qk, v, cum, do, states, scores):
    s = qk.shape[0]
    nb = s // GLA_BLOCK

    def body(q_ref, k_ref, v_ref, cum_ref, do_ref, st_ref, sc_ref, dq_ref, dk_ref, dv_ref, dcum_ref, dstate):
        @pl.when(pl.program_id(0) == 0)
        def _():
            dstate[...] = jnp.zeros_like(dstate)

        lower, _ = _chunk_masks()
        is_last = lax.broadcasted_iota(jnp.int32, (CHUNK, HEAD_K), 0) == CHUNK - 1

        def chunk(step, carry):
            cc = CHUNKS_PER_BLOCK - 1 - step
            rows = pl.ds(pl.multiple_of(cc * CHUNK, CHUNK), CHUNK)
            for h in range(HEADS):
                kc = slice(h * HEAD_K, (h + 1) * HEAD_K)
                vc = slice(h * HEAD_V, (h + 1) * HEAD_V)
                q = q_ref[rows, kc]
                k = k_ref[rows, kc]
                v = v_ref[rows, vc]
                do_c = do_ref[rows, vc]
                ep, en, qs, ed, dec = _gla_chunk_terms(q, cum_ref[rows, kc])
                a = _bf(qs * ep)
                b = _bf(k * en)
                c = _bf(qs * en)
                dk_dec = _bf(k * ep)
                kd = _bf(k * ed)
                scores = sc_ref[rows, h * CHUNK:(h + 1) * CHUNK]
                st = st_ref[cc, h]
                dst = dstate[h]
                dst_bf = _bf(dst)

                dscores = _nt(do_c, v)
                dfwd = _bf(jnp.where(lower, dscores, 0.0))
                dbwd = _bf(jnp.where(lower, 0.0, dscores))
                dv_ref[rows, vc] = _bf(_tn(scores, do_c) + _nt(kd, dst_bf))
                da = _nn(dfwd, b) + _nn(do_c, _bf(st))
                db = _tn(dfwd, a)
                dc = _nn(dbwd, dk_dec)
                ddk = _tn(dbwd, c)
                dkd = _nn(v, dst_bf)
                ddec = jnp.sum(dst * st, axis=0, keepdims=True)
                dstate[h] = dst * dec + _tn(do_c, a)

                m = dkd * k * ed
                dq_ref[rows, kc] = _bf((da * ep + dc * en) * (HEAD_K ** -0.5))
                dk_ref[rows, kc] = _bf(db * en + ddk * ep + dkd * ed)
                dcum = (da * qs + ddk * k) * ep - (db * k + dc * qs) * en - m
                dlast = jnp.sum(m, axis=0, keepdims=True) + ddec * dec
                dcum_ref[rows, kc] = dcum + jnp.where(is_last, dlast, 0.0)
            return carry

        lax.fori_loop(0, CHUNKS_PER_BLOCK, chunk, 0, unroll=4)

    rev = lambda cols, col_block: pl.BlockSpec((GLA_BLOCK, cols), lambda i: (nb - 1 - i, col_block))
    return pl.pallas_call(
        body, name="gla_backward", grid=(nb,),
        out_shape=(jax.ShapeDtypeStruct((s, KEY_W), BF16), jax.ShapeDtypeStruct((s, KEY_W), BF16),
                   jax.ShapeDtypeStruct((s, D), BF16), jax.ShapeDtypeStruct((s, KEY_W), F32)),
        in_specs=[rev(KEY_W, 0), rev(KEY_W, 1), rev(D, 0), rev(KEY_W, 0), rev(D, 0),
                  pl.BlockSpec((CHUNKS_PER_BLOCK, HEADS, HEAD_V, HEAD_K), lambda i: (nb - 1 - i, 0, 0, 0)),
                  rev(HEADS * CHUNK, 0)],
        out_specs=(rev(KEY_W, 0), rev(KEY_W, 0), rev(D, 0), rev(KEY_W, 0)),
        scratch_shapes=[pltpu.VMEM((HEADS, HEAD_V, HEAD_K), F32)],
        compiler_params=_params("arbitrary"),
    )(qk, qk, v, cum, do, states, scores)


def head_and_loss(o, gate, h1, target, hw, wgo, wf):
    s = o.shape[0]
    ts = ROW_TILE

    def body(o_ref, gate_ref, h1_ref, tgt_ref, hw_ref, wgo_ref, wf_ref,
             dh2_ref, do_ref, dgate_ref, ggo_ref, small_ref):
        @pl.when(pl.program_id(0) == 0)
        def _():
            ggo_ref[...] = jnp.zeros_like(ggo_ref)
            small_ref[...] = jnp.zeros_like(small_ref)

        gate = gate_ref[...]
        hw = hw_ref[...]
        sg = _sigmoid(gate)
        silu = gate * sg
        ohat, ro = [], []
        for h in range(HEADS):
            oh = o_ref[:, h * HEAD_V:(h + 1) * HEAD_V]
            rh = lax.rsqrt(jnp.mean(oh * oh, axis=-1, keepdims=True) + EPS)
            ro.append(rh)
            ohat.append(oh * rh)
        ohat = jnp.concatenate(ohat, axis=-1)
        on = ohat * hw
        y2 = _bf(on * silu)
        h2 = h1_ref[...] + _nn(y2, wgo_ref[...])
        rf = lax.rsqrt(jnp.mean(h2 * h2, axis=-1, keepdims=True) + EPS)
        h2hat = h2 * rf
        wf = wf_ref[...]
        diff = h2hat * wf - tgt_ref[...]
        small_ref[2:3, :] += jnp.zeros((1, D), F32) + 0.5 * jnp.sum(diff * diff) / D
        dout = diff / D
        small_ref[0:1, :] += jnp.sum(dout * h2hat, axis=0, keepdims=True)
        dxh = dout * wf
        dh2 = rf * (dxh - h2hat * jnp.mean(dxh * h2hat, axis=-1, keepdims=True))
        dh2_ref[...] = dh2
        dh2_bf = _bf(dh2)
        ggo_ref[...] += _tn(y2, dh2_bf)
        dy2 = _nt(dh2_bf, wgo_ref[...])
        don = dy2 * silu
        dgate_ref[...] = _bf(dy2 * on * (sg * (1.0 + gate * (1.0 - sg))))
        ghw = jnp.sum(don * ohat, axis=0, keepdims=True)
        small_ref[1:2, 0:HEAD_V] += sum(ghw[:, h * HEAD_V:(h + 1) * HEAD_V] for h in range(HEADS))
        dohat = don * hw
        for h in range(HEADS):
            cols = slice(h * HEAD_V, (h + 1) * HEAD_V)
            oh, dh = ohat[:, cols], dohat[:, cols]
            do_ref[:, cols] = _bf(ro[h] * (dh - oh * jnp.mean(dh * oh, axis=-1, keepdims=True)))

    row = lambda cols: pl.BlockSpec((ts, cols), lambda i: (i, 0))
    act = jax.ShapeDtypeStruct((s, D), F32)
    act_bf = jax.ShapeDtypeStruct((s, D), BF16)
    return pl.pallas_call(
        body, name="head_and_loss", grid=(s // ts,),
        out_shape=(act, act_bf, act_bf, jax.ShapeDtypeStruct((D, D), F32), jax.ShapeDtypeStruct((8, D), F32)),
        in_specs=[row(D), row(D), row(D), row(D),
                  _full((1, D)), _full((D, D)), _full((1, D))],
        out_specs=(row(D), row(D), row(D), _full((D, D)), _full((8, D))),
        compiler_params=_params("arbitrary"),
    )(o, gate, h1, target, hw, wgo, wf)


def gla_project_backward(dq, dk, dv, dgate, dcum, low, h1, dh2, w1, wgi, wlow, wgk, bgk):
    s = h1.shape[0]
    ts = ROW_TILE

    def body(dq_ref, dk_ref, dv_ref, dgate_ref, dcum_ref, low_ref, h1_ref, dh2_ref, w1_ref,
             wgi_ref, wlow_ref, wgk_ref, bgk_ref, dh1_ref, dproj_ref, dlow_ref, ggk_ref, small_ref):
        @pl.when(pl.program_id(0) == 0)
        def _():
            ggk_ref[...] = jnp.zeros_like(ggk_ref)
            small_ref[...] = jnp.zeros_like(small_ref)

        low = low_ref[...]
        z = _nn(low, wgk_ref[...]) + bgk_ref[...]
        upper_f = _chunk_masks()[1].astype(F32)
        dlg = jnp.concatenate([_nn_exact(upper_f, dcum_ref[r0:r0 + CHUNK, :]) for r0 in range(0, ts, CHUNK)],
                              axis=0)
        dz = dlg * (1.0 / GATE_NORM) * _sigmoid(-z)
        dz_bf = _bf(dz)
        ggk_ref[...] += _tn(low, dz_bf)
        small_ref[1:2, 0:KEY_W] += jnp.sum(dz, axis=0, keepdims=True)
        dlow = _bf(_nt(dz_bf, wgk_ref[...]))
        dlow_ref[...] = dlow
        dn1 = _nt(dlow, wlow_ref[...])
        for ref, lo, hi in ((dq_ref, 0, KEY_W), (dk_ref, KEY_W, 2 * KEY_W),
                            (dv_ref, 2 * KEY_W, 2 * KEY_W + D), (dgate_ref, 2 * KEY_W + D, GLA_MAIN)):
            piece = ref[...]
            dproj_ref[:, lo:hi] = piece
            dn1 = dn1 + _nt(piece, wgi_ref[:, lo:hi])
        hv = h1_ref[...]
        r = lax.rsqrt(jnp.mean(hv * hv, axis=-1, keepdims=True) + EPS)
        hhat = hv * r
        small_ref[0:1, :] += jnp.sum(dn1 * hhat, axis=0, keepdims=True)
        dxh = dn1 * w1_ref[...]
        dh1_ref[...] = dh2_ref[...] + r * (dxh - hhat * jnp.mean(dxh * hhat, axis=-1, keepdims=True))

    row = lambda cols: pl.BlockSpec((ts, cols), lambda i: (i, 0))
    return pl.pallas_call(
        body, name="gla_project_backward", grid=(s // ts,),
        out_shape=(jax.ShapeDtypeStruct((s, D), F32), jax.ShapeDtypeStruct((s, GLA_MAIN), BF16),
                   jax.ShapeDtypeStruct((s, RANK_PAD), BF16), jax.ShapeDtypeStruct((RANK_PAD, KEY_W), F32),
                   jax.ShapeDtypeStruct((8, D), F32)),
        in_specs=[row(KEY_W), row(KEY_W), row(D), row(D), row(KEY_W), row(RANK_PAD), row(D), row(D),
                  _full((1, D)), _full((D, GLA_MAIN)), _full((D, RANK_PAD)), _full((RANK_PAD, KEY_W)),
                  _full((1, KEY_W))],
        out_specs=(row(D), row(GLA_MAIN), row(RANK_PAD), _full((RANK_PAD, KEY_W)), _full((8, D))),
        compiler_params=_params("arbitrary"),
    )(dq, dk, dv, dgate, dcum, low, h1, dh2, w1, wgi, wlow, wgk, bgk)


def _groups_from_quarters(a):
    return a.reshape(N_CHIPS, GROUPS, 64, GROUP_DIM).transpose(1, 0, 2, 3).reshape(GROUPS, GROUP_DIM, GROUP_DIM)


def _quarters_from_groups(a):
    return a.reshape(GROUPS, N_CHIPS, 64, GROUP_DIM).transpose(1, 0, 2, 3).reshape(N_CHIPS, GROUP_DIM, GROUP_DIM)


def _gla_in_weights(wgi_q):
    wgi_all = jnp.concatenate([wgi_q[q] for q in range(N_CHIPS)], axis=1)
    wlow = jnp.pad(wgi_all[:, GLA_MAIN:], ((0, 0), (0, RANK_PAD - GATE_RANK)))
    return wgi_all, wlow


def local_gradients(xs, target, w0, w1, wf, wpi, gw, gb, scale, wpo, gla_quarters, wgk, bgk, hw_tiled, place):
    wgi_q, wgo_q = gla_quarters
    (h1, pooled, gt, n0), (wgi_q,) = pool_forward(xs, w0, wpi, gw, gb, scale, wpo, [wgi_q])
    wgi, wlow = _gla_in_weights(wgi_q)
    (qk, v, gate, low, cum, n1), (wgo_q,) = gla_project(h1, w1, wgi, wlow, wgk, bgk, [wgo_q])
    wgo = wgo_q.reshape(D, D)
    o, states, scores = gla_forward(qk, v, cum)

    dh2, do, dgate, g_gla_out, small_top = head_and_loss(o, gate, h1, target, hw_tiled, wgo, wf)
    dq, dk, dv, dcum = gla_backward(qk, v, cum, do, states, scores)
    dh1, dproj, dlow, g_gk_pad, small_gla = gla_project_backward(
        dq, dk, dv, dgate, dcum, low, h1, dh2, w1, wgi, wlow, wgk, bgk)
    g_gla_main, g_gla_low = matmul_tn(n1, dproj, dlow, "grad_gla_in")
    g_gla_in = jnp.concatenate([g_gla_main, g_gla_low[:, :GATE_RANK]], axis=1)

    def chip_sums(grads, tag):
        theirs = exchange_with_sibling(grads, "exchange_with_sibling_" + tag)
        return add_halves(grads, theirs, place, "add_halves_" + tag)

    gla_sums = chip_sums(
        [jnp.stack([g_gla_in[:, GLA_IN_QUARTER * q:GLA_IN_QUARTER * (q + 1)] for q in range(N_CHIPS)]),
         g_gla_out.reshape(N_CHIPS, D // N_CHIPS, D)], "gla")
    (dpool, g_pool_out, g_group_w, small_pool), gla_got = pool_backward(
        dh1, pooled, gt, gw, gb, scale, wpo, [b for _, b in gla_sums])
    dx, g_pool_in, small_in = pool_in_backward(xs, dh1, dpool, n0, w0, wpi)

    pool_sums = chip_sums(
        [g_pool_in, _quarters_from_groups(g_group_w), g_pool_out.reshape(N_CHIPS, D // N_CHIPS, D)], "pool")
    pool_got = scatter_to_owners([b for _, b in pool_sums], "scatter_to_owners_pool")
    reduced, total = join_halves(
        add_parts([f for f, _ in pool_sums + gla_sums], list(pool_got) + list(gla_got), place, "add_parts"),
        small_in, small_pool, small_gla, small_top, g_gk_pad)
    return dx, reduced, total


def kernel(x, norm_w, pool_in_w, pool_group_w, pool_group_b, pool_scale, pool_out_w, gla_in_w, gla_gk_w, gla_gk_b, gla_head_norm_w, gla_out_w, final_norm_w, loss_target, m_norm_w, m_pool_in_w, m_pool_group_w, m_pool_group_b, m_pool_scale, m_pool_out_w, m_gla_in_w, m_gla_gk_w, m_gla_gk_b, m_gla_head_norm_w, m_gla_out_w, m_final_norm_w, v_norm_w, v_pool_in_w, v_pool_group_w, v_pool_group_b, v_pool_scale, v_pool_out_w, v_gla_in_w, v_gla_gk_w, v_gla_gk_b, v_gla_head_norm_w, v_gla_out_w, v_final_norm_w):
    xs = x[0]
    target = loss_target[0]
    q_chip = 2 * lax.axis_index("x") + lax.axis_index("y")
    place = jnp.stack([lax.axis_index("c"), q_chip]).astype(jnp.int32)

    (wpi, gw_q, wpo_q, wgi_q, wgo_q), small_all = allgather_weights(
        [pool_in_w[0], pool_group_w[0].reshape(GROUP_DIM, GROUP_DIM), pool_out_w[0], gla_in_w[0], gla_out_w[0]],
        exchange=(True, True, True, False, False),
        smalls=[gla_gk_b, gla_head_norm_w, pool_group_b[0], gla_gk_w[0]])
    gw = _groups_from_quarters(gw_q)
    wpo = wpo_q.reshape(D, D)
    small_all = small_all[0::2]
    bgk = small_all[:, 0, :].reshape(1, KEY_W)
    hw = small_all[:, 1, 0:64].reshape(1, HEAD_V)
    gb = small_all[:, 2:2 + GROUPS, 0:64].transpose(1, 0, 2).reshape(1, D)
    wgk16 = small_all[:, 8:8 + GATE_RANK, :].transpose(1, 0, 2).reshape(GATE_RANK, KEY_W)
    wgk = _bf(jnp.pad(wgk16, ((0, RANK_PAD - GATE_RANK), (0, 0))))
    hw_tiled = jnp.tile(hw, (1, HEADS))

    w0 = norm_w[0:1]
    w1 = norm_w[1:2]
    wf = final_norm_w.reshape(1, D)

    dx, reduced, total = local_gradients(
        xs, target, w0, w1, wf, wpi, gw, gb, pool_scale, wpo, [wgi_q, wgo_q], wgk, bgk, hw_tiled, place)
    r_pool_in, r_group_w, r_pool_out, r_gla_in, r_gla_out = reduced
    r_group_w = r_group_w.reshape(GROUPS, 64, GROUP_DIM)

    loss = total[7, 0]
    g_norm = jnp.stack([total[0], total[3]])
    g_scale = total[1:2]
    g_final = total[5]
    pick = lambda full, width: lax.dynamic_slice_in_dim(full, q_chip * width, width, axis=-1)
    g_gk_b = pick(total[4:5, 0:KEY_W], 128)
    g_hnw = pick(total[6:7, 0:HEAD_V], 64)
    g_group_b = pick(total[2].reshape(GROUPS, GROUP_DIM), 64)[None]
    g_gk_w = pick(total[8:16].reshape(GATE_RANK, KEY_W), 128)[None]

    turn = lambda a: jnp.transpose(a, (2, 0, 1))
    back = lambda a: jnp.transpose(a, (1, 2, 0))
    as2d = lambda a, w: a.reshape(-1, w.shape[-1])
    big_names = ("pool_in_w", "pool_group_w", "pool_out_w", "gla_in_w", "gla_out_w")
    big_args = [(pool_in_w, r_pool_in[None], m_pool_in_w, v_pool_in_w),
                (pool_group_w, r_group_w[None], m_pool_group_w, v_pool_group_w),
                (pool_out_w, r_pool_out[None], m_pool_out_w, v_pool_out_w),
                (gla_in_w, r_gla_in[None], m_gla_in_w, v_gla_in_w),
                (gla_out_w, r_gla_out[None], m_gla_out_w, v_gla_out_w)]
    to_kernel = lambda n, a, w: turn(a) if n == "gla_in_w" else as2d(a, w)
    from_kernel = lambda n, a, w: back(a) if n == "gla_in_w" else a.reshape(w.shape)
    big_in = [tuple(to_kernel(n, a, p[0]) for a in p) for n, p in zip(big_names, big_args)]
    big_out = adamw(big_in, "adamw")
    big = {n: (from_kernel(n, i[1], p[0]),) + tuple(from_kernel(n, o, p[0]) for o in out)
           for n, p, i, out in zip(big_names, big_args, big_in, big_out)}

    small_names = ("norm_w", "pool_group_b", "pool_scale", "gla_gk_w", "gla_gk_b", "gla_head_norm_w",
                   "final_norm_w")
    small_args = [(norm_w, g_norm, m_norm_w, v_norm_w),
                  (pool_group_b, g_group_b, m_pool_group_b, v_pool_group_b),
                  (pool_scale, g_scale, m_pool_scale, v_pool_scale),
                  (gla_gk_w, g_gk_w, m_gla_gk_w, v_gla_gk_w),
                  (gla_gk_b, g_gk_b, m_gla_gk_b, v_gla_gk_b),
                  (gla_head_norm_w, g_hnw, m_gla_head_norm_w, v_gla_head_norm_w),
                  (final_norm_w, g_final, m_final_norm_w, v_final_norm_w)]
    small_out = adamw_small([tuple(as2d(a, p[0]) for a in p) for p in small_args])
    small = {n: (p[1].reshape(p[0].shape),) + tuple(o.reshape(p[0].shape) for o in out)
             for n, p, out in zip(small_names, small_args, small_out)}
    results = [
        small["norm_w"],
        big["pool_in_w"],
        big["pool_group_w"],
        small["pool_group_b"],
        small["pool_scale"],
        big["pool_out_w"],
        big["gla_in_w"],
        small["gla_gk_w"],
        small["gla_gk_b"],
        small["gla_head_norm_w"],
        big["gla_out_w"],
        small["final_norm_w"],
    ]
    grads, deltas, new_m, new_v = zip(*results)
    return (loss, dx[None], *grads, *deltas, *new_m, *new_v)
```

```python
import jax
import jax.numpy as jnp
from jax import lax
from jax.experimental import pallas as pl
from jax.experimental.pallas import tpu as pltpu

F32 = jnp.float32
BF16 = jnp.bfloat16
MESH = pl.DeviceIdType.MESH

D = 1024
POOL_WINDOWS = (2, 4, 8, 16)
GROUPS = 4
GROUP_DIM = 256
HEADS = 4
HEAD_K = 128
HEAD_V = 256
KEY_W = 512
CHUNK = 64
GATE_RANK = 16
GATE_NORM = 16.0
GLA_IN = 3088
GLA_MAIN = 3072
RANK_PAD = 128
EPS = 1e-6
HALO = 32

ADAM_LR = 0.001
ADAM_B1 = 0.9
ADAM_B2 = 0.999
ADAM_EPS = 1e-08
ADAM_WD = 0.01
ADAM_STEP = 10

N_CHIPS = 4
N_DEV = 8
GLA_IN_QUARTER = GLA_IN // N_CHIPS

VMEM_LIMIT = 56 * 1024 * 1024


def _nn(a, b):
    return lax.dot_general(a, b, (((1,), (0,)), ((), ())), preferred_element_type=F32)


def _nt(a, b):
    return lax.dot_general(a, b, (((1,), (1,)), ((), ())), preferred_element_type=F32)


def _tn(a, b):
    return lax.dot_general(a, b, (((0,), (0,)), ((), ())), preferred_element_type=F32)


def _nn_exact(a, b):
    return lax.dot_general(a, b, (((1,), (0,)), ((), ())), preferred_element_type=F32,
                           precision=lax.Precision.HIGHEST)


def _bf(a):
    return a.astype(BF16)


def _params(*sem):
    return pltpu.CompilerParams(dimension_semantics=sem, vmem_limit_bytes=VMEM_LIMIT)


def _full(shape):
    return pl.BlockSpec(shape, lambda i: (0,) * len(shape))


def _position():
    return lax.axis_index("x"), lax.axis_index("y"), lax.axis_index("c")


def _gather_small(in_ref, all_ref, send_sems, recv_sems, local_sem):
    x, y, c = _position()
    me = 4 * x + 2 * y + c
    mine = pltpu.make_async_copy(in_ref, all_ref.at[me], local_sem)
    mine.start()
    sends = []
    for k in range(N_DEV - 1):
        fx, fy, fc = (k + 1) >> 2 & 1, (k + 1) >> 1 & 1, (k + 1) & 1
        cp = pltpu.make_async_remote_copy(
            src_ref=in_ref, dst_ref=all_ref.at[me],
            send_sem=send_sems.at[k], recv_sem=recv_sems.at[k],
            device_id=(x ^ fx, y ^ fy, c ^ fc), device_id_type=MESH)
        cp.start()
        sends.append(cp)
    def wait():
        for k in range(N_DEV - 1):
            fx, fy, fc = (k + 1) >> 2 & 1, (k + 1) >> 1 & 1, (k + 1) & 1
            src_dev = 4 * (x ^ fx) + 2 * (y ^ fy) + (c ^ fc)
            pltpu.make_async_remote_copy(
                src_ref=in_ref, dst_ref=all_ref.at[src_dev],
                send_sem=send_sems.at[k], recv_sem=recv_sems.at[k],
                device_id=(x, y, c), device_id_type=MESH).wait_recv()
        for cp in sends:
            cp.wait_send()
        mine.wait()

    return wait


SMALL_SEMS = [pltpu.SemaphoreType.DMA((N_DEV - 1,)), pltpu.SemaphoreType.DMA((N_DEV - 1,)),
              pltpu.SemaphoreType.DMA]
VMEM_SPEC = pl.BlockSpec(memory_space=pltpu.VMEM)


def _other_chips(x, y):
    return [(1 - x, y), (x, 1 - y), (1 - x, 1 - y)]


def _any_specs(n):
    return [pl.BlockSpec(memory_space=pl.ANY)] * n


def _halves(rows, c):
    half = rows // 2
    return pl.ds(c * half, half), pl.ds((1 - c) * half, half)


CAST_ROWS = 256


def _gather_copy(out_ref, send_sems, recv_sems, k, quarter, half, to, src=None):
    dst = out_ref.at[quarter, half]
    return pltpu.make_async_remote_copy(
        src_ref=dst if src is None else src, dst_ref=dst,
        send_sem=send_sems.at[k], recv_sem=recv_sems.at[k], device_id=to, device_id_type=MESH)


SMALL_IN_ROWS = 24


def allgather_weights(quarters, exchange, smalls):
    n = len(quarters)
    shapes = [w.shape for w in quarters]
    moved = [i for i in range(n) if exchange[i]]

    def body(*refs):
        w_refs, (gkb_ref, hnw_ref, gb_ref, gkw_ref) = refs[:n], refs[n:n + 4]
        out_refs, small_all_ref = refs[n + 4:2 * n + 4], refs[2 * n + 4]
        refs = refs[2 * n + 5:]
        f32_bufs, bf_bufs = refs[:n], refs[n:2 * n]
        send_sems, recv_sems, local_sems, small_ref = refs[2 * n:2 * n + 4]
        small_ref[...] = jnp.zeros_like(small_ref)
        small_ref[0:1, :] = gkb_ref[...]
        small_ref[1:2, 0:64] = hnw_ref[...]
        small_ref[2:2 + GROUPS, 0:64] = gb_ref[...]
        small_ref[8:8 + GATE_RANK, :] = gkw_ref[...]
        wait_small = _gather_small(small_ref, small_all_ref, *refs[2 * n + 4:])
        x, y, c = _position()
        q = 2 * x + y
        sibling = (x, y, 1 - c)
        chips = _other_chips(x, y)

        def copy(k, i, quarter, half, to, src=None):
            return _gather_copy(out_refs[i], send_sems, recv_sems, k * n + i, quarter, half, to, src)

        loads = [pltpu.make_async_copy(w_refs[i], f32_bufs[i], local_sems.at[i]) for i in range(n)]
        for cp in loads:
            cp.start()
        keeps, sends = [], []
        for i in range(n):
            loads[i].wait()
            for r0 in range(0, shapes[i][0], CAST_ROWS):
                bf_bufs[i][r0:r0 + CAST_ROWS, :] = _bf(f32_bufs[i][r0:r0 + CAST_ROWS, :])
            keep = pltpu.make_async_copy(bf_bufs[i], out_refs[i].at[q], local_sems.at[n + i])
            keep.start()
            keeps.append(keep)
            if not exchange[i]:
                continue
            mine, _ = _halves(shapes[i][0], c)
            for j, chip in enumerate(chips):
                cp = copy(j, i, q, mine, (*chip, c), src=bf_bufs[i].at[mine])
                cp.start()
                sends.append(cp)
        for j, chip in enumerate(chips):
            qj = 2 * chip[0] + chip[1]
            for i in moved:
                mine, _ = _halves(shapes[i][0], c)
                copy(j, i, qj, mine, (x, y, c)).wait_recv()
                cp = copy(3 + j, i, qj, mine, sibling)
                cp.start()
                sends.append(cp)
        for j, chip in enumerate(chips):
            qj = 2 * chip[0] + chip[1]
            for i in moved:
                _, other = _halves(shapes[i][0], c)
                copy(3 + j, i, qj, other, (x, y, c)).wait_recv()
        wait_small()
        for cp in sends:
            cp.wait_send()
        for cp in keeps:
            cp.wait()

    outs = pl.pallas_call(
        body, name="allgather_weights",
        out_shape=[jax.ShapeDtypeStruct((N_CHIPS, *s), BF16) for s in shapes]
                  + [jax.ShapeDtypeStruct((N_DEV, SMALL_IN_ROWS, 128), F32)],
        in_specs=_any_specs(n) + [VMEM_SPEC] * 4, out_specs=_any_specs(n) + [VMEM_SPEC],
        scratch_shapes=([pltpu.VMEM(s, F32) for s in shapes] + [pltpu.VMEM(s, BF16) for s in shapes]
                        + [pltpu.SemaphoreType.DMA((6 * n,)), pltpu.SemaphoreType.DMA((6 * n,)),
                           pltpu.SemaphoreType.DMA((2 * n,)), pltpu.VMEM((SMALL_IN_ROWS, 128), F32)] + SMALL_SEMS),
        compiler_params=pltpu.CompilerParams(vmem_limit_bytes=VMEM_LIMIT),
    )(*quarters, *smalls)
    return outs[:n], outs[n]


def exchange_with_sibling(grads, name):
    n = len(grads)

    def body(*refs):
        g_refs, theirs_refs = refs[:n], refs[n:2 * n]
        send_sems, recv_sems = refs[2 * n:]
        x, y, c = _position()
        copies = []
        for i in range(n):
            _, other = _halves(g_refs[i].shape[1], c)
            cp = pltpu.make_async_remote_copy(
                src_ref=g_refs[i].at[:, other], dst_ref=theirs_refs[i],
                send_sem=send_sems.at[i], recv_sem=recv_sems.at[i],
                device_id=(x, y, 1 - c), device_id_type=MESH)
            cp.start()
            copies.append(cp)
        for cp in copies:
            cp.wait()

    return pl.pallas_call(
        body, name=name,
        out_shape=[jax.ShapeDtypeStruct((N_CHIPS, g.shape[1] // 2, g.shape[2]), F32) for g in grads],
        in_specs=_any_specs(n), out_specs=_any_specs(n),
        scratch_shapes=[pltpu.SemaphoreType.DMA((n,)), pltpu.SemaphoreType.DMA((n,))],
    )(*grads)


def _scatter_copies(b_refs, got_refs, send_sems, recv_sems):
    n = len(b_refs)
    x, y, c = _position()
    copies = []
    for j, chip in enumerate(_other_chips(x, y)):
        qj = 2 * chip[0] + chip[1]
        for i in range(n):
            copies.append(pltpu.make_async_remote_copy(
                src_ref=b_refs[i].at[qj], dst_ref=got_refs[i].at[j],
                send_sem=send_sems.at[j * n + i], recv_sem=recv_sems.at[j * n + i],
                device_id=(*chip, c), device_id_type=MESH))
    return copies


def _scatter_shapes(chip_sums):
    return [jax.ShapeDtypeStruct((N_CHIPS - 1, *b.shape[1:]), BF16) for b in chip_sums]


def scatter_to_owners(chip_sums, name):
    n = len(chip_sums)

    def body(*refs):
        copies = _scatter_copies(refs[:n], refs[n:2 * n], *refs[2 * n:])
        for cp in copies:
            cp.start()
        for cp in copies:
            cp.wait()

    return pl.pallas_call(
        body, name=name,
        out_shape=_scatter_shapes(chip_sums),
        in_specs=_any_specs(n), out_specs=_any_specs(n),
        scratch_shapes=[pltpu.SemaphoreType.DMA((3 * n,)), pltpu.SemaphoreType.DMA((3 * n,))],
    )(*chip_sums)


SMALL_SUM_ROWS = 16


def join_halves(reduced, small_pool, small_gla, small_top, g_gk_pad):
    n = len(reduced)

    def body(*refs):
        pool_ref, gla_ref, top_ref, gk_ref = refs[n:n + 4]
        buf_refs, total_ref = refs[n + 4:2 * n + 4], refs[2 * n + 4]
        send_sems, recv_sems, all_ref, small_ref = refs[2 * n + 5:2 * n + 9]
        small_ref[0:3, :] = pool_ref[0:3, :]
        small_ref[3:5, :] = gla_ref[0:2, :]
        small_ref[5:8, :] = top_ref[0:3, :]
        for r in range(GATE_RANK):
            small_ref[8 + r // 2:9 + r // 2, (r % 2) * KEY_W:(r % 2 + 1) * KEY_W] = gk_ref[r:r + 1, :]
        x, y, c = _position()
        copies = []
        for i in range(n):
            mine, _ = _halves(buf_refs[i].shape[0], c)
            cp = pltpu.make_async_remote_copy(
                src_ref=buf_refs[i].at[mine], dst_ref=buf_refs[i].at[mine],
                send_sem=send_sems.at[i], recv_sem=recv_sems.at[i],
                device_id=(x, y, 1 - c), device_id_type=MESH)
            cp.start()
            copies.append(cp)
        _gather_small(small_ref, all_ref, *refs[2 * n + 9:])()
        total = all_ref[0]
        for dev in range(1, N_DEV):
            total = total + all_ref[dev]
        total_ref[...] = total
        for cp in copies:
            cp.wait()

    outs = pl.pallas_call(
        body, name="join_halves",
        out_shape=[jax.ShapeDtypeStruct(r.shape, F32) for r in reduced]
                  + [jax.ShapeDtypeStruct((SMALL_SUM_ROWS, D), F32)],
        in_specs=_any_specs(n) + [VMEM_SPEC] * 4, out_specs=_any_specs(n) + [VMEM_SPEC],
        input_output_aliases={i: i for i in range(n)},
        scratch_shapes=[pltpu.SemaphoreType.DMA((n,)), pltpu.SemaphoreType.DMA((n,)),
                        pltpu.VMEM((N_DEV, SMALL_SUM_ROWS, D), F32), pltpu.VMEM((SMALL_SUM_ROWS, D), F32)]
                       + SMALL_SEMS,
    )(*reduced, small_pool, small_gla, small_top, g_gk_pad)
    return outs[:n], outs[n]


ADD_ROWS = 512
ADD_HALVES_ROWS = 256


def _spans(counts):
    starts, total = [], 0
    for count in counts:
        starts.append(total)
        total += count
    return starts, total


def _local_step(t, start, count):
    return jnp.clip(t - start, 0, count - 1)


def add_halves(grads, theirs, place, name):
    n = len(grads)
    shapes = [t.shape for t in theirs]
    rbs = [min(ADD_HALVES_ROWS, sh[1]) for sh in shapes]
    counts = [sh[1] // rb for sh, rb in zip(shapes, rbs)]
    starts, total = _spans(counts)

    def body(place_ref, *refs):
        a_refs, b_refs = refs[:n], refs[n:2 * n]
        f_refs, h_refs = refs[2 * n:3 * n], refs[3 * n:4 * n]
        t = pl.program_id(0)
        q = place_ref[1]
        for i in range(n):
            @pl.when((t >= starts[i]) & (t < starts[i] + counts[i]))
            def _(i=i):
                h_refs[i][...] = _bf(a_refs[i][...] + b_refs[i][...])
                f_refs[i][...] = a_refs[i][q] + b_refs[i][q]

    def specs(i):
        block = (N_CHIPS, rbs[i], shapes[i][2])
        step = lambda t: _local_step(t, starts[i], counts[i])
        mine = pl.BlockSpec(block, lambda t, place: (0, place[0] * counts[i] + step(t), 0))
        same = pl.BlockSpec(block, lambda t, place: (0, step(t), 0))
        own = pl.BlockSpec(block[1:], lambda t, place: (step(t), 0))
        return mine, same, own

    all_specs = [specs(i) for i in range(n)]
    outs = pl.pallas_call(
        body, name=name,
        grid_spec=pltpu.PrefetchScalarGridSpec(
            num_scalar_prefetch=1, grid=(total,),
            in_specs=[sp[0] for sp in all_specs] + [sp[1] for sp in all_specs],
            out_specs=[sp[2] for sp in all_specs] + [sp[1] for sp in all_specs]),
        out_shape=[jax.ShapeDtypeStruct(sh[1:], F32) for sh in shapes]
                  + [jax.ShapeDtypeStruct(sh, BF16) for sh in shapes],
        compiler_params=_params("arbitrary"),
    )(place, *grads, *theirs)
    return list(zip(outs[:n], outs[n:]))


def add_parts(owns, gots, place, name):
    n = len(owns)
    shapes = [g.shape for g in gots]
    rbs = [min(ADD_ROWS, sh[1]) for sh in shapes]
    counts = [sh[1] // rb for sh, rb in zip(shapes, rbs)]
    starts, total = _spans(counts)

    def body(place_ref, *refs):
        o_refs, g_refs, out_refs = refs[:n], refs[n:2 * n], refs[2 * n:]
        t = pl.program_id(0)
        for i in range(n):
            @pl.when((t >= starts[i]) & (t < starts[i] + counts[i]))
            def _(i=i):
                total_i = o_refs[i][...]
                for j in range(N_CHIPS - 1):
                    total_i = total_i + g_refs[i][j].astype(F32)
                out_refs[i][...] = total_i

    def specs(i):
        rb, cols = rbs[i], shapes[i][2]
        step = lambda t: _local_step(t, starts[i], counts[i])
        return (pl.BlockSpec((rb, cols), lambda t, place: (step(t), 0)),
                pl.BlockSpec((N_CHIPS - 1, rb, cols), lambda t, place: (0, step(t), 0)),
                pl.BlockSpec((rb, cols), lambda t, place: (place[0] * counts[i] + step(t), 0)))

    all_specs = [specs(i) for i in range(n)]
    return pl.pallas_call(
        body, name=name,
        grid_spec=pltpu.PrefetchScalarGridSpec(
            num_scalar_prefetch=1, grid=(total,),
            in_specs=[sp[0] for sp in all_specs] + [sp[1] for sp in all_specs],
            out_specs=[sp[2] for sp in all_specs]),
        out_shape=[jax.ShapeDtypeStruct((2 * sh[1], sh[2]), F32) for sh in shapes],
        compiler_params=_params("arbitrary"),
    )(place, *owns, *gots)


def _adam_math(w, g, m, v):
    m = ADAM_B1 * m + (1.0 - ADAM_B1) * g
    v = ADAM_B2 * v + (1.0 - ADAM_B2) * (g * g)
    m_hat = m / (1.0 - ADAM_B1 ** ADAM_STEP)
    v_hat = v / (1.0 - ADAM_B2 ** ADAM_STEP)
    delta = -ADAM_LR * (m_hat / (jnp.sqrt(v_hat) + ADAM_EPS) + ADAM_WD * w)
    return delta, m, v


ADAM_BLOCK_BYTES = 2 ** 19
ADAM_MOST_STEPS = 8


def adamw(params, name):
    n = len(params)
    shapes = [p[0].shape for p in params]

    def tile_rows(shape):
        rows, cols = shape[0], shape[-1]
        aligned = 1 if len(shape) == 3 else 8
        divisors = [t for t in range(aligned, rows + 1, aligned) if rows % t == 0]
        tile = max(t for t in divisors if t * cols * 4 <= ADAM_BLOCK_BYTES)
        if rows // tile > ADAM_MOST_STEPS:
            tile = min(t for t in divisors if rows // t <= ADAM_MOST_STEPS)
        return tile

    tiles = [tile_rows(sh) for sh in shapes]
    counts = [sh[0] // tl for sh, tl in zip(shapes, tiles)]
    starts, total = _spans(counts)

    def body(*refs):
        ins, outs = refs[:4 * n], refs[4 * n:]
        t = pl.program_id(0)
        for i in range(n):
            @pl.when((t >= starts[i]) & (t < starts[i] + counts[i]))
            def _(i=i):
                w_ref, g_ref, m_ref, v_ref = ins[4 * i:4 * i + 4]
                d, nm, nv = _adam_math(w_ref[...], g_ref[...], m_ref[...], v_ref[...])
                outs[3 * i][...] = d
                outs[3 * i + 1][...] = nm
                outs[3 * i + 2][...] = nv

    def spec(i):
        block = (tiles[i],) + shapes[i][1:]
        zeros = (0,) * (len(block) - 1)
        return pl.BlockSpec(block, lambda t: (_local_step(t, starts[i], counts[i]),) + zeros)

    outs = pl.pallas_call(
        body, name=name, grid=(total,),
        out_shape=[jax.ShapeDtypeStruct(sh, F32) for sh in shapes for _ in range(3)],
        in_specs=[spec(i) for i in range(n) for _ in range(4)],
        out_specs=[spec(i) for i in range(n) for _ in range(3)],
        compiler_params=_params("arbitrary"),
    )(*[a for p in params for a in p])
    return [tuple(outs[3 * i:3 * i + 3]) for i in range(n)]


def adamw_small(params):
    n = len(params)

    def body(*refs):
        ins, outs = refs[:4 * n], refs[4 * n:]
        for k in range(n):
            w_ref, g_ref, m_ref, v_ref = ins[4 * k:4 * k + 4]
            d, nm, nv = _adam_math(w_ref[...], g_ref[...], m_ref[...], v_ref[...])
            outs[3 * k][...] = d
            outs[3 * k + 1][...] = nm
            outs[3 * k + 2][...] = nv

    flat = [a for p in params for a in p]
    outs = pl.pallas_call(
        body, name="adamw_small",
        out_shape=[jax.ShapeDtypeStruct(p[0].shape, F32) for p in params for _ in range(3)],
        in_specs=[VMEM_SPEC] * (4 * n), out_specs=[VMEM_SPEC] * (3 * n),
    )(*flat)
    return [tuple(outs[3 * k:3 * k + 3]) for k in range(n)]


def matmul_tn(a, b, name, tile_n=512, by_column_tile=False, narrow=None):
    s, m = a.shape
    n = b.shape[1]
    tile_n = min(tile_n, n)
    if by_column_tile:
        out_shape = jax.ShapeDtypeStruct((n // tile_n, m, tile_n), F32)
        out_spec = pl.BlockSpec((None, m, tile_n), lambda j: (j, 0, 0))
    else:
        out_shape = jax.ShapeDtypeStruct((m, n), F32)
        out_spec = pl.BlockSpec((m, tile_n), lambda j: (0, j))

    if narrow is None:
        def body(a_ref, b_ref, out_ref):
            out_ref[...] = _tn(a_ref[...], b_ref[...])

        return pl.pallas_call(
            body, name=name, grid=(n // tile_n,),
            out_shape=out_shape,
            in_specs=[_full((s, m)), pl.BlockSpec((s, tile_n), lambda j: (0, j))],
            out_specs=out_spec,
            compiler_params=_params("parallel"),
        )(a, b)

    def body_with_narrow(a_ref, b_ref, c_ref, out_ref, out_c_ref):
        out_ref[...] = _tn(a_ref[...], b_ref[...])

        @pl.when(pl.program_id(0) == 0)
        def _():
            out_c_ref[...] = _tn(a_ref[...], c_ref[...])

    return pl.pallas_call(
        body_with_narrow, name=name, grid=(n // tile_n,),
        out_shape=(out_shape, jax.ShapeDtypeStruct((m, narrow.shape[1]), F32)),
        in_specs=[_full((s, m)), pl.BlockSpec((s, tile_n), lambda j: (0, j)), _full(narrow.shape)],
        out_specs=(out_spec, _full((m, narrow.shape[1]))),
        compiler_params=_params("arbitrary"),
    )(a, b, narrow)


ROW_TILE = 512


def _row_index(tile, rows):
    return tile * rows + lax.broadcasted_iota(jnp.int32, (rows, 1), 0)


def _inverse_counts(t_glob):
    return [1.0 / jnp.minimum(t_glob + 1, w).astype(F32) for w in POOL_WINDOWS]


def _sigmoid(z):
    return 1.0 / (1.0 + jnp.exp(-z))


def _trailing_sums(src, tmp, cols, window, rows):
    bufs = (src, tmp)
    span, level, start = 1, 0, 0
    while span < window:
        start += 8
        a, b = bufs[level % 2], bufs[(level + 1) % 2]
        n = HALO + rows - start
        b[start:start + n, cols] = a[start:start + n, cols] + a[start - span:start - span + n, cols]
        span, level = 2 * span, level + 1
    return bufs[level % 2][HALO:HALO + rows, cols]


def _leading_sums(src, tmp, cols, window, rows):
    bufs = (src, tmp)
    span, level, n = 1, 0, rows + HALO
    while span < window:
        n -= 8
        a, b = bufs[level % 2], bufs[(level + 1) % 2]
        b[0:n, cols] = a[0:n, cols] + a[span:span + n, cols]
        span, level = 2 * span, level + 1
    return bufs[level % 2][0:rows, cols]


def gather_in_background(step, last, out_refs, send_sems, recv_sems, finish):
    n = len(out_refs)
    x, y, c = _position()
    q = 2 * x + y
    chips = _other_chips(x, y)

    def copy(k, i, quarter, half, to):
        return _gather_copy(out_refs[i], send_sems, recv_sems, k * n + i, quarter, half, to)

    if not finish:
        @pl.when(step == 0)
        def _():
            for i in range(n):
                mine, _ = _halves(out_refs[i].shape[1], c)
                for j, chip in enumerate(chips):
                    copy(j, i, q, mine, (*chip, c)).start()

        @pl.when(step == last)
        def _():
            for j, chip in enumerate(chips):
                qj = 2 * chip[0] + chip[1]
                for i in range(n):
                    mine, _ = _halves(out_refs[i].shape[1], c)
                    copy(j, i, qj, mine, (x, y, c)).wait_recv()
                    copy(3 + j, i, qj, mine, (x, y, 1 - c)).start()
        return

    @pl.when(step == last)
    def _():
        for j, chip in enumerate(chips):
            qj = 2 * chip[0] + chip[1]
            for i in range(n):
                mine, other = _halves(out_refs[i].shape[1], c)
                copy(3 + j, i, qj, other, (x, y, c)).wait_recv()
                copy(j, i, q, mine, (x, y, c)).wait_send()
                copy(3 + j, i, qj, mine, (x, y, c)).wait_send()


def pool_forward(x, w0, wpi, gw, gb, scale, wpo, later):
    s = x.shape[0]
    ts = ROW_TILE
    nt = s // ts
    assert nt >= 2
    n_later = len(later)

    def body(x_ref, w0_ref, wpi_ref, gw_ref, gb_ref, sc_ref, wpo_ref, *rest):
        rest = rest[n_later:]
        h1_ref, pooled_ref, gt_ref, n0_ref = rest[:4]
        later_refs = rest[4:4 + n_later]
        ubuf, tbuf, hist, send_sems, recv_sems = rest[4 + n_later:]
        i = pl.program_id(0)
        gather_in_background(i, nt - 1, later_refs, send_sems, recv_sems, finish=False)
        xv = x_ref[...]
        r = lax.rsqrt(jnp.mean(xv * xv, axis=-1, keepdims=True) + EPS)
        n0 = _bf(xv * r * w0_ref[...])
        n0_ref[...] = n0
        u = jnp.concatenate([_nn(n0, wpi_ref[0]), _nn(n0, wpi_ref[1])], axis=-1)
        gt = jnp.concatenate([_nn(n0, wpi_ref[2]), _nn(n0, wpi_ref[3])], axis=-1)
        gt_ref[...] = gt

        @pl.when(i == 0)
        def _():
            hist[...] = jnp.zeros_like(hist)

        ubuf[0:HALO, :] = hist[...]
        ubuf[HALO:HALO + ts, :] = u
        hist[...] = u[ts - HALO:, :]
        inv = _inverse_counts(_row_index(i, ts))
        mixed = []
        for g, w in enumerate(POOL_WINDOWS):
            cols = slice(g * GROUP_DIM, (g + 1) * GROUP_DIM)
            pooled = _bf(_trailing_sums(ubuf, tbuf, cols, w, ts) * inv[g] - u[:, cols])
            pooled_ref[:, cols] = pooled
            mixed.append(_nn(pooled, gw_ref[g]))
        mixed = jnp.concatenate(mixed, axis=-1) + gb_ref[...]
        y = mixed * sc_ref[...] * (gt * _sigmoid(gt))
        h1_ref[...] = xv + _nn(_bf(y), wpo_ref[...])
        gather_in_background(i, nt - 1, later_refs, send_sems, recv_sems, finish=True)

    row = lambda cols: pl.BlockSpec((ts, cols), lambda i: (i, 0))
    outs = pl.pallas_call(
        body, name="pool_forward", grid=(nt,),
        out_shape=[jax.ShapeDtypeStruct((s, D), F32), jax.ShapeDtypeStruct((s, D), BF16),
                   jax.ShapeDtypeStruct((s, D), F32), jax.ShapeDtypeStruct((s, D), BF16)]
                  + [jax.ShapeDtypeStruct(a.shape, a.dtype) for a in later],
        in_specs=[row(D), _full((1, D)), _full((N_CHIPS, D, D // 2)), _full((GROUPS, GROUP_DIM, GROUP_DIM)),
                  _full((1, D)), _full((1, D)), _full((D, D))] + _any_specs(n_later),
        out_specs=[row(D), row(D), row(D), row(D)] + _any_specs(n_later),
        input_output_aliases={7 + k: 4 + k for k in range(n_later)},
        scratch_shapes=[pltpu.VMEM((HALO + ts, D), F32), pltpu.VMEM((HALO + ts, D), F32),
                        pltpu.VMEM((HALO, D), F32),
                        pltpu.SemaphoreType.DMA((6 * n_later,)), pltpu.SemaphoreType.DMA((6 * n_later,))],
        compiler_params=_params("arbitrary"),
    )(x, w0, wpi, gw, gb, scale, wpo, *later)
    return outs[:4], outs[4:]


def pool_backward(x, dh1, pooled, gt, w0, wpi, gw, gb, scale, wpo, chip_sums):
    s = x.shape[0]
    ts = ROW_TILE
    nt = s // ts
    n_sums = len(chip_sums)

    def body(x_ref, dh1_ref, pooled_ref, gt_ref, w0_ref, wpi_ref, gw_ref, gb_ref, sc_ref, wpo_ref, *rest):
        sum_refs, rest = rest[:n_sums], rest[n_sums:]
        dx_ref, dproj_ref, gpo_ref, ggw_ref, small_ref = rest[:5]
        got_refs = rest[5:5 + n_sums]
        ebuf, tbuf, ahead, send_sems, recv_sems = rest[5 + n_sums:]
        i = pl.program_id(0)
        copies = _scatter_copies(sum_refs, got_refs, send_sems, recv_sems)

        @pl.when(i == 0)
        def _():
            for cp in copies:
                cp.start()

        @pl.when(i == 0)
        def _():
            gpo_ref[...] = jnp.zeros_like(gpo_ref)
            ggw_ref[...] = jnp.zeros_like(ggw_ref)
            small_ref[...] = jnp.zeros_like(small_ref)
            ahead[...] = jnp.zeros_like(ahead)

        dh1 = dh1_ref[...]
        dh1_bf = _bf(dh1)
        gt = gt_ref[...]
        sc = sc_ref[...]
        dy = _nt(dh1_bf, wpo_ref[...])
        pooled_bf = []
        mixed = []
        for g in range(GROUPS):
            cols = slice(g * GROUP_DIM, (g + 1) * GROUP_DIM)
            pb = pooled_ref[:, cols]
            pooled_bf.append(pb)
            mixed.append(_nn(pb, gw_ref[g]))
        mixed = jnp.concatenate(mixed, axis=-1) + gb_ref[...]
        sg = _sigmoid(gt)
        silu = gt * sg
        gpo_ref[...] += _tn(_bf(mixed * sc * silu), dh1_bf)
        dmixed = dy * sc * silu
        dgt = dy * mixed * sc * (sg * (1.0 + gt * (1.0 - sg)))
        dproj_ref[:, D:] = _bf(dgt)
        small_ref[1:2, :] += jnp.sum(dy * mixed * silu, axis=0, keepdims=True)
        small_ref[2:3, :] += jnp.sum(dmixed, axis=0, keepdims=True)

        inv = _inverse_counts(_row_index(nt - 1 - i, ts))
        ebuf[ts:ts + HALO, :] = ahead[...]
        dpooled = []
        for g in range(GROUPS):
            cols = slice(g * GROUP_DIM, (g + 1) * GROUP_DIM)
            dm = _bf(dmixed[:, cols])
            ggw_ref[g] += _tn(pooled_bf[g], dm)
            dp = _nt(dm, gw_ref[g])
            dpooled.append(dp)
            ebuf[0:ts, cols] = dp * inv[g]
        ahead[...] = ebuf[0:HALO, :]
        du = []
        for g, w in enumerate(POOL_WINDOWS):
            cols = slice(g * GROUP_DIM, (g + 1) * GROUP_DIM)
            du.append(_leading_sums(ebuf, tbuf, cols, w, ts) - dpooled[g])
        du = _bf(jnp.concatenate(du, axis=-1))
        dproj_ref[:, :D] = du
        dgt_bf = _bf(dgt)
        half = D // 2
        dn0 = (_nt(du[:, :half], wpi_ref[0]) + _nt(du[:, half:], wpi_ref[1])
               + _nt(dgt_bf[:, :half], wpi_ref[2]) + _nt(dgt_bf[:, half:], wpi_ref[3]))

        xv = x_ref[...]
        r = lax.rsqrt(jnp.mean(xv * xv, axis=-1, keepdims=True) + EPS)
        xhat = xv * r
        small_ref[0:1, :] += jnp.sum(dn0 * xhat, axis=0, keepdims=True)
        dxh = dn0 * w0_ref[...]
        dx_ref[...] = dh1 + r * (dxh - xhat * jnp.mean(dxh * xhat, axis=-1, keepdims=True))

        @pl.when(i == nt - 1)
        def _():
            for cp in copies:
                cp.wait()

    row = lambda cols: pl.BlockSpec((ts, cols), lambda i: (nt - 1 - i, 0))
    outs = pl.pallas_call(
        body, name="pool_backward", grid=(nt,),
        out_shape=[jax.ShapeDtypeStruct((s, D), F32), jax.ShapeDtypeStruct((s, 2 * D), BF16),
                   jax.ShapeDtypeStruct((D, D), F32),
                   jax.ShapeDtypeStruct((GROUPS, GROUP_DIM, GROUP_DIM), F32),
                   jax.ShapeDtypeStruct((8, D), F32)] + _scatter_shapes(chip_sums),
        in_specs=[row(D), row(D), row(D), row(D), _full((1, D)), _full((N_CHIPS, D, D // 2)),
                  _full((GROUPS, GROUP_DIM, GROUP_DIM)), _full((1, D)), _full((1, D)), _full((D, D))]
                 + _any_specs(n_sums),
        out_specs=[row(D), row(2 * D), _full((D, D)), _full((GROUPS, GROUP_DIM, GROUP_DIM)), _full((8, D))]
                  + _any_specs(n_sums),
        scratch_shapes=[pltpu.VMEM((ts + HALO, D), F32), pltpu.VMEM((ts + HALO, D), F32),
                        pltpu.VMEM((HALO, D), F32),
                        pltpu.SemaphoreType.DMA((3 * n_sums,)), pltpu.SemaphoreType.DMA((3 * n_sums,))],
        compiler_params=_params("arbitrary"),
    )(x, dh1, pooled, gt, w0, wpi, gw, gb, scale, wpo, *chip_sums)
    return outs[:5], outs[5:]


def gla_project(h1, w1, wgi_q, wgk, bgk, later):
    s = h1.shape[0]
    ts = ROW_TILE
    nt = s // ts
    assert nt >= 2
    n_later = len(later)

    def body(h_ref, w1_ref, wq_ref, wgk_ref, bgk_ref, *rest):
        rest = rest[n_later:]
        qk_ref, v_ref, gate_ref, low_ref, cum_ref, n1_ref = rest[:6]
        later_refs = rest[6:6 + n_later]
        send_sems, recv_sems, wgi_ref = rest[6 + n_later:]
        gather_in_background(pl.program_id(0), nt - 1, later_refs, send_sems, recv_sems, finish=False)

        @pl.when(pl.program_id(0) == 0)
        def _():
            _assemble_gla_in(wq_ref, wgi_ref)

        hv = h_ref[...]
        r = lax.rsqrt(jnp.mean(hv * hv, axis=-1, keepdims=True) + EPS)
        n1 = _bf(hv * r * w1_ref[...])
        n1_ref[...] = n1
        qk_ref[...] = _nn(n1, wgi_ref[:, 0:2 * KEY_W])
        v_ref[...] = _bf(_nn(n1, wgi_ref[:, 2 * KEY_W:2 * KEY_W + D]))
        gate_ref[...] = _nn(n1, wgi_ref[:, 2 * KEY_W + D:GLA_MAIN])
        low = _bf(_nn(n1, wgi_ref[:, GLA_MAIN:]))
        low_ref[...] = low
        z = _nn(low, wgk_ref[...]) + bgk_ref[...]
        lg = (jnp.minimum(z, 0.0) - jnp.log(1.0 + jnp.exp(-jnp.abs(z)))) / GATE_NORM
        lower_f = _chunk_masks()[0].astype(F32)
        for r0 in range(0, ts, CHUNK):
            cum_ref[r0:r0 + CHUNK, :] = _nn_exact(lower_f, lg[r0:r0 + CHUNK, :])
        gather_in_background(pl.program_id(0), nt - 1, later_refs, send_sems, recv_sems, finish=True)

    row = lambda cols: pl.BlockSpec((ts, cols), lambda i: (i, 0))
    outs = pl.pallas_call(
        body, name="gla_project", grid=(nt,),
        out_shape=[jax.ShapeDtypeStruct((s, D), F32), jax.ShapeDtypeStruct((s, D), BF16),
                   jax.ShapeDtypeStruct((s, D), F32), jax.ShapeDtypeStruct((s, RANK_PAD), BF16),
                   jax.ShapeDtypeStruct((s, KEY_W), F32), jax.ShapeDtypeStruct((s, D), BF16)]
                  + [jax.ShapeDtypeStruct(a.shape, a.dtype) for a in later],
        in_specs=[row(D), _full((1, D)), _full((N_CHIPS, D, GLA_IN_QUARTER)),
                  _full((RANK_PAD, KEY_W)), _full((1, KEY_W))] + _any_specs(n_later),
        out_specs=[row(D), row(D), row(D), row(RANK_PAD), row(KEY_W), row(D)] + _any_specs(n_later),
        input_output_aliases={5 + k: 6 + k for k in range(n_later)},
        scratch_shapes=[pltpu.SemaphoreType.DMA((6 * n_later,)), pltpu.SemaphoreType.DMA((6 * n_later,)),
                        pltpu.VMEM((D, GLA_MAIN + RANK_PAD), BF16)],
        compiler_params=_params("arbitrary"),
    )(h1, w1, wgi_q, wgk, bgk, *later)
    return outs[:6], outs[6:]


def _assemble_gla_in(wq_ref, wfull):
    pad = jnp.zeros((CAST_ROWS, GLA_MAIN + RANK_PAD - GLA_IN), BF16)
    for r0 in range(0, D, CAST_ROWS):
        rows = slice(r0, r0 + CAST_ROWS)
        wfull[rows, :] = jnp.concatenate([wq_ref[q, rows, :] for q in range(N_CHIPS)] + [pad], axis=1)


GLA_BLOCK = 512
CHUNKS_PER_BLOCK = GLA_BLOCK // CHUNK


def _chunk_masks():
    t = lax.broadcasted_iota(jnp.int32, (CHUNK, CHUNK), 0)
    u = lax.broadcasted_iota(jnp.int32, (CHUNK, CHUNK), 1)
    return t >= u, t <= u


def _gla_chunk_terms(q, cum):
    ep = jnp.exp(cum)
    en = jnp.exp(-cum)
    qs = q * (HEAD_K ** -0.5)
    last = cum[CHUNK - 1:CHUNK, :]
    ed = jnp.exp(last - cum)
    dec = jnp.exp(last)
    return ep, en, qs, ed, dec


def gla_forward(qk, v, cum):
    s = qk.shape[0]
    nb = s // GLA_BLOCK
    nc = s // CHUNK

    def body(q_ref, k_ref, v_ref, cum_ref, o_ref, st_ref, sc_ref, state):
        @pl.when(pl.program_id(0) == 0)
        def _():
            state[...] = jnp.zeros_like(state)

        lower, _ = _chunk_masks()

        def chunk(cc, carry):
            rows = pl.ds(pl.multiple_of(cc * CHUNK, CHUNK), CHUNK)
            for h in range(HEADS):
                kc = slice(h * HEAD_K, (h + 1) * HEAD_K)
                vc = slice(h * HEAD_V, (h + 1) * HEAD_V)
                q = q_ref[rows, kc]
                k = k_ref[rows, kc]
                v = v_ref[rows, vc]
                ep, en, qs, ed, dec = _gla_chunk_terms(q, cum_ref[rows, kc])
                a = _bf(qs * ep)
                fwd = _nt(a, _bf(k * en))
                bwd = _nt(_bf(qs * en), _bf(k * ep))
                scores = _bf(jnp.where(lower, fwd, bwd))
                sc_ref[rows, h * CHUNK:(h + 1) * CHUNK] = scores
                st = state[h]
                st_ref[cc, h] = st
                o_ref[rows, vc] = _nn(scores, v) + _nt(a, _bf(st))
                state[h] = st * dec + _tn(v, _bf(k * ed))
            return carry

        lax.fori_loop(0, CHUNKS_PER_BLOCK, chunk, 0, unroll=4)

    return pl.pallas_call(
        body, name="gla_forward", grid=(nb,),
        out_shape=(jax.ShapeDtypeStruct((s, D), F32),
                   jax.ShapeDtypeStruct((nc, HEADS, HEAD_V, HEAD_K), F32),
                   jax.ShapeDtypeStruct((s, HEADS * CHUNK), BF16)),
        in_specs=[pl.BlockSpec((GLA_BLOCK, KEY_W), lambda i: (i, 0)),
                  pl.BlockSpec((GLA_BLOCK, KEY_W), lambda i: (i, 1)),
                  pl.BlockSpec((GLA_BLOCK, D), lambda i: (i, 0)),
                  pl.BlockSpec((GLA_BLOCK, KEY_W), lambda i: (i, 0))],
        out_specs=(pl.BlockSpec((GLA_BLOCK, D), lambda i: (i, 0)),
                   pl.BlockSpec((CHUNKS_PER_BLOCK, HEADS, HEAD_V, HEAD_K), lambda i: (i, 0, 0, 0)),
                   pl.BlockSpec((GLA_BLOCK, HEADS * CHUNK), lambda i: (i, 0))),
        scratch_shapes=[pltpu.VMEM((HEADS, HEAD_V, HEAD_K), F32)],
        compiler_params=_params("arbitrary"),
    )(qk, qk, v, cum)


def gla_backward(qk, v, cum, do, states, scores):
    s = qk.shape[0]
    nb = s // GLA_BLOCK

    def body(q_ref, k_ref, v_ref, cum_ref, do_ref, st_ref, sc_ref, dq_ref, dk_ref, dv_ref, dcum_ref, dstate):
        @pl.when(pl.program_id(0) == 0)
        def _():
            dstate[...] = jnp.zeros_like(dstate)

        lower, _ = _chunk_masks()
        is_last = lax.broadcasted_iota(jnp.int32, (CHUNK, HEAD_K), 0) == CHUNK - 1

        def chunk(step, carry):
            cc = CHUNKS_PER_BLOCK - 1 - step
            rows = pl.ds(pl.multiple_of(cc * CHUNK, CHUNK), CHUNK)
            for h in range(HEADS):
                kc = slice(h * HEAD_K, (h + 1) * HEAD_K)
                vc = slice(h * HEAD_V, (h + 1) * HEAD_V)
                q = q_ref[rows, kc]
                k = k_ref[rows, kc]
                v = v_ref[rows, vc]
                do_c = do_ref[rows, vc]
                ep, en, qs, ed, dec = _gla_chunk_terms(q, cum_ref[rows, kc])
                a = _bf(qs * ep)
                b = _bf(k * en)
                c = _bf(qs * en)
                dk_dec = _bf(k * ep)
                kd = _bf(k * ed)
                scores = sc_ref[rows, h * CHUNK:(h + 1) * CHUNK]
                st = st_ref[cc, h]
                dst = dstate[h]
                dst_bf = _bf(dst)

                dscores = _nt(do_c, v)
                dfwd = _bf(jnp.where(lower, dscores, 0.0))
                dbwd = _bf(jnp.where(lower, 0.0, dscores))
                dv_ref[rows, vc] = _bf(_tn(scores, do_c) + _nt(kd, dst_bf))
                da = _nn(dfwd, b) + _nn(do_c, _bf(st))
                db = _tn(dfwd, a)
                dc = _nn(dbwd, dk_dec)
                ddk = _tn(dbwd, c)
                dkd = _nn(v, dst_bf)
                ddec = jnp.sum(dst * st, axis=0, keepdims=True)
                dstate[h] = dst * dec + _tn(do_c, a)

                m = dkd * k * ed
                dq_ref[rows, kc] = _bf((da * ep + dc * en) * (HEAD_K ** -0.5))
                dk_ref[rows, kc] = _bf(db * en + ddk * ep + dkd * ed)
                dcum = (da * qs + ddk * k) * ep - (db * k + dc * qs) * en - m
                dlast = jnp.sum(m, axis=0, keepdims=True) + ddec * dec
                dcum_ref[rows, kc] = dcum + jnp.where(is_last, dlast, 0.0)
            return carry

        lax.fori_loop(0, CHUNKS_PER_BLOCK, chunk, 0, unroll=4)

    rev = lambda cols, col_block: pl.BlockSpec((GLA_BLOCK, cols), lambda i: (nb - 1 - i, col_block))
    return pl.pallas_call(
        body, name="gla_backward", grid=(nb,),
        out_shape=(jax.ShapeDtypeStruct((s, KEY_W), BF16), jax.ShapeDtypeStruct((s, KEY_W), BF16),
                   jax.ShapeDtypeStruct((s, D), BF16), jax.ShapeDtypeStruct((s, KEY_W), F32)),
        in_specs=[rev(KEY_W, 0), rev(KEY_W, 1), rev(D, 0), rev(KEY_W, 0), rev(D, 0),
                  pl.BlockSpec((CHUNKS_PER_BLOCK, HEADS, HEAD_V, HEAD_K), lambda i: (nb - 1 - i, 0, 0, 0)),
                  rev(HEADS * CHUNK, 0)],
        out_specs=(rev(KEY_W, 0), rev(KEY_W, 0), rev(D, 0), rev(KEY_W, 0)),
        scratch_shapes=[pltpu.VMEM((HEADS, HEAD_V, HEAD_K), F32)],
        compiler_params=_params("arbitrary"),
    )(qk, qk, v, cum, do, states, scores)


def head_and_loss(o, gate, h1, target, hw, wgo, wf):
    s = o.shape[0]
    ts = ROW_TILE

    def body(o_ref, gate_ref, h1_ref, tgt_ref, hw_ref, wgo_ref, wf_ref,
             dh2_ref, do_ref, dgate_ref, ggo_ref, small_ref):
        @pl.when(pl.program_id(0) == 0)
        def _():
            ggo_ref[...] = jnp.zeros_like(ggo_ref)
            small_ref[...] = jnp.zeros_like(small_ref)

        gate = gate_ref[...]
        hw = hw_ref[...]
        sg = _sigmoid(gate)
        silu = gate * sg
        ohat, ro = [], []
        for h in range(HEADS):
            oh = o_ref[:, h * HEAD_V:(h + 1) * HEAD_V]
            rh = lax.rsqrt(jnp.mean(oh * oh, axis=-1, keepdims=True) + EPS)
            ro.append(rh)
            ohat.append(oh * rh)
        ohat = jnp.concatenate(ohat, axis=-1)
        on = ohat * hw
        y2 = _bf(on * silu)
        h2 = h1_ref[...] + _nn(y2, wgo_ref[...])
        rf = lax.rsqrt(jnp.mean(h2 * h2, axis=-1, keepdims=True) + EPS)
        h2hat = h2 * rf
        wf = wf_ref[...]
        diff = h2hat * wf - tgt_ref[...]
        small_ref[2:3, :] += jnp.zeros((1, D), F32) + 0.5 * jnp.sum(diff * diff) / D
        dout = diff / D
        small_ref[0:1, :] += jnp.sum(dout * h2hat, axis=0, keepdims=True)
        dxh = dout * wf
        dh2 = rf * (dxh - h2hat * jnp.mean(dxh * h2hat, axis=-1, keepdims=True))
        dh2_ref[...] = dh2
        dh2_bf = _bf(dh2)
        ggo_ref[...] += _tn(y2, dh2_bf)
        dy2 = _nt(dh2_bf, wgo_ref[...])
        don = dy2 * silu
        dgate_ref[...] = _bf(dy2 * on * (sg * (1.0 + gate * (1.0 - sg))))
        ghw = jnp.sum(don * ohat, axis=0, keepdims=True)
        small_ref[1:2, 0:HEAD_V] += sum(ghw[:, h * HEAD_V:(h + 1) * HEAD_V] for h in range(HEADS))
        dohat = don * hw
        for h in range(HEADS):
            cols = slice(h * HEAD_V, (h + 1) * HEAD_V)
            oh, dh = ohat[:, cols], dohat[:, cols]
            do_ref[:, cols] = _bf(ro[h] * (dh - oh * jnp.mean(dh * oh, axis=-1, keepdims=True)))

    row = lambda cols: pl.BlockSpec((ts, cols), lambda i: (i, 0))
    act = jax.ShapeDtypeStruct((s, D), F32)
    act_bf = jax.ShapeDtypeStruct((s, D), BF16)
    return pl.pallas_call(
        body, name="head_and_loss", grid=(s // ts,),
        out_shape=(act, act_bf, act_bf, jax.ShapeDtypeStruct((D, D), F32), jax.ShapeDtypeStruct((8, D), F32)),
        in_specs=[row(D), row(D), row(D), row(D),
                  _full((1, D)), _full((D, D)), _full((1, D))],
        out_specs=(row(D), row(D), row(D), _full((D, D)), _full((8, D))),
        compiler_params=_params("arbitrary"),
    )(o, gate, h1, target, hw, wgo, wf)


def gla_project_backward(dq, dk, dv, dgate, dcum, low, h1, dh2, w1, wgi_q, wgk, bgk):
    s = h1.shape[0]
    ts = ROW_TILE

    def body(dq_ref, dk_ref, dv_ref, dgate_ref, dcum_ref, low_ref, h1_ref, dh2_ref, w1_ref,
             wq_ref, wgk_ref, bgk_ref, dh1_ref, dproj_ref, dlow_ref, ggk_ref, small_ref, wgi_ref):
        @pl.when(pl.program_id(0) == 0)
        def _():
            ggk_ref[...] = jnp.zeros_like(ggk_ref)
            small_ref[...] = jnp.zeros_like(small_ref)
            _assemble_gla_in(wq_ref, wgi_ref)

        low = low_ref[...]
        z = _nn(low, wgk_ref[...]) + bgk_ref[...]
        upper_f = _chunk_masks()[1].astype(F32)
        dlg = jnp.concatenate([_nn_exact(upper_f, dcum_ref[r0:r0 + CHUNK, :]) for r0 in range(0, ts, CHUNK)],
                              axis=0)
        dz = dlg * (1.0 / GATE_NORM) * _sigmoid(-z)
        dz_bf = _bf(dz)
        ggk_ref[...] += _tn(low, dz_bf)
        small_ref[1:2, 0:KEY_W] += jnp.sum(dz, axis=0, keepdims=True)
        dlow = _bf(_nt(dz_bf, wgk_ref[...]))
        dlow_ref[...] = dlow
        dn1 = _nt(dlow, wgi_ref[:, GLA_MAIN:])
        for ref, lo, hi in ((dq_ref, 0, KEY_W), (dk_ref, KEY_W, 2 * KEY_W),
                            (dv_ref, 2 * KEY_W, 2 * KEY_W + D), (dgate_ref, 2 * KEY_W + D, GLA_MAIN)):
            piece = ref[...]
            dproj_ref[:, lo:hi] = piece
            dn1 = dn1 + _nt(piece, wgi_ref[:, lo:hi])
        hv = h1_ref[...]
        r = lax.rsqrt(jnp.mean(hv * hv, axis=-1, keepdims=True) + EPS)
        hhat = hv * r
        small_ref[0:1, :] += jnp.sum(dn1 * hhat, axis=0, keepdims=True)
        dxh = dn1 * w1_ref[...]
        dh1_ref[...] = dh2_ref[...] + r * (dxh - hhat * jnp.mean(dxh * hhat, axis=-1, keepdims=True))

    row = lambda cols: pl.BlockSpec((ts, cols), lambda i: (i, 0))
    return pl.pallas_call(
        body, name="gla_project_backward", grid=(s // ts,),
        out_shape=(jax.ShapeDtypeStruct((s, D), F32), jax.ShapeDtypeStruct((s, GLA_MAIN), BF16),
                   jax.ShapeDtypeStruct((s, RANK_PAD), BF16), jax.ShapeDtypeStruct((RANK_PAD, KEY_W), F32),
                   jax.ShapeDtypeStruct((8, D), F32)),
        in_specs=[row(KEY_W), row(KEY_W), row(D), row(D), row(KEY_W), row(RANK_PAD), row(D), row(D),
                  _full((1, D)), _full((N_CHIPS, D, GLA_IN_QUARTER)), _full((RANK_PAD, KEY_W)),
                  _full((1, KEY_W))],
        out_specs=(row(D), row(GLA_MAIN), row(RANK_PAD), _full((RANK_PAD, KEY_W)), _full((8, D))),
        scratch_shapes=[pltpu.VMEM((D, GLA_MAIN + RANK_PAD), BF16)],
        compiler_params=_params("arbitrary"),
    )(dq, dk, dv, dgate, dcum, low, h1, dh2, w1, wgi_q, wgk, bgk)


def _groups_from_quarters(a):
    return a.reshape(N_CHIPS, GROUPS, 64, GROUP_DIM).transpose(1, 0, 2, 3).reshape(GROUPS, GROUP_DIM, GROUP_DIM)


def _quarters_from_groups(a):
    return a.reshape(GROUPS, N_CHIPS, 64, GROUP_DIM).transpose(1, 0, 2, 3).reshape(N_CHIPS, GROUP_DIM, GROUP_DIM)


def local_gradients(xs, target, w0, w1, wf, wpi, gw, gb, scale, wpo, gla_quarters, wgk, bgk, hw_tiled, place):
    wgi_q, wgo_q = gla_quarters
    (h1, pooled, gt, n0), (wgi_q,) = pool_forward(xs, w0, wpi, gw, gb, scale, wpo, [wgi_q])
    (qk, v, gate, low, cum, n1), (wgo_q,) = gla_project(h1, w1, wgi_q, wgk, bgk, [wgo_q])
    wgo = wgo_q.reshape(D, D)
    o, states, scores = gla_forward(qk, v, cum)

    dh2, do, dgate, g_gla_out, small_top = head_and_loss(o, gate, h1, target, hw_tiled, wgo, wf)
    dq, dk, dv, dcum = gla_backward(qk, v, cum, do, states, scores)
    dh1, dproj, dlow, g_gk_pad, small_gla = gla_project_backward(
        dq, dk, dv, dgate, dcum, low, h1, dh2, w1, wgi_q, wgk, bgk)
    g_gla_main, g_gla_low = matmul_tn(n1, dproj, "grad_gla_in", narrow=dlow)
    g_gla_in = jnp.concatenate([g_gla_main, g_gla_low[:, :GATE_RANK]], axis=1)

    def chip_sums(grads, tag):
        theirs = exchange_with_sibling(grads, "exchange_with_sibling_" + tag)
        return add_halves(grads, theirs, place, "add_halves_" + tag)

    gla_sums = chip_sums(
        [jnp.stack([g_gla_in[:, GLA_IN_QUARTER * q:GLA_IN_QUARTER * (q + 1)] for q in range(N_CHIPS)]),
         g_gla_out.reshape(N_CHIPS, D // N_CHIPS, D)], "gla")
    (dx, dpool, g_pool_out, g_group_w, small_pool), gla_got = pool_backward(
        xs, dh1, pooled, gt, w0, wpi, gw, gb, scale, wpo, [b for _, b in gla_sums])
    g_pool_in = matmul_tn(n0, dpool, "grad_pool_in", by_column_tile=True)

    pool_sums = chip_sums(
        [g_pool_in, _quarters_from_groups(g_group_w), g_pool_out.reshape(N_CHIPS, D // N_CHIPS, D)], "pool")
    pool_got = scatter_to_owners([b for _, b in pool_sums], "scatter_to_owners_pool")
    reduced, total = join_halves(
        add_parts([f for f, _ in pool_sums + gla_sums], list(pool_got) + list(gla_got), place, "add_parts"),
        small_pool, small_gla, small_top, g_gk_pad)
    return dx, reduced, total


def kernel(x, norm_w, pool_in_w, pool_group_w, pool_group_b, pool_scale, pool_out_w, gla_in_w, gla_gk_w, gla_gk_b, gla_head_norm_w, gla_out_w, final_norm_w, loss_target, m_norm_w, m_pool_in_w, m_pool_group_w, m_pool_group_b, m_pool_scale, m_pool_out_w, m_gla_in_w, m_gla_gk_w, m_gla_gk_b, m_gla_head_norm_w, m_gla_out_w, m_final_norm_w, v_norm_w, v_pool_in_w, v_pool_group_w, v_pool_group_b, v_pool_scale, v_pool_out_w, v_gla_in_w, v_gla_gk_w, v_gla_gk_b, v_gla_head_norm_w, v_gla_out_w, v_final_norm_w):
    xs = x[0]
    target = loss_target[0]
    q_chip = 2 * lax.axis_index("x") + lax.axis_index("y")
    place = jnp.stack([lax.axis_index("c"), q_chip]).astype(jnp.int32)

    (wpi, gw_q, wpo_q, wgi_q, wgo_q), small_all = allgather_weights(
        [pool_in_w[0], pool_group_w[0].reshape(GROUP_DIM, GROUP_DIM), pool_out_w[0], gla_in_w[0], gla_out_w[0]],
        exchange=(True, True, True, False, False),
        smalls=[gla_gk_b, gla_head_norm_w, pool_group_b[0], gla_gk_w[0]])
    gw = _groups_from_quarters(gw_q)
    wpo = wpo_q.reshape(D, D)
    small_all = small_all[0::2]
    bgk = small_all[:, 0, :].reshape(1, KEY_W)
    hw = small_all[:, 1, 0:64].reshape(1, HEAD_V)
    gb = small_all[:, 2:2 + GROUPS, 0:64].transpose(1, 0, 2).reshape(1, D)
    wgk16 = small_all[:, 8:8 + GATE_RANK, :].transpose(1, 0, 2).reshape(GATE_RANK, KEY_W)
    wgk = _bf(jnp.pad(wgk16, ((0, RANK_PAD - GATE_RANK), (0, 0))))
    hw_tiled = jnp.tile(hw, (1, HEADS))

    w0 = norm_w[0:1]
    w1 = norm_w[1:2]
    wf = final_norm_w.reshape(1, D)

    dx, reduced, total = local_gradients(
        xs, target, w0, w1, wf, wpi, gw, gb, pool_scale, wpo, [wgi_q, wgo_q], wgk, bgk, hw_tiled, place)
    r_pool_in, r_group_w, r_pool_out, r_gla_in, r_gla_out = reduced
    r_group_w = r_group_w.reshape(GROUPS, 64, GROUP_DIM)

    loss = total[7, 0]
    g_norm = jnp.stack([total[0], total[3]])
    g_scale = total[1:2]
    g_final = total[5]
    pick = lambda full, width: lax.dynamic_slice_in_dim(full, q_chip * width, width, axis=-1)
    g_gk_b = pick(total[4:5, 0:KEY_W], 128)
    g_hnw = pick(total[6:7, 0:HEAD_V], 64)
    g_group_b = pick(total[2].reshape(GROUPS, GROUP_DIM), 64)[None]
    g_gk_w = pick(total[8:16].reshape(GATE_RANK, KEY_W), 128)[None]

    turn = lambda a: jnp.transpose(a, (2, 0, 1))
    back = lambda a: jnp.transpose(a, (1, 2, 0))
    as2d = lambda a, w: a.reshape(-1, w.shape[-1])
    big_names = ("pool_in_w", "pool_group_w", "pool_out_w", "gla_in_w", "gla_out_w")
    big_args = [(pool_in_w, r_pool_in[None], m_pool_in_w, v_pool_in_w),
                (pool_group_w, r_group_w[None], m_pool_group_w, v_pool_group_w),
                (pool_out_w, r_pool_out[None], m_pool_out_w, v_pool_out_w),
                (gla_in_w, r_gla_in[None], m_gla_in_w, v_gla_in_w),
                (gla_out_w, r_gla_out[None], m_gla_out_w, v_gla_out_w)]
    to_kernel = lambda n, a, w: turn(a) if n == "gla_in_w" else as2d(a, w)
    from_kernel = lambda n, a, w: back(a) if n == "gla_in_w" else a.reshape(w.shape)
    big_in = [tuple(to_kernel(n, a, p[0]) for a in p) for n, p in zip(big_names, big_args)]
    big_out = adamw(big_in, "adamw")
    big = {n: (from_kernel(n, i[1], p[0]),) + tuple(from_kernel(n, o, p[0]) for o in out)
           for n, p, i, out in zip(big_names, big_args, big_in, big_out)}

    small_names = ("norm_w", "pool_group_b", "pool_scale", "gla_gk_w", "gla_gk_b", "gla_head_norm_w",
                   "final_norm_w")
    small_args = [(norm_w, g_norm, m_norm_w, v_norm_w),
                  (pool_group_b, g_group_b, m_pool_group_b, v_pool_group_b),
                  (pool_scale, g_scale, m_pool_scale, v_pool_scale),
                  (gla_gk_w, g_gk_w, m_gla_gk_w, v_gla_gk_w),
                  (gla_gk_b, g_gk_b, m_gla_gk_b, v_gla_gk_b),
                  (gla_head_norm_w, g_hnw, m_gla_head_norm_w, v_gla_head_norm_w),
                  (final_norm_w, g_final, m_final_norm_w, v_final_norm_w)]
    small_out = adamw_small([tuple(as2d(a, p[0]) for a in p) for p in small_args])
    small = {n: (p[1].reshape(p[0].shape),) + tuple(o.reshape(p[0].shape) for o in out)
             for n, p, out in zip(small_names, small_args, small_out)}
    results = [
        small["norm_w"],
        big["pool_in_w"],
        big["pool_group_w"],
        small["pool_group_b"],
        small["pool_scale"],
        big["pool_out_w"],
        big["gla_in_w"],
        small["gla_gk_w"],
        small["gla_gk_b"],
        small["gla_head_norm_w"],
        big["gla_out_w"],
        small["final_norm_w"],
    ]
    grads, deltas, new_m, new_v = zip(*results)
    return (loss, dx[None], *grads, *deltas, *new_m, *new_v)
```

```python
import jax
import jax.numpy as jnp
from jax import lax
from jax.experimental import pallas as pl
from jax.experimental.pallas import tpu as pltpu

F32 = jnp.float32
BF16 = jnp.bfloat16
MESH = pl.DeviceIdType.MESH

D = 1024
POOL_WINDOWS = (2, 4, 8, 16)
GROUPS = 4
GROUP_DIM = 256
HEADS = 4
HEAD_K = 128
HEAD_V = 256
KEY_W = 512
CHUNK = 64
GATE_RANK = 16
GATE_NORM = 16.0
GLA_IN = 3088
GLA_MAIN = 3072
RANK_PAD = 128
EPS = 1e-6
HALO = 32

ADAM_LR = 0.001
ADAM_B1 = 0.9
ADAM_B2 = 0.999
ADAM_EPS = 1e-08
ADAM_WD = 0.01
ADAM_STEP = 10

N_CHIPS = 4
N_DEV = 8
GLA_IN_QUARTER = GLA_IN // N_CHIPS

VMEM_LIMIT = 56 * 1024 * 1024


def _nn(a, b):
    return lax.dot_general(a, b, (((1,), (0,)), ((), ())), preferred_element_type=F32)


def _nt(a, b):
    return lax.dot_general(a, b, (((1,), (1,)), ((), ())), preferred_element_type=F32)


def _tn(a, b):
    return lax.dot_general(a, b, (((0,), (0,)), ((), ())), preferred_element_type=F32)


def _nn_exact(a, b):
    return lax.dot_general(a, b, (((1,), (0,)), ((), ())), preferred_element_type=F32,
                           precision=lax.Precision.HIGHEST)


def _bf(a):
    return a.astype(BF16)


def _params(*sem):
    return pltpu.CompilerParams(dimension_semantics=sem, vmem_limit_bytes=VMEM_LIMIT)


def _full(shape):
    return pl.BlockSpec(shape, lambda i: (0,) * len(shape))


def _position():
    return lax.axis_index("x"), lax.axis_index("y"), lax.axis_index("c")


def _gather_small(in_ref, all_ref, send_sems, recv_sems, local_sem):
    x, y, c = _position()
    me = 4 * x + 2 * y + c
    mine = pltpu.make_async_copy(in_ref, all_ref.at[me], local_sem)
    mine.start()
    sends = []
    for k in range(N_DEV - 1):
        fx, fy, fc = (k + 1) >> 2 & 1, (k + 1) >> 1 & 1, (k + 1) & 1
        cp = pltpu.make_async_remote_copy(
            src_ref=in_ref, dst_ref=all_ref.at[me],
            send_sem=send_sems.at[k], recv_sem=recv_sems.at[k],
            device_id=(x ^ fx, y ^ fy, c ^ fc), device_id_type=MESH)
        cp.start()
        sends.append(cp)
    def wait():
        for k in range(N_DEV - 1):
            fx, fy, fc = (k + 1) >> 2 & 1, (k + 1) >> 1 & 1, (k + 1) & 1
            src_dev = 4 * (x ^ fx) + 2 * (y ^ fy) + (c ^ fc)
            pltpu.make_async_remote_copy(
                src_ref=in_ref, dst_ref=all_ref.at[src_dev],
                send_sem=send_sems.at[k], recv_sem=recv_sems.at[k],
                device_id=(x, y, c), device_id_type=MESH).wait_recv()
        for cp in sends:
            cp.wait_send()
        mine.wait()

    return wait


SMALL_SEMS = [pltpu.SemaphoreType.DMA((N_DEV - 1,)), pltpu.SemaphoreType.DMA((N_DEV - 1,)),
              pltpu.SemaphoreType.DMA]
VMEM_SPEC = pl.BlockSpec(memory_space=pltpu.VMEM)


def _other_chips(x, y):
    return [(1 - x, y), (x, 1 - y), (1 - x, 1 - y)]


def _any_specs(n):
    return [pl.BlockSpec(memory_space=pl.ANY)] * n


def _halves(rows, c):
    half = rows // 2
    return pl.ds(c * half, half), pl.ds((1 - c) * half, half)


CAST_ROWS = 256


def _gather_copy(out_ref, send_sems, recv_sems, k, quarter, half, to, src=None):
    dst = out_ref.at[quarter, half]
    return pltpu.make_async_remote_copy(
        src_ref=dst if src is None else src, dst_ref=dst,
        send_sem=send_sems.at[k], recv_sem=recv_sems.at[k], device_id=to, device_id_type=MESH)


SMALL_IN_ROWS = 24


def allgather_weights(quarters, exchange, smalls):
    n = len(quarters)
    shapes = [w.shape for w in quarters]
    moved = [i for i in range(n) if exchange[i]]

    def body(*refs):
        w_refs, (gkb_ref, hnw_ref, gb_ref, gkw_ref) = refs[:n], refs[n:n + 4]
        out_refs, small_all_ref = refs[n + 4:2 * n + 4], refs[2 * n + 4]
        refs = refs[2 * n + 5:]
        f32_bufs, bf_bufs = refs[:n], refs[n:2 * n]
        send_sems, recv_sems, local_sems, small_ref = refs[2 * n:2 * n + 4]
        small_ref[...] = jnp.zeros_like(small_ref)
        small_ref[0:1, :] = gkb_ref[...]
        small_ref[1:2, 0:64] = hnw_ref[...]
        small_ref[2:2 + GROUPS, 0:64] = gb_ref[...]
        small_ref[8:8 + GATE_RANK, :] = gkw_ref[...]
        wait_small = _gather_small(small_ref, small_all_ref, *refs[2 * n + 4:])
        x, y, c = _position()
        q = 2 * x + y
        sibling = (x, y, 1 - c)
        chips = _other_chips(x, y)

        def copy(k, i, quarter, half, to, src=None):
            return _gather_copy(out_refs[i], send_sems, recv_sems, k * n + i, quarter, half, to, src)

        loads = [pltpu.make_async_copy(w_refs[i], f32_bufs[i], local_sems.at[i]) for i in range(n)]
        for cp in loads:
            cp.start()
        keeps, sends = [], []
        for i in range(n):
            loads[i].wait()
            for r0 in range(0, shapes[i][0], CAST_ROWS):
                bf_bufs[i][r0:r0 + CAST_ROWS, :] = _bf(f32_bufs[i][r0:r0 + CAST_ROWS, :])
            keep = pltpu.make_async_copy(bf_bufs[i], out_refs[i].at[q], local_sems.at[n + i])
            keep.start()
            keeps.append(keep)
            if not exchange[i]:
                continue
            mine, _ = _halves(shapes[i][0], c)
            for j, chip in enumerate(chips):
                cp = copy(j, i, q, mine, (*chip, c), src=bf_bufs[i].at[mine])
                cp.start()
                sends.append(cp)
        for j, chip in enumerate(chips):
            qj = 2 * chip[0] + chip[1]
            for i in moved:
                mine, _ = _halves(shapes[i][0], c)
                copy(j, i, qj, mine, (x, y, c)).wait_recv()
                cp = copy(3 + j, i, qj, mine, sibling)
                cp.start()
                sends.append(cp)
        for j, chip in enumerate(chips):
            qj = 2 * chip[0] + chip[1]
            for i in moved:
                _, other = _halves(shapes[i][0], c)
                copy(3 + j, i, qj, other, (x, y, c)).wait_recv()
        wait_small()
        for cp in sends:
            cp.wait_send()
        for cp in keeps:
            cp.wait()

    outs = pl.pallas_call(
        body, name="allgather_weights",
        out_shape=[jax.ShapeDtypeStruct((N_CHIPS, *s), BF16) for s in shapes]
                  + [jax.ShapeDtypeStruct((N_DEV, SMALL_IN_ROWS, 128), F32)],
        in_specs=_any_specs(n) + [VMEM_SPEC] * 4, out_specs=_any_specs(n) + [VMEM_SPEC],
        scratch_shapes=([pltpu.VMEM(s, F32) for s in shapes] + [pltpu.VMEM(s, BF16) for s in shapes]
                        + [pltpu.SemaphoreType.DMA((6 * n,)), pltpu.SemaphoreType.DMA((6 * n,)),
                           pltpu.SemaphoreType.DMA((2 * n,)), pltpu.VMEM((SMALL_IN_ROWS, 128), F32)] + SMALL_SEMS),
        compiler_params=pltpu.CompilerParams(vmem_limit_bytes=VMEM_LIMIT),
    )(*quarters, *smalls)
    return outs[:n], outs[n]


def exchange_with_sibling(grads, name):
    n = len(grads)
    half_shape = lambda g: (*g.shape[:-2], g.shape[-2] // 2, g.shape[-1])

    def body(*refs):
        g_refs, theirs_refs = refs[:n], refs[n:2 * n]
        send_sems, recv_sems = refs[2 * n:]
        x, y, c = _position()
        copies = []
        for i in range(n):
            _, other = _halves(g_refs[i].shape[-2], c)
            cp = pltpu.make_async_remote_copy(
                src_ref=g_refs[i].at[:, other] if len(g_refs[i].shape) == 3 else g_refs[i].at[other],
                dst_ref=theirs_refs[i],
                send_sem=send_sems.at[i], recv_sem=recv_sems.at[i],
                device_id=(x, y, 1 - c), device_id_type=MESH)
            cp.start()
            copies.append(cp)
        for cp in copies:
            cp.wait()

    return pl.pallas_call(
        body, name=name,
        out_shape=[jax.ShapeDtypeStruct(half_shape(g), F32) for g in grads],
        in_specs=_any_specs(n), out_specs=_any_specs(n),
        scratch_shapes=[pltpu.SemaphoreType.DMA((n,)), pltpu.SemaphoreType.DMA((n,))],
    )(*grads)


def _scatter_copies(b_refs, got_refs, send_sems, recv_sems):
    n = len(b_refs)
    x, y, c = _position()
    copies = []
    for j, chip in enumerate(_other_chips(x, y)):
        qj = 2 * chip[0] + chip[1]
        for i in range(n):
            copies.append(pltpu.make_async_remote_copy(
                src_ref=b_refs[i].at[qj], dst_ref=got_refs[i].at[j],
                send_sem=send_sems.at[j * n + i], recv_sem=recv_sems.at[j * n + i],
                device_id=(*chip, c), device_id_type=MESH))
    return copies


def _scatter_shapes(chip_sums):
    return [jax.ShapeDtypeStruct((N_CHIPS - 1, *b.shape[1:]), BF16) for b in chip_sums]


def scatter_to_owners(chip_sums, name):
    n = len(chip_sums)

    def body(*refs):
        copies = _scatter_copies(refs[:n], refs[n:2 * n], *refs[2 * n:])
        for cp in copies:
            cp.start()
        for cp in copies:
            cp.wait()

    return pl.pallas_call(
        body, name=name,
        out_shape=_scatter_shapes(chip_sums),
        in_specs=_any_specs(n), out_specs=_any_specs(n),
        scratch_shapes=[pltpu.SemaphoreType.DMA((3 * n,)), pltpu.SemaphoreType.DMA((3 * n,))],
    )(*chip_sums)


SMALL_SUM_ROWS = 16


def join_halves(reduced, small_pool, small_gla, small_top, g_gk_pad):
    n = len(reduced)

    def body(*refs):
        pool_ref, gla_ref, top_ref, gk_ref = refs[n:n + 4]
        buf_refs, total_ref = refs[n + 4:2 * n + 4], refs[2 * n + 4]
        send_sems, recv_sems, all_ref, small_ref = refs[2 * n + 5:2 * n + 9]
        small_ref[0:3, :] = pool_ref[0:3, :]
        small_ref[3:5, :] = gla_ref[0:2, :]
        small_ref[5:8, :] = top_ref[0:3, :]
        for r in range(GATE_RANK):
            small_ref[8 + r // 2:9 + r // 2, (r % 2) * KEY_W:(r % 2 + 1) * KEY_W] = gk_ref[r:r + 1, :]
        x, y, c = _position()
        copies = []
        for i in range(n):
            mine, _ = _halves(buf_refs[i].shape[0], c)
            cp = pltpu.make_async_remote_copy(
                src_ref=buf_refs[i].at[mine], dst_ref=buf_refs[i].at[mine],
                send_sem=send_sems.at[i], recv_sem=recv_sems.at[i],
                device_id=(x, y, 1 - c), device_id_type=MESH)
            cp.start()
            copies.append(cp)
        _gather_small(small_ref, all_ref, *refs[2 * n + 9:])()
        total = all_ref[0]
        for dev in range(1, N_DEV):
            total = total + all_ref[dev]
        total_ref[...] = total
        for cp in copies:
            cp.wait()

    outs = pl.pallas_call(
        body, name="join_halves",
        out_shape=[jax.ShapeDtypeStruct(r.shape, F32) for r in reduced]
                  + [jax.ShapeDtypeStruct((SMALL_SUM_ROWS, D), F32)],
        in_specs=_any_specs(n) + [VMEM_SPEC] * 4, out_specs=_any_specs(n) + [VMEM_SPEC],
        input_output_aliases={i: i for i in range(n)},
        scratch_shapes=[pltpu.SemaphoreType.DMA((n,)), pltpu.SemaphoreType.DMA((n,)),
                        pltpu.VMEM((N_DEV, SMALL_SUM_ROWS, D), F32), pltpu.VMEM((SMALL_SUM_ROWS, D), F32)]
                       + SMALL_SEMS,
    )(*reduced, small_pool, small_gla, small_top, g_gk_pad)
    return outs[:n], outs[n]


ADD_ROWS = 512
ADD_HALVES_ROWS = 256


def _spans(counts):
    starts, total = [], 0
    for count in counts:
        starts.append(total)
        total += count
    return starts, total


def _local_step(t, start, count):
    return jnp.clip(t - start, 0, count - 1)


def add_halves(grads, theirs, place, name):
    n = len(grads)
    whole = [len(t.shape) == 2 for t in theirs]
    halves = [t.shape[-2] for t in theirs]
    cols = [GLA_IN_QUARTER if w else t.shape[-1] for t, w in zip(theirs, whole)]
    rbs = [min(ADD_HALVES_ROWS, h) for h in halves]
    counts = [h // rb for h, rb in zip(halves, rbs)]
    starts, total = _spans(counts)

    def body(place_ref, *refs):
        a_refs, b_refs = refs[:n], refs[n:2 * n]
        f_refs, h_refs = refs[2 * n:3 * n], refs[3 * n:4 * n]
        t = pl.program_id(0)
        q = place_ref[1]
        for i in range(n):
            @pl.when((t >= starts[i]) & (t < starts[i] + counts[i]))
            def _(i=i):
                if not whole[i]:
                    h_refs[i][...] = _bf(a_refs[i][...] + b_refs[i][...])
                    f_refs[i][...] = a_refs[i][q] + b_refs[i][q]
                    return
                total_i = a_refs[i][...] + b_refs[i][...]
                for k in range(N_CHIPS):
                    piece = total_i[:, k * cols[i]:(k + 1) * cols[i]]
                    h_refs[i][k] = _bf(piece)

                    @pl.when(q == k)
                    def _():
                        f_refs[i][...] = piece

    def specs(i):
        step = lambda t: _local_step(t, starts[i], counts[i])
        by_quarter = (N_CHIPS, rbs[i], cols[i])
        block = (rbs[i], theirs[i].shape[-1]) if whole[i] else by_quarter
        lead = () if whole[i] else (0,)
        mine = pl.BlockSpec(block, lambda t, place: (*lead, place[0] * counts[i] + step(t), 0))
        same = pl.BlockSpec(block, lambda t, place: (*lead, step(t), 0))
        sums = pl.BlockSpec(by_quarter, lambda t, place: (0, step(t), 0))
        own = pl.BlockSpec(by_quarter[1:], lambda t, place: (step(t), 0))
        return mine, same, own, sums

    all_specs = [specs(i) for i in range(n)]
    outs = pl.pallas_call(
        body, name=name,
        grid_spec=pltpu.PrefetchScalarGridSpec(
            num_scalar_prefetch=1, grid=(total,),
            in_specs=[sp[0] for sp in all_specs] + [sp[1] for sp in all_specs],
            out_specs=[sp[2] for sp in all_specs] + [sp[3] for sp in all_specs]),
        out_shape=[jax.ShapeDtypeStruct((h, cl), F32) for h, cl in zip(halves, cols)]
                  + [jax.ShapeDtypeStruct((N_CHIPS, h, cl), BF16) for h, cl in zip(halves, cols)],
        compiler_params=_params("arbitrary"),
    )(place, *grads, *theirs)
    return list(zip(outs[:n], outs[n:]))


def add_parts(owns, gots, place, name):
    n = len(owns)
    shapes = [g.shape for g in gots]
    rbs = [min(ADD_ROWS, sh[1]) for sh in shapes]
    counts = [sh[1] // rb for sh, rb in zip(shapes, rbs)]
    starts, total = _spans(counts)

    def body(place_ref, *refs):
        o_refs, g_refs, out_refs = refs[:n], refs[n:2 * n], refs[2 * n:]
        t = pl.program_id(0)
        for i in range(n):
            @pl.when((t >= starts[i]) & (t < starts[i] + counts[i]))
            def _(i=i):
                total_i = o_refs[i][...]
                for j in range(N_CHIPS - 1):
                    total_i = total_i + g_refs[i][j].astype(F32)
                out_refs[i][...] = total_i

    def specs(i):
        rb, cols = rbs[i], shapes[i][2]
        step = lambda t: _local_step(t, starts[i], counts[i])
        return (pl.BlockSpec((rb, cols), lambda t, place: (step(t), 0)),
                pl.BlockSpec((N_CHIPS - 1, rb, cols), lambda t, place: (0, step(t), 0)),
                pl.BlockSpec((rb, cols), lambda t, place: (place[0] * counts[i] + step(t), 0)))

    all_specs = [specs(i) for i in range(n)]
    return pl.pallas_call(
        body, name=name,
        grid_spec=pltpu.PrefetchScalarGridSpec(
            num_scalar_prefetch=1, grid=(total,),
            in_specs=[sp[0] for sp in all_specs] + [sp[1] for sp in all_specs],
            out_specs=[sp[2] for sp in all_specs]),
        out_shape=[jax.ShapeDtypeStruct((2 * sh[1], sh[2]), F32) for sh in shapes],
        compiler_params=_params("arbitrary"),
    )(place, *owns, *gots)


def _adam_math(w, g, m, v):
    m = ADAM_B1 * m + (1.0 - ADAM_B1) * g
    v = ADAM_B2 * v + (1.0 - ADAM_B2) * (g * g)
    m_hat = m / (1.0 - ADAM_B1 ** ADAM_STEP)
    v_hat = v / (1.0 - ADAM_B2 ** ADAM_STEP)
    delta = -ADAM_LR * (m_hat / (jnp.sqrt(v_hat) + ADAM_EPS) + ADAM_WD * w)
    return delta, m, v


ADAM_BLOCK_BYTES = 2 ** 19
ADAM_MOST_STEPS = 8


def adamw(params, name):
    n = len(params)
    shapes = [p[0].shape for p in params]

    def tile_rows(shape):
        rows, cols = shape[0], shape[-1]
        aligned = 1 if len(shape) == 3 else 8
        divisors = [t for t in range(aligned, rows + 1, aligned) if rows % t == 0]
        tile = max(t for t in divisors if t * cols * 4 <= ADAM_BLOCK_BYTES)
        if rows // tile > ADAM_MOST_STEPS:
            tile = min(t for t in divisors if rows // t <= ADAM_MOST_STEPS)
        return tile

    tiles = [tile_rows(sh) for sh in shapes]
    counts = [sh[0] // tl for sh, tl in zip(shapes, tiles)]
    starts, total = _spans(counts)

    def body(*refs):
        ins, outs = refs[:4 * n], refs[4 * n:]
        t = pl.program_id(0)
        for i in range(n):
            @pl.when((t >= starts[i]) & (t < starts[i] + counts[i]))
            def _(i=i):
                w_ref, g_ref, m_ref, v_ref = ins[4 * i:4 * i + 4]
                d, nm, nv = _adam_math(w_ref[...], g_ref[...], m_ref[...], v_ref[...])
                outs[3 * i][...] = d
                outs[3 * i + 1][...] = nm
                outs[3 * i + 2][...] = nv

    def spec(i):
        block = (tiles[i],) + shapes[i][1:]
        zeros = (0,) * (len(block) - 1)
        return pl.BlockSpec(block, lambda t: (_local_step(t, starts[i], counts[i]),) + zeros)

    outs = pl.pallas_call(
        body, name=name, grid=(total,),
        out_shape=[jax.ShapeDtypeStruct(sh, F32) for sh in shapes for _ in range(3)],
        in_specs=[spec(i) for i in range(n) for _ in range(4)],
        out_specs=[spec(i) for i in range(n) for _ in range(3)],
        compiler_params=_params("arbitrary"),
    )(*[a for p in params for a in p])
    return [tuple(outs[3 * i:3 * i + 3]) for i in range(n)]


def adamw_small(params):
    n = len(params)

    def body(*refs):
        ins, outs = refs[:4 * n], refs[4 * n:]
        for k in range(n):
            w_ref, g_ref, m_ref, v_ref = ins[4 * k:4 * k + 4]
            d, nm, nv = _adam_math(w_ref[...], g_ref[...], m_ref[...], v_ref[...])
            outs[3 * k][...] = d
            outs[3 * k + 1][...] = nm
            outs[3 * k + 2][...] = nv

    flat = [a for p in params for a in p]
    outs = pl.pallas_call(
        body, name="adamw_small",
        out_shape=[jax.ShapeDtypeStruct(p[0].shape, F32) for p in params for _ in range(3)],
        in_specs=[VMEM_SPEC] * (4 * n), out_specs=[VMEM_SPEC] * (3 * n),
    )(*flat)
    return [tuple(outs[3 * k:3 * k + 3]) for k in range(n)]


def matmul_tn(a, b, name, tile_n, by_column_tile=False):
    s, m = a.shape
    n = b.shape[1]
    if by_column_tile:
        out_shape = jax.ShapeDtypeStruct((n // tile_n, m, tile_n), F32)
        out_spec = pl.BlockSpec((None, m, tile_n), lambda j: (j, 0, 0))
    else:
        out_shape = jax.ShapeDtypeStruct((m, n), F32)
        out_spec = pl.BlockSpec((m, tile_n), lambda j: (0, j))

    def body(a_ref, b_ref, out_ref):
        out_ref[...] = _tn(a_ref[...], b_ref[...])

    return pl.pallas_call(
        body, name=name, grid=(n // tile_n,),
        out_shape=out_shape,
        in_specs=[_full((s, m)), pl.BlockSpec((s, tile_n), lambda j: (0, j))],
        out_specs=out_spec,
        compiler_params=_params("parallel"),
    )(a, b)


ROW_TILE = 512


def _row_index(tile, rows):
    return tile * rows + lax.broadcasted_iota(jnp.int32, (rows, 1), 0)


def _inverse_counts(t_glob):
    return [1.0 / jnp.minimum(t_glob + 1, w).astype(F32) for w in POOL_WINDOWS]


def _sigmoid(z):
    return 1.0 / (1.0 + jnp.exp(-z))


def _trailing_sums(src, tmp, cols, window, rows):
    bufs = (src, tmp)
    span, level, start = 1, 0, 0
    while span < window:
        start += 8
        a, b = bufs[level % 2], bufs[(level + 1) % 2]
        n = HALO + rows - start
        b[start:start + n, cols] = a[start:start + n, cols] + a[start - span:start - span + n, cols]
        span, level = 2 * span, level + 1
    return bufs[level % 2][HALO:HALO + rows, cols]


def _leading_sums(src, tmp, cols, window, rows):
    bufs = (src, tmp)
    span, level, n = 1, 0, rows + HALO
    while span < window:
        n -= 8
        a, b = bufs[level % 2], bufs[(level + 1) % 2]
        b[0:n, cols] = a[0:n, cols] + a[span:span + n, cols]
        span, level = 2 * span, level + 1
    return bufs[level % 2][0:rows, cols]


def gather_in_background(step, last, out_refs, send_sems, recv_sems, finish):
    n = len(out_refs)
    x, y, c = _position()
    q = 2 * x + y
    chips = _other_chips(x, y)

    def copy(k, i, quarter, half, to):
        return _gather_copy(out_refs[i], send_sems, recv_sems, k * n + i, quarter, half, to)

    if not finish:
        @pl.when(step == 0)
        def _():
            for i in range(n):
                mine, _ = _halves(out_refs[i].shape[1], c)
                for j, chip in enumerate(chips):
                    copy(j, i, q, mine, (*chip, c)).start()

        @pl.when(step == last)
        def _():
            for j, chip in enumerate(chips):
                qj = 2 * chip[0] + chip[1]
                for i in range(n):
                    mine, _ = _halves(out_refs[i].shape[1], c)
                    copy(j, i, qj, mine, (x, y, c)).wait_recv()
                    copy(3 + j, i, qj, mine, (x, y, 1 - c)).start()
        return

    @pl.when(step == last)
    def _():
        for j, chip in enumerate(chips):
            qj = 2 * chip[0] + chip[1]
            for i in range(n):
                mine, other = _halves(out_refs[i].shape[1], c)
                copy(3 + j, i, qj, other, (x, y, c)).wait_recv()
                copy(j, i, q, mine, (x, y, c)).wait_send()
                copy(3 + j, i, qj, mine, (x, y, c)).wait_send()


def pool_forward(x, w0, wpi, gw, gb, scale, wpo, later):
    s = x.shape[0]
    ts = ROW_TILE
    nt = s // ts
    assert nt >= 2
    n_later = len(later)

    def body(x_ref, w0_ref, wpi_ref, gw_ref, gb_ref, sc_ref, wpo_ref, *rest):
        rest = rest[n_later:]
        h1_ref, pooled_ref, gt_ref, n0_ref = rest[:4]
        later_refs = rest[4:4 + n_later]
        ubuf, tbuf, hist, send_sems, recv_sems = rest[4 + n_later:]
        i = pl.program_id(0)
        gather_in_background(i, nt - 1, later_refs, send_sems, recv_sems, finish=False)
        xv = x_ref[...]
        r = lax.rsqrt(jnp.mean(xv * xv, axis=-1, keepdims=True) + EPS)
        n0 = _bf(xv * r * w0_ref[...])
        n0_ref[...] = n0
        u = jnp.concatenate([_nn(n0, wpi_ref[0]), _nn(n0, wpi_ref[1])], axis=-1)
        gt = jnp.concatenate([_nn(n0, wpi_ref[2]), _nn(n0, wpi_ref[3])], axis=-1)
        gt_ref[...] = gt

        @pl.when(i == 0)
        def _():
            hist[...] = jnp.zeros_like(hist)

        ubuf[0:HALO, :] = hist[...]
        ubuf[HALO:HALO + ts, :] = u
        hist[...] = u[ts - HALO:, :]
        inv = _inverse_counts(_row_index(i, ts))
        mixed = []
        for g, w in enumerate(POOL_WINDOWS):
            cols = slice(g * GROUP_DIM, (g + 1) * GROUP_DIM)
            pooled = _bf(_trailing_sums(ubuf, tbuf, cols, w, ts) * inv[g] - u[:, cols])
            pooled_ref[:, cols] = pooled
            mixed.append(_nn(pooled, gw_ref[g]))
        mixed = jnp.concatenate(mixed, axis=-1) + gb_ref[...]
        y = mixed * sc_ref[...] * (gt * _sigmoid(gt))
        h1_ref[...] = xv + _nn(_bf(y), wpo_ref[...])
        gather_in_background(i, nt - 1, later_refs, send_sems, recv_sems, finish=True)

    row = lambda cols: pl.BlockSpec((ts, cols), lambda i: (i, 0))
    outs = pl.pallas_call(
        body, name="pool_forward", grid=(nt,),
        out_shape=[jax.ShapeDtypeStruct((s, D), F32), jax.ShapeDtypeStruct((s, D), BF16),
                   jax.ShapeDtypeStruct((s, D), F32), jax.ShapeDtypeStruct((s, D), BF16)]
                  + [jax.ShapeDtypeStruct(a.shape, a.dtype) for a in later],
        in_specs=[row(D), _full((1, D)), _full((N_CHIPS, D, D // 2)), _full((GROUPS, GROUP_DIM, GROUP_DIM)),
                  _full((1, D)), _full((1, D)), _full((D, D))] + _any_specs(n_later),
        out_specs=[row(D), row(D), row(D), row(D)] + _any_specs(n_later),
        input_output_aliases={7 + k: 4 + k for k in range(n_later)},
        scratch_shapes=[pltpu.VMEM((HALO + ts, D), F32), pltpu.VMEM((HALO + ts, D), F32),
                        pltpu.VMEM((HALO, D), F32),
                        pltpu.SemaphoreType.DMA((6 * n_later,)), pltpu.SemaphoreType.DMA((6 * n_later,))],
        compiler_params=_params("arbitrary"),
    )(x, w0, wpi, gw, gb, scale, wpo, *later)
    return outs[:4], outs[4:]


def pool_backward(x, dh1, pooled, gt, w0, wpi, gw, gb, scale, wpo, chip_sums):
    s = x.shape[0]
    ts = ROW_TILE
    nt = s // ts
    n_sums = len(chip_sums)

    def body(x_ref, dh1_ref, pooled_ref, gt_ref, w0_ref, wpi_ref, gw_ref, gb_ref, sc_ref, wpo_ref, *rest):
        sum_refs, rest = rest[:n_sums], rest[n_sums:]
        dx_ref, dproj_ref, gpo_ref, ggw_ref, small_ref = rest[:5]
        got_refs = rest[5:5 + n_sums]
        ebuf, tbuf, ahead, send_sems, recv_sems = rest[5 + n_sums:]
        i = pl.program_id(0)
        copies = _scatter_copies(sum_refs, got_refs, send_sems, recv_sems)

        @pl.when(i == 0)
        def _():
            for cp in copies:
                cp.start()

        @pl.when(i == 0)
        def _():
            gpo_ref[...] = jnp.zeros_like(gpo_ref)
            ggw_ref[...] = jnp.zeros_like(ggw_ref)
            small_ref[...] = jnp.zeros_like(small_ref)
            ahead[...] = jnp.zeros_like(ahead)

        dh1 = dh1_ref[...]
        dh1_bf = _bf(dh1)
        gt = gt_ref[...]
        sc = sc_ref[...]
        dy = _nt(dh1_bf, wpo_ref[...])
        pooled_bf = []
        mixed = []
        for g in range(GROUPS):
            cols = slice(g * GROUP_DIM, (g + 1) * GROUP_DIM)
            pb = pooled_ref[:, cols]
            pooled_bf.append(pb)
            mixed.append(_nn(pb, gw_ref[g]))
        mixed = jnp.concatenate(mixed, axis=-1) + gb_ref[...]
        sg = _sigmoid(gt)
        silu = gt * sg
        gpo_ref[...] += _tn(_bf(mixed * sc * silu), dh1_bf)
        dmixed = dy * sc * silu
        dgt = dy * mixed * sc * (sg * (1.0 + gt * (1.0 - sg)))
        dproj_ref[:, D:] = _bf(dgt)
        small_ref[1:2, :] += jnp.sum(dy * mixed * silu, axis=0, keepdims=True)
        small_ref[2:3, :] += jnp.sum(dmixed, axis=0, keepdims=True)

        inv = _inverse_counts(_row_index(nt - 1 - i, ts))
        ebuf[ts:ts + HALO, :] = ahead[...]
        dpooled = []
        for g in range(GROUPS):
            cols = slice(g * GROUP_DIM, (g + 1) * GROUP_DIM)
            dm = _bf(dmixed[:, cols])
            ggw_ref[g] += _tn(pooled_bf[g], dm)
            dp = _nt(dm, gw_ref[g])
            dpooled.append(dp)
            ebuf[0:ts, cols] = dp * inv[g]
        ahead[...] = ebuf[0:HALO, :]
        du = []
        for g, w in enumerate(POOL_WINDOWS):
            cols = slice(g * GROUP_DIM, (g + 1) * GROUP_DIM)
            du.append(_leading_sums(ebuf, tbuf, cols, w, ts) - dpooled[g])
        du = _bf(jnp.concatenate(du, axis=-1))
        dproj_ref[:, :D] = du
        dgt_bf = _bf(dgt)
        half = D // 2
        dn0 = (_nt(du[:, :half], wpi_ref[0]) + _nt(du[:, half:], wpi_ref[1])
               + _nt(dgt_bf[:, :half], wpi_ref[2]) + _nt(dgt_bf[:, half:], wpi_ref[3]))

        xv = x_ref[...]
        r = lax.rsqrt(jnp.mean(xv * xv, axis=-1, keepdims=True) + EPS)
        xhat = xv * r
        small_ref[0:1, :] += jnp.sum(dn0 * xhat, axis=0, keepdims=True)
        dxh = dn0 * w0_ref[...]
        dx_ref[...] = dh1 + r * (dxh - xhat * jnp.mean(dxh * xhat, axis=-1, keepdims=True))

        @pl.when(i == nt - 1)
        def _():
            for cp in copies:
                cp.wait()

    row = lambda cols: pl.BlockSpec((ts, cols), lambda i: (nt - 1 - i, 0))
    outs = pl.pallas_call(
        body, name="pool_backward", grid=(nt,),
        out_shape=[jax.ShapeDtypeStruct((s, D), F32), jax.ShapeDtypeStruct((s, 2 * D), BF16),
                   jax.ShapeDtypeStruct((D, D), F32),
                   jax.ShapeDtypeStruct((GROUPS, GROUP_DIM, GROUP_DIM), F32),
                   jax.ShapeDtypeStruct((8, D), F32)] + _scatter_shapes(chip_sums),
        in_specs=[row(D), row(D), row(D), row(D), _full((1, D)), _full((N_CHIPS, D, D // 2)),
                  _full((GROUPS, GROUP_DIM, GROUP_DIM)), _full((1, D)), _full((1, D)), _full((D, D))]
                 + _any_specs(n_sums),
        out_specs=[row(D), row(2 * D), _full((D, D)), _full((GROUPS, GROUP_DIM, GROUP_DIM)), _full((8, D))]
                  + _any_specs(n_sums),
        scratch_shapes=[pltpu.VMEM((ts + HALO, D), F32), pltpu.VMEM((ts + HALO, D), F32),
                        pltpu.VMEM((HALO, D), F32),
                        pltpu.SemaphoreType.DMA((3 * n_sums,)), pltpu.SemaphoreType.DMA((3 * n_sums,))],
        compiler_params=_params("arbitrary"),
    )(x, dh1, pooled, gt, w0, wpi, gw, gb, scale, wpo, *chip_sums)
    return outs[:5], outs[5:]


def gla_project(h1, w1, wgi_q, wgk, bgk, later):
    s = h1.shape[0]
    ts = ROW_TILE
    nt = s // ts
    assert nt >= 2
    n_later = len(later)

    def body(h_ref, w1_ref, wq_ref, wgk_ref, bgk_ref, *rest):
        rest = rest[n_later:]
        qk_ref, v_ref, gate_ref, low_ref, cum_ref, n1_ref = rest[:6]
        later_refs = rest[6:6 + n_later]
        send_sems, recv_sems, wgi_ref = rest[6 + n_later:]
        gather_in_background(pl.program_id(0), nt - 1, later_refs, send_sems, recv_sems, finish=False)

        @pl.when(pl.program_id(0) == 0)
        def _():
            _assemble_gla_in(wq_ref, wgi_ref)

        hv = h_ref[...]
        r = lax.rsqrt(jnp.mean(hv * hv, axis=-1, keepdims=True) + EPS)
        n1 = _bf(hv * r * w1_ref[...])
        n1_ref[...] = n1
        qk_ref[...] = _nn(n1, wgi_ref[:, 0:2 * KEY_W])
        v_ref[...] = _bf(_nn(n1, wgi_ref[:, 2 * KEY_W:2 * KEY_W + D]))
        gate_ref[...] = _nn(n1, wgi_ref[:, 2 * KEY_W + D:GLA_MAIN])
        low = _bf(_nn(n1, wgi_ref[:, GLA_MAIN:]))
        low_ref[...] = low
        z = _nn(low, wgk_ref[...]) + bgk_ref[...]
        lg = (jnp.minimum(z, 0.0) - jnp.log(1.0 + jnp.exp(-jnp.abs(z)))) / GATE_NORM
        lower_f = _chunk_masks()[0].astype(F32)
        for r0 in range(0, ts, CHUNK):
            cum_ref[r0:r0 + CHUNK, :] = _nn_exact(lower_f, lg[r0:r0 + CHUNK, :])
        gather_in_background(pl.program_id(0), nt - 1, later_refs, send_sems, recv_sems, finish=True)

    row = lambda cols: pl.BlockSpec((ts, cols), lambda i: (i, 0))
    outs = pl.pallas_call(
        body, name="gla_project", grid=(nt,),
        out_shape=[jax.ShapeDtypeStruct((s, D), F32), jax.ShapeDtypeStruct((s, D), BF16),
                   jax.ShapeDtypeStruct((s, D), F32), jax.ShapeDtypeStruct((s, RANK_PAD), BF16),
                   jax.ShapeDtypeStruct((s, KEY_W), F32), jax.ShapeDtypeStruct((s, D), BF16)]
                  + [jax.ShapeDtypeStruct(a.shape, a.dtype) for a in later],
        in_specs=[row(D), _full((1, D)), _full((N_CHIPS, D, GLA_IN_QUARTER)),
                  _full((RANK_PAD, KEY_W)), _full((1, KEY_W))] + _any_specs(n_later),
        out_specs=[row(D), row(D), row(D), row(RANK_PAD), row(KEY_W), row(D)] + _any_specs(n_later),
        input_output_aliases={5 + k: 6 + k for k in range(n_later)},
        scratch_shapes=[pltpu.SemaphoreType.DMA((6 * n_later,)), pltpu.SemaphoreType.DMA((6 * n_later,)),
                        pltpu.VMEM((D, GLA_MAIN + RANK_PAD), BF16)],
        compiler_params=_params("arbitrary"),
    )(h1, w1, wgi_q, wgk, bgk, *later)
    return outs[:6], outs[6:]


def _assemble_gla_in(wq_ref, wfull):
    pad = jnp.zeros((CAST_ROWS, GLA_MAIN + RANK_PAD - GLA_IN), BF16)
    for r0 in range(0, D, CAST_ROWS):
        rows = slice(r0, r0 + CAST_ROWS)
        wfull[rows, :] = jnp.concatenate([wq_ref[q, rows, :] for q in range(N_CHIPS)] + [pad], axis=1)


GLA_BLOCK = 512
CHUNKS_PER_BLOCK = GLA_BLOCK // CHUNK


def _chunk_masks():
    t = lax.broadcasted_iota(jnp.int32, (CHUNK, CHUNK), 0)
    u = lax.broadcasted_iota(jnp.int32, (CHUNK, CHUNK), 1)
    return t >= u, t <= u


def _gla_chunk_terms(q, cum):
    ep = jnp.exp(cum)
    en = jnp.exp(-cum)
    qs = q * (HEAD_K ** -0.5)
    last = cum[CHUNK - 1:CHUNK, :]
    ed = jnp.exp(last - cum)
    dec = jnp.exp(last)
    return ep, en, qs, ed, dec


def gla_forward(qk, v, cum):
    s = qk.shape[0]
    nb = s // GLA_BLOCK
    nc = s // CHUNK

    def body(q_ref, k_ref, v_ref, cum_ref, o_ref, st_ref, sc_ref, state):
        @pl.when(pl.program_id(0) == 0)
        def _():
            state[...] = jnp.zeros_like(state)

        lower, _ = _chunk_masks()

        def chunk(cc, carry):
            rows = pl.ds(pl.multiple_of(cc * CHUNK, CHUNK), CHUNK)
            for h in range(HEADS):
                kc = slice(h * HEAD_K, (h + 1) * HEAD_K)
                vc = slice(h * HEAD_V, (h + 1) * HEAD_V)
                q = q_ref[rows, kc]
                k = k_ref[rows, kc]
                v = v_ref[rows, vc]
                ep, en, qs, ed, dec = _gla_chunk_terms(q, cum_ref[rows, kc])
                a = _bf(qs * ep)
                fwd = _nt(a, _bf(k * en))
                bwd = _nt(_bf(qs * en), _bf(k * ep))
                scores = _bf(jnp.where(lower, fwd, bwd))
                sc_ref[rows, h * CHUNK:(h + 1) * CHUNK] = scores
                st = state[h]
                st_ref[cc, h] = st
                o_ref[rows, vc] = _nn(scores, v) + _nt(a, _bf(st))
                state[h] = st * dec + _tn(v, _bf(k * ed))
            return carry

        lax.fori_loop(0, CHUNKS_PER_BLOCK, chunk, 0, unroll=4)

    return pl.pallas_call(
        body, name="gla_forward", grid=(nb,),
        out_shape=(jax.ShapeDtypeStruct((s, D), F32),
                   jax.ShapeDtypeStruct((nc, HEADS, HEAD_V, HEAD_K), F32),
                   jax.ShapeDtypeStruct((s, HEADS * CHUNK), BF16)),
        in_specs=[pl.BlockSpec((GLA_BLOCK, KEY_W), lambda i: (i, 0)),
                  pl.BlockSpec((GLA_BLOCK, KEY_W), lambda i: (i, 1)),
                  pl.BlockSpec((GLA_BLOCK, D), lambda i: (i, 0)),
                  pl.BlockSpec((GLA_BLOCK, KEY_W), lambda i: (i, 0))],
        out_specs=(pl.BlockSpec((GLA_BLOCK, D), lambda i: (i, 0)),
                   pl.BlockSpec((CHUNKS_PER_BLOCK, HEADS, HEAD_V, HEAD_K), lambda i: (i, 0, 0, 0)),
                   pl.BlockSpec((GLA_BLOCK, HEADS * CHUNK), lambda i: (i, 0))),
        scratch_shapes=[pltpu.VMEM((HEADS, HEAD_V, HEAD_K), F32)],
        compiler_params=_params("arbitrary"),
    )(qk, qk, v, cum)


def gla_backward(qk, v, cum, do, states, scores):
    s = qk.shape[0]
    nb = s // GLA_BLOCK

    def body(q_ref, k_ref, v_ref, cum_ref, do_ref, st_ref, sc_ref, dq_ref, dk_ref, dv_ref, dcum_ref, dstate):
        @pl.when(pl.program_id(0) == 0)
        def _():
            dstate[...] = jnp.zeros_like(dstate)

        lower, _ = _chunk_masks()
        is_last = lax.broadcasted_iota(jnp.int32, (CHUNK, HEAD_K), 0) == CHUNK - 1

        def chunk(step, carry):
            cc = CHUNKS_PER_BLOCK - 1 - step
            rows = pl.ds(pl.multiple_of(cc * CHUNK, CHUNK), CHUNK)
            for h in range(HEADS):
                kc = slice(h * HEAD_K, (h + 1) * HEAD_K)
                vc = slice(h * HEAD_V, (h + 1) * HEAD_V)
                q = q_ref[rows, kc]
                k = k_ref[rows, kc]
                v = v_ref[rows, vc]
                do_c = do_ref[rows, vc]
                ep, en, qs, ed, dec = _gla_chunk_terms(q, cum_ref[rows, kc])
                a = _bf(qs * ep)
                b = _bf(k * en)
                c = _bf(qs * en)
                dk_dec = _bf(k * ep)
                kd = _bf(k * ed)
                scores = sc_ref[rows, h * CHUNK:(h + 1) * CHUNK]
                st = st_ref[cc, h]
                dst = dstate[h]
                dst_bf = _bf(dst)

                dscores = _nt(do_c, v)
                dfwd = _bf(jnp.where(lower, dscores, 0.0))
                dbwd = _bf(jnp.where(lower, 0.0, dscores))
                dv_ref[rows, vc] = _bf(_tn(scores, do_c) + _nt(kd, dst_bf))
                da = _nn(dfwd, b) + _nn(do_c, _bf(st))
                db = _tn(dfwd, a)
                dc = _nn(dbwd, dk_dec)
                ddk = _tn(dbwd, c)
                dkd = _nn(v, dst_bf)
                ddec = jnp.sum(dst * st, axis=0, keepdims=True)
                dstate[h] = dst * dec + _tn(do_c, a)

                m = dkd * k * ed
                dq_ref[rows, kc] = _bf((da * ep + dc * en) * (HEAD_K ** -0.5))
                dk_ref[rows, kc] = _bf(db * en + ddk * ep + dkd * ed)
                dcum = (da * qs + ddk * k) * ep - (db * k + dc * qs) * en - m
                dlast = jnp.sum(m, axis=0, keepdims=True) + ddec * dec
                dcum_ref[rows, kc] = dcum + jnp.where(is_last, dlast, 0.0)
            return carry

        lax.fori_loop(0, CHUNKS_PER_BLOCK, chunk, 0, unroll=4)

    rev = lambda cols, col_block: pl.BlockSpec((GLA_BLOCK, cols), lambda i: (nb - 1 - i, col_block))
    return pl.pallas_call(
        body, name="gla_backward", grid=(nb,),
        out_shape=(jax.ShapeDtypeStruct((s, KEY_W), BF16), jax.ShapeDtypeStruct((s, KEY_W), BF16),
                   jax.ShapeDtypeStruct((s, D), BF16), jax.ShapeDtypeStruct((s, KEY_W), F32)),
        in_specs=[rev(KEY_W, 0), rev(KEY_W, 1), rev(D, 0), rev(KEY_W, 0), rev(D, 0),
                  pl.BlockSpec((CHUNKS_PER_BLOCK, HEADS, HEAD_V, HEAD_K), lambda i: (nb - 1 - i, 0, 0, 0)),
                  rev(HEADS * CHUNK, 0)],
        out_specs=(rev(KEY_W, 0), rev(KEY_W, 0), rev(D, 0), rev(KEY_W, 0)),
        scratch_shapes=[pltpu.VMEM((HEADS, HEAD_V, HEAD_K), F32)],
        compiler_params=_params("arbitrary"),
    )(qk, qk, v, cum, do, states, scores)


def head_and_loss(o, gate, h1, target, hw, wgo, wf):
    s = o.shape[0]
    ts = ROW_TILE

    def body(o_ref, gate_ref, h1_ref, tgt_ref, hw_ref, wgo_ref, wf_ref,
             dh2_ref, do_ref, dgate_ref, ggo_ref, small_ref):
        @pl.when(pl.program_id(0) == 0)
        def _():
            ggo_ref[...] = jnp.zeros_like(ggo_ref)
            small_ref[...] = jnp.zeros_like(small_ref)

        gate = gate_ref[...]
        hw = hw_ref[...]
        sg = _sigmoid(gate)
        silu = gate * sg
        ohat, ro = [], []
        for h in range(HEADS):
            oh = o_ref[:, h * HEAD_V:(h + 1) * HEAD_V]
            rh = lax.rsqrt(jnp.mean(oh * oh, axis=-1, keepdims=True) + EPS)
            ro.append(rh)
            ohat.append(oh * rh)
        ohat = jnp.concatenate(ohat, axis=-1)
        on = ohat * hw
        y2 = _bf(on * silu)
        h2 = h1_ref[...] + _nn(y2, wgo_ref[...])
        rf = lax.rsqrt(jnp.mean(h2 * h2, axis=-1, keepdims=True) + EPS)
        h2hat = h2 * rf
        wf = wf_ref[...]
        diff = h2hat * wf - tgt_ref[...]
        small_ref[2:3, :] += jnp.zeros((1, D), F32) + 0.5 * jnp.sum(diff * diff) / D
        dout = diff / D
        small_ref[0:1, :] += jnp.sum(dout * h2hat, axis=0, keepdims=True)
        dxh = dout * wf
        dh2 = rf * (dxh - h2hat * jnp.mean(dxh * h2hat, axis=-1, keepdims=True))
        dh2_ref[...] = dh2
        dh2_bf = _bf(dh2)
        ggo_ref[...] += _tn(y2, dh2_bf)
        dy2 = _nt(dh2_bf, wgo_ref[...])
        don = dy2 * silu
        dgate_ref[...] = _bf(dy2 * on * (sg * (1.0 + gate * (1.0 - sg))))
        ghw = jnp.sum(don * ohat, axis=0, keepdims=True)
        small_ref[1:2, 0:HEAD_V] += sum(ghw[:, h * HEAD_V:(h + 1) * HEAD_V] for h in range(HEADS))
        dohat = don * hw
        for h in range(HEADS):
            cols = slice(h * HEAD_V, (h + 1) * HEAD_V)
            oh, dh = ohat[:, cols], dohat[:, cols]
            do_ref[:, cols] = _bf(ro[h] * (dh - oh * jnp.mean(dh * oh, axis=-1, keepdims=True)))

    row = lambda cols: pl.BlockSpec((ts, cols), lambda i: (i, 0))
    act = jax.ShapeDtypeStruct((s, D), F32)
    act_bf = jax.ShapeDtypeStruct((s, D), BF16)
    return pl.pallas_call(
        body, name="head_and_loss", grid=(s // ts,),
        out_shape=(act, act_bf, act_bf, jax.ShapeDtypeStruct((D, D), F32), jax.ShapeDtypeStruct((8, D), F32)),
        in_specs=[row(D), row(D), row(D), row(D),
                  _full((1, D)), _full((D, D)), _full((1, D))],
        out_specs=(row(D), row(D), row(D), _full((D, D)), _full((8, D))),
        compiler_params=_params("arbitrary"),
    )(o, gate, h1, target, hw, wgo, wf)


def gla_project_backward(dq, dk, dv, dgate, dcum, low, h1, dh2, w1, wgi_q, wgk, bgk):
    s = h1.shape[0]
    ts = ROW_TILE

    def body(dq_ref, dk_ref, dv_ref, dgate_ref, dcum_ref, low_ref, h1_ref, dh2_ref, w1_ref,
             wq_ref, wgk_ref, bgk_ref, dh1_ref, dproj_ref, ggk_ref, small_ref, wgi_ref):
        @pl.when(pl.program_id(0) == 0)
        def _():
            ggk_ref[...] = jnp.zeros_like(ggk_ref)
            small_ref[...] = jnp.zeros_like(small_ref)
            _assemble_gla_in(wq_ref, wgi_ref)

        low = low_ref[...]
        z = _nn(low, wgk_ref[...]) + bgk_ref[...]
        upper_f = _chunk_masks()[1].astype(F32)
        dlg = jnp.concatenate([_nn_exact(upper_f, dcum_ref[r0:r0 + CHUNK, :]) for r0 in range(0, ts, CHUNK)],
                              axis=0)
        dz = dlg * (1.0 / GATE_NORM) * _sigmoid(-z)
        dz_bf = _bf(dz)
        ggk_ref[...] += _tn(low, dz_bf)
        small_ref[1:2, 0:KEY_W] += jnp.sum(dz, axis=0, keepdims=True)
        dlow = _bf(_nt(dz_bf, wgk_ref[...]))
        dproj_ref[:, GLA_MAIN:] = dlow
        dn1 = _nt(dlow, wgi_ref[:, GLA_MAIN:])
        for ref, lo, hi in ((dq_ref, 0, KEY_W), (dk_ref, KEY_W, 2 * KEY_W),
                            (dv_ref, 2 * KEY_W, 2 * KEY_W + D), (dgate_ref, 2 * KEY_W + D, GLA_MAIN)):
            piece = ref[...]
            dproj_ref[:, lo:hi] = piece
            dn1 = dn1 + _nt(piece, wgi_ref[:, lo:hi])
        hv = h1_ref[...]
        r = lax.rsqrt(jnp.mean(hv * hv, axis=-1, keepdims=True) + EPS)
        hhat = hv * r
        small_ref[0:1, :] += jnp.sum(dn1 * hhat, axis=0, keepdims=True)
        dxh = dn1 * w1_ref[...]
        dh1_ref[...] = dh2_ref[...] + r * (dxh - hhat * jnp.mean(dxh * hhat, axis=-1, keepdims=True))

    row = lambda cols: pl.BlockSpec((ts, cols), lambda i: (i, 0))
    return pl.pallas_call(
        body, name="gla_project_backward", grid=(s // ts,),
        out_shape=(jax.ShapeDtypeStruct((s, D), F32), jax.ShapeDtypeStruct((s, GLA_MAIN + RANK_PAD), BF16),
                   jax.ShapeDtypeStruct((RANK_PAD, KEY_W), F32),
                   jax.ShapeDtypeStruct((8, D), F32)),
        in_specs=[row(KEY_W), row(KEY_W), row(D), row(D), row(KEY_W), row(RANK_PAD), row(D), row(D),
                  _full((1, D)), _full((N_CHIPS, D, GLA_IN_QUARTER)), _full((RANK_PAD, KEY_W)),
                  _full((1, KEY_W))],
        out_specs=(row(D), row(GLA_MAIN + RANK_PAD), _full((RANK_PAD, KEY_W)), _full((8, D))),
        scratch_shapes=[pltpu.VMEM((D, GLA_MAIN + RANK_PAD), BF16)],
        compiler_params=_params("arbitrary"),
    )(dq, dk, dv, dgate, dcum, low, h1, dh2, w1, wgi_q, wgk, bgk)


def _groups_from_quarters(a):
    return a.reshape(N_CHIPS, GROUPS, 64, GROUP_DIM).transpose(1, 0, 2, 3).reshape(GROUPS, GROUP_DIM, GROUP_DIM)


def _quarters_from_groups(a):
    return a.reshape(GROUPS, N_CHIPS, 64, GROUP_DIM).transpose(1, 0, 2, 3).reshape(N_CHIPS, GROUP_DIM, GROUP_DIM)


def local_gradients(xs, target, w0, w1, wf, wpi, gw, gb, scale, wpo, gla_quarters, wgk, bgk, hw_tiled, place):
    wgi_q, wgo_q = gla_quarters
    (h1, pooled, gt, n0), (wgi_q,) = pool_forward(xs, w0, wpi, gw, gb, scale, wpo, [wgi_q])
    (qk, v, gate, low, cum, n1), (wgo_q,) = gla_project(h1, w1, wgi_q, wgk, bgk, [wgo_q])
    wgo = wgo_q.reshape(D, D)
    o, states, scores = gla_forward(qk, v, cum)

    dh2, do, dgate, g_gla_out, small_top = head_and_loss(o, gate, h1, target, hw_tiled, wgo, wf)
    dq, dk, dv, dcum = gla_backward(qk, v, cum, do, states, scores)
    dh1, dproj, g_gk_pad, small_gla = gla_project_backward(
        dq, dk, dv, dgate, dcum, low, h1, dh2, w1, wgi_q, wgk, bgk)
    g_gla_in = matmul_tn(n1, dproj, "grad_gla_in", tile_n=(GLA_MAIN + RANK_PAD) // 5)

    def chip_sums(grads, tag):
        theirs = exchange_with_sibling(grads, "exchange_with_sibling_" + tag)
        return add_halves(grads, theirs, place, "add_halves_" + tag)

    gla_sums = chip_sums([g_gla_in, g_gla_out.reshape(N_CHIPS, D // N_CHIPS, D)], "gla")
    (dx, dpool, g_pool_out, g_group_w, small_pool), gla_got = pool_backward(
        xs, dh1, pooled, gt, w0, wpi, gw, gb, scale, wpo, [b for _, b in gla_sums])
    g_pool_in = matmul_tn(n0, dpool, "grad_pool_in", tile_n=D // 2, by_column_tile=True)

    pool_sums = chip_sums(
        [g_pool_in, _quarters_from_groups(g_group_w), g_pool_out.reshape(N_CHIPS, D // N_CHIPS, D)], "pool")
    pool_got = scatter_to_owners([b for _, b in pool_sums], "scatter_to_owners_pool")
    reduced, total = join_halves(
        add_parts([f for f, _ in pool_sums + gla_sums], list(pool_got) + list(gla_got), place, "add_parts"),
        small_pool, small_gla, small_top, g_gk_pad)
    return dx, reduced, total


def kernel(x, norm_w, pool_in_w, pool_group_w, pool_group_b, pool_scale, pool_out_w, gla_in_w, gla_gk_w, gla_gk_b, gla_head_norm_w, gla_out_w, final_norm_w, loss_target, m_norm_w, m_pool_in_w, m_pool_group_w, m_pool_group_b, m_pool_scale, m_pool_out_w, m_gla_in_w, m_gla_gk_w, m_gla_gk_b, m_gla_head_norm_w, m_gla_out_w, m_final_norm_w, v_norm_w, v_pool_in_w, v_pool_group_w, v_pool_group_b, v_pool_scale, v_pool_out_w, v_gla_in_w, v_gla_gk_w, v_gla_gk_b, v_gla_head_norm_w, v_gla_out_w, v_final_norm_w):
    xs = x[0]
    target = loss_target[0]
    q_chip = 2 * lax.axis_index("x") + lax.axis_index("y")
    place = jnp.stack([lax.axis_index("c"), q_chip]).astype(jnp.int32)

    (wpi, gw_q, wpo_q, wgi_q, wgo_q), small_all = allgather_weights(
        [pool_in_w[0], pool_group_w[0].reshape(GROUP_DIM, GROUP_DIM), pool_out_w[0], gla_in_w[0], gla_out_w[0]],
        exchange=(True, True, True, False, False),
        smalls=[gla_gk_b, gla_head_norm_w, pool_group_b[0], gla_gk_w[0]])
    gw = _groups_from_quarters(gw_q)
    wpo = wpo_q.reshape(D, D)
    small_all = small_all[0::2]
    bgk = small_all[:, 0, :].reshape(1, KEY_W)
    hw = small_all[:, 1, 0:64].reshape(1, HEAD_V)
    gb = small_all[:, 2:2 + GROUPS, 0:64].transpose(1, 0, 2).reshape(1, D)
    wgk16 = small_all[:, 8:8 + GATE_RANK, :].transpose(1, 0, 2).reshape(GATE_RANK, KEY_W)
    wgk = _bf(jnp.pad(wgk16, ((0, RANK_PAD - GATE_RANK), (0, 0))))
    hw_tiled = jnp.tile(hw, (1, HEADS))

    w0 = norm_w[0:1]
    w1 = norm_w[1:2]
    wf = final_norm_w.reshape(1, D)

    dx, reduced, total = local_gradients(
        xs, target, w0, w1, wf, wpi, gw, gb, pool_scale, wpo, [wgi_q, wgo_q], wgk, bgk, hw_tiled, place)
    r_pool_in, r_group_w, r_pool_out, r_gla_in, r_gla_out = reduced
    r_group_w = r_group_w.reshape(GROUPS, 64, GROUP_DIM)

    loss = total[7, 0]
    g_norm = jnp.stack([total[0], total[3]])
    g_scale = total[1:2]
    g_final = total[5]
    pick = lambda full, width: lax.dynamic_slice_in_dim(full, q_chip * width, width, axis=-1)
    g_gk_b = pick(total[4:5, 0:KEY_W], 128)
    g_hnw = pick(total[6:7, 0:HEAD_V], 64)
    g_group_b = pick(total[2].reshape(GROUPS, GROUP_DIM), 64)[None]
    g_gk_w = pick(total[8:16].reshape(GATE_RANK, KEY_W), 128)[None]

    turn = lambda a: jnp.transpose(a, (2, 0, 1))
    back = lambda a: jnp.transpose(a, (1, 2, 0))
    as2d = lambda a, w: a.reshape(-1, w.shape[-1])
    big_names = ("pool_in_w", "pool_group_w", "pool_out_w", "gla_in_w", "gla_out_w")
    big_args = [(pool_in_w, r_pool_in[None], m_pool_in_w, v_pool_in_w),
                (pool_group_w, r_group_w[None], m_pool_group_w, v_pool_group_w),
                (pool_out_w, r_pool_out[None], m_pool_out_w, v_pool_out_w),
                (gla_in_w, r_gla_in[None], m_gla_in_w, v_gla_in_w),
                (gla_out_w, r_gla_out[None], m_gla_out_w, v_gla_out_w)]
    to_kernel = lambda n, a, w: turn(a) if n == "gla_in_w" else as2d(a, w)
    from_kernel = lambda n, a, w: back(a) if n == "gla_in_w" else a.reshape(w.shape)
    big_in = [tuple(to_kernel(n, a, p[0]) for a in p) for n, p in zip(big_names, big_args)]
    big_out = adamw(big_in, "adamw")
    big = {n: (from_kernel(n, i[1], p[0]),) + tuple(from_kernel(n, o, p[0]) for o in out)
           for n, p, i, out in zip(big_names, big_args, big_in, big_out)}

    small_names = ("norm_w", "pool_group_b", "pool_scale", "gla_gk_w", "gla_gk_b", "gla_head_norm_w",
                   "final_norm_w")
    small_args = [(norm_w, g_norm, m_norm_w, v_norm_w),
                  (pool_group_b, g_group_b, m_pool_group_b, v_pool_group_b),
                  (pool_scale, g_scale, m_pool_scale, v_pool_scale),
                  (gla_gk_w, g_gk_w, m_gla_gk_w, v_gla_gk_w),
                  (gla_gk_b, g_gk_b, m_gla_gk_b, v_gla_gk_b),
                  (gla_head_norm_w, g_hnw, m_gla_head_norm_w, v_gla_head_norm_w),
                  (final_norm_w, g_final, m_final_norm_w, v_final_norm_w)]
    small_out = adamw_small([tuple(as2d(a, p[0]) for a in p) for p in small_args])
    small = {n: (p[1].reshape(p[0].shape),) + tuple(o.reshape(p[0].shape) for o in out)
             for n, p, out in zip(small_names, small_args, small_out)}
    results = [
        small["norm_w"],
        big["pool_in_w"],
        big["pool_group_w"],
        small["pool_group_b"],
        small["pool_scale"],
        big["pool_out_w"],
        big["gla_in_w"],
        small["gla_gk_w"],
        small["gla_gk_b"],
        small["gla_head_norm_w"],
        big["gla_out_w"],
        small["final_norm_w"],
    ]
    grads, deltas, new_m, new_v = zip(*results)
    return (loss, dx[None], *grads, *deltas, *new_m, *new_v)
```

```python
import jax
import jax.numpy as jnp
from jax import lax
from jax.experimental import pallas as pl
from jax.experimental.pallas import tpu as pltpu

F32 = jnp.float32
BF16 = jnp.bfloat16
MESH = pl.DeviceIdType.MESH

D = 1024
POOL_WINDOWS = (2, 4, 8, 16)
GROUPS = 4
GROUP_DIM = 256
HEADS = 4
HEAD_K = 128
HEAD_V = 256
KEY_W = 512
CHUNK = 64
GATE_RANK = 16
GATE_NORM = 16.0
GLA_IN = 3088
GLA_MAIN = 3072
RANK_PAD = 128
EPS = 1e-6
HALO = 32

ADAM_LR = 0.001
ADAM_B1 = 0.9
ADAM_B2 = 0.999
ADAM_EPS = 1e-08
ADAM_WD = 0.01
ADAM_STEP = 10

N_CHIPS = 4
N_DEV = 8
GLA_IN_QUARTER = GLA_IN // N_CHIPS

VMEM_LIMIT = 56 * 1024 * 1024


def _nn(a, b):
    return lax.dot_general(a, b, (((1,), (0,)), ((), ())), preferred_element_type=F32)


def _nt(a, b):
    return lax.dot_general(a, b, (((1,), (1,)), ((), ())), preferred_element_type=F32)


def _tn(a, b):
    return lax.dot_general(a, b, (((0,), (0,)), ((), ())), preferred_element_type=F32)


def _nn_exact(a, b):
    return lax.dot_general(a, b, (((1,), (0,)), ((), ())), preferred_element_type=F32,
                           precision=lax.Precision.HIGHEST)


def _bf(a):
    return a.astype(BF16)


def _params(*sem):
    return pltpu.CompilerParams(dimension_semantics=sem, vmem_limit_bytes=VMEM_LIMIT)


def _full(shape):
    return pl.BlockSpec(shape, lambda i: (0,) * len(shape))


def _position():
    return lax.axis_index("x"), lax.axis_index("y"), lax.axis_index("c")


def _gather_small(in_ref, all_ref, send_sems, recv_sems, local_sem):
    x, y, c = _position()
    me = 4 * x + 2 * y + c
    mine = pltpu.make_async_copy(in_ref, all_ref.at[me], local_sem)
    mine.start()
    sends = []
    for k in range(N_DEV - 1):
        fx, fy, fc = (k + 1) >> 2 & 1, (k + 1) >> 1 & 1, (k + 1) & 1
        cp = pltpu.make_async_remote_copy(
            src_ref=in_ref, dst_ref=all_ref.at[me],
            send_sem=send_sems.at[k], recv_sem=recv_sems.at[k],
            device_id=(x ^ fx, y ^ fy, c ^ fc), device_id_type=MESH)
        cp.start()
        sends.append(cp)
    def wait():
        for k in range(N_DEV - 1):
            fx, fy, fc = (k + 1) >> 2 & 1, (k + 1) >> 1 & 1, (k + 1) & 1
            src_dev = 4 * (x ^ fx) + 2 * (y ^ fy) + (c ^ fc)
            pltpu.make_async_remote_copy(
                src_ref=in_ref, dst_ref=all_ref.at[src_dev],
                send_sem=send_sems.at[k], recv_sem=recv_sems.at[k],
                device_id=(x, y, c), device_id_type=MESH).wait_recv()
        for cp in sends:
            cp.wait_send()
        mine.wait()

    return wait


SMALL_SEMS = [pltpu.SemaphoreType.DMA((N_DEV - 1,)), pltpu.SemaphoreType.DMA((N_DEV - 1,)),
              pltpu.SemaphoreType.DMA]
VMEM_SPEC = pl.BlockSpec(memory_space=pltpu.VMEM)


def _other_chips(x, y):
    return [(1 - x, y), (x, 1 - y), (1 - x, 1 - y)]


def _any_specs(n):
    return [pl.BlockSpec(memory_space=pl.ANY)] * n


def _halves(rows, c):
    half = rows // 2
    return pl.ds(c * half, half), pl.ds((1 - c) * half, half)


CAST_ROWS = 256


def _gather_copy(out_ref, send_sems, recv_sems, k, quarter, half, to, src=None):
    dst = out_ref.at[quarter, half]
    return pltpu.make_async_remote_copy(
        src_ref=dst if src is None else src, dst_ref=dst,
        send_sem=send_sems.at[k], recv_sem=recv_sems.at[k], device_id=to, device_id_type=MESH)


SMALL_IN_ROWS = 24


def allgather_weights(quarters, exchange, smalls):
    n = len(quarters)
    shapes = [w.shape for w in quarters]
    moved = [i for i in range(n) if exchange[i]]

    def body(*refs):
        w_refs, (gkb_ref, hnw_ref, gb_ref, gkw_ref) = refs[:n], refs[n:n + 4]
        out_refs, small_all_ref = refs[n + 4:2 * n + 4], refs[2 * n + 4]
        refs = refs[2 * n + 5:]
        f32_bufs, bf_bufs = refs[:n], refs[n:2 * n]
        send_sems, recv_sems, local_sems, small_ref = refs[2 * n:2 * n + 4]
        small_ref[...] = jnp.zeros_like(small_ref)
        small_ref[0:1, :] = gkb_ref[...]
        small_ref[1:2, 0:64] = hnw_ref[...]
        small_ref[2:2 + GROUPS, 0:64] = gb_ref[...]
        small_ref[8:8 + GATE_RANK, :] = gkw_ref[...]
        wait_small = _gather_small(small_ref, small_all_ref, *refs[2 * n + 4:])
        x, y, c = _position()
        q = 2 * x + y
        sibling = (x, y, 1 - c)
        chips = _other_chips(x, y)

        def copy(k, i, quarter, half, to, src=None):
            return _gather_copy(out_refs[i], send_sems, recv_sems, k * n + i, quarter, half, to, src)

        loads = [pltpu.make_async_copy(w_refs[i], f32_bufs[i], local_sems.at[i]) for i in range(n)]
        for cp in loads:
            cp.start()
        keeps, sends = [], []
        for i in range(n):
            loads[i].wait()
            for r0 in range(0, shapes[i][0], CAST_ROWS):
                bf_bufs[i][r0:r0 + CAST_ROWS, :] = _bf(f32_bufs[i][r0:r0 + CAST_ROWS, :])
            keep = pltpu.make_async_copy(bf_bufs[i], out_refs[i].at[q], local_sems.at[n + i])
            keep.start()
            keeps.append(keep)
            if not exchange[i]:
                continue
            mine, _ = _halves(shapes[i][0], c)
            for j, chip in enumerate(chips):
                cp = copy(j, i, q, mine, (*chip, c), src=bf_bufs[i].at[mine])
                cp.start()
                sends.append(cp)
        for j, chip in enumerate(chips):
            qj = 2 * chip[0] + chip[1]
            for i in moved:
                mine, _ = _halves(shapes[i][0], c)
                copy(j, i, qj, mine, (x, y, c)).wait_recv()
                cp = copy(3 + j, i, qj, mine, sibling)
                cp.start()
                sends.append(cp)
        for j, chip in enumerate(chips):
            qj = 2 * chip[0] + chip[1]
            for i in moved:
                _, other = _halves(shapes[i][0], c)
                copy(3 + j, i, qj, other, (x, y, c)).wait_recv()
        wait_small()
        for cp in sends:
            cp.wait_send()
        for cp in keeps:
            cp.wait()

    outs = pl.pallas_call(
        body, name="allgather_weights",
        out_shape=[jax.ShapeDtypeStruct((N_CHIPS, *s), BF16) for s in shapes]
                  + [jax.ShapeDtypeStruct((N_DEV, SMALL_IN_ROWS, 128), F32)],
        in_specs=_any_specs(n) + [VMEM_SPEC] * 4, out_specs=_any_specs(n) + [VMEM_SPEC],
        scratch_shapes=([pltpu.VMEM(s, F32) for s in shapes] + [pltpu.VMEM(s, BF16) for s in shapes]
                        + [pltpu.SemaphoreType.DMA((6 * n,)), pltpu.SemaphoreType.DMA((6 * n,)),
                           pltpu.SemaphoreType.DMA((2 * n,)), pltpu.VMEM((SMALL_IN_ROWS, 128), F32)] + SMALL_SEMS),
        compiler_params=pltpu.CompilerParams(vmem_limit_bytes=VMEM_LIMIT),
    )(*quarters, *smalls)
    return outs[:n], outs[n]


def exchange_with_sibling(grads, name):
    n = len(grads)
    half_shape = lambda g: (*g.shape[:-2], g.shape[-2] // 2, g.shape[-1])

    def body(*refs):
        g_refs, theirs_refs = refs[:n], refs[n:2 * n]
        send_sems, recv_sems = refs[2 * n:]
        x, y, c = _position()
        copies = []
        for i in range(n):
            _, other = _halves(g_refs[i].shape[-2], c)
            cp = pltpu.make_async_remote_copy(
                src_ref=g_refs[i].at[:, other] if len(g_refs[i].shape) == 3 else g_refs[i].at[other],
                dst_ref=theirs_refs[i],
                send_sem=send_sems.at[i], recv_sem=recv_sems.at[i],
                device_id=(x, y, 1 - c), device_id_type=MESH)
            cp.start()
            copies.append(cp)
        for cp in copies:
            cp.wait()

    return pl.pallas_call(
        body, name=name,
        out_shape=[jax.ShapeDtypeStruct(half_shape(g), F32) for g in grads],
        in_specs=_any_specs(n), out_specs=_any_specs(n),
        scratch_shapes=[pltpu.SemaphoreType.DMA((n,)), pltpu.SemaphoreType.DMA((n,))],
    )(*grads)


def _scatter_copies(b_refs, got_refs, send_sems, recv_sems):
    n = len(b_refs)
    x, y, c = _position()
    copies = []
    for j, chip in enumerate(_other_chips(x, y)):
        qj = 2 * chip[0] + chip[1]
        for i in range(n):
            copies.append(pltpu.make_async_remote_copy(
                src_ref=b_refs[i].at[qj], dst_ref=got_refs[i].at[j],
                send_sem=send_sems.at[j * n + i], recv_sem=recv_sems.at[j * n + i],
                device_id=(*chip, c), device_id_type=MESH))
    return copies


def _scatter_shapes(chip_sums):
    return [jax.ShapeDtypeStruct((N_CHIPS - 1, *b.shape[1:]), BF16) for b in chip_sums]


def scatter_to_owners(chip_sums, name):
    n = len(chip_sums)

    def body(*refs):
        copies = _scatter_copies(refs[:n], refs[n:2 * n], *refs[2 * n:])
        for cp in copies:
            cp.start()
        for cp in copies:
            cp.wait()

    return pl.pallas_call(
        body, name=name,
        out_shape=_scatter_shapes(chip_sums),
        in_specs=_any_specs(n), out_specs=_any_specs(n),
        scratch_shapes=[pltpu.SemaphoreType.DMA((3 * n,)), pltpu.SemaphoreType.DMA((3 * n,))],
    )(*chip_sums)


SMALL_SUM_ROWS = 16


def join_halves(reduced, small_pool, small_gla, small_top, g_gk_pad):
    n = len(reduced)

    def body(*refs):
        pool_ref, gla_ref, top_ref, gk_ref = refs[n:n + 4]
        buf_refs, total_ref = refs[n + 4:2 * n + 4], refs[2 * n + 4]
        send_sems, recv_sems, all_ref, small_ref = refs[2 * n + 5:2 * n + 9]
        small_ref[0:3, :] = pool_ref[0:3, :]
        small_ref[3:5, :] = gla_ref[0:2, :]
        small_ref[5:8, :] = top_ref[0:3, :]
        for r in range(GATE_RANK):
            small_ref[8 + r // 2:9 + r // 2, (r % 2) * KEY_W:(r % 2 + 1) * KEY_W] = gk_ref[r:r + 1, :]
        x, y, c = _position()
        copies = []
        for i in range(n):
            mine, _ = _halves(buf_refs[i].shape[0], c)
            cp = pltpu.make_async_remote_copy(
                src_ref=buf_refs[i].at[mine], dst_ref=buf_refs[i].at[mine],
                send_sem=send_sems.at[i], recv_sem=recv_sems.at[i],
                device_id=(x, y, 1 - c), device_id_type=MESH)
            cp.start()
            copies.append(cp)
        _gather_small(small_ref, all_ref, *refs[2 * n + 9:])()
        total = all_ref[0]
        for dev in range(1, N_DEV):
            total = total + all_ref[dev]
        total_ref[...] = total
        for cp in copies:
            cp.wait()

    outs = pl.pallas_call(
        body, name="join_halves",
        out_shape=[jax.ShapeDtypeStruct(r.shape, F32) for r in reduced]
                  + [jax.ShapeDtypeStruct((SMALL_SUM_ROWS, D), F32)],
        in_specs=_any_specs(n) + [VMEM_SPEC] * 4, out_specs=_any_specs(n) + [VMEM_SPEC],
        input_output_aliases={i: i for i in range(n)},
        scratch_shapes=[pltpu.SemaphoreType.DMA((n,)), pltpu.SemaphoreType.DMA((n,)),
                        pltpu.VMEM((N_DEV, SMALL_SUM_ROWS, D), F32), pltpu.VMEM((SMALL_SUM_ROWS, D), F32)]
                       + SMALL_SEMS,
    )(*reduced, small_pool, small_gla, small_top, g_gk_pad)
    return outs[:n], outs[n]


ADD_ROWS = 512
ADD_HALVES_ROWS = 256


def _spans(counts):
    starts, total = [], 0
    for count in counts:
        starts.append(total)
        total += count
    return starts, total


def _local_step(t, start, count):
    return jnp.clip(t - start, 0, count - 1)


def add_halves(grads, theirs, place, name):
    n = len(grads)
    whole = [len(t.shape) == 2 for t in theirs]
    halves = [t.shape[-2] for t in theirs]
    cols = [GLA_IN_QUARTER if w else t.shape[-1] for t, w in zip(theirs, whole)]
    rbs = [min(ADD_HALVES_ROWS, h) for h in halves]
    counts = [h // rb for h, rb in zip(halves, rbs)]
    starts, total = _spans(counts)

    def body(place_ref, *refs):
        a_refs, b_refs = refs[:n], refs[n:2 * n]
        f_refs, h_refs = refs[2 * n:3 * n], refs[3 * n:4 * n]
        t = pl.program_id(0)
        q = place_ref[1]
        for i in range(n):
            @pl.when((t >= starts[i]) & (t < starts[i] + counts[i]))
            def _(i=i):
                if not whole[i]:
                    h_refs[i][...] = _bf(a_refs[i][...] + b_refs[i][...])
                    f_refs[i][...] = a_refs[i][q] + b_refs[i][q]
                    return
                total_i = a_refs[i][...] + b_refs[i][...]
                for k in range(N_CHIPS):
                    piece = total_i[:, k * cols[i]:(k + 1) * cols[i]]
                    h_refs[i][k] = _bf(piece)

                    @pl.when(q == k)
                    def _():
                        f_refs[i][...] = piece

    def specs(i):
        step = lambda t: _local_step(t, starts[i], counts[i])
        by_quarter = (N_CHIPS, rbs[i], cols[i])
        block = (rbs[i], theirs[i].shape[-1]) if whole[i] else by_quarter
        lead = () if whole[i] else (0,)
        mine = pl.BlockSpec(block, lambda t, place: (*lead, place[0] * counts[i] + step(t), 0))
        same = pl.BlockSpec(block, lambda t, place: (*lead, step(t), 0))
        sums = pl.BlockSpec(by_quarter, lambda t, place: (0, step(t), 0))
        own = pl.BlockSpec(by_quarter[1:], lambda t, place: (step(t), 0))
        return mine, same, own, sums

    all_specs = [specs(i) for i in range(n)]
    outs = pl.pallas_call(
        body, name=name,
        grid_spec=pltpu.PrefetchScalarGridSpec(
            num_scalar_prefetch=1, grid=(total,),
            in_specs=[sp[0] for sp in all_specs] + [sp[1] for sp in all_specs],
            out_specs=[sp[2] for sp in all_specs] + [sp[3] for sp in all_specs]),
        out_shape=[jax.ShapeDtypeStruct((h, cl), F32) for h, cl in zip(halves, cols)]
                  + [jax.ShapeDtypeStruct((N_CHIPS, h, cl), BF16) for h, cl in zip(halves, cols)],
        compiler_params=_params("arbitrary"),
    )(place, *grads, *theirs)
    return list(zip(outs[:n], outs[n:]))


def add_parts(owns, gots, place, name):
    n = len(owns)
    shapes = [g.shape for g in gots]
    rbs = [min(ADD_ROWS, sh[1]) for sh in shapes]
    counts = [sh[1] // rb for sh, rb in zip(shapes, rbs)]
    starts, total = _spans(counts)

    def body(place_ref, *refs):
        o_refs, g_refs, out_refs = refs[:n], refs[n:2 * n], refs[2 * n:]
        t = pl.program_id(0)
        for i in range(n):
            @pl.when((t >= starts[i]) & (t < starts[i] + counts[i]))
            def _(i=i):
                total_i = o_refs[i][...]
                for j in range(N_CHIPS - 1):
                    total_i = total_i + g_refs[i][j].astype(F32)
                out_refs[i][...] = total_i

    def specs(i):
        rb, cols = rbs[i], shapes[i][2]
        step = lambda t: _local_step(t, starts[i], counts[i])
        return (pl.BlockSpec((rb, cols), lambda t, place: (step(t), 0)),
                pl.BlockSpec((N_CHIPS - 1, rb, cols), lambda t, place: (0, step(t), 0)),
                pl.BlockSpec((rb, cols), lambda t, place: (place[0] * counts[i] + step(t), 0)))

    all_specs = [specs(i) for i in range(n)]
    return pl.pallas_call(
        body, name=name,
        grid_spec=pltpu.PrefetchScalarGridSpec(
            num_scalar_prefetch=1, grid=(total,),
            in_specs=[sp[0] for sp in all_specs] + [sp[1] for sp in all_specs],
            out_specs=[sp[2] for sp in all_specs]),
        out_shape=[jax.ShapeDtypeStruct((2 * sh[1], sh[2]), F32) for sh in shapes],
        compiler_params=_params("arbitrary"),
    )(place, *owns, *gots)


def _adam_math(w, g, m, v):
    m = ADAM_B1 * m + (1.0 - ADAM_B1) * g
    v = ADAM_B2 * v + (1.0 - ADAM_B2) * (g * g)
    m_hat = m / (1.0 - ADAM_B1 ** ADAM_STEP)
    v_hat = v / (1.0 - ADAM_B2 ** ADAM_STEP)
    delta = -ADAM_LR * (m_hat / (jnp.sqrt(v_hat) + ADAM_EPS) + ADAM_WD * w)
    return delta, m, v


ADAM_BLOCK_BYTES = 2 ** 19
ADAM_MOST_STEPS = 8


def adamw(params, name):
    n = len(params)
    shapes = [p[0].shape for p in params]

    def tile_rows(shape):
        rows, cols = shape[0], shape[-1]
        aligned = 1 if len(shape) == 3 else 8
        divisors = [t for t in range(aligned, rows + 1, aligned) if rows % t == 0]
        tile = max(t for t in divisors if t * cols * 4 <= ADAM_BLOCK_BYTES)
        if rows // tile > ADAM_MOST_STEPS:
            tile = min(t for t in divisors if rows // t <= ADAM_MOST_STEPS)
        return tile

    tiles = [tile_rows(sh) for sh in shapes]
    counts = [sh[0] // tl for sh, tl in zip(shapes, tiles)]
    starts, total = _spans(counts)

    def body(*refs):
        ins, outs = refs[:4 * n], refs[4 * n:]
        t = pl.program_id(0)
        for i in range(n):
            @pl.when((t >= starts[i]) & (t < starts[i] + counts[i]))
            def _(i=i):
                w_ref, g_ref, m_ref, v_ref = ins[4 * i:4 * i + 4]
                d, nm, nv = _adam_math(w_ref[...], g_ref[...], m_ref[...], v_ref[...])
                outs[3 * i][...] = d
                outs[3 * i + 1][...] = nm
                outs[3 * i + 2][...] = nv

    def spec(i):
        block = (tiles[i],) + shapes[i][1:]
        zeros = (0,) * (len(block) - 1)
        return pl.BlockSpec(block, lambda t: (_local_step(t, starts[i], counts[i]),) + zeros)

    outs = pl.pallas_call(
        body, name=name, grid=(total,),
        out_shape=[jax.ShapeDtypeStruct(sh, F32) for sh in shapes for _ in range(3)],
        in_specs=[spec(i) for i in range(n) for _ in range(4)],
        out_specs=[spec(i) for i in range(n) for _ in range(3)],
        compiler_params=_params("arbitrary"),
    )(*[a for p in params for a in p])
    return [tuple(outs[3 * i:3 * i + 3]) for i in range(n)]


def adamw_small(params):
    n = len(params)

    def body(*refs):
        ins, outs = refs[:4 * n], refs[4 * n:]
        for k in range(n):
            w_ref, g_ref, m_ref, v_ref = ins[4 * k:4 * k + 4]
            d, nm, nv = _adam_math(w_ref[...], g_ref[...], m_ref[...], v_ref[...])
            outs[3 * k][...] = d
            outs[3 * k + 1][...] = nm
            outs[3 * k + 2][...] = nv

    flat = [a for p in params for a in p]
    outs = pl.pallas_call(
        body, name="adamw_small",
        out_shape=[jax.ShapeDtypeStruct(p[0].shape, F32) for p in params for _ in range(3)],
        in_specs=[VMEM_SPEC] * (4 * n), out_specs=[VMEM_SPEC] * (3 * n),
    )(*flat)
    return [tuple(outs[3 * k:3 * k + 3]) for k in range(n)]


def matmul_tn(a, b, name, tile_n, by_column_tile=False, chip_sums=()):
    s, m = a.shape
    n = b.shape[1]
    n_sums = len(chip_sums)
    steps = n // tile_n
    if by_column_tile:
        out_shape = jax.ShapeDtypeStruct((steps, m, tile_n), F32)
        out_spec = pl.BlockSpec((None, m, tile_n), lambda j: (j, 0, 0))
    else:
        out_shape = jax.ShapeDtypeStruct((m, n), F32)
        out_spec = pl.BlockSpec((m, tile_n), lambda j: (0, j))

    def body(a_ref, b_ref, *rest):
        sum_refs, out_ref, got_refs = rest[:n_sums], rest[n_sums], rest[n_sums + 1:2 * n_sums + 1]
        j = pl.program_id(0)
        copies = _scatter_copies(sum_refs, got_refs, *rest[2 * n_sums + 1:]) if n_sums else []

        @pl.when(j == 0)
        def _():
            for cp in copies:
                cp.start()

        out_ref[...] = _tn(a_ref[...], b_ref[...])

        @pl.when(j == steps - 1)
        def _():
            for cp in copies:
                cp.wait()

    outs = pl.pallas_call(
        body, name=name, grid=(steps,),
        out_shape=[out_shape] + _scatter_shapes(chip_sums),
        in_specs=[_full((s, m)), pl.BlockSpec((s, tile_n), lambda j: (0, j))] + _any_specs(n_sums),
        out_specs=[out_spec] + _any_specs(n_sums),
        scratch_shapes=[pltpu.SemaphoreType.DMA((3 * n_sums,)), pltpu.SemaphoreType.DMA((3 * n_sums,))]
                       if n_sums else [],
        compiler_params=_params("arbitrary"),
    )(a, b, *chip_sums)
    return outs[0], outs[1:]


ROW_TILE = 512


def _row_index(tile, rows):
    return tile * rows + lax.broadcasted_iota(jnp.int32, (rows, 1), 0)


def _inverse_counts(t_glob):
    return [1.0 / jnp.minimum(t_glob + 1, w).astype(F32) for w in POOL_WINDOWS]


def _sigmoid(z):
    return 1.0 / (1.0 + jnp.exp(-z))


def _trailing_sums(src, tmp, cols, window, rows):
    bufs = (src, tmp)
    span, level, start = 1, 0, 0
    while span < window:
        start += 8
        a, b = bufs[level % 2], bufs[(level + 1) % 2]
        n = HALO + rows - start
        b[start:start + n, cols] = a[start:start + n, cols] + a[start - span:start - span + n, cols]
        span, level = 2 * span, level + 1
    return bufs[level % 2][HALO:HALO + rows, cols]


def _leading_sums(src, tmp, cols, window, rows):
    bufs = (src, tmp)
    span, level, n = 1, 0, rows + HALO
    while span < window:
        n -= 8
        a, b = bufs[level % 2], bufs[(level + 1) % 2]
        b[0:n, cols] = a[0:n, cols] + a[span:span + n, cols]
        span, level = 2 * span, level + 1
    return bufs[level % 2][0:rows, cols]


def gather_in_background(step, last, out_refs, send_sems, recv_sems, finish):
    n = len(out_refs)
    x, y, c = _position()
    q = 2 * x + y
    chips = _other_chips(x, y)

    def copy(k, i, quarter, half, to):
        return _gather_copy(out_refs[i], send_sems, recv_sems, k * n + i, quarter, half, to)

    if not finish:
        @pl.when(step == 0)
        def _():
            for i in range(n):
                mine, _ = _halves(out_refs[i].shape[1], c)
                for j, chip in enumerate(chips):
                    copy(j, i, q, mine, (*chip, c)).start()

        @pl.when(step == last)
        def _():
            for j, chip in enumerate(chips):
                qj = 2 * chip[0] + chip[1]
                for i in range(n):
                    mine, _ = _halves(out_refs[i].shape[1], c)
                    copy(j, i, qj, mine, (x, y, c)).wait_recv()
                    copy(3 + j, i, qj, mine, (x, y, 1 - c)).start()
        return

    @pl.when(step == last)
    def _():
        for j, chip in enumerate(chips):
            qj = 2 * chip[0] + chip[1]
            for i in range(n):
                mine, other = _halves(out_refs[i].shape[1], c)
                copy(3 + j, i, qj, other, (x, y, c)).wait_recv()
                copy(j, i, q, mine, (x, y, c)).wait_send()
                copy(3 + j, i, qj, mine, (x, y, c)).wait_send()


def pool_forward(x, w0, wpi, gw, gb, scale, wpo, later):
    s = x.shape[0]
    ts = ROW_TILE
    nt = s // ts
    assert nt >= 2
    n_later = len(later)

    def body(x_ref, w0_ref, wpi_ref, gw_ref, gb_ref, sc_ref, wpo_ref, *rest):
        rest = rest[n_later:]
        h1_ref, pooled_ref, gt_ref, n0_ref = rest[:4]
        later_refs = rest[4:4 + n_later]
        ubuf, tbuf, hist, send_sems, recv_sems = rest[4 + n_later:]
        i = pl.program_id(0)
        gather_in_background(i, nt - 1, later_refs, send_sems, recv_sems, finish=False)
        xv = x_ref[...]
        r = lax.rsqrt(jnp.mean(xv * xv, axis=-1, keepdims=True) + EPS)
        n0 = _bf(xv * r * w0_ref[...])
        n0_ref[...] = n0
        u = jnp.concatenate([_nn(n0, wpi_ref[0]), _nn(n0, wpi_ref[1])], axis=-1)
        gt = jnp.concatenate([_nn(n0, wpi_ref[2]), _nn(n0, wpi_ref[3])], axis=-1)
        gt_ref[...] = gt

        @pl.when(i == 0)
        def _():
            hist[...] = jnp.zeros_like(hist)

        ubuf[0:HALO, :] = hist[...]
        ubuf[HALO:HALO + ts, :] = u
        hist[...] = u[ts - HALO:, :]
        inv = _inverse_counts(_row_index(i, ts))
        mixed = []
        for g, w in enumerate(POOL_WINDOWS):
            cols = slice(g * GROUP_DIM, (g + 1) * GROUP_DIM)
            pooled = _bf(_trailing_sums(ubuf, tbuf, cols, w, ts) * inv[g] - u[:, cols])
            pooled_ref[:, cols] = pooled
            mixed.append(_nn(pooled, gw_ref[g]))
        mixed = jnp.concatenate(mixed, axis=-1) + gb_ref[...]
        y = mixed * sc_ref[...] * (gt * _sigmoid(gt))
        h1_ref[...] = xv + _nn(_bf(y), wpo_ref[...])
        gather_in_background(i, nt - 1, later_refs, send_sems, recv_sems, finish=True)

    row = lambda cols: pl.BlockSpec((ts, cols), lambda i: (i, 0))
    outs = pl.pallas_call(
        body, name="pool_forward", grid=(nt,),
        out_shape=[jax.ShapeDtypeStruct((s, D), F32), jax.ShapeDtypeStruct((s, D), BF16),
                   jax.ShapeDtypeStruct((s, D), F32), jax.ShapeDtypeStruct((s, D), BF16)]
                  + [jax.ShapeDtypeStruct(a.shape, a.dtype) for a in later],
        in_specs=[row(D), _full((1, D)), _full((N_CHIPS, D, D // 2)), _full((GROUPS, GROUP_DIM, GROUP_DIM)),
                  _full((1, D)), _full((1, D)), _full((D, D))] + _any_specs(n_later),
        out_specs=[row(D), row(D), row(D), row(D)] + _any_specs(n_later),
        input_output_aliases={7 + k: 4 + k for k in range(n_later)},
        scratch_shapes=[pltpu.VMEM((HALO + ts, D), F32), pltpu.VMEM((HALO + ts, D), F32),
                        pltpu.VMEM((HALO, D), F32),
                        pltpu.SemaphoreType.DMA((6 * n_later,)), pltpu.SemaphoreType.DMA((6 * n_later,))],
        compiler_params=_params("arbitrary"),
    )(x, w0, wpi, gw, gb, scale, wpo, *later)
    return outs[:4], outs[4:]


def pool_backward(x, dh1, pooled, gt, w0, wpi, gw, gb, scale, wpo, chip_sums):
    s = x.shape[0]
    ts = ROW_TILE
    nt = s // ts
    n_sums = len(chip_sums)

    def body(x_ref, dh1_ref, pooled_ref, gt_ref, w0_ref, wpi_ref, gw_ref, gb_ref, sc_ref, wpo_ref, *rest):
        sum_refs, rest = rest[:n_sums], rest[n_sums:]
        dx_ref, dproj_ref, gpo_ref, ggw_ref, small_ref = rest[:5]
        got_refs = rest[5:5 + n_sums]
        ebuf, tbuf, ahead, send_sems, recv_sems = rest[5 + n_sums:]
        i = pl.program_id(0)
        copies = _scatter_copies(sum_refs, got_refs, send_sems, recv_sems)

        @pl.when(i == 0)
        def _():
            for cp in copies:
                cp.start()

        @pl.when(i == 0)
        def _():
            gpo_ref[...] = jnp.zeros_like(gpo_ref)
            ggw_ref[...] = jnp.zeros_like(ggw_ref)
            small_ref[...] = jnp.zeros_like(small_ref)
            ahead[...] = jnp.zeros_like(ahead)

        dh1 = dh1_ref[...]
        dh1_bf = _bf(dh1)
        gt = gt_ref[...]
        sc = sc_ref[...]
        dy = _nt(dh1_bf, wpo_ref[...])
        pooled_bf = []
        mixed = []
        for g in range(GROUPS):
            cols = slice(g * GROUP_DIM, (g + 1) * GROUP_DIM)
            pb = pooled_ref[:, cols]
            pooled_bf.append(pb)
            mixed.append(_nn(pb, gw_ref[g]))
        mixed = jnp.concatenate(mixed, axis=-1) + gb_ref[...]
        sg = _sigmoid(gt)
        silu = gt * sg
        gpo_ref[...] += _tn(_bf(mixed * sc * silu), dh1_bf)
        dmixed = dy * sc * silu
        dgt = dy * mixed * sc * (sg * (1.0 + gt * (1.0 - sg)))
        dproj_ref[:, D:] = _bf(dgt)
        small_ref[1:2, :] += jnp.sum(dy * mixed * silu, axis=0, keepdims=True)
        small_ref[2:3, :] += jnp.sum(dmixed, axis=0, keepdims=True)

        inv = _inverse_counts(_row_index(nt - 1 - i, ts))
        ebuf[ts:ts + HALO, :] = ahead[...]
        dpooled = []
        for g in range(GROUPS):
            cols = slice(g * GROUP_DIM, (g + 1) * GROUP_DIM)
            dm = _bf(dmixed[:, cols])
            ggw_ref[g] += _tn(pooled_bf[g], dm)
            dp = _nt(dm, gw_ref[g])
            dpooled.append(dp)
            ebuf[0:ts, cols] = dp * inv[g]
        ahead[...] = ebuf[0:HALO, :]
        du = []
        for g, w in enumerate(POOL_WINDOWS):
            cols = slice(g * GROUP_DIM, (g + 1) * GROUP_DIM)
            du.append(_leading_sums(ebuf, tbuf, cols, w, ts) - dpooled[g])
        du = _bf(jnp.concatenate(du, axis=-1))
        dproj_ref[:, :D] = du
        dgt_bf = _bf(dgt)
        half = D // 2
        dn0 = (_nt(du[:, :half], wpi_ref[0]) + _nt(du[:, half:], wpi_ref[1])
               + _nt(dgt_bf[:, :half], wpi_ref[2]) + _nt(dgt_bf[:, half:], wpi_ref[3]))

        xv = x_ref[...]
        r = lax.rsqrt(jnp.mean(xv * xv, axis=-1, keepdims=True) + EPS)
        xhat = xv * r
        small_ref[0:1, :] += jnp.sum(dn0 * xhat, axis=0, keepdims=True)
        dxh = dn0 * w0_ref[...]
        dx_ref[...] = dh1 + r * (dxh - xhat * jnp.mean(dxh * xhat, axis=-1, keepdims=True))

        @pl.when(i == nt - 1)
        def _():
            for cp in copies:
                cp.wait()

    row = lambda cols: pl.BlockSpec((ts, cols), lambda i: (nt - 1 - i, 0))
    outs = pl.pallas_call(
        body, name="pool_backward", grid=(nt,),
        out_shape=[jax.ShapeDtypeStruct((s, D), F32), jax.ShapeDtypeStruct((s, 2 * D), BF16),
                   jax.ShapeDtypeStruct((D, D), F32),
                   jax.ShapeDtypeStruct((GROUPS, GROUP_DIM, GROUP_DIM), F32),
                   jax.ShapeDtypeStruct((8, D), F32)] + _scatter_shapes(chip_sums),
        in_specs=[row(D), row(D), row(D), row(D), _full((1, D)), _full((N_CHIPS, D, D // 2)),
                  _full((GROUPS, GROUP_DIM, GROUP_DIM)), _full((1, D)), _full((1, D)), _full((D, D))]
                 + _any_specs(n_sums),
        out_specs=[row(D), row(2 * D), _full((D, D)), _full((GROUPS, GROUP_DIM, GROUP_DIM)), _full((8, D))]
                  + _any_specs(n_sums),
        scratch_shapes=[pltpu.VMEM((ts + HALO, D), F32), pltpu.VMEM((ts + HALO, D), F32),
                        pltpu.VMEM((HALO, D), F32),
                        pltpu.SemaphoreType.DMA((3 * n_sums,)), pltpu.SemaphoreType.DMA((3 * n_sums,))],
        compiler_params=_params("arbitrary"),
    )(x, dh1, pooled, gt, w0, wpi, gw, gb, scale, wpo, *chip_sums)
    return outs[:5], outs[5:]


def gla_project(h1, w1, wgi_q, wgk, bgk, later):
    s = h1.shape[0]
    ts = ROW_TILE
    nt = s // ts
    assert nt >= 2
    n_later = len(later)

    def body(h_ref, w1_ref, wq_ref, wgk_ref, bgk_ref, *rest):
        rest = rest[n_later:]
        qk_ref, v_ref, gate_ref, low_ref, cum_ref, n1_ref = rest[:6]
        later_refs = rest[6:6 + n_later]
        send_sems, recv_sems, wgi_ref = rest[6 + n_later:]
        gather_in_background(pl.program_id(0), nt - 1, later_refs, send_sems, recv_sems, finish=False)

        @pl.when(pl.program_id(0) == 0)
        def _():
            _assemble_gla_in(wq_ref, wgi_ref)

        hv = h_ref[...]
        r = lax.rsqrt(jnp.mean(hv * hv, axis=-1, keepdims=True) + EPS)
        n1 = _bf(hv * r * w1_ref[...])
        n1_ref[...] = n1
        qk_ref[...] = _nn(n1, wgi_ref[:, 0:2 * KEY_W])
        v_ref[...] = _bf(_nn(n1, wgi_ref[:, 2 * KEY_W:2 * KEY_W + D]))
        gate_ref[...] = _nn(n1, wgi_ref[:, 2 * KEY_W + D:GLA_MAIN])
        low = _bf(_nn(n1, wgi_ref[:, GLA_MAIN:]))
        low_ref[...] = low
        z = _nn(low, wgk_ref[...]) + bgk_ref[...]
        lg = (jnp.minimum(z, 0.0) - jnp.log(1.0 + jnp.exp(-jnp.abs(z)))) / GATE_NORM
        lower_f = _chunk_masks()[0].astype(F32)
        for r0 in range(0, ts, CHUNK):
            cum_ref[r0:r0 + CHUNK, :] = _nn_exact(lower_f, lg[r0:r0 + CHUNK, :])
        gather_in_background(pl.program_id(0), nt - 1, later_refs, send_sems, recv_sems, finish=True)

    row = lambda cols: pl.BlockSpec((ts, cols), lambda i: (i, 0))
    outs = pl.pallas_call(
        body, name="gla_project", grid=(nt,),
        out_shape=[jax.ShapeDtypeStruct((s, D), F32), jax.ShapeDtypeStruct((s, D), BF16),
                   jax.ShapeDtypeStruct((s, D), F32), jax.ShapeDtypeStruct((s, RANK_PAD), BF16),
                   jax.ShapeDtypeStruct((s, KEY_W), F32), jax.ShapeDtypeStruct((s, D), BF16)]
                  + [jax.ShapeDtypeStruct(a.shape, a.dtype) for a in later],
        in_specs=[row(D), _full((1, D)), _full((N_CHIPS, D, GLA_IN_QUARTER)),
                  _full((RANK_PAD, KEY_W)), _full((1, KEY_W))] + _any_specs(n_later),
        out_specs=[row(D), row(D), row(D), row(RANK_PAD), row(KEY_W), row(D)] + _any_specs(n_later),
        input_output_aliases={5 + k: 6 + k for k in range(n_later)},
        scratch_shapes=[pltpu.SemaphoreType.DMA((6 * n_later,)), pltpu.SemaphoreType.DMA((6 * n_later,)),
                        pltpu.VMEM((D, GLA_MAIN + RANK_PAD), BF16)],
        compiler_params=_params("arbitrary"),
    )(h1, w1, wgi_q, wgk, bgk, *later)
    return outs[:6], outs[6:]


def _assemble_gla_in(wq_ref, wfull):
    pad = jnp.zeros((CAST_ROWS, GLA_MAIN + RANK_PAD - GLA_IN), BF16)
    for r0 in range(0, D, CAST_ROWS):
        rows = slice(r0, r0 + CAST_ROWS)
        wfull[rows, :] = jnp.concatenate([wq_ref[q, rows, :] for q in range(N_CHIPS)] + [pad], axis=1)


GLA_BLOCK = 512
CHUNKS_PER_BLOCK = GLA_BLOCK // CHUNK


def _chunk_masks():
    t = lax.broadcasted_iota(jnp.int32, (CHUNK, CHUNK), 0)
    u = lax.broadcasted_iota(jnp.int32, (CHUNK, CHUNK), 1)
    return t >= u, t <= u


def _gla_chunk_terms(q, cum):
    ep = jnp.exp(cum)
    en = jnp.exp(-cum)
    qs = q * (HEAD_K ** -0.5)
    last = cum[CHUNK - 1:CHUNK, :]
    ed = jnp.exp(last - cum)
    dec = jnp.exp(last)
    return ep, en, qs, ed, dec


def gla_forward(qk, v, cum):
    s = qk.shape[0]
    nb = s // GLA_BLOCK
    nc = s // CHUNK

    def body(q_ref, k_ref, v_ref, cum_ref, o_ref, st_ref, sc_ref, state):
        @pl.when(pl.program_id(0) == 0)
        def _():
            state[...] = jnp.zeros_like(state)

        lower, _ = _chunk_masks()

        def chunk(cc, carry):
            rows = pl.ds(pl.multiple_of(cc * CHUNK, CHUNK), CHUNK)
            for h in range(HEADS):
                kc = slice(h * HEAD_K, (h + 1) * HEAD_K)
                vc = slice(h * HEAD_V, (h + 1) * HEAD_V)
                q = q_ref[rows, kc]
                k = k_ref[rows, kc]
                v = v_ref[rows, vc]
                ep, en, qs, ed, dec = _gla_chunk_terms(q, cum_ref[rows, kc])
                a = _bf(qs * ep)
                fwd = _nt(a, _bf(k * en))
                bwd = _nt(_bf(qs * en), _bf(k * ep))
                scores = _bf(jnp.where(lower, fwd, bwd))
                sc_ref[rows, h * CHUNK:(h + 1) * CHUNK] = scores
                st = state[h]
                st_ref[cc, h] = st
                o_ref[rows, vc] = _nn(scores, v) + _nt(a, _bf(st))
                state[h] = st * dec + _tn(v, _bf(k * ed))
            return carry

        lax.fori_loop(0, CHUNKS_PER_BLOCK, chunk, 0, unroll=4)

    return pl.pallas_call(
        body, name="gla_forward", grid=(nb,),
        out_shape=(jax.ShapeDtypeStruct((s, D), F32),
                   jax.ShapeDtypeStruct((nc, HEADS, HEAD_V, HEAD_K), F32),
                   jax.ShapeDtypeStruct((s, HEADS * CHUNK), BF16)),
        in_specs=[pl.BlockSpec((GLA_BLOCK, KEY_W), lambda i: (i, 0)),
                  pl.BlockSpec((GLA_BLOCK, KEY_W), lambda i: (i, 1)),
                  pl.BlockSpec((GLA_BLOCK, D), lambda i: (i, 0)),
                  pl.BlockSpec((GLA_BLOCK, KEY_W), lambda i: (i, 0))],
        out_specs=(pl.BlockSpec((GLA_BLOCK, D), lambda i: (i, 0)),
                   pl.BlockSpec((CHUNKS_PER_BLOCK, HEADS, HEAD_V, HEAD_K), lambda i: (i, 0, 0, 0)),
                   pl.BlockSpec((GLA_BLOCK, HEADS * CHUNK), lambda i: (i, 0))),
        scratch_shapes=[pltpu.VMEM((HEADS, HEAD_V, HEAD_K), F32)],
        compiler_params=_params("arbitrary"),
    )(qk, qk, v, cum)


def gla_backward(qk, v, cum, do, states, scores):
    s = qk.shape[0]
    nb = s // GLA_BLOCK

    def body(q_ref, k_ref, v_ref, cum_ref, do_ref, st_ref, sc_ref, dq_ref, dk_ref, dv_ref, dcum_ref, dstate):
        @pl.when(pl.program_id(0) == 0)
        def _():
            dstate[...] = jnp.zeros_like(dstate)

        lower, _ = _chunk_masks()
        is_last = lax.broadcasted_iota(jnp.int32, (CHUNK, HEAD_K), 0) == CHUNK - 1

        def chunk(step, carry):
            cc = CHUNKS_PER_BLOCK - 1 - step
            rows = pl.ds(pl.multiple_of(cc * CHUNK, CHUNK), CHUNK)
            for h in range(HEADS):
                kc = slice(h * HEAD_K, (h + 1) * HEAD_K)
                vc = slice(h * HEAD_V, (h + 1) * HEAD_V)
                q = q_ref[rows, kc]
                k = k_ref[rows, kc]
                v = v_ref[rows, vc]
                do_c = do_ref[rows, vc]
                ep, en, qs, ed, dec = _gla_chunk_terms(q, cum_ref[rows, kc])
                a = _bf(qs * ep)
                b = _bf(k * en)
                c = _bf(qs * en)
                dk_dec = _bf(k * ep)
                kd = _bf(k * ed)
                scores = sc_ref[rows, h * CHUNK:(h + 1) * CHUNK]
                st = st_ref[cc, h]
                dst = dstate[h]
                dst_bf = _bf(dst)

                dscores = _nt(do_c, v)
                dfwd = _bf(jnp.where(lower, dscores, 0.0))
                dbwd = _bf(jnp.where(lower, 0.0, dscores))
                dv_ref[rows, vc] = _bf(_tn(scores, do_c) + _nt(kd, dst_bf))
                da = _nn(dfwd, b) + _nn(do_c, _bf(st))
                db = _tn(dfwd, a)
                dc = _nn(dbwd, dk_dec)
                ddk = _tn(dbwd, c)
                dkd = _nn(v, dst_bf)
                ddec = jnp.sum(dst * st, axis=0, keepdims=True)
                dstate[h] = dst * dec + _tn(do_c, a)

                m = dkd * k * ed
                dq_ref[rows, kc] = _bf((da * ep + dc * en) * (HEAD_K ** -0.5))
                dk_ref[rows, kc] = _bf(db * en + ddk * ep + dkd * ed)
                dcum = (da * qs + ddk * k) * ep - (db * k + dc * qs) * en - m
                dlast = jnp.sum(m, axis=0, keepdims=True) + ddec * dec
                dcum_ref[rows, kc] = dcum + jnp.where(is_last, dlast, 0.0)
            return carry

        lax.fori_loop(0, CHUNKS_PER_BLOCK, chunk, 0, unroll=4)

    rev = lambda cols, col_block: pl.BlockSpec((GLA_BLOCK, cols), lambda i: (nb - 1 - i, col_block))
    return pl.pallas_call(
        body, name="gla_backward", grid=(nb,),
        out_shape=(jax.ShapeDtypeStruct((s, KEY_W), BF16), jax.ShapeDtypeStruct((s, KEY_W), BF16),
                   jax.ShapeDtypeStruct((s, D), BF16), jax.ShapeDtypeStruct((s, KEY_W), F32)),
        in_specs=[rev(KEY_W, 0), rev(KEY_W, 1), rev(D, 0), rev(KEY_W, 0), rev(D, 0),
                  pl.BlockSpec((CHUNKS_PER_BLOCK, HEADS, HEAD_V, HEAD_K), lambda i: (nb - 1 - i, 0, 0, 0)),
                  rev(HEADS * CHUNK, 0)],
        out_specs=(rev(KEY_W, 0), rev(KEY_W, 0), rev(D, 0), rev(KEY_W, 0)),
        scratch_shapes=[pltpu.VMEM((HEADS, HEAD_V, HEAD_K), F32)],
        compiler_params=_params("arbitrary"),
    )(qk, qk, v, cum, do, states, scores)


def head_and_loss(o, gate, h1, target, hw, wgo, wf):
    s = o.shape[0]
    ts = ROW_TILE

    def body(o_ref, gate_ref, h1_ref, tgt_ref, hw_ref, wgo_ref, wf_ref,
             dh2_ref, do_ref, dgate_ref, ggo_ref, small_ref):
        @pl.when(pl.program_id(0) == 0)
        def _():
            ggo_ref[...] = jnp.zeros_like(ggo_ref)
            small_ref[...] = jnp.zeros_like(small_ref)

        gate = gate_ref[...]
        hw = hw_ref[...]
        sg = _sigmoid(gate)
        silu = gate * sg
        ohat, ro = [], []
        for h in range(HEADS):
            oh = o_ref[:, h * HEAD_V:(h + 1) * HEAD_V]
            rh = lax.rsqrt(jnp.mean(oh * oh, axis=-1, keepdims=True) + EPS)
            ro.append(rh)
            ohat.append(oh * rh)
        ohat = jnp.concatenate(ohat, axis=-1)
        on = ohat * hw
        y2 = _bf(on * silu)
        h2 = h1_ref[...] + _nn(y2, wgo_ref[...])
        rf = lax.rsqrt(jnp.mean(h2 * h2, axis=-1, keepdims=True) + EPS)
        h2hat = h2 * rf
        wf = wf_ref[...]
        diff = h2hat * wf - tgt_ref[...]
        small_ref[2:3, :] += jnp.zeros((1, D), F32) + 0.5 * jnp.sum(diff * diff) / D
        dout = diff / D
        small_ref[0:1, :] += jnp.sum(dout * h2hat, axis=0, keepdims=True)
        dxh = dout * wf
        dh2 = rf * (dxh - h2hat * jnp.mean(dxh * h2hat, axis=-1, keepdims=True))
        dh2_ref[...] = dh2
        dh2_bf = _bf(dh2)
        ggo_ref[...] += _tn(y2, dh2_bf)
        dy2 = _nt(dh2_bf, wgo_ref[...])
        don = dy2 * silu
        dgate_ref[...] = _bf(dy2 * on * (sg * (1.0 + gate * (1.0 - sg))))
        ghw = jnp.sum(don * ohat, axis=0, keepdims=True)
        small_ref[1:2, 0:HEAD_V] += sum(ghw[:, h * HEAD_V:(h + 1) * HEAD_V] for h in range(HEADS))
        dohat = don * hw
        for h in range(HEADS):
            cols = slice(h * HEAD_V, (h + 1) * HEAD_V)
            oh, dh = ohat[:, cols], dohat[:, cols]
            do_ref[:, cols] = _bf(ro[h] * (dh - oh * jnp.mean(dh * oh, axis=-1, keepdims=True)))

    row = lambda cols: pl.BlockSpec((ts, cols), lambda i: (i, 0))
    act = jax.ShapeDtypeStruct((s, D), F32)
    act_bf = jax.ShapeDtypeStruct((s, D), BF16)
    return pl.pallas_call(
        body, name="head_and_loss", grid=(s // ts,),
        out_shape=(act, act_bf, act_bf, jax.ShapeDtypeStruct((D, D), F32), jax.ShapeDtypeStruct((8, D), F32)),
        in_specs=[row(D), row(D), row(D), row(D),
                  _full((1, D)), _full((D, D)), _full((1, D))],
        out_specs=(row(D), row(D), row(D), _full((D, D)), _full((8, D))),
        compiler_params=_params("arbitrary"),
    )(o, gate, h1, target, hw, wgo, wf)


def gla_project_backward(dq, dk, dv, dgate, dcum, low, h1, dh2, w1, wgi_q, wgk, bgk):
    s = h1.shape[0]
    ts = ROW_TILE

    def body(dq_ref, dk_ref, dv_ref, dgate_ref, dcum_ref, low_ref, h1_ref, dh2_ref, w1_ref,
             wq_ref, wgk_ref, bgk_ref, dh1_ref, dproj_ref, ggk_ref, small_ref, wgi_ref):
        @pl.when(pl.program_id(0) == 0)
        def _():
            ggk_ref[...] = jnp.zeros_like(ggk_ref)
            small_ref[...] = jnp.zeros_like(small_ref)
            _assemble_gla_in(wq_ref, wgi_ref)

        low = low_ref[...]
        z = _nn(low, wgk_ref[...]) + bgk_ref[...]
        upper_f = _chunk_masks()[1].astype(F32)
        dlg = jnp.concatenate([_nn_exact(upper_f, dcum_ref[r0:r0 + CHUNK, :]) for r0 in range(0, ts, CHUNK)],
                              axis=0)
        dz = dlg * (1.0 / GATE_NORM) * _sigmoid(-z)
        dz_bf = _bf(dz)
        ggk_ref[...] += _tn(low, dz_bf)
        small_ref[1:2, 0:KEY_W] += jnp.sum(dz, axis=0, keepdims=True)
        dlow = _bf(_nt(dz_bf, wgk_ref[...]))
        dproj_ref[:, GLA_MAIN:] = dlow
        dn1 = _nt(dlow, wgi_ref[:, GLA_MAIN:])
        for ref, lo, hi in ((dq_ref, 0, KEY_W), (dk_ref, KEY_W, 2 * KEY_W),
                            (dv_ref, 2 * KEY_W, 2 * KEY_W + D), (dgate_ref, 2 * KEY_W + D, GLA_MAIN)):
            piece = ref[...]
            dproj_ref[:, lo:hi] = piece
            dn1 = dn1 + _nt(piece, wgi_ref[:, lo:hi])
        hv = h1_ref[...]
        r = lax.rsqrt(jnp.mean(hv * hv, axis=-1, keepdims=True) + EPS)
        hhat = hv * r
        small_ref[0:1, :] += jnp.sum(dn1 * hhat, axis=0, keepdims=True)
        dxh = dn1 * w1_ref[...]
        dh1_ref[...] = dh2_ref[...] + r * (dxh - hhat * jnp.mean(dxh * hhat, axis=-1, keepdims=True))

    row = lambda cols: pl.BlockSpec((ts, cols), lambda i: (i, 0))
    return pl.pallas_call(
        body, name="gla_project_backward", grid=(s // ts,),
        out_shape=(jax.ShapeDtypeStruct((s, D), F32), jax.ShapeDtypeStruct((s, GLA_MAIN + RANK_PAD), BF16),
                   jax.ShapeDtypeStruct((RANK_PAD, KEY_W), F32),
                   jax.ShapeDtypeStruct((8, D), F32)),
        in_specs=[row(KEY_W), row(KEY_W), row(D), row(D), row(KEY_W), row(RANK_PAD), row(D), row(D),
                  _full((1, D)), _full((N_CHIPS, D, GLA_IN_QUARTER)), _full((RANK_PAD, KEY_W)),
                  _full((1, KEY_W))],
        out_specs=(row(D), row(GLA_MAIN + RANK_PAD), _full((RANK_PAD, KEY_W)), _full((8, D))),
        scratch_shapes=[pltpu.VMEM((D, GLA_MAIN + RANK_PAD), BF16)],
        compiler_params=_params("arbitrary"),
    )(dq, dk, dv, dgate, dcum, low, h1, dh2, w1, wgi_q, wgk, bgk)


def _groups_from_quarters(a):
    return a.reshape(N_CHIPS, GROUPS, 64, GROUP_DIM).transpose(1, 0, 2, 3).reshape(GROUPS, GROUP_DIM, GROUP_DIM)


def _quarters_from_groups(a):
    return a.reshape(GROUPS, N_CHIPS, 64, GROUP_DIM).transpose(1, 0, 2, 3).reshape(N_CHIPS, GROUP_DIM, GROUP_DIM)


def local_gradients(xs, target, w0, w1, wf, wpi, gw, gb, scale, wpo, gla_quarters, wgk, bgk, hw_tiled, place):
    wgi_q, wgo_q = gla_quarters
    (h1, pooled, gt, n0), (wgi_q,) = pool_forward(xs, w0, wpi, gw, gb, scale, wpo, [wgi_q])
    (qk, v, gate, low, cum, n1), (wgo_q,) = gla_project(h1, w1, wgi_q, wgk, bgk, [wgo_q])
    wgo = wgo_q.reshape(D, D)
    o, states, scores = gla_forward(qk, v, cum)

    dh2, do, dgate, g_gla_out, small_top = head_and_loss(o, gate, h1, target, hw_tiled, wgo, wf)
    dq, dk, dv, dcum = gla_backward(qk, v, cum, do, states, scores)
    dh1, dproj, g_gk_pad, small_gla = gla_project_backward(
        dq, dk, dv, dgate, dcum, low, h1, dh2, w1, wgi_q, wgk, bgk)
    g_gla_in, _ = matmul_tn(n1, dproj, "grad_gla_in", tile_n=(GLA_MAIN + RANK_PAD) // 5)

    def chip_sums(grads, tag):
        theirs = exchange_with_sibling(grads, "exchange_with_sibling_" + tag)
        return add_halves(grads, theirs, place, "add_halves_" + tag)

    gla_sums = chip_sums([g_gla_in, g_gla_out.reshape(N_CHIPS, D // N_CHIPS, D)], "gla")
    (dx, dpool, g_pool_out, g_group_w, small_pool), gla_got = pool_backward(
        xs, dh1, pooled, gt, w0, wpi, gw, gb, scale, wpo, [b for _, b in gla_sums])
    mix_sums = chip_sums([_quarters_from_groups(g_group_w), g_pool_out.reshape(N_CHIPS, D // N_CHIPS, D)], "pool_mix")
    g_pool_in, mix_got = matmul_tn(n0, dpool, "grad_pool_in", tile_n=D // 2, by_column_tile=True,
                                   chip_sums=[b for _, b in mix_sums])

    in_sums = chip_sums([g_pool_in], "pool_in")
    in_got = scatter_to_owners([b for _, b in in_sums], "scatter_to_owners_pool_in")
    reduced, total = join_halves(
        add_parts([f for f, _ in in_sums + mix_sums + gla_sums], list(in_got) + list(mix_got) + list(gla_got),
                  place, "add_parts"),
        small_pool, small_gla, small_top, g_gk_pad)
    return dx, reduced, total


def kernel(x, norm_w, pool_in_w, pool_group_w, pool_group_b, pool_scale, pool_out_w, gla_in_w, gla_gk_w, gla_gk_b, gla_head_norm_w, gla_out_w, final_norm_w, loss_target, m_norm_w, m_pool_in_w, m_pool_group_w, m_pool_group_b, m_pool_scale, m_pool_out_w, m_gla_in_w, m_gla_gk_w, m_gla_gk_b, m_gla_head_norm_w, m_gla_out_w, m_final_norm_w, v_norm_w, v_pool_in_w, v_pool_group_w, v_pool_group_b, v_pool_scale, v_pool_out_w, v_gla_in_w, v_gla_gk_w, v_gla_gk_b, v_gla_head_norm_w, v_gla_out_w, v_final_norm_w):
    xs = x[0]
    target = loss_target[0]
    q_chip = 2 * lax.axis_index("x") + lax.axis_index("y")
    place = jnp.stack([lax.axis_index("c"), q_chip]).astype(jnp.int32)

    (wpi, gw_q, wpo_q, wgi_q, wgo_q), small_all = allgather_weights(
        [pool_in_w[0], pool_group_w[0].reshape(GROUP_DIM, GROUP_DIM), pool_out_w[0], gla_in_w[0], gla_out_w[0]],
        exchange=(True, True, True, False, False),
        smalls=[gla_gk_b, gla_head_norm_w, pool_group_b[0], gla_gk_w[0]])
    gw = _groups_from_quarters(gw_q)
    wpo = wpo_q.reshape(D, D)
    small_all = small_all[0::2]
    bgk = small_all[:, 0, :].reshape(1, KEY_W)
    hw = small_all[:, 1, 0:64].reshape(1, HEAD_V)
    gb = small_all[:, 2:2 + GROUPS, 0:64].transpose(1, 0, 2).reshape(1, D)
    wgk16 = small_all[:, 8:8 + GATE_RANK, :].transpose(1, 0, 2).reshape(GATE_RANK, KEY_W)
    wgk = _bf(jnp.pad(wgk16, ((0, RANK_PAD - GATE_RANK), (0, 0))))
    hw_tiled = jnp.tile(hw, (1, HEADS))

    w0 = norm_w[0:1]
    w1 = norm_w[1:2]
    wf = final_norm_w.reshape(1, D)

    dx, reduced, total = local_gradients(
        xs, target, w0, w1, wf, wpi, gw, gb, pool_scale, wpo, [wgi_q, wgo_q], wgk, bgk, hw_tiled, place)
    r_pool_in, r_group_w, r_pool_out, r_gla_in, r_gla_out = reduced
    r_group_w = r_group_w.reshape(GROUPS, 64, GROUP_DIM)

    loss = total[7, 0]
    g_norm = jnp.stack([total[0], total[3]])
    g_scale = total[1:2]
    g_final = total[5]
    pick = lambda full, width: lax.dynamic_slice_in_dim(full, q_chip * width, width, axis=-1)
    g_gk_b = pick(total[4:5, 0:KEY_W], 128)
    g_hnw = pick(total[6:7, 0:HEAD_V], 64)
    g_group_b = pick(total[2].reshape(GROUPS, GROUP_DIM), 64)[None]
    g_gk_w = pick(total[8:16].reshape(GATE_RANK, KEY_W), 128)[None]

    turn = lambda a: jnp.transpose(a, (2, 0, 1))
    back = lambda a: jnp.transpose(a, (1, 2, 0))
    as2d = lambda a, w: a.reshape(-1, w.shape[-1])
    big_names = ("pool_in_w", "pool_group_w", "pool_out_w", "gla_in_w", "gla_out_w")
    big_args = [(pool_in_w, r_pool_in[None], m_pool_in_w, v_pool_in_w),
                (pool_group_w, r_group_w[None], m_pool_group_w, v_pool_group_w),
                (pool_out_w, r_pool_out[None], m_pool_out_w, v_pool_out_w),
                (gla_in_w, r_gla_in[None], m_gla_in_w, v_gla_in_w),
                (gla_out_w, r_gla_out[None], m_gla_out_w, v_gla_out_w)]
    to_kernel = lambda n, a, w: turn(a) if n == "gla_in_w" else as2d(a, w)
    from_kernel = lambda n, a, w: back(a) if n == "gla_in_w" else a.reshape(w.shape)
    big_in = [tuple(to_kernel(n, a, p[0]) for a in p) for n, p in zip(big_names, big_args)]
    big_out = adamw(big_in, "adamw")
    big = {n: (from_kernel(n, i[1], p[0]),) + tuple(from_kernel(n, o, p[0]) for o in out)
           for n, p, i, out in zip(big_names, big_args, big_in, big_out)}

    small_names = ("norm_w", "pool_group_b", "pool_scale", "gla_gk_w", "gla_gk_b", "gla_head_norm_w",
                   "final_norm_w")
    small_args = [(norm_w, g_norm, m_norm_w, v_norm_w),
                  (pool_group_b, g_group_b, m_pool_group_b, v_pool_group_b),
                  (pool_scale, g_scale, m_pool_scale, v_pool_scale),
                  (gla_gk_w, g_gk_w, m_gla_gk_w, v_gla_gk_w),
                  (gla_gk_b, g_gk_b, m_gla_gk_b, v_gla_gk_b),
                  (gla_head_norm_w, g_hnw, m_gla_head_norm_w, v_gla_head_norm_w),
                  (final_norm_w, g_final, m_final_norm_w, v_final_norm_w)]
    small_out = adamw_small([tuple(as2d(a, p[0]) for a in p) for p in small_args])
    small = {n: (p[1].reshape(p[0].shape),) + tuple(o.reshape(p[0].shape) for o in out)
             for n, p, out in zip(small_names, small_args, small_out)}
    results = [
        small["norm_w"],
        big["pool_in_w"],
        big["pool_group_w"],
        small["pool_group_b"],
        small["pool_scale"],
        big["pool_out_w"],
        big["gla_in_w"],
        small["gla_gk_w"],
        small["gla_gk_b"],
        small["gla_head_norm_w"],
        big["gla_out_w"],
        small["final_norm_w"],
    ]
    grads, deltas, new_m, new_v = zip(*results)
    return (loss, dx[None], *grads, *deltas, *new_m, *new_v)
```

```python
import jax
import jax.numpy as jnp
from jax import lax
from jax.experimental import pallas as pl
from jax.experimental.pallas import tpu as pltpu

F32 = jnp.float32
BF16 = jnp.bfloat16
MESH = pl.DeviceIdType.MESH

D = 1024
POOL_WINDOWS = (2, 4, 8, 16)
GROUPS = 4
GROUP_DIM = 256
HEADS = 4
HEAD_K = 128
HEAD_V = 256
KEY_W = 512
CHUNK = 64
GATE_RANK = 16
GATE_NORM = 16.0
GLA_IN = 3088
GLA_MAIN = 3072
RANK_PAD = 128
EPS = 1e-6
HALO = 32

ADAM_LR = 0.001
ADAM_B1 = 0.9
ADAM_B2 = 0.999
ADAM_EPS = 1e-08
ADAM_WD = 0.01
ADAM_STEP = 10

N_CHIPS = 4
N_DEV = 8
GLA_IN_QUARTER = GLA_IN // N_CHIPS

VMEM_LIMIT = 56 * 1024 * 1024


def _nn(a, b):
    return lax.dot_general(a, b, (((1,), (0,)), ((), ())), preferred_element_type=F32)


def _nt(a, b):
    return lax.dot_general(a, b, (((1,), (1,)), ((), ())), preferred_element_type=F32)


def _tn(a, b):
    return lax.dot_general(a, b, (((0,), (0,)), ((), ())), preferred_element_type=F32)


def _nn_exact(a, b):
    return lax.dot_general(a, b, (((1,), (0,)), ((), ())), preferred_element_type=F32,
                           precision=lax.Precision.HIGHEST)


def _bf(a):
    return a.astype(BF16)


def _params(*sem):
    return pltpu.CompilerParams(dimension_semantics=sem, vmem_limit_bytes=VMEM_LIMIT)


def _full(shape):
    return pl.BlockSpec(shape, lambda i: (0,) * len(shape))


def _position():
    return lax.axis_index("x"), lax.axis_index("y"), lax.axis_index("c")


def _gather_small(in_ref, all_ref, send_sems, recv_sems, local_sem):
    x, y, c = _position()
    me = 4 * x + 2 * y + c
    mine = pltpu.make_async_copy(in_ref, all_ref.at[me], local_sem)
    mine.start()
    sends = []
    for k in range(N_DEV - 1):
        fx, fy, fc = (k + 1) >> 2 & 1, (k + 1) >> 1 & 1, (k + 1) & 1
        cp = pltpu.make_async_remote_copy(
            src_ref=in_ref, dst_ref=all_ref.at[me],
            send_sem=send_sems.at[k], recv_sem=recv_sems.at[k],
            device_id=(x ^ fx, y ^ fy, c ^ fc), device_id_type=MESH)
        cp.start()
        sends.append(cp)
    def wait():
        for k in range(N_DEV - 1):
            fx, fy, fc = (k + 1) >> 2 & 1, (k + 1) >> 1 & 1, (k + 1) & 1
            src_dev = 4 * (x ^ fx) + 2 * (y ^ fy) + (c ^ fc)
            pltpu.make_async_remote_copy(
                src_ref=in_ref, dst_ref=all_ref.at[src_dev],
                send_sem=send_sems.at[k], recv_sem=recv_sems.at[k],
                device_id=(x, y, c), device_id_type=MESH).wait_recv()
        for cp in sends:
            cp.wait_send()
        mine.wait()

    return wait


SMALL_SEMS = [pltpu.SemaphoreType.DMA((N_DEV - 1,)), pltpu.SemaphoreType.DMA((N_DEV - 1,)),
              pltpu.SemaphoreType.DMA]
VMEM_SPEC = pl.BlockSpec(memory_space=pltpu.VMEM)


def _other_chips(x, y):
    return [(1 - x, y), (x, 1 - y), (1 - x, 1 - y)]


def _any_specs(n):
    return [pl.BlockSpec(memory_space=pl.ANY)] * n


def _halves(rows, c):
    half = rows // 2
    return pl.ds(c * half, half), pl.ds((1 - c) * half, half)


CAST_ROWS = 256


def _gather_copy(out_ref, send_sems, recv_sems, k, quarter, half, to, src=None):
    dst = out_ref.at[quarter, half]
    return pltpu.make_async_remote_copy(
        src_ref=dst if src is None else src, dst_ref=dst,
        send_sem=send_sems.at[k], recv_sem=recv_sems.at[k], device_id=to, device_id_type=MESH)


SMALL_IN_ROWS = 24


def allgather_weights(quarters, exchange, smalls):
    n = len(quarters)
    shapes = [w.shape for w in quarters]
    moved = [i for i in range(n) if exchange[i]]

    def body(*refs):
        w_refs, (gkb_ref, hnw_ref, gb_ref, gkw_ref) = refs[:n], refs[n:n + 4]
        out_refs, small_all_ref = refs[n + 4:2 * n + 4], refs[2 * n + 4]
        refs = refs[2 * n + 5:]
        f32_bufs, bf_bufs = refs[:n], refs[n:2 * n]
        send_sems, recv_sems, local_sems, small_ref = refs[2 * n:2 * n + 4]
        small_ref[...] = jnp.zeros_like(small_ref)
        small_ref[0:1, :] = gkb_ref[...]
        small_ref[1:2, 0:64] = hnw_ref[...]
        small_ref[2:2 + GROUPS, 0:64] = gb_ref[...]
        small_ref[8:8 + GATE_RANK, :] = gkw_ref[...]
        wait_small = _gather_small(small_ref, small_all_ref, *refs[2 * n + 4:])
        x, y, c = _position()
        q = 2 * x + y
        sibling = (x, y, 1 - c)
        chips = _other_chips(x, y)

        def copy(k, i, quarter, half, to, src=None):
            return _gather_copy(out_refs[i], send_sems, recv_sems, k * n + i, quarter, half, to, src)

        loads = [pltpu.make_async_copy(w_refs[i], f32_bufs[i], local_sems.at[i]) for i in range(n)]
        for cp in loads:
            cp.start()
        keeps, sends = [], []
        for i in range(n):
            loads[i].wait()
            for r0 in range(0, shapes[i][0], CAST_ROWS):
                bf_bufs[i][r0:r0 + CAST_ROWS, :] = _bf(f32_bufs[i][r0:r0 + CAST_ROWS, :])
            keep = pltpu.make_async_copy(bf_bufs[i], out_refs[i].at[q], local_sems.at[n + i])
            keep.start()
            keeps.append(keep)
            if not exchange[i]:
                continue
            mine, _ = _halves(shapes[i][0], c)
            for j, chip in enumerate(chips):
                cp = copy(j, i, q, mine, (*chip, c), src=bf_bufs[i].at[mine])
                cp.start()
                sends.append(cp)
        for j, chip in enumerate(chips):
            qj = 2 * chip[0] + chip[1]
            for i in moved:
                mine, _ = _halves(shapes[i][0], c)
                copy(j, i, qj, mine, (x, y, c)).wait_recv()
                cp = copy(3 + j, i, qj, mine, sibling)
                cp.start()
                sends.append(cp)
        for j, chip in enumerate(chips):
            qj = 2 * chip[0] + chip[1]
            for i in moved:
                _, other = _halves(shapes[i][0], c)
                copy(3 + j, i, qj, other, (x, y, c)).wait_recv()
        wait_small()
        for cp in sends:
            cp.wait_send()
        for cp in keeps:
            cp.wait()

    outs = pl.pallas_call(
        body, name="allgather_weights",
        out_shape=[jax.ShapeDtypeStruct((N_CHIPS, *s), BF16) for s in shapes]
                  + [jax.ShapeDtypeStruct((N_DEV, SMALL_IN_ROWS, 128), F32)],
        in_specs=_any_specs(n) + [VMEM_SPEC] * 4, out_specs=_any_specs(n) + [VMEM_SPEC],
        scratch_shapes=([pltpu.VMEM(s, F32) for s in shapes] + [pltpu.VMEM(s, BF16) for s in shapes]
                        + [pltpu.SemaphoreType.DMA((6 * n,)), pltpu.SemaphoreType.DMA((6 * n,)),
                           pltpu.SemaphoreType.DMA((2 * n,)), pltpu.VMEM((SMALL_IN_ROWS, 128), F32)] + SMALL_SEMS),
        compiler_params=pltpu.CompilerParams(vmem_limit_bytes=VMEM_LIMIT),
    )(*quarters, *smalls)
    return outs[:n], outs[n]


def _scatter_copies(b_refs, got_refs, send_sems, recv_sems):
    n = len(b_refs)
    x, y, c = _position()
    copies = []
    for j, chip in enumerate(_other_chips(x, y)):
        qj = 2 * chip[0] + chip[1]
        for i in range(n):
            copies.append(pltpu.make_async_remote_copy(
                src_ref=b_refs[i].at[qj], dst_ref=got_refs[i].at[j],
                send_sem=send_sems.at[j * n + i], recv_sem=recv_sems.at[j * n + i],
                device_id=(*chip, c), device_id_type=MESH))
    return copies


def _scatter_shapes(chip_sums):
    return [jax.ShapeDtypeStruct((N_CHIPS - 1, *b.shape[1:]), BF16) for b in chip_sums]


def scatter_to_owners(chip_sums, name):
    n = len(chip_sums)

    def body(*refs):
        copies = _scatter_copies(refs[:n], refs[n:2 * n], *refs[2 * n:])
        for cp in copies:
            cp.start()
        for cp in copies:
            cp.wait()

    return pl.pallas_call(
        body, name=name,
        out_shape=_scatter_shapes(chip_sums),
        in_specs=_any_specs(n), out_specs=_any_specs(n),
        scratch_shapes=[pltpu.SemaphoreType.DMA((3 * n,)), pltpu.SemaphoreType.DMA((3 * n,))],
    )(*chip_sums)


SMALL_SUM_ROWS = 16


def join_halves(reduced, small_pool, small_gla, small_top, g_gk_pad):
    n = len(reduced)

    def body(*refs):
        pool_ref, gla_ref, top_ref, gk_ref = refs[n:n + 4]
        buf_refs, total_ref = refs[n + 4:2 * n + 4], refs[2 * n + 4]
        send_sems, recv_sems, all_ref, small_ref = refs[2 * n + 5:2 * n + 9]
        small_ref[0:3, :] = pool_ref[0:3, :]
        small_ref[3:5, :] = gla_ref[0:2, :]
        small_ref[5:8, :] = top_ref[0:3, :]
        for r in range(GATE_RANK):
            small_ref[8 + r // 2:9 + r // 2, (r % 2) * KEY_W:(r % 2 + 1) * KEY_W] = gk_ref[r:r + 1, :]
        x, y, c = _position()
        copies = []
        for i in range(n):
            mine, _ = _halves(buf_refs[i].shape[0], c)
            cp = pltpu.make_async_remote_copy(
                src_ref=buf_refs[i].at[mine], dst_ref=buf_refs[i].at[mine],
                send_sem=send_sems.at[i], recv_sem=recv_sems.at[i],
                device_id=(x, y, 1 - c), device_id_type=MESH)
            cp.start()
            copies.append(cp)
        _gather_small(small_ref, all_ref, *refs[2 * n + 9:])()
        total = all_ref[0]
        for dev in range(1, N_DEV):
            total = total + all_ref[dev]
        total_ref[...] = total
        for cp in copies:
            cp.wait()

    outs = pl.pallas_call(
        body, name="join_halves",
        out_shape=[jax.ShapeDtypeStruct(r.shape, F32) for r in reduced]
                  + [jax.ShapeDtypeStruct((SMALL_SUM_ROWS, D), F32)],
        in_specs=_any_specs(n) + [VMEM_SPEC] * 4, out_specs=_any_specs(n) + [VMEM_SPEC],
        input_output_aliases={i: i for i in range(n)},
        scratch_shapes=[pltpu.SemaphoreType.DMA((n,)), pltpu.SemaphoreType.DMA((n,)),
                        pltpu.VMEM((N_DEV, SMALL_SUM_ROWS, D), F32), pltpu.VMEM((SMALL_SUM_ROWS, D), F32)]
                       + SMALL_SEMS,
    )(*reduced, small_pool, small_gla, small_top, g_gk_pad)
    return outs[:n], outs[n]


ADD_ROWS = 512
ADD_HALVES_ROWS = 256


def _spans(counts):
    starts, total = [], 0
    for count in counts:
        starts.append(total)
        total += count
    return starts, total


def _local_step(t, start, count):
    return jnp.clip(t - start, 0, count - 1)


def add_halves(grads, place, name):
    n = len(grads)
    whole = [len(g.shape) == 2 for g in grads]
    halves = [g.shape[-2] // 2 for g in grads]
    cols = [GLA_IN_QUARTER if w else g.shape[-1] for g, w in zip(grads, whole)]
    rbs = [min(ADD_HALVES_ROWS, h) for h in halves]
    counts = [h // rb for h, rb in zip(halves, rbs)]
    starts, total = _spans(counts)
    half_shapes = [(*g.shape[:-2], h, g.shape[-1]) for g, h in zip(grads, halves)]

    def rows_of(ref, i, start):
        return ref.at[pl.ds(start, rbs[i])] if whole[i] else ref.at[:, pl.ds(start, rbs[i])]

    def body(place_ref, *refs):
        a_refs, src_refs = refs[:n], refs[n:2 * n]
        f_refs, h_refs = refs[2 * n:3 * n], refs[3 * n:4 * n]
        their_refs, (send_sems, recv_sems) = refs[4 * n:5 * n], refs[5 * n:]
        t = pl.program_id(0)
        q = place_ref[1]
        x, y, c = _position()
        copies = [[pltpu.make_async_remote_copy(
            src_ref=rows_of(src_refs[i], i, (1 - c) * halves[i] + k * rbs[i]),
            dst_ref=rows_of(their_refs[i], i, k * rbs[i]),
            send_sem=send_sems.at[starts[i] + k], recv_sem=recv_sems.at[starts[i] + k],
            device_id=(x, y, 1 - c), device_id_type=MESH) for k in range(counts[i])] for i in range(n)]

        @pl.when(t == 0)
        def _():
            for of_matrix in copies:
                for cp in of_matrix:
                    cp.start()

        for i in range(n):
            for k in range(counts[i]):
                @pl.when(t == starts[i] + k)
                def _(i=i, k=k):
                    copies[i][k].wait_recv()
                    b_ref = rows_of(their_refs[i], i, k * rbs[i])
                    if not whole[i]:
                        h_refs[i][...] = _bf(a_refs[i][...] + b_ref[...])
                        f_refs[i][...] = a_refs[i][q] + b_ref[q]
                        return
                    total_i = a_refs[i][...] + b_ref[...]
                    for k4 in range(N_CHIPS):
                        piece = total_i[:, k4 * cols[i]:(k4 + 1) * cols[i]]
                        h_refs[i][k4] = _bf(piece)

                        @pl.when(q == k4)
                        def _():
                            f_refs[i][...] = piece

        @pl.when(t == total - 1)
        def _():
            for of_matrix in copies:
                for cp in of_matrix:
                    cp.wait_send()

    def specs(i):
        step = lambda t: _local_step(t, starts[i], counts[i])
        by_quarter = (N_CHIPS, rbs[i], cols[i])
        block = (rbs[i], grads[i].shape[-1]) if whole[i] else by_quarter
        lead = () if whole[i] else (0,)
        mine = pl.BlockSpec(block, lambda t, place: (*lead, place[0] * counts[i] + step(t), 0))
        sums = pl.BlockSpec(by_quarter, lambda t, place: (0, step(t), 0))
        own = pl.BlockSpec(by_quarter[1:], lambda t, place: (step(t), 0))
        return mine, own, sums

    all_specs = [specs(i) for i in range(n)]
    outs = pl.pallas_call(
        body, name=name,
        grid_spec=pltpu.PrefetchScalarGridSpec(
            num_scalar_prefetch=1, grid=(total,),
            in_specs=[sp[0] for sp in all_specs] + _any_specs(n),
            out_specs=[sp[1] for sp in all_specs] + [sp[2] for sp in all_specs],
            scratch_shapes=[pltpu.VMEM(sh, F32) for sh in half_shapes]
                           + [pltpu.SemaphoreType.DMA((total,)), pltpu.SemaphoreType.DMA((total,))]),
        out_shape=[jax.ShapeDtypeStruct((h, cl), F32) for h, cl in zip(halves, cols)]
                  + [jax.ShapeDtypeStruct((N_CHIPS, h, cl), BF16) for h, cl in zip(halves, cols)],
        compiler_params=_params("arbitrary"),
    )(place, *grads, *grads)
    return list(zip(outs[:n], outs[n:]))


def add_parts(owns, gots, place, name):
    n = len(owns)
    shapes = [g.shape for g in gots]
    rbs = [min(ADD_ROWS, sh[1]) for sh in shapes]
    counts = [sh[1] // rb for sh, rb in zip(shapes, rbs)]
    starts, total = _spans(counts)

    def body(place_ref, *refs):
        o_refs, g_refs, out_refs = refs[:n], refs[n:2 * n], refs[2 * n:]
        t = pl.program_id(0)
        for i in range(n):
            @pl.when((t >= starts[i]) & (t < starts[i] + counts[i]))
            def _(i=i):
                total_i = o_refs[i][...]
                for j in range(N_CHIPS - 1):
                    total_i = total_i + g_refs[i][j].astype(F32)
                out_refs[i][...] = total_i

    def specs(i):
        rb, cols = rbs[i], shapes[i][2]
        step = lambda t: _local_step(t, starts[i], counts[i])
        return (pl.BlockSpec((rb, cols), lambda t, place: (step(t), 0)),
                pl.BlockSpec((N_CHIPS - 1, rb, cols), lambda t, place: (0, step(t), 0)),
                pl.BlockSpec((rb, cols), lambda t, place: (place[0] * counts[i] + step(t), 0)))

    all_specs = [specs(i) for i in range(n)]
    return pl.pallas_call(
        body, name=name,
        grid_spec=pltpu.PrefetchScalarGridSpec(
            num_scalar_prefetch=1, grid=(total,),
            in_specs=[sp[0] for sp in all_specs] + [sp[1] for sp in all_specs],
            out_specs=[sp[2] for sp in all_specs]),
        out_shape=[jax.ShapeDtypeStruct((2 * sh[1], sh[2]), F32) for sh in shapes],
        compiler_params=_params("arbitrary"),
    )(place, *owns, *gots)


def _adam_math(w, g, m, v):
    m = ADAM_B1 * m + (1.0 - ADAM_B1) * g
    v = ADAM_B2 * v + (1.0 - ADAM_B2) * (g * g)
    m_hat = m / (1.0 - ADAM_B1 ** ADAM_STEP)
    v_hat = v / (1.0 - ADAM_B2 ** ADAM_STEP)
    delta = -ADAM_LR * (m_hat / (jnp.sqrt(v_hat) + ADAM_EPS) + ADAM_WD * w)
    return delta, m, v


ADAM_BLOCK_BYTES = 2 ** 19
ADAM_MOST_STEPS = 8


def adamw(params, name):
    n = len(params)
    shapes = [p[0].shape for p in params]

    def tile_rows(shape):
        rows, cols = shape[0], shape[-1]
        aligned = 1 if len(shape) == 3 else 8
        divisors = [t for t in range(aligned, rows + 1, aligned) if rows % t == 0]
        tile = max(t for t in divisors if t * cols * 4 <= ADAM_BLOCK_BYTES)
        if rows // tile > ADAM_MOST_STEPS:
            tile = min(t for t in divisors if rows // t <= ADAM_MOST_STEPS)
        return tile

    tiles = [tile_rows(sh) for sh in shapes]
    counts = [sh[0] // tl for sh, tl in zip(shapes, tiles)]
    starts, total = _spans(counts)

    def body(*refs):
        ins, outs = refs[:4 * n], refs[4 * n:]
        t = pl.program_id(0)
        for i in range(n):
            @pl.when((t >= starts[i]) & (t < starts[i] + counts[i]))
            def _(i=i):
                w_ref, g_ref, m_ref, v_ref = ins[4 * i:4 * i + 4]
                d, nm, nv = _adam_math(w_ref[...], g_ref[...], m_ref[...], v_ref[...])
                outs[3 * i][...] = d
                outs[3 * i + 1][...] = nm
                outs[3 * i + 2][...] = nv

    def spec(i):
        block = (tiles[i],) + shapes[i][1:]
        zeros = (0,) * (len(block) - 1)
        return pl.BlockSpec(block, lambda t: (_local_step(t, starts[i], counts[i]),) + zeros)

    outs = pl.pallas_call(
        body, name=name, grid=(total,),
        out_shape=[jax.ShapeDtypeStruct(sh, F32) for sh in shapes for _ in range(3)],
        in_specs=[spec(i) for i in range(n) for _ in range(4)],
        out_specs=[spec(i) for i in range(n) for _ in range(3)],
        compiler_params=_params("arbitrary"),
    )(*[a for p in params for a in p])
    return [tuple(outs[3 * i:3 * i + 3]) for i in range(n)]


def adamw_small(params):
    n = len(params)

    def body(*refs):
        ins, outs = refs[:4 * n], refs[4 * n:]
        for k in range(n):
            w_ref, g_ref, m_ref, v_ref = ins[4 * k:4 * k + 4]
            d, nm, nv = _adam_math(w_ref[...], g_ref[...], m_ref[...], v_ref[...])
            outs[3 * k][...] = d
            outs[3 * k + 1][...] = nm
            outs[3 * k + 2][...] = nv

    flat = [a for p in params for a in p]
    outs = pl.pallas_call(
        body, name="adamw_small",
        out_shape=[jax.ShapeDtypeStruct(p[0].shape, F32) for p in params for _ in range(3)],
        in_specs=[VMEM_SPEC] * (4 * n), out_specs=[VMEM_SPEC] * (3 * n),
    )(*flat)
    return [tuple(outs[3 * k:3 * k + 3]) for k in range(n)]


def matmul_tn(a, b, name, tile_n, by_column_tile=False, chip_sums=()):
    s, m = a.shape
    n = b.shape[1]
    n_sums = len(chip_sums)
    steps = n // tile_n
    if by_column_tile:
        out_shape = jax.ShapeDtypeStruct((steps, m, tile_n), F32)
        out_spec = pl.BlockSpec((None, m, tile_n), lambda j: (j, 0, 0))
    else:
        out_shape = jax.ShapeDtypeStruct((m, n), F32)
        out_spec = pl.BlockSpec((m, tile_n), lambda j: (0, j))

    def body(a_ref, b_ref, *rest):
        sum_refs, out_ref, got_refs = rest[:n_sums], rest[n_sums], rest[n_sums + 1:2 * n_sums + 1]
        j = pl.program_id(0)
        copies = _scatter_copies(sum_refs, got_refs, *rest[2 * n_sums + 1:]) if n_sums else []

        @pl.when(j == 0)
        def _():
            for cp in copies:
                cp.start()

        out_ref[...] = _tn(a_ref[...], b_ref[...])

        @pl.when(j == steps - 1)
        def _():
            for cp in copies:
                cp.wait()

    outs = pl.pallas_call(
        body, name=name, grid=(steps,),
        out_shape=[out_shape] + _scatter_shapes(chip_sums),
        in_specs=[_full((s, m)), pl.BlockSpec((s, tile_n), lambda j: (0, j))] + _any_specs(n_sums),
        out_specs=[out_spec] + _any_specs(n_sums),
        scratch_shapes=[pltpu.SemaphoreType.DMA((3 * n_sums,)), pltpu.SemaphoreType.DMA((3 * n_sums,))]
                       if n_sums else [],
        compiler_params=_params("arbitrary"),
    )(a, b, *chip_sums)
    return outs[0], outs[1:]


ROW_TILE = 512


def _row_index(tile, rows):
    return tile * rows + lax.broadcasted_iota(jnp.int32, (rows, 1), 0)


def _inverse_counts(t_glob):
    return [1.0 / jnp.minimum(t_glob + 1, w).astype(F32) for w in POOL_WINDOWS]


def _sigmoid(z):
    return 1.0 / (1.0 + jnp.exp(-z))


def _trailing_sums(src, tmp, cols, window, rows):
    bufs = (src, tmp)
    span, level, start = 1, 0, 0
    while span < window:
        start += 8
        a, b = bufs[level % 2], bufs[(level + 1) % 2]
        n = HALO + rows - start
        b[start:start + n, cols] = a[start:start + n, cols] + a[start - span:start - span + n, cols]
        span, level = 2 * span, level + 1
    return bufs[level % 2][HALO:HALO + rows, cols]


def _leading_sums(src, tmp, cols, window, rows):
    bufs = (src, tmp)
    span, level, n = 1, 0, rows + HALO
    while span < window:
        n -= 8
        a, b = bufs[level % 2], bufs[(level + 1) % 2]
        b[0:n, cols] = a[0:n, cols] + a[span:span + n, cols]
        span, level = 2 * span, level + 1
    return bufs[level % 2][0:rows, cols]


def gather_in_background(step, last, out_refs, send_sems, recv_sems, finish):
    n = len(out_refs)
    x, y, c = _position()
    q = 2 * x + y
    chips = _other_chips(x, y)

    def copy(k, i, quarter, half, to):
        return _gather_copy(out_refs[i], send_sems, recv_sems, k * n + i, quarter, half, to)

    if not finish:
        @pl.when(step == 0)
        def _():
            for i in range(n):
                mine, _ = _halves(out_refs[i].shape[1], c)
                for j, chip in enumerate(chips):
                    copy(j, i, q, mine, (*chip, c)).start()

        @pl.when(step == last)
        def _():
            for j, chip in enumerate(chips):
                qj = 2 * chip[0] + chip[1]
                for i in range(n):
                    mine, _ = _halves(out_refs[i].shape[1], c)
                    copy(j, i, qj, mine, (x, y, c)).wait_recv()
                    copy(3 + j, i, qj, mine, (x, y, 1 - c)).start()
        return

    @pl.when(step == last)
    def _():
        for j, chip in enumerate(chips):
            qj = 2 * chip[0] + chip[1]
            for i in range(n):
                mine, other = _halves(out_refs[i].shape[1], c)
                copy(3 + j, i, qj, other, (x, y, c)).wait_recv()
                copy(j, i, q, mine, (x, y, c)).wait_send()
                copy(3 + j, i, qj, mine, (x, y, c)).wait_send()


def pool_forward(x, w0, wpi, gw, gb, scale, wpo, later):
    s = x.shape[0]
    ts = ROW_TILE
    nt = s // ts
    assert nt >= 2
    n_later = len(later)

    def body(x_ref, w0_ref, wpi_ref, gw_ref, gb_ref, sc_ref, wpo_ref, *rest):
        rest = rest[n_later:]
        h1_ref, pooled_ref, gt_ref, n0_ref = rest[:4]
        later_refs = rest[4:4 + n_later]
        ubuf, tbuf, hist, send_sems, recv_sems = rest[4 + n_later:]
        i = pl.program_id(0)
        gather_in_background(i, nt - 1, later_refs, send_sems, recv_sems, finish=False)
        xv = x_ref[...]
        r = lax.rsqrt(jnp.mean(xv * xv, axis=-1, keepdims=True) + EPS)
        n0 = _bf(xv * r * w0_ref[...])
        n0_ref[...] = n0
        u = jnp.concatenate([_nn(n0, wpi_ref[0]), _nn(n0, wpi_ref[1])], axis=-1)
        gt = jnp.concatenate([_nn(n0, wpi_ref[2]), _nn(n0, wpi_ref[3])], axis=-1)
        gt_ref[...] = gt

        @pl.when(i == 0)
        def _():
            hist[...] = jnp.zeros_like(hist)

        ubuf[0:HALO, :] = hist[...]
        ubuf[HALO:HALO + ts, :] = u
        hist[...] = u[ts - HALO:, :]
        inv = _inverse_counts(_row_index(i, ts))
        mixed = []
        for g, w in enumerate(POOL_WINDOWS):
            cols = slice(g * GROUP_DIM, (g + 1) * GROUP_DIM)
            pooled = _bf(_trailing_sums(ubuf, tbuf, cols, w, ts) * inv[g] - u[:, cols])
            pooled_ref[:, cols] = pooled
            mixed.append(_nn(pooled, gw_ref[g]))
        mixed = jnp.concatenate(mixed, axis=-1) + gb_ref[...]
        y = mixed * sc_ref[...] * (gt * _sigmoid(gt))
        h1_ref[...] = xv + _nn(_bf(y), wpo_ref[...])
        gather_in_background(i, nt - 1, later_refs, send_sems, recv_sems, finish=True)

    row = lambda cols: pl.BlockSpec((ts, cols), lambda i: (i, 0))
    outs = pl.pallas_call(
        body, name="pool_forward", grid=(nt,),
        out_shape=[jax.ShapeDtypeStruct((s, D), F32), jax.ShapeDtypeStruct((s, D), BF16),
                   jax.ShapeDtypeStruct((s, D), F32), jax.ShapeDtypeStruct((s, D), BF16)]
                  + [jax.ShapeDtypeStruct(a.shape, a.dtype) for a in later],
        in_specs=[row(D), _full((1, D)), _full((N_CHIPS, D, D // 2)), _full((GROUPS, GROUP_DIM, GROUP_DIM)),
                  _full((1, D)), _full((1, D)), _full((D, D))] + _any_specs(n_later),
        out_specs=[row(D), row(D), row(D), row(D)] + _any_specs(n_later),
        input_output_aliases={7 + k: 4 + k for k in range(n_later)},
        scratch_shapes=[pltpu.VMEM((HALO + ts, D), F32), pltpu.VMEM((HALO + ts, D), F32),
                        pltpu.VMEM((HALO, D), F32),
                        pltpu.SemaphoreType.DMA((6 * n_later,)), pltpu.SemaphoreType.DMA((6 * n_later,))],
        compiler_params=_params("arbitrary"),
    )(x, w0, wpi, gw, gb, scale, wpo, *later)
    return outs[:4], outs[4:]


def pool_backward(x, dh1, pooled, gt, w0, wpi, gw, gb, scale, wpo, chip_sums):
    s = x.shape[0]
    ts = ROW_TILE
    nt = s // ts
    n_sums = len(chip_sums)

    def body(x_ref, dh1_ref, pooled_ref, gt_ref, w0_ref, wpi_ref, gw_ref, gb_ref, sc_ref, wpo_ref, *rest):
        sum_refs, rest = rest[:n_sums], rest[n_sums:]
        dx_ref, dproj_ref, gpo_ref, ggw_ref, small_ref = rest[:5]
        got_refs = rest[5:5 + n_sums]
        ebuf, tbuf, ahead, send_sems, recv_sems = rest[5 + n_sums:]
        i = pl.program_id(0)
        copies = _scatter_copies(sum_refs, got_refs, send_sems, recv_sems)

        @pl.when(i == 0)
        def _():
            for cp in copies:
                cp.start()

        @pl.when(i == 0)
        def _():
            gpo_ref[...] = jnp.zeros_like(gpo_ref)
            ggw_ref[...] = jnp.zeros_like(ggw_ref)
            small_ref[...] = jnp.zeros_like(small_ref)
            ahead[...] = jnp.zeros_like(ahead)

        dh1 = dh1_ref[...]
        dh1_bf = _bf(dh1)
        gt = gt_ref[...]
        sc = sc_ref[...]
        dy = _nt(dh1_bf, wpo_ref[...])
        pooled_bf = []
        mixed = []
        for g in range(GROUPS):
            cols = slice(g * GROUP_DIM, (g + 1) * GROUP_DIM)
            pb = pooled_ref[:, cols]
            pooled_bf.append(pb)
            mixed.append(_nn(pb, gw_ref[g]))
        mixed = jnp.concatenate(mixed, axis=-1) + gb_ref[...]
        sg = _sigmoid(gt)
        silu = gt * sg
        gpo_ref[...] += _tn(_bf(mixed * sc * silu), dh1_bf)
        dmixed = dy * sc * silu
        dgt = dy * mixed * sc * (sg * (1.0 + gt * (1.0 - sg)))
        dproj_ref[:, D:] = _bf(dgt)
        small_ref[1:2, :] += jnp.sum(dy * mixed * silu, axis=0, keepdims=True)
        small_ref[2:3, :] += jnp.sum(dmixed, axis=0, keepdims=True)

        inv = _inverse_counts(_row_index(nt - 1 - i, ts))
        ebuf[ts:ts + HALO, :] = ahead[...]
        dpooled = []
        for g in range(GROUPS):
            cols = slice(g * GROUP_DIM, (g + 1) * GROUP_DIM)
            dm = _bf(dmixed[:, cols])
            ggw_ref[g] += _tn(pooled_bf[g], dm)
            dp = _nt(dm, gw_ref[g])
            dpooled.append(dp)
            ebuf[0:ts, cols] = dp * inv[g]
        ahead[...] = ebuf[0:HALO, :]
        du = []
        for g, w in enumerate(POOL_WINDOWS):
            cols = slice(g * GROUP_DIM, (g + 1) * GROUP_DIM)
            du.append(_leading_sums(ebuf, tbuf, cols, w, ts) - dpooled[g])
        du = _bf(jnp.concatenate(du, axis=-1))
        dproj_ref[:, :D] = du
        dgt_bf = _bf(dgt)
        half = D // 2
        dn0 = (_nt(du[:, :half], wpi_ref[0]) + _nt(du[:, half:], wpi_ref[1])
               + _nt(dgt_bf[:, :half], wpi_ref[2]) + _nt(dgt_bf[:, half:], wpi_ref[3]))

        xv = x_ref[...]
        r = lax.rsqrt(jnp.mean(xv * xv, axis=-1, keepdims=True) + EPS)
        xhat = xv * r
        small_ref[0:1, :] += jnp.sum(dn0 * xhat, axis=0, keepdims=True)
        dxh = dn0 * w0_ref[...]
        dx_ref[...] = dh1 + r * (dxh - xhat * jnp.mean(dxh * xhat, axis=-1, keepdims=True))

        @pl.when(i == nt - 1)
        def _():
            for cp in copies:
                cp.wait()

    row = lambda cols: pl.BlockSpec((ts, cols), lambda i: (nt - 1 - i, 0))
    outs = pl.pallas_call(
        body, name="pool_backward", grid=(nt,),
        out_shape=[jax.ShapeDtypeStruct((s, D), F32), jax.ShapeDtypeStruct((s, 2 * D), BF16),
                   jax.ShapeDtypeStruct((D, D), F32),
                   jax.ShapeDtypeStruct((GROUPS, GROUP_DIM, GROUP_DIM), F32),
                   jax.ShapeDtypeStruct((8, D), F32)] + _scatter_shapes(chip_sums),
        in_specs=[row(D), row(D), row(D), row(D), _full((1, D)), _full((N_CHIPS, D, D // 2)),
                  _full((GROUPS, GROUP_DIM, GROUP_DIM)), _full((1, D)), _full((1, D)), _full((D, D))]
                 + _any_specs(n_sums),
        out_specs=[row(D), row(2 * D), _full((D, D)), _full((GROUPS, GROUP_DIM, GROUP_DIM)), _full((8, D))]
                  + _any_specs(n_sums),
        scratch_shapes=[pltpu.VMEM((ts + HALO, D), F32), pltpu.VMEM((ts + HALO, D), F32),
                        pltpu.VMEM((HALO, D), F32),
                        pltpu.SemaphoreType.DMA((3 * n_sums,)), pltpu.SemaphoreType.DMA((3 * n_sums,))],
        compiler_params=_params("arbitrary"),
    )(x, dh1, pooled, gt, w0, wpi, gw, gb, scale, wpo, *chip_sums)
    return outs[:5], outs[5:]


def gla_project(h1, w1, wgi_q, wgk, bgk, later):
    s = h1.shape[0]
    ts = ROW_TILE
    nt = s // ts
    assert nt >= 2
    n_later = len(later)

    def body(h_ref, w1_ref, wq_ref, wgk_ref, bgk_ref, *rest):
        rest = rest[n_later:]
        qk_ref, v_ref, gate_ref, low_ref, cum_ref, n1_ref = rest[:6]
        later_refs = rest[6:6 + n_later]
        send_sems, recv_sems, wgi_ref = rest[6 + n_later:]
        gather_in_background(pl.program_id(0), nt - 1, later_refs, send_sems, recv_sems, finish=False)

        @pl.when(pl.program_id(0) == 0)
        def _():
            _assemble_gla_in(wq_ref, wgi_ref)

        hv = h_ref[...]
        r = lax.rsqrt(jnp.mean(hv * hv, axis=-1, keepdims=True) + EPS)
        n1 = _bf(hv * r * w1_ref[...])
        n1_ref[...] = n1
        qk_ref[...] = _nn(n1, wgi_ref[:, 0:2 * KEY_W])
        v_ref[...] = _bf(_nn(n1, wgi_ref[:, 2 * KEY_W:2 * KEY_W + D]))
        gate_ref[...] = _nn(n1, wgi_ref[:, 2 * KEY_W + D:GLA_MAIN])
        low = _bf(_nn(n1, wgi_ref[:, GLA_MAIN:]))
        low_ref[...] = low
        z = _nn(low, wgk_ref[...]) + bgk_ref[...]
        lg = (jnp.minimum(z, 0.0) - jnp.log(1.0 + jnp.exp(-jnp.abs(z)))) / GATE_NORM
        lower_f = _chunk_masks()[0].astype(F32)
        for r0 in range(0, ts, CHUNK):
            cum_ref[r0:r0 + CHUNK, :] = _nn_exact(lower_f, lg[r0:r0 + CHUNK, :])
        gather_in_background(pl.program_id(0), nt - 1, later_refs, send_sems, recv_sems, finish=True)

    row = lambda cols: pl.BlockSpec((ts, cols), lambda i: (i, 0))
    outs = pl.pallas_call(
        body, name="gla_project", grid=(nt,),
        out_shape=[jax.ShapeDtypeStruct((s, D), F32), jax.ShapeDtypeStruct((s, D), BF16),
                   jax.ShapeDtypeStruct((s, D), F32), jax.ShapeDtypeStruct((s, RANK_PAD), BF16),
                   jax.ShapeDtypeStruct((s, KEY_W), F32), jax.ShapeDtypeStruct((s, D), BF16)]
                  + [jax.ShapeDtypeStruct(a.shape, a.dtype) for a in later],
        in_specs=[row(D), _full((1, D)), _full((N_CHIPS, D, GLA_IN_QUARTER)),
                  _full((RANK_PAD, KEY_W)), _full((1, KEY_W))] + _any_specs(n_later),
        out_specs=[row(D), row(D), row(D), row(RANK_PAD), row(KEY_W), row(D)] + _any_specs(n_later),
        input_output_aliases={5 + k: 6 + k for k in range(n_later)},
        scratch_shapes=[pltpu.SemaphoreType.DMA((6 * n_later,)), pltpu.SemaphoreType.DMA((6 * n_later,)),
                        pltpu.VMEM((D, GLA_MAIN + RANK_PAD), BF16)],
        compiler_params=_params("arbitrary"),
    )(h1, w1, wgi_q, wgk, bgk, *later)
    return outs[:6], outs[6:]


def _assemble_gla_in(wq_ref, wfull):
    pad = jnp.zeros((CAST_ROWS, GLA_MAIN + RANK_PAD - GLA_IN), BF16)
    for r0 in range(0, D, CAST_ROWS):
        rows = slice(r0, r0 + CAST_ROWS)
        wfull[rows, :] = jnp.concatenate([wq_ref[q, rows, :] for q in range(N_CHIPS)] + [pad], axis=1)


GLA_BLOCK = 512
CHUNKS_PER_BLOCK = GLA_BLOCK // CHUNK


def _chunk_masks():
    t = lax.broadcasted_iota(jnp.int32, (CHUNK, CHUNK), 0)
    u = lax.broadcasted_iota(jnp.int32, (CHUNK, CHUNK), 1)
    return t >= u, t <= u


def _gla_chunk_terms(q, cum):
    ep = jnp.exp(cum)
    en = jnp.exp(-cum)
    qs = q * (HEAD_K ** -0.5)
    last = cum[CHUNK - 1:CHUNK, :]
    ed = jnp.exp(last - cum)
    dec = jnp.exp(last)
    return ep, en, qs, ed, dec


def gla_forward(qk, v, cum):
    s = qk.shape[0]
    nb = s // GLA_BLOCK
    nc = s // CHUNK

    def body(q_ref, k_ref, v_ref, cum_ref, o_ref, st_ref, sc_ref, state):
        @pl.when(pl.program_id(0) == 0)
        def _():
            state[...] = jnp.zeros_like(state)

        lower, _ = _chunk_masks()

        def chunk(cc, carry):
            rows = pl.ds(pl.multiple_of(cc * CHUNK, CHUNK), CHUNK)
            for h in range(HEADS):
                kc = slice(h * HEAD_K, (h + 1) * HEAD_K)
                vc = slice(h * HEAD_V, (h + 1) * HEAD_V)
                q = q_ref[rows, kc]
                k = k_ref[rows, kc]
                v = v_ref[rows, vc]
                ep, en, qs, ed, dec = _gla_chunk_terms(q, cum_ref[rows, kc])
                a = _bf(qs * ep)
                fwd = _nt(a, _bf(k * en))
                bwd = _nt(_bf(qs * en), _bf(k * ep))
                scores = _bf(jnp.where(lower, fwd, bwd))
                sc_ref[rows, h * CHUNK:(h + 1) * CHUNK] = scores
                st = state[h]
                st_ref[cc, h] = st
                o_ref[rows, vc] = _nn(scores, v) + _nt(a, _bf(st))
                state[h] = st * dec + _tn(v, _bf(k * ed))
            return carry

        lax.fori_loop(0, CHUNKS_PER_BLOCK, chunk, 0, unroll=4)

    return pl.pallas_call(
        body, name="gla_forward", grid=(nb,),
        out_shape=(jax.ShapeDtypeStruct((s, D), F32),
                   jax.ShapeDtypeStruct((nc, HEADS, HEAD_V, HEAD_K), F32),
                   jax.ShapeDtypeStruct((s, HEADS * CHUNK), BF16)),
        in_specs=[pl.BlockSpec((GLA_BLOCK, KEY_W), lambda i: (i, 0)),
                  pl.BlockSpec((GLA_BLOCK, KEY_W), lambda i: (i, 1)),
                  pl.BlockSpec((GLA_BLOCK, D), lambda i: (i, 0)),
                  pl.BlockSpec((GLA_BLOCK, KEY_W), lambda i: (i, 0))],
        out_specs=(pl.BlockSpec((GLA_BLOCK, D), lambda i: (i, 0)),
                   pl.BlockSpec((CHUNKS_PER_BLOCK, HEADS, HEAD_V, HEAD_K), lambda i: (i, 0, 0, 0)),
                   pl.BlockSpec((GLA_BLOCK, HEADS * CHUNK), lambda i: (i, 0))),
        scratch_shapes=[pltpu.VMEM((HEADS, HEAD_V, HEAD_K), F32)],
        compiler_params=_params("arbitrary"),
    )(qk, qk, v, cum)


def gla_backward(qk, v, cum, do, states, scores):
    s = qk.shape[0]
    nb = s // GLA_BLOCK

    def body(q_ref, k_ref, v_ref, cum_ref, do_ref, st_ref, sc_ref, dq_ref, dk_ref, dv_ref, dcum_ref, dstate):
        @pl.when(pl.program_id(0) == 0)
        def _():
            dstate[...] = jnp.zeros_like(dstate)

        lower, _ = _chunk_masks()
        is_last = lax.broadcasted_iota(jnp.int32, (CHUNK, HEAD_K), 0) == CHUNK - 1

        def chunk(step, carry):
            cc = CHUNKS_PER_BLOCK - 1 - step
            rows = pl.ds(pl.multiple_of(cc * CHUNK, CHUNK), CHUNK)
            for h in range(HEADS):
                kc = slice(h * HEAD_K, (h + 1) * HEAD_K)
                vc = slice(h * HEAD_V, (h + 1) * HEAD_V)
                q = q_ref[rows, kc]
                k = k_ref[rows, kc]
                v = v_ref[rows, vc]
                do_c = do_ref[rows, vc]
                ep, en, qs, ed, dec = _gla_chunk_terms(q, cum_ref[rows, kc])
                a = _bf(qs * ep)
                b = _bf(k * en)
                c = _bf(qs * en)
                dk_dec = _bf(k * ep)
                kd = _bf(k * ed)
                scores = sc_ref[rows, h * CHUNK:(h + 1) * CHUNK]
                st = st_ref[cc, h]
                dst = dstate[h]
                dst_bf = _bf(dst)

                dscores = _nt(do_c, v)
                dfwd = _bf(jnp.where(lower, dscores, 0.0))
                dbwd = _bf(jnp.where(lower, 0.0, dscores))
                dv_ref[rows, vc] = _bf(_tn(scores, do_c) + _nt(kd, dst_bf))
                da = _nn(dfwd, b) + _nn(do_c, _bf(st))
                db = _tn(dfwd, a)
                dc = _nn(dbwd, dk_dec)
                ddk = _tn(dbwd, c)
                dkd = _nn(v, dst_bf)
                ddec = jnp.sum(dst * st, axis=0, keepdims=True)
                dstate[h] = dst * dec + _tn(do_c, a)

                m = dkd * k * ed
                dq_ref[rows, kc] = _bf((da * ep + dc * en) * (HEAD_K ** -0.5))
                dk_ref[rows, kc] = _bf(db * en + ddk * ep + dkd * ed)
                dcum = (da * qs + ddk * k) * ep - (db * k + dc * qs) * en - m
                dlast = jnp.sum(m, axis=0, keepdims=True) + ddec * dec
                dcum_ref[rows, kc] = dcum + jnp.where(is_last, dlast, 0.0)
            return carry

        lax.fori_loop(0, CHUNKS_PER_BLOCK, chunk, 0, unroll=4)

    rev = lambda cols, col_block: pl.BlockSpec((GLA_BLOCK, cols), lambda i: (nb - 1 - i, col_block))
    return pl.pallas_call(
        body, name="gla_backward", grid=(nb,),
        out_shape=(jax.ShapeDtypeStruct((s, KEY_W), BF16), jax.ShapeDtypeStruct((s, KEY_W), BF16),
                   jax.ShapeDtypeStruct((s, D), BF16), jax.ShapeDtypeStruct((s, KEY_W), F32)),
        in_specs=[rev(KEY_W, 0), rev(KEY_W, 1), rev(D, 0), rev(KEY_W, 0), rev(D, 0),
                  pl.BlockSpec((CHUNKS_PER_BLOCK, HEADS, HEAD_V, HEAD_K), lambda i: (nb - 1 - i, 0, 0, 0)),
                  rev(HEADS * CHUNK, 0)],
        out_specs=(rev(KEY_W, 0), rev(KEY_W, 0), rev(D, 0), rev(KEY_W, 0)),
        scratch_shapes=[pltpu.VMEM((HEADS, HEAD_V, HEAD_K), F32)],
        compiler_params=_params("arbitrary"),
    )(qk, qk, v, cum, do, states, scores)


def head_and_loss(o, gate, h1, target, hw, wgo, wf):
    s = o.shape[0]
    ts = ROW_TILE

    def body(o_ref, gate_ref, h1_ref, tgt_ref, hw_ref, wgo_ref, wf_ref,
             dh2_ref, do_ref, dgate_ref, ggo_ref, small_ref):
        @pl.when(pl.program_id(0) == 0)
        def _():
            ggo_ref[...] = jnp.zeros_like(ggo_ref)
            small_ref[...] = jnp.zeros_like(small_ref)

        gate = gate_ref[...]
        hw = hw_ref[...]
        sg = _sigmoid(gate)
        silu = gate * sg
        ohat, ro = [], []
        for h in range(HEADS):
            oh = o_ref[:, h * HEAD_V:(h + 1) * HEAD_V]
            rh = lax.rsqrt(jnp.mean(oh * oh, axis=-1, keepdims=True) + EPS)
            ro.append(rh)
            ohat.append(oh * rh)
        ohat = jnp.concatenate(ohat, axis=-1)
        on = ohat * hw
        y2 = _bf(on * silu)
        h2 = h1_ref[...] + _nn(y2, wgo_ref[...])
        rf = lax.rsqrt(jnp.mean(h2 * h2, axis=-1, keepdims=True) + EPS)
        h2hat = h2 * rf
        wf = wf_ref[...]
        diff = h2hat * wf - tgt_ref[...]
        small_ref[2:3, :] += jnp.zeros((1, D), F32) + 0.5 * jnp.sum(diff * diff) / D
        dout = diff / D
        small_ref[0:1, :] += jnp.sum(dout * h2hat, axis=0, keepdims=True)
        dxh = dout * wf
        dh2 = rf * (dxh - h2hat * jnp.mean(dxh * h2hat, axis=-1, keepdims=True))
        dh2_ref[...] = dh2
        dh2_bf = _bf(dh2)
        ggo_ref[...] += _tn(y2, dh2_bf)
        dy2 = _nt(dh2_bf, wgo_ref[...])
        don = dy2 * silu
        dgate_ref[...] = _bf(dy2 * on * (sg * (1.0 + gate * (1.0 - sg))))
        ghw = jnp.sum(don * ohat, axis=0, keepdims=True)
        small_ref[1:2, 0:HEAD_V] += sum(ghw[:, h * HEAD_V:(h + 1) * HEAD_V] for h in range(HEADS))
        dohat = don * hw
        for h in range(HEADS):
            cols = slice(h * HEAD_V, (h + 1) * HEAD_V)
            oh, dh = ohat[:, cols], dohat[:, cols]
            do_ref[:, cols] = _bf(ro[h] * (dh - oh * jnp.mean(dh * oh, axis=-1, keepdims=True)))

    row = lambda cols: pl.BlockSpec((ts, cols), lambda i: (i, 0))
    act = jax.ShapeDtypeStruct((s, D), F32)
    act_bf = jax.ShapeDtypeStruct((s, D), BF16)
    return pl.pallas_call(
        body, name="head_and_loss", grid=(s // ts,),
        out_shape=(act, act_bf, act_bf, jax.ShapeDtypeStruct((D, D), F32), jax.ShapeDtypeStruct((8, D), F32)),
        in_specs=[row(D), row(D), row(D), row(D),
                  _full((1, D)), _full((D, D)), _full((1, D))],
        out_specs=(row(D), row(D), row(D), _full((D, D)), _full((8, D))),
        compiler_params=_params("arbitrary"),
    )(o, gate, h1, target, hw, wgo, wf)


def gla_project_backward(dq, dk, dv, dgate, dcum, low, h1, dh2, w1, wgi_q, wgk, bgk):
    s = h1.shape[0]
    ts = ROW_TILE

    def body(dq_ref, dk_ref, dv_ref, dgate_ref, dcum_ref, low_ref, h1_ref, dh2_ref, w1_ref,
             wq_ref, wgk_ref, bgk_ref, dh1_ref, dproj_ref, ggk_ref, small_ref, wgi_ref):
        @pl.when(pl.program_id(0) == 0)
        def _():
            ggk_ref[...] = jnp.zeros_like(ggk_ref)
            small_ref[...] = jnp.zeros_like(small_ref)
            _assemble_gla_in(wq_ref, wgi_ref)

        low = low_ref[...]
        z = _nn(low, wgk_ref[...]) + bgk_ref[...]
        upper_f = _chunk_masks()[1].astype(F32)
        dlg = jnp.concatenate([_nn_exact(upper_f, dcum_ref[r0:r0 + CHUNK, :]) for r0 in range(0, ts, CHUNK)],
                              axis=0)
        dz = dlg * (1.0 / GATE_NORM) * _sigmoid(-z)
        dz_bf = _bf(dz)
        ggk_ref[...] += _tn(low, dz_bf)
        small_ref[1:2, 0:KEY_W] += jnp.sum(dz, axis=0, keepdims=True)
        dlow = _bf(_nt(dz_bf, wgk_ref[...]))
        dproj_ref[:, GLA_MAIN:] = dlow
        dn1 = _nt(dlow, wgi_ref[:, GLA_MAIN:])
        for ref, lo, hi in ((dq_ref, 0, KEY_W), (dk_ref, KEY_W, 2 * KEY_W),
                            (dv_ref, 2 * KEY_W, 2 * KEY_W + D), (dgate_ref, 2 * KEY_W + D, GLA_MAIN)):
            piece = ref[...]
            dproj_ref[:, lo:hi] = piece
            dn1 = dn1 + _nt(piece, wgi_ref[:, lo:hi])
        hv = h1_ref[...]
        r = lax.rsqrt(jnp.mean(hv * hv, axis=-1, keepdims=True) + EPS)
        hhat = hv * r
        small_ref[0:1, :] += jnp.sum(dn1 * hhat, axis=0, keepdims=True)
        dxh = dn1 * w1_ref[...]
        dh1_ref[...] = dh2_ref[...] + r * (dxh - hhat * jnp.mean(dxh * hhat, axis=-1, keepdims=True))

    row = lambda cols: pl.BlockSpec((ts, cols), lambda i: (i, 0))
    return pl.pallas_call(
        body, name="gla_project_backward", grid=(s // ts,),
        out_shape=(jax.ShapeDtypeStruct((s, D), F32), jax.ShapeDtypeStruct((s, GLA_MAIN + RANK_PAD), BF16),
                   jax.ShapeDtypeStruct((RANK_PAD, KEY_W), F32),
                   jax.ShapeDtypeStruct((8, D), F32)),
        in_specs=[row(KEY_W), row(KEY_W), row(D), row(D), row(KEY_W), row(RANK_PAD), row(D), row(D),
                  _full((1, D)), _full((N_CHIPS, D, GLA_IN_QUARTER)), _full((RANK_PAD, KEY_W)),
                  _full((1, KEY_W))],
        out_specs=(row(D), row(GLA_MAIN + RANK_PAD), _full((RANK_PAD, KEY_W)), _full((8, D))),
        scratch_shapes=[pltpu.VMEM((D, GLA_MAIN + RANK_PAD), BF16)],
        compiler_params=_params("arbitrary"),
    )(dq, dk, dv, dgate, dcum, low, h1, dh2, w1, wgi_q, wgk, bgk)


def _groups_from_quarters(a):
    return a.reshape(N_CHIPS, GROUPS, 64, GROUP_DIM).transpose(1, 0, 2, 3).reshape(GROUPS, GROUP_DIM, GROUP_DIM)


def _quarters_from_groups(a):
    return a.reshape(GROUPS, N_CHIPS, 64, GROUP_DIM).transpose(1, 0, 2, 3).reshape(N_CHIPS, GROUP_DIM, GROUP_DIM)


def local_gradients(xs, target, w0, w1, wf, wpi, gw, gb, scale, wpo, gla_quarters, wgk, bgk, hw_tiled, place):
    wgi_q, wgo_q = gla_quarters
    (h1, pooled, gt, n0), (wgi_q,) = pool_forward(xs, w0, wpi, gw, gb, scale, wpo, [wgi_q])
    (qk, v, gate, low, cum, n1), (wgo_q,) = gla_project(h1, w1, wgi_q, wgk, bgk, [wgo_q])
    wgo = wgo_q.reshape(D, D)
    o, states, scores = gla_forward(qk, v, cum)

    dh2, do, dgate, g_gla_out, small_top = head_and_loss(o, gate, h1, target, hw_tiled, wgo, wf)
    dq, dk, dv, dcum = gla_backward(qk, v, cum, do, states, scores)
    dh1, dproj, g_gk_pad, small_gla = gla_project_backward(
        dq, dk, dv, dgate, dcum, low, h1, dh2, w1, wgi_q, wgk, bgk)
    g_gla_in, _ = matmul_tn(n1, dproj, "grad_gla_in", tile_n=(GLA_MAIN + RANK_PAD) // 5)

    def chip_sums(grads, tag):
        return add_halves(grads, place, "add_halves_" + tag)

    gla_sums = chip_sums([g_gla_in, g_gla_out.reshape(N_CHIPS, D // N_CHIPS, D)], "gla")
    (dx, dpool, g_pool_out, g_group_w, small_pool), gla_got = pool_backward(
        xs, dh1, pooled, gt, w0, wpi, gw, gb, scale, wpo, [b for _, b in gla_sums])
    mix_sums = chip_sums([_quarters_from_groups(g_group_w), g_pool_out.reshape(N_CHIPS, D // N_CHIPS, D)], "pool_mix")
    g_pool_in, mix_got = matmul_tn(n0, dpool, "grad_pool_in", tile_n=D // 2, by_column_tile=True,
                                   chip_sums=[b for _, b in mix_sums])

    in_sums = chip_sums([g_pool_in], "pool_in")
    in_got = scatter_to_owners([b for _, b in in_sums], "scatter_to_owners_pool_in")
    reduced, total = join_halves(
        add_parts([f for f, _ in in_sums + mix_sums + gla_sums], list(in_got) + list(mix_got) + list(gla_got),
                  place, "add_parts"),
        small_pool, small_gla, small_top, g_gk_pad)
    return dx, reduced, total


def kernel(x, norm_w, pool_in_w, pool_group_w, pool_group_b, pool_scale, pool_out_w, gla_in_w, gla_gk_w, gla_gk_b, gla_head_norm_w, gla_out_w, final_norm_w, loss_target, m_norm_w, m_pool_in_w, m_pool_group_w, m_pool_group_b, m_pool_scale, m_pool_out_w, m_gla_in_w, m_gla_gk_w, m_gla_gk_b, m_gla_head_norm_w, m_gla_out_w, m_final_norm_w, v_norm_w, v_pool_in_w, v_pool_group_w, v_pool_group_b, v_pool_scale, v_pool_out_w, v_gla_in_w, v_gla_gk_w, v_gla_gk_b, v_gla_head_norm_w, v_gla_out_w, v_final_norm_w):
    xs = x[0]
    target = loss_target[0]
    q_chip = 2 * lax.axis_index("x") + lax.axis_index("y")
    place = jnp.stack([lax.axis_index("c"), q_chip]).astype(jnp.int32)

    (wpi, gw_q, wpo_q, wgi_q, wgo_q), small_all = allgather_weights(
        [pool_in_w[0], pool_group_w[0].reshape(GROUP_DIM, GROUP_DIM), pool_out_w[0], gla_in_w[0], gla_out_w[0]],
        exchange=(True, True, True, False, False),
        smalls=[gla_gk_b, gla_head_norm_w, pool_group_b[0], gla_gk_w[0]])
    gw = _groups_from_quarters(gw_q)
    wpo = wpo_q.reshape(D, D)
    small_all = small_all[0::2]
    bgk = small_all[:, 0, :].reshape(1, KEY_W)
    hw = small_all[:, 1, 0:64].reshape(1, HEAD_V)
    gb = small_all[:, 2:2 + GROUPS, 0:64].transpose(1, 0, 2).reshape(1, D)
    wgk16 = small_all[:, 8:8 + GATE_RANK, :].transpose(1, 0, 2).reshape(GATE_RANK, KEY_W)
    wgk = _bf(jnp.pad(wgk16, ((0, RANK_PAD - GATE_RANK), (0, 0))))
    hw_tiled = jnp.tile(hw, (1, HEADS))

    w0 = norm_w[0:1]
    w1 = norm_w[1:2]
    wf = final_norm_w.reshape(1, D)

    dx, reduced, total = local_gradients(
        xs, target, w0, w1, wf, wpi, gw, gb, pool_scale, wpo, [wgi_q, wgo_q], wgk, bgk, hw_tiled, place)
    r_pool_in, r_group_w, r_pool_out, r_gla_in, r_gla_out = reduced
    r_group_w = r_group_w.reshape(GROUPS, 64, GROUP_DIM)

    loss = total[7, 0]
    g_norm = jnp.stack([total[0], total[3]])
    g_scale = total[1:2]
    g_final = total[5]
    pick = lambda full, width: lax.dynamic_slice_in_dim(full, q_chip * width, width, axis=-1)
    g_gk_b = pick(total[4:5, 0:KEY_W], 128)
    g_hnw = pick(total[6:7, 0:HEAD_V], 64)
    g_group_b = pick(total[2].reshape(GROUPS, GROUP_DIM), 64)[None]
    g_gk_w = pick(total[8:16].reshape(GATE_RANK, KEY_W), 128)[None]

    turn = lambda a: jnp.transpose(a, (2, 0, 1))
    back = lambda a: jnp.transpose(a, (1, 2, 0))
    as2d = lambda a, w: a.reshape(-1, w.shape[-1])
    big_names = ("pool_in_w", "pool_group_w", "pool_out_w", "gla_in_w", "gla_out_w")
    big_args = [(pool_in_w, r_pool_in[None], m_pool_in_w, v_pool_in_w),
                (pool_group_w, r_group_w[None], m_pool_group_w, v_pool_group_w),
                (pool_out_w, r_pool_out[None], m_pool_out_w, v_pool_out_w),
                (gla_in_w, r_gla_in[None], m_gla_in_w, v_gla_in_w),
                (gla_out_w, r_gla_out[None], m_gla_out_w, v_gla_out_w)]
    to_kernel = lambda n, a, w: turn(a) if n == "gla_in_w" else as2d(a, w)
    from_kernel = lambda n, a, w: back(a) if n == "gla_in_w" else a.reshape(w.shape)
    big_in = [tuple(to_kernel(n, a, p[0]) for a in p) for n, p in zip(big_names, big_args)]
    big_out = adamw(big_in, "adamw")
    big = {n: (from_kernel(n, i[1], p[0]),) + tuple(from_kernel(n, o, p[0]) for o in out)
           for n, p, i, out in zip(big_names, big_args, big_in, big_out)}

    small_names = ("norm_w", "pool_group_b", "pool_scale", "gla_gk_w", "gla_gk_b", "gla_head_norm_w",
                   "final_norm_w")
    small_args = [(norm_w, g_norm, m_norm_w, v_norm_w),
                  (pool_group_b, g_group_b, m_pool_group_b, v_pool_group_b),
                  (pool_scale, g_scale, m_pool_scale, v_pool_scale),
                  (gla_gk_w, g_gk_w, m_gla_gk_w, v_gla_gk_w),
                  (gla_gk_b, g_gk_b, m_gla_gk_b, v_gla_gk_b),
                  (gla_head_norm_w, g_hnw, m_gla_head_norm_w, v_gla_head_norm_w),
                  (final_norm_w, g_final, m_final_norm_w, v_final_norm_w)]
    small_out = adamw_small([tuple(as2d(a, p[0]) for a in p) for p in small_args])
    small = {n: (p[1].reshape(p[0].shape),) + tuple(o.reshape(p[0].shape) for o in out)
             for n, p, out in zip(small_names, small_args, small_out)}
    results = [
        small["norm_w"],
        big["pool_in_w"],
        big["pool_group_w"],
        small["pool_group_b"],
        small["pool_scale"],
        big["pool_out_w"],
        big["gla_in_w"],
        small["gla_gk_w"],
        small["gla_gk_b"],
        small["gla_head_norm_w"],
        big["gla_out_w"],
        small["final_norm_w"],
    ]
    grads, deltas, new_m, new_v = zip(*results)
    return (loss, dx[None], *grads, *deltas, *new_m, *new_v)
```

```python
import jax
import jax.numpy as jnp
from jax import lax
from jax.experimental import pallas as pl
from jax.experimental.pallas import tpu as pltpu

F32 = jnp.float32
BF16 = jnp.bfloat16
MESH = pl.DeviceIdType.MESH

D = 1024
POOL_WINDOWS = (2, 4, 8, 16)
GROUPS = 4
GROUP_DIM = 256
HEADS = 4
HEAD_K = 128
HEAD_V = 256
KEY_W = 512
CHUNK = 64
GATE_RANK = 16
GATE_NORM = 16.0
GLA_IN = 3088
GLA_MAIN = 3072
RANK_PAD = 128
EPS = 1e-6
HALO = 32

ADAM_LR = 0.001
ADAM_B1 = 0.9
ADAM_B2 = 0.999
ADAM_EPS = 1e-08
ADAM_WD = 0.01
ADAM_STEP = 10

N_CHIPS = 4
N_DEV = 8
GLA_IN_QUARTER = GLA_IN // N_CHIPS

VMEM_LIMIT = 56 * 1024 * 1024


def _nn(a, b):
    return lax.dot_general(a, b, (((1,), (0,)), ((), ())), preferred_element_type=F32)


def _nt(a, b):
    return lax.dot_general(a, b, (((1,), (1,)), ((), ())), preferred_element_type=F32)


def _tn(a, b):
    return lax.dot_general(a, b, (((0,), (0,)), ((), ())), preferred_element_type=F32)


def _nn_exact(a, b):
    return lax.dot_general(a, b, (((1,), (0,)), ((), ())), preferred_element_type=F32,
                           precision=lax.Precision.HIGHEST)


def _bf(a):
    return a.astype(BF16)


def _params(*sem):
    return pltpu.CompilerParams(dimension_semantics=sem, vmem_limit_bytes=VMEM_LIMIT)


def _full(shape):
    return pl.BlockSpec(shape, lambda i: (0,) * len(shape))


def _position():
    return lax.axis_index("x"), lax.axis_index("y"), lax.axis_index("c")


def _gather_small(in_ref, all_ref, send_sems, recv_sems, local_sem):
    x, y, c = _position()
    me = 4 * x + 2 * y + c
    mine = pltpu.make_async_copy(in_ref, all_ref.at[me], local_sem)
    mine.start()
    sends = []
    for k in range(N_DEV - 1):
        fx, fy, fc = (k + 1) >> 2 & 1, (k + 1) >> 1 & 1, (k + 1) & 1
        cp = pltpu.make_async_remote_copy(
            src_ref=in_ref, dst_ref=all_ref.at[me],
            send_sem=send_sems.at[k], recv_sem=recv_sems.at[k],
            device_id=(x ^ fx, y ^ fy, c ^ fc), device_id_type=MESH)
        cp.start()
        sends.append(cp)
    def wait():
        for k in range(N_DEV - 1):
            fx, fy, fc = (k + 1) >> 2 & 1, (k + 1) >> 1 & 1, (k + 1) & 1
            src_dev = 4 * (x ^ fx) + 2 * (y ^ fy) + (c ^ fc)
            pltpu.make_async_remote_copy(
                src_ref=in_ref, dst_ref=all_ref.at[src_dev],
                send_sem=send_sems.at[k], recv_sem=recv_sems.at[k],
                device_id=(x, y, c), device_id_type=MESH).wait_recv()
        for cp in sends:
            cp.wait_send()
        mine.wait()

    return wait


SMALL_SEMS = [pltpu.SemaphoreType.DMA((N_DEV - 1,)), pltpu.SemaphoreType.DMA((N_DEV - 1,)),
              pltpu.SemaphoreType.DMA]
VMEM_SPEC = pl.BlockSpec(memory_space=pltpu.VMEM)


def _other_chips(x, y):
    return [(1 - x, y), (x, 1 - y), (1 - x, 1 - y)]


def _any_specs(n):
    return [pl.BlockSpec(memory_space=pl.ANY)] * n


def _halves(rows, c):
    half = rows // 2
    return pl.ds(c * half, half), pl.ds((1 - c) * half, half)


CAST_ROWS = 256


def _gather_copy(out_ref, send_sems, recv_sems, k, quarter, half, to, src=None):
    dst = out_ref.at[quarter, half]
    return pltpu.make_async_remote_copy(
        src_ref=dst if src is None else src, dst_ref=dst,
        send_sem=send_sems.at[k], recv_sem=recv_sems.at[k], device_id=to, device_id_type=MESH)


SMALL_IN_ROWS = 24


def allgather_weights(quarters, exchange, smalls):
    n = len(quarters)
    shapes = [w.shape for w in quarters]
    moved = [i for i in range(n) if exchange[i]]

    def body(*refs):
        w_refs, (gkb_ref, hnw_ref, gb_ref, gkw_ref) = refs[:n], refs[n:n + 4]
        out_refs, small_all_ref = refs[n + 4:2 * n + 4], refs[2 * n + 4]
        refs = refs[2 * n + 5:]
        f32_bufs, bf_bufs = refs[:n], refs[n:2 * n]
        send_sems, recv_sems, local_sems, small_ref = refs[2 * n:2 * n + 4]
        small_ref[...] = jnp.zeros_like(small_ref)
        small_ref[0:1, :] = gkb_ref[...]
        small_ref[1:2, 0:64] = hnw_ref[...]
        small_ref[2:2 + GROUPS, 0:64] = gb_ref[...]
        small_ref[8:8 + GATE_RANK, :] = gkw_ref[...]
        wait_small = _gather_small(small_ref, small_all_ref, *refs[2 * n + 4:])
        x, y, c = _position()
        q = 2 * x + y
        sibling = (x, y, 1 - c)
        chips = _other_chips(x, y)

        def copy(k, i, quarter, half, to, src=None):
            return _gather_copy(out_refs[i], send_sems, recv_sems, k * n + i, quarter, half, to, src)

        loads = [pltpu.make_async_copy(w_refs[i], f32_bufs[i], local_sems.at[i]) for i in range(n)]
        for cp in loads:
            cp.start()
        keeps, sends = [], []
        for i in range(n):
            loads[i].wait()
            for r0 in range(0, shapes[i][0], CAST_ROWS):
                bf_bufs[i][r0:r0 + CAST_ROWS, :] = _bf(f32_bufs[i][r0:r0 + CAST_ROWS, :])
            keep = pltpu.make_async_copy(bf_bufs[i], out_refs[i].at[q], local_sems.at[n + i])
            keep.start()
            keeps.append(keep)
            if not exchange[i]:
                continue
            mine, _ = _halves(shapes[i][0], c)
            for j, chip in enumerate(chips):
                cp = copy(j, i, q, mine, (*chip, c), src=bf_bufs[i].at[mine])
                cp.start()
                sends.append(cp)
        for j, chip in enumerate(chips):
            qj = 2 * chip[0] + chip[1]
            for i in moved:
                mine, _ = _halves(shapes[i][0], c)
                copy(j, i, qj, mine, (x, y, c)).wait_recv()
                cp = copy(3 + j, i, qj, mine, sibling)
                cp.start()
                sends.append(cp)
        for j, chip in enumerate(chips):
            qj = 2 * chip[0] + chip[1]
            for i in moved:
                _, other = _halves(shapes[i][0], c)
                copy(3 + j, i, qj, other, (x, y, c)).wait_recv()
        wait_small()
        for cp in sends:
            cp.wait_send()
        for cp in keeps:
            cp.wait()

    outs = pl.pallas_call(
        body, name="allgather_weights",
        out_shape=[jax.ShapeDtypeStruct((N_CHIPS, *s), BF16) for s in shapes]
                  + [jax.ShapeDtypeStruct((N_DEV, SMALL_IN_ROWS, 128), F32)],
        in_specs=_any_specs(n) + [VMEM_SPEC] * 4, out_specs=_any_specs(n) + [VMEM_SPEC],
        scratch_shapes=([pltpu.VMEM(s, F32) for s in shapes] + [pltpu.VMEM(s, BF16) for s in shapes]
                        + [pltpu.SemaphoreType.DMA((6 * n,)), pltpu.SemaphoreType.DMA((6 * n,)),
                           pltpu.SemaphoreType.DMA((2 * n,)), pltpu.VMEM((SMALL_IN_ROWS, 128), F32)] + SMALL_SEMS),
        compiler_params=pltpu.CompilerParams(vmem_limit_bytes=VMEM_LIMIT),
    )(*quarters, *smalls)
    return outs[:n], outs[n]


def _scatter_copies(b_refs, got_refs, send_sems, recv_sems):
    n = len(b_refs)
    x, y, c = _position()
    copies = []
    for j, chip in enumerate(_other_chips(x, y)):
        qj = 2 * chip[0] + chip[1]
        for i in range(n):
            copies.append(pltpu.make_async_remote_copy(
                src_ref=b_refs[i].at[qj], dst_ref=got_refs[i].at[j],
                send_sem=send_sems.at[j * n + i], recv_sem=recv_sems.at[j * n + i],
                device_id=(*chip, c), device_id_type=MESH))
    return copies


def _scatter_shapes(chip_sums):
    return [jax.ShapeDtypeStruct((N_CHIPS - 1, *b.shape[1:]), BF16) for b in chip_sums]


def scatter_to_owners(chip_sums, name):
    n = len(chip_sums)

    def body(*refs):
        copies = _scatter_copies(refs[:n], refs[n:2 * n], *refs[2 * n:])
        for cp in copies:
            cp.start()
        for cp in copies:
            cp.wait()

    return pl.pallas_call(
        body, name=name,
        out_shape=_scatter_shapes(chip_sums),
        in_specs=_any_specs(n), out_specs=_any_specs(n),
        scratch_shapes=[pltpu.SemaphoreType.DMA((3 * n,)), pltpu.SemaphoreType.DMA((3 * n,))],
    )(*chip_sums)


SMALL_SUM_ROWS = 16


def join_halves(reduced, small_pool, small_gla, small_top, g_gk_pad):
    n = len(reduced)

    def body(*refs):
        pool_ref, gla_ref, top_ref, gk_ref = refs[n:n + 4]
        buf_refs, total_ref = refs[n + 4:2 * n + 4], refs[2 * n + 4]
        send_sems, recv_sems, all_ref, small_ref = refs[2 * n + 5:2 * n + 9]
        small_ref[0:3, :] = pool_ref[0:3, :]
        small_ref[3:5, :] = gla_ref[0:2, :]
        small_ref[5:8, :] = top_ref[0:3, :]
        for r in range(GATE_RANK):
            small_ref[8 + r // 2:9 + r // 2, (r % 2) * KEY_W:(r % 2 + 1) * KEY_W] = gk_ref[r:r + 1, :]
        x, y, c = _position()
        copies = []
        for i in range(n):
            mine, _ = _halves(buf_refs[i].shape[0], c)
            cp = pltpu.make_async_remote_copy(
                src_ref=buf_refs[i].at[mine], dst_ref=buf_refs[i].at[mine],
                send_sem=send_sems.at[i], recv_sem=recv_sems.at[i],
                device_id=(x, y, 1 - c), device_id_type=MESH)
            cp.start()
            copies.append(cp)
        _gather_small(small_ref, all_ref, *refs[2 * n + 9:])()
        total = all_ref[0]
        for dev in range(1, N_DEV):
            total = total + all_ref[dev]
        total_ref[...] = total
        for cp in copies:
            cp.wait()

    outs = pl.pallas_call(
        body, name="join_halves",
        out_shape=[jax.ShapeDtypeStruct(r.shape, F32) for r in reduced]
                  + [jax.ShapeDtypeStruct((SMALL_SUM_ROWS, D), F32)],
        in_specs=_any_specs(n) + [VMEM_SPEC] * 4, out_specs=_any_specs(n) + [VMEM_SPEC],
        input_output_aliases={i: i for i in range(n)},
        scratch_shapes=[pltpu.SemaphoreType.DMA((n,)), pltpu.SemaphoreType.DMA((n,)),
                        pltpu.VMEM((N_DEV, SMALL_SUM_ROWS, D), F32), pltpu.VMEM((SMALL_SUM_ROWS, D), F32)]
                       + SMALL_SEMS,
    )(*reduced, small_pool, small_gla, small_top, g_gk_pad)
    return outs[:n], outs[n]


ADD_ROWS = 512
ADD_HALVES_ROWS = 128
ADD_HALVES_AHEAD = 2


def _spans(counts):
    starts, total = [], 0
    for count in counts:
        starts.append(total)
        total += count
    return starts, total


def _local_step(t, start, count):
    return jnp.clip(t - start, 0, count - 1)


def add_halves(grads, place, name):
    n = len(grads)
    whole = [len(g.shape) == 2 for g in grads]
    halves = [g.shape[-2] // 2 for g in grads]
    cols = [GLA_IN_QUARTER if w else g.shape[-1] for g, w in zip(grads, whole)]
    rbs = [min(ADD_HALVES_ROWS, h) for h in halves]
    counts = [h // rb for h, rb in zip(halves, rbs)]
    starts, total = _spans(counts)
    half_shapes = [(*g.shape[:-2], h, g.shape[-1]) for g, h in zip(grads, halves)]

    def rows_of(ref, i, start):
        return ref.at[pl.ds(start, rbs[i])] if whole[i] else ref.at[:, pl.ds(start, rbs[i])]

    def body(place_ref, *refs):
        a_refs, src_refs = refs[:n], refs[n:2 * n]
        f_refs, h_refs = refs[2 * n:3 * n], refs[3 * n:4 * n]
        their_refs, (send_sems, recv_sems) = refs[4 * n:5 * n], refs[5 * n:]
        t = pl.program_id(0)
        q = place_ref[1]
        x, y, c = _position()
        copies = [[pltpu.make_async_remote_copy(
            src_ref=rows_of(src_refs[i], i, (1 - c) * halves[i] + k * rbs[i]),
            dst_ref=rows_of(their_refs[i], i, k * rbs[i]),
            send_sem=send_sems.at[starts[i] + k], recv_sem=recv_sems.at[starts[i] + k],
            device_id=(x, y, 1 - c), device_id_type=MESH) for k in range(counts[i])] for i in range(n)]

        in_order = [cp for of_matrix in copies for cp in of_matrix]

        @pl.when(t == 0)
        def _():
            for cp in in_order[:ADD_HALVES_AHEAD]:
                cp.start()

        for i in range(n):
            for k in range(counts[i]):
                @pl.when(t == starts[i] + k)
                def _(i=i, k=k):
                    copies[i][k].wait_recv()
                    for cp in in_order[starts[i] + k + ADD_HALVES_AHEAD:][:1]:
                        cp.start()
                    b_ref = rows_of(their_refs[i], i, k * rbs[i])
                    if not whole[i]:
                        h_refs[i][...] = _bf(a_refs[i][...] + b_ref[...])
                        f_refs[i][...] = a_refs[i][q] + b_ref[q]
                        return
                    total_i = a_refs[i][...] + b_ref[...]
                    for k4 in range(N_CHIPS):
                        piece = total_i[:, k4 * cols[i]:(k4 + 1) * cols[i]]
                        h_refs[i][k4] = _bf(piece)

                        @pl.when(q == k4)
                        def _():
                            f_refs[i][...] = piece

        @pl.when(t == total - 1)
        def _():
            for cp in in_order:
                cp.wait_send()

    def specs(i):
        step = lambda t: _local_step(t, starts[i], counts[i])
        by_quarter = (N_CHIPS, rbs[i], cols[i])
        block = (rbs[i], grads[i].shape[-1]) if whole[i] else by_quarter
        lead = () if whole[i] else (0,)
        mine = pl.BlockSpec(block, lambda t, place: (*lead, place[0] * counts[i] + step(t), 0))
        sums = pl.BlockSpec(by_quarter, lambda t, place: (0, step(t), 0))
        own = pl.BlockSpec(by_quarter[1:], lambda t, place: (step(t), 0))
        return mine, own, sums

    all_specs = [specs(i) for i in range(n)]
    outs = pl.pallas_call(
        body, name=name,
        grid_spec=pltpu.PrefetchScalarGridSpec(
            num_scalar_prefetch=1, grid=(total,),
            in_specs=[sp[0] for sp in all_specs] + _any_specs(n),
            out_specs=[sp[1] for sp in all_specs] + [sp[2] for sp in all_specs],
            scratch_shapes=[pltpu.VMEM(sh, F32) for sh in half_shapes]
                           + [pltpu.SemaphoreType.DMA((total,)), pltpu.SemaphoreType.DMA((total,))]),
        out_shape=[jax.ShapeDtypeStruct((h, cl), F32) for h, cl in zip(halves, cols)]
                  + [jax.ShapeDtypeStruct((N_CHIPS, h, cl), BF16) for h, cl in zip(halves, cols)],
        compiler_params=_params("arbitrary"),
    )(place, *grads, *grads)
    return list(zip(outs[:n], outs[n:]))


def add_parts(owns, gots, place, name):
    n = len(owns)
    shapes = [g.shape for g in gots]
    rbs = [min(ADD_ROWS, sh[1]) for sh in shapes]
    counts = [sh[1] // rb for sh, rb in zip(shapes, rbs)]
    starts, total = _spans(counts)

    def body(place_ref, *refs):
        o_refs, g_refs, out_refs = refs[:n], refs[n:2 * n], refs[2 * n:]
        t = pl.program_id(0)
        for i in range(n):
            @pl.when((t >= starts[i]) & (t < starts[i] + counts[i]))
            def _(i=i):
                total_i = o_refs[i][...]
                for j in range(N_CHIPS - 1):
                    total_i = total_i + g_refs[i][j].astype(F32)
                out_refs[i][...] = total_i

    def specs(i):
        rb, cols = rbs[i], shapes[i][2]
        step = lambda t: _local_step(t, starts[i], counts[i])
        return (pl.BlockSpec((rb, cols), lambda t, place: (step(t), 0)),
                pl.BlockSpec((N_CHIPS - 1, rb, cols), lambda t, place: (0, step(t), 0)),
                pl.BlockSpec((rb, cols), lambda t, place: (place[0] * counts[i] + step(t), 0)))

    all_specs = [specs(i) for i in range(n)]
    return pl.pallas_call(
        body, name=name,
        grid_spec=pltpu.PrefetchScalarGridSpec(
            num_scalar_prefetch=1, grid=(total,),
            in_specs=[sp[0] for sp in all_specs] + [sp[1] for sp in all_specs],
            out_specs=[sp[2] for sp in all_specs]),
        out_shape=[jax.ShapeDtypeStruct((2 * sh[1], sh[2]), F32) for sh in shapes],
        compiler_params=_params("arbitrary"),
    )(place, *owns, *gots)


def _adam_math(w, g, m, v):
    m = ADAM_B1 * m + (1.0 - ADAM_B1) * g
    v = ADAM_B2 * v + (1.0 - ADAM_B2) * (g * g)
    m_hat = m / (1.0 - ADAM_B1 ** ADAM_STEP)
    v_hat = v / (1.0 - ADAM_B2 ** ADAM_STEP)
    delta = -ADAM_LR * (m_hat / (jnp.sqrt(v_hat) + ADAM_EPS) + ADAM_WD * w)
    return delta, m, v


ADAM_BLOCK_BYTES = 2 ** 19
ADAM_MOST_STEPS = 8


def adamw(params, name):
    n = len(params)
    shapes = [p[0].shape for p in params]

    def tile_rows(shape):
        rows, cols = shape[0], shape[-1]
        aligned = 1 if len(shape) == 3 else 8
        divisors = [t for t in range(aligned, rows + 1, aligned) if rows % t == 0]
        tile = max(t for t in divisors if t * cols * 4 <= ADAM_BLOCK_BYTES)
        if rows // tile > ADAM_MOST_STEPS:
            tile = min(t for t in divisors if rows // t <= ADAM_MOST_STEPS)
        return tile

    tiles = [tile_rows(sh) for sh in shapes]
    counts = [sh[0] // tl for sh, tl in zip(shapes, tiles)]
    starts, total = _spans(counts)

    def body(*refs):
        ins, outs = refs[:4 * n], refs[4 * n:]
        t = pl.program_id(0)
        for i in range(n):
            @pl.when((t >= starts[i]) & (t < starts[i] + counts[i]))
            def _(i=i):
                w_ref, g_ref, m_ref, v_ref = ins[4 * i:4 * i + 4]
                d, nm, nv = _adam_math(w_ref[...], g_ref[...], m_ref[...], v_ref[...])
                outs[3 * i][...] = d
                outs[3 * i + 1][...] = nm
                outs[3 * i + 2][...] = nv

    def spec(i):
        block = (tiles[i],) + shapes[i][1:]
        zeros = (0,) * (len(block) - 1)
        return pl.BlockSpec(block, lambda t: (_local_step(t, starts[i], counts[i]),) + zeros)

    outs = pl.pallas_call(
        body, name=name, grid=(total,),
        out_shape=[jax.ShapeDtypeStruct(sh, F32) for sh in shapes for _ in range(3)],
        in_specs=[spec(i) for i in range(n) for _ in range(4)],
        out_specs=[spec(i) for i in range(n) for _ in range(3)],
        compiler_params=_params("arbitrary"),
    )(*[a for p in params for a in p])
    return [tuple(outs[3 * i:3 * i + 3]) for i in range(n)]


def adamw_small(params):
    n = len(params)

    def body(*refs):
        ins, outs = refs[:4 * n], refs[4 * n:]
        for k in range(n):
            w_ref, g_ref, m_ref, v_ref = ins[4 * k:4 * k + 4]
            d, nm, nv = _adam_math(w_ref[...], g_ref[...], m_ref[...], v_ref[...])
            outs[3 * k][...] = d
            outs[3 * k + 1][...] = nm
            outs[3 * k + 2][...] = nv

    flat = [a for p in params for a in p]
    outs = pl.pallas_call(
        body, name="adamw_small",
        out_shape=[jax.ShapeDtypeStruct(p[0].shape, F32) for p in params for _ in range(3)],
        in_specs=[VMEM_SPEC] * (4 * n), out_specs=[VMEM_SPEC] * (3 * n),
    )(*flat)
    return [tuple(outs[3 * k:3 * k + 3]) for k in range(n)]


def matmul_tn(a, b, name, tile_n, by_column_tile=False, chip_sums=()):
    s, m = a.shape
    n = b.shape[1]
    n_sums = len(chip_sums)
    steps = n // tile_n
    if by_column_tile:
        out_shape = jax.ShapeDtypeStruct((steps, m, tile_n), F32)
        out_spec = pl.BlockSpec((None, m, tile_n), lambda j: (j, 0, 0))
    else:
        out_shape = jax.ShapeDtypeStruct((m, n), F32)
        out_spec = pl.BlockSpec((m, tile_n), lambda j: (0, j))

    def body(a_ref, b_ref, *rest):
        sum_refs, out_ref, got_refs = rest[:n_sums], rest[n_sums], rest[n_sums + 1:2 * n_sums + 1]
        j = pl.program_id(0)
        copies = _scatter_copies(sum_refs, got_refs, *rest[2 * n_sums + 1:]) if n_sums else []

        @pl.when(j == 0)
        def _():
            for cp in copies:
                cp.start()

        out_ref[...] = _tn(a_ref[...], b_ref[...])

        @pl.when(j == steps - 1)
        def _():
            for cp in copies:
                cp.wait()

    outs = pl.pallas_call(
        body, name=name, grid=(steps,),
        out_shape=[out_shape] + _scatter_shapes(chip_sums),
        in_specs=[_full((s, m)), pl.BlockSpec((s, tile_n), lambda j: (0, j))] + _any_specs(n_sums),
        out_specs=[out_spec] + _any_specs(n_sums),
        scratch_shapes=[pltpu.SemaphoreType.DMA((3 * n_sums,)), pltpu.SemaphoreType.DMA((3 * n_sums,))]
                       if n_sums else [],
        compiler_params=_params("arbitrary"),
    )(a, b, *chip_sums)
    return outs[0], outs[1:]


ROW_TILE = 512


def _row_index(tile, rows):
    return tile * rows + lax.broadcasted_iota(jnp.int32, (rows, 1), 0)


def _inverse_counts(t_glob):
    return [1.0 / jnp.minimum(t_glob + 1, w).astype(F32) for w in POOL_WINDOWS]


def _sigmoid(z):
    return 1.0 / (1.0 + jnp.exp(-z))


def _trailing_sums(src, tmp, cols, window, rows):
    bufs = (src, tmp)
    span, level, start = 1, 0, 0
    while span < window:
        start += 8
        a, b = bufs[level % 2], bufs[(level + 1) % 2]
        n = HALO + rows - start
        b[start:start + n, cols] = a[start:start + n, cols] + a[start - span:start - span + n, cols]
        span, level = 2 * span, level + 1
    return bufs[level % 2][HALO:HALO + rows, cols]


def _leading_sums(src, tmp, cols, window, rows):
    bufs = (src, tmp)
    span, level, n = 1, 0, rows + HALO
    while span < window:
        n -= 8
        a, b = bufs[level % 2], bufs[(level + 1) % 2]
        b[0:n, cols] = a[0:n, cols] + a[span:span + n, cols]
        span, level = 2 * span, level + 1
    return bufs[level % 2][0:rows, cols]


def gather_in_background(step, last, out_refs, send_sems, recv_sems, finish):
    n = len(out_refs)
    x, y, c = _position()
    q = 2 * x + y
    chips = _other_chips(x, y)

    def copy(k, i, quarter, half, to):
        return _gather_copy(out_refs[i], send_sems, recv_sems, k * n + i, quarter, half, to)

    if not finish:
        @pl.when(step == 0)
        def _():
            for i in range(n):
                mine, _ = _halves(out_refs[i].shape[1], c)
                for j, chip in enumerate(chips):
                    copy(j, i, q, mine, (*chip, c)).start()

        @pl.when(step == last)
        def _():
            for j, chip in enumerate(chips):
                qj = 2 * chip[0] + chip[1]
                for i in range(n):
                    mine, _ = _halves(out_refs[i].shape[1], c)
                    copy(j, i, qj, mine, (x, y, c)).wait_recv()
                    copy(3 + j, i, qj, mine, (x, y, 1 - c)).start()
        return

    @pl.when(step == last)
    def _():
        for j, chip in enumerate(chips):
            qj = 2 * chip[0] + chip[1]
            for i in range(n):
                mine, other = _halves(out_refs[i].shape[1], c)
                copy(3 + j, i, qj, other, (x, y, c)).wait_recv()
                copy(j, i, q, mine, (x, y, c)).wait_send()
                copy(3 + j, i, qj, mine, (x, y, c)).wait_send()


def pool_forward(x, w0, wpi, gw, gb, scale, wpo, later):
    s = x.shape[0]
    ts = ROW_TILE
    nt = s // ts
    assert nt >= 2
    n_later = len(later)

    def body(x_ref, w0_ref, wpi_ref, gw_ref, gb_ref, sc_ref, wpo_ref, *rest):
        rest = rest[n_later:]
        h1_ref, pooled_ref, gt_ref, n0_ref = rest[:4]
        later_refs = rest[4:4 + n_later]
        ubuf, tbuf, hist, send_sems, recv_sems = rest[4 + n_later:]
        i = pl.program_id(0)
        gather_in_background(i, nt - 1, later_refs, send_sems, recv_sems, finish=False)
        xv = x_ref[...]
        r = lax.rsqrt(jnp.mean(xv * xv, axis=-1, keepdims=True) + EPS)
        n0 = _bf(xv * r * w0_ref[...])
        n0_ref[...] = n0
        u = jnp.concatenate([_nn(n0, wpi_ref[0]), _nn(n0, wpi_ref[1])], axis=-1)
        gt = jnp.concatenate([_nn(n0, wpi_ref[2]), _nn(n0, wpi_ref[3])], axis=-1)
        gt_ref[...] = gt

        @pl.when(i == 0)
        def _():
            hist[...] = jnp.zeros_like(hist)

        ubuf[0:HALO, :] = hist[...]
        ubuf[HALO:HALO + ts, :] = u
        hist[...] = u[ts - HALO:, :]
        inv = _inverse_counts(_row_index(i, ts))
        mixed = []
        for g, w in enumerate(POOL_WINDOWS):
            cols = slice(g * GROUP_DIM, (g + 1) * GROUP_DIM)
            pooled = _bf(_trailing_sums(ubuf, tbuf, cols, w, ts) * inv[g] - u[:, cols])
            pooled_ref[:, cols] = pooled
            mixed.append(_nn(pooled, gw_ref[g]))
        mixed = jnp.concatenate(mixed, axis=-1) + gb_ref[...]
        y = mixed * sc_ref[...] * (gt * _sigmoid(gt))
        h1_ref[...] = xv + _nn(_bf(y), wpo_ref[...])
        gather_in_background(i, nt - 1, later_refs, send_sems, recv_sems, finish=True)

    row = lambda cols: pl.BlockSpec((ts, cols), lambda i: (i, 0))
    outs = pl.pallas_call(
        body, name="pool_forward", grid=(nt,),
        out_shape=[jax.ShapeDtypeStruct((s, D), F32), jax.ShapeDtypeStruct((s, D), BF16),
                   jax.ShapeDtypeStruct((s, D), F32), jax.ShapeDtypeStruct((s, D), BF16)]
                  + [jax.ShapeDtypeStruct(a.shape, a.dtype) for a in later],
        in_specs=[row(D), _full((1, D)), _full((N_CHIPS, D, D // 2)), _full((GROUPS, GROUP_DIM, GROUP_DIM)),
                  _full((1, D)), _full((1, D)), _full((D, D))] + _any_specs(n_later),
        out_specs=[row(D), row(D), row(D), row(D)] + _any_specs(n_later),
        input_output_aliases={7 + k: 4 + k for k in range(n_later)},
        scratch_shapes=[pltpu.VMEM((HALO + ts, D), F32), pltpu.VMEM((HALO + ts, D), F32),
                        pltpu.VMEM((HALO, D), F32),
                        pltpu.SemaphoreType.DMA((6 * n_later,)), pltpu.SemaphoreType.DMA((6 * n_later,))],
        compiler_params=_params("arbitrary"),
    )(x, w0, wpi, gw, gb, scale, wpo, *later)
    return outs[:4], outs[4:]


def pool_backward(x, dh1, pooled, gt, w0, wpi, gw, gb, scale, wpo, chip_sums):
    s = x.shape[0]
    ts = ROW_TILE
    nt = s // ts
    n_sums = len(chip_sums)

    def body(x_ref, dh1_ref, pooled_ref, gt_ref, w0_ref, wpi_ref, gw_ref, gb_ref, sc_ref, wpo_ref, *rest):
        sum_refs, rest = rest[:n_sums], rest[n_sums:]
        dx_ref, dproj_ref, gpo_ref, ggw_ref, small_ref = rest[:5]
        got_refs = rest[5:5 + n_sums]
        ebuf, tbuf, ahead, send_sems, recv_sems = rest[5 + n_sums:]
        i = pl.program_id(0)
        copies = _scatter_copies(sum_refs, got_refs, send_sems, recv_sems)

        @pl.when(i == 0)
        def _():
            for cp in copies:
                cp.start()

        @pl.when(i == 0)
        def _():
            gpo_ref[...] = jnp.zeros_like(gpo_ref)
            ggw_ref[...] = jnp.zeros_like(ggw_ref)
            small_ref[...] = jnp.zeros_like(small_ref)
            ahead[...] = jnp.zeros_like(ahead)

        dh1 = dh1_ref[...]
        dh1_bf = _bf(dh1)
        gt = gt_ref[...]
        sc = sc_ref[...]
        dy = _nt(dh1_bf, wpo_ref[...])
        pooled_bf = []
        mixed = []
        for g in range(GROUPS):
            cols = slice(g * GROUP_DIM, (g + 1) * GROUP_DIM)
            pb = pooled_ref[:, cols]
            pooled_bf.append(pb)
            mixed.append(_nn(pb, gw_ref[g]))
        mixed = jnp.concatenate(mixed, axis=-1) + gb_ref[...]
        sg = _sigmoid(gt)
        silu = gt * sg
        gpo_ref[...] += _tn(_bf(mixed * sc * silu), dh1_bf)
        dmixed = dy * sc * silu
        dgt = dy * mixed * sc * (sg * (1.0 + gt * (1.0 - sg)))
        dproj_ref[:, D:] = _bf(dgt)
        small_ref[1:2, :] += jnp.sum(dy * mixed * silu, axis=0, keepdims=True)
        small_ref[2:3, :] += jnp.sum(dmixed, axis=0, keepdims=True)

        inv = _inverse_counts(_row_index(nt - 1 - i, ts))
        ebuf[ts:ts + HALO, :] = ahead[...]
        dpooled = []
        for g in range(GROUPS):
            cols = slice(g * GROUP_DIM, (g + 1) * GROUP_DIM)
            dm = _bf(dmixed[:, cols])
            ggw_ref[g] += _tn(pooled_bf[g], dm)
            dp = _nt(dm, gw_ref[g])
            dpooled.append(dp)
            ebuf[0:ts, cols] = dp * inv[g]
        ahead[...] = ebuf[0:HALO, :]
        du = []
        for g, w in enumerate(POOL_WINDOWS):
            cols = slice(g * GROUP_DIM, (g + 1) * GROUP_DIM)
            du.append(_leading_sums(ebuf, tbuf, cols, w, ts) - dpooled[g])
        du = _bf(jnp.concatenate(du, axis=-1))
        dproj_ref[:, :D] = du
        dgt_bf = _bf(dgt)
        half = D // 2
        dn0 = (_nt(du[:, :half], wpi_ref[0]) + _nt(du[:, half:], wpi_ref[1])
               + _nt(dgt_bf[:, :half], wpi_ref[2]) + _nt(dgt_bf[:, half:], wpi_ref[3]))

        xv = x_ref[...]
        r = lax.rsqrt(jnp.mean(xv * xv, axis=-1, keepdims=True) + EPS)
        xhat = xv * r
        small_ref[0:1, :] += jnp.sum(dn0 * xhat, axis=0, keepdims=True)
        dxh = dn0 * w0_ref[...]
        dx_ref[...] = dh1 + r * (dxh - xhat * jnp.mean(dxh * xhat, axis=-1, keepdims=True))

        @pl.when(i == nt - 1)
        def _():
            for cp in copies:
                cp.wait()

    row = lambda cols: pl.BlockSpec((ts, cols), lambda i: (nt - 1 - i, 0))
    outs = pl.pallas_call(
        body, name="pool_backward", grid=(nt,),
        out_shape=[jax.ShapeDtypeStruct((s, D), F32), jax.ShapeDtypeStruct((s, 2 * D), BF16),
                   jax.ShapeDtypeStruct((D, D), F32),
                   jax.ShapeDtypeStruct((GROUPS, GROUP_DIM, GROUP_DIM), F32),
                   jax.ShapeDtypeStruct((8, D), F32)] + _scatter_shapes(chip_sums),
        in_specs=[row(D), row(D), row(D), row(D), _full((1, D)), _full((N_CHIPS, D, D // 2)),
                  _full((GROUPS, GROUP_DIM, GROUP_DIM)), _full((1, D)), _full((1, D)), _full((D, D))]
                 + _any_specs(n_sums),
        out_specs=[row(D), row(2 * D), _full((D, D)), _full((GROUPS, GROUP_DIM, GROUP_DIM)), _full((8, D))]
                  + _any_specs(n_sums),
        scratch_shapes=[pltpu.VMEM((ts + HALO, D), F32), pltpu.VMEM((ts + HALO, D), F32),
                        pltpu.VMEM((HALO, D), F32),
                        pltpu.SemaphoreType.DMA((3 * n_sums,)), pltpu.SemaphoreType.DMA((3 * n_sums,))],
        compiler_params=_params("arbitrary"),
    )(x, dh1, pooled, gt, w0, wpi, gw, gb, scale, wpo, *chip_sums)
    return outs[:5], outs[5:]


def gla_project(h1, w1, wgi_q, wgk, bgk, later):
    s = h1.shape[0]
    ts = ROW_TILE
    nt = s // ts
    assert nt >= 2
    n_later = len(later)

    def body(h_ref, w1_ref, wq_ref, wgk_ref, bgk_ref, *rest):
        rest = rest[n_later:]
        qk_ref, v_ref, gate_ref, low_ref, cum_ref, n1_ref = rest[:6]
        later_refs = rest[6:6 + n_later]
        send_sems, recv_sems, wgi_ref = rest[6 + n_later:]
        gather_in_background(pl.program_id(0), nt - 1, later_refs, send_sems, recv_sems, finish=False)

        @pl.when(pl.program_id(0) == 0)
        def _():
            _assemble_gla_in(wq_ref, wgi_ref)

        hv = h_ref[...]
        r = lax.rsqrt(jnp.mean(hv * hv, axis=-1, keepdims=True) + EPS)
        n1 = _bf(hv * r * w1_ref[...])
        n1_ref[...] = n1
        qk_ref[...] = _nn(n1, wgi_ref[:, 0:2 * KEY_W])
        v_ref[...] = _bf(_nn(n1, wgi_ref[:, 2 * KEY_W:2 * KEY_W + D]))
        gate_ref[...] = _nn(n1, wgi_ref[:, 2 * KEY_W + D:GLA_MAIN])
        low = _bf(_nn(n1, wgi_ref[:, GLA_MAIN:]))
        low_ref[...] = low
        z = _nn(low, wgk_ref[...]) + bgk_ref[...]
        lg = (jnp.minimum(z, 0.0) - jnp.log(1.0 + jnp.exp(-jnp.abs(z)))) / GATE_NORM
        lower_f = _chunk_masks()[0].astype(F32)
        for r0 in range(0, ts, CHUNK):
            cum_ref[r0:r0 + CHUNK, :] = _nn_exact(lower_f, lg[r0:r0 + CHUNK, :])
        gather_in_background(pl.program_id(0), nt - 1, later_refs, send_sems, recv_sems, finish=True)

    row = lambda cols: pl.BlockSpec((ts, cols), lambda i: (i, 0))
    outs = pl.pallas_call(
        body, name="gla_project", grid=(nt,),
        out_shape=[jax.ShapeDtypeStruct((s, D), F32), jax.ShapeDtypeStruct((s, D), BF16),
                   jax.ShapeDtypeStruct((s, D), F32), jax.ShapeDtypeStruct((s, RANK_PAD), BF16),
                   jax.ShapeDtypeStruct((s, KEY_W), F32), jax.ShapeDtypeStruct((s, D), BF16)]
                  + [jax.ShapeDtypeStruct(a.shape, a.dtype) for a in later],
        in_specs=[row(D), _full((1, D)), _full((N_CHIPS, D, GLA_IN_QUARTER)),
                  _full((RANK_PAD, KEY_W)), _full((1, KEY_W))] + _any_specs(n_later),
        out_specs=[row(D), row(D), row(D), row(RANK_PAD), row(KEY_W), row(D)] + _any_specs(n_later),
        input_output_aliases={5 + k: 6 + k for k in range(n_later)},
        scratch_shapes=[pltpu.SemaphoreType.DMA((6 * n_later,)), pltpu.SemaphoreType.DMA((6 * n_later,)),
                        pltpu.VMEM((D, GLA_MAIN + RANK_PAD), BF16)],
        compiler_params=_params("arbitrary"),
    )(h1, w1, wgi_q, wgk, bgk, *later)
    return outs[:6], outs[6:]


def _assemble_gla_in(wq_ref, wfull):
    pad = jnp.zeros((CAST_ROWS, GLA_MAIN + RANK_PAD - GLA_IN), BF16)
    for r0 in range(0, D, CAST_ROWS):
        rows = slice(r0, r0 + CAST_ROWS)
        wfull[rows, :] = jnp.concatenate([wq_ref[q, rows, :] for q in range(N_CHIPS)] + [pad], axis=1)


GLA_BLOCK = 512
CHUNKS_PER_BLOCK = GLA_BLOCK // CHUNK


def _chunk_masks():
    t = lax.broadcasted_iota(jnp.int32, (CHUNK, CHUNK), 0)
    u = lax.broadcasted_iota(jnp.int32, (CHUNK, CHUNK), 1)
    return t >= u, t <= u


def _gla_chunk_terms(q, cum):
    ep = jnp.exp(cum)
    en = jnp.exp(-cum)
    qs = q * (HEAD_K ** -0.5)
    last = cum[CHUNK - 1:CHUNK, :]
    ed = jnp.exp(last - cum)
    dec = jnp.exp(last)
    return ep, en, qs, ed, dec


def gla_forward(qk, v, cum):
    s = qk.shape[0]
    nb = s // GLA_BLOCK
    nc = s // CHUNK

    def body(q_ref, k_ref, v_ref, cum_ref, o_ref, st_ref, sc_ref, state):
        @pl.when(pl.program_id(0) == 0)
        def _():
            state[...] = jnp.zeros_like(state)

        lower, _ = _chunk_masks()

        def chunk(cc, carry):
            rows = pl.ds(pl.multiple_of(cc * CHUNK, CHUNK), CHUNK)
            for h in range(HEADS):
                kc = slice(h * HEAD_K, (h + 1) * HEAD_K)
                vc = slice(h * HEAD_V, (h + 1) * HEAD_V)
                q = q_ref[rows, kc]
                k = k_ref[rows, kc]
                v = v_ref[rows, vc]
                ep, en, qs, ed, dec = _gla_chunk_terms(q, cum_ref[rows, kc])
                a = _bf(qs * ep)
                fwd = _nt(a, _bf(k * en))
                bwd = _nt(_bf(qs * en), _bf(k * ep))
                scores = _bf(jnp.where(lower, fwd, bwd))
                sc_ref[rows, h * CHUNK:(h + 1) * CHUNK] = scores
                st = state[h]
                st_ref[cc, h] = st
                o_ref[rows, vc] = _nn(scores, v) + _nt(a, _bf(st))
                state[h] = st * dec + _tn(v, _bf(k * ed))
            return carry

        lax.fori_loop(0, CHUNKS_PER_BLOCK, chunk, 0, unroll=4)

    return pl.pallas_call(
        body, name="gla_forward", grid=(nb,),
        out_shape=(jax.ShapeDtypeStruct((s, D), F32),
                   jax.ShapeDtypeStruct((nc, HEADS, HEAD_V, HEAD_K), F32),
                   jax.ShapeDtypeStruct((s, HEADS * CHUNK), BF16)),
        in_specs=[pl.BlockSpec((GLA_BLOCK, KEY_W), lambda i: (i, 0)),
                  pl.BlockSpec((GLA_BLOCK, KEY_W), lambda i: (i, 1)),
                  pl.BlockSpec((GLA_BLOCK, D), lambda i: (i, 0)),
                  pl.BlockSpec((GLA_BLOCK, KEY_W), lambda i: (i, 0))],
        out_specs=(pl.BlockSpec((GLA_BLOCK, D), lambda i: (i, 0)),
                   pl.BlockSpec((CHUNKS_PER_BLOCK, HEADS, HEAD_V, HEAD_K), lambda i: (i, 0, 0, 0)),
                   pl.BlockSpec((GLA_BLOCK, HEADS * CHUNK), lambda i: (i, 0))),
        scratch_shapes=[pltpu.VMEM((HEADS, HEAD_V, HEAD_K), F32)],
        compiler_params=_params("arbitrary"),
    )(qk, qk, v, cum)


def gla_backward(qk, v, cum, do, states, scores):
    s = qk.shape[0]
    nb = s // GLA_BLOCK

    def body(q_ref, k_ref, v_ref, cum_ref, do_ref, st_ref, sc_ref, dq_ref, dk_ref, dv_ref, dcum_ref, dstate):
        @pl.when(pl.program_id(0) == 0)
        def _():
            dstate[...] = jnp.zeros_like(dstate)

        lower, _ = _chunk_masks()
        is_last = lax.broadcasted_iota(jnp.int32, (CHUNK, HEAD_K), 0) == CHUNK - 1

        def chunk(step, carry):
            cc = CHUNKS_PER_BLOCK - 1 - step
            rows = pl.ds(pl.multiple_of(cc * CHUNK, CHUNK), CHUNK)
            for h in range(HEADS):
                kc = slice(h * HEAD_K, (h + 1) * HEAD_K)
                vc = slice(h * HEAD_V, (h + 1) * HEAD_V)
                q = q_ref[rows, kc]
                k = k_ref[rows, kc]
                v = v_ref[rows, vc]
                do_c = do_ref[rows, vc]
                ep, en, qs, ed, dec = _gla_chunk_terms(q, cum_ref[rows, kc])
                a = _bf(qs * ep)
                b = _bf(k * en)
                c = _bf(qs * en)
                dk_dec = _bf(k * ep)
                kd = _bf(k * ed)
                scores = sc_ref[rows, h * CHUNK:(h + 1) * CHUNK]
                st = st_ref[cc, h]
                dst = dstate[h]
                dst_bf = _bf(dst)

                dscores = _nt(do_c, v)
                dfwd = _bf(jnp.where(lower, dscores, 0.0))
                dbwd = _bf(jnp.where(lower, 0.0, dscores))
                dv_ref[rows, vc] = _bf(_tn(scores, do_c) + _nt(kd, dst_bf))
                da = _nn(dfwd, b) + _nn(do_c, _bf(st))
                db = _tn(dfwd, a)
                dc = _nn(dbwd, dk_dec)
                ddk = _tn(dbwd, c)
                dkd = _nn(v, dst_bf)
                ddec = jnp.sum(dst * st, axis=0, keepdims=True)
                dstate[h] = dst * dec + _tn(do_c, a)

                m = dkd * k * ed
                dq_ref[rows, kc] = _bf((da * ep + dc * en) * (HEAD_K ** -0.5))
                dk_ref[rows, kc] = _bf(db * en + ddk * ep + dkd * ed)
                dcum = (da * qs + ddk * k) * ep - (db * k + dc * qs) * en - m
                dlast = jnp.sum(m, axis=0, keepdims=True) + ddec * dec
                dcum_ref[rows, kc] = dcum + jnp.where(is_last, dlast, 0.0)
            return carry

        lax.fori_loop(0, CHUNKS_PER_BLOCK, chunk, 0, unroll=4)

    rev = lambda cols, col_block: pl.BlockSpec((GLA_BLOCK, cols), lambda i: (nb - 1 - i, col_block))
    return pl.pallas_call(
        body, name="gla_backward", grid=(nb,),
        out_shape=(jax.ShapeDtypeStruct((s, KEY_W), BF16), jax.ShapeDtypeStruct((s, KEY_W), BF16),
                   jax.ShapeDtypeStruct((s, D), BF16), jax.ShapeDtypeStruct((s, KEY_W), F32)),
        in_specs=[rev(KEY_W, 0), rev(KEY_W, 1), rev(D, 0), rev(KEY_W, 0), rev(D, 0),
                  pl.BlockSpec((CHUNKS_PER_BLOCK, HEADS, HEAD_V, HEAD_K), lambda i: (nb - 1 - i, 0, 0, 0)),
                  rev(HEADS * CHUNK, 0)],
        out_specs=(rev(KEY_W, 0), rev(KEY_W, 0), rev(D, 0), rev(KEY_W, 0)),
        scratch_shapes=[pltpu.VMEM((HEADS, HEAD_V, HEAD_K), F32)],
        compiler_params=_params("arbitrary"),
    )(qk, qk, v, cum, do, states, scores)


def head_and_loss(o, gate, h1, target, hw, wgo, wf):
    s = o.shape[0]
    ts = ROW_TILE

    def body(o_ref, gate_ref, h1_ref, tgt_ref, hw_ref, wgo_ref, wf_ref,
             dh2_ref, do_ref, dgate_ref, ggo_ref, small_ref):
        @pl.when(pl.program_id(0) == 0)
        def _():
            ggo_ref[...] = jnp.zeros_like(ggo_ref)
            small_ref[...] = jnp.zeros_like(small_ref)

        gate = gate_ref[...]
        hw = hw_ref[...]
        sg = _sigmoid(gate)
        silu = gate * sg
        ohat, ro = [], []
        for h in range(HEADS):
            oh = o_ref[:, h * HEAD_V:(h + 1) * HEAD_V]
            rh = lax.rsqrt(jnp.mean(oh * oh, axis=-1, keepdims=True) + EPS)
            ro.append(rh)
            ohat.append(oh * rh)
        ohat = jnp.concatenate(ohat, axis=-1)
        on = ohat * hw
        y2 = _bf(on * silu)
        h2 = h1_ref[...] + _nn(y2, wgo_ref[...])
        rf = lax.rsqrt(jnp.mean(h2 * h2, axis=-1, keepdims=True) + EPS)
        h2hat = h2 * rf
        wf = wf_ref[...]
        diff = h2hat * wf - tgt_ref[...]
        small_ref[2:3, :] += jnp.zeros((1, D), F32) + 0.5 * jnp.sum(diff * diff) / D
        dout = diff / D
        small_ref[0:1, :] += jnp.sum(dout * h2hat, axis=0, keepdims=True)
        dxh = dout * wf
        dh2 = rf * (dxh - h2hat * jnp.mean(dxh * h2hat, axis=-1, keepdims=True))
        dh2_ref[...] = dh2
        dh2_bf = _bf(dh2)
        ggo_ref[...] += _tn(y2, dh2_bf)
        dy2 = _nt(dh2_bf, wgo_ref[...])
        don = dy2 * silu
        dgate_ref[...] = _bf(dy2 * on * (sg * (1.0 + gate * (1.0 - sg))))
        ghw = jnp.sum(don * ohat, axis=0, keepdims=True)
        small_ref[1:2, 0:HEAD_V] += sum(ghw[:, h * HEAD_V:(h + 1) * HEAD_V] for h in range(HEADS))
        dohat = don * hw
        for h in range(HEADS):
            cols = slice(h * HEAD_V, (h + 1) * HEAD_V)
            oh, dh = ohat[:, cols], dohat[:, cols]
            do_ref[:, cols] = _bf(ro[h] * (dh - oh * jnp.mean(dh * oh, axis=-1, keepdims=True)))

    row = lambda cols: pl.BlockSpec((ts, cols), lambda i: (i, 0))
    act = jax.ShapeDtypeStruct((s, D), F32)
    act_bf = jax.ShapeDtypeStruct((s, D), BF16)
    return pl.pallas_call(
        body, name="head_and_loss", grid=(s // ts,),
        out_shape=(act, act_bf, act_bf, jax.ShapeDtypeStruct((D, D), F32), jax.ShapeDtypeStruct((8, D), F32)),
        in_specs=[row(D), row(D), row(D), row(D),
                  _full((1, D)), _full((D, D)), _full((1, D))],
        out_specs=(row(D), row(D), row(D), _full((D, D)), _full((8, D))),
        compiler_params=_params("arbitrary"),
    )(o, gate, h1, target, hw, wgo, wf)


def gla_project_backward(dq, dk, dv, dgate, dcum, low, h1, dh2, w1, wgi_q, wgk, bgk):
    s = h1.shape[0]
    ts = ROW_TILE

    def body(dq_ref, dk_ref, dv_ref, dgate_ref, dcum_ref, low_ref, h1_ref, dh2_ref, w1_ref,
             wq_ref, wgk_ref, bgk_ref, dh1_ref, dproj_ref, ggk_ref, small_ref, wgi_ref):
        @pl.when(pl.program_id(0) == 0)
        def _():
            ggk_ref[...] = jnp.zeros_like(ggk_ref)
            small_ref[...] = jnp.zeros_like(small_ref)
            _assemble_gla_in(wq_ref, wgi_ref)

        low = low_ref[...]
        z = _nn(low, wgk_ref[...]) + bgk_ref[...]
        upper_f = _chunk_masks()[1].astype(F32)
        dlg = jnp.concatenate([_nn_exact(upper_f, dcum_ref[r0:r0 + CHUNK, :]) for r0 in range(0, ts, CHUNK)],
                              axis=0)
        dz = dlg * (1.0 / GATE_NORM) * _sigmoid(-z)
        dz_bf = _bf(dz)
        ggk_ref[...] += _tn(low, dz_bf)
        small_ref[1:2, 0:KEY_W] += jnp.sum(dz, axis=0, keepdims=True)
        dlow = _bf(_nt(dz_bf, wgk_ref[...]))
        dproj_ref[:, GLA_MAIN:] = dlow
        dn1 = _nt(dlow, wgi_ref[:, GLA_MAIN:])
        for ref, lo, hi in ((dq_ref, 0, KEY_W), (dk_ref, KEY_W, 2 * KEY_W),
                            (dv_ref, 2 * KEY_W, 2 * KEY_W + D), (dgate_ref, 2 * KEY_W + D, GLA_MAIN)):
            piece = ref[...]
            dproj_ref[:, lo:hi] = piece
            dn1 = dn1 + _nt(piece, wgi_ref[:, lo:hi])
        hv = h1_ref[...]
        r = lax.rsqrt(jnp.mean(hv * hv, axis=-1, keepdims=True) + EPS)
        hhat = hv * r
        small_ref[0:1, :] += jnp.sum(dn1 * hhat, axis=0, keepdims=True)
        dxh = dn1 * w1_ref[...]
        dh1_ref[...] = dh2_ref[...] + r * (dxh - hhat * jnp.mean(dxh * hhat, axis=-1, keepdims=True))

    row = lambda cols: pl.BlockSpec((ts, cols), lambda i: (i, 0))
    return pl.pallas_call(
        body, name="gla_project_backward", grid=(s // ts,),
        out_shape=(jax.ShapeDtypeStruct((s, D), F32), jax.ShapeDtypeStruct((s, GLA_MAIN + RANK_PAD), BF16),
                   jax.ShapeDtypeStruct((RANK_PAD, KEY_W), F32),
                   jax.ShapeDtypeStruct((8, D), F32)),
        in_specs=[row(KEY_W), row(KEY_W), row(D), row(D), row(KEY_W), row(RANK_PAD), row(D), row(D),
                  _full((1, D)), _full((N_CHIPS, D, GLA_IN_QUARTER)), _full((RANK_PAD, KEY_W)),
                  _full((1, KEY_W))],
        out_specs=(row(D), row(GLA_MAIN + RANK_PAD), _full((RANK_PAD, KEY_W)), _full((8, D))),
        scratch_shapes=[pltpu.VMEM((D, GLA_MAIN + RANK_PAD), BF16)],
        compiler_params=_params("arbitrary"),
    )(dq, dk, dv, dgate, dcum, low, h1, dh2, w1, wgi_q, wgk, bgk)


def _groups_from_quarters(a):
    return a.reshape(N_CHIPS, GROUPS, 64, GROUP_DIM).transpose(1, 0, 2, 3).reshape(GROUPS, GROUP_DIM, GROUP_DIM)


def _quarters_from_groups(a):
    return a.reshape(GROUPS, N_CHIPS, 64, GROUP_DIM).transpose(1, 0, 2, 3).reshape(N_CHIPS, GROUP_DIM, GROUP_DIM)


def local_gradients(xs, target, w0, w1, wf, wpi, gw, gb, scale, wpo, gla_quarters, wgk, bgk, hw_tiled, place):
    wgi_q, wgo_q = gla_quarters
    (h1, pooled, gt, n0), (wgi_q,) = pool_forward(xs, w0, wpi, gw, gb, scale, wpo, [wgi_q])
    (qk, v, gate, low, cum, n1), (wgo_q,) = gla_project(h1, w1, wgi_q, wgk, bgk, [wgo_q])
    wgo = wgo_q.reshape(D, D)
    o, states, scores = gla_forward(qk, v, cum)

    dh2, do, dgate, g_gla_out, small_top = head_and_loss(o, gate, h1, target, hw_tiled, wgo, wf)
    dq, dk, dv, dcum = gla_backward(qk, v, cum, do, states, scores)
    dh1, dproj, g_gk_pad, small_gla = gla_project_backward(
        dq, dk, dv, dgate, dcum, low, h1, dh2, w1, wgi_q, wgk, bgk)
    g_gla_in, _ = matmul_tn(n1, dproj, "grad_gla_in", tile_n=(GLA_MAIN + RANK_PAD) // 5)

    def chip_sums(grads, tag):
        return add_halves(grads, place, "add_halves_" + tag)

    gla_sums = chip_sums([g_gla_in, g_gla_out.reshape(N_CHIPS, D // N_CHIPS, D)], "gla")
    (dx, dpool, g_pool_out, g_group_w, small_pool), gla_got = pool_backward(
        xs, dh1, pooled, gt, w0, wpi, gw, gb, scale, wpo, [b for _, b in gla_sums])
    mix_sums = chip_sums([_quarters_from_groups(g_group_w), g_pool_out.reshape(N_CHIPS, D // N_CHIPS, D)], "pool_mix")
    g_pool_in, mix_got = matmul_tn(n0, dpool, "grad_pool_in", tile_n=D // 2, by_column_tile=True,
                                   chip_sums=[b for _, b in mix_sums])

    in_sums = chip_sums([g_pool_in], "pool_in")
    in_got = scatter_to_owners([b for _, b in in_sums], "scatter_to_owners_pool_in")
    reduced, total = join_halves(
        add_parts([f for f, _ in in_sums + mix_sums + gla_sums], list(in_got) + list(mix_got) + list(gla_got),
                  place, "add_parts"),
        small_pool, small_gla, small_top, g_gk_pad)
    return dx, reduced, total


def kernel(x, norm_w, pool_in_w, pool_group_w, pool_group_b, pool_scale, pool_out_w, gla_in_w, gla_gk_w, gla_gk_b, gla_head_norm_w, gla_out_w, final_norm_w, loss_target, m_norm_w, m_pool_in_w, m_pool_group_w, m_pool_group_b, m_pool_scale, m_pool_out_w, m_gla_in_w, m_gla_gk_w, m_gla_gk_b, m_gla_head_norm_w, m_gla_out_w, m_final_norm_w, v_norm_w, v_pool_in_w, v_pool_group_w, v_pool_group_b, v_pool_scale, v_pool_out_w, v_gla_in_w, v_gla_gk_w, v_gla_gk_b, v_gla_head_norm_w, v_gla_out_w, v_final_norm_w):
    xs = x[0]
    target = loss_target[0]
    q_chip = 2 * lax.axis_index("x") + lax.axis_index("y")
    place = jnp.stack([lax.axis_index("c"), q_chip]).astype(jnp.int32)

    (wpi, gw_q, wpo_q, wgi_q, wgo_q), small_all = allgather_weights(
        [pool_in_w[0], pool_group_w[0].reshape(GROUP_DIM, GROUP_DIM), pool_out_w[0], gla_in_w[0], gla_out_w[0]],
        exchange=(True, True, True, False, False),
        smalls=[gla_gk_b, gla_head_norm_w, pool_group_b[0], gla_gk_w[0]])
    gw = _groups_from_quarters(gw_q)
    wpo = wpo_q.reshape(D, D)
    small_all = small_all[0::2]
    bgk = small_all[:, 0, :].reshape(1, KEY_W)
    hw = small_all[:, 1, 0:64].reshape(1, HEAD_V)
    gb = small_all[:, 2:2 + GROUPS, 0:64].transpose(1, 0, 2).reshape(1, D)
    wgk16 = small_all[:, 8:8 + GATE_RANK, :].transpose(1, 0, 2).reshape(GATE_RANK, KEY_W)
    wgk = _bf(jnp.pad(wgk16, ((0, RANK_PAD - GATE_RANK), (0, 0))))
    hw_tiled = jnp.tile(hw, (1, HEADS))

    w0 = norm_w[0:1]
    w1 = norm_w[1:2]
    wf = final_norm_w.reshape(1, D)

    dx, reduced, total = local_gradients(
        xs, target, w0, w1, wf, wpi, gw, gb, pool_scale, wpo, [wgi_q, wgo_q], wgk, bgk, hw_tiled, place)
    r_pool_in, r_group_w, r_pool_out, r_gla_in, r_gla_out = reduced
    r_group_w = r_group_w.reshape(GROUPS, 64, GROUP_DIM)

    loss = total[7, 0]
    g_norm = jnp.stack([total[0], total[3]])
    g_scale = total[1:2]
    g_final = total[5]
    pick = lambda full, width: lax.dynamic_slice_in_dim(full, q_chip * width, width, axis=-1)
    g_gk_b = pick(total[4:5, 0:KEY_W], 128)
    g_hnw = pick(total[6:7, 0:HEAD_V], 64)
    g_group_b = pick(total[2].reshape(GROUPS, GROUP_DIM), 64)[None]
    g_gk_w = pick(total[8:16].reshape(GATE_RANK, KEY_W), 128)[None]

    turn = lambda a: jnp.transpose(a, (2, 0, 1))
    back = lambda a: jnp.transpose(a, (1, 2, 0))
    as2d = lambda a, w: a.reshape(-1, w.shape[-1])
    big_names = ("pool_in_w", "pool_group_w", "pool_out_w", "gla_in_w", "gla_out_w")
    big_args = [(pool_in_w, r_pool_in[None], m_pool_in_w, v_pool_in_w),
                (pool_group_w, r_group_w[None], m_pool_group_w, v_pool_group_w),
                (pool_out_w, r_pool_out[None], m_pool_out_w, v_pool_out_w),
                (gla_in_w, r_gla_in[None], m_gla_in_w, v_gla_in_w),
                (gla_out_w, r_gla_out[None], m_gla_out_w, v_gla_out_w)]
    to_kernel = lambda n, a, w: turn(a) if n == "gla_in_w" else as2d(a, w)
    from_kernel = lambda n, a, w: back(a) if n == "gla_in_w" else a.reshape(w.shape)
    big_in = [tuple(to_kernel(n, a, p[0]) for a in p) for n, p in zip(big_names, big_args)]
    big_out = adamw(big_in, "adamw")
    big = {n: (from_kernel(n, i[1], p[0]),) + tuple(from_kernel(n, o, p[0]) for o in out)
           for n, p, i, out in zip(big_names, big_args, big_in, big_out)}

    small_names = ("norm_w", "pool_group_b", "pool_scale", "gla_gk_w", "gla_gk_b", "gla_head_norm_w",
                   "final_norm_w")
    small_args = [(norm_w, g_norm, m_norm_w, v_norm_w),
                  (pool_group_b, g_group_b, m_pool_group_b, v_pool_group_b),
                  (pool_scale, g_scale, m_pool_scale, v_pool_scale),
                  (gla_gk_w, g_gk_w, m_gla_gk_w, v_gla_gk_w),
                  (gla_gk_b, g_gk_b, m_gla_gk_b, v_gla_gk_b),
                  (gla_head_norm_w, g_hnw, m_gla_head_norm_w, v_gla_head_norm_w),
                  (final_norm_w, g_final, m_final_norm_w, v_final_norm_w)]
    small_out = adamw_small([tuple(as2d(a, p[0]) for a in p) for p in small_args])
    small = {n: (p[1].reshape(p[0].shape),) + tuple(o.reshape(p[0].shape) for o in out)
             for n, p, out in zip(small_names, small_args, small_out)}
    results = [
        small["norm_w"],
        big["pool_in_w"],
        big["pool_group_w"],
        small["pool_group_b"],
        small["pool_scale"],
        big["pool_out_w"],
        big["gla_in_w"],
        small["gla_gk_w"],
        small["gla_gk_b"],
        small["gla_head_norm_w"],
        big["gla_out_w"],
        small["final_norm_w"],
    ]
    grads, deltas, new_m, new_v = zip(*results)
    return (loss, dx[None], *grads, *deltas, *new_m, *new_v)
```

```python
import jax
import jax.numpy as jnp
from jax import lax
from jax.experimental import pallas as pl
from jax.experimental.pallas import tpu as pltpu

F32 = jnp.float32
BF16 = jnp.bfloat16
MESH = pl.DeviceIdType.MESH

D = 1024
POOL_WINDOWS = (2, 4, 8, 16)
GROUPS = 4
GROUP_DIM = 256
HEADS = 4
HEAD_K = 128
HEAD_V = 256
KEY_W = 512
CHUNK = 64
GATE_RANK = 16
GATE_NORM = 16.0
GLA_IN = 3088
GLA_MAIN = 3072
RANK_PAD = 128
EPS = 1e-6
HALO = 32

ADAM_LR = 0.001
ADAM_B1 = 0.9
ADAM_B2 = 0.999
ADAM_EPS = 1e-08
ADAM_WD = 0.01
ADAM_STEP = 10

N_CHIPS = 4
N_DEV = 8
GLA_IN_QUARTER = GLA_IN // N_CHIPS

VMEM_LIMIT = 56 * 1024 * 1024


def _nn(a, b):
    return lax.dot_general(a, b, (((1,), (0,)), ((), ())), preferred_element_type=F32)


def _nt(a, b):
    return lax.dot_general(a, b, (((1,), (1,)), ((), ())), preferred_element_type=F32)


def _tn(a, b):
    return lax.dot_general(a, b, (((0,), (0,)), ((), ())), preferred_element_type=F32)


def _nn_exact(a, b):
    return lax.dot_general(a, b, (((1,), (0,)), ((), ())), preferred_element_type=F32,
                           precision=lax.Precision.HIGHEST)


def _bf(a):
    return a.astype(BF16)


def _params(*sem):
    return pltpu.CompilerParams(dimension_semantics=sem, vmem_limit_bytes=VMEM_LIMIT)


def _full(shape):
    return pl.BlockSpec(shape, lambda i: (0,) * len(shape))


def _position():
    return lax.axis_index("x"), lax.axis_index("y"), lax.axis_index("c")


def _gather_small(in_ref, all_ref, send_sems, recv_sems, local_sem):
    x, y, c = _position()
    me = 4 * x + 2 * y + c
    mine = pltpu.make_async_copy(in_ref, all_ref.at[me], local_sem)
    mine.start()
    sends = []
    for k in range(N_DEV - 1):
        fx, fy, fc = (k + 1) >> 2 & 1, (k + 1) >> 1 & 1, (k + 1) & 1
        cp = pltpu.make_async_remote_copy(
            src_ref=in_ref, dst_ref=all_ref.at[me],
            send_sem=send_sems.at[k], recv_sem=recv_sems.at[k],
            device_id=(x ^ fx, y ^ fy, c ^ fc), device_id_type=MESH)
        cp.start()
        sends.append(cp)
    def wait():
        for k in range(N_DEV - 1):
            fx, fy, fc = (k + 1) >> 2 & 1, (k + 1) >> 1 & 1, (k + 1) & 1
            src_dev = 4 * (x ^ fx) + 2 * (y ^ fy) + (c ^ fc)
            pltpu.make_async_remote_copy(
                src_ref=in_ref, dst_ref=all_ref.at[src_dev],
                send_sem=send_sems.at[k], recv_sem=recv_sems.at[k],
                device_id=(x, y, c), device_id_type=MESH).wait_recv()
        for cp in sends:
            cp.wait_send()
        mine.wait()

    return wait


SMALL_SEMS = [pltpu.SemaphoreType.DMA((N_DEV - 1,)), pltpu.SemaphoreType.DMA((N_DEV - 1,)),
              pltpu.SemaphoreType.DMA]
VMEM_SPEC = pl.BlockSpec(memory_space=pltpu.VMEM)


def _other_chips(x, y):
    return [(1 - x, y), (x, 1 - y), (1 - x, 1 - y)]


def _any_specs(n):
    return [pl.BlockSpec(memory_space=pl.ANY)] * n


def _halves(rows, c):
    half = rows // 2
    return pl.ds(c * half, half), pl.ds((1 - c) * half, half)


CAST_ROWS = 256


def _gather_copy(out_ref, send_sems, recv_sems, k, quarter, half, to, src=None):
    dst = out_ref.at[quarter, half]
    return pltpu.make_async_remote_copy(
        src_ref=dst if src is None else src, dst_ref=dst,
        send_sem=send_sems.at[k], recv_sem=recv_sems.at[k], device_id=to, device_id_type=MESH)


SMALL_IN_ROWS = 24


def allgather_weights(quarters, exchange, smalls):
    n = len(quarters)
    shapes = [w.shape for w in quarters]
    moved = [i for i in range(n) if exchange[i]]

    def body(*refs):
        w_refs, (gkb_ref, hnw_ref, gb_ref, gkw_ref) = refs[:n], refs[n:n + 4]
        out_refs, small_all_ref = refs[n + 4:2 * n + 4], refs[2 * n + 4]
        refs = refs[2 * n + 5:]
        f32_bufs, bf_bufs = refs[:n], refs[n:2 * n]
        send_sems, recv_sems, local_sems, small_ref = refs[2 * n:2 * n + 4]
        small_ref[...] = jnp.zeros_like(small_ref)
        small_ref[0:1, :] = gkb_ref[...]
        small_ref[1:2, 0:64] = hnw_ref[...]
        small_ref[2:2 + GROUPS, 0:64] = gb_ref[...]
        small_ref[8:8 + GATE_RANK, :] = gkw_ref[...]
        wait_small = _gather_small(small_ref, small_all_ref, *refs[2 * n + 4:])
        x, y, c = _position()
        q = 2 * x + y
        sibling = (x, y, 1 - c)
        chips = _other_chips(x, y)

        def copy(k, i, quarter, half, to, src=None):
            return _gather_copy(out_refs[i], send_sems, recv_sems, k * n + i, quarter, half, to, src)

        loads = [pltpu.make_async_copy(w_refs[i], f32_bufs[i], local_sems.at[i]) for i in range(n)]
        for cp in loads:
            cp.start()
        keeps, sends = [], []
        for i in range(n):
            loads[i].wait()
            for r0 in range(0, shapes[i][0], CAST_ROWS):
                bf_bufs[i][r0:r0 + CAST_ROWS, :] = _bf(f32_bufs[i][r0:r0 + CAST_ROWS, :])
            keep = pltpu.make_async_copy(bf_bufs[i], out_refs[i].at[q], local_sems.at[n + i])
            keep.start()
            keeps.append(keep)
            if not exchange[i]:
                continue
            mine, _ = _halves(shapes[i][0], c)
            for j, chip in enumerate(chips):
                cp = copy(j, i, q, mine, (*chip, c), src=bf_bufs[i].at[mine])
                cp.start()
                sends.append(cp)
        for j, chip in enumerate(chips):
            qj = 2 * chip[0] + chip[1]
            for i in moved:
                mine, _ = _halves(shapes[i][0], c)
                copy(j, i, qj, mine, (x, y, c)).wait_recv()
                cp = copy(3 + j, i, qj, mine, sibling)
                cp.start()
                sends.append(cp)
        for j, chip in enumerate(chips):
            qj = 2 * chip[0] + chip[1]
            for i in moved:
                _, other = _halves(shapes[i][0], c)
                copy(3 + j, i, qj, other, (x, y, c)).wait_recv()
        wait_small()
        for cp in sends:
            cp.wait_send()
        for cp in keeps:
            cp.wait()

    outs = pl.pallas_call(
        body, name="allgather_weights",
        out_shape=[jax.ShapeDtypeStruct((N_CHIPS, *s), BF16) for s in shapes]
                  + [jax.ShapeDtypeStruct((N_DEV, SMALL_IN_ROWS, 128), F32)],
        in_specs=_any_specs(n) + [VMEM_SPEC] * 4, out_specs=_any_specs(n) + [VMEM_SPEC],
        scratch_shapes=([pltpu.VMEM(s, F32) for s in shapes] + [pltpu.VMEM(s, BF16) for s in shapes]
                        + [pltpu.SemaphoreType.DMA((6 * n,)), pltpu.SemaphoreType.DMA((6 * n,)),
                           pltpu.SemaphoreType.DMA((2 * n,)), pltpu.VMEM((SMALL_IN_ROWS, 128), F32)] + SMALL_SEMS),
        compiler_params=pltpu.CompilerParams(vmem_limit_bytes=VMEM_LIMIT),
    )(*quarters, *smalls)
    return outs[:n], outs[n]


def _scatter_copies(b_refs, got_refs, send_sems, recv_sems):
    n = len(b_refs)
    x, y, c = _position()
    copies = []
    for j, chip in enumerate(_other_chips(x, y)):
        qj = 2 * chip[0] + chip[1]
        for i in range(n):
            copies.append(pltpu.make_async_remote_copy(
                src_ref=b_refs[i].at[qj], dst_ref=got_refs[i].at[j],
                send_sem=send_sems.at[j * n + i], recv_sem=recv_sems.at[j * n + i],
                device_id=(*chip, c), device_id_type=MESH))
    return copies


def _scatter_shapes(chip_sums):
    return [jax.ShapeDtypeStruct((N_CHIPS - 1, *b.shape[1:]), BF16) for b in chip_sums]


SMALL_SUM_ROWS = 16


def join_halves(reduced, small_pool, small_gla, small_top, g_gk_pad):
    n = len(reduced)

    def body(*refs):
        pool_ref, gla_ref, top_ref, gk_ref = refs[n:n + 4]
        buf_refs, total_ref = refs[n + 4:2 * n + 4], refs[2 * n + 4]
        send_sems, recv_sems, all_ref, small_ref = refs[2 * n + 5:2 * n + 9]
        small_ref[0:3, :] = pool_ref[0:3, :]
        small_ref[3:5, :] = gla_ref[0:2, :]
        small_ref[5:8, :] = top_ref[0:3, :]
        for r in range(GATE_RANK):
            small_ref[8 + r // 2:9 + r // 2, (r % 2) * KEY_W:(r % 2 + 1) * KEY_W] = gk_ref[r:r + 1, :]
        x, y, c = _position()
        copies = []
        for i in range(n):
            mine, _ = _halves(buf_refs[i].shape[0], c)
            cp = pltpu.make_async_remote_copy(
                src_ref=buf_refs[i].at[mine], dst_ref=buf_refs[i].at[mine],
                send_sem=send_sems.at[i], recv_sem=recv_sems.at[i],
                device_id=(x, y, 1 - c), device_id_type=MESH)
            cp.start()
            copies.append(cp)
        _gather_small(small_ref, all_ref, *refs[2 * n + 9:])()
        total = all_ref[0]
        for dev in range(1, N_DEV):
            total = total + all_ref[dev]
        total_ref[...] = total
        for cp in copies:
            cp.wait()

    outs = pl.pallas_call(
        body, name="join_halves",
        out_shape=[jax.ShapeDtypeStruct(r.shape, F32) for r in reduced]
                  + [jax.ShapeDtypeStruct((SMALL_SUM_ROWS, D), F32)],
        in_specs=_any_specs(n) + [VMEM_SPEC] * 4, out_specs=_any_specs(n) + [VMEM_SPEC],
        input_output_aliases={i: i for i in range(n)},
        scratch_shapes=[pltpu.SemaphoreType.DMA((n,)), pltpu.SemaphoreType.DMA((n,)),
                        pltpu.VMEM((N_DEV, SMALL_SUM_ROWS, D), F32), pltpu.VMEM((SMALL_SUM_ROWS, D), F32)]
                       + SMALL_SEMS,
    )(*reduced, small_pool, small_gla, small_top, g_gk_pad)
    return outs[:n], outs[n]


ADD_ROWS = 512
ADD_HALVES_ROWS = 128
ADD_HALVES_AHEAD = 2


def _spans(counts):
    starts, total = [], 0
    for count in counts:
        starts.append(total)
        total += count
    return starts, total


def _local_step(t, start, count):
    return jnp.clip(t - start, 0, count - 1)


def add_halves(grads, place, name, scatter=False):
    n = len(grads)
    whole = [len(g.shape) == 2 for g in grads]
    halves = [g.shape[-2] // 2 for g in grads]
    cols = [GLA_IN_QUARTER if w else g.shape[-1] for g, w in zip(grads, whole)]
    rbs = [min(ADD_HALVES_ROWS, h) for h in halves]
    counts = [h // rb for h, rb in zip(halves, rbs)]
    starts, total = _spans(counts)
    half_shapes = [(*g.shape[:-2], h, g.shape[-1]) for g, h in zip(grads, halves)]

    def rows_of(ref, i, start):
        return ref.at[pl.ds(start, rbs[i])] if whole[i] else ref.at[:, pl.ds(start, rbs[i])]

    def body(place_ref, *refs):
        a_refs, src_refs = refs[:n], refs[n:2 * n]
        f_refs, h_refs = refs[2 * n:3 * n], refs[3 * n:4 * n]
        their_refs, rest = refs[4 * n:5 * n], refs[5 * n:]
        send_sems, recv_sems = rest[:2]
        t = pl.program_id(0)
        q = place_ref[1]
        x, y, c = _position()
        sum_refs = rest[2:2 + n] if scatter else h_refs

        def to_owners(i, k):
            out_sems, in_sems = rest[2 + n:]
            rows = pl.ds(k * rbs[i], rbs[i])
            return [pltpu.make_async_remote_copy(
                src_ref=sum_refs[i].at[2 * chip[0] + chip[1], rows], dst_ref=h_refs[i].at[j, rows],
                send_sem=out_sems.at[3 * (starts[i] + k) + j], recv_sem=in_sems.at[3 * (starts[i] + k) + j],
                device_id=(*chip, c), device_id_type=MESH) for j, chip in enumerate(_other_chips(x, y))]

        copies = [[pltpu.make_async_remote_copy(
            src_ref=rows_of(src_refs[i], i, (1 - c) * halves[i] + k * rbs[i]),
            dst_ref=rows_of(their_refs[i], i, k * rbs[i]),
            send_sem=send_sems.at[starts[i] + k], recv_sem=recv_sems.at[starts[i] + k],
            device_id=(x, y, 1 - c), device_id_type=MESH) for k in range(counts[i])] for i in range(n)]

        in_order = [cp for of_matrix in copies for cp in of_matrix]

        @pl.when(t == 0)
        def _():
            for cp in in_order[:ADD_HALVES_AHEAD]:
                cp.start()

        for i in range(n):
            for k in range(counts[i]):
                @pl.when(t == starts[i] + k)
                def _(i=i, k=k):
                    copies[i][k].wait_recv()
                    for cp in in_order[starts[i] + k + ADD_HALVES_AHEAD:][:1]:
                        cp.start()
                    b_ref = rows_of(their_refs[i], i, k * rbs[i])
                    h_ref = sum_refs[i].at[:, pl.ds(k * rbs[i], rbs[i])] if scatter else h_refs[i]
                    if not whole[i]:
                        h_ref[...] = _bf(a_refs[i][...] + b_ref[...])
                        f_refs[i][...] = a_refs[i][q] + b_ref[q]
                    else:
                        total_i = a_refs[i][...] + b_ref[...]
                        for k4 in range(N_CHIPS):
                            piece = total_i[:, k4 * cols[i]:(k4 + 1) * cols[i]]
                            h_ref[k4] = _bf(piece)

                            @pl.when(q == k4)
                            def _():
                                f_refs[i][...] = piece
                    if scatter:
                        for cp in to_owners(i, k):
                            cp.start()

        @pl.when(t == total - 1)
        def _():
            for cp in in_order:
                cp.wait_send()
            if scatter:
                for i in range(n):
                    for k in range(counts[i]):
                        for cp in to_owners(i, k):
                            cp.wait()

    def specs(i):
        step = lambda t: _local_step(t, starts[i], counts[i])
        by_quarter = (N_CHIPS, rbs[i], cols[i])
        block = (rbs[i], grads[i].shape[-1]) if whole[i] else by_quarter
        lead = () if whole[i] else (0,)
        mine = pl.BlockSpec(block, lambda t, place: (*lead, place[0] * counts[i] + step(t), 0))
        sums = pl.BlockSpec(by_quarter, lambda t, place: (0, step(t), 0))
        own = pl.BlockSpec(by_quarter[1:], lambda t, place: (step(t), 0))
        return mine, own, sums

    all_specs = [specs(i) for i in range(n)]
    sum_shapes = [(N_CHIPS, h, cl) for h, cl in zip(halves, cols)]
    scratch = [pltpu.VMEM(sh, F32) for sh in half_shapes]
    scratch += [pltpu.SemaphoreType.DMA((total,)), pltpu.SemaphoreType.DMA((total,))]
    if scatter:
        scratch += [pltpu.VMEM(sh, BF16) for sh in sum_shapes]
        scratch += [pltpu.SemaphoreType.DMA((3 * total,)), pltpu.SemaphoreType.DMA((3 * total,))]
    outs = pl.pallas_call(
        body, name=name,
        grid_spec=pltpu.PrefetchScalarGridSpec(
            num_scalar_prefetch=1, grid=(total,),
            in_specs=[sp[0] for sp in all_specs] + _any_specs(n),
            out_specs=[sp[1] for sp in all_specs] + (_any_specs(n) if scatter else [sp[2] for sp in all_specs]),
            scratch_shapes=scratch),
        out_shape=[jax.ShapeDtypeStruct((h, cl), F32) for h, cl in zip(halves, cols)]
                  + [jax.ShapeDtypeStruct((N_CHIPS - 1 if scatter else N_CHIPS, *sh[1:]), BF16) for sh in sum_shapes],
        compiler_params=_params("arbitrary"),
    )(place, *grads, *grads)
    return list(zip(outs[:n], outs[n:]))


def add_parts(owns, gots, place, name):
    n = len(owns)
    shapes = [g.shape for g in gots]
    rbs = [min(ADD_ROWS, sh[1]) for sh in shapes]
    counts = [sh[1] // rb for sh, rb in zip(shapes, rbs)]
    starts, total = _spans(counts)

    def body(place_ref, *refs):
        o_refs, g_refs, out_refs = refs[:n], refs[n:2 * n], refs[2 * n:]
        t = pl.program_id(0)
        for i in range(n):
            @pl.when((t >= starts[i]) & (t < starts[i] + counts[i]))
            def _(i=i):
                total_i = o_refs[i][...]
                for j in range(N_CHIPS - 1):
                    total_i = total_i + g_refs[i][j].astype(F32)
                out_refs[i][...] = total_i

    def specs(i):
        rb, cols = rbs[i], shapes[i][2]
        step = lambda t: _local_step(t, starts[i], counts[i])
        return (pl.BlockSpec((rb, cols), lambda t, place: (step(t), 0)),
                pl.BlockSpec((N_CHIPS - 1, rb, cols), lambda t, place: (0, step(t), 0)),
                pl.BlockSpec((rb, cols), lambda t, place: (place[0] * counts[i] + step(t), 0)))

    all_specs = [specs(i) for i in range(n)]
    return pl.pallas_call(
        body, name=name,
        grid_spec=pltpu.PrefetchScalarGridSpec(
            num_scalar_prefetch=1, grid=(total,),
            in_specs=[sp[0] for sp in all_specs] + [sp[1] for sp in all_specs],
            out_specs=[sp[2] for sp in all_specs]),
        out_shape=[jax.ShapeDtypeStruct((2 * sh[1], sh[2]), F32) for sh in shapes],
        compiler_params=_params("arbitrary"),
    )(place, *owns, *gots)


def _adam_math(w, g, m, v):
    m = ADAM_B1 * m + (1.0 - ADAM_B1) * g
    v = ADAM_B2 * v + (1.0 - ADAM_B2) * (g * g)
    m_hat = m / (1.0 - ADAM_B1 ** ADAM_STEP)
    v_hat = v / (1.0 - ADAM_B2 ** ADAM_STEP)
    delta = -ADAM_LR * (m_hat / (jnp.sqrt(v_hat) + ADAM_EPS) + ADAM_WD * w)
    return delta, m, v


ADAM_BLOCK_BYTES = 2 ** 19
ADAM_MOST_STEPS = 8


def adamw(params, name):
    n = len(params)
    shapes = [p[0].shape for p in params]

    def tile_rows(shape):
        rows, cols = shape[0], shape[-1]
        aligned = 1 if len(shape) == 3 else 8
        divisors = [t for t in range(aligned, rows + 1, aligned) if rows % t == 0]
        tile = max(t for t in divisors if t * cols * 4 <= ADAM_BLOCK_BYTES)
        if rows // tile > ADAM_MOST_STEPS:
            tile = min(t for t in divisors if rows // t <= ADAM_MOST_STEPS)
        return tile

    tiles = [tile_rows(sh) for sh in shapes]
    counts = [sh[0] // tl for sh, tl in zip(shapes, tiles)]
    starts, total = _spans(counts)

    def body(*refs):
        ins, outs = refs[:4 * n], refs[4 * n:]
        t = pl.program_id(0)
        for i in range(n):
            @pl.when((t >= starts[i]) & (t < starts[i] + counts[i]))
            def _(i=i):
                w_ref, g_ref, m_ref, v_ref = ins[4 * i:4 * i + 4]
                d, nm, nv = _adam_math(w_ref[...], g_ref[...], m_ref[...], v_ref[...])
                outs[3 * i][...] = d
                outs[3 * i + 1][...] = nm
                outs[3 * i + 2][...] = nv

    def spec(i):
        block = (tiles[i],) + shapes[i][1:]
        zeros = (0,) * (len(block) - 1)
        return pl.BlockSpec(block, lambda t: (_local_step(t, starts[i], counts[i]),) + zeros)

    outs = pl.pallas_call(
        body, name=name, grid=(total,),
        out_shape=[jax.ShapeDtypeStruct(sh, F32) for sh in shapes for _ in range(3)],
        in_specs=[spec(i) for i in range(n) for _ in range(4)],
        out_specs=[spec(i) for i in range(n) for _ in range(3)],
        compiler_params=_params("arbitrary"),
    )(*[a for p in params for a in p])
    return [tuple(outs[3 * i:3 * i + 3]) for i in range(n)]


def adamw_small(params):
    n = len(params)

    def body(*refs):
        ins, outs = refs[:4 * n], refs[4 * n:]
        for k in range(n):
            w_ref, g_ref, m_ref, v_ref = ins[4 * k:4 * k + 4]
            d, nm, nv = _adam_math(w_ref[...], g_ref[...], m_ref[...], v_ref[...])
            outs[3 * k][...] = d
            outs[3 * k + 1][...] = nm
            outs[3 * k + 2][...] = nv

    flat = [a for p in params for a in p]
    outs = pl.pallas_call(
        body, name="adamw_small",
        out_shape=[jax.ShapeDtypeStruct(p[0].shape, F32) for p in params for _ in range(3)],
        in_specs=[VMEM_SPEC] * (4 * n), out_specs=[VMEM_SPEC] * (3 * n),
    )(*flat)
    return [tuple(outs[3 * k:3 * k + 3]) for k in range(n)]


def matmul_tn(a, b, name, tile_n, by_column_tile=False, chip_sums=()):
    s, m = a.shape
    n = b.shape[1]
    n_sums = len(chip_sums)
    steps = n // tile_n
    if by_column_tile:
        out_shape = jax.ShapeDtypeStruct((steps, m, tile_n), F32)
        out_spec = pl.BlockSpec((None, m, tile_n), lambda j: (j, 0, 0))
    else:
        out_shape = jax.ShapeDtypeStruct((m, n), F32)
        out_spec = pl.BlockSpec((m, tile_n), lambda j: (0, j))

    def body(a_ref, b_ref, *rest):
        sum_refs, out_ref, got_refs = rest[:n_sums], rest[n_sums], rest[n_sums + 1:2 * n_sums + 1]
        j = pl.program_id(0)
        copies = _scatter_copies(sum_refs, got_refs, *rest[2 * n_sums + 1:]) if n_sums else []

        @pl.when(j == 0)
        def _():
            for cp in copies:
                cp.start()

        out_ref[...] = _tn(a_ref[...], b_ref[...])

        @pl.when(j == steps - 1)
        def _():
            for cp in copies:
                cp.wait()

    outs = pl.pallas_call(
        body, name=name, grid=(steps,),
        out_shape=[out_shape] + _scatter_shapes(chip_sums),
        in_specs=[_full((s, m)), pl.BlockSpec((s, tile_n), lambda j: (0, j))] + _any_specs(n_sums),
        out_specs=[out_spec] + _any_specs(n_sums),
        scratch_shapes=[pltpu.SemaphoreType.DMA((3 * n_sums,)), pltpu.SemaphoreType.DMA((3 * n_sums,))]
                       if n_sums else [],
        compiler_params=_params("arbitrary"),
    )(a, b, *chip_sums)
    return outs[0], outs[1:]


ROW_TILE = 512


def _row_index(tile, rows):
    return tile * rows + lax.broadcasted_iota(jnp.int32, (rows, 1), 0)


def _inverse_counts(t_glob):
    return [1.0 / jnp.minimum(t_glob + 1, w).astype(F32) for w in POOL_WINDOWS]


def _sigmoid(z):
    return 1.0 / (1.0 + jnp.exp(-z))


def _trailing_sums(src, tmp, cols, window, rows):
    bufs = (src, tmp)
    span, level, start = 1, 0, 0
    while span < window:
        start += 8
        a, b = bufs[level % 2], bufs[(level + 1) % 2]
        n = HALO + rows - start
        b[start:start + n, cols] = a[start:start + n, cols] + a[start - span:start - span + n, cols]
        span, level = 2 * span, level + 1
    return bufs[level % 2][HALO:HALO + rows, cols]


def _leading_sums(src, tmp, cols, window, rows):
    bufs = (src, tmp)
    span, level, n = 1, 0, rows + HALO
    while span < window:
        n -= 8
        a, b = bufs[level % 2], bufs[(level + 1) % 2]
        b[0:n, cols] = a[0:n, cols] + a[span:span + n, cols]
        span, level = 2 * span, level + 1
    return bufs[level % 2][0:rows, cols]


def gather_in_background(step, last, out_refs, send_sems, recv_sems, finish):
    n = len(out_refs)
    x, y, c = _position()
    q = 2 * x + y
    chips = _other_chips(x, y)

    def copy(k, i, quarter, half, to):
        return _gather_copy(out_refs[i], send_sems, recv_sems, k * n + i, quarter, half, to)

    if not finish:
        @pl.when(step == 0)
        def _():
            for i in range(n):
                mine, _ = _halves(out_refs[i].shape[1], c)
                for j, chip in enumerate(chips):
                    copy(j, i, q, mine, (*chip, c)).start()

        @pl.when(step == last)
        def _():
            for j, chip in enumerate(chips):
                qj = 2 * chip[0] + chip[1]
                for i in range(n):
                    mine, _ = _halves(out_refs[i].shape[1], c)
                    copy(j, i, qj, mine, (x, y, c)).wait_recv()
                    copy(3 + j, i, qj, mine, (x, y, 1 - c)).start()
        return

    @pl.when(step == last)
    def _():
        for j, chip in enumerate(chips):
            qj = 2 * chip[0] + chip[1]
            for i in range(n):
                mine, other = _halves(out_refs[i].shape[1], c)
                copy(3 + j, i, qj, other, (x, y, c)).wait_recv()
                copy(j, i, q, mine, (x, y, c)).wait_send()
                copy(3 + j, i, qj, mine, (x, y, c)).wait_send()


def pool_forward(x, w0, wpi, gw, gb, scale, wpo, later):
    s = x.shape[0]
    ts = ROW_TILE
    nt = s // ts
    assert nt >= 2
    n_later = len(later)

    def body(x_ref, w0_ref, wpi_ref, gw_ref, gb_ref, sc_ref, wpo_ref, *rest):
        rest = rest[n_later:]
        h1_ref, pooled_ref, gt_ref, n0_ref = rest[:4]
        later_refs = rest[4:4 + n_later]
        ubuf, tbuf, hist, send_sems, recv_sems = rest[4 + n_later:]
        i = pl.program_id(0)
        gather_in_background(i, nt - 1, later_refs, send_sems, recv_sems, finish=False)
        xv = x_ref[...]
        r = lax.rsqrt(jnp.mean(xv * xv, axis=-1, keepdims=True) + EPS)
        n0 = _bf(xv * r * w0_ref[...])
        n0_ref[...] = n0
        u = jnp.concatenate([_nn(n0, wpi_ref[0]), _nn(n0, wpi_ref[1])], axis=-1)
        gt = jnp.concatenate([_nn(n0, wpi_ref[2]), _nn(n0, wpi_ref[3])], axis=-1)
        gt_ref[...] = gt

        @pl.when(i == 0)
        def _():
            hist[...] = jnp.zeros_like(hist)

        ubuf[0:HALO, :] = hist[...]
        ubuf[HALO:HALO + ts, :] = u
        hist[...] = u[ts - HALO:, :]
        inv = _inverse_counts(_row_index(i, ts))
        mixed = []
        for g, w in enumerate(POOL_WINDOWS):
            cols = slice(g * GROUP_DIM, (g + 1) * GROUP_DIM)
            pooled = _bf(_trailing_sums(ubuf, tbuf, cols, w, ts) * inv[g] - u[:, cols])
            pooled_ref[:, cols] = pooled
            mixed.append(_nn(pooled, gw_ref[g]))
        mixed = jnp.concatenate(mixed, axis=-1) + gb_ref[...]
        y = mixed * sc_ref[...] * (gt * _sigmoid(gt))
        h1_ref[...] = xv + _nn(_bf(y), wpo_ref[...])
        gather_in_background(i, nt - 1, later_refs, send_sems, recv_sems, finish=True)

    row = lambda cols: pl.BlockSpec((ts, cols), lambda i: (i, 0))
    outs = pl.pallas_call(
        body, name="pool_forward", grid=(nt,),
        out_shape=[jax.ShapeDtypeStruct((s, D), F32), jax.ShapeDtypeStruct((s, D), BF16),
                   jax.ShapeDtypeStruct((s, D), F32), jax.ShapeDtypeStruct((s, D), BF16)]
                  + [jax.ShapeDtypeStruct(a.shape, a.dtype) for a in later],
        in_specs=[row(D), _full((1, D)), _full((N_CHIPS, D, D // 2)), _full((GROUPS, GROUP_DIM, GROUP_DIM)),
                  _full((1, D)), _full((1, D)), _full((D, D))] + _any_specs(n_later),
        out_specs=[row(D), row(D), row(D), row(D)] + _any_specs(n_later),
        input_output_aliases={7 + k: 4 + k for k in range(n_later)},
        scratch_shapes=[pltpu.VMEM((HALO + ts, D), F32), pltpu.VMEM((HALO + ts, D), F32),
                        pltpu.VMEM((HALO, D), F32),
                        pltpu.SemaphoreType.DMA((6 * n_later,)), pltpu.SemaphoreType.DMA((6 * n_later,))],
        compiler_params=_params("arbitrary"),
    )(x, w0, wpi, gw, gb, scale, wpo, *later)
    return outs[:4], outs[4:]


def pool_backward(x, dh1, pooled, gt, w0, wpi, gw, gb, scale, wpo, chip_sums):
    s = x.shape[0]
    ts = ROW_TILE
    nt = s // ts
    n_sums = len(chip_sums)

    def body(x_ref, dh1_ref, pooled_ref, gt_ref, w0_ref, wpi_ref, gw_ref, gb_ref, sc_ref, wpo_ref, *rest):
        sum_refs, rest = rest[:n_sums], rest[n_sums:]
        dx_ref, dproj_ref, gpo_ref, ggw_ref, small_ref = rest[:5]
        got_refs = rest[5:5 + n_sums]
        ebuf, tbuf, ahead, send_sems, recv_sems = rest[5 + n_sums:]
        i = pl.program_id(0)
        copies = _scatter_copies(sum_refs, got_refs, send_sems, recv_sems)

        @pl.when(i == 0)
        def _():
            for cp in copies:
                cp.start()

        @pl.when(i == 0)
        def _():
            gpo_ref[...] = jnp.zeros_like(gpo_ref)
            ggw_ref[...] = jnp.zeros_like(ggw_ref)
            small_ref[...] = jnp.zeros_like(small_ref)
            ahead[...] = jnp.zeros_like(ahead)

        dh1 = dh1_ref[...]
        dh1_bf = _bf(dh1)
        gt = gt_ref[...]
        sc = sc_ref[...]
        dy = _nt(dh1_bf, wpo_ref[...])
        pooled_bf = []
        mixed = []
        for g in range(GROUPS):
            cols = slice(g * GROUP_DIM, (g + 1) * GROUP_DIM)
            pb = pooled_ref[:, cols]
            pooled_bf.append(pb)
            mixed.append(_nn(pb, gw_ref[g]))
        mixed = jnp.concatenate(mixed, axis=-1) + gb_ref[...]
        sg = _sigmoid(gt)
        silu = gt * sg
        gpo_ref[...] += _tn(_bf(mixed * sc * silu), dh1_bf)
        dmixed = dy * sc * silu
        dgt = dy * mixed * sc * (sg * (1.0 + gt * (1.0 - sg)))
        dproj_ref[:, D:] = _bf(dgt)
        small_ref[1:2, :] += jnp.sum(dy * mixed * silu, axis=0, keepdims=True)
        small_ref[2:3, :] += jnp.sum(dmixed, axis=0, keepdims=True)

        inv = _inverse_counts(_row_index(nt - 1 - i, ts))
        ebuf[ts:ts + HALO, :] = ahead[...]
        dpooled = []
        for g in range(GROUPS):
            cols = slice(g * GROUP_DIM, (g + 1) * GROUP_DIM)
            dm = _bf(dmixed[:, cols])
            ggw_ref[g] += _tn(pooled_bf[g], dm)
            dp = _nt(dm, gw_ref[g])
            dpooled.append(dp)
            ebuf[0:ts, cols] = dp * inv[g]
        ahead[...] = ebuf[0:HALO, :]
        du = []
        for g, w in enumerate(POOL_WINDOWS):
            cols = slice(g * GROUP_DIM, (g + 1) * GROUP_DIM)
            du.append(_leading_sums(ebuf, tbuf, cols, w, ts) - dpooled[g])
        du = _bf(jnp.concatenate(du, axis=-1))
        dproj_ref[:, :D] = du
        dgt_bf = _bf(dgt)
        half = D // 2
        dn0 = (_nt(du[:, :half], wpi_ref[0]) + _nt(du[:, half:], wpi_ref[1])
               + _nt(dgt_bf[:, :half], wpi_ref[2]) + _nt(dgt_bf[:, half:], wpi_ref[3]))

        xv = x_ref[...]
        r = lax.rsqrt(jnp.mean(xv * xv, axis=-1, keepdims=True) + EPS)
        xhat = xv * r
        small_ref[0:1, :] += jnp.sum(dn0 * xhat, axis=0, keepdims=True)
        dxh = dn0 * w0_ref[...]
        dx_ref[...] = dh1 + r * (dxh - xhat * jnp.mean(dxh * xhat, axis=-1, keepdims=True))

        @pl.when(i == nt - 1)
        def _():
            for cp in copies:
                cp.wait()

    row = lambda cols: pl.BlockSpec((ts, cols), lambda i: (nt - 1 - i, 0))
    outs = pl.pallas_call(
        body, name="pool_backward", grid=(nt,),
        out_shape=[jax.ShapeDtypeStruct((s, D), F32), jax.ShapeDtypeStruct((s, 2 * D), BF16),
                   jax.ShapeDtypeStruct((D, D), F32),
                   jax.ShapeDtypeStruct((GROUPS, GROUP_DIM, GROUP_DIM), F32),
                   jax.ShapeDtypeStruct((8, D), F32)] + _scatter_shapes(chip_sums),
        in_specs=[row(D), row(D), row(D), row(D), _full((1, D)), _full((N_CHIPS, D, D // 2)),
                  _full((GROUPS, GROUP_DIM, GROUP_DIM)), _full((1, D)), _full((1, D)), _full((D, D))]
                 + _any_specs(n_sums),
        out_specs=[row(D), row(2 * D), _full((D, D)), _full((GROUPS, GROUP_DIM, GROUP_DIM)), _full((8, D))]
                  + _any_specs(n_sums),
        scratch_shapes=[pltpu.VMEM((ts + HALO, D), F32), pltpu.VMEM((ts + HALO, D), F32),
                        pltpu.VMEM((HALO, D), F32),
                        pltpu.SemaphoreType.DMA((3 * n_sums,)), pltpu.SemaphoreType.DMA((3 * n_sums,))],
        compiler_params=_params("arbitrary"),
    )(x, dh1, pooled, gt, w0, wpi, gw, gb, scale, wpo, *chip_sums)
    return outs[:5], outs[5:]


def gla_project(h1, w1, wgi_q, wgk, bgk, later):
    s = h1.shape[0]
    ts = ROW_TILE
    nt = s // ts
    assert nt >= 2
    n_later = len(later)

    def body(h_ref, w1_ref, wq_ref, wgk_ref, bgk_ref, *rest):
        rest = rest[n_later:]
        qk_ref, v_ref, gate_ref, low_ref, cum_ref, n1_ref = rest[:6]
        later_refs = rest[6:6 + n_later]
        send_sems, recv_sems, wgi_ref = rest[6 + n_later:]
        gather_in_background(pl.program_id(0), nt - 1, later_refs, send_sems, recv_sems, finish=False)

        @pl.when(pl.program_id(0) == 0)
        def _():
            _assemble_gla_in(wq_ref, wgi_ref)

        hv = h_ref[...]
        r = lax.rsqrt(jnp.mean(hv * hv, axis=-1, keepdims=True) + EPS)
        n1 = _bf(hv * r * w1_ref[...])
        n1_ref[...] = n1
        qk_ref[...] = _nn(n1, wgi_ref[:, 0:2 * KEY_W])
        v_ref[...] = _bf(_nn(n1, wgi_ref[:, 2 * KEY_W:2 * KEY_W + D]))
        gate_ref[...] = _nn(n1, wgi_ref[:, 2 * KEY_W + D:GLA_MAIN])
        low = _bf(_nn(n1, wgi_ref[:, GLA_MAIN:]))
        low_ref[...] = low
        z = _nn(low, wgk_ref[...]) + bgk_ref[...]
        lg = (jnp.minimum(z, 0.0) - jnp.log(1.0 + jnp.exp(-jnp.abs(z)))) / GATE_NORM
        lower_f = _chunk_masks()[0].astype(F32)
        for r0 in range(0, ts, CHUNK):
            cum_ref[r0:r0 + CHUNK, :] = _nn_exact(lower_f, lg[r0:r0 + CHUNK, :])
        gather_in_background(pl.program_id(0), nt - 1, later_refs, send_sems, recv_sems, finish=True)

    row = lambda cols: pl.BlockSpec((ts, cols), lambda i: (i, 0))
    outs = pl.pallas_call(
        body, name="gla_project", grid=(nt,),
        out_shape=[jax.ShapeDtypeStruct((s, D), F32), jax.ShapeDtypeStruct((s, D), BF16),
                   jax.ShapeDtypeStruct((s, D), F32), jax.ShapeDtypeStruct((s, RANK_PAD), BF16),
                   jax.ShapeDtypeStruct((s, KEY_W), F32), jax.ShapeDtypeStruct((s, D), BF16)]
                  + [jax.ShapeDtypeStruct(a.shape, a.dtype) for a in later],
        in_specs=[row(D), _full((1, D)), _full((N_CHIPS, D, GLA_IN_QUARTER)),
                  _full((RANK_PAD, KEY_W)), _full((1, KEY_W))] + _any_specs(n_later),
        out_specs=[row(D), row(D), row(D), row(RANK_PAD), row(KEY_W), row(D)] + _any_specs(n_later),
        input_output_aliases={5 + k: 6 + k for k in range(n_later)},
        scratch_shapes=[pltpu.SemaphoreType.DMA((6 * n_later,)), pltpu.SemaphoreType.DMA((6 * n_later,)),
                        pltpu.VMEM((D, GLA_MAIN + RANK_PAD), BF16)],
        compiler_params=_params("arbitrary"),
    )(h1, w1, wgi_q, wgk, bgk, *later)
    return outs[:6], outs[6:]


def _assemble_gla_in(wq_ref, wfull):
    pad = jnp.zeros((CAST_ROWS, GLA_MAIN + RANK_PAD - GLA_IN), BF16)
    for r0 in range(0, D, CAST_ROWS):
        rows = slice(r0, r0 + CAST_ROWS)
        wfull[rows, :] = jnp.concatenate([wq_ref[q, rows, :] for q in range(N_CHIPS)] + [pad], axis=1)


GLA_BLOCK = 512
CHUNKS_PER_BLOCK = GLA_BLOCK // CHUNK


def _chunk_masks():
    t = lax.broadcasted_iota(jnp.int32, (CHUNK, CHUNK), 0)
    u = lax.broadcasted_iota(jnp.int32, (CHUNK, CHUNK), 1)
    return t >= u, t <= u


def _gla_chunk_terms(q, cum):
    ep = jnp.exp(cum)
    en = jnp.exp(-cum)
    qs = q * (HEAD_K ** -0.5)
    last = cum[CHUNK - 1:CHUNK, :]
    ed = jnp.exp(last - cum)
    dec = jnp.exp(last)
    return ep, en, qs, ed, dec


def gla_forward(qk, v, cum):
    s = qk.shape[0]
    nb = s // GLA_BLOCK
    nc = s // CHUNK

    def body(q_ref, k_ref, v_ref, cum_ref, o_ref, st_ref, sc_ref, state):
        @pl.when(pl.program_id(0) == 0)
        def _():
            state[...] = jnp.zeros_like(state)

        lower, _ = _chunk_masks()

        def chunk(cc, carry):
            rows = pl.ds(pl.multiple_of(cc * CHUNK, CHUNK), CHUNK)
            for h in range(HEADS):
                kc = slice(h * HEAD_K, (h + 1) * HEAD_K)
                vc = slice(h * HEAD_V, (h + 1) * HEAD_V)
                q = q_ref[rows, kc]
                k = k_ref[rows, kc]
                v = v_ref[rows, vc]
                ep, en, qs, ed, dec = _gla_chunk_terms(q, cum_ref[rows, kc])
                a = _bf(qs * ep)
                fwd = _nt(a, _bf(k * en))
                bwd = _nt(_bf(qs * en), _bf(k * ep))
                scores = _bf(jnp.where(lower, fwd, bwd))
                sc_ref[rows, h * CHUNK:(h + 1) * CHUNK] = scores
                st = state[h]
                st_ref[cc, h] = st
                o_ref[rows, vc] = _nn(scores, v) + _nt(a, _bf(st))
                state[h] = st * dec + _tn(v, _bf(k * ed))
            return carry

        lax.fori_loop(0, CHUNKS_PER_BLOCK, chunk, 0, unroll=4)

    return pl.pallas_call(
        body, name="gla_forward", grid=(nb,),
        out_shape=(jax.ShapeDtypeStruct((s, D), F32),
                   jax.ShapeDtypeStruct((nc, HEADS, HEAD_V, HEAD_K), F32),
                   jax.ShapeDtypeStruct((s, HEADS * CHUNK), BF16)),
        in_specs=[pl.BlockSpec((GLA_BLOCK, KEY_W), lambda i: (i, 0)),
                  pl.BlockSpec((GLA_BLOCK, KEY_W), lambda i: (i, 1)),
                  pl.BlockSpec((GLA_BLOCK, D), lambda i: (i, 0)),
                  pl.BlockSpec((GLA_BLOCK, KEY_W), lambda i: (i, 0))],
        out_specs=(pl.BlockSpec((GLA_BLOCK, D), lambda i: (i, 0)),
                   pl.BlockSpec((CHUNKS_PER_BLOCK, HEADS, HEAD_V, HEAD_K), lambda i: (i, 0, 0, 0)),
                   pl.BlockSpec((GLA_BLOCK, HEADS * CHUNK), lambda i: (i, 0))),
        scratch_shapes=[pltpu.VMEM((HEADS, HEAD_V, HEAD_K), F32)],
        compiler_params=_params("arbitrary"),
    )(qk, qk, v, cum)


def gla_backward(qk, v, cum, do, states, scores):
    s = qk.shape[0]
    nb = s // GLA_BLOCK

    def body(q_ref, k_ref, v_ref, cum_ref, do_ref, st_ref, sc_ref, dq_ref, dk_ref, dv_ref, dcum_ref, dstate):
        @pl.when(pl.program_id(0) == 0)
        def _():
            dstate[...] = jnp.zeros_like(dstate)

        lower, _ = _chunk_masks()
        is_last = lax.broadcasted_iota(jnp.int32, (CHUNK, HEAD_K), 0) == CHUNK - 1

        def chunk(step, carry):
            cc = CHUNKS_PER_BLOCK - 1 - step
            rows = pl.ds(pl.multiple_of(cc * CHUNK, CHUNK), CHUNK)
            for h in range(HEADS):
                kc = slice(h * HEAD_K, (h + 1) * HEAD_K)
                vc = slice(h * HEAD_V, (h + 1) * HEAD_V)
                q = q_ref[rows, kc]
                k = k_ref[rows, kc]
                v = v_ref[rows, vc]
                do_c = do_ref[rows, vc]
                ep, en, qs, ed, dec = _gla_chunk_terms(q, cum_ref[rows, kc])
                a = _bf(qs * ep)
                b = _bf(k * en)
                c = _bf(qs * en)
                dk_dec = _bf(k * ep)
                kd = _bf(k * ed)
                scores = sc_ref[rows, h * CHUNK:(h + 1) * CHUNK]
                st = st_ref[cc, h]
                dst = dstate[h]
                dst_bf = _bf(dst)

                dscores = _nt(do_c, v)
                dfwd = _bf(jnp.where(lower, dscores, 0.0))
                dbwd = _bf(jnp.where(lower, 0.0, dscores))
                dv_ref[rows, vc] = _bf(_tn(scores, do_c) + _nt(kd, dst_bf))
                da = _nn(dfwd, b) + _nn(do_c, _bf(st))
                db = _tn(dfwd, a)
                dc = _nn(dbwd, dk_dec)
                ddk = _tn(dbwd, c)
                dkd = _nn(v, dst_bf)
                ddec = jnp.sum(dst * st, axis=0, keepdims=True)
                dstate[h] = dst * dec + _tn(do_c, a)

                m = dkd * k * ed
                dq_ref[rows, kc] = _bf((da * ep + dc * en) * (HEAD_K ** -0.5))
                dk_ref[rows, kc] = _bf(db * en + ddk * ep + dkd * ed)
                dcum = (da * qs + ddk * k) * ep - (db * k + dc * qs) * en - m
                dlast = jnp.sum(m, axis=0, keepdims=True) + ddec * dec
                dcum_ref[rows, kc] = dcum + jnp.where(is_last, dlast, 0.0)
            return carry

        lax.fori_loop(0, CHUNKS_PER_BLOCK, chunk, 0, unroll=4)

    rev = lambda cols, col_block: pl.BlockSpec((GLA_BLOCK, cols), lambda i: (nb - 1 - i, col_block))
    return pl.pallas_call(
        body, name="gla_backward", grid=(nb,),
        out_shape=(jax.ShapeDtypeStruct((s, KEY_W), BF16), jax.ShapeDtypeStruct((s, KEY_W), BF16),
                   jax.ShapeDtypeStruct((s, D), BF16), jax.ShapeDtypeStruct((s, KEY_W), F32)),
        in_specs=[rev(KEY_W, 0), rev(KEY_W, 1), rev(D, 0), rev(KEY_W, 0), rev(D, 0),
                  pl.BlockSpec((CHUNKS_PER_BLOCK, HEADS, HEAD_V, HEAD_K), lambda i: (nb - 1 - i, 0, 0, 0)),
                  rev(HEADS * CHUNK, 0)],
        out_specs=(rev(KEY_W, 0), rev(KEY_W, 0), rev(D, 0), rev(KEY_W, 0)),
        scratch_shapes=[pltpu.VMEM((HEADS, HEAD_V, HEAD_K), F32)],
        compiler_params=_params("arbitrary"),
    )(qk, qk, v, cum, do, states, scores)


def head_and_loss(o, gate, h1, target, hw, wgo, wf):
    s = o.shape[0]
    ts = ROW_TILE

    def body(o_ref, gate_ref, h1_ref, tgt_ref, hw_ref, wgo_ref, wf_ref,
             dh2_ref, do_ref, dgate_ref, ggo_ref, small_ref):
        @pl.when(pl.program_id(0) == 0)
        def _():
            ggo_ref[...] = jnp.zeros_like(ggo_ref)
            small_ref[...] = jnp.zeros_like(small_ref)

        gate = gate_ref[...]
        hw = hw_ref[...]
        sg = _sigmoid(gate)
        silu = gate * sg
        ohat, ro = [], []
        for h in range(HEADS):
            oh = o_ref[:, h * HEAD_V:(h + 1) * HEAD_V]
            rh = lax.rsqrt(jnp.mean(oh * oh, axis=-1, keepdims=True) + EPS)
            ro.append(rh)
            ohat.append(oh * rh)
        ohat = jnp.concatenate(ohat, axis=-1)
        on = ohat * hw
        y2 = _bf(on * silu)
        h2 = h1_ref[...] + _nn(y2, wgo_ref[...])
        rf = lax.rsqrt(jnp.mean(h2 * h2, axis=-1, keepdims=True) + EPS)
        h2hat = h2 * rf
        wf = wf_ref[...]
        diff = h2hat * wf - tgt_ref[...]
        small_ref[2:3, :] += jnp.zeros((1, D), F32) + 0.5 * jnp.sum(diff * diff) / D
        dout = diff / D
        small_ref[0:1, :] += jnp.sum(dout * h2hat, axis=0, keepdims=True)
        dxh = dout * wf
        dh2 = rf * (dxh - h2hat * jnp.mean(dxh * h2hat, axis=-1, keepdims=True))
        dh2_ref[...] = dh2
        dh2_bf = _bf(dh2)
        ggo_ref[...] += _tn(y2, dh2_bf)
        dy2 = _nt(dh2_bf, wgo_ref[...])
        don = dy2 * silu
        dgate_ref[...] = _bf(dy2 * on * (sg * (1.0 + gate * (1.0 - sg))))
        ghw = jnp.sum(don * ohat, axis=0, keepdims=True)
        small_ref[1:2, 0:HEAD_V] += sum(ghw[:, h * HEAD_V:(h + 1) * HEAD_V] for h in range(HEADS))
        dohat = don * hw
        for h in range(HEADS):
            cols = slice(h * HEAD_V, (h + 1) * HEAD_V)
            oh, dh = ohat[:, cols], dohat[:, cols]
            do_ref[:, cols] = _bf(ro[h] * (dh - oh * jnp.mean(dh * oh, axis=-1, keepdims=True)))

    row = lambda cols: pl.BlockSpec((ts, cols), lambda i: (i, 0))
    act = jax.ShapeDtypeStruct((s, D), F32)
    act_bf = jax.ShapeDtypeStruct((s, D), BF16)
    return pl.pallas_call(
        body, name="head_and_loss", grid=(s // ts,),
        out_shape=(act, act_bf, act_bf, jax.ShapeDtypeStruct((D, D), F32), jax.ShapeDtypeStruct((8, D), F32)),
        in_specs=[row(D), row(D), row(D), row(D),
                  _full((1, D)), _full((D, D)), _full((1, D))],
        out_specs=(row(D), row(D), row(D), _full((D, D)), _full((8, D))),
        compiler_params=_params("arbitrary"),
    )(o, gate, h1, target, hw, wgo, wf)


def gla_project_backward(dq, dk, dv, dgate, dcum, low, h1, dh2, w1, wgi_q, wgk, bgk):
    s = h1.shape[0]
    ts = ROW_TILE

    def body(dq_ref, dk_ref, dv_ref, dgate_ref, dcum_ref, low_ref, h1_ref, dh2_ref, w1_ref,
             wq_ref, wgk_ref, bgk_ref, dh1_ref, dproj_ref, ggk_ref, small_ref, wgi_ref):
        @pl.when(pl.program_id(0) == 0)
        def _():
            ggk_ref[...] = jnp.zeros_like(ggk_ref)
            small_ref[...] = jnp.zeros_like(small_ref)
            _assemble_gla_in(wq_ref, wgi_ref)

        low = low_ref[...]
        z = _nn(low, wgk_ref[...]) + bgk_ref[...]
        upper_f = _chunk_masks()[1].astype(F32)
        dlg = jnp.concatenate([_nn_exact(upper_f, dcum_ref[r0:r0 + CHUNK, :]) for r0 in range(0, ts, CHUNK)],
                              axis=0)
        dz = dlg * (1.0 / GATE_NORM) * _sigmoid(-z)
        dz_bf = _bf(dz)
        ggk_ref[...] += _tn(low, dz_bf)
        small_ref[1:2, 0:KEY_W] += jnp.sum(dz, axis=0, keepdims=True)
        dlow = _bf(_nt(dz_bf, wgk_ref[...]))
        dproj_ref[:, GLA_MAIN:] = dlow
        dn1 = _nt(dlow, wgi_ref[:, GLA_MAIN:])
        for ref, lo, hi in ((dq_ref, 0, KEY_W), (dk_ref, KEY_W, 2 * KEY_W),
                            (dv_ref, 2 * KEY_W, 2 * KEY_W + D), (dgate_ref, 2 * KEY_W + D, GLA_MAIN)):
            piece = ref[...]
            dproj_ref[:, lo:hi] = piece
            dn1 = dn1 + _nt(piece, wgi_ref[:, lo:hi])
        hv = h1_ref[...]
        r = lax.rsqrt(jnp.mean(hv * hv, axis=-1, keepdims=True) + EPS)
        hhat = hv * r
        small_ref[0:1, :] += jnp.sum(dn1 * hhat, axis=0, keepdims=True)
        dxh = dn1 * w1_ref[...]
        dh1_ref[...] = dh2_ref[...] + r * (dxh - hhat * jnp.mean(dxh * hhat, axis=-1, keepdims=True))

    row = lambda cols: pl.BlockSpec((ts, cols), lambda i: (i, 0))
    return pl.pallas_call(
        body, name="gla_project_backward", grid=(s // ts,),
        out_shape=(jax.ShapeDtypeStruct((s, D), F32), jax.ShapeDtypeStruct((s, GLA_MAIN + RANK_PAD), BF16),
                   jax.ShapeDtypeStruct((RANK_PAD, KEY_W), F32),
                   jax.ShapeDtypeStruct((8, D), F32)),
        in_specs=[row(KEY_W), row(KEY_W), row(D), row(D), row(KEY_W), row(RANK_PAD), row(D), row(D),
                  _full((1, D)), _full((N_CHIPS, D, GLA_IN_QUARTER)), _full((RANK_PAD, KEY_W)),
                  _full((1, KEY_W))],
        out_specs=(row(D), row(GLA_MAIN + RANK_PAD), _full((RANK_PAD, KEY_W)), _full((8, D))),
        scratch_shapes=[pltpu.VMEM((D, GLA_MAIN + RANK_PAD), BF16)],
        compiler_params=_params("arbitrary"),
    )(dq, dk, dv, dgate, dcum, low, h1, dh2, w1, wgi_q, wgk, bgk)


def _groups_from_quarters(a):
    return a.reshape(N_CHIPS, GROUPS, 64, GROUP_DIM).transpose(1, 0, 2, 3).reshape(GROUPS, GROUP_DIM, GROUP_DIM)


def _quarters_from_groups(a):
    return a.reshape(GROUPS, N_CHIPS, 64, GROUP_DIM).transpose(1, 0, 2, 3).reshape(N_CHIPS, GROUP_DIM, GROUP_DIM)


def local_gradients(xs, target, w0, w1, wf, wpi, gw, gb, scale, wpo, gla_quarters, wgk, bgk, hw_tiled, place):
    wgi_q, wgo_q = gla_quarters
    (h1, pooled, gt, n0), (wgi_q,) = pool_forward(xs, w0, wpi, gw, gb, scale, wpo, [wgi_q])
    (qk, v, gate, low, cum, n1), (wgo_q,) = gla_project(h1, w1, wgi_q, wgk, bgk, [wgo_q])
    wgo = wgo_q.reshape(D, D)
    o, states, scores = gla_forward(qk, v, cum)

    dh2, do, dgate, g_gla_out, small_top = head_and_loss(o, gate, h1, target, hw_tiled, wgo, wf)
    dq, dk, dv, dcum = gla_backward(qk, v, cum, do, states, scores)
    dh1, dproj, g_gk_pad, small_gla = gla_project_backward(
        dq, dk, dv, dgate, dcum, low, h1, dh2, w1, wgi_q, wgk, bgk)
    g_gla_in, _ = matmul_tn(n1, dproj, "grad_gla_in", tile_n=(GLA_MAIN + RANK_PAD) // 5)

    def chip_sums(grads, tag):
        return add_halves(grads, place, "add_halves_" + tag)

    gla_sums = chip_sums([g_gla_in, g_gla_out.reshape(N_CHIPS, D // N_CHIPS, D)], "gla")
    (dx, dpool, g_pool_out, g_group_w, small_pool), gla_got = pool_backward(
        xs, dh1, pooled, gt, w0, wpi, gw, gb, scale, wpo, [b for _, b in gla_sums])
    mix_sums = chip_sums([_quarters_from_groups(g_group_w), g_pool_out.reshape(N_CHIPS, D // N_CHIPS, D)], "pool_mix")
    g_pool_in, mix_got = matmul_tn(n0, dpool, "grad_pool_in", tile_n=D // 2, by_column_tile=True,
                                   chip_sums=[b for _, b in mix_sums])

    in_sums = add_halves([g_pool_in], place, "add_halves_and_scatter_pool_in", scatter=True)
    reduced, total = join_halves(
        add_parts([f for f, _ in in_sums + mix_sums + gla_sums],
                  [got for _, got in in_sums] + list(mix_got) + list(gla_got), place, "add_parts"),
        small_pool, small_gla, small_top, g_gk_pad)
    return dx, reduced, total


def kernel(x, norm_w, pool_in_w, pool_group_w, pool_group_b, pool_scale, pool_out_w, gla_in_w, gla_gk_w, gla_gk_b, gla_head_norm_w, gla_out_w, final_norm_w, loss_target, m_norm_w, m_pool_in_w, m_pool_group_w, m_pool_group_b, m_pool_scale, m_pool_out_w, m_gla_in_w, m_gla_gk_w, m_gla_gk_b, m_gla_head_norm_w, m_gla_out_w, m_final_norm_w, v_norm_w, v_pool_in_w, v_pool_group_w, v_pool_group_b, v_pool_scale, v_pool_out_w, v_gla_in_w, v_gla_gk_w, v_gla_gk_b, v_gla_head_norm_w, v_gla_out_w, v_final_norm_w):
    xs = x[0]
    target = loss_target[0]
    q_chip = 2 * lax.axis_index("x") + lax.axis_index("y")
    place = jnp.stack([lax.axis_index("c"), q_chip]).astype(jnp.int32)

    (wpi, gw_q, wpo_q, wgi_q, wgo_q), small_all = allgather_weights(
        [pool_in_w[0], pool_group_w[0].reshape(GROUP_DIM, GROUP_DIM), pool_out_w[0], gla_in_w[0], gla_out_w[0]],
        exchange=(True, True, True, False, False),
        smalls=[gla_gk_b, gla_head_norm_w, pool_group_b[0], gla_gk_w[0]])
    gw = _groups_from_quarters(gw_q)
    wpo = wpo_q.reshape(D, D)
    small_all = small_all[0::2]
    bgk = small_all[:, 0, :].reshape(1, KEY_W)
    hw = small_all[:, 1, 0:64].reshape(1, HEAD_V)
    gb = small_all[:, 2:2 + GROUPS, 0:64].transpose(1, 0, 2).reshape(1, D)
    wgk16 = small_all[:, 8:8 + GATE_RANK, :].transpose(1, 0, 2).reshape(GATE_RANK, KEY_W)
    wgk = _bf(jnp.pad(wgk16, ((0, RANK_PAD - GATE_RANK), (0, 0))))
    hw_tiled = jnp.tile(hw, (1, HEADS))

    w0 = norm_w[0:1]
    w1 = norm_w[1:2]
    wf = final_norm_w.reshape(1, D)

    dx, reduced, total = local_gradients(
        xs, target, w0, w1, wf, wpi, gw, gb, pool_scale, wpo, [wgi_q, wgo_q], wgk, bgk, hw_tiled, place)
    r_pool_in, r_group_w, r_pool_out, r_gla_in, r_gla_out = reduced
    r_group_w = r_group_w.reshape(GROUPS, 64, GROUP_DIM)

    loss = total[7, 0]
    g_norm = jnp.stack([total[0], total[3]])
    g_scale = total[1:2]
    g_final = total[5]
    pick = lambda full, width: lax.dynamic_slice_in_dim(full, q_chip * width, width, axis=-1)
    g_gk_b = pick(total[4:5, 0:KEY_W], 128)
    g_hnw = pick(total[6:7, 0:HEAD_V], 64)
    g_group_b = pick(total[2].reshape(GROUPS, GROUP_DIM), 64)[None]
    g_gk_w = pick(total[8:16].reshape(GATE_RANK, KEY_W), 128)[None]

    turn = lambda a: jnp.transpose(a, (2, 0, 1))
    back = lambda a: jnp.transpose(a, (1, 2, 0))
    as2d = lambda a, w: a.reshape(-1, w.shape[-1])
    big_names = ("pool_in_w", "pool_group_w", "pool_out_w", "gla_in_w", "gla_out_w")
    big_args = [(pool_in_w, r_pool_in[None], m_pool_in_w, v_pool_in_w),
                (pool_group_w, r_group_w[None], m_pool_group_w, v_pool_group_w),
                (pool_out_w, r_pool_out[None], m_pool_out_w, v_pool_out_w),
                (gla_in_w, r_gla_in[None], m_gla_in_w, v_gla_in_w),
                (gla_out_w, r_gla_out[None], m_gla_out_w, v_gla_out_w)]
    to_kernel = lambda n, a, w: turn(a) if n == "gla_in_w" else as2d(a, w)
    from_kernel = lambda n, a, w: back(a) if n == "gla_in_w" else a.reshape(w.shape)
    big_in = [tuple(to_kernel(n, a, p[0]) for a in p) for n, p in zip(big_names, big_args)]
    big_out = adamw(big_in, "adamw")
    big = {n: (from_kernel(n, i[1], p[0]),) + tuple(from_kernel(n, o, p[0]) for o in out)
           for n, p, i, out in zip(big_names, big_args, big_in, big_out)}

    small_names = ("norm_w", "pool_group_b", "pool_scale", "gla_gk_w", "gla_gk_b", "gla_head_norm_w",
                   "final_norm_w")
    small_args = [(norm_w, g_norm, m_norm_w, v_norm_w),
                  (pool_group_b, g_group_b, m_pool_group_b, v_pool_group_b),
                  (pool_scale, g_scale, m_pool_scale, v_pool_scale),
                  (gla_gk_w, g_gk_w, m_gla_gk_w, v_gla_gk_w),
                  (gla_gk_b, g_gk_b, m_gla_gk_b, v_gla_gk_b),
                  (gla_head_norm_w, g_hnw, m_gla_head_norm_w, v_gla_head_norm_w),
                  (final_norm_w, g_final, m_final_norm_w, v_final_norm_w)]
    small_out = adamw_small([tuple(as2d(a, p[0]) for a in p) for p in small_args])
    small = {n: (p[1].reshape(p[0].shape),) + tuple(o.reshape(p[0].shape) for o in out)
             for n, p, out in zip(small_names, small_args, small_out)}
    results = [
        small["norm_w"],
        big["pool_in_w"],
        big["pool_group_w"],
        small["pool_group_b"],
        small["pool_scale"],
        big["pool_out_w"],
        big["gla_in_w"],
        small["gla_gk_w"],
        small["gla_gk_b"],
        small["gla_head_norm_w"],
        big["gla_out_w"],
        small["final_norm_w"],
    ]
    grads, deltas, new_m, new_v = zip(*results)
    return (loss, dx[None], *grads, *deltas, *new_m, *new_v)
```

```python
import jax
import jax.numpy as jnp
from jax import lax
from jax.experimental import pallas as pl
from jax.experimental.pallas import tpu as pltpu

F32 = jnp.float32
BF16 = jnp.bfloat16
MESH = pl.DeviceIdType.MESH

D = 1024
POOL_WINDOWS = (2, 4, 8, 16)
GROUPS = 4
GROUP_DIM = 256
HEADS = 4
HEAD_K = 128
HEAD_V = 256
KEY_W = 512
CHUNK = 64
GATE_RANK = 16
GATE_NORM = 16.0
GLA_IN = 3088
GLA_MAIN = 3072
RANK_PAD = 128
EPS = 1e-6
HALO = 32

ADAM_LR = 0.001
ADAM_B1 = 0.9
ADAM_B2 = 0.999
ADAM_EPS = 1e-08
ADAM_WD = 0.01
ADAM_STEP = 10

N_CHIPS = 4
N_DEV = 8
GLA_IN_QUARTER = GLA_IN // N_CHIPS

VMEM_LIMIT = 56 * 1024 * 1024


def _nn(a, b):
    return lax.dot_general(a, b, (((1,), (0,)), ((), ())), preferred_element_type=F32)


def _nt(a, b):
    return lax.dot_general(a, b, (((1,), (1,)), ((), ())), preferred_element_type=F32)


def _tn(a, b):
    return lax.dot_general(a, b, (((0,), (0,)), ((), ())), preferred_element_type=F32)


def _nn_exact(a, b):
    return lax.dot_general(a, b, (((1,), (0,)), ((), ())), preferred_element_type=F32,
                           precision=lax.Precision.HIGHEST)


def _bf(a):
    return a.astype(BF16)


def _params(*sem):
    return pltpu.CompilerParams(dimension_semantics=sem, vmem_limit_bytes=VMEM_LIMIT)


def _full(shape):
    return pl.BlockSpec(shape, lambda i: (0,) * len(shape))


def _position():
    return lax.axis_index("x"), lax.axis_index("y"), lax.axis_index("c")


def _gather_small(in_ref, all_ref, send_sems, recv_sems, local_sem):
    x, y, c = _position()
    me = 4 * x + 2 * y + c
    mine = pltpu.make_async_copy(in_ref, all_ref.at[me], local_sem)
    sends = []
    for k in range(N_DEV - 1):
        fx, fy, fc = (k + 1) >> 2 & 1, (k + 1) >> 1 & 1, (k + 1) & 1
        sends.append(pltpu.make_async_remote_copy(
            src_ref=in_ref, dst_ref=all_ref.at[me],
            send_sem=send_sems.at[k], recv_sem=recv_sems.at[k],
            device_id=(x ^ fx, y ^ fy, c ^ fc), device_id_type=MESH))

    def start():
        mine.start()
        for cp in sends:
            cp.start()

    def wait():
        for k in range(N_DEV - 1):
            fx, fy, fc = (k + 1) >> 2 & 1, (k + 1) >> 1 & 1, (k + 1) & 1
            src_dev = 4 * (x ^ fx) + 2 * (y ^ fy) + (c ^ fc)
            pltpu.make_async_remote_copy(
                src_ref=in_ref, dst_ref=all_ref.at[src_dev],
                send_sem=send_sems.at[k], recv_sem=recv_sems.at[k],
                device_id=(x, y, c), device_id_type=MESH).wait_recv()
        for cp in sends:
            cp.wait_send()
        mine.wait()

    return start, wait


SMALL_SEMS = [pltpu.SemaphoreType.DMA((N_DEV - 1,)), pltpu.SemaphoreType.DMA((N_DEV - 1,)),
              pltpu.SemaphoreType.DMA]
VMEM_SPEC = pl.BlockSpec(memory_space=pltpu.VMEM)


def _other_chips(x, y):
    return [(1 - x, y), (x, 1 - y), (1 - x, 1 - y)]


def _any_specs(n):
    return [pl.BlockSpec(memory_space=pl.ANY)] * n


def _halves(rows, c):
    half = rows // 2
    return pl.ds(c * half, half), pl.ds((1 - c) * half, half)


CAST_ROWS = 256


def _gather_copy(out_ref, send_sems, recv_sems, k, quarter, half, to, src=None):
    dst = out_ref.at[quarter, half]
    return pltpu.make_async_remote_copy(
        src_ref=dst if src is None else src, dst_ref=dst,
        send_sem=send_sems.at[k], recv_sem=recv_sems.at[k], device_id=to, device_id_type=MESH)


SMALL_IN_ROWS = 24


def allgather_weights(quarters, exchange, smalls):
    n = len(quarters)
    shapes = [w.shape for w in quarters]
    moved = [i for i in range(n) if exchange[i]]

    def body(*refs):
        w_refs, (gkb_ref, hnw_ref, gb_ref, gkw_ref) = refs[:n], refs[n:n + 4]
        out_refs, small_all_ref = refs[n + 4:2 * n + 4], refs[2 * n + 4]
        refs = refs[2 * n + 5:]
        f32_bufs, bf_bufs = refs[:n], refs[n:2 * n]
        send_sems, recv_sems, local_sems, small_ref = refs[2 * n:2 * n + 4]
        small_ref[...] = jnp.zeros_like(small_ref)
        small_ref[0:1, :] = gkb_ref[...]
        small_ref[1:2, 0:64] = hnw_ref[...]
        small_ref[2:2 + GROUPS, 0:64] = gb_ref[...]
        small_ref[8:8 + GATE_RANK, :] = gkw_ref[...]
        start_small, wait_small = _gather_small(small_ref, small_all_ref, *refs[2 * n + 4:])
        start_small()
        x, y, c = _position()
        q = 2 * x + y
        sibling = (x, y, 1 - c)
        chips = _other_chips(x, y)

        def copy(k, i, quarter, half, to, src=None):
            return _gather_copy(out_refs[i], send_sems, recv_sems, k * n + i, quarter, half, to, src)

        loads = [pltpu.make_async_copy(w_refs[i], f32_bufs[i], local_sems.at[i]) for i in range(n)]
        for cp in loads:
            cp.start()
        keeps, sends = [], []
        for i in range(n):
            loads[i].wait()
            for r0 in range(0, shapes[i][0], CAST_ROWS):
                bf_bufs[i][r0:r0 + CAST_ROWS, :] = _bf(f32_bufs[i][r0:r0 + CAST_ROWS, :])
            keep = pltpu.make_async_copy(bf_bufs[i], out_refs[i].at[q], local_sems.at[n + i])
            keep.start()
            keeps.append(keep)
            if not exchange[i]:
                continue
            mine, _ = _halves(shapes[i][0], c)
            for j, chip in enumerate(chips):
                cp = copy(j, i, q, mine, (*chip, c), src=bf_bufs[i].at[mine])
                cp.start()
                sends.append(cp)
        for j, chip in enumerate(chips):
            qj = 2 * chip[0] + chip[1]
            for i in moved:
                mine, _ = _halves(shapes[i][0], c)
                copy(j, i, qj, mine, (x, y, c)).wait_recv()
                cp = copy(3 + j, i, qj, mine, sibling)
                cp.start()
                sends.append(cp)
        for j, chip in enumerate(chips):
            qj = 2 * chip[0] + chip[1]
            for i in moved:
                _, other = _halves(shapes[i][0], c)
                copy(3 + j, i, qj, other, (x, y, c)).wait_recv()
        wait_small()
        for cp in sends:
            cp.wait_send()
        for cp in keeps:
            cp.wait()

    outs = pl.pallas_call(
        body, name="allgather_weights",
        out_shape=[jax.ShapeDtypeStruct((N_CHIPS, *s), BF16) for s in shapes]
                  + [jax.ShapeDtypeStruct((N_DEV, SMALL_IN_ROWS, 128), F32)],
        in_specs=_any_specs(n) + [VMEM_SPEC] * 4, out_specs=_any_specs(n) + [VMEM_SPEC],
        scratch_shapes=([pltpu.VMEM(s, F32) for s in shapes] + [pltpu.VMEM(s, BF16) for s in shapes]
                        + [pltpu.SemaphoreType.DMA((6 * n,)), pltpu.SemaphoreType.DMA((6 * n,)),
                           pltpu.SemaphoreType.DMA((2 * n,)), pltpu.VMEM((SMALL_IN_ROWS, 128), F32)] + SMALL_SEMS),
        compiler_params=pltpu.CompilerParams(vmem_limit_bytes=VMEM_LIMIT),
    )(*quarters, *smalls)
    return outs[:n], outs[n]


def _scatter_copies(b_refs, got_refs, send_sems, recv_sems):
    n = len(b_refs)
    x, y, c = _position()
    copies = []
    for j, chip in enumerate(_other_chips(x, y)):
        qj = 2 * chip[0] + chip[1]
        for i in range(n):
            copies.append(pltpu.make_async_remote_copy(
                src_ref=b_refs[i].at[qj], dst_ref=got_refs[i].at[j],
                send_sem=send_sems.at[j * n + i], recv_sem=recv_sems.at[j * n + i],
                device_id=(*chip, c), device_id_type=MESH))
    return copies


def _scatter_shapes(chip_sums):
    return [jax.ShapeDtypeStruct((N_CHIPS - 1, *b.shape[1:]), BF16) for b in chip_sums]


ADD_ROWS = 512
ADD_HALVES_ROWS = 128
ADD_HALVES_AHEAD = 2


def _spans(counts):
    starts, total = [], 0
    for count in counts:
        starts.append(total)
        total += count
    return starts, total


def _local_step(t, start, count):
    return jnp.clip(t - start, 0, count - 1)


def add_halves(grads, place, name, scatter=False):
    n = len(grads)
    whole = [len(g.shape) == 2 for g in grads]
    halves = [g.shape[-2] // 2 for g in grads]
    cols = [GLA_IN_QUARTER if w else g.shape[-1] for g, w in zip(grads, whole)]
    rbs = [min(ADD_HALVES_ROWS, h) for h in halves]
    counts = [h // rb for h, rb in zip(halves, rbs)]
    starts, total = _spans(counts)
    half_shapes = [(*g.shape[:-2], h, g.shape[-1]) for g, h in zip(grads, halves)]

    def rows_of(ref, i, start):
        return ref.at[pl.ds(start, rbs[i])] if whole[i] else ref.at[:, pl.ds(start, rbs[i])]

    def body(place_ref, *refs):
        a_refs, src_refs = refs[:n], refs[n:2 * n]
        f_refs, h_refs = refs[2 * n:3 * n], refs[3 * n:4 * n]
        their_refs, rest = refs[4 * n:5 * n], refs[5 * n:]
        send_sems, recv_sems = rest[:2]
        t = pl.program_id(0)
        q = place_ref[1]
        x, y, c = _position()
        sum_refs = rest[2:2 + n] if scatter else h_refs

        def to_owners(i, k):
            out_sems, in_sems = rest[2 + n:]
            rows = pl.ds(k * rbs[i], rbs[i])
            return [pltpu.make_async_remote_copy(
                src_ref=sum_refs[i].at[2 * chip[0] + chip[1], rows], dst_ref=h_refs[i].at[j, rows],
                send_sem=out_sems.at[3 * (starts[i] + k) + j], recv_sem=in_sems.at[3 * (starts[i] + k) + j],
                device_id=(*chip, c), device_id_type=MESH) for j, chip in enumerate(_other_chips(x, y))]

        copies = [[pltpu.make_async_remote_copy(
            src_ref=rows_of(src_refs[i], i, (1 - c) * halves[i] + k * rbs[i]),
            dst_ref=rows_of(their_refs[i], i, k * rbs[i]),
            send_sem=send_sems.at[starts[i] + k], recv_sem=recv_sems.at[starts[i] + k],
            device_id=(x, y, 1 - c), device_id_type=MESH) for k in range(counts[i])] for i in range(n)]

        in_order = [cp for of_matrix in copies for cp in of_matrix]

        @pl.when(t == 0)
        def _():
            for cp in in_order[:ADD_HALVES_AHEAD]:
                cp.start()

        for i in range(n):
            for k in range(counts[i]):
                @pl.when(t == starts[i] + k)
                def _(i=i, k=k):
                    copies[i][k].wait_recv()
                    for cp in in_order[starts[i] + k + ADD_HALVES_AHEAD:][:1]:
                        cp.start()
                    b_ref = rows_of(their_refs[i], i, k * rbs[i])
                    h_ref = sum_refs[i].at[:, pl.ds(k * rbs[i], rbs[i])] if scatter else h_refs[i]
                    if not whole[i]:
                        h_ref[...] = _bf(a_refs[i][...] + b_ref[...])
                        f_refs[i][...] = a_refs[i][q] + b_ref[q]
                    else:
                        total_i = a_refs[i][...] + b_ref[...]
                        for k4 in range(N_CHIPS):
                            piece = total_i[:, k4 * cols[i]:(k4 + 1) * cols[i]]
                            h_ref[k4] = _bf(piece)

                            @pl.when(q == k4)
                            def _():
                                f_refs[i][...] = piece
                    if scatter:
                        for cp in to_owners(i, k):
                            cp.start()

        @pl.when(t == total - 1)
        def _():
            for cp in in_order:
                cp.wait_send()
            if scatter:
                for i in range(n):
                    for k in range(counts[i]):
                        for cp in to_owners(i, k):
                            cp.wait()

    def specs(i):
        step = lambda t: _local_step(t, starts[i], counts[i])
        by_quarter = (N_CHIPS, rbs[i], cols[i])
        block = (rbs[i], grads[i].shape[-1]) if whole[i] else by_quarter
        lead = () if whole[i] else (0,)
        mine = pl.BlockSpec(block, lambda t, place: (*lead, place[0] * counts[i] + step(t), 0))
        sums = pl.BlockSpec(by_quarter, lambda t, place: (0, step(t), 0))
        own = pl.BlockSpec(by_quarter[1:], lambda t, place: (step(t), 0))
        return mine, own, sums

    all_specs = [specs(i) for i in range(n)]
    sum_shapes = [(N_CHIPS, h, cl) for h, cl in zip(halves, cols)]
    scratch = [pltpu.VMEM(sh, F32) for sh in half_shapes]
    scratch += [pltpu.SemaphoreType.DMA((total,)), pltpu.SemaphoreType.DMA((total,))]
    if scatter:
        scratch += [pltpu.VMEM(sh, BF16) for sh in sum_shapes]
        scratch += [pltpu.SemaphoreType.DMA((3 * total,)), pltpu.SemaphoreType.DMA((3 * total,))]
    outs = pl.pallas_call(
        body, name=name,
        grid_spec=pltpu.PrefetchScalarGridSpec(
            num_scalar_prefetch=1, grid=(total,),
            in_specs=[sp[0] for sp in all_specs] + _any_specs(n),
            out_specs=[sp[1] for sp in all_specs] + (_any_specs(n) if scatter else [sp[2] for sp in all_specs]),
            scratch_shapes=scratch),
        out_shape=[jax.ShapeDtypeStruct((h, cl), F32) for h, cl in zip(halves, cols)]
                  + [jax.ShapeDtypeStruct((N_CHIPS - 1 if scatter else N_CHIPS, *sh[1:]), BF16) for sh in sum_shapes],
        compiler_params=_params("arbitrary"),
    )(place, *grads, *grads)
    return list(zip(outs[:n], outs[n:]))


SMALL_SUM_ROWS = 16


def join_halves(owns, gots, small_pool, small_gla, small_top, g_gk_pad):
    n = len(owns)
    shapes = [g.shape for g in gots]
    rbs = [min(ADD_ROWS, sh[1]) for sh in shapes]
    counts = [sh[1] // rb for sh, rb in zip(shapes, rbs)]
    starts, total = _spans(counts)

    def body(*refs):
        o_refs, g_refs = refs[:n], refs[n:2 * n]
        pool_ref, gla_ref, top_ref, gk_ref = refs[2 * n:2 * n + 4]
        out_refs, total_ref = refs[2 * n + 4:3 * n + 4], refs[3 * n + 4]
        sum_refs = refs[3 * n + 5:4 * n + 5]
        local_sems, send_sems, recv_sems, all_ref, small_ref = refs[4 * n + 5:4 * n + 10]
        t = pl.program_id(0)
        x, y, c = _position()
        start_small, wait_small = _gather_small(small_ref, all_ref, *refs[4 * n + 10:])

        def copies(i, k):
            src = sum_refs[i].at[pl.ds(k * rbs[i], rbs[i])]
            rows = pl.ds(c * shapes[i][1] + k * rbs[i], rbs[i])
            return (pltpu.make_async_copy(src, out_refs[i].at[rows], local_sems.at[starts[i] + k]),
                    pltpu.make_async_remote_copy(
                        src_ref=src, dst_ref=out_refs[i].at[rows],
                        send_sem=send_sems.at[starts[i] + k], recv_sem=recv_sems.at[starts[i] + k],
                        device_id=(x, y, 1 - c), device_id_type=MESH))

        @pl.when(t == 0)
        def _():
            small_ref[0:3, :] = pool_ref[0:3, :]
            small_ref[3:5, :] = gla_ref[0:2, :]
            small_ref[5:8, :] = top_ref[0:3, :]
            for r in range(GATE_RANK):
                small_ref[8 + r // 2:9 + r // 2, (r % 2) * KEY_W:(r % 2 + 1) * KEY_W] = gk_ref[r:r + 1, :]
            start_small()

        for i in range(n):
            for k in range(counts[i]):
                @pl.when(t == starts[i] + k)
                def _(i=i, k=k):
                    total_i = o_refs[i][...]
                    for j in range(N_CHIPS - 1):
                        total_i = total_i + g_refs[i][j].astype(F32)
                    sum_refs[i][k * rbs[i]:(k + 1) * rbs[i], :] = total_i
                    for cp in copies(i, k):
                        cp.start()

        @pl.when(t == total - 1)
        def _():
            wait_small()
            small_total = all_ref[0]
            for dev in range(1, N_DEV):
                small_total = small_total + all_ref[dev]
            total_ref[...] = small_total
            for i in range(n):
                for k in range(counts[i]):
                    for cp in copies(i, k):
                        cp.wait()

    def specs(i):
        rb, cols = rbs[i], shapes[i][2]
        step = lambda t: _local_step(t, starts[i], counts[i])
        return (pl.BlockSpec((rb, cols), lambda t: (step(t), 0)),
                pl.BlockSpec((N_CHIPS - 1, rb, cols), lambda t: (0, step(t), 0)))

    all_specs = [specs(i) for i in range(n)]
    outs = pl.pallas_call(
        body, name="join_halves", grid=(total,),
        out_shape=[jax.ShapeDtypeStruct((2 * sh[1], sh[2]), F32) for sh in shapes]
                  + [jax.ShapeDtypeStruct((SMALL_SUM_ROWS, D), F32)],
        in_specs=[sp[0] for sp in all_specs] + [sp[1] for sp in all_specs] + [VMEM_SPEC] * 4,
        out_specs=_any_specs(n) + [VMEM_SPEC],
        scratch_shapes=[pltpu.VMEM(sh[1:], F32) for sh in shapes]
                       + [pltpu.SemaphoreType.DMA((total,)), pltpu.SemaphoreType.DMA((total,)),
                          pltpu.SemaphoreType.DMA((total,)),
                          pltpu.VMEM((N_DEV, SMALL_SUM_ROWS, D), F32), pltpu.VMEM((SMALL_SUM_ROWS, D), F32)]
                       + SMALL_SEMS,
        compiler_params=_params("arbitrary"),
    )(*owns, *gots, small_pool, small_gla, small_top, g_gk_pad)
    return outs[:n], outs[n]


def _adam_math(w, g, m, v):
    m = ADAM_B1 * m + (1.0 - ADAM_B1) * g
    v = ADAM_B2 * v + (1.0 - ADAM_B2) * (g * g)
    m_hat = m / (1.0 - ADAM_B1 ** ADAM_STEP)
    v_hat = v / (1.0 - ADAM_B2 ** ADAM_STEP)
    delta = -ADAM_LR * (m_hat / (jnp.sqrt(v_hat) + ADAM_EPS) + ADAM_WD * w)
    return delta, m, v


ADAM_BLOCK_BYTES = 2 ** 19
ADAM_MOST_STEPS = 8


def adamw(params, name):
    n = len(params)
    shapes = [p[0].shape for p in params]

    def tile_rows(shape):
        rows, cols = shape[0], shape[-1]
        aligned = 1 if len(shape) == 3 else 8
        divisors = [t for t in range(aligned, rows + 1, aligned) if rows % t == 0]
        tile = max(t for t in divisors if t * cols * 4 <= ADAM_BLOCK_BYTES)
        if rows // tile > ADAM_MOST_STEPS:
            tile = min(t for t in divisors if rows // t <= ADAM_MOST_STEPS)
        return tile

    tiles = [tile_rows(sh) for sh in shapes]
    counts = [sh[0] // tl for sh, tl in zip(shapes, tiles)]
    starts, total = _spans(counts)

    def body(*refs):
        ins, outs = refs[:4 * n], refs[4 * n:]
        t = pl.program_id(0)
        for i in range(n):
            @pl.when((t >= starts[i]) & (t < starts[i] + counts[i]))
            def _(i=i):
                w_ref, g_ref, m_ref, v_ref = ins[4 * i:4 * i + 4]
                d, nm, nv = _adam_math(w_ref[...], g_ref[...], m_ref[...], v_ref[...])
                outs[3 * i][...] = d
                outs[3 * i + 1][...] = nm
                outs[3 * i + 2][...] = nv

    def spec(i):
        block = (tiles[i],) + shapes[i][1:]
        zeros = (0,) * (len(block) - 1)
        return pl.BlockSpec(block, lambda t: (_local_step(t, starts[i], counts[i]),) + zeros)

    outs = pl.pallas_call(
        body, name=name, grid=(total,),
        out_shape=[jax.ShapeDtypeStruct(sh, F32) for sh in shapes for _ in range(3)],
        in_specs=[spec(i) for i in range(n) for _ in range(4)],
        out_specs=[spec(i) for i in range(n) for _ in range(3)],
        compiler_params=_params("arbitrary"),
    )(*[a for p in params for a in p])
    return [tuple(outs[3 * i:3 * i + 3]) for i in range(n)]


def adamw_small(params):
    n = len(params)

    def body(*refs):
        ins, outs = refs[:4 * n], refs[4 * n:]
        for k in range(n):
            w_ref, g_ref, m_ref, v_ref = ins[4 * k:4 * k + 4]
            d, nm, nv = _adam_math(w_ref[...], g_ref[...], m_ref[...], v_ref[...])
            outs[3 * k][...] = d
            outs[3 * k + 1][...] = nm
            outs[3 * k + 2][...] = nv

    flat = [a for p in params for a in p]
    outs = pl.pallas_call(
        body, name="adamw_small",
        out_shape=[jax.ShapeDtypeStruct(p[0].shape, F32) for p in params for _ in range(3)],
        in_specs=[VMEM_SPEC] * (4 * n), out_specs=[VMEM_SPEC] * (3 * n),
    )(*flat)
    return [tuple(outs[3 * k:3 * k + 3]) for k in range(n)]


def matmul_tn(a, b, name, tile_n, by_column_tile=False, chip_sums=()):
    s, m = a.shape
    n = b.shape[1]
    n_sums = len(chip_sums)
    steps = n // tile_n
    if by_column_tile:
        out_shape = jax.ShapeDtypeStruct((steps, m, tile_n), F32)
        out_spec = pl.BlockSpec((None, m, tile_n), lambda j: (j, 0, 0))
    else:
        out_shape = jax.ShapeDtypeStruct((m, n), F32)
        out_spec = pl.BlockSpec((m, tile_n), lambda j: (0, j))

    def body(a_ref, b_ref, *rest):
        sum_refs, out_ref, got_refs = rest[:n_sums], rest[n_sums], rest[n_sums + 1:2 * n_sums + 1]
        j = pl.program_id(0)
        copies = _scatter_copies(sum_refs, got_refs, *rest[2 * n_sums + 1:]) if n_sums else []

        @pl.when(j == 0)
        def _():
            for cp in copies:
                cp.start()

        out_ref[...] = _tn(a_ref[...], b_ref[...])

        @pl.when(j == steps - 1)
        def _():
            for cp in copies:
                cp.wait()

    outs = pl.pallas_call(
        body, name=name, grid=(steps,),
        out_shape=[out_shape] + _scatter_shapes(chip_sums),
        in_specs=[_full((s, m)), pl.BlockSpec((s, tile_n), lambda j: (0, j))] + _any_specs(n_sums),
        out_specs=[out_spec] + _any_specs(n_sums),
        scratch_shapes=[pltpu.SemaphoreType.DMA((3 * n_sums,)), pltpu.SemaphoreType.DMA((3 * n_sums,))]
                       if n_sums else [],
        compiler_params=_params("arbitrary"),
    )(a, b, *chip_sums)
    return outs[0], outs[1:]


ROW_TILE = 512


def _row_index(tile, rows):
    return tile * rows + lax.broadcasted_iota(jnp.int32, (rows, 1), 0)


def _inverse_counts(t_glob):
    return [1.0 / jnp.minimum(t_glob + 1, w).astype(F32) for w in POOL_WINDOWS]


def _sigmoid(z):
    return 1.0 / (1.0 + jnp.exp(-z))


def _trailing_sums(src, tmp, cols, window, rows):
    bufs = (src, tmp)
    span, level, start = 1, 0, 0
    while span < window:
        start += 8
        a, b = bufs[level % 2], bufs[(level + 1) % 2]
        n = HALO + rows - start
        b[start:start + n, cols] = a[start:start + n, cols] + a[start - span:start - span + n, cols]
        span, level = 2 * span, level + 1
    return bufs[level % 2][HALO:HALO + rows, cols]


def _leading_sums(src, tmp, cols, window, rows):
    bufs = (src, tmp)
    span, level, n = 1, 0, rows + HALO
    while span < window:
        n -= 8
        a, b = bufs[level % 2], bufs[(level + 1) % 2]
        b[0:n, cols] = a[0:n, cols] + a[span:span + n, cols]
        span, level = 2 * span, level + 1
    return bufs[level % 2][0:rows, cols]


def gather_in_background(step, last, out_refs, send_sems, recv_sems, finish):
    n = len(out_refs)
    x, y, c = _position()
    q = 2 * x + y
    chips = _other_chips(x, y)

    def copy(k, i, quarter, half, to):
        return _gather_copy(out_refs[i], send_sems, recv_sems, k * n + i, quarter, half, to)

    if not finish:
        @pl.when(step == 0)
        def _():
            for i in range(n):
                mine, _ = _halves(out_refs[i].shape[1], c)
                for j, chip in enumerate(chips):
                    copy(j, i, q, mine, (*chip, c)).start()

        @pl.when(step == last)
        def _():
            for j, chip in enumerate(chips):
                qj = 2 * chip[0] + chip[1]
                for i in range(n):
                    mine, _ = _halves(out_refs[i].shape[1], c)
                    copy(j, i, qj, mine, (x, y, c)).wait_recv()
                    copy(3 + j, i, qj, mine, (x, y, 1 - c)).start()
        return

    @pl.when(step == last)
    def _():
        for j, chip in enumerate(chips):
            qj = 2 * chip[0] + chip[1]
            for i in range(n):
                mine, other = _halves(out_refs[i].shape[1], c)
                copy(3 + j, i, qj, other, (x, y, c)).wait_recv()
                copy(j, i, q, mine, (x, y, c)).wait_send()
                copy(3 + j, i, qj, mine, (x, y, c)).wait_send()


def pool_forward(x, w0, wpi, gw, gb, scale, wpo, later):
    s = x.shape[0]
    ts = ROW_TILE
    nt = s // ts
    assert nt >= 2
    n_later = len(later)

    def body(x_ref, w0_ref, wpi_ref, gw_ref, gb_ref, sc_ref, wpo_ref, *rest):
        rest = rest[n_later:]
        h1_ref, pooled_ref, gt_ref, n0_ref = rest[:4]
        later_refs = rest[4:4 + n_later]
        ubuf, tbuf, hist, send_sems, recv_sems = rest[4 + n_later:]
        i = pl.program_id(0)
        gather_in_background(i, nt - 1, later_refs, send_sems, recv_sems, finish=False)
        xv = x_ref[...]
        r = lax.rsqrt(jnp.mean(xv * xv, axis=-1, keepdims=True) + EPS)
        n0 = _bf(xv * r * w0_ref[...])
        n0_ref[...] = n0
        u = jnp.concatenate([_nn(n0, wpi_ref[0]), _nn(n0, wpi_ref[1])], axis=-1)
        gt = jnp.concatenate([_nn(n0, wpi_ref[2]), _nn(n0, wpi_ref[3])], axis=-1)
        gt_ref[...] = gt

        @pl.when(i == 0)
        def _():
            hist[...] = jnp.zeros_like(hist)

        ubuf[0:HALO, :] = hist[...]
        ubuf[HALO:HALO + ts, :] = u
        hist[...] = u[ts - HALO:, :]
        inv = _inverse_counts(_row_index(i, ts))
        mixed = []
        for g, w in enumerate(POOL_WINDOWS):
            cols = slice(g * GROUP_DIM, (g + 1) * GROUP_DIM)
            pooled = _bf(_trailing_sums(ubuf, tbuf, cols, w, ts) * inv[g] - u[:, cols])
            pooled_ref[:, cols] = pooled
            mixed.append(_nn(pooled, gw_ref[g]))
        mixed = jnp.concatenate(mixed, axis=-1) + gb_ref[...]
        y = mixed * sc_ref[...] * (gt * _sigmoid(gt))
        h1_ref[...] = xv + _nn(_bf(y), wpo_ref[...])
        gather_in_background(i, nt - 1, later_refs, send_sems, recv_sems, finish=True)

    row = lambda cols: pl.BlockSpec((ts, cols), lambda i: (i, 0))
    outs = pl.pallas_call(
        body, name="pool_forward", grid=(nt,),
        out_shape=[jax.ShapeDtypeStruct((s, D), F32), jax.ShapeDtypeStruct((s, D), BF16),
                   jax.ShapeDtypeStruct((s, D), F32), jax.ShapeDtypeStruct((s, D), BF16)]
                  + [jax.ShapeDtypeStruct(a.shape, a.dtype) for a in later],
        in_specs=[row(D), _full((1, D)), _full((N_CHIPS, D, D // 2)), _full((GROUPS, GROUP_DIM, GROUP_DIM)),
                  _full((1, D)), _full((1, D)), _full((D, D))] + _any_specs(n_later),
        out_specs=[row(D), row(D), row(D), row(D)] + _any_specs(n_later),
        input_output_aliases={7 + k: 4 + k for k in range(n_later)},
        scratch_shapes=[pltpu.VMEM((HALO + ts, D), F32), pltpu.VMEM((HALO + ts, D), F32),
                        pltpu.VMEM((HALO, D), F32),
                        pltpu.SemaphoreType.DMA((6 * n_later,)), pltpu.SemaphoreType.DMA((6 * n_later,))],
        compiler_params=_params("arbitrary"),
    )(x, w0, wpi, gw, gb, scale, wpo, *later)
    return outs[:4], outs[4:]


def pool_backward(x, dh1, pooled, gt, w0, wpi, gw, gb, scale, wpo, chip_sums):
    s = x.shape[0]
    ts = ROW_TILE
    nt = s // ts
    n_sums = len(chip_sums)

    def body(x_ref, dh1_ref, pooled_ref, gt_ref, w0_ref, wpi_ref, gw_ref, gb_ref, sc_ref, wpo_ref, *rest):
        sum_refs, rest = rest[:n_sums], rest[n_sums:]
        dx_ref, dproj_ref, gpo_ref, ggw_ref, small_ref = rest[:5]
        got_refs = rest[5:5 + n_sums]
        ebuf, tbuf, ahead, send_sems, recv_sems = rest[5 + n_sums:]
        i = pl.program_id(0)
        copies = _scatter_copies(sum_refs, got_refs, send_sems, recv_sems)

        @pl.when(i == 0)
        def _():
            for cp in copies:
                cp.start()

        @pl.when(i == 0)
        def _():
            gpo_ref[...] = jnp.zeros_like(gpo_ref)
            ggw_ref[...] = jnp.zeros_like(ggw_ref)
            small_ref[...] = jnp.zeros_like(small_ref)
            ahead[...] = jnp.zeros_like(ahead)

        dh1 = dh1_ref[...]
        dh1_bf = _bf(dh1)
        gt = gt_ref[...]
        sc = sc_ref[...]
        dy = _nt(dh1_bf, wpo_ref[...])
        pooled_bf = []
        mixed = []
        for g in range(GROUPS):
            cols = slice(g * GROUP_DIM, (g + 1) * GROUP_DIM)
            pb = pooled_ref[:, cols]
            pooled_bf.append(pb)
            mixed.append(_nn(pb, gw_ref[g]))
        mixed = jnp.concatenate(mixed, axis=-1) + gb_ref[...]
        sg = _sigmoid(gt)
        silu = gt * sg
        gpo_ref[...] += _tn(_bf(mixed * sc * silu), dh1_bf)
        dmixed = dy * sc * silu
        dgt = dy * mixed * sc * (sg * (1.0 + gt * (1.0 - sg)))
        dproj_ref[:, D:] = _bf(dgt)
        small_ref[1:2, :] += jnp.sum(dy * mixed * silu, axis=0, keepdims=True)
        small_ref[2:3, :] += jnp.sum(dmixed, axis=0, keepdims=True)

        inv = _inverse_counts(_row_index(nt - 1 - i, ts))
        ebuf[ts:ts + HALO, :] = ahead[...]
        dpooled = []
        for g in range(GROUPS):
            cols = slice(g * GROUP_DIM, (g + 1) * GROUP_DIM)
            dm = _bf(dmixed[:, cols])
            ggw_ref[g] += _tn(pooled_bf[g], dm)
            dp = _nt(dm, gw_ref[g])
            dpooled.append(dp)
            ebuf[0:ts, cols] = dp * inv[g]
        ahead[...] = ebuf[0:HALO, :]
        du = []
        for g, w in enumerate(POOL_WINDOWS):
            cols = slice(g * GROUP_DIM, (g + 1) * GROUP_DIM)
            du.append(_leading_sums(ebuf, tbuf, cols, w, ts) - dpooled[g])
        du = _bf(jnp.concatenate(du, axis=-1))
        dproj_ref[:, :D] = du
        dgt_bf = _bf(dgt)
        half = D // 2
        dn0 = (_nt(du[:, :half], wpi_ref[0]) + _nt(du[:, half:], wpi_ref[1])
               + _nt(dgt_bf[:, :half], wpi_ref[2]) + _nt(dgt_bf[:, half:], wpi_ref[3]))

        xv = x_ref[...]
        r = lax.rsqrt(jnp.mean(xv * xv, axis=-1, keepdims=True) + EPS)
        xhat = xv * r
        small_ref[0:1, :] += jnp.sum(dn0 * xhat, axis=0, keepdims=True)
        dxh = dn0 * w0_ref[...]
        dx_ref[...] = dh1 + r * (dxh - xhat * jnp.mean(dxh * xhat, axis=-1, keepdims=True))

        @pl.when(i == nt - 1)
        def _():
            for cp in copies:
                cp.wait()

    row = lambda cols: pl.BlockSpec((ts, cols), lambda i: (nt - 1 - i, 0))
    outs = pl.pallas_call(
        body, name="pool_backward", grid=(nt,),
        out_shape=[jax.ShapeDtypeStruct((s, D), F32), jax.ShapeDtypeStruct((s, 2 * D), BF16),
                   jax.ShapeDtypeStruct((D, D), F32),
                   jax.ShapeDtypeStruct((GROUPS, GROUP_DIM, GROUP_DIM), F32),
                   jax.ShapeDtypeStruct((8, D), F32)] + _scatter_shapes(chip_sums),
        in_specs=[row(D), row(D), row(D), row(D), _full((1, D)), _full((N_CHIPS, D, D // 2)),
                  _full((GROUPS, GROUP_DIM, GROUP_DIM)), _full((1, D)), _full((1, D)), _full((D, D))]
                 + _any_specs(n_sums),
        out_specs=[row(D), row(2 * D), _full((D, D)), _full((GROUPS, GROUP_DIM, GROUP_DIM)), _full((8, D))]
                  + _any_specs(n_sums),
        scratch_shapes=[pltpu.VMEM((ts + HALO, D), F32), pltpu.VMEM((ts + HALO, D), F32),
                        pltpu.VMEM((HALO, D), F32),
                        pltpu.SemaphoreType.DMA((3 * n_sums,)), pltpu.SemaphoreType.DMA((3 * n_sums,))],
        compiler_params=_params("arbitrary"),
    )(x, dh1, pooled, gt, w0, wpi, gw, gb, scale, wpo, *chip_sums)
    return outs[:5], outs[5:]


def gla_project(h1, w1, wgi_q, wgk, bgk, later):
    s = h1.shape[0]
    ts = ROW_TILE
    nt = s // ts
    assert nt >= 2
    n_later = len(later)

    def body(h_ref, w1_ref, wq_ref, wgk_ref, bgk_ref, *rest):
        rest = rest[n_later:]
        qk_ref, v_ref, gate_ref, low_ref, cum_ref, n1_ref = rest[:6]
        later_refs = rest[6:6 + n_later]
        send_sems, recv_sems, wgi_ref = rest[6 + n_later:]
        gather_in_background(pl.program_id(0), nt - 1, later_refs, send_sems, recv_sems, finish=False)

        @pl.when(pl.program_id(0) == 0)
        def _():
            _assemble_gla_in(wq_ref, wgi_ref)

        hv = h_ref[...]
        r = lax.rsqrt(jnp.mean(hv * hv, axis=-1, keepdims=True) + EPS)
        n1 = _bf(hv * r * w1_ref[...])
        n1_ref[...] = n1
        qk_ref[...] = _nn(n1, wgi_ref[:, 0:2 * KEY_W])
        v_ref[...] = _bf(_nn(n1, wgi_ref[:, 2 * KEY_W:2 * KEY_W + D]))
        gate_ref[...] = _nn(n1, wgi_ref[:, 2 * KEY_W + D:GLA_MAIN])
        low = _bf(_nn(n1, wgi_ref[:, GLA_MAIN:]))
        low_ref[...] = low
        z = _nn(low, wgk_ref[...]) + bgk_ref[...]
        lg = (jnp.minimum(z, 0.0) - jnp.log(1.0 + jnp.exp(-jnp.abs(z)))) / GATE_NORM
        lower_f = _chunk_masks()[0].astype(F32)
        for r0 in range(0, ts, CHUNK):
            cum_ref[r0:r0 + CHUNK, :] = _nn_exact(lower_f, lg[r0:r0 + CHUNK, :])
        gather_in_background(pl.program_id(0), nt - 1, later_refs, send_sems, recv_sems, finish=True)

    row = lambda cols: pl.BlockSpec((ts, cols), lambda i: (i, 0))
    outs = pl.pallas_call(
        body, name="gla_project", grid=(nt,),
        out_shape=[jax.ShapeDtypeStruct((s, D), F32), jax.ShapeDtypeStruct((s, D), BF16),
                   jax.ShapeDtypeStruct((s, D), F32), jax.ShapeDtypeStruct((s, RANK_PAD), BF16),
                   jax.ShapeDtypeStruct((s, KEY_W), F32), jax.ShapeDtypeStruct((s, D), BF16)]
                  + [jax.ShapeDtypeStruct(a.shape, a.dtype) for a in later],
        in_specs=[row(D), _full((1, D)), _full((N_CHIPS, D, GLA_IN_QUARTER)),
                  _full((RANK_PAD, KEY_W)), _full((1, KEY_W))] + _any_specs(n_later),
        out_specs=[row(D), row(D), row(D), row(RANK_PAD), row(KEY_W), row(D)] + _any_specs(n_later),
        input_output_aliases={5 + k: 6 + k for k in range(n_later)},
        scratch_shapes=[pltpu.SemaphoreType.DMA((6 * n_later,)), pltpu.SemaphoreType.DMA((6 * n_later,)),
                        pltpu.VMEM((D, GLA_MAIN + RANK_PAD), BF16)],
        compiler_params=_params("arbitrary"),
    )(h1, w1, wgi_q, wgk, bgk, *later)
    return outs[:6], outs[6:]


def _assemble_gla_in(wq_ref, wfull):
    pad = jnp.zeros((CAST_ROWS, GLA_MAIN + RANK_PAD - GLA_IN), BF16)
    for r0 in range(0, D, CAST_ROWS):
        rows = slice(r0, r0 + CAST_ROWS)
        wfull[rows, :] = jnp.concatenate([wq_ref[q, rows, :] for q in range(N_CHIPS)] + [pad], axis=1)


GLA_BLOCK = 512
CHUNKS_PER_BLOCK = GLA_BLOCK // CHUNK


def _chunk_masks():
    t = lax.broadcasted_iota(jnp.int32, (CHUNK, CHUNK), 0)
    u = lax.broadcasted_iota(jnp.int32, (CHUNK, CHUNK), 1)
    return t >= u, t <= u


def _gla_chunk_terms(q, cum):
    ep = jnp.exp(cum)
    en = jnp.exp(-cum)
    qs = q * (HEAD_K ** -0.5)
    last = cum[CHUNK - 1:CHUNK, :]
    ed = jnp.exp(last - cum)
    dec = jnp.exp(last)
    return ep, en, qs, ed, dec


def gla_forward(qk, v, cum):
    s = qk.shape[0]
    nb = s // GLA_BLOCK
    nc = s // CHUNK

    def body(q_ref, k_ref, v_ref, cum_ref, o_ref, st_ref, sc_ref, state):
        @pl.when(pl.program_id(0) == 0)
        def _():
            state[...] = jnp.zeros_like(state)

        lower, _ = _chunk_masks()

        def chunk(cc, carry):
            rows = pl.ds(pl.multiple_of(cc * CHUNK, CHUNK), CHUNK)
            for h in range(HEADS):
                kc = slice(h * HEAD_K, (h + 1) * HEAD_K)
                vc = slice(h * HEAD_V, (h + 1) * HEAD_V)
                q = q_ref[rows, kc]
                k = k_ref[rows, kc]
                v = v_ref[rows, vc]
                ep, en, qs, ed, dec = _gla_chunk_terms(q, cum_ref[rows, kc])
                a = _bf(qs * ep)
                fwd = _nt(a, _bf(k * en))
                bwd = _nt(_bf(qs * en), _bf(k * ep))
                scores = _bf(jnp.where(lower, fwd, bwd))
                sc_ref[rows, h * CHUNK:(h + 1) * CHUNK] = scores
                st = state[h]
                st_ref[cc, h] = st
                o_ref[rows, vc] = _nn(scores, v) + _nt(a, _bf(st))
                state[h] = st * dec + _tn(v, _bf(k * ed))
            return carry

        lax.fori_loop(0, CHUNKS_PER_BLOCK, chunk, 0, unroll=4)

    return pl.pallas_call(
        body, name="gla_forward", grid=(nb,),
        out_shape=(jax.ShapeDtypeStruct((s, D), F32),
                   jax.ShapeDtypeStruct((nc, HEADS, HEAD_V, HEAD_K), F32),
                   jax.ShapeDtypeStruct((s, HEADS * CHUNK), BF16)),
        in_specs=[pl.BlockSpec((GLA_BLOCK, KEY_W), lambda i: (i, 0)),
                  pl.BlockSpec((GLA_BLOCK, KEY_W), lambda i: (i, 1)),
                  pl.BlockSpec((GLA_BLOCK, D), lambda i: (i, 0)),
                  pl.BlockSpec((GLA_BLOCK, KEY_W), lambda i: (i, 0))],
        out_specs=(pl.BlockSpec((GLA_BLOCK, D), lambda i: (i, 0)),
                   pl.BlockSpec((CHUNKS_PER_BLOCK, HEADS, HEAD_V, HEAD_K), lambda i: (i, 0, 0, 0)),
                   pl.BlockSpec((GLA_BLOCK, HEADS * CHUNK), lambda i: (i, 0))),
        scratch_shapes=[pltpu.VMEM((HEADS, HEAD_V, HEAD_K), F32)],
        compiler_params=_params("arbitrary"),
    )(qk, qk, v, cum)


def gla_backward(qk, v, cum, do, states, scores):
    s = qk.shape[0]
    nb = s // GLA_BLOCK

    def body(q_ref, k_ref, v_ref, cum_ref, do_ref, st_ref, sc_ref, dq_ref, dk_ref, dv_ref, dcum_ref, dstate):
        @pl.when(pl.program_id(0) == 0)
        def _():
            dstate[...] = jnp.zeros_like(dstate)

        lower, _ = _chunk_masks()
        is_last = lax.broadcasted_iota(jnp.int32, (CHUNK, HEAD_K), 0) == CHUNK - 1

        def chunk(step, carry):
            cc = CHUNKS_PER_BLOCK - 1 - step
            rows = pl.ds(pl.multiple_of(cc * CHUNK, CHUNK), CHUNK)
            for h in range(HEADS):
                kc = slice(h * HEAD_K, (h + 1) * HEAD_K)
                vc = slice(h * HEAD_V, (h + 1) * HEAD_V)
                q = q_ref[rows, kc]
                k = k_ref[rows, kc]
                v = v_ref[rows, vc]
                do_c = do_ref[rows, vc]
                ep, en, qs, ed, dec = _gla_chunk_terms(q, cum_ref[rows, kc])
                a = _bf(qs * ep)
                b = _bf(k * en)
                c = _bf(qs * en)
                dk_dec = _bf(k * ep)
                kd = _bf(k * ed)
                scores = sc_ref[rows, h * CHUNK:(h + 1) * CHUNK]
                st = st_ref[cc, h]
                dst = dstate[h]
                dst_bf = _bf(dst)

                dscores = _nt(do_c, v)
                dfwd = _bf(jnp.where(lower, dscores, 0.0))
                dbwd = _bf(jnp.where(lower, 0.0, dscores))
                dv_ref[rows, vc] = _bf(_tn(scores, do_c) + _nt(kd, dst_bf))
                da = _nn(dfwd, b) + _nn(do_c, _bf(st))
                db = _tn(dfwd, a)
                dc = _nn(dbwd, dk_dec)
                ddk = _tn(dbwd, c)
                dkd = _nn(v, dst_bf)
                ddec = jnp.sum(dst * st, axis=0, keepdims=True)
                dstate[h] = dst * dec + _tn(do_c, a)

                m = dkd * k * ed
                dq_ref[rows, kc] = _bf((da * ep + dc * en) * (HEAD_K ** -0.5))
                dk_ref[rows, kc] = _bf(db * en + ddk * ep + dkd * ed)
                dcum = (da * qs + ddk * k) * ep - (db * k + dc * qs) * en - m
                dlast = jnp.sum(m, axis=0, keepdims=True) + ddec * dec
                dcum_ref[rows, kc] = dcum + jnp.where(is_last, dlast, 0.0)
            return carry

        lax.fori_loop(0, CHUNKS_PER_BLOCK, chunk, 0, unroll=4)

    rev = lambda cols, col_block: pl.BlockSpec((GLA_BLOCK, cols), lambda i: (nb - 1 - i, col_block))
    return pl.pallas_call(
        body, name="gla_backward", grid=(nb,),
        out_shape=(jax.ShapeDtypeStruct((s, KEY_W), BF16), jax.ShapeDtypeStruct((s, KEY_W), BF16),
                   jax.ShapeDtypeStruct((s, D), BF16), jax.ShapeDtypeStruct((s, KEY_W), F32)),
        in_specs=[rev(KEY_W, 0), rev(KEY_W, 1), rev(D, 0), rev(KEY_W, 0), rev(D, 0),
                  pl.BlockSpec((CHUNKS_PER_BLOCK, HEADS, HEAD_V, HEAD_K), lambda i: (nb - 1 - i, 0, 0, 0)),
                  rev(HEADS * CHUNK, 0)],
        out_specs=(rev(KEY_W, 0), rev(KEY_W, 0), rev(D, 0), rev(KEY_W, 0)),
        scratch_shapes=[pltpu.VMEM((HEADS, HEAD_V, HEAD_K), F32)],
        compiler_params=_params("arbitrary"),
    )(qk, qk, v, cum, do, states, scores)


def head_and_loss(o, gate, h1, target, hw, wgo, wf):
    s = o.shape[0]
    ts = ROW_TILE

    def body(o_ref, gate_ref, h1_ref, tgt_ref, hw_ref, wgo_ref, wf_ref,
             dh2_ref, do_ref, dgate_ref, ggo_ref, small_ref):
        @pl.when(pl.program_id(0) == 0)
        def _():
            ggo_ref[...] = jnp.zeros_like(ggo_ref)
            small_ref[...] = jnp.zeros_like(small_ref)

        gate = gate_ref[...]
        hw = hw_ref[...]
        sg = _sigmoid(gate)
        silu = gate * sg
        ohat, ro = [], []
        for h in range(HEADS):
            oh = o_ref[:, h * HEAD_V:(h + 1) * HEAD_V]
            rh = lax.rsqrt(jnp.mean(oh * oh, axis=-1, keepdims=True) + EPS)
            ro.append(rh)
            ohat.append(oh * rh)
        ohat = jnp.concatenate(ohat, axis=-1)
        on = ohat * hw
        y2 = _bf(on * silu)
        h2 = h1_ref[...] + _nn(y2, wgo_ref[...])
        rf = lax.rsqrt(jnp.mean(h2 * h2, axis=-1, keepdims=True) + EPS)
        h2hat = h2 * rf
        wf = wf_ref[...]
        diff = h2hat * wf - tgt_ref[...]
        small_ref[2:3, :] += jnp.zeros((1, D), F32) + 0.5 * jnp.sum(diff * diff) / D
        dout = diff / D
        small_ref[0:1, :] += jnp.sum(dout * h2hat, axis=0, keepdims=True)
        dxh = dout * wf
        dh2 = rf * (dxh - h2hat * jnp.mean(dxh * h2hat, axis=-1, keepdims=True))
        dh2_ref[...] = dh2
        dh2_bf = _bf(dh2)
        ggo_ref[...] += _tn(y2, dh2_bf)
        dy2 = _nt(dh2_bf, wgo_ref[...])
        don = dy2 * silu
        dgate_ref[...] = _bf(dy2 * on * (sg * (1.0 + gate * (1.0 - sg))))
        ghw = jnp.sum(don * ohat, axis=0, keepdims=True)
        small_ref[1:2, 0:HEAD_V] += sum(ghw[:, h * HEAD_V:(h + 1) * HEAD_V] for h in range(HEADS))
        dohat = don * hw
        for h in range(HEADS):
            cols = slice(h * HEAD_V, (h + 1) * HEAD_V)
            oh, dh = ohat[:, cols], dohat[:, cols]
            do_ref[:, cols] = _bf(ro[h] * (dh - oh * jnp.mean(dh * oh, axis=-1, keepdims=True)))

    row = lambda cols: pl.BlockSpec((ts, cols), lambda i: (i, 0))
    act = jax.ShapeDtypeStruct((s, D), F32)
    act_bf = jax.ShapeDtypeStruct((s, D), BF16)
    return pl.pallas_call(
        body, name="head_and_loss", grid=(s // ts,),
        out_shape=(act, act_bf, act_bf, jax.ShapeDtypeStruct((D, D), F32), jax.ShapeDtypeStruct((8, D), F32)),
        in_specs=[row(D), row(D), row(D), row(D),
                  _full((1, D)), _full((D, D)), _full((1, D))],
        out_specs=(row(D), row(D), row(D), _full((D, D)), _full((8, D))),
        compiler_params=_params("arbitrary"),
    )(o, gate, h1, target, hw, wgo, wf)


def gla_project_backward(dq, dk, dv, dgate, dcum, low, h1, dh2, w1, wgi_q, wgk, bgk):
    s = h1.shape[0]
    ts = ROW_TILE

    def body(dq_ref, dk_ref, dv_ref, dgate_ref, dcum_ref, low_ref, h1_ref, dh2_ref, w1_ref,
             wq_ref, wgk_ref, bgk_ref, dh1_ref, dproj_ref, ggk_ref, small_ref, wgi_ref):
        @pl.when(pl.program_id(0) == 0)
        def _():
            ggk_ref[...] = jnp.zeros_like(ggk_ref)
            small_ref[...] = jnp.zeros_like(small_ref)
            _assemble_gla_in(wq_ref, wgi_ref)

        low = low_ref[...]
        z = _nn(low, wgk_ref[...]) + bgk_ref[...]
        upper_f = _chunk_masks()[1].astype(F32)
        dlg = jnp.concatenate([_nn_exact(upper_f, dcum_ref[r0:r0 + CHUNK, :]) for r0 in range(0, ts, CHUNK)],
                              axis=0)
        dz = dlg * (1.0 / GATE_NORM) * _sigmoid(-z)
        dz_bf = _bf(dz)
        ggk_ref[...] += _tn(low, dz_bf)
        small_ref[1:2, 0:KEY_W] += jnp.sum(dz, axis=0, keepdims=True)
        dlow = _bf(_nt(dz_bf, wgk_ref[...]))
        dproj_ref[:, GLA_MAIN:] = dlow
        dn1 = _nt(dlow, wgi_ref[:, GLA_MAIN:])
        for ref, lo, hi in ((dq_ref, 0, KEY_W), (dk_ref, KEY_W, 2 * KEY_W),
                            (dv_ref, 2 * KEY_W, 2 * KEY_W + D), (dgate_ref, 2 * KEY_W + D, GLA_MAIN)):
            piece = ref[...]
            dproj_ref[:, lo:hi] = piece
            dn1 = dn1 + _nt(piece, wgi_ref[:, lo:hi])
        hv = h1_ref[...]
        r = lax.rsqrt(jnp.mean(hv * hv, axis=-1, keepdims=True) + EPS)
        hhat = hv * r
        small_ref[0:1, :] += jnp.sum(dn1 * hhat, axis=0, keepdims=True)
        dxh = dn1 * w1_ref[...]
        dh1_ref[...] = dh2_ref[...] + r * (dxh - hhat * jnp.mean(dxh * hhat, axis=-1, keepdims=True))

    row = lambda cols: pl.BlockSpec((ts, cols), lambda i: (i, 0))
    return pl.pallas_call(
        body, name="gla_project_backward", grid=(s // ts,),
        out_shape=(jax.ShapeDtypeStruct((s, D), F32), jax.ShapeDtypeStruct((s, GLA_MAIN + RANK_PAD), BF16),
                   jax.ShapeDtypeStruct((RANK_PAD, KEY_W), F32),
                   jax.ShapeDtypeStruct((8, D), F32)),
        in_specs=[row(KEY_W), row(KEY_W), row(D), row(D), row(KEY_W), row(RANK_PAD), row(D), row(D),
                  _full((1, D)), _full((N_CHIPS, D, GLA_IN_QUARTER)), _full((RANK_PAD, KEY_W)),
                  _full((1, KEY_W))],
        out_specs=(row(D), row(GLA_MAIN + RANK_PAD), _full((RANK_PAD, KEY_W)), _full((8, D))),
        scratch_shapes=[pltpu.VMEM((D, GLA_MAIN + RANK_PAD), BF16)],
        compiler_params=_params("arbitrary"),
    )(dq, dk, dv, dgate, dcum, low, h1, dh2, w1, wgi_q, wgk, bgk)


def _groups_from_quarters(a):
    return a.reshape(N_CHIPS, GROUPS, 64, GROUP_DIM).transpose(1, 0, 2, 3).reshape(GROUPS, GROUP_DIM, GROUP_DIM)


def _quarters_from_groups(a):
    return a.reshape(GROUPS, N_CHIPS, 64, GROUP_DIM).transpose(1, 0, 2, 3).reshape(N_CHIPS, GROUP_DIM, GROUP_DIM)


def local_gradients(xs, target, w0, w1, wf, wpi, gw, gb, scale, wpo, gla_quarters, wgk, bgk, hw_tiled, place):
    wgi_q, wgo_q = gla_quarters
    (h1, pooled, gt, n0), (wgi_q,) = pool_forward(xs, w0, wpi, gw, gb, scale, wpo, [wgi_q])
    (qk, v, gate, low, cum, n1), (wgo_q,) = gla_project(h1, w1, wgi_q, wgk, bgk, [wgo_q])
    wgo = wgo_q.reshape(D, D)
    o, states, scores = gla_forward(qk, v, cum)

    dh2, do, dgate, g_gla_out, small_top = head_and_loss(o, gate, h1, target, hw_tiled, wgo, wf)
    dq, dk, dv, dcum = gla_backward(qk, v, cum, do, states, scores)
    dh1, dproj, g_gk_pad, small_gla = gla_project_backward(
        dq, dk, dv, dgate, dcum, low, h1, dh2, w1, wgi_q, wgk, bgk)
    g_gla_in, _ = matmul_tn(n1, dproj, "grad_gla_in", tile_n=(GLA_MAIN + RANK_PAD) // 5)

    def chip_sums(grads, tag):
        return add_halves(grads, place, "add_halves_" + tag)

    gla_sums = chip_sums([g_gla_in, g_gla_out.reshape(N_CHIPS, D // N_CHIPS, D)], "gla")
    (dx, dpool, g_pool_out, g_group_w, small_pool), gla_got = pool_backward(
        xs, dh1, pooled, gt, w0, wpi, gw, gb, scale, wpo, [b for _, b in gla_sums])
    mix_sums = chip_sums([_quarters_from_groups(g_group_w), g_pool_out.reshape(N_CHIPS, D // N_CHIPS, D)], "pool_mix")
    g_pool_in, mix_got = matmul_tn(n0, dpool, "grad_pool_in", tile_n=D // 2, by_column_tile=True,
                                   chip_sums=[b for _, b in mix_sums])

    in_sums = add_halves([g_pool_in], place, "add_halves_and_scatter_pool_in", scatter=True)
    reduced, total = join_halves(
        [f for f, _ in in_sums + mix_sums + gla_sums], [got for _, got in in_sums] + list(mix_got) + list(gla_got),
        small_pool, small_gla, small_top, g_gk_pad)
    return dx, reduced, total


def kernel(x, norm_w, pool_in_w, pool_group_w, pool_group_b, pool_scale, pool_out_w, gla_in_w, gla_gk_w, gla_gk_b, gla_head_norm_w, gla_out_w, final_norm_w, loss_target, m_norm_w, m_pool_in_w, m_pool_group_w, m_pool_group_b, m_pool_scale, m_pool_out_w, m_gla_in_w, m_gla_gk_w, m_gla_gk_b, m_gla_head_norm_w, m_gla_out_w, m_final_norm_w, v_norm_w, v_pool_in_w, v_pool_group_w, v_pool_group_b, v_pool_scale, v_pool_out_w, v_gla_in_w, v_gla_gk_w, v_gla_gk_b, v_gla_head_norm_w, v_gla_out_w, v_final_norm_w):
    xs = x[0]
    target = loss_target[0]
    q_chip = 2 * lax.axis_index("x") + lax.axis_index("y")
    place = jnp.stack([lax.axis_index("c"), q_chip]).astype(jnp.int32)

    (wpi, gw_q, wpo_q, wgi_q, wgo_q), small_all = allgather_weights(
        [pool_in_w[0], pool_group_w[0].reshape(GROUP_DIM, GROUP_DIM), pool_out_w[0], gla_in_w[0], gla_out_w[0]],
        exchange=(True, True, True, False, False),
        smalls=[gla_gk_b, gla_head_norm_w, pool_group_b[0], gla_gk_w[0]])
    gw = _groups_from_quarters(gw_q)
    wpo = wpo_q.reshape(D, D)
    small_all = small_all[0::2]
    bgk = small_all[:, 0, :].reshape(1, KEY_W)
    hw = small_all[:, 1, 0:64].reshape(1, HEAD_V)
    gb = small_all[:, 2:2 + GROUPS, 0:64].transpose(1, 0, 2).reshape(1, D)
    wgk16 = small_all[:, 8:8 + GATE_RANK, :].transpose(1, 0, 2).reshape(GATE_RANK, KEY_W)
    wgk = _bf(jnp.pad(wgk16, ((0, RANK_PAD - GATE_RANK), (0, 0))))
    hw_tiled = jnp.tile(hw, (1, HEADS))

    w0 = norm_w[0:1]
    w1 = norm_w[1:2]
    wf = final_norm_w.reshape(1, D)

    dx, reduced, total = local_gradients(
        xs, target, w0, w1, wf, wpi, gw, gb, pool_scale, wpo, [wgi_q, wgo_q], wgk, bgk, hw_tiled, place)
    r_pool_in, r_group_w, r_pool_out, r_gla_in, r_gla_out = reduced
    r_group_w = r_group_w.reshape(GROUPS, 64, GROUP_DIM)

    loss = total[7, 0]
    g_norm = jnp.stack([total[0], total[3]])
    g_scale = total[1:2]
    g_final = total[5]
    pick = lambda full, width: lax.dynamic_slice_in_dim(full, q_chip * width, width, axis=-1)
    g_gk_b = pick(total[4:5, 0:KEY_W], 128)
    g_hnw = pick(total[6:7, 0:HEAD_V], 64)
    g_group_b = pick(total[2].reshape(GROUPS, GROUP_DIM), 64)[None]
    g_gk_w = pick(total[8:16].reshape(GATE_RANK, KEY_W), 128)[None]

    turn = lambda a: jnp.transpose(a, (2, 0, 1))
    back = lambda a: jnp.transpose(a, (1, 2, 0))
    as2d = lambda a, w: a.reshape(-1, w.shape[-1])
    big_names = ("pool_in_w", "pool_group_w", "pool_out_w", "gla_in_w", "gla_out_w")
    big_args = [(pool_in_w, r_pool_in[None], m_pool_in_w, v_pool_in_w),
                (pool_group_w, r_group_w[None], m_pool_group_w, v_pool_group_w),
                (pool_out_w, r_pool_out[None], m_pool_out_w, v_pool_out_w),
                (gla_in_w, r_gla_in[None], m_gla_in_w, v_gla_in_w),
                (gla_out_w, r_gla_out[None], m_gla_out_w, v_gla_out_w)]
    to_kernel = lambda n, a, w: turn(a) if n == "gla_in_w" else as2d(a, w)
    from_kernel = lambda n, a, w: back(a) if n == "gla_in_w" else a.reshape(w.shape)
    big_in = [tuple(to_kernel(n, a, p[0]) for a in p) for n, p in zip(big_names, big_args)]
    big_out = adamw(big_in, "adamw")
    big = {n: (from_kernel(n, i[1], p[0]),) + tuple(from_kernel(n, o, p[0]) for o in out)
           for n, p, i, out in zip(big_names, big_args, big_in, big_out)}

    small_names = ("norm_w", "pool_group_b", "pool_scale", "gla_gk_w", "gla_gk_b", "gla_head_norm_w",
                   "final_norm_w")
    small_args = [(norm_w, g_norm, m_norm_w, v_norm_w),
                  (pool_group_b, g_group_b, m_pool_group_b, v_pool_group_b),
                  (pool_scale, g_scale, m_pool_scale, v_pool_scale),
                  (gla_gk_w, g_gk_w, m_gla_gk_w, v_gla_gk_w),
                  (gla_gk_b, g_gk_b, m_gla_gk_b, v_gla_gk_b),
                  (gla_head_norm_w, g_hnw, m_gla_head_norm_w, v_gla_head_norm_w),
                  (final_norm_w, g_final, m_final_norm_w, v_final_norm_w)]
    small_out = adamw_small([tuple(as2d(a, p[0]) for a in p) for p in small_args])
    small = {n: (p[1].reshape(p[0].shape),) + tuple(o.reshape(p[0].shape) for o in out)
             for n, p, out in zip(small_names, small_args, small_out)}
    results = [
        small["norm_w"],
        big["pool_in_w"],
        big["pool_group_w"],
        small["pool_group_b"],
        small["pool_scale"],
        big["pool_out_w"],
        big["gla_in_w"],
        small["gla_gk_w"],
        small["gla_gk_b"],
        small["gla_head_norm_w"],
        big["gla_out_w"],
        small["final_norm_w"],
    ]
    grads, deltas, new_m, new_v = zip(*results)
    return (loss, dx[None], *grads, *deltas, *new_m, *new_v)
```

```python
import jax
import jax.numpy as jnp
from jax import lax
from jax.experimental import pallas as pl
from jax.experimental.pallas import tpu as pltpu

F32 = jnp.float32
BF16 = jnp.bfloat16
MESH = pl.DeviceIdType.MESH

D = 1024
POOL_WINDOWS = (2, 4, 8, 16)
GROUPS = 4
GROUP_DIM = 256
HEADS = 4
HEAD_K = 128
HEAD_V = 256
KEY_W = 512
CHUNK = 64
GATE_RANK = 16
GATE_NORM = 16.0
GLA_IN = 3088
GLA_MAIN = 3072
RANK_PAD = 128
EPS = 1e-6
HALO = 32

ADAM_LR = 0.001
ADAM_B1 = 0.9
ADAM_B2 = 0.999
ADAM_EPS = 1e-08
ADAM_WD = 0.01
ADAM_STEP = 10

N_CHIPS = 4
N_DEV = 8
GLA_IN_QUARTER = GLA_IN // N_CHIPS

VMEM_LIMIT = 56 * 1024 * 1024


def _nn(a, b):
    return lax.dot_general(a, b, (((1,), (0,)), ((), ())), preferred_element_type=F32)


def _nt(a, b):
    return lax.dot_general(a, b, (((1,), (1,)), ((), ())), preferred_element_type=F32)


def _tn(a, b):
    return lax.dot_general(a, b, (((0,), (0,)), ((), ())), preferred_element_type=F32)


def _nn_exact(a, b):
    return lax.dot_general(a, b, (((1,), (0,)), ((), ())), preferred_element_type=F32,
                           precision=lax.Precision.HIGHEST)


def _bf(a):
    return a.astype(BF16)


def _params(*sem):
    return pltpu.CompilerParams(dimension_semantics=sem, vmem_limit_bytes=VMEM_LIMIT)


def _full(shape):
    return pl.BlockSpec(shape, lambda i: (0,) * len(shape))


def _position():
    return lax.axis_index("x"), lax.axis_index("y"), lax.axis_index("c")


def _gather_small(in_ref, all_ref, send_sems, recv_sems, local_sem):
    x, y, c = _position()
    me = 4 * x + 2 * y + c
    mine = pltpu.make_async_copy(in_ref, all_ref.at[me], local_sem)
    sends = []
    for k in range(N_DEV - 1):
        fx, fy, fc = (k + 1) >> 2 & 1, (k + 1) >> 1 & 1, (k + 1) & 1
        sends.append(pltpu.make_async_remote_copy(
            src_ref=in_ref, dst_ref=all_ref.at[me],
            send_sem=send_sems.at[k], recv_sem=recv_sems.at[k],
            device_id=(x ^ fx, y ^ fy, c ^ fc), device_id_type=MESH))

    def start():
        mine.start()
        for cp in sends:
            cp.start()

    def wait():
        for k in range(N_DEV - 1):
            fx, fy, fc = (k + 1) >> 2 & 1, (k + 1) >> 1 & 1, (k + 1) & 1
            src_dev = 4 * (x ^ fx) + 2 * (y ^ fy) + (c ^ fc)
            pltpu.make_async_remote_copy(
                src_ref=in_ref, dst_ref=all_ref.at[src_dev],
                send_sem=send_sems.at[k], recv_sem=recv_sems.at[k],
                device_id=(x, y, c), device_id_type=MESH).wait_recv()
        for cp in sends:
            cp.wait_send()
        mine.wait()

    return start, wait


SMALL_SEMS = [pltpu.SemaphoreType.DMA((N_DEV - 1,)), pltpu.SemaphoreType.DMA((N_DEV - 1,)),
              pltpu.SemaphoreType.DMA]
VMEM_SPEC = pl.BlockSpec(memory_space=pltpu.VMEM)


def _other_chips(x, y):
    return [(1 - x, y), (x, 1 - y), (1 - x, 1 - y)]


def _any_specs(n):
    return [pl.BlockSpec(memory_space=pl.ANY)] * n


def _halves(rows, c):
    half = rows // 2
    return pl.ds(c * half, half), pl.ds((1 - c) * half, half)


CAST_ROWS = 256


def _gather_copy(out_ref, send_sems, recv_sems, k, quarter, half, to, src=None):
    dst = out_ref.at[quarter, half]
    return pltpu.make_async_remote_copy(
        src_ref=dst if src is None else src, dst_ref=dst,
        send_sem=send_sems.at[k], recv_sem=recv_sems.at[k], device_id=to, device_id_type=MESH)


SMALL_IN_ROWS = 24


def allgather_weights(quarters, exchange, smalls):
    n = len(quarters)
    shapes = [w.shape for w in quarters]
    moved = [i for i in range(n) if exchange[i]]

    def body(*refs):
        w_refs, (gkb_ref, hnw_ref, gb_ref, gkw_ref) = refs[:n], refs[n:n + 4]
        out_refs, small_all_ref = refs[n + 4:2 * n + 4], refs[2 * n + 4]
        refs = refs[2 * n + 5:]
        f32_bufs, bf_bufs = refs[:n], refs[n:2 * n]
        send_sems, recv_sems, local_sems, small_ref = refs[2 * n:2 * n + 4]
        small_ref[...] = jnp.zeros_like(small_ref)
        small_ref[0:1, :] = gkb_ref[...]
        small_ref[1:2, 0:64] = hnw_ref[...]
        small_ref[2:2 + GROUPS, 0:64] = gb_ref[...]
        small_ref[8:8 + GATE_RANK, :] = gkw_ref[...]
        start_small, wait_small = _gather_small(small_ref, small_all_ref, *refs[2 * n + 4:])
        start_small()
        x, y, c = _position()
        q = 2 * x + y
        sibling = (x, y, 1 - c)
        chips = _other_chips(x, y)

        def copy(k, i, quarter, half, to, src=None):
            return _gather_copy(out_refs[i], send_sems, recv_sems, k * n + i, quarter, half, to, src)

        loads = [pltpu.make_async_copy(w_refs[i], f32_bufs[i], local_sems.at[i]) for i in range(n)]
        for cp in loads:
            cp.start()
        keeps, sends = [], []
        for i in range(n):
            loads[i].wait()
            for r0 in range(0, shapes[i][0], CAST_ROWS):
                bf_bufs[i][r0:r0 + CAST_ROWS, :] = _bf(f32_bufs[i][r0:r0 + CAST_ROWS, :])
            keep = pltpu.make_async_copy(bf_bufs[i], out_refs[i].at[q], local_sems.at[n + i])
            keep.start()
            keeps.append(keep)
            if not exchange[i]:
                continue
            mine, _ = _halves(shapes[i][0], c)
            for j, chip in enumerate(chips):
                cp = copy(j, i, q, mine, (*chip, c), src=bf_bufs[i].at[mine])
                cp.start()
                sends.append(cp)
        for j, chip in enumerate(chips):
            qj = 2 * chip[0] + chip[1]
            for i in moved:
                mine, _ = _halves(shapes[i][0], c)
                copy(j, i, qj, mine, (x, y, c)).wait_recv()
                cp = copy(3 + j, i, qj, mine, sibling)
                cp.start()
                sends.append(cp)
        for j, chip in enumerate(chips):
            qj = 2 * chip[0] + chip[1]
            for i in moved:
                _, other = _halves(shapes[i][0], c)
                copy(3 + j, i, qj, other, (x, y, c)).wait_recv()
        wait_small()
        for cp in sends:
            cp.wait_send()
        for cp in keeps:
            cp.wait()

    outs = pl.pallas_call(
        body, name="allgather_weights",
        out_shape=[jax.ShapeDtypeStruct((N_CHIPS, *s), BF16) for s in shapes]
                  + [jax.ShapeDtypeStruct((N_DEV, SMALL_IN_ROWS, 128), F32)],
        in_specs=_any_specs(n) + [VMEM_SPEC] * 4, out_specs=_any_specs(n) + [VMEM_SPEC],
        scratch_shapes=([pltpu.VMEM(s, F32) for s in shapes] + [pltpu.VMEM(s, BF16) for s in shapes]
                        + [pltpu.SemaphoreType.DMA((6 * n,)), pltpu.SemaphoreType.DMA((6 * n,)),
                           pltpu.SemaphoreType.DMA((2 * n,)), pltpu.VMEM((SMALL_IN_ROWS, 128), F32)] + SMALL_SEMS),
        compiler_params=pltpu.CompilerParams(vmem_limit_bytes=VMEM_LIMIT),
    )(*quarters, *smalls)
    return outs[:n], outs[n]


def _scatter_copies(b_refs, got_refs, send_sems, recv_sems):
    n = len(b_refs)
    x, y, c = _position()
    copies = []
    for j, chip in enumerate(_other_chips(x, y)):
        qj = 2 * chip[0] + chip[1]
        for i in range(n):
            copies.append(pltpu.make_async_remote_copy(
                src_ref=b_refs[i].at[qj], dst_ref=got_refs[i].at[j],
                send_sem=send_sems.at[j * n + i], recv_sem=recv_sems.at[j * n + i],
                device_id=(*chip, c), device_id_type=MESH))
    return copies


def _scatter_shapes(chip_sums):
    return [jax.ShapeDtypeStruct((N_CHIPS - 1, *b.shape[1:]), BF16) for b in chip_sums]


ADD_ROWS = 512
ADD_HALVES_ROWS = 128
ADD_HALVES_AHEAD = 2


def _spans(counts):
    starts, total = [], 0
    for count in counts:
        starts.append(total)
        total += count
    return starts, total


def _local_step(t, start, count):
    return jnp.clip(t - start, 0, count - 1)


def add_halves(grads, place, name, scatter=None):
    n = len(grads)
    scatter = scatter or (False,) * n
    scattered = [i for i in range(n) if scatter[i]]
    whole = [len(g.shape) == 2 for g in grads]
    halves = [g.shape[-2] // 2 for g in grads]
    cols = [GLA_IN_QUARTER if w else g.shape[-1] for g, w in zip(grads, whole)]
    rbs = [min(ADD_HALVES_ROWS, h) for h in halves]
    counts = [h // rb for h, rb in zip(halves, rbs)]
    starts, total = _spans(counts)
    half_shapes = [(*g.shape[:-2], h, g.shape[-1]) for g, h in zip(grads, halves)]

    def rows_of(ref, i, start):
        return ref.at[pl.ds(start, rbs[i])] if whole[i] else ref.at[:, pl.ds(start, rbs[i])]

    def body(place_ref, *refs):
        a_refs, src_refs = refs[:n], refs[n:2 * n]
        f_refs, h_refs = refs[2 * n:3 * n], refs[3 * n:4 * n]
        their_refs, rest = refs[4 * n:5 * n], refs[5 * n:]
        send_sems, recv_sems = rest[:2]
        t = pl.program_id(0)
        q = place_ref[1]
        x, y, c = _position()
        sum_refs = {i: ref for i, ref in zip(scattered, rest[2:])}

        def to_owners(i, k):
            out_sems, in_sems = rest[2 + len(scattered):]
            rows = pl.ds(k * rbs[i], rbs[i])
            return [pltpu.make_async_remote_copy(
                src_ref=sum_refs[i].at[2 * chip[0] + chip[1], rows], dst_ref=h_refs[i].at[j, rows],
                send_sem=out_sems.at[3 * (starts[i] + k) + j], recv_sem=in_sems.at[3 * (starts[i] + k) + j],
                device_id=(*chip, c), device_id_type=MESH) for j, chip in enumerate(_other_chips(x, y))]

        copies = [[pltpu.make_async_remote_copy(
            src_ref=rows_of(src_refs[i], i, (1 - c) * halves[i] + k * rbs[i]),
            dst_ref=rows_of(their_refs[i], i, k * rbs[i]),
            send_sem=send_sems.at[starts[i] + k], recv_sem=recv_sems.at[starts[i] + k],
            device_id=(x, y, 1 - c), device_id_type=MESH) for k in range(counts[i])] for i in range(n)]

        in_order = [cp for of_matrix in copies for cp in of_matrix]

        @pl.when(t == 0)
        def _():
            for cp in in_order[:ADD_HALVES_AHEAD]:
                cp.start()

        for i in range(n):
            for k in range(counts[i]):
                @pl.when(t == starts[i] + k)
                def _(i=i, k=k):
                    copies[i][k].wait_recv()
                    for cp in in_order[starts[i] + k + ADD_HALVES_AHEAD:][:1]:
                        cp.start()
                    b_ref = rows_of(their_refs[i], i, k * rbs[i])
                    h_ref = sum_refs[i].at[:, pl.ds(k * rbs[i], rbs[i])] if scatter[i] else h_refs[i]
                    if not whole[i]:
                        h_ref[...] = _bf(a_refs[i][...] + b_ref[...])
                        f_refs[i][...] = a_refs[i][q] + b_ref[q]
                    else:
                        total_i = a_refs[i][...] + b_ref[...]
                        for k4 in range(N_CHIPS):
                            piece = total_i[:, k4 * cols[i]:(k4 + 1) * cols[i]]
                            h_ref[k4] = _bf(piece)

                            @pl.when(q == k4)
                            def _():
                                f_refs[i][...] = piece
                    if scatter[i]:
                        for cp in to_owners(i, k):
                            cp.start()

        @pl.when(t == total - 1)
        def _():
            for cp in in_order:
                cp.wait_send()
            for i in scattered:
                for k in range(counts[i]):
                    for cp in to_owners(i, k):
                        cp.wait()

    def specs(i):
        step = lambda t: _local_step(t, starts[i], counts[i])
        by_quarter = (N_CHIPS, rbs[i], cols[i])
        block = (rbs[i], grads[i].shape[-1]) if whole[i] else by_quarter
        lead = () if whole[i] else (0,)
        mine = pl.BlockSpec(block, lambda t, place: (*lead, place[0] * counts[i] + step(t), 0))
        sums = pl.BlockSpec(by_quarter, lambda t, place: (0, step(t), 0))
        own = pl.BlockSpec(by_quarter[1:], lambda t, place: (step(t), 0))
        return mine, own, sums

    all_specs = [specs(i) for i in range(n)]
    sum_shapes = [(N_CHIPS, h, cl) for h, cl in zip(halves, cols)]
    scratch = [pltpu.VMEM(sh, F32) for sh in half_shapes]
    scratch += [pltpu.SemaphoreType.DMA((total,)), pltpu.SemaphoreType.DMA((total,))]
    if scattered:
        scratch += [pltpu.VMEM(sum_shapes[i], BF16) for i in scattered]
        scratch += [pltpu.SemaphoreType.DMA((3 * total,)), pltpu.SemaphoreType.DMA((3 * total,))]
    outs = pl.pallas_call(
        body, name=name,
        grid_spec=pltpu.PrefetchScalarGridSpec(
            num_scalar_prefetch=1, grid=(total,),
            in_specs=[sp[0] for sp in all_specs] + _any_specs(n),
            out_specs=[sp[1] for sp in all_specs]
                      + [pl.BlockSpec(memory_space=pl.ANY) if s else sp[2] for s, sp in zip(scatter, all_specs)],
            scratch_shapes=scratch),
        out_shape=[jax.ShapeDtypeStruct((h, cl), F32) for h, cl in zip(halves, cols)]
                  + [jax.ShapeDtypeStruct((N_CHIPS - 1 if s else N_CHIPS, *sh[1:]), BF16)
                     for s, sh in zip(scatter, sum_shapes)],
        compiler_params=_params("arbitrary"),
    )(place, *grads, *grads)
    return list(zip(outs[:n], outs[n:]))


SMALL_SUM_ROWS = 16


def join_halves(owns, gots, small_pool, small_gla, small_top, g_gk_pad):
    n = len(owns)
    shapes = [g.shape for g in gots]
    rbs = [min(ADD_ROWS, sh[1]) for sh in shapes]
    counts = [sh[1] // rb for sh, rb in zip(shapes, rbs)]
    starts, total = _spans(counts)

    def body(*refs):
        o_refs, g_refs = refs[:n], refs[n:2 * n]
        pool_ref, gla_ref, top_ref, gk_ref = refs[2 * n:2 * n + 4]
        out_refs, total_ref = refs[2 * n + 4:3 * n + 4], refs[3 * n + 4]
        sum_refs = refs[3 * n + 5:4 * n + 5]
        local_sems, send_sems, recv_sems, all_ref, small_ref = refs[4 * n + 5:4 * n + 10]
        t = pl.program_id(0)
        x, y, c = _position()
        start_small, wait_small = _gather_small(small_ref, all_ref, *refs[4 * n + 10:])

        def copies(i, k):
            src = sum_refs[i].at[pl.ds(k * rbs[i], rbs[i])]
            rows = pl.ds(c * shapes[i][1] + k * rbs[i], rbs[i])
            return (pltpu.make_async_copy(src, out_refs[i].at[rows], local_sems.at[starts[i] + k]),
                    pltpu.make_async_remote_copy(
                        src_ref=src, dst_ref=out_refs[i].at[rows],
                        send_sem=send_sems.at[starts[i] + k], recv_sem=recv_sems.at[starts[i] + k],
                        device_id=(x, y, 1 - c), device_id_type=MESH))

        @pl.when(t == 0)
        def _():
            small_ref[0:3, :] = pool_ref[0:3, :]
            small_ref[3:5, :] = gla_ref[0:2, :]
            small_ref[5:8, :] = top_ref[0:3, :]
            for r in range(GATE_RANK):
                small_ref[8 + r // 2:9 + r // 2, (r % 2) * KEY_W:(r % 2 + 1) * KEY_W] = gk_ref[r:r + 1, :]
            start_small()

        for i in range(n):
            for k in range(counts[i]):
                @pl.when(t == starts[i] + k)
                def _(i=i, k=k):
                    total_i = o_refs[i][...]
                    for j in range(N_CHIPS - 1):
                        total_i = total_i + g_refs[i][j].astype(F32)
                    sum_refs[i][k * rbs[i]:(k + 1) * rbs[i], :] = total_i
                    for cp in copies(i, k):
                        cp.start()

        @pl.when(t == total - 1)
        def _():
            wait_small()
            small_total = all_ref[0]
            for dev in range(1, N_DEV):
                small_total = small_total + all_ref[dev]
            total_ref[...] = small_total
            for i in range(n):
                for k in range(counts[i]):
                    for cp in copies(i, k):
                        cp.wait()

    def specs(i):
        rb, cols = rbs[i], shapes[i][2]
        step = lambda t: _local_step(t, starts[i], counts[i])
        return (pl.BlockSpec((rb, cols), lambda t: (step(t), 0)),
                pl.BlockSpec((N_CHIPS - 1, rb, cols), lambda t: (0, step(t), 0)))

    all_specs = [specs(i) for i in range(n)]
    outs = pl.pallas_call(
        body, name="join_halves", grid=(total,),
        out_shape=[jax.ShapeDtypeStruct((2 * sh[1], sh[2]), F32) for sh in shapes]
                  + [jax.ShapeDtypeStruct((SMALL_SUM_ROWS, D), F32)],
        in_specs=[sp[0] for sp in all_specs] + [sp[1] for sp in all_specs] + [VMEM_SPEC] * 4,
        out_specs=_any_specs(n) + [VMEM_SPEC],
        scratch_shapes=[pltpu.VMEM(sh[1:], F32) for sh in shapes]
                       + [pltpu.SemaphoreType.DMA((total,)), pltpu.SemaphoreType.DMA((total,)),
                          pltpu.SemaphoreType.DMA((total,)),
                          pltpu.VMEM((N_DEV, SMALL_SUM_ROWS, D), F32), pltpu.VMEM((SMALL_SUM_ROWS, D), F32)]
                       + SMALL_SEMS,
        compiler_params=_params("arbitrary"),
    )(*owns, *gots, small_pool, small_gla, small_top, g_gk_pad)
    return outs[:n], outs[n]


def _adam_math(w, g, m, v):
    m = ADAM_B1 * m + (1.0 - ADAM_B1) * g
    v = ADAM_B2 * v + (1.0 - ADAM_B2) * (g * g)
    m_hat = m / (1.0 - ADAM_B1 ** ADAM_STEP)
    v_hat = v / (1.0 - ADAM_B2 ** ADAM_STEP)
    delta = -ADAM_LR * (m_hat / (jnp.sqrt(v_hat) + ADAM_EPS) + ADAM_WD * w)
    return delta, m, v


ADAM_BLOCK_BYTES = 2 ** 19
ADAM_MOST_STEPS = 8


def adamw(params, name):
    n = len(params)
    shapes = [p[0].shape for p in params]

    def tile_rows(shape):
        rows, cols = shape[0], shape[-1]
        aligned = 1 if len(shape) == 3 else 8
        divisors = [t for t in range(aligned, rows + 1, aligned) if rows % t == 0]
        tile = max(t for t in divisors if t * cols * 4 <= ADAM_BLOCK_BYTES)
        if rows // tile > ADAM_MOST_STEPS:
            tile = min(t for t in divisors if rows // t <= ADAM_MOST_STEPS)
        return tile

    tiles = [tile_rows(sh) for sh in shapes]
    counts = [sh[0] // tl for sh, tl in zip(shapes, tiles)]
    starts, total = _spans(counts)

    def body(*refs):
        ins, outs = refs[:4 * n], refs[4 * n:]
        t = pl.program_id(0)
        for i in range(n):
            @pl.when((t >= starts[i]) & (t < starts[i] + counts[i]))
            def _(i=i):
                w_ref, g_ref, m_ref, v_ref = ins[4 * i:4 * i + 4]
                d, nm, nv = _adam_math(w_ref[...], g_ref[...], m_ref[...], v_ref[...])
                outs[3 * i][...] = d
                outs[3 * i + 1][...] = nm
                outs[3 * i + 2][...] = nv

    def spec(i):
        block = (tiles[i],) + shapes[i][1:]
        zeros = (0,) * (len(block) - 1)
        return pl.BlockSpec(block, lambda t: (_local_step(t, starts[i], counts[i]),) + zeros)

    outs = pl.pallas_call(
        body, name=name, grid=(total,),
        out_shape=[jax.ShapeDtypeStruct(sh, F32) for sh in shapes for _ in range(3)],
        in_specs=[spec(i) for i in range(n) for _ in range(4)],
        out_specs=[spec(i) for i in range(n) for _ in range(3)],
        compiler_params=_params("arbitrary"),
    )(*[a for p in params for a in p])
    return [tuple(outs[3 * i:3 * i + 3]) for i in range(n)]


def adamw_small(params):
    n = len(params)

    def body(*refs):
        ins, outs = refs[:4 * n], refs[4 * n:]
        for k in range(n):
            w_ref, g_ref, m_ref, v_ref = ins[4 * k:4 * k + 4]
            d, nm, nv = _adam_math(w_ref[...], g_ref[...], m_ref[...], v_ref[...])
            outs[3 * k][...] = d
            outs[3 * k + 1][...] = nm
            outs[3 * k + 2][...] = nv

    flat = [a for p in params for a in p]
    outs = pl.pallas_call(
        body, name="adamw_small",
        out_shape=[jax.ShapeDtypeStruct(p[0].shape, F32) for p in params for _ in range(3)],
        in_specs=[VMEM_SPEC] * (4 * n), out_specs=[VMEM_SPEC] * (3 * n),
    )(*flat)
    return [tuple(outs[3 * k:3 * k + 3]) for k in range(n)]


def matmul_tn(a, b, name, tile_n, by_column_tile=False, chip_sums=()):
    s, m = a.shape
    n = b.shape[1]
    n_sums = len(chip_sums)
    steps = n // tile_n
    if by_column_tile:
        out_shape = jax.ShapeDtypeStruct((steps, m, tile_n), F32)
        out_spec = pl.BlockSpec((None, m, tile_n), lambda j: (j, 0, 0))
    else:
        out_shape = jax.ShapeDtypeStruct((m, n), F32)
        out_spec = pl.BlockSpec((m, tile_n), lambda j: (0, j))

    def body(a_ref, b_ref, *rest):
        sum_refs, out_ref, got_refs = rest[:n_sums], rest[n_sums], rest[n_sums + 1:2 * n_sums + 1]
        j = pl.program_id(0)
        copies = _scatter_copies(sum_refs, got_refs, *rest[2 * n_sums + 1:]) if n_sums else []

        @pl.when(j == 0)
        def _():
            for cp in copies:
                cp.start()

        out_ref[...] = _tn(a_ref[...], b_ref[...])

        @pl.when(j == steps - 1)
        def _():
            for cp in copies:
                cp.wait()

    outs = pl.pallas_call(
        body, name=name, grid=(steps,),
        out_shape=[out_shape] + _scatter_shapes(chip_sums),
        in_specs=[_full((s, m)), pl.BlockSpec((s, tile_n), lambda j: (0, j))] + _any_specs(n_sums),
        out_specs=[out_spec] + _any_specs(n_sums),
        scratch_shapes=[pltpu.SemaphoreType.DMA((3 * n_sums,)), pltpu.SemaphoreType.DMA((3 * n_sums,))]
                       if n_sums else [],
        compiler_params=_params("arbitrary"),
    )(a, b, *chip_sums)
    return outs[0], outs[1:]


ROW_TILE = 512


def _row_index(tile, rows):
    return tile * rows + lax.broadcasted_iota(jnp.int32, (rows, 1), 0)


def _inverse_counts(t_glob):
    return [1.0 / jnp.minimum(t_glob + 1, w).astype(F32) for w in POOL_WINDOWS]


def _sigmoid(z):
    return 1.0 / (1.0 + jnp.exp(-z))


def _trailing_sums(src, tmp, cols, window, rows):
    bufs = (src, tmp)
    span, level, start = 1, 0, 0
    while span < window:
        start += 8
        a, b = bufs[level % 2], bufs[(level + 1) % 2]
        n = HALO + rows - start
        b[start:start + n, cols] = a[start:start + n, cols] + a[start - span:start - span + n, cols]
        span, level = 2 * span, level + 1
    return bufs[level % 2][HALO:HALO + rows, cols]


def _leading_sums(src, tmp, cols, window, rows):
    bufs = (src, tmp)
    span, level, n = 1, 0, rows + HALO
    while span < window:
        n -= 8
        a, b = bufs[level % 2], bufs[(level + 1) % 2]
        b[0:n, cols] = a[0:n, cols] + a[span:span + n, cols]
        span, level = 2 * span, level + 1
    return bufs[level % 2][0:rows, cols]


def gather_in_background(step, last, out_refs, send_sems, recv_sems, finish):
    n = len(out_refs)
    x, y, c = _position()
    q = 2 * x + y
    chips = _other_chips(x, y)

    def copy(k, i, quarter, half, to):
        return _gather_copy(out_refs[i], send_sems, recv_sems, k * n + i, quarter, half, to)

    if not finish:
        @pl.when(step == 0)
        def _():
            for i in range(n):
                mine, _ = _halves(out_refs[i].shape[1], c)
                for j, chip in enumerate(chips):
                    copy(j, i, q, mine, (*chip, c)).start()

        @pl.when(step == last)
        def _():
            for j, chip in enumerate(chips):
                qj = 2 * chip[0] + chip[1]
                for i in range(n):
                    mine, _ = _halves(out_refs[i].shape[1], c)
                    copy(j, i, qj, mine, (x, y, c)).wait_recv()
                    copy(3 + j, i, qj, mine, (x, y, 1 - c)).start()
        return

    @pl.when(step == last)
    def _():
        for j, chip in enumerate(chips):
            qj = 2 * chip[0] + chip[1]
            for i in range(n):
                mine, other = _halves(out_refs[i].shape[1], c)
                copy(3 + j, i, qj, other, (x, y, c)).wait_recv()
                copy(j, i, q, mine, (x, y, c)).wait_send()
                copy(3 + j, i, qj, mine, (x, y, c)).wait_send()


def pool_forward(x, w0, wpi, gw, gb, scale, wpo, later):
    s = x.shape[0]
    ts = ROW_TILE
    nt = s // ts
    assert nt >= 2
    n_later = len(later)

    def body(x_ref, w0_ref, wpi_ref, gw_ref, gb_ref, sc_ref, wpo_ref, *rest):
        rest = rest[n_later:]
        h1_ref, pooled_ref, gt_ref, n0_ref = rest[:4]
        later_refs = rest[4:4 + n_later]
        ubuf, tbuf, hist, send_sems, recv_sems = rest[4 + n_later:]
        i = pl.program_id(0)
        gather_in_background(i, nt - 1, later_refs, send_sems, recv_sems, finish=False)
        xv = x_ref[...]
        r = lax.rsqrt(jnp.mean(xv * xv, axis=-1, keepdims=True) + EPS)
        n0 = _bf(xv * r * w0_ref[...])
        n0_ref[...] = n0
        u = jnp.concatenate([_nn(n0, wpi_ref[0]), _nn(n0, wpi_ref[1])], axis=-1)
        gt = jnp.concatenate([_nn(n0, wpi_ref[2]), _nn(n0, wpi_ref[3])], axis=-1)
        gt_ref[...] = gt

        @pl.when(i == 0)
        def _():
            hist[...] = jnp.zeros_like(hist)

        ubuf[0:HALO, :] = hist[...]
        ubuf[HALO:HALO + ts, :] = u
        hist[...] = u[ts - HALO:, :]
        inv = _inverse_counts(_row_index(i, ts))
        mixed = []
        for g, w in enumerate(POOL_WINDOWS):
            cols = slice(g * GROUP_DIM, (g + 1) * GROUP_DIM)
            pooled = _bf(_trailing_sums(ubuf, tbuf, cols, w, ts) * inv[g] - u[:, cols])
            pooled_ref[:, cols] = pooled
            mixed.append(_nn(pooled, gw_ref[g]))
        mixed = jnp.concatenate(mixed, axis=-1) + gb_ref[...]
        y = mixed * sc_ref[...] * (gt * _sigmoid(gt))
        h1_ref[...] = xv + _nn(_bf(y), wpo_ref[...])
        gather_in_background(i, nt - 1, later_refs, send_sems, recv_sems, finish=True)

    row = lambda cols: pl.BlockSpec((ts, cols), lambda i: (i, 0))
    outs = pl.pallas_call(
        body, name="pool_forward", grid=(nt,),
        out_shape=[jax.ShapeDtypeStruct((s, D), F32), jax.ShapeDtypeStruct((s, D), BF16),
                   jax.ShapeDtypeStruct((s, D), F32), jax.ShapeDtypeStruct((s, D), BF16)]
                  + [jax.ShapeDtypeStruct(a.shape, a.dtype) for a in later],
        in_specs=[row(D), _full((1, D)), _full((N_CHIPS, D, D // 2)), _full((GROUPS, GROUP_DIM, GROUP_DIM)),
                  _full((1, D)), _full((1, D)), _full((D, D))] + _any_specs(n_later),
        out_specs=[row(D), row(D), row(D), row(D)] + _any_specs(n_later),
        input_output_aliases={7 + k: 4 + k for k in range(n_later)},
        scratch_shapes=[pltpu.VMEM((HALO + ts, D), F32), pltpu.VMEM((HALO + ts, D), F32),
                        pltpu.VMEM((HALO, D), F32),
                        pltpu.SemaphoreType.DMA((6 * n_later,)), pltpu.SemaphoreType.DMA((6 * n_later,))],
        compiler_params=_params("arbitrary"),
    )(x, w0, wpi, gw, gb, scale, wpo, *later)
    return outs[:4], outs[4:]


def pool_backward(x, dh1, pooled, gt, w0, wpi, gw, gb, scale, wpo, chip_sums):
    s = x.shape[0]
    ts = ROW_TILE
    nt = s // ts
    n_sums = len(chip_sums)

    def body(x_ref, dh1_ref, pooled_ref, gt_ref, w0_ref, wpi_ref, gw_ref, gb_ref, sc_ref, wpo_ref, *rest):
        sum_refs, rest = rest[:n_sums], rest[n_sums:]
        dx_ref, dproj_ref, gpo_ref, ggw_ref, small_ref = rest[:5]
        got_refs = rest[5:5 + n_sums]
        ebuf, tbuf, ahead, send_sems, recv_sems = rest[5 + n_sums:]
        i = pl.program_id(0)
        copies = _scatter_copies(sum_refs, got_refs, send_sems, recv_sems)

        @pl.when(i == 0)
        def _():
            for cp in copies:
                cp.start()

        @pl.when(i == 0)
        def _():
            gpo_ref[...] = jnp.zeros_like(gpo_ref)
            ggw_ref[...] = jnp.zeros_like(ggw_ref)
            small_ref[...] = jnp.zeros_like(small_ref)
            ahead[...] = jnp.zeros_like(ahead)

        dh1 = dh1_ref[...]
        dh1_bf = _bf(dh1)
        gt = gt_ref[...]
        sc = sc_ref[...]
        dy = _nt(dh1_bf, wpo_ref[...])
        pooled_bf = []
        mixed = []
        for g in range(GROUPS):
            cols = slice(g * GROUP_DIM, (g + 1) * GROUP_DIM)
            pb = pooled_ref[:, cols]
            pooled_bf.append(pb)
            mixed.append(_nn(pb, gw_ref[g]))
        mixed = jnp.concatenate(mixed, axis=-1) + gb_ref[...]
        sg = _sigmoid(gt)
        silu = gt * sg
        gpo_ref[...] += _tn(_bf(mixed * sc * silu), dh1_bf)
        dmixed = dy * sc * silu
        dgt = dy * mixed * sc * (sg * (1.0 + gt * (1.0 - sg)))
        dproj_ref[:, D:] = _bf(dgt)
        small_ref[1:2, :] += jnp.sum(dy * mixed * silu, axis=0, keepdims=True)
        small_ref[2:3, :] += jnp.sum(dmixed, axis=0, keepdims=True)

        inv = _inverse_counts(_row_index(nt - 1 - i, ts))
        ebuf[ts:ts + HALO, :] = ahead[...]
        dpooled = []
        for g in range(GROUPS):
            cols = slice(g * GROUP_DIM, (g + 1) * GROUP_DIM)
            dm = _bf(dmixed[:, cols])
            ggw_ref[g] += _tn(pooled_bf[g], dm)
            dp = _nt(dm, gw_ref[g])
            dpooled.append(dp)
            ebuf[0:ts, cols] = dp * inv[g]
        ahead[...] = ebuf[0:HALO, :]
        du = []
        for g, w in enumerate(POOL_WINDOWS):
            cols = slice(g * GROUP_DIM, (g + 1) * GROUP_DIM)
            du.append(_leading_sums(ebuf, tbuf, cols, w, ts) - dpooled[g])
        du = _bf(jnp.concatenate(du, axis=-1))
        dproj_ref[:, :D] = du
        dgt_bf = _bf(dgt)
        half = D // 2
        dn0 = (_nt(du[:, :half], wpi_ref[0]) + _nt(du[:, half:], wpi_ref[1])
               + _nt(dgt_bf[:, :half], wpi_ref[2]) + _nt(dgt_bf[:, half:], wpi_ref[3]))

        xv = x_ref[...]
        r = lax.rsqrt(jnp.mean(xv * xv, axis=-1, keepdims=True) + EPS)
        xhat = xv * r
        small_ref[0:1, :] += jnp.sum(dn0 * xhat, axis=0, keepdims=True)
        dxh = dn0 * w0_ref[...]
        dx_ref[...] = dh1 + r * (dxh - xhat * jnp.mean(dxh * xhat, axis=-1, keepdims=True))

        @pl.when(i == nt - 1)
        def _():
            for cp in copies:
                cp.wait()

    row = lambda cols: pl.BlockSpec((ts, cols), lambda i: (nt - 1 - i, 0))
    outs = pl.pallas_call(
        body, name="pool_backward", grid=(nt,),
        out_shape=[jax.ShapeDtypeStruct((s, D), F32), jax.ShapeDtypeStruct((s, 2 * D), BF16),
                   jax.ShapeDtypeStruct((D, D), F32),
                   jax.ShapeDtypeStruct((GROUPS, GROUP_DIM, GROUP_DIM), F32),
                   jax.ShapeDtypeStruct((8, D), F32)] + _scatter_shapes(chip_sums),
        in_specs=[row(D), row(D), row(D), row(D), _full((1, D)), _full((N_CHIPS, D, D // 2)),
                  _full((GROUPS, GROUP_DIM, GROUP_DIM)), _full((1, D)), _full((1, D)), _full((D, D))]
                 + _any_specs(n_sums),
        out_specs=[row(D), row(2 * D), _full((D, D)), _full((GROUPS, GROUP_DIM, GROUP_DIM)), _full((8, D))]
                  + _any_specs(n_sums),
        scratch_shapes=[pltpu.VMEM((ts + HALO, D), F32), pltpu.VMEM((ts + HALO, D), F32),
                        pltpu.VMEM((HALO, D), F32),
                        pltpu.SemaphoreType.DMA((3 * n_sums,)), pltpu.SemaphoreType.DMA((3 * n_sums,))],
        compiler_params=_params("arbitrary"),
    )(x, dh1, pooled, gt, w0, wpi, gw, gb, scale, wpo, *chip_sums)
    return outs[:5], outs[5:]


def gla_project(h1, w1, wgi_q, wgk, bgk, later):
    s = h1.shape[0]
    ts = ROW_TILE
    nt = s // ts
    assert nt >= 2
    n_later = len(later)

    def body(h_ref, w1_ref, wq_ref, wgk_ref, bgk_ref, *rest):
        rest = rest[n_later:]
        qk_ref, v_ref, gate_ref, low_ref, cum_ref, n1_ref = rest[:6]
        later_refs = rest[6:6 + n_later]
        send_sems, recv_sems, wgi_ref = rest[6 + n_later:]
        gather_in_background(pl.program_id(0), nt - 1, later_refs, send_sems, recv_sems, finish=False)

        @pl.when(pl.program_id(0) == 0)
        def _():
            _assemble_gla_in(wq_ref, wgi_ref)

        hv = h_ref[...]
        r = lax.rsqrt(jnp.mean(hv * hv, axis=-1, keepdims=True) + EPS)
        n1 = _bf(hv * r * w1_ref[...])
        n1_ref[...] = n1
        qk_ref[...] = _nn(n1, wgi_ref[:, 0:2 * KEY_W])
        v_ref[...] = _bf(_nn(n1, wgi_ref[:, 2 * KEY_W:2 * KEY_W + D]))
        gate_ref[...] = _nn(n1, wgi_ref[:, 2 * KEY_W + D:GLA_MAIN])
        low = _bf(_nn(n1, wgi_ref[:, GLA_MAIN:]))
        low_ref[...] = low
        z = _nn(low, wgk_ref[...]) + bgk_ref[...]
        lg = (jnp.minimum(z, 0.0) - jnp.log(1.0 + jnp.exp(-jnp.abs(z)))) / GATE_NORM
        lower_f = _chunk_masks()[0].astype(F32)
        for r0 in range(0, ts, CHUNK):
            cum_ref[r0:r0 + CHUNK, :] = _nn_exact(lower_f, lg[r0:r0 + CHUNK, :])
        gather_in_background(pl.program_id(0), nt - 1, later_refs, send_sems, recv_sems, finish=True)

    row = lambda cols: pl.BlockSpec((ts, cols), lambda i: (i, 0))
    outs = pl.pallas_call(
        body, name="gla_project", grid=(nt,),
        out_shape=[jax.ShapeDtypeStruct((s, D), F32), jax.ShapeDtypeStruct((s, D), BF16),
                   jax.ShapeDtypeStruct((s, D), F32), jax.ShapeDtypeStruct((s, RANK_PAD), BF16),
                   jax.ShapeDtypeStruct((s, KEY_W), F32), jax.ShapeDtypeStruct((s, D), BF16)]
                  + [jax.ShapeDtypeStruct(a.shape, a.dtype) for a in later],
        in_specs=[row(D), _full((1, D)), _full((N_CHIPS, D, GLA_IN_QUARTER)),
                  _full((RANK_PAD, KEY_W)), _full((1, KEY_W))] + _any_specs(n_later),
        out_specs=[row(D), row(D), row(D), row(RANK_PAD), row(KEY_W), row(D)] + _any_specs(n_later),
        input_output_aliases={5 + k: 6 + k for k in range(n_later)},
        scratch_shapes=[pltpu.SemaphoreType.DMA((6 * n_later,)), pltpu.SemaphoreType.DMA((6 * n_later,)),
                        pltpu.VMEM((D, GLA_MAIN + RANK_PAD), BF16)],
        compiler_params=_params("arbitrary"),
    )(h1, w1, wgi_q, wgk, bgk, *later)
    return outs[:6], outs[6:]


def _assemble_gla_in(wq_ref, wfull):
    pad = jnp.zeros((CAST_ROWS, GLA_MAIN + RANK_PAD - GLA_IN), BF16)
    for r0 in range(0, D, CAST_ROWS):
        rows = slice(r0, r0 + CAST_ROWS)
        wfull[rows, :] = jnp.concatenate([wq_ref[q, rows, :] for q in range(N_CHIPS)] + [pad], axis=1)


GLA_BLOCK = 512
CHUNKS_PER_BLOCK = GLA_BLOCK // CHUNK


def _chunk_masks():
    t = lax.broadcasted_iota(jnp.int32, (CHUNK, CHUNK), 0)
    u = lax.broadcasted_iota(jnp.int32, (CHUNK, CHUNK), 1)
    return t >= u, t <= u


def _gla_chunk_terms(q, cum):
    ep = jnp.exp(cum)
    en = jnp.exp(-cum)
    qs = q * (HEAD_K ** -0.5)
    last = cum[CHUNK - 1:CHUNK, :]
    ed = jnp.exp(last - cum)
    dec = jnp.exp(last)
    return ep, en, qs, ed, dec


def gla_forward(qk, v, cum):
    s = qk.shape[0]
    nb = s // GLA_BLOCK
    nc = s // CHUNK

    def body(q_ref, k_ref, v_ref, cum_ref, o_ref, st_ref, sc_ref, state):
        @pl.when(pl.program_id(0) == 0)
        def _():
            state[...] = jnp.zeros_like(state)

        lower, _ = _chunk_masks()

        def chunk(cc, carry):
            rows = pl.ds(pl.multiple_of(cc * CHUNK, CHUNK), CHUNK)
            for h in range(HEADS):
                kc = slice(h * HEAD_K, (h + 1) * HEAD_K)
                vc = slice(h * HEAD_V, (h + 1) * HEAD_V)
                q = q_ref[rows, kc]
                k = k_ref[rows, kc]
                v = v_ref[rows, vc]
                ep, en, qs, ed, dec = _gla_chunk_terms(q, cum_ref[rows, kc])
                a = _bf(qs * ep)
                fwd = _nt(a, _bf(k * en))
                bwd = _nt(_bf(qs * en), _bf(k * ep))
                scores = _bf(jnp.where(lower, fwd, bwd))
                sc_ref[rows, h * CHUNK:(h + 1) * CHUNK] = scores
                st = state[h]
                st_ref[cc, h] = st
                o_ref[rows, vc] = _nn(scores, v) + _nt(a, _bf(st))
                state[h] = st * dec + _tn(v, _bf(k * ed))
            return carry

        lax.fori_loop(0, CHUNKS_PER_BLOCK, chunk, 0, unroll=4)

    return pl.pallas_call(
        body, name="gla_forward", grid=(nb,),
        out_shape=(jax.ShapeDtypeStruct((s, D), F32),
                   jax.ShapeDtypeStruct((nc, HEADS, HEAD_V, HEAD_K), F32),
                   jax.ShapeDtypeStruct((s, HEADS * CHUNK), BF16)),
        in_specs=[pl.BlockSpec((GLA_BLOCK, KEY_W), lambda i: (i, 0)),
                  pl.BlockSpec((GLA_BLOCK, KEY_W), lambda i: (i, 1)),
                  pl.BlockSpec((GLA_BLOCK, D), lambda i: (i, 0)),
                  pl.BlockSpec((GLA_BLOCK, KEY_W), lambda i: (i, 0))],
        out_specs=(pl.BlockSpec((GLA_BLOCK, D), lambda i: (i, 0)),
                   pl.BlockSpec((CHUNKS_PER_BLOCK, HEADS, HEAD_V, HEAD_K), lambda i: (i, 0, 0, 0)),
                   pl.BlockSpec((GLA_BLOCK, HEADS * CHUNK), lambda i: (i, 0))),
        scratch_shapes=[pltpu.VMEM((HEADS, HEAD_V, HEAD_K), F32)],
        compiler_params=_params("arbitrary"),
    )(qk, qk, v, cum)


def gla_backward(qk, v, cum, do, states, scores):
    s = qk.shape[0]
    nb = s // GLA_BLOCK

    def body(q_ref, k_ref, v_ref, cum_ref, do_ref, st_ref, sc_ref, dq_ref, dk_ref, dv_ref, dcum_ref, dstate):
        @pl.when(pl.program_id(0) == 0)
        def _():
            dstate[...] = jnp.zeros_like(dstate)

        lower, _ = _chunk_masks()
        is_last = lax.broadcasted_iota(jnp.int32, (CHUNK, HEAD_K), 0) == CHUNK - 1

        def chunk(step, carry):
            cc = CHUNKS_PER_BLOCK - 1 - step
            rows = pl.ds(pl.multiple_of(cc * CHUNK, CHUNK), CHUNK)
            for h in range(HEADS):
                kc = slice(h * HEAD_K, (h + 1) * HEAD_K)
                vc = slice(h * HEAD_V, (h + 1) * HEAD_V)
                q = q_ref[rows, kc]
                k = k_ref[rows, kc]
                v = v_ref[rows, vc]
                do_c = do_ref[rows, vc]
                ep, en, qs, ed, dec = _gla_chunk_terms(q, cum_ref[rows, kc])
                a = _bf(qs * ep)
                b = _bf(k * en)
                c = _bf(qs * en)
                dk_dec = _bf(k * ep)
                kd = _bf(k * ed)
                scores = sc_ref[rows, h * CHUNK:(h + 1) * CHUNK]
                st = st_ref[cc, h]
                dst = dstate[h]
                dst_bf = _bf(dst)

                dscores = _nt(do_c, v)
                dfwd = _bf(jnp.where(lower, dscores, 0.0))
                dbwd = _bf(jnp.where(lower, 0.0, dscores))
                dv_ref[rows, vc] = _bf(_tn(scores, do_c) + _nt(kd, dst_bf))
                da = _nn(dfwd, b) + _nn(do_c, _bf(st))
                db = _tn(dfwd, a)
                dc = _nn(dbwd, dk_dec)
                ddk = _tn(dbwd, c)
                dkd = _nn(v, dst_bf)
                ddec = jnp.sum(dst * st, axis=0, keepdims=True)
                dstate[h] = dst * dec + _tn(do_c, a)

                m = dkd * k * ed
                dq_ref[rows, kc] = _bf((da * ep + dc * en) * (HEAD_K ** -0.5))
                dk_ref[rows, kc] = _bf(db * en + ddk * ep + dkd * ed)
                dcum = (da * qs + ddk * k) * ep - (db * k + dc * qs) * en - m
                dlast = jnp.sum(m, axis=0, keepdims=True) + ddec * dec
                dcum_ref[rows, kc] = dcum + jnp.where(is_last, dlast, 0.0)
            return carry

        lax.fori_loop(0, CHUNKS_PER_BLOCK, chunk, 0, unroll=4)

    rev = lambda cols, col_block: pl.BlockSpec((GLA_BLOCK, cols), lambda i: (nb - 1 - i, col_block))
    return pl.pallas_call(
        body, name="gla_backward", grid=(nb,),
        out_shape=(jax.ShapeDtypeStruct((s, KEY_W), BF16), jax.ShapeDtypeStruct((s, KEY_W), BF16),
                   jax.ShapeDtypeStruct((s, D), BF16), jax.ShapeDtypeStruct((s, KEY_W), F32)),
        in_specs=[rev(KEY_W, 0), rev(KEY_W, 1), rev(D, 0), rev(KEY_W, 0), rev(D, 0),
                  pl.BlockSpec((CHUNKS_PER_BLOCK, HEADS, HEAD_V, HEAD_K), lambda i: (nb - 1 - i, 0, 0, 0)),
                  rev(HEADS * CHUNK, 0)],
        out_specs=(rev(KEY_W, 0), rev(KEY_W, 0), rev(D, 0), rev(KEY_W, 0)),
        scratch_shapes=[pltpu.VMEM((HEADS, HEAD_V, HEAD_K), F32)],
        compiler_params=_params("arbitrary"),
    )(qk, qk, v, cum, do, states, scores)


def head_and_loss(o, gate, h1, target, hw, wgo, wf):
    s = o.shape[0]
    ts = ROW_TILE

    def body(o_ref, gate_ref, h1_ref, tgt_ref, hw_ref, wgo_ref, wf_ref,
             dh2_ref, do_ref, dgate_ref, ggo_ref, small_ref):
        @pl.when(pl.program_id(0) == 0)
        def _():
            ggo_ref[...] = jnp.zeros_like(ggo_ref)
            small_ref[...] = jnp.zeros_like(small_ref)

        gate = gate_ref[...]
        hw = hw_ref[...]
        sg = _sigmoid(gate)
        silu = gate * sg
        ohat, ro = [], []
        for h in range(HEADS):
            oh = o_ref[:, h * HEAD_V:(h + 1) * HEAD_V]
            rh = lax.rsqrt(jnp.mean(oh * oh, axis=-1, keepdims=True) + EPS)
            ro.append(rh)
            ohat.append(oh * rh)
        ohat = jnp.concatenate(ohat, axis=-1)
        on = ohat * hw
        y2 = _bf(on * silu)
        h2 = h1_ref[...] + _nn(y2, wgo_ref[...])
        rf = lax.rsqrt(jnp.mean(h2 * h2, axis=-1, keepdims=True) + EPS)
        h2hat = h2 * rf
        wf = wf_ref[...]
        diff = h2hat * wf - tgt_ref[...]
        small_ref[2:3, :] += jnp.zeros((1, D), F32) + 0.5 * jnp.sum(diff * diff) / D
        dout = diff / D
        small_ref[0:1, :] += jnp.sum(dout * h2hat, axis=0, keepdims=True)
        dxh = dout * wf
        dh2 = rf * (dxh - h2hat * jnp.mean(dxh * h2hat, axis=-1, keepdims=True))
        dh2_ref[...] = dh2
        dh2_bf = _bf(dh2)
        ggo_ref[...] += _tn(y2, dh2_bf)
        dy2 = _nt(dh2_bf, wgo_ref[...])
        don = dy2 * silu
        dgate_ref[...] = _bf(dy2 * on * (sg * (1.0 + gate * (1.0 - sg))))
        ghw = jnp.sum(don * ohat, axis=0, keepdims=True)
        small_ref[1:2, 0:HEAD_V] += sum(ghw[:, h * HEAD_V:(h + 1) * HEAD_V] for h in range(HEADS))
        dohat = don * hw
        for h in range(HEADS):
            cols = slice(h * HEAD_V, (h + 1) * HEAD_V)
            oh, dh = ohat[:, cols], dohat[:, cols]
            do_ref[:, cols] = _bf(ro[h] * (dh - oh * jnp.mean(dh * oh, axis=-1, keepdims=True)))

    row = lambda cols: pl.BlockSpec((ts, cols), lambda i: (i, 0))
    act = jax.ShapeDtypeStruct((s, D), F32)
    act_bf = jax.ShapeDtypeStruct((s, D), BF16)
    return pl.pallas_call(
        body, name="head_and_loss", grid=(s // ts,),
        out_shape=(act, act_bf, act_bf, jax.ShapeDtypeStruct((D, D), F32), jax.ShapeDtypeStruct((8, D), F32)),
        in_specs=[row(D), row(D), row(D), row(D),
                  _full((1, D)), _full((D, D)), _full((1, D))],
        out_specs=(row(D), row(D), row(D), _full((D, D)), _full((8, D))),
        compiler_params=_params("arbitrary"),
    )(o, gate, h1, target, hw, wgo, wf)


def gla_project_backward(dq, dk, dv, dgate, dcum, low, h1, dh2, w1, wgi_q, wgk, bgk):
    s = h1.shape[0]
    ts = ROW_TILE

    def body(dq_ref, dk_ref, dv_ref, dgate_ref, dcum_ref, low_ref, h1_ref, dh2_ref, w1_ref,
             wq_ref, wgk_ref, bgk_ref, dh1_ref, dproj_ref, ggk_ref, small_ref, wgi_ref):
        @pl.when(pl.program_id(0) == 0)
        def _():
            ggk_ref[...] = jnp.zeros_like(ggk_ref)
            small_ref[...] = jnp.zeros_like(small_ref)
            _assemble_gla_in(wq_ref, wgi_ref)

        low = low_ref[...]
        z = _nn(low, wgk_ref[...]) + bgk_ref[...]
        upper_f = _chunk_masks()[1].astype(F32)
        dlg = jnp.concatenate([_nn_exact(upper_f, dcum_ref[r0:r0 + CHUNK, :]) for r0 in range(0, ts, CHUNK)],
                              axis=0)
        dz = dlg * (1.0 / GATE_NORM) * _sigmoid(-z)
        dz_bf = _bf(dz)
        ggk_ref[...] += _tn(low, dz_bf)
        small_ref[1:2, 0:KEY_W] += jnp.sum(dz, axis=0, keepdims=True)
        dlow = _bf(_nt(dz_bf, wgk_ref[...]))
        dproj_ref[:, GLA_MAIN:] = dlow
        dn1 = _nt(dlow, wgi_ref[:, GLA_MAIN:])
        for ref, lo, hi in ((dq_ref, 0, KEY_W), (dk_ref, KEY_W, 2 * KEY_W),
                            (dv_ref, 2 * KEY_W, 2 * KEY_W + D), (dgate_ref, 2 * KEY_W + D, GLA_MAIN)):
            piece = ref[...]
            dproj_ref[:, lo:hi] = piece
            dn1 = dn1 + _nt(piece, wgi_ref[:, lo:hi])
        hv = h1_ref[...]
        r = lax.rsqrt(jnp.mean(hv * hv, axis=-1, keepdims=True) + EPS)
        hhat = hv * r
        small_ref[0:1, :] += jnp.sum(dn1 * hhat, axis=0, keepdims=True)
        dxh = dn1 * w1_ref[...]
        dh1_ref[...] = dh2_ref[...] + r * (dxh - hhat * jnp.mean(dxh * hhat, axis=-1, keepdims=True))

    row = lambda cols: pl.BlockSpec((ts, cols), lambda i: (i, 0))
    return pl.pallas_call(
        body, name="gla_project_backward", grid=(s // ts,),
        out_shape=(jax.ShapeDtypeStruct((s, D), F32), jax.ShapeDtypeStruct((s, GLA_MAIN + RANK_PAD), BF16),
                   jax.ShapeDtypeStruct((RANK_PAD, KEY_W), F32),
                   jax.ShapeDtypeStruct((8, D), F32)),
        in_specs=[row(KEY_W), row(KEY_W), row(D), row(D), row(KEY_W), row(RANK_PAD), row(D), row(D),
                  _full((1, D)), _full((N_CHIPS, D, GLA_IN_QUARTER)), _full((RANK_PAD, KEY_W)),
                  _full((1, KEY_W))],
        out_specs=(row(D), row(GLA_MAIN + RANK_PAD), _full((RANK_PAD, KEY_W)), _full((8, D))),
        scratch_shapes=[pltpu.VMEM((D, GLA_MAIN + RANK_PAD), BF16)],
        compiler_params=_params("arbitrary"),
    )(dq, dk, dv, dgate, dcum, low, h1, dh2, w1, wgi_q, wgk, bgk)


def _groups_from_quarters(a):
    return a.reshape(N_CHIPS, GROUPS, 64, GROUP_DIM).transpose(1, 0, 2, 3).reshape(GROUPS, GROUP_DIM, GROUP_DIM)


def _quarters_from_groups(a):
    return a.reshape(GROUPS, N_CHIPS, 64, GROUP_DIM).transpose(1, 0, 2, 3).reshape(N_CHIPS, GROUP_DIM, GROUP_DIM)


def local_gradients(xs, target, w0, w1, wf, wpi, gw, gb, scale, wpo, gla_quarters, wgk, bgk, hw_tiled, place):
    wgi_q, wgo_q = gla_quarters
    (h1, pooled, gt, n0), (wgi_q,) = pool_forward(xs, w0, wpi, gw, gb, scale, wpo, [wgi_q])
    (qk, v, gate, low, cum, n1), (wgo_q,) = gla_project(h1, w1, wgi_q, wgk, bgk, [wgo_q])
    wgo = wgo_q.reshape(D, D)
    o, states, scores = gla_forward(qk, v, cum)

    dh2, do, dgate, g_gla_out, small_top = head_and_loss(o, gate, h1, target, hw_tiled, wgo, wf)
    dq, dk, dv, dcum = gla_backward(qk, v, cum, do, states, scores)
    dh1, dproj, g_gk_pad, small_gla = gla_project_backward(
        dq, dk, dv, dgate, dcum, low, h1, dh2, w1, wgi_q, wgk, bgk)
    g_gla_in, _ = matmul_tn(n1, dproj, "grad_gla_in", tile_n=(GLA_MAIN + RANK_PAD) // 5)

    (own_gla_out, got_gla_out), (own_gla_in, sums_gla_in) = add_halves(
        [g_gla_out.reshape(N_CHIPS, D // N_CHIPS, D), g_gla_in], place, "add_halves_gla", scatter=(True, False))
    (dx, dpool, g_pool_out, g_group_w, small_pool), (got_gla_in,) = pool_backward(
        xs, dh1, pooled, gt, w0, wpi, gw, gb, scale, wpo, [sums_gla_in])
    (own_group, sums_group), (own_pool_out, sums_pool_out) = add_halves(
        [_quarters_from_groups(g_group_w), g_pool_out.reshape(N_CHIPS, D // N_CHIPS, D)], place, "add_halves_pool_mix")
    g_pool_in, (got_group, got_pool_out) = matmul_tn(
        n0, dpool, "grad_pool_in", tile_n=D // 2, by_column_tile=True,
        chip_sums=[sums_group, sums_pool_out])
    ((own_pool_in, got_pool_in),) = add_halves(
        [g_pool_in], place, "add_halves_and_scatter_pool_in", scatter=(True,))
    reduced, total = join_halves(
        [own_pool_in, own_group, own_pool_out, own_gla_in, own_gla_out],
        [got_pool_in, got_group, got_pool_out, got_gla_in, got_gla_out],
        small_pool, small_gla, small_top, g_gk_pad)
    return dx, reduced, total


def kernel(x, norm_w, pool_in_w, pool_group_w, pool_group_b, pool_scale, pool_out_w, gla_in_w, gla_gk_w, gla_gk_b, gla_head_norm_w, gla_out_w, final_norm_w, loss_target, m_norm_w, m_pool_in_w, m_pool_group_w, m_pool_group_b, m_pool_scale, m_pool_out_w, m_gla_in_w, m_gla_gk_w, m_gla_gk_b, m_gla_head_norm_w, m_gla_out_w, m_final_norm_w, v_norm_w, v_pool_in_w, v_pool_group_w, v_pool_group_b, v_pool_scale, v_pool_out_w, v_gla_in_w, v_gla_gk_w, v_gla_gk_b, v_gla_head_norm_w, v_gla_out_w, v_final_norm_w):
    xs = x[0]
    target = loss_target[0]
    q_chip = 2 * lax.axis_index("x") + lax.axis_index("y")
    place = jnp.stack([lax.axis_index("c"), q_chip]).astype(jnp.int32)

    (wpi, gw_q, wpo_q, wgi_q, wgo_q), small_all = allgather_weights(
        [pool_in_w[0], pool_group_w[0].reshape(GROUP_DIM, GROUP_DIM), pool_out_w[0], gla_in_w[0], gla_out_w[0]],
        exchange=(True, True, True, False, False),
        smalls=[gla_gk_b, gla_head_norm_w, pool_group_b[0], gla_gk_w[0]])
    gw = _groups_from_quarters(gw_q)
    wpo = wpo_q.reshape(D, D)
    small_all = small_all[0::2]
    bgk = small_all[:, 0, :].reshape(1, KEY_W)
    hw = small_all[:, 1, 0:64].reshape(1, HEAD_V)
    gb = small_all[:, 2:2 + GROUPS, 0:64].transpose(1, 0, 2).reshape(1, D)
    wgk16 = small_all[:, 8:8 + GATE_RANK, :].transpose(1, 0, 2).reshape(GATE_RANK, KEY_W)
    wgk = _bf(jnp.pad(wgk16, ((0, RANK_PAD - GATE_RANK), (0, 0))))
    hw_tiled = jnp.tile(hw, (1, HEADS))

    w0 = norm_w[0:1]
    w1 = norm_w[1:2]
    wf = final_norm_w.reshape(1, D)

    dx, reduced, total = local_gradients(
        xs, target, w0, w1, wf, wpi, gw, gb, pool_scale, wpo, [wgi_q, wgo_q], wgk, bgk, hw_tiled, place)
    r_pool_in, r_group_w, r_pool_out, r_gla_in, r_gla_out = reduced
    r_group_w = r_group_w.reshape(GROUPS, 64, GROUP_DIM)

    loss = total[7, 0]
    g_norm = jnp.stack([total[0], total[3]])
    g_scale = total[1:2]
    g_final = total[5]
    pick = lambda full, width: lax.dynamic_slice_in_dim(full, q_chip * width, width, axis=-1)
    g_gk_b = pick(total[4:5, 0:KEY_W], 128)
    g_hnw = pick(total[6:7, 0:HEAD_V], 64)
    g_group_b = pick(total[2].reshape(GROUPS, GROUP_DIM), 64)[None]
    g_gk_w = pick(total[8:16].reshape(GATE_RANK, KEY_W), 128)[None]

    turn = lambda a: jnp.transpose(a, (2, 0, 1))
    back = lambda a: jnp.transpose(a, (1, 2, 0))
    as2d = lambda a, w: a.reshape(-1, w.shape[-1])
    big_names = ("pool_in_w", "pool_group_w", "pool_out_w", "gla_in_w", "gla_out_w")
    big_args = [(pool_in_w, r_pool_in[None], m_pool_in_w, v_pool_in_w),
                (pool_group_w, r_group_w[None], m_pool_group_w, v_pool_group_w),
                (pool_out_w, r_pool_out[None], m_pool_out_w, v_pool_out_w),
                (gla_in_w, r_gla_in[None], m_gla_in_w, v_gla_in_w),
                (gla_out_w, r_gla_out[None], m_gla_out_w, v_gla_out_w)]
    to_kernel = lambda n, a, w: turn(a) if n == "gla_in_w" else as2d(a, w)
    from_kernel = lambda n, a, w: back(a) if n == "gla_in_w" else a.reshape(w.shape)
    big_in = [tuple(to_kernel(n, a, p[0]) for a in p) for n, p in zip(big_names, big_args)]
    big_out = adamw(big_in, "adamw")
    big = {n: (from_kernel(n, i[1], p[0]),) + tuple(from_kernel(n, o, p[0]) for o in out)
           for n, p, i, out in zip(big_names, big_args, big_in, big_out)}

    small_names = ("norm_w", "pool_group_b", "pool_scale", "gla_gk_w", "gla_gk_b", "gla_head_norm_w",
                   "final_norm_w")
    small_args = [(norm_w, g_norm, m_norm_w, v_norm_w),
                  (pool_group_b, g_group_b, m_pool_group_b, v_pool_group_b),
                  (pool_scale, g_scale, m_pool_scale, v_pool_scale),
                  (gla_gk_w, g_gk_w, m_gla_gk_w, v_gla_gk_w),
                  (gla_gk_b, g_gk_b, m_gla_gk_b, v_gla_gk_b),
                  (gla_head_norm_w, g_hnw, m_gla_head_norm_w, v_gla_head_norm_w),
                  (final_norm_w, g_final, m_final_norm_w, v_final_norm_w)]
    small_out = adamw_small([tuple(as2d(a, p[0]) for a in p) for p in small_args])
    small = {n: (p[1].reshape(p[0].shape),) + tuple(o.reshape(p[0].shape) for o in out)
             for n, p, out in zip(small_names, small_args, small_out)}
    results = [
        small["norm_w"],
        big["pool_in_w"],
        big["pool_group_w"],
        small["pool_group_b"],
        small["pool_scale"],
        big["pool_out_w"],
        big["gla_in_w"],
        small["gla_gk_w"],
        small["gla_gk_b"],
        small["gla_head_norm_w"],
        big["gla_out_w"],
        small["final_norm_w"],
    ]
    grads, deltas, new_m, new_v = zip(*results)
    return (loss, dx[None], *grads, *deltas, *new_m, *new_v)
```

```python
import jax
import jax.numpy as jnp
from jax import lax
from jax.experimental import pallas as pl
from jax.experimental.pallas import tpu as pltpu

F32 = jnp.float32
BF16 = jnp.bfloat16
MESH = pl.DeviceIdType.MESH

D = 1024
POOL_WINDOWS = (2, 4, 8, 16)
GROUPS = 4
GROUP_DIM = 256
HEADS = 4
HEAD_K = 128
HEAD_V = 256
KEY_W = 512
CHUNK = 64
GATE_RANK = 16
GATE_NORM = 16.0
GLA_IN = 3088
GLA_MAIN = 3072
RANK_PAD = 128
EPS = 1e-6
HALO = 32

ADAM_LR = 0.001
ADAM_B1 = 0.9
ADAM_B2 = 0.999
ADAM_EPS = 1e-08
ADAM_WD = 0.01
ADAM_STEP = 10

N_CHIPS = 4
N_DEV = 8
GLA_IN_QUARTER = GLA_IN // N_CHIPS

VMEM_LIMIT = 56 * 1024 * 1024


def _nn(a, b):
    return lax.dot_general(a, b, (((1,), (0,)), ((), ())), preferred_element_type=F32)


def _nt(a, b):
    return lax.dot_general(a, b, (((1,), (1,)), ((), ())), preferred_element_type=F32)


def _tn(a, b):
    return lax.dot_general(a, b, (((0,), (0,)), ((), ())), preferred_element_type=F32)


def _nn_exact(a, b):
    return lax.dot_general(a, b, (((1,), (0,)), ((), ())), preferred_element_type=F32,
                           precision=lax.Precision.HIGHEST)


def _bf(a):
    return a.astype(BF16)


def _params(*sem):
    return pltpu.CompilerParams(dimension_semantics=sem, vmem_limit_bytes=VMEM_LIMIT)


def _full(shape):
    return pl.BlockSpec(shape, lambda i: (0,) * len(shape))


def _position():
    return lax.axis_index("x"), lax.axis_index("y"), lax.axis_index("c")


def _gather_small(in_ref, all_ref, send_sems, recv_sems, local_sem):
    x, y, c = _position()
    me = 4 * x + 2 * y + c
    mine = pltpu.make_async_copy(in_ref, all_ref.at[me], local_sem)
    sends = []
    for k in range(N_DEV - 1):
        fx, fy, fc = (k + 1) >> 2 & 1, (k + 1) >> 1 & 1, (k + 1) & 1
        sends.append(pltpu.make_async_remote_copy(
            src_ref=in_ref, dst_ref=all_ref.at[me],
            send_sem=send_sems.at[k], recv_sem=recv_sems.at[k],
            device_id=(x ^ fx, y ^ fy, c ^ fc), device_id_type=MESH))

    def start():
        mine.start()
        for cp in sends:
            cp.start()

    def wait():
        for k in range(N_DEV - 1):
            fx, fy, fc = (k + 1) >> 2 & 1, (k + 1) >> 1 & 1, (k + 1) & 1
            src_dev = 4 * (x ^ fx) + 2 * (y ^ fy) + (c ^ fc)
            pltpu.make_async_remote_copy(
                src_ref=in_ref, dst_ref=all_ref.at[src_dev],
                send_sem=send_sems.at[k], recv_sem=recv_sems.at[k],
                device_id=(x, y, c), device_id_type=MESH).wait_recv()
        for cp in sends:
            cp.wait_send()
        mine.wait()

    return start, wait


SMALL_SEMS = [pltpu.SemaphoreType.DMA((N_DEV - 1,)), pltpu.SemaphoreType.DMA((N_DEV - 1,)),
              pltpu.SemaphoreType.DMA]
VMEM_SPEC = pl.BlockSpec(memory_space=pltpu.VMEM)


def _other_chips(x, y):
    return [(1 - x, y), (x, 1 - y), (1 - x, 1 - y)]


def _any_specs(n):
    return [pl.BlockSpec(memory_space=pl.ANY)] * n


def _halves(rows, c):
    half = rows // 2
    return pl.ds(c * half, half), pl.ds((1 - c) * half, half)


CAST_ROWS = 256


def _gather_copy(out_ref, send_sems, recv_sems, k, quarter, half, to, src=None):
    dst = out_ref.at[quarter, half]
    return pltpu.make_async_remote_copy(
        src_ref=dst if src is None else src, dst_ref=dst,
        send_sem=send_sems.at[k], recv_sem=recv_sems.at[k], device_id=to, device_id_type=MESH)


SMALL_IN_ROWS = 24


def allgather_weights(quarters, exchange, smalls):
    n = len(quarters)
    shapes = [w.shape for w in quarters]
    moved = [i for i in range(n) if exchange[i]]

    def body(*refs):
        w_refs, (gkb_ref, hnw_ref, gb_ref, gkw_ref) = refs[:n], refs[n:n + 4]
        out_refs, small_all_ref = refs[n + 4:2 * n + 4], refs[2 * n + 4]
        refs = refs[2 * n + 5:]
        f32_bufs, bf_bufs = refs[:n], refs[n:2 * n]
        send_sems, recv_sems, local_sems, small_ref = refs[2 * n:2 * n + 4]
        small_ref[...] = jnp.zeros_like(small_ref)
        small_ref[0:1, :] = gkb_ref[...]
        small_ref[1:2, 0:64] = hnw_ref[...]
        small_ref[2:2 + GROUPS, 0:64] = gb_ref[...]
        small_ref[8:8 + GATE_RANK, :] = gkw_ref[...]
        start_small, wait_small = _gather_small(small_ref, small_all_ref, *refs[2 * n + 4:])
        start_small()
        x, y, c = _position()
        q = 2 * x + y
        sibling = (x, y, 1 - c)
        chips = _other_chips(x, y)

        def copy(k, i, quarter, half, to, src=None):
            return _gather_copy(out_refs[i], send_sems, recv_sems, k * n + i, quarter, half, to, src)

        loads = [pltpu.make_async_copy(w_refs[i], f32_bufs[i], local_sems.at[i]) for i in range(n)]
        for cp in loads:
            cp.start()
        keeps, sends = [], []
        for i in range(n):
            loads[i].wait()
            for r0 in range(0, shapes[i][0], CAST_ROWS):
                bf_bufs[i][r0:r0 + CAST_ROWS, :] = _bf(f32_bufs[i][r0:r0 + CAST_ROWS, :])
            keep = pltpu.make_async_copy(bf_bufs[i], out_refs[i].at[q], local_sems.at[n + i])
            keep.start()
            keeps.append(keep)
            if not exchange[i]:
                continue
            mine, _ = _halves(shapes[i][0], c)
            for j, chip in enumerate(chips):
                cp = copy(j, i, q, mine, (*chip, c), src=bf_bufs[i].at[mine])
                cp.start()
                sends.append(cp)
        for j, chip in enumerate(chips):
            qj = 2 * chip[0] + chip[1]
            for i in moved:
                mine, _ = _halves(shapes[i][0], c)
                copy(j, i, qj, mine, (x, y, c)).wait_recv()
                cp = copy(3 + j, i, qj, mine, sibling)
                cp.start()
                sends.append(cp)
        for j, chip in enumerate(chips):
            qj = 2 * chip[0] + chip[1]
            for i in moved:
                _, other = _halves(shapes[i][0], c)
                copy(3 + j, i, qj, other, (x, y, c)).wait_recv()
        wait_small()
        for cp in sends:
            cp.wait_send()
        for cp in keeps:
            cp.wait()

    outs = pl.pallas_call(
        body, name="allgather_weights",
        out_shape=[jax.ShapeDtypeStruct((N_CHIPS, *s), BF16) for s in shapes]
                  + [jax.ShapeDtypeStruct((N_DEV, SMALL_IN_ROWS, 128), F32)],
        in_specs=_any_specs(n) + [VMEM_SPEC] * 4, out_specs=_any_specs(n) + [VMEM_SPEC],
        scratch_shapes=([pltpu.VMEM(s, F32) for s in shapes] + [pltpu.VMEM(s, BF16) for s in shapes]
                        + [pltpu.SemaphoreType.DMA((6 * n,)), pltpu.SemaphoreType.DMA((6 * n,)),
                           pltpu.SemaphoreType.DMA((2 * n,)), pltpu.VMEM((SMALL_IN_ROWS, 128), F32)] + SMALL_SEMS),
        compiler_params=pltpu.CompilerParams(vmem_limit_bytes=VMEM_LIMIT),
    )(*quarters, *smalls)
    return outs[:n], outs[n]


def _scatter_copies(b_refs, got_refs, send_sems, recv_sems):
    n = len(b_refs)
    x, y, c = _position()
    copies = []
    for j, chip in enumerate(_other_chips(x, y)):
        qj = 2 * chip[0] + chip[1]
        for i in range(n):
            copies.append(pltpu.make_async_remote_copy(
                src_ref=b_refs[i].at[qj], dst_ref=got_refs[i].at[j],
                send_sem=send_sems.at[j * n + i], recv_sem=recv_sems.at[j * n + i],
                device_id=(*chip, c), device_id_type=MESH))
    return copies


def _scatter_shapes(chip_sums):
    return [jax.ShapeDtypeStruct((N_CHIPS - 1, *b.shape[1:]), BF16) for b in chip_sums]


ADD_ROWS = 512
ADD_HALVES_ROWS = 128


def _spans(counts):
    starts, total = [], 0
    for count in counts:
        starts.append(total)
        total += count
    return starts, total


def _local_step(t, start, count):
    return jnp.clip(t - start, 0, count - 1)


def add_halves(grads, place, name, scatter=False):
    n = len(grads)
    whole = [len(g.shape) == 2 for g in grads]
    halves = [g.shape[-2] // 2 for g in grads]
    cols = [GLA_IN_QUARTER if w else g.shape[-1] for g, w in zip(grads, whole)]
    rbs = [min(ADD_HALVES_ROWS, h) for h in halves]
    counts = [h // rb for h, rb in zip(halves, rbs)]
    starts, total = _spans(counts)
    half_shapes = [(*g.shape[:-2], h, g.shape[-1]) for g, h in zip(grads, halves)]

    def rows_of(ref, i, start):
        return ref.at[pl.ds(start, rbs[i])] if whole[i] else ref.at[:, pl.ds(start, rbs[i])]

    def body(place_ref, *refs):
        a_refs, o_refs = refs[:n], refs[n:2 * n]
        f_refs, h_refs = refs[2 * n:3 * n], refs[3 * n:4 * n]
        send_refs, their_refs, rest = refs[4 * n:5 * n], refs[5 * n:6 * n], refs[6 * n:]
        send_sems, recv_sems = rest[:2]
        t = pl.program_id(0)
        q = place_ref[1]
        x, y, c = _position()
        sum_refs = rest[2:2 + n] if scatter else h_refs

        def to_owners(i, k):
            out_sems, in_sems = rest[2 + n:]
            rows = pl.ds(k * rbs[i], rbs[i])
            return [pltpu.make_async_remote_copy(
                src_ref=sum_refs[i].at[2 * chip[0] + chip[1], rows], dst_ref=h_refs[i].at[j, rows],
                send_sem=out_sems.at[3 * (starts[i] + k) + j], recv_sem=in_sems.at[3 * (starts[i] + k) + j],
                device_id=(*chip, c), device_id_type=MESH) for j, chip in enumerate(_other_chips(x, y))]

        copies = [[pltpu.make_async_remote_copy(
            src_ref=rows_of(send_refs[i], i, k * rbs[i]), dst_ref=rows_of(their_refs[i], i, k * rbs[i]),
            send_sem=send_sems.at[starts[i] + k], recv_sem=recv_sems.at[starts[i] + k],
            device_id=(x, y, 1 - c), device_id_type=MESH) for k in range(counts[i])] for i in range(n)]

        for i in range(n):
            for k in range(counts[i]):
                @pl.when(t == starts[i] + k)
                def _(i=i, k=k):
                    rows_of(send_refs[i], i, k * rbs[i])[...] = _bf(o_refs[i][...])
                    copies[i][k].start()

        for i in range(n):
            for k in range(counts[i]):
                @pl.when(t == starts[i] + k + 1)
                def _(i=i, k=k):
                    copies[i][k].wait_recv()
                    b_ref = rows_of(their_refs[i], i, k * rbs[i])
                    h_ref = sum_refs[i].at[:, pl.ds(k * rbs[i], rbs[i])] if scatter else h_refs[i]
                    if not whole[i]:
                        h_ref[...] = _bf(a_refs[i][...] + b_ref[...].astype(F32))
                        f_refs[i][...] = a_refs[i][q] + b_ref[q].astype(F32)
                    else:
                        total_i = a_refs[i][...] + b_ref[...].astype(F32)
                        for k4 in range(N_CHIPS):
                            piece = total_i[:, k4 * cols[i]:(k4 + 1) * cols[i]]
                            h_ref[k4] = _bf(piece)

                            @pl.when(q == k4)
                            def _():
                                f_refs[i][...] = piece
                    if scatter:
                        for cp in to_owners(i, k):
                            cp.start()

        @pl.when(t == total)
        def _():
            for of_matrix in copies:
                for cp in of_matrix:
                    cp.wait_send()
            if scatter:
                for i in range(n):
                    for k in range(counts[i]):
                        for cp in to_owners(i, k):
                            cp.wait()

    def specs(i):
        sent = lambda t: _local_step(t, starts[i], counts[i])
        added = lambda t: _local_step(t - 1, starts[i], counts[i])
        by_quarter = (N_CHIPS, rbs[i], cols[i])
        block = (rbs[i], grads[i].shape[-1]) if whole[i] else by_quarter
        lead = () if whole[i] else (0,)
        mine = pl.BlockSpec(block, lambda t, place: (*lead, place[0] * counts[i] + added(t), 0))
        other = pl.BlockSpec(block, lambda t, place: (*lead, (1 - place[0]) * counts[i] + sent(t), 0))
        sums = pl.BlockSpec(by_quarter, lambda t, place: (0, added(t), 0))
        own = pl.BlockSpec(by_quarter[1:], lambda t, place: (added(t), 0))
        return mine, other, own, sums

    all_specs = [specs(i) for i in range(n)]
    sum_shapes = [(N_CHIPS, h, cl) for h, cl in zip(halves, cols)]
    scratch = [pltpu.VMEM(sh, BF16) for sh in half_shapes] + [pltpu.VMEM(sh, BF16) for sh in half_shapes]
    scratch += [pltpu.SemaphoreType.DMA((total,)), pltpu.SemaphoreType.DMA((total,))]
    if scatter:
        scratch += [pltpu.VMEM(sh, BF16) for sh in sum_shapes]
        scratch += [pltpu.SemaphoreType.DMA((3 * total,)), pltpu.SemaphoreType.DMA((3 * total,))]
    outs = pl.pallas_call(
        body, name=name,
        grid_spec=pltpu.PrefetchScalarGridSpec(
            num_scalar_prefetch=1, grid=(total + 1,),
            in_specs=[sp[0] for sp in all_specs] + [sp[1] for sp in all_specs],
            out_specs=[sp[2] for sp in all_specs] + (_any_specs(n) if scatter else [sp[3] for sp in all_specs]),
            scratch_shapes=scratch),
        out_shape=[jax.ShapeDtypeStruct((h, cl), F32) for h, cl in zip(halves, cols)]
                  + [jax.ShapeDtypeStruct((N_CHIPS - 1 if scatter else N_CHIPS, *sh[1:]), BF16) for sh in sum_shapes],
        compiler_params=_params("arbitrary"),
    )(place, *grads, *grads)
    return list(zip(outs[:n], outs[n:]))


SMALL_SUM_ROWS = 16


def join_halves(owns, gots, small_pool, small_gla, small_top, g_gk_pad):
    n = len(owns)
    shapes = [g.shape for g in gots]
    rbs = [min(ADD_ROWS, sh[1]) for sh in shapes]
    counts = [sh[1] // rb for sh, rb in zip(shapes, rbs)]
    starts, total = _spans(counts)

    def body(*refs):
        o_refs, g_refs = refs[:n], refs[n:2 * n]
        pool_ref, gla_ref, top_ref, gk_ref = refs[2 * n:2 * n + 4]
        out_refs, total_ref = refs[2 * n + 4:3 * n + 4], refs[3 * n + 4]
        sum_refs = refs[3 * n + 5:4 * n + 5]
        local_sems, send_sems, recv_sems, all_ref, small_ref = refs[4 * n + 5:4 * n + 10]
        t = pl.program_id(0)
        x, y, c = _position()
        start_small, wait_small = _gather_small(small_ref, all_ref, *refs[4 * n + 10:])

        def copies(i, k):
            src = sum_refs[i].at[pl.ds(k * rbs[i], rbs[i])]
            rows = pl.ds(c * shapes[i][1] + k * rbs[i], rbs[i])
            return (pltpu.make_async_copy(src, out_refs[i].at[rows], local_sems.at[starts[i] + k]),
                    pltpu.make_async_remote_copy(
                        src_ref=src, dst_ref=out_refs[i].at[rows],
                        send_sem=send_sems.at[starts[i] + k], recv_sem=recv_sems.at[starts[i] + k],
                        device_id=(x, y, 1 - c), device_id_type=MESH))

        @pl.when(t == 0)
        def _():
            small_ref[0:3, :] = pool_ref[0:3, :]
            small_ref[3:5, :] = gla_ref[0:2, :]
            small_ref[5:8, :] = top_ref[0:3, :]
            for r in range(GATE_RANK):
                small_ref[8 + r // 2:9 + r // 2, (r % 2) * KEY_W:(r % 2 + 1) * KEY_W] = gk_ref[r:r + 1, :]
            start_small()

        for i in range(n):
            for k in range(counts[i]):
                @pl.when(t == starts[i] + k)
                def _(i=i, k=k):
                    total_i = o_refs[i][...]
                    for j in range(N_CHIPS - 1):
                        total_i = total_i + g_refs[i][j].astype(F32)
                    sum_refs[i][k * rbs[i]:(k + 1) * rbs[i], :] = total_i
                    for cp in copies(i, k):
                        cp.start()

        @pl.when(t == total - 1)
        def _():
            wait_small()
            small_total = all_ref[0]
            for dev in range(1, N_DEV):
                small_total = small_total + all_ref[dev]
            total_ref[...] = small_total
            for i in range(n):
                for k in range(counts[i]):
                    for cp in copies(i, k):
                        cp.wait()

    def specs(i):
        rb, cols = rbs[i], shapes[i][2]
        step = lambda t: _local_step(t, starts[i], counts[i])
        return (pl.BlockSpec((rb, cols), lambda t: (step(t), 0)),
                pl.BlockSpec((N_CHIPS - 1, rb, cols), lambda t: (0, step(t), 0)))

    all_specs = [specs(i) for i in range(n)]
    outs = pl.pallas_call(
        body, name="join_halves", grid=(total,),
        out_shape=[jax.ShapeDtypeStruct((2 * sh[1], sh[2]), F32) for sh in shapes]
                  + [jax.ShapeDtypeStruct((SMALL_SUM_ROWS, D), F32)],
        in_specs=[sp[0] for sp in all_specs] + [sp[1] for sp in all_specs] + [VMEM_SPEC] * 4,
        out_specs=_any_specs(n) + [VMEM_SPEC],
        scratch_shapes=[pltpu.VMEM(sh[1:], F32) for sh in shapes]
                       + [pltpu.SemaphoreType.DMA((total,)), pltpu.SemaphoreType.DMA((total,)),
                          pltpu.SemaphoreType.DMA((total,)),
                          pltpu.VMEM((N_DEV, SMALL_SUM_ROWS, D), F32), pltpu.VMEM((SMALL_SUM_ROWS, D), F32)]
                       + SMALL_SEMS,
        compiler_params=_params("arbitrary"),
    )(*owns, *gots, small_pool, small_gla, small_top, g_gk_pad)
    return outs[:n], outs[n]


def _adam_math(w, g, m, v):
    m = ADAM_B1 * m + (1.0 - ADAM_B1) * g
    v = ADAM_B2 * v + (1.0 - ADAM_B2) * (g * g)
    m_hat = m / (1.0 - ADAM_B1 ** ADAM_STEP)
    v_hat = v / (1.0 - ADAM_B2 ** ADAM_STEP)
    delta = -ADAM_LR * (m_hat / (jnp.sqrt(v_hat) + ADAM_EPS) + ADAM_WD * w)
    return delta, m, v


ADAM_BLOCK_BYTES = 2 ** 19
ADAM_MOST_STEPS = 8


def adamw(params, name):
    n = len(params)
    shapes = [p[0].shape for p in params]

    def tile_rows(shape):
        rows, cols = shape[0], shape[-1]
        aligned = 1 if len(shape) == 3 else 8
        divisors = [t for t in range(aligned, rows + 1, aligned) if rows % t == 0]
        tile = max(t for t in divisors if t * cols * 4 <= ADAM_BLOCK_BYTES)
        if rows // tile > ADAM_MOST_STEPS:
            tile = min(t for t in divisors if rows // t <= ADAM_MOST_STEPS)
        return tile

    tiles = [tile_rows(sh) for sh in shapes]
    counts = [sh[0] // tl for sh, tl in zip(shapes, tiles)]
    starts, total = _spans(counts)

    def body(*refs):
        ins, outs = refs[:4 * n], refs[4 * n:]
        t = pl.program_id(0)
        for i in range(n):
            @pl.when((t >= starts[i]) & (t < starts[i] + counts[i]))
            def _(i=i):
                w_ref, g_ref, m_ref, v_ref = ins[4 * i:4 * i + 4]
                d, nm, nv = _adam_math(w_ref[...], g_ref[...], m_ref[...], v_ref[...])
                outs[3 * i][...] = d
                outs[3 * i + 1][...] = nm
                outs[3 * i + 2][...] = nv

    def spec(i):
        block = (tiles[i],) + shapes[i][1:]
        zeros = (0,) * (len(block) - 1)
        return pl.BlockSpec(block, lambda t: (_local_step(t, starts[i], counts[i]),) + zeros)

    outs = pl.pallas_call(
        body, name=name, grid=(total,),
        out_shape=[jax.ShapeDtypeStruct(sh, F32) for sh in shapes for _ in range(3)],
        in_specs=[spec(i) for i in range(n) for _ in range(4)],
        out_specs=[spec(i) for i in range(n) for _ in range(3)],
        compiler_params=_params("arbitrary"),
    )(*[a for p in params for a in p])
    return [tuple(outs[3 * i:3 * i + 3]) for i in range(n)]


def adamw_small(params):
    n = len(params)

    def body(*refs):
        ins, outs = refs[:4 * n], refs[4 * n:]
        for k in range(n):
            w_ref, g_ref, m_ref, v_ref = ins[4 * k:4 * k + 4]
            d, nm, nv = _adam_math(w_ref[...], g_ref[...], m_ref[...], v_ref[...])
            outs[3 * k][...] = d
            outs[3 * k + 1][...] = nm
            outs[3 * k + 2][...] = nv

    flat = [a for p in params for a in p]
    outs = pl.pallas_call(
        body, name="adamw_small",
        out_shape=[jax.ShapeDtypeStruct(p[0].shape, F32) for p in params for _ in range(3)],
        in_specs=[VMEM_SPEC] * (4 * n), out_specs=[VMEM_SPEC] * (3 * n),
    )(*flat)
    return [tuple(outs[3 * k:3 * k + 3]) for k in range(n)]


def matmul_tn(a, b, name, tile_n, by_column_tile=False, chip_sums=()):
    s, m = a.shape
    n = b.shape[1]
    n_sums = len(chip_sums)
    steps = n // tile_n
    if by_column_tile:
        out_shape = jax.ShapeDtypeStruct((steps, m, tile_n), F32)
        out_spec = pl.BlockSpec((None, m, tile_n), lambda j: (j, 0, 0))
    else:
        out_shape = jax.ShapeDtypeStruct((m, n), F32)
        out_spec = pl.BlockSpec((m, tile_n), lambda j: (0, j))

    def body(a_ref, b_ref, *rest):
        sum_refs, out_ref, got_refs = rest[:n_sums], rest[n_sums], rest[n_sums + 1:2 * n_sums + 1]
        j = pl.program_id(0)
        copies = _scatter_copies(sum_refs, got_refs, *rest[2 * n_sums + 1:]) if n_sums else []

        @pl.when(j == 0)
        def _():
            for cp in copies:
                cp.start()

        out_ref[...] = _tn(a_ref[...], b_ref[...])

        @pl.when(j == steps - 1)
        def _():
            for cp in copies:
                cp.wait()

    outs = pl.pallas_call(
        body, name=name, grid=(steps,),
        out_shape=[out_shape] + _scatter_shapes(chip_sums),
        in_specs=[_full((s, m)), pl.BlockSpec((s, tile_n), lambda j: (0, j))] + _any_specs(n_sums),
        out_specs=[out_spec] + _any_specs(n_sums),
        scratch_shapes=[pltpu.SemaphoreType.DMA((3 * n_sums,)), pltpu.SemaphoreType.DMA((3 * n_sums,))]
                       if n_sums else [],
        compiler_params=_params("arbitrary"),
    )(a, b, *chip_sums)
    return outs[0], outs[1:]


ROW_TILE = 512


def _row_index(tile, rows):
    return tile * rows + lax.broadcasted_iota(jnp.int32, (rows, 1), 0)


def _inverse_counts(t_glob):
    return [1.0 / jnp.minimum(t_glob + 1, w).astype(F32) for w in POOL_WINDOWS]


def _sigmoid(z):
    return 1.0 / (1.0 + jnp.exp(-z))


def _trailing_sums(src, tmp, cols, window, rows):
    bufs = (src, tmp)
    span, level, start = 1, 0, 0
    while span < window:
        start += 8
        a, b = bufs[level % 2], bufs[(level + 1) % 2]
        n = HALO + rows - start
        b[start:start + n, cols] = a[start:start + n, cols] + a[start - span:start - span + n, cols]
        span, level = 2 * span, level + 1
    return bufs[level % 2][HALO:HALO + rows, cols]


def _leading_sums(src, tmp, cols, window, rows):
    bufs = (src, tmp)
    span, level, n = 1, 0, rows + HALO
    while span < window:
        n -= 8
        a, b = bufs[level % 2], bufs[(level + 1) % 2]
        b[0:n, cols] = a[0:n, cols] + a[span:span + n, cols]
        span, level = 2 * span, level + 1
    return bufs[level % 2][0:rows, cols]


def gather_in_background(step, last, out_refs, send_sems, recv_sems, finish):
    n = len(out_refs)
    x, y, c = _position()
    q = 2 * x + y
    chips = _other_chips(x, y)

    def copy(k, i, quarter, half, to):
        return _gather_copy(out_refs[i], send_sems, recv_sems, k * n + i, quarter, half, to)

    if not finish:
        @pl.when(step == 0)
        def _():
            for i in range(n):
                mine, _ = _halves(out_refs[i].shape[1], c)
                for j, chip in enumerate(chips):
                    copy(j, i, q, mine, (*chip, c)).start()

        @pl.when(step == last)
        def _():
            for j, chip in enumerate(chips):
                qj = 2 * chip[0] + chip[1]
                for i in range(n):
                    mine, _ = _halves(out_refs[i].shape[1], c)
                    copy(j, i, qj, mine, (x, y, c)).wait_recv()
                    copy(3 + j, i, qj, mine, (x, y, 1 - c)).start()
        return

    @pl.when(step == last)
    def _():
        for j, chip in enumerate(chips):
            qj = 2 * chip[0] + chip[1]
            for i in range(n):
                mine, other = _halves(out_refs[i].shape[1], c)
                copy(3 + j, i, qj, other, (x, y, c)).wait_recv()
                copy(j, i, q, mine, (x, y, c)).wait_send()
                copy(3 + j, i, qj, mine, (x, y, c)).wait_send()


def pool_forward(x, w0, wpi, gw, gb, scale, wpo, later):
    s = x.shape[0]
    ts = ROW_TILE
    nt = s // ts
    assert nt >= 2
    n_later = len(later)

    def body(x_ref, w0_ref, wpi_ref, gw_ref, gb_ref, sc_ref, wpo_ref, *rest):
        rest = rest[n_later:]
        h1_ref, pooled_ref, gt_ref, n0_ref = rest[:4]
        later_refs = rest[4:4 + n_later]
        ubuf, tbuf, hist, send_sems, recv_sems = rest[4 + n_later:]
        i = pl.program_id(0)
        gather_in_background(i, nt - 1, later_refs, send_sems, recv_sems, finish=False)
        xv = x_ref[...]
        r = lax.rsqrt(jnp.mean(xv * xv, axis=-1, keepdims=True) + EPS)
        n0 = _bf(xv * r * w0_ref[...])
        n0_ref[...] = n0
        u = jnp.concatenate([_nn(n0, wpi_ref[0]), _nn(n0, wpi_ref[1])], axis=-1)
        gt = jnp.concatenate([_nn(n0, wpi_ref[2]), _nn(n0, wpi_ref[3])], axis=-1)
        gt_ref[...] = gt

        @pl.when(i == 0)
        def _():
            hist[...] = jnp.zeros_like(hist)

        ubuf[0:HALO, :] = hist[...]
        ubuf[HALO:HALO + ts, :] = u
        hist[...] = u[ts - HALO:, :]
        inv = _inverse_counts(_row_index(i, ts))
        mixed = []
        for g, w in enumerate(POOL_WINDOWS):
            cols = slice(g * GROUP_DIM, (g + 1) * GROUP_DIM)
            pooled = _bf(_trailing_sums(ubuf, tbuf, cols, w, ts) * inv[g] - u[:, cols])
            pooled_ref[:, cols] = pooled
            mixed.append(_nn(pooled, gw_ref[g]))
        mixed = jnp.concatenate(mixed, axis=-1) + gb_ref[...]
        y = mixed * sc_ref[...] * (gt * _sigmoid(gt))
        h1_ref[...] = xv + _nn(_bf(y), wpo_ref[...])
        gather_in_background(i, nt - 1, later_refs, send_sems, recv_sems, finish=True)

    row = lambda cols: pl.BlockSpec((ts, cols), lambda i: (i, 0))
    outs = pl.pallas_call(
        body, name="pool_forward", grid=(nt,),
        out_shape=[jax.ShapeDtypeStruct((s, D), F32), jax.ShapeDtypeStruct((s, D), BF16),
                   jax.ShapeDtypeStruct((s, D), F32), jax.ShapeDtypeStruct((s, D), BF16)]
                  + [jax.ShapeDtypeStruct(a.shape, a.dtype) for a in later],
        in_specs=[row(D), _full((1, D)), _full((N_CHIPS, D, D // 2)), _full((GROUPS, GROUP_DIM, GROUP_DIM)),
                  _full((1, D)), _full((1, D)), _full((D, D))] + _any_specs(n_later),
        out_specs=[row(D), row(D), row(D), row(D)] + _any_specs(n_later),
        input_output_aliases={7 + k: 4 + k for k in range(n_later)},
        scratch_shapes=[pltpu.VMEM((HALO + ts, D), F32), pltpu.VMEM((HALO + ts, D), F32),
                        pltpu.VMEM((HALO, D), F32),
                        pltpu.SemaphoreType.DMA((6 * n_later,)), pltpu.SemaphoreType.DMA((6 * n_later,))],
        compiler_params=_params("arbitrary"),
    )(x, w0, wpi, gw, gb, scale, wpo, *later)
    return outs[:4], outs[4:]


def pool_backward(x, dh1, pooled, gt, w0, wpi, gw, gb, scale, wpo, chip_sums):
    s = x.shape[0]
    ts = ROW_TILE
    nt = s // ts
    n_sums = len(chip_sums)

    def body(x_ref, dh1_ref, pooled_ref, gt_ref, w0_ref, wpi_ref, gw_ref, gb_ref, sc_ref, wpo_ref, *rest):
        sum_refs, rest = rest[:n_sums], rest[n_sums:]
        dx_ref, dproj_ref, gpo_ref, ggw_ref, small_ref = rest[:5]
        got_refs = rest[5:5 + n_sums]
        ebuf, tbuf, ahead, send_sems, recv_sems = rest[5 + n_sums:]
        i = pl.program_id(0)
        copies = _scatter_copies(sum_refs, got_refs, send_sems, recv_sems)

        @pl.when(i == 0)
        def _():
            for cp in copies:
                cp.start()

        @pl.when(i == 0)
        def _():
            gpo_ref[...] = jnp.zeros_like(gpo_ref)
            ggw_ref[...] = jnp.zeros_like(ggw_ref)
            small_ref[...] = jnp.zeros_like(small_ref)
            ahead[...] = jnp.zeros_like(ahead)

        dh1 = dh1_ref[...]
        dh1_bf = _bf(dh1)
        gt = gt_ref[...]
        sc = sc_ref[...]
        dy = _nt(dh1_bf, wpo_ref[...])
        pooled_bf = []
        mixed = []
        for g in range(GROUPS):
            cols = slice(g * GROUP_DIM, (g + 1) * GROUP_DIM)
            pb = pooled_ref[:, cols]
            pooled_bf.append(pb)
            mixed.append(_nn(pb, gw_ref[g]))
        mixed = jnp.concatenate(mixed, axis=-1) + gb_ref[...]
        sg = _sigmoid(gt)
        silu = gt * sg
        gpo_ref[...] += _tn(_bf(mixed * sc * silu), dh1_bf)
        dmixed = dy * sc * silu
        dgt = dy * mixed * sc * (sg * (1.0 + gt * (1.0 - sg)))
        dproj_ref[:, D:] = _bf(dgt)
        small_ref[1:2, :] += jnp.sum(dy * mixed * silu, axis=0, keepdims=True)
        small_ref[2:3, :] += jnp.sum(dmixed, axis=0, keepdims=True)

        inv = _inverse_counts(_row_index(nt - 1 - i, ts))
        ebuf[ts:ts + HALO, :] = ahead[...]
        dpooled = []
        for g in range(GROUPS):
            cols = slice(g * GROUP_DIM, (g + 1) * GROUP_DIM)
            dm = _bf(dmixed[:, cols])
            ggw_ref[g] += _tn(pooled_bf[g], dm)
            dp = _nt(dm, gw_ref[g])
            dpooled.append(dp)
            ebuf[0:ts, cols] = dp * inv[g]
        ahead[...] = ebuf[0:HALO, :]
        du = []
        for g, w in enumerate(POOL_WINDOWS):
            cols = slice(g * GROUP_DIM, (g + 1) * GROUP_DIM)
            du.append(_leading_sums(ebuf, tbuf, cols, w, ts) - dpooled[g])
        du = _bf(jnp.concatenate(du, axis=-1))
        dproj_ref[:, :D] = du
        dgt_bf = _bf(dgt)
        half = D // 2
        dn0 = (_nt(du[:, :half], wpi_ref[0]) + _nt(du[:, half:], wpi_ref[1])
               + _nt(dgt_bf[:, :half], wpi_ref[2]) + _nt(dgt_bf[:, half:], wpi_ref[3]))

        xv = x_ref[...]
        r = lax.rsqrt(jnp.mean(xv * xv, axis=-1, keepdims=True) + EPS)
        xhat = xv * r
        small_ref[0:1, :] += jnp.sum(dn0 * xhat, axis=0, keepdims=True)
        dxh = dn0 * w0_ref[...]
        dx_ref[...] = dh1 + r * (dxh - xhat * jnp.mean(dxh * xhat, axis=-1, keepdims=True))

        @pl.when(i == nt - 1)
        def _():
            for cp in copies:
                cp.wait()

    row = lambda cols: pl.BlockSpec((ts, cols), lambda i: (nt - 1 - i, 0))
    outs = pl.pallas_call(
        body, name="pool_backward", grid=(nt,),
        out_shape=[jax.ShapeDtypeStruct((s, D), F32), jax.ShapeDtypeStruct((s, 2 * D), BF16),
                   jax.ShapeDtypeStruct((D, D), F32),
                   jax.ShapeDtypeStruct((GROUPS, GROUP_DIM, GROUP_DIM), F32),
                   jax.ShapeDtypeStruct((8, D), F32)] + _scatter_shapes(chip_sums),
        in_specs=[row(D), row(D), row(D), row(D), _full((1, D)), _full((N_CHIPS, D, D // 2)),
                  _full((GROUPS, GROUP_DIM, GROUP_DIM)), _full((1, D)), _full((1, D)), _full((D, D))]
                 + _any_specs(n_sums),
        out_specs=[row(D), row(2 * D), _full((D, D)), _full((GROUPS, GROUP_DIM, GROUP_DIM)), _full((8, D))]
                  + _any_specs(n_sums),
        scratch_shapes=[pltpu.VMEM((ts + HALO, D), F32), pltpu.VMEM((ts + HALO, D), F32),
                        pltpu.VMEM((HALO, D), F32),
                        pltpu.SemaphoreType.DMA((3 * n_sums,)), pltpu.SemaphoreType.DMA((3 * n_sums,))],
        compiler_params=_params("arbitrary"),
    )(x, dh1, pooled, gt, w0, wpi, gw, gb, scale, wpo, *chip_sums)
    return outs[:5], outs[5:]


def gla_project(h1, w1, wgi_q, wgk, bgk, later):
    s = h1.shape[0]
    ts = ROW_TILE
    nt = s // ts
    assert nt >= 2
    n_later = len(later)

    def body(h_ref, w1_ref, wq_ref, wgk_ref, bgk_ref, *rest):
        rest = rest[n_later:]
        qk_ref, v_ref, gate_ref, low_ref, cum_ref, n1_ref = rest[:6]
        later_refs = rest[6:6 + n_later]
        send_sems, recv_sems, wgi_ref = rest[6 + n_later:]
        gather_in_background(pl.program_id(0), nt - 1, later_refs, send_sems, recv_sems, finish=False)

        @pl.when(pl.program_id(0) == 0)
        def _():
            _assemble_gla_in(wq_ref, wgi_ref)

        hv = h_ref[...]
        r = lax.rsqrt(jnp.mean(hv * hv, axis=-1, keepdims=True) + EPS)
        n1 = _bf(hv * r * w1_ref[...])
        n1_ref[...] = n1
        qk_ref[...] = _nn(n1, wgi_ref[:, 0:2 * KEY_W])
        v_ref[...] = _bf(_nn(n1, wgi_ref[:, 2 * KEY_W:2 * KEY_W + D]))
        gate_ref[...] = _nn(n1, wgi_ref[:, 2 * KEY_W + D:GLA_MAIN])
        low = _bf(_nn(n1, wgi_ref[:, GLA_MAIN:]))
        low_ref[...] = low
        z = _nn(low, wgk_ref[...]) + bgk_ref[...]
        lg = (jnp.minimum(z, 0.0) - jnp.log(1.0 + jnp.exp(-jnp.abs(z)))) / GATE_NORM
        lower_f = _chunk_masks()[0].astype(F32)
        for r0 in range(0, ts, CHUNK):
            cum_ref[r0:r0 + CHUNK, :] = _nn_exact(lower_f, lg[r0:r0 + CHUNK, :])
        gather_in_background(pl.program_id(0), nt - 1, later_refs, send_sems, recv_sems, finish=True)

    row = lambda cols: pl.BlockSpec((ts, cols), lambda i: (i, 0))
    outs = pl.pallas_call(
        body, name="gla_project", grid=(nt,),
        out_shape=[jax.ShapeDtypeStruct((s, D), F32), jax.ShapeDtypeStruct((s, D), BF16),
                   jax.ShapeDtypeStruct((s, D), F32), jax.ShapeDtypeStruct((s, RANK_PAD), BF16),
                   jax.ShapeDtypeStruct((s, KEY_W), F32), jax.ShapeDtypeStruct((s, D), BF16)]
                  + [jax.ShapeDtypeStruct(a.shape, a.dtype) for a in later],
        in_specs=[row(D), _full((1, D)), _full((N_CHIPS, D, GLA_IN_QUARTER)),
                  _full((RANK_PAD, KEY_W)), _full((1, KEY_W))] + _any_specs(n_later),
        out_specs=[row(D), row(D), row(D), row(RANK_PAD), row(KEY_W), row(D)] + _any_specs(n_later),
        input_output_aliases={5 + k: 6 + k for k in range(n_later)},
        scratch_shapes=[pltpu.SemaphoreType.DMA((6 * n_later,)), pltpu.SemaphoreType.DMA((6 * n_later,)),
                        pltpu.VMEM((D, GLA_MAIN + RANK_PAD), BF16)],
        compiler_params=_params("arbitrary"),
    )(h1, w1, wgi_q, wgk, bgk, *later)
    return outs[:6], outs[6:]


def _assemble_gla_in(wq_ref, wfull):
    pad = jnp.zeros((CAST_ROWS, GLA_MAIN + RANK_PAD - GLA_IN), BF16)
    for r0 in range(0, D, CAST_ROWS):
        rows = slice(r0, r0 + CAST_ROWS)
        wfull[rows, :] = jnp.concatenate([wq_ref[q, rows, :] for q in range(N_CHIPS)] + [pad], axis=1)


GLA_BLOCK = 512
CHUNKS_PER_BLOCK = GLA_BLOCK // CHUNK


def _chunk_masks():
    t = lax.broadcasted_iota(jnp.int32, (CHUNK, CHUNK), 0)
    u = lax.broadcasted_iota(jnp.int32, (CHUNK, CHUNK), 1)
    return t >= u, t <= u


def _gla_chunk_terms(q, cum):
    ep = jnp.exp(cum)
    en = jnp.exp(-cum)
    qs = q * (HEAD_K ** -0.5)
    last = cum[CHUNK - 1:CHUNK, :]
    ed = jnp.exp(last - cum)
    dec = jnp.exp(last)
    return ep, en, qs, ed, dec


def gla_forward(qk, v, cum):
    s = qk.shape[0]
    nb = s // GLA_BLOCK
    nc = s // CHUNK

    def body(q_ref, k_ref, v_ref, cum_ref, o_ref, st_ref, sc_ref, state):
        @pl.when(pl.program_id(0) == 0)
        def _():
            state[...] = jnp.zeros_like(state)

        lower, _ = _chunk_masks()

        def chunk(cc, carry):
            rows = pl.ds(pl.multiple_of(cc * CHUNK, CHUNK), CHUNK)
            for h in range(HEADS):
                kc = slice(h * HEAD_K, (h + 1) * HEAD_K)
                vc = slice(h * HEAD_V, (h + 1) * HEAD_V)
                q = q_ref[rows, kc]
                k = k_ref[rows, kc]
                v = v_ref[rows, vc]
                ep, en, qs, ed, dec = _gla_chunk_terms(q, cum_ref[rows, kc])
                a = _bf(qs * ep)
                fwd = _nt(a, _bf(k * en))
                bwd = _nt(_bf(qs * en), _bf(k * ep))
                scores = _bf(jnp.where(lower, fwd, bwd))
                sc_ref[rows, h * CHUNK:(h + 1) * CHUNK] = scores
                st = state[h]
                st_ref[cc, h] = st
                o_ref[rows, vc] = _nn(scores, v) + _nt(a, _bf(st))
                state[h] = st * dec + _tn(v, _bf(k * ed))
            return carry

        lax.fori_loop(0, CHUNKS_PER_BLOCK, chunk, 0, unroll=4)

    return pl.pallas_call(
        body, name="gla_forward", grid=(nb,),
        out_shape=(jax.ShapeDtypeStruct((s, D), F32),
                   jax.ShapeDtypeStruct((nc, HEADS, HEAD_V, HEAD_K), F32),
                   jax.ShapeDtypeStruct((s, HEADS * CHUNK), BF16)),
        in_specs=[pl.BlockSpec((GLA_BLOCK, KEY_W), lambda i: (i, 0)),
                  pl.BlockSpec((GLA_BLOCK, KEY_W), lambda i: (i, 1)),
                  pl.BlockSpec((GLA_BLOCK, D), lambda i: (i, 0)),
                  pl.BlockSpec((GLA_BLOCK, KEY_W), lambda i: (i, 0))],
        out_specs=(pl.BlockSpec((GLA_BLOCK, D), lambda i: (i, 0)),
                   pl.BlockSpec((CHUNKS_PER_BLOCK, HEADS, HEAD_V, HEAD_K), lambda i: (i, 0, 0, 0)),
                   pl.BlockSpec((GLA_BLOCK, HEADS * CHUNK), lambda i: (i, 0))),
        scratch_shapes=[pltpu.VMEM((HEADS, HEAD_V, HEAD_K), F32)],
        compiler_params=_params("arbitrary"),
    )(qk, qk, v, cum)


def gla_backward(qk, v, cum, do, states, scores):
    s = qk.shape[0]
    nb = s // GLA_BLOCK

    def body(q_ref, k_ref, v_ref, cum_ref, do_ref, st_ref, sc_ref, dq_ref, dk_ref, dv_ref, dcum_ref, dstate):
        @pl.when(pl.program_id(0) == 0)
        def _():
            dstate[...] = jnp.zeros_like(dstate)

        lower, _ = _chunk_masks()
        is_last = lax.broadcasted_iota(jnp.int32, (CHUNK, HEAD_K), 0) == CHUNK - 1

        def chunk(step, carry):
            cc = CHUNKS_PER_BLOCK - 1 - step
            rows = pl.ds(pl.multiple_of(cc * CHUNK, CHUNK), CHUNK)
            for h in range(HEADS):
                kc = slice(h * HEAD_K, (h + 1) * HEAD_K)
                vc = slice(h * HEAD_V, (h + 1) * HEAD_V)
                q = q_ref[rows, kc]
                k = k_ref[rows, kc]
                v = v_ref[rows, vc]
                do_c = do_ref[rows, vc]
                ep, en, qs, ed, dec = _gla_chunk_terms(q, cum_ref[rows, kc])
                a = _bf(qs * ep)
                b = _bf(k * en)
                c = _bf(qs * en)
                dk_dec = _bf(k * ep)
                kd = _bf(k * ed)
                scores = sc_ref[rows, h * CHUNK:(h + 1) * CHUNK]
                st = st_ref[cc, h]
                dst = dstate[h]
                dst_bf = _bf(dst)

                dscores = _nt(do_c, v)
                dfwd = _bf(jnp.where(lower, dscores, 0.0))
                dbwd = _bf(jnp.where(lower, 0.0, dscores))
                dv_ref[rows, vc] = _bf(_tn(scores, do_c) + _nt(kd, dst_bf))
                da = _nn(dfwd, b) + _nn(do_c, _bf(st))
                db = _tn(dfwd, a)
                dc = _nn(dbwd, dk_dec)
                ddk = _tn(dbwd, c)
                dkd = _nn(v, dst_bf)
                ddec = jnp.sum(dst * st, axis=0, keepdims=True)
                dstate[h] = dst * dec + _tn(do_c, a)

                m = dkd * k * ed
                dq_ref[rows, kc] = _bf((da * ep + dc * en) * (HEAD_K ** -0.5))
                dk_ref[rows, kc] = _bf(db * en + ddk * ep + dkd * ed)
                dcum = (da * qs + ddk * k) * ep - (db * k + dc * qs) * en - m
                dlast = jnp.sum(m, axis=0, keepdims=True) + ddec * dec
                dcum_ref[rows, kc] = dcum + jnp.where(is_last, dlast, 0.0)
            return carry

        lax.fori_loop(0, CHUNKS_PER_BLOCK, chunk, 0, unroll=4)

    rev = lambda cols, col_block: pl.BlockSpec((GLA_BLOCK, cols), lambda i: (nb - 1 - i, col_block))
    return pl.pallas_call(
        body, name="gla_backward", grid=(nb,),
        out_shape=(jax.ShapeDtypeStruct((s, KEY_W), BF16), jax.ShapeDtypeStruct((s, KEY_W), BF16),
                   jax.ShapeDtypeStruct((s, D), BF16), jax.ShapeDtypeStruct((s, KEY_W), F32)),
        in_specs=[rev(KEY_W, 0), rev(KEY_W, 1), rev(D, 0), rev(KEY_W, 0), rev(D, 0),
                  pl.BlockSpec((CHUNKS_PER_BLOCK, HEADS, HEAD_V, HEAD_K), lambda i: (nb - 1 - i, 0, 0, 0)),
                  rev(HEADS * CHUNK, 0)],
        out_specs=(rev(KEY_W, 0), rev(KEY_W, 0), rev(D, 0), rev(KEY_W, 0)),
        scratch_shapes=[pltpu.VMEM((HEADS, HEAD_V, HEAD_K), F32)],
        compiler_params=_params("arbitrary"),
    )(qk, qk, v, cum, do, states, scores)


def head_and_loss(o, gate, h1, target, hw, wgo, wf):
    s = o.shape[0]
    ts = ROW_TILE

    def body(o_ref, gate_ref, h1_ref, tgt_ref, hw_ref, wgo_ref, wf_ref,
             dh2_ref, do_ref, dgate_ref, ggo_ref, small_ref):
        @pl.when(pl.program_id(0) == 0)
        def _():
            ggo_ref[...] = jnp.zeros_like(ggo_ref)
            small_ref[...] = jnp.zeros_like(small_ref)

        gate = gate_ref[...]
        hw = hw_ref[...]
        sg = _sigmoid(gate)
        silu = gate * sg
        ohat, ro = [], []
        for h in range(HEADS):
            oh = o_ref[:, h * HEAD_V:(h + 1) * HEAD_V]
            rh = lax.rsqrt(jnp.mean(oh * oh, axis=-1, keepdims=True) + EPS)
            ro.append(rh)
            ohat.append(oh * rh)
        ohat = jnp.concatenate(ohat, axis=-1)
        on = ohat * hw
        y2 = _bf(on * silu)
        h2 = h1_ref[...] + _nn(y2, wgo_ref[...])
        rf = lax.rsqrt(jnp.mean(h2 * h2, axis=-1, keepdims=True) + EPS)
        h2hat = h2 * rf
        wf = wf_ref[...]
        diff = h2hat * wf - tgt_ref[...]
        small_ref[2:3, :] += jnp.zeros((1, D), F32) + 0.5 * jnp.sum(diff * diff) / D
        dout = diff / D
        small_ref[0:1, :] += jnp.sum(dout * h2hat, axis=0, keepdims=True)
        dxh = dout * wf
        dh2 = rf * (dxh - h2hat * jnp.mean(dxh * h2hat, axis=-1, keepdims=True))
        dh2_ref[...] = dh2
        dh2_bf = _bf(dh2)
        ggo_ref[...] += _tn(y2, dh2_bf)
        dy2 = _nt(dh2_bf, wgo_ref[...])
        don = dy2 * silu
        dgate_ref[...] = _bf(dy2 * on * (sg * (1.0 + gate * (1.0 - sg))))
        ghw = jnp.sum(don * ohat, axis=0, keepdims=True)
        small_ref[1:2, 0:HEAD_V] += sum(ghw[:, h * HEAD_V:(h + 1) * HEAD_V] for h in range(HEADS))
        dohat = don * hw
        for h in range(HEADS):
            cols = slice(h * HEAD_V, (h + 1) * HEAD_V)
            oh, dh = ohat[:, cols], dohat[:, cols]
            do_ref[:, cols] = _bf(ro[h] * (dh - oh * jnp.mean(dh * oh, axis=-1, keepdims=True)))

    row = lambda cols: pl.BlockSpec((ts, cols), lambda i: (i, 0))
    act = jax.ShapeDtypeStruct((s, D), F32)
    act_bf = jax.ShapeDtypeStruct((s, D), BF16)
    return pl.pallas_call(
        body, name="head_and_loss", grid=(s // ts,),
        out_shape=(act, act_bf, act_bf, jax.ShapeDtypeStruct((D, D), F32), jax.ShapeDtypeStruct((8, D), F32)),
        in_specs=[row(D), row(D), row(D), row(D),
                  _full((1, D)), _full((D, D)), _full((1, D))],
        out_specs=(row(D), row(D), row(D), _full((D, D)), _full((8, D))),
        compiler_params=_params("arbitrary"),
    )(o, gate, h1, target, hw, wgo, wf)


def gla_project_backward(dq, dk, dv, dgate, dcum, low, h1, dh2, w1, wgi_q, wgk, bgk):
    s = h1.shape[0]
    ts = ROW_TILE

    def body(dq_ref, dk_ref, dv_ref, dgate_ref, dcum_ref, low_ref, h1_ref, dh2_ref, w1_ref,
             wq_ref, wgk_ref, bgk_ref, dh1_ref, dproj_ref, ggk_ref, small_ref, wgi_ref):
        @pl.when(pl.program_id(0) == 0)
        def _():
            ggk_ref[...] = jnp.zeros_like(ggk_ref)
            small_ref[...] = jnp.zeros_like(small_ref)
            _assemble_gla_in(wq_ref, wgi_ref)

        low = low_ref[...]
        z = _nn(low, wgk_ref[...]) + bgk_ref[...]
        upper_f = _chunk_masks()[1].astype(F32)
        dlg = jnp.concatenate([_nn_exact(upper_f, dcum_ref[r0:r0 + CHUNK, :]) for r0 in range(0, ts, CHUNK)],
                              axis=0)
        dz = dlg * (1.0 / GATE_NORM) * _sigmoid(-z)
        dz_bf = _bf(dz)
        ggk_ref[...] += _tn(low, dz_bf)
        small_ref[1:2, 0:KEY_W] += jnp.sum(dz, axis=0, keepdims=True)
        dlow = _bf(_nt(dz_bf, wgk_ref[...]))
        dproj_ref[:, GLA_MAIN:] = dlow
        dn1 = _nt(dlow, wgi_ref[:, GLA_MAIN:])
        for ref, lo, hi in ((dq_ref, 0, KEY_W), (dk_ref, KEY_W, 2 * KEY_W),
                            (dv_ref, 2 * KEY_W, 2 * KEY_W + D), (dgate_ref, 2 * KEY_W + D, GLA_MAIN)):
            piece = ref[...]
            dproj_ref[:, lo:hi] = piece
            dn1 = dn1 + _nt(piece, wgi_ref[:, lo:hi])
        hv = h1_ref[...]
        r = lax.rsqrt(jnp.mean(hv * hv, axis=-1, keepdims=True) + EPS)
        hhat = hv * r
        small_ref[0:1, :] += jnp.sum(dn1 * hhat, axis=0, keepdims=True)
        dxh = dn1 * w1_ref[...]
        dh1_ref[...] = dh2_ref[...] + r * (dxh - hhat * jnp.mean(dxh * hhat, axis=-1, keepdims=True))

    row = lambda cols: pl.BlockSpec((ts, cols), lambda i: (i, 0))
    return pl.pallas_call(
        body, name="gla_project_backward", grid=(s // ts,),
        out_shape=(jax.ShapeDtypeStruct((s, D), F32), jax.ShapeDtypeStruct((s, GLA_MAIN + RANK_PAD), BF16),
                   jax.ShapeDtypeStruct((RANK_PAD, KEY_W), F32),
                   jax.ShapeDtypeStruct((8, D), F32)),
        in_specs=[row(KEY_W), row(KEY_W), row(D), row(D), row(KEY_W), row(RANK_PAD), row(D), row(D),
                  _full((1, D)), _full((N_CHIPS, D, GLA_IN_QUARTER)), _full((RANK_PAD, KEY_W)),
                  _full((1, KEY_W))],
        out_specs=(row(D), row(GLA_MAIN + RANK_PAD), _full((RANK_PAD, KEY_W)), _full((8, D))),
        scratch_shapes=[pltpu.VMEM((D, GLA_MAIN + RANK_PAD), BF16)],
        compiler_params=_params("arbitrary"),
    )(dq, dk, dv, dgate, dcum, low, h1, dh2, w1, wgi_q, wgk, bgk)


def _groups_from_quarters(a):
    return a.reshape(N_CHIPS, GROUPS, 64, GROUP_DIM).transpose(1, 0, 2, 3).reshape(GROUPS, GROUP_DIM, GROUP_DIM)


def _quarters_from_groups(a):
    return a.reshape(GROUPS, N_CHIPS, 64, GROUP_DIM).transpose(1, 0, 2, 3).reshape(N_CHIPS, GROUP_DIM, GROUP_DIM)


def local_gradients(xs, target, w0, w1, wf, wpi, gw, gb, scale, wpo, gla_quarters, wgk, bgk, hw_tiled, place):
    wgi_q, wgo_q = gla_quarters
    (h1, pooled, gt, n0), (wgi_q,) = pool_forward(xs, w0, wpi, gw, gb, scale, wpo, [wgi_q])
    (qk, v, gate, low, cum, n1), (wgo_q,) = gla_project(h1, w1, wgi_q, wgk, bgk, [wgo_q])
    wgo = wgo_q.reshape(D, D)
    o, states, scores = gla_forward(qk, v, cum)

    dh2, do, dgate, g_gla_out, small_top = head_and_loss(o, gate, h1, target, hw_tiled, wgo, wf)
    dq, dk, dv, dcum = gla_backward(qk, v, cum, do, states, scores)
    dh1, dproj, g_gk_pad, small_gla = gla_project_backward(
        dq, dk, dv, dgate, dcum, low, h1, dh2, w1, wgi_q, wgk, bgk)
    g_gla_in, _ = matmul_tn(n1, dproj, "grad_gla_in", tile_n=(GLA_MAIN + RANK_PAD) // 5)

    def chip_sums(grads, tag):
        return add_halves(grads, place, "add_halves_" + tag)

    gla_sums = chip_sums([g_gla_in, g_gla_out.reshape(N_CHIPS, D // N_CHIPS, D)], "gla")
    (dx, dpool, g_pool_out, g_group_w, small_pool), gla_got = pool_backward(
        xs, dh1, pooled, gt, w0, wpi, gw, gb, scale, wpo, [b for _, b in gla_sums])
    mix_sums = chip_sums([_quarters_from_groups(g_group_w), g_pool_out.reshape(N_CHIPS, D // N_CHIPS, D)], "pool_mix")
    g_pool_in, mix_got = matmul_tn(n0, dpool, "grad_pool_in", tile_n=D // 2, by_column_tile=True,
                                   chip_sums=[b for _, b in mix_sums])

    in_sums = add_halves([g_pool_in], place, "add_halves_and_scatter_pool_in", scatter=True)
    reduced, total = join_halves(
        [f for f, _ in in_sums + mix_sums + gla_sums], [got for _, got in in_sums] + list(mix_got) + list(gla_got),
        small_pool, small_gla, small_top, g_gk_pad)
    return dx, reduced, total


def kernel(x, norm_w, pool_in_w, pool_group_w, pool_group_b, pool_scale, pool_out_w, gla_in_w, gla_gk_w, gla_gk_b, gla_head_norm_w, gla_out_w, final_norm_w, loss_target, m_norm_w, m_pool_in_w, m_pool_group_w, m_pool_group_b, m_pool_scale, m_pool_out_w, m_gla_in_w, m_gla_gk_w, m_gla_gk_b, m_gla_head_norm_w, m_gla_out_w, m_final_norm_w, v_norm_w, v_pool_in_w, v_pool_group_w, v_pool_group_b, v_pool_scale, v_pool_out_w, v_gla_in_w, v_gla_gk_w, v_gla_gk_b, v_gla_head_norm_w, v_gla_out_w, v_final_norm_w):
    xs = x[0]
    target = loss_target[0]
    q_chip = 2 * lax.axis_index("x") + lax.axis_index("y")
    place = jnp.stack([lax.axis_index("c"), q_chip]).astype(jnp.int32)

    (wpi, gw_q, wpo_q, wgi_q, wgo_q), small_all = allgather_weights(
        [pool_in_w[0], pool_group_w[0].reshape(GROUP_DIM, GROUP_DIM), pool_out_w[0], gla_in_w[0], gla_out_w[0]],
        exchange=(True, True, True, False, False),
        smalls=[gla_gk_b, gla_head_norm_w, pool_group_b[0], gla_gk_w[0]])
    gw = _groups_from_quarters(gw_q)
    wpo = wpo_q.reshape(D, D)
    small_all = small_all[0::2]
    bgk = small_all[:, 0, :].reshape(1, KEY_W)
    hw = small_all[:, 1, 0:64].reshape(1, HEAD_V)
    gb = small_all[:, 2:2 + GROUPS, 0:64].transpose(1, 0, 2).reshape(1, D)
    wgk16 = small_all[:, 8:8 + GATE_RANK, :].transpose(1, 0, 2).reshape(GATE_RANK, KEY_W)
    wgk = _bf(jnp.pad(wgk16, ((0, RANK_PAD - GATE_RANK), (0, 0))))
    hw_tiled = jnp.tile(hw, (1, HEADS))

    w0 = norm_w[0:1]
    w1 = norm_w[1:2]
    wf = final_norm_w.reshape(1, D)

    dx, reduced, total = local_gradients(
        xs, target, w0, w1, wf, wpi, gw, gb, pool_scale, wpo, [wgi_q, wgo_q], wgk, bgk, hw_tiled, place)
    r_pool_in, r_group_w, r_pool_out, r_gla_in, r_gla_out = reduced
    r_group_w = r_group_w.reshape(GROUPS, 64, GROUP_DIM)

    loss = total[7, 0]
    g_norm = jnp.stack([total[0], total[3]])
    g_scale = total[1:2]
    g_final = total[5]
    pick = lambda full, width: lax.dynamic_slice_in_dim(full, q_chip * width, width, axis=-1)
    g_gk_b = pick(total[4:5, 0:KEY_W], 128)
    g_hnw = pick(total[6:7, 0:HEAD_V], 64)
    g_group_b = pick(total[2].reshape(GROUPS, GROUP_DIM), 64)[None]
    g_gk_w = pick(total[8:16].reshape(GATE_RANK, KEY_W), 128)[None]

    turn = lambda a: jnp.transpose(a, (2, 0, 1))
    back = lambda a: jnp.transpose(a, (1, 2, 0))
    as2d = lambda a, w: a.reshape(-1, w.shape[-1])
    big_names = ("pool_in_w", "pool_group_w", "pool_out_w", "gla_in_w", "gla_out_w")
    big_args = [(pool_in_w, r_pool_in[None], m_pool_in_w, v_pool_in_w),
                (pool_group_w, r_group_w[None], m_pool_group_w, v_pool_group_w),
                (pool_out_w, r_pool_out[None], m_pool_out_w, v_pool_out_w),
                (gla_in_w, r_gla_in[None], m_gla_in_w, v_gla_in_w),
                (gla_out_w, r_gla_out[None], m_gla_out_w, v_gla_out_w)]
    to_kernel = lambda n, a, w: turn(a) if n == "gla_in_w" else as2d(a, w)
    from_kernel = lambda n, a, w: back(a) if n == "gla_in_w" else a.reshape(w.shape)
    big_in = [tuple(to_kernel(n, a, p[0]) for a in p) for n, p in zip(big_names, big_args)]
    big_out = adamw(big_in, "adamw")
    big = {n: (from_kernel(n, i[1], p[0]),) + tuple(from_kernel(n, o, p[0]) for o in out)
           for n, p, i, out in zip(big_names, big_args, big_in, big_out)}

    small_names = ("norm_w", "pool_group_b", "pool_scale", "gla_gk_w", "gla_gk_b", "gla_head_norm_w",
                   "final_norm_w")
    small_args = [(norm_w, g_norm, m_norm_w, v_norm_w),
                  (pool_group_b, g_group_b, m_pool_group_b, v_pool_group_b),
                  (pool_scale, g_scale, m_pool_scale, v_pool_scale),
                  (gla_gk_w, g_gk_w, m_gla_gk_w, v_gla_gk_w),
                  (gla_gk_b, g_gk_b, m_gla_gk_b, v_gla_gk_b),
                  (gla_head_norm_w, g_hnw, m_gla_head_norm_w, v_gla_head_norm_w),
                  (final_norm_w, g_final, m_final_norm_w, v_final_norm_w)]
    small_out = adamw_small([tuple(as2d(a, p[0]) for a in p) for p in small_args])
    small = {n: (p[1].reshape(p[0].shape),) + tuple(o.reshape(p[0].shape) for o in out)
             for n, p, out in zip(small_names, small_args, small_out)}
    results = [
        small["norm_w"],
        big["pool_in_w"],
        big["pool_group_w"],
        small["pool_group_b"],
        small["pool_scale"],
        big["pool_out_w"],
        big["gla_in_w"],
        small["gla_gk_w"],
        small["gla_gk_b"],
        small["gla_head_norm_w"],
        big["gla_out_w"],
        small["final_norm_w"],
    ]
    grads, deltas, new_m, new_v = zip(*results)
    return (loss, dx[None], *grads, *deltas, *new_m, *new_v)
```

```python
import jax
import jax.numpy as jnp
from jax import lax
from jax.experimental import pallas as pl
from jax.experimental.pallas import tpu as pltpu

F32 = jnp.float32
BF16 = jnp.bfloat16
MESH = pl.DeviceIdType.MESH

D = 1024
POOL_WINDOWS = (2, 4, 8, 16)
GROUPS = 4
GROUP_DIM = 256
HEADS = 4
HEAD_K = 128
HEAD_V = 256
KEY_W = 512
CHUNK = 64
GATE_RANK = 16
GATE_NORM = 16.0
GLA_IN = 3088
GLA_MAIN = 3072
RANK_PAD = 128
EPS = 1e-6
HALO = 32

ADAM_LR = 0.001
ADAM_B1 = 0.9
ADAM_B2 = 0.999
ADAM_EPS = 1e-08
ADAM_WD = 0.01
ADAM_STEP = 10

N_CHIPS = 4
N_DEV = 8
GLA_IN_QUARTER = GLA_IN // N_CHIPS

VMEM_LIMIT = 56 * 1024 * 1024


def _nn(a, b):
    return lax.dot_general(a, b, (((1,), (0,)), ((), ())), preferred_element_type=F32)


def _nt(a, b):
    return lax.dot_general(a, b, (((1,), (1,)), ((), ())), preferred_element_type=F32)


def _tn(a, b):
    return lax.dot_general(a, b, (((0,), (0,)), ((), ())), preferred_element_type=F32)


def _nn_exact(a, b):
    return lax.dot_general(a, b, (((1,), (0,)), ((), ())), preferred_element_type=F32,
                           precision=lax.Precision.HIGHEST)


def _bf(a):
    return a.astype(BF16)


def _params(*sem):
    return pltpu.CompilerParams(dimension_semantics=sem, vmem_limit_bytes=VMEM_LIMIT)


def _full(shape):
    return pl.BlockSpec(shape, lambda i: (0,) * len(shape))


def _position():
    return lax.axis_index("x"), lax.axis_index("y"), lax.axis_index("c")


def _gather_small(in_ref, all_ref, send_sems, recv_sems, local_sem):
    x, y, c = _position()
    me = 4 * x + 2 * y + c
    mine = pltpu.make_async_copy(in_ref, all_ref.at[me], local_sem)
    sends = []
    for k in range(N_DEV - 1):
        fx, fy, fc = (k + 1) >> 2 & 1, (k + 1) >> 1 & 1, (k + 1) & 1
        sends.append(pltpu.make_async_remote_copy(
            src_ref=in_ref, dst_ref=all_ref.at[me],
            send_sem=send_sems.at[k], recv_sem=recv_sems.at[k],
            device_id=(x ^ fx, y ^ fy, c ^ fc), device_id_type=MESH))

    def start():
        mine.start()
        for cp in sends:
            cp.start()

    def wait():
        for k in range(N_DEV - 1):
            fx, fy, fc = (k + 1) >> 2 & 1, (k + 1) >> 1 & 1, (k + 1) & 1
            src_dev = 4 * (x ^ fx) + 2 * (y ^ fy) + (c ^ fc)
            pltpu.make_async_remote_copy(
                src_ref=in_ref, dst_ref=all_ref.at[src_dev],
                send_sem=send_sems.at[k], recv_sem=recv_sems.at[k],
                device_id=(x, y, c), device_id_type=MESH).wait_recv()
        for cp in sends:
            cp.wait_send()
        mine.wait()

    return start, wait


SMALL_SEMS = [pltpu.SemaphoreType.DMA((N_DEV - 1,)), pltpu.SemaphoreType.DMA((N_DEV - 1,)),
              pltpu.SemaphoreType.DMA]
VMEM_SPEC = pl.BlockSpec(memory_space=pltpu.VMEM)


def _other_chips(x, y):
    return [(1 - x, y), (x, 1 - y), (1 - x, 1 - y)]


def _any_specs(n):
    return [pl.BlockSpec(memory_space=pl.ANY)] * n


def _halves(rows, c):
    half = rows // 2
    return pl.ds(c * half, half), pl.ds((1 - c) * half, half)


CAST_ROWS = 256


def _gather_copy(out_ref, send_sems, recv_sems, k, quarter, half, to, src=None):
    dst = out_ref.at[quarter, half]
    return pltpu.make_async_remote_copy(
        src_ref=dst if src is None else src, dst_ref=dst,
        send_sem=send_sems.at[k], recv_sem=recv_sems.at[k], device_id=to, device_id_type=MESH)


SMALL_IN_ROWS = 24


def allgather_weights(quarters, exchange, smalls):
    n = len(quarters)
    shapes = [w.shape for w in quarters]
    moved = [i for i in range(n) if exchange[i]]

    def body(*refs):
        w_refs, (gkb_ref, hnw_ref, gb_ref, gkw_ref) = refs[:n], refs[n:n + 4]
        out_refs, (bgk_ref, hw_ref, gbias_ref, wgk_ref) = refs[n + 4:2 * n + 4], refs[2 * n + 4:2 * n + 8]
        refs = refs[2 * n + 8:]
        f32_bufs, bf_bufs = refs[:n], refs[n:2 * n]
        send_sems, recv_sems, local_sems, small_ref, small_all_ref = refs[2 * n:2 * n + 5]
        small_ref[...] = jnp.zeros_like(small_ref)
        small_ref[0:1, :] = gkb_ref[...]
        small_ref[1:2, 0:64] = hnw_ref[...]
        small_ref[2:2 + GROUPS, 0:64] = gb_ref[...]
        small_ref[8:8 + GATE_RANK, :] = gkw_ref[...]
        start_small, wait_small = _gather_small(small_ref, small_all_ref, *refs[2 * n + 5:])
        start_small()
        x, y, c = _position()
        q = 2 * x + y
        sibling = (x, y, 1 - c)
        chips = _other_chips(x, y)

        def copy(k, i, quarter, half, to, src=None):
            return _gather_copy(out_refs[i], send_sems, recv_sems, k * n + i, quarter, half, to, src)

        loads = [pltpu.make_async_copy(w_refs[i], f32_bufs[i], local_sems.at[i]) for i in range(n)]
        for cp in loads:
            cp.start()
        keeps, sends = [], []
        for i in range(n):
            loads[i].wait()
            for r0 in range(0, shapes[i][0], CAST_ROWS):
                bf_bufs[i][r0:r0 + CAST_ROWS, :] = _bf(f32_bufs[i][r0:r0 + CAST_ROWS, :])
            keep = pltpu.make_async_copy(bf_bufs[i], out_refs[i].at[q], local_sems.at[n + i])
            keep.start()
            keeps.append(keep)
            if not exchange[i]:
                continue
            mine, _ = _halves(shapes[i][0], c)
            for j, chip in enumerate(chips):
                cp = copy(j, i, q, mine, (*chip, c), src=bf_bufs[i].at[mine])
                cp.start()
                sends.append(cp)
        for j, chip in enumerate(chips):
            qj = 2 * chip[0] + chip[1]
            for i in moved:
                mine, _ = _halves(shapes[i][0], c)
                copy(j, i, qj, mine, (x, y, c)).wait_recv()
                cp = copy(3 + j, i, qj, mine, sibling)
                cp.start()
                sends.append(cp)
        for j, chip in enumerate(chips):
            qj = 2 * chip[0] + chip[1]
            for i in moved:
                _, other = _halves(shapes[i][0], c)
                copy(3 + j, i, qj, other, (x, y, c)).wait_recv()
        wait_small()
        wgk_ref[...] = jnp.zeros_like(wgk_ref)
        for j in range(N_CHIPS):
            block = small_all_ref.at[2 * j]
            bgk_ref[:, 128 * j:128 * (j + 1)] = block[0:1, :]
            for h in range(HEADS):
                hw_ref[:, HEAD_V * h + 64 * j:HEAD_V * h + 64 * (j + 1)] = block[1:2, 0:64]
            for g in range(GROUPS):
                gbias_ref[:, GROUP_DIM * g + 64 * j:GROUP_DIM * g + 64 * (j + 1)] = block[2 + g:3 + g, 0:64]
            wgk_ref[0:GATE_RANK, 128 * j:128 * (j + 1)] = _bf(block[8:8 + GATE_RANK, :])
        for cp in sends:
            cp.wait_send()
        for cp in keeps:
            cp.wait()

    outs = pl.pallas_call(
        body, name="allgather_weights",
        out_shape=[jax.ShapeDtypeStruct((N_CHIPS, *s), BF16) for s in shapes]
                  + [jax.ShapeDtypeStruct((1, KEY_W), F32), jax.ShapeDtypeStruct((1, D), F32),
                     jax.ShapeDtypeStruct((1, D), F32), jax.ShapeDtypeStruct((RANK_PAD, KEY_W), BF16)],
        in_specs=_any_specs(n) + [VMEM_SPEC] * 4, out_specs=_any_specs(n) + [VMEM_SPEC] * 4,
        scratch_shapes=([pltpu.VMEM(s, F32) for s in shapes] + [pltpu.VMEM(s, BF16) for s in shapes]
                        + [pltpu.SemaphoreType.DMA((6 * n,)), pltpu.SemaphoreType.DMA((6 * n,)),
                           pltpu.SemaphoreType.DMA((2 * n,)), pltpu.VMEM((SMALL_IN_ROWS, 128), F32),
                           pltpu.VMEM((N_DEV, SMALL_IN_ROWS, 128), F32)] + SMALL_SEMS),
        compiler_params=pltpu.CompilerParams(vmem_limit_bytes=VMEM_LIMIT),
    )(*quarters, *smalls)
    return outs[:n], outs[n:]


def _scatter_copies(b_refs, got_refs, send_sems, recv_sems):
    n = len(b_refs)
    x, y, c = _position()
    copies = []
    for j, chip in enumerate(_other_chips(x, y)):
        qj = 2 * chip[0] + chip[1]
        for i in range(n):
            copies.append(pltpu.make_async_remote_copy(
                src_ref=b_refs[i].at[qj], dst_ref=got_refs[i].at[j],
                send_sem=send_sems.at[j * n + i], recv_sem=recv_sems.at[j * n + i],
                device_id=(*chip, c), device_id_type=MESH))
    return copies


def _scatter_shapes(chip_sums):
    return [jax.ShapeDtypeStruct((N_CHIPS - 1, *b.shape[1:]), BF16) for b in chip_sums]


ADD_ROWS = 512
ADD_HALVES_ROWS = 128


def _spans(counts):
    starts, total = [], 0
    for count in counts:
        starts.append(total)
        total += count
    return starts, total


def _local_step(t, start, count):
    return jnp.clip(t - start, 0, count - 1)


def add_halves(grads, place, name, scatter=False):
    n = len(grads)
    whole = [len(g.shape) == 2 for g in grads]
    halves = [g.shape[-2] // 2 for g in grads]
    cols = [GLA_IN_QUARTER if w else g.shape[-1] for g, w in zip(grads, whole)]
    rbs = [min(ADD_HALVES_ROWS, h) for h in halves]
    counts = [h // rb for h, rb in zip(halves, rbs)]
    starts, total = _spans(counts)
    half_shapes = [(*g.shape[:-2], h, g.shape[-1]) for g, h in zip(grads, halves)]

    def rows_of(ref, i, start):
        return ref.at[pl.ds(start, rbs[i])] if whole[i] else ref.at[:, pl.ds(start, rbs[i])]

    def body(place_ref, *refs):
        a_refs, o_refs = refs[:n], refs[n:2 * n]
        f_refs, h_refs = refs[2 * n:3 * n], refs[3 * n:4 * n]
        send_refs, their_refs, rest = refs[4 * n:5 * n], refs[5 * n:6 * n], refs[6 * n:]
        send_sems, recv_sems = rest[:2]
        t = pl.program_id(0)
        q = place_ref[1]
        x, y, c = _position()
        sum_refs = rest[2:2 + n] if scatter else h_refs

        def to_owners(i, k):
            out_sems, in_sems = rest[2 + n:]
            rows = pl.ds(k * rbs[i], rbs[i])
            return [pltpu.make_async_remote_copy(
                src_ref=sum_refs[i].at[2 * chip[0] + chip[1], rows], dst_ref=h_refs[i].at[j, rows],
                send_sem=out_sems.at[3 * (starts[i] + k) + j], recv_sem=in_sems.at[3 * (starts[i] + k) + j],
                device_id=(*chip, c), device_id_type=MESH) for j, chip in enumerate(_other_chips(x, y))]

        copies = [[pltpu.make_async_remote_copy(
            src_ref=rows_of(send_refs[i], i, k * rbs[i]), dst_ref=rows_of(their_refs[i], i, k * rbs[i]),
            send_sem=send_sems.at[starts[i] + k], recv_sem=recv_sems.at[starts[i] + k],
            device_id=(x, y, 1 - c), device_id_type=MESH) for k in range(counts[i])] for i in range(n)]

        for i in range(n):
            for k in range(counts[i]):
                @pl.when(t == starts[i] + k)
                def _(i=i, k=k):
                    rows_of(send_refs[i], i, k * rbs[i])[...] = _bf(o_refs[i][...])
                    copies[i][k].start()

        for i in range(n):
            for k in range(counts[i]):
                @pl.when(t == starts[i] + k + 1)
                def _(i=i, k=k):
                    copies[i][k].wait_recv()
                    b_ref = rows_of(their_refs[i], i, k * rbs[i])
                    h_ref = sum_refs[i].at[:, pl.ds(k * rbs[i], rbs[i])] if scatter else h_refs[i]
                    if not whole[i]:
                        h_ref[...] = _bf(a_refs[i][...] + b_ref[...].astype(F32))
                        f_refs[i][...] = a_refs[i][q] + b_ref[q].astype(F32)
                    else:
                        total_i = a_refs[i][...] + b_ref[...].astype(F32)
                        for k4 in range(N_CHIPS):
                            piece = total_i[:, k4 * cols[i]:(k4 + 1) * cols[i]]
                            h_ref[k4] = _bf(piece)

                            @pl.when(q == k4)
                            def _():
                                f_refs[i][...] = piece
                    if scatter:
                        for cp in to_owners(i, k):
                            cp.start()

        @pl.when(t == total)
        def _():
            for of_matrix in copies:
                for cp in of_matrix:
                    cp.wait_send()
            if scatter:
                for i in range(n):
                    for k in range(counts[i]):
                        for cp in to_owners(i, k):
                            cp.wait()

    def specs(i):
        sent = lambda t: _local_step(t, starts[i], counts[i])
        added = lambda t: _local_step(t - 1, starts[i], counts[i])
        by_quarter = (N_CHIPS, rbs[i], cols[i])
        block = (rbs[i], grads[i].shape[-1]) if whole[i] else by_quarter
        lead = () if whole[i] else (0,)
        mine = pl.BlockSpec(block, lambda t, place: (*lead, place[0] * counts[i] + added(t), 0))
        other = pl.BlockSpec(block, lambda t, place: (*lead, (1 - place[0]) * counts[i] + sent(t), 0))
        sums = pl.BlockSpec(by_quarter, lambda t, place: (0, added(t), 0))
        own = pl.BlockSpec(by_quarter[1:], lambda t, place: (added(t), 0))
        return mine, other, own, sums

    all_specs = [specs(i) for i in range(n)]
    sum_shapes = [(N_CHIPS, h, cl) for h, cl in zip(halves, cols)]
    scratch = [pltpu.VMEM(sh, BF16) for sh in half_shapes] + [pltpu.VMEM(sh, BF16) for sh in half_shapes]
    scratch += [pltpu.SemaphoreType.DMA((total,)), pltpu.SemaphoreType.DMA((total,))]
    if scatter:
        scratch += [pltpu.VMEM(sh, BF16) for sh in sum_shapes]
        scratch += [pltpu.SemaphoreType.DMA((3 * total,)), pltpu.SemaphoreType.DMA((3 * total,))]
    outs = pl.pallas_call(
        body, name=name,
        grid_spec=pltpu.PrefetchScalarGridSpec(
            num_scalar_prefetch=1, grid=(total + 1,),
            in_specs=[sp[0] for sp in all_specs] + [sp[1] for sp in all_specs],
            out_specs=[sp[2] for sp in all_specs] + (_any_specs(n) if scatter else [sp[3] for sp in all_specs]),
            scratch_shapes=scratch),
        out_shape=[jax.ShapeDtypeStruct((h, cl), F32) for h, cl in zip(halves, cols)]
                  + [jax.ShapeDtypeStruct((N_CHIPS - 1 if scatter else N_CHIPS, *sh[1:]), BF16) for sh in sum_shapes],
        compiler_params=_params("arbitrary"),
    )(place, *grads, *grads)
    return list(zip(outs[:n], outs[n:]))


SMALL_SUM_ROWS = 16


def join_halves(owns, gots, small_pool, small_gla, small_top, g_gk_pad):
    n = len(owns)
    shapes = [g.shape for g in gots]
    rbs = [min(ADD_ROWS, sh[1]) for sh in shapes]
    counts = [sh[1] // rb for sh, rb in zip(shapes, rbs)]
    starts, total = _spans(counts)

    def body(*refs):
        o_refs, g_refs = refs[:n], refs[n:2 * n]
        pool_ref, gla_ref, top_ref, gk_ref = refs[2 * n:2 * n + 4]
        out_refs, total_ref = refs[2 * n + 4:3 * n + 4], refs[3 * n + 4]
        sum_refs = refs[3 * n + 5:4 * n + 5]
        local_sems, send_sems, recv_sems, all_ref, small_ref = refs[4 * n + 5:4 * n + 10]
        t = pl.program_id(0)
        x, y, c = _position()
        start_small, wait_small = _gather_small(small_ref, all_ref, *refs[4 * n + 10:])

        def copies(i, k):
            src = sum_refs[i].at[pl.ds(k * rbs[i], rbs[i])]
            rows = pl.ds(c * shapes[i][1] + k * rbs[i], rbs[i])
            return (pltpu.make_async_copy(src, out_refs[i].at[rows], local_sems.at[starts[i] + k]),
                    pltpu.make_async_remote_copy(
                        src_ref=src, dst_ref=out_refs[i].at[rows],
                        send_sem=send_sems.at[starts[i] + k], recv_sem=recv_sems.at[starts[i] + k],
                        device_id=(x, y, 1 - c), device_id_type=MESH))

        @pl.when(t == 0)
        def _():
            small_ref[0:3, :] = pool_ref[0:3, :]
            small_ref[3:5, :] = gla_ref[0:2, :]
            small_ref[5:8, :] = top_ref[0:3, :]
            for r in range(GATE_RANK):
                small_ref[8 + r // 2:9 + r // 2, (r % 2) * KEY_W:(r % 2 + 1) * KEY_W] = gk_ref[r:r + 1, :]
            start_small()

        for i in range(n):
            for k in range(counts[i]):
                @pl.when(t == starts[i] + k)
                def _(i=i, k=k):
                    total_i = o_refs[i][...]
                    for j in range(N_CHIPS - 1):
                        total_i = total_i + g_refs[i][j].astype(F32)
                    sum_refs[i][k * rbs[i]:(k + 1) * rbs[i], :] = total_i
                    for cp in copies(i, k):
                        cp.start()

        @pl.when(t == total - 1)
        def _():
            wait_small()
            small_total = all_ref[0]
            for dev in range(1, N_DEV):
                small_total = small_total + all_ref[dev]
            total_ref[...] = small_total
            for i in range(n):
                for k in range(counts[i]):
                    for cp in copies(i, k):
                        cp.wait()

    def specs(i):
        rb, cols = rbs[i], shapes[i][2]
        step = lambda t: _local_step(t, starts[i], counts[i])
        return (pl.BlockSpec((rb, cols), lambda t: (step(t), 0)),
                pl.BlockSpec((N_CHIPS - 1, rb, cols), lambda t: (0, step(t), 0)))

    all_specs = [specs(i) for i in range(n)]
    outs = pl.pallas_call(
        body, name="join_halves", grid=(total,),
        out_shape=[jax.ShapeDtypeStruct((2 * sh[1], sh[2]), F32) for sh in shapes]
                  + [jax.ShapeDtypeStruct((SMALL_SUM_ROWS, D), F32)],
        in_specs=[sp[0] for sp in all_specs] + [sp[1] for sp in all_specs] + [VMEM_SPEC] * 4,
        out_specs=_any_specs(n) + [VMEM_SPEC],
        scratch_shapes=[pltpu.VMEM(sh[1:], F32) for sh in shapes]
                       + [pltpu.SemaphoreType.DMA((total,)), pltpu.SemaphoreType.DMA((total,)),
                          pltpu.SemaphoreType.DMA((total,)),
                          pltpu.VMEM((N_DEV, SMALL_SUM_ROWS, D), F32), pltpu.VMEM((SMALL_SUM_ROWS, D), F32)]
                       + SMALL_SEMS,
        compiler_params=_params("arbitrary"),
    )(*owns, *gots, small_pool, small_gla, small_top, g_gk_pad)
    return outs[:n], outs[n]


def _adam_math(w, g, m, v):
    m = ADAM_B1 * m + (1.0 - ADAM_B1) * g
    v = ADAM_B2 * v + (1.0 - ADAM_B2) * (g * g)
    m_hat = m / (1.0 - ADAM_B1 ** ADAM_STEP)
    v_hat = v / (1.0 - ADAM_B2 ** ADAM_STEP)
    delta = -ADAM_LR * (m_hat / (jnp.sqrt(v_hat) + ADAM_EPS) + ADAM_WD * w)
    return delta, m, v


ADAM_BLOCK_BYTES = 2 ** 19
ADAM_MOST_STEPS = 8


def adamw(params, name):
    n = len(params)
    shapes = [p[0].shape for p in params]

    def tile_rows(shape):
        rows, cols = shape[0], shape[-1]
        aligned = 1 if len(shape) == 3 else 8
        divisors = [t for t in range(aligned, rows + 1, aligned) if rows % t == 0]
        tile = max(t for t in divisors if t * cols * 4 <= ADAM_BLOCK_BYTES)
        if rows // tile > ADAM_MOST_STEPS:
            tile = min(t for t in divisors if rows // t <= ADAM_MOST_STEPS)
        return tile

    tiles = [tile_rows(sh) for sh in shapes]
    counts = [sh[0] // tl for sh, tl in zip(shapes, tiles)]
    starts, total = _spans(counts)

    def body(*refs):
        ins, outs = refs[:4 * n], refs[4 * n:]
        t = pl.program_id(0)
        for i in range(n):
            @pl.when((t >= starts[i]) & (t < starts[i] + counts[i]))
            def _(i=i):
                w_ref, g_ref, m_ref, v_ref = ins[4 * i:4 * i + 4]
                d, nm, nv = _adam_math(w_ref[...], g_ref[...], m_ref[...], v_ref[...])
                outs[3 * i][...] = d
                outs[3 * i + 1][...] = nm
                outs[3 * i + 2][...] = nv

    def spec(i):
        block = (tiles[i],) + shapes[i][1:]
        zeros = (0,) * (len(block) - 1)
        return pl.BlockSpec(block, lambda t: (_local_step(t, starts[i], counts[i]),) + zeros)

    outs = pl.pallas_call(
        body, name=name, grid=(total,),
        out_shape=[jax.ShapeDtypeStruct(sh, F32) for sh in shapes for _ in range(3)],
        in_specs=[spec(i) for i in range(n) for _ in range(4)],
        out_specs=[spec(i) for i in range(n) for _ in range(3)],
        compiler_params=_params("arbitrary"),
    )(*[a for p in params for a in p])
    return [tuple(outs[3 * i:3 * i + 3]) for i in range(n)]


def adamw_small(params, total, place):
    n = len(params)

    def cut_gradients(total_ref, q, g_refs):
        g_norm, g_group_b, g_scale, g_gk_w, g_gk_b, g_head_norm, g_final = g_refs
        g_norm[0:1, :] = total_ref[0:1, :]
        g_norm[1:2, :] = total_ref[3:4, :]
        g_scale[...] = total_ref[1:2, :]
        g_final[...] = total_ref[5:6, :]
        g_gk_b[...] = total_ref[4:5, pl.ds(pl.multiple_of(q * 128, 128), 128)]
        for r in range(GATE_RANK):
            lanes = pl.ds(pl.multiple_of((r % 2) * KEY_W + q * 128, 128), 128)
            g_gk_w[r:r + 1, :] = total_ref[8 + r // 2:9 + r // 2, lanes]
        for k in range(N_CHIPS):
            @pl.when(q == k)
            def _(k=k):
                g_head_norm[...] = total_ref[6:7, 64 * k:64 * (k + 1)]
                for g in range(GROUPS):
                    g_group_b[g:g + 1, :] = total_ref[2:3, GROUP_DIM * g + 64 * k:GROUP_DIM * g + 64 * (k + 1)]

    def body(place_ref, total_ref, *refs):
        ins, outs = refs[:3 * n], refs[3 * n:]
        cut_gradients(total_ref, place_ref[1], outs[0::4])
        for k in range(n):
            w_ref, m_ref, v_ref = ins[3 * k:3 * k + 3]
            d, nm, nv = _adam_math(w_ref[...], outs[4 * k][...], m_ref[...], v_ref[...])
            outs[4 * k + 1][...] = d
            outs[4 * k + 2][...] = nm
            outs[4 * k + 3][...] = nv

    flat = [a for p in params for a in p]
    outs = pl.pallas_call(
        body, name="adamw_small",
        out_shape=[jax.ShapeDtypeStruct(p[0].shape, F32) for p in params for _ in range(4)],
        in_specs=[pl.BlockSpec(memory_space=pltpu.SMEM)] + [VMEM_SPEC] * (1 + 3 * n),
        out_specs=[VMEM_SPEC] * (4 * n),
    )(place, total, *flat)
    return [tuple(outs[4 * k:4 * k + 4]) for k in range(n)]


def matmul_tn(a, b, name, tile_n, by_column_tile=False, chip_sums=()):
    s, m = a.shape
    n = b.shape[1]
    n_sums = len(chip_sums)
    steps = n // tile_n
    if by_column_tile:
        out_shape = jax.ShapeDtypeStruct((steps, m, tile_n), F32)
        out_spec = pl.BlockSpec((None, m, tile_n), lambda j: (j, 0, 0))
    else:
        out_shape = jax.ShapeDtypeStruct((m, n), F32)
        out_spec = pl.BlockSpec((m, tile_n), lambda j: (0, j))

    def body(a_ref, b_ref, *rest):
        sum_refs, out_ref, got_refs = rest[:n_sums], rest[n_sums], rest[n_sums + 1:2 * n_sums + 1]
        j = pl.program_id(0)
        copies = _scatter_copies(sum_refs, got_refs, *rest[2 * n_sums + 1:]) if n_sums else []

        @pl.when(j == 0)
        def _():
            for cp in copies:
                cp.start()

        out_ref[...] = _tn(a_ref[...], b_ref[...])

        @pl.when(j == steps - 1)
        def _():
            for cp in copies:
                cp.wait()

    outs = pl.pallas_call(
        body, name=name, grid=(steps,),
        out_shape=[out_shape] + _scatter_shapes(chip_sums),
        in_specs=[_full((s, m)), pl.BlockSpec((s, tile_n), lambda j: (0, j))] + _any_specs(n_sums),
        out_specs=[out_spec] + _any_specs(n_sums),
        scratch_shapes=[pltpu.SemaphoreType.DMA((3 * n_sums,)), pltpu.SemaphoreType.DMA((3 * n_sums,))]
                       if n_sums else [],
        compiler_params=_params("arbitrary"),
    )(a, b, *chip_sums)
    return outs[0], outs[1:]


ROW_TILE = 512


def _row_index(tile, rows):
    return tile * rows + lax.broadcasted_iota(jnp.int32, (rows, 1), 0)


def _inverse_counts(t_glob):
    return [1.0 / jnp.minimum(t_glob + 1, w).astype(F32) for w in POOL_WINDOWS]


def _sigmoid(z):
    return 1.0 / (1.0 + jnp.exp(-z))


def _trailing_sums(src, tmp, cols, window, rows):
    bufs = (src, tmp)
    span, level, start = 1, 0, 0
    while span < window:
        start += 8
        a, b = bufs[level % 2], bufs[(level + 1) % 2]
        n = HALO + rows - start
        b[start:start + n, cols] = a[start:start + n, cols] + a[start - span:start - span + n, cols]
        span, level = 2 * span, level + 1
    return bufs[level % 2][HALO:HALO + rows, cols]


def _leading_sums(src, tmp, cols, window, rows):
    bufs = (src, tmp)
    span, level, n = 1, 0, rows + HALO
    while span < window:
        n -= 8
        a, b = bufs[level % 2], bufs[(level + 1) % 2]
        b[0:n, cols] = a[0:n, cols] + a[span:span + n, cols]
        span, level = 2 * span, level + 1
    return bufs[level % 2][0:rows, cols]


def gather_in_background(step, last, out_refs, send_sems, recv_sems, finish):
    n = len(out_refs)
    x, y, c = _position()
    q = 2 * x + y
    chips = _other_chips(x, y)

    def copy(k, i, quarter, half, to):
        return _gather_copy(out_refs[i], send_sems, recv_sems, k * n + i, quarter, half, to)

    if not finish:
        @pl.when(step == 0)
        def _():
            for i in range(n):
                mine, _ = _halves(out_refs[i].shape[1], c)
                for j, chip in enumerate(chips):
                    copy(j, i, q, mine, (*chip, c)).start()

        @pl.when(step == last)
        def _():
            for j, chip in enumerate(chips):
                qj = 2 * chip[0] + chip[1]
                for i in range(n):
                    mine, _ = _halves(out_refs[i].shape[1], c)
                    copy(j, i, qj, mine, (x, y, c)).wait_recv()
                    copy(3 + j, i, qj, mine, (x, y, 1 - c)).start()
        return

    @pl.when(step == last)
    def _():
        for j, chip in enumerate(chips):
            qj = 2 * chip[0] + chip[1]
            for i in range(n):
                mine, other = _halves(out_refs[i].shape[1], c)
                copy(3 + j, i, qj, other, (x, y, c)).wait_recv()
                copy(j, i, q, mine, (x, y, c)).wait_send()
                copy(3 + j, i, qj, mine, (x, y, c)).wait_send()


def pool_forward(x, w0, wpi, gw, gb, scale, wpo, later):
    s = x.shape[0]
    ts = ROW_TILE
    nt = s // ts
    assert nt >= 2
    n_later = len(later)

    def body(x_ref, w0_ref, wpi_ref, gw_ref, gb_ref, sc_ref, wpo_ref, *rest):
        rest = rest[n_later:]
        h1_ref, pooled_ref, gt_ref, n0_ref = rest[:4]
        later_refs = rest[4:4 + n_later]
        ubuf, tbuf, hist, send_sems, recv_sems = rest[4 + n_later:]
        i = pl.program_id(0)
        gather_in_background(i, nt - 1, later_refs, send_sems, recv_sems, finish=False)
        xv = x_ref[...]
        r = lax.rsqrt(jnp.mean(xv * xv, axis=-1, keepdims=True) + EPS)
        n0 = _bf(xv * r * w0_ref[...])
        n0_ref[...] = n0
        u = jnp.concatenate([_nn(n0, wpi_ref[0]), _nn(n0, wpi_ref[1])], axis=-1)
        gt = jnp.concatenate([_nn(n0, wpi_ref[2]), _nn(n0, wpi_ref[3])], axis=-1)
        gt_ref[...] = gt

        @pl.when(i == 0)
        def _():
            hist[...] = jnp.zeros_like(hist)

        ubuf[0:HALO, :] = hist[...]
        ubuf[HALO:HALO + ts, :] = u
        hist[...] = u[ts - HALO:, :]
        inv = _inverse_counts(_row_index(i, ts))
        mixed = []
        for g, w in enumerate(POOL_WINDOWS):
            cols = slice(g * GROUP_DIM, (g + 1) * GROUP_DIM)
            pooled = _bf(_trailing_sums(ubuf, tbuf, cols, w, ts) * inv[g] - u[:, cols])
            pooled_ref[:, cols] = pooled
            mixed.append(_nn(pooled, gw_ref[g]))
        mixed = jnp.concatenate(mixed, axis=-1) + gb_ref[...]
        y = mixed * sc_ref[...] * (gt * _sigmoid(gt))
        h1_ref[...] = xv + _nn(_bf(y), wpo_ref[...])
        gather_in_background(i, nt - 1, later_refs, send_sems, recv_sems, finish=True)

    row = lambda cols: pl.BlockSpec((ts, cols), lambda i: (i, 0))
    outs = pl.pallas_call(
        body, name="pool_forward", grid=(nt,),
        out_shape=[jax.ShapeDtypeStruct((s, D), F32), jax.ShapeDtypeStruct((s, D), BF16),
                   jax.ShapeDtypeStruct((s, D), F32), jax.ShapeDtypeStruct((s, D), BF16)]
                  + [jax.ShapeDtypeStruct(a.shape, a.dtype) for a in later],
        in_specs=[row(D), _full((1, D)), _full((N_CHIPS, D, D // 2)), _full((GROUPS, GROUP_DIM, GROUP_DIM)),
                  _full((1, D)), _full((1, D)), _full((D, D))] + _any_specs(n_later),
        out_specs=[row(D), row(D), row(D), row(D)] + _any_specs(n_later),
        input_output_aliases={7 + k: 4 + k for k in range(n_later)},
        scratch_shapes=[pltpu.VMEM((HALO + ts, D), F32), pltpu.VMEM((HALO + ts, D), F32),
                        pltpu.VMEM((HALO, D), F32),
                        pltpu.SemaphoreType.DMA((6 * n_later,)), pltpu.SemaphoreType.DMA((6 * n_later,))],
        compiler_params=_params("arbitrary"),
    )(x, w0, wpi, gw, gb, scale, wpo, *later)
    return outs[:4], outs[4:]


def pool_backward(x, dh1, pooled, gt, w0, wpi, gw, gb, scale, wpo, chip_sums):
    s = x.shape[0]
    ts = ROW_TILE
    nt = s // ts
    n_sums = len(chip_sums)

    def body(x_ref, dh1_ref, pooled_ref, gt_ref, w0_ref, wpi_ref, gw_ref, gb_ref, sc_ref, wpo_ref, *rest):
        sum_refs, rest = rest[:n_sums], rest[n_sums:]
        dx_ref, dproj_ref, gpo_ref, ggw_ref, small_ref = rest[:5]
        got_refs = rest[5:5 + n_sums]
        ebuf, tbuf, ahead, send_sems, recv_sems = rest[5 + n_sums:]
        i = pl.program_id(0)
        copies = _scatter_copies(sum_refs, got_refs, send_sems, recv_sems)

        @pl.when(i == 0)
        def _():
            for cp in copies:
                cp.start()

        @pl.when(i == 0)
        def _():
            gpo_ref[...] = jnp.zeros_like(gpo_ref)
            ggw_ref[...] = jnp.zeros_like(ggw_ref)
            small_ref[...] = jnp.zeros_like(small_ref)
            ahead[...] = jnp.zeros_like(ahead)

        dh1 = dh1_ref[...]
        dh1_bf = _bf(dh1)
        gt = gt_ref[...]
        sc = sc_ref[...]
        dy = _nt(dh1_bf, wpo_ref[...])
        pooled_bf = []
        mixed = []
        for g in range(GROUPS):
            cols = slice(g * GROUP_DIM, (g + 1) * GROUP_DIM)
            pb = pooled_ref[:, cols]
            pooled_bf.append(pb)
            mixed.append(_nn(pb, gw_ref[g]))
        mixed = jnp.concatenate(mixed, axis=-1) + gb_ref[...]
        sg = _sigmoid(gt)
        silu = gt * sg
        gpo_ref[...] += _tn(_bf(mixed * sc * silu), dh1_bf)
        dmixed = dy * sc * silu
        dgt = dy * mixed * sc * (sg * (1.0 + gt * (1.0 - sg)))
        dproj_ref[:, D:] = _bf(dgt)
        small_ref[1:2, :] += jnp.sum(dy * mixed * silu, axis=0, keepdims=True)
        small_ref[2:3, :] += jnp.sum(dmixed, axis=0, keepdims=True)

        inv = _inverse_counts(_row_index(nt - 1 - i, ts))
        ebuf[ts:ts + HALO, :] = ahead[...]
        dpooled = []
        for g in range(GROUPS):
            cols = slice(g * GROUP_DIM, (g + 1) * GROUP_DIM)
            dm = _bf(dmixed[:, cols])
            ggw_ref[g] += _tn(pooled_bf[g], dm)
            dp = _nt(dm, gw_ref[g])
            dpooled.append(dp)
            ebuf[0:ts, cols] = dp * inv[g]
        ahead[...] = ebuf[0:HALO, :]
        du = []
        for g, w in enumerate(POOL_WINDOWS):
            cols = slice(g * GROUP_DIM, (g + 1) * GROUP_DIM)
            du.append(_leading_sums(ebuf, tbuf, cols, w, ts) - dpooled[g])
        du = _bf(jnp.concatenate(du, axis=-1))
        dproj_ref[:, :D] = du
        dgt_bf = _bf(dgt)
        half = D // 2
        dn0 = (_nt(du[:, :half], wpi_ref[0]) + _nt(du[:, half:], wpi_ref[1])
               + _nt(dgt_bf[:, :half], wpi_ref[2]) + _nt(dgt_bf[:, half:], wpi_ref[3]))

        xv = x_ref[...]
        r = lax.rsqrt(jnp.mean(xv * xv, axis=-1, keepdims=True) + EPS)
        xhat = xv * r
        small_ref[0:1, :] += jnp.sum(dn0 * xhat, axis=0, keepdims=True)
        dxh = dn0 * w0_ref[...]
        dx_ref[...] = dh1 + r * (dxh - xhat * jnp.mean(dxh * xhat, axis=-1, keepdims=True))

        @pl.when(i == nt - 1)
        def _():
            for cp in copies:
                cp.wait()

    row = lambda cols: pl.BlockSpec((ts, cols), lambda i: (nt - 1 - i, 0))
    outs = pl.pallas_call(
        body, name="pool_backward", grid=(nt,),
        out_shape=[jax.ShapeDtypeStruct((s, D), F32), jax.ShapeDtypeStruct((s, 2 * D), BF16),
                   jax.ShapeDtypeStruct((D, D), F32),
                   jax.ShapeDtypeStruct((GROUPS, GROUP_DIM, GROUP_DIM), F32),
                   jax.ShapeDtypeStruct((8, D), F32)] + _scatter_shapes(chip_sums),
        in_specs=[row(D), row(D), row(D), row(D), _full((1, D)), _full((N_CHIPS, D, D // 2)),
                  _full((GROUPS, GROUP_DIM, GROUP_DIM)), _full((1, D)), _full((1, D)), _full((D, D))]
                 + _any_specs(n_sums),
        out_specs=[row(D), row(2 * D), _full((D, D)), _full((GROUPS, GROUP_DIM, GROUP_DIM)), _full((8, D))]
                  + _any_specs(n_sums),
        scratch_shapes=[pltpu.VMEM((ts + HALO, D), F32), pltpu.VMEM((ts + HALO, D), F32),
                        pltpu.VMEM((HALO, D), F32),
                        pltpu.SemaphoreType.DMA((3 * n_sums,)), pltpu.SemaphoreType.DMA((3 * n_sums,))],
        compiler_params=_params("arbitrary"),
    )(x, dh1, pooled, gt, w0, wpi, gw, gb, scale, wpo, *chip_sums)
    return outs[:5], outs[5:]


def gla_project(h1, w1, wgi_q, wgk, bgk, later):
    s = h1.shape[0]
    ts = ROW_TILE
    nt = s // ts
    assert nt >= 2
    n_later = len(later)

    def body(h_ref, w1_ref, wq_ref, wgk_ref, bgk_ref, *rest):
        rest = rest[n_later:]
        qk_ref, v_ref, gate_ref, low_ref, cum_ref, n1_ref = rest[:6]
        later_refs = rest[6:6 + n_later]
        send_sems, recv_sems, wgi_ref = rest[6 + n_later:]
        gather_in_background(pl.program_id(0), nt - 1, later_refs, send_sems, recv_sems, finish=False)

        @pl.when(pl.program_id(0) == 0)
        def _():
            _assemble_gla_in(wq_ref, wgi_ref)

        hv = h_ref[...]
        r = lax.rsqrt(jnp.mean(hv * hv, axis=-1, keepdims=True) + EPS)
        n1 = _bf(hv * r * w1_ref[...])
        n1_ref[...] = n1
        qk_ref[...] = _nn(n1, wgi_ref[:, 0:2 * KEY_W])
        v_ref[...] = _bf(_nn(n1, wgi_ref[:, 2 * KEY_W:2 * KEY_W + D]))
        gate_ref[...] = _nn(n1, wgi_ref[:, 2 * KEY_W + D:GLA_MAIN])
        low = _bf(_nn(n1, wgi_ref[:, GLA_MAIN:]))
        low_ref[...] = low
        z = _nn(low, wgk_ref[...]) + bgk_ref[...]
        lg = (jnp.minimum(z, 0.0) - jnp.log(1.0 + jnp.exp(-jnp.abs(z)))) / GATE_NORM
        lower_f = _chunk_masks()[0].astype(F32)
        for r0 in range(0, ts, CHUNK):
            cum_ref[r0:r0 + CHUNK, :] = _nn_exact(lower_f, lg[r0:r0 + CHUNK, :])
        gather_in_background(pl.program_id(0), nt - 1, later_refs, send_sems, recv_sems, finish=True)

    row = lambda cols: pl.BlockSpec((ts, cols), lambda i: (i, 0))
    outs = pl.pallas_call(
        body, name="gla_project", grid=(nt,),
        out_shape=[jax.ShapeDtypeStruct((s, D), F32), jax.ShapeDtypeStruct((s, D), BF16),
                   jax.ShapeDtypeStruct((s, D), F32), jax.ShapeDtypeStruct((s, RANK_PAD), BF16),
                   jax.ShapeDtypeStruct((s, KEY_W), F32), jax.ShapeDtypeStruct((s, D), BF16)]
                  + [jax.ShapeDtypeStruct(a.shape, a.dtype) for a in later],
        in_specs=[row(D), _full((1, D)), _full((N_CHIPS, D, GLA_IN_QUARTER)),
                  _full((RANK_PAD, KEY_W)), _full((1, KEY_W))] + _any_specs(n_later),
        out_specs=[row(D), row(D), row(D), row(RANK_PAD), row(KEY_W), row(D)] + _any_specs(n_later),
        input_output_aliases={5 + k: 6 + k for k in range(n_later)},
        scratch_shapes=[pltpu.SemaphoreType.DMA((6 * n_later,)), pltpu.SemaphoreType.DMA((6 * n_later,)),
                        pltpu.VMEM((D, GLA_MAIN + RANK_PAD), BF16)],
        compiler_params=_params("arbitrary"),
    )(h1, w1, wgi_q, wgk, bgk, *later)
    return outs[:6], outs[6:]


def _assemble_gla_in(wq_ref, wfull):
    pad = jnp.zeros((CAST_ROWS, GLA_MAIN + RANK_PAD - GLA_IN), BF16)
    for r0 in range(0, D, CAST_ROWS):
        rows = slice(r0, r0 + CAST_ROWS)
        wfull[rows, :] = jnp.concatenate([wq_ref[q, rows, :] for q in range(N_CHIPS)] + [pad], axis=1)


GLA_BLOCK = 512
CHUNKS_PER_BLOCK = GLA_BLOCK // CHUNK


def _chunk_masks():
    t = lax.broadcasted_iota(jnp.int32, (CHUNK, CHUNK), 0)
    u = lax.broadcasted_iota(jnp.int32, (CHUNK, CHUNK), 1)
    return t >= u, t <= u


def _gla_chunk_terms(q, cum):
    ep = jnp.exp(cum)
    en = jnp.exp(-cum)
    qs = q * (HEAD_K ** -0.5)
    last = cum[CHUNK - 1:CHUNK, :]
    ed = jnp.exp(last - cum)
    dec = jnp.exp(last)
    return ep, en, qs, ed, dec


def gla_forward(qk, v, cum):
    s = qk.shape[0]
    nb = s // GLA_BLOCK
    nc = s // CHUNK

    def body(q_ref, k_ref, v_ref, cum_ref, o_ref, st_ref, sc_ref, state):
        @pl.when(pl.program_id(0) == 0)
        def _():
            state[...] = jnp.zeros_like(state)

        lower, _ = _chunk_masks()

        def chunk(cc, carry):
            rows = pl.ds(pl.multiple_of(cc * CHUNK, CHUNK), CHUNK)
            for h in range(HEADS):
                kc = slice(h * HEAD_K, (h + 1) * HEAD_K)
                vc = slice(h * HEAD_V, (h + 1) * HEAD_V)
                q = q_ref[rows, kc]
                k = k_ref[rows, kc]
                v = v_ref[rows, vc]
                ep, en, qs, ed, dec = _gla_chunk_terms(q, cum_ref[rows, kc])
                a = _bf(qs * ep)
                fwd = _nt(a, _bf(k * en))
                bwd = _nt(_bf(qs * en), _bf(k * ep))
                scores = _bf(jnp.where(lower, fwd, bwd))
                sc_ref[rows, h * CHUNK:(h + 1) * CHUNK] = scores
                st = state[h]
                st_ref[cc, h] = st
                o_ref[rows, vc] = _nn(scores, v) + _nt(a, _bf(st))
                state[h] = st * dec + _tn(v, _bf(k * ed))
            return carry

        lax.fori_loop(0, CHUNKS_PER_BLOCK, chunk, 0, unroll=4)

    return pl.pallas_call(
        body, name="gla_forward", grid=(nb,),
        out_shape=(jax.ShapeDtypeStruct((s, D), F32),
                   jax.ShapeDtypeStruct((nc, HEADS, HEAD_V, HEAD_K), F32),
                   jax.ShapeDtypeStruct((s, HEADS * CHUNK), BF16)),
        in_specs=[pl.BlockSpec((GLA_BLOCK, KEY_W), lambda i: (i, 0)),
                  pl.BlockSpec((GLA_BLOCK, KEY_W), lambda i: (i, 1)),
                  pl.BlockSpec((GLA_BLOCK, D), lambda i: (i, 0)),
                  pl.BlockSpec((GLA_BLOCK, KEY_W), lambda i: (i, 0))],
        out_specs=(pl.BlockSpec((GLA_BLOCK, D), lambda i: (i, 0)),
                   pl.BlockSpec((CHUNKS_PER_BLOCK, HEADS, HEAD_V, HEAD_K), lambda i: (i, 0, 0, 0)),
                   pl.BlockSpec((GLA_BLOCK, HEADS * CHUNK), lambda i: (i, 0))),
        scratch_shapes=[pltpu.VMEM((HEADS, HEAD_V, HEAD_K), F32)],
        compiler_params=_params("arbitrary"),
    )(qk, qk, v, cum)


def gla_backward(qk, v, cum, do, states, scores):
    s = qk.shape[0]
    nb = s // GLA_BLOCK

    def body(q_ref, k_ref, v_ref, cum_ref, do_ref, st_ref, sc_ref, dq_ref, dk_ref, dv_ref, dcum_ref, dstate):
        @pl.when(pl.program_id(0) == 0)
        def _():
            dstate[...] = jnp.zeros_like(dstate)

        lower, _ = _chunk_masks()
        is_last = lax.broadcasted_iota(jnp.int32, (CHUNK, HEAD_K), 0) == CHUNK - 1

        def chunk(step, carry):
            cc = CHUNKS_PER_BLOCK - 1 - step
            rows = pl.ds(pl.multiple_of(cc * CHUNK, CHUNK), CHUNK)
            for h in range(HEADS):
                kc = slice(h * HEAD_K, (h + 1) * HEAD_K)
                vc = slice(h * HEAD_V, (h + 1) * HEAD_V)
                q = q_ref[rows, kc]
                k = k_ref[rows, kc]
                v = v_ref[rows, vc]
                do_c = do_ref[rows, vc]
                ep, en, qs, ed, dec = _gla_chunk_terms(q, cum_ref[rows, kc])
                a = _bf(qs * ep)
                b = _bf(k * en)
                c = _bf(qs * en)
                dk_dec = _bf(k * ep)
                kd = _bf(k * ed)
                scores = sc_ref[rows, h * CHUNK:(h + 1) * CHUNK]
                st = st_ref[cc, h]
                dst = dstate[h]
                dst_bf = _bf(dst)

                dscores = _nt(do_c, v)
                dfwd = _bf(jnp.where(lower, dscores, 0.0))
                dbwd = _bf(jnp.where(lower, 0.0, dscores))
                dv_ref[rows, vc] = _bf(_tn(scores, do_c) + _nt(kd, dst_bf))
                da = _nn(dfwd, b) + _nn(do_c, _bf(st))
                db = _tn(dfwd, a)
                dc = _nn(dbwd, dk_dec)
                ddk = _tn(dbwd, c)
                dkd = _nn(v, dst_bf)
                ddec = jnp.sum(dst * st, axis=0, keepdims=True)
                dstate[h] = dst * dec + _tn(do_c, a)

                m = dkd * k * ed
                dq_ref[rows, kc] = _bf((da * ep + dc * en) * (HEAD_K ** -0.5))
                dk_ref[rows, kc] = _bf(db * en + ddk * ep + dkd * ed)
                dcum = (da * qs + ddk * k) * ep - (db * k + dc * qs) * en - m
                dlast = jnp.sum(m, axis=0, keepdims=True) + ddec * dec
                dcum_ref[rows, kc] = dcum + jnp.where(is_last, dlast, 0.0)
            return carry

        lax.fori_loop(0, CHUNKS_PER_BLOCK, chunk, 0, unroll=4)

    rev = lambda cols, col_block: pl.BlockSpec((GLA_BLOCK, cols), lambda i: (nb - 1 - i, col_block))
    return pl.pallas_call(
        body, name="gla_backward", grid=(nb,),
        out_shape=(jax.ShapeDtypeStruct((s, KEY_W), BF16), jax.ShapeDtypeStruct((s, KEY_W), BF16),
                   jax.ShapeDtypeStruct((s, D), BF16), jax.ShapeDtypeStruct((s, KEY_W), F32)),
        in_specs=[rev(KEY_W, 0), rev(KEY_W, 1), rev(D, 0), rev(KEY_W, 0), rev(D, 0),
                  pl.BlockSpec((CHUNKS_PER_BLOCK, HEADS, HEAD_V, HEAD_K), lambda i: (nb - 1 - i, 0, 0, 0)),
                  rev(HEADS * CHUNK, 0)],
        out_specs=(rev(KEY_W, 0), rev(KEY_W, 0), rev(D, 0), rev(KEY_W, 0)),
        scratch_shapes=[pltpu.VMEM((HEADS, HEAD_V, HEAD_K), F32)],
        compiler_params=_params("arbitrary"),
    )(qk, qk, v, cum, do, states, scores)


def head_and_loss(o, gate, h1, target, hw, wgo, wf):
    s = o.shape[0]
    ts = ROW_TILE

    def body(o_ref, gate_ref, h1_ref, tgt_ref, hw_ref, wgo_ref, wf_ref,
             dh2_ref, do_ref, dgate_ref, ggo_ref, small_ref):
        @pl.when(pl.program_id(0) == 0)
        def _():
            ggo_ref[...] = jnp.zeros_like(ggo_ref)
            small_ref[...] = jnp.zeros_like(small_ref)

        gate = gate_ref[...]
        hw = hw_ref[...]
        sg = _sigmoid(gate)
        silu = gate * sg
        ohat, ro = [], []
        for h in range(HEADS):
            oh = o_ref[:, h * HEAD_V:(h + 1) * HEAD_V]
            rh = lax.rsqrt(jnp.mean(oh * oh, axis=-1, keepdims=True) + EPS)
            ro.append(rh)
            ohat.append(oh * rh)
        ohat = jnp.concatenate(ohat, axis=-1)
        on = ohat * hw
        y2 = _bf(on * silu)
        h2 = h1_ref[...] + _nn(y2, wgo_ref[...])
        rf = lax.rsqrt(jnp.mean(h2 * h2, axis=-1, keepdims=True) + EPS)
        h2hat = h2 * rf
        wf = wf_ref[...]
        diff = h2hat * wf - tgt_ref[...]
        small_ref[2:3, :] += jnp.zeros((1, D), F32) + 0.5 * jnp.sum(diff * diff) / D
        dout = diff / D
        small_ref[0:1, :] += jnp.sum(dout * h2hat, axis=0, keepdims=True)
        dxh = dout * wf
        dh2 = rf * (dxh - h2hat * jnp.mean(dxh * h2hat, axis=-1, keepdims=True))
        dh2_ref[...] = dh2
        dh2_bf = _bf(dh2)
        ggo_ref[...] += _tn(y2, dh2_bf)
        dy2 = _nt(dh2_bf, wgo_ref[...])
        don = dy2 * silu
        dgate_ref[...] = _bf(dy2 * on * (sg * (1.0 + gate * (1.0 - sg))))
        ghw = jnp.sum(don * ohat, axis=0, keepdims=True)
        small_ref[1:2, 0:HEAD_V] += sum(ghw[:, h * HEAD_V:(h + 1) * HEAD_V] for h in range(HEADS))
        dohat = don * hw
        for h in range(HEADS):
            cols = slice(h * HEAD_V, (h + 1) * HEAD_V)
            oh, dh = ohat[:, cols], dohat[:, cols]
            do_ref[:, cols] = _bf(ro[h] * (dh - oh * jnp.mean(dh * oh, axis=-1, keepdims=True)))

    row = lambda cols: pl.BlockSpec((ts, cols), lambda i: (i, 0))
    act = jax.ShapeDtypeStruct((s, D), F32)
    act_bf = jax.ShapeDtypeStruct((s, D), BF16)
    return pl.pallas_call(
        body, name="head_and_loss", grid=(s // ts,),
        out_shape=(act, act_bf, act_bf, jax.ShapeDtypeStruct((D, D), F32), jax.ShapeDtypeStruct((8, D), F32)),
        in_specs=[row(D), row(D), row(D), row(D),
                  _full((1, D)), _full((D, D)), _full((1, D))],
        out_specs=(row(D), row(D), row(D), _full((D, D)), _full((8, D))),
        compiler_params=_params("arbitrary"),
    )(o, gate, h1, target, hw, wgo, wf)


def gla_project_backward(dq, dk, dv, dgate, dcum, low, h1, dh2, w1, wgi_q, wgk, bgk):
    s = h1.shape[0]
    ts = ROW_TILE

    def body(dq_ref, dk_ref, dv_ref, dgate_ref, dcum_ref, low_ref, h1_ref, dh2_ref, w1_ref,
             wq_ref, wgk_ref, bgk_ref, dh1_ref, dproj_ref, ggk_ref, small_ref, wgi_ref):
        @pl.when(pl.program_id(0) == 0)
        def _():
            ggk_ref[...] = jnp.zeros_like(ggk_ref)
            small_ref[...] = jnp.zeros_like(small_ref)
            _assemble_gla_in(wq_ref, wgi_ref)

        low = low_ref[...]
        z = _nn(low, wgk_ref[...]) + bgk_ref[...]
        upper_f = _chunk_masks()[1].astype(F32)
        dlg = jnp.concatenate([_nn_exact(upper_f, dcum_ref[r0:r0 + CHUNK, :]) for r0 in range(0, ts, CHUNK)],
                              axis=0)
        dz = dlg * (1.0 / GATE_NORM) * _sigmoid(-z)
        dz_bf = _bf(dz)
        ggk_ref[...] += _tn(low, dz_bf)
        small_ref[1:2, 0:KEY_W] += jnp.sum(dz, axis=0, keepdims=True)
        dlow = _bf(_nt(dz_bf, wgk_ref[...]))
        dproj_ref[:, GLA_MAIN:] = dlow
        dn1 = _nt(dlow, wgi_ref[:, GLA_MAIN:])
        for ref, lo, hi in ((dq_ref, 0, KEY_W), (dk_ref, KEY_W, 2 * KEY_W),
                            (dv_ref, 2 * KEY_W, 2 * KEY_W + D), (dgate_ref, 2 * KEY_W + D, GLA_MAIN)):
            piece = ref[...]
            dproj_ref[:, lo:hi] = piece
            dn1 = dn1 + _nt(piece, wgi_ref[:, lo:hi])
        hv = h1_ref[...]
        r = lax.rsqrt(jnp.mean(hv * hv, axis=-1, keepdims=True) + EPS)
        hhat = hv * r
        small_ref[0:1, :] += jnp.sum(dn1 * hhat, axis=0, keepdims=True)
        dxh = dn1 * w1_ref[...]
        dh1_ref[...] = dh2_ref[...] + r * (dxh - hhat * jnp.mean(dxh * hhat, axis=-1, keepdims=True))

    row = lambda cols: pl.BlockSpec((ts, cols), lambda i: (i, 0))
    return pl.pallas_call(
        body, name="gla_project_backward", grid=(s // ts,),
        out_shape=(jax.ShapeDtypeStruct((s, D), F32), jax.ShapeDtypeStruct((s, GLA_MAIN + RANK_PAD), BF16),
                   jax.ShapeDtypeStruct((RANK_PAD, KEY_W), F32),
                   jax.ShapeDtypeStruct((8, D), F32)),
        in_specs=[row(KEY_W), row(KEY_W), row(D), row(D), row(KEY_W), row(RANK_PAD), row(D), row(D),
                  _full((1, D)), _full((N_CHIPS, D, GLA_IN_QUARTER)), _full((RANK_PAD, KEY_W)),
                  _full((1, KEY_W))],
        out_specs=(row(D), row(GLA_MAIN + RANK_PAD), _full((RANK_PAD, KEY_W)), _full((8, D))),
        scratch_shapes=[pltpu.VMEM((D, GLA_MAIN + RANK_PAD), BF16)],
        compiler_params=_params("arbitrary"),
    )(dq, dk, dv, dgate, dcum, low, h1, dh2, w1, wgi_q, wgk, bgk)


def _groups_from_quarters(a):
    return a.reshape(N_CHIPS, GROUPS, 64, GROUP_DIM).transpose(1, 0, 2, 3).reshape(GROUPS, GROUP_DIM, GROUP_DIM)


def _quarters_from_groups(a):
    return a.reshape(GROUPS, N_CHIPS, 64, GROUP_DIM).transpose(1, 0, 2, 3).reshape(N_CHIPS, GROUP_DIM, GROUP_DIM)


def local_gradients(xs, target, w0, w1, wf, wpi, gw, gb, scale, wpo, gla_quarters, wgk, bgk, hw_tiled, place):
    wgi_q, wgo_q = gla_quarters
    (h1, pooled, gt, n0), (wgi_q,) = pool_forward(xs, w0, wpi, gw, gb, scale, wpo, [wgi_q])
    (qk, v, gate, low, cum, n1), (wgo_q,) = gla_project(h1, w1, wgi_q, wgk, bgk, [wgo_q])
    wgo = wgo_q.reshape(D, D)
    o, states, scores = gla_forward(qk, v, cum)

    dh2, do, dgate, g_gla_out, small_top = head_and_loss(o, gate, h1, target, hw_tiled, wgo, wf)
    dq, dk, dv, dcum = gla_backward(qk, v, cum, do, states, scores)
    dh1, dproj, g_gk_pad, small_gla = gla_project_backward(
        dq, dk, dv, dgate, dcum, low, h1, dh2, w1, wgi_q, wgk, bgk)
    g_gla_in, _ = matmul_tn(n1, dproj, "grad_gla_in", tile_n=(GLA_MAIN + RANK_PAD) // 5)

    def chip_sums(grads, tag):
        return add_halves(grads, place, "add_halves_" + tag)

    gla_sums = chip_sums([g_gla_in, g_gla_out.reshape(N_CHIPS, D // N_CHIPS, D)], "gla")
    (dx, dpool, g_pool_out, g_group_w, small_pool), gla_got = pool_backward(
        xs, dh1, pooled, gt, w0, wpi, gw, gb, scale, wpo, [b for _, b in gla_sums])
    mix_sums = chip_sums([_quarters_from_groups(g_group_w), g_pool_out.reshape(N_CHIPS, D // N_CHIPS, D)], "pool_mix")
    g_pool_in, mix_got = matmul_tn(n0, dpool, "grad_pool_in", tile_n=D // 2, by_column_tile=True,
                                   chip_sums=[b for _, b in mix_sums])

    in_sums = add_halves([g_pool_in], place, "add_halves_and_scatter_pool_in", scatter=True)
    reduced, total = join_halves(
        [f for f, _ in in_sums + mix_sums + gla_sums], [got for _, got in in_sums] + list(mix_got) + list(gla_got),
        small_pool, small_gla, small_top, g_gk_pad)
    return dx, reduced, total


def kernel(x, norm_w, pool_in_w, pool_group_w, pool_group_b, pool_scale, pool_out_w, gla_in_w, gla_gk_w, gla_gk_b, gla_head_norm_w, gla_out_w, final_norm_w, loss_target, m_norm_w, m_pool_in_w, m_pool_group_w, m_pool_group_b, m_pool_scale, m_pool_out_w, m_gla_in_w, m_gla_gk_w, m_gla_gk_b, m_gla_head_norm_w, m_gla_out_w, m_final_norm_w, v_norm_w, v_pool_in_w, v_pool_group_w, v_pool_group_b, v_pool_scale, v_pool_out_w, v_gla_in_w, v_gla_gk_w, v_gla_gk_b, v_gla_head_norm_w, v_gla_out_w, v_final_norm_w):
    xs = x[0]
    target = loss_target[0]
    q_chip = 2 * lax.axis_index("x") + lax.axis_index("y")
    place = jnp.stack([lax.axis_index("c"), q_chip]).astype(jnp.int32)

    (wpi, gw_q, wpo_q, wgi_q, wgo_q), (bgk, hw_tiled, gb, wgk) = allgather_weights(
        [pool_in_w[0], pool_group_w[0].reshape(GROUP_DIM, GROUP_DIM), pool_out_w[0], gla_in_w[0], gla_out_w[0]],
        exchange=(True, True, True, False, False),
        smalls=[gla_gk_b, gla_head_norm_w, pool_group_b[0], gla_gk_w[0]])
    gw = _groups_from_quarters(gw_q)
    wpo = wpo_q.reshape(D, D)

    w0 = norm_w[0:1]
    w1 = norm_w[1:2]
    wf = final_norm_w.reshape(1, D)

    dx, reduced, total = local_gradients(
        xs, target, w0, w1, wf, wpi, gw, gb, pool_scale, wpo, [wgi_q, wgo_q], wgk, bgk, hw_tiled, place)
    r_pool_in, r_group_w, r_pool_out, r_gla_in, r_gla_out = reduced
    r_group_w = r_group_w.reshape(GROUPS, 64, GROUP_DIM)

    loss = total[7, 0]

    turn = lambda a: jnp.transpose(a, (2, 0, 1))
    back = lambda a: jnp.transpose(a, (1, 2, 0))
    as2d = lambda a, w: a.reshape(-1, w.shape[-1])
    big_names = ("pool_in_w", "pool_group_w", "pool_out_w", "gla_in_w", "gla_out_w")
    big_args = [(pool_in_w, r_pool_in[None], m_pool_in_w, v_pool_in_w),
                (pool_group_w, r_group_w[None], m_pool_group_w, v_pool_group_w),
                (pool_out_w, r_pool_out[None], m_pool_out_w, v_pool_out_w),
                (gla_in_w, r_gla_in[None], m_gla_in_w, v_gla_in_w),
                (gla_out_w, r_gla_out[None], m_gla_out_w, v_gla_out_w)]
    to_kernel = lambda n, a, w: turn(a) if n == "gla_in_w" else as2d(a, w)
    from_kernel = lambda n, a, w: back(a) if n == "gla_in_w" else a.reshape(w.shape)
    big_in = [tuple(to_kernel(n, a, p[0]) for a in p) for n, p in zip(big_names, big_args)]
    big_out = adamw(big_in, "adamw")
    big = {n: (from_kernel(n, i[1], p[0]),) + tuple(from_kernel(n, o, p[0]) for o in out)
           for n, p, i, out in zip(big_names, big_args, big_in, big_out)}

    small_names = ("norm_w", "pool_group_b", "pool_scale", "gla_gk_w", "gla_gk_b", "gla_head_norm_w",
                   "final_norm_w")
    small_args = [(norm_w, m_norm_w, v_norm_w),
                  (pool_group_b, m_pool_group_b, v_pool_group_b),
                  (pool_scale, m_pool_scale, v_pool_scale),
                  (gla_gk_w, m_gla_gk_w, v_gla_gk_w),
                  (gla_gk_b, m_gla_gk_b, v_gla_gk_b),
                  (gla_head_norm_w, m_gla_head_norm_w, v_gla_head_norm_w),
                  (final_norm_w, m_final_norm_w, v_final_norm_w)]
    small_out = adamw_small([tuple(as2d(a, p[0]) for a in p) for p in small_args], total, place)
    small = {n: tuple(o.reshape(p[0].shape) for o in out) for n, p, out in zip(small_names, small_args, small_out)}
    results = [
        small["norm_w"],
        big["pool_in_w"],
        big["pool_group_w"],
        small["pool_group_b"],
        small["pool_scale"],
        big["pool_out_w"],
        big["gla_in_w"],
        small["gla_gk_w"],
        small["gla_gk_b"],
        small["gla_head_norm_w"],
        big["gla_out_w"],
        small["final_norm_w"],
    ]
    grads, deltas, new_m, new_v = zip(*results)
    return (loss, dx[None], *grads, *deltas, *new_m, *new_v)
```

```python
import jax
import jax.numpy as jnp
from jax import lax
from jax.experimental import pallas as pl
from jax.experimental.pallas import tpu as pltpu

F32 = jnp.float32
BF16 = jnp.bfloat16
MESH = pl.DeviceIdType.MESH

D = 1024
POOL_WINDOWS = (2, 4, 8, 16)
GROUPS = 4
GROUP_DIM = 256
HEADS = 4
HEAD_K = 128
HEAD_V = 256
KEY_W = 512
CHUNK = 64
GATE_RANK = 16
GATE_NORM = 16.0
GLA_IN = 3088
GLA_MAIN = 3072
RANK_PAD = 128
EPS = 1e-6
HALO = 32

ADAM_LR = 0.001
ADAM_B1 = 0.9
ADAM_B2 = 0.999
ADAM_EPS = 1e-08
ADAM_WD = 0.01
ADAM_STEP = 10

N_CHIPS = 4
N_DEV = 8
GLA_IN_QUARTER = GLA_IN // N_CHIPS

VMEM_LIMIT = 56 * 1024 * 1024


def _nn(a, b):
    return lax.dot_general(a, b, (((1,), (0,)), ((), ())), preferred_element_type=F32)


def _nt(a, b):
    return lax.dot_general(a, b, (((1,), (1,)), ((), ())), preferred_element_type=F32)


def _tn(a, b):
    return lax.dot_general(a, b, (((0,), (0,)), ((), ())), preferred_element_type=F32)


def _nn_exact(a, b):
    return lax.dot_general(a, b, (((1,), (0,)), ((), ())), preferred_element_type=F32,
                           precision=lax.Precision.HIGHEST)


def _bf(a):
    return a.astype(BF16)


def _params(*sem):
    return pltpu.CompilerParams(dimension_semantics=sem, vmem_limit_bytes=VMEM_LIMIT)


def _full(shape):
    return pl.BlockSpec(shape, lambda i: (0,) * len(shape))


def _position():
    return lax.axis_index("x"), lax.axis_index("y"), lax.axis_index("c")


def _gather_small(in_ref, all_ref, send_sems, recv_sems, local_sem):
    x, y, c = _position()
    me = 4 * x + 2 * y + c
    mine = pltpu.make_async_copy(in_ref, all_ref.at[me], local_sem)
    sends = []
    for k in range(N_DEV - 1):
        fx, fy, fc = (k + 1) >> 2 & 1, (k + 1) >> 1 & 1, (k + 1) & 1
        sends.append(pltpu.make_async_remote_copy(
            src_ref=in_ref, dst_ref=all_ref.at[me],
            send_sem=send_sems.at[k], recv_sem=recv_sems.at[k],
            device_id=(x ^ fx, y ^ fy, c ^ fc), device_id_type=MESH))

    def start():
        mine.start()
        for cp in sends:
            cp.start()

    def wait():
        for k in range(N_DEV - 1):
            fx, fy, fc = (k + 1) >> 2 & 1, (k + 1) >> 1 & 1, (k + 1) & 1
            src_dev = 4 * (x ^ fx) + 2 * (y ^ fy) + (c ^ fc)
            pltpu.make_async_remote_copy(
                src_ref=in_ref, dst_ref=all_ref.at[src_dev],
                send_sem=send_sems.at[k], recv_sem=recv_sems.at[k],
                device_id=(x, y, c), device_id_type=MESH).wait_recv()
        for cp in sends:
            cp.wait_send()
        mine.wait()

    return start, wait


SMALL_SEMS = [pltpu.SemaphoreType.DMA((N_DEV - 1,)), pltpu.SemaphoreType.DMA((N_DEV - 1,)),
              pltpu.SemaphoreType.DMA]
VMEM_SPEC = pl.BlockSpec(memory_space=pltpu.VMEM)


def _other_chips(x, y):
    return [(1 - x, y), (x, 1 - y), (1 - x, 1 - y)]


def _any_specs(n):
    return [pl.BlockSpec(memory_space=pl.ANY)] * n


def _halves(rows, c):
    half = rows // 2
    return pl.ds(c * half, half), pl.ds((1 - c) * half, half)


CAST_ROWS = 256


def _gather_copy(out_ref, send_sems, recv_sems, k, quarter, half, to, src=None):
    dst = out_ref.at[quarter, half]
    return pltpu.make_async_remote_copy(
        src_ref=dst if src is None else src, dst_ref=dst,
        send_sem=send_sems.at[k], recv_sem=recv_sems.at[k], device_id=to, device_id_type=MESH)


SMALL_IN_ROWS = 24


def allgather_weights(quarters, exchange, smalls):
    n = len(quarters)
    shapes = [w.shape for w in quarters]
    moved = [i for i in range(n) if exchange[i]]

    def body(*refs):
        w_refs, (gkb_ref, hnw_ref, gb_ref, gkw_ref) = refs[:n], refs[n:n + 4]
        out_refs, (bgk_ref, hw_ref, gbias_ref, wgk_ref) = refs[n + 4:2 * n + 4], refs[2 * n + 4:2 * n + 8]
        refs = refs[2 * n + 8:]
        f32_bufs, bf_bufs = refs[:n], refs[n:2 * n]
        send_sems, recv_sems, local_sems, small_ref, small_all_ref = refs[2 * n:2 * n + 5]
        small_ref[...] = jnp.zeros_like(small_ref)
        small_ref[0:1, :] = gkb_ref[...]
        small_ref[1:2, 0:64] = hnw_ref[...]
        small_ref[2:2 + GROUPS, 0:64] = gb_ref[...]
        small_ref[8:8 + GATE_RANK, :] = gkw_ref[...]
        start_small, wait_small = _gather_small(small_ref, small_all_ref, *refs[2 * n + 5:])
        start_small()
        x, y, c = _position()
        q = 2 * x + y
        sibling = (x, y, 1 - c)
        chips = _other_chips(x, y)

        def copy(k, i, quarter, half, to, src=None):
            return _gather_copy(out_refs[i], send_sems, recv_sems, k * n + i, quarter, half, to, src)

        loads = [pltpu.make_async_copy(w_refs[i], f32_bufs[i], local_sems.at[i]) for i in range(n)]
        for cp in loads:
            cp.start()
        keeps, sends = [], []
        for i in range(n):
            loads[i].wait()
            for r0 in range(0, shapes[i][0], CAST_ROWS):
                bf_bufs[i][r0:r0 + CAST_ROWS, :] = _bf(f32_bufs[i][r0:r0 + CAST_ROWS, :])
            keep = pltpu.make_async_copy(bf_bufs[i], out_refs[i].at[q], local_sems.at[n + i])
            keep.start()
            keeps.append(keep)
            if not exchange[i]:
                continue
            mine, _ = _halves(shapes[i][0], c)
            for j, chip in enumerate(chips):
                cp = copy(j, i, q, mine, (*chip, c), src=bf_bufs[i].at[mine])
                cp.start()
                sends.append(cp)
        for j, chip in enumerate(chips):
            qj = 2 * chip[0] + chip[1]
            for i in moved:
                mine, _ = _halves(shapes[i][0], c)
                copy(j, i, qj, mine, (x, y, c)).wait_recv()
                cp = copy(3 + j, i, qj, mine, sibling)
                cp.start()
                sends.append(cp)
        for j, chip in enumerate(chips):
            qj = 2 * chip[0] + chip[1]
            for i in moved:
                _, other = _halves(shapes[i][0], c)
                copy(3 + j, i, qj, other, (x, y, c)).wait_recv()
        wait_small()
        wgk_ref[...] = jnp.zeros_like(wgk_ref)
        for j in range(N_CHIPS):
            block = small_all_ref.at[2 * j]
            bgk_ref[:, 128 * j:128 * (j + 1)] = block[0:1, :]
            for h in range(HEADS):
                hw_ref[:, HEAD_V * h + 64 * j:HEAD_V * h + 64 * (j + 1)] = block[1:2, 0:64]
            for g in range(GROUPS):
                gbias_ref[:, GROUP_DIM * g + 64 * j:GROUP_DIM * g + 64 * (j + 1)] = block[2 + g:3 + g, 0:64]
            wgk_ref[0:GATE_RANK, 128 * j:128 * (j + 1)] = _bf(block[8:8 + GATE_RANK, :])
        for cp in sends:
            cp.wait_send()
        for cp in keeps:
            cp.wait()

    outs = pl.pallas_call(
        body, name="allgather_weights",
        out_shape=[jax.ShapeDtypeStruct((N_CHIPS, *s), BF16) for s in shapes]
                  + [jax.ShapeDtypeStruct((1, KEY_W), F32), jax.ShapeDtypeStruct((1, D), F32),
                     jax.ShapeDtypeStruct((1, D), F32), jax.ShapeDtypeStruct((RANK_PAD, KEY_W), BF16)],
        in_specs=_any_specs(n) + [VMEM_SPEC] * 4, out_specs=_any_specs(n) + [VMEM_SPEC] * 4,
        scratch_shapes=([pltpu.VMEM(s, F32) for s in shapes] + [pltpu.VMEM(s, BF16) for s in shapes]
                        + [pltpu.SemaphoreType.DMA((6 * n,)), pltpu.SemaphoreType.DMA((6 * n,)),
                           pltpu.SemaphoreType.DMA((2 * n,)), pltpu.VMEM((SMALL_IN_ROWS, 128), F32),
                           pltpu.VMEM((N_DEV, SMALL_IN_ROWS, 128), F32)] + SMALL_SEMS),
        compiler_params=pltpu.CompilerParams(vmem_limit_bytes=VMEM_LIMIT),
    )(*quarters, *smalls)
    return outs[:n], outs[n:]


def _scatter_copies(b_refs, got_refs, send_sems, recv_sems):
    n = len(b_refs)
    x, y, c = _position()
    copies = []
    for j, chip in enumerate(_other_chips(x, y)):
        qj = 2 * chip[0] + chip[1]
        for i in range(n):
            copies.append(pltpu.make_async_remote_copy(
                src_ref=b_refs[i].at[qj], dst_ref=got_refs[i].at[j],
                send_sem=send_sems.at[j * n + i], recv_sem=recv_sems.at[j * n + i],
                device_id=(*chip, c), device_id_type=MESH))
    return copies


def _scatter_shapes(chip_sums):
    return [jax.ShapeDtypeStruct((N_CHIPS - 1, *b.shape[1:]), BF16) for b in chip_sums]


ADD_ROWS = 512
ADD_HALVES_ROWS = 128


def _spans(counts):
    starts, total = [], 0
    for count in counts:
        starts.append(total)
        total += count
    return starts, total


def _local_step(t, start, count):
    return jnp.clip(t - start, 0, count - 1)


def add_halves(grads, place, name, scatter=False):
    n = len(grads)
    whole = [len(g.shape) == 2 for g in grads]
    halves = [g.shape[-2] // 2 for g in grads]
    cols = [GLA_IN_QUARTER if w else g.shape[-1] for g, w in zip(grads, whole)]
    rbs = [min(ADD_HALVES_ROWS, h) for h in halves]
    counts = [h // rb for h, rb in zip(halves, rbs)]
    starts, total = _spans(counts)
    half_shapes = [(*g.shape[:-2], h, g.shape[-1]) for g, h in zip(grads, halves)]

    def rows_of(ref, i, start):
        return ref.at[pl.ds(start, rbs[i])] if whole[i] else ref.at[:, pl.ds(start, rbs[i])]

    def body(place_ref, *refs):
        a_refs, o_refs = refs[:n], refs[n:2 * n]
        f_refs, h_refs = refs[2 * n:3 * n], refs[3 * n:4 * n]
        send_refs, their_refs, rest = refs[4 * n:5 * n], refs[5 * n:6 * n], refs[6 * n:]
        send_sems, recv_sems = rest[:2]
        t = pl.program_id(0)
        q = place_ref[1]
        x, y, c = _position()
        sum_refs = rest[2:2 + n] if scatter else h_refs

        def to_owners(i, k):
            out_sems, in_sems = rest[2 + n:]
            rows = pl.ds(k * rbs[i], rbs[i])
            return [pltpu.make_async_remote_copy(
                src_ref=sum_refs[i].at[2 * chip[0] + chip[1], rows], dst_ref=h_refs[i].at[j, rows],
                send_sem=out_sems.at[3 * (starts[i] + k) + j], recv_sem=in_sems.at[3 * (starts[i] + k) + j],
                device_id=(*chip, c), device_id_type=MESH) for j, chip in enumerate(_other_chips(x, y))]

        copies = [[pltpu.make_async_remote_copy(
            src_ref=rows_of(send_refs[i], i, k * rbs[i]), dst_ref=rows_of(their_refs[i], i, k * rbs[i]),
            send_sem=send_sems.at[starts[i] + k], recv_sem=recv_sems.at[starts[i] + k],
            device_id=(x, y, 1 - c), device_id_type=MESH) for k in range(counts[i])] for i in range(n)]

        for i in range(n):
            for k in range(counts[i]):
                @pl.when(t == starts[i] + k)
                def _(i=i, k=k):
                    rows_of(send_refs[i], i, k * rbs[i])[...] = _bf(o_refs[i][...])
                    copies[i][k].start()

        for i in range(n):
            for k in range(counts[i]):
                @pl.when(t == starts[i] + k + 1)
                def _(i=i, k=k):
                    copies[i][k].wait_recv()
                    b_ref = rows_of(their_refs[i], i, k * rbs[i])
                    h_ref = sum_refs[i].at[:, pl.ds(k * rbs[i], rbs[i])] if scatter else h_refs[i]
                    if not whole[i]:
                        h_ref[...] = _bf(a_refs[i][...] + b_ref[...].astype(F32))
                        f_refs[i][...] = a_refs[i][q] + b_ref[q].astype(F32)
                    else:
                        total_i = a_refs[i][...] + b_ref[...].astype(F32)
                        for k4 in range(N_CHIPS):
                            piece = total_i[:, k4 * cols[i]:(k4 + 1) * cols[i]]
                            h_ref[k4] = _bf(piece)

                            @pl.when(q == k4)
                            def _():
                                f_refs[i][...] = piece
                    if scatter:
                        for cp in to_owners(i, k):
                            cp.start()

        @pl.when(t == total)
        def _():
            for of_matrix in copies:
                for cp in of_matrix:
                    cp.wait_send()
            if scatter:
                for i in range(n):
                    for k in range(counts[i]):
                        for cp in to_owners(i, k):
                            cp.wait()

    def specs(i):
        sent = lambda t: _local_step(t, starts[i], counts[i])
        added = lambda t: _local_step(t - 1, starts[i], counts[i])
        by_quarter = (N_CHIPS, rbs[i], cols[i])
        block = (rbs[i], grads[i].shape[-1]) if whole[i] else by_quarter
        lead = () if whole[i] else (0,)
        mine = pl.BlockSpec(block, lambda t, place: (*lead, place[0] * counts[i] + added(t), 0))
        other = pl.BlockSpec(block, lambda t, place: (*lead, (1 - place[0]) * counts[i] + sent(t), 0))
        sums = pl.BlockSpec(by_quarter, lambda t, place: (0, added(t), 0))
        own = pl.BlockSpec(by_quarter[1:], lambda t, place: (added(t), 0))
        return mine, other, own, sums

    all_specs = [specs(i) for i in range(n)]
    sum_shapes = [(N_CHIPS, h, cl) for h, cl in zip(halves, cols)]
    scratch = [pltpu.VMEM(sh, BF16) for sh in half_shapes] + [pltpu.VMEM(sh, BF16) for sh in half_shapes]
    scratch += [pltpu.SemaphoreType.DMA((total,)), pltpu.SemaphoreType.DMA((total,))]
    if scatter:
        scratch += [pltpu.VMEM(sh, BF16) for sh in sum_shapes]
        scratch += [pltpu.SemaphoreType.DMA((3 * total,)), pltpu.SemaphoreType.DMA((3 * total,))]
    outs = pl.pallas_call(
        body, name=name,
        grid_spec=pltpu.PrefetchScalarGridSpec(
            num_scalar_prefetch=1, grid=(total + 1,),
            in_specs=[sp[0] for sp in all_specs] + [sp[1] for sp in all_specs],
            out_specs=[sp[2] for sp in all_specs] + (_any_specs(n) if scatter else [sp[3] for sp in all_specs]),
            scratch_shapes=scratch),
        out_shape=[jax.ShapeDtypeStruct((h, cl), F32) for h, cl in zip(halves, cols)]
                  + [jax.ShapeDtypeStruct((N_CHIPS - 1 if scatter else N_CHIPS, *sh[1:]), BF16) for sh in sum_shapes],
        compiler_params=_params("arbitrary"),
    )(place, *grads, *grads)
    return list(zip(outs[:n], outs[n:]))


SMALL_SUM_ROWS = 16


def join_halves(owns, gots, small_pool, small_gla, small_top, g_gk_pad):
    n = len(owns)
    shapes = [g.shape for g in gots]
    rbs = [min(ADD_ROWS, sh[1]) for sh in shapes]
    counts = [sh[1] // rb for sh, rb in zip(shapes, rbs)]
    starts, total = _spans(counts)

    def body(*refs):
        o_refs, g_refs = refs[:n], refs[n:2 * n]
        pool_ref, gla_ref, top_ref, gk_ref = refs[2 * n:2 * n + 4]
        out_refs, total_ref = refs[2 * n + 4:3 * n + 4], refs[3 * n + 4]
        sum_refs = refs[3 * n + 5:4 * n + 5]
        local_sems, send_sems, recv_sems, all_ref, small_ref = refs[4 * n + 5:4 * n + 10]
        t = pl.program_id(0)
        x, y, c = _position()
        start_small, wait_small = _gather_small(small_ref, all_ref, *refs[4 * n + 10:])

        def copies(i, k):
            src = sum_refs[i].at[pl.ds(k * rbs[i], rbs[i])]
            rows = pl.ds(c * shapes[i][1] + k * rbs[i], rbs[i])
            return (pltpu.make_async_copy(src, out_refs[i].at[rows], local_sems.at[starts[i] + k]),
                    pltpu.make_async_remote_copy(
                        src_ref=src, dst_ref=out_refs[i].at[rows],
                        send_sem=send_sems.at[starts[i] + k], recv_sem=recv_sems.at[starts[i] + k],
                        device_id=(x, y, 1 - c), device_id_type=MESH))

        @pl.when(t == 0)
        def _():
            small_ref[0:3, :] = pool_ref[0:3, :]
            small_ref[3:5, :] = gla_ref[0:2, :]
            small_ref[5:8, :] = top_ref[0:3, :]
            for r in range(GATE_RANK):
                small_ref[8 + r // 2:9 + r // 2, (r % 2) * KEY_W:(r % 2 + 1) * KEY_W] = gk_ref[r:r + 1, :]
            start_small()

        for i in range(n):
            for k in range(counts[i]):
                @pl.when(t == starts[i] + k)
                def _(i=i, k=k):
                    total_i = o_refs[i][...]
                    for j in range(N_CHIPS - 1):
                        total_i = total_i + g_refs[i][j].astype(F32)
                    sum_refs[i][k * rbs[i]:(k + 1) * rbs[i], :] = total_i
                    for cp in copies(i, k):
                        cp.start()

        @pl.when(t == total - 1)
        def _():
            wait_small()
            small_total = all_ref[0]
            for dev in range(1, N_DEV):
                small_total = small_total + all_ref[dev]
            total_ref[...] = small_total
            for i in range(n):
                for k in range(counts[i]):
                    for cp in copies(i, k):
                        cp.wait()

    def specs(i):
        rb, cols = rbs[i], shapes[i][2]
        step = lambda t: _local_step(t, starts[i], counts[i])
        return (pl.BlockSpec((rb, cols), lambda t: (step(t), 0)),
                pl.BlockSpec((N_CHIPS - 1, rb, cols), lambda t: (0, step(t), 0)))

    all_specs = [specs(i) for i in range(n)]
    outs = pl.pallas_call(
        body, name="join_halves", grid=(total,),
        out_shape=[jax.ShapeDtypeStruct((2 * sh[1], sh[2]), F32) for sh in shapes]
                  + [jax.ShapeDtypeStruct((SMALL_SUM_ROWS, D), F32)],
        in_specs=[sp[0] for sp in all_specs] + [sp[1] for sp in all_specs] + [VMEM_SPEC] * 4,
        out_specs=_any_specs(n) + [VMEM_SPEC],
        scratch_shapes=[pltpu.VMEM(sh[1:], F32) for sh in shapes]
                       + [pltpu.SemaphoreType.DMA((total,)), pltpu.SemaphoreType.DMA((total,)),
                          pltpu.SemaphoreType.DMA((total,)),
                          pltpu.VMEM((N_DEV, SMALL_SUM_ROWS, D), F32), pltpu.VMEM((SMALL_SUM_ROWS, D), F32)]
                       + SMALL_SEMS,
        compiler_params=_params("arbitrary"),
    )(*owns, *gots, small_pool, small_gla, small_top, g_gk_pad)
    return outs[:n], outs[n]


def _adam_math(w, g, m, v):
    m = ADAM_B1 * m + (1.0 - ADAM_B1) * g
    v = ADAM_B2 * v + (1.0 - ADAM_B2) * (g * g)
    m_hat = m / (1.0 - ADAM_B1 ** ADAM_STEP)
    v_hat = v / (1.0 - ADAM_B2 ** ADAM_STEP)
    delta = -ADAM_LR * (m_hat / (jnp.sqrt(v_hat) + ADAM_EPS) + ADAM_WD * w)
    return delta, m, v


ADAM_BLOCK_BYTES = 2 ** 19
ADAM_MOST_STEPS = 8


def adamw(params, name):
    n = len(params)
    shapes = [p[0].shape for p in params]

    def tile_rows(shape):
        rows, cols = shape[0], shape[-1]
        aligned = 1 if len(shape) == 3 else 8
        divisors = [t for t in range(aligned, rows + 1, aligned) if rows % t == 0]
        tile = max(t for t in divisors if t * cols * 4 <= ADAM_BLOCK_BYTES)
        if rows // tile > ADAM_MOST_STEPS:
            tile = min(t for t in divisors if rows // t <= ADAM_MOST_STEPS)
        return tile

    tiles = [tile_rows(sh) for sh in shapes]
    counts = [sh[0] // tl for sh, tl in zip(shapes, tiles)]
    starts, total = _spans(counts)

    def body(*refs):
        ins, outs = refs[:4 * n], refs[4 * n:]
        t = pl.program_id(0)
        for i in range(n):
            @pl.when((t >= starts[i]) & (t < starts[i] + counts[i]))
            def _(i=i):
                w_ref, g_ref, m_ref, v_ref = ins[4 * i:4 * i + 4]
                d, nm, nv = _adam_math(w_ref[...], g_ref[...], m_ref[...], v_ref[...])
                outs[3 * i][...] = d
                outs[3 * i + 1][...] = nm
                outs[3 * i + 2][...] = nv

    def spec(i):
        block = (tiles[i],) + shapes[i][1:]
        zeros = (0,) * (len(block) - 1)
        return pl.BlockSpec(block, lambda t: (_local_step(t, starts[i], counts[i]),) + zeros)

    outs = pl.pallas_call(
        body, name=name, grid=(total,),
        out_shape=[jax.ShapeDtypeStruct(sh, F32) for sh in shapes for _ in range(3)],
        in_specs=[spec(i) for i in range(n) for _ in range(4)],
        out_specs=[spec(i) for i in range(n) for _ in range(3)],
        compiler_params=_params("arbitrary"),
    )(*[a for p in params for a in p])
    return [tuple(outs[3 * i:3 * i + 3]) for i in range(n)]


def adamw_small(params, total, place):
    n = len(params)

    def cut_gradients(total_ref, q, g_refs):
        g_norm, g_group_b, g_scale, g_gk_w, g_gk_b, g_head_norm, g_final = g_refs
        g_norm[0:1, :] = total_ref[0:1, :]
        g_norm[1:2, :] = total_ref[3:4, :]
        g_scale[...] = total_ref[1:2, :]
        g_final[...] = total_ref[5:6, :]
        g_gk_b[...] = total_ref[4:5, pl.ds(pl.multiple_of(q * 128, 128), 128)]
        for r in range(GATE_RANK):
            lanes = pl.ds(pl.multiple_of((r % 2) * KEY_W + q * 128, 128), 128)
            g_gk_w[r:r + 1, :] = total_ref[8 + r // 2:9 + r // 2, lanes]
        for k in range(N_CHIPS):
            @pl.when(q == k)
            def _(k=k):
                g_head_norm[...] = total_ref[6:7, 64 * k:64 * (k + 1)]
                for g in range(GROUPS):
                    g_group_b[g:g + 1, :] = total_ref[2:3, GROUP_DIM * g + 64 * k:GROUP_DIM * g + 64 * (k + 1)]

    def body(place_ref, total_ref, *refs):
        ins, outs = refs[:3 * n], refs[3 * n:]
        outs[4 * n][...] = total_ref[7:8, 0:1]
        cut_gradients(total_ref, place_ref[1], outs[0:4 * n:4])
        for k in range(n):
            w_ref, m_ref, v_ref = ins[3 * k:3 * k + 3]
            d, nm, nv = _adam_math(w_ref[...], outs[4 * k][...], m_ref[...], v_ref[...])
            outs[4 * k + 1][...] = d
            outs[4 * k + 2][...] = nm
            outs[4 * k + 3][...] = nv

    flat = [a for p in params for a in p]
    outs = pl.pallas_call(
        body, name="adamw_small",
        out_shape=[jax.ShapeDtypeStruct(p[0].shape, F32) for p in params for _ in range(4)]
                  + [jax.ShapeDtypeStruct((1, 1), F32)],
        in_specs=[pl.BlockSpec(memory_space=pltpu.SMEM)] + [VMEM_SPEC] * (1 + 3 * n),
        out_specs=[VMEM_SPEC] * (4 * n + 1),
    )(place, total, *flat)
    return [tuple(outs[4 * k:4 * k + 4]) for k in range(n)], outs[4 * n]


def matmul_tn(a, b, name, tile_n, by_column_tile=False, chip_sums=()):
    s, m = a.shape
    n = b.shape[1]
    n_sums = len(chip_sums)
    steps = n // tile_n
    if by_column_tile:
        out_shape = jax.ShapeDtypeStruct((steps, m, tile_n), F32)
        out_spec = pl.BlockSpec((None, m, tile_n), lambda j: (j, 0, 0))
    else:
        out_shape = jax.ShapeDtypeStruct((m, n), F32)
        out_spec = pl.BlockSpec((m, tile_n), lambda j: (0, j))

    def body(a_ref, b_ref, *rest):
        sum_refs, out_ref, got_refs = rest[:n_sums], rest[n_sums], rest[n_sums + 1:2 * n_sums + 1]
        j = pl.program_id(0)
        copies = _scatter_copies(sum_refs, got_refs, *rest[2 * n_sums + 1:]) if n_sums else []

        @pl.when(j == 0)
        def _():
            for cp in copies:
                cp.start()

        out_ref[...] = _tn(a_ref[...], b_ref[...])

        @pl.when(j == steps - 1)
        def _():
            for cp in copies:
                cp.wait()

    outs = pl.pallas_call(
        body, name=name, grid=(steps,),
        out_shape=[out_shape] + _scatter_shapes(chip_sums),
        in_specs=[_full((s, m)), pl.BlockSpec((s, tile_n), lambda j: (0, j))] + _any_specs(n_sums),
        out_specs=[out_spec] + _any_specs(n_sums),
        scratch_shapes=[pltpu.SemaphoreType.DMA((3 * n_sums,)), pltpu.SemaphoreType.DMA((3 * n_sums,))]
                       if n_sums else [],
        compiler_params=_params("arbitrary"),
    )(a, b, *chip_sums)
    return outs[0], outs[1:]


ROW_TILE = 512


def _row_index(tile, rows):
    return tile * rows + lax.broadcasted_iota(jnp.int32, (rows, 1), 0)


def _inverse_counts(t_glob):
    return [1.0 / jnp.minimum(t_glob + 1, w).astype(F32) for w in POOL_WINDOWS]


def _sigmoid(z):
    return 1.0 / (1.0 + jnp.exp(-z))


def _trailing_sums(src, tmp, cols, window, rows):
    bufs = (src, tmp)
    span, level, start = 1, 0, 0
    while span < window:
        start += 8
        a, b = bufs[level % 2], bufs[(level + 1) % 2]
        n = HALO + rows - start
        b[start:start + n, cols] = a[start:start + n, cols] + a[start - span:start - span + n, cols]
        span, level = 2 * span, level + 1
    return bufs[level % 2][HALO:HALO + rows, cols]


def _leading_sums(src, tmp, cols, window, rows):
    bufs = (src, tmp)
    span, level, n = 1, 0, rows + HALO
    while span < window:
        n -= 8
        a, b = bufs[level % 2], bufs[(level + 1) % 2]
        b[0:n, cols] = a[0:n, cols] + a[span:span + n, cols]
        span, level = 2 * span, level + 1
    return bufs[level % 2][0:rows, cols]


def gather_in_background(step, last, out_refs, send_sems, recv_sems, finish):
    n = len(out_refs)
    x, y, c = _position()
    q = 2 * x + y
    chips = _other_chips(x, y)

    def copy(k, i, quarter, half, to):
        return _gather_copy(out_refs[i], send_sems, recv_sems, k * n + i, quarter, half, to)

    if not finish:
        @pl.when(step == 0)
        def _():
            for i in range(n):
                mine, _ = _halves(out_refs[i].shape[1], c)
                for j, chip in enumerate(chips):
                    copy(j, i, q, mine, (*chip, c)).start()

        @pl.when(step == last)
        def _():
            for j, chip in enumerate(chips):
                qj = 2 * chip[0] + chip[1]
                for i in range(n):
                    mine, _ = _halves(out_refs[i].shape[1], c)
                    copy(j, i, qj, mine, (x, y, c)).wait_recv()
                    copy(3 + j, i, qj, mine, (x, y, 1 - c)).start()
        return

    @pl.when(step == last)
    def _():
        for j, chip in enumerate(chips):
            qj = 2 * chip[0] + chip[1]
            for i in range(n):
                mine, other = _halves(out_refs[i].shape[1], c)
                copy(3 + j, i, qj, other, (x, y, c)).wait_recv()
                copy(j, i, q, mine, (x, y, c)).wait_send()
                copy(3 + j, i, qj, mine, (x, y, c)).wait_send()


def _group_matrix(gw_ref, g):
    rows = GROUP_DIM // N_CHIPS
    return jnp.concatenate([gw_ref[j, rows * g:rows * (g + 1), :] for j in range(N_CHIPS)], axis=0)


def pool_forward(x, w0, wpi, gw, gb, scale, wpo, later):
    s = x.shape[0]
    ts = ROW_TILE
    nt = s // ts
    assert nt >= 2
    n_later = len(later)

    def body(x_ref, w0_ref, wpi_ref, gw_ref, gb_ref, sc_ref, wpo_ref, *rest):
        rest = rest[n_later:]
        h1_ref, pooled_ref, gt_ref, n0_ref = rest[:4]
        later_refs = rest[4:4 + n_later]
        ubuf, tbuf, hist, send_sems, recv_sems = rest[4 + n_later:]
        i = pl.program_id(0)
        gather_in_background(i, nt - 1, later_refs, send_sems, recv_sems, finish=False)
        xv = x_ref[...]
        r = lax.rsqrt(jnp.mean(xv * xv, axis=-1, keepdims=True) + EPS)
        n0 = _bf(xv * r * w0_ref[...])
        n0_ref[...] = n0
        u = jnp.concatenate([_nn(n0, wpi_ref[0]), _nn(n0, wpi_ref[1])], axis=-1)
        gt = jnp.concatenate([_nn(n0, wpi_ref[2]), _nn(n0, wpi_ref[3])], axis=-1)
        gt_ref[...] = gt

        @pl.when(i == 0)
        def _():
            hist[...] = jnp.zeros_like(hist)

        ubuf[0:HALO, :] = hist[...]
        ubuf[HALO:HALO + ts, :] = u
        hist[...] = u[ts - HALO:, :]
        inv = _inverse_counts(_row_index(i, ts))
        mixed = []
        for g, w in enumerate(POOL_WINDOWS):
            cols = slice(g * GROUP_DIM, (g + 1) * GROUP_DIM)
            pooled = _bf(_trailing_sums(ubuf, tbuf, cols, w, ts) * inv[g] - u[:, cols])
            pooled_ref[:, cols] = pooled
            mixed.append(_nn(pooled, _group_matrix(gw_ref, g)))
        mixed = jnp.concatenate(mixed, axis=-1) + gb_ref[...]
        y = mixed * sc_ref[...] * (gt * _sigmoid(gt))
        h1_ref[...] = xv + _nn(_bf(y), wpo_ref[...])
        gather_in_background(i, nt - 1, later_refs, send_sems, recv_sems, finish=True)

    row = lambda cols: pl.BlockSpec((ts, cols), lambda i: (i, 0))
    outs = pl.pallas_call(
        body, name="pool_forward", grid=(nt,),
        out_shape=[jax.ShapeDtypeStruct((s, D), F32), jax.ShapeDtypeStruct((s, D), BF16),
                   jax.ShapeDtypeStruct((s, D), F32), jax.ShapeDtypeStruct((s, D), BF16)]
                  + [jax.ShapeDtypeStruct(a.shape, a.dtype) for a in later],
        in_specs=[row(D), _full((1, D)), _full((N_CHIPS, D, D // 2)), _full((GROUPS, GROUP_DIM, GROUP_DIM)),
                  _full((1, D)), _full((1, D)), _full((D, D))] + _any_specs(n_later),
        out_specs=[row(D), row(D), row(D), row(D)] + _any_specs(n_later),
        input_output_aliases={7 + k: 4 + k for k in range(n_later)},
        scratch_shapes=[pltpu.VMEM((HALO + ts, D), F32), pltpu.VMEM((HALO + ts, D), F32),
                        pltpu.VMEM((HALO, D), F32),
                        pltpu.SemaphoreType.DMA((6 * n_later,)), pltpu.SemaphoreType.DMA((6 * n_later,))],
        compiler_params=_params("arbitrary"),
    )(x, w0, wpi, gw, gb, scale, wpo, *later)
    return outs[:4], outs[4:]


def pool_backward(x, dh1, pooled, gt, w0, wpi, gw, gb, scale, wpo, chip_sums):
    s = x.shape[0]
    ts = ROW_TILE
    nt = s // ts
    n_sums = len(chip_sums)

    def body(x_ref, dh1_ref, pooled_ref, gt_ref, w0_ref, wpi_ref, gw_ref, gb_ref, sc_ref, wpo_ref, *rest):
        sum_refs, rest = rest[:n_sums], rest[n_sums:]
        dx_ref, dproj_ref, gpo_ref, ggw_ref, small_ref = rest[:5]
        got_refs = rest[5:5 + n_sums]
        ebuf, tbuf, ahead, send_sems, recv_sems = rest[5 + n_sums:]
        i = pl.program_id(0)
        copies = _scatter_copies(sum_refs, got_refs, send_sems, recv_sems)

        @pl.when(i == 0)
        def _():
            for cp in copies:
                cp.start()

        @pl.when(i == 0)
        def _():
            gpo_ref[...] = jnp.zeros_like(gpo_ref)
            ggw_ref[...] = jnp.zeros_like(ggw_ref)
            small_ref[...] = jnp.zeros_like(small_ref)
            ahead[...] = jnp.zeros_like(ahead)

        dh1 = dh1_ref[...]
        dh1_bf = _bf(dh1)
        gt = gt_ref[...]
        sc = sc_ref[...]
        dy = _nt(dh1_bf, wpo_ref[...])
        pooled_bf = []
        mixed = []
        for g in range(GROUPS):
            cols = slice(g * GROUP_DIM, (g + 1) * GROUP_DIM)
            pb = pooled_ref[:, cols]
            pooled_bf.append(pb)
            mixed.append(_nn(pb, _group_matrix(gw_ref, g)))
        mixed = jnp.concatenate(mixed, axis=-1) + gb_ref[...]
        sg = _sigmoid(gt)
        silu = gt * sg
        gpo_ref[...] += _tn(_bf(mixed * sc * silu), dh1_bf)
        dmixed = dy * sc * silu
        dgt = dy * mixed * sc * (sg * (1.0 + gt * (1.0 - sg)))
        dproj_ref[:, D:] = _bf(dgt)
        small_ref[1:2, :] += jnp.sum(dy * mixed * silu, axis=0, keepdims=True)
        small_ref[2:3, :] += jnp.sum(dmixed, axis=0, keepdims=True)

        inv = _inverse_counts(_row_index(nt - 1 - i, ts))
        rows_q = GROUP_DIM // N_CHIPS
        ebuf[ts:ts + HALO, :] = ahead[...]
        dpooled = []
        for g in range(GROUPS):
            cols = slice(g * GROUP_DIM, (g + 1) * GROUP_DIM)
            dm = _bf(dmixed[:, cols])
            ggw = _tn(pooled_bf[g], dm)
            for j in range(N_CHIPS):
                ggw_ref[j, rows_q * g:rows_q * (g + 1), :] += ggw[rows_q * j:rows_q * (j + 1), :]
            dp = _nt(dm, _group_matrix(gw_ref, g))
            dpooled.append(dp)
            ebuf[0:ts, cols] = dp * inv[g]
        ahead[...] = ebuf[0:HALO, :]
        du = []
        for g, w in enumerate(POOL_WINDOWS):
            cols = slice(g * GROUP_DIM, (g + 1) * GROUP_DIM)
            du.append(_leading_sums(ebuf, tbuf, cols, w, ts) - dpooled[g])
        du = _bf(jnp.concatenate(du, axis=-1))
        dproj_ref[:, :D] = du
        dgt_bf = _bf(dgt)
        half = D // 2
        dn0 = (_nt(du[:, :half], wpi_ref[0]) + _nt(du[:, half:], wpi_ref[1])
               + _nt(dgt_bf[:, :half], wpi_ref[2]) + _nt(dgt_bf[:, half:], wpi_ref[3]))

        xv = x_ref[...]
        r = lax.rsqrt(jnp.mean(xv * xv, axis=-1, keepdims=True) + EPS)
        xhat = xv * r
        small_ref[0:1, :] += jnp.sum(dn0 * xhat, axis=0, keepdims=True)
        dxh = dn0 * w0_ref[...]
        dx_ref[...] = dh1 + r * (dxh - xhat * jnp.mean(dxh * xhat, axis=-1, keepdims=True))

        @pl.when(i == nt - 1)
        def _():
            for cp in copies:
                cp.wait()

    row = lambda cols: pl.BlockSpec((ts, cols), lambda i: (nt - 1 - i, 0))
    outs = pl.pallas_call(
        body, name="pool_backward", grid=(nt,),
        out_shape=[jax.ShapeDtypeStruct((s, D), F32), jax.ShapeDtypeStruct((s, 2 * D), BF16),
                   jax.ShapeDtypeStruct((D, D), F32),
                   jax.ShapeDtypeStruct((GROUPS, GROUP_DIM, GROUP_DIM), F32),
                   jax.ShapeDtypeStruct((8, D), F32)] + _scatter_shapes(chip_sums),
        in_specs=[row(D), row(D), row(D), row(D), _full((1, D)), _full((N_CHIPS, D, D // 2)),
                  _full((GROUPS, GROUP_DIM, GROUP_DIM)), _full((1, D)), _full((1, D)), _full((D, D))]
                 + _any_specs(n_sums),
        out_specs=[row(D), row(2 * D), _full((D, D)), _full((GROUPS, GROUP_DIM, GROUP_DIM)), _full((8, D))]
                  + _any_specs(n_sums),
        scratch_shapes=[pltpu.VMEM((ts + HALO, D), F32), pltpu.VMEM((ts + HALO, D), F32),
                        pltpu.VMEM((HALO, D), F32),
                        pltpu.SemaphoreType.DMA((3 * n_sums,)), pltpu.SemaphoreType.DMA((3 * n_sums,))],
        compiler_params=_params("arbitrary"),
    )(x, dh1, pooled, gt, w0, wpi, gw, gb, scale, wpo, *chip_sums)
    return outs[:5], outs[5:]


def gla_project(h1, w1, wgi_q, wgk, bgk, later):
    s = h1.shape[0]
    ts = ROW_TILE
    nt = s // ts
    assert nt >= 2
    n_later = len(later)

    def body(h_ref, w1_ref, wq_ref, wgk_ref, bgk_ref, *rest):
        rest = rest[n_later:]
        qk_ref, v_ref, gate_ref, low_ref, cum_ref, n1_ref = rest[:6]
        later_refs = rest[6:6 + n_later]
        send_sems, recv_sems, wgi_ref = rest[6 + n_later:]
        gather_in_background(pl.program_id(0), nt - 1, later_refs, send_sems, recv_sems, finish=False)

        @pl.when(pl.program_id(0) == 0)
        def _():
            _assemble_gla_in(wq_ref, wgi_ref)

        hv = h_ref[...]
        r = lax.rsqrt(jnp.mean(hv * hv, axis=-1, keepdims=True) + EPS)
        n1 = _bf(hv * r * w1_ref[...])
        n1_ref[...] = n1
        qk_ref[...] = _nn(n1, wgi_ref[:, 0:2 * KEY_W])
        v_ref[...] = _bf(_nn(n1, wgi_ref[:, 2 * KEY_W:2 * KEY_W + D]))
        gate_ref[...] = _nn(n1, wgi_ref[:, 2 * KEY_W + D:GLA_MAIN])
        low = _bf(_nn(n1, wgi_ref[:, GLA_MAIN:]))
        low_ref[...] = low
        z = _nn(low, wgk_ref[...]) + bgk_ref[...]
        lg = (jnp.minimum(z, 0.0) - jnp.log(1.0 + jnp.exp(-jnp.abs(z)))) / GATE_NORM
        lower_f = _chunk_masks()[0].astype(F32)
        for r0 in range(0, ts, CHUNK):
            cum_ref[r0:r0 + CHUNK, :] = _nn_exact(lower_f, lg[r0:r0 + CHUNK, :])
        gather_in_background(pl.program_id(0), nt - 1, later_refs, send_sems, recv_sems, finish=True)

    row = lambda cols: pl.BlockSpec((ts, cols), lambda i: (i, 0))
    outs = pl.pallas_call(
        body, name="gla_project", grid=(nt,),
        out_shape=[jax.ShapeDtypeStruct((s, D), F32), jax.ShapeDtypeStruct((s, D), BF16),
                   jax.ShapeDtypeStruct((s, D), F32), jax.ShapeDtypeStruct((s, RANK_PAD), BF16),
                   jax.ShapeDtypeStruct((s, KEY_W), F32), jax.ShapeDtypeStruct((s, D), BF16)]
                  + [jax.ShapeDtypeStruct(a.shape, a.dtype) for a in later],
        in_specs=[row(D), _full((1, D)), _full((N_CHIPS, D, GLA_IN_QUARTER)),
                  _full((RANK_PAD, KEY_W)), _full((1, KEY_W))] + _any_specs(n_later),
        out_specs=[row(D), row(D), row(D), row(RANK_PAD), row(KEY_W), row(D)] + _any_specs(n_later),
        input_output_aliases={5 + k: 6 + k for k in range(n_later)},
        scratch_shapes=[pltpu.SemaphoreType.DMA((6 * n_later,)), pltpu.SemaphoreType.DMA((6 * n_later,)),
                        pltpu.VMEM((D, GLA_MAIN + RANK_PAD), BF16)],
        compiler_params=_params("arbitrary"),
    )(h1, w1, wgi_q, wgk, bgk, *later)
    return outs[:6], outs[6:]


def _assemble_gla_in(wq_ref, wfull):
    pad = jnp.zeros((CAST_ROWS, GLA_MAIN + RANK_PAD - GLA_IN), BF16)
    for r0 in range(0, D, CAST_ROWS):
        rows = slice(r0, r0 + CAST_ROWS)
        wfull[rows, :] = jnp.concatenate([wq_ref[q, rows, :] for q in range(N_CHIPS)] + [pad], axis=1)


GLA_BLOCK = 512
CHUNKS_PER_BLOCK = GLA_BLOCK // CHUNK


def _chunk_masks():
    t = lax.broadcasted_iota(jnp.int32, (CHUNK, CHUNK), 0)
    u = lax.broadcasted_iota(jnp.int32, (CHUNK, CHUNK), 1)
    return t >= u, t <= u


def _gla_chunk_terms(q, cum):
    ep = jnp.exp(cum)
    en = jnp.exp(-cum)
    qs = q * (HEAD_K ** -0.5)
    last = cum[CHUNK - 1:CHUNK, :]
    ed = jnp.exp(last - cum)
    dec = jnp.exp(last)
    return ep, en, qs, ed, dec


def gla_forward(qk, v, cum):
    s = qk.shape[0]
    nb = s // GLA_BLOCK
    nc = s // CHUNK

    def body(q_ref, k_ref, v_ref, cum_ref, o_ref, st_ref, sc_ref, state):
        @pl.when(pl.program_id(0) == 0)
        def _():
            state[...] = jnp.zeros_like(state)

        lower, _ = _chunk_masks()

        def chunk(cc, carry):
            rows = pl.ds(pl.multiple_of(cc * CHUNK, CHUNK), CHUNK)
            for h in range(HEADS):
                kc = slice(h * HEAD_K, (h + 1) * HEAD_K)
                vc = slice(h * HEAD_V, (h + 1) * HEAD_V)
                q = q_ref[rows, kc]
                k = k_ref[rows, kc]
                v = v_ref[rows, vc]
                ep, en, qs, ed, dec = _gla_chunk_terms(q, cum_ref[rows, kc])
                a = _bf(qs * ep)
                fwd = _nt(a, _bf(k * en))
                bwd = _nt(_bf(qs * en), _bf(k * ep))
                scores = _bf(jnp.where(lower, fwd, bwd))
                sc_ref[rows, h * CHUNK:(h + 1) * CHUNK] = scores
                st = state[h]
                st_ref[cc, h] = st
                o_ref[rows, vc] = _nn(scores, v) + _nt(a, _bf(st))
                state[h] = st * dec + _tn(v, _bf(k * ed))
            return carry

        lax.fori_loop(0, CHUNKS_PER_BLOCK, chunk, 0, unroll=4)

    return pl.pallas_call(
        body, name="gla_forward", grid=(nb,),
        out_shape=(jax.ShapeDtypeStruct((s, D), F32),
                   jax.ShapeDtypeStruct((nc, HEADS, HEAD_V, HEAD_K), F32),
                   jax.ShapeDtypeStruct((s, HEADS * CHUNK), BF16)),
        in_specs=[pl.BlockSpec((GLA_BLOCK, KEY_W), lambda i: (i, 0)),
                  pl.BlockSpec((GLA_BLOCK, KEY_W), lambda i: (i, 1)),
                  pl.BlockSpec((GLA_BLOCK, D), lambda i: (i, 0)),
                  pl.BlockSpec((GLA_BLOCK, KEY_W), lambda i: (i, 0))],
        out_specs=(pl.BlockSpec((GLA_BLOCK, D), lambda i: (i, 0)),
                   pl.BlockSpec((CHUNKS_PER_BLOCK, HEADS, HEAD_V, HEAD_K), lambda i: (i, 0, 0, 0)),
                   pl.BlockSpec((GLA_BLOCK, HEADS * CHUNK), lambda i: (i, 0))),
        scratch_shapes=[pltpu.VMEM((HEADS, HEAD_V, HEAD_K), F32)],
        compiler_params=_params("arbitrary"),
    )(qk, qk, v, cum)


def gla_backward(qk, v, cum, do, states, scores):
    s = qk.shape[0]
    nb = s // GLA_BLOCK

    def body(q_ref, k_ref, v_ref, cum_ref, do_ref, st_ref, sc_ref, dq_ref, dk_ref, dv_ref, dcum_ref, dstate):
        @pl.when(pl.program_id(0) == 0)
        def _():
            dstate[...] = jnp.zeros_like(dstate)

        lower, _ = _chunk_masks()
        is_last = lax.broadcasted_iota(jnp.int32, (CHUNK, HEAD_K), 0) == CHUNK - 1

        def chunk(step, carry):
            cc = CHUNKS_PER_BLOCK - 1 - step
            rows = pl.ds(pl.multiple_of(cc * CHUNK, CHUNK), CHUNK)
            for h in range(HEADS):
                kc = slice(h * HEAD_K, (h + 1) * HEAD_K)
                vc = slice(h * HEAD_V, (h + 1) * HEAD_V)
                q = q_ref[rows, kc]
                k = k_ref[rows, kc]
                v = v_ref[rows, vc]
                do_c = do_ref[rows, vc]
                ep, en, qs, ed, dec = _gla_chunk_terms(q, cum_ref[rows, kc])
                a = _bf(qs * ep)
                b = _bf(k * en)
                c = _bf(qs * en)
                dk_dec = _bf(k * ep)
                kd = _bf(k * ed)
                scores = sc_ref[rows, h * CHUNK:(h + 1) * CHUNK]
                st = st_ref[cc, h]
                dst = dstate[h]
                dst_bf = _bf(dst)

                dscores = _nt(do_c, v)
                dfwd = _bf(jnp.where(lower, dscores, 0.0))
                dbwd = _bf(jnp.where(lower, 0.0, dscores))
                dv_ref[rows, vc] = _bf(_tn(scores, do_c) + _nt(kd, dst_bf))
                da = _nn(dfwd, b) + _nn(do_c, _bf(st))
                db = _tn(dfwd, a)
                dc = _nn(dbwd, dk_dec)
                ddk = _tn(dbwd, c)
                dkd = _nn(v, dst_bf)
                ddec = jnp.sum(dst * st, axis=0, keepdims=True)
                dstate[h] = dst * dec + _tn(do_c, a)

                m = dkd * k * ed
                dq_ref[rows, kc] = _bf((da * ep + dc * en) * (HEAD_K ** -0.5))
                dk_ref[rows, kc] = _bf(db * en + ddk * ep + dkd * ed)
                dcum = (da * qs + ddk * k) * ep - (db * k + dc * qs) * en - m
                dlast = jnp.sum(m, axis=0, keepdims=True) + ddec * dec
                dcum_ref[rows, kc] = dcum + jnp.where(is_last, dlast, 0.0)
            return carry

        lax.fori_loop(0, CHUNKS_PER_BLOCK, chunk, 0, unroll=4)

    rev = lambda cols, col_block: pl.BlockSpec((GLA_BLOCK, cols), lambda i: (nb - 1 - i, col_block))
    return pl.pallas_call(
        body, name="gla_backward", grid=(nb,),
        out_shape=(jax.ShapeDtypeStruct((s, KEY_W), BF16), jax.ShapeDtypeStruct((s, KEY_W), BF16),
                   jax.ShapeDtypeStruct((s, D), BF16), jax.ShapeDtypeStruct((s, KEY_W), F32)),
        in_specs=[rev(KEY_W, 0), rev(KEY_W, 1), rev(D, 0), rev(KEY_W, 0), rev(D, 0),
                  pl.BlockSpec((CHUNKS_PER_BLOCK, HEADS, HEAD_V, HEAD_K), lambda i: (nb - 1 - i, 0, 0, 0)),
                  rev(HEADS * CHUNK, 0)],
        out_specs=(rev(KEY_W, 0), rev(KEY_W, 0), rev(D, 0), rev(KEY_W, 0)),
        scratch_shapes=[pltpu.VMEM((HEADS, HEAD_V, HEAD_K), F32)],
        compiler_params=_params("arbitrary"),
    )(qk, qk, v, cum, do, states, scores)


def head_and_loss(o, gate, h1, target, hw, wgo, wf):
    s = o.shape[0]
    ts = ROW_TILE

    def body(o_ref, gate_ref, h1_ref, tgt_ref, hw_ref, wgo_ref, wf_ref,
             dh2_ref, do_ref, dgate_ref, ggo_ref, small_ref):
        @pl.when(pl.program_id(0) == 0)
        def _():
            ggo_ref[...] = jnp.zeros_like(ggo_ref)
            small_ref[...] = jnp.zeros_like(small_ref)

        gate = gate_ref[...]
        hw = hw_ref[...]
        sg = _sigmoid(gate)
        silu = gate * sg
        ohat, ro = [], []
        for h in range(HEADS):
            oh = o_ref[:, h * HEAD_V:(h + 1) * HEAD_V]
            rh = lax.rsqrt(jnp.mean(oh * oh, axis=-1, keepdims=True) + EPS)
            ro.append(rh)
            ohat.append(oh * rh)
        ohat = jnp.concatenate(ohat, axis=-1)
        on = ohat * hw
        y2 = _bf(on * silu)
        h2 = h1_ref[...] + _nn(y2, wgo_ref[...])
        rf = lax.rsqrt(jnp.mean(h2 * h2, axis=-1, keepdims=True) + EPS)
        h2hat = h2 * rf
        wf = wf_ref[...]
        diff = h2hat * wf - tgt_ref[...]
        small_ref[2:3, :] += jnp.zeros((1, D), F32) + 0.5 * jnp.sum(diff * diff) / D
        dout = diff / D
        small_ref[0:1, :] += jnp.sum(dout * h2hat, axis=0, keepdims=True)
        dxh = dout * wf
        dh2 = rf * (dxh - h2hat * jnp.mean(dxh * h2hat, axis=-1, keepdims=True))
        dh2_ref[...] = dh2
        dh2_bf = _bf(dh2)
        ggo_ref[...] += _tn(y2, dh2_bf)
        dy2 = _nt(dh2_bf, wgo_ref[...])
        don = dy2 * silu
        dgate_ref[...] = _bf(dy2 * on * (sg * (1.0 + gate * (1.0 - sg))))
        ghw = jnp.sum(don * ohat, axis=0, keepdims=True)
        small_ref[1:2, 0:HEAD_V] += sum(ghw[:, h * HEAD_V:(h + 1) * HEAD_V] for h in range(HEADS))
        dohat = don * hw
        for h in range(HEADS):
            cols = slice(h * HEAD_V, (h + 1) * HEAD_V)
            oh, dh = ohat[:, cols], dohat[:, cols]
            do_ref[:, cols] = _bf(ro[h] * (dh - oh * jnp.mean(dh * oh, axis=-1, keepdims=True)))

    row = lambda cols: pl.BlockSpec((ts, cols), lambda i: (i, 0))
    act = jax.ShapeDtypeStruct((s, D), F32)
    act_bf = jax.ShapeDtypeStruct((s, D), BF16)
    return pl.pallas_call(
        body, name="head_and_loss", grid=(s // ts,),
        out_shape=(act, act_bf, act_bf, jax.ShapeDtypeStruct((D, D), F32), jax.ShapeDtypeStruct((8, D), F32)),
        in_specs=[row(D), row(D), row(D), row(D),
                  _full((1, D)), _full((D, D)), _full((1, D))],
        out_specs=(row(D), row(D), row(D), _full((D, D)), _full((8, D))),
        compiler_params=_params("arbitrary"),
    )(o, gate, h1, target, hw, wgo, wf)


def gla_project_backward(dq, dk, dv, dgate, dcum, low, h1, dh2, w1, wgi_q, wgk, bgk):
    s = h1.shape[0]
    ts = ROW_TILE

    def body(dq_ref, dk_ref, dv_ref, dgate_ref, dcum_ref, low_ref, h1_ref, dh2_ref, w1_ref,
             wq_ref, wgk_ref, bgk_ref, dh1_ref, dproj_ref, ggk_ref, small_ref, wgi_ref):
        @pl.when(pl.program_id(0) == 0)
        def _():
            ggk_ref[...] = jnp.zeros_like(ggk_ref)
            small_ref[...] = jnp.zeros_like(small_ref)
            _assemble_gla_in(wq_ref, wgi_ref)

        low = low_ref[...]
        z = _nn(low, wgk_ref[...]) + bgk_ref[...]
        upper_f = _chunk_masks()[1].astype(F32)
        dlg = jnp.concatenate([_nn_exact(upper_f, dcum_ref[r0:r0 + CHUNK, :]) for r0 in range(0, ts, CHUNK)],
                              axis=0)
        dz = dlg * (1.0 / GATE_NORM) * _sigmoid(-z)
        dz_bf = _bf(dz)
        ggk_ref[...] += _tn(low, dz_bf)
        small_ref[1:2, 0:KEY_W] += jnp.sum(dz, axis=0, keepdims=True)
        dlow = _bf(_nt(dz_bf, wgk_ref[...]))
        dproj_ref[:, GLA_MAIN:] = dlow
        dn1 = _nt(dlow, wgi_ref[:, GLA_MAIN:])
        for ref, lo, hi in ((dq_ref, 0, KEY_W), (dk_ref, KEY_W, 2 * KEY_W),
                            (dv_ref, 2 * KEY_W, 2 * KEY_W + D), (dgate_ref, 2 * KEY_W + D, GLA_MAIN)):
            piece = ref[...]
            dproj_ref[:, lo:hi] = piece
            dn1 = dn1 + _nt(piece, wgi_ref[:, lo:hi])
        hv = h1_ref[...]
        r = lax.rsqrt(jnp.mean(hv * hv, axis=-1, keepdims=True) + EPS)
        hhat = hv * r
        small_ref[0:1, :] += jnp.sum(dn1 * hhat, axis=0, keepdims=True)
        dxh = dn1 * w1_ref[...]
        dh1_ref[...] = dh2_ref[...] + r * (dxh - hhat * jnp.mean(dxh * hhat, axis=-1, keepdims=True))

    row = lambda cols: pl.BlockSpec((ts, cols), lambda i: (i, 0))
    return pl.pallas_call(
        body, name="gla_project_backward", grid=(s // ts,),
        out_shape=(jax.ShapeDtypeStruct((s, D), F32), jax.ShapeDtypeStruct((s, GLA_MAIN + RANK_PAD), BF16),
                   jax.ShapeDtypeStruct((RANK_PAD, KEY_W), F32),
                   jax.ShapeDtypeStruct((8, D), F32)),
        in_specs=[row(KEY_W), row(KEY_W), row(D), row(D), row(KEY_W), row(RANK_PAD), row(D), row(D),
                  _full((1, D)), _full((N_CHIPS, D, GLA_IN_QUARTER)), _full((RANK_PAD, KEY_W)),
                  _full((1, KEY_W))],
        out_specs=(row(D), row(GLA_MAIN + RANK_PAD), _full((RANK_PAD, KEY_W)), _full((8, D))),
        scratch_shapes=[pltpu.VMEM((D, GLA_MAIN + RANK_PAD), BF16)],
        compiler_params=_params("arbitrary"),
    )(dq, dk, dv, dgate, dcum, low, h1, dh2, w1, wgi_q, wgk, bgk)


def local_gradients(xs, target, w0, w1, wf, wpi, gw, gb, scale, wpo, gla_quarters, wgk, bgk, hw_tiled, place):
    wgi_q, wgo_q = gla_quarters
    (h1, pooled, gt, n0), (wgi_q,) = pool_forward(xs, w0, wpi, gw, gb, scale, wpo, [wgi_q])
    (qk, v, gate, low, cum, n1), (wgo_q,) = gla_project(h1, w1, wgi_q, wgk, bgk, [wgo_q])
    wgo = wgo_q.reshape(D, D)
    o, states, scores = gla_forward(qk, v, cum)

    dh2, do, dgate, g_gla_out, small_top = head_and_loss(o, gate, h1, target, hw_tiled, wgo, wf)
    dq, dk, dv, dcum = gla_backward(qk, v, cum, do, states, scores)
    dh1, dproj, g_gk_pad, small_gla = gla_project_backward(
        dq, dk, dv, dgate, dcum, low, h1, dh2, w1, wgi_q, wgk, bgk)
    g_gla_in, _ = matmul_tn(n1, dproj, "grad_gla_in", tile_n=(GLA_MAIN + RANK_PAD) // 5)

    def chip_sums(grads, tag):
        return add_halves(grads, place, "add_halves_" + tag)

    gla_sums = chip_sums([g_gla_in, g_gla_out.reshape(N_CHIPS, D // N_CHIPS, D)], "gla")
    (dx, dpool, g_pool_out, g_group_w, small_pool), gla_got = pool_backward(
        xs, dh1, pooled, gt, w0, wpi, gw, gb, scale, wpo, [b for _, b in gla_sums])
    mix_sums = chip_sums([g_group_w, g_pool_out.reshape(N_CHIPS, D // N_CHIPS, D)], "pool_mix")
    g_pool_in, mix_got = matmul_tn(n0, dpool, "grad_pool_in", tile_n=D // 2, by_column_tile=True,
                                   chip_sums=[b for _, b in mix_sums])

    in_sums = add_halves([g_pool_in], place, "add_halves_and_scatter_pool_in", scatter=True)
    reduced, total = join_halves(
        [f for f, _ in in_sums + mix_sums + gla_sums], [got for _, got in in_sums] + list(mix_got) + list(gla_got),
        small_pool, small_gla, small_top, g_gk_pad)
    return dx, reduced, total


def kernel(x, norm_w, pool_in_w, pool_group_w, pool_group_b, pool_scale, pool_out_w, gla_in_w, gla_gk_w, gla_gk_b, gla_head_norm_w, gla_out_w, final_norm_w, loss_target, m_norm_w, m_pool_in_w, m_pool_group_w, m_pool_group_b, m_pool_scale, m_pool_out_w, m_gla_in_w, m_gla_gk_w, m_gla_gk_b, m_gla_head_norm_w, m_gla_out_w, m_final_norm_w, v_norm_w, v_pool_in_w, v_pool_group_w, v_pool_group_b, v_pool_scale, v_pool_out_w, v_gla_in_w, v_gla_gk_w, v_gla_gk_b, v_gla_head_norm_w, v_gla_out_w, v_final_norm_w):
    xs = x[0]
    target = loss_target[0]
    q_chip = 2 * lax.axis_index("x") + lax.axis_index("y")
    place = jnp.stack([lax.axis_index("c"), q_chip]).astype(jnp.int32)

    (wpi, gw_q, wpo_q, wgi_q, wgo_q), (bgk, hw_tiled, gb, wgk) = allgather_weights(
        [pool_in_w[0], pool_group_w[0].reshape(GROUP_DIM, GROUP_DIM), pool_out_w[0], gla_in_w[0], gla_out_w[0]],
        exchange=(True, True, True, False, False),
        smalls=[gla_gk_b, gla_head_norm_w, pool_group_b[0], gla_gk_w[0]])
    wpo = wpo_q.reshape(D, D)

    w0 = norm_w[0:1]
    w1 = norm_w[1:2]
    wf = final_norm_w.reshape(1, D)

    dx, reduced, total = local_gradients(
        xs, target, w0, w1, wf, wpi, gw_q, gb, pool_scale, wpo, [wgi_q, wgo_q], wgk, bgk, hw_tiled, place)
    r_pool_in, r_group_w, r_pool_out, r_gla_in, r_gla_out = reduced
    r_group_w = r_group_w.reshape(GROUPS, 64, GROUP_DIM)

    turn = lambda a: jnp.transpose(a, (2, 0, 1))
    back = lambda a: jnp.transpose(a, (1, 2, 0))
    as2d = lambda a, w: a.reshape(-1, w.shape[-1])
    big_names = ("pool_in_w", "pool_group_w", "pool_out_w", "gla_in_w", "gla_out_w")
    big_args = [(pool_in_w, r_pool_in[None], m_pool_in_w, v_pool_in_w),
                (pool_group_w, r_group_w[None], m_pool_group_w, v_pool_group_w),
                (pool_out_w, r_pool_out[None], m_pool_out_w, v_pool_out_w),
                (gla_in_w, r_gla_in[None], m_gla_in_w, v_gla_in_w),
                (gla_out_w, r_gla_out[None], m_gla_out_w, v_gla_out_w)]
    to_kernel = lambda n, a, w: turn(a) if n == "gla_in_w" else as2d(a, w)
    from_kernel = lambda n, a, w: back(a) if n == "gla_in_w" else a.reshape(w.shape)
    big_in = [tuple(to_kernel(n, a, p[0]) for a in p) for n, p in zip(big_names, big_args)]
    big_out = adamw(big_in, "adamw")
    big = {n: (from_kernel(n, i[1], p[0]),) + tuple(from_kernel(n, o, p[0]) for o in out)
           for n, p, i, out in zip(big_names, big_args, big_in, big_out)}

    small_names = ("norm_w", "pool_group_b", "pool_scale", "gla_gk_w", "gla_gk_b", "gla_head_norm_w",
                   "final_norm_w")
    small_args = [(norm_w, m_norm_w, v_norm_w),
                  (pool_group_b, m_pool_group_b, v_pool_group_b),
                  (pool_scale, m_pool_scale, v_pool_scale),
                  (gla_gk_w, m_gla_gk_w, v_gla_gk_w),
                  (gla_gk_b, m_gla_gk_b, v_gla_gk_b),
                  (gla_head_norm_w, m_gla_head_norm_w, v_gla_head_norm_w),
                  (final_norm_w, m_final_norm_w, v_final_norm_w)]
    small_out, loss = adamw_small([tuple(as2d(a, p[0]) for a in p) for p in small_args], total, place)
    small = {n: tuple(o.reshape(p[0].shape) for o in out) for n, p, out in zip(small_names, small_args, small_out)}
    results = [
        small["norm_w"],
        big["pool_in_w"],
        big["pool_group_w"],
        small["pool_group_b"],
        small["pool_scale"],
        big["pool_out_w"],
        big["gla_in_w"],
        small["gla_gk_w"],
        small["gla_gk_b"],
        small["gla_head_norm_w"],
        big["gla_out_w"],
        small["final_norm_w"],
    ]
    grads, deltas, new_m, new_v = zip(*results)
    return (loss.reshape(()), dx[None], *grads, *deltas, *new_m, *new_v)
```

```python
import jax
import jax.numpy as jnp
from jax import lax
from jax.experimental import pallas as pl
from jax.experimental.pallas import tpu as pltpu

F32 = jnp.float32
BF16 = jnp.bfloat16
MESH = pl.DeviceIdType.MESH

D = 1024
POOL_WINDOWS = (2, 4, 8, 16)
GROUPS = 4
GROUP_DIM = 256
HEADS = 4
HEAD_K = 128
HEAD_V = 256
KEY_W = 512
CHUNK = 64
GATE_RANK = 16
GATE_NORM = 16.0
GLA_IN = 3088
GLA_MAIN = 3072
RANK_PAD = 128
EPS = 1e-6
HALO = 32

ADAM_LR = 0.001
ADAM_B1 = 0.9
ADAM_B2 = 0.999
ADAM_EPS = 1e-08
ADAM_WD = 0.01
ADAM_STEP = 10

N_CHIPS = 4
N_DEV = 8
GLA_IN_QUARTER = GLA_IN // N_CHIPS

VMEM_LIMIT = 56 * 1024 * 1024


def _nn(a, b):
    return lax.dot_general(a, b, (((1,), (0,)), ((), ())), preferred_element_type=F32)


def _nt(a, b):
    return lax.dot_general(a, b, (((1,), (1,)), ((), ())), preferred_element_type=F32)


def _tn(a, b):
    return lax.dot_general(a, b, (((0,), (0,)), ((), ())), preferred_element_type=F32)


def _nn_exact(a, b):
    return lax.dot_general(a, b, (((1,), (0,)), ((), ())), preferred_element_type=F32,
                           precision=lax.Precision.HIGHEST)


def _bf(a):
    return a.astype(BF16)


def _params(*sem):
    return pltpu.CompilerParams(dimension_semantics=sem, vmem_limit_bytes=VMEM_LIMIT)


def _full(shape):
    return pl.BlockSpec(shape, lambda i: (0,) * len(shape))


def _position():
    return lax.axis_index("x"), lax.axis_index("y"), lax.axis_index("c")


def _gather_small(in_ref, all_ref, send_sems, recv_sems, local_sem):
    x, y, c = _position()
    me = 4 * x + 2 * y + c
    mine = pltpu.make_async_copy(in_ref, all_ref.at[me], local_sem)
    sends = []
    for k in range(N_DEV - 1):
        fx, fy, fc = (k + 1) >> 2 & 1, (k + 1) >> 1 & 1, (k + 1) & 1
        sends.append(pltpu.make_async_remote_copy(
            src_ref=in_ref, dst_ref=all_ref.at[me],
            send_sem=send_sems.at[k], recv_sem=recv_sems.at[k],
            device_id=(x ^ fx, y ^ fy, c ^ fc), device_id_type=MESH))

    def start():
        mine.start()
        for cp in sends:
            cp.start()

    def wait():
        for k in range(N_DEV - 1):
            fx, fy, fc = (k + 1) >> 2 & 1, (k + 1) >> 1 & 1, (k + 1) & 1
            src_dev = 4 * (x ^ fx) + 2 * (y ^ fy) + (c ^ fc)
            pltpu.make_async_remote_copy(
                src_ref=in_ref, dst_ref=all_ref.at[src_dev],
                send_sem=send_sems.at[k], recv_sem=recv_sems.at[k],
                device_id=(x, y, c), device_id_type=MESH).wait_recv()
        for cp in sends:
            cp.wait_send()
        mine.wait()

    return start, wait


SMALL_SEMS = [pltpu.SemaphoreType.DMA((N_DEV - 1,)), pltpu.SemaphoreType.DMA((N_DEV - 1,)),
              pltpu.SemaphoreType.DMA]
VMEM_SPEC = pl.BlockSpec(memory_space=pltpu.VMEM)


def _other_chips(x, y):
    return [(1 - x, y), (x, 1 - y), (1 - x, 1 - y)]


def _any_specs(n):
    return [pl.BlockSpec(memory_space=pl.ANY)] * n


def _halves(rows, c):
    half = rows // 2
    return pl.ds(c * half, half), pl.ds((1 - c) * half, half)


CAST_ROWS = 256


def _gather_copy(out_ref, send_sems, recv_sems, k, quarter, half, to, src=None):
    dst = out_ref.at[quarter, half]
    return pltpu.make_async_remote_copy(
        src_ref=dst if src is None else src, dst_ref=dst,
        send_sem=send_sems.at[k], recv_sem=recv_sems.at[k], device_id=to, device_id_type=MESH)


SMALL_IN_ROWS = 24


def allgather_weights(quarters, exchange, smalls):
    n = len(quarters)
    shapes = [w.shape for w in quarters]
    moved = [i for i in range(n) if exchange[i]]

    def body(*refs):
        w_refs, (gkb_ref, hnw_ref, gb_ref, gkw_ref) = refs[:n], refs[n:n + 4]
        out_refs, (bgk_ref, hw_ref, gbias_ref, wgk_ref) = refs[n + 4:2 * n + 4], refs[2 * n + 4:2 * n + 8]
        refs = refs[2 * n + 8:]
        f32_bufs, bf_bufs = refs[:n], refs[n:2 * n]
        send_sems, recv_sems, local_sems, small_ref, small_all_ref = refs[2 * n:2 * n + 5]
        small_ref[...] = jnp.zeros_like(small_ref)
        small_ref[0:1, :] = gkb_ref[...]
        small_ref[1:2, 0:64] = hnw_ref[...]
        small_ref[2:2 + GROUPS, 0:64] = gb_ref[...]
        small_ref[8:8 + GATE_RANK, :] = gkw_ref[...]
        start_small, wait_small = _gather_small(small_ref, small_all_ref, *refs[2 * n + 5:])
        start_small()
        x, y, c = _position()
        q = 2 * x + y
        sibling = (x, y, 1 - c)
        chips = _other_chips(x, y)

        def copy(k, i, quarter, half, to, src=None):
            return _gather_copy(out_refs[i], send_sems, recv_sems, k * n + i, quarter, half, to, src)

        loads = [pltpu.make_async_copy(w_refs[i], f32_bufs[i], local_sems.at[i]) for i in range(n)]
        for cp in loads:
            cp.start()
        keeps, sends = [], []
        for i in range(n):
            loads[i].wait()
            for r0 in range(0, shapes[i][0], CAST_ROWS):
                bf_bufs[i][r0:r0 + CAST_ROWS, :] = _bf(f32_bufs[i][r0:r0 + CAST_ROWS, :])
            keep = pltpu.make_async_copy(bf_bufs[i], out_refs[i].at[q], local_sems.at[n + i])
            keep.start()
            keeps.append(keep)
            if not exchange[i]:
                continue
            mine, _ = _halves(shapes[i][0], c)
            for j, chip in enumerate(chips):
                cp = copy(j, i, q, mine, (*chip, c), src=bf_bufs[i].at[mine])
                cp.start()
                sends.append(cp)
        for j, chip in enumerate(chips):
            qj = 2 * chip[0] + chip[1]
            for i in moved:
                mine, _ = _halves(shapes[i][0], c)
                copy(j, i, qj, mine, (x, y, c)).wait_recv()
                cp = copy(3 + j, i, qj, mine, sibling)
                cp.start()
                sends.append(cp)
        for j, chip in enumerate(chips):
            qj = 2 * chip[0] + chip[1]
            for i in moved:
                _, other = _halves(shapes[i][0], c)
                copy(3 + j, i, qj, other, (x, y, c)).wait_recv()
        wait_small()
        wgk_ref[...] = jnp.zeros_like(wgk_ref)
        for j in range(N_CHIPS):
            block = small_all_ref.at[2 * j]
            bgk_ref[:, 128 * j:128 * (j + 1)] = block[0:1, :]
            for h in range(HEADS):
                hw_ref[:, HEAD_V * h + 64 * j:HEAD_V * h + 64 * (j + 1)] = block[1:2, 0:64]
            for g in range(GROUPS):
                gbias_ref[:, GROUP_DIM * g + 64 * j:GROUP_DIM * g + 64 * (j + 1)] = block[2 + g:3 + g, 0:64]
            wgk_ref[0:GATE_RANK, 128 * j:128 * (j + 1)] = _bf(block[8:8 + GATE_RANK, :])
        for cp in sends:
            cp.wait_send()
        for cp in keeps:
            cp.wait()

    outs = pl.pallas_call(
        body, name="allgather_weights",
        out_shape=[jax.ShapeDtypeStruct((N_CHIPS, *s), BF16) for s in shapes]
                  + [jax.ShapeDtypeStruct((1, KEY_W), F32), jax.ShapeDtypeStruct((1, D), F32),
                     jax.ShapeDtypeStruct((1, D), F32), jax.ShapeDtypeStruct((RANK_PAD, KEY_W), BF16)],
        in_specs=_any_specs(n) + [VMEM_SPEC] * 4, out_specs=_any_specs(n) + [VMEM_SPEC] * 4,
        scratch_shapes=([pltpu.VMEM(s, F32) for s in shapes] + [pltpu.VMEM(s, BF16) for s in shapes]
                        + [pltpu.SemaphoreType.DMA((6 * n,)), pltpu.SemaphoreType.DMA((6 * n,)),
                           pltpu.SemaphoreType.DMA((2 * n,)), pltpu.VMEM((SMALL_IN_ROWS, 128), F32),
                           pltpu.VMEM((N_DEV, SMALL_IN_ROWS, 128), F32)] + SMALL_SEMS),
        compiler_params=pltpu.CompilerParams(vmem_limit_bytes=VMEM_LIMIT),
    )(*quarters, *smalls)
    return outs[:n], outs[n:]


def _scatter_copies(b_refs, got_refs, send_sems, recv_sems):
    n = len(b_refs)
    x, y, c = _position()
    copies = []
    for j, chip in enumerate(_other_chips(x, y)):
        qj = 2 * chip[0] + chip[1]
        for i in range(n):
            copies.append(pltpu.make_async_remote_copy(
                src_ref=b_refs[i].at[qj], dst_ref=got_refs[i].at[j],
                send_sem=send_sems.at[j * n + i], recv_sem=recv_sems.at[j * n + i],
                device_id=(*chip, c), device_id_type=MESH))
    return copies


def _scatter_shapes(chip_sums):
    return [jax.ShapeDtypeStruct((N_CHIPS - 1, *b.shape[1:]), BF16) for b in chip_sums]


ADD_ROWS = 512
ADD_HALVES_ROWS = 128


def _spans(counts):
    starts, total = [], 0
    for count in counts:
        starts.append(total)
        total += count
    return starts, total


def _local_step(t, start, count):
    return jnp.clip(t - start, 0, count - 1)


def add_halves(grads, place, name, scatter=False):
    n = len(grads)
    whole = [len(g.shape) == 2 for g in grads]
    halves = [g.shape[-2] // 2 for g in grads]
    cols = [GLA_IN_QUARTER if w else g.shape[-1] for g, w in zip(grads, whole)]
    rbs = [min(ADD_HALVES_ROWS, h) for h in halves]
    counts = [h // rb for h, rb in zip(halves, rbs)]
    starts, total = _spans(counts)
    half_shapes = [(*g.shape[:-2], h, g.shape[-1]) for g, h in zip(grads, halves)]

    def rows_of(ref, i, start):
        return ref.at[pl.ds(start, rbs[i])] if whole[i] else ref.at[:, pl.ds(start, rbs[i])]

    def body(place_ref, *refs):
        a_refs, o_refs = refs[:n], refs[n:2 * n]
        f_refs, h_refs = refs[2 * n:3 * n], refs[3 * n:4 * n]
        send_refs, their_refs, rest = refs[4 * n:5 * n], refs[5 * n:6 * n], refs[6 * n:]
        send_sems, recv_sems = rest[:2]
        t = pl.program_id(0)
        q = place_ref[1]
        x, y, c = _position()
        sum_refs = rest[2:2 + n] if scatter else h_refs

        def to_owners(i, k):
            out_sems, in_sems = rest[2 + n:]
            rows = pl.ds(k * rbs[i], rbs[i])
            return [pltpu.make_async_remote_copy(
                src_ref=sum_refs[i].at[2 * chip[0] + chip[1], rows], dst_ref=h_refs[i].at[j, rows],
                send_sem=out_sems.at[3 * (starts[i] + k) + j], recv_sem=in_sems.at[3 * (starts[i] + k) + j],
                device_id=(*chip, c), device_id_type=MESH) for j, chip in enumerate(_other_chips(x, y))]

        copies = [[pltpu.make_async_remote_copy(
            src_ref=rows_of(send_refs[i], i, k * rbs[i]), dst_ref=rows_of(their_refs[i], i, k * rbs[i]),
            send_sem=send_sems.at[starts[i] + k], recv_sem=recv_sems.at[starts[i] + k],
            device_id=(x, y, 1 - c), device_id_type=MESH) for k in range(counts[i])] for i in range(n)]

        for i in range(n):
            for k in range(counts[i]):
                @pl.when(t == starts[i] + k)
                def _(i=i, k=k):
                    rows_of(send_refs[i], i, k * rbs[i])[...] = _bf(o_refs[i][...])
                    copies[i][k].start()

        for i in range(n):
            for k in range(counts[i]):
                @pl.when(t == starts[i] + k + 1)
                def _(i=i, k=k):
                    copies[i][k].wait_recv()
                    b_ref = rows_of(their_refs[i], i, k * rbs[i])
                    h_ref = sum_refs[i].at[:, pl.ds(k * rbs[i], rbs[i])] if scatter else h_refs[i]
                    if not whole[i]:
                        h_ref[...] = _bf(a_refs[i][...] + b_ref[...].astype(F32))
                        f_refs[i][...] = a_refs[i][q] + b_ref[q].astype(F32)
                    else:
                        total_i = a_refs[i][...] + b_ref[...].astype(F32)
                        for k4 in range(N_CHIPS):
                            piece = total_i[:, k4 * cols[i]:(k4 + 1) * cols[i]]
                            h_ref[k4] = _bf(piece)

                            @pl.when(q == k4)
                            def _():
                                f_refs[i][...] = piece
                    if scatter:
                        for cp in to_owners(i, k):
                            cp.start()

        @pl.when(t == total)
        def _():
            for of_matrix in copies:
                for cp in of_matrix:
                    cp.wait_send()
            if scatter:
                for i in range(n):
                    for k in range(counts[i]):
                        for cp in to_owners(i, k):
                            cp.wait()

    def specs(i):
        sent = lambda t: _local_step(t, starts[i], counts[i])
        added = lambda t: _local_step(t - 1, starts[i], counts[i])
        by_quarter = (N_CHIPS, rbs[i], cols[i])
        block = (rbs[i], grads[i].shape[-1]) if whole[i] else by_quarter
        lead = () if whole[i] else (0,)
        mine = pl.BlockSpec(block, lambda t, place: (*lead, place[0] * counts[i] + added(t), 0))
        other = pl.BlockSpec(block, lambda t, place: (*lead, (1 - place[0]) * counts[i] + sent(t), 0))
        sums = pl.BlockSpec(by_quarter, lambda t, place: (0, added(t), 0))
        own = pl.BlockSpec(by_quarter[1:], lambda t, place: (added(t), 0))
        return mine, other, own, sums

    all_specs = [specs(i) for i in range(n)]
    sum_shapes = [(N_CHIPS, h, cl) for h, cl in zip(halves, cols)]
    scratch = [pltpu.VMEM(sh, BF16) for sh in half_shapes] + [pltpu.VMEM(sh, BF16) for sh in half_shapes]
    scratch += [pltpu.SemaphoreType.DMA((total,)), pltpu.SemaphoreType.DMA((total,))]
    if scatter:
        scratch += [pltpu.VMEM(sh, BF16) for sh in sum_shapes]
        scratch += [pltpu.SemaphoreType.DMA((3 * total,)), pltpu.SemaphoreType.DMA((3 * total,))]
    outs = pl.pallas_call(
        body, name=name,
        grid_spec=pltpu.PrefetchScalarGridSpec(
            num_scalar_prefetch=1, grid=(total + 1,),
            in_specs=[sp[0] for sp in all_specs] + [sp[1] for sp in all_specs],
            out_specs=[sp[2] for sp in all_specs] + (_any_specs(n) if scatter else [sp[3] for sp in all_specs]),
            scratch_shapes=scratch),
        out_shape=[jax.ShapeDtypeStruct((h, cl), F32) for h, cl in zip(halves, cols)]
                  + [jax.ShapeDtypeStruct((N_CHIPS - 1 if scatter else N_CHIPS, *sh[1:]), BF16) for sh in sum_shapes],
        compiler_params=_params("arbitrary"),
    )(place, *grads, *grads)
    return list(zip(outs[:n], outs[n:]))


SMALL_SUM_ROWS = 16


def join_halves(owns, gots, small_pool, small_gla, small_top, g_gk_pad):
    n = len(owns)
    shapes = [g.shape for g in gots]
    rbs = [min(ADD_ROWS, sh[1]) for sh in shapes]
    counts = [sh[1] // rb for sh, rb in zip(shapes, rbs)]
    starts, total = _spans(counts)

    def body(*refs):
        o_refs, g_refs = refs[:n], refs[n:2 * n]
        pool_ref, gla_ref, top_ref, gk_ref = refs[2 * n:2 * n + 4]
        out_refs, total_ref = refs[2 * n + 4:3 * n + 4], refs[3 * n + 4]
        sum_refs = refs[3 * n + 5:4 * n + 5]
        local_sems, send_sems, recv_sems, all_ref, small_ref = refs[4 * n + 5:4 * n + 10]
        t = pl.program_id(0)
        x, y, c = _position()
        start_small, wait_small = _gather_small(small_ref, all_ref, *refs[4 * n + 10:])

        def copies(i, k):
            src = sum_refs[i].at[pl.ds(k * rbs[i], rbs[i])]
            rows = pl.ds(c * shapes[i][1] + k * rbs[i], rbs[i])
            return (pltpu.make_async_copy(src, out_refs[i].at[rows], local_sems.at[starts[i] + k]),
                    pltpu.make_async_remote_copy(
                        src_ref=src, dst_ref=out_refs[i].at[rows],
                        send_sem=send_sems.at[starts[i] + k], recv_sem=recv_sems.at[starts[i] + k],
                        device_id=(x, y, 1 - c), device_id_type=MESH))

        @pl.when(t == 0)
        def _():
            small_ref[0:3, :] = pool_ref[0:3, :]
            small_ref[3:5, :] = gla_ref[0:2, :]
            small_ref[5:8, :] = top_ref[0:3, :]
            for r in range(GATE_RANK):
                small_ref[8 + r // 2:9 + r // 2, (r % 2) * KEY_W:(r % 2 + 1) * KEY_W] = gk_ref[r:r + 1, :]
            start_small()

        for i in range(n):
            for k in range(counts[i]):
                @pl.when(t == starts[i] + k)
                def _(i=i, k=k):
                    total_i = o_refs[i][...]
                    for j in range(N_CHIPS - 1):
                        total_i = total_i + g_refs[i][j].astype(F32)
                    sum_refs[i][k * rbs[i]:(k + 1) * rbs[i], :] = total_i
                    for cp in copies(i, k):
                        cp.start()

        @pl.when(t == total - 1)
        def _():
            wait_small()
            small_total = all_ref[0]
            for dev in range(1, N_DEV):
                small_total = small_total + all_ref[dev]
            total_ref[...] = small_total
            for i in range(n):
                for k in range(counts[i]):
                    for cp in copies(i, k):
                        cp.wait()

    def specs(i):
        rb, cols = rbs[i], shapes[i][2]
        step = lambda t: _local_step(t, starts[i], counts[i])
        return (pl.BlockSpec((rb, cols), lambda t: (step(t), 0)),
                pl.BlockSpec((N_CHIPS - 1, rb, cols), lambda t: (0, step(t), 0)))

    all_specs = [specs(i) for i in range(n)]
    outs = pl.pallas_call(
        body, name="join_halves", grid=(total,),
        out_shape=[jax.ShapeDtypeStruct((2 * sh[1], sh[2]), F32) for sh in shapes]
                  + [jax.ShapeDtypeStruct((SMALL_SUM_ROWS, D), F32)],
        in_specs=[sp[0] for sp in all_specs] + [sp[1] for sp in all_specs] + [VMEM_SPEC] * 4,
        out_specs=_any_specs(n) + [VMEM_SPEC],
        scratch_shapes=[pltpu.VMEM(sh[1:], F32) for sh in shapes]
                       + [pltpu.SemaphoreType.DMA((total,)), pltpu.SemaphoreType.DMA((total,)),
                          pltpu.SemaphoreType.DMA((total,)),
                          pltpu.VMEM((N_DEV, SMALL_SUM_ROWS, D), F32), pltpu.VMEM((SMALL_SUM_ROWS, D), F32)]
                       + SMALL_SEMS,
        compiler_params=_params("arbitrary"),
    )(*owns, *gots, small_pool, small_gla, small_top, g_gk_pad)
    return outs[:n], outs[n]


def _adam_math(w, g, m, v):
    m = ADAM_B1 * m + (1.0 - ADAM_B1) * g
    v = ADAM_B2 * v + (1.0 - ADAM_B2) * (g * g)
    m_hat = m / (1.0 - ADAM_B1 ** ADAM_STEP)
    v_hat = v / (1.0 - ADAM_B2 ** ADAM_STEP)
    delta = -ADAM_LR * (m_hat / (jnp.sqrt(v_hat) + ADAM_EPS) + ADAM_WD * w)
    return delta, m, v


ADAM_BLOCK_BYTES = 2 ** 19
ADAM_MOST_STEPS = 8


def adamw(params, name):
    n = len(params)
    shapes = [p[0].shape for p in params]

    def tile_rows(shape):
        rows, cols = shape[0], shape[-1]
        aligned = 1 if len(shape) == 3 else 8
        divisors = [t for t in range(aligned, rows + 1, aligned) if rows % t == 0]
        tile = max(t for t in divisors if t * cols * 4 <= ADAM_BLOCK_BYTES)
        if rows // tile > ADAM_MOST_STEPS:
            tile = min(t for t in divisors if rows // t <= ADAM_MOST_STEPS)
        return tile

    tiles = [tile_rows(sh) for sh in shapes]
    counts = [sh[0] // tl for sh, tl in zip(shapes, tiles)]
    starts, total = _spans(counts)

    def body(*refs):
        ins, outs = refs[:4 * n], refs[4 * n:]
        t = pl.program_id(0)
        for i in range(n):
            @pl.when((t >= starts[i]) & (t < starts[i] + counts[i]))
            def _(i=i):
                w_ref, g_ref, m_ref, v_ref = ins[4 * i:4 * i + 4]
                d, nm, nv = _adam_math(w_ref[...], g_ref[...], m_ref[...], v_ref[...])
                outs[3 * i][...] = d
                outs[3 * i + 1][...] = nm
                outs[3 * i + 2][...] = nv

    def spec(i):
        block = (tiles[i],) + shapes[i][1:]
        zeros = (0,) * (len(block) - 1)
        return pl.BlockSpec(block, lambda t: (_local_step(t, starts[i], counts[i]),) + zeros)

    outs = pl.pallas_call(
        body, name=name, grid=(total,),
        out_shape=[jax.ShapeDtypeStruct(sh, F32) for sh in shapes for _ in range(3)],
        in_specs=[spec(i) for i in range(n) for _ in range(4)],
        out_specs=[spec(i) for i in range(n) for _ in range(3)],
        compiler_params=_params("arbitrary"),
    )(*[a for p in params for a in p])
    return [tuple(outs[3 * i:3 * i + 3]) for i in range(n)]


def adamw_small(params, total, place):
    n = len(params)

    def cut_gradients(total_ref, q, g_refs):
        g_norm, g_group_b, g_scale, g_gk_w, g_gk_b, g_head_norm, g_final = g_refs
        g_norm[0:1, :] = total_ref[0:1, :]
        g_norm[1:2, :] = total_ref[3:4, :]
        g_scale[...] = total_ref[1:2, :]
        g_final[...] = total_ref[5:6, :]
        g_gk_b[...] = total_ref[4:5, pl.ds(pl.multiple_of(q * 128, 128), 128)]
        for r in range(GATE_RANK):
            lanes = pl.ds(pl.multiple_of((r % 2) * KEY_W + q * 128, 128), 128)
            g_gk_w[r:r + 1, :] = total_ref[8 + r // 2:9 + r // 2, lanes]
        for k in range(N_CHIPS):
            @pl.when(q == k)
            def _(k=k):
                g_head_norm[...] = total_ref[6:7, 64 * k:64 * (k + 1)]
                for g in range(GROUPS):
                    g_group_b[g:g + 1, :] = total_ref[2:3, GROUP_DIM * g + 64 * k:GROUP_DIM * g + 64 * (k + 1)]

    def body(place_ref, total_ref, *refs):
        ins, outs = refs[:3 * n], refs[3 * n:]
        outs[4 * n][...] = total_ref[7:8, 0:1]
        cut_gradients(total_ref, place_ref[1], outs[0:4 * n:4])
        for k in range(n):
            w_ref, m_ref, v_ref = ins[3 * k:3 * k + 3]
            d, nm, nv = _adam_math(w_ref[...], outs[4 * k][...], m_ref[...], v_ref[...])
            outs[4 * k + 1][...] = d
            outs[4 * k + 2][...] = nm
            outs[4 * k + 3][...] = nv

    flat = [a for p in params for a in p]
    outs = pl.pallas_call(
        body, name="adamw_small",
        out_shape=[jax.ShapeDtypeStruct(p[0].shape, F32) for p in params for _ in range(4)]
                  + [jax.ShapeDtypeStruct((1, 1), F32)],
        in_specs=[pl.BlockSpec(memory_space=pltpu.SMEM)] + [VMEM_SPEC] * (1 + 3 * n),
        out_specs=[VMEM_SPEC] * (4 * n + 1),
    )(place, total, *flat)
    return [tuple(outs[4 * k:4 * k + 4]) for k in range(n)], outs[4 * n]


def matmul_tn(a, b, name, tile_n, by_column_tile=False, chip_sums=()):
    s, m = a.shape
    n = b.shape[1]
    n_sums = len(chip_sums)
    steps = n // tile_n
    if by_column_tile:
        out_shape = jax.ShapeDtypeStruct((steps, m, tile_n), F32)
        out_spec = pl.BlockSpec((None, m, tile_n), lambda j: (j, 0, 0))
    else:
        out_shape = jax.ShapeDtypeStruct((m, n), F32)
        out_spec = pl.BlockSpec((m, tile_n), lambda j: (0, j))

    def body(a_ref, b_ref, *rest):
        sum_refs, out_ref, got_refs = rest[:n_sums], rest[n_sums], rest[n_sums + 1:2 * n_sums + 1]
        j = pl.program_id(0)
        copies = _scatter_copies(sum_refs, got_refs, *rest[2 * n_sums + 1:]) if n_sums else []

        @pl.when(j == 0)
        def _():
            for cp in copies:
                cp.start()

        out_ref[...] = _tn(a_ref[...], b_ref[...])

        @pl.when(j == steps - 1)
        def _():
            for cp in copies:
                cp.wait()

    outs = pl.pallas_call(
        body, name=name, grid=(steps,),
        out_shape=[out_shape] + _scatter_shapes(chip_sums),
        in_specs=[_full((s, m)), pl.BlockSpec((s, tile_n), lambda j: (0, j))] + _any_specs(n_sums),
        out_specs=[out_spec] + _any_specs(n_sums),
        scratch_shapes=[pltpu.SemaphoreType.DMA((3 * n_sums,)), pltpu.SemaphoreType.DMA((3 * n_sums,))]
                       if n_sums else [],
        compiler_params=_params("arbitrary"),
    )(a, b, *chip_sums)
    return outs[0], outs[1:]


ROW_TILE = 512


def _row_index(tile, rows):
    return tile * rows + lax.broadcasted_iota(jnp.int32, (rows, 1), 0)


def _inverse_counts(t_glob):
    return [1.0 / jnp.minimum(t_glob + 1, w).astype(F32) for w in POOL_WINDOWS]


def _sigmoid(z):
    return 1.0 / (1.0 + jnp.exp(-z))


def _trailing_sums(src, tmp, cols, window, rows):
    bufs = (src, tmp)
    span, level, start = 1, 0, 0
    while span < window:
        start += 8
        a, b = bufs[level % 2], bufs[(level + 1) % 2]
        n = HALO + rows - start
        b[start:start + n, cols] = a[start:start + n, cols] + a[start - span:start - span + n, cols]
        span, level = 2 * span, level + 1
    return bufs[level % 2][HALO:HALO + rows, cols]


def _leading_sums(src, tmp, cols, window, rows):
    bufs = (src, tmp)
    span, level, n = 1, 0, rows + HALO
    while span < window:
        n -= 8
        a, b = bufs[level % 2], bufs[(level + 1) % 2]
        b[0:n, cols] = a[0:n, cols] + a[span:span + n, cols]
        span, level = 2 * span, level + 1
    return bufs[level % 2][0:rows, cols]


def gather_in_background(step, last, out_refs, send_sems, recv_sems, finish, partners=None):
    n = len(out_refs)
    partners = partners or [(0, 1, 2)] * n
    x, y, c = _position()
    q = 2 * x + y
    chips = _other_chips(x, y)

    def copy(k, i, quarter, half, to):
        return _gather_copy(out_refs[i], send_sems, recv_sems, k * n + i, quarter, half, to)

    if not finish:
        @pl.when(step == 0)
        def _():
            for i in range(n):
                mine, _ = _halves(out_refs[i].shape[1], c)
                for j in partners[i]:
                    copy(j, i, q, mine, (*chips[j], c)).start()

        @pl.when(step == last)
        def _():
            for j, chip in enumerate(chips):
                qj = 2 * chip[0] + chip[1]
                for i in range(n):
                    if j in partners[i]:
                        mine, _ = _halves(out_refs[i].shape[1], c)
                        copy(j, i, qj, mine, (x, y, c)).wait_recv()
                        copy(3 + j, i, qj, mine, (x, y, 1 - c)).start()
        return

    @pl.when(step == last)
    def _():
        for j, chip in enumerate(chips):
            qj = 2 * chip[0] + chip[1]
            for i in range(n):
                if j in partners[i]:
                    mine, other = _halves(out_refs[i].shape[1], c)
                    copy(3 + j, i, qj, other, (x, y, c)).wait_recv()
                    copy(j, i, q, mine, (x, y, c)).wait_send()
                    copy(3 + j, i, qj, mine, (x, y, c)).wait_send()


def _group_matrix(gw_ref, g):
    rows = GROUP_DIM // N_CHIPS
    return jnp.concatenate([gw_ref[j, rows * g:rows * (g + 1), :] for j in range(N_CHIPS)], axis=0)


def _later_call(body, name, nt, out_shape, in_specs, out_specs, scratch_shapes, args, later):
    n_in, n_out, n_later = len(args), len(out_shape), len(later)
    outs = pl.pallas_call(
        body, name=name, grid=(nt,),
        out_shape=list(out_shape) + [jax.ShapeDtypeStruct(a.shape, a.dtype) for a in later],
        in_specs=list(in_specs) + _any_specs(n_later), out_specs=list(out_specs) + _any_specs(n_later),
        input_output_aliases={n_in + k: n_out + k for k in range(n_later)},
        scratch_shapes=list(scratch_shapes)
                       + [pltpu.SemaphoreType.DMA((6 * n_later,)), pltpu.SemaphoreType.DMA((6 * n_later,))],
        compiler_params=_params("arbitrary"),
    )(*args, *later)
    return outs[:n_out], outs[n_out:]


def pool_project(x, w0, wpi, later, partners):
    s = x.shape[0]
    ts = ROW_TILE
    nt = s // ts
    n_later = len(later)

    def body(x_ref, w0_ref, wpi_ref, *rest):
        n0_ref, u_ref, gt_ref = rest[n_later:n_later + 3]
        later_refs, sems = rest[n_later + 3:2 * n_later + 3], rest[2 * n_later + 3:]
        i = pl.program_id(0)
        gather_in_background(i, nt - 1, later_refs, *sems, finish=False, partners=partners)
        xv = x_ref[...]
        r = lax.rsqrt(jnp.mean(xv * xv, axis=-1, keepdims=True) + EPS)
        n0 = _bf(xv * r * w0_ref[...])
        n0_ref[...] = n0
        u_ref[...] = jnp.concatenate([_nn(n0, wpi_ref[0]), _nn(n0, wpi_ref[1])], axis=-1)
        gt_ref[...] = jnp.concatenate([_nn(n0, wpi_ref[2]), _nn(n0, wpi_ref[3])], axis=-1)
        gather_in_background(i, nt - 1, later_refs, *sems, finish=True, partners=partners)

    row = pl.BlockSpec((ts, D), lambda i: (i, 0))
    return _later_call(
        body, "pool_project", nt,
        out_shape=[jax.ShapeDtypeStruct((s, D), BF16), jax.ShapeDtypeStruct((s, D), F32),
                   jax.ShapeDtypeStruct((s, D), F32)],
        in_specs=[row, _full((1, D)), _full((N_CHIPS, D, D // 2))], out_specs=[row, row, row],
        scratch_shapes=[], args=(x, w0, wpi), later=later)


def pool_mix(x, u, gt, gw, gb, scale, wpo, later, partners):
    s = x.shape[0]
    ts = ROW_TILE
    nt = s // ts
    assert nt >= 2
    n_later = len(later)

    def body(x_ref, u_ref, gt_ref, gw_ref, gb_ref, sc_ref, wpo_ref, *rest):
        h1_ref, pooled_ref = rest[n_later:n_later + 2]
        later_refs = rest[n_later + 2:2 * n_later + 2]
        ubuf, tbuf, hist = rest[2 * n_later + 2:2 * n_later + 5]
        sems = rest[2 * n_later + 5:]
        i = pl.program_id(0)
        gather_in_background(i, nt - 1, later_refs, *sems, finish=False, partners=partners)
        u = u_ref[...]
        gt = gt_ref[...]

        @pl.when(i == 0)
        def _():
            hist[...] = jnp.zeros_like(hist)

        ubuf[0:HALO, :] = hist[...]
        ubuf[HALO:HALO + ts, :] = u
        hist[...] = u[ts - HALO:, :]
        inv = _inverse_counts(_row_index(i, ts))
        mixed = []
        for g, w in enumerate(POOL_WINDOWS):
            cols = slice(g * GROUP_DIM, (g + 1) * GROUP_DIM)
            pooled = _bf(_trailing_sums(ubuf, tbuf, cols, w, ts) * inv[g] - u[:, cols])
            pooled_ref[:, cols] = pooled
            mixed.append(_nn(pooled, _group_matrix(gw_ref, g)))
        mixed = jnp.concatenate(mixed, axis=-1) + gb_ref[...]
        y = mixed * sc_ref[...] * (gt * _sigmoid(gt))
        h1_ref[...] = x_ref[...] + _nn(_bf(y), wpo_ref[...])
        gather_in_background(i, nt - 1, later_refs, *sems, finish=True, partners=partners)

    row = pl.BlockSpec((ts, D), lambda i: (i, 0))
    return _later_call(
        body, "pool_mix", nt,
        out_shape=[jax.ShapeDtypeStruct((s, D), F32), jax.ShapeDtypeStruct((s, D), BF16)],
        in_specs=[row, row, row, _full((N_CHIPS, GROUP_DIM, GROUP_DIM)), _full((1, D)), _full((1, D)),
                  _full((D, D))],
        out_specs=[row, row],
        scratch_shapes=[pltpu.VMEM((HALO + ts, D), F32), pltpu.VMEM((HALO + ts, D), F32),
                        pltpu.VMEM((HALO, D), F32)],
        args=(x, u, gt, gw, gb, scale, wpo), later=later)


def pool_backward(x, dh1, pooled, gt, w0, wpi, gw, gb, scale, wpo, chip_sums):
    s = x.shape[0]
    ts = ROW_TILE
    nt = s // ts
    n_sums = len(chip_sums)

    def body(x_ref, dh1_ref, pooled_ref, gt_ref, w0_ref, wpi_ref, gw_ref, gb_ref, sc_ref, wpo_ref, *rest):
        sum_refs, rest = rest[:n_sums], rest[n_sums:]
        dx_ref, dproj_ref, gpo_ref, ggw_ref, small_ref = rest[:5]
        got_refs = rest[5:5 + n_sums]
        ebuf, tbuf, ahead, send_sems, recv_sems = rest[5 + n_sums:]
        i = pl.program_id(0)
        copies = _scatter_copies(sum_refs, got_refs, send_sems, recv_sems)

        @pl.when(i == 0)
        def _():
            for cp in copies:
                cp.start()

        @pl.when(i == 0)
        def _():
            gpo_ref[...] = jnp.zeros_like(gpo_ref)
            ggw_ref[...] = jnp.zeros_like(ggw_ref)
            small_ref[...] = jnp.zeros_like(small_ref)
            ahead[...] = jnp.zeros_like(ahead)

        dh1 = dh1_ref[...]
        dh1_bf = _bf(dh1)
        gt = gt_ref[...]
        sc = sc_ref[...]
        dy = _nt(dh1_bf, wpo_ref[...])
        pooled_bf = []
        mixed = []
        for g in range(GROUPS):
            cols = slice(g * GROUP_DIM, (g + 1) * GROUP_DIM)
            pb = pooled_ref[:, cols]
            pooled_bf.append(pb)
            mixed.append(_nn(pb, _group_matrix(gw_ref, g)))
        mixed = jnp.concatenate(mixed, axis=-1) + gb_ref[...]
        sg = _sigmoid(gt)
        silu = gt * sg
        gpo_ref[...] += _tn(_bf(mixed * sc * silu), dh1_bf)
        dmixed = dy * sc * silu
        dgt = dy * mixed * sc * (sg * (1.0 + gt * (1.0 - sg)))
        dproj_ref[:, D:] = _bf(dgt)
        small_ref[1:2, :] += jnp.sum(dy * mixed * silu, axis=0, keepdims=True)
        small_ref[2:3, :] += jnp.sum(dmixed, axis=0, keepdims=True)

        inv = _inverse_counts(_row_index(nt - 1 - i, ts))
        rows_q = GROUP_DIM // N_CHIPS
        ebuf[ts:ts + HALO, :] = ahead[...]
        dpooled = []
        for g in range(GROUPS):
            cols = slice(g * GROUP_DIM, (g + 1) * GROUP_DIM)
            dm = _bf(dmixed[:, cols])
            ggw = _tn(pooled_bf[g], dm)
            for j in range(N_CHIPS):
                ggw_ref[j, rows_q * g:rows_q * (g + 1), :] += ggw[rows_q * j:rows_q * (j + 1), :]
            dp = _nt(dm, _group_matrix(gw_ref, g))
            dpooled.append(dp)
            ebuf[0:ts, cols] = dp * inv[g]
        ahead[...] = ebuf[0:HALO, :]
        du = []
        for g, w in enumerate(POOL_WINDOWS):
            cols = slice(g * GROUP_DIM, (g + 1) * GROUP_DIM)
            du.append(_leading_sums(ebuf, tbuf, cols, w, ts) - dpooled[g])
        du = _bf(jnp.concatenate(du, axis=-1))
        dproj_ref[:, :D] = du
        dgt_bf = _bf(dgt)
        half = D // 2
        dn0 = (_nt(du[:, :half], wpi_ref[0]) + _nt(du[:, half:], wpi_ref[1])
               + _nt(dgt_bf[:, :half], wpi_ref[2]) + _nt(dgt_bf[:, half:], wpi_ref[3]))

        xv = x_ref[...]
        r = lax.rsqrt(jnp.mean(xv * xv, axis=-1, keepdims=True) + EPS)
        xhat = xv * r
        small_ref[0:1, :] += jnp.sum(dn0 * xhat, axis=0, keepdims=True)
        dxh = dn0 * w0_ref[...]
        dx_ref[...] = dh1 + r * (dxh - xhat * jnp.mean(dxh * xhat, axis=-1, keepdims=True))

        @pl.when(i == nt - 1)
        def _():
            for cp in copies:
                cp.wait()

    row = lambda cols: pl.BlockSpec((ts, cols), lambda i: (nt - 1 - i, 0))
    outs = pl.pallas_call(
        body, name="pool_backward", grid=(nt,),
        out_shape=[jax.ShapeDtypeStruct((s, D), F32), jax.ShapeDtypeStruct((s, 2 * D), BF16),
                   jax.ShapeDtypeStruct((D, D), F32),
                   jax.ShapeDtypeStruct((GROUPS, GROUP_DIM, GROUP_DIM), F32),
                   jax.ShapeDtypeStruct((8, D), F32)] + _scatter_shapes(chip_sums),
        in_specs=[row(D), row(D), row(D), row(D), _full((1, D)), _full((N_CHIPS, D, D // 2)),
                  _full((GROUPS, GROUP_DIM, GROUP_DIM)), _full((1, D)), _full((1, D)), _full((D, D))]
                 + _any_specs(n_sums),
        out_specs=[row(D), row(2 * D), _full((D, D)), _full((GROUPS, GROUP_DIM, GROUP_DIM)), _full((8, D))]
                  + _any_specs(n_sums),
        scratch_shapes=[pltpu.VMEM((ts + HALO, D), F32), pltpu.VMEM((ts + HALO, D), F32),
                        pltpu.VMEM((HALO, D), F32),
                        pltpu.SemaphoreType.DMA((3 * n_sums,)), pltpu.SemaphoreType.DMA((3 * n_sums,))],
        compiler_params=_params("arbitrary"),
    )(x, dh1, pooled, gt, w0, wpi, gw, gb, scale, wpo, *chip_sums)
    return outs[:5], outs[5:]


def gla_project(h1, w1, wgi_q, wgk, bgk, later):
    s = h1.shape[0]
    ts = ROW_TILE
    nt = s // ts
    assert nt >= 2
    n_later = len(later)

    def body(h_ref, w1_ref, wq_ref, wgk_ref, bgk_ref, *rest):
        rest = rest[n_later:]
        qk_ref, v_ref, gate_ref, low_ref, cum_ref, n1_ref = rest[:6]
        later_refs = rest[6:6 + n_later]
        send_sems, recv_sems, wgi_ref = rest[6 + n_later:]
        gather_in_background(pl.program_id(0), nt - 1, later_refs, send_sems, recv_sems, finish=False)

        @pl.when(pl.program_id(0) == 0)
        def _():
            _assemble_gla_in(wq_ref, wgi_ref)

        hv = h_ref[...]
        r = lax.rsqrt(jnp.mean(hv * hv, axis=-1, keepdims=True) + EPS)
        n1 = _bf(hv * r * w1_ref[...])
        n1_ref[...] = n1
        qk_ref[...] = _nn(n1, wgi_ref[:, 0:2 * KEY_W])
        v_ref[...] = _bf(_nn(n1, wgi_ref[:, 2 * KEY_W:2 * KEY_W + D]))
        gate_ref[...] = _nn(n1, wgi_ref[:, 2 * KEY_W + D:GLA_MAIN])
        low = _bf(_nn(n1, wgi_ref[:, GLA_MAIN:]))
        low_ref[...] = low
        z = _nn(low, wgk_ref[...]) + bgk_ref[...]
        lg = (jnp.minimum(z, 0.0) - jnp.log(1.0 + jnp.exp(-jnp.abs(z)))) / GATE_NORM
        lower_f = _chunk_masks()[0].astype(F32)
        for r0 in range(0, ts, CHUNK):
            cum_ref[r0:r0 + CHUNK, :] = _nn_exact(lower_f, lg[r0:r0 + CHUNK, :])
        gather_in_background(pl.program_id(0), nt - 1, later_refs, send_sems, recv_sems, finish=True)

    row = lambda cols: pl.BlockSpec((ts, cols), lambda i: (i, 0))
    outs = pl.pallas_call(
        body, name="gla_project", grid=(nt,),
        out_shape=[jax.ShapeDtypeStruct((s, D), F32), jax.ShapeDtypeStruct((s, D), BF16),
                   jax.ShapeDtypeStruct((s, D), F32), jax.ShapeDtypeStruct((s, RANK_PAD), BF16),
                   jax.ShapeDtypeStruct((s, KEY_W), F32), jax.ShapeDtypeStruct((s, D), BF16)]
                  + [jax.ShapeDtypeStruct(a.shape, a.dtype) for a in later],
        in_specs=[row(D), _full((1, D)), _full((N_CHIPS, D, GLA_IN_QUARTER)),
                  _full((RANK_PAD, KEY_W)), _full((1, KEY_W))] + _any_specs(n_later),
        out_specs=[row(D), row(D), row(D), row(RANK_PAD), row(KEY_W), row(D)] + _any_specs(n_later),
        input_output_aliases={5 + k: 6 + k for k in range(n_later)},
        scratch_shapes=[pltpu.SemaphoreType.DMA((6 * n_later,)), pltpu.SemaphoreType.DMA((6 * n_later,)),
                        pltpu.VMEM((D, GLA_MAIN + RANK_PAD), BF16)],
        compiler_params=_params("arbitrary"),
    )(h1, w1, wgi_q, wgk, bgk, *later)
    return outs[:6], outs[6:]


def _assemble_gla_in(wq_ref, wfull):
    pad = jnp.zeros((CAST_ROWS, GLA_MAIN + RANK_PAD - GLA_IN), BF16)
    for r0 in range(0, D, CAST_ROWS):
        rows = slice(r0, r0 + CAST_ROWS)
        wfull[rows, :] = jnp.concatenate([wq_ref[q, rows, :] for q in range(N_CHIPS)] + [pad], axis=1)


GLA_BLOCK = 512
CHUNKS_PER_BLOCK = GLA_BLOCK // CHUNK


def _chunk_masks():
    t = lax.broadcasted_iota(jnp.int32, (CHUNK, CHUNK), 0)
    u = lax.broadcasted_iota(jnp.int32, (CHUNK, CHUNK), 1)
    return t >= u, t <= u


def _gla_chunk_terms(q, cum):
    ep = jnp.exp(cum)
    en = jnp.exp(-cum)
    qs = q * (HEAD_K ** -0.5)
    last = cum[CHUNK - 1:CHUNK, :]
    ed = jnp.exp(last - cum)
    dec = jnp.exp(last)
    return ep, en, qs, ed, dec


def gla_forward(qk, v, cum):
    s = qk.shape[0]
    nb = s // GLA_BLOCK
    nc = s // CHUNK

    def body(q_ref, k_ref, v_ref, cum_ref, o_ref, st_ref, sc_ref, state):
        @pl.when(pl.program_id(0) == 0)
        def _():
            state[...] = jnp.zeros_like(state)

        lower, _ = _chunk_masks()

        def chunk(cc, carry):
            rows = pl.ds(pl.multiple_of(cc * CHUNK, CHUNK), CHUNK)
            for h in range(HEADS):
                kc = slice(h * HEAD_K, (h + 1) * HEAD_K)
                vc = slice(h * HEAD_V, (h + 1) * HEAD_V)
                q = q_ref[rows, kc]
                k = k_ref[rows, kc]
                v = v_ref[rows, vc]
                ep, en, qs, ed, dec = _gla_chunk_terms(q, cum_ref[rows, kc])
                a = _bf(qs * ep)
                fwd = _nt(a, _bf(k * en))
                bwd = _nt(_bf(qs * en), _bf(k * ep))
                scores = _bf(jnp.where(lower, fwd, bwd))
                sc_ref[rows, h * CHUNK:(h + 1) * CHUNK] = scores
                st = state[h]
                st_ref[cc, h] = st
                o_ref[rows, vc] = _nn(scores, v) + _nt(a, _bf(st))
                state[h] = st * dec + _tn(v, _bf(k * ed))
            return carry

        lax.fori_loop(0, CHUNKS_PER_BLOCK, chunk, 0, unroll=4)

    return pl.pallas_call(
        body, name="gla_forward", grid=(nb,),
        out_shape=(jax.ShapeDtypeStruct((s, D), F32),
                   jax.ShapeDtypeStruct((nc, HEADS, HEAD_V, HEAD_K), F32),
                   jax.ShapeDtypeStruct((s, HEADS * CHUNK), BF16)),
        in_specs=[pl.BlockSpec((GLA_BLOCK, KEY_W), lambda i: (i, 0)),
                  pl.BlockSpec((GLA_BLOCK, KEY_W), lambda i: (i, 1)),
                  pl.BlockSpec((GLA_BLOCK, D), lambda i: (i, 0)),
                  pl.BlockSpec((GLA_BLOCK, KEY_W), lambda i: (i, 0))],
        out_specs=(pl.BlockSpec((GLA_BLOCK, D), lambda i: (i, 0)),
                   pl.BlockSpec((CHUNKS_PER_BLOCK, HEADS, HEAD_V, HEAD_K), lambda i: (i, 0, 0, 0)),
                   pl.BlockSpec((GLA_BLOCK, HEADS * CHUNK), lambda i: (i, 0))),
        scratch_shapes=[pltpu.VMEM((HEADS, HEAD_V, HEAD_K), F32)],
        compiler_params=_params("arbitrary"),
    )(qk, qk, v, cum)


def gla_backward(qk, v, cum, do, states, scores):
    s = qk.shape[0]
    nb = s // GLA_BLOCK

    def body(q_ref, k_ref, v_ref, cum_ref, do_ref, st_ref, sc_ref, dq_ref, dk_ref, dv_ref, dcum_ref, dstate):
        @pl.when(pl.program_id(0) == 0)
        def _():
            dstate[...] = jnp.zeros_like(dstate)

        lower, _ = _chunk_masks()
        is_last = lax.broadcasted_iota(jnp.int32, (CHUNK, HEAD_K), 0) == CHUNK - 1

        def chunk(step, carry):
            cc = CHUNKS_PER_BLOCK - 1 - step
            rows = pl.ds(pl.multiple_of(cc * CHUNK, CHUNK), CHUNK)
            for h in range(HEADS):
                kc = slice(h * HEAD_K, (h + 1) * HEAD_K)
                vc = slice(h * HEAD_V, (h + 1) * HEAD_V)
                q = q_ref[rows, kc]
                k = k_ref[rows, kc]
                v = v_ref[rows, vc]
                do_c = do_ref[rows, vc]
                ep, en, qs, ed, dec = _gla_chunk_terms(q, cum_ref[rows, kc])
                a = _bf(qs * ep)
                b = _bf(k * en)
                c = _bf(qs * en)
                dk_dec = _bf(k * ep)
                kd = _bf(k * ed)
                scores = sc_ref[rows, h * CHUNK:(h + 1) * CHUNK]
                st = st_ref[cc, h]
                dst = dstate[h]
                dst_bf = _bf(dst)

                dscores = _nt(do_c, v)
                dfwd = _bf(jnp.where(lower, dscores, 0.0))
                dbwd = _bf(jnp.where(lower, 0.0, dscores))
                dv_ref[rows, vc] = _bf(_tn(scores, do_c) + _nt(kd, dst_bf))
                da = _nn(dfwd, b) + _nn(do_c, _bf(st))
                db = _tn(dfwd, a)
                dc = _nn(dbwd, dk_dec)
                ddk = _tn(dbwd, c)
                dkd = _nn(v, dst_bf)
                ddec = jnp.sum(dst * st, axis=0, keepdims=True)
                dstate[h] = dst * dec + _tn(do_c, a)

                m = dkd * k * ed
                dq_ref[rows, kc] = _bf((da * ep + dc * en) * (HEAD_K ** -0.5))
                dk_ref[rows, kc] = _bf(db * en + ddk * ep + dkd * ed)
                dcum = (da * qs + ddk * k) * ep - (db * k + dc * qs) * en - m
                dlast = jnp.sum(m, axis=0, keepdims=True) + ddec * dec
                dcum_ref[rows, kc] = dcum + jnp.where(is_last, dlast, 0.0)
            return carry

        lax.fori_loop(0, CHUNKS_PER_BLOCK, chunk, 0, unroll=4)

    rev = lambda cols, col_block: pl.BlockSpec((GLA_BLOCK, cols), lambda i: (nb - 1 - i, col_block))
    return pl.pallas_call(
        body, name="gla_backward", grid=(nb,),
        out_shape=(jax.ShapeDtypeStruct((s, KEY_W), BF16), jax.ShapeDtypeStruct((s, KEY_W), BF16),
                   jax.ShapeDtypeStruct((s, D), BF16), jax.ShapeDtypeStruct((s, KEY_W), F32)),
        in_specs=[rev(KEY_W, 0), rev(KEY_W, 1), rev(D, 0), rev(KEY_W, 0), rev(D, 0),
                  pl.BlockSpec((CHUNKS_PER_BLOCK, HEADS, HEAD_V, HEAD_K), lambda i: (nb - 1 - i, 0, 0, 0)),
                  rev(HEADS * CHUNK, 0)],
        out_specs=(rev(KEY_W, 0), rev(KEY_W, 0), rev(D, 0), rev(KEY_W, 0)),
        scratch_shapes=[pltpu.VMEM((HEADS, HEAD_V, HEAD_K), F32)],
        compiler_params=_params("arbitrary"),
    )(qk, qk, v, cum, do, states, scores)


def head_and_loss(o, gate, h1, target, hw, wgo, wf):
    s = o.shape[0]
    ts = ROW_TILE

    def body(o_ref, gate_ref, h1_ref, tgt_ref, hw_ref, wgo_ref, wf_ref,
             dh2_ref, do_ref, dgate_ref, ggo_ref, small_ref):
        @pl.when(pl.program_id(0) == 0)
        def _():
            ggo_ref[...] = jnp.zeros_like(ggo_ref)
            small_ref[...] = jnp.zeros_like(small_ref)

        gate = gate_ref[...]
        hw = hw_ref[...]
        sg = _sigmoid(gate)
        silu = gate * sg
        ohat, ro = [], []
        for h in range(HEADS):
            oh = o_ref[:, h * HEAD_V:(h + 1) * HEAD_V]
            rh = lax.rsqrt(jnp.mean(oh * oh, axis=-1, keepdims=True) + EPS)
            ro.append(rh)
            ohat.append(oh * rh)
        ohat = jnp.concatenate(ohat, axis=-1)
        on = ohat * hw
        y2 = _bf(on * silu)
        h2 = h1_ref[...] + _nn(y2, wgo_ref[...])
        rf = lax.rsqrt(jnp.mean(h2 * h2, axis=-1, keepdims=True) + EPS)
        h2hat = h2 * rf
        wf = wf_ref[...]
        diff = h2hat * wf - tgt_ref[...]
        small_ref[2:3, :] += jnp.zeros((1, D), F32) + 0.5 * jnp.sum(diff * diff) / D
        dout = diff / D
        small_ref[0:1, :] += jnp.sum(dout * h2hat, axis=0, keepdims=True)
        dxh = dout * wf
        dh2 = rf * (dxh - h2hat * jnp.mean(dxh * h2hat, axis=-1, keepdims=True))
        dh2_ref[...] = dh2
        dh2_bf = _bf(dh2)
        ggo_ref[...] += _tn(y2, dh2_bf)
        dy2 = _nt(dh2_bf, wgo_ref[...])
        don = dy2 * silu
        dgate_ref[...] = _bf(dy2 * on * (sg * (1.0 + gate * (1.0 - sg))))
        ghw = jnp.sum(don * ohat, axis=0, keepdims=True)
        small_ref[1:2, 0:HEAD_V] += sum(ghw[:, h * HEAD_V:(h + 1) * HEAD_V] for h in range(HEADS))
        dohat = don * hw
        for h in range(HEADS):
            cols = slice(h * HEAD_V, (h + 1) * HEAD_V)
            oh, dh = ohat[:, cols], dohat[:, cols]
            do_ref[:, cols] = _bf(ro[h] * (dh - oh * jnp.mean(dh * oh, axis=-1, keepdims=True)))

    row = lambda cols: pl.BlockSpec((ts, cols), lambda i: (i, 0))
    act = jax.ShapeDtypeStruct((s, D), F32)
    act_bf = jax.ShapeDtypeStruct((s, D), BF16)
    return pl.pallas_call(
        body, name="head_and_loss", grid=(s // ts,),
        out_shape=(act, act_bf, act_bf, jax.ShapeDtypeStruct((D, D), F32), jax.ShapeDtypeStruct((8, D), F32)),
        in_specs=[row(D), row(D), row(D), row(D),
                  _full((1, D)), _full((D, D)), _full((1, D))],
        out_specs=(row(D), row(D), row(D), _full((D, D)), _full((8, D))),
        compiler_params=_params("arbitrary"),
    )(o, gate, h1, target, hw, wgo, wf)


def gla_project_backward(dq, dk, dv, dgate, dcum, low, h1, dh2, w1, wgi_q, wgk, bgk):
    s = h1.shape[0]
    ts = ROW_TILE

    def body(dq_ref, dk_ref, dv_ref, dgate_ref, dcum_ref, low_ref, h1_ref, dh2_ref, w1_ref,
             wq_ref, wgk_ref, bgk_ref, dh1_ref, dproj_ref, ggk_ref, small_ref, wgi_ref):
        @pl.when(pl.program_id(0) == 0)
        def _():
            ggk_ref[...] = jnp.zeros_like(ggk_ref)
            small_ref[...] = jnp.zeros_like(small_ref)
            _assemble_gla_in(wq_ref, wgi_ref)

        low = low_ref[...]
        z = _nn(low, wgk_ref[...]) + bgk_ref[...]
        upper_f = _chunk_masks()[1].astype(F32)
        dlg = jnp.concatenate([_nn_exact(upper_f, dcum_ref[r0:r0 + CHUNK, :]) for r0 in range(0, ts, CHUNK)],
                              axis=0)
        dz = dlg * (1.0 / GATE_NORM) * _sigmoid(-z)
        dz_bf = _bf(dz)
        ggk_ref[...] += _tn(low, dz_bf)
        small_ref[1:2, 0:KEY_W] += jnp.sum(dz, axis=0, keepdims=True)
        dlow = _bf(_nt(dz_bf, wgk_ref[...]))
        dproj_ref[:, GLA_MAIN:] = dlow
        dn1 = _nt(dlow, wgi_ref[:, GLA_MAIN:])
        for ref, lo, hi in ((dq_ref, 0, KEY_W), (dk_ref, KEY_W, 2 * KEY_W),
                            (dv_ref, 2 * KEY_W, 2 * KEY_W + D), (dgate_ref, 2 * KEY_W + D, GLA_MAIN)):
            piece = ref[...]
            dproj_ref[:, lo:hi] = piece
            dn1 = dn1 + _nt(piece, wgi_ref[:, lo:hi])
        hv = h1_ref[...]
        r = lax.rsqrt(jnp.mean(hv * hv, axis=-1, keepdims=True) + EPS)
        hhat = hv * r
        small_ref[0:1, :] += jnp.sum(dn1 * hhat, axis=0, keepdims=True)
        dxh = dn1 * w1_ref[...]
        dh1_ref[...] = dh2_ref[...] + r * (dxh - hhat * jnp.mean(dxh * hhat, axis=-1, keepdims=True))

    row = lambda cols: pl.BlockSpec((ts, cols), lambda i: (i, 0))
    return pl.pallas_call(
        body, name="gla_project_backward", grid=(s // ts,),
        out_shape=(jax.ShapeDtypeStruct((s, D), F32), jax.ShapeDtypeStruct((s, GLA_MAIN + RANK_PAD), BF16),
                   jax.ShapeDtypeStruct((RANK_PAD, KEY_W), F32),
                   jax.ShapeDtypeStruct((8, D), F32)),
        in_specs=[row(KEY_W), row(KEY_W), row(D), row(D), row(KEY_W), row(RANK_PAD), row(D), row(D),
                  _full((1, D)), _full((N_CHIPS, D, GLA_IN_QUARTER)), _full((RANK_PAD, KEY_W)),
                  _full((1, KEY_W))],
        out_specs=(row(D), row(GLA_MAIN + RANK_PAD), _full((RANK_PAD, KEY_W)), _full((8, D))),
        scratch_shapes=[pltpu.VMEM((D, GLA_MAIN + RANK_PAD), BF16)],
        compiler_params=_params("arbitrary"),
    )(dq, dk, dv, dgate, dcum, low, h1, dh2, w1, wgi_q, wgk, bgk)


def local_gradients(xs, target, w0, w1, wf, wpi, gb, scale, quarters, wgk, bgk, hw_tiled, place):
    gw, wpo_q, wgi_q, wgo_q = quarters
    everyone, diagonal, neighbours = (0, 1, 2), (2,), (0, 1)
    (n0, u, gt), (gw, wpo_q, wgi_q) = pool_project(
        xs, w0, wpi, [gw, wpo_q, wgi_q], partners=[everyone, everyone, diagonal])
    wpo = wpo_q.reshape(D, D)
    (h1, pooled), (wgi_q,) = pool_mix(xs, u, gt, gw, gb, scale, wpo, [wgi_q], partners=[neighbours])
    (qk, v, gate, low, cum, n1), (wgo_q,) = gla_project(h1, w1, wgi_q, wgk, bgk, [wgo_q])
    wgo = wgo_q.reshape(D, D)
    o, states, scores = gla_forward(qk, v, cum)

    dh2, do, dgate, g_gla_out, small_top = head_and_loss(o, gate, h1, target, hw_tiled, wgo, wf)
    dq, dk, dv, dcum = gla_backward(qk, v, cum, do, states, scores)
    dh1, dproj, g_gk_pad, small_gla = gla_project_backward(
        dq, dk, dv, dgate, dcum, low, h1, dh2, w1, wgi_q, wgk, bgk)
    g_gla_in, _ = matmul_tn(n1, dproj, "grad_gla_in", tile_n=(GLA_MAIN + RANK_PAD) // 5)

    def chip_sums(grads, tag):
        return add_halves(grads, place, "add_halves_" + tag)

    gla_sums = chip_sums([g_gla_in, g_gla_out.reshape(N_CHIPS, D // N_CHIPS, D)], "gla")
    (dx, dpool, g_pool_out, g_group_w, small_pool), gla_got = pool_backward(
        xs, dh1, pooled, gt, w0, wpi, gw, gb, scale, wpo, [b for _, b in gla_sums])
    mix_sums = chip_sums([g_group_w, g_pool_out.reshape(N_CHIPS, D // N_CHIPS, D)], "pool_mix")
    g_pool_in, mix_got = matmul_tn(n0, dpool, "grad_pool_in", tile_n=D // 2, by_column_tile=True,
                                   chip_sums=[b for _, b in mix_sums])

    in_sums = add_halves([g_pool_in], place, "add_halves_and_scatter_pool_in", scatter=True)
    reduced, total = join_halves(
        [f for f, _ in in_sums + mix_sums + gla_sums], [got for _, got in in_sums] + list(mix_got) + list(gla_got),
        small_pool, small_gla, small_top, g_gk_pad)
    return dx, reduced, total


def kernel(x, norm_w, pool_in_w, pool_group_w, pool_group_b, pool_scale, pool_out_w, gla_in_w, gla_gk_w, gla_gk_b, gla_head_norm_w, gla_out_w, final_norm_w, loss_target, m_norm_w, m_pool_in_w, m_pool_group_w, m_pool_group_b, m_pool_scale, m_pool_out_w, m_gla_in_w, m_gla_gk_w, m_gla_gk_b, m_gla_head_norm_w, m_gla_out_w, m_final_norm_w, v_norm_w, v_pool_in_w, v_pool_group_w, v_pool_group_b, v_pool_scale, v_pool_out_w, v_gla_in_w, v_gla_gk_w, v_gla_gk_b, v_gla_head_norm_w, v_gla_out_w, v_final_norm_w):
    xs = x[0]
    target = loss_target[0]
    q_chip = 2 * lax.axis_index("x") + lax.axis_index("y")
    place = jnp.stack([lax.axis_index("c"), q_chip]).astype(jnp.int32)

    (wpi, gw_q, wpo_q, wgi_q, wgo_q), (bgk, hw_tiled, gb, wgk) = allgather_weights(
        [pool_in_w[0], pool_group_w[0].reshape(GROUP_DIM, GROUP_DIM), pool_out_w[0], gla_in_w[0], gla_out_w[0]],
        exchange=(True, False, False, False, False),
        smalls=[gla_gk_b, gla_head_norm_w, pool_group_b[0], gla_gk_w[0]])

    w0 = norm_w[0:1]
    w1 = norm_w[1:2]
    wf = final_norm_w.reshape(1, D)

    dx, reduced, total = local_gradients(
        xs, target, w0, w1, wf, wpi, gb, pool_scale, [gw_q, wpo_q, wgi_q, wgo_q], wgk, bgk, hw_tiled, place)
    r_pool_in, r_group_w, r_pool_out, r_gla_in, r_gla_out = reduced
    r_group_w = r_group_w.reshape(GROUPS, 64, GROUP_DIM)

    turn = lambda a: jnp.transpose(a, (2, 0, 1))
    back = lambda a: jnp.transpose(a, (1, 2, 0))
    as2d = lambda a, w: a.reshape(-1, w.shape[-1])
    big_names = ("pool_in_w", "pool_group_w", "pool_out_w", "gla_in_w", "gla_out_w")
    big_args = [(pool_in_w, r_pool_in[None], m_pool_in_w, v_pool_in_w),
                (pool_group_w, r_group_w[None], m_pool_group_w, v_pool_group_w),
                (pool_out_w, r_pool_out[None], m_pool_out_w, v_pool_out_w),
                (gla_in_w, r_gla_in[None], m_gla_in_w, v_gla_in_w),
                (gla_out_w, r_gla_out[None], m_gla_out_w, v_gla_out_w)]
    to_kernel = lambda n, a, w: turn(a) if n == "gla_in_w" else as2d(a, w)
    from_kernel = lambda n, a, w: back(a) if n == "gla_in_w" else a.reshape(w.shape)
    big_in = [tuple(to_kernel(n, a, p[0]) for a in p) for n, p in zip(big_names, big_args)]
    big_out = adamw(big_in, "adamw")
    big = {n: (from_kernel(n, i[1], p[0]),) + tuple(from_kernel(n, o, p[0]) for o in out)
           for n, p, i, out in zip(big_names, big_args, big_in, big_out)}

    small_names = ("norm_w", "pool_group_b", "pool_scale", "gla_gk_w", "gla_gk_b", "gla_head_norm_w",
                   "final_norm_w")
    small_args = [(norm_w, m_norm_w, v_norm_w),
                  (pool_group_b, m_pool_group_b, v_pool_group_b),
                  (pool_scale, m_pool_scale, v_pool_scale),
                  (gla_gk_w, m_gla_gk_w, v_gla_gk_w),
                  (gla_gk_b, m_gla_gk_b, v_gla_gk_b),
                  (gla_head_norm_w, m_gla_head_norm_w, v_gla_head_norm_w),
                  (final_norm_w, m_final_norm_w, v_final_norm_w)]
    small_out, loss = adamw_small([tuple(as2d(a, p[0]) for a in p) for p in small_args], total, place)
    small = {n: tuple(o.reshape(p[0].shape) for o in out) for n, p, out in zip(small_names, small_args, small_out)}
    results = [
        small["norm_w"],
        big["pool_in_w"],
        big["pool_group_w"],
        small["pool_group_b"],
        small["pool_scale"],
        big["pool_out_w"],
        big["gla_in_w"],
        small["gla_gk_w"],
        small["gla_gk_b"],
        small["gla_head_norm_w"],
        big["gla_out_w"],
        small["final_norm_w"],
    ]
    grads, deltas, new_m, new_v = zip(*results)
    return (loss.reshape(()), dx[None], *grads, *deltas, *new_m, *new_v)
```

```python
import jax
import jax.numpy as jnp
from jax import lax
from jax.experimental import pallas as pl
from jax.experimental.pallas import tpu as pltpu

F32 = jnp.float32
BF16 = jnp.bfloat16
MESH = pl.DeviceIdType.MESH

D = 1024
POOL_WINDOWS = (2, 4, 8, 16)
GROUPS = 4
GROUP_DIM = 256
HEADS = 4
HEAD_K = 128
HEAD_V = 256
KEY_W = 512
CHUNK = 64
GATE_RANK = 16
GATE_NORM = 16.0
GLA_IN = 3088
GLA_MAIN = 3072
RANK_PAD = 128
EPS = 1e-6
HALO = 32

ADAM_LR = 0.001
ADAM_B1 = 0.9
ADAM_B2 = 0.999
ADAM_EPS = 1e-08
ADAM_WD = 0.01
ADAM_STEP = 10

N_CHIPS = 4
N_DEV = 8
GLA_IN_QUARTER = GLA_IN // N_CHIPS

VMEM_LIMIT = 56 * 1024 * 1024


def _nn(a, b):
    return lax.dot_general(a, b, (((1,), (0,)), ((), ())), preferred_element_type=F32)


def _nt(a, b):
    return lax.dot_general(a, b, (((1,), (1,)), ((), ())), preferred_element_type=F32)


def _tn(a, b):
    return lax.dot_general(a, b, (((0,), (0,)), ((), ())), preferred_element_type=F32)


def _nn_exact(a, b):
    return lax.dot_general(a, b, (((1,), (0,)), ((), ())), preferred_element_type=F32,
                           precision=lax.Precision.HIGHEST)


def _bf(a):
    return a.astype(BF16)


def _params(*sem):
    return pltpu.CompilerParams(dimension_semantics=sem, vmem_limit_bytes=VMEM_LIMIT)


def _full(shape):
    return pl.BlockSpec(shape, lambda i: (0,) * len(shape))


def _position():
    return lax.axis_index("x"), lax.axis_index("y"), lax.axis_index("c")


def _gather_small(in_ref, all_ref, send_sems, recv_sems, local_sem):
    x, y, c = _position()
    me = 4 * x + 2 * y + c
    mine = pltpu.make_async_copy(in_ref, all_ref.at[me], local_sem)
    sends = []
    for k in range(N_DEV - 1):
        fx, fy, fc = (k + 1) >> 2 & 1, (k + 1) >> 1 & 1, (k + 1) & 1
        sends.append(pltpu.make_async_remote_copy(
            src_ref=in_ref, dst_ref=all_ref.at[me],
            send_sem=send_sems.at[k], recv_sem=recv_sems.at[k],
            device_id=(x ^ fx, y ^ fy, c ^ fc), device_id_type=MESH))

    def start():
        mine.start()
        for cp in sends:
            cp.start()

    def wait():
        for k in range(N_DEV - 1):
            fx, fy, fc = (k + 1) >> 2 & 1, (k + 1) >> 1 & 1, (k + 1) & 1
            src_dev = 4 * (x ^ fx) + 2 * (y ^ fy) + (c ^ fc)
            pltpu.make_async_remote_copy(
                src_ref=in_ref, dst_ref=all_ref.at[src_dev],
                send_sem=send_sems.at[k], recv_sem=recv_sems.at[k],
                device_id=(x, y, c), device_id_type=MESH).wait_recv()
        for cp in sends:
            cp.wait_send()
        mine.wait()

    return start, wait


SMALL_SEMS = [pltpu.SemaphoreType.DMA((N_DEV - 1,)), pltpu.SemaphoreType.DMA((N_DEV - 1,)),
              pltpu.SemaphoreType.DMA]
VMEM_SPEC = pl.BlockSpec(memory_space=pltpu.VMEM)


def _other_chips(x, y):
    return [(1 - x, y), (x, 1 - y), (1 - x, 1 - y)]


def _any_specs(n):
    return [pl.BlockSpec(memory_space=pl.ANY)] * n


def _halves(rows, c):
    half = rows // 2
    return pl.ds(c * half, half), pl.ds((1 - c) * half, half)


CAST_ROWS = 256


def _gather_copy(out_ref, send_sems, recv_sems, k, quarter, half, to, src=None):
    dst = out_ref.at[quarter, half]
    return pltpu.make_async_remote_copy(
        src_ref=dst if src is None else src, dst_ref=dst,
        send_sem=send_sems.at[k], recv_sem=recv_sems.at[k], device_id=to, device_id_type=MESH)


SMALL_IN_ROWS = 24


def allgather_weights(quarters, exchange, smalls):
    n = len(quarters)
    shapes = [w.shape for w in quarters]
    moved = [i for i in range(n) if exchange[i]]

    def body(*refs):
        w_refs, (gkb_ref, hnw_ref, gb_ref, gkw_ref) = refs[:n], refs[n:n + 4]
        out_refs, (bgk_ref, hw_ref, gbias_ref, wgk_ref) = refs[n + 4:2 * n + 4], refs[2 * n + 4:2 * n + 8]
        refs = refs[2 * n + 8:]
        f32_bufs, bf_bufs = refs[:n], refs[n:2 * n]
        send_sems, recv_sems, local_sems, small_ref, small_all_ref = refs[2 * n:2 * n + 5]
        small_ref[...] = jnp.zeros_like(small_ref)
        small_ref[0:1, :] = gkb_ref[...]
        small_ref[1:2, 0:64] = hnw_ref[...]
        small_ref[2:2 + GROUPS, 0:64] = gb_ref[...]
        small_ref[8:8 + GATE_RANK, :] = gkw_ref[...]
        start_small, wait_small = _gather_small(small_ref, small_all_ref, *refs[2 * n + 5:])
        start_small()
        x, y, c = _position()
        q = 2 * x + y
        sibling = (x, y, 1 - c)
        chips = _other_chips(x, y)

        def copy(k, i, quarter, half, to, src=None):
            return _gather_copy(out_refs[i], send_sems, recv_sems, k * n + i, quarter, half, to, src)

        loads = [pltpu.make_async_copy(w_refs[i], f32_bufs[i], local_sems.at[i]) for i in range(n)]
        for cp in loads:
            cp.start()
        keeps, sends = [], []
        for i in range(n):
            loads[i].wait()
            for r0 in range(0, shapes[i][0], CAST_ROWS):
                bf_bufs[i][r0:r0 + CAST_ROWS, :] = _bf(f32_bufs[i][r0:r0 + CAST_ROWS, :])
            keep = pltpu.make_async_copy(bf_bufs[i], out_refs[i].at[q], local_sems.at[n + i])
            keep.start()
            keeps.append(keep)
            if not exchange[i]:
                continue
            mine, _ = _halves(shapes[i][0], c)
            for j, chip in enumerate(chips):
                cp = copy(j, i, q, mine, (*chip, c), src=bf_bufs[i].at[mine])
                cp.start()
                sends.append(cp)
        for j, chip in enumerate(chips):
            qj = 2 * chip[0] + chip[1]
            for i in moved:
                mine, _ = _halves(shapes[i][0], c)
                copy(j, i, qj, mine, (x, y, c)).wait_recv()
                cp = copy(3 + j, i, qj, mine, sibling)
                cp.start()
                sends.append(cp)
        for j, chip in enumerate(chips):
            qj = 2 * chip[0] + chip[1]
            for i in moved:
                _, other = _halves(shapes[i][0], c)
                copy(3 + j, i, qj, other, (x, y, c)).wait_recv()
        wait_small()
        wgk_ref[...] = jnp.zeros_like(wgk_ref)
        for j in range(N_CHIPS):
            block = small_all_ref.at[2 * j]
            bgk_ref[:, 128 * j:128 * (j + 1)] = block[0:1, :]
            for h in range(HEADS):
                hw_ref[:, HEAD_V * h + 64 * j:HEAD_V * h + 64 * (j + 1)] = block[1:2, 0:64]
            for g in range(GROUPS):
                gbias_ref[:, GROUP_DIM * g + 64 * j:GROUP_DIM * g + 64 * (j + 1)] = block[2 + g:3 + g, 0:64]
            wgk_ref[0:GATE_RANK, 128 * j:128 * (j + 1)] = _bf(block[8:8 + GATE_RANK, :])
        for cp in sends:
            cp.wait_send()
        for cp in keeps:
            cp.wait()

    outs = pl.pallas_call(
        body, name="allgather_weights",
        out_shape=[jax.ShapeDtypeStruct((N_CHIPS, *s), BF16) for s in shapes]
                  + [jax.ShapeDtypeStruct((1, KEY_W), F32), jax.ShapeDtypeStruct((1, D), F32),
                     jax.ShapeDtypeStruct((1, D), F32), jax.ShapeDtypeStruct((RANK_PAD, KEY_W), BF16)],
        in_specs=_any_specs(n) + [VMEM_SPEC] * 4, out_specs=_any_specs(n) + [VMEM_SPEC] * 4,
        scratch_shapes=([pltpu.VMEM(s, F32) for s in shapes] + [pltpu.VMEM(s, BF16) for s in shapes]
                        + [pltpu.SemaphoreType.DMA((6 * n,)), pltpu.SemaphoreType.DMA((6 * n,)),
                           pltpu.SemaphoreType.DMA((2 * n,)), pltpu.VMEM((SMALL_IN_ROWS, 128), F32),
                           pltpu.VMEM((N_DEV, SMALL_IN_ROWS, 128), F32)] + SMALL_SEMS),
        compiler_params=pltpu.CompilerParams(vmem_limit_bytes=VMEM_LIMIT),
    )(*quarters, *smalls)
    return outs[:n], outs[n:]


def _scatter_copies(b_refs, got_refs, send_sems, recv_sems):
    n = len(b_refs)
    x, y, c = _position()
    copies = []
    for j, chip in enumerate(_other_chips(x, y)):
        qj = 2 * chip[0] + chip[1]
        for i in range(n):
            copies.append(pltpu.make_async_remote_copy(
                src_ref=b_refs[i].at[qj], dst_ref=got_refs[i].at[j],
                send_sem=send_sems.at[j * n + i], recv_sem=recv_sems.at[j * n + i],
                device_id=(*chip, c), device_id_type=MESH))
    return copies


def _scatter_shapes(chip_sums):
    return [jax.ShapeDtypeStruct((N_CHIPS - 1, *b.shape[1:]), BF16) for b in chip_sums]


ADD_ROWS = 512
ADD_HALVES_ROWS = 128


def _spans(counts):
    starts, total = [], 0
    for count in counts:
        starts.append(total)
        total += count
    return starts, total


def _local_step(t, start, count):
    return jnp.clip(t - start, 0, count - 1)


def add_halves(grads, place, name, scatter=False):
    n = len(grads)
    whole = [len(g.shape) == 2 for g in grads]
    halves = [g.shape[-2] // 2 for g in grads]
    cols = [GLA_IN_QUARTER if w else g.shape[-1] for g, w in zip(grads, whole)]
    rbs = [min(ADD_HALVES_ROWS, h) for h in halves]
    counts = [h // rb for h, rb in zip(halves, rbs)]
    starts, total = _spans(counts)
    half_shapes = [(*g.shape[:-2], h, g.shape[-1]) for g, h in zip(grads, halves)]

    def rows_of(ref, i, start):
        return ref.at[pl.ds(start, rbs[i])] if whole[i] else ref.at[:, pl.ds(start, rbs[i])]

    def body(place_ref, *refs):
        a_refs, o_refs = refs[:n], refs[n:2 * n]
        f_refs, h_refs = refs[2 * n:3 * n], refs[3 * n:4 * n]
        send_refs, their_refs, rest = refs[4 * n:5 * n], refs[5 * n:6 * n], refs[6 * n:]
        send_sems, recv_sems = rest[:2]
        t = pl.program_id(0)
        q = place_ref[1]
        x, y, c = _position()
        sum_refs = rest[2:2 + n] if scatter else h_refs

        def to_owners(i, k):
            out_sems, in_sems = rest[2 + n:]
            rows = pl.ds(k * rbs[i], rbs[i])
            return [pltpu.make_async_remote_copy(
                src_ref=sum_refs[i].at[2 * chip[0] + chip[1], rows], dst_ref=h_refs[i].at[j, rows],
                send_sem=out_sems.at[3 * (starts[i] + k) + j], recv_sem=in_sems.at[3 * (starts[i] + k) + j],
                device_id=(*chip, c), device_id_type=MESH) for j, chip in enumerate(_other_chips(x, y))]

        copies = [[pltpu.make_async_remote_copy(
            src_ref=rows_of(send_refs[i], i, k * rbs[i]), dst_ref=rows_of(their_refs[i], i, k * rbs[i]),
            send_sem=send_sems.at[starts[i] + k], recv_sem=recv_sems.at[starts[i] + k],
            device_id=(x, y, 1 - c), device_id_type=MESH) for k in range(counts[i])] for i in range(n)]

        for i in range(n):
            for k in range(counts[i]):
                @pl.when(t == starts[i] + k)
                def _(i=i, k=k):
                    rows_of(send_refs[i], i, k * rbs[i])[...] = _bf(o_refs[i][...])
                    copies[i][k].start()

        for i in range(n):
            for k in range(counts[i]):
                @pl.when(t == starts[i] + k + 1)
                def _(i=i, k=k):
                    copies[i][k].wait_recv()
                    b_ref = rows_of(their_refs[i], i, k * rbs[i])
                    h_ref = sum_refs[i].at[:, pl.ds(k * rbs[i], rbs[i])] if scatter else h_refs[i]
                    if not whole[i]:
                        h_ref[...] = _bf(a_refs[i][...] + b_ref[...].astype(F32))
                        f_refs[i][...] = a_refs[i][q] + b_ref[q].astype(F32)
                    else:
                        total_i = a_refs[i][...] + b_ref[...].astype(F32)
                        for k4 in range(N_CHIPS):
                            piece = total_i[:, k4 * cols[i]:(k4 + 1) * cols[i]]
                            h_ref[k4] = _bf(piece)

                            @pl.when(q == k4)
                            def _():
                                f_refs[i][...] = piece
                    if scatter:
                        for cp in to_owners(i, k):
                            cp.start()

        @pl.when(t == total)
        def _():
            for of_matrix in copies:
                for cp in of_matrix:
                    cp.wait_send()
            if scatter:
                for i in range(n):
                    for k in range(counts[i]):
                        for cp in to_owners(i, k):
                            cp.wait()

    def specs(i):
        sent = lambda t: _local_step(t, starts[i], counts[i])
        added = lambda t: _local_step(t - 1, starts[i], counts[i])
        by_quarter = (N_CHIPS, rbs[i], cols[i])
        block = (rbs[i], grads[i].shape[-1]) if whole[i] else by_quarter
        lead = () if whole[i] else (0,)
        mine = pl.BlockSpec(block, lambda t, place: (*lead, place[0] * counts[i] + added(t), 0))
        other = pl.BlockSpec(block, lambda t, place: (*lead, (1 - place[0]) * counts[i] + sent(t), 0))
        sums = pl.BlockSpec(by_quarter, lambda t, place: (0, added(t), 0))
        own = pl.BlockSpec(by_quarter[1:], lambda t, place: (added(t), 0))
        return mine, other, own, sums

    all_specs = [specs(i) for i in range(n)]
    sum_shapes = [(N_CHIPS, h, cl) for h, cl in zip(halves, cols)]
    scratch = [pltpu.VMEM(sh, BF16) for sh in half_shapes] + [pltpu.VMEM(sh, BF16) for sh in half_shapes]
    scratch += [pltpu.SemaphoreType.DMA((total,)), pltpu.SemaphoreType.DMA((total,))]
    if scatter:
        scratch += [pltpu.VMEM(sh, BF16) for sh in sum_shapes]
        scratch += [pltpu.SemaphoreType.DMA((3 * total,)), pltpu.SemaphoreType.DMA((3 * total,))]
    outs = pl.pallas_call(
        body, name=name,
        grid_spec=pltpu.PrefetchScalarGridSpec(
            num_scalar_prefetch=1, grid=(total + 1,),
            in_specs=[sp[0] for sp in all_specs] + [sp[1] for sp in all_specs],
            out_specs=[sp[2] for sp in all_specs] + (_any_specs(n) if scatter else [sp[3] for sp in all_specs]),
            scratch_shapes=scratch),
        out_shape=[jax.ShapeDtypeStruct((h, cl), F32) for h, cl in zip(halves, cols)]
                  + [jax.ShapeDtypeStruct((N_CHIPS - 1 if scatter else N_CHIPS, *sh[1:]), BF16) for sh in sum_shapes],
        compiler_params=_params("arbitrary"),
    )(place, *grads, *grads)
    return list(zip(outs[:n], outs[n:]))


SMALL_SUM_ROWS = 16


def join_halves(owns, gots, small_pool, small_gla, small_top, g_gk_pad):
    n = len(owns)
    shapes = [g.shape for g in gots]
    rbs = [min(ADD_ROWS, sh[1]) for sh in shapes]
    counts = [sh[1] // rb for sh, rb in zip(shapes, rbs)]
    starts, total = _spans(counts)

    def body(*refs):
        o_refs, g_refs = refs[:n], refs[n:2 * n]
        pool_ref, gla_ref, top_ref, gk_ref = refs[2 * n:2 * n + 4]
        out_refs, total_ref = refs[2 * n + 4:3 * n + 4], refs[3 * n + 4]
        sum_refs = refs[3 * n + 5:4 * n + 5]
        local_sems, send_sems, recv_sems, all_ref, small_ref = refs[4 * n + 5:4 * n + 10]
        t = pl.program_id(0)
        x, y, c = _position()
        start_small, wait_small = _gather_small(small_ref, all_ref, *refs[4 * n + 10:])

        def copies(i, k):
            src = sum_refs[i].at[pl.ds(k * rbs[i], rbs[i])]
            rows = pl.ds(c * shapes[i][1] + k * rbs[i], rbs[i])
            return (pltpu.make_async_copy(src, out_refs[i].at[rows], local_sems.at[starts[i] + k]),
                    pltpu.make_async_remote_copy(
                        src_ref=src, dst_ref=out_refs[i].at[rows],
                        send_sem=send_sems.at[starts[i] + k], recv_sem=recv_sems.at[starts[i] + k],
                        device_id=(x, y, 1 - c), device_id_type=MESH))

        @pl.when(t == 0)
        def _():
            small_ref[0:3, :] = pool_ref[0:3, :]
            small_ref[3:5, :] = gla_ref[0:2, :]
            small_ref[5:8, :] = top_ref[0:3, :]
            for r in range(GATE_RANK):
                small_ref[8 + r // 2:9 + r // 2, (r % 2) * KEY_W:(r % 2 + 1) * KEY_W] = gk_ref[r:r + 1, :]
            start_small()

        for i in range(n):
            for k in range(counts[i]):
                @pl.when(t == starts[i] + k)
                def _(i=i, k=k):
                    total_i = o_refs[i][...]
                    for j in range(N_CHIPS - 1):
                        total_i = total_i + g_refs[i][j].astype(F32)
                    sum_refs[i][k * rbs[i]:(k + 1) * rbs[i], :] = total_i
                    for cp in copies(i, k):
                        cp.start()

        @pl.when(t == total - 1)
        def _():
            wait_small()
            small_total = all_ref[0]
            for dev in range(1, N_DEV):
                small_total = small_total + all_ref[dev]
            total_ref[...] = small_total
            for i in range(n):
                for k in range(counts[i]):
                    for cp in copies(i, k):
                        cp.wait()

    def specs(i):
        rb, cols = rbs[i], shapes[i][2]
        step = lambda t: _local_step(t, starts[i], counts[i])
        return (pl.BlockSpec((rb, cols), lambda t: (step(t), 0)),
                pl.BlockSpec((N_CHIPS - 1, rb, cols), lambda t: (0, step(t), 0)))

    all_specs = [specs(i) for i in range(n)]
    outs = pl.pallas_call(
        body, name="join_halves", grid=(total,),
        out_shape=[jax.ShapeDtypeStruct((2 * sh[1], sh[2]), F32) for sh in shapes]
                  + [jax.ShapeDtypeStruct((SMALL_SUM_ROWS, D), F32)],
        in_specs=[sp[0] for sp in all_specs] + [sp[1] for sp in all_specs] + [VMEM_SPEC] * 4,
        out_specs=_any_specs(n) + [VMEM_SPEC],
        scratch_shapes=[pltpu.VMEM(sh[1:], F32) for sh in shapes]
                       + [pltpu.SemaphoreType.DMA((total,)), pltpu.SemaphoreType.DMA((total,)),
                          pltpu.SemaphoreType.DMA((total,)),
                          pltpu.VMEM((N_DEV, SMALL_SUM_ROWS, D), F32), pltpu.VMEM((SMALL_SUM_ROWS, D), F32)]
                       + SMALL_SEMS,
        compiler_params=_params("arbitrary"),
    )(*owns, *gots, small_pool, small_gla, small_top, g_gk_pad)
    return outs[:n], outs[n]


def _adam_math(w, g, m, v):
    m = ADAM_B1 * m + (1.0 - ADAM_B1) * g
    v = ADAM_B2 * v + (1.0 - ADAM_B2) * (g * g)
    m_hat = m / (1.0 - ADAM_B1 ** ADAM_STEP)
    v_hat = v / (1.0 - ADAM_B2 ** ADAM_STEP)
    delta = -ADAM_LR * (m_hat / (jnp.sqrt(v_hat) + ADAM_EPS) + ADAM_WD * w)
    return delta, m, v


ADAM_BLOCK_BYTES = 2 ** 19
ADAM_MOST_STEPS = 8


def adamw(params, name):
    n = len(params)
    shapes = [p[0].shape for p in params]

    def tile_rows(shape):
        rows, cols = shape[0], shape[-1]
        aligned = 1 if len(shape) == 3 else 8
        divisors = [t for t in range(aligned, rows + 1, aligned) if rows % t == 0]
        tile = max(t for t in divisors if t * cols * 4 <= ADAM_BLOCK_BYTES)
        if rows // tile > ADAM_MOST_STEPS:
            tile = min(t for t in divisors if rows // t <= ADAM_MOST_STEPS)
        return tile

    tiles = [tile_rows(sh) for sh in shapes]
    counts = [sh[0] // tl for sh, tl in zip(shapes, tiles)]
    starts, total = _spans(counts)

    def body(*refs):
        ins, outs = refs[:4 * n], refs[4 * n:]
        t = pl.program_id(0)
        for i in range(n):
            @pl.when((t >= starts[i]) & (t < starts[i] + counts[i]))
            def _(i=i):
                w_ref, g_ref, m_ref, v_ref = ins[4 * i:4 * i + 4]
                d, nm, nv = _adam_math(w_ref[...], g_ref[...], m_ref[...], v_ref[...])
                outs[3 * i][...] = d
                outs[3 * i + 1][...] = nm
                outs[3 * i + 2][...] = nv

    def spec(i):
        block = (tiles[i],) + shapes[i][1:]
        zeros = (0,) * (len(block) - 1)
        return pl.BlockSpec(block, lambda t: (_local_step(t, starts[i], counts[i]),) + zeros)

    outs = pl.pallas_call(
        body, name=name, grid=(total,),
        out_shape=[jax.ShapeDtypeStruct(sh, F32) for sh in shapes for _ in range(3)],
        in_specs=[spec(i) for i in range(n) for _ in range(4)],
        out_specs=[spec(i) for i in range(n) for _ in range(3)],
        compiler_params=_params("arbitrary"),
    )(*[a for p in params for a in p])
    return [tuple(outs[3 * i:3 * i + 3]) for i in range(n)]


def adamw_small(params, total, place):
    n = len(params)

    def cut_gradients(total_ref, q, g_refs):
        g_norm, g_group_b, g_scale, g_gk_w, g_gk_b, g_head_norm, g_final = g_refs
        g_norm[0:1, :] = total_ref[0:1, :]
        g_norm[1:2, :] = total_ref[3:4, :]
        g_scale[...] = total_ref[1:2, :]
        g_final[...] = total_ref[5:6, :]
        g_gk_b[...] = total_ref[4:5, pl.ds(pl.multiple_of(q * 128, 128), 128)]
        for r in range(GATE_RANK):
            lanes = pl.ds(pl.multiple_of((r % 2) * KEY_W + q * 128, 128), 128)
            g_gk_w[r:r + 1, :] = total_ref[8 + r // 2:9 + r // 2, lanes]
        for k in range(N_CHIPS):
            @pl.when(q == k)
            def _(k=k):
                g_head_norm[...] = total_ref[6:7, 64 * k:64 * (k + 1)]
                for g in range(GROUPS):
                    g_group_b[g:g + 1, :] = total_ref[2:3, GROUP_DIM * g + 64 * k:GROUP_DIM * g + 64 * (k + 1)]

    def body(place_ref, total_ref, *refs):
        ins, outs = refs[:3 * n], refs[3 * n:]
        outs[4 * n][...] = total_ref[7:8, 0:1]
        cut_gradients(total_ref, place_ref[1], outs[0:4 * n:4])
        for k in range(n):
            w_ref, m_ref, v_ref = ins[3 * k:3 * k + 3]
            d, nm, nv = _adam_math(w_ref[...], outs[4 * k][...], m_ref[...], v_ref[...])
            outs[4 * k + 1][...] = d
            outs[4 * k + 2][...] = nm
            outs[4 * k + 3][...] = nv

    flat = [a for p in params for a in p]
    outs = pl.pallas_call(
        body, name="adamw_small",
        out_shape=[jax.ShapeDtypeStruct(p[0].shape, F32) for p in params for _ in range(4)]
                  + [jax.ShapeDtypeStruct((1, 1), F32)],
        in_specs=[pl.BlockSpec(memory_space=pltpu.SMEM)] + [VMEM_SPEC] * (1 + 3 * n),
        out_specs=[VMEM_SPEC] * (4 * n + 1),
    )(place, total, *flat)
    return [tuple(outs[4 * k:4 * k + 4]) for k in range(n)], outs[4 * n]


def matmul_tn(a, b, name, tile_n, by_column_tile=False, chip_sums=()):
    s, m = a.shape
    n = b.shape[1]
    n_sums = len(chip_sums)
    steps = n // tile_n
    if by_column_tile:
        out_shape = jax.ShapeDtypeStruct((steps, m, tile_n), F32)
        out_spec = pl.BlockSpec((None, m, tile_n), lambda j: (j, 0, 0))
    else:
        out_shape = jax.ShapeDtypeStruct((m, n), F32)
        out_spec = pl.BlockSpec((m, tile_n), lambda j: (0, j))

    def body(a_ref, b_ref, *rest):
        sum_refs, out_ref, got_refs = rest[:n_sums], rest[n_sums], rest[n_sums + 1:2 * n_sums + 1]
        j = pl.program_id(0)
        copies = _scatter_copies(sum_refs, got_refs, *rest[2 * n_sums + 1:]) if n_sums else []

        @pl.when(j == 0)
        def _():
            for cp in copies:
                cp.start()

        out_ref[...] = _tn(a_ref[...], b_ref[...])

        @pl.when(j == steps - 1)
        def _():
            for cp in copies:
                cp.wait()

    outs = pl.pallas_call(
        body, name=name, grid=(steps,),
        out_shape=[out_shape] + _scatter_shapes(chip_sums),
        in_specs=[_full((s, m)), pl.BlockSpec((s, tile_n), lambda j: (0, j))] + _any_specs(n_sums),
        out_specs=[out_spec] + _any_specs(n_sums),
        scratch_shapes=[pltpu.SemaphoreType.DMA((3 * n_sums,)), pltpu.SemaphoreType.DMA((3 * n_sums,))]
                       if n_sums else [],
        compiler_params=_params("arbitrary"),
    )(a, b, *chip_sums)
    return outs[0], outs[1:]


ROW_TILE = 512


def _row_index(tile, rows):
    return tile * rows + lax.broadcasted_iota(jnp.int32, (rows, 1), 0)


def _inverse_counts(t_glob):
    return [1.0 / jnp.minimum(t_glob + 1, w).astype(F32) for w in POOL_WINDOWS]


def _sigmoid(z):
    return 1.0 / (1.0 + jnp.exp(-z))


def _trailing_sums(src, tmp, cols, window, rows):
    bufs = (src, tmp)
    span, level, start = 1, 0, 0
    while span < window:
        start += 8
        a, b = bufs[level % 2], bufs[(level + 1) % 2]
        n = HALO + rows - start
        b[start:start + n, cols] = a[start:start + n, cols] + a[start - span:start - span + n, cols]
        span, level = 2 * span, level + 1
    return bufs[level % 2][HALO:HALO + rows, cols]


def _leading_sums(src, tmp, cols, window, rows):
    bufs = (src, tmp)
    span, level, n = 1, 0, rows + HALO
    while span < window:
        n -= 8
        a, b = bufs[level % 2], bufs[(level + 1) % 2]
        b[0:n, cols] = a[0:n, cols] + a[span:span + n, cols]
        span, level = 2 * span, level + 1
    return bufs[level % 2][0:rows, cols]


def gather_in_background(step, last, out_refs, send_sems, recv_sems, finish):
    n = len(out_refs)
    x, y, c = _position()
    q = 2 * x + y
    chips = _other_chips(x, y)

    def copy(k, i, quarter, half, to):
        return _gather_copy(out_refs[i], send_sems, recv_sems, k * n + i, quarter, half, to)

    if not finish:
        @pl.when(step == 0)
        def _():
            for i in range(n):
                mine, _ = _halves(out_refs[i].shape[1], c)
                for j, chip in enumerate(chips):
                    copy(j, i, q, mine, (*chip, c)).start()

        @pl.when(step == last)
        def _():
            for j, chip in enumerate(chips):
                qj = 2 * chip[0] + chip[1]
                for i in range(n):
                    mine, _ = _halves(out_refs[i].shape[1], c)
                    copy(j, i, qj, mine, (x, y, c)).wait_recv()
                    copy(3 + j, i, qj, mine, (x, y, 1 - c)).start()
        return

    @pl.when(step == last)
    def _():
        for j, chip in enumerate(chips):
            qj = 2 * chip[0] + chip[1]
            for i in range(n):
                mine, other = _halves(out_refs[i].shape[1], c)
                copy(3 + j, i, qj, other, (x, y, c)).wait_recv()
                copy(j, i, q, mine, (x, y, c)).wait_send()
                copy(3 + j, i, qj, mine, (x, y, c)).wait_send()


def _group_matrix(gw_ref, g):
    rows = GROUP_DIM // N_CHIPS
    return jnp.concatenate([gw_ref[j, rows * g:rows * (g + 1), :] for j in range(N_CHIPS)], axis=0)


def pool_forward(x, w0, wpi, gw, gb, scale, wpo, later):
    s = x.shape[0]
    ts = ROW_TILE
    nt = s // ts
    assert nt >= 2
    n_later = len(later)

    def body(x_ref, w0_ref, wpi_ref, gw_ref, gb_ref, sc_ref, wpo_ref, *rest):
        rest = rest[n_later:]
        h1_ref, pooled_ref, gt_ref, n0_ref = rest[:4]
        later_refs = rest[4:4 + n_later]
        ubuf, tbuf, hist, send_sems, recv_sems = rest[4 + n_later:]
        i = pl.program_id(0)
        gather_in_background(i, nt - 1, later_refs, send_sems, recv_sems, finish=False)
        xv = x_ref[...]
        r = lax.rsqrt(jnp.mean(xv * xv, axis=-1, keepdims=True) + EPS)
        n0 = _bf(xv * r * w0_ref[...])
        n0_ref[...] = n0
        u = jnp.concatenate([_nn(n0, wpi_ref[0]), _nn(n0, wpi_ref[1])], axis=-1)
        gt = jnp.concatenate([_nn(n0, wpi_ref[2]), _nn(n0, wpi_ref[3])], axis=-1)
        gt_ref[...] = gt

        @pl.when(i == 0)
        def _():
            hist[...] = jnp.zeros_like(hist)

        ubuf[0:HALO, :] = hist[...]
        ubuf[HALO:HALO + ts, :] = u
        hist[...] = u[ts - HALO:, :]
        inv = _inverse_counts(_row_index(i, ts))
        mixed = []
        for g, w in enumerate(POOL_WINDOWS):
            cols = slice(g * GROUP_DIM, (g + 1) * GROUP_DIM)
            pooled = _bf(_trailing_sums(ubuf, tbuf, cols, w, ts) * inv[g] - u[:, cols])
            pooled_ref[:, cols] = pooled
            mixed.append(_nn(pooled, _group_matrix(gw_ref, g)))
        mixed = jnp.concatenate(mixed, axis=-1) + gb_ref[...]
        y = mixed * sc_ref[...] * (gt * _sigmoid(gt))
        h1_ref[...] = xv + _nn(_bf(y), wpo_ref[...])
        gather_in_background(i, nt - 1, later_refs, send_sems, recv_sems, finish=True)

    row = lambda cols: pl.BlockSpec((ts, cols), lambda i: (i, 0))
    outs = pl.pallas_call(
        body, name="pool_forward", grid=(nt,),
        out_shape=[jax.ShapeDtypeStruct((s, D), F32), jax.ShapeDtypeStruct((s, D), BF16),
                   jax.ShapeDtypeStruct((s, D), F32), jax.ShapeDtypeStruct((s, D), BF16)]
                  + [jax.ShapeDtypeStruct(a.shape, a.dtype) for a in later],
        in_specs=[row(D), _full((1, D)), _full((N_CHIPS, D, D // 2)), _full((GROUPS, GROUP_DIM, GROUP_DIM)),
                  _full((1, D)), _full((1, D)), _full((D, D))] + _any_specs(n_later),
        out_specs=[row(D), row(D), row(D), row(D)] + _any_specs(n_later),
        input_output_aliases={7 + k: 4 + k for k in range(n_later)},
        scratch_shapes=[pltpu.VMEM((HALO + ts, D), F32), pltpu.VMEM((HALO + ts, D), F32),
                        pltpu.VMEM((HALO, D), F32),
                        pltpu.SemaphoreType.DMA((6 * n_later,)), pltpu.SemaphoreType.DMA((6 * n_later,))],
        compiler_params=_params("arbitrary"),
    )(x, w0, wpi, gw, gb, scale, wpo, *later)
    return outs[:4], outs[4:]


def pool_backward(x, dh1, pooled, gt, w0, wpi, gw, gb, scale, wpo, chip_sums):
    s = x.shape[0]
    ts = ROW_TILE
    nt = s // ts
    n_sums = len(chip_sums)

    def body(x_ref, dh1_ref, pooled_ref, gt_ref, w0_ref, wpi_ref, gw_ref, gb_ref, sc_ref, wpo_ref, *rest):
        sum_refs, rest = rest[:n_sums], rest[n_sums:]
        dx_ref, dproj_ref, gpo_ref, ggw_ref, small_ref = rest[:5]
        got_refs = rest[5:5 + n_sums]
        ebuf, tbuf, ahead, send_sems, recv_sems = rest[5 + n_sums:]
        i = pl.program_id(0)
        copies = _scatter_copies(sum_refs, got_refs, send_sems, recv_sems)

        @pl.when(i == 0)
        def _():
            for cp in copies:
                cp.start()

        @pl.when(i == 0)
        def _():
            gpo_ref[...] = jnp.zeros_like(gpo_ref)
            ggw_ref[...] = jnp.zeros_like(ggw_ref)
            small_ref[...] = jnp.zeros_like(small_ref)
            ahead[...] = jnp.zeros_like(ahead)

        dh1 = dh1_ref[...]
        dh1_bf = _bf(dh1)
        gt = gt_ref[...]
        sc = sc_ref[...]
        dy = _nt(dh1_bf, wpo_ref[...])
        pooled_bf = []
        mixed = []
        for g in range(GROUPS):
            cols = slice(g * GROUP_DIM, (g + 1) * GROUP_DIM)
            pb = pooled_ref[:, cols]
            pooled_bf.append(pb)
            mixed.append(_nn(pb, _group_matrix(gw_ref, g)))
        mixed = jnp.concatenate(mixed, axis=-1) + gb_ref[...]
        sg = _sigmoid(gt)
        silu = gt * sg
        gpo_ref[...] += _tn(_bf(mixed * sc * silu), dh1_bf)
        dmixed = dy * sc * silu
        dgt = dy * mixed * sc * (sg * (1.0 + gt * (1.0 - sg)))
        dproj_ref[:, D:] = _bf(dgt)
        small_ref[1:2, :] += jnp.sum(dy * mixed * silu, axis=0, keepdims=True)
        small_ref[2:3, :] += jnp.sum(dmixed, axis=0, keepdims=True)

        inv = _inverse_counts(_row_index(nt - 1 - i, ts))
        rows_q = GROUP_DIM // N_CHIPS
        ebuf[ts:ts + HALO, :] = ahead[...]
        dpooled = []
        for g in range(GROUPS):
            cols = slice(g * GROUP_DIM, (g + 1) * GROUP_DIM)
            dm = _bf(dmixed[:, cols])
            ggw = _tn(pooled_bf[g], dm)
            for j in range(N_CHIPS):
                ggw_ref[j, rows_q * g:rows_q * (g + 1), :] += ggw[rows_q * j:rows_q * (j + 1), :]
            dp = _nt(dm, _group_matrix(gw_ref, g))
            dpooled.append(dp)
            ebuf[0:ts, cols] = dp * inv[g]
        ahead[...] = ebuf[0:HALO, :]
        du = []
        for g, w in enumerate(POOL_WINDOWS):
            cols = slice(g * GROUP_DIM, (g + 1) * GROUP_DIM)
            du.append(_leading_sums(ebuf, tbuf, cols, w, ts) - dpooled[g])
        du = _bf(jnp.concatenate(du, axis=-1))
        dproj_ref[:, :D] = du
        dgt_bf = _bf(dgt)
        half = D // 2
        dn0 = (_nt(du[:, :half], wpi_ref[0]) + _nt(du[:, half:], wpi_ref[1])
               + _nt(dgt_bf[:, :half], wpi_ref[2]) + _nt(dgt_bf[:, half:], wpi_ref[3]))

        xv = x_ref[...]
        r = lax.rsqrt(jnp.mean(xv * xv, axis=-1, keepdims=True) + EPS)
        xhat = xv * r
        small_ref[0:1, :] += jnp.sum(dn0 * xhat, axis=0, keepdims=True)
        dxh = dn0 * w0_ref[...]
        dx_ref[...] = dh1 + r * (dxh - xhat * jnp.mean(dxh * xhat, axis=-1, keepdims=True))

        @pl.when(i == nt - 1)
        def _():
            for cp in copies:
                cp.wait()

    row = lambda cols: pl.BlockSpec((ts, cols), lambda i: (nt - 1 - i, 0))
    outs = pl.pallas_call(
        body, name="pool_backward", grid=(nt,),
        out_shape=[jax.ShapeDtypeStruct((s, D), F32), jax.ShapeDtypeStruct((s, 2 * D), BF16),
                   jax.ShapeDtypeStruct((D, D), F32),
                   jax.ShapeDtypeStruct((GROUPS, GROUP_DIM, GROUP_DIM), F32),
                   jax.ShapeDtypeStruct((8, D), F32)] + _scatter_shapes(chip_sums),
        in_specs=[row(D), row(D), row(D), row(D), _full((1, D)), _full((N_CHIPS, D, D // 2)),
                  _full((GROUPS, GROUP_DIM, GROUP_DIM)), _full((1, D)), _full((1, D)), _full((D, D))]
                 + _any_specs(n_sums),
        out_specs=[row(D), row(2 * D), _full((D, D)), _full((GROUPS, GROUP_DIM, GROUP_DIM)), _full((8, D))]
                  + _any_specs(n_sums),
        scratch_shapes=[pltpu.VMEM((ts + HALO, D), F32), pltpu.VMEM((ts + HALO, D), F32),
                        pltpu.VMEM((HALO, D), F32),
                        pltpu.SemaphoreType.DMA((3 * n_sums,)), pltpu.SemaphoreType.DMA((3 * n_sums,))],
        compiler_params=_params("arbitrary"),
    )(x, dh1, pooled, gt, w0, wpi, gw, gb, scale, wpo, *chip_sums)
    return outs[:5], outs[5:]


def gla_project(h1, w1, wgi_q, wgk, bgk, later):
    s = h1.shape[0]
    ts = ROW_TILE
    nt = s // ts
    assert nt >= 2
    n_later = len(later)

    def body(h_ref, w1_ref, wq_ref, wgk_ref, bgk_ref, *rest):
        rest = rest[n_later:]
        qk_ref, v_ref, gate_ref, low_ref, cum_ref, n1_ref = rest[:6]
        later_refs = rest[6:6 + n_later]
        send_sems, recv_sems, wgi_ref = rest[6 + n_later:]
        gather_in_background(pl.program_id(0), nt - 1, later_refs, send_sems, recv_sems, finish=False)

        @pl.when(pl.program_id(0) == 0)
        def _():
            _assemble_gla_in(wq_ref, wgi_ref)

        hv = h_ref[...]
        r = lax.rsqrt(jnp.mean(hv * hv, axis=-1, keepdims=True) + EPS)
        n1 = _bf(hv * r * w1_ref[...])
        n1_ref[...] = n1
        qk_ref[...] = _nn(n1, wgi_ref[:, 0:2 * KEY_W])
        v_ref[...] = _bf(_nn(n1, wgi_ref[:, 2 * KEY_W:2 * KEY_W + D]))
        gate_ref[...] = _nn(n1, wgi_ref[:, 2 * KEY_W + D:GLA_MAIN])
        low = _bf(_nn(n1, wgi_ref[:, GLA_MAIN:]))
        low_ref[...] = low
        z = _nn(low, wgk_ref[...]) + bgk_ref[...]
        lg = (jnp.minimum(z, 0.0) - jnp.log(1.0 + jnp.exp(-jnp.abs(z)))) / GATE_NORM
        lower_f = _chunk_masks()[0].astype(F32)
        for r0 in range(0, ts, CHUNK):
            cum_ref[r0:r0 + CHUNK, :] = _nn_exact(lower_f, lg[r0:r0 + CHUNK, :])
        gather_in_background(pl.program_id(0), nt - 1, later_refs, send_sems, recv_sems, finish=True)

    row = lambda cols: pl.BlockSpec((ts, cols), lambda i: (i, 0))
    outs = pl.pallas_call(
        body, name="gla_project", grid=(nt,),
        out_shape=[jax.ShapeDtypeStruct((s, D), F32), jax.ShapeDtypeStruct((s, D), BF16),
                   jax.ShapeDtypeStruct((s, D), F32), jax.ShapeDtypeStruct((s, RANK_PAD), BF16),
                   jax.ShapeDtypeStruct((s, KEY_W), F32), jax.ShapeDtypeStruct((s, D), BF16)]
                  + [jax.ShapeDtypeStruct(a.shape, a.dtype) for a in later],
        in_specs=[row(D), _full((1, D)), _full((N_CHIPS, D, GLA_IN_QUARTER)),
                  _full((RANK_PAD, KEY_W)), _full((1, KEY_W))] + _any_specs(n_later),
        out_specs=[row(D), row(D), row(D), row(RANK_PAD), row(KEY_W), row(D)] + _any_specs(n_later),
        input_output_aliases={5 + k: 6 + k for k in range(n_later)},
        scratch_shapes=[pltpu.SemaphoreType.DMA((6 * n_later,)), pltpu.SemaphoreType.DMA((6 * n_later,)),
                        pltpu.VMEM((D, GLA_MAIN + RANK_PAD), BF16)],
        compiler_params=_params("arbitrary"),
    )(h1, w1, wgi_q, wgk, bgk, *later)
    return outs[:6], outs[6:]


def _assemble_gla_in(wq_ref, wfull):
    pad = jnp.zeros((CAST_ROWS, GLA_MAIN + RANK_PAD - GLA_IN), BF16)
    for r0 in range(0, D, CAST_ROWS):
        rows = slice(r0, r0 + CAST_ROWS)
        wfull[rows, :] = jnp.concatenate([wq_ref[q, rows, :] for q in range(N_CHIPS)] + [pad], axis=1)


GLA_BLOCK = 512
CHUNKS_PER_BLOCK = GLA_BLOCK // CHUNK


def _chunk_masks():
    t = lax.broadcasted_iota(jnp.int32, (CHUNK, CHUNK), 0)
    u = lax.broadcasted_iota(jnp.int32, (CHUNK, CHUNK), 1)
    return t >= u, t <= u


def _gla_chunk_terms(q, cum):
    ep = jnp.exp(cum)
    en = jnp.exp(-cum)
    qs = q * (HEAD_K ** -0.5)
    last = cum[CHUNK - 1:CHUNK, :]
    ed = jnp.exp(last - cum)
    dec = jnp.exp(last)
    return ep, en, qs, ed, dec


def gla_forward(qk, v, cum):
    s = qk.shape[0]
    nb = s // GLA_BLOCK
    nc = s // CHUNK

    def body(q_ref, k_ref, v_ref, cum_ref, o_ref, st_ref, sc_ref, state):
        @pl.when(pl.program_id(0) == 0)
        def _():
            state[...] = jnp.zeros_like(state)

        lower, _ = _chunk_masks()

        def chunk(cc, carry):
            rows = pl.ds(pl.multiple_of(cc * CHUNK, CHUNK), CHUNK)
            for h in range(HEADS):
                kc = slice(h * HEAD_K, (h + 1) * HEAD_K)
                vc = slice(h * HEAD_V, (h + 1) * HEAD_V)
                q = q_ref[rows, kc]
                k = k_ref[rows, kc]
                v = v_ref[rows, vc]
                ep, en, qs, ed, dec = _gla_chunk_terms(q, cum_ref[rows, kc])
                a = _bf(qs * ep)
                fwd = _nt(a, _bf(k * en))
                bwd = _nt(_bf(qs * en), _bf(k * ep))
                scores = _bf(jnp.where(lower, fwd, bwd))
                sc_ref[rows, h * CHUNK:(h + 1) * CHUNK] = scores
                st = state[h]
                st_ref[cc, h] = st
                o_ref[rows, vc] = _nn(scores, v) + _nt(a, _bf(st))
                state[h] = st * dec + _tn(v, _bf(k * ed))
            return carry

        lax.fori_loop(0, CHUNKS_PER_BLOCK, chunk, 0, unroll=True)

    return pl.pallas_call(
        body, name="gla_forward", grid=(nb,),
        out_shape=(jax.ShapeDtypeStruct((s, D), F32),
                   jax.ShapeDtypeStruct((nc, HEADS, HEAD_V, HEAD_K), F32),
                   jax.ShapeDtypeStruct((s, HEADS * CHUNK), BF16)),
        in_specs=[pl.BlockSpec((GLA_BLOCK, KEY_W), lambda i: (i, 0)),
                  pl.BlockSpec((GLA_BLOCK, KEY_W), lambda i: (i, 1)),
                  pl.BlockSpec((GLA_BLOCK, D), lambda i: (i, 0)),
                  pl.BlockSpec((GLA_BLOCK, KEY_W), lambda i: (i, 0))],
        out_specs=(pl.BlockSpec((GLA_BLOCK, D), lambda i: (i, 0)),
                   pl.BlockSpec((CHUNKS_PER_BLOCK, HEADS, HEAD_V, HEAD_K), lambda i: (i, 0, 0, 0)),
                   pl.BlockSpec((GLA_BLOCK, HEADS * CHUNK), lambda i: (i, 0))),
        scratch_shapes=[pltpu.VMEM((HEADS, HEAD_V, HEAD_K), F32)],
        compiler_params=_params("arbitrary"),
    )(qk, qk, v, cum)


def gla_backward(qk, v, cum, do, states, scores):
    s = qk.shape[0]
    nb = s // GLA_BLOCK

    def body(q_ref, k_ref, v_ref, cum_ref, do_ref, st_ref, sc_ref, dq_ref, dk_ref, dv_ref, dcum_ref, dstate):
        @pl.when(pl.program_id(0) == 0)
        def _():
            dstate[...] = jnp.zeros_like(dstate)

        lower, _ = _chunk_masks()
        is_last = lax.broadcasted_iota(jnp.int32, (CHUNK, HEAD_K), 0) == CHUNK - 1

        def chunk(step, carry):
            cc = CHUNKS_PER_BLOCK - 1 - step
            rows = pl.ds(pl.multiple_of(cc * CHUNK, CHUNK), CHUNK)
            for h in range(HEADS):
                kc = slice(h * HEAD_K, (h + 1) * HEAD_K)
                vc = slice(h * HEAD_V, (h + 1) * HEAD_V)
                q = q_ref[rows, kc]
                k = k_ref[rows, kc]
                v = v_ref[rows, vc]
                do_c = do_ref[rows, vc]
                ep, en, qs, ed, dec = _gla_chunk_terms(q, cum_ref[rows, kc])
                a = _bf(qs * ep)
                b = _bf(k * en)
                c = _bf(qs * en)
                dk_dec = _bf(k * ep)
                kd = _bf(k * ed)
                scores = sc_ref[rows, h * CHUNK:(h + 1) * CHUNK]
                st = st_ref[cc, h]
                dst = dstate[h]
                dst_bf = _bf(dst)

                dscores = _nt(do_c, v)
                dfwd = _bf(jnp.where(lower, dscores, 0.0))
                dbwd = _bf(jnp.where(lower, 0.0, dscores))
                dv_ref[rows, vc] = _bf(_tn(scores, do_c) + _nt(kd, dst_bf))
                da = _nn(dfwd, b) + _nn(do_c, _bf(st))
                db = _tn(dfwd, a)
                dc = _nn(dbwd, dk_dec)
                ddk = _tn(dbwd, c)
                dkd = _nn(v, dst_bf)
                ddec = jnp.sum(dst * st, axis=0, keepdims=True)
                dstate[h] = dst * dec + _tn(do_c, a)

                m = dkd * k * ed
                dq_ref[rows, kc] = _bf((da * ep + dc * en) * (HEAD_K ** -0.5))
                dk_ref[rows, kc] = _bf(db * en + ddk * ep + dkd * ed)
                dcum = (da * qs + ddk * k) * ep - (db * k + dc * qs) * en - m
                dlast = jnp.sum(m, axis=0, keepdims=True) + ddec * dec
                dcum_ref[rows, kc] = dcum + jnp.where(is_last, dlast, 0.0)
            return carry

        lax.fori_loop(0, CHUNKS_PER_BLOCK, chunk, 0, unroll=True)

    rev = lambda cols, col_block: pl.BlockSpec((GLA_BLOCK, cols), lambda i: (nb - 1 - i, col_block))
    return pl.pallas_call(
        body, name="gla_backward", grid=(nb,),
        out_shape=(jax.ShapeDtypeStruct((s, KEY_W), BF16), jax.ShapeDtypeStruct((s, KEY_W), BF16),
                   jax.ShapeDtypeStruct((s, D), BF16), jax.ShapeDtypeStruct((s, KEY_W), F32)),
        in_specs=[rev(KEY_W, 0), rev(KEY_W, 1), rev(D, 0), rev(KEY_W, 0), rev(D, 0),
                  pl.BlockSpec((CHUNKS_PER_BLOCK, HEADS, HEAD_V, HEAD_K), lambda i: (nb - 1 - i, 0, 0, 0)),
                  rev(HEADS * CHUNK, 0)],
        out_specs=(rev(KEY_W, 0), rev(KEY_W, 0), rev(D, 0), rev(KEY_W, 0)),
        scratch_shapes=[pltpu.VMEM((HEADS, HEAD_V, HEAD_K), F32)],
        compiler_params=_params("arbitrary"),
    )(qk, qk, v, cum, do, states, scores)


def head_and_loss(o, gate, h1, target, hw, wgo, wf):
    s = o.shape[0]
    ts = ROW_TILE

    def body(o_ref, gate_ref, h1_ref, tgt_ref, hw_ref, wgo_ref, wf_ref,
             dh2_ref, do_ref, dgate_ref, ggo_ref, small_ref):
        @pl.when(pl.program_id(0) == 0)
        def _():
            ggo_ref[...] = jnp.zeros_like(ggo_ref)
            small_ref[...] = jnp.zeros_like(small_ref)

        gate = gate_ref[...]
        hw = hw_ref[...]
        sg = _sigmoid(gate)
        silu = gate * sg
        ohat, ro = [], []
        for h in range(HEADS):
            oh = o_ref[:, h * HEAD_V:(h + 1) * HEAD_V]
            rh = lax.rsqrt(jnp.mean(oh * oh, axis=-1, keepdims=True) + EPS)
            ro.append(rh)
            ohat.append(oh * rh)
        ohat = jnp.concatenate(ohat, axis=-1)
        on = ohat * hw
        y2 = _bf(on * silu)
        h2 = h1_ref[...] + _nn(y2, wgo_ref[...])
        rf = lax.rsqrt(jnp.mean(h2 * h2, axis=-1, keepdims=True) + EPS)
        h2hat = h2 * rf
        wf = wf_ref[...]
        diff = h2hat * wf - tgt_ref[...]
        small_ref[2:3, :] += jnp.zeros((1, D), F32) + 0.5 * jnp.sum(diff * diff) / D
        dout = diff / D
        small_ref[0:1, :] += jnp.sum(dout * h2hat, axis=0, keepdims=True)
        dxh = dout * wf
        dh2 = rf * (dxh - h2hat * jnp.mean(dxh * h2hat, axis=-1, keepdims=True))
        dh2_ref[...] = dh2
        dh2_bf = _bf(dh2)
        ggo_ref[...] += _tn(y2, dh2_bf)
        dy2 = _nt(dh2_bf, wgo_ref[...])
        don = dy2 * silu
        dgate_ref[...] = _bf(dy2 * on * (sg * (1.0 + gate * (1.0 - sg))))
        ghw = jnp.sum(don * ohat, axis=0, keepdims=True)
        small_ref[1:2, 0:HEAD_V] += sum(ghw[:, h * HEAD_V:(h + 1) * HEAD_V] for h in range(HEADS))
        dohat = don * hw
        for h in range(HEADS):
            cols = slice(h * HEAD_V, (h + 1) * HEAD_V)
            oh, dh = ohat[:, cols], dohat[:, cols]
            do_ref[:, cols] = _bf(ro[h] * (dh - oh * jnp.mean(dh * oh, axis=-1, keepdims=True)))

    row = lambda cols: pl.BlockSpec((ts, cols), lambda i: (i, 0))
    act = jax.ShapeDtypeStruct((s, D), F32)
    act_bf = jax.ShapeDtypeStruct((s, D), BF16)
    return pl.pallas_call(
        body, name="head_and_loss", grid=(s // ts,),
        out_shape=(act, act_bf, act_bf, jax.ShapeDtypeStruct((D, D), F32), jax.ShapeDtypeStruct((8, D), F32)),
        in_specs=[row(D), row(D), row(D), row(D),
                  _full((1, D)), _full((D, D)), _full((1, D))],
        out_specs=(row(D), row(D), row(D), _full((D, D)), _full((8, D))),
        compiler_params=_params("arbitrary"),
    )(o, gate, h1, target, hw, wgo, wf)


def gla_project_backward(dq, dk, dv, dgate, dcum, low, h1, dh2, w1, wgi_q, wgk, bgk):
    s = h1.shape[0]
    ts = ROW_TILE

    def body(dq_ref, dk_ref, dv_ref, dgate_ref, dcum_ref, low_ref, h1_ref, dh2_ref, w1_ref,
             wq_ref, wgk_ref, bgk_ref, dh1_ref, dproj_ref, ggk_ref, small_ref, wgi_ref):
        @pl.when(pl.program_id(0) == 0)
        def _():
            ggk_ref[...] = jnp.zeros_like(ggk_ref)
            small_ref[...] = jnp.zeros_like(small_ref)
            _assemble_gla_in(wq_ref, wgi_ref)

        low = low_ref[...]
        z = _nn(low, wgk_ref[...]) + bgk_ref[...]
        upper_f = _chunk_masks()[1].astype(F32)
        dlg = jnp.concatenate([_nn_exact(upper_f, dcum_ref[r0:r0 + CHUNK, :]) for r0 in range(0, ts, CHUNK)],
                              axis=0)
        dz = dlg * (1.0 / GATE_NORM) * _sigmoid(-z)
        dz_bf = _bf(dz)
        ggk_ref[...] += _tn(low, dz_bf)
        small_ref[1:2, 0:KEY_W] += jnp.sum(dz, axis=0, keepdims=True)
        dlow = _bf(_nt(dz_bf, wgk_ref[...]))
        dproj_ref[:, GLA_MAIN:] = dlow
        dn1 = _nt(dlow, wgi_ref[:, GLA_MAIN:])
        for ref, lo, hi in ((dq_ref, 0, KEY_W), (dk_ref, KEY_W, 2 * KEY_W),
                            (dv_ref, 2 * KEY_W, 2 * KEY_W + D), (dgate_ref, 2 * KEY_W + D, GLA_MAIN)):
            piece = ref[...]
            dproj_ref[:, lo:hi] = piece
            dn1 = dn1 + _nt(piece, wgi_ref[:, lo:hi])
        hv = h1_ref[...]
        r = lax.rsqrt(jnp.mean(hv * hv, axis=-1, keepdims=True) + EPS)
        hhat = hv * r
        small_ref[0:1, :] += jnp.sum(dn1 * hhat, axis=0, keepdims=True)
        dxh = dn1 * w1_ref[...]
        dh1_ref[...] = dh2_ref[...] + r * (dxh - hhat * jnp.mean(dxh * hhat, axis=-1, keepdims=True))

    row = lambda cols: pl.BlockSpec((ts, cols), lambda i: (i, 0))
    return pl.pallas_call(
        body, name="gla_project_backward", grid=(s // ts,),
        out_shape=(jax.ShapeDtypeStruct((s, D), F32), jax.ShapeDtypeStruct((s, GLA_MAIN + RANK_PAD), BF16),
                   jax.ShapeDtypeStruct((RANK_PAD, KEY_W), F32),
                   jax.ShapeDtypeStruct((8, D), F32)),
        in_specs=[row(KEY_W), row(KEY_W), row(D), row(D), row(KEY_W), row(RANK_PAD), row(D), row(D),
                  _full((1, D)), _full((N_CHIPS, D, GLA_IN_QUARTER)), _full((RANK_PAD, KEY_W)),
                  _full((1, KEY_W))],
        out_specs=(row(D), row(GLA_MAIN + RANK_PAD), _full((RANK_PAD, KEY_W)), _full((8, D))),
        scratch_shapes=[pltpu.VMEM((D, GLA_MAIN + RANK_PAD), BF16)],
        compiler_params=_params("arbitrary"),
    )(dq, dk, dv, dgate, dcum, low, h1, dh2, w1, wgi_q, wgk, bgk)


def local_gradients(xs, target, w0, w1, wf, wpi, gw, gb, scale, wpo, gla_quarters, wgk, bgk, hw_tiled, place):
    wgi_q, wgo_q = gla_quarters
    (h1, pooled, gt, n0), (wgi_q,) = pool_forward(xs, w0, wpi, gw, gb, scale, wpo, [wgi_q])
    (qk, v, gate, low, cum, n1), (wgo_q,) = gla_project(h1, w1, wgi_q, wgk, bgk, [wgo_q])
    wgo = wgo_q.reshape(D, D)
    o, states, scores = gla_forward(qk, v, cum)

    dh2, do, dgate, g_gla_out, small_top = head_and_loss(o, gate, h1, target, hw_tiled, wgo, wf)
    dq, dk, dv, dcum = gla_backward(qk, v, cum, do, states, scores)
    dh1, dproj, g_gk_pad, small_gla = gla_project_backward(
        dq, dk, dv, dgate, dcum, low, h1, dh2, w1, wgi_q, wgk, bgk)
    g_gla_in, _ = matmul_tn(n1, dproj, "grad_gla_in", tile_n=(GLA_MAIN + RANK_PAD) // 5)

    def chip_sums(grads, tag):
        return add_halves(grads, place, "add_halves_" + tag)

    gla_sums = chip_sums([g_gla_in, g_gla_out.reshape(N_CHIPS, D // N_CHIPS, D)], "gla")
    (dx, dpool, g_pool_out, g_group_w, small_pool), gla_got = pool_backward(
        xs, dh1, pooled, gt, w0, wpi, gw, gb, scale, wpo, [b for _, b in gla_sums])
    mix_sums = chip_sums([g_group_w, g_pool_out.reshape(N_CHIPS, D // N_CHIPS, D)], "pool_mix")
    g_pool_in, mix_got = matmul_tn(n0, dpool, "grad_pool_in", tile_n=D // 2, by_column_tile=True,
                                   chip_sums=[b for _, b in mix_sums])

    in_sums = add_halves([g_pool_in], place, "add_halves_and_scatter_pool_in", scatter=True)
    reduced, total = join_halves(
        [f for f, _ in in_sums + mix_sums + gla_sums], [got for _, got in in_sums] + list(mix_got) + list(gla_got),
        small_pool, small_gla, small_top, g_gk_pad)
    return dx, reduced, total


def kernel(x, norm_w, pool_in_w, pool_group_w, pool_group_b, pool_scale, pool_out_w, gla_in_w, gla_gk_w, gla_gk_b, gla_head_norm_w, gla_out_w, final_norm_w, loss_target, m_norm_w, m_pool_in_w, m_pool_group_w, m_pool_group_b, m_pool_scale, m_pool_out_w, m_gla_in_w, m_gla_gk_w, m_gla_gk_b, m_gla_head_norm_w, m_gla_out_w, m_final_norm_w, v_norm_w, v_pool_in_w, v_pool_group_w, v_pool_group_b, v_pool_scale, v_pool_out_w, v_gla_in_w, v_gla_gk_w, v_gla_gk_b, v_gla_head_norm_w, v_gla_out_w, v_final_norm_w):
    xs = x[0]
    target = loss_target[0]
    q_chip = 2 * lax.axis_index("x") + lax.axis_index("y")
    place = jnp.stack([lax.axis_index("c"), q_chip]).astype(jnp.int32)

    (wpi, gw_q, wpo_q, wgi_q, wgo_q), (bgk, hw_tiled, gb, wgk) = allgather_weights(
        [pool_in_w[0], pool_group_w[0].reshape(GROUP_DIM, GROUP_DIM), pool_out_w[0], gla_in_w[0], gla_out_w[0]],
        exchange=(True, True, True, False, False),
        smalls=[gla_gk_b, gla_head_norm_w, pool_group_b[0], gla_gk_w[0]])
    wpo = wpo_q.reshape(D, D)

    w0 = norm_w[0:1]
    w1 = norm_w[1:2]
    wf = final_norm_w.reshape(1, D)

    dx, reduced, total = local_gradients(
        xs, target, w0, w1, wf, wpi, gw_q, gb, pool_scale, wpo, [wgi_q, wgo_q], wgk, bgk, hw_tiled, place)
    r_pool_in, r_group_w, r_pool_out, r_gla_in, r_gla_out = reduced
    r_group_w = r_group_w.reshape(GROUPS, 64, GROUP_DIM)

    turn = lambda a: jnp.transpose(a, (2, 0, 1))
    back = lambda a: jnp.transpose(a, (1, 2, 0))
    as2d = lambda a, w: a.reshape(-1, w.shape[-1])
    big_names = ("pool_in_w", "pool_group_w", "pool_out_w", "gla_in_w", "gla_out_w")
    big_args = [(pool_in_w, r_pool_in[None], m_pool_in_w, v_pool_in_w),
                (pool_group_w, r_group_w[None], m_pool_group_w, v_pool_group_w),
                (pool_out_w, r_pool_out[None], m_pool_out_w, v_pool_out_w),
                (gla_in_w, r_gla_in[None], m_gla_in_w, v_gla_in_w),
                (gla_out_w, r_gla_out[None], m_gla_out_w, v_gla_out_w)]
    to_kernel = lambda n, a, w: turn(a) if n == "gla_in_w" else as2d(a, w)
    from_kernel = lambda n, a, w: back(a) if n == "gla_in_w" else a.reshape(w.shape)
    big_in = [tuple(to_kernel(n, a, p[0]) for a in p) for n, p in zip(big_names, big_args)]
    big_out = adamw(big_in, "adamw")
    big = {n: (from_kernel(n, i[1], p[0]),) + tuple(from_kernel(n, o, p[0]) for o in out)
           for n, p, i, out in zip(big_names, big_args, big_in, big_out)}

    small_names = ("norm_w", "pool_group_b", "pool_scale", "gla_gk_w", "gla_gk_b", "gla_head_norm_w",
                   "final_norm_w")
    small_args = [(norm_w, m_norm_w, v_norm_w),
                  (pool_group_b, m_pool_group_b, v_pool_group_b),
                  (pool_scale, m_pool_scale, v_pool_scale),
                  (gla_gk_w, m_gla_gk_w, v_gla_gk_w),
                  (gla_gk_b, m_gla_gk_b, v_gla_gk_b),
                  (gla_head_norm_w, m_gla_head_norm_w, v_gla_head_norm_w),
                  (final_norm_w, m_final_norm_w, v_final_norm_w)]
    small_out, loss = adamw_small([tuple(as2d(a, p[0]) for a in p) for p in small_args], total, place)
    small = {n: tuple(o.reshape(p[0].shape) for o in out) for n, p, out in zip(small_names, small_args, small_out)}
    results = [
        small["norm_w"],
        big["pool_in_w"],
        big["pool_group_w"],
        small["pool_group_b"],
        small["pool_scale"],
        big["pool_out_w"],
        big["gla_in_w"],
        small["gla_gk_w"],
        small["gla_gk_b"],
        small["gla_head_norm_w"],
        big["gla_out_w"],
        small["final_norm_w"],
    ]
    grads, deltas, new_m, new_v = zip(*results)
    return (loss.reshape(()), dx[None], *grads, *deltas, *new_m, *new_v)
```

```python
import jax
import jax.numpy as jnp
from jax import lax
from jax.experimental import pallas as pl
from jax.experimental.pallas import tpu as pltpu

F32 = jnp.float32
BF16 = jnp.bfloat16
MESH = pl.DeviceIdType.MESH

D = 1024
POOL_WINDOWS = (2, 4, 8, 16)
GROUPS = 4
GROUP_DIM = 256
HEADS = 4
HEAD_K = 128
HEAD_V = 256
KEY_W = 512
CHUNK = 64
GATE_RANK = 16
GATE_NORM = 16.0
GLA_IN = 3088
GLA_MAIN = 3072
RANK_PAD = 128
EPS = 1e-6
HALO = 32

ADAM_LR = 0.001
ADAM_B1 = 0.9
ADAM_B2 = 0.999
ADAM_EPS = 1e-08
ADAM_WD = 0.01
ADAM_STEP = 10

N_CHIPS = 4
N_DEV = 8
GLA_IN_QUARTER = GLA_IN // N_CHIPS

VMEM_LIMIT = 56 * 1024 * 1024


def _nn(a, b):
    return lax.dot_general(a, b, (((1,), (0,)), ((), ())), preferred_element_type=F32)


def _nt(a, b):
    return lax.dot_general(a, b, (((1,), (1,)), ((), ())), preferred_element_type=F32)


def _tn(a, b):
    return lax.dot_general(a, b, (((0,), (0,)), ((), ())), preferred_element_type=F32)


def _nn_exact(a, b):
    return lax.dot_general(a, b, (((1,), (0,)), ((), ())), preferred_element_type=F32,
                           precision=lax.Precision.HIGHEST)


def _bf(a):
    return a.astype(BF16)


def _params(*sem):
    return pltpu.CompilerParams(dimension_semantics=sem, vmem_limit_bytes=VMEM_LIMIT)


def _full(shape):
    return pl.BlockSpec(shape, lambda i: (0,) * len(shape))


def _position():
    return lax.axis_index("x"), lax.axis_index("y"), lax.axis_index("c")


def _gather_small(in_ref, all_ref, send_sems, recv_sems, local_sem):
    x, y, c = _position()
    me = 4 * x + 2 * y + c
    mine = pltpu.make_async_copy(in_ref, all_ref.at[me], local_sem)
    sends = []
    for k in range(N_DEV - 1):
        fx, fy, fc = (k + 1) >> 2 & 1, (k + 1) >> 1 & 1, (k + 1) & 1
        sends.append(pltpu.make_async_remote_copy(
            src_ref=in_ref, dst_ref=all_ref.at[me],
            send_sem=send_sems.at[k], recv_sem=recv_sems.at[k],
            device_id=(x ^ fx, y ^ fy, c ^ fc), device_id_type=MESH))

    def start():
        mine.start()
        for cp in sends:
            cp.start()

    def wait():
        for k in range(N_DEV - 1):
            fx, fy, fc = (k + 1) >> 2 & 1, (k + 1) >> 1 & 1, (k + 1) & 1
            src_dev = 4 * (x ^ fx) + 2 * (y ^ fy) + (c ^ fc)
            pltpu.make_async_remote_copy(
                src_ref=in_ref, dst_ref=all_ref.at[src_dev],
                send_sem=send_sems.at[k], recv_sem=recv_sems.at[k],
                device_id=(x, y, c), device_id_type=MESH).wait_recv()
        for cp in sends:
            cp.wait_send()
        mine.wait()

    return start, wait


SMALL_SEMS = [pltpu.SemaphoreType.DMA((N_DEV - 1,)), pltpu.SemaphoreType.DMA((N_DEV - 1,)),
              pltpu.SemaphoreType.DMA]
VMEM_SPEC = pl.BlockSpec(memory_space=pltpu.VMEM)


def _other_chips(x, y):
    return [(1 - x, y), (x, 1 - y), (1 - x, 1 - y)]


def _any_specs(n):
    return [pl.BlockSpec(memory_space=pl.ANY)] * n


def _halves(rows, c):
    half = rows // 2
    return pl.ds(c * half, half), pl.ds((1 - c) * half, half)


CAST_ROWS = 256


def _gather_copy(out_ref, send_sems, recv_sems, k, quarter, half, to, src=None):
    dst = out_ref.at[quarter, half]
    return pltpu.make_async_remote_copy(
        src_ref=dst if src is None else src, dst_ref=dst,
        send_sem=send_sems.at[k], recv_sem=recv_sems.at[k], device_id=to, device_id_type=MESH)


SMALL_IN_ROWS = 24


def allgather_weights(quarters, exchange, smalls):
    n = len(quarters)
    shapes = [w.shape for w in quarters]
    moved = [i for i in range(n) if exchange[i]]

    def body(*refs):
        w_refs, (gkb_ref, hnw_ref, gb_ref, gkw_ref) = refs[:n], refs[n:n + 4]
        out_refs, (bgk_ref, hw_ref, gbias_ref, wgk_ref) = refs[n + 4:2 * n + 4], refs[2 * n + 4:2 * n + 8]
        refs = refs[2 * n + 8:]
        f32_bufs, bf_bufs = refs[:n], refs[n:2 * n]
        send_sems, recv_sems, local_sems, small_ref, small_all_ref = refs[2 * n:2 * n + 5]
        small_ref[...] = jnp.zeros_like(small_ref)
        small_ref[0:1, :] = gkb_ref[...]
        small_ref[1:2, 0:64] = hnw_ref[...]
        small_ref[2:2 + GROUPS, 0:64] = gb_ref[...]
        small_ref[8:8 + GATE_RANK, :] = gkw_ref[...]
        start_small, wait_small = _gather_small(small_ref, small_all_ref, *refs[2 * n + 5:])
        start_small()
        x, y, c = _position()
        q = 2 * x + y
        sibling = (x, y, 1 - c)
        chips = _other_chips(x, y)

        def copy(k, i, quarter, half, to, src=None):
            return _gather_copy(out_refs[i], send_sems, recv_sems, k * n + i, quarter, half, to, src)

        loads = [pltpu.make_async_copy(w_refs[i], f32_bufs[i], local_sems.at[i]) for i in range(n)]
        for cp in loads:
            cp.start()
        keeps, sends = [], []
        for i in range(n):
            loads[i].wait()
            for r0 in range(0, shapes[i][0], CAST_ROWS):
                bf_bufs[i][r0:r0 + CAST_ROWS, :] = _bf(f32_bufs[i][r0:r0 + CAST_ROWS, :])
            keep = pltpu.make_async_copy(bf_bufs[i], out_refs[i].at[q], local_sems.at[n + i])
            keep.start()
            keeps.append(keep)
            if not exchange[i]:
                continue
            mine, _ = _halves(shapes[i][0], c)
            for j, chip in enumerate(chips):
                cp = copy(j, i, q, mine, (*chip, c), src=bf_bufs[i].at[mine])
                cp.start()
                sends.append(cp)
        for j, chip in enumerate(chips):
            qj = 2 * chip[0] + chip[1]
            for i in moved:
                mine, _ = _halves(shapes[i][0], c)
                copy(j, i, qj, mine, (x, y, c)).wait_recv()
                cp = copy(3 + j, i, qj, mine, sibling)
                cp.start()
                sends.append(cp)
        for j, chip in enumerate(chips):
            qj = 2 * chip[0] + chip[1]
            for i in moved:
                _, other = _halves(shapes[i][0], c)
                copy(3 + j, i, qj, other, (x, y, c)).wait_recv()
        wait_small()
        wgk_ref[...] = jnp.zeros_like(wgk_ref)
        for j in range(N_CHIPS):
            block = small_all_ref.at[2 * j]
            bgk_ref[:, 128 * j:128 * (j + 1)] = block[0:1, :]
            for h in range(HEADS):
                hw_ref[:, HEAD_V * h + 64 * j:HEAD_V * h + 64 * (j + 1)] = block[1:2, 0:64]
            for g in range(GROUPS):
                gbias_ref[:, GROUP_DIM * g + 64 * j:GROUP_DIM * g + 64 * (j + 1)] = block[2 + g:3 + g, 0:64]
            wgk_ref[0:GATE_RANK, 128 * j:128 * (j + 1)] = _bf(block[8:8 + GATE_RANK, :])
        for cp in sends:
            cp.wait_send()
        for cp in keeps:
            cp.wait()

    outs = pl.pallas_call(
        body, name="allgather_weights",
        out_shape=[jax.ShapeDtypeStruct((N_CHIPS, *s), BF16) for s in shapes]
                  + [jax.ShapeDtypeStruct((1, KEY_W), F32), jax.ShapeDtypeStruct((1, D), F32),
                     jax.ShapeDtypeStruct((1, D), F32), jax.ShapeDtypeStruct((RANK_PAD, KEY_W), BF16)],
        in_specs=_any_specs(n) + [VMEM_SPEC] * 4, out_specs=_any_specs(n) + [VMEM_SPEC] * 4,
        scratch_shapes=([pltpu.VMEM(s, F32) for s in shapes] + [pltpu.VMEM(s, BF16) for s in shapes]
                        + [pltpu.SemaphoreType.DMA((6 * n,)), pltpu.SemaphoreType.DMA((6 * n,)),
                           pltpu.SemaphoreType.DMA((2 * n,)), pltpu.VMEM((SMALL_IN_ROWS, 128), F32),
                           pltpu.VMEM((N_DEV, SMALL_IN_ROWS, 128), F32)] + SMALL_SEMS),
        compiler_params=pltpu.CompilerParams(vmem_limit_bytes=VMEM_LIMIT),
    )(*quarters, *smalls)
    return outs[:n], outs[n:]


def _scatter_copies(b_refs, got_refs, send_sems, recv_sems):
    n = len(b_refs)
    x, y, c = _position()
    copies = []
    for j, chip in enumerate(_other_chips(x, y)):
        qj = 2 * chip[0] + chip[1]
        for i in range(n):
            copies.append(pltpu.make_async_remote_copy(
                src_ref=b_refs[i].at[qj], dst_ref=got_refs[i].at[j],
                send_sem=send_sems.at[j * n + i], recv_sem=recv_sems.at[j * n + i],
                device_id=(*chip, c), device_id_type=MESH))
    return copies


def _scatter_shapes(chip_sums):
    return [jax.ShapeDtypeStruct((N_CHIPS - 1, *b.shape[1:]), BF16) for b in chip_sums]


ADD_ROWS = 512
ADD_HALVES_ROWS = 128


def _spans(counts):
    starts, total = [], 0
    for count in counts:
        starts.append(total)
        total += count
    return starts, total


def _local_step(t, start, count):
    return jnp.clip(t - start, 0, count - 1)


def add_halves(grads, place, name, scatter=False):
    n = len(grads)
    whole = [len(g.shape) == 2 for g in grads]
    halves = [g.shape[-2] // 2 for g in grads]
    cols = [GLA_IN_QUARTER if w else g.shape[-1] for g, w in zip(grads, whole)]
    rbs = [min(ADD_HALVES_ROWS, h) for h in halves]
    counts = [h // rb for h, rb in zip(halves, rbs)]
    starts, total = _spans(counts)
    half_shapes = [(*g.shape[:-2], h, g.shape[-1]) for g, h in zip(grads, halves)]

    def rows_of(ref, i, start):
        return ref.at[pl.ds(start, rbs[i])] if whole[i] else ref.at[:, pl.ds(start, rbs[i])]

    def body(place_ref, *refs):
        a_refs, o_refs = refs[:n], refs[n:2 * n]
        f_refs, h_refs = refs[2 * n:3 * n], refs[3 * n:4 * n]
        send_refs, their_refs, rest = refs[4 * n:5 * n], refs[5 * n:6 * n], refs[6 * n:]
        send_sems, recv_sems = rest[:2]
        t = pl.program_id(0)
        q = place_ref[1]
        x, y, c = _position()
        sum_refs = rest[2:2 + n] if scatter else h_refs

        def to_owners(i, k):
            out_sems, in_sems = rest[2 + n:]
            rows = pl.ds(k * rbs[i], rbs[i])
            return [pltpu.make_async_remote_copy(
                src_ref=sum_refs[i].at[2 * chip[0] + chip[1], rows], dst_ref=h_refs[i].at[j, rows],
                send_sem=out_sems.at[3 * (starts[i] + k) + j], recv_sem=in_sems.at[3 * (starts[i] + k) + j],
                device_id=(*chip, c), device_id_type=MESH) for j, chip in enumerate(_other_chips(x, y))]

        copies = [[pltpu.make_async_remote_copy(
            src_ref=rows_of(send_refs[i], i, k * rbs[i]), dst_ref=rows_of(their_refs[i], i, k * rbs[i]),
            send_sem=send_sems.at[starts[i] + k], recv_sem=recv_sems.at[starts[i] + k],
            device_id=(x, y, 1 - c), device_id_type=MESH) for k in range(counts[i])] for i in range(n)]

        for i in range(n):
            for k in range(counts[i]):
                @pl.when(t == starts[i] + k)
                def _(i=i, k=k):
                    rows_of(send_refs[i], i, k * rbs[i])[...] = _bf(o_refs[i][...])
                    copies[i][k].start()

        for i in range(n):
            for k in range(counts[i]):
                @pl.when(t == starts[i] + k + 1)
                def _(i=i, k=k):
                    copies[i][k].wait_recv()
                    b_ref = rows_of(their_refs[i], i, k * rbs[i])
                    h_ref = sum_refs[i].at[:, pl.ds(k * rbs[i], rbs[i])] if scatter else h_refs[i]
                    if not whole[i]:
                        h_ref[...] = _bf(a_refs[i][...] + b_ref[...].astype(F32))
                        f_refs[i][...] = a_refs[i][q] + b_ref[q].astype(F32)
                    else:
                        total_i = a_refs[i][...] + b_ref[...].astype(F32)
                        for k4 in range(N_CHIPS):
                            piece = total_i[:, k4 * cols[i]:(k4 + 1) * cols[i]]
                            h_ref[k4] = _bf(piece)

                            @pl.when(q == k4)
                            def _():
                                f_refs[i][...] = piece
                    if scatter:
                        for cp in to_owners(i, k):
                            cp.start()

        @pl.when(t == total)
        def _():
            for of_matrix in copies:
                for cp in of_matrix:
                    cp.wait_send()
            if scatter:
                for i in range(n):
                    for k in range(counts[i]):
                        for cp in to_owners(i, k):
                            cp.wait()

    def specs(i):
        sent = lambda t: _local_step(t, starts[i], counts[i])
        added = lambda t: _local_step(t - 1, starts[i], counts[i])
        by_quarter = (N_CHIPS, rbs[i], cols[i])
        block = (rbs[i], grads[i].shape[-1]) if whole[i] else by_quarter
        lead = () if whole[i] else (0,)
        mine = pl.BlockSpec(block, lambda t, place: (*lead, place[0] * counts[i] + added(t), 0))
        other = pl.BlockSpec(block, lambda t, place: (*lead, (1 - place[0]) * counts[i] + sent(t), 0))
        sums = pl.BlockSpec(by_quarter, lambda t, place: (0, added(t), 0))
        own = pl.BlockSpec(by_quarter[1:], lambda t, place: (added(t), 0))
        return mine, other, own, sums

    all_specs = [specs(i) for i in range(n)]
    sum_shapes = [(N_CHIPS, h, cl) for h, cl in zip(halves, cols)]
    scratch = [pltpu.VMEM(sh, BF16) for sh in half_shapes] + [pltpu.VMEM(sh, BF16) for sh in half_shapes]
    scratch += [pltpu.SemaphoreType.DMA((total,)), pltpu.SemaphoreType.DMA((total,))]
    if scatter:
        scratch += [pltpu.VMEM(sh, BF16) for sh in sum_shapes]
        scratch += [pltpu.SemaphoreType.DMA((3 * total,)), pltpu.SemaphoreType.DMA((3 * total,))]
    outs = pl.pallas_call(
        body, name=name,
        grid_spec=pltpu.PrefetchScalarGridSpec(
            num_scalar_prefetch=1, grid=(total + 1,),
            in_specs=[sp[0] for sp in all_specs] + [sp[1] for sp in all_specs],
            out_specs=[sp[2] for sp in all_specs] + (_any_specs(n) if scatter else [sp[3] for sp in all_specs]),
            scratch_shapes=scratch),
        out_shape=[jax.ShapeDtypeStruct((h, cl), F32) for h, cl in zip(halves, cols)]
                  + [jax.ShapeDtypeStruct((N_CHIPS - 1 if scatter else N_CHIPS, *sh[1:]), BF16) for sh in sum_shapes],
        compiler_params=_params("arbitrary"),
    )(place, *grads, *grads)
    return list(zip(outs[:n], outs[n:]))


SMALL_SUM_ROWS = 16


def join_halves(owns, gots, small_pool, small_gla, small_top, g_gk_pad):
    n = len(owns)
    shapes = [g.shape for g in gots]
    rbs = [min(ADD_ROWS, sh[1]) for sh in shapes]
    counts = [sh[1] // rb for sh, rb in zip(shapes, rbs)]
    starts, total = _spans(counts)

    def body(*refs):
        o_refs, g_refs = refs[:n], refs[n:2 * n]
        pool_ref, gla_ref, top_ref, gk_ref = refs[2 * n:2 * n + 4]
        out_refs, total_ref = refs[2 * n + 4:3 * n + 4], refs[3 * n + 4]
        sum_refs = refs[3 * n + 5:4 * n + 5]
        local_sems, send_sems, recv_sems, all_ref, small_ref = refs[4 * n + 5:4 * n + 10]
        t = pl.program_id(0)
        x, y, c = _position()
        start_small, wait_small = _gather_small(small_ref, all_ref, *refs[4 * n + 10:])

        def copies(i, k):
            src = sum_refs[i].at[pl.ds(k * rbs[i], rbs[i])]
            rows = pl.ds(c * shapes[i][1] + k * rbs[i], rbs[i])
            return (pltpu.make_async_copy(src, out_refs[i].at[rows], local_sems.at[starts[i] + k]),
                    pltpu.make_async_remote_copy(
                        src_ref=src, dst_ref=out_refs[i].at[rows],
                        send_sem=send_sems.at[starts[i] + k], recv_sem=recv_sems.at[starts[i] + k],
                        device_id=(x, y, 1 - c), device_id_type=MESH))

        @pl.when(t == 0)
        def _():
            small_ref[0:3, :] = pool_ref[0:3, :]
            small_ref[3:5, :] = gla_ref[0:2, :]
            small_ref[5:8, :] = top_ref[0:3, :]
            for r in range(GATE_RANK):
                small_ref[8 + r // 2:9 + r // 2, (r % 2) * KEY_W:(r % 2 + 1) * KEY_W] = gk_ref[r:r + 1, :]
            start_small()

        for i in range(n):
            for k in range(counts[i]):
                @pl.when(t == starts[i] + k)
                def _(i=i, k=k):
                    total_i = o_refs[i][...]
                    for j in range(N_CHIPS - 1):
                        total_i = total_i + g_refs[i][j].astype(F32)
                    sum_refs[i][k * rbs[i]:(k + 1) * rbs[i], :] = total_i
                    for cp in copies(i, k):
                        cp.start()

        @pl.when(t == total - 1)
        def _():
            wait_small()
            small_total = all_ref[0]
            for dev in range(1, N_DEV):
                small_total = small_total + all_ref[dev]
            total_ref[...] = small_total
            for i in range(n):
                for k in range(counts[i]):
                    for cp in copies(i, k):
                        cp.wait()

    def specs(i):
        rb, cols = rbs[i], shapes[i][2]
        step = lambda t: _local_step(t, starts[i], counts[i])
        return (pl.BlockSpec((rb, cols), lambda t: (step(t), 0)),
                pl.BlockSpec((N_CHIPS - 1, rb, cols), lambda t: (0, step(t), 0)))

    all_specs = [specs(i) for i in range(n)]
    outs = pl.pallas_call(
        body, name="join_halves", grid=(total,),
        out_shape=[jax.ShapeDtypeStruct((2 * sh[1], sh[2]), F32) for sh in shapes]
                  + [jax.ShapeDtypeStruct((SMALL_SUM_ROWS, D), F32)],
        in_specs=[sp[0] for sp in all_specs] + [sp[1] for sp in all_specs] + [VMEM_SPEC] * 4,
        out_specs=_any_specs(n) + [VMEM_SPEC],
        scratch_shapes=[pltpu.VMEM(sh[1:], F32) for sh in shapes]
                       + [pltpu.SemaphoreType.DMA((total,)), pltpu.SemaphoreType.DMA((total,)),
                          pltpu.SemaphoreType.DMA((total,)),
                          pltpu.VMEM((N_DEV, SMALL_SUM_ROWS, D), F32), pltpu.VMEM((SMALL_SUM_ROWS, D), F32)]
                       + SMALL_SEMS,
        compiler_params=_params("arbitrary"),
    )(*owns, *gots, small_pool, small_gla, small_top, g_gk_pad)
    return outs[:n], outs[n]


def _adam_math(w, g, m, v):
    m = ADAM_B1 * m + (1.0 - ADAM_B1) * g
    v = ADAM_B2 * v + (1.0 - ADAM_B2) * (g * g)
    m_hat = m / (1.0 - ADAM_B1 ** ADAM_STEP)
    v_hat = v / (1.0 - ADAM_B2 ** ADAM_STEP)
    delta = -ADAM_LR * (m_hat / (jnp.sqrt(v_hat) + ADAM_EPS) + ADAM_WD * w)
    return delta, m, v


ADAM_BLOCK_BYTES = 2 ** 19
ADAM_MOST_STEPS = 8


def adamw(params, name):
    n = len(params)
    shapes = [p[0].shape for p in params]

    def tile_rows(shape):
        rows, cols = shape[0], shape[-1]
        aligned = 1 if len(shape) == 3 else 8
        divisors = [t for t in range(aligned, rows + 1, aligned) if rows % t == 0]
        tile = max(t for t in divisors if t * cols * 4 <= ADAM_BLOCK_BYTES)
        if rows // tile > ADAM_MOST_STEPS:
            tile = min(t for t in divisors if rows // t <= ADAM_MOST_STEPS)
        return tile

    tiles = [tile_rows(sh) for sh in shapes]
    counts = [sh[0] // tl for sh, tl in zip(shapes, tiles)]
    starts, total = _spans(counts)

    def body(*refs):
        ins, outs = refs[:4 * n], refs[4 * n:]
        t = pl.program_id(0)
        for i in range(n):
            @pl.when((t >= starts[i]) & (t < starts[i] + counts[i]))
            def _(i=i):
                w_ref, g_ref, m_ref, v_ref = ins[4 * i:4 * i + 4]
                d, nm, nv = _adam_math(w_ref[...], g_ref[...], m_ref[...], v_ref[...])
                outs[3 * i][...] = d
                outs[3 * i + 1][...] = nm
                outs[3 * i + 2][...] = nv

    def spec(i):
        block = (tiles[i],) + shapes[i][1:]
        zeros = (0,) * (len(block) - 1)
        return pl.BlockSpec(block, lambda t: (_local_step(t, starts[i], counts[i]),) + zeros)

    outs = pl.pallas_call(
        body, name=name, grid=(total,),
        out_shape=[jax.ShapeDtypeStruct(sh, F32) for sh in shapes for _ in range(3)],
        in_specs=[spec(i) for i in range(n) for _ in range(4)],
        out_specs=[spec(i) for i in range(n) for _ in range(3)],
        compiler_params=_params("arbitrary"),
    )(*[a for p in params for a in p])
    return [tuple(outs[3 * i:3 * i + 3]) for i in range(n)]


def adamw_small(params, total, place):
    n = len(params)

    def cut_gradients(total_ref, q, g_refs):
        g_norm, g_group_b, g_scale, g_gk_w, g_gk_b, g_head_norm, g_final = g_refs
        g_norm[0:1, :] = total_ref[0:1, :]
        g_norm[1:2, :] = total_ref[3:4, :]
        g_scale[...] = total_ref[1:2, :]
        g_final[...] = total_ref[5:6, :]
        g_gk_b[...] = total_ref[4:5, pl.ds(pl.multiple_of(q * 128, 128), 128)]
        for r in range(GATE_RANK):
            lanes = pl.ds(pl.multiple_of((r % 2) * KEY_W + q * 128, 128), 128)
            g_gk_w[r:r + 1, :] = total_ref[8 + r // 2:9 + r // 2, lanes]
        for k in range(N_CHIPS):
            @pl.when(q == k)
            def _(k=k):
                g_head_norm[...] = total_ref[6:7, 64 * k:64 * (k + 1)]
                for g in range(GROUPS):
                    g_group_b[g:g + 1, :] = total_ref[2:3, GROUP_DIM * g + 64 * k:GROUP_DIM * g + 64 * (k + 1)]

    def body(place_ref, total_ref, *refs):
        ins, outs = refs[:3 * n], refs[3 * n:]
        outs[4 * n][...] = total_ref[7:8, 0:1]
        cut_gradients(total_ref, place_ref[1], outs[0:4 * n:4])
        for k in range(n):
            w_ref, m_ref, v_ref = ins[3 * k:3 * k + 3]
            d, nm, nv = _adam_math(w_ref[...], outs[4 * k][...], m_ref[...], v_ref[...])
            outs[4 * k + 1][...] = d
            outs[4 * k + 2][...] = nm
            outs[4 * k + 3][...] = nv

    flat = [a for p in params for a in p]
    outs = pl.pallas_call(
        body, name="adamw_small",
        out_shape=[jax.ShapeDtypeStruct(p[0].shape, F32) for p in params for _ in range(4)]
                  + [jax.ShapeDtypeStruct((1, 1), F32)],
        in_specs=[pl.BlockSpec(memory_space=pltpu.SMEM)] + [VMEM_SPEC] * (1 + 3 * n),
        out_specs=[VMEM_SPEC] * (4 * n + 1),
    )(place, total, *flat)
    return [tuple(outs[4 * k:4 * k + 4]) for k in range(n)], outs[4 * n]


def matmul_tn(a, b, name, tile_n, by_column_tile=False, chip_sums=()):
    s, m = a.shape
    n = b.shape[1]
    n_sums = len(chip_sums)
    steps = n // tile_n
    if by_column_tile:
        out_shape = jax.ShapeDtypeStruct((steps, m, tile_n), F32)
        out_spec = pl.BlockSpec((None, m, tile_n), lambda j: (j, 0, 0))
    else:
        out_shape = jax.ShapeDtypeStruct((m, n), F32)
        out_spec = pl.BlockSpec((m, tile_n), lambda j: (0, j))

    def body(a_ref, b_ref, *rest):
        sum_refs, out_ref, got_refs = rest[:n_sums], rest[n_sums], rest[n_sums + 1:2 * n_sums + 1]
        j = pl.program_id(0)
        copies = _scatter_copies(sum_refs, got_refs, *rest[2 * n_sums + 1:]) if n_sums else []

        @pl.when(j == 0)
        def _():
            for cp in copies:
                cp.start()

        out_ref[...] = _tn(a_ref[...], b_ref[...])

        @pl.when(j == steps - 1)
        def _():
            for cp in copies:
                cp.wait()

    outs = pl.pallas_call(
        body, name=name, grid=(steps,),
        out_shape=[out_shape] + _scatter_shapes(chip_sums),
        in_specs=[_full((s, m)), pl.BlockSpec((s, tile_n), lambda j: (0, j))] + _any_specs(n_sums),
        out_specs=[out_spec] + _any_specs(n_sums),
        scratch_shapes=[pltpu.SemaphoreType.DMA((3 * n_sums,)), pltpu.SemaphoreType.DMA((3 * n_sums,))]
                       if n_sums else [],
        compiler_params=_params("arbitrary"),
    )(a, b, *chip_sums)
    return outs[0], outs[1:]


ROW_TILE = 512


def _row_index(tile, rows):
    return tile * rows + lax.broadcasted_iota(jnp.int32, (rows, 1), 0)


def _inverse_counts(t_glob):
    return [1.0 / jnp.minimum(t_glob + 1, w).astype(F32) for w in POOL_WINDOWS]


def _sigmoid(z):
    return 1.0 / (1.0 + jnp.exp(-z))


def _trailing_sums(src, tmp, cols, window, rows):
    bufs = (src, tmp)
    span, level, start = 1, 0, 0
    while span < window:
        start += 8
        a, b = bufs[level % 2], bufs[(level + 1) % 2]
        n = HALO + rows - start
        b[start:start + n, cols] = a[start:start + n, cols] + a[start - span:start - span + n, cols]
        span, level = 2 * span, level + 1
    return bufs[level % 2][HALO:HALO + rows, cols]


def _leading_sums(src, tmp, cols, window, rows):
    bufs = (src, tmp)
    span, level, n = 1, 0, rows + HALO
    while span < window:
        n -= 8
        a, b = bufs[level % 2], bufs[(level + 1) % 2]
        b[0:n, cols] = a[0:n, cols] + a[span:span + n, cols]
        span, level = 2 * span, level + 1
    return bufs[level % 2][0:rows, cols]


def gather_in_background(step, last, out_refs, send_sems, recv_sems, finish):
    n = len(out_refs)
    x, y, c = _position()
    q = 2 * x + y
    chips = _other_chips(x, y)

    def copy(k, i, quarter, half, to):
        return _gather_copy(out_refs[i], send_sems, recv_sems, k * n + i, quarter, half, to)

    if not finish:
        @pl.when(step == 0)
        def _():
            for i in range(n):
                mine, _ = _halves(out_refs[i].shape[1], c)
                for j, chip in enumerate(chips):
                    copy(j, i, q, mine, (*chip, c)).start()

        @pl.when(step == last)
        def _():
            for j, chip in enumerate(chips):
                qj = 2 * chip[0] + chip[1]
                for i in range(n):
                    mine, _ = _halves(out_refs[i].shape[1], c)
                    copy(j, i, qj, mine, (x, y, c)).wait_recv()
                    copy(3 + j, i, qj, mine, (x, y, 1 - c)).start()
        return

    @pl.when(step == last)
    def _():
        for j, chip in enumerate(chips):
            qj = 2 * chip[0] + chip[1]
            for i in range(n):
                mine, other = _halves(out_refs[i].shape[1], c)
                copy(3 + j, i, qj, other, (x, y, c)).wait_recv()
                copy(j, i, q, mine, (x, y, c)).wait_send()
                copy(3 + j, i, qj, mine, (x, y, c)).wait_send()


def _group_matrix(gw_ref, g):
    rows = GROUP_DIM // N_CHIPS
    return jnp.concatenate([gw_ref[j, rows * g:rows * (g + 1), :] for j in range(N_CHIPS)], axis=0)


def pool_forward(x, w0, wpi, gw, gb, scale, wpo, later):
    s = x.shape[0]
    ts = ROW_TILE
    nt = s // ts
    assert nt >= 2
    n_later = len(later)

    def body(x_ref, w0_ref, wpi_ref, gw_ref, gb_ref, sc_ref, wpo_ref, *rest):
        rest = rest[n_later:]
        h1_ref, pooled_ref, gt_ref, n0_ref = rest[:4]
        later_refs = rest[4:4 + n_later]
        ubuf, tbuf, hist, send_sems, recv_sems = rest[4 + n_later:]
        i = pl.program_id(0)
        gather_in_background(i, nt - 1, later_refs, send_sems, recv_sems, finish=False)
        xv = x_ref[...]
        r = lax.rsqrt(jnp.mean(xv * xv, axis=-1, keepdims=True) + EPS)
        n0 = _bf(xv * r * w0_ref[...])
        n0_ref[...] = n0
        u = jnp.concatenate([_nn(n0, wpi_ref[0]), _nn(n0, wpi_ref[1])], axis=-1)
        gt = jnp.concatenate([_nn(n0, wpi_ref[2]), _nn(n0, wpi_ref[3])], axis=-1)
        gt_ref[...] = gt

        @pl.when(i == 0)
        def _():
            hist[...] = jnp.zeros_like(hist)

        ubuf[0:HALO, :] = hist[...]
        ubuf[HALO:HALO + ts, :] = u
        hist[...] = u[ts - HALO:, :]
        inv = _inverse_counts(_row_index(i, ts))
        mixed = []
        for g, w in enumerate(POOL_WINDOWS):
            cols = slice(g * GROUP_DIM, (g + 1) * GROUP_DIM)
            pooled = _bf(_trailing_sums(ubuf, tbuf, cols, w, ts) * inv[g] - u[:, cols])
            pooled_ref[:, cols] = pooled
            mixed.append(_nn(pooled, _group_matrix(gw_ref, g)))
        mixed = jnp.concatenate(mixed, axis=-1) + gb_ref[...]
        y = mixed * sc_ref[...] * (gt * _sigmoid(gt))
        h1_ref[...] = xv + _nn(_bf(y), wpo_ref[...])
        gather_in_background(i, nt - 1, later_refs, send_sems, recv_sems, finish=True)

    row = lambda cols: pl.BlockSpec((ts, cols), lambda i: (i, 0))
    outs = pl.pallas_call(
        body, name="pool_forward", grid=(nt,),
        out_shape=[jax.ShapeDtypeStruct((s, D), F32), jax.ShapeDtypeStruct((s, D), BF16),
                   jax.ShapeDtypeStruct((s, D), F32), jax.ShapeDtypeStruct((s, D), BF16)]
                  + [jax.ShapeDtypeStruct(a.shape, a.dtype) for a in later],
        in_specs=[row(D), _full((1, D)), _full((N_CHIPS, D, D // 2)), _full((GROUPS, GROUP_DIM, GROUP_DIM)),
                  _full((1, D)), _full((1, D)), _full((D, D))] + _any_specs(n_later),
        out_specs=[row(D), row(D), row(D), row(D)] + _any_specs(n_later),
        input_output_aliases={7 + k: 4 + k for k in range(n_later)},
        scratch_shapes=[pltpu.VMEM((HALO + ts, D), F32), pltpu.VMEM((HALO + ts, D), F32),
                        pltpu.VMEM((HALO, D), F32),
                        pltpu.SemaphoreType.DMA((6 * n_later,)), pltpu.SemaphoreType.DMA((6 * n_later,))],
        compiler_params=_params("arbitrary"),
    )(x, w0, wpi, gw, gb, scale, wpo, *later)
    return outs[:4], outs[4:]


def pool_backward(x, dh1, pooled, gt, w0, wpi, gw, gb, scale, wpo, chip_sums):
    s = x.shape[0]
    ts = ROW_TILE
    nt = s // ts
    n_sums = len(chip_sums)

    def body(x_ref, dh1_ref, pooled_ref, gt_ref, w0_ref, wpi_ref, gw_ref, gb_ref, sc_ref, wpo_ref, *rest):
        sum_refs, rest = rest[:n_sums], rest[n_sums:]
        dx_ref, dproj_ref, gpo_ref, ggw_ref, small_ref = rest[:5]
        got_refs = rest[5:5 + n_sums]
        ebuf, tbuf, ahead, send_sems, recv_sems = rest[5 + n_sums:]
        i = pl.program_id(0)
        copies = _scatter_copies(sum_refs, got_refs, send_sems, recv_sems)

        @pl.when(i == 0)
        def _():
            for cp in copies:
                cp.start()

        @pl.when(i == 0)
        def _():
            gpo_ref[...] = jnp.zeros_like(gpo_ref)
            ggw_ref[...] = jnp.zeros_like(ggw_ref)
            small_ref[...] = jnp.zeros_like(small_ref)
            ahead[...] = jnp.zeros_like(ahead)

        dh1 = dh1_ref[...]
        dh1_bf = _bf(dh1)
        gt = gt_ref[...]
        sc = sc_ref[...]
        dy = _nt(dh1_bf, wpo_ref[...])
        pooled_bf = []
        mixed = []
        for g in range(GROUPS):
            cols = slice(g * GROUP_DIM, (g + 1) * GROUP_DIM)
            pb = pooled_ref[:, cols]
            pooled_bf.append(pb)
            mixed.append(_nn(pb, _group_matrix(gw_ref, g)))
        mixed = jnp.concatenate(mixed, axis=-1) + gb_ref[...]
        sg = _sigmoid(gt)
        silu = gt * sg
        gpo_ref[...] += _tn(_bf(mixed * sc * silu), dh1_bf)
        dmixed = dy * sc * silu
        dgt = dy * mixed * sc * (sg * (1.0 + gt * (1.0 - sg)))
        dproj_ref[:, D:] = _bf(dgt)
        small_ref[1:2, :] += jnp.sum(dy * mixed * silu, axis=0, keepdims=True)
        small_ref[2:3, :] += jnp.sum(dmixed, axis=0, keepdims=True)

        inv = _inverse_counts(_row_index(nt - 1 - i, ts))
        rows_q = GROUP_DIM // N_CHIPS
        ebuf[ts:ts + HALO, :] = ahead[...]
        dpooled = []
        for g in range(GROUPS):
            cols = slice(g * GROUP_DIM, (g + 1) * GROUP_DIM)
            dm = _bf(dmixed[:, cols])
            ggw = _tn(pooled_bf[g], dm)
            for j in range(N_CHIPS):
                ggw_ref[j, rows_q * g:rows_q * (g + 1), :] += ggw[rows_q * j:rows_q * (j + 1), :]
            dp = _nt(dm, _group_matrix(gw_ref, g))
            dpooled.append(dp)
            ebuf[0:ts, cols] = dp * inv[g]
        ahead[...] = ebuf[0:HALO, :]
        du = []
        for g, w in enumerate(POOL_WINDOWS):
            cols = slice(g * GROUP_DIM, (g + 1) * GROUP_DIM)
            du.append(_leading_sums(ebuf, tbuf, cols, w, ts) - dpooled[g])
        du = _bf(jnp.concatenate(du, axis=-1))
        dproj_ref[:, :D] = du
        dgt_bf = _bf(dgt)
        half = D // 2
        dn0 = (_nt(du[:, :half], wpi_ref[0]) + _nt(du[:, half:], wpi_ref[1])
               + _nt(dgt_bf[:, :half], wpi_ref[2]) + _nt(dgt_bf[:, half:], wpi_ref[3]))

        xv = x_ref[...]
        r = lax.rsqrt(jnp.mean(xv * xv, axis=-1, keepdims=True) + EPS)
        xhat = xv * r
        small_ref[0:1, :] += jnp.sum(dn0 * xhat, axis=0, keepdims=True)
        dxh = dn0 * w0_ref[...]
        dx_ref[...] = dh1 + r * (dxh - xhat * jnp.mean(dxh * xhat, axis=-1, keepdims=True))

        @pl.when(i == nt - 1)
        def _():
            for cp in copies:
                cp.wait()

    row = lambda cols: pl.BlockSpec((ts, cols), lambda i: (nt - 1 - i, 0))
    outs = pl.pallas_call(
        body, name="pool_backward", grid=(nt,),
        out_shape=[jax.ShapeDtypeStruct((s, D), F32), jax.ShapeDtypeStruct((s, 2 * D), BF16),
                   jax.ShapeDtypeStruct((D, D), F32),
                   jax.ShapeDtypeStruct((GROUPS, GROUP_DIM, GROUP_DIM), F32),
                   jax.ShapeDtypeStruct((8, D), F32)] + _scatter_shapes(chip_sums),
        in_specs=[row(D), row(D), row(D), row(D), _full((1, D)), _full((N_CHIPS, D, D // 2)),
                  _full((GROUPS, GROUP_DIM, GROUP_DIM)), _full((1, D)), _full((1, D)), _full((D, D))]
                 + _any_specs(n_sums),
        out_specs=[row(D), row(2 * D), _full((D, D)), _full((GROUPS, GROUP_DIM, GROUP_DIM)), _full((8, D))]
                  + _any_specs(n_sums),
        scratch_shapes=[pltpu.VMEM((ts + HALO, D), F32), pltpu.VMEM((ts + HALO, D), F32),
                        pltpu.VMEM((HALO, D), F32),
                        pltpu.SemaphoreType.DMA((3 * n_sums,)), pltpu.SemaphoreType.DMA((3 * n_sums,))],
        compiler_params=_params("arbitrary"),
    )(x, dh1, pooled, gt, w0, wpi, gw, gb, scale, wpo, *chip_sums)
    return outs[:5], outs[5:]


def gla_project(h1, w1, wgi_q, wgk, bgk, later):
    s = h1.shape[0]
    ts = ROW_TILE
    nt = s // ts
    assert nt >= 2
    n_later = len(later)

    def body(h_ref, w1_ref, wq_ref, wgk_ref, bgk_ref, *rest):
        rest = rest[n_later:]
        qk_ref, v_ref, gate_ref, low_ref, cum_ref, n1_ref = rest[:6]
        later_refs = rest[6:6 + n_later]
        send_sems, recv_sems, wgi_ref = rest[6 + n_later:]
        gather_in_background(pl.program_id(0), nt - 1, later_refs, send_sems, recv_sems, finish=False)

        @pl.when(pl.program_id(0) == 0)
        def _():
            _assemble_gla_in(wq_ref, wgi_ref)

        hv = h_ref[...]
        r = lax.rsqrt(jnp.mean(hv * hv, axis=-1, keepdims=True) + EPS)
        n1 = _bf(hv * r * w1_ref[...])
        n1_ref[...] = n1
        qk_ref[...] = _nn(n1, wgi_ref[:, 0:2 * KEY_W])
        v_ref[...] = _bf(_nn(n1, wgi_ref[:, 2 * KEY_W:2 * KEY_W + D]))
        gate_ref[...] = _nn(n1, wgi_ref[:, 2 * KEY_W + D:GLA_MAIN])
        low = _bf(_nn(n1, wgi_ref[:, GLA_MAIN:]))
        low_ref[...] = low
        z = _nn(low, wgk_ref[...]) + bgk_ref[...]
        lg = (jnp.minimum(z, 0.0) - jnp.log(1.0 + jnp.exp(-jnp.abs(z)))) / GATE_NORM
        lower_f = _chunk_masks()[0].astype(F32)
        for r0 in range(0, ts, CHUNK):
            cum_ref[r0:r0 + CHUNK, :] = _nn_exact(lower_f, lg[r0:r0 + CHUNK, :])
        gather_in_background(pl.program_id(0), nt - 1, later_refs, send_sems, recv_sems, finish=True)

    row = lambda cols: pl.BlockSpec((ts, cols), lambda i: (i, 0))
    outs = pl.pallas_call(
        body, name="gla_project", grid=(nt,),
        out_shape=[jax.ShapeDtypeStruct((s, D), F32), jax.ShapeDtypeStruct((s, D), BF16),
                   jax.ShapeDtypeStruct((s, D), F32), jax.ShapeDtypeStruct((s, RANK_PAD), BF16),
                   jax.ShapeDtypeStruct((s, KEY_W), F32), jax.ShapeDtypeStruct((s, D), BF16)]
                  + [jax.ShapeDtypeStruct(a.shape, a.dtype) for a in later],
        in_specs=[row(D), _full((1, D)), _full((N_CHIPS, D, GLA_IN_QUARTER)),
                  _full((RANK_PAD, KEY_W)), _full((1, KEY_W))] + _any_specs(n_later),
        out_specs=[row(D), row(D), row(D), row(RANK_PAD), row(KEY_W), row(D)] + _any_specs(n_later),
        input_output_aliases={5 + k: 6 + k for k in range(n_later)},
        scratch_shapes=[pltpu.SemaphoreType.DMA((6 * n_later,)), pltpu.SemaphoreType.DMA((6 * n_later,)),
                        pltpu.VMEM((D, GLA_MAIN + RANK_PAD), BF16)],
        compiler_params=_params("arbitrary"),
    )(h1, w1, wgi_q, wgk, bgk, *later)
    return outs[:6], outs[6:]


def _assemble_gla_in(wq_ref, wfull):
    pad = jnp.zeros((CAST_ROWS, GLA_MAIN + RANK_PAD - GLA_IN), BF16)
    for r0 in range(0, D, CAST_ROWS):
        rows = slice(r0, r0 + CAST_ROWS)
        wfull[rows, :] = jnp.concatenate([wq_ref[q, rows, :] for q in range(N_CHIPS)] + [pad], axis=1)


GLA_BLOCK = 1024
CHUNKS_PER_BLOCK = GLA_BLOCK // CHUNK


def _chunk_masks():
    t = lax.broadcasted_iota(jnp.int32, (CHUNK, CHUNK), 0)
    u = lax.broadcasted_iota(jnp.int32, (CHUNK, CHUNK), 1)
    return t >= u, t <= u


def _gla_chunk_terms(q, cum):
    ep = jnp.exp(cum)
    en = jnp.exp(-cum)
    qs = q * (HEAD_K ** -0.5)
    last = cum[CHUNK - 1:CHUNK, :]
    ed = jnp.exp(last - cum)
    dec = jnp.exp(last)
    return ep, en, qs, ed, dec


def gla_forward(qk, v, cum):
    s = qk.shape[0]
    nb = s // GLA_BLOCK
    nc = s // CHUNK

    def body(q_ref, k_ref, v_ref, cum_ref, o_ref, st_ref, sc_ref, state):
        @pl.when(pl.program_id(0) == 0)
        def _():
            state[...] = jnp.zeros_like(state)

        lower, _ = _chunk_masks()

        def chunk(cc, carry):
            rows = pl.ds(pl.multiple_of(cc * CHUNK, CHUNK), CHUNK)
            for h in range(HEADS):
                kc = slice(h * HEAD_K, (h + 1) * HEAD_K)
                vc = slice(h * HEAD_V, (h + 1) * HEAD_V)
                q = q_ref[rows, kc]
                k = k_ref[rows, kc]
                v = v_ref[rows, vc]
                ep, en, qs, ed, dec = _gla_chunk_terms(q, cum_ref[rows, kc])
                a = _bf(qs * ep)
                fwd = _nt(a, _bf(k * en))
                bwd = _nt(_bf(qs * en), _bf(k * ep))
                scores = _bf(jnp.where(lower, fwd, bwd))
                sc_ref[rows, h * CHUNK:(h + 1) * CHUNK] = scores
                st = state[h]
                st_ref[cc, h] = st
                o_ref[rows, vc] = _nn(scores, v) + _nt(a, _bf(st))
                state[h] = st * dec + _tn(v, _bf(k * ed))
            return carry

        lax.fori_loop(0, CHUNKS_PER_BLOCK, chunk, 0, unroll=True)

    return pl.pallas_call(
        body, name="gla_forward", grid=(nb,),
        out_shape=(jax.ShapeDtypeStruct((s, D), F32),
                   jax.ShapeDtypeStruct((nc, HEADS, HEAD_V, HEAD_K), F32),
                   jax.ShapeDtypeStruct((s, HEADS * CHUNK), BF16)),
        in_specs=[pl.BlockSpec((GLA_BLOCK, KEY_W), lambda i: (i, 0)),
                  pl.BlockSpec((GLA_BLOCK, KEY_W), lambda i: (i, 1)),
                  pl.BlockSpec((GLA_BLOCK, D), lambda i: (i, 0)),
                  pl.BlockSpec((GLA_BLOCK, KEY_W), lambda i: (i, 0))],
        out_specs=(pl.BlockSpec((GLA_BLOCK, D), lambda i: (i, 0)),
                   pl.BlockSpec((CHUNKS_PER_BLOCK, HEADS, HEAD_V, HEAD_K), lambda i: (i, 0, 0, 0)),
                   pl.BlockSpec((GLA_BLOCK, HEADS * CHUNK), lambda i: (i, 0))),
        scratch_shapes=[pltpu.VMEM((HEADS, HEAD_V, HEAD_K), F32)],
        compiler_params=_params("arbitrary"),
    )(qk, qk, v, cum)


def gla_backward(qk, v, cum, do, states, scores):
    s = qk.shape[0]
    nb = s // GLA_BLOCK

    def body(q_ref, k_ref, v_ref, cum_ref, do_ref, st_ref, sc_ref, dq_ref, dk_ref, dv_ref, dcum_ref, dstate):
        @pl.when(pl.program_id(0) == 0)
        def _():
            dstate[...] = jnp.zeros_like(dstate)

        lower, _ = _chunk_masks()
        is_last = lax.broadcasted_iota(jnp.int32, (CHUNK, HEAD_K), 0) == CHUNK - 1

        def chunk(step, carry):
            cc = CHUNKS_PER_BLOCK - 1 - step
            rows = pl.ds(pl.multiple_of(cc * CHUNK, CHUNK), CHUNK)
            for h in range(HEADS):
                kc = slice(h * HEAD_K, (h + 1) * HEAD_K)
                vc = slice(h * HEAD_V, (h + 1) * HEAD_V)
                q = q_ref[rows, kc]
                k = k_ref[rows, kc]
                v = v_ref[rows, vc]
                do_c = do_ref[rows, vc]
                ep, en, qs, ed, dec = _gla_chunk_terms(q, cum_ref[rows, kc])
                a = _bf(qs * ep)
                b = _bf(k * en)
                c = _bf(qs * en)
                dk_dec = _bf(k * ep)
                kd = _bf(k * ed)
                scores = sc_ref[rows, h * CHUNK:(h + 1) * CHUNK]
                st = st_ref[cc, h]
                dst = dstate[h]
                dst_bf = _bf(dst)

                dscores = _nt(do_c, v)
                dfwd = _bf(jnp.where(lower, dscores, 0.0))
                dbwd = _bf(jnp.where(lower, 0.0, dscores))
                dv_ref[rows, vc] = _bf(_tn(scores, do_c) + _nt(kd, dst_bf))
                da = _nn(dfwd, b) + _nn(do_c, _bf(st))
                db = _tn(dfwd, a)
                dc = _nn(dbwd, dk_dec)
                ddk = _tn(dbwd, c)
                dkd = _nn(v, dst_bf)
                ddec = jnp.sum(dst * st, axis=0, keepdims=True)
                dstate[h] = dst * dec + _tn(do_c, a)

                m = dkd * k * ed
                dq_ref[rows, kc] = _bf((da * ep + dc * en) * (HEAD_K ** -0.5))
                dk_ref[rows, kc] = _bf(db * en + ddk * ep + dkd * ed)
                dcum = (da * qs + ddk * k) * ep - (db * k + dc * qs) * en - m
                dlast = jnp.sum(m, axis=0, keepdims=True) + ddec * dec
                dcum_ref[rows, kc] = dcum + jnp.where(is_last, dlast, 0.0)
            return carry

        lax.fori_loop(0, CHUNKS_PER_BLOCK, chunk, 0, unroll=True)

    rev = lambda cols, col_block: pl.BlockSpec((GLA_BLOCK, cols), lambda i: (nb - 1 - i, col_block))
    return pl.pallas_call(
        body, name="gla_backward", grid=(nb,),
        out_shape=(jax.ShapeDtypeStruct((s, KEY_W), BF16), jax.ShapeDtypeStruct((s, KEY_W), BF16),
                   jax.ShapeDtypeStruct((s, D), BF16), jax.ShapeDtypeStruct((s, KEY_W), F32)),
        in_specs=[rev(KEY_W, 0), rev(KEY_W, 1), rev(D, 0), rev(KEY_W, 0), rev(D, 0),
                  pl.BlockSpec((CHUNKS_PER_BLOCK, HEADS, HEAD_V, HEAD_K), lambda i: (nb - 1 - i, 0, 0, 0)),
                  rev(HEADS * CHUNK, 0)],
        out_specs=(rev(KEY_W, 0), rev(KEY_W, 0), rev(D, 0), rev(KEY_W, 0)),
        scratch_shapes=[pltpu.VMEM((HEADS, HEAD_V, HEAD_K), F32)],
        compiler_params=_params("arbitrary"),
    )(qk, qk, v, cum, do, states, scores)


def head_and_loss(o, gate, h1, target, hw, wgo, wf):
    s = o.shape[0]
    ts = ROW_TILE

    def body(o_ref, gate_ref, h1_ref, tgt_ref, hw_ref, wgo_ref, wf_ref,
             dh2_ref, do_ref, dgate_ref, ggo_ref, small_ref):
        @pl.when(pl.program_id(0) == 0)
        def _():
            ggo_ref[...] = jnp.zeros_like(ggo_ref)
            small_ref[...] = jnp.zeros_like(small_ref)

        gate = gate_ref[...]
        hw = hw_ref[...]
        sg = _sigmoid(gate)
        silu = gate * sg
        ohat, ro = [], []
        for h in range(HEADS):
            oh = o_ref[:, h * HEAD_V:(h + 1) * HEAD_V]
            rh = lax.rsqrt(jnp.mean(oh * oh, axis=-1, keepdims=True) + EPS)
            ro.append(rh)
            ohat.append(oh * rh)
        ohat = jnp.concatenate(ohat, axis=-1)
        on = ohat * hw
        y2 = _bf(on * silu)
        h2 = h1_ref[...] + _nn(y2, wgo_ref[...])
        rf = lax.rsqrt(jnp.mean(h2 * h2, axis=-1, keepdims=True) + EPS)
        h2hat = h2 * rf
        wf = wf_ref[...]
        diff = h2hat * wf - tgt_ref[...]
        small_ref[2:3, :] += jnp.zeros((1, D), F32) + 0.5 * jnp.sum(diff * diff) / D
        dout = diff / D
        small_ref[0:1, :] += jnp.sum(dout * h2hat, axis=0, keepdims=True)
        dxh = dout * wf
        dh2 = rf * (dxh - h2hat * jnp.mean(dxh * h2hat, axis=-1, keepdims=True))
        dh2_ref[...] = dh2
        dh2_bf = _bf(dh2)
        ggo_ref[...] += _tn(y2, dh2_bf)
        dy2 = _nt(dh2_bf, wgo_ref[...])
        don = dy2 * silu
        dgate_ref[...] = _bf(dy2 * on * (sg * (1.0 + gate * (1.0 - sg))))
        ghw = jnp.sum(don * ohat, axis=0, keepdims=True)
        small_ref[1:2, 0:HEAD_V] += sum(ghw[:, h * HEAD_V:(h + 1) * HEAD_V] for h in range(HEADS))
        dohat = don * hw
        for h in range(HEADS):
            cols = slice(h * HEAD_V, (h + 1) * HEAD_V)
            oh, dh = ohat[:, cols], dohat[:, cols]
            do_ref[:, cols] = _bf(ro[h] * (dh - oh * jnp.mean(dh * oh, axis=-1, keepdims=True)))

    row = lambda cols: pl.BlockSpec((ts, cols), lambda i: (i, 0))
    act = jax.ShapeDtypeStruct((s, D), F32)
    act_bf = jax.ShapeDtypeStruct((s, D), BF16)
    return pl.pallas_call(
        body, name="head_and_loss", grid=(s // ts,),
        out_shape=(act, act_bf, act_bf, jax.ShapeDtypeStruct((D, D), F32), jax.ShapeDtypeStruct((8, D), F32)),
        in_specs=[row(D), row(D), row(D), row(D),
                  _full((1, D)), _full((D, D)), _full((1, D))],
        out_specs=(row(D), row(D), row(D), _full((D, D)), _full((8, D))),
        compiler_params=_params("arbitrary"),
    )(o, gate, h1, target, hw, wgo, wf)


def gla_project_backward(dq, dk, dv, dgate, dcum, low, h1, dh2, w1, wgi_q, wgk, bgk):
    s = h1.shape[0]
    ts = ROW_TILE

    def body(dq_ref, dk_ref, dv_ref, dgate_ref, dcum_ref, low_ref, h1_ref, dh2_ref, w1_ref,
             wq_ref, wgk_ref, bgk_ref, dh1_ref, dproj_ref, ggk_ref, small_ref, wgi_ref):
        @pl.when(pl.program_id(0) == 0)
        def _():
            ggk_ref[...] = jnp.zeros_like(ggk_ref)
            small_ref[...] = jnp.zeros_like(small_ref)
            _assemble_gla_in(wq_ref, wgi_ref)

        low = low_ref[...]
        z = _nn(low, wgk_ref[...]) + bgk_ref[...]
        upper_f = _chunk_masks()[1].astype(F32)
        dlg = jnp.concatenate([_nn_exact(upper_f, dcum_ref[r0:r0 + CHUNK, :]) for r0 in range(0, ts, CHUNK)],
                              axis=0)
        dz = dlg * (1.0 / GATE_NORM) * _sigmoid(-z)
        dz_bf = _bf(dz)
        ggk_ref[...] += _tn(low, dz_bf)
        small_ref[1:2, 0:KEY_W] += jnp.sum(dz, axis=0, keepdims=True)
        dlow = _bf(_nt(dz_bf, wgk_ref[...]))
        dproj_ref[:, GLA_MAIN:] = dlow
        dn1 = _nt(dlow, wgi_ref[:, GLA_MAIN:])
        for ref, lo, hi in ((dq_ref, 0, KEY_W), (dk_ref, KEY_W, 2 * KEY_W),
                            (dv_ref, 2 * KEY_W, 2 * KEY_W + D), (dgate_ref, 2 * KEY_W + D, GLA_MAIN)):
            piece = ref[...]
            dproj_ref[:, lo:hi] = piece
            dn1 = dn1 + _nt(piece, wgi_ref[:, lo:hi])
        hv = h1_ref[...]
        r = lax.rsqrt(jnp.mean(hv * hv, axis=-1, keepdims=True) + EPS)
        hhat = hv * r
        small_ref[0:1, :] += jnp.sum(dn1 * hhat, axis=0, keepdims=True)
        dxh = dn1 * w1_ref[...]
        dh1_ref[...] = dh2_ref[...] + r * (dxh - hhat * jnp.mean(dxh * hhat, axis=-1, keepdims=True))

    row = lambda cols: pl.BlockSpec((ts, cols), lambda i: (i, 0))
    return pl.pallas_call(
        body, name="gla_project_backward", grid=(s // ts,),
        out_shape=(jax.ShapeDtypeStruct((s, D), F32), jax.ShapeDtypeStruct((s, GLA_MAIN + RANK_PAD), BF16),
                   jax.ShapeDtypeStruct((RANK_PAD, KEY_W), F32),
                   jax.ShapeDtypeStruct((8, D), F32)),
        in_specs=[row(KEY_W), row(KEY_W), row(D), row(D), row(KEY_W), row(RANK_PAD), row(D), row(D),
                  _full((1, D)), _full((N_CHIPS, D, GLA_IN_QUARTER)), _full((RANK_PAD, KEY_W)),
                  _full((1, KEY_W))],
        out_specs=(row(D), row(GLA_MAIN + RANK_PAD), _full((RANK_PAD, KEY_W)), _full((8, D))),
        scratch_shapes=[pltpu.VMEM((D, GLA_MAIN + RANK_PAD), BF16)],
        compiler_params=_params("arbitrary"),
    )(dq, dk, dv, dgate, dcum, low, h1, dh2, w1, wgi_q, wgk, bgk)


def local_gradients(xs, target, w0, w1, wf, wpi, gw, gb, scale, wpo, gla_quarters, wgk, bgk, hw_tiled, place):
    wgi_q, wgo_q = gla_quarters
    (h1, pooled, gt, n0), (wgi_q,) = pool_forward(xs, w0, wpi, gw, gb, scale, wpo, [wgi_q])
    (qk, v, gate, low, cum, n1), (wgo_q,) = gla_project(h1, w1, wgi_q, wgk, bgk, [wgo_q])
    wgo = wgo_q.reshape(D, D)
    o, states, scores = gla_forward(qk, v, cum)

    dh2, do, dgate, g_gla_out, small_top = head_and_loss(o, gate, h1, target, hw_tiled, wgo, wf)
    dq, dk, dv, dcum = gla_backward(qk, v, cum, do, states, scores)
    dh1, dproj, g_gk_pad, small_gla = gla_project_backward(
        dq, dk, dv, dgate, dcum, low, h1, dh2, w1, wgi_q, wgk, bgk)
    g_gla_in, _ = matmul_tn(n1, dproj, "grad_gla_in", tile_n=(GLA_MAIN + RANK_PAD) // 5)

    def chip_sums(grads, tag):
        return add_halves(grads, place, "add_halves_" + tag)

    gla_sums = chip_sums([g_gla_in, g_gla_out.reshape(N_CHIPS, D // N_CHIPS, D)], "gla")
    (dx, dpool, g_pool_out, g_group_w, small_pool), gla_got = pool_backward(
        xs, dh1, pooled, gt, w0, wpi, gw, gb, scale, wpo, [b for _, b in gla_sums])
    mix_sums = chip_sums([g_group_w, g_pool_out.reshape(N_CHIPS, D // N_CHIPS, D)], "pool_mix")
    g_pool_in, mix_got = matmul_tn(n0, dpool, "grad_pool_in", tile_n=D // 2, by_column_tile=True,
                                   chip_sums=[b for _, b in mix_sums])

    in_sums = add_halves([g_pool_in], place, "add_halves_and_scatter_pool_in", scatter=True)
    reduced, total = join_halves(
        [f for f, _ in in_sums + mix_sums + gla_sums], [got for _, got in in_sums] + list(mix_got) + list(gla_got),
        small_pool, small_gla, small_top, g_gk_pad)
    return dx, reduced, total


def kernel(x, norm_w, pool_in_w, pool_group_w, pool_group_b, pool_scale, pool_out_w, gla_in_w, gla_gk_w, gla_gk_b, gla_head_norm_w, gla_out_w, final_norm_w, loss_target, m_norm_w, m_pool_in_w, m_pool_group_w, m_pool_group_b, m_pool_scale, m_pool_out_w, m_gla_in_w, m_gla_gk_w, m_gla_gk_b, m_gla_head_norm_w, m_gla_out_w, m_final_norm_w, v_norm_w, v_pool_in_w, v_pool_group_w, v_pool_group_b, v_pool_scale, v_pool_out_w, v_gla_in_w, v_gla_gk_w, v_gla_gk_b, v_gla_head_norm_w, v_gla_out_w, v_final_norm_w):
    xs = x[0]
    target = loss_target[0]
    q_chip = 2 * lax.axis_index("x") + lax.axis_index("y")
    place = jnp.stack([lax.axis_index("c"), q_chip]).astype(jnp.int32)

    (wpi, gw_q, wpo_q, wgi_q, wgo_q), (bgk, hw_tiled, gb, wgk) = allgather_weights(
        [pool_in_w[0], pool_group_w[0].reshape(GROUP_DIM, GROUP_DIM), pool_out_w[0], gla_in_w[0], gla_out_w[0]],
        exchange=(True, True, True, False, False),
        smalls=[gla_gk_b, gla_head_norm_w, pool_group_b[0], gla_gk_w[0]])
    wpo = wpo_q.reshape(D, D)

    w0 = norm_w[0:1]
    w1 = norm_w[1:2]
    wf = final_norm_w.reshape(1, D)

    dx, reduced, total = local_gradients(
        xs, target, w0, w1, wf, wpi, gw_q, gb, pool_scale, wpo, [wgi_q, wgo_q], wgk, bgk, hw_tiled, place)
    r_pool_in, r_group_w, r_pool_out, r_gla_in, r_gla_out = reduced
    r_group_w = r_group_w.reshape(GROUPS, 64, GROUP_DIM)

    turn = lambda a: jnp.transpose(a, (2, 0, 1))
    back = lambda a: jnp.transpose(a, (1, 2, 0))
    as2d = lambda a, w: a.reshape(-1, w.shape[-1])
    big_names = ("pool_in_w", "pool_group_w", "pool_out_w", "gla_in_w", "gla_out_w")
    big_args = [(pool_in_w, r_pool_in[None], m_pool_in_w, v_pool_in_w),
                (pool_group_w, r_group_w[None], m_pool_group_w, v_pool_group_w),
                (pool_out_w, r_pool_out[None], m_pool_out_w, v_pool_out_w),
                (gla_in_w, r_gla_in[None], m_gla_in_w, v_gla_in_w),
                (gla_out_w, r_gla_out[None], m_gla_out_w, v_gla_out_w)]
    to_kernel = lambda n, a, w: turn(a) if n == "gla_in_w" else as2d(a, w)
    from_kernel = lambda n, a, w: back(a) if n == "gla_in_w" else a.reshape(w.shape)
    big_in = [tuple(to_kernel(n, a, p[0]) for a in p) for n, p in zip(big_names, big_args)]
    big_out = adamw(big_in, "adamw")
    big = {n: (from_kernel(n, i[1], p[0]),) + tuple(from_kernel(n, o, p[0]) for o in out)
           for n, p, i, out in zip(big_names, big_args, big_in, big_out)}

    small_names = ("norm_w", "pool_group_b", "pool_scale", "gla_gk_w", "gla_gk_b", "gla_head_norm_w",
                   "final_norm_w")
    small_args = [(norm_w, m_norm_w, v_norm_w),
                  (pool_group_b, m_pool_group_b, v_pool_group_b),
                  (pool_scale, m_pool_scale, v_pool_scale),
                  (gla_gk_w, m_gla_gk_w, v_gla_gk_w),
                  (gla_gk_b, m_gla_gk_b, v_gla_gk_b),
                  (gla_head_norm_w, m_gla_head_norm_w, v_gla_head_norm_w),
                  (final_norm_w, m_final_norm_w, v_final_norm_w)]
    small_out, loss = adamw_small([tuple(as2d(a, p[0]) for a in p) for p in small_args], total, place)
    small = {n: tuple(o.reshape(p[0].shape) for o in out) for n, p, out in zip(small_names, small_args, small_out)}
    results = [
        small["norm_w"],
        big["pool_in_w"],
        big["pool_group_w"],
        small["pool_group_b"],
        small["pool_scale"],
        big["pool_out_w"],
        big["gla_in_w"],
        small["gla_gk_w"],
        small["gla_gk_b"],
        small["gla_head_norm_w"],
        big["gla_out_w"],
        small["final_norm_w"],
    ]
    grads, deltas, new_m, new_v = zip(*results)
    return (loss.reshape(()), dx[None], *grads, *deltas, *new_m, *new_v)
```

```python
import jax
import jax.numpy as jnp
from jax import lax
from jax.experimental import pallas as pl
from jax.experimental.pallas import tpu as pltpu

F32 = jnp.float32
BF16 = jnp.bfloat16
MESH = pl.DeviceIdType.MESH

D = 1024
POOL_WINDOWS = (2, 4, 8, 16)
GROUPS = 4
GROUP_DIM = 256
HEADS = 4
HEAD_K = 128
HEAD_V = 256
KEY_W = 512
CHUNK = 64
GATE_RANK = 16
GATE_NORM = 16.0
GLA_IN = 3088
GLA_MAIN = 3072
RANK_PAD = 128
EPS = 1e-6
HALO = 32

ADAM_LR = 0.001
ADAM_B1 = 0.9
ADAM_B2 = 0.999
ADAM_EPS = 1e-08
ADAM_WD = 0.01
ADAM_STEP = 10

N_CHIPS = 4
N_DEV = 8
GLA_IN_QUARTER = GLA_IN // N_CHIPS

VMEM_LIMIT = 56 * 1024 * 1024


def _nn(a, b):
    return lax.dot_general(a, b, (((1,), (0,)), ((), ())), preferred_element_type=F32)


def _nt(a, b):
    return lax.dot_general(a, b, (((1,), (1,)), ((), ())), preferred_element_type=F32)


def _tn(a, b):
    return lax.dot_general(a, b, (((0,), (0,)), ((), ())), preferred_element_type=F32)


def _nn_exact(a, b):
    return lax.dot_general(a, b, (((1,), (0,)), ((), ())), preferred_element_type=F32,
                           precision=lax.Precision.HIGHEST)


def _bf(a):
    return a.astype(BF16)


def _params(*sem):
    return pltpu.CompilerParams(dimension_semantics=sem, vmem_limit_bytes=VMEM_LIMIT)


def _full(shape):
    return pl.BlockSpec(shape, lambda i: (0,) * len(shape))


def _position():
    return lax.axis_index("x"), lax.axis_index("y"), lax.axis_index("c")


def _gather_small(in_ref, all_ref, send_sems, recv_sems, local_sem):
    x, y, c = _position()
    me = 4 * x + 2 * y + c
    mine = pltpu.make_async_copy(in_ref, all_ref.at[me], local_sem)
    sends = []
    for k in range(N_DEV - 1):
        fx, fy, fc = (k + 1) >> 2 & 1, (k + 1) >> 1 & 1, (k + 1) & 1
        sends.append(pltpu.make_async_remote_copy(
            src_ref=in_ref, dst_ref=all_ref.at[me],
            send_sem=send_sems.at[k], recv_sem=recv_sems.at[k],
            device_id=(x ^ fx, y ^ fy, c ^ fc), device_id_type=MESH))

    def start():
        mine.start()
        for cp in sends:
            cp.start()

    def wait():
        for k in range(N_DEV - 1):
            fx, fy, fc = (k + 1) >> 2 & 1, (k + 1) >> 1 & 1, (k + 1) & 1
            src_dev = 4 * (x ^ fx) + 2 * (y ^ fy) + (c ^ fc)
            pltpu.make_async_remote_copy(
                src_ref=in_ref, dst_ref=all_ref.at[src_dev],
                send_sem=send_sems.at[k], recv_sem=recv_sems.at[k],
                device_id=(x, y, c), device_id_type=MESH).wait_recv()
        for cp in sends:
            cp.wait_send()
        mine.wait()

    return start, wait


SMALL_SEMS = [pltpu.SemaphoreType.DMA((N_DEV - 1,)), pltpu.SemaphoreType.DMA((N_DEV - 1,)),
              pltpu.SemaphoreType.DMA]
VMEM_SPEC = pl.BlockSpec(memory_space=pltpu.VMEM)


def _other_chips(x, y):
    return [(1 - x, y), (x, 1 - y), (1 - x, 1 - y)]


def _any_specs(n):
    return [pl.BlockSpec(memory_space=pl.ANY)] * n


def _halves(rows, c):
    half = rows // 2
    return pl.ds(c * half, half), pl.ds((1 - c) * half, half)


CAST_ROWS = 256


def _gather_copy(out_ref, send_sems, recv_sems, k, quarter, half, to, src=None):
    dst = out_ref.at[quarter, half]
    return pltpu.make_async_remote_copy(
        src_ref=dst if src is None else src, dst_ref=dst,
        send_sem=send_sems.at[k], recv_sem=recv_sems.at[k], device_id=to, device_id_type=MESH)


SMALL_IN_ROWS = 24


def allgather_weights(quarters, exchange, smalls):
    n = len(quarters)
    shapes = [w.shape for w in quarters]
    moved = [i for i in range(n) if exchange[i]]

    def body(*refs):
        w_refs, (gkb_ref, hnw_ref, gb_ref, gkw_ref) = refs[:n], refs[n:n + 4]
        out_refs, (bgk_ref, hw_ref, gbias_ref, wgk_ref) = refs[n + 4:2 * n + 4], refs[2 * n + 4:2 * n + 8]
        refs = refs[2 * n + 8:]
        f32_bufs, bf_bufs = refs[:n], refs[n:2 * n]
        send_sems, recv_sems, local_sems, small_ref, small_all_ref = refs[2 * n:2 * n + 5]
        small_ref[...] = jnp.zeros_like(small_ref)
        small_ref[0:1, :] = gkb_ref[...]
        small_ref[1:2, 0:64] = hnw_ref[...]
        small_ref[2:2 + GROUPS, 0:64] = gb_ref[...]
        small_ref[8:8 + GATE_RANK, :] = gkw_ref[...]
        start_small, wait_small = _gather_small(small_ref, small_all_ref, *refs[2 * n + 5:])
        start_small()
        x, y, c = _position()
        q = 2 * x + y
        sibling = (x, y, 1 - c)
        chips = _other_chips(x, y)

        def copy(k, i, quarter, half, to, src=None):
            return _gather_copy(out_refs[i], send_sems, recv_sems, k * n + i, quarter, half, to, src)

        loads = [pltpu.make_async_copy(w_refs[i], f32_bufs[i], local_sems.at[i]) for i in range(n)]
        for cp in loads:
            cp.start()
        keeps, sends = [], []
        for i in range(n):
            loads[i].wait()
            for r0 in range(0, shapes[i][0], CAST_ROWS):
                bf_bufs[i][r0:r0 + CAST_ROWS, :] = _bf(f32_bufs[i][r0:r0 + CAST_ROWS, :])
            keep = pltpu.make_async_copy(bf_bufs[i], out_refs[i].at[q], local_sems.at[n + i])
            keep.start()
            keeps.append(keep)
            if not exchange[i]:
                continue
            mine, _ = _halves(shapes[i][0], c)
            for j, chip in enumerate(chips):
                cp = copy(j, i, q, mine, (*chip, c), src=bf_bufs[i].at[mine])
                cp.start()
                sends.append(cp)
        for j, chip in enumerate(chips):
            qj = 2 * chip[0] + chip[1]
            for i in moved:
                mine, _ = _halves(shapes[i][0], c)
                copy(j, i, qj, mine, (x, y, c)).wait_recv()
                cp = copy(3 + j, i, qj, mine, sibling)
                cp.start()
                sends.append(cp)
        for j, chip in enumerate(chips):
            qj = 2 * chip[0] + chip[1]
            for i in moved:
                _, other = _halves(shapes[i][0], c)
                copy(3 + j, i, qj, other, (x, y, c)).wait_recv()
        wait_small()
        wgk_ref[...] = jnp.zeros_like(wgk_ref)
        for j in range(N_CHIPS):
            block = small_all_ref.at[2 * j]
            bgk_ref[:, 128 * j:128 * (j + 1)] = block[0:1, :]
            for h in range(HEADS):
                hw_ref[:, HEAD_V * h + 64 * j:HEAD_V * h + 64 * (j + 1)] = block[1:2, 0:64]
            for g in range(GROUPS):
                gbias_ref[:, GROUP_DIM * g + 64 * j:GROUP_DIM * g + 64 * (j + 1)] = block[2 + g:3 + g, 0:64]
            wgk_ref[0:GATE_RANK, 128 * j:128 * (j + 1)] = _bf(block[8:8 + GATE_RANK, :])
        for cp in sends:
            cp.wait_send()
        for cp in keeps:
            cp.wait()

    outs = pl.pallas_call(
        body, name="allgather_weights",
        out_shape=[jax.ShapeDtypeStruct((N_CHIPS, *s), BF16) for s in shapes]
                  + [jax.ShapeDtypeStruct((1, KEY_W), F32), jax.ShapeDtypeStruct((1, D), F32),
                     jax.ShapeDtypeStruct((1, D), F32), jax.ShapeDtypeStruct((RANK_PAD, KEY_W), BF16)],
        in_specs=_any_specs(n) + [VMEM_SPEC] * 4, out_specs=_any_specs(n) + [VMEM_SPEC] * 4,
        scratch_shapes=([pltpu.VMEM(s, F32) for s in shapes] + [pltpu.VMEM(s, BF16) for s in shapes]
                        + [pltpu.SemaphoreType.DMA((6 * n,)), pltpu.SemaphoreType.DMA((6 * n,)),
                           pltpu.SemaphoreType.DMA((2 * n,)), pltpu.VMEM((SMALL_IN_ROWS, 128), F32),
                           pltpu.VMEM((N_DEV, SMALL_IN_ROWS, 128), F32)] + SMALL_SEMS),
        compiler_params=pltpu.CompilerParams(vmem_limit_bytes=VMEM_LIMIT),
    )(*quarters, *smalls)
    return outs[:n], outs[n:]


def _scatter_copies(b_refs, got_refs, send_sems, recv_sems):
    n = len(b_refs)
    x, y, c = _position()
    copies = []
    for j, chip in enumerate(_other_chips(x, y)):
        qj = 2 * chip[0] + chip[1]
        for i in range(n):
            copies.append(pltpu.make_async_remote_copy(
                src_ref=b_refs[i].at[qj], dst_ref=got_refs[i].at[j],
                send_sem=send_sems.at[j * n + i], recv_sem=recv_sems.at[j * n + i],
                device_id=(*chip, c), device_id_type=MESH))
    return copies


def _scatter_shapes(chip_sums):
    return [jax.ShapeDtypeStruct((N_CHIPS - 1, *b.shape[1:]), BF16) for b in chip_sums]


ADD_ROWS = 512
ADD_HALVES_ROWS = 128


def _spans(counts):
    starts, total = [], 0
    for count in counts:
        starts.append(total)
        total += count
    return starts, total


def _local_step(t, start, count):
    return jnp.clip(t - start, 0, count - 1)


def add_halves(grads, place, name, scatter=False):
    n = len(grads)
    whole = [len(g.shape) == 2 for g in grads]
    halves = [g.shape[-2] // 2 for g in grads]
    cols = [GLA_IN_QUARTER if w else g.shape[-1] for g, w in zip(grads, whole)]
    rbs = [min(ADD_HALVES_ROWS, h) for h in halves]
    counts = [h // rb for h, rb in zip(halves, rbs)]
    starts, total = _spans(counts)
    half_shapes = [(*g.shape[:-2], h, g.shape[-1]) for g, h in zip(grads, halves)]

    def rows_of(ref, i, start):
        return ref.at[pl.ds(start, rbs[i])] if whole[i] else ref.at[:, pl.ds(start, rbs[i])]

    def body(place_ref, *refs):
        a_refs, o_refs = refs[:n], refs[n:2 * n]
        f_refs, h_refs = refs[2 * n:3 * n], refs[3 * n:4 * n]
        send_refs, their_refs, rest = refs[4 * n:5 * n], refs[5 * n:6 * n], refs[6 * n:]
        send_sems, recv_sems = rest[:2]
        t = pl.program_id(0)
        q = place_ref[1]
        x, y, c = _position()
        sum_refs = rest[2:2 + n] if scatter else h_refs

        def to_owners(i, k):
            out_sems, in_sems = rest[2 + n:]
            rows = pl.ds(k * rbs[i], rbs[i])
            return [pltpu.make_async_remote_copy(
                src_ref=sum_refs[i].at[2 * chip[0] + chip[1], rows], dst_ref=h_refs[i].at[j, rows],
                send_sem=out_sems.at[3 * (starts[i] + k) + j], recv_sem=in_sems.at[3 * (starts[i] + k) + j],
                device_id=(*chip, c), device_id_type=MESH) for j, chip in enumerate(_other_chips(x, y))]

        copies = [[pltpu.make_async_remote_copy(
            src_ref=rows_of(send_refs[i], i, k * rbs[i]), dst_ref=rows_of(their_refs[i], i, k * rbs[i]),
            send_sem=send_sems.at[starts[i] + k], recv_sem=recv_sems.at[starts[i] + k],
            device_id=(x, y, 1 - c), device_id_type=MESH) for k in range(counts[i])] for i in range(n)]

        for i in range(n):
            for k in range(counts[i]):
                @pl.when(t == starts[i] + k)
                def _(i=i, k=k):
                    rows_of(send_refs[i], i, k * rbs[i])[...] = _bf(o_refs[i][...])
                    copies[i][k].start()

        for i in range(n):
            for k in range(counts[i]):
                @pl.when(t == starts[i] + k + 1)
                def _(i=i, k=k):
                    copies[i][k].wait_recv()
                    b_ref = rows_of(their_refs[i], i, k * rbs[i])
                    h_ref = sum_refs[i].at[:, pl.ds(k * rbs[i], rbs[i])] if scatter else h_refs[i]
                    if not whole[i]:
                        h_ref[...] = _bf(a_refs[i][...] + b_ref[...].astype(F32))
                        f_refs[i][...] = a_refs[i][q] + b_ref[q].astype(F32)
                    else:
                        total_i = a_refs[i][...] + b_ref[...].astype(F32)
                        for k4 in range(N_CHIPS):
                            piece = total_i[:, k4 * cols[i]:(k4 + 1) * cols[i]]
                            h_ref[k4] = _bf(piece)

                            @pl.when(q == k4)
                            def _():
                                f_refs[i][...] = piece
                    if scatter:
                        for cp in to_owners(i, k):
                            cp.start()

        @pl.when(t == total)
        def _():
            for of_matrix in copies:
                for cp in of_matrix:
                    cp.wait_send()
            if scatter:
                for i in range(n):
                    for k in range(counts[i]):
                        for cp in to_owners(i, k):
                            cp.wait()

    def specs(i):
        sent = lambda t: _local_step(t, starts[i], counts[i])
        added = lambda t: _local_step(t - 1, starts[i], counts[i])
        by_quarter = (N_CHIPS, rbs[i], cols[i])
        block = (rbs[i], grads[i].shape[-1]) if whole[i] else by_quarter
        lead = () if whole[i] else (0,)
        mine = pl.BlockSpec(block, lambda t, place: (*lead, place[0] * counts[i] + added(t), 0))
        other = pl.BlockSpec(block, lambda t, place: (*lead, (1 - place[0]) * counts[i] + sent(t), 0))
        sums = pl.BlockSpec(by_quarter, lambda t, place: (0, added(t), 0))
        own = pl.BlockSpec(by_quarter[1:], lambda t, place: (added(t), 0))
        return mine, other, own, sums

    all_specs = [specs(i) for i in range(n)]
    sum_shapes = [(N_CHIPS, h, cl) for h, cl in zip(halves, cols)]
    scratch = [pltpu.VMEM(sh, BF16) for sh in half_shapes] + [pltpu.VMEM(sh, BF16) for sh in half_shapes]
    scratch += [pltpu.SemaphoreType.DMA((total,)), pltpu.SemaphoreType.DMA((total,))]
    if scatter:
        scratch += [pltpu.VMEM(sh, BF16) for sh in sum_shapes]
        scratch += [pltpu.SemaphoreType.DMA((3 * total,)), pltpu.SemaphoreType.DMA((3 * total,))]
    outs = pl.pallas_call(
        body, name=name,
        grid_spec=pltpu.PrefetchScalarGridSpec(
            num_scalar_prefetch=1, grid=(total + 1,),
            in_specs=[sp[0] for sp in all_specs] + [sp[1] for sp in all_specs],
            out_specs=[sp[2] for sp in all_specs] + (_any_specs(n) if scatter else [sp[3] for sp in all_specs]),
            scratch_shapes=scratch),
        out_shape=[jax.ShapeDtypeStruct((h, cl), F32) for h, cl in zip(halves, cols)]
                  + [jax.ShapeDtypeStruct((N_CHIPS - 1 if scatter else N_CHIPS, *sh[1:]), BF16) for sh in sum_shapes],
        compiler_params=_params("arbitrary"),
    )(place, *grads, *grads)
    return list(zip(outs[:n], outs[n:]))


SMALL_SUM_ROWS = 16


def join_halves(grad, place, owns, gots, small_pool, small_gla, small_top, g_gk_pad):
    n = len(owns)
    half, cols = grad.shape[1] // 2, grad.shape[2]
    rb = min(ADD_HALVES_ROWS, half)
    sent = half // rb
    shapes = [g.shape for g in gots] + [(N_CHIPS - 1, half, cols)]
    rbs = [min(ADD_ROWS, sh[1]) for sh in shapes]
    counts = [sh[1] // r for sh, r in zip(shapes, rbs)]
    starts, joined = _spans(counts)
    first_join = sent + 1
    steps = first_join + joined

    def body(place_ref, *refs):
        refs = iter(refs)
        take = lambda count: [next(refs) for _ in range(count)]
        (a_ref, o_ref), o_refs, g_refs = take(2), take(n), take(n)
        pool_ref, gla_ref, top_ref, gk_ref = take(4)
        out_refs, (total_ref,) = take(n + 1), take(1)
        send_buf, their_buf, sums_buf, got_buf = take(4)
        sum_refs = take(n + 1)
        to_core_sems, from_core_sems, to_chip_sems, from_chip_sems, local_sems, send_sems, recv_sems = take(7)
        all_ref, small_ref = take(2)
        t = pl.program_id(0)
        q = place_ref[1]
        x, y, c = _position()
        start_small, wait_small = _gather_small(small_ref, all_ref, *refs)
        block = lambda k: pl.ds(k * rb, rb)

        def to_core(k):
            return pltpu.make_async_remote_copy(
                src_ref=send_buf.at[:, block(k)], dst_ref=their_buf.at[:, block(k)],
                send_sem=to_core_sems.at[k], recv_sem=from_core_sems.at[k],
                device_id=(x, y, 1 - c), device_id_type=MESH)

        def to_owners(k):
            return [pltpu.make_async_remote_copy(
                src_ref=sums_buf.at[2 * chip[0] + chip[1], block(k)], dst_ref=got_buf.at[j, block(k)],
                send_sem=to_chip_sems.at[3 * k + j], recv_sem=from_chip_sems.at[3 * k + j],
                device_id=(*chip, c), device_id_type=MESH) for j, chip in enumerate(_other_chips(x, y))]

        def copies(i, k):
            src = sum_refs[i].at[pl.ds(k * rbs[i], rbs[i])]
            rows = pl.ds(c * shapes[i][1] + k * rbs[i], rbs[i])
            return (pltpu.make_async_copy(src, out_refs[i].at[rows], local_sems.at[starts[i] + k]),
                    pltpu.make_async_remote_copy(
                        src_ref=src, dst_ref=out_refs[i].at[rows],
                        send_sem=send_sems.at[starts[i] + k], recv_sem=recv_sems.at[starts[i] + k],
                        device_id=(x, y, 1 - c), device_id_type=MESH))

        @pl.when(t == 0)
        def _():
            small_ref[0:3, :] = pool_ref[0:3, :]
            small_ref[3:5, :] = gla_ref[0:2, :]
            small_ref[5:8, :] = top_ref[0:3, :]
            for r in range(GATE_RANK):
                small_ref[8 + r // 2:9 + r // 2, (r % 2) * KEY_W:(r % 2 + 1) * KEY_W] = gk_ref[r:r + 1, :]
            start_small()

        for k in range(sent):
            @pl.when(t == k)
            def _(k=k):
                send_buf[:, k * rb:(k + 1) * rb, :] = _bf(o_ref[...])
                to_core(k).start()

        for k in range(sent):
            @pl.when(t == k + 1)
            def _(k=k):
                to_core(k).wait_recv()
                theirs = their_buf.at[:, block(k)]
                sums_buf[:, k * rb:(k + 1) * rb, :] = _bf(a_ref[...] + theirs[...].astype(F32))
                sum_refs[n][k * rb:(k + 1) * rb, :] = a_ref[q] + theirs[q].astype(F32)
                for cp in to_owners(k):
                    cp.start()

        for i in range(n + 1):
            for k in range(counts[i]):
                @pl.when(t == first_join + starts[i] + k)
                def _(i=i, k=k):
                    rows = slice(k * rbs[i], (k + 1) * rbs[i])
                    if i < n:
                        total_i = o_refs[i][...]
                        arrived = [g_refs[i][j] for j in range(N_CHIPS - 1)]
                    else:
                        if k == 0:
                            for kk in range(sent):
                                for cp in to_owners(kk):
                                    cp.wait_recv()
                        total_i = sum_refs[n][rows, :]
                        arrived = [got_buf[j, rows, :] for j in range(N_CHIPS - 1)]
                    for part in arrived:
                        total_i = total_i + part.astype(F32)
                    sum_refs[i][rows, :] = total_i
                    for cp in copies(i, k):
                        cp.start()

        @pl.when(t == steps - 1)
        def _():
            wait_small()
            small_total = all_ref[0]
            for dev in range(1, N_DEV):
                small_total = small_total + all_ref[dev]
            total_ref[...] = small_total
            for k in range(sent):
                to_core(k).wait_send()
                for cp in to_owners(k):
                    cp.wait_send()
            for i in range(n + 1):
                for k in range(counts[i]):
                    for cp in copies(i, k):
                        cp.wait()

    def specs(i):
        step = lambda t: _local_step(t - first_join, starts[i], counts[i])
        return (pl.BlockSpec((rbs[i], shapes[i][2]), lambda t, place: (step(t), 0)),
                pl.BlockSpec((N_CHIPS - 1, rbs[i], shapes[i][2]), lambda t, place: (0, step(t), 0)))

    by_quarter = (N_CHIPS, rb, cols)
    mine = pl.BlockSpec(by_quarter, lambda t, place: (0, place[0] * sent + jnp.clip(t - 1, 0, sent - 1), 0))
    other = pl.BlockSpec(by_quarter, lambda t, place: (0, (1 - place[0]) * sent + jnp.clip(t, 0, sent - 1), 0))
    all_specs = [specs(i) for i in range(n)]
    sems = lambda count: pltpu.SemaphoreType.DMA((count,))
    outs = pl.pallas_call(
        body, name="join_halves",
        grid_spec=pltpu.PrefetchScalarGridSpec(
            num_scalar_prefetch=1, grid=(steps,),
            in_specs=[mine, other] + [sp[0] for sp in all_specs] + [sp[1] for sp in all_specs] + [VMEM_SPEC] * 4,
            out_specs=_any_specs(n + 1) + [VMEM_SPEC],
            scratch_shapes=[pltpu.VMEM((N_CHIPS, half, cols), BF16) for _ in range(3)]
                           + [pltpu.VMEM(shapes[n], BF16)] + [pltpu.VMEM(sh[1:], F32) for sh in shapes]
                           + [sems(sent), sems(sent), sems(3 * sent), sems(3 * sent),
                              sems(joined), sems(joined), sems(joined),
                              pltpu.VMEM((N_DEV, SMALL_SUM_ROWS, D), F32), pltpu.VMEM((SMALL_SUM_ROWS, D), F32)]
                           + SMALL_SEMS),
        out_shape=[jax.ShapeDtypeStruct((2 * sh[1], sh[2]), F32) for sh in shapes]
                  + [jax.ShapeDtypeStruct((SMALL_SUM_ROWS, D), F32)],
        compiler_params=_params("arbitrary"),
    )(place, grad, grad, *owns, *gots, small_pool, small_gla, small_top, g_gk_pad)
    return [outs[n]] + list(outs[:n]), outs[n + 1]


def _adam_math(w, g, m, v):
    m = ADAM_B1 * m + (1.0 - ADAM_B1) * g
    v = ADAM_B2 * v + (1.0 - ADAM_B2) * (g * g)
    m_hat = m / (1.0 - ADAM_B1 ** ADAM_STEP)
    v_hat = v / (1.0 - ADAM_B2 ** ADAM_STEP)
    delta = -ADAM_LR * (m_hat / (jnp.sqrt(v_hat) + ADAM_EPS) + ADAM_WD * w)
    return delta, m, v


ADAM_BLOCK_BYTES = 2 ** 19
ADAM_MOST_STEPS = 8


def adamw(params, name):
    n = len(params)
    shapes = [p[0].shape for p in params]

    def tile_rows(shape):
        rows, cols = shape[0], shape[-1]
        aligned = 1 if len(shape) == 3 else 8
        divisors = [t for t in range(aligned, rows + 1, aligned) if rows % t == 0]
        tile = max(t for t in divisors if t * cols * 4 <= ADAM_BLOCK_BYTES)
        if rows // tile > ADAM_MOST_STEPS:
            tile = min(t for t in divisors if rows // t <= ADAM_MOST_STEPS)
        return tile

    tiles = [tile_rows(sh) for sh in shapes]
    counts = [sh[0] // tl for sh, tl in zip(shapes, tiles)]
    starts, total = _spans(counts)

    def body(*refs):
        ins, outs = refs[:4 * n], refs[4 * n:]
        t = pl.program_id(0)
        for i in range(n):
            @pl.when((t >= starts[i]) & (t < starts[i] + counts[i]))
            def _(i=i):
                w_ref, g_ref, m_ref, v_ref = ins[4 * i:4 * i + 4]
                d, nm, nv = _adam_math(w_ref[...], g_ref[...], m_ref[...], v_ref[...])
                outs[3 * i][...] = d
                outs[3 * i + 1][...] = nm
                outs[3 * i + 2][...] = nv

    def spec(i):
        block = (tiles[i],) + shapes[i][1:]
        zeros = (0,) * (len(block) - 1)
        return pl.BlockSpec(block, lambda t: (_local_step(t, starts[i], counts[i]),) + zeros)

    outs = pl.pallas_call(
        body, name=name, grid=(total,),
        out_shape=[jax.ShapeDtypeStruct(sh, F32) for sh in shapes for _ in range(3)],
        in_specs=[spec(i) for i in range(n) for _ in range(4)],
        out_specs=[spec(i) for i in range(n) for _ in range(3)],
        compiler_params=_params("arbitrary"),
    )(*[a for p in params for a in p])
    return [tuple(outs[3 * i:3 * i + 3]) for i in range(n)]


def adamw_small(params, total, place):
    n = len(params)

    def cut_gradients(total_ref, q, g_refs):
        g_norm, g_group_b, g_scale, g_gk_w, g_gk_b, g_head_norm, g_final = g_refs
        g_norm[0:1, :] = total_ref[0:1, :]
        g_norm[1:2, :] = total_ref[3:4, :]
        g_scale[...] = total_ref[1:2, :]
        g_final[...] = total_ref[5:6, :]
        g_gk_b[...] = total_ref[4:5, pl.ds(pl.multiple_of(q * 128, 128), 128)]
        for r in range(GATE_RANK):
            lanes = pl.ds(pl.multiple_of((r % 2) * KEY_W + q * 128, 128), 128)
            g_gk_w[r:r + 1, :] = total_ref[8 + r // 2:9 + r // 2, lanes]
        for k in range(N_CHIPS):
            @pl.when(q == k)
            def _(k=k):
                g_head_norm[...] = total_ref[6:7, 64 * k:64 * (k + 1)]
                for g in range(GROUPS):
                    g_group_b[g:g + 1, :] = total_ref[2:3, GROUP_DIM * g + 64 * k:GROUP_DIM * g + 64 * (k + 1)]

    def body(place_ref, total_ref, *refs):
        ins, outs = refs[:3 * n], refs[3 * n:]
        outs[4 * n][...] = total_ref[7:8, 0:1]
        cut_gradients(total_ref, place_ref[1], outs[0:4 * n:4])
        for k in range(n):
            w_ref, m_ref, v_ref = ins[3 * k:3 * k + 3]
            d, nm, nv = _adam_math(w_ref[...], outs[4 * k][...], m_ref[...], v_ref[...])
            outs[4 * k + 1][...] = d
            outs[4 * k + 2][...] = nm
            outs[4 * k + 3][...] = nv

    flat = [a for p in params for a in p]
    outs = pl.pallas_call(
        body, name="adamw_small",
        out_shape=[jax.ShapeDtypeStruct(p[0].shape, F32) for p in params for _ in range(4)]
                  + [jax.ShapeDtypeStruct((1, 1), F32)],
        in_specs=[pl.BlockSpec(memory_space=pltpu.SMEM)] + [VMEM_SPEC] * (1 + 3 * n),
        out_specs=[VMEM_SPEC] * (4 * n + 1),
    )(place, total, *flat)
    return [tuple(outs[4 * k:4 * k + 4]) for k in range(n)], outs[4 * n]


def matmul_tn(a, b, name, tile_n, by_column_tile=False, chip_sums=()):
    s, m = a.shape
    n = b.shape[1]
    n_sums = len(chip_sums)
    steps = n // tile_n
    if by_column_tile:
        out_shape = jax.ShapeDtypeStruct((steps, m, tile_n), F32)
        out_spec = pl.BlockSpec((None, m, tile_n), lambda j: (j, 0, 0))
    else:
        out_shape = jax.ShapeDtypeStruct((m, n), F32)
        out_spec = pl.BlockSpec((m, tile_n), lambda j: (0, j))

    def body(a_ref, b_ref, *rest):
        sum_refs, out_ref, got_refs = rest[:n_sums], rest[n_sums], rest[n_sums + 1:2 * n_sums + 1]
        j = pl.program_id(0)
        copies = _scatter_copies(sum_refs, got_refs, *rest[2 * n_sums + 1:]) if n_sums else []

        @pl.when(j == 0)
        def _():
            for cp in copies:
                cp.start()

        out_ref[...] = _tn(a_ref[...], b_ref[...])

        @pl.when(j == steps - 1)
        def _():
            for cp in copies:
                cp.wait()

    outs = pl.pallas_call(
        body, name=name, grid=(steps,),
        out_shape=[out_shape] + _scatter_shapes(chip_sums),
        in_specs=[_full((s, m)), pl.BlockSpec((s, tile_n), lambda j: (0, j))] + _any_specs(n_sums),
        out_specs=[out_spec] + _any_specs(n_sums),
        scratch_shapes=[pltpu.SemaphoreType.DMA((3 * n_sums,)), pltpu.SemaphoreType.DMA((3 * n_sums,))]
                       if n_sums else [],
        compiler_params=_params("arbitrary"),
    )(a, b, *chip_sums)
    return outs[0], outs[1:]


ROW_TILE = 512


def _row_index(tile, rows):
    return tile * rows + lax.broadcasted_iota(jnp.int32, (rows, 1), 0)


def _inverse_counts(t_glob):
    return [1.0 / jnp.minimum(t_glob + 1, w).astype(F32) for w in POOL_WINDOWS]


def _sigmoid(z):
    return 1.0 / (1.0 + jnp.exp(-z))


def _trailing_sums(src, tmp, cols, window, rows):
    bufs = (src, tmp)
    span, level, start = 1, 0, 0
    while span < window:
        start += 8
        a, b = bufs[level % 2], bufs[(level + 1) % 2]
        n = HALO + rows - start
        b[start:start + n, cols] = a[start:start + n, cols] + a[start - span:start - span + n, cols]
        span, level = 2 * span, level + 1
    return bufs[level % 2][HALO:HALO + rows, cols]


def _leading_sums(src, tmp, cols, window, rows):
    bufs = (src, tmp)
    span, level, n = 1, 0, rows + HALO
    while span < window:
        n -= 8
        a, b = bufs[level % 2], bufs[(level + 1) % 2]
        b[0:n, cols] = a[0:n, cols] + a[span:span + n, cols]
        span, level = 2 * span, level + 1
    return bufs[level % 2][0:rows, cols]


def gather_in_background(step, last, out_refs, send_sems, recv_sems, finish):
    n = len(out_refs)
    x, y, c = _position()
    q = 2 * x + y
    chips = _other_chips(x, y)

    def copy(k, i, quarter, half, to):
        return _gather_copy(out_refs[i], send_sems, recv_sems, k * n + i, quarter, half, to)

    if not finish:
        @pl.when(step == 0)
        def _():
            for i in range(n):
                mine, _ = _halves(out_refs[i].shape[1], c)
                for j, chip in enumerate(chips):
                    copy(j, i, q, mine, (*chip, c)).start()

        @pl.when(step == last)
        def _():
            for j, chip in enumerate(chips):
                qj = 2 * chip[0] + chip[1]
                for i in range(n):
                    mine, _ = _halves(out_refs[i].shape[1], c)
                    copy(j, i, qj, mine, (x, y, c)).wait_recv()
                    copy(3 + j, i, qj, mine, (x, y, 1 - c)).start()
        return

    @pl.when(step == last)
    def _():
        for j, chip in enumerate(chips):
            qj = 2 * chip[0] + chip[1]
            for i in range(n):
                mine, other = _halves(out_refs[i].shape[1], c)
                copy(3 + j, i, qj, other, (x, y, c)).wait_recv()
                copy(j, i, q, mine, (x, y, c)).wait_send()
                copy(3 + j, i, qj, mine, (x, y, c)).wait_send()


def _group_matrix(gw_ref, g):
    rows = GROUP_DIM // N_CHIPS
    return jnp.concatenate([gw_ref[j, rows * g:rows * (g + 1), :] for j in range(N_CHIPS)], axis=0)


def pool_forward(x, w0, wpi, gw, gb, scale, wpo, later):
    s = x.shape[0]
    ts = ROW_TILE
    nt = s // ts
    assert nt >= 2
    n_later = len(later)

    def body(x_ref, w0_ref, wpi_ref, gw_ref, gb_ref, sc_ref, wpo_ref, *rest):
        rest = rest[n_later:]
        h1_ref, pooled_ref, gt_ref, n0_ref = rest[:4]
        later_refs = rest[4:4 + n_later]
        ubuf, tbuf, hist, send_sems, recv_sems = rest[4 + n_later:]
        i = pl.program_id(0)
        gather_in_background(i, nt - 1, later_refs, send_sems, recv_sems, finish=False)
        xv = x_ref[...]
        r = lax.rsqrt(jnp.mean(xv * xv, axis=-1, keepdims=True) + EPS)
        n0 = _bf(xv * r * w0_ref[...])
        n0_ref[...] = n0
        u = jnp.concatenate([_nn(n0, wpi_ref[0]), _nn(n0, wpi_ref[1])], axis=-1)
        gt = jnp.concatenate([_nn(n0, wpi_ref[2]), _nn(n0, wpi_ref[3])], axis=-1)
        gt_ref[...] = gt

        @pl.when(i == 0)
        def _():
            hist[...] = jnp.zeros_like(hist)

        ubuf[0:HALO, :] = hist[...]
        ubuf[HALO:HALO + ts, :] = u
        hist[...] = u[ts - HALO:, :]
        inv = _inverse_counts(_row_index(i, ts))
        mixed = []
        for g, w in enumerate(POOL_WINDOWS):
            cols = slice(g * GROUP_DIM, (g + 1) * GROUP_DIM)
            pooled = _bf(_trailing_sums(ubuf, tbuf, cols, w, ts) * inv[g] - u[:, cols])
            pooled_ref[:, cols] = pooled
            mixed.append(_nn(pooled, _group_matrix(gw_ref, g)))
        mixed = jnp.concatenate(mixed, axis=-1) + gb_ref[...]
        y = mixed * sc_ref[...] * (gt * _sigmoid(gt))
        h1_ref[...] = xv + _nn(_bf(y), wpo_ref[...])
        gather_in_background(i, nt - 1, later_refs, send_sems, recv_sems, finish=True)

    row = lambda cols: pl.BlockSpec((ts, cols), lambda i: (i, 0))
    outs = pl.pallas_call(
        body, name="pool_forward", grid=(nt,),
        out_shape=[jax.ShapeDtypeStruct((s, D), F32), jax.ShapeDtypeStruct((s, D), BF16),
                   jax.ShapeDtypeStruct((s, D), F32), jax.ShapeDtypeStruct((s, D), BF16)]
                  + [jax.ShapeDtypeStruct(a.shape, a.dtype) for a in later],
        in_specs=[row(D), _full((1, D)), _full((N_CHIPS, D, D // 2)), _full((GROUPS, GROUP_DIM, GROUP_DIM)),
                  _full((1, D)), _full((1, D)), _full((D, D))] + _any_specs(n_later),
        out_specs=[row(D), row(D), row(D), row(D)] + _any_specs(n_later),
        input_output_aliases={7 + k: 4 + k for k in range(n_later)},
        scratch_shapes=[pltpu.VMEM((HALO + ts, D), F32), pltpu.VMEM((HALO + ts, D), F32),
                        pltpu.VMEM((HALO, D), F32),
                        pltpu.SemaphoreType.DMA((6 * n_later,)), pltpu.SemaphoreType.DMA((6 * n_later,))],
        compiler_params=_params("arbitrary"),
    )(x, w0, wpi, gw, gb, scale, wpo, *later)
    return outs[:4], outs[4:]


def pool_backward(x, dh1, pooled, gt, w0, wpi, gw, gb, scale, wpo, chip_sums):
    s = x.shape[0]
    ts = ROW_TILE
    nt = s // ts
    n_sums = len(chip_sums)

    def body(x_ref, dh1_ref, pooled_ref, gt_ref, w0_ref, wpi_ref, gw_ref, gb_ref, sc_ref, wpo_ref, *rest):
        sum_refs, rest = rest[:n_sums], rest[n_sums:]
        dx_ref, dproj_ref, gpo_ref, ggw_ref, small_ref = rest[:5]
        got_refs = rest[5:5 + n_sums]
        ebuf, tbuf, ahead, send_sems, recv_sems = rest[5 + n_sums:]
        i = pl.program_id(0)
        copies = _scatter_copies(sum_refs, got_refs, send_sems, recv_sems)

        @pl.when(i == 0)
        def _():
            for cp in copies:
                cp.start()

        @pl.when(i == 0)
        def _():
            gpo_ref[...] = jnp.zeros_like(gpo_ref)
            ggw_ref[...] = jnp.zeros_like(ggw_ref)
            small_ref[...] = jnp.zeros_like(small_ref)
            ahead[...] = jnp.zeros_like(ahead)

        dh1 = dh1_ref[...]
        dh1_bf = _bf(dh1)
        gt = gt_ref[...]
        sc = sc_ref[...]
        dy = _nt(dh1_bf, wpo_ref[...])
        pooled_bf = []
        mixed = []
        for g in range(GROUPS):
            cols = slice(g * GROUP_DIM, (g + 1) * GROUP_DIM)
            pb = pooled_ref[:, cols]
            pooled_bf.append(pb)
            mixed.append(_nn(pb, _group_matrix(gw_ref, g)))
        mixed = jnp.concatenate(mixed, axis=-1) + gb_ref[...]
        sg = _sigmoid(gt)
        silu = gt * sg
        gpo_ref[...] += _tn(_bf(mixed * sc * silu), dh1_bf)
        dmixed = dy * sc * silu
        dgt = dy * mixed * sc * (sg * (1.0 + gt * (1.0 - sg)))
        dproj_ref[:, D:] = _bf(dgt)
        small_ref[1:2, :] += jnp.sum(dy * mixed * silu, axis=0, keepdims=True)
        small_ref[2:3, :] += jnp.sum(dmixed, axis=0, keepdims=True)

        inv = _inverse_counts(_row_index(nt - 1 - i, ts))
        rows_q = GROUP_DIM // N_CHIPS
        ebuf[ts:ts + HALO, :] = ahead[...]
        dpooled = []
        for g in range(GROUPS):
            cols = slice(g * GROUP_DIM, (g + 1) * GROUP_DIM)
            dm = _bf(dmixed[:, cols])
            ggw = _tn(pooled_bf[g], dm)
            for j in range(N_CHIPS):
                ggw_ref[j, rows_q * g:rows_q * (g + 1), :] += ggw[rows_q * j:rows_q * (j + 1), :]
            dp = _nt(dm, _group_matrix(gw_ref, g))
            dpooled.append(dp)
            ebuf[0:ts, cols] = dp * inv[g]
        ahead[...] = ebuf[0:HALO, :]
        du = []
        for g, w in enumerate(POOL_WINDOWS):
            cols = slice(g * GROUP_DIM, (g + 1) * GROUP_DIM)
            du.append(_leading_sums(ebuf, tbuf, cols, w, ts) - dpooled[g])
        du = _bf(jnp.concatenate(du, axis=-1))
        dproj_ref[:, :D] = du
        dgt_bf = _bf(dgt)
        half = D // 2
        dn0 = (_nt(du[:, :half], wpi_ref[0]) + _nt(du[:, half:], wpi_ref[1])
               + _nt(dgt_bf[:, :half], wpi_ref[2]) + _nt(dgt_bf[:, half:], wpi_ref[3]))

        xv = x_ref[...]
        r = lax.rsqrt(jnp.mean(xv * xv, axis=-1, keepdims=True) + EPS)
        xhat = xv * r
        small_ref[0:1, :] += jnp.sum(dn0 * xhat, axis=0, keepdims=True)
        dxh = dn0 * w0_ref[...]
        dx_ref[...] = dh1 + r * (dxh - xhat * jnp.mean(dxh * xhat, axis=-1, keepdims=True))

        @pl.when(i == nt - 1)
        def _():
            for cp in copies:
                cp.wait()

    row = lambda cols: pl.BlockSpec((ts, cols), lambda i: (nt - 1 - i, 0))
    outs = pl.pallas_call(
        body, name="pool_backward", grid=(nt,),
        out_shape=[jax.ShapeDtypeStruct((s, D), F32), jax.ShapeDtypeStruct((s, 2 * D), BF16),
                   jax.ShapeDtypeStruct((D, D), F32),
                   jax.ShapeDtypeStruct((GROUPS, GROUP_DIM, GROUP_DIM), F32),
                   jax.ShapeDtypeStruct((8, D), F32)] + _scatter_shapes(chip_sums),
        in_specs=[row(D), row(D), row(D), row(D), _full((1, D)), _full((N_CHIPS, D, D // 2)),
                  _full((GROUPS, GROUP_DIM, GROUP_DIM)), _full((1, D)), _full((1, D)), _full((D, D))]
                 + _any_specs(n_sums),
        out_specs=[row(D), row(2 * D), _full((D, D)), _full((GROUPS, GROUP_DIM, GROUP_DIM)), _full((8, D))]
                  + _any_specs(n_sums),
        scratch_shapes=[pltpu.VMEM((ts + HALO, D), F32), pltpu.VMEM((ts + HALO, D), F32),
                        pltpu.VMEM((HALO, D), F32),
                        pltpu.SemaphoreType.DMA((3 * n_sums,)), pltpu.SemaphoreType.DMA((3 * n_sums,))],
        compiler_params=_params("arbitrary"),
    )(x, dh1, pooled, gt, w0, wpi, gw, gb, scale, wpo, *chip_sums)
    return outs[:5], outs[5:]


def gla_project(h1, w1, wgi_q, wgk, bgk, later):
    s = h1.shape[0]
    ts = ROW_TILE
    nt = s // ts
    assert nt >= 2
    n_later = len(later)

    def body(h_ref, w1_ref, wq_ref, wgk_ref, bgk_ref, *rest):
        rest = rest[n_later:]
        qk_ref, v_ref, gate_ref, low_ref, cum_ref, n1_ref = rest[:6]
        later_refs = rest[6:6 + n_later]
        send_sems, recv_sems, wgi_ref = rest[6 + n_later:]
        gather_in_background(pl.program_id(0), nt - 1, later_refs, send_sems, recv_sems, finish=False)

        @pl.when(pl.program_id(0) == 0)
        def _():
            _assemble_gla_in(wq_ref, wgi_ref)

        hv = h_ref[...]
        r = lax.rsqrt(jnp.mean(hv * hv, axis=-1, keepdims=True) + EPS)
        n1 = _bf(hv * r * w1_ref[...])
        n1_ref[...] = n1
        qk_ref[...] = _nn(n1, wgi_ref[:, 0:2 * KEY_W])
        v_ref[...] = _bf(_nn(n1, wgi_ref[:, 2 * KEY_W:2 * KEY_W + D]))
        gate_ref[...] = _nn(n1, wgi_ref[:, 2 * KEY_W + D:GLA_MAIN])
        low = _bf(_nn(n1, wgi_ref[:, GLA_MAIN:]))
        low_ref[...] = low
        z = _nn(low, wgk_ref[...]) + bgk_ref[...]
        lg = (jnp.minimum(z, 0.0) - jnp.log(1.0 + jnp.exp(-jnp.abs(z)))) / GATE_NORM
        lower_f = _chunk_masks()[0].astype(F32)
        for r0 in range(0, ts, CHUNK):
            cum_ref[r0:r0 + CHUNK, :] = _nn_exact(lower_f, lg[r0:r0 + CHUNK, :])
        gather_in_background(pl.program_id(0), nt - 1, later_refs, send_sems, recv_sems, finish=True)

    row = lambda cols: pl.BlockSpec((ts, cols), lambda i: (i, 0))
    outs = pl.pallas_call(
        body, name="gla_project", grid=(nt,),
        out_shape=[jax.ShapeDtypeStruct((s, D), F32), jax.ShapeDtypeStruct((s, D), BF16),
                   jax.ShapeDtypeStruct((s, D), F32), jax.ShapeDtypeStruct((s, RANK_PAD), BF16),
                   jax.ShapeDtypeStruct((s, KEY_W), F32), jax.ShapeDtypeStruct((s, D), BF16)]
                  + [jax.ShapeDtypeStruct(a.shape, a.dtype) for a in later],
        in_specs=[row(D), _full((1, D)), _full((N_CHIPS, D, GLA_IN_QUARTER)),
                  _full((RANK_PAD, KEY_W)), _full((1, KEY_W))] + _any_specs(n_later),
        out_specs=[row(D), row(D), row(D), row(RANK_PAD), row(KEY_W), row(D)] + _any_specs(n_later),
        input_output_aliases={5 + k: 6 + k for k in range(n_later)},
        scratch_shapes=[pltpu.SemaphoreType.DMA((6 * n_later,)), pltpu.SemaphoreType.DMA((6 * n_later,)),
                        pltpu.VMEM((D, GLA_MAIN + RANK_PAD), BF16)],
        compiler_params=_params("arbitrary"),
    )(h1, w1, wgi_q, wgk, bgk, *later)
    return outs[:6], outs[6:]


def _assemble_gla_in(wq_ref, wfull):
    pad = jnp.zeros((CAST_ROWS, GLA_MAIN + RANK_PAD - GLA_IN), BF16)
    for r0 in range(0, D, CAST_ROWS):
        rows = slice(r0, r0 + CAST_ROWS)
        wfull[rows, :] = jnp.concatenate([wq_ref[q, rows, :] for q in range(N_CHIPS)] + [pad], axis=1)


GLA_BLOCK = 512
CHUNKS_PER_BLOCK = GLA_BLOCK // CHUNK


def _chunk_masks():
    t = lax.broadcasted_iota(jnp.int32, (CHUNK, CHUNK), 0)
    u = lax.broadcasted_iota(jnp.int32, (CHUNK, CHUNK), 1)
    return t >= u, t <= u


def _gla_chunk_terms(q, cum):
    ep = jnp.exp(cum)
    en = jnp.exp(-cum)
    qs = q * (HEAD_K ** -0.5)
    last = cum[CHUNK - 1:CHUNK, :]
    ed = jnp.exp(last - cum)
    dec = jnp.exp(last)
    return ep, en, qs, ed, dec


def gla_forward(qk, v, cum):
    s = qk.shape[0]
    nb = s // GLA_BLOCK
    nc = s // CHUNK

    def body(q_ref, k_ref, v_ref, cum_ref, o_ref, st_ref, sc_ref, state):
        @pl.when(pl.program_id(0) == 0)
        def _():
            state[...] = jnp.zeros_like(state)

        lower, _ = _chunk_masks()

        def chunk(cc, carry):
            rows = pl.ds(pl.multiple_of(cc * CHUNK, CHUNK), CHUNK)
            for h in range(HEADS):
                kc = slice(h * HEAD_K, (h + 1) * HEAD_K)
                vc = slice(h * HEAD_V, (h + 1) * HEAD_V)
                q = q_ref[rows, kc]
                k = k_ref[rows, kc]
                v = v_ref[rows, vc]
                ep, en, qs, ed, dec = _gla_chunk_terms(q, cum_ref[rows, kc])
                a = _bf(qs * ep)
                fwd = _nt(a, _bf(k * en))
                bwd = _nt(_bf(qs * en), _bf(k * ep))
                scores = _bf(jnp.where(lower, fwd, bwd))
                sc_ref[rows, h * CHUNK:(h + 1) * CHUNK] = scores
                st = state[h]
                st_ref[cc, h] = st
                o_ref[rows, vc] = _nn(scores, v) + _nt(a, _bf(st))
                state[h] = st * dec + _tn(v, _bf(k * ed))
            return carry

        lax.fori_loop(0, CHUNKS_PER_BLOCK, chunk, 0, unroll=True)

    return pl.pallas_call(
        body, name="gla_forward", grid=(nb,),
        out_shape=(jax.ShapeDtypeStruct((s, D), F32),
                   jax.ShapeDtypeStruct((nc, HEADS, HEAD_V, HEAD_K), F32),
                   jax.ShapeDtypeStruct((s, HEADS * CHUNK), BF16)),
        in_specs=[pl.BlockSpec((GLA_BLOCK, KEY_W), lambda i: (i, 0)),
                  pl.BlockSpec((GLA_BLOCK, KEY_W), lambda i: (i, 1)),
                  pl.BlockSpec((GLA_BLOCK, D), lambda i: (i, 0)),
                  pl.BlockSpec((GLA_BLOCK, KEY_W), lambda i: (i, 0))],
        out_specs=(pl.BlockSpec((GLA_BLOCK, D), lambda i: (i, 0)),
                   pl.BlockSpec((CHUNKS_PER_BLOCK, HEADS, HEAD_V, HEAD_K), lambda i: (i, 0, 0, 0)),
                   pl.BlockSpec((GLA_BLOCK, HEADS * CHUNK), lambda i: (i, 0))),
        scratch_shapes=[pltpu.VMEM((HEADS, HEAD_V, HEAD_K), F32)],
        compiler_params=_params("arbitrary"),
    )(qk, qk, v, cum)


def gla_backward(qk, v, cum, do, states, scores):
    s = qk.shape[0]
    nb = s // GLA_BLOCK

    def body(q_ref, k_ref, v_ref, cum_ref, do_ref, st_ref, sc_ref, dq_ref, dk_ref, dv_ref, dcum_ref, dstate):
        @pl.when(pl.program_id(0) == 0)
        def _():
            dstate[...] = jnp.zeros_like(dstate)

        lower, _ = _chunk_masks()
        is_last = lax.broadcasted_iota(jnp.int32, (CHUNK, HEAD_K), 0) == CHUNK - 1

        def chunk(step, carry):
            cc = CHUNKS_PER_BLOCK - 1 - step
            rows = pl.ds(pl.multiple_of(cc * CHUNK, CHUNK), CHUNK)
            for h in range(HEADS):
                kc = slice(h * HEAD_K, (h + 1) * HEAD_K)
                vc = slice(h * HEAD_V, (h + 1) * HEAD_V)
                q = q_ref[rows, kc]
                k = k_ref[rows, kc]
                v = v_ref[rows, vc]
                do_c = do_ref[rows, vc]
                ep, en, qs, ed, dec = _gla_chunk_terms(q, cum_ref[rows, kc])
                a = _bf(qs * ep)
                b = _bf(k * en)
                c = _bf(qs * en)
                dk_dec = _bf(k * ep)
                kd = _bf(k * ed)
                scores = sc_ref[rows, h * CHUNK:(h + 1) * CHUNK]
                st = st_ref[cc, h]
                dst = dstate[h]
                dst_bf = _bf(dst)

                dscores = _nt(do_c, v)
                dfwd = _bf(jnp.where(lower, dscores, 0.0))
                dbwd = _bf(jnp.where(lower, 0.0, dscores))
                dv_ref[rows, vc] = _bf(_tn(scores, do_c) + _nt(kd, dst_bf))
                da = _nn(dfwd, b) + _nn(do_c, _bf(st))
                db = _tn(dfwd, a)
                dc = _nn(dbwd, dk_dec)
                ddk = _tn(dbwd, c)
                dkd = _nn(v, dst_bf)
                ddec = jnp.sum(dst * st, axis=0, keepdims=True)
                dstate[h] = dst * dec + _tn(do_c, a)

                m = dkd * k * ed
                dq_ref[rows, kc] = _bf((da * ep + dc * en) * (HEAD_K ** -0.5))
                dk_ref[rows, kc] = _bf(db * en + ddk * ep + dkd * ed)
                dcum = (da * qs + ddk * k) * ep - (db * k + dc * qs) * en - m
                dlast = jnp.sum(m, axis=0, keepdims=True) + ddec * dec
                dcum_ref[rows, kc] = dcum + jnp.where(is_last, dlast, 0.0)
            return carry

        lax.fori_loop(0, CHUNKS_PER_BLOCK, chunk, 0, unroll=True)

    rev = lambda cols, col_block: pl.BlockSpec((GLA_BLOCK, cols), lambda i: (nb - 1 - i, col_block))
    return pl.pallas_call(
        body, name="gla_backward", grid=(nb,),
        out_shape=(jax.ShapeDtypeStruct((s, KEY_W), BF16), jax.ShapeDtypeStruct((s, KEY_W), BF16),
                   jax.ShapeDtypeStruct((s, D), BF16), jax.ShapeDtypeStruct((s, KEY_W), F32)),
        in_specs=[rev(KEY_W, 0), rev(KEY_W, 1), rev(D, 0), rev(KEY_W, 0), rev(D, 0),
                  pl.BlockSpec((CHUNKS_PER_BLOCK, HEADS, HEAD_V, HEAD_K), lambda i: (nb - 1 - i, 0, 0, 0)),
                  rev(HEADS * CHUNK, 0)],
        out_specs=(rev(KEY_W, 0), rev(KEY_W, 0), rev(D, 0), rev(KEY_W, 0)),
        scratch_shapes=[pltpu.VMEM((HEADS, HEAD_V, HEAD_K), F32)],
        compiler_params=_params("arbitrary"),
    )(qk, qk, v, cum, do, states, scores)


def head_and_loss(o, gate, h1, target, hw, wgo, wf):
    s = o.shape[0]
    ts = ROW_TILE

    def body(o_ref, gate_ref, h1_ref, tgt_ref, hw_ref, wgo_ref, wf_ref,
             dh2_ref, do_ref, dgate_ref, ggo_ref, small_ref):
        @pl.when(pl.program_id(0) == 0)
        def _():
            ggo_ref[...] = jnp.zeros_like(ggo_ref)
            small_ref[...] = jnp.zeros_like(small_ref)

        gate = gate_ref[...]
        hw = hw_ref[...]
        sg = _sigmoid(gate)
        silu = gate * sg
        ohat, ro = [], []
        for h in range(HEADS):
            oh = o_ref[:, h * HEAD_V:(h + 1) * HEAD_V]
            rh = lax.rsqrt(jnp.mean(oh * oh, axis=-1, keepdims=True) + EPS)
            ro.append(rh)
            ohat.append(oh * rh)
        ohat = jnp.concatenate(ohat, axis=-1)
        on = ohat * hw
        y2 = _bf(on * silu)
        h2 = h1_ref[...] + _nn(y2, wgo_ref[...])
        rf = lax.rsqrt(jnp.mean(h2 * h2, axis=-1, keepdims=True) + EPS)
        h2hat = h2 * rf
        wf = wf_ref[...]
        diff = h2hat * wf - tgt_ref[...]
        small_ref[2:3, :] += jnp.zeros((1, D), F32) + 0.5 * jnp.sum(diff * diff) / D
        dout = diff / D
        small_ref[0:1, :] += jnp.sum(dout * h2hat, axis=0, keepdims=True)
        dxh = dout * wf
        dh2 = rf * (dxh - h2hat * jnp.mean(dxh * h2hat, axis=-1, keepdims=True))
        dh2_ref[...] = dh2
        dh2_bf = _bf(dh2)
        ggo_ref[...] += _tn(y2, dh2_bf)
        dy2 = _nt(dh2_bf, wgo_ref[...])
        don = dy2 * silu
        dgate_ref[...] = _bf(dy2 * on * (sg * (1.0 + gate * (1.0 - sg))))
        ghw = jnp.sum(don * ohat, axis=0, keepdims=True)
        small_ref[1:2, 0:HEAD_V] += sum(ghw[:, h * HEAD_V:(h + 1) * HEAD_V] for h in range(HEADS))
        dohat = don * hw
        for h in range(HEADS):
            cols = slice(h * HEAD_V, (h + 1) * HEAD_V)
            oh, dh = ohat[:, cols], dohat[:, cols]
            do_ref[:, cols] = _bf(ro[h] * (dh - oh * jnp.mean(dh * oh, axis=-1, keepdims=True)))

    row = lambda cols: pl.BlockSpec((ts, cols), lambda i: (i, 0))
    act = jax.ShapeDtypeStruct((s, D), F32)
    act_bf = jax.ShapeDtypeStruct((s, D), BF16)
    return pl.pallas_call(
        body, name="head_and_loss", grid=(s // ts,),
        out_shape=(act, act_bf, act_bf, jax.ShapeDtypeStruct((D, D), F32), jax.ShapeDtypeStruct((8, D), F32)),
        in_specs=[row(D), row(D), row(D), row(D),
                  _full((1, D)), _full((D, D)), _full((1, D))],
        out_specs=(row(D), row(D), row(D), _full((D, D)), _full((8, D))),
        compiler_params=_params("arbitrary"),
    )(o, gate, h1, target, hw, wgo, wf)


def gla_project_backward(dq, dk, dv, dgate, dcum, low, h1, dh2, w1, wgi_q, wgk, bgk):
    s = h1.shape[0]
    ts = ROW_TILE

    def body(dq_ref, dk_ref, dv_ref, dgate_ref, dcum_ref, low_ref, h1_ref, dh2_ref, w1_ref,
             wq_ref, wgk_ref, bgk_ref, dh1_ref, dproj_ref, ggk_ref, small_ref, wgi_ref):
        @pl.when(pl.program_id(0) == 0)
        def _():
            ggk_ref[...] = jnp.zeros_like(ggk_ref)
            small_ref[...] = jnp.zeros_like(small_ref)
            _assemble_gla_in(wq_ref, wgi_ref)

        low = low_ref[...]
        z = _nn(low, wgk_ref[...]) + bgk_ref[...]
        upper_f = _chunk_masks()[1].astype(F32)
        dlg = jnp.concatenate([_nn_exact(upper_f, dcum_ref[r0:r0 + CHUNK, :]) for r0 in range(0, ts, CHUNK)],
                              axis=0)
        dz = dlg * (1.0 / GATE_NORM) * _sigmoid(-z)
        dz_bf = _bf(dz)
        ggk_ref[...] += _tn(low, dz_bf)
        small_ref[1:2, 0:KEY_W] += jnp.sum(dz, axis=0, keepdims=True)
        dlow = _bf(_nt(dz_bf, wgk_ref[...]))
        dproj_ref[:, GLA_MAIN:] = dlow
        dn1 = _nt(dlow, wgi_ref[:, GLA_MAIN:])
        for ref, lo, hi in ((dq_ref, 0, KEY_W), (dk_ref, KEY_W, 2 * KEY_W),
                            (dv_ref, 2 * KEY_W, 2 * KEY_W + D), (dgate_ref, 2 * KEY_W + D, GLA_MAIN)):
            piece = ref[...]
            dproj_ref[:, lo:hi] = piece
            dn1 = dn1 + _nt(piece, wgi_ref[:, lo:hi])
        hv = h1_ref[...]
        r = lax.rsqrt(jnp.mean(hv * hv, axis=-1, keepdims=True) + EPS)
        hhat = hv * r
        small_ref[0:1, :] += jnp.sum(dn1 * hhat, axis=0, keepdims=True)
        dxh = dn1 * w1_ref[...]
        dh1_ref[...] = dh2_ref[...] + r * (dxh - hhat * jnp.mean(dxh * hhat, axis=-1, keepdims=True))

    row = lambda cols: pl.BlockSpec((ts, cols), lambda i: (i, 0))
    return pl.pallas_call(
        body, name="gla_project_backward", grid=(s // ts,),
        out_shape=(jax.ShapeDtypeStruct((s, D), F32), jax.ShapeDtypeStruct((s, GLA_MAIN + RANK_PAD), BF16),
                   jax.ShapeDtypeStruct((RANK_PAD, KEY_W), F32),
                   jax.ShapeDtypeStruct((8, D), F32)),
        in_specs=[row(KEY_W), row(KEY_W), row(D), row(D), row(KEY_W), row(RANK_PAD), row(D), row(D),
                  _full((1, D)), _full((N_CHIPS, D, GLA_IN_QUARTER)), _full((RANK_PAD, KEY_W)),
                  _full((1, KEY_W))],
        out_specs=(row(D), row(GLA_MAIN + RANK_PAD), _full((RANK_PAD, KEY_W)), _full((8, D))),
        scratch_shapes=[pltpu.VMEM((D, GLA_MAIN + RANK_PAD), BF16)],
        compiler_params=_params("arbitrary"),
    )(dq, dk, dv, dgate, dcum, low, h1, dh2, w1, wgi_q, wgk, bgk)


def local_gradients(xs, target, w0, w1, wf, wpi, gw, gb, scale, wpo, gla_quarters, wgk, bgk, hw_tiled, place):
    wgi_q, wgo_q = gla_quarters
    (h1, pooled, gt, n0), (wgi_q,) = pool_forward(xs, w0, wpi, gw, gb, scale, wpo, [wgi_q])
    (qk, v, gate, low, cum, n1), (wgo_q,) = gla_project(h1, w1, wgi_q, wgk, bgk, [wgo_q])
    wgo = wgo_q.reshape(D, D)
    o, states, scores = gla_forward(qk, v, cum)

    dh2, do, dgate, g_gla_out, small_top = head_and_loss(o, gate, h1, target, hw_tiled, wgo, wf)
    dq, dk, dv, dcum = gla_backward(qk, v, cum, do, states, scores)
    dh1, dproj, g_gk_pad, small_gla = gla_project_backward(
        dq, dk, dv, dgate, dcum, low, h1, dh2, w1, wgi_q, wgk, bgk)
    g_gla_in, _ = matmul_tn(n1, dproj, "grad_gla_in", tile_n=(GLA_MAIN + RANK_PAD) // 5)

    def chip_sums(grads, tag):
        return add_halves(grads, place, "add_halves_" + tag)

    gla_sums = chip_sums([g_gla_in, g_gla_out.reshape(N_CHIPS, D // N_CHIPS, D)], "gla")
    (dx, dpool, g_pool_out, g_group_w, small_pool), gla_got = pool_backward(
        xs, dh1, pooled, gt, w0, wpi, gw, gb, scale, wpo, [b for _, b in gla_sums])
    mix_sums = chip_sums([g_group_w, g_pool_out.reshape(N_CHIPS, D // N_CHIPS, D)], "pool_mix")
    g_pool_in, mix_got = matmul_tn(n0, dpool, "grad_pool_in", tile_n=D // 2, by_column_tile=True,
                                   chip_sums=[b for _, b in mix_sums])

    reduced, total = join_halves(
        g_pool_in, place, [f for f, _ in mix_sums + gla_sums], list(mix_got) + list(gla_got),
        small_pool, small_gla, small_top, g_gk_pad)
    return dx, reduced, total


def kernel(x, norm_w, pool_in_w, pool_group_w, pool_group_b, pool_scale, pool_out_w, gla_in_w, gla_gk_w, gla_gk_b, gla_head_norm_w, gla_out_w, final_norm_w, loss_target, m_norm_w, m_pool_in_w, m_pool_group_w, m_pool_group_b, m_pool_scale, m_pool_out_w, m_gla_in_w, m_gla_gk_w, m_gla_gk_b, m_gla_head_norm_w, m_gla_out_w, m_final_norm_w, v_norm_w, v_pool_in_w, v_pool_group_w, v_pool_group_b, v_pool_scale, v_pool_out_w, v_gla_in_w, v_gla_gk_w, v_gla_gk_b, v_gla_head_norm_w, v_gla_out_w, v_final_norm_w):
    xs = x[0]
    target = loss_target[0]
    q_chip = 2 * lax.axis_index("x") + lax.axis_index("y")
    place = jnp.stack([lax.axis_index("c"), q_chip]).astype(jnp.int32)

    (wpi, gw_q, wpo_q, wgi_q, wgo_q), (bgk, hw_tiled, gb, wgk) = allgather_weights(
        [pool_in_w[0], pool_group_w[0].reshape(GROUP_DIM, GROUP_DIM), pool_out_w[0], gla_in_w[0], gla_out_w[0]],
        exchange=(True, True, True, False, False),
        smalls=[gla_gk_b, gla_head_norm_w, pool_group_b[0], gla_gk_w[0]])
    wpo = wpo_q.reshape(D, D)

    w0 = norm_w[0:1]
    w1 = norm_w[1:2]
    wf = final_norm_w.reshape(1, D)

    dx, reduced, total = local_gradients(
        xs, target, w0, w1, wf, wpi, gw_q, gb, pool_scale, wpo, [wgi_q, wgo_q], wgk, bgk, hw_tiled, place)
    r_pool_in, r_group_w, r_pool_out, r_gla_in, r_gla_out = reduced
    r_group_w = r_group_w.reshape(GROUPS, 64, GROUP_DIM)

    turn = lambda a: jnp.transpose(a, (2, 0, 1))
    back = lambda a: jnp.transpose(a, (1, 2, 0))
    as2d = lambda a, w: a.reshape(-1, w.shape[-1])
    big_names = ("pool_in_w", "pool_group_w", "pool_out_w", "gla_in_w", "gla_out_w")
    big_args = [(pool_in_w, r_pool_in[None], m_pool_in_w, v_pool_in_w),
                (pool_group_w, r_group_w[None], m_pool_group_w, v_pool_group_w),
                (pool_out_w, r_pool_out[None], m_pool_out_w, v_pool_out_w),
                (gla_in_w, r_gla_in[None], m_gla_in_w, v_gla_in_w),
                (gla_out_w, r_gla_out[None], m_gla_out_w, v_gla_out_w)]
    to_kernel = lambda n, a, w: turn(a) if n == "gla_in_w" else as2d(a, w)
    from_kernel = lambda n, a, w: back(a) if n == "gla_in_w" else a.reshape(w.shape)
    big_in = [tuple(to_kernel(n, a, p[0]) for a in p) for n, p in zip(big_names, big_args)]
    big_out = adamw(big_in, "adamw")
    big = {n: (from_kernel(n, i[1], p[0]),) + tuple(from_kernel(n, o, p[0]) for o in out)
           for n, p, i, out in zip(big_names, big_args, big_in, big_out)}

    small_names = ("norm_w", "pool_group_b", "pool_scale", "gla_gk_w", "gla_gk_b", "gla_head_norm_w",
                   "final_norm_w")
    small_args = [(norm_w, m_norm_w, v_norm_w),
                  (pool_group_b, m_pool_group_b, v_pool_group_b),
                  (pool_scale, m_pool_scale, v_pool_scale),
                  (gla_gk_w, m_gla_gk_w, v_gla_gk_w),
                  (gla_gk_b, m_gla_gk_b, v_gla_gk_b),
                  (gla_head_norm_w, m_gla_head_norm_w, v_gla_head_norm_w),
                  (final_norm_w, m_final_norm_w, v_final_norm_w)]
    small_out, loss = adamw_small([tuple(as2d(a, p[0]) for a in p) for p in small_args], total, place)
    small = {n: tuple(o.reshape(p[0].shape) for o in out) for n, p, out in zip(small_names, small_args, small_out)}
    results = [
        small["norm_w"],
        big["pool_in_w"],
        big["pool_group_w"],
        small["pool_group_b"],
        small["pool_scale"],
        big["pool_out_w"],
        big["gla_in_w"],
        small["gla_gk_w"],
        small["gla_gk_b"],
        small["gla_head_norm_w"],
        big["gla_out_w"],
        small["final_norm_w"],
    ]
    grads, deltas, new_m, new_v = zip(*results)
    return (loss.reshape(()), dx[None], *grads, *deltas, *new_m, *new_v)
```

```python
import jax
import jax.numpy as jnp
from jax import lax
from jax.experimental import pallas as pl
from jax.experimental.pallas import tpu as pltpu

F32 = jnp.float32
BF16 = jnp.bfloat16
MESH = pl.DeviceIdType.MESH

D = 1024
POOL_WINDOWS = (2, 4, 8, 16)
GROUPS = 4
GROUP_DIM = 256
HEADS = 4
HEAD_K = 128
HEAD_V = 256
KEY_W = 512
CHUNK = 64
GATE_RANK = 16
GATE_NORM = 16.0
GLA_IN = 3088
GLA_MAIN = 3072
RANK_PAD = 128
EPS = 1e-6
HALO = 32

ADAM_LR = 0.001
ADAM_B1 = 0.9
ADAM_B2 = 0.999
ADAM_EPS = 1e-08
ADAM_WD = 0.01
ADAM_STEP = 10

N_CHIPS = 4
N_DEV = 8
GLA_IN_QUARTER = GLA_IN // N_CHIPS

VMEM_LIMIT = 56 * 1024 * 1024


def _nn(a, b):
    return lax.dot_general(a, b, (((1,), (0,)), ((), ())), preferred_element_type=F32)


def _nt(a, b):
    return lax.dot_general(a, b, (((1,), (1,)), ((), ())), preferred_element_type=F32)


def _tn(a, b):
    return lax.dot_general(a, b, (((0,), (0,)), ((), ())), preferred_element_type=F32)


def _nn_exact(a, b):
    return lax.dot_general(a, b, (((1,), (0,)), ((), ())), preferred_element_type=F32,
                           precision=lax.Precision.HIGHEST)


def _bf(a):
    return a.astype(BF16)


def _params(*sem):
    return pltpu.CompilerParams(dimension_semantics=sem, vmem_limit_bytes=VMEM_LIMIT)


def _full(shape):
    return pl.BlockSpec(shape, lambda i: (0,) * len(shape))


def _position():
    return lax.axis_index("x"), lax.axis_index("y"), lax.axis_index("c")


def _gather_small(in_ref, all_ref, send_sems, recv_sems, local_sem):
    x, y, c = _position()
    me = 4 * x + 2 * y + c
    mine = pltpu.make_async_copy(in_ref, all_ref.at[me], local_sem)
    sends = []
    for k in range(N_DEV - 1):
        fx, fy, fc = (k + 1) >> 2 & 1, (k + 1) >> 1 & 1, (k + 1) & 1
        sends.append(pltpu.make_async_remote_copy(
            src_ref=in_ref, dst_ref=all_ref.at[me],
            send_sem=send_sems.at[k], recv_sem=recv_sems.at[k],
            device_id=(x ^ fx, y ^ fy, c ^ fc), device_id_type=MESH))

    def start():
        mine.start()
        for cp in sends:
            cp.start()

    def wait():
        for k in range(N_DEV - 1):
            fx, fy, fc = (k + 1) >> 2 & 1, (k + 1) >> 1 & 1, (k + 1) & 1
            src_dev = 4 * (x ^ fx) + 2 * (y ^ fy) + (c ^ fc)
            pltpu.make_async_remote_copy(
                src_ref=in_ref, dst_ref=all_ref.at[src_dev],
                send_sem=send_sems.at[k], recv_sem=recv_sems.at[k],
                device_id=(x, y, c), device_id_type=MESH).wait_recv()
        for cp in sends:
            cp.wait_send()
        mine.wait()

    return start, wait


SMALL_SEMS = [pltpu.SemaphoreType.DMA((N_DEV - 1,)), pltpu.SemaphoreType.DMA((N_DEV - 1,)),
              pltpu.SemaphoreType.DMA]
VMEM_SPEC = pl.BlockSpec(memory_space=pltpu.VMEM)


def _other_chips(x, y):
    return [(1 - x, y), (x, 1 - y), (1 - x, 1 - y)]


def _any_specs(n):
    return [pl.BlockSpec(memory_space=pl.ANY)] * n


def _halves(rows, c):
    half = rows // 2
    return pl.ds(c * half, half), pl.ds((1 - c) * half, half)


CAST_ROWS = 256


def _gather_copy(out_ref, send_sems, recv_sems, k, quarter, half, to, src=None):
    dst = out_ref.at[quarter, half]
    return pltpu.make_async_remote_copy(
        src_ref=dst if src is None else src, dst_ref=dst,
        send_sem=send_sems.at[k], recv_sem=recv_sems.at[k], device_id=to, device_id_type=MESH)


SMALL_IN_ROWS = 24


def allgather_weights(quarters, exchange, smalls):
    n = len(quarters)
    shapes = [w.shape for w in quarters]
    moved = [i for i in range(n) if exchange[i]]

    def body(*refs):
        w_refs, (gkb_ref, hnw_ref, gb_ref, gkw_ref) = refs[:n], refs[n:n + 4]
        out_refs, (bgk_ref, hw_ref, gbias_ref, wgk_ref) = refs[n + 4:2 * n + 4], refs[2 * n + 4:2 * n + 8]
        refs = refs[2 * n + 8:]
        f32_bufs, bf_bufs = refs[:n], refs[n:2 * n]
        send_sems, recv_sems, local_sems, small_ref, small_all_ref = refs[2 * n:2 * n + 5]
        small_ref[...] = jnp.zeros_like(small_ref)
        small_ref[0:1, :] = gkb_ref[...]
        small_ref[1:2, 0:64] = hnw_ref[...]
        small_ref[2:2 + GROUPS, 0:64] = gb_ref[...]
        small_ref[8:8 + GATE_RANK, :] = gkw_ref[...]
        start_small, wait_small = _gather_small(small_ref, small_all_ref, *refs[2 * n + 5:])
        start_small()
        x, y, c = _position()
        q = 2 * x + y
        sibling = (x, y, 1 - c)
        chips = _other_chips(x, y)

        def copy(k, i, quarter, half, to, src=None):
            return _gather_copy(out_refs[i], send_sems, recv_sems, k * n + i, quarter, half, to, src)

        loads = [pltpu.make_async_copy(w_refs[i], f32_bufs[i], local_sems.at[i]) for i in range(n)]
        for cp in loads:
            cp.start()
        keeps, sends = [], []
        for i in range(n):
            loads[i].wait()
            for r0 in range(0, shapes[i][0], CAST_ROWS):
                bf_bufs[i][r0:r0 + CAST_ROWS, :] = _bf(f32_bufs[i][r0:r0 + CAST_ROWS, :])
            keep = pltpu.make_async_copy(bf_bufs[i], out_refs[i].at[q], local_sems.at[n + i])
            keep.start()
            keeps.append(keep)
            if not exchange[i]:
                continue
            mine, _ = _halves(shapes[i][0], c)
            for j, chip in enumerate(chips):
                cp = copy(j, i, q, mine, (*chip, c), src=bf_bufs[i].at[mine])
                cp.start()
                sends.append(cp)
        for j, chip in enumerate(chips):
            qj = 2 * chip[0] + chip[1]
            for i in moved:
                mine, _ = _halves(shapes[i][0], c)
                copy(j, i, qj, mine, (x, y, c)).wait_recv()
                cp = copy(3 + j, i, qj, mine, sibling)
                cp.start()
                sends.append(cp)
        for j, chip in enumerate(chips):
            qj = 2 * chip[0] + chip[1]
            for i in moved:
                _, other = _halves(shapes[i][0], c)
                copy(3 + j, i, qj, other, (x, y, c)).wait_recv()
        wait_small()
        wgk_ref[...] = jnp.zeros_like(wgk_ref)
        for j in range(N_CHIPS):
            block = small_all_ref.at[2 * j]
            bgk_ref[:, 128 * j:128 * (j + 1)] = block[0:1, :]
            for h in range(HEADS):
                hw_ref[:, HEAD_V * h + 64 * j:HEAD_V * h + 64 * (j + 1)] = block[1:2, 0:64]
            for g in range(GROUPS):
                gbias_ref[:, GROUP_DIM * g + 64 * j:GROUP_DIM * g + 64 * (j + 1)] = block[2 + g:3 + g, 0:64]
            wgk_ref[0:GATE_RANK, 128 * j:128 * (j + 1)] = _bf(block[8:8 + GATE_RANK, :])
        for cp in sends:
            cp.wait_send()
        for cp in keeps:
            cp.wait()

    outs = pl.pallas_call(
        body, name="allgather_weights",
        out_shape=[jax.ShapeDtypeStruct((N_CHIPS, *s), BF16) for s in shapes]
                  + [jax.ShapeDtypeStruct((1, KEY_W), F32), jax.ShapeDtypeStruct((1, D), F32),
                     jax.ShapeDtypeStruct((1, D), F32), jax.ShapeDtypeStruct((RANK_PAD, KEY_W), BF16)],
        in_specs=_any_specs(n) + [VMEM_SPEC] * 4, out_specs=_any_specs(n) + [VMEM_SPEC] * 4,
        scratch_shapes=([pltpu.VMEM(s, F32) for s in shapes] + [pltpu.VMEM(s, BF16) for s in shapes]
                        + [pltpu.SemaphoreType.DMA((6 * n,)), pltpu.SemaphoreType.DMA((6 * n,)),
                           pltpu.SemaphoreType.DMA((2 * n,)), pltpu.VMEM((SMALL_IN_ROWS, 128), F32),
                           pltpu.VMEM((N_DEV, SMALL_IN_ROWS, 128), F32)] + SMALL_SEMS),
        compiler_params=pltpu.CompilerParams(vmem_limit_bytes=VMEM_LIMIT),
    )(*quarters, *smalls)
    return outs[:n], outs[n:]


def _scatter_copies(b_refs, got_refs, send_sems, recv_sems):
    n = len(b_refs)
    x, y, c = _position()
    copies = []
    for j, chip in enumerate(_other_chips(x, y)):
        qj = 2 * chip[0] + chip[1]
        for i in range(n):
            copies.append(pltpu.make_async_remote_copy(
                src_ref=b_refs[i].at[qj], dst_ref=got_refs[i].at[j],
                send_sem=send_sems.at[j * n + i], recv_sem=recv_sems.at[j * n + i],
                device_id=(*chip, c), device_id_type=MESH))
    return copies


def _scatter_shapes(chip_sums):
    return [jax.ShapeDtypeStruct((N_CHIPS - 1, *b.shape[1:]), BF16) for b in chip_sums]


ADD_ROWS = 512
ADD_HALVES_ROWS = 128


def _spans(counts):
    starts, total = [], 0
    for count in counts:
        starts.append(total)
        total += count
    return starts, total


def _local_step(t, start, count):
    return jnp.clip(t - start, 0, count - 1)


def add_halves(grads, place, name):
    n = len(grads)
    whole = [len(g.shape) == 2 for g in grads]
    halves = [g.shape[-2] // 2 for g in grads]
    cols = [GLA_IN_QUARTER if w else g.shape[-1] for g, w in zip(grads, whole)]
    rbs = [min(ADD_HALVES_ROWS, h) for h in halves]
    counts = [h // rb for h, rb in zip(halves, rbs)]
    starts, total = _spans(counts)
    half_shapes = [(*g.shape[:-2], h, g.shape[-1]) for g, h in zip(grads, halves)]

    def rows_of(ref, i, start):
        return ref.at[pl.ds(start, rbs[i])] if whole[i] else ref.at[:, pl.ds(start, rbs[i])]

    def body(place_ref, *refs):
        a_refs, o_refs = refs[:n], refs[n:2 * n]
        f_refs, h_refs = refs[2 * n:3 * n], refs[3 * n:4 * n]
        send_refs, their_refs, (send_sems, recv_sems) = refs[4 * n:5 * n], refs[5 * n:6 * n], refs[6 * n:]
        t = pl.program_id(0)
        q = place_ref[1]
        x, y, c = _position()
        copies = [[pltpu.make_async_remote_copy(
            src_ref=rows_of(send_refs[i], i, k * rbs[i]), dst_ref=rows_of(their_refs[i], i, k * rbs[i]),
            send_sem=send_sems.at[starts[i] + k], recv_sem=recv_sems.at[starts[i] + k],
            device_id=(x, y, 1 - c), device_id_type=MESH) for k in range(counts[i])] for i in range(n)]

        for i in range(n):
            for k in range(counts[i]):
                @pl.when(t == starts[i] + k)
                def _(i=i, k=k):
                    rows_of(send_refs[i], i, k * rbs[i])[...] = _bf(o_refs[i][...])
                    copies[i][k].start()

        for i in range(n):
            for k in range(counts[i]):
                @pl.when(t == starts[i] + k + 1)
                def _(i=i, k=k):
                    copies[i][k].wait_recv()
                    b_ref = rows_of(their_refs[i], i, k * rbs[i])
                    if not whole[i]:
                        h_refs[i][...] = _bf(a_refs[i][...] + b_ref[...].astype(F32))
                        f_refs[i][...] = a_refs[i][q] + b_ref[q].astype(F32)
                        return
                    total_i = a_refs[i][...] + b_ref[...].astype(F32)
                    for k4 in range(N_CHIPS):
                        piece = total_i[:, k4 * cols[i]:(k4 + 1) * cols[i]]
                        h_refs[i][k4] = _bf(piece)

                        @pl.when(q == k4)
                        def _():
                            f_refs[i][...] = piece

        @pl.when(t == total)
        def _():
            for of_matrix in copies:
                for cp in of_matrix:
                    cp.wait_send()

    def specs(i):
        sent = lambda t: _local_step(t, starts[i], counts[i])
        added = lambda t: _local_step(t - 1, starts[i], counts[i])
        by_quarter = (N_CHIPS, rbs[i], cols[i])
        block = (rbs[i], grads[i].shape[-1]) if whole[i] else by_quarter
        lead = () if whole[i] else (0,)
        mine = pl.BlockSpec(block, lambda t, place: (*lead, place[0] * counts[i] + added(t), 0))
        other = pl.BlockSpec(block, lambda t, place: (*lead, (1 - place[0]) * counts[i] + sent(t), 0))
        sums = pl.BlockSpec(by_quarter, lambda t, place: (0, added(t), 0))
        own = pl.BlockSpec(by_quarter[1:], lambda t, place: (added(t), 0))
        return mine, other, own, sums

    all_specs = [specs(i) for i in range(n)]
    outs = pl.pallas_call(
        body, name=name,
        grid_spec=pltpu.PrefetchScalarGridSpec(
            num_scalar_prefetch=1, grid=(total + 1,),
            in_specs=[sp[0] for sp in all_specs] + [sp[1] for sp in all_specs],
            out_specs=[sp[2] for sp in all_specs] + [sp[3] for sp in all_specs],
            scratch_shapes=[pltpu.VMEM(sh, BF16) for sh in half_shapes] + [pltpu.VMEM(sh, BF16) for sh in half_shapes]
                           + [pltpu.SemaphoreType.DMA((total,)), pltpu.SemaphoreType.DMA((total,))]),
        out_shape=[jax.ShapeDtypeStruct((h, cl), F32) for h, cl in zip(halves, cols)]
                  + [jax.ShapeDtypeStruct((N_CHIPS, h, cl), BF16) for h, cl in zip(halves, cols)],
        compiler_params=_params("arbitrary"),
    )(place, *grads, *grads)
    return list(zip(outs[:n], outs[n:]))


SMALL_SUM_ROWS = 16


def join_halves(grad, place, owns, gots, small_pool, small_gla, small_top, g_gk_pad):
    n = len(owns)
    half, cols = grad.shape[1] // 2, grad.shape[2]
    rb = min(ADD_HALVES_ROWS, half)
    sent = half // rb
    shapes = [g.shape for g in gots] + [(N_CHIPS - 1, half, cols)]
    rbs = [min(ADD_ROWS, sh[1]) for sh in shapes]
    counts = [sh[1] // r for sh, r in zip(shapes, rbs)]
    starts, joined = _spans(counts)
    first_join = sent + 1
    steps = first_join + joined

    def body(place_ref, *refs):
        refs = iter(refs)
        take = lambda count: [next(refs) for _ in range(count)]
        (a_ref, o_ref), o_refs, g_refs = take(2), take(n), take(n)
        pool_ref, gla_ref, top_ref, gk_ref = take(4)
        out_refs, (total_ref,) = take(n + 1), take(1)
        send_buf, their_buf, sums_buf, got_buf = take(4)
        sum_refs = take(n + 1)
        to_core_sems, from_core_sems, to_chip_sems, from_chip_sems, local_sems, send_sems, recv_sems = take(7)
        all_ref, small_ref = take(2)
        t = pl.program_id(0)
        q = place_ref[1]
        x, y, c = _position()
        start_small, wait_small = _gather_small(small_ref, all_ref, *refs)
        block = lambda k: pl.ds(k * rb, rb)

        def to_core(k):
            return pltpu.make_async_remote_copy(
                src_ref=send_buf.at[:, block(k)], dst_ref=their_buf.at[:, block(k)],
                send_sem=to_core_sems.at[k], recv_sem=from_core_sems.at[k],
                device_id=(x, y, 1 - c), device_id_type=MESH)

        def to_owners(k):
            return [pltpu.make_async_remote_copy(
                src_ref=sums_buf.at[2 * chip[0] + chip[1], block(k)], dst_ref=got_buf.at[j, block(k)],
                send_sem=to_chip_sems.at[3 * k + j], recv_sem=from_chip_sems.at[3 * k + j],
                device_id=(*chip, c), device_id_type=MESH) for j, chip in enumerate(_other_chips(x, y))]

        def copies(i, k):
            src = sum_refs[i].at[pl.ds(k * rbs[i], rbs[i])]
            rows = pl.ds(c * shapes[i][1] + k * rbs[i], rbs[i])
            return (pltpu.make_async_copy(src, out_refs[i].at[rows], local_sems.at[starts[i] + k]),
                    pltpu.make_async_remote_copy(
                        src_ref=src, dst_ref=out_refs[i].at[rows],
                        send_sem=send_sems.at[starts[i] + k], recv_sem=recv_sems.at[starts[i] + k],
                        device_id=(x, y, 1 - c), device_id_type=MESH))

        @pl.when(t == 0)
        def _():
            small_ref[0:3, :] = pool_ref[0:3, :]
            small_ref[3:5, :] = gla_ref[0:2, :]
            small_ref[5:8, :] = top_ref[0:3, :]
            for r in range(GATE_RANK):
                small_ref[8 + r // 2:9 + r // 2, (r % 2) * KEY_W:(r % 2 + 1) * KEY_W] = gk_ref[r:r + 1, :]
            start_small()

        for k in range(sent):
            @pl.when(t == k)
            def _(k=k):
                send_buf[:, k * rb:(k + 1) * rb, :] = _bf(o_ref[...])
                to_core(k).start()

        for k in range(sent):
            @pl.when(t == k + 1)
            def _(k=k):
                to_core(k).wait_recv()
                theirs = their_buf.at[:, block(k)]
                sums_buf[:, k * rb:(k + 1) * rb, :] = _bf(a_ref[...] + theirs[...].astype(F32))
                sum_refs[n][k * rb:(k + 1) * rb, :] = a_ref[q] + theirs[q].astype(F32)
                for cp in to_owners(k):
                    cp.start()

        for i in range(n + 1):
            for k in range(counts[i]):
                @pl.when(t == first_join + starts[i] + k)
                def _(i=i, k=k):
                    rows = slice(k * rbs[i], (k + 1) * rbs[i])
                    if i < n:
                        total_i = o_refs[i][...]
                        arrived = [g_refs[i][j] for j in range(N_CHIPS - 1)]
                    else:
                        if k == 0:
                            for kk in range(sent):
                                for cp in to_owners(kk):
                                    cp.wait_recv()
                        total_i = sum_refs[n][rows, :]
                        arrived = [got_buf[j, rows, :] for j in range(N_CHIPS - 1)]
                    for part in arrived:
                        total_i = total_i + part.astype(F32)
                    sum_refs[i][rows, :] = total_i
                    for cp in copies(i, k):
                        cp.start()

        @pl.when(t == steps - 1)
        def _():
            wait_small()
            small_total = all_ref[0]
            for dev in range(1, N_DEV):
                small_total = small_total + all_ref[dev]
            total_ref[...] = small_total
            for k in range(sent):
                to_core(k).wait_send()
                for cp in to_owners(k):
                    cp.wait_send()
            for i in range(n + 1):
                for k in range(counts[i]):
                    for cp in copies(i, k):
                        cp.wait()

    def specs(i):
        step = lambda t: _local_step(t - first_join, starts[i], counts[i])
        return (pl.BlockSpec((rbs[i], shapes[i][2]), lambda t, place: (step(t), 0)),
                pl.BlockSpec((N_CHIPS - 1, rbs[i], shapes[i][2]), lambda t, place: (0, step(t), 0)))

    by_quarter = (N_CHIPS, rb, cols)
    mine = pl.BlockSpec(by_quarter, lambda t, place: (0, place[0] * sent + jnp.clip(t - 1, 0, sent - 1), 0))
    other = pl.BlockSpec(by_quarter, lambda t, place: (0, (1 - place[0]) * sent + jnp.clip(t, 0, sent - 1), 0))
    all_specs = [specs(i) for i in range(n)]
    sems = lambda count: pltpu.SemaphoreType.DMA((count,))
    outs = pl.pallas_call(
        body, name="join_halves",
        grid_spec=pltpu.PrefetchScalarGridSpec(
            num_scalar_prefetch=1, grid=(steps,),
            in_specs=[mine, other] + [sp[0] for sp in all_specs] + [sp[1] for sp in all_specs] + [VMEM_SPEC] * 4,
            out_specs=_any_specs(n + 1) + [VMEM_SPEC],
            scratch_shapes=[pltpu.VMEM((N_CHIPS, half, cols), BF16) for _ in range(3)]
                           + [pltpu.VMEM(shapes[n], BF16)] + [pltpu.VMEM(sh[1:], F32) for sh in shapes]
                           + [sems(sent), sems(sent), sems(3 * sent), sems(3 * sent),
                              sems(joined), sems(joined), sems(joined),
                              pltpu.VMEM((N_DEV, SMALL_SUM_ROWS, D), F32), pltpu.VMEM((SMALL_SUM_ROWS, D), F32)]
                           + SMALL_SEMS),
        out_shape=[jax.ShapeDtypeStruct((2 * sh[1], sh[2]), F32) for sh in shapes]
                  + [jax.ShapeDtypeStruct((SMALL_SUM_ROWS, D), F32)],
        compiler_params=_params("arbitrary"),
    )(place, grad, grad, *owns, *gots, small_pool, small_gla, small_top, g_gk_pad)
    return [outs[n]] + list(outs[:n]), outs[n + 1]


def _adam_math(w, g, m, v):
    m = ADAM_B1 * m + (1.0 - ADAM_B1) * g
    v = ADAM_B2 * v + (1.0 - ADAM_B2) * (g * g)
    m_hat = m / (1.0 - ADAM_B1 ** ADAM_STEP)
    v_hat = v / (1.0 - ADAM_B2 ** ADAM_STEP)
    delta = -ADAM_LR * (m_hat / (jnp.sqrt(v_hat) + ADAM_EPS) + ADAM_WD * w)
    return delta, m, v


ADAM_BLOCK_BYTES = 2 ** 19
ADAM_MOST_STEPS = 8


def adamw(params, name):
    n = len(params)
    shapes = [p[0].shape for p in params]

    def tile_rows(shape):
        rows, cols = shape[0], shape[-1]
        aligned = 1 if len(shape) == 3 else 8
        divisors = [t for t in range(aligned, rows + 1, aligned) if rows % t == 0]
        tile = max(t for t in divisors if t * cols * 4 <= ADAM_BLOCK_BYTES)
        if rows // tile > ADAM_MOST_STEPS:
            tile = min(t for t in divisors if rows // t <= ADAM_MOST_STEPS)
        return tile

    tiles = [tile_rows(sh) for sh in shapes]
    counts = [sh[0] // tl for sh, tl in zip(shapes, tiles)]
    starts, total = _spans(counts)

    def body(*refs):
        ins, outs = refs[:4 * n], refs[4 * n:]
        t = pl.program_id(0)
        for i in range(n):
            @pl.when((t >= starts[i]) & (t < starts[i] + counts[i]))
            def _(i=i):
                w_ref, g_ref, m_ref, v_ref = ins[4 * i:4 * i + 4]
                d, nm, nv = _adam_math(w_ref[...], g_ref[...], m_ref[...], v_ref[...])
                outs[3 * i][...] = d
                outs[3 * i + 1][...] = nm
                outs[3 * i + 2][...] = nv

    def spec(i):
        block = (tiles[i],) + shapes[i][1:]
        zeros = (0,) * (len(block) - 1)
        return pl.BlockSpec(block, lambda t: (_local_step(t, starts[i], counts[i]),) + zeros)

    outs = pl.pallas_call(
        body, name=name, grid=(total,),
        out_shape=[jax.ShapeDtypeStruct(sh, F32) for sh in shapes for _ in range(3)],
        in_specs=[spec(i) for i in range(n) for _ in range(4)],
        out_specs=[spec(i) for i in range(n) for _ in range(3)],
        compiler_params=_params("arbitrary"),
    )(*[a for p in params for a in p])
    return [tuple(outs[3 * i:3 * i + 3]) for i in range(n)]


def adamw_small(params, total, place):
    n = len(params)

    def cut_gradients(total_ref, q, g_refs):
        g_norm, g_group_b, g_scale, g_gk_w, g_gk_b, g_head_norm, g_final = g_refs
        g_norm[0:1, :] = total_ref[0:1, :]
        g_norm[1:2, :] = total_ref[3:4, :]
        g_scale[...] = total_ref[1:2, :]
        g_final[...] = total_ref[5:6, :]
        g_gk_b[...] = total_ref[4:5, pl.ds(pl.multiple_of(q * 128, 128), 128)]
        for r in range(GATE_RANK):
            lanes = pl.ds(pl.multiple_of((r % 2) * KEY_W + q * 128, 128), 128)
            g_gk_w[r:r + 1, :] = total_ref[8 + r // 2:9 + r // 2, lanes]
        for k in range(N_CHIPS):
            @pl.when(q == k)
            def _(k=k):
                g_head_norm[...] = total_ref[6:7, 64 * k:64 * (k + 1)]
                for g in range(GROUPS):
                    g_group_b[g:g + 1, :] = total_ref[2:3, GROUP_DIM * g + 64 * k:GROUP_DIM * g + 64 * (k + 1)]

    def body(place_ref, total_ref, *refs):
        ins, outs = refs[:3 * n], refs[3 * n:]
        outs[4 * n][...] = total_ref[7:8, 0:1]
        cut_gradients(total_ref, place_ref[1], outs[0:4 * n:4])
        for k in range(n):
            w_ref, m_ref, v_ref = ins[3 * k:3 * k + 3]
            d, nm, nv = _adam_math(w_ref[...], outs[4 * k][...], m_ref[...], v_ref[...])
            outs[4 * k + 1][...] = d
            outs[4 * k + 2][...] = nm
            outs[4 * k + 3][...] = nv

    flat = [a for p in params for a in p]
    outs = pl.pallas_call(
        body, name="adamw_small",
        out_shape=[jax.ShapeDtypeStruct(p[0].shape, F32) for p in params for _ in range(4)]
                  + [jax.ShapeDtypeStruct((1, 1), F32)],
        in_specs=[pl.BlockSpec(memory_space=pltpu.SMEM)] + [VMEM_SPEC] * (1 + 3 * n),
        out_specs=[VMEM_SPEC] * (4 * n + 1),
    )(place, total, *flat)
    return [tuple(outs[4 * k:4 * k + 4]) for k in range(n)], outs[4 * n]


def matmul_tn(a, b, name, tile_n, place, by_column_tile=False, also_reduce=()):
    s, m = a.shape
    n = b.shape[1]
    steps = n // tile_n
    n_red = len(also_reduce)
    assert n_red == 0 or steps >= 2
    halves = [(g.shape[1] // 2, g.shape[2]) for g in also_reduce]
    if by_column_tile:
        out_shape = jax.ShapeDtypeStruct((steps, m, tile_n), F32)
        out_spec = pl.BlockSpec((None, m, tile_n), lambda j, place: (j, 0, 0))
    else:
        out_shape = jax.ShapeDtypeStruct((m, n), F32)
        out_spec = pl.BlockSpec((m, tile_n), lambda j, place: (0, j))

    def body(place_ref, a_ref, b_ref, *rest):
        rest = iter(rest)
        take = lambda count: [next(rest) for _ in range(count)]
        mine_refs, other_refs, (out_ref,) = take(n_red), take(n_red), take(1)
        own_refs, got_refs = take(n_red), take(n_red)
        send_bufs, their_bufs, sums_bufs = take(n_red), take(n_red), take(n_red)
        sems = list(rest)
        j = pl.program_id(0)
        q = place_ref[1]
        x, y, c = _position()

        def to_core(i):
            return pltpu.make_async_remote_copy(
                src_ref=send_bufs[i], dst_ref=their_bufs[i], send_sem=sems[0].at[i], recv_sem=sems[1].at[i],
                device_id=(x, y, 1 - c), device_id_type=MESH)

        def to_owners(i):
            return [pltpu.make_async_remote_copy(
                src_ref=sums_bufs[i].at[2 * chip[0] + chip[1]], dst_ref=got_refs[i].at[k],
                send_sem=sems[2].at[3 * i + k], recv_sem=sems[3].at[3 * i + k],
                device_id=(*chip, c), device_id_type=MESH) for k, chip in enumerate(_other_chips(x, y))]

        if n_red:
            @pl.when(j == 0)
            def _():
                for i in range(n_red):
                    send_bufs[i][...] = _bf(other_refs[i][...])
                    to_core(i).start()

            @pl.when(j == 1)
            def _():
                for i in range(n_red):
                    to_core(i).wait_recv()
                    sums_bufs[i][...] = _bf(mine_refs[i][...] + their_bufs[i][...].astype(F32))
                    own_refs[i][...] = mine_refs[i][q] + their_bufs[i][q].astype(F32)
                    for cp in to_owners(i):
                        cp.start()

        out_ref[...] = _tn(a_ref[...], b_ref[...])

        if n_red:
            @pl.when(j == steps - 1)
            def _():
                for i in range(n_red):
                    to_core(i).wait_send()
                    for cp in to_owners(i):
                        cp.wait()

    by_quarter = [(N_CHIPS, *h) for h in halves]
    outs = pl.pallas_call(
        body, name=name,
        grid_spec=pltpu.PrefetchScalarGridSpec(
            num_scalar_prefetch=1, grid=(steps,),
            in_specs=[pl.BlockSpec((s, m), lambda j, place: (0, 0)), pl.BlockSpec((s, tile_n), lambda j, place: (0, j))]
                     + [pl.BlockSpec(sh, lambda j, place: (0, place[0], 0)) for sh in by_quarter]
                     + [pl.BlockSpec(sh, lambda j, place: (0, 1 - place[0], 0)) for sh in by_quarter],
            out_specs=[out_spec] + [pl.BlockSpec(h, lambda j, place: (0, 0)) for h in halves] + _any_specs(n_red),
            scratch_shapes=[pltpu.VMEM(sh, BF16) for sh in by_quarter * 3]
                           + ([pltpu.SemaphoreType.DMA((n_red,)), pltpu.SemaphoreType.DMA((n_red,)),
                               pltpu.SemaphoreType.DMA((3 * n_red,)), pltpu.SemaphoreType.DMA((3 * n_red,))]
                              if n_red else [])),
        out_shape=[out_shape] + [jax.ShapeDtypeStruct(h, F32) for h in halves]
                  + [jax.ShapeDtypeStruct((N_CHIPS - 1, *h), BF16) for h in halves],
        compiler_params=_params("arbitrary"),
    )(place, a, b, *also_reduce, *also_reduce)
    return outs[0], list(zip(outs[1:1 + n_red], outs[1 + n_red:]))


ROW_TILE = 512


def _row_index(tile, rows):
    return tile * rows + lax.broadcasted_iota(jnp.int32, (rows, 1), 0)


def _inverse_counts(t_glob):
    return [1.0 / jnp.minimum(t_glob + 1, w).astype(F32) for w in POOL_WINDOWS]


def _sigmoid(z):
    return 1.0 / (1.0 + jnp.exp(-z))


def _trailing_sums(src, tmp, cols, window, rows):
    bufs = (src, tmp)
    span, level, start = 1, 0, 0
    while span < window:
        start += 8
        a, b = bufs[level % 2], bufs[(level + 1) % 2]
        n = HALO + rows - start
        b[start:start + n, cols] = a[start:start + n, cols] + a[start - span:start - span + n, cols]
        span, level = 2 * span, level + 1
    return bufs[level % 2][HALO:HALO + rows, cols]


def _leading_sums(src, tmp, cols, window, rows):
    bufs = (src, tmp)
    span, level, n = 1, 0, rows + HALO
    while span < window:
        n -= 8
        a, b = bufs[level % 2], bufs[(level + 1) % 2]
        b[0:n, cols] = a[0:n, cols] + a[span:span + n, cols]
        span, level = 2 * span, level + 1
    return bufs[level % 2][0:rows, cols]


def gather_in_background(step, last, out_refs, send_sems, recv_sems, finish):
    n = len(out_refs)
    x, y, c = _position()
    q = 2 * x + y
    chips = _other_chips(x, y)

    def copy(k, i, quarter, half, to):
        return _gather_copy(out_refs[i], send_sems, recv_sems, k * n + i, quarter, half, to)

    if not finish:
        @pl.when(step == 0)
        def _():
            for i in range(n):
                mine, _ = _halves(out_refs[i].shape[1], c)
                for j, chip in enumerate(chips):
                    copy(j, i, q, mine, (*chip, c)).start()

        @pl.when(step == last)
        def _():
            for j, chip in enumerate(chips):
                qj = 2 * chip[0] + chip[1]
                for i in range(n):
                    mine, _ = _halves(out_refs[i].shape[1], c)
                    copy(j, i, qj, mine, (x, y, c)).wait_recv()
                    copy(3 + j, i, qj, mine, (x, y, 1 - c)).start()
        return

    @pl.when(step == last)
    def _():
        for j, chip in enumerate(chips):
            qj = 2 * chip[0] + chip[1]
            for i in range(n):
                mine, other = _halves(out_refs[i].shape[1], c)
                copy(3 + j, i, qj, other, (x, y, c)).wait_recv()
                copy(j, i, q, mine, (x, y, c)).wait_send()
                copy(3 + j, i, qj, mine, (x, y, c)).wait_send()


def _group_matrix(gw_ref, g):
    rows = GROUP_DIM // N_CHIPS
    return jnp.concatenate([gw_ref[j, rows * g:rows * (g + 1), :] for j in range(N_CHIPS)], axis=0)


def pool_forward(x, w0, wpi, gw, gb, scale, wpo, later):
    s = x.shape[0]
    ts = ROW_TILE
    nt = s // ts
    assert nt >= 2
    n_later = len(later)

    def body(x_ref, w0_ref, wpi_ref, gw_ref, gb_ref, sc_ref, wpo_ref, *rest):
        rest = rest[n_later:]
        h1_ref, pooled_ref, gt_ref, n0_ref = rest[:4]
        later_refs = rest[4:4 + n_later]
        ubuf, tbuf, hist, send_sems, recv_sems = rest[4 + n_later:]
        i = pl.program_id(0)
        gather_in_background(i, nt - 1, later_refs, send_sems, recv_sems, finish=False)
        xv = x_ref[...]
        r = lax.rsqrt(jnp.mean(xv * xv, axis=-1, keepdims=True) + EPS)
        n0 = _bf(xv * r * w0_ref[...])
        n0_ref[...] = n0
        u = jnp.concatenate([_nn(n0, wpi_ref[0]), _nn(n0, wpi_ref[1])], axis=-1)
        gt = jnp.concatenate([_nn(n0, wpi_ref[2]), _nn(n0, wpi_ref[3])], axis=-1)
        gt_ref[...] = gt

        @pl.when(i == 0)
        def _():
            hist[...] = jnp.zeros_like(hist)

        ubuf[0:HALO, :] = hist[...]
        ubuf[HALO:HALO + ts, :] = u
        hist[...] = u[ts - HALO:, :]
        inv = _inverse_counts(_row_index(i, ts))
        mixed = []
        for g, w in enumerate(POOL_WINDOWS):
            cols = slice(g * GROUP_DIM, (g + 1) * GROUP_DIM)
            pooled = _bf(_trailing_sums(ubuf, tbuf, cols, w, ts) * inv[g] - u[:, cols])
            pooled_ref[:, cols] = pooled
            mixed.append(_nn(pooled, _group_matrix(gw_ref, g)))
        mixed = jnp.concatenate(mixed, axis=-1) + gb_ref[...]
        y = mixed * sc_ref[...] * (gt * _sigmoid(gt))
        h1_ref[...] = xv + _nn(_bf(y), wpo_ref[...])
        gather_in_background(i, nt - 1, later_refs, send_sems, recv_sems, finish=True)

    row = lambda cols: pl.BlockSpec((ts, cols), lambda i: (i, 0))
    outs = pl.pallas_call(
        body, name="pool_forward", grid=(nt,),
        out_shape=[jax.ShapeDtypeStruct((s, D), F32), jax.ShapeDtypeStruct((s, D), BF16),
                   jax.ShapeDtypeStruct((s, D), F32), jax.ShapeDtypeStruct((s, D), BF16)]
                  + [jax.ShapeDtypeStruct(a.shape, a.dtype) for a in later],
        in_specs=[row(D), _full((1, D)), _full((N_CHIPS, D, D // 2)), _full((GROUPS, GROUP_DIM, GROUP_DIM)),
                  _full((1, D)), _full((1, D)), _full((D, D))] + _any_specs(n_later),
        out_specs=[row(D), row(D), row(D), row(D)] + _any_specs(n_later),
        input_output_aliases={7 + k: 4 + k for k in range(n_later)},
        scratch_shapes=[pltpu.VMEM((HALO + ts, D), F32), pltpu.VMEM((HALO + ts, D), F32),
                        pltpu.VMEM((HALO, D), F32),
                        pltpu.SemaphoreType.DMA((6 * n_later,)), pltpu.SemaphoreType.DMA((6 * n_later,))],
        compiler_params=_params("arbitrary"),
    )(x, w0, wpi, gw, gb, scale, wpo, *later)
    return outs[:4], outs[4:]


def pool_backward(x, dh1, pooled, gt, w0, wpi, gw, gb, scale, wpo, chip_sums):
    s = x.shape[0]
    ts = ROW_TILE
    nt = s // ts
    n_sums = len(chip_sums)

    def body(x_ref, dh1_ref, pooled_ref, gt_ref, w0_ref, wpi_ref, gw_ref, gb_ref, sc_ref, wpo_ref, *rest):
        sum_refs, rest = rest[:n_sums], rest[n_sums:]
        dx_ref, dproj_ref, gpo_ref, ggw_ref, small_ref = rest[:5]
        got_refs = rest[5:5 + n_sums]
        ebuf, tbuf, ahead, send_sems, recv_sems = rest[5 + n_sums:]
        i = pl.program_id(0)
        copies = _scatter_copies(sum_refs, got_refs, send_sems, recv_sems)

        @pl.when(i == 0)
        def _():
            for cp in copies:
                cp.start()

        @pl.when(i == 0)
        def _():
            gpo_ref[...] = jnp.zeros_like(gpo_ref)
            ggw_ref[...] = jnp.zeros_like(ggw_ref)
            small_ref[...] = jnp.zeros_like(small_ref)
            ahead[...] = jnp.zeros_like(ahead)

        dh1 = dh1_ref[...]
        dh1_bf = _bf(dh1)
        gt = gt_ref[...]
        sc = sc_ref[...]
        dy = _nt(dh1_bf, wpo_ref[...])
        pooled_bf = []
        mixed = []
        for g in range(GROUPS):
            cols = slice(g * GROUP_DIM, (g + 1) * GROUP_DIM)
            pb = pooled_ref[:, cols]
            pooled_bf.append(pb)
            mixed.append(_nn(pb, _group_matrix(gw_ref, g)))
        mixed = jnp.concatenate(mixed, axis=-1) + gb_ref[...]
        sg = _sigmoid(gt)
        silu = gt * sg
        gpo_ref[...] += _tn(_bf(mixed * sc * silu), dh1_bf)
        dmixed = dy * sc * silu
        dgt = dy * mixed * sc * (sg * (1.0 + gt * (1.0 - sg)))
        dproj_ref[:, D:] = _bf(dgt)
        small_ref[1:2, :] += jnp.sum(dy * mixed * silu, axis=0, keepdims=True)
        small_ref[2:3, :] += jnp.sum(dmixed, axis=0, keepdims=True)

        inv = _inverse_counts(_row_index(nt - 1 - i, ts))
        rows_q = GROUP_DIM // N_CHIPS
        ebuf[ts:ts + HALO, :] = ahead[...]
        dpooled = []
        for g in range(GROUPS):
            cols = slice(g * GROUP_DIM, (g + 1) * GROUP_DIM)
            dm = _bf(dmixed[:, cols])
            ggw = _tn(pooled_bf[g], dm)
            for j in range(N_CHIPS):
                ggw_ref[j, rows_q * g:rows_q * (g + 1), :] += ggw[rows_q * j:rows_q * (j + 1), :]
            dp = _nt(dm, _group_matrix(gw_ref, g))
            dpooled.append(dp)
            ebuf[0:ts, cols] = dp * inv[g]
        ahead[...] = ebuf[0:HALO, :]
        du = []
        for g, w in enumerate(POOL_WINDOWS):
            cols = slice(g * GROUP_DIM, (g + 1) * GROUP_DIM)
            du.append(_leading_sums(ebuf, tbuf, cols, w, ts) - dpooled[g])
        du = _bf(jnp.concatenate(du, axis=-1))
        dproj_ref[:, :D] = du
        dgt_bf = _bf(dgt)
        half = D // 2
        dn0 = (_nt(du[:, :half], wpi_ref[0]) + _nt(du[:, half:], wpi_ref[1])
               + _nt(dgt_bf[:, :half], wpi_ref[2]) + _nt(dgt_bf[:, half:], wpi_ref[3]))

        xv = x_ref[...]
        r = lax.rsqrt(jnp.mean(xv * xv, axis=-1, keepdims=True) + EPS)
        xhat = xv * r
        small_ref[0:1, :] += jnp.sum(dn0 * xhat, axis=0, keepdims=True)
        dxh = dn0 * w0_ref[...]
        dx_ref[...] = dh1 + r * (dxh - xhat * jnp.mean(dxh * xhat, axis=-1, keepdims=True))

        @pl.when(i == nt - 1)
        def _():
            for cp in copies:
                cp.wait()

    row = lambda cols: pl.BlockSpec((ts, cols), lambda i: (nt - 1 - i, 0))
    outs = pl.pallas_call(
        body, name="pool_backward", grid=(nt,),
        out_shape=[jax.ShapeDtypeStruct((s, D), F32), jax.ShapeDtypeStruct((s, 2 * D), BF16),
                   jax.ShapeDtypeStruct((D, D), F32),
                   jax.ShapeDtypeStruct((GROUPS, GROUP_DIM, GROUP_DIM), F32),
                   jax.ShapeDtypeStruct((8, D), F32)] + _scatter_shapes(chip_sums),
        in_specs=[row(D), row(D), row(D), row(D), _full((1, D)), _full((N_CHIPS, D, D // 2)),
                  _full((GROUPS, GROUP_DIM, GROUP_DIM)), _full((1, D)), _full((1, D)), _full((D, D))]
                 + _any_specs(n_sums),
        out_specs=[row(D), row(2 * D), _full((D, D)), _full((GROUPS, GROUP_DIM, GROUP_DIM)), _full((8, D))]
                  + _any_specs(n_sums),
        scratch_shapes=[pltpu.VMEM((ts + HALO, D), F32), pltpu.VMEM((ts + HALO, D), F32),
                        pltpu.VMEM((HALO, D), F32),
                        pltpu.SemaphoreType.DMA((3 * n_sums,)), pltpu.SemaphoreType.DMA((3 * n_sums,))],
        compiler_params=_params("arbitrary"),
    )(x, dh1, pooled, gt, w0, wpi, gw, gb, scale, wpo, *chip_sums)
    return outs[:5], outs[5:]


def gla_project(h1, w1, wgi_q, wgk, bgk, later):
    s = h1.shape[0]
    ts = ROW_TILE
    nt = s // ts
    assert nt >= 2
    n_later = len(later)

    def body(h_ref, w1_ref, wq_ref, wgk_ref, bgk_ref, *rest):
        rest = rest[n_later:]
        qk_ref, v_ref, gate_ref, low_ref, cum_ref, n1_ref = rest[:6]
        later_refs = rest[6:6 + n_later]
        send_sems, recv_sems, wgi_ref = rest[6 + n_later:]
        gather_in_background(pl.program_id(0), nt - 1, later_refs, send_sems, recv_sems, finish=False)

        @pl.when(pl.program_id(0) == 0)
        def _():
            _assemble_gla_in(wq_ref, wgi_ref)

        hv = h_ref[...]
        r = lax.rsqrt(jnp.mean(hv * hv, axis=-1, keepdims=True) + EPS)
        n1 = _bf(hv * r * w1_ref[...])
        n1_ref[...] = n1
        qk_ref[...] = _nn(n1, wgi_ref[:, 0:2 * KEY_W])
        v_ref[...] = _bf(_nn(n1, wgi_ref[:, 2 * KEY_W:2 * KEY_W + D]))
        gate_ref[...] = _nn(n1, wgi_ref[:, 2 * KEY_W + D:GLA_MAIN])
        low = _bf(_nn(n1, wgi_ref[:, GLA_MAIN:]))
        low_ref[...] = low
        z = _nn(low, wgk_ref[...]) + bgk_ref[...]
        lg = (jnp.minimum(z, 0.0) - jnp.log(1.0 + jnp.exp(-jnp.abs(z)))) / GATE_NORM
        lower_f = _chunk_masks()[0].astype(F32)
        for r0 in range(0, ts, CHUNK):
            cum_ref[r0:r0 + CHUNK, :] = _nn_exact(lower_f, lg[r0:r0 + CHUNK, :])
        gather_in_background(pl.program_id(0), nt - 1, later_refs, send_sems, recv_sems, finish=True)

    row = lambda cols: pl.BlockSpec((ts, cols), lambda i: (i, 0))
    outs = pl.pallas_call(
        body, name="gla_project", grid=(nt,),
        out_shape=[jax.ShapeDtypeStruct((s, D), F32), jax.ShapeDtypeStruct((s, D), BF16),
                   jax.ShapeDtypeStruct((s, D), F32), jax.ShapeDtypeStruct((s, RANK_PAD), BF16),
                   jax.ShapeDtypeStruct((s, KEY_W), F32), jax.ShapeDtypeStruct((s, D), BF16)]
                  + [jax.ShapeDtypeStruct(a.shape, a.dtype) for a in later],
        in_specs=[row(D), _full((1, D)), _full((N_CHIPS, D, GLA_IN_QUARTER)),
                  _full((RANK_PAD, KEY_W)), _full((1, KEY_W))] + _any_specs(n_later),
        out_specs=[row(D), row(D), row(D), row(RANK_PAD), row(KEY_W), row(D)] + _any_specs(n_later),
        input_output_aliases={5 + k: 6 + k for k in range(n_later)},
        scratch_shapes=[pltpu.SemaphoreType.DMA((6 * n_later,)), pltpu.SemaphoreType.DMA((6 * n_later,)),
                        pltpu.VMEM((D, GLA_MAIN + RANK_PAD), BF16)],
        compiler_params=_params("arbitrary"),
    )(h1, w1, wgi_q, wgk, bgk, *later)
    return outs[:6], outs[6:]


def _assemble_gla_in(wq_ref, wfull):
    pad = jnp.zeros((CAST_ROWS, GLA_MAIN + RANK_PAD - GLA_IN), BF16)
    for r0 in range(0, D, CAST_ROWS):
        rows = slice(r0, r0 + CAST_ROWS)
        wfull[rows, :] = jnp.concatenate([wq_ref[q, rows, :] for q in range(N_CHIPS)] + [pad], axis=1)


GLA_BLOCK = 512
CHUNKS_PER_BLOCK = GLA_BLOCK // CHUNK


def _chunk_masks():
    t = lax.broadcasted_iota(jnp.int32, (CHUNK, CHUNK), 0)
    u = lax.broadcasted_iota(jnp.int32, (CHUNK, CHUNK), 1)
    return t >= u, t <= u


def _gla_chunk_terms(q, cum):
    ep = jnp.exp(cum)
    en = jnp.exp(-cum)
    qs = q * (HEAD_K ** -0.5)
    last = cum[CHUNK - 1:CHUNK, :]
    ed = jnp.exp(last - cum)
    dec = jnp.exp(last)
    return ep, en, qs, ed, dec


def gla_forward(qk, v, cum):
    s = qk.shape[0]
    nb = s // GLA_BLOCK
    nc = s // CHUNK

    def body(q_ref, k_ref, v_ref, cum_ref, o_ref, st_ref, sc_ref, state):
        @pl.when(pl.program_id(0) == 0)
        def _():
            state[...] = jnp.zeros_like(state)

        lower, _ = _chunk_masks()

        def chunk(cc, carry):
            rows = pl.ds(pl.multiple_of(cc * CHUNK, CHUNK), CHUNK)
            for h in range(HEADS):
                kc = slice(h * HEAD_K, (h + 1) * HEAD_K)
                vc = slice(h * HEAD_V, (h + 1) * HEAD_V)
                q = q_ref[rows, kc]
                k = k_ref[rows, kc]
                v = v_ref[rows, vc]
                ep, en, qs, ed, dec = _gla_chunk_terms(q, cum_ref[rows, kc])
                a = _bf(qs * ep)
                fwd = _nt(a, _bf(k * en))
                bwd = _nt(_bf(qs * en), _bf(k * ep))
                scores = _bf(jnp.where(lower, fwd, bwd))
                sc_ref[rows, h * CHUNK:(h + 1) * CHUNK] = scores
                st = state[h]
                st_ref[cc, h] = st
                o_ref[rows, vc] = _nn(scores, v) + _nt(a, _bf(st))
                state[h] = st * dec + _tn(v, _bf(k * ed))
            return carry

        lax.fori_loop(0, CHUNKS_PER_BLOCK, chunk, 0, unroll=True)

    return pl.pallas_call(
        body, name="gla_forward", grid=(nb,),
        out_shape=(jax.ShapeDtypeStruct((s, D), F32),
                   jax.ShapeDtypeStruct((nc, HEADS, HEAD_V, HEAD_K), F32),
                   jax.ShapeDtypeStruct((s, HEADS * CHUNK), BF16)),
        in_specs=[pl.BlockSpec((GLA_BLOCK, KEY_W), lambda i: (i, 0)),
                  pl.BlockSpec((GLA_BLOCK, KEY_W), lambda i: (i, 1)),
                  pl.BlockSpec((GLA_BLOCK, D), lambda i: (i, 0)),
                  pl.BlockSpec((GLA_BLOCK, KEY_W), lambda i: (i, 0))],
        out_specs=(pl.BlockSpec((GLA_BLOCK, D), lambda i: (i, 0)),
                   pl.BlockSpec((CHUNKS_PER_BLOCK, HEADS, HEAD_V, HEAD_K), lambda i: (i, 0, 0, 0)),
                   pl.BlockSpec((GLA_BLOCK, HEADS * CHUNK), lambda i: (i, 0))),
        scratch_shapes=[pltpu.VMEM((HEADS, HEAD_V, HEAD_K), F32)],
        compiler_params=_params("arbitrary"),
    )(qk, qk, v, cum)


def gla_backward(qk, v, cum, do, states, scores):
    s = qk.shape[0]
    nb = s // GLA_BLOCK

    def body(q_ref, k_ref, v_ref, cum_ref, do_ref, st_ref, sc_ref, dq_ref, dk_ref, dv_ref, dcum_ref, dstate):
        @pl.when(pl.program_id(0) == 0)
        def _():
            dstate[...] = jnp.zeros_like(dstate)

        lower, _ = _chunk_masks()
        is_last = lax.broadcasted_iota(jnp.int32, (CHUNK, HEAD_K), 0) == CHUNK - 1

        def chunk(step, carry):
            cc = CHUNKS_PER_BLOCK - 1 - step
            rows = pl.ds(pl.multiple_of(cc * CHUNK, CHUNK), CHUNK)
            for h in range(HEADS):
                kc = slice(h * HEAD_K, (h + 1) * HEAD_K)
                vc = slice(h * HEAD_V, (h + 1) * HEAD_V)
                q = q_ref[rows, kc]
                k = k_ref[rows, kc]
                v = v_ref[rows, vc]
                do_c = do_ref[rows, vc]
                ep, en, qs, ed, dec = _gla_chunk_terms(q, cum_ref[rows, kc])
                a = _bf(qs * ep)
                b = _bf(k * en)
                c = _bf(qs * en)
                dk_dec = _bf(k * ep)
                kd = _bf(k * ed)
                scores = sc_ref[rows, h * CHUNK:(h + 1) * CHUNK]
                st = st_ref[cc, h]
                dst = dstate[h]
                dst_bf = _bf(dst)

                dscores = _nt(do_c, v)
                dfwd = _bf(jnp.where(lower, dscores, 0.0))
                dbwd = _bf(jnp.where(lower, 0.0, dscores))
                dv_ref[rows, vc] = _bf(_tn(scores, do_c) + _nt(kd, dst_bf))
                da = _nn(dfwd, b) + _nn(do_c, _bf(st))
                db = _tn(dfwd, a)
                dc = _nn(dbwd, dk_dec)
                ddk = _tn(dbwd, c)
                dkd = _nn(v, dst_bf)
                ddec = jnp.sum(dst * st, axis=0, keepdims=True)
                dstate[h] = dst * dec + _tn(do_c, a)

                m = dkd * k * ed
                dq_ref[rows, kc] = _bf((da * ep + dc * en) * (HEAD_K ** -0.5))
                dk_ref[rows, kc] = _bf(db * en + ddk * ep + dkd * ed)
                dcum = (da * qs + ddk * k) * ep - (db * k + dc * qs) * en - m
                dlast = jnp.sum(m, axis=0, keepdims=True) + ddec * dec
                dcum_ref[rows, kc] = dcum + jnp.where(is_last, dlast, 0.0)
            return carry

        lax.fori_loop(0, CHUNKS_PER_BLOCK, chunk, 0, unroll=True)

    rev = lambda cols, col_block: pl.BlockSpec((GLA_BLOCK, cols), lambda i: (nb - 1 - i, col_block))
    return pl.pallas_call(
        body, name="gla_backward", grid=(nb,),
        out_shape=(jax.ShapeDtypeStruct((s, KEY_W), BF16), jax.ShapeDtypeStruct((s, KEY_W), BF16),
                   jax.ShapeDtypeStruct((s, D), BF16), jax.ShapeDtypeStruct((s, KEY_W), F32)),
        in_specs=[rev(KEY_W, 0), rev(KEY_W, 1), rev(D, 0), rev(KEY_W, 0), rev(D, 0),
                  pl.BlockSpec((CHUNKS_PER_BLOCK, HEADS, HEAD_V, HEAD_K), lambda i: (nb - 1 - i, 0, 0, 0)),
                  rev(HEADS * CHUNK, 0)],
        out_specs=(rev(KEY_W, 0), rev(KEY_W, 0), rev(D, 0), rev(KEY_W, 0)),
        scratch_shapes=[pltpu.VMEM((HEADS, HEAD_V, HEAD_K), F32)],
        compiler_params=_params("arbitrary"),
    )(qk, qk, v, cum, do, states, scores)


def head_and_loss(o, gate, h1, target, hw, wgo, wf):
    s = o.shape[0]
    ts = ROW_TILE

    def body(o_ref, gate_ref, h1_ref, tgt_ref, hw_ref, wgo_ref, wf_ref,
             dh2_ref, do_ref, dgate_ref, ggo_ref, small_ref):
        @pl.when(pl.program_id(0) == 0)
        def _():
            ggo_ref[...] = jnp.zeros_like(ggo_ref)
            small_ref[...] = jnp.zeros_like(small_ref)

        gate = gate_ref[...]
        hw = hw_ref[...]
        sg = _sigmoid(gate)
        silu = gate * sg
        ohat, ro = [], []
        for h in range(HEADS):
            oh = o_ref[:, h * HEAD_V:(h + 1) * HEAD_V]
            rh = lax.rsqrt(jnp.mean(oh * oh, axis=-1, keepdims=True) + EPS)
            ro.append(rh)
            ohat.append(oh * rh)
        ohat = jnp.concatenate(ohat, axis=-1)
        on = ohat * hw
        y2 = _bf(on * silu)
        h2 = h1_ref[...] + _nn(y2, wgo_ref[...])
        rf = lax.rsqrt(jnp.mean(h2 * h2, axis=-1, keepdims=True) + EPS)
        h2hat = h2 * rf
        wf = wf_ref[...]
        diff = h2hat * wf - tgt_ref[...]
        small_ref[2:3, :] += jnp.zeros((1, D), F32) + 0.5 * jnp.sum(diff * diff) / D
        dout = diff / D
        small_ref[0:1, :] += jnp.sum(dout * h2hat, axis=0, keepdims=True)
        dxh = dout * wf
        dh2 = rf * (dxh - h2hat * jnp.mean(dxh * h2hat, axis=-1, keepdims=True))
        dh2_ref[...] = dh2
        dh2_bf = _bf(dh2)
        ggo_ref[...] += _tn(y2, dh2_bf)
        dy2 = _nt(dh2_bf, wgo_ref[...])
        don = dy2 * silu
        dgate_ref[...] = _bf(dy2 * on * (sg * (1.0 + gate * (1.0 - sg))))
        ghw = jnp.sum(don * ohat, axis=0, keepdims=True)
        small_ref[1:2, 0:HEAD_V] += sum(ghw[:, h * HEAD_V:(h + 1) * HEAD_V] for h in range(HEADS))
        dohat = don * hw
        for h in range(HEADS):
            cols = slice(h * HEAD_V, (h + 1) * HEAD_V)
            oh, dh = ohat[:, cols], dohat[:, cols]
            do_ref[:, cols] = _bf(ro[h] * (dh - oh * jnp.mean(dh * oh, axis=-1, keepdims=True)))

    row = lambda cols: pl.BlockSpec((ts, cols), lambda i: (i, 0))
    act = jax.ShapeDtypeStruct((s, D), F32)
    act_bf = jax.ShapeDtypeStruct((s, D), BF16)
    return pl.pallas_call(
        body, name="head_and_loss", grid=(s // ts,),
        out_shape=(act, act_bf, act_bf, jax.ShapeDtypeStruct((D, D), F32), jax.ShapeDtypeStruct((8, D), F32)),
        in_specs=[row(D), row(D), row(D), row(D),
                  _full((1, D)), _full((D, D)), _full((1, D))],
        out_specs=(row(D), row(D), row(D), _full((D, D)), _full((8, D))),
        compiler_params=_params("arbitrary"),
    )(o, gate, h1, target, hw, wgo, wf)


def gla_project_backward(dq, dk, dv, dgate, dcum, low, h1, dh2, w1, wgi_q, wgk, bgk):
    s = h1.shape[0]
    ts = ROW_TILE

    def body(dq_ref, dk_ref, dv_ref, dgate_ref, dcum_ref, low_ref, h1_ref, dh2_ref, w1_ref,
             wq_ref, wgk_ref, bgk_ref, dh1_ref, dproj_ref, ggk_ref, small_ref, wgi_ref):
        @pl.when(pl.program_id(0) == 0)
        def _():
            ggk_ref[...] = jnp.zeros_like(ggk_ref)
            small_ref[...] = jnp.zeros_like(small_ref)
            _assemble_gla_in(wq_ref, wgi_ref)

        low = low_ref[...]
        z = _nn(low, wgk_ref[...]) + bgk_ref[...]
        upper_f = _chunk_masks()[1].astype(F32)
        dlg = jnp.concatenate([_nn_exact(upper_f, dcum_ref[r0:r0 + CHUNK, :]) for r0 in range(0, ts, CHUNK)],
                              axis=0)
        dz = dlg * (1.0 / GATE_NORM) * _sigmoid(-z)
        dz_bf = _bf(dz)
        ggk_ref[...] += _tn(low, dz_bf)
        small_ref[1:2, 0:KEY_W] += jnp.sum(dz, axis=0, keepdims=True)
        dlow = _bf(_nt(dz_bf, wgk_ref[...]))
        dproj_ref[:, GLA_MAIN:] = dlow
        dn1 = _nt(dlow, wgi_ref[:, GLA_MAIN:])
        for ref, lo, hi in ((dq_ref, 0, KEY_W), (dk_ref, KEY_W, 2 * KEY_W),
                            (dv_ref, 2 * KEY_W, 2 * KEY_W + D), (dgate_ref, 2 * KEY_W + D, GLA_MAIN)):
            piece = ref[...]
            dproj_ref[:, lo:hi] = piece
            dn1 = dn1 + _nt(piece, wgi_ref[:, lo:hi])
        hv = h1_ref[...]
        r = lax.rsqrt(jnp.mean(hv * hv, axis=-1, keepdims=True) + EPS)
        hhat = hv * r
        small_ref[0:1, :] += jnp.sum(dn1 * hhat, axis=0, keepdims=True)
        dxh = dn1 * w1_ref[...]
        dh1_ref[...] = dh2_ref[...] + r * (dxh - hhat * jnp.mean(dxh * hhat, axis=-1, keepdims=True))

    row = lambda cols: pl.BlockSpec((ts, cols), lambda i: (i, 0))
    return pl.pallas_call(
        body, name="gla_project_backward", grid=(s // ts,),
        out_shape=(jax.ShapeDtypeStruct((s, D), F32), jax.ShapeDtypeStruct((s, GLA_MAIN + RANK_PAD), BF16),
                   jax.ShapeDtypeStruct((RANK_PAD, KEY_W), F32),
                   jax.ShapeDtypeStruct((8, D), F32)),
        in_specs=[row(KEY_W), row(KEY_W), row(D), row(D), row(KEY_W), row(RANK_PAD), row(D), row(D),
                  _full((1, D)), _full((N_CHIPS, D, GLA_IN_QUARTER)), _full((RANK_PAD, KEY_W)),
                  _full((1, KEY_W))],
        out_specs=(row(D), row(GLA_MAIN + RANK_PAD), _full((RANK_PAD, KEY_W)), _full((8, D))),
        scratch_shapes=[pltpu.VMEM((D, GLA_MAIN + RANK_PAD), BF16)],
        compiler_params=_params("arbitrary"),
    )(dq, dk, dv, dgate, dcum, low, h1, dh2, w1, wgi_q, wgk, bgk)


def local_gradients(xs, target, w0, w1, wf, wpi, gw, gb, scale, wpo, gla_quarters, wgk, bgk, hw_tiled, place):
    wgi_q, wgo_q = gla_quarters
    (h1, pooled, gt, n0), (wgi_q,) = pool_forward(xs, w0, wpi, gw, gb, scale, wpo, [wgi_q])
    (qk, v, gate, low, cum, n1), (wgo_q,) = gla_project(h1, w1, wgi_q, wgk, bgk, [wgo_q])
    wgo = wgo_q.reshape(D, D)
    o, states, scores = gla_forward(qk, v, cum)

    dh2, do, dgate, g_gla_out, small_top = head_and_loss(o, gate, h1, target, hw_tiled, wgo, wf)
    dq, dk, dv, dcum = gla_backward(qk, v, cum, do, states, scores)
    dh1, dproj, g_gk_pad, small_gla = gla_project_backward(
        dq, dk, dv, dgate, dcum, low, h1, dh2, w1, wgi_q, wgk, bgk)
    g_gla_in, _ = matmul_tn(n1, dproj, "grad_gla_in", (GLA_MAIN + RANK_PAD) // 5, place)

    gla_sums = add_halves([g_gla_in, g_gla_out.reshape(N_CHIPS, D // N_CHIPS, D)], place, "add_halves_gla")
    (dx, dpool, g_pool_out, g_group_w, small_pool), gla_got = pool_backward(
        xs, dh1, pooled, gt, w0, wpi, gw, gb, scale, wpo, [b for _, b in gla_sums])
    g_pool_in, mix = matmul_tn(n0, dpool, "grad_pool_in", D // 2, place, by_column_tile=True,
                               also_reduce=[g_group_w, g_pool_out.reshape(N_CHIPS, D // N_CHIPS, D)])

    reduced, total = join_halves(
        g_pool_in, place, [own for own, _ in mix] + [f for f, _ in gla_sums], [got for _, got in mix] + list(gla_got),
        small_pool, small_gla, small_top, g_gk_pad)
    return dx, reduced, total


def kernel(x, norm_w, pool_in_w, pool_group_w, pool_group_b, pool_scale, pool_out_w, gla_in_w, gla_gk_w, gla_gk_b, gla_head_norm_w, gla_out_w, final_norm_w, loss_target, m_norm_w, m_pool_in_w, m_pool_group_w, m_pool_group_b, m_pool_scale, m_pool_out_w, m_gla_in_w, m_gla_gk_w, m_gla_gk_b, m_gla_head_norm_w, m_gla_out_w, m_final_norm_w, v_norm_w, v_pool_in_w, v_pool_group_w, v_pool_group_b, v_pool_scale, v_pool_out_w, v_gla_in_w, v_gla_gk_w, v_gla_gk_b, v_gla_head_norm_w, v_gla_out_w, v_final_norm_w):
    xs = x[0]
    target = loss_target[0]
    q_chip = 2 * lax.axis_index("x") + lax.axis_index("y")
    place = jnp.stack([lax.axis_index("c"), q_chip]).astype(jnp.int32)

    (wpi, gw_q, wpo_q, wgi_q, wgo_q), (bgk, hw_tiled, gb, wgk) = allgather_weights(
        [pool_in_w[0], pool_group_w[0].reshape(GROUP_DIM, GROUP_DIM), pool_out_w[0], gla_in_w[0], gla_out_w[0]],
        exchange=(True, True, True, False, False),
        smalls=[gla_gk_b, gla_head_norm_w, pool_group_b[0], gla_gk_w[0]])
    wpo = wpo_q.reshape(D, D)

    w0 = norm_w[0:1]
    w1 = norm_w[1:2]
    wf = final_norm_w.reshape(1, D)

    dx, reduced, total = local_gradients(
        xs, target, w0, w1, wf, wpi, gw_q, gb, pool_scale, wpo, [wgi_q, wgo_q], wgk, bgk, hw_tiled, place)
    r_pool_in, r_group_w, r_pool_out, r_gla_in, r_gla_out = reduced
    r_group_w = r_group_w.reshape(GROUPS, 64, GROUP_DIM)

    turn = lambda a: jnp.transpose(a, (2, 0, 1))
    back = lambda a: jnp.transpose(a, (1, 2, 0))
    as2d = lambda a, w: a.reshape(-1, w.shape[-1])
    big_names = ("pool_in_w", "pool_group_w", "pool_out_w", "gla_in_w", "gla_out_w")
    big_args = [(pool_in_w, r_pool_in[None], m_pool_in_w, v_pool_in_w),
                (pool_group_w, r_group_w[None], m_pool_group_w, v_pool_group_w),
                (pool_out_w, r_pool_out[None], m_pool_out_w, v_pool_out_w),
                (gla_in_w, r_gla_in[None], m_gla_in_w, v_gla_in_w),
                (gla_out_w, r_gla_out[None], m_gla_out_w, v_gla_out_w)]
    to_kernel = lambda n, a, w: turn(a) if n == "gla_in_w" else as2d(a, w)
    from_kernel = lambda n, a, w: back(a) if n == "gla_in_w" else a.reshape(w.shape)
    big_in = [tuple(to_kernel(n, a, p[0]) for a in p) for n, p in zip(big_names, big_args)]
    big_out = adamw(big_in, "adamw")
    big = {n: (from_kernel(n, i[1], p[0]),) + tuple(from_kernel(n, o, p[0]) for o in out)
           for n, p, i, out in zip(big_names, big_args, big_in, big_out)}

    small_names = ("norm_w", "pool_group_b", "pool_scale", "gla_gk_w", "gla_gk_b", "gla_head_norm_w",
                   "final_norm_w")
    small_args = [(norm_w, m_norm_w, v_norm_w),
                  (pool_group_b, m_pool_group_b, v_pool_group_b),
                  (pool_scale, m_pool_scale, v_pool_scale),
                  (gla_gk_w, m_gla_gk_w, v_gla_gk_w),
                  (gla_gk_b, m_gla_gk_b, v_gla_gk_b),
                  (gla_head_norm_w, m_gla_head_norm_w, v_gla_head_norm_w),
                  (final_norm_w, m_final_norm_w, v_final_norm_w)]
    small_out, loss = adamw_small([tuple(as2d(a, p[0]) for a in p) for p in small_args], total, place)
    small = {n: tuple(o.reshape(p[0].shape) for o in out) for n, p, out in zip(small_names, small_args, small_out)}
    results = [
        small["norm_w"],
        big["pool_in_w"],
        big["pool_group_w"],
        small["pool_group_b"],
        small["pool_scale"],
        big["pool_out_w"],
        big["gla_in_w"],
        small["gla_gk_w"],
        small["gla_gk_b"],
        small["gla_head_norm_w"],
        big["gla_out_w"],
        small["final_norm_w"],
    ]
    grads, deltas, new_m, new_v = zip(*results)
    return (loss.reshape(()), dx[None], *grads, *deltas, *new_m, *new_v)
```

```python
import jax
import jax.numpy as jnp
from jax import lax
from jax.experimental import pallas as pl
from jax.experimental.pallas import tpu as pltpu

F32 = jnp.float32
BF16 = jnp.bfloat16
MESH = pl.DeviceIdType.MESH

D = 1024
POOL_WINDOWS = (2, 4, 8, 16)
GROUPS = 4
GROUP_DIM = 256
HEADS = 4
HEAD_K = 128
HEAD_V = 256
KEY_W = 512
CHUNK = 64
GATE_RANK = 16
GATE_NORM = 16.0
GLA_IN = 3088
GLA_MAIN = 3072
RANK_PAD = 128
EPS = 1e-6
HALO = 32

ADAM_LR = 0.001
ADAM_B1 = 0.9
ADAM_B2 = 0.999
ADAM_EPS = 1e-08
ADAM_WD = 0.01
ADAM_STEP = 10

N_CHIPS = 4
N_DEV = 8
GLA_IN_QUARTER = GLA_IN // N_CHIPS

VMEM_LIMIT = 56 * 1024 * 1024


def _nn(a, b):
    return lax.dot_general(a, b, (((1,), (0,)), ((), ())), preferred_element_type=F32)


def _nt(a, b):
    return lax.dot_general(a, b, (((1,), (1,)), ((), ())), preferred_element_type=F32)


def _tn(a, b):
    return lax.dot_general(a, b, (((0,), (0,)), ((), ())), preferred_element_type=F32)


def _nn_exact(a, b):
    return lax.dot_general(a, b, (((1,), (0,)), ((), ())), preferred_element_type=F32,
                           precision=lax.Precision.HIGHEST)


def _bf(a):
    return a.astype(BF16)


def _params(*sem):
    return pltpu.CompilerParams(dimension_semantics=sem, vmem_limit_bytes=VMEM_LIMIT)


def _full(shape):
    return pl.BlockSpec(shape, lambda i: (0,) * len(shape))


def _position():
    return lax.axis_index("x"), lax.axis_index("y"), lax.axis_index("c")


def _gather_small(in_ref, all_ref, send_sems, recv_sems, local_sem):
    x, y, c = _position()
    me = 4 * x + 2 * y + c
    mine = pltpu.make_async_copy(in_ref, all_ref.at[me], local_sem)
    sends = []
    for k in range(N_DEV - 1):
        fx, fy, fc = (k + 1) >> 2 & 1, (k + 1) >> 1 & 1, (k + 1) & 1
        sends.append(pltpu.make_async_remote_copy(
            src_ref=in_ref, dst_ref=all_ref.at[me],
            send_sem=send_sems.at[k], recv_sem=recv_sems.at[k],
            device_id=(x ^ fx, y ^ fy, c ^ fc), device_id_type=MESH))

    def start():
        mine.start()
        for cp in sends:
            cp.start()

    def wait():
        for k in range(N_DEV - 1):
            fx, fy, fc = (k + 1) >> 2 & 1, (k + 1) >> 1 & 1, (k + 1) & 1
            src_dev = 4 * (x ^ fx) + 2 * (y ^ fy) + (c ^ fc)
            pltpu.make_async_remote_copy(
                src_ref=in_ref, dst_ref=all_ref.at[src_dev],
                send_sem=send_sems.at[k], recv_sem=recv_sems.at[k],
                device_id=(x, y, c), device_id_type=MESH).wait_recv()
        for cp in sends:
            cp.wait_send()
        mine.wait()

    return start, wait


SMALL_SEMS = [pltpu.SemaphoreType.DMA((N_DEV - 1,)), pltpu.SemaphoreType.DMA((N_DEV - 1,)),
              pltpu.SemaphoreType.DMA]
VMEM_SPEC = pl.BlockSpec(memory_space=pltpu.VMEM)


def _other_chips(x, y):
    return [(1 - x, y), (x, 1 - y), (1 - x, 1 - y)]


def _any_specs(n):
    return [pl.BlockSpec(memory_space=pl.ANY)] * n


def _halves(rows, c):
    half = rows // 2
    return pl.ds(c * half, half), pl.ds((1 - c) * half, half)


CAST_ROWS = 256


def _gather_copy(out_ref, send_sems, recv_sems, k, quarter, half, to, src=None):
    dst = out_ref.at[quarter, half]
    return pltpu.make_async_remote_copy(
        src_ref=dst if src is None else src, dst_ref=dst,
        send_sem=send_sems.at[k], recv_sem=recv_sems.at[k], device_id=to, device_id_type=MESH)


SMALL_IN_ROWS = 24


def allgather_weights(quarters, exchange, smalls):
    n = len(quarters)
    shapes = [w.shape for w in quarters]
    moved = [i for i in range(n) if exchange[i]]

    def body(*refs):
        w_refs, (gkb_ref, hnw_ref, gb_ref, gkw_ref) = refs[:n], refs[n:n + 4]
        out_refs, (bgk_ref, hw_ref, gbias_ref, wgk_ref) = refs[n + 4:2 * n + 4], refs[2 * n + 4:2 * n + 8]
        refs = refs[2 * n + 8:]
        f32_bufs, bf_bufs = refs[:n], refs[n:2 * n]
        send_sems, recv_sems, local_sems, small_ref, small_all_ref = refs[2 * n:2 * n + 5]
        small_ref[...] = jnp.zeros_like(small_ref)
        small_ref[0:1, :] = gkb_ref[...]
        small_ref[1:2, 0:64] = hnw_ref[...]
        small_ref[2:2 + GROUPS, 0:64] = gb_ref[...]
        small_ref[8:8 + GATE_RANK, :] = gkw_ref[...]
        start_small, wait_small = _gather_small(small_ref, small_all_ref, *refs[2 * n + 5:])
        start_small()
        x, y, c = _position()
        q = 2 * x + y
        sibling = (x, y, 1 - c)
        chips = _other_chips(x, y)

        def copy(k, i, quarter, half, to, src=None):
            return _gather_copy(out_refs[i], send_sems, recv_sems, k * n + i, quarter, half, to, src)

        loads = [pltpu.make_async_copy(w_refs[i], f32_bufs[i], local_sems.at[i]) for i in range(n)]
        for cp in loads:
            cp.start()
        keeps, sends = [], []
        for i in range(n):
            loads[i].wait()
            for r0 in range(0, shapes[i][0], CAST_ROWS):
                bf_bufs[i][r0:r0 + CAST_ROWS, :] = _bf(f32_bufs[i][r0:r0 + CAST_ROWS, :])
            keep = pltpu.make_async_copy(bf_bufs[i], out_refs[i].at[q], local_sems.at[n + i])
            keep.start()
            keeps.append(keep)
            if not exchange[i]:
                continue
            mine, _ = _halves(shapes[i][0], c)
            for j, chip in enumerate(chips):
                cp = copy(j, i, q, mine, (*chip, c), src=bf_bufs[i].at[mine])
                cp.start()
                sends.append(cp)
        for j, chip in enumerate(chips):
            qj = 2 * chip[0] + chip[1]
            for i in moved:
                mine, _ = _halves(shapes[i][0], c)
                copy(j, i, qj, mine, (x, y, c)).wait_recv()
                cp = copy(3 + j, i, qj, mine, sibling)
                cp.start()
                sends.append(cp)
        for j, chip in enumerate(chips):
            qj = 2 * chip[0] + chip[1]
            for i in moved:
                _, other = _halves(shapes[i][0], c)
                copy(3 + j, i, qj, other, (x, y, c)).wait_recv()
        wait_small()
        wgk_ref[...] = jnp.zeros_like(wgk_ref)
        for j in range(N_CHIPS):
            block = small_all_ref.at[2 * j]
            bgk_ref[:, 128 * j:128 * (j + 1)] = block[0:1, :]
            for h in range(HEADS):
                hw_ref[:, HEAD_V * h + 64 * j:HEAD_V * h + 64 * (j + 1)] = block[1:2, 0:64]
            for g in range(GROUPS):
                gbias_ref[:, GROUP_DIM * g + 64 * j:GROUP_DIM * g + 64 * (j + 1)] = block[2 + g:3 + g, 0:64]
            wgk_ref[0:GATE_RANK, 128 * j:128 * (j + 1)] = _bf(block[8:8 + GATE_RANK, :])
        for cp in sends:
            cp.wait_send()
        for cp in keeps:
            cp.wait()

    outs = pl.pallas_call(
        body, name="allgather_weights",
        out_shape=[jax.ShapeDtypeStruct((N_CHIPS, *s), BF16) for s in shapes]
                  + [jax.ShapeDtypeStruct((1, KEY_W), F32), jax.ShapeDtypeStruct((1, D), F32),
                     jax.ShapeDtypeStruct((1, D), F32), jax.ShapeDtypeStruct((RANK_PAD, KEY_W), BF16)],
        in_specs=_any_specs(n) + [VMEM_SPEC] * 4, out_specs=_any_specs(n) + [VMEM_SPEC] * 4,
        scratch_shapes=([pltpu.VMEM(s, F32) for s in shapes] + [pltpu.VMEM(s, BF16) for s in shapes]
                        + [pltpu.SemaphoreType.DMA((6 * n,)), pltpu.SemaphoreType.DMA((6 * n,)),
                           pltpu.SemaphoreType.DMA((2 * n,)), pltpu.VMEM((SMALL_IN_ROWS, 128), F32),
                           pltpu.VMEM((N_DEV, SMALL_IN_ROWS, 128), F32)] + SMALL_SEMS),
        compiler_params=pltpu.CompilerParams(vmem_limit_bytes=VMEM_LIMIT),
    )(*quarters, *smalls)
    return outs[:n], outs[n:]


def _scatter_copies(b_refs, got_refs, send_sems, recv_sems):
    n = len(b_refs)
    x, y, c = _position()
    copies = []
    for j, chip in enumerate(_other_chips(x, y)):
        qj = 2 * chip[0] + chip[1]
        for i in range(n):
            copies.append(pltpu.make_async_remote_copy(
                src_ref=b_refs[i].at[qj], dst_ref=got_refs[i].at[j],
                send_sem=send_sems.at[j * n + i], recv_sem=recv_sems.at[j * n + i],
                device_id=(*chip, c), device_id_type=MESH))
    return copies


def _scatter_shapes(chip_sums):
    return [jax.ShapeDtypeStruct((N_CHIPS - 1, *b.shape[1:]), BF16) for b in chip_sums]


ADD_ROWS = 512
ADD_HALVES_ROWS = 128


def _spans(counts):
    starts, total = [], 0
    for count in counts:
        starts.append(total)
        total += count
    return starts, total


def _local_step(t, start, count):
    return jnp.clip(t - start, 0, count - 1)


def add_halves(grads, place, name):
    n = len(grads)
    whole = [len(g.shape) == 2 for g in grads]
    halves = [g.shape[-2] // 2 for g in grads]
    cols = [GLA_IN_QUARTER if w else g.shape[-1] for g, w in zip(grads, whole)]
    rbs = [min(ADD_HALVES_ROWS, h) for h in halves]
    counts = [h // rb for h, rb in zip(halves, rbs)]
    starts, total = _spans(counts)
    half_shapes = [(*g.shape[:-2], h, g.shape[-1]) for g, h in zip(grads, halves)]

    def rows_of(ref, i, start):
        return ref.at[pl.ds(start, rbs[i])] if whole[i] else ref.at[:, pl.ds(start, rbs[i])]

    def body(place_ref, *refs):
        a_refs, o_refs = refs[:n], refs[n:2 * n]
        f_refs, h_refs = refs[2 * n:3 * n], refs[3 * n:4 * n]
        send_refs, their_refs, (send_sems, recv_sems) = refs[4 * n:5 * n], refs[5 * n:6 * n], refs[6 * n:]
        t = pl.program_id(0)
        q = place_ref[1]
        x, y, c = _position()
        copies = [[pltpu.make_async_remote_copy(
            src_ref=rows_of(send_refs[i], i, k * rbs[i]), dst_ref=rows_of(their_refs[i], i, k * rbs[i]),
            send_sem=send_sems.at[starts[i] + k], recv_sem=recv_sems.at[starts[i] + k],
            device_id=(x, y, 1 - c), device_id_type=MESH) for k in range(counts[i])] for i in range(n)]

        for i in range(n):
            for k in range(counts[i]):
                @pl.when(t == starts[i] + k)
                def _(i=i, k=k):
                    rows_of(send_refs[i], i, k * rbs[i])[...] = _bf(o_refs[i][...])
                    copies[i][k].start()

        for i in range(n):
            for k in range(counts[i]):
                @pl.when(t == starts[i] + k + 1)
                def _(i=i, k=k):
                    copies[i][k].wait_recv()
                    b_ref = rows_of(their_refs[i], i, k * rbs[i])
                    if not whole[i]:
                        h_refs[i][...] = _bf(a_refs[i][...] + b_ref[...].astype(F32))
                        f_refs[i][...] = a_refs[i][q] + b_ref[q].astype(F32)
                        return
                    total_i = a_refs[i][...] + b_ref[...].astype(F32)
                    for k4 in range(N_CHIPS):
                        piece = total_i[:, k4 * cols[i]:(k4 + 1) * cols[i]]
                        h_refs[i][k4] = _bf(piece)

                        @pl.when(q == k4)
                        def _():
                            f_refs[i][...] = piece

        @pl.when(t == total)
        def _():
            for of_matrix in copies:
                for cp in of_matrix:
                    cp.wait_send()

    def specs(i):
        sent = lambda t: _local_step(t, starts[i], counts[i])
        added = lambda t: _local_step(t - 1, starts[i], counts[i])
        by_quarter = (N_CHIPS, rbs[i], cols[i])
        block = (rbs[i], grads[i].shape[-1]) if whole[i] else by_quarter
        lead = () if whole[i] else (0,)
        mine = pl.BlockSpec(block, lambda t, place: (*lead, place[0] * counts[i] + added(t), 0))
        other = pl.BlockSpec(block, lambda t, place: (*lead, (1 - place[0]) * counts[i] + sent(t), 0))
        sums = pl.BlockSpec(by_quarter, lambda t, place: (0, added(t), 0))
        own = pl.BlockSpec(by_quarter[1:], lambda t, place: (added(t), 0))
        return mine, other, own, sums

    all_specs = [specs(i) for i in range(n)]
    outs = pl.pallas_call(
        body, name=name,
        grid_spec=pltpu.PrefetchScalarGridSpec(
            num_scalar_prefetch=1, grid=(total + 1,),
            in_specs=[sp[0] for sp in all_specs] + [sp[1] for sp in all_specs],
            out_specs=[sp[2] for sp in all_specs] + [sp[3] for sp in all_specs],
            scratch_shapes=[pltpu.VMEM(sh, BF16) for sh in half_shapes] + [pltpu.VMEM(sh, BF16) for sh in half_shapes]
                           + [pltpu.SemaphoreType.DMA((total,)), pltpu.SemaphoreType.DMA((total,))]),
        out_shape=[jax.ShapeDtypeStruct((h, cl), F32) for h, cl in zip(halves, cols)]
                  + [jax.ShapeDtypeStruct((N_CHIPS, h, cl), BF16) for h, cl in zip(halves, cols)],
        compiler_params=_params("arbitrary"),
    )(place, *grads, *grads)
    return list(zip(outs[:n], outs[n:]))


SMALL_SUM_ROWS = 16


def join_halves(grad, place, owns, gots, small_pool, small_gla, small_top, g_gk_pad):
    n = len(owns)
    half, cols = grad.shape[1] // 2, grad.shape[2]
    rb = min(ADD_HALVES_ROWS, half)
    sent = half // rb
    shapes = [g.shape for g in gots] + [(N_CHIPS - 1, half, cols)]
    rbs = [min(ADD_ROWS, sh[1]) for sh in shapes]
    counts = [sh[1] // r for sh, r in zip(shapes, rbs)]
    starts, joined = _spans(counts)
    first_join = sent + 1
    steps = first_join + joined

    def body(place_ref, *refs):
        refs = iter(refs)
        take = lambda count: [next(refs) for _ in range(count)]
        (a_ref, o_ref), o_refs, g_refs = take(2), take(n), take(n)
        pool_ref, gla_ref, top_ref, gk_ref = take(4)
        out_refs, (total_ref,) = take(n + 1), take(1)
        send_buf, their_buf, sums_buf, got_buf = take(4)
        sum_refs = take(n + 1)
        to_core_sems, from_core_sems, to_chip_sems, from_chip_sems, local_sems, send_sems, recv_sems = take(7)
        all_ref, small_ref = take(2)
        t = pl.program_id(0)
        q = place_ref[1]
        x, y, c = _position()
        start_small, wait_small = _gather_small(small_ref, all_ref, *refs)
        block = lambda k: pl.ds(k * rb, rb)

        def to_core(k):
            return pltpu.make_async_remote_copy(
                src_ref=send_buf.at[:, block(k)], dst_ref=their_buf.at[:, block(k)],
                send_sem=to_core_sems.at[k], recv_sem=from_core_sems.at[k],
                device_id=(x, y, 1 - c), device_id_type=MESH)

        def to_owners(k):
            return [pltpu.make_async_remote_copy(
                src_ref=sums_buf.at[2 * chip[0] + chip[1], block(k)], dst_ref=got_buf.at[j, block(k)],
                send_sem=to_chip_sems.at[3 * k + j], recv_sem=from_chip_sems.at[3 * k + j],
                device_id=(*chip, c), device_id_type=MESH) for j, chip in enumerate(_other_chips(x, y))]

        def copies(i, k):
            src = sum_refs[i].at[pl.ds(k * rbs[i], rbs[i])]
            rows = pl.ds(c * shapes[i][1] + k * rbs[i], rbs[i])
            return (pltpu.make_async_copy(src, out_refs[i].at[rows], local_sems.at[starts[i] + k]),
                    pltpu.make_async_remote_copy(
                        src_ref=src, dst_ref=out_refs[i].at[rows],
                        send_sem=send_sems.at[starts[i] + k], recv_sem=recv_sems.at[starts[i] + k],
                        device_id=(x, y, 1 - c), device_id_type=MESH))

        @pl.when(t == 0)
        def _():
            small_ref[0:3, :] = pool_ref[0:3, :]
            small_ref[3:5, :] = gla_ref[0:2, :]
            small_ref[5:8, :] = top_ref[0:3, :]
            for r in range(GATE_RANK):
                small_ref[8 + r // 2:9 + r // 2, (r % 2) * KEY_W:(r % 2 + 1) * KEY_W] = gk_ref[r:r + 1, :]
            start_small()

        for k in range(sent):
            @pl.when(t == k)
            def _(k=k):
                send_buf[:, k * rb:(k + 1) * rb, :] = _bf(o_ref[...])
                to_core(k).start()

        for k in range(sent):
            @pl.when(t == k + 1)
            def _(k=k):
                to_core(k).wait_recv()
                theirs = their_buf.at[:, block(k)]
                sums_buf[:, k * rb:(k + 1) * rb, :] = _bf(a_ref[...] + theirs[...].astype(F32))
                sum_refs[n][k * rb:(k + 1) * rb, :] = a_ref[q] + theirs[q].astype(F32)
                for cp in to_owners(k):
                    cp.start()

        for i in range(n + 1):
            for k in range(counts[i]):
                @pl.when(t == first_join + starts[i] + k)
                def _(i=i, k=k):
                    rows = slice(k * rbs[i], (k + 1) * rbs[i])
                    if i < n:
                        total_i = o_refs[i][...]
                        arrived = [g_refs[i][j] for j in range(N_CHIPS - 1)]
                    else:
                        if k == 0:
                            for kk in range(sent):
                                for cp in to_owners(kk):
                                    cp.wait_recv()
                        total_i = sum_refs[n][rows, :]
                        arrived = [got_buf[j, rows, :] for j in range(N_CHIPS - 1)]
                    for part in arrived:
                        total_i = total_i + part.astype(F32)
                    sum_refs[i][rows, :] = total_i
                    for cp in copies(i, k):
                        cp.start()

        @pl.when(t == steps - 1)
        def _():
            wait_small()
            small_total = all_ref[0]
            for dev in range(1, N_DEV):
                small_total = small_total + all_ref[dev]
            total_ref[...] = small_total
            for k in range(sent):
                to_core(k).wait_send()
                for cp in to_owners(k):
                    cp.wait_send()
            for i in range(n + 1):
                for k in range(counts[i]):
                    for cp in copies(i, k):
                        cp.wait()

    def specs(i):
        step = lambda t: _local_step(t - first_join, starts[i], counts[i])
        return (pl.BlockSpec((rbs[i], shapes[i][2]), lambda t, place: (step(t), 0)),
                pl.BlockSpec((N_CHIPS - 1, rbs[i], shapes[i][2]), lambda t, place: (0, step(t), 0)))

    by_quarter = (N_CHIPS, rb, cols)
    mine = pl.BlockSpec(by_quarter, lambda t, place: (0, place[0] * sent + jnp.clip(t - 1, 0, sent - 1), 0))
    other = pl.BlockSpec(by_quarter, lambda t, place: (0, (1 - place[0]) * sent + jnp.clip(t, 0, sent - 1), 0))
    all_specs = [specs(i) for i in range(n)]
    sems = lambda count: pltpu.SemaphoreType.DMA((count,))
    outs = pl.pallas_call(
        body, name="join_halves",
        grid_spec=pltpu.PrefetchScalarGridSpec(
            num_scalar_prefetch=1, grid=(steps,),
            in_specs=[mine, other] + [sp[0] for sp in all_specs] + [sp[1] for sp in all_specs] + [VMEM_SPEC] * 4,
            out_specs=_any_specs(n + 1) + [VMEM_SPEC],
            scratch_shapes=[pltpu.VMEM((N_CHIPS, half, cols), BF16) for _ in range(3)]
                           + [pltpu.VMEM(shapes[n], BF16)] + [pltpu.VMEM(sh[1:], F32) for sh in shapes]
                           + [sems(sent), sems(sent), sems(3 * sent), sems(3 * sent),
                              sems(joined), sems(joined), sems(joined),
                              pltpu.VMEM((N_DEV, SMALL_SUM_ROWS, D), F32), pltpu.VMEM((SMALL_SUM_ROWS, D), F32)]
                           + SMALL_SEMS),
        out_shape=[jax.ShapeDtypeStruct((2 * sh[1], sh[2]), F32) for sh in shapes]
                  + [jax.ShapeDtypeStruct((SMALL_SUM_ROWS, D), F32)],
        compiler_params=_params("arbitrary"),
    )(place, grad, grad, *owns, *gots, small_pool, small_gla, small_top, g_gk_pad)
    return [outs[n]] + list(outs[:n]), outs[n + 1]


def _adam_math(w, g, m, v):
    m = ADAM_B1 * m + (1.0 - ADAM_B1) * g
    v = ADAM_B2 * v + (1.0 - ADAM_B2) * (g * g)
    m_hat = m / (1.0 - ADAM_B1 ** ADAM_STEP)
    v_hat = v / (1.0 - ADAM_B2 ** ADAM_STEP)
    delta = -ADAM_LR * (m_hat / (jnp.sqrt(v_hat) + ADAM_EPS) + ADAM_WD * w)
    return delta, m, v


ADAM_BLOCK_BYTES = 2 ** 19
ADAM_MOST_STEPS = 8


def adamw(params, name):
    n = len(params)
    shapes = [p[0].shape for p in params]

    def tile_rows(shape):
        rows, cols = shape[0], shape[-1]
        aligned = 1 if len(shape) == 3 else 8
        divisors = [t for t in range(aligned, rows + 1, aligned) if rows % t == 0]
        tile = max(t for t in divisors if t * cols * 4 <= ADAM_BLOCK_BYTES)
        if rows // tile > ADAM_MOST_STEPS:
            tile = min(t for t in divisors if rows // t <= ADAM_MOST_STEPS)
        return tile

    tiles = [tile_rows(sh) for sh in shapes]
    counts = [sh[0] // tl for sh, tl in zip(shapes, tiles)]
    starts, total = _spans(counts)

    def body(*refs):
        ins, outs = refs[:4 * n], refs[4 * n:]
        t = pl.program_id(0)
        for i in range(n):
            @pl.when((t >= starts[i]) & (t < starts[i] + counts[i]))
            def _(i=i):
                w_ref, g_ref, m_ref, v_ref = ins[4 * i:4 * i + 4]
                d, nm, nv = _adam_math(w_ref[...], g_ref[...], m_ref[...], v_ref[...])
                outs[3 * i][...] = d
                outs[3 * i + 1][...] = nm
                outs[3 * i + 2][...] = nv

    def spec(i):
        block = (tiles[i],) + shapes[i][1:]
        zeros = (0,) * (len(block) - 1)
        return pl.BlockSpec(block, lambda t: (_local_step(t, starts[i], counts[i]),) + zeros)

    outs = pl.pallas_call(
        body, name=name, grid=(total,),
        out_shape=[jax.ShapeDtypeStruct(sh, F32) for sh in shapes for _ in range(3)],
        in_specs=[spec(i) for i in range(n) for _ in range(4)],
        out_specs=[spec(i) for i in range(n) for _ in range(3)],
        compiler_params=_params("arbitrary"),
    )(*[a for p in params for a in p])
    return [tuple(outs[3 * i:3 * i + 3]) for i in range(n)]


def adamw_small(params, total, place):
    n = len(params)

    def cut_gradients(total_ref, q, g_refs):
        g_norm, g_group_b, g_scale, g_gk_w, g_gk_b, g_head_norm, g_final = g_refs
        g_norm[0:1, :] = total_ref[0:1, :]
        g_norm[1:2, :] = total_ref[3:4, :]
        g_scale[...] = total_ref[1:2, :]
        g_final[...] = total_ref[5:6, :]
        g_gk_b[...] = total_ref[4:5, pl.ds(pl.multiple_of(q * 128, 128), 128)]
        for r in range(GATE_RANK):
            lanes = pl.ds(pl.multiple_of((r % 2) * KEY_W + q * 128, 128), 128)
            g_gk_w[r:r + 1, :] = total_ref[8 + r // 2:9 + r // 2, lanes]
        for k in range(N_CHIPS):
            @pl.when(q == k)
            def _(k=k):
                g_head_norm[...] = total_ref[6:7, 64 * k:64 * (k + 1)]
                for g in range(GROUPS):
                    g_group_b[g:g + 1, :] = total_ref[2:3, GROUP_DIM * g + 64 * k:GROUP_DIM * g + 64 * (k + 1)]

    def body(place_ref, total_ref, *refs):
        ins, outs = refs[:3 * n], refs[3 * n:]
        outs[4 * n][...] = total_ref[7:8, 0:1]
        cut_gradients(total_ref, place_ref[1], outs[0:4 * n:4])
        for k in range(n):
            w_ref, m_ref, v_ref = ins[3 * k:3 * k + 3]
            d, nm, nv = _adam_math(w_ref[...], outs[4 * k][...], m_ref[...], v_ref[...])
            outs[4 * k + 1][...] = d
            outs[4 * k + 2][...] = nm
            outs[4 * k + 3][...] = nv

    flat = [a for p in params for a in p]
    outs = pl.pallas_call(
        body, name="adamw_small",
        out_shape=[jax.ShapeDtypeStruct(p[0].shape, F32) for p in params for _ in range(4)]
                  + [jax.ShapeDtypeStruct((1, 1), F32)],
        in_specs=[pl.BlockSpec(memory_space=pltpu.SMEM)] + [VMEM_SPEC] * (1 + 3 * n),
        out_specs=[VMEM_SPEC] * (4 * n + 1),
    )(place, total, *flat)
    return [tuple(outs[4 * k:4 * k + 4]) for k in range(n)], outs[4 * n]


def matmul_tn(a, b, name, tile_n, place, by_column_tile=False, also_reduce=()):
    s, m = a.shape
    n = b.shape[1]
    steps = n // tile_n
    n_red = len(also_reduce)
    assert n_red == 0 or steps >= 2
    halves = [(g.shape[1] // 2, g.shape[2]) for g in also_reduce]
    if by_column_tile:
        out_shape = jax.ShapeDtypeStruct((steps, m, tile_n), F32)
        out_spec = pl.BlockSpec((None, m, tile_n), lambda j, place: (j, 0, 0))
    else:
        out_shape = jax.ShapeDtypeStruct((m, n), F32)
        out_spec = pl.BlockSpec((m, tile_n), lambda j, place: (0, j))

    def body(place_ref, a_ref, b_ref, *rest):
        rest = iter(rest)
        take = lambda count: [next(rest) for _ in range(count)]
        mine_refs, other_refs, (out_ref,) = take(n_red), take(n_red), take(1)
        own_refs, got_refs = take(n_red), take(n_red)
        send_bufs, their_bufs, sums_bufs = take(n_red), take(n_red), take(n_red)
        sems = list(rest)
        j = pl.program_id(0)
        q = place_ref[1]
        x, y, c = _position()

        def to_core(i):
            return pltpu.make_async_remote_copy(
                src_ref=send_bufs[i], dst_ref=their_bufs[i], send_sem=sems[0].at[i], recv_sem=sems[1].at[i],
                device_id=(x, y, 1 - c), device_id_type=MESH)

        def to_owners(i):
            return [pltpu.make_async_remote_copy(
                src_ref=sums_bufs[i].at[2 * chip[0] + chip[1]], dst_ref=got_refs[i].at[k],
                send_sem=sems[2].at[3 * i + k], recv_sem=sems[3].at[3 * i + k],
                device_id=(*chip, c), device_id_type=MESH) for k, chip in enumerate(_other_chips(x, y))]

        if n_red:
            @pl.when(j == 0)
            def _():
                for i in range(n_red):
                    send_bufs[i][...] = _bf(other_refs[i][...])
                    to_core(i).start()

            @pl.when(j == 1)
            def _():
                for i in range(n_red):
                    to_core(i).wait_recv()
                    sums_bufs[i][...] = _bf(mine_refs[i][...] + their_bufs[i][...].astype(F32))
                    own_refs[i][...] = mine_refs[i][q] + their_bufs[i][q].astype(F32)
                    for cp in to_owners(i):
                        cp.start()

        out_ref[...] = _tn(a_ref[...], b_ref[...])

        if n_red:
            @pl.when(j == steps - 1)
            def _():
                for i in range(n_red):
                    to_core(i).wait_send()
                    for cp in to_owners(i):
                        cp.wait()

    by_quarter = [(N_CHIPS, *h) for h in halves]
    outs = pl.pallas_call(
        body, name=name,
        grid_spec=pltpu.PrefetchScalarGridSpec(
            num_scalar_prefetch=1, grid=(steps,),
            in_specs=[pl.BlockSpec((s, m), lambda j, place: (0, 0)), pl.BlockSpec((s, tile_n), lambda j, place: (0, j))]
                     + [pl.BlockSpec(sh, lambda j, place: (0, place[0], 0)) for sh in by_quarter]
                     + [pl.BlockSpec(sh, lambda j, place: (0, 1 - place[0], 0)) for sh in by_quarter],
            out_specs=[out_spec] + [pl.BlockSpec(h, lambda j, place: (0, 0)) for h in halves] + _any_specs(n_red),
            scratch_shapes=[pltpu.VMEM(sh, BF16) for sh in by_quarter * 3]
                           + ([pltpu.SemaphoreType.DMA((n_red,)), pltpu.SemaphoreType.DMA((n_red,)),
                               pltpu.SemaphoreType.DMA((3 * n_red,)), pltpu.SemaphoreType.DMA((3 * n_red,))]
                              if n_red else [])),
        out_shape=[out_shape] + [jax.ShapeDtypeStruct(h, F32) for h in halves]
                  + [jax.ShapeDtypeStruct((N_CHIPS - 1, *h), BF16) for h in halves],
        compiler_params=_params("arbitrary"),
    )(place, a, b, *also_reduce, *also_reduce)
    return outs[0], list(zip(outs[1:1 + n_red], outs[1 + n_red:]))


ROW_TILE = 512


def _row_index(tile, rows):
    return tile * rows + lax.broadcasted_iota(jnp.int32, (rows, 1), 0)


def _inverse_counts(t_glob):
    return [1.0 / jnp.minimum(t_glob + 1, w).astype(F32) for w in POOL_WINDOWS]


def _sigmoid(z):
    return 1.0 / (1.0 + jnp.exp(-z))


def _trailing_sums(src, tmp, cols, window, rows):
    bufs = (src, tmp)
    span, level, start = 1, 0, 0
    while span < window:
        start += 8
        a, b = bufs[level % 2], bufs[(level + 1) % 2]
        n = HALO + rows - start
        b[start:start + n, cols] = a[start:start + n, cols] + a[start - span:start - span + n, cols]
        span, level = 2 * span, level + 1
    return bufs[level % 2][HALO:HALO + rows, cols]


def _leading_sums(src, tmp, cols, window, rows):
    bufs = (src, tmp)
    span, level, n = 1, 0, rows + HALO
    while span < window:
        n -= 8
        a, b = bufs[level % 2], bufs[(level + 1) % 2]
        b[0:n, cols] = a[0:n, cols] + a[span:span + n, cols]
        span, level = 2 * span, level + 1
    return bufs[level % 2][0:rows, cols]


def gather_in_background(step, last, out_refs, send_sems, recv_sems, finish):
    n = len(out_refs)
    x, y, c = _position()
    q = 2 * x + y
    chips = _other_chips(x, y)

    def copy(k, i, quarter, half, to):
        return _gather_copy(out_refs[i], send_sems, recv_sems, k * n + i, quarter, half, to)

    if not finish:
        @pl.when(step == 0)
        def _():
            for i in range(n):
                mine, _ = _halves(out_refs[i].shape[1], c)
                for j, chip in enumerate(chips):
                    copy(j, i, q, mine, (*chip, c)).start()

        @pl.when(step == last)
        def _():
            for j, chip in enumerate(chips):
                qj = 2 * chip[0] + chip[1]
                for i in range(n):
                    mine, _ = _halves(out_refs[i].shape[1], c)
                    copy(j, i, qj, mine, (x, y, c)).wait_recv()
                    copy(3 + j, i, qj, mine, (x, y, 1 - c)).start()
        return

    @pl.when(step == last)
    def _():
        for j, chip in enumerate(chips):
            qj = 2 * chip[0] + chip[1]
            for i in range(n):
                mine, other = _halves(out_refs[i].shape[1], c)
                copy(3 + j, i, qj, other, (x, y, c)).wait_recv()
                copy(j, i, q, mine, (x, y, c)).wait_send()
                copy(3 + j, i, qj, mine, (x, y, c)).wait_send()


def _group_matrix(gw_ref, g):
    rows = GROUP_DIM // N_CHIPS
    return jnp.concatenate([gw_ref[j, rows * g:rows * (g + 1), :] for j in range(N_CHIPS)], axis=0)


def pool_forward(x, w0, wpi, gw, gb, scale, wpo, later):
    s = x.shape[0]
    ts = ROW_TILE
    nt = s // ts
    assert nt >= 2
    n_later = len(later)

    def body(x_ref, w0_ref, wpi_ref, gw_ref, gb_ref, sc_ref, wpo_ref, *rest):
        rest = rest[n_later:]
        h1_ref, pooled_ref, gt_ref, n0_ref = rest[:4]
        later_refs = rest[4:4 + n_later]
        ubuf, tbuf, hist, send_sems, recv_sems = rest[4 + n_later:]
        i = pl.program_id(0)
        gather_in_background(i, nt - 1, later_refs, send_sems, recv_sems, finish=False)
        xv = x_ref[...]
        r = lax.rsqrt(jnp.mean(xv * xv, axis=-1, keepdims=True) + EPS)
        n0 = _bf(xv * r * w0_ref[...])
        n0_ref[...] = n0
        u = jnp.concatenate([_nn(n0, wpi_ref[0]), _nn(n0, wpi_ref[1])], axis=-1)
        gt = jnp.concatenate([_nn(n0, wpi_ref[2]), _nn(n0, wpi_ref[3])], axis=-1)
        gt_ref[...] = gt

        @pl.when(i == 0)
        def _():
            hist[...] = jnp.zeros_like(hist)

        ubuf[0:HALO, :] = hist[...]
        ubuf[HALO:HALO + ts, :] = u
        hist[...] = u[ts - HALO:, :]
        inv = _inverse_counts(_row_index(i, ts))
        mixed = []
        for g, w in enumerate(POOL_WINDOWS):
            cols = slice(g * GROUP_DIM, (g + 1) * GROUP_DIM)
            pooled = _bf(_trailing_sums(ubuf, tbuf, cols, w, ts) * inv[g] - u[:, cols])
            pooled_ref[:, cols] = pooled
            mixed.append(_nn(pooled, _group_matrix(gw_ref, g)))
        mixed = jnp.concatenate(mixed, axis=-1) + gb_ref[...]
        y = mixed * sc_ref[...] * (gt * _sigmoid(gt))
        h1_ref[...] = xv + _nn(_bf(y), wpo_ref[...])
        gather_in_background(i, nt - 1, later_refs, send_sems, recv_sems, finish=True)

    row = lambda cols: pl.BlockSpec((ts, cols), lambda i: (i, 0))
    outs = pl.pallas_call(
        body, name="pool_forward", grid=(nt,),
        out_shape=[jax.ShapeDtypeStruct((s, D), F32), jax.ShapeDtypeStruct((s, D), BF16),
                   jax.ShapeDtypeStruct((s, D), F32), jax.ShapeDtypeStruct((s, D), BF16)]
                  + [jax.ShapeDtypeStruct(a.shape, a.dtype) for a in later],
        in_specs=[row(D), _full((1, D)), _full((N_CHIPS, D, D // 2)), _full((GROUPS, GROUP_DIM, GROUP_DIM)),
                  _full((1, D)), _full((1, D)), _full((D, D))] + _any_specs(n_later),
        out_specs=[row(D), row(D), row(D), row(D)] + _any_specs(n_later),
        input_output_aliases={7 + k: 4 + k for k in range(n_later)},
        scratch_shapes=[pltpu.VMEM((HALO + ts, D), F32), pltpu.VMEM((HALO + ts, D), F32),
                        pltpu.VMEM((HALO, D), F32),
                        pltpu.SemaphoreType.DMA((6 * n_later,)), pltpu.SemaphoreType.DMA((6 * n_later,))],
        compiler_params=_params("arbitrary"),
    )(x, w0, wpi, gw, gb, scale, wpo, *later)
    return outs[:4], outs[4:]


def pool_backward(x, dh1, pooled, gt, w0, wpi, gw, gb, scale, wpo, chip_sums):
    s = x.shape[0]
    ts = ROW_TILE
    nt = s // ts
    n_sums = len(chip_sums)

    def body(x_ref, dh1_ref, pooled_ref, gt_ref, w0_ref, wpi_ref, gw_ref, gb_ref, sc_ref, wpo_ref, *rest):
        sum_refs, rest = rest[:n_sums], rest[n_sums:]
        dx_ref, dproj_ref, gpo_ref, ggw_ref, small_ref = rest[:5]
        got_refs = rest[5:5 + n_sums]
        ebuf, tbuf, ahead, send_sems, recv_sems = rest[5 + n_sums:]
        i = pl.program_id(0)
        copies = _scatter_copies(sum_refs, got_refs, send_sems, recv_sems)

        @pl.when(i == 0)
        def _():
            for cp in copies:
                cp.start()

        @pl.when(i == 0)
        def _():
            gpo_ref[...] = jnp.zeros_like(gpo_ref)
            ggw_ref[...] = jnp.zeros_like(ggw_ref)
            small_ref[...] = jnp.zeros_like(small_ref)
            ahead[...] = jnp.zeros_like(ahead)

        dh1 = dh1_ref[...]
        dh1_bf = _bf(dh1)
        gt = gt_ref[...]
        sc = sc_ref[...]
        dy = _nt(dh1_bf, wpo_ref[...])
        pooled_bf = []
        mixed = []
        for g in range(GROUPS):
            cols = slice(g * GROUP_DIM, (g + 1) * GROUP_DIM)
            pb = pooled_ref[:, cols]
            pooled_bf.append(pb)
            mixed.append(_nn(pb, _group_matrix(gw_ref, g)))
        mixed = jnp.concatenate(mixed, axis=-1) + gb_ref[...]
        sg = _sigmoid(gt)
        silu = gt * sg
        gpo_ref[...] += _tn(_bf(mixed * sc * silu), dh1_bf)
        dmixed = dy * sc * silu
        dgt = dy * mixed * sc * (sg * (1.0 + gt * (1.0 - sg)))
        dproj_ref[:, D:] = _bf(dgt)
        small_ref[1:2, :] += jnp.sum(dy * mixed * silu, axis=0, keepdims=True)
        small_ref[2:3, :] += jnp.sum(dmixed, axis=0, keepdims=True)

        inv = _inverse_counts(_row_index(nt - 1 - i, ts))
        rows_q = GROUP_DIM // N_CHIPS
        ebuf[ts:ts + HALO, :] = ahead[...]
        dpooled = []
        for g in range(GROUPS):
            cols = slice(g * GROUP_DIM, (g + 1) * GROUP_DIM)
            dm = _bf(dmixed[:, cols])
            ggw = _tn(pooled_bf[g], dm)
            for j in range(N_CHIPS):
                ggw_ref[j, rows_q * g:rows_q * (g + 1), :] += ggw[rows_q * j:rows_q * (j + 1), :]
            dp = _nt(dm, _group_matrix(gw_ref, g))
            dpooled.append(dp)
            ebuf[0:ts, cols] = dp * inv[g]
        ahead[...] = ebuf[0:HALO, :]
        du = []
        for g, w in enumerate(POOL_WINDOWS):
            cols = slice(g * GROUP_DIM, (g + 1) * GROUP_DIM)
            du.append(_leading_sums(ebuf, tbuf, cols, w, ts) - dpooled[g])
        du = _bf(jnp.concatenate(du, axis=-1))
        dproj_ref[:, :D] = du
        dgt_bf = _bf(dgt)
        half = D // 2
        dn0 = (_nt(du[:, :half], wpi_ref[0]) + _nt(du[:, half:], wpi_ref[1])
               + _nt(dgt_bf[:, :half], wpi_ref[2]) + _nt(dgt_bf[:, half:], wpi_ref[3]))

        xv = x_ref[...]
        r = lax.rsqrt(jnp.mean(xv * xv, axis=-1, keepdims=True) + EPS)
        xhat = xv * r
        small_ref[0:1, :] += jnp.sum(dn0 * xhat, axis=0, keepdims=True)
        dxh = dn0 * w0_ref[...]
        dx_ref[...] = dh1 + r * (dxh - xhat * jnp.mean(dxh * xhat, axis=-1, keepdims=True))

        @pl.when(i == nt - 1)
        def _():
            for cp in copies:
                cp.wait()

    row = lambda cols: pl.BlockSpec((ts, cols), lambda i: (nt - 1 - i, 0))
    outs = pl.pallas_call(
        body, name="pool_backward", grid=(nt,),
        out_shape=[jax.ShapeDtypeStruct((s, D), F32), jax.ShapeDtypeStruct((s, 2 * D), BF16),
                   jax.ShapeDtypeStruct((D, D), F32),
                   jax.ShapeDtypeStruct((GROUPS, GROUP_DIM, GROUP_DIM), F32),
                   jax.ShapeDtypeStruct((8, D), F32)] + _scatter_shapes(chip_sums),
        in_specs=[row(D), row(D), row(D), row(D), _full((1, D)), _full((N_CHIPS, D, D // 2)),
                  _full((GROUPS, GROUP_DIM, GROUP_DIM)), _full((1, D)), _full((1, D)), _full((D, D))]
                 + _any_specs(n_sums),
        out_specs=[row(D), row(2 * D), _full((D, D)), _full((GROUPS, GROUP_DIM, GROUP_DIM)), _full((8, D))]
                  + _any_specs(n_sums),
        scratch_shapes=[pltpu.VMEM((ts + HALO, D), F32), pltpu.VMEM((ts + HALO, D), F32),
                        pltpu.VMEM((HALO, D), F32),
                        pltpu.SemaphoreType.DMA((3 * n_sums,)), pltpu.SemaphoreType.DMA((3 * n_sums,))],
        compiler_params=_params("arbitrary"),
    )(x, dh1, pooled, gt, w0, wpi, gw, gb, scale, wpo, *chip_sums)
    return outs[:5], outs[5:]


def gla_project(h1, w1, wgi_q, wgk, bgk, later):
    s = h1.shape[0]
    ts = ROW_TILE
    nt = s // ts
    assert nt >= 2
    n_later = len(later)

    def body(h_ref, w1_ref, wq_ref, wgk_ref, bgk_ref, *rest):
        rest = rest[n_later:]
        qk_ref, v_ref, gate_ref, low_ref, cum_ref, n1_ref = rest[:6]
        later_refs = rest[6:6 + n_later]
        send_sems, recv_sems, wgi_ref = rest[6 + n_later:]
        gather_in_background(pl.program_id(0), nt - 1, later_refs, send_sems, recv_sems, finish=False)

        @pl.when(pl.program_id(0) == 0)
        def _():
            _assemble_gla_in(wq_ref, wgi_ref)

        hv = h_ref[...]
        r = lax.rsqrt(jnp.mean(hv * hv, axis=-1, keepdims=True) + EPS)
        n1 = _bf(hv * r * w1_ref[...])
        n1_ref[...] = n1
        qk_ref[...] = _nn(n1, wgi_ref[:, 0:2 * KEY_W])
        v_ref[...] = _bf(_nn(n1, wgi_ref[:, 2 * KEY_W:2 * KEY_W + D]))
        gate_ref[...] = _nn(n1, wgi_ref[:, 2 * KEY_W + D:GLA_MAIN])
        low = _bf(_nn(n1, wgi_ref[:, GLA_MAIN:]))
        low_ref[...] = low
        z = _nn(low, wgk_ref[...]) + bgk_ref[...]
        lg = (jnp.minimum(z, 0.0) - jnp.log(1.0 + jnp.exp(-jnp.abs(z)))) / GATE_NORM
        lower_f = _chunk_masks()[0].astype(F32)
        for r0 in range(0, ts, CHUNK):
            cum_ref[r0:r0 + CHUNK, :] = _nn_exact(lower_f, lg[r0:r0 + CHUNK, :])
        gather_in_background(pl.program_id(0), nt - 1, later_refs, send_sems, recv_sems, finish=True)

    row = lambda cols: pl.BlockSpec((ts, cols), lambda i: (i, 0))
    outs = pl.pallas_call(
        body, name="gla_project", grid=(nt,),
        out_shape=[jax.ShapeDtypeStruct((s, D), F32), jax.ShapeDtypeStruct((s, D), BF16),
                   jax.ShapeDtypeStruct((s, D), F32), jax.ShapeDtypeStruct((s, RANK_PAD), BF16),
                   jax.ShapeDtypeStruct((s, KEY_W), F32), jax.ShapeDtypeStruct((s, D), BF16)]
                  + [jax.ShapeDtypeStruct(a.shape, a.dtype) for a in later],
        in_specs=[row(D), _full((1, D)), _full((N_CHIPS, D, GLA_IN_QUARTER)),
                  _full((RANK_PAD, KEY_W)), _full((1, KEY_W))] + _any_specs(n_later),
        out_specs=[row(D), row(D), row(D), row(RANK_PAD), row(KEY_W), row(D)] + _any_specs(n_later),
        input_output_aliases={5 + k: 6 + k for k in range(n_later)},
        scratch_shapes=[pltpu.SemaphoreType.DMA((6 * n_later,)), pltpu.SemaphoreType.DMA((6 * n_later,)),
                        pltpu.VMEM((D, GLA_MAIN + RANK_PAD), BF16)],
        compiler_params=_params("arbitrary"),
    )(h1, w1, wgi_q, wgk, bgk, *later)
    return outs[:6], outs[6:]


def _assemble_gla_in(wq_ref, wfull):
    pad = jnp.zeros((CAST_ROWS, GLA_MAIN + RANK_PAD - GLA_IN), BF16)
    for r0 in range(0, D, CAST_ROWS):
        rows = slice(r0, r0 + CAST_ROWS)
        wfull[rows, :] = jnp.concatenate([wq_ref[q, rows, :] for q in range(N_CHIPS)] + [pad], axis=1)


GLA_BLOCK = 512
CHUNKS_PER_BLOCK = GLA_BLOCK // CHUNK


def _chunk_masks():
    t = lax.broadcasted_iota(jnp.int32, (CHUNK, CHUNK), 0)
    u = lax.broadcasted_iota(jnp.int32, (CHUNK, CHUNK), 1)
    return t >= u, t <= u


def _gla_chunk_terms(q, cum):
    ep = jnp.exp(cum)
    en = jnp.exp(-cum)
    qs = q * (HEAD_K ** -0.5)
    last = cum[CHUNK - 1:CHUNK, :]
    ed = jnp.exp(last - cum)
    dec = jnp.exp(last)
    return ep, en, qs, ed, dec


def gla_forward(qk, v, cum):
    s = qk.shape[0]
    nb = s // GLA_BLOCK
    nc = s // CHUNK

    def body(q_ref, k_ref, v_ref, cum_ref, o_ref, st_ref, sc_ref, state):
        @pl.when(pl.program_id(0) == 0)
        def _():
            state[...] = jnp.zeros_like(state)

        lower, _ = _chunk_masks()

        def chunk(cc, carry):
            rows = pl.ds(pl.multiple_of(cc * CHUNK, CHUNK), CHUNK)
            for h in range(HEADS):
                kc = slice(h * HEAD_K, (h + 1) * HEAD_K)
                vc = slice(h * HEAD_V, (h + 1) * HEAD_V)
                q = q_ref[rows, kc]
                k = k_ref[rows, kc]
                v = v_ref[rows, vc]
                ep, en, qs, ed, dec = _gla_chunk_terms(q, cum_ref[rows, kc])
                a = _bf(qs * ep)
                fwd = _nt(a, _bf(k * en))
                bwd = _nt(_bf(qs * en), _bf(k * ep))
                scores = _bf(jnp.where(lower, fwd, bwd))
                sc_ref[rows, h * CHUNK:(h + 1) * CHUNK] = scores
                st = state[h]
                st_ref[cc, h] = st
                o_ref[rows, vc] = _nn(scores, v) + _nt(a, _bf(st))
                state[h] = st * dec + _tn(v, _bf(k * ed))
            return carry

        lax.fori_loop(0, CHUNKS_PER_BLOCK, chunk, 0, unroll=True)

    return pl.pallas_call(
        body, name="gla_forward", grid=(nb,),
        out_shape=(jax.ShapeDtypeStruct((s, D), F32),
                   jax.ShapeDtypeStruct((nc, HEADS, HEAD_V, HEAD_K), F32),
                   jax.ShapeDtypeStruct((s, HEADS * CHUNK), BF16)),
        in_specs=[pl.BlockSpec((GLA_BLOCK, KEY_W), lambda i: (i, 0)),
                  pl.BlockSpec((GLA_BLOCK, KEY_W), lambda i: (i, 1)),
                  pl.BlockSpec((GLA_BLOCK, D), lambda i: (i, 0)),
                  pl.BlockSpec((GLA_BLOCK, KEY_W), lambda i: (i, 0))],
        out_specs=(pl.BlockSpec((GLA_BLOCK, D), lambda i: (i, 0)),
                   pl.BlockSpec((CHUNKS_PER_BLOCK, HEADS, HEAD_V, HEAD_K), lambda i: (i, 0, 0, 0)),
                   pl.BlockSpec((GLA_BLOCK, HEADS * CHUNK), lambda i: (i, 0))),
        scratch_shapes=[pltpu.VMEM((HEADS, HEAD_V, HEAD_K), F32)],
        compiler_params=_params("arbitrary"),
    )(qk, qk, v, cum)


def gla_backward(qk, v, cum, do, states, scores):
    s = qk.shape[0]
    nb = s // GLA_BLOCK

    def body(q_ref, k_ref, v_ref, cum_ref, do_ref, st_ref, sc_ref, dq_ref, dk_ref, dv_ref, dcum_ref, dstate):
        @pl.when(pl.program_id(0) == 0)
        def _():
            dstate[...] = jnp.zeros_like(dstate)

        lower, _ = _chunk_masks()
        is_last = lax.broadcasted_iota(jnp.int32, (CHUNK, HEAD_K), 0) == CHUNK - 1

        def chunk(step, carry):
            cc = CHUNKS_PER_BLOCK - 1 - step
            rows = pl.ds(pl.multiple_of(cc * CHUNK, CHUNK), CHUNK)
            for h in range(HEADS):
                kc = slice(h * HEAD_K, (h + 1) * HEAD_K)
                vc = slice(h * HEAD_V, (h + 1) * HEAD_V)
                q = q_ref[rows, kc]
                k = k_ref[rows, kc]
                v = v_ref[rows, vc]
                do_c = do_ref[rows, vc]
                ep, en, qs, ed, dec = _gla_chunk_terms(q, cum_ref[rows, kc])
                a = _bf(qs * ep)
                b = _bf(k * en)
                c = _bf(qs * en)
                dk_dec = _bf(k * ep)
                kd = _bf(k * ed)
                scores = sc_ref[rows, h * CHUNK:(h + 1) * CHUNK]
                st = st_ref[cc, h]
                dst = dstate[h]
                dst_bf = _bf(dst)

                dscores = _nt(do_c, v)
                dfwd = _bf(jnp.where(lower, dscores, 0.0))
                dbwd = _bf(jnp.where(lower, 0.0, dscores))
                dv_ref[rows, vc] = _bf(_tn(scores, do_c) + _nt(kd, dst_bf))
                da = _nn(dfwd, b) + _nn(do_c, _bf(st))
                db = _tn(dfwd, a)
                dc = _nn(dbwd, dk_dec)
                ddk = _tn(dbwd, c)
                dkd = _nn(v, dst_bf)
                ddec = jnp.sum(dst * st, axis=0, keepdims=True)
                dstate[h] = dst * dec + _tn(do_c, a)

                m = dkd * k * ed
                dq_ref[rows, kc] = _bf((da * ep + dc * en) * (HEAD_K ** -0.5))
                dk_ref[rows, kc] = _bf(db * en + ddk * ep + dkd * ed)
                dcum = (da * qs + ddk * k) * ep - (db * k + dc * qs) * en - m
                dlast = jnp.sum(m, axis=0, keepdims=True) + ddec * dec
                dcum_ref[rows, kc] = dcum + jnp.where(is_last, dlast, 0.0)
            return carry

        lax.fori_loop(0, CHUNKS_PER_BLOCK, chunk, 0, unroll=True)

    rev = lambda cols, col_block: pl.BlockSpec((GLA_BLOCK, cols), lambda i: (nb - 1 - i, col_block))
    return pl.pallas_call(
        body, name="gla_backward", grid=(nb,),
        out_shape=(jax.ShapeDtypeStruct((s, KEY_W), BF16), jax.ShapeDtypeStruct((s, KEY_W), BF16),
                   jax.ShapeDtypeStruct((s, D), BF16), jax.ShapeDtypeStruct((s, KEY_W), F32)),
        in_specs=[rev(KEY_W, 0), rev(KEY_W, 1), rev(D, 0), rev(KEY_W, 0), rev(D, 0),
                  pl.BlockSpec((CHUNKS_PER_BLOCK, HEADS, HEAD_V, HEAD_K), lambda i: (nb - 1 - i, 0, 0, 0)),
                  rev(HEADS * CHUNK, 0)],
        out_specs=(rev(KEY_W, 0), rev(KEY_W, 0), rev(D, 0), rev(KEY_W, 0)),
        scratch_shapes=[pltpu.VMEM((HEADS, HEAD_V, HEAD_K), F32)],
        compiler_params=_params("arbitrary"),
    )(qk, qk, v, cum, do, states, scores)


def head_and_loss(o, gate, h1, target, hw, wgo, wf):
    s = o.shape[0]
    ts = ROW_TILE

    def body(o_ref, gate_ref, h1_ref, tgt_ref, hw_ref, wgo_ref, wf_ref,
             dh2_ref, do_ref, dgate_ref, ggo_ref, small_ref):
        @pl.when(pl.program_id(0) == 0)
        def _():
            ggo_ref[...] = jnp.zeros_like(ggo_ref)
            small_ref[...] = jnp.zeros_like(small_ref)

        gate = gate_ref[...]
        hw = hw_ref[...]
        sg = _sigmoid(gate)
        silu = gate * sg
        ohat, ro = [], []
        for h in range(HEADS):
            oh = o_ref[:, h * HEAD_V:(h + 1) * HEAD_V]
            rh = lax.rsqrt(jnp.mean(oh * oh, axis=-1, keepdims=True) + EPS)
            ro.append(rh)
            ohat.append(oh * rh)
        ohat = jnp.concatenate(ohat, axis=-1)
        on = ohat * hw
        y2 = _bf(on * silu)
        h2 = h1_ref[...] + _nn(y2, wgo_ref[...])
        rf = lax.rsqrt(jnp.mean(h2 * h2, axis=-1, keepdims=True) + EPS)
        h2hat = h2 * rf
        wf = wf_ref[...]
        diff = h2hat * wf - tgt_ref[...]
        small_ref[2:3, :] += jnp.zeros((1, D), F32) + 0.5 * jnp.sum(diff * diff) / D
        dout = diff / D
        small_ref[0:1, :] += jnp.sum(dout * h2hat, axis=0, keepdims=True)
        dxh = dout * wf
        dh2 = rf * (dxh - h2hat * jnp.mean(dxh * h2hat, axis=-1, keepdims=True))
        dh2_ref[...] = dh2
        dh2_bf = _bf(dh2)
        ggo_ref[...] += _tn(y2, dh2_bf)
        dy2 = _nt(dh2_bf, wgo_ref[...])
        don = dy2 * silu
        dgate_ref[...] = _bf(dy2 * on * (sg * (1.0 + gate * (1.0 - sg))))
        ghw = jnp.sum(don * ohat, axis=0, keepdims=True)
        small_ref[1:2, 0:HEAD_V] += sum(ghw[:, h * HEAD_V:(h + 1) * HEAD_V] for h in range(HEADS))
        dohat = don * hw
        for h in range(HEADS):
            cols = slice(h * HEAD_V, (h + 1) * HEAD_V)
            oh, dh = ohat[:, cols], dohat[:, cols]
            do_ref[:, cols] = _bf(ro[h] * (dh - oh * jnp.mean(dh * oh, axis=-1, keepdims=True)))

    row = lambda cols: pl.BlockSpec((ts, cols), lambda i: (i, 0))
    act = jax.ShapeDtypeStruct((s, D), F32)
    act_bf = jax.ShapeDtypeStruct((s, D), BF16)
    return pl.pallas_call(
        body, name="head_and_loss", grid=(s // ts,),
        out_shape=(act, act_bf, act_bf, jax.ShapeDtypeStruct((D, D), F32), jax.ShapeDtypeStruct((8, D), F32)),
        in_specs=[row(D), row(D), row(D), row(D),
                  _full((1, D)), _full((D, D)), _full((1, D))],
        out_specs=(row(D), row(D), row(D), _full((D, D)), _full((8, D))),
        compiler_params=_params("arbitrary"),
    )(o, gate, h1, target, hw, wgo, wf)


def gla_project_backward(dq, dk, dv, dgate, dcum, low, h1, dh2, w1, wgi_q, wgk, bgk):
    s = h1.shape[0]
    ts = ROW_TILE

    def body(dq_ref, dk_ref, dv_ref, dgate_ref, dcum_ref, low_ref, h1_ref, dh2_ref, w1_ref,
             wq_ref, wgk_ref, bgk_ref, dh1_ref, dproj_ref, ggk_ref, small_ref, wgi_ref):
        @pl.when(pl.program_id(0) == 0)
        def _():
            ggk_ref[...] = jnp.zeros_like(ggk_ref)
            small_ref[...] = jnp.zeros_like(small_ref)
            _assemble_gla_in(wq_ref, wgi_ref)

        low = low_ref[...]
        z = _nn(low, wgk_ref[...]) + bgk_ref[...]
        upper_f = _chunk_masks()[1].astype(F32)
        dlg = jnp.concatenate([_nn_exact(upper_f, dcum_ref[r0:r0 + CHUNK, :]) for r0 in range(0, ts, CHUNK)],
                              axis=0)
        dz = dlg * (1.0 / GATE_NORM) * _sigmoid(-z)
        dz_bf = _bf(dz)
        ggk_ref[...] += _tn(low, dz_bf)
        small_ref[1:2, 0:KEY_W] += jnp.sum(dz, axis=0, keepdims=True)
        dlow = _bf(_nt(dz_bf, wgk_ref[...]))
        dproj_ref[:, GLA_MAIN:] = dlow
        dn1 = _nt(dlow, wgi_ref[:, GLA_MAIN:])
        for ref, lo, hi in ((dq_ref, 0, KEY_W), (dk_ref, KEY_W, 2 * KEY_W),
                            (dv_ref, 2 * KEY_W, 2 * KEY_W + D), (dgate_ref, 2 * KEY_W + D, GLA_MAIN)):
            piece = ref[...]
            dproj_ref[:, lo:hi] = piece
            dn1 = dn1 + _nt(piece, wgi_ref[:, lo:hi])
        hv = h1_ref[...]
        r = lax.rsqrt(jnp.mean(hv * hv, axis=-1, keepdims=True) + EPS)
        hhat = hv * r
        small_ref[0:1, :] += jnp.sum(dn1 * hhat, axis=0, keepdims=True)
        dxh = dn1 * w1_ref[...]
        dh1_ref[...] = dh2_ref[...] + r * (dxh - hhat * jnp.mean(dxh * hhat, axis=-1, keepdims=True))

    row = lambda cols: pl.BlockSpec((ts, cols), lambda i: (i, 0))
    return pl.pallas_call(
        body, name="gla_project_backward", grid=(s // ts,),
        out_shape=(jax.ShapeDtypeStruct((s, D), F32), jax.ShapeDtypeStruct((s, GLA_MAIN + RANK_PAD), BF16),
                   jax.ShapeDtypeStruct((RANK_PAD, KEY_W), F32),
                   jax.ShapeDtypeStruct((8, D), F32)),
        in_specs=[row(KEY_W), row(KEY_W), row(D), row(D), row(KEY_W), row(RANK_PAD), row(D), row(D),
                  _full((1, D)), _full((N_CHIPS, D, GLA_IN_QUARTER)), _full((RANK_PAD, KEY_W)),
                  _full((1, KEY_W))],
        out_specs=(row(D), row(GLA_MAIN + RANK_PAD), _full((RANK_PAD, KEY_W)), _full((8, D))),
        scratch_shapes=[pltpu.VMEM((D, GLA_MAIN + RANK_PAD), BF16)],
        compiler_params=_params("arbitrary"),
    )(dq, dk, dv, dgate, dcum, low, h1, dh2, w1, wgi_q, wgk, bgk)


def local_gradients(xs, target, w0, w1, wf, wpi, gw, gb, scale, wpo, gla_quarters, wgk, bgk, hw_tiled, place):
    wgi_q, wgo_q = gla_quarters
    (h1, pooled, gt, n0), (wgi_q,) = pool_forward(xs, w0, wpi, gw, gb, scale, wpo, [wgi_q])
    (qk, v, gate, low, cum, n1), (wgo_q,) = gla_project(h1, w1, wgi_q, wgk, bgk, [wgo_q])
    wgo = wgo_q.reshape(D, D)
    o, states, scores = gla_forward(qk, v, cum)

    dh2, do, dgate, g_gla_out, small_top = head_and_loss(o, gate, h1, target, hw_tiled, wgo, wf)
    dq, dk, dv, dcum = gla_backward(qk, v, cum, do, states, scores)
    dh1, dproj, g_gk_pad, small_gla = gla_project_backward(
        dq, dk, dv, dgate, dcum, low, h1, dh2, w1, wgi_q, wgk, bgk)
    g_gla_in, (gla_out,) = matmul_tn(n1, dproj, "grad_gla_in", (GLA_MAIN + RANK_PAD) // 5, place,
                                     also_reduce=[g_gla_out.reshape(N_CHIPS, D // N_CHIPS, D)])
    ((own_gla_in, sums_gla_in),) = add_halves([g_gla_in], place, "add_halves_gla")
    (dx, dpool, g_pool_out, g_group_w, small_pool), (got_gla_in,) = pool_backward(
        xs, dh1, pooled, gt, w0, wpi, gw, gb, scale, wpo, [sums_gla_in])
    g_pool_in, mix = matmul_tn(n0, dpool, "grad_pool_in", D // 2, place, by_column_tile=True,
                               also_reduce=[g_group_w, g_pool_out.reshape(N_CHIPS, D // N_CHIPS, D)])

    done = mix + [(own_gla_in, got_gla_in), gla_out]
    reduced, total = join_halves(
        g_pool_in, place, [own for own, _ in done], [got for _, got in done],
        small_pool, small_gla, small_top, g_gk_pad)
    return dx, reduced, total


def kernel(x, norm_w, pool_in_w, pool_group_w, pool_group_b, pool_scale, pool_out_w, gla_in_w, gla_gk_w, gla_gk_b, gla_head_norm_w, gla_out_w, final_norm_w, loss_target, m_norm_w, m_pool_in_w, m_pool_group_w, m_pool_group_b, m_pool_scale, m_pool_out_w, m_gla_in_w, m_gla_gk_w, m_gla_gk_b, m_gla_head_norm_w, m_gla_out_w, m_final_norm_w, v_norm_w, v_pool_in_w, v_pool_group_w, v_pool_group_b, v_pool_scale, v_pool_out_w, v_gla_in_w, v_gla_gk_w, v_gla_gk_b, v_gla_head_norm_w, v_gla_out_w, v_final_norm_w):
    xs = x[0]
    target = loss_target[0]
    q_chip = 2 * lax.axis_index("x") + lax.axis_index("y")
    place = jnp.stack([lax.axis_index("c"), q_chip]).astype(jnp.int32)

    (wpi, gw_q, wpo_q, wgi_q, wgo_q), (bgk, hw_tiled, gb, wgk) = allgather_weights(
        [pool_in_w[0], pool_group_w[0].reshape(GROUP_DIM, GROUP_DIM), pool_out_w[0], gla_in_w[0], gla_out_w[0]],
        exchange=(True, True, True, False, False),
        smalls=[gla_gk_b, gla_head_norm_w, pool_group_b[0], gla_gk_w[0]])
    wpo = wpo_q.reshape(D, D)

    w0 = norm_w[0:1]
    w1 = norm_w[1:2]
    wf = final_norm_w.reshape(1, D)

    dx, reduced, total = local_gradients(
        xs, target, w0, w1, wf, wpi, gw_q, gb, pool_scale, wpo, [wgi_q, wgo_q], wgk, bgk, hw_tiled, place)
    r_pool_in, r_group_w, r_pool_out, r_gla_in, r_gla_out = reduced
    r_group_w = r_group_w.reshape(GROUPS, 64, GROUP_DIM)

    turn = lambda a: jnp.transpose(a, (2, 0, 1))
    back = lambda a: jnp.transpose(a, (1, 2, 0))
    as2d = lambda a, w: a.reshape(-1, w.shape[-1])
    big_names = ("pool_in_w", "pool_group_w", "pool_out_w", "gla_in_w", "gla_out_w")
    big_args = [(pool_in_w, r_pool_in[None], m_pool_in_w, v_pool_in_w),
                (pool_group_w, r_group_w[None], m_pool_group_w, v_pool_group_w),
                (pool_out_w, r_pool_out[None], m_pool_out_w, v_pool_out_w),
                (gla_in_w, r_gla_in[None], m_gla_in_w, v_gla_in_w),
                (gla_out_w, r_gla_out[None], m_gla_out_w, v_gla_out_w)]
    to_kernel = lambda n, a, w: turn(a) if n == "gla_in_w" else as2d(a, w)
    from_kernel = lambda n, a, w: back(a) if n == "gla_in_w" else a.reshape(w.shape)
    big_in = [tuple(to_kernel(n, a, p[0]) for a in p) for n, p in zip(big_names, big_args)]
    big_out = adamw(big_in, "adamw")
    big = {n: (from_kernel(n, i[1], p[0]),) + tuple(from_kernel(n, o, p[0]) for o in out)
           for n, p, i, out in zip(big_names, big_args, big_in, big_out)}

    small_names = ("norm_w", "pool_group_b", "pool_scale", "gla_gk_w", "gla_gk_b", "gla_head_norm_w",
                   "final_norm_w")
    small_args = [(norm_w, m_norm_w, v_norm_w),
                  (pool_group_b, m_pool_group_b, v_pool_group_b),
                  (pool_scale, m_pool_scale, v_pool_scale),
                  (gla_gk_w, m_gla_gk_w, v_gla_gk_w),
                  (gla_gk_b, m_gla_gk_b, v_gla_gk_b),
                  (gla_head_norm_w, m_gla_head_norm_w, v_gla_head_norm_w),
                  (final_norm_w, m_final_norm_w, v_final_norm_w)]
    small_out, loss = adamw_small([tuple(as2d(a, p[0]) for a in p) for p in small_args], total, place)
    small = {n: tuple(o.reshape(p[0].shape) for o in out) for n, p, out in zip(small_names, small_args, small_out)}
    results = [
        small["norm_w"],
        big["pool_in_w"],
        big["pool_group_w"],
        small["pool_group_b"],
        small["pool_scale"],
        big["pool_out_w"],
        big["gla_in_w"],
        small["gla_gk_w"],
        small["gla_gk_b"],
        small["gla_head_norm_w"],
        big["gla_out_w"],
        small["final_norm_w"],
    ]
    grads, deltas, new_m, new_v = zip(*results)
    return (loss.reshape(()), dx[None], *grads, *deltas, *new_m, *new_v)
```

```python
import jax
import jax.numpy as jnp
from jax import lax
from jax.experimental import pallas as pl
from jax.experimental.pallas import tpu as pltpu

F32 = jnp.float32
BF16 = jnp.bfloat16
MESH = pl.DeviceIdType.MESH

D = 1024
POOL_WINDOWS = (2, 4, 8, 16)
GROUPS = 4
GROUP_DIM = 256
HEADS = 4
HEAD_K = 128
HEAD_V = 256
KEY_W = 512
CHUNK = 64
GATE_RANK = 16
GATE_NORM = 16.0
GLA_IN = 3088
GLA_MAIN = 3072
RANK_PAD = 128
EPS = 1e-6
HALO = 32

ADAM_LR = 0.001
ADAM_B1 = 0.9
ADAM_B2 = 0.999
ADAM_EPS = 1e-08
ADAM_WD = 0.01
ADAM_STEP = 10

N_CHIPS = 4
N_DEV = 8
GLA_IN_QUARTER = GLA_IN // N_CHIPS

VMEM_LIMIT = 56 * 1024 * 1024


def _nn(a, b):
    return lax.dot_general(a, b, (((1,), (0,)), ((), ())), preferred_element_type=F32)


def _nt(a, b):
    return lax.dot_general(a, b, (((1,), (1,)), ((), ())), preferred_element_type=F32)


def _tn(a, b):
    return lax.dot_general(a, b, (((0,), (0,)), ((), ())), preferred_element_type=F32)


def _nn_exact(a, b):
    return lax.dot_general(a, b, (((1,), (0,)), ((), ())), preferred_element_type=F32,
                           precision=lax.Precision.HIGHEST)


def _bf(a):
    return a.astype(BF16)


def _params(*sem):
    return pltpu.CompilerParams(dimension_semantics=sem, vmem_limit_bytes=VMEM_LIMIT)


def _full(shape):
    return pl.BlockSpec(shape, lambda i: (0,) * len(shape))


def _position():
    return lax.axis_index("x"), lax.axis_index("y"), lax.axis_index("c")


def _gather_small(in_ref, all_ref, send_sems, recv_sems, local_sem):
    x, y, c = _position()
    me = 4 * x + 2 * y + c
    mine = pltpu.make_async_copy(in_ref, all_ref.at[me], local_sem)
    sends = []
    for k in range(N_DEV - 1):
        fx, fy, fc = (k + 1) >> 2 & 1, (k + 1) >> 1 & 1, (k + 1) & 1
        sends.append(pltpu.make_async_remote_copy(
            src_ref=in_ref, dst_ref=all_ref.at[me],
            send_sem=send_sems.at[k], recv_sem=recv_sems.at[k],
            device_id=(x ^ fx, y ^ fy, c ^ fc), device_id_type=MESH))

    def start():
        mine.start()
        for cp in sends:
            cp.start()

    def wait():
        for k in range(N_DEV - 1):
            fx, fy, fc = (k + 1) >> 2 & 1, (k + 1) >> 1 & 1, (k + 1) & 1
            src_dev = 4 * (x ^ fx) + 2 * (y ^ fy) + (c ^ fc)
            pltpu.make_async_remote_copy(
                src_ref=in_ref, dst_ref=all_ref.at[src_dev],
                send_sem=send_sems.at[k], recv_sem=recv_sems.at[k],
                device_id=(x, y, c), device_id_type=MESH).wait_recv()
        for cp in sends:
            cp.wait_send()
        mine.wait()

    return start, wait


SMALL_SEMS = [pltpu.SemaphoreType.DMA((N_DEV - 1,)), pltpu.SemaphoreType.DMA((N_DEV - 1,)),
              pltpu.SemaphoreType.DMA]
VMEM_SPEC = pl.BlockSpec(memory_space=pltpu.VMEM)


def _other_chips(x, y):
    return [(1 - x, y), (x, 1 - y), (1 - x, 1 - y)]


def _any_specs(n):
    return [pl.BlockSpec(memory_space=pl.ANY)] * n


def _halves(rows, c):
    half = rows // 2
    return pl.ds(c * half, half), pl.ds((1 - c) * half, half)


CAST_ROWS = 256


def _gather_copy(out_ref, send_sems, recv_sems, k, quarter, half, to, src=None):
    dst = out_ref.at[quarter, half]
    return pltpu.make_async_remote_copy(
        src_ref=dst if src is None else src, dst_ref=dst,
        send_sem=send_sems.at[k], recv_sem=recv_sems.at[k], device_id=to, device_id_type=MESH)


SMALL_IN_ROWS = 24


def allgather_weights(quarters, exchange, smalls):
    n = len(quarters)
    shapes = [w.shape for w in quarters]
    moved = [i for i in range(n) if exchange[i]]

    def body(*refs):
        w_refs, (gkb_ref, hnw_ref, gb_ref, gkw_ref) = refs[:n], refs[n:n + 4]
        out_refs, (bgk_ref, hw_ref, gbias_ref, wgk_ref) = refs[n + 4:2 * n + 4], refs[2 * n + 4:2 * n + 8]
        refs = refs[2 * n + 8:]
        f32_bufs, bf_bufs = refs[:n], refs[n:2 * n]
        send_sems, recv_sems, local_sems, small_ref, small_all_ref = refs[2 * n:2 * n + 5]
        small_ref[...] = jnp.zeros_like(small_ref)
        small_ref[0:1, :] = gkb_ref[...]
        small_ref[1:2, 0:64] = hnw_ref[...]
        small_ref[2:2 + GROUPS, 0:64] = gb_ref[...]
        small_ref[8:8 + GATE_RANK, :] = gkw_ref[...]
        start_small, wait_small = _gather_small(small_ref, small_all_ref, *refs[2 * n + 5:])
        start_small()
        x, y, c = _position()
        q = 2 * x + y
        sibling = (x, y, 1 - c)
        chips = _other_chips(x, y)

        def copy(k, i, quarter, half, to, src=None):
            return _gather_copy(out_refs[i], send_sems, recv_sems, k * n + i, quarter, half, to, src)

        loads = [pltpu.make_async_copy(w_refs[i], f32_bufs[i], local_sems.at[i]) for i in range(n)]
        for cp in loads:
            cp.start()
        keeps, sends = [], []
        for i in range(n):
            loads[i].wait()
            for r0 in range(0, shapes[i][0], CAST_ROWS):
                bf_bufs[i][r0:r0 + CAST_ROWS, :] = _bf(f32_bufs[i][r0:r0 + CAST_ROWS, :])
            keep = pltpu.make_async_copy(bf_bufs[i], out_refs[i].at[q], local_sems.at[n + i])
            keep.start()
            keeps.append(keep)
            if not exchange[i]:
                continue
            mine, _ = _halves(shapes[i][0], c)
            for j, chip in enumerate(chips):
                cp = copy(j, i, q, mine, (*chip, c), src=bf_bufs[i].at[mine])
                cp.start()
                sends.append(cp)
        for j, chip in enumerate(chips):
            qj = 2 * chip[0] + chip[1]
            for i in moved:
                mine, _ = _halves(shapes[i][0], c)
                copy(j, i, qj, mine, (x, y, c)).wait_recv()
                cp = copy(3 + j, i, qj, mine, sibling)
                cp.start()
                sends.append(cp)
        for j, chip in enumerate(chips):
            qj = 2 * chip[0] + chip[1]
            for i in moved:
                _, other = _halves(shapes[i][0], c)
                copy(3 + j, i, qj, other, (x, y, c)).wait_recv()
        wait_small()
        wgk_ref[...] = jnp.zeros_like(wgk_ref)
        for j in range(N_CHIPS):
            block = small_all_ref.at[2 * j]
            bgk_ref[:, 128 * j:128 * (j + 1)] = block[0:1, :]
            for h in range(HEADS):
                hw_ref[:, HEAD_V * h + 64 * j:HEAD_V * h + 64 * (j + 1)] = block[1:2, 0:64]
            for g in range(GROUPS):
                gbias_ref[:, GROUP_DIM * g + 64 * j:GROUP_DIM * g + 64 * (j + 1)] = block[2 + g:3 + g, 0:64]
            wgk_ref[0:GATE_RANK, 128 * j:128 * (j + 1)] = _bf(block[8:8 + GATE_RANK, :])
        for cp in sends:
            cp.wait_send()
        for cp in keeps:
            cp.wait()

    outs = pl.pallas_call(
        body, name="allgather_weights",
        out_shape=[jax.ShapeDtypeStruct((N_CHIPS, *s), BF16) for s in shapes]
                  + [jax.ShapeDtypeStruct((1, KEY_W), F32), jax.ShapeDtypeStruct((1, D), F32),
                     jax.ShapeDtypeStruct((1, D), F32), jax.ShapeDtypeStruct((RANK_PAD, KEY_W), BF16)],
        in_specs=_any_specs(n) + [VMEM_SPEC] * 4, out_specs=_any_specs(n) + [VMEM_SPEC] * 4,
        scratch_shapes=([pltpu.VMEM(s, F32) for s in shapes] + [pltpu.VMEM(s, BF16) for s in shapes]
                        + [pltpu.SemaphoreType.DMA((6 * n,)), pltpu.SemaphoreType.DMA((6 * n,)),
                           pltpu.SemaphoreType.DMA((2 * n,)), pltpu.VMEM((SMALL_IN_ROWS, 128), F32),
                           pltpu.VMEM((N_DEV, SMALL_IN_ROWS, 128), F32)] + SMALL_SEMS),
        compiler_params=pltpu.CompilerParams(vmem_limit_bytes=VMEM_LIMIT),
    )(*quarters, *smalls)
    return outs[:n], outs[n:]


def _scatter_copies(b_refs, got_refs, send_sems, recv_sems):
    n = len(b_refs)
    x, y, c = _position()
    copies = []
    for j, chip in enumerate(_other_chips(x, y)):
        qj = 2 * chip[0] + chip[1]
        for i in range(n):
            copies.append(pltpu.make_async_remote_copy(
                src_ref=b_refs[i].at[qj], dst_ref=got_refs[i].at[j],
                send_sem=send_sems.at[j * n + i], recv_sem=recv_sems.at[j * n + i],
                device_id=(*chip, c), device_id_type=MESH))
    return copies


def _scatter_shapes(chip_sums):
    return [jax.ShapeDtypeStruct((N_CHIPS - 1, *b.shape[1:]), BF16) for b in chip_sums]


ADD_ROWS = 512
ADD_HALVES_ROWS = 256


def _spans(counts):
    starts, total = [], 0
    for count in counts:
        starts.append(total)
        total += count
    return starts, total


def _local_step(t, start, count):
    return jnp.clip(t - start, 0, count - 1)


def add_halves(grads, place, name):
    n = len(grads)
    whole = [len(g.shape) == 2 for g in grads]
    halves = [g.shape[-2] // 2 for g in grads]
    cols = [GLA_IN_QUARTER if w else g.shape[-1] for g, w in zip(grads, whole)]
    rbs = [min(ADD_HALVES_ROWS, h) for h in halves]
    counts = [h // rb for h, rb in zip(halves, rbs)]
    starts, total = _spans(counts)
    half_shapes = [(*g.shape[:-2], h, g.shape[-1]) for g, h in zip(grads, halves)]

    def rows_of(ref, i, start):
        return ref.at[pl.ds(start, rbs[i])] if whole[i] else ref.at[:, pl.ds(start, rbs[i])]

    def body(place_ref, *refs):
        a_refs, o_refs = refs[:n], refs[n:2 * n]
        f_refs, h_refs = refs[2 * n:3 * n], refs[3 * n:4 * n]
        send_refs, their_refs, (send_sems, recv_sems) = refs[4 * n:5 * n], refs[5 * n:6 * n], refs[6 * n:]
        t = pl.program_id(0)
        q = place_ref[1]
        x, y, c = _position()
        copies = [[pltpu.make_async_remote_copy(
            src_ref=rows_of(send_refs[i], i, k * rbs[i]), dst_ref=rows_of(their_refs[i], i, k * rbs[i]),
            send_sem=send_sems.at[starts[i] + k], recv_sem=recv_sems.at[starts[i] + k],
            device_id=(x, y, 1 - c), device_id_type=MESH) for k in range(counts[i])] for i in range(n)]

        for i in range(n):
            for k in range(counts[i]):
                @pl.when(t == starts[i] + k)
                def _(i=i, k=k):
                    rows_of(send_refs[i], i, k * rbs[i])[...] = _bf(o_refs[i][...])
                    copies[i][k].start()

        for i in range(n):
            for k in range(counts[i]):
                @pl.when(t == starts[i] + k + 1)
                def _(i=i, k=k):
                    copies[i][k].wait_recv()
                    b_ref = rows_of(their_refs[i], i, k * rbs[i])
                    if not whole[i]:
                        h_refs[i][...] = _bf(a_refs[i][...] + b_ref[...].astype(F32))
                        f_refs[i][...] = a_refs[i][q] + b_ref[q].astype(F32)
                        return
                    total_i = a_refs[i][...] + b_ref[...].astype(F32)
                    for k4 in range(N_CHIPS):
                        piece = total_i[:, k4 * cols[i]:(k4 + 1) * cols[i]]
                        h_refs[i][k4] = _bf(piece)

                        @pl.when(q == k4)
                        def _():
                            f_refs[i][...] = piece

        @pl.when(t == total)
        def _():
            for of_matrix in copies:
                for cp in of_matrix:
                    cp.wait_send()

    def specs(i):
        sent = lambda t: _local_step(t, starts[i], counts[i])
        added = lambda t: _local_step(t - 1, starts[i], counts[i])
        by_quarter = (N_CHIPS, rbs[i], cols[i])
        block = (rbs[i], grads[i].shape[-1]) if whole[i] else by_quarter
        lead = () if whole[i] else (0,)
        mine = pl.BlockSpec(block, lambda t, place: (*lead, place[0] * counts[i] + added(t), 0))
        other = pl.BlockSpec(block, lambda t, place: (*lead, (1 - place[0]) * counts[i] + sent(t), 0))
        sums = pl.BlockSpec(by_quarter, lambda t, place: (0, added(t), 0))
        own = pl.BlockSpec(by_quarter[1:], lambda t, place: (added(t), 0))
        return mine, other, own, sums

    all_specs = [specs(i) for i in range(n)]
    outs = pl.pallas_call(
        body, name=name,
        grid_spec=pltpu.PrefetchScalarGridSpec(
            num_scalar_prefetch=1, grid=(total + 1,),
            in_specs=[sp[0] for sp in all_specs] + [sp[1] for sp in all_specs],
            out_specs=[sp[2] for sp in all_specs] + [sp[3] for sp in all_specs],
            scratch_shapes=[pltpu.VMEM(sh, BF16) for sh in half_shapes] + [pltpu.VMEM(sh, BF16) for sh in half_shapes]
                           + [pltpu.SemaphoreType.DMA((total,)), pltpu.SemaphoreType.DMA((total,))]),
        out_shape=[jax.ShapeDtypeStruct((h, cl), F32) for h, cl in zip(halves, cols)]
                  + [jax.ShapeDtypeStruct((N_CHIPS, h, cl), BF16) for h, cl in zip(halves, cols)],
        compiler_params=_params("arbitrary"),
    )(place, *grads, *grads)
    return list(zip(outs[:n], outs[n:]))


SMALL_SUM_ROWS = 16


def join_halves(grad, place, owns, gots, small_pool, small_gla, small_top, g_gk_pad):
    n = len(owns)
    half, cols = grad.shape[1] // 2, grad.shape[2]
    rb = min(ADD_HALVES_ROWS, half)
    sent = half // rb
    shapes = [g.shape for g in gots] + [(N_CHIPS - 1, half, cols)]
    rbs = [min(ADD_ROWS, sh[1]) for sh in shapes]
    counts = [sh[1] // r for sh, r in zip(shapes, rbs)]
    starts, joined = _spans(counts)
    first_join = sent + 1
    steps = first_join + joined

    def body(place_ref, *refs):
        refs = iter(refs)
        take = lambda count: [next(refs) for _ in range(count)]
        (a_ref, o_ref), o_refs, g_refs = take(2), take(n), take(n)
        pool_ref, gla_ref, top_ref, gk_ref = take(4)
        out_refs, (total_ref,) = take(n + 1), take(1)
        send_buf, their_buf, sums_buf, got_buf = take(4)
        sum_refs = take(n + 1)
        to_core_sems, from_core_sems, to_chip_sems, from_chip_sems, local_sems, send_sems, recv_sems = take(7)
        all_ref, small_ref = take(2)
        t = pl.program_id(0)
        q = place_ref[1]
        x, y, c = _position()
        start_small, wait_small = _gather_small(small_ref, all_ref, *refs)
        block = lambda k: pl.ds(k * rb, rb)

        def to_core(k):
            return pltpu.make_async_remote_copy(
                src_ref=send_buf.at[:, block(k)], dst_ref=their_buf.at[:, block(k)],
                send_sem=to_core_sems.at[k], recv_sem=from_core_sems.at[k],
                device_id=(x, y, 1 - c), device_id_type=MESH)

        def to_owners(k):
            return [pltpu.make_async_remote_copy(
                src_ref=sums_buf.at[2 * chip[0] + chip[1], block(k)], dst_ref=got_buf.at[j, block(k)],
                send_sem=to_chip_sems.at[3 * k + j], recv_sem=from_chip_sems.at[3 * k + j],
                device_id=(*chip, c), device_id_type=MESH) for j, chip in enumerate(_other_chips(x, y))]

        def copies(i, k):
            src = sum_refs[i].at[pl.ds(k * rbs[i], rbs[i])]
            rows = pl.ds(c * shapes[i][1] + k * rbs[i], rbs[i])
            return (pltpu.make_async_copy(src, out_refs[i].at[rows], local_sems.at[starts[i] + k]),
                    pltpu.make_async_remote_copy(
                        src_ref=src, dst_ref=out_refs[i].at[rows],
                        send_sem=send_sems.at[starts[i] + k], recv_sem=recv_sems.at[starts[i] + k],
                        device_id=(x, y, 1 - c), device_id_type=MESH))

        @pl.when(t == 0)
        def _():
            small_ref[0:3, :] = pool_ref[0:3, :]
            small_ref[3:5, :] = gla_ref[0:2, :]
            small_ref[5:8, :] = top_ref[0:3, :]
            for r in range(GATE_RANK):
                small_ref[8 + r // 2:9 + r // 2, (r % 2) * KEY_W:(r % 2 + 1) * KEY_W] = gk_ref[r:r + 1, :]
            start_small()

        for k in range(sent):
            @pl.when(t == k)
            def _(k=k):
                send_buf[:, k * rb:(k + 1) * rb, :] = _bf(o_ref[...])
                to_core(k).start()

        for k in range(sent):
            @pl.when(t == k + 1)
            def _(k=k):
                to_core(k).wait_recv()
                theirs = their_buf.at[:, block(k)]
                sums_buf[:, k * rb:(k + 1) * rb, :] = _bf(a_ref[...] + theirs[...].astype(F32))
                sum_refs[n][k * rb:(k + 1) * rb, :] = a_ref[q] + theirs[q].astype(F32)
                for cp in to_owners(k):
                    cp.start()

        for i in range(n + 1):
            for k in range(counts[i]):
                @pl.when(t == first_join + starts[i] + k)
                def _(i=i, k=k):
                    rows = slice(k * rbs[i], (k + 1) * rbs[i])
                    if i < n:
                        total_i = o_refs[i][...]
                        arrived = [g_refs[i][j] for j in range(N_CHIPS - 1)]
                    else:
                        if k == 0:
                            for kk in range(sent):
                                for cp in to_owners(kk):
                                    cp.wait_recv()
                        total_i = sum_refs[n][rows, :]
                        arrived = [got_buf[j, rows, :] for j in range(N_CHIPS - 1)]
                    for part in arrived:
                        total_i = total_i + part.astype(F32)
                    sum_refs[i][rows, :] = total_i
                    for cp in copies(i, k):
                        cp.start()

        @pl.when(t == steps - 1)
        def _():
            wait_small()
            small_total = all_ref[0]
            for dev in range(1, N_DEV):
                small_total = small_total + all_ref[dev]
            total_ref[...] = small_total
            for k in range(sent):
                to_core(k).wait_send()
                for cp in to_owners(k):
                    cp.wait_send()
            for i in range(n + 1):
                for k in range(counts[i]):
                    for cp in copies(i, k):
                        cp.wait()

    def specs(i):
        step = lambda t: _local_step(t - first_join, starts[i], counts[i])
        return (pl.BlockSpec((rbs[i], shapes[i][2]), lambda t, place: (step(t), 0)),
                pl.BlockSpec((N_CHIPS - 1, rbs[i], shapes[i][2]), lambda t, place: (0, step(t), 0)))

    by_quarter = (N_CHIPS, rb, cols)
    mine = pl.BlockSpec(by_quarter, lambda t, place: (0, place[0] * sent + jnp.clip(t - 1, 0, sent - 1), 0))
    other = pl.BlockSpec(by_quarter, lambda t, place: (0, (1 - place[0]) * sent + jnp.clip(t, 0, sent - 1), 0))
    all_specs = [specs(i) for i in range(n)]
    sems = lambda count: pltpu.SemaphoreType.DMA((count,))
    outs = pl.pallas_call(
        body, name="join_halves",
        grid_spec=pltpu.PrefetchScalarGridSpec(
            num_scalar_prefetch=1, grid=(steps,),
            in_specs=[mine, other] + [sp[0] for sp in all_specs] + [sp[1] for sp in all_specs] + [VMEM_SPEC] * 4,
            out_specs=_any_specs(n + 1) + [VMEM_SPEC],
            scratch_shapes=[pltpu.VMEM((N_CHIPS, half, cols), BF16) for _ in range(3)]
                           + [pltpu.VMEM(shapes[n], BF16)] + [pltpu.VMEM(sh[1:], F32) for sh in shapes]
                           + [sems(sent), sems(sent), sems(3 * sent), sems(3 * sent),
                              sems(joined), sems(joined), sems(joined),
                              pltpu.VMEM((N_DEV, SMALL_SUM_ROWS, D), F32), pltpu.VMEM((SMALL_SUM_ROWS, D), F32)]
                           + SMALL_SEMS),
        out_shape=[jax.ShapeDtypeStruct((2 * sh[1], sh[2]), F32) for sh in shapes]
                  + [jax.ShapeDtypeStruct((SMALL_SUM_ROWS, D), F32)],
        compiler_params=_params("arbitrary"),
    )(place, grad, grad, *owns, *gots, small_pool, small_gla, small_top, g_gk_pad)
    return [outs[n]] + list(outs[:n]), outs[n + 1]


def _adam_math(w, g, m, v):
    m = ADAM_B1 * m + (1.0 - ADAM_B1) * g
    v = ADAM_B2 * v + (1.0 - ADAM_B2) * (g * g)
    m_hat = m / (1.0 - ADAM_B1 ** ADAM_STEP)
    v_hat = v / (1.0 - ADAM_B2 ** ADAM_STEP)
    delta = -ADAM_LR * (m_hat / (jnp.sqrt(v_hat) + ADAM_EPS) + ADAM_WD * w)
    return delta, m, v


ADAM_BLOCK_BYTES = 2 ** 19
ADAM_MOST_STEPS = 8


def adamw(params, name):
    n = len(params)
    shapes = [p[0].shape for p in params]

    def tile_rows(shape):
        rows, cols = shape[0], shape[-1]
        aligned = 1 if len(shape) == 3 else 8
        divisors = [t for t in range(aligned, rows + 1, aligned) if rows % t == 0]
        tile = max(t for t in divisors if t * cols * 4 <= ADAM_BLOCK_BYTES)
        if rows // tile > ADAM_MOST_STEPS:
            tile = min(t for t in divisors if rows // t <= ADAM_MOST_STEPS)
        return tile

    tiles = [tile_rows(sh) for sh in shapes]
    counts = [sh[0] // tl for sh, tl in zip(shapes, tiles)]
    starts, total = _spans(counts)

    def body(*refs):
        ins, outs = refs[:4 * n], refs[4 * n:]
        t = pl.program_id(0)
        for i in range(n):
            @pl.when((t >= starts[i]) & (t < starts[i] + counts[i]))
            def _(i=i):
                w_ref, g_ref, m_ref, v_ref = ins[4 * i:4 * i + 4]
                d, nm, nv = _adam_math(w_ref[...], g_ref[...], m_ref[...], v_ref[...])
                outs[3 * i][...] = d
                outs[3 * i + 1][...] = nm
                outs[3 * i + 2][...] = nv

    def spec(i):
        block = (tiles[i],) + shapes[i][1:]
        zeros = (0,) * (len(block) - 1)
        return pl.BlockSpec(block, lambda t: (_local_step(t, starts[i], counts[i]),) + zeros)

    outs = pl.pallas_call(
        body, name=name, grid=(total,),
        out_shape=[jax.ShapeDtypeStruct(sh, F32) for sh in shapes for _ in range(3)],
        in_specs=[spec(i) for i in range(n) for _ in range(4)],
        out_specs=[spec(i) for i in range(n) for _ in range(3)],
        compiler_params=_params("arbitrary"),
    )(*[a for p in params for a in p])
    return [tuple(outs[3 * i:3 * i + 3]) for i in range(n)]


def adamw_small(params, total, place):
    n = len(params)

    def cut_gradients(total_ref, q, g_refs):
        g_norm, g_group_b, g_scale, g_gk_w, g_gk_b, g_head_norm, g_final = g_refs
        g_norm[0:1, :] = total_ref[0:1, :]
        g_norm[1:2, :] = total_ref[3:4, :]
        g_scale[...] = total_ref[1:2, :]
        g_final[...] = total_ref[5:6, :]
        g_gk_b[...] = total_ref[4:5, pl.ds(pl.multiple_of(q * 128, 128), 128)]
        for r in range(GATE_RANK):
            lanes = pl.ds(pl.multiple_of((r % 2) * KEY_W + q * 128, 128), 128)
            g_gk_w[r:r + 1, :] = total_ref[8 + r // 2:9 + r // 2, lanes]
        for k in range(N_CHIPS):
            @pl.when(q == k)
            def _(k=k):
                g_head_norm[...] = total_ref[6:7, 64 * k:64 * (k + 1)]
                for g in range(GROUPS):
                    g_group_b[g:g + 1, :] = total_ref[2:3, GROUP_DIM * g + 64 * k:GROUP_DIM * g + 64 * (k + 1)]

    def body(place_ref, total_ref, *refs):
        ins, outs = refs[:3 * n], refs[3 * n:]
        outs[4 * n][...] = total_ref[7:8, 0:1]
        cut_gradients(total_ref, place_ref[1], outs[0:4 * n:4])
        for k in range(n):
            w_ref, m_ref, v_ref = ins[3 * k:3 * k + 3]
            d, nm, nv = _adam_math(w_ref[...], outs[4 * k][...], m_ref[...], v_ref[...])
            outs[4 * k + 1][...] = d
            outs[4 * k + 2][...] = nm
            outs[4 * k + 3][...] = nv

    flat = [a for p in params for a in p]
    outs = pl.pallas_call(
        body, name="adamw_small",
        out_shape=[jax.ShapeDtypeStruct(p[0].shape, F32) for p in params for _ in range(4)]
                  + [jax.ShapeDtypeStruct((1, 1), F32)],
        in_specs=[pl.BlockSpec(memory_space=pltpu.SMEM)] + [VMEM_SPEC] * (1 + 3 * n),
        out_specs=[VMEM_SPEC] * (4 * n + 1),
    )(place, total, *flat)
    return [tuple(outs[4 * k:4 * k + 4]) for k in range(n)], outs[4 * n]


def matmul_tn(a, b, name, tile_n, place, by_column_tile=False, also_reduce=()):
    s, m = a.shape
    n = b.shape[1]
    steps = n // tile_n
    n_red = len(also_reduce)
    assert n_red == 0 or steps >= 2
    halves = [(g.shape[1] // 2, g.shape[2]) for g in also_reduce]
    if by_column_tile:
        out_shape = jax.ShapeDtypeStruct((steps, m, tile_n), F32)
        out_spec = pl.BlockSpec((None, m, tile_n), lambda j, place: (j, 0, 0))
    else:
        out_shape = jax.ShapeDtypeStruct((m, n), F32)
        out_spec = pl.BlockSpec((m, tile_n), lambda j, place: (0, j))

    def body(place_ref, a_ref, b_ref, *rest):
        rest = iter(rest)
        take = lambda count: [next(rest) for _ in range(count)]
        mine_refs, other_refs, (out_ref,) = take(n_red), take(n_red), take(1)
        own_refs, got_refs = take(n_red), take(n_red)
        send_bufs, their_bufs, sums_bufs = take(n_red), take(n_red), take(n_red)
        sems = list(rest)
        j = pl.program_id(0)
        q = place_ref[1]
        x, y, c = _position()

        def to_core(i):
            return pltpu.make_async_remote_copy(
                src_ref=send_bufs[i], dst_ref=their_bufs[i], send_sem=sems[0].at[i], recv_sem=sems[1].at[i],
                device_id=(x, y, 1 - c), device_id_type=MESH)

        def to_owners(i):
            return [pltpu.make_async_remote_copy(
                src_ref=sums_bufs[i].at[2 * chip[0] + chip[1]], dst_ref=got_refs[i].at[k],
                send_sem=sems[2].at[3 * i + k], recv_sem=sems[3].at[3 * i + k],
                device_id=(*chip, c), device_id_type=MESH) for k, chip in enumerate(_other_chips(x, y))]

        if n_red:
            @pl.when(j == 0)
            def _():
                for i in range(n_red):
                    send_bufs[i][...] = _bf(other_refs[i][...])
                    to_core(i).start()

            @pl.when(j == 1)
            def _():
                for i in range(n_red):
                    to_core(i).wait_recv()
                    sums_bufs[i][...] = _bf(mine_refs[i][...] + their_bufs[i][...].astype(F32))
                    own_refs[i][...] = mine_refs[i][q] + their_bufs[i][q].astype(F32)
                    for cp in to_owners(i):
                        cp.start()

        out_ref[...] = _tn(a_ref[...], b_ref[...])

        if n_red:
            @pl.when(j == steps - 1)
            def _():
                for i in range(n_red):
                    to_core(i).wait_send()
                    for cp in to_owners(i):
                        cp.wait()

    by_quarter = [(N_CHIPS, *h) for h in halves]
    outs = pl.pallas_call(
        body, name=name,
        grid_spec=pltpu.PrefetchScalarGridSpec(
            num_scalar_prefetch=1, grid=(steps,),
            in_specs=[pl.BlockSpec((s, m), lambda j, place: (0, 0)), pl.BlockSpec((s, tile_n), lambda j, place: (0, j))]
                     + [pl.BlockSpec(sh, lambda j, place: (0, place[0], 0)) for sh in by_quarter]
                     + [pl.BlockSpec(sh, lambda j, place: (0, 1 - place[0], 0)) for sh in by_quarter],
            out_specs=[out_spec] + [pl.BlockSpec(h, lambda j, place: (0, 0)) for h in halves] + _any_specs(n_red),
            scratch_shapes=[pltpu.VMEM(sh, BF16) for sh in by_quarter * 3]
                           + ([pltpu.SemaphoreType.DMA((n_red,)), pltpu.SemaphoreType.DMA((n_red,)),
                               pltpu.SemaphoreType.DMA((3 * n_red,)), pltpu.SemaphoreType.DMA((3 * n_red,))]
                              if n_red else [])),
        out_shape=[out_shape] + [jax.ShapeDtypeStruct(h, F32) for h in halves]
                  + [jax.ShapeDtypeStruct((N_CHIPS - 1, *h), BF16) for h in halves],
        compiler_params=_params("arbitrary"),
    )(place, a, b, *also_reduce, *also_reduce)
    return outs[0], list(zip(outs[1:1 + n_red], outs[1 + n_red:]))


ROW_TILE = 512


def _row_index(tile, rows):
    return tile * rows + lax.broadcasted_iota(jnp.int32, (rows, 1), 0)


def _inverse_counts(t_glob):
    return [1.0 / jnp.minimum(t_glob + 1, w).astype(F32) for w in POOL_WINDOWS]


def _sigmoid(z):
    return 1.0 / (1.0 + jnp.exp(-z))


def _trailing_sums(src, tmp, cols, window, rows):
    bufs = (src, tmp)
    span, level, start = 1, 0, 0
    while span < window:
        start += 8
        a, b = bufs[level % 2], bufs[(level + 1) % 2]
        n = HALO + rows - start
        b[start:start + n, cols] = a[start:start + n, cols] + a[start - span:start - span + n, cols]
        span, level = 2 * span, level + 1
    return bufs[level % 2][HALO:HALO + rows, cols]


def _leading_sums(src, tmp, cols, window, rows):
    bufs = (src, tmp)
    span, level, n = 1, 0, rows + HALO
    while span < window:
        n -= 8
        a, b = bufs[level % 2], bufs[(level + 1) % 2]
        b[0:n, cols] = a[0:n, cols] + a[span:span + n, cols]
        span, level = 2 * span, level + 1
    return bufs[level % 2][0:rows, cols]


def gather_in_background(step, last, out_refs, send_sems, recv_sems, finish):
    n = len(out_refs)
    x, y, c = _position()
    q = 2 * x + y
    chips = _other_chips(x, y)

    def copy(k, i, quarter, half, to):
        return _gather_copy(out_refs[i], send_sems, recv_sems, k * n + i, quarter, half, to)

    if not finish:
        @pl.when(step == 0)
        def _():
            for i in range(n):
                mine, _ = _halves(out_refs[i].shape[1], c)
                for j, chip in enumerate(chips):
                    copy(j, i, q, mine, (*chip, c)).start()

        @pl.when(step == last)
        def _():
            for j, chip in enumerate(chips):
                qj = 2 * chip[0] + chip[1]
                for i in range(n):
                    mine, _ = _halves(out_refs[i].shape[1], c)
                    copy(j, i, qj, mine, (x, y, c)).wait_recv()
                    copy(3 + j, i, qj, mine, (x, y, 1 - c)).start()
        return

    @pl.when(step == last)
    def _():
        for j, chip in enumerate(chips):
            qj = 2 * chip[0] + chip[1]
            for i in range(n):
                mine, other = _halves(out_refs[i].shape[1], c)
                copy(3 + j, i, qj, other, (x, y, c)).wait_recv()
                copy(j, i, q, mine, (x, y, c)).wait_send()
                copy(3 + j, i, qj, mine, (x, y, c)).wait_send()


def _group_matrix(gw_ref, g):
    rows = GROUP_DIM // N_CHIPS
    return jnp.concatenate([gw_ref[j, rows * g:rows * (g + 1), :] for j in range(N_CHIPS)], axis=0)


def pool_forward(x, w0, wpi, gw, gb, scale, wpo, later):
    s = x.shape[0]
    ts = ROW_TILE
    nt = s // ts
    assert nt >= 2
    n_later = len(later)

    def body(x_ref, w0_ref, wpi_ref, gw_ref, gb_ref, sc_ref, wpo_ref, *rest):
        rest = rest[n_later:]
        h1_ref, pooled_ref, gt_ref, n0_ref = rest[:4]
        later_refs = rest[4:4 + n_later]
        ubuf, tbuf, hist, send_sems, recv_sems = rest[4 + n_later:]
        i = pl.program_id(0)
        gather_in_background(i, nt - 1, later_refs, send_sems, recv_sems, finish=False)
        xv = x_ref[...]
        r = lax.rsqrt(jnp.mean(xv * xv, axis=-1, keepdims=True) + EPS)
        n0 = _bf(xv * r * w0_ref[...])
        n0_ref[...] = n0
        u = jnp.concatenate([_nn(n0, wpi_ref[0]), _nn(n0, wpi_ref[1])], axis=-1)
        gt = jnp.concatenate([_nn(n0, wpi_ref[2]), _nn(n0, wpi_ref[3])], axis=-1)
        gt_ref[...] = gt

        @pl.when(i == 0)
        def _():
            hist[...] = jnp.zeros_like(hist)

        ubuf[0:HALO, :] = hist[...]
        ubuf[HALO:HALO + ts, :] = u
        hist[...] = u[ts - HALO:, :]
        inv = _inverse_counts(_row_index(i, ts))
        mixed = []
        for g, w in enumerate(POOL_WINDOWS):
            cols = slice(g * GROUP_DIM, (g + 1) * GROUP_DIM)
            pooled = _bf(_trailing_sums(ubuf, tbuf, cols, w, ts) * inv[g] - u[:, cols])
            pooled_ref[:, cols] = pooled
            mixed.append(_nn(pooled, _group_matrix(gw_ref, g)))
        mixed = jnp.concatenate(mixed, axis=-1) + gb_ref[...]
        y = mixed * sc_ref[...] * (gt * _sigmoid(gt))
        h1_ref[...] = xv + _nn(_bf(y), wpo_ref[...])
        gather_in_background(i, nt - 1, later_refs, send_sems, recv_sems, finish=True)

    row = lambda cols: pl.BlockSpec((ts, cols), lambda i: (i, 0))
    outs = pl.pallas_call(
        body, name="pool_forward", grid=(nt,),
        out_shape=[jax.ShapeDtypeStruct((s, D), F32), jax.ShapeDtypeStruct((s, D), BF16),
                   jax.ShapeDtypeStruct((s, D), F32), jax.ShapeDtypeStruct((s, D), BF16)]
                  + [jax.ShapeDtypeStruct(a.shape, a.dtype) for a in later],
        in_specs=[row(D), _full((1, D)), _full((N_CHIPS, D, D // 2)), _full((GROUPS, GROUP_DIM, GROUP_DIM)),
                  _full((1, D)), _full((1, D)), _full((D, D))] + _any_specs(n_later),
        out_specs=[row(D), row(D), row(D), row(D)] + _any_specs(n_later),
        input_output_aliases={7 + k: 4 + k for k in range(n_later)},
        scratch_shapes=[pltpu.VMEM((HALO + ts, D), F32), pltpu.VMEM((HALO + ts, D), F32),
                        pltpu.VMEM((HALO, D), F32),
                        pltpu.SemaphoreType.DMA((6 * n_later,)), pltpu.SemaphoreType.DMA((6 * n_later,))],
        compiler_params=_params("arbitrary"),
    )(x, w0, wpi, gw, gb, scale, wpo, *later)
    return outs[:4], outs[4:]


def pool_backward(x, dh1, pooled, gt, w0, wpi, gw, gb, scale, wpo, chip_sums):
    s = x.shape[0]
    ts = ROW_TILE
    nt = s // ts
    n_sums = len(chip_sums)

    def body(x_ref, dh1_ref, pooled_ref, gt_ref, w0_ref, wpi_ref, gw_ref, gb_ref, sc_ref, wpo_ref, *rest):
        sum_refs, rest = rest[:n_sums], rest[n_sums:]
        dx_ref, dproj_ref, gpo_ref, ggw_ref, small_ref = rest[:5]
        got_refs = rest[5:5 + n_sums]
        ebuf, tbuf, ahead, send_sems, recv_sems = rest[5 + n_sums:]
        i = pl.program_id(0)
        copies = _scatter_copies(sum_refs, got_refs, send_sems, recv_sems)

        @pl.when(i == 0)
        def _():
            for cp in copies:
                cp.start()

        @pl.when(i == 0)
        def _():
            gpo_ref[...] = jnp.zeros_like(gpo_ref)
            ggw_ref[...] = jnp.zeros_like(ggw_ref)
            small_ref[...] = jnp.zeros_like(small_ref)
            ahead[...] = jnp.zeros_like(ahead)

        dh1 = dh1_ref[...]
        dh1_bf = _bf(dh1)
        gt = gt_ref[...]
        sc = sc_ref[...]
        dy = _nt(dh1_bf, wpo_ref[...])
        pooled_bf = []
        mixed = []
        for g in range(GROUPS):
            cols = slice(g * GROUP_DIM, (g + 1) * GROUP_DIM)
            pb = pooled_ref[:, cols]
            pooled_bf.append(pb)
            mixed.append(_nn(pb, _group_matrix(gw_ref, g)))
        mixed = jnp.concatenate(mixed, axis=-1) + gb_ref[...]
        sg = _sigmoid(gt)
        silu = gt * sg
        gpo_ref[...] += _tn(_bf(mixed * sc * silu), dh1_bf)
        dmixed = dy * sc * silu
        dgt = dy * mixed * sc * (sg * (1.0 + gt * (1.0 - sg)))
        dproj_ref[:, D:] = _bf(dgt)
        small_ref[1:2, :] += jnp.sum(dy * mixed * silu, axis=0, keepdims=True)
        small_ref[2:3, :] += jnp.sum(dmixed, axis=0, keepdims=True)

        inv = _inverse_counts(_row_index(nt - 1 - i, ts))
        rows_q = GROUP_DIM // N_CHIPS
        ebuf[ts:ts + HALO, :] = ahead[...]
        dpooled = []
        for g in range(GROUPS):
            cols = slice(g * GROUP_DIM, (g + 1) * GROUP_DIM)
            dm = _bf(dmixed[:, cols])
            ggw = _tn(pooled_bf[g], dm)
            for j in range(N_CHIPS):
                ggw_ref[j, rows_q * g:rows_q * (g + 1), :] += ggw[rows_q * j:rows_q * (j + 1), :]
            dp = _nt(dm, _group_matrix(gw_ref, g))
            dpooled.append(dp)
            ebuf[0:ts, cols] = dp * inv[g]
        ahead[...] = ebuf[0:HALO, :]
        du = []
        for g, w in enumerate(POOL_WINDOWS):
            cols = slice(g * GROUP_DIM, (g + 1) * GROUP_DIM)
            du.append(_leading_sums(ebuf, tbuf, cols, w, ts) - dpooled[g])
        du = _bf(jnp.concatenate(du, axis=-1))
        dproj_ref[:, :D] = du
        dgt_bf = _bf(dgt)
        half = D // 2
        dn0 = (_nt(du[:, :half], wpi_ref[0]) + _nt(du[:, half:], wpi_ref[1])
               + _nt(dgt_bf[:, :half], wpi_ref[2]) + _nt(dgt_bf[:, half:], wpi_ref[3]))

        xv = x_ref[...]
        r = lax.rsqrt(jnp.mean(xv * xv, axis=-1, keepdims=True) + EPS)
        xhat = xv * r
        small_ref[0:1, :] += jnp.sum(dn0 * xhat, axis=0, keepdims=True)
        dxh = dn0 * w0_ref[...]
        dx_ref[...] = dh1 + r * (dxh - xhat * jnp.mean(dxh * xhat, axis=-1, keepdims=True))

        @pl.when(i == nt - 1)
        def _():
            for cp in copies:
                cp.wait()

    row = lambda cols: pl.BlockSpec((ts, cols), lambda i: (nt - 1 - i, 0))
    outs = pl.pallas_call(
        body, name="pool_backward", grid=(nt,),
        out_shape=[jax.ShapeDtypeStruct((s, D), F32), jax.ShapeDtypeStruct((s, 2 * D), BF16),
                   jax.ShapeDtypeStruct((D, D), F32),
                   jax.ShapeDtypeStruct((GROUPS, GROUP_DIM, GROUP_DIM), F32),
                   jax.ShapeDtypeStruct((8, D), F32)] + _scatter_shapes(chip_sums),
        in_specs=[row(D), row(D), row(D), row(D), _full((1, D)), _full((N_CHIPS, D, D // 2)),
                  _full((GROUPS, GROUP_DIM, GROUP_DIM)), _full((1, D)), _full((1, D)), _full((D, D))]
                 + _any_specs(n_sums),
        out_specs=[row(D), row(2 * D), _full((D, D)), _full((GROUPS, GROUP_DIM, GROUP_DIM)), _full((8, D))]
                  + _any_specs(n_sums),
        scratch_shapes=[pltpu.VMEM((ts + HALO, D), F32), pltpu.VMEM((ts + HALO, D), F32),
                        pltpu.VMEM((HALO, D), F32),
                        pltpu.SemaphoreType.DMA((3 * n_sums,)), pltpu.SemaphoreType.DMA((3 * n_sums,))],
        compiler_params=_params("arbitrary"),
    )(x, dh1, pooled, gt, w0, wpi, gw, gb, scale, wpo, *chip_sums)
    return outs[:5], outs[5:]


def gla_project(h1, w1, wgi_q, wgk, bgk, later):
    s = h1.shape[0]
    ts = ROW_TILE
    nt = s // ts
    assert nt >= 2
    n_later = len(later)

    def body(h_ref, w1_ref, wq_ref, wgk_ref, bgk_ref, *rest):
        rest = rest[n_later:]
        qk_ref, v_ref, gate_ref, low_ref, cum_ref, n1_ref = rest[:6]
        later_refs = rest[6:6 + n_later]
        send_sems, recv_sems, wgi_ref = rest[6 + n_later:]
        gather_in_background(pl.program_id(0), nt - 1, later_refs, send_sems, recv_sems, finish=False)

        @pl.when(pl.program_id(0) == 0)
        def _():
            _assemble_gla_in(wq_ref, wgi_ref)

        hv = h_ref[...]
        r = lax.rsqrt(jnp.mean(hv * hv, axis=-1, keepdims=True) + EPS)
        n1 = _bf(hv * r * w1_ref[...])
        n1_ref[...] = n1
        qk_ref[...] = _nn(n1, wgi_ref[:, 0:2 * KEY_W])
        v_ref[...] = _bf(_nn(n1, wgi_ref[:, 2 * KEY_W:2 * KEY_W + D]))
        gate_ref[...] = _nn(n1, wgi_ref[:, 2 * KEY_W + D:GLA_MAIN])
        low = _bf(_nn(n1, wgi_ref[:, GLA_MAIN:]))
        low_ref[...] = low
        z = _nn(low, wgk_ref[...]) + bgk_ref[...]
        lg = (jnp.minimum(z, 0.0) - jnp.log(1.0 + jnp.exp(-jnp.abs(z)))) / GATE_NORM
        lower_f = _chunk_masks()[0].astype(F32)
        for r0 in range(0, ts, CHUNK):
            cum_ref[r0:r0 + CHUNK, :] = _nn_exact(lower_f, lg[r0:r0 + CHUNK, :])
        gather_in_background(pl.program_id(0), nt - 1, later_refs, send_sems, recv_sems, finish=True)

    row = lambda cols: pl.BlockSpec((ts, cols), lambda i: (i, 0))
    outs = pl.pallas_call(
        body, name="gla_project", grid=(nt,),
        out_shape=[jax.ShapeDtypeStruct((s, D), F32), jax.ShapeDtypeStruct((s, D), BF16),
                   jax.ShapeDtypeStruct((s, D), F32), jax.ShapeDtypeStruct((s, RANK_PAD), BF16),
                   jax.ShapeDtypeStruct((s, KEY_W), F32), jax.ShapeDtypeStruct((s, D), BF16)]
                  + [jax.ShapeDtypeStruct(a.shape, a.dtype) for a in later],
        in_specs=[row(D), _full((1, D)), _full((N_CHIPS, D, GLA_IN_QUARTER)),
                  _full((RANK_PAD, KEY_W)), _full((1, KEY_W))] + _any_specs(n_later),
        out_specs=[row(D), row(D), row(D), row(RANK_PAD), row(KEY_W), row(D)] + _any_specs(n_later),
        input_output_aliases={5 + k: 6 + k for k in range(n_later)},
        scratch_shapes=[pltpu.SemaphoreType.DMA((6 * n_later,)), pltpu.SemaphoreType.DMA((6 * n_later,)),
                        pltpu.VMEM((D, GLA_MAIN + RANK_PAD), BF16)],
        compiler_params=_params("arbitrary"),
    )(h1, w1, wgi_q, wgk, bgk, *later)
    return outs[:6], outs[6:]


def _assemble_gla_in(wq_ref, wfull):
    pad = jnp.zeros((CAST_ROWS, GLA_MAIN + RANK_PAD - GLA_IN), BF16)
    for r0 in range(0, D, CAST_ROWS):
        rows = slice(r0, r0 + CAST_ROWS)
        wfull[rows, :] = jnp.concatenate([wq_ref[q, rows, :] for q in range(N_CHIPS)] + [pad], axis=1)


GLA_BLOCK = 512
CHUNKS_PER_BLOCK = GLA_BLOCK // CHUNK


def _chunk_masks():
    t = lax.broadcasted_iota(jnp.int32, (CHUNK, CHUNK), 0)
    u = lax.broadcasted_iota(jnp.int32, (CHUNK, CHUNK), 1)
    return t >= u, t <= u


def _gla_chunk_terms(q, cum):
    ep = jnp.exp(cum)
    en = jnp.exp(-cum)
    qs = q * (HEAD_K ** -0.5)
    last = cum[CHUNK - 1:CHUNK, :]
    ed = jnp.exp(last - cum)
    dec = jnp.exp(last)
    return ep, en, qs, ed, dec


def gla_forward(qk, v, cum):
    s = qk.shape[0]
    nb = s // GLA_BLOCK
    nc = s // CHUNK

    def body(q_ref, k_ref, v_ref, cum_ref, o_ref, st_ref, sc_ref, state):
        @pl.when(pl.program_id(0) == 0)
        def _():
            state[...] = jnp.zeros_like(state)

        lower, _ = _chunk_masks()

        def chunk(cc, carry):
            rows = pl.ds(pl.multiple_of(cc * CHUNK, CHUNK), CHUNK)
            for h in range(HEADS):
                kc = slice(h * HEAD_K, (h + 1) * HEAD_K)
                vc = slice(h * HEAD_V, (h + 1) * HEAD_V)
                q = q_ref[rows, kc]
                k = k_ref[rows, kc]
                v = v_ref[rows, vc]
                ep, en, qs, ed, dec = _gla_chunk_terms(q, cum_ref[rows, kc])
                a = _bf(qs * ep)
                fwd = _nt(a, _bf(k * en))
                bwd = _nt(_bf(qs * en), _bf(k * ep))
                scores = _bf(jnp.where(lower, fwd, bwd))
                sc_ref[rows, h * CHUNK:(h + 1) * CHUNK] = scores
                st = state[h]
                st_ref[cc, h] = st
                o_ref[rows, vc] = _nn(scores, v) + _nt(a, _bf(st))
                state[h] = st * dec + _tn(v, _bf(k * ed))
            return carry

        lax.fori_loop(0, CHUNKS_PER_BLOCK, chunk, 0, unroll=True)

    return pl.pallas_call(
        body, name="gla_forward", grid=(nb,),
        out_shape=(jax.ShapeDtypeStruct((s, D), F32),
                   jax.ShapeDtypeStruct((nc, HEADS, HEAD_V, HEAD_K), F32),
                   jax.ShapeDtypeStruct((s, HEADS * CHUNK), BF16)),
        in_specs=[pl.BlockSpec((GLA_BLOCK, KEY_W), lambda i: (i, 0)),
                  pl.BlockSpec((GLA_BLOCK, KEY_W), lambda i: (i, 1)),
                  pl.BlockSpec((GLA_BLOCK, D), lambda i: (i, 0)),
                  pl.BlockSpec((GLA_BLOCK, KEY_W), lambda i: (i, 0))],
        out_specs=(pl.BlockSpec((GLA_BLOCK, D), lambda i: (i, 0)),
                   pl.BlockSpec((CHUNKS_PER_BLOCK, HEADS, HEAD_V, HEAD_K), lambda i: (i, 0, 0, 0)),
                   pl.BlockSpec((GLA_BLOCK, HEADS * CHUNK), lambda i: (i, 0))),
        scratch_shapes=[pltpu.VMEM((HEADS, HEAD_V, HEAD_K), F32)],
        compiler_params=_params("arbitrary"),
    )(qk, qk, v, cum)


def gla_backward(qk, v, cum, do, states, scores):
    s = qk.shape[0]
    nb = s // GLA_BLOCK

    def body(q_ref, k_ref, v_ref, cum_ref, do_ref, st_ref, sc_ref, dq_ref, dk_ref, dv_ref, dcum_ref, dstate):
        @pl.when(pl.program_id(0) == 0)
        def _():
            dstate[...] = jnp.zeros_like(dstate)

        lower, _ = _chunk_masks()
        is_last = lax.broadcasted_iota(jnp.int32, (CHUNK, HEAD_K), 0) == CHUNK - 1

        def chunk(step, carry):
            cc = CHUNKS_PER_BLOCK - 1 - step
            rows = pl.ds(pl.multiple_of(cc * CHUNK, CHUNK), CHUNK)
            for h in range(HEADS):
                kc = slice(h * HEAD_K, (h + 1) * HEAD_K)
                vc = slice(h * HEAD_V, (h + 1) * HEAD_V)
                q = q_ref[rows, kc]
                k = k_ref[rows, kc]
                v = v_ref[rows, vc]
                do_c = do_ref[rows, vc]
                ep, en, qs, ed, dec = _gla_chunk_terms(q, cum_ref[rows, kc])
                a = _bf(qs * ep)
                b = _bf(k * en)
                c = _bf(qs * en)
                dk_dec = _bf(k * ep)
                kd = _bf(k * ed)
                scores = sc_ref[rows, h * CHUNK:(h + 1) * CHUNK]
                st = st_ref[cc, h]
                dst = dstate[h]
                dst_bf = _bf(dst)

                dscores = _nt(do_c, v)
                dfwd = _bf(jnp.where(lower, dscores, 0.0))
                dbwd = _bf(jnp.where(lower, 0.0, dscores))
                dv_ref[rows, vc] = _bf(_tn(scores, do_c) + _nt(kd, dst_bf))
                da = _nn(dfwd, b) + _nn(do_c, _bf(st))
                db = _tn(dfwd, a)
                dc = _nn(dbwd, dk_dec)
                ddk = _tn(dbwd, c)
                dkd = _nn(v, dst_bf)
                ddec = jnp.sum(dst * st, axis=0, keepdims=True)
                dstate[h] = dst * dec + _tn(do_c, a)

                m = dkd * k * ed
                dq_ref[rows, kc] = _bf((da * ep + dc * en) * (HEAD_K ** -0.5))
                dk_ref[rows, kc] = _bf(db * en + ddk * ep + dkd * ed)
                dcum = (da * qs + ddk * k) * ep - (db * k + dc * qs) * en - m
                dlast = jnp.sum(m, axis=0, keepdims=True) + ddec * dec
                dcum_ref[rows, kc] = dcum + jnp.where(is_last, dlast, 0.0)
            return carry

        lax.fori_loop(0, CHUNKS_PER_BLOCK, chunk, 0, unroll=True)

    rev = lambda cols, col_block: pl.BlockSpec((GLA_BLOCK, cols), lambda i: (nb - 1 - i, col_block))
    return pl.pallas_call(
        body, name="gla_backward", grid=(nb,),
        out_shape=(jax.ShapeDtypeStruct((s, KEY_W), BF16), jax.ShapeDtypeStruct((s, KEY_W), BF16),
                   jax.ShapeDtypeStruct((s, D), BF16), jax.ShapeDtypeStruct((s, KEY_W), F32)),
        in_specs=[rev(KEY_W, 0), rev(KEY_W, 1), rev(D, 0), rev(KEY_W, 0), rev(D, 0),
                  pl.BlockSpec((CHUNKS_PER_BLOCK, HEADS, HEAD_V, HEAD_K), lambda i: (nb - 1 - i, 0, 0, 0)),
                  rev(HEADS * CHUNK, 0)],
        out_specs=(rev(KEY_W, 0), rev(KEY_W, 0), rev(D, 0), rev(KEY_W, 0)),
        scratch_shapes=[pltpu.VMEM((HEADS, HEAD_V, HEAD_K), F32)],
        compiler_params=_params("arbitrary"),
    )(qk, qk, v, cum, do, states, scores)


def head_and_loss(o, gate, h1, target, hw, wgo, wf):
    s = o.shape[0]
    ts = ROW_TILE

    def body(o_ref, gate_ref, h1_ref, tgt_ref, hw_ref, wgo_ref, wf_ref,
             dh2_ref, do_ref, dgate_ref, ggo_ref, small_ref):
        @pl.when(pl.program_id(0) == 0)
        def _():
            ggo_ref[...] = jnp.zeros_like(ggo_ref)
            small_ref[...] = jnp.zeros_like(small_ref)

        gate = gate_ref[...]
        hw = hw_ref[...]
        sg = _sigmoid(gate)
        silu = gate * sg
        ohat, ro = [], []
        for h in range(HEADS):
            oh = o_ref[:, h * HEAD_V:(h + 1) * HEAD_V]
            rh = lax.rsqrt(jnp.mean(oh * oh, axis=-1, keepdims=True) + EPS)
            ro.append(rh)
            ohat.append(oh * rh)
        ohat = jnp.concatenate(ohat, axis=-1)
        on = ohat * hw
        y2 = _bf(on * silu)
        h2 = h1_ref[...] + _nn(y2, wgo_ref[...])
        rf = lax.rsqrt(jnp.mean(h2 * h2, axis=-1, keepdims=True) + EPS)
        h2hat = h2 * rf
        wf = wf_ref[...]
        diff = h2hat * wf - tgt_ref[...]
        small_ref[2:3, :] += jnp.zeros((1, D), F32) + 0.5 * jnp.sum(diff * diff) / D
        dout = diff / D
        small_ref[0:1, :] += jnp.sum(dout * h2hat, axis=0, keepdims=True)
        dxh = dout * wf
        dh2 = rf * (dxh - h2hat * jnp.mean(dxh * h2hat, axis=-1, keepdims=True))
        dh2_ref[...] = dh2
        dh2_bf = _bf(dh2)
        ggo_ref[...] += _tn(y2, dh2_bf)
        dy2 = _nt(dh2_bf, wgo_ref[...])
        don = dy2 * silu
        dgate_ref[...] = _bf(dy2 * on * (sg * (1.0 + gate * (1.0 - sg))))
        ghw = jnp.sum(don * ohat, axis=0, keepdims=True)
        small_ref[1:2, 0:HEAD_V] += sum(ghw[:, h * HEAD_V:(h + 1) * HEAD_V] for h in range(HEADS))
        dohat = don * hw
        for h in range(HEADS):
            cols = slice(h * HEAD_V, (h + 1) * HEAD_V)
            oh, dh = ohat[:, cols], dohat[:, cols]
            do_ref[:, cols] = _bf(ro[h] * (dh - oh * jnp.mean(dh * oh, axis=-1, keepdims=True)))

    row = lambda cols: pl.BlockSpec((ts, cols), lambda i: (i, 0))
    act = jax.ShapeDtypeStruct((s, D), F32)
    act_bf = jax.ShapeDtypeStruct((s, D), BF16)
    return pl.pallas_call(
        body, name="head_and_loss", grid=(s // ts,),
        out_shape=(act, act_bf, act_bf, jax.ShapeDtypeStruct((D, D), F32), jax.ShapeDtypeStruct((8, D), F32)),
        in_specs=[row(D), row(D), row(D), row(D),
                  _full((1, D)), _full((D, D)), _full((1, D))],
        out_specs=(row(D), row(D), row(D), _full((D, D)), _full((8, D))),
        compiler_params=_params("arbitrary"),
    )(o, gate, h1, target, hw, wgo, wf)


def gla_project_backward(dq, dk, dv, dgate, dcum, low, h1, dh2, w1, wgi_q, wgk, bgk):
    s = h1.shape[0]
    ts = ROW_TILE

    def body(dq_ref, dk_ref, dv_ref, dgate_ref, dcum_ref, low_ref, h1_ref, dh2_ref, w1_ref,
             wq_ref, wgk_ref, bgk_ref, dh1_ref, dproj_ref, ggk_ref, small_ref, wgi_ref):
        @pl.when(pl.program_id(0) == 0)
        def _():
            ggk_ref[...] = jnp.zeros_like(ggk_ref)
            small_ref[...] = jnp.zeros_like(small_ref)
            _assemble_gla_in(wq_ref, wgi_ref)

        low = low_ref[...]
        z = _nn(low, wgk_ref[...]) + bgk_ref[...]
        upper_f = _chunk_masks()[1].astype(F32)
        dlg = jnp.concatenate([_nn_exact(upper_f, dcum_ref[r0:r0 + CHUNK, :]) for r0 in range(0, ts, CHUNK)],
                              axis=0)
        dz = dlg * (1.0 / GATE_NORM) * _sigmoid(-z)
        dz_bf = _bf(dz)
        ggk_ref[...] += _tn(low, dz_bf)
        small_ref[1:2, 0:KEY_W] += jnp.sum(dz, axis=0, keepdims=True)
        dlow = _bf(_nt(dz_bf, wgk_ref[...]))
        dproj_ref[:, GLA_MAIN:] = dlow
        dn1 = _nt(dlow, wgi_ref[:, GLA_MAIN:])
        for ref, lo, hi in ((dq_ref, 0, KEY_W), (dk_ref, KEY_W, 2 * KEY_W),
                            (dv_ref, 2 * KEY_W, 2 * KEY_W + D), (dgate_ref, 2 * KEY_W + D, GLA_MAIN)):
            piece = ref[...]
            dproj_ref[:, lo:hi] = piece
            dn1 = dn1 + _nt(piece, wgi_ref[:, lo:hi])
        hv = h1_ref[...]
        r = lax.rsqrt(jnp.mean(hv * hv, axis=-1, keepdims=True) + EPS)
        hhat = hv * r
        small_ref[0:1, :] += jnp.sum(dn1 * hhat, axis=0, keepdims=True)
        dxh = dn1 * w1_ref[...]
        dh1_ref[...] = dh2_ref[...] + r * (dxh - hhat * jnp.mean(dxh * hhat, axis=-1, keepdims=True))

    row = lambda cols: pl.BlockSpec((ts, cols), lambda i: (i, 0))
    return pl.pallas_call(
        body, name="gla_project_backward", grid=(s // ts,),
        out_shape=(jax.ShapeDtypeStruct((s, D), F32), jax.ShapeDtypeStruct((s, GLA_MAIN + RANK_PAD), BF16),
                   jax.ShapeDtypeStruct((RANK_PAD, KEY_W), F32),
                   jax.ShapeDtypeStruct((8, D), F32)),
        in_specs=[row(KEY_W), row(KEY_W), row(D), row(D), row(KEY_W), row(RANK_PAD), row(D), row(D),
                  _full((1, D)), _full((N_CHIPS, D, GLA_IN_QUARTER)), _full((RANK_PAD, KEY_W)),
                  _full((1, KEY_W))],
        out_specs=(row(D), row(GLA_MAIN + RANK_PAD), _full((RANK_PAD, KEY_W)), _full((8, D))),
        scratch_shapes=[pltpu.VMEM((D, GLA_MAIN + RANK_PAD), BF16)],
        compiler_params=_params("arbitrary"),
    )(dq, dk, dv, dgate, dcum, low, h1, dh2, w1, wgi_q, wgk, bgk)


def local_gradients(xs, target, w0, w1, wf, wpi, gw, gb, scale, wpo, gla_quarters, wgk, bgk, hw_tiled, place):
    wgi_q, wgo_q = gla_quarters
    (h1, pooled, gt, n0), (wgi_q,) = pool_forward(xs, w0, wpi, gw, gb, scale, wpo, [wgi_q])
    (qk, v, gate, low, cum, n1), (wgo_q,) = gla_project(h1, w1, wgi_q, wgk, bgk, [wgo_q])
    wgo = wgo_q.reshape(D, D)
    o, states, scores = gla_forward(qk, v, cum)

    dh2, do, dgate, g_gla_out, small_top = head_and_loss(o, gate, h1, target, hw_tiled, wgo, wf)
    dq, dk, dv, dcum = gla_backward(qk, v, cum, do, states, scores)
    dh1, dproj, g_gk_pad, small_gla = gla_project_backward(
        dq, dk, dv, dgate, dcum, low, h1, dh2, w1, wgi_q, wgk, bgk)
    g_gla_in, _ = matmul_tn(n1, dproj, "grad_gla_in", (GLA_MAIN + RANK_PAD) // 5, place)

    gla_sums = add_halves([g_gla_in, g_gla_out.reshape(N_CHIPS, D // N_CHIPS, D)], place, "add_halves_gla")
    (dx, dpool, g_pool_out, g_group_w, small_pool), gla_got = pool_backward(
        xs, dh1, pooled, gt, w0, wpi, gw, gb, scale, wpo, [b for _, b in gla_sums])
    g_pool_in, mix = matmul_tn(n0, dpool, "grad_pool_in", D // 2, place, by_column_tile=True,
                               also_reduce=[g_group_w, g_pool_out.reshape(N_CHIPS, D // N_CHIPS, D)])

    reduced, total = join_halves(
        g_pool_in, place, [own for own, _ in mix] + [f for f, _ in gla_sums], [got for _, got in mix] + list(gla_got),
        small_pool, small_gla, small_top, g_gk_pad)
    return dx, reduced, total


def kernel(x, norm_w, pool_in_w, pool_group_w, pool_group_b, pool_scale, pool_out_w, gla_in_w, gla_gk_w, gla_gk_b, gla_head_norm_w, gla_out_w, final_norm_w, loss_target, m_norm_w, m_pool_in_w, m_pool_group_w, m_pool_group_b, m_pool_scale, m_pool_out_w, m_gla_in_w, m_gla_gk_w, m_gla_gk_b, m_gla_head_norm_w, m_gla_out_w, m_final_norm_w, v_norm_w, v_pool_in_w, v_pool_group_w, v_pool_group_b, v_pool_scale, v_pool_out_w, v_gla_in_w, v_gla_gk_w, v_gla_gk_b, v_gla_head_norm_w, v_gla_out_w, v_final_norm_w):
    xs = x[0]
    target = loss_target[0]
    q_chip = 2 * lax.axis_index("x") + lax.axis_index("y")
    place = jnp.stack([lax.axis_index("c"), q_chip]).astype(jnp.int32)

    (wpi, gw_q, wpo_q, wgi_q, wgo_q), (bgk, hw_tiled, gb, wgk) = allgather_weights(
        [pool_in_w[0], pool_group_w[0].reshape(GROUP_DIM, GROUP_DIM), pool_out_w[0], gla_in_w[0], gla_out_w[0]],
        exchange=(True, True, True, False, False),
        smalls=[gla_gk_b, gla_head_norm_w, pool_group_b[0], gla_gk_w[0]])
    wpo = wpo_q.reshape(D, D)

    w0 = norm_w[0:1]
    w1 = norm_w[1:2]
    wf = final_norm_w.reshape(1, D)

    dx, reduced, total = local_gradients(
        xs, target, w0, w1, wf, wpi, gw_q, gb, pool_scale, wpo, [wgi_q, wgo_q], wgk, bgk, hw_tiled, place)
    r_pool_in, r_group_w, r_pool_out, r_gla_in, r_gla_out = reduced
    r_group_w = r_group_w.reshape(GROUPS, 64, GROUP_DIM)

    turn = lambda a: jnp.transpose(a, (2, 0, 1))
    back = lambda a: jnp.transpose(a, (1, 2, 0))
    as2d = lambda a, w: a.reshape(-1, w.shape[-1])
    big_names = ("pool_in_w", "pool_group_w", "pool_out_w", "gla_in_w", "gla_out_w")
    big_args = [(pool_in_w, r_pool_in[None], m_pool_in_w, v_pool_in_w),
                (pool_group_w, r_group_w[None], m_pool_group_w, v_pool_group_w),
                (pool_out_w, r_pool_out[None], m_pool_out_w, v_pool_out_w),
                (gla_in_w, r_gla_in[None], m_gla_in_w, v_gla_in_w),
                (gla_out_w, r_gla_out[None], m_gla_out_w, v_gla_out_w)]
    to_kernel = lambda n, a, w: turn(a) if n == "gla_in_w" else as2d(a, w)
    from_kernel = lambda n, a, w: back(a) if n == "gla_in_w" else a.reshape(w.shape)
    big_in = [tuple(to_kernel(n, a, p[0]) for a in p) for n, p in zip(big_names, big_args)]
    big_out = adamw(big_in, "adamw")
    big = {n: (from_kernel(n, i[1], p[0]),) + tuple(from_kernel(n, o, p[0]) for o in out)
           for n, p, i, out in zip(big_names, big_args, big_in, big_out)}

    small_names = ("norm_w", "pool_group_b", "pool_scale", "gla_gk_w", "gla_gk_b", "gla_head_norm_w",
                   "final_norm_w")
    small_args = [(norm_w, m_norm_w, v_norm_w),
                  (pool_group_b, m_pool_group_b, v_pool_group_b),
                  (pool_scale, m_pool_scale, v_pool_scale),
                  (gla_gk_w, m_gla_gk_w, v_gla_gk_w),
                  (gla_gk_b, m_gla_gk_b, v_gla_gk_b),
                  (gla_head_norm_w, m_gla_head_norm_w, v_gla_head_norm_w),
                  (final_norm_w, m_final_norm_w, v_final_norm_w)]
    small_out, loss = adamw_small([tuple(as2d(a, p[0]) for a in p) for p in small_args], total, place)
    small = {n: tuple(o.reshape(p[0].shape) for o in out) for n, p, out in zip(small_names, small_args, small_out)}
    results = [
        small["norm_w"],
        big["pool_in_w"],
        big["pool_group_w"],
        small["pool_group_b"],
        small["pool_scale"],
        big["pool_out_w"],
        big["gla_in_w"],
        small["gla_gk_w"],
        small["gla_gk_b"],
        small["gla_head_norm_w"],
        big["gla_out_w"],
        small["final_norm_w"],
    ]
    grads, deltas, new_m, new_v = zip(*results)
    return (loss.reshape(()), dx[None], *grads, *deltas, *new_m, *new_v)
```

```python
import jax
import jax.numpy as jnp
from jax import lax
from jax.experimental import pallas as pl
from jax.experimental.pallas import tpu as pltpu

F32 = jnp.float32
BF16 = jnp.bfloat16
MESH = pl.DeviceIdType.MESH

D = 1024
POOL_WINDOWS = (2, 4, 8, 16)
GROUPS = 4
GROUP_DIM = 256
HEADS = 4
HEAD_K = 128
HEAD_V = 256
KEY_W = 512
CHUNK = 64
GATE_RANK = 16
GATE_NORM = 16.0
GLA_IN = 3088
GLA_MAIN = 3072
RANK_PAD = 128
EPS = 1e-6
HALO = 32

ADAM_LR = 0.001
ADAM_B1 = 0.9
ADAM_B2 = 0.999
ADAM_EPS = 1e-08
ADAM_WD = 0.01
ADAM_STEP = 10

N_CHIPS = 4
N_DEV = 8
GLA_IN_QUARTER = GLA_IN // N_CHIPS

VMEM_LIMIT = 56 * 1024 * 1024


def _nn(a, b):
    return lax.dot_general(a, b, (((1,), (0,)), ((), ())), preferred_element_type=F32)


def _nt(a, b):
    return lax.dot_general(a, b, (((1,), (1,)), ((), ())), preferred_element_type=F32)


def _tn(a, b):
    return lax.dot_general(a, b, (((0,), (0,)), ((), ())), preferred_element_type=F32)


def _nn_exact(a, b):
    return lax.dot_general(a, b, (((1,), (0,)), ((), ())), preferred_element_type=F32,
                           precision=lax.Precision.HIGHEST)


def _bf(a):
    return a.astype(BF16)


def _params(*sem):
    return pltpu.CompilerParams(dimension_semantics=sem, vmem_limit_bytes=VMEM_LIMIT)


def _full(shape):
    return pl.BlockSpec(shape, lambda i: (0,) * len(shape))


def _position():
    return lax.axis_index("x"), lax.axis_index("y"), lax.axis_index("c")


def _gather_small(in_ref, all_ref, send_sems, recv_sems, local_sem):
    x, y, c = _position()
    me = 4 * x + 2 * y + c
    mine = pltpu.make_async_copy(in_ref, all_ref.at[me], local_sem)
    sends = []
    for k in range(N_DEV - 1):
        fx, fy, fc = (k + 1) >> 2 & 1, (k + 1) >> 1 & 1, (k + 1) & 1
        sends.append(pltpu.make_async_remote_copy(
            src_ref=in_ref, dst_ref=all_ref.at[me],
            send_sem=send_sems.at[k], recv_sem=recv_sems.at[k],
            device_id=(x ^ fx, y ^ fy, c ^ fc), device_id_type=MESH))

    def start():
        mine.start()
        for cp in sends:
            cp.start()

    def wait():
        for k in range(N_DEV - 1):
            fx, fy, fc = (k + 1) >> 2 & 1, (k + 1) >> 1 & 1, (k + 1) & 1
            src_dev = 4 * (x ^ fx) + 2 * (y ^ fy) + (c ^ fc)
            pltpu.make_async_remote_copy(
                src_ref=in_ref, dst_ref=all_ref.at[src_dev],
                send_sem=send_sems.at[k], recv_sem=recv_sems.at[k],
                device_id=(x, y, c), device_id_type=MESH).wait_recv()
        for cp in sends:
            cp.wait_send()
        mine.wait()

    return start, wait


SMALL_SEMS = [pltpu.SemaphoreType.DMA((N_DEV - 1,)), pltpu.SemaphoreType.DMA((N_DEV - 1,)),
              pltpu.SemaphoreType.DMA]
VMEM_SPEC = pl.BlockSpec(memory_space=pltpu.VMEM)


def _other_chips(x, y):
    return [(1 - x, y), (x, 1 - y), (1 - x, 1 - y)]


def _any_specs(n):
    return [pl.BlockSpec(memory_space=pl.ANY)] * n


def _halves(rows, c):
    half = rows // 2
    return pl.ds(c * half, half), pl.ds((1 - c) * half, half)


CAST_ROWS = 256


def _gather_copy(out_ref, send_sems, recv_sems, k, quarter, half, to, src=None):
    dst = out_ref.at[quarter, half]
    return pltpu.make_async_remote_copy(
        src_ref=dst if src is None else src, dst_ref=dst,
        send_sem=send_sems.at[k], recv_sem=recv_sems.at[k], device_id=to, device_id_type=MESH)


SMALL_IN_ROWS = 24


def allgather_weights(quarters, exchange, smalls):
    n = len(quarters)
    shapes = [w.shape for w in quarters]
    moved = [i for i in range(n) if exchange[i]]

    def body(*refs):
        w_refs, (gkb_ref, hnw_ref, gb_ref, gkw_ref) = refs[:n], refs[n:n + 4]
        out_refs, (bgk_ref, hw_ref, gbias_ref, wgk_ref) = refs[n + 4:2 * n + 4], refs[2 * n + 4:2 * n + 8]
        refs = refs[2 * n + 8:]
        f32_bufs, bf_bufs = refs[:n], refs[n:2 * n]
        send_sems, recv_sems, local_sems, small_ref, small_all_ref = refs[2 * n:2 * n + 5]
        small_ref[...] = jnp.zeros_like(small_ref)
        small_ref[0:1, :] = gkb_ref[...]
        small_ref[1:2, 0:64] = hnw_ref[...]
        small_ref[2:2 + GROUPS, 0:64] = gb_ref[...]
        small_ref[8:8 + GATE_RANK, :] = gkw_ref[...]
        start_small, wait_small = _gather_small(small_ref, small_all_ref, *refs[2 * n + 5:])
        start_small()
        x, y, c = _position()
        q = 2 * x + y
        sibling = (x, y, 1 - c)
        chips = _other_chips(x, y)

        def copy(k, i, quarter, half, to, src=None):
            return _gather_copy(out_refs[i], send_sems, recv_sems, k * n + i, quarter, half, to, src)

        loads = [pltpu.make_async_copy(w_refs[i], f32_bufs[i], local_sems.at[i]) for i in range(n)]
        for cp in loads:
            cp.start()
        keeps, sends = [], []
        for i in range(n):
            loads[i].wait()
            for r0 in range(0, shapes[i][0], CAST_ROWS):
                bf_bufs[i][r0:r0 + CAST_ROWS, :] = _bf(f32_bufs[i][r0:r0 + CAST_ROWS, :])
            keep = pltpu.make_async_copy(bf_bufs[i], out_refs[i].at[q], local_sems.at[n + i])
            keep.start()
            keeps.append(keep)
            if not exchange[i]:
                continue
            mine, _ = _halves(shapes[i][0], c)
            for j, chip in enumerate(chips):
                cp = copy(j, i, q, mine, (*chip, c), src=bf_bufs[i].at[mine])
                cp.start()
                sends.append(cp)
        for j, chip in enumerate(chips):
            qj = 2 * chip[0] + chip[1]
            for i in moved:
                mine, _ = _halves(shapes[i][0], c)
                copy(j, i, qj, mine, (x, y, c)).wait_recv()
                cp = copy(3 + j, i, qj, mine, sibling)
                cp.start()
                sends.append(cp)
        for j, chip in enumerate(chips):
            qj = 2 * chip[0] + chip[1]
            for i in moved:
                _, other = _halves(shapes[i][0], c)
                copy(3 + j, i, qj, other, (x, y, c)).wait_recv()
        wait_small()
        wgk_ref[...] = jnp.zeros_like(wgk_ref)
        for j in range(N_CHIPS):
            block = small_all_ref.at[2 * j]
            bgk_ref[:, 128 * j:128 * (j + 1)] = block[0:1, :]
            for h in range(HEADS):
                hw_ref[:, HEAD_V * h + 64 * j:HEAD_V * h + 64 * (j + 1)] = block[1:2, 0:64]
            for g in range(GROUPS):
                gbias_ref[:, GROUP_DIM * g + 64 * j:GROUP_DIM * g + 64 * (j + 1)] = block[2 + g:3 + g, 0:64]
            wgk_ref[0:GATE_RANK, 128 * j:128 * (j + 1)] = _bf(block[8:8 + GATE_RANK, :])
        for cp in sends:
            cp.wait_send()
        for cp in keeps:
            cp.wait()

    outs = pl.pallas_call(
        body, name="allgather_weights",
        out_shape=[jax.ShapeDtypeStruct((N_CHIPS, *s), BF16) for s in shapes]
                  + [jax.ShapeDtypeStruct((1, KEY_W), F32), jax.ShapeDtypeStruct((1, D), F32),
                     jax.ShapeDtypeStruct((1, D), F32), jax.ShapeDtypeStruct((RANK_PAD, KEY_W), BF16)],
        in_specs=_any_specs(n) + [VMEM_SPEC] * 4, out_specs=_any_specs(n) + [VMEM_SPEC] * 4,
        scratch_shapes=([pltpu.VMEM(s, F32) for s in shapes] + [pltpu.VMEM(s, BF16) for s in shapes]
                        + [pltpu.SemaphoreType.DMA((6 * n,)), pltpu.SemaphoreType.DMA((6 * n,)),
                           pltpu.SemaphoreType.DMA((2 * n,)), pltpu.VMEM((SMALL_IN_ROWS, 128), F32),
                           pltpu.VMEM((N_DEV, SMALL_IN_ROWS, 128), F32)] + SMALL_SEMS),
        compiler_params=pltpu.CompilerParams(vmem_limit_bytes=VMEM_LIMIT),
    )(*quarters, *smalls)
    return outs[:n], outs[n:]


def _scatter_copies(b_refs, got_refs, send_sems, recv_sems):
    n = len(b_refs)
    x, y, c = _position()
    copies = []
    for j, chip in enumerate(_other_chips(x, y)):
        qj = 2 * chip[0] + chip[1]
        for i in range(n):
            copies.append(pltpu.make_async_remote_copy(
                src_ref=b_refs[i].at[qj], dst_ref=got_refs[i].at[j],
                send_sem=send_sems.at[j * n + i], recv_sem=recv_sems.at[j * n + i],
                device_id=(*chip, c), device_id_type=MESH))
    return copies


def _scatter_shapes(chip_sums):
    return [jax.ShapeDtypeStruct((N_CHIPS - 1, *b.shape[1:]), BF16) for b in chip_sums]


ADD_ROWS = 512
ADD_HALVES_ROWS = 128


def _spans(counts):
    starts, total = [], 0
    for count in counts:
        starts.append(total)
        total += count
    return starts, total


def _local_step(t, start, count):
    return jnp.clip(t - start, 0, count - 1)


def add_halves(grads, place, name):
    n = len(grads)
    whole = [len(g.shape) == 2 for g in grads]
    halves = [g.shape[-2] // 2 for g in grads]
    cols = [GLA_IN_QUARTER if w else g.shape[-1] for g, w in zip(grads, whole)]
    rbs = [min(ADD_HALVES_ROWS, h) for h in halves]
    counts = [h // rb for h, rb in zip(halves, rbs)]
    starts, total = _spans(counts)
    half_shapes = [(*g.shape[:-2], h, g.shape[-1]) for g, h in zip(grads, halves)]

    def rows_of(ref, i, start):
        return ref.at[pl.ds(start, rbs[i])] if whole[i] else ref.at[:, pl.ds(start, rbs[i])]

    def body(place_ref, *refs):
        a_refs, o_refs = refs[:n], refs[n:2 * n]
        f_refs, h_refs = refs[2 * n:3 * n], refs[3 * n:4 * n]
        send_refs, their_refs, (send_sems, recv_sems) = refs[4 * n:5 * n], refs[5 * n:6 * n], refs[6 * n:]
        t = pl.program_id(0)
        q = place_ref[1]
        x, y, c = _position()
        copies = [[pltpu.make_async_remote_copy(
            src_ref=rows_of(send_refs[i], i, k * rbs[i]), dst_ref=rows_of(their_refs[i], i, k * rbs[i]),
            send_sem=send_sems.at[starts[i] + k], recv_sem=recv_sems.at[starts[i] + k],
            device_id=(x, y, 1 - c), device_id_type=MESH) for k in range(counts[i])] for i in range(n)]

        for i in range(n):
            for k in range(counts[i]):
                @pl.when(t == starts[i] + k)
                def _(i=i, k=k):
                    rows_of(send_refs[i], i, k * rbs[i])[...] = _bf(o_refs[i][...])
                    copies[i][k].start()

        for i in range(n):
            for k in range(counts[i]):
                @pl.when(t == starts[i] + k + 1)
                def _(i=i, k=k):
                    copies[i][k].wait_recv()
                    b_ref = rows_of(their_refs[i], i, k * rbs[i])
                    if not whole[i]:
                        h_refs[i][...] = _bf(a_refs[i][...] + b_ref[...].astype(F32))
                        f_refs[i][...] = a_refs[i][q] + b_ref[q].astype(F32)
                        return
                    total_i = a_refs[i][...] + b_ref[...].astype(F32)
                    for k4 in range(N_CHIPS):
                        piece = total_i[:, k4 * cols[i]:(k4 + 1) * cols[i]]
                        h_refs[i][k4] = _bf(piece)

                        @pl.when(q == k4)
                        def _():
                            f_refs[i][...] = piece

        @pl.when(t == total)
        def _():
            for of_matrix in copies:
                for cp in of_matrix:
                    cp.wait_send()

    def specs(i):
        sent = lambda t: _local_step(t, starts[i], counts[i])
        added = lambda t: _local_step(t - 1, starts[i], counts[i])
        by_quarter = (N_CHIPS, rbs[i], cols[i])
        block = (rbs[i], grads[i].shape[-1]) if whole[i] else by_quarter
        lead = () if whole[i] else (0,)
        mine = pl.BlockSpec(block, lambda t, place: (*lead, place[0] * counts[i] + added(t), 0))
        other = pl.BlockSpec(block, lambda t, place: (*lead, (1 - place[0]) * counts[i] + sent(t), 0))
        sums = pl.BlockSpec(by_quarter, lambda t, place: (0, added(t), 0))
        own = pl.BlockSpec(by_quarter[1:], lambda t, place: (added(t), 0))
        return mine, other, own, sums

    all_specs = [specs(i) for i in range(n)]
    outs = pl.pallas_call(
        body, name=name,
        grid_spec=pltpu.PrefetchScalarGridSpec(
            num_scalar_prefetch=1, grid=(total + 1,),
            in_specs=[sp[0] for sp in all_specs] + [sp[1] for sp in all_specs],
            out_specs=[sp[2] for sp in all_specs] + [sp[3] for sp in all_specs],
            scratch_shapes=[pltpu.VMEM(sh, BF16) for sh in half_shapes] + [pltpu.VMEM(sh, BF16) for sh in half_shapes]
                           + [pltpu.SemaphoreType.DMA((total,)), pltpu.SemaphoreType.DMA((total,))]),
        out_shape=[jax.ShapeDtypeStruct((h, cl), F32) for h, cl in zip(halves, cols)]
                  + [jax.ShapeDtypeStruct((N_CHIPS, h, cl), BF16) for h, cl in zip(halves, cols)],
        compiler_params=_params("arbitrary"),
    )(place, *grads, *grads)
    return list(zip(outs[:n], outs[n:]))


SMALL_SUM_ROWS = 16


def join_halves(grad, place, owns, gots, small_pool, small_gla, small_top, g_gk_pad):
    n = len(owns)
    half, cols = grad.shape[1] // 2, grad.shape[2]
    rb = min(ADD_HALVES_ROWS, half)
    sent = half // rb
    shapes = [g.shape for g in gots] + [(N_CHIPS - 1, half, cols)]
    rbs = [min(ADD_ROWS, sh[1]) for sh in shapes]
    counts = [sh[1] // r for sh, r in zip(shapes, rbs)]
    starts, joined = _spans(counts)
    first_join = sent + 1
    steps = first_join + joined

    def body(place_ref, *refs):
        refs = iter(refs)
        take = lambda count: [next(refs) for _ in range(count)]
        (a_ref, o_ref), o_refs, g_refs = take(2), take(n), take(n)
        pool_ref, gla_ref, top_ref, gk_ref = take(4)
        out_refs, (total_ref,) = take(n + 1), take(1)
        send_buf, their_buf, sums_buf, got_buf = take(4)
        sum_refs = take(n + 1)
        to_core_sems, from_core_sems, to_chip_sems, from_chip_sems, local_sems, send_sems, recv_sems = take(7)
        all_ref, small_ref = take(2)
        t = pl.program_id(0)
        q = place_ref[1]
        x, y, c = _position()
        start_small, wait_small = _gather_small(small_ref, all_ref, *refs)
        block = lambda k: pl.ds(k * rb, rb)

        def to_core(k):
            return pltpu.make_async_remote_copy(
                src_ref=send_buf.at[:, block(k)], dst_ref=their_buf.at[:, block(k)],
                send_sem=to_core_sems.at[k], recv_sem=from_core_sems.at[k],
                device_id=(x, y, 1 - c), device_id_type=MESH)

        def to_owners(k):
            return [pltpu.make_async_remote_copy(
                src_ref=sums_buf.at[2 * chip[0] + chip[1], block(k)], dst_ref=got_buf.at[j, block(k)],
                send_sem=to_chip_sems.at[3 * k + j], recv_sem=from_chip_sems.at[3 * k + j],
                device_id=(*chip, c), device_id_type=MESH) for j, chip in enumerate(_other_chips(x, y))]

        def copies(i, k):
            src = sum_refs[i].at[pl.ds(k * rbs[i], rbs[i])]
            rows = pl.ds(c * shapes[i][1] + k * rbs[i], rbs[i])
            return (pltpu.make_async_copy(src, out_refs[i].at[rows], local_sems.at[starts[i] + k]),
                    pltpu.make_async_remote_copy(
                        src_ref=src, dst_ref=out_refs[i].at[rows],
                        send_sem=send_sems.at[starts[i] + k], recv_sem=recv_sems.at[starts[i] + k],
                        device_id=(x, y, 1 - c), device_id_type=MESH))

        @pl.when(t == 0)
        def _():
            small_ref[0:3, :] = pool_ref[0:3, :]
            small_ref[3:5, :] = gla_ref[0:2, :]
            small_ref[5:8, :] = top_ref[0:3, :]
            for r in range(GATE_RANK):
                small_ref[8 + r // 2:9 + r // 2, (r % 2) * KEY_W:(r % 2 + 1) * KEY_W] = gk_ref[r:r + 1, :]
            start_small()

        for k in range(sent):
            @pl.when(t == k)
            def _(k=k):
                send_buf[:, k * rb:(k + 1) * rb, :] = _bf(o_ref[...])
                to_core(k).start()

        for k in range(sent):
            @pl.when(t == k + 1)
            def _(k=k):
                to_core(k).wait_recv()
                theirs = their_buf.at[:, block(k)]
                sums_buf[:, k * rb:(k + 1) * rb, :] = _bf(a_ref[...] + theirs[...].astype(F32))
                sum_refs[n][k * rb:(k + 1) * rb, :] = a_ref[q] + theirs[q].astype(F32)
                for cp in to_owners(k):
                    cp.start()

        for i in range(n + 1):
            for k in range(counts[i]):
                @pl.when(t == first_join + starts[i] + k)
                def _(i=i, k=k):
                    rows = slice(k * rbs[i], (k + 1) * rbs[i])
                    if i < n:
                        total_i = o_refs[i][...]
                        arrived = [g_refs[i][j] for j in range(N_CHIPS - 1)]
                    else:
                        if k == 0:
                            for kk in range(sent):
                                for cp in to_owners(kk):
                                    cp.wait_recv()
                        total_i = sum_refs[n][rows, :]
                        arrived = [got_buf[j, rows, :] for j in range(N_CHIPS - 1)]
                    for part in arrived:
                        total_i = total_i + part.astype(F32)
                    sum_refs[i][rows, :] = total_i
                    for cp in copies(i, k):
                        cp.start()

        @pl.when(t == steps - 1)
        def _():
            wait_small()
            small_total = all_ref[0]
            for dev in range(1, N_DEV):
                small_total = small_total + all_ref[dev]
            total_ref[...] = small_total
            for k in range(sent):
                to_core(k).wait_send()
                for cp in to_owners(k):
                    cp.wait_send()
            for i in range(n + 1):
                for k in range(counts[i]):
                    for cp in copies(i, k):
                        cp.wait()

    def specs(i):
        step = lambda t: _local_step(t - first_join, starts[i], counts[i])
        return (pl.BlockSpec((rbs[i], shapes[i][2]), lambda t, place: (step(t), 0)),
                pl.BlockSpec((N_CHIPS - 1, rbs[i], shapes[i][2]), lambda t, place: (0, step(t), 0)))

    by_quarter = (N_CHIPS, rb, cols)
    mine = pl.BlockSpec(by_quarter, lambda t, place: (0, place[0] * sent + jnp.clip(t - 1, 0, sent - 1), 0))
    other = pl.BlockSpec(by_quarter, lambda t, place: (0, (1 - place[0]) * sent + jnp.clip(t, 0, sent - 1), 0))
    all_specs = [specs(i) for i in range(n)]
    sems = lambda count: pltpu.SemaphoreType.DMA((count,))
    outs = pl.pallas_call(
        body, name="join_halves",
        grid_spec=pltpu.PrefetchScalarGridSpec(
            num_scalar_prefetch=1, grid=(steps,),
            in_specs=[mine, other] + [sp[0] for sp in all_specs] + [sp[1] for sp in all_specs] + [VMEM_SPEC] * 4,
            out_specs=_any_specs(n + 1) + [VMEM_SPEC],
            scratch_shapes=[pltpu.VMEM((N_CHIPS, half, cols), BF16) for _ in range(3)]
                           + [pltpu.VMEM(shapes[n], BF16)] + [pltpu.VMEM(sh[1:], F32) for sh in shapes]
                           + [sems(sent), sems(sent), sems(3 * sent), sems(3 * sent),
                              sems(joined), sems(joined), sems(joined),
                              pltpu.VMEM((N_DEV, SMALL_SUM_ROWS, D), F32), pltpu.VMEM((SMALL_SUM_ROWS, D), F32)]
                           + SMALL_SEMS),
        out_shape=[jax.ShapeDtypeStruct((2 * sh[1], sh[2]), F32) for sh in shapes]
                  + [jax.ShapeDtypeStruct((SMALL_SUM_ROWS, D), F32)],
        compiler_params=_params("arbitrary"),
    )(place, grad, grad, *owns, *gots, small_pool, small_gla, small_top, g_gk_pad)
    return [outs[n]] + list(outs[:n]), outs[n + 1]


def _adam_math(w, g, m, v):
    m = ADAM_B1 * m + (1.0 - ADAM_B1) * g
    v = ADAM_B2 * v + (1.0 - ADAM_B2) * (g * g)
    m_hat = m / (1.0 - ADAM_B1 ** ADAM_STEP)
    v_hat = v / (1.0 - ADAM_B2 ** ADAM_STEP)
    delta = -ADAM_LR * (m_hat / (jnp.sqrt(v_hat) + ADAM_EPS) + ADAM_WD * w)
    return delta, m, v


ADAM_BLOCK_BYTES = 2 ** 19
ADAM_MOST_STEPS = 8


def adamw(params, name, echo):
    n = len(params)
    shapes = [p[0].shape for p in params]
    first_out, _ = _spans([4 if e else 3 for e in echo])

    def tile_rows(shape):
        rows, cols = shape[0], shape[-1]
        aligned = 1 if len(shape) == 3 else 8
        divisors = [t for t in range(aligned, rows + 1, aligned) if rows % t == 0]
        tile = max(t for t in divisors if t * cols * 4 <= ADAM_BLOCK_BYTES)
        if rows // tile > ADAM_MOST_STEPS:
            tile = min(t for t in divisors if rows // t <= ADAM_MOST_STEPS)
        return tile

    tiles = [tile_rows(sh) for sh in shapes]
    counts = [sh[0] // tl for sh, tl in zip(shapes, tiles)]
    starts, total = _spans(counts)

    def body(*refs):
        ins, outs = refs[:4 * n], refs[4 * n:]
        t = pl.program_id(0)
        for i in range(n):
            @pl.when((t >= starts[i]) & (t < starts[i] + counts[i]))
            def _(i=i):
                w_ref, g_ref, m_ref, v_ref = ins[4 * i:4 * i + 4]
                g = g_ref[...]
                d, nm, nv = _adam_math(w_ref[...], g, m_ref[...], v_ref[...])
                outs[first_out[i]][...] = d
                outs[first_out[i] + 1][...] = nm
                outs[first_out[i] + 2][...] = nv
                if echo[i]:
                    outs[first_out[i] + 3][...] = g

    def spec(i):
        block = (tiles[i],) + shapes[i][1:]
        zeros = (0,) * (len(block) - 1)
        return pl.BlockSpec(block, lambda t: (_local_step(t, starts[i], counts[i]),) + zeros)

    outs = pl.pallas_call(
        body, name=name, grid=(total,),
        out_shape=[jax.ShapeDtypeStruct(sh, F32) for sh, e in zip(shapes, echo) for _ in range(4 if e else 3)],
        in_specs=[spec(i) for i in range(n) for _ in range(4)],
        out_specs=[spec(i) for i in range(n) for _ in range(4 if echo[i] else 3)],
        compiler_params=_params("arbitrary"),
    )(*[a for p in params for a in p])
    return [tuple(outs[first_out[i]:first_out[i] + (4 if echo[i] else 3)]) for i in range(n)]


def adamw_small(params, total, place):
    n = len(params)

    def cut_gradients(total_ref, q, g_refs):
        g_norm, g_group_b, g_scale, g_gk_w, g_gk_b, g_head_norm, g_final = g_refs
        g_norm[0:1, :] = total_ref[0:1, :]
        g_norm[1:2, :] = total_ref[3:4, :]
        g_scale[...] = total_ref[1:2, :]
        g_final[...] = total_ref[5:6, :]
        g_gk_b[...] = total_ref[4:5, pl.ds(pl.multiple_of(q * 128, 128), 128)]
        for r in range(GATE_RANK):
            lanes = pl.ds(pl.multiple_of((r % 2) * KEY_W + q * 128, 128), 128)
            g_gk_w[r:r + 1, :] = total_ref[8 + r // 2:9 + r // 2, lanes]
        for k in range(N_CHIPS):
            @pl.when(q == k)
            def _(k=k):
                g_head_norm[...] = total_ref[6:7, 64 * k:64 * (k + 1)]
                for g in range(GROUPS):
                    g_group_b[g:g + 1, :] = total_ref[2:3, GROUP_DIM * g + 64 * k:GROUP_DIM * g + 64 * (k + 1)]

    def body(place_ref, total_ref, *refs):
        ins, outs = refs[:3 * n], refs[3 * n:]
        outs[4 * n][...] = total_ref[7:8, 0:1]
        cut_gradients(total_ref, place_ref[1], outs[0:4 * n:4])
        for k in range(n):
            w_ref, m_ref, v_ref = ins[3 * k:3 * k + 3]
            d, nm, nv = _adam_math(w_ref[...], outs[4 * k][...], m_ref[...], v_ref[...])
            outs[4 * k + 1][...] = d
            outs[4 * k + 2][...] = nm
            outs[4 * k + 3][...] = nv

    flat = [a for p in params for a in p]
    outs = pl.pallas_call(
        body, name="adamw_small",
        out_shape=[jax.ShapeDtypeStruct(p[0].shape, F32) for p in params for _ in range(4)]
                  + [jax.ShapeDtypeStruct((1, 1), F32)],
        in_specs=[pl.BlockSpec(memory_space=pltpu.SMEM)] + [VMEM_SPEC] * (1 + 3 * n),
        out_specs=[VMEM_SPEC] * (4 * n + 1),
    )(place, total, *flat)
    return [tuple(outs[4 * k:4 * k + 4]) for k in range(n)], outs[4 * n]


def matmul_tn(a, b, name, tile_n, place, by_column_tile=False, also_reduce=()):
    s, m = a.shape
    n = b.shape[1]
    steps = n // tile_n
    n_red = len(also_reduce)
    assert n_red == 0 or steps >= 2
    halves = [(g.shape[1] // 2, g.shape[2]) for g in also_reduce]
    if by_column_tile:
        out_shape = jax.ShapeDtypeStruct((steps, m, tile_n), F32)
        out_spec = pl.BlockSpec((None, m, tile_n), lambda j, place: (j, 0, 0))
    else:
        out_shape = jax.ShapeDtypeStruct((m, n), F32)
        out_spec = pl.BlockSpec((m, tile_n), lambda j, place: (0, j))

    def body(place_ref, a_ref, b_ref, *rest):
        rest = iter(rest)
        take = lambda count: [next(rest) for _ in range(count)]
        mine_refs, other_refs, (out_ref,) = take(n_red), take(n_red), take(1)
        own_refs, got_refs = take(n_red), take(n_red)
        send_bufs, their_bufs, sums_bufs = take(n_red), take(n_red), take(n_red)
        sems = list(rest)
        j = pl.program_id(0)
        q = place_ref[1]
        x, y, c = _position()

        def to_core(i):
            return pltpu.make_async_remote_copy(
                src_ref=send_bufs[i], dst_ref=their_bufs[i], send_sem=sems[0].at[i], recv_sem=sems[1].at[i],
                device_id=(x, y, 1 - c), device_id_type=MESH)

        def to_owners(i):
            return [pltpu.make_async_remote_copy(
                src_ref=sums_bufs[i].at[2 * chip[0] + chip[1]], dst_ref=got_refs[i].at[k],
                send_sem=sems[2].at[3 * i + k], recv_sem=sems[3].at[3 * i + k],
                device_id=(*chip, c), device_id_type=MESH) for k, chip in enumerate(_other_chips(x, y))]

        if n_red:
            @pl.when(j == 0)
            def _():
                for i in range(n_red):
                    send_bufs[i][...] = _bf(other_refs[i][...])
                    to_core(i).start()

            @pl.when(j == 1)
            def _():
                for i in range(n_red):
                    to_core(i).wait_recv()
                    sums_bufs[i][...] = _bf(mine_refs[i][...] + their_bufs[i][...].astype(F32))
                    own_refs[i][...] = mine_refs[i][q] + their_bufs[i][q].astype(F32)
                    for cp in to_owners(i):
                        cp.start()

        out_ref[...] = _tn(a_ref[...], b_ref[...])

        if n_red:
            @pl.when(j == steps - 1)
            def _():
                for i in range(n_red):
                    to_core(i).wait_send()
                    for cp in to_owners(i):
                        cp.wait()

    by_quarter = [(N_CHIPS, *h) for h in halves]
    outs = pl.pallas_call(
        body, name=name,
        grid_spec=pltpu.PrefetchScalarGridSpec(
            num_scalar_prefetch=1, grid=(steps,),
            in_specs=[pl.BlockSpec((s, m), lambda j, place: (0, 0)), pl.BlockSpec((s, tile_n), lambda j, place: (0, j))]
                     + [pl.BlockSpec(sh, lambda j, place: (0, place[0], 0)) for sh in by_quarter]
                     + [pl.BlockSpec(sh, lambda j, place: (0, 1 - place[0], 0)) for sh in by_quarter],
            out_specs=[out_spec] + [pl.BlockSpec(h, lambda j, place: (0, 0)) for h in halves] + _any_specs(n_red),
            scratch_shapes=[pltpu.VMEM(sh, BF16) for sh in by_quarter * 3]
                           + ([pltpu.SemaphoreType.DMA((n_red,)), pltpu.SemaphoreType.DMA((n_red,)),
                               pltpu.SemaphoreType.DMA((3 * n_red,)), pltpu.SemaphoreType.DMA((3 * n_red,))]
                              if n_red else [])),
        out_shape=[out_shape] + [jax.ShapeDtypeStruct(h, F32) for h in halves]
                  + [jax.ShapeDtypeStruct((N_CHIPS - 1, *h), BF16) for h in halves],
        compiler_params=_params("arbitrary"),
    )(place, a, b, *also_reduce, *also_reduce)
    return outs[0], list(zip(outs[1:1 + n_red], outs[1 + n_red:]))


ROW_TILE = 512


def _row_index(tile, rows):
    return tile * rows + lax.broadcasted_iota(jnp.int32, (rows, 1), 0)


def _inverse_counts(t_glob):
    return [1.0 / jnp.minimum(t_glob + 1, w).astype(F32) for w in POOL_WINDOWS]


def _sigmoid(z):
    return 1.0 / (1.0 + jnp.exp(-z))


def _trailing_sums(src, tmp, cols, window, rows):
    bufs = (src, tmp)
    span, level, start = 1, 0, 0
    while span < window:
        start += 8
        a, b = bufs[level % 2], bufs[(level + 1) % 2]
        n = HALO + rows - start
        b[start:start + n, cols] = a[start:start + n, cols] + a[start - span:start - span + n, cols]
        span, level = 2 * span, level + 1
    return bufs[level % 2][HALO:HALO + rows, cols]


def _leading_sums(src, tmp, cols, window, rows):
    bufs = (src, tmp)
    span, level, n = 1, 0, rows + HALO
    while span < window:
        n -= 8
        a, b = bufs[level % 2], bufs[(level + 1) % 2]
        b[0:n, cols] = a[0:n, cols] + a[span:span + n, cols]
        span, level = 2 * span, level + 1
    return bufs[level % 2][0:rows, cols]


def gather_in_background(step, last, out_refs, send_sems, recv_sems, finish):
    n = len(out_refs)
    x, y, c = _position()
    q = 2 * x + y
    chips = _other_chips(x, y)

    def copy(k, i, quarter, half, to):
        return _gather_copy(out_refs[i], send_sems, recv_sems, k * n + i, quarter, half, to)

    if not finish:
        @pl.when(step == 0)
        def _():
            for i in range(n):
                mine, _ = _halves(out_refs[i].shape[1], c)
                for j, chip in enumerate(chips):
                    copy(j, i, q, mine, (*chip, c)).start()

        @pl.when(step == last)
        def _():
            for j, chip in enumerate(chips):
                qj = 2 * chip[0] + chip[1]
                for i in range(n):
                    mine, _ = _halves(out_refs[i].shape[1], c)
                    copy(j, i, qj, mine, (x, y, c)).wait_recv()
                    copy(3 + j, i, qj, mine, (x, y, 1 - c)).start()
        return

    @pl.when(step == last)
    def _():
        for j, chip in enumerate(chips):
            qj = 2 * chip[0] + chip[1]
            for i in range(n):
                mine, other = _halves(out_refs[i].shape[1], c)
                copy(3 + j, i, qj, other, (x, y, c)).wait_recv()
                copy(j, i, q, mine, (x, y, c)).wait_send()
                copy(3 + j, i, qj, mine, (x, y, c)).wait_send()


def _group_matrix(gw_ref, g):
    rows = GROUP_DIM // N_CHIPS
    return jnp.concatenate([gw_ref[j, rows * g:rows * (g + 1), :] for j in range(N_CHIPS)], axis=0)


def pool_forward(x, w0, wpi, gw, gb, scale, wpo, later):
    s = x.shape[0]
    ts = ROW_TILE
    nt = s // ts
    assert nt >= 2
    n_later = len(later)

    def body(x_ref, w0_ref, wpi_ref, gw_ref, gb_ref, sc_ref, wpo_ref, *rest):
        rest = rest[n_later:]
        h1_ref, pooled_ref, gt_ref, n0_ref = rest[:4]
        later_refs = rest[4:4 + n_later]
        ubuf, tbuf, hist, send_sems, recv_sems = rest[4 + n_later:]
        i = pl.program_id(0)
        gather_in_background(i, nt - 1, later_refs, send_sems, recv_sems, finish=False)
        xv = x_ref[...]
        r = lax.rsqrt(jnp.mean(xv * xv, axis=-1, keepdims=True) + EPS)
        n0 = _bf(xv * r * w0_ref[...])
        n0_ref[...] = n0
        u = jnp.concatenate([_nn(n0, wpi_ref[0]), _nn(n0, wpi_ref[1])], axis=-1)
        gt = jnp.concatenate([_nn(n0, wpi_ref[2]), _nn(n0, wpi_ref[3])], axis=-1)
        gt_ref[...] = gt

        @pl.when(i == 0)
        def _():
            hist[...] = jnp.zeros_like(hist)

        ubuf[0:HALO, :] = hist[...]
        ubuf[HALO:HALO + ts, :] = u
        hist[...] = u[ts - HALO:, :]
        inv = _inverse_counts(_row_index(i, ts))
        mixed = []
        for g, w in enumerate(POOL_WINDOWS):
            cols = slice(g * GROUP_DIM, (g + 1) * GROUP_DIM)
            pooled = _bf(_trailing_sums(ubuf, tbuf, cols, w, ts) * inv[g] - u[:, cols])
            pooled_ref[:, cols] = pooled
            mixed.append(_nn(pooled, _group_matrix(gw_ref, g)))
        mixed = jnp.concatenate(mixed, axis=-1) + gb_ref[...]
        y = mixed * sc_ref[...] * (gt * _sigmoid(gt))
        h1_ref[...] = xv + _nn(_bf(y), wpo_ref[...])
        gather_in_background(i, nt - 1, later_refs, send_sems, recv_sems, finish=True)

    row = lambda cols: pl.BlockSpec((ts, cols), lambda i: (i, 0))
    outs = pl.pallas_call(
        body, name="pool_forward", grid=(nt,),
        out_shape=[jax.ShapeDtypeStruct((s, D), F32), jax.ShapeDtypeStruct((s, D), BF16),
                   jax.ShapeDtypeStruct((s, D), F32), jax.ShapeDtypeStruct((s, D), BF16)]
                  + [jax.ShapeDtypeStruct(a.shape, a.dtype) for a in later],
        in_specs=[row(D), _full((1, D)), _full((N_CHIPS, D, D // 2)), _full((GROUPS, GROUP_DIM, GROUP_DIM)),
                  _full((1, D)), _full((1, D)), _full((D, D))] + _any_specs(n_later),
        out_specs=[row(D), row(D), row(D), row(D)] + _any_specs(n_later),
        input_output_aliases={7 + k: 4 + k for k in range(n_later)},
        scratch_shapes=[pltpu.VMEM((HALO + ts, D), F32), pltpu.VMEM((HALO + ts, D), F32),
                        pltpu.VMEM((HALO, D), F32),
                        pltpu.SemaphoreType.DMA((6 * n_later,)), pltpu.SemaphoreType.DMA((6 * n_later,))],
        compiler_params=_params("arbitrary"),
    )(x, w0, wpi, gw, gb, scale, wpo, *later)
    return outs[:4], outs[4:]


def pool_backward(x, dh1, pooled, gt, w0, wpi, gw, gb, scale, wpo, chip_sums):
    s = x.shape[0]
    ts = ROW_TILE
    nt = s // ts
    n_sums = len(chip_sums)

    def body(x_ref, dh1_ref, pooled_ref, gt_ref, w0_ref, wpi_ref, gw_ref, gb_ref, sc_ref, wpo_ref, *rest):
        sum_refs, rest = rest[:n_sums], rest[n_sums:]
        dx_ref, dproj_ref, gpo_ref, ggw_ref, small_ref = rest[:5]
        got_refs = rest[5:5 + n_sums]
        ebuf, tbuf, ahead, send_sems, recv_sems = rest[5 + n_sums:]
        i = pl.program_id(0)
        copies = _scatter_copies(sum_refs, got_refs, send_sems, recv_sems)

        @pl.when(i == 0)
        def _():
            for cp in copies:
                cp.start()

        @pl.when(i == 0)
        def _():
            gpo_ref[...] = jnp.zeros_like(gpo_ref)
            ggw_ref[...] = jnp.zeros_like(ggw_ref)
            small_ref[...] = jnp.zeros_like(small_ref)
            ahead[...] = jnp.zeros_like(ahead)

        dh1 = dh1_ref[...]
        dh1_bf = _bf(dh1)
        gt = gt_ref[...]
        sc = sc_ref[...]
        dy = _nt(dh1_bf, wpo_ref[...])
        pooled_bf = []
        mixed = []
        for g in range(GROUPS):
            cols = slice(g * GROUP_DIM, (g + 1) * GROUP_DIM)
            pb = pooled_ref[:, cols]
            pooled_bf.append(pb)
            mixed.append(_nn(pb, _group_matrix(gw_ref, g)))
        mixed = jnp.concatenate(mixed, axis=-1) + gb_ref[...]
        sg = _sigmoid(gt)
        silu = gt * sg
        gpo_ref[...] += _tn(_bf(mixed * sc * silu), dh1_bf)
        dmixed = dy * sc * silu
        dgt = dy * mixed * sc * (sg * (1.0 + gt * (1.0 - sg)))
        dproj_ref[:, D:] = _bf(dgt)
        small_ref[1:2, :] += jnp.sum(dy * mixed * silu, axis=0, keepdims=True)
        small_ref[2:3, :] += jnp.sum(dmixed, axis=0, keepdims=True)

        inv = _inverse_counts(_row_index(nt - 1 - i, ts))
        rows_q = GROUP_DIM // N_CHIPS
        ebuf[ts:ts + HALO, :] = ahead[...]
        dpooled = []
        for g in range(GROUPS):
            cols = slice(g * GROUP_DIM, (g + 1) * GROUP_DIM)
            dm = _bf(dmixed[:, cols])
            ggw = _tn(pooled_bf[g], dm)
            for j in range(N_CHIPS):
                ggw_ref[j, rows_q * g:rows_q * (g + 1), :] += ggw[rows_q * j:rows_q * (j + 1), :]
            dp = _nt(dm, _group_matrix(gw_ref, g))
            dpooled.append(dp)
            ebuf[0:ts, cols] = dp * inv[g]
        ahead[...] = ebuf[0:HALO, :]
        du = []
        for g, w in enumerate(POOL_WINDOWS):
            cols = slice(g * GROUP_DIM, (g + 1) * GROUP_DIM)
            du.append(_leading_sums(ebuf, tbuf, cols, w, ts) - dpooled[g])
        du = _bf(jnp.concatenate(du, axis=-1))
        dproj_ref[:, :D] = du
        dgt_bf = _bf(dgt)
        half = D // 2
        dn0 = (_nt(du[:, :half], wpi_ref[0]) + _nt(du[:, half:], wpi_ref[1])
               + _nt(dgt_bf[:, :half], wpi_ref[2]) + _nt(dgt_bf[:, half:], wpi_ref[3]))

        xv = x_ref[...]
        r = lax.rsqrt(jnp.mean(xv * xv, axis=-1, keepdims=True) + EPS)
        xhat = xv * r
        small_ref[0:1, :] += jnp.sum(dn0 * xhat, axis=0, keepdims=True)
        dxh = dn0 * w0_ref[...]
        dx_ref[...] = dh1 + r * (dxh - xhat * jnp.mean(dxh * xhat, axis=-1, keepdims=True))

        @pl.when(i == nt - 1)
        def _():
            for cp in copies:
                cp.wait()

    row = lambda cols: pl.BlockSpec((ts, cols), lambda i: (nt - 1 - i, 0))
    outs = pl.pallas_call(
        body, name="pool_backward", grid=(nt,),
        out_shape=[jax.ShapeDtypeStruct((s, D), F32), jax.ShapeDtypeStruct((s, 2 * D), BF16),
                   jax.ShapeDtypeStruct((D, D), F32),
                   jax.ShapeDtypeStruct((GROUPS, GROUP_DIM, GROUP_DIM), F32),
                   jax.ShapeDtypeStruct((8, D), F32)] + _scatter_shapes(chip_sums),
        in_specs=[row(D), row(D), row(D), row(D), _full((1, D)), _full((N_CHIPS, D, D // 2)),
                  _full((GROUPS, GROUP_DIM, GROUP_DIM)), _full((1, D)), _full((1, D)), _full((D, D))]
                 + _any_specs(n_sums),
        out_specs=[row(D), row(2 * D), _full((D, D)), _full((GROUPS, GROUP_DIM, GROUP_DIM)), _full((8, D))]
                  + _any_specs(n_sums),
        scratch_shapes=[pltpu.VMEM((ts + HALO, D), F32), pltpu.VMEM((ts + HALO, D), F32),
                        pltpu.VMEM((HALO, D), F32),
                        pltpu.SemaphoreType.DMA((3 * n_sums,)), pltpu.SemaphoreType.DMA((3 * n_sums,))],
        compiler_params=_params("arbitrary"),
    )(x, dh1, pooled, gt, w0, wpi, gw, gb, scale, wpo, *chip_sums)
    return outs[:5], outs[5:]


def gla_project(h1, w1, wgi_q, wgk, bgk, later):
    s = h1.shape[0]
    ts = ROW_TILE
    nt = s // ts
    assert nt >= 2
    n_later = len(later)

    def body(h_ref, w1_ref, wq_ref, wgk_ref, bgk_ref, *rest):
        rest = rest[n_later:]
        qk_ref, v_ref, gate_ref, low_ref, cum_ref, n1_ref = rest[:6]
        later_refs = rest[6:6 + n_later]
        send_sems, recv_sems, wgi_ref = rest[6 + n_later:]
        gather_in_background(pl.program_id(0), nt - 1, later_refs, send_sems, recv_sems, finish=False)

        @pl.when(pl.program_id(0) == 0)
        def _():
            _assemble_gla_in(wq_ref, wgi_ref)

        hv = h_ref[...]
        r = lax.rsqrt(jnp.mean(hv * hv, axis=-1, keepdims=True) + EPS)
        n1 = _bf(hv * r * w1_ref[...])
        n1_ref[...] = n1
        qk_ref[...] = _nn(n1, wgi_ref[:, 0:2 * KEY_W])
        v_ref[...] = _bf(_nn(n1, wgi_ref[:, 2 * KEY_W:2 * KEY_W + D]))
        gate_ref[...] = _nn(n1, wgi_ref[:, 2 * KEY_W + D:GLA_MAIN])
        low = _bf(_nn(n1, wgi_ref[:, GLA_MAIN:]))
        low_ref[...] = low
        z = _nn(low, wgk_ref[...]) + bgk_ref[...]
        lg = (jnp.minimum(z, 0.0) - jnp.log(1.0 + jnp.exp(-jnp.abs(z)))) / GATE_NORM
        lower_f = _chunk_masks()[0].astype(F32)
        for r0 in range(0, ts, CHUNK):
            cum_ref[r0:r0 + CHUNK, :] = _nn_exact(lower_f, lg[r0:r0 + CHUNK, :])
        gather_in_background(pl.program_id(0), nt - 1, later_refs, send_sems, recv_sems, finish=True)

    row = lambda cols: pl.BlockSpec((ts, cols), lambda i: (i, 0))
    outs = pl.pallas_call(
        body, name="gla_project", grid=(nt,),
        out_shape=[jax.ShapeDtypeStruct((s, D), F32), jax.ShapeDtypeStruct((s, D), BF16),
                   jax.ShapeDtypeStruct((s, D), F32), jax.ShapeDtypeStruct((s, RANK_PAD), BF16),
                   jax.ShapeDtypeStruct((s, KEY_W), F32), jax.ShapeDtypeStruct((s, D), BF16)]
                  + [jax.ShapeDtypeStruct(a.shape, a.dtype) for a in later],
        in_specs=[row(D), _full((1, D)), _full((N_CHIPS, D, GLA_IN_QUARTER)),
                  _full((RANK_PAD, KEY_W)), _full((1, KEY_W))] + _any_specs(n_later),
        out_specs=[row(D), row(D), row(D), row(RANK_PAD), row(KEY_W), row(D)] + _any_specs(n_later),
        input_output_aliases={5 + k: 6 + k for k in range(n_later)},
        scratch_shapes=[pltpu.SemaphoreType.DMA((6 * n_later,)), pltpu.SemaphoreType.DMA((6 * n_later,)),
                        pltpu.VMEM((D, GLA_MAIN + RANK_PAD), BF16)],
        compiler_params=_params("arbitrary"),
    )(h1, w1, wgi_q, wgk, bgk, *later)
    return outs[:6], outs[6:]


def _assemble_gla_in(wq_ref, wfull):
    pad = jnp.zeros((CAST_ROWS, GLA_MAIN + RANK_PAD - GLA_IN), BF16)
    for r0 in range(0, D, CAST_ROWS):
        rows = slice(r0, r0 + CAST_ROWS)
        wfull[rows, :] = jnp.concatenate([wq_ref[q, rows, :] for q in range(N_CHIPS)] + [pad], axis=1)


GLA_BLOCK = 512
CHUNKS_PER_BLOCK = GLA_BLOCK // CHUNK


def _chunk_masks():
    t = lax.broadcasted_iota(jnp.int32, (CHUNK, CHUNK), 0)
    u = lax.broadcasted_iota(jnp.int32, (CHUNK, CHUNK), 1)
    return t >= u, t <= u


def _gla_chunk_terms(q, cum):
    ep = jnp.exp(cum)
    en = jnp.exp(-cum)
    qs = q * (HEAD_K ** -0.5)
    last = cum[CHUNK - 1:CHUNK, :]
    ed = jnp.exp(last - cum)
    dec = jnp.exp(last)
    return ep, en, qs, ed, dec


def gla_forward(qk, v, cum):
    s = qk.shape[0]
    nb = s // GLA_BLOCK
    nc = s // CHUNK

    def body(q_ref, k_ref, v_ref, cum_ref, o_ref, st_ref, sc_ref, state):
        @pl.when(pl.program_id(0) == 0)
        def _():
            state[...] = jnp.zeros_like(state)

        lower, _ = _chunk_masks()

        def chunk(cc, carry):
            rows = pl.ds(pl.multiple_of(cc * CHUNK, CHUNK), CHUNK)
            for h in range(HEADS):
                kc = slice(h * HEAD_K, (h + 1) * HEAD_K)
                vc = slice(h * HEAD_V, (h + 1) * HEAD_V)
                q = q_ref[rows, kc]
                k = k_ref[rows, kc]
                v = v_ref[rows, vc]
                ep, en, qs, ed, dec = _gla_chunk_terms(q, cum_ref[rows, kc])
                a = _bf(qs * ep)
                fwd = _nt(a, _bf(k * en))
                bwd = _nt(_bf(qs * en), _bf(k * ep))
                scores = _bf(jnp.where(lower, fwd, bwd))
                sc_ref[rows, h * CHUNK:(h + 1) * CHUNK] = scores
                st = state[h]
                st_ref[cc, h] = st
                o_ref[rows, vc] = _nn(scores, v) + _nt(a, _bf(st))
                state[h] = st * dec + _tn(v, _bf(k * ed))
            return carry

        lax.fori_loop(0, CHUNKS_PER_BLOCK, chunk, 0, unroll=True)

    return pl.pallas_call(
        body, name="gla_forward", grid=(nb,),
        out_shape=(jax.ShapeDtypeStruct((s, D), F32),
                   jax.ShapeDtypeStruct((nc, HEADS, HEAD_V, HEAD_K), F32),
                   jax.ShapeDtypeStruct((s, HEADS * CHUNK), BF16)),
        in_specs=[pl.BlockSpec((GLA_BLOCK, KEY_W), lambda i: (i, 0)),
                  pl.BlockSpec((GLA_BLOCK, KEY_W), lambda i: (i, 1)),
                  pl.BlockSpec((GLA_BLOCK, D), lambda i: (i, 0)),
                  pl.BlockSpec((GLA_BLOCK, KEY_W), lambda i: (i, 0))],
        out_specs=(pl.BlockSpec((GLA_BLOCK, D), lambda i: (i, 0)),
                   pl.BlockSpec((CHUNKS_PER_BLOCK, HEADS, HEAD_V, HEAD_K), lambda i: (i, 0, 0, 0)),
                   pl.BlockSpec((GLA_BLOCK, HEADS * CHUNK), lambda i: (i, 0))),
        scratch_shapes=[pltpu.VMEM((HEADS, HEAD_V, HEAD_K), F32)],
        compiler_params=_params("arbitrary"),
    )(qk, qk, v, cum)


def gla_backward(qk, v, cum, do, states, scores):
    s = qk.shape[0]
    nb = s // GLA_BLOCK

    def body(q_ref, k_ref, v_ref, cum_ref, do_ref, st_ref, sc_ref, dq_ref, dk_ref, dv_ref, dcum_ref, dstate):
        @pl.when(pl.program_id(0) == 0)
        def _():
            dstate[...] = jnp.zeros_like(dstate)

        lower, _ = _chunk_masks()
        is_last = lax.broadcasted_iota(jnp.int32, (CHUNK, HEAD_K), 0) == CHUNK - 1

        def chunk(step, carry):
            cc = CHUNKS_PER_BLOCK - 1 - step
            rows = pl.ds(pl.multiple_of(cc * CHUNK, CHUNK), CHUNK)
            for h in range(HEADS):
                kc = slice(h * HEAD_K, (h + 1) * HEAD_K)
                vc = slice(h * HEAD_V, (h + 1) * HEAD_V)
                q = q_ref[rows, kc]
                k = k_ref[rows, kc]
                v = v_ref[rows, vc]
                do_c = do_ref[rows, vc]
                ep, en, qs, ed, dec = _gla_chunk_terms(q, cum_ref[rows, kc])
                a = _bf(qs * ep)
                b = _bf(k * en)
                c = _bf(qs * en)
                dk_dec = _bf(k * ep)
                kd = _bf(k * ed)
                scores = sc_ref[rows, h * CHUNK:(h + 1) * CHUNK]
                st = st_ref[cc, h]
                dst = dstate[h]
                dst_bf = _bf(dst)

                dscores = _nt(do_c, v)
                dfwd = _bf(jnp.where(lower, dscores, 0.0))
                dbwd = _bf(jnp.where(lower, 0.0, dscores))
                dv_ref[rows, vc] = _bf(_tn(scores, do_c) + _nt(kd, dst_bf))
                da = _nn(dfwd, b) + _nn(do_c, _bf(st))
                db = _tn(dfwd, a)
                dc = _nn(dbwd, dk_dec)
                ddk = _tn(dbwd, c)
                dkd = _nn(v, dst_bf)
                ddec = jnp.sum(dst * st, axis=0, keepdims=True)
                dstate[h] = dst * dec + _tn(do_c, a)

                m = dkd * k * ed
                dq_ref[rows, kc] = _bf((da * ep + dc * en) * (HEAD_K ** -0.5))
                dk_ref[rows, kc] = _bf(db * en + ddk * ep + dkd * ed)
                dcum = (da * qs + ddk * k) * ep - (db * k + dc * qs) * en - m
                dlast = jnp.sum(m, axis=0, keepdims=True) + ddec * dec
                dcum_ref[rows, kc] = dcum + jnp.where(is_last, dlast, 0.0)
            return carry

        lax.fori_loop(0, CHUNKS_PER_BLOCK, chunk, 0, unroll=True)

    rev = lambda cols, col_block: pl.BlockSpec((GLA_BLOCK, cols), lambda i: (nb - 1 - i, col_block))
    return pl.pallas_call(
        body, name="gla_backward", grid=(nb,),
        out_shape=(jax.ShapeDtypeStruct((s, KEY_W), BF16), jax.ShapeDtypeStruct((s, KEY_W), BF16),
                   jax.ShapeDtypeStruct((s, D), BF16), jax.ShapeDtypeStruct((s, KEY_W), F32)),
        in_specs=[rev(KEY_W, 0), rev(KEY_W, 1), rev(D, 0), rev(KEY_W, 0), rev(D, 0),
                  pl.BlockSpec((CHUNKS_PER_BLOCK, HEADS, HEAD_V, HEAD_K), lambda i: (nb - 1 - i, 0, 0, 0)),
                  rev(HEADS * CHUNK, 0)],
        out_specs=(rev(KEY_W, 0), rev(KEY_W, 0), rev(D, 0), rev(KEY_W, 0)),
        scratch_shapes=[pltpu.VMEM((HEADS, HEAD_V, HEAD_K), F32)],
        compiler_params=_params("arbitrary"),
    )(qk, qk, v, cum, do, states, scores)


def head_and_loss(o, gate, h1, target, hw, wgo, wf):
    s = o.shape[0]
    ts = ROW_TILE

    def body(o_ref, gate_ref, h1_ref, tgt_ref, hw_ref, wgo_ref, wf_ref,
             dh2_ref, do_ref, dgate_ref, ggo_ref, small_ref):
        @pl.when(pl.program_id(0) == 0)
        def _():
            ggo_ref[...] = jnp.zeros_like(ggo_ref)
            small_ref[...] = jnp.zeros_like(small_ref)

        gate = gate_ref[...]
        hw = hw_ref[...]
        sg = _sigmoid(gate)
        silu = gate * sg
        ohat, ro = [], []
        for h in range(HEADS):
            oh = o_ref[:, h * HEAD_V:(h + 1) * HEAD_V]
            rh = lax.rsqrt(jnp.mean(oh * oh, axis=-1, keepdims=True) + EPS)
            ro.append(rh)
            ohat.append(oh * rh)
        ohat = jnp.concatenate(ohat, axis=-1)
        on = ohat * hw
        y2 = _bf(on * silu)
        h2 = h1_ref[...] + _nn(y2, wgo_ref[...])
        rf = lax.rsqrt(jnp.mean(h2 * h2, axis=-1, keepdims=True) + EPS)
        h2hat = h2 * rf
        wf = wf_ref[...]
        diff = h2hat * wf - tgt_ref[...]
        small_ref[2:3, :] += jnp.zeros((1, D), F32) + 0.5 * jnp.sum(diff * diff) / D
        dout = diff / D
        small_ref[0:1, :] += jnp.sum(dout * h2hat, axis=0, keepdims=True)
        dxh = dout * wf
        dh2 = rf * (dxh - h2hat * jnp.mean(dxh * h2hat, axis=-1, keepdims=True))
        dh2_ref[...] = dh2
        dh2_bf = _bf(dh2)
        ggo_ref[...] += _tn(y2, dh2_bf)
        dy2 = _nt(dh2_bf, wgo_ref[...])
        don = dy2 * silu
        dgate_ref[...] = _bf(dy2 * on * (sg * (1.0 + gate * (1.0 - sg))))
        ghw = jnp.sum(don * ohat, axis=0, keepdims=True)
        small_ref[1:2, 0:HEAD_V] += sum(ghw[:, h * HEAD_V:(h + 1) * HEAD_V] for h in range(HEADS))
        dohat = don * hw
        for h in range(HEADS):
            cols = slice(h * HEAD_V, (h + 1) * HEAD_V)
            oh, dh = ohat[:, cols], dohat[:, cols]
            do_ref[:, cols] = _bf(ro[h] * (dh - oh * jnp.mean(dh * oh, axis=-1, keepdims=True)))

    row = lambda cols: pl.BlockSpec((ts, cols), lambda i: (i, 0))
    act = jax.ShapeDtypeStruct((s, D), F32)
    act_bf = jax.ShapeDtypeStruct((s, D), BF16)
    return pl.pallas_call(
        body, name="head_and_loss", grid=(s // ts,),
        out_shape=(act, act_bf, act_bf, jax.ShapeDtypeStruct((D, D), F32), jax.ShapeDtypeStruct((8, D), F32)),
        in_specs=[row(D), row(D), row(D), row(D),
                  _full((1, D)), _full((D, D)), _full((1, D))],
        out_specs=(row(D), row(D), row(D), _full((D, D)), _full((8, D))),
        compiler_params=_params("arbitrary"),
    )(o, gate, h1, target, hw, wgo, wf)


def gla_project_backward(dq, dk, dv, dgate, dcum, low, h1, dh2, w1, wgi_q, wgk, bgk):
    s = h1.shape[0]
    ts = ROW_TILE

    def body(dq_ref, dk_ref, dv_ref, dgate_ref, dcum_ref, low_ref, h1_ref, dh2_ref, w1_ref,
             wq_ref, wgk_ref, bgk_ref, dh1_ref, dproj_ref, ggk_ref, small_ref, wgi_ref):
        @pl.when(pl.program_id(0) == 0)
        def _():
            ggk_ref[...] = jnp.zeros_like(ggk_ref)
            small_ref[...] = jnp.zeros_like(small_ref)
            _assemble_gla_in(wq_ref, wgi_ref)

        low = low_ref[...]
        z = _nn(low, wgk_ref[...]) + bgk_ref[...]
        upper_f = _chunk_masks()[1].astype(F32)
        dlg = jnp.concatenate([_nn_exact(upper_f, dcum_ref[r0:r0 + CHUNK, :]) for r0 in range(0, ts, CHUNK)],
                              axis=0)
        dz = dlg * (1.0 / GATE_NORM) * _sigmoid(-z)
        dz_bf = _bf(dz)
        ggk_ref[...] += _tn(low, dz_bf)
        small_ref[1:2, 0:KEY_W] += jnp.sum(dz, axis=0, keepdims=True)
        dlow = _bf(_nt(dz_bf, wgk_ref[...]))
        dproj_ref[:, GLA_MAIN:] = dlow
        dn1 = _nt(dlow, wgi_ref[:, GLA_MAIN:])
        for ref, lo, hi in ((dq_ref, 0, KEY_W), (dk_ref, KEY_W, 2 * KEY_W),
                            (dv_ref, 2 * KEY_W, 2 * KEY_W + D), (dgate_ref, 2 * KEY_W + D, GLA_MAIN)):
            piece = ref[...]
            dproj_ref[:, lo:hi] = piece
            dn1 = dn1 + _nt(piece, wgi_ref[:, lo:hi])
        hv = h1_ref[...]
        r = lax.rsqrt(jnp.mean(hv * hv, axis=-1, keepdims=True) + EPS)
        hhat = hv * r
        small_ref[0:1, :] += jnp.sum(dn1 * hhat, axis=0, keepdims=True)
        dxh = dn1 * w1_ref[...]
        dh1_ref[...] = dh2_ref[...] + r * (dxh - hhat * jnp.mean(dxh * hhat, axis=-1, keepdims=True))

    row = lambda cols: pl.BlockSpec((ts, cols), lambda i: (i, 0))
    return pl.pallas_call(
        body, name="gla_project_backward", grid=(s // ts,),
        out_shape=(jax.ShapeDtypeStruct((s, D), F32), jax.ShapeDtypeStruct((s, GLA_MAIN + RANK_PAD), BF16),
                   jax.ShapeDtypeStruct((RANK_PAD, KEY_W), F32),
                   jax.ShapeDtypeStruct((8, D), F32)),
        in_specs=[row(KEY_W), row(KEY_W), row(D), row(D), row(KEY_W), row(RANK_PAD), row(D), row(D),
                  _full((1, D)), _full((N_CHIPS, D, GLA_IN_QUARTER)), _full((RANK_PAD, KEY_W)),
                  _full((1, KEY_W))],
        out_specs=(row(D), row(GLA_MAIN + RANK_PAD), _full((RANK_PAD, KEY_W)), _full((8, D))),
        scratch_shapes=[pltpu.VMEM((D, GLA_MAIN + RANK_PAD), BF16)],
        compiler_params=_params("arbitrary"),
    )(dq, dk, dv, dgate, dcum, low, h1, dh2, w1, wgi_q, wgk, bgk)


def local_gradients(xs, target, w0, w1, wf, wpi, gw, gb, scale, wpo, gla_quarters, wgk, bgk, hw_tiled, place):
    wgi_q, wgo_q = gla_quarters
    (h1, pooled, gt, n0), (wgi_q,) = pool_forward(xs, w0, wpi, gw, gb, scale, wpo, [wgi_q])
    (qk, v, gate, low, cum, n1), (wgo_q,) = gla_project(h1, w1, wgi_q, wgk, bgk, [wgo_q])
    wgo = wgo_q.reshape(D, D)
    o, states, scores = gla_forward(qk, v, cum)

    dh2, do, dgate, g_gla_out, small_top = head_and_loss(o, gate, h1, target, hw_tiled, wgo, wf)
    dq, dk, dv, dcum = gla_backward(qk, v, cum, do, states, scores)
    dh1, dproj, g_gk_pad, small_gla = gla_project_backward(
        dq, dk, dv, dgate, dcum, low, h1, dh2, w1, wgi_q, wgk, bgk)
    g_gla_in, _ = matmul_tn(n1, dproj, "grad_gla_in", (GLA_MAIN + RANK_PAD) // 5, place)

    gla_sums = add_halves([g_gla_in, g_gla_out.reshape(N_CHIPS, D // N_CHIPS, D)], place, "add_halves_gla")
    (dx, dpool, g_pool_out, g_group_w, small_pool), gla_got = pool_backward(
        xs, dh1, pooled, gt, w0, wpi, gw, gb, scale, wpo, [b for _, b in gla_sums])
    g_pool_in, mix = matmul_tn(n0, dpool, "grad_pool_in", D // 2, place, by_column_tile=True,
                               also_reduce=[g_group_w, g_pool_out.reshape(N_CHIPS, D // N_CHIPS, D)])

    reduced, total = join_halves(
        g_pool_in, place, [own for own, _ in mix] + [f for f, _ in gla_sums], [got for _, got in mix] + list(gla_got),
        small_pool, small_gla, small_top, g_gk_pad)
    return dx, reduced, total


def kernel(x, norm_w, pool_in_w, pool_group_w, pool_group_b, pool_scale, pool_out_w, gla_in_w, gla_gk_w, gla_gk_b, gla_head_norm_w, gla_out_w, final_norm_w, loss_target, m_norm_w, m_pool_in_w, m_pool_group_w, m_pool_group_b, m_pool_scale, m_pool_out_w, m_gla_in_w, m_gla_gk_w, m_gla_gk_b, m_gla_head_norm_w, m_gla_out_w, m_final_norm_w, v_norm_w, v_pool_in_w, v_pool_group_w, v_pool_group_b, v_pool_scale, v_pool_out_w, v_gla_in_w, v_gla_gk_w, v_gla_gk_b, v_gla_head_norm_w, v_gla_out_w, v_final_norm_w):
    xs = x[0]
    target = loss_target[0]
    q_chip = 2 * lax.axis_index("x") + lax.axis_index("y")
    place = jnp.stack([lax.axis_index("c"), q_chip]).astype(jnp.int32)

    (wpi, gw_q, wpo_q, wgi_q, wgo_q), (bgk, hw_tiled, gb, wgk) = allgather_weights(
        [pool_in_w[0], pool_group_w[0].reshape(GROUP_DIM, GROUP_DIM), pool_out_w[0], gla_in_w[0], gla_out_w[0]],
        exchange=(True, True, True, False, False),
        smalls=[gla_gk_b, gla_head_norm_w, pool_group_b[0], gla_gk_w[0]])
    wpo = wpo_q.reshape(D, D)

    w0 = norm_w[0:1]
    w1 = norm_w[1:2]
    wf = final_norm_w.reshape(1, D)

    dx, reduced, total = local_gradients(
        xs, target, w0, w1, wf, wpi, gw_q, gb, pool_scale, wpo, [wgi_q, wgo_q], wgk, bgk, hw_tiled, place)
    r_pool_in, r_group_w, r_pool_out, r_gla_in, r_gla_out = reduced
    r_group_w = r_group_w.reshape(GROUPS, 64, GROUP_DIM)

    turn = lambda a: jnp.transpose(a, (2, 0, 1))
    back = lambda a: jnp.transpose(a, (1, 2, 0))
    as2d = lambda a, w: a.reshape(-1, w.shape[-1])
    big_names = ("pool_in_w", "pool_group_w", "pool_out_w", "gla_in_w", "gla_out_w")
    big_args = [(pool_in_w, r_pool_in[None], m_pool_in_w, v_pool_in_w),
                (pool_group_w, r_group_w[None], m_pool_group_w, v_pool_group_w),
                (pool_out_w, r_pool_out[None], m_pool_out_w, v_pool_out_w),
                (gla_in_w, r_gla_in[None], m_gla_in_w, v_gla_in_w),
                (gla_out_w, r_gla_out[None], m_gla_out_w, v_gla_out_w)]
    to_kernel = lambda n, a, w: turn(a) if n == "gla_in_w" else as2d(a, w)
    from_kernel = lambda n, a, w: back(a) if n == "gla_in_w" else a.reshape(w.shape)
    big_in = [tuple(to_kernel(n, a, p[0]) for a in p) for n, p in zip(big_names, big_args)]
    big_out = adamw(big_in, "adamw", echo=[n != "gla_in_w" for n in big_names])
    big = {}
    for n, p, i, out in zip(big_names, big_args, big_in, big_out):
        g = out[3] if len(out) == 4 else i[1]
        big[n] = tuple(from_kernel(n, o, p[0]) for o in (g, *out[:3]))

    small_names = ("norm_w", "pool_group_b", "pool_scale", "gla_gk_w", "gla_gk_b", "gla_head_norm_w",
                   "final_norm_w")
    small_args = [(norm_w, m_norm_w, v_norm_w),
                  (pool_group_b, m_pool_group_b, v_pool_group_b),
                  (pool_scale, m_pool_scale, v_pool_scale),
                  (gla_gk_w, m_gla_gk_w, v_gla_gk_w),
                  (gla_gk_b, m_gla_gk_b, v_gla_gk_b),
                  (gla_head_norm_w, m_gla_head_norm_w, v_gla_head_norm_w),
                  (final_norm_w, m_final_norm_w, v_final_norm_w)]
    small_out, loss = adamw_small([tuple(as2d(a, p[0]) for a in p) for p in small_args], total, place)
    small = {n: tuple(o.reshape(p[0].shape) for o in out) for n, p, out in zip(small_names, small_args, small_out)}
    results = [
        small["norm_w"],
        big["pool_in_w"],
        big["pool_group_w"],
        small["pool_group_b"],
        small["pool_scale"],
        big["pool_out_w"],
        big["gla_in_w"],
        small["gla_gk_w"],
        small["gla_gk_b"],
        small["gla_head_norm_w"],
        big["gla_out_w"],
        small["final_norm_w"],
    ]
    grads, deltas, new_m, new_v = zip(*results)
    return (loss.reshape(()), dx[None], *grads, *deltas, *new_m, *new_v)
```

```python
import jax
import jax.numpy as jnp
from jax import lax
from jax.experimental import pallas as pl
from jax.experimental.pallas import tpu as pltpu

F32 = jnp.float32
BF16 = jnp.bfloat16
MESH = pl.DeviceIdType.MESH

D = 1024
POOL_WINDOWS = (2, 4, 8, 16)
GROUPS = 4
GROUP_DIM = 256
HEADS = 4
HEAD_K = 128
HEAD_V = 256
KEY_W = 512
CHUNK = 64
GATE_RANK = 16
GATE_NORM = 16.0
GLA_IN = 3088
GLA_MAIN = 3072
RANK_PAD = 128
EPS = 1e-6
HALO = 32

ADAM_LR = 0.001
ADAM_B1 = 0.9
ADAM_B2 = 0.999
ADAM_EPS = 1e-08
ADAM_WD = 0.01
ADAM_STEP = 10

N_CHIPS = 4
N_DEV = 8
GLA_IN_QUARTER = GLA_IN // N_CHIPS

VMEM_LIMIT = 56 * 1024 * 1024


def _nn(a, b):
    return lax.dot_general(a, b, (((1,), (0,)), ((), ())), preferred_element_type=F32)


def _nt(a, b):
    return lax.dot_general(a, b, (((1,), (1,)), ((), ())), preferred_element_type=F32)


def _tn(a, b):
    return lax.dot_general(a, b, (((0,), (0,)), ((), ())), preferred_element_type=F32)


def _nn_exact(a, b):
    return lax.dot_general(a, b, (((1,), (0,)), ((), ())), preferred_element_type=F32,
                           precision=lax.Precision.HIGHEST)


def _bf(a):
    return a.astype(BF16)


def _params(*sem):
    return pltpu.CompilerParams(dimension_semantics=sem, vmem_limit_bytes=VMEM_LIMIT)


def _full(shape):
    return pl.BlockSpec(shape, lambda i: (0,) * len(shape))


def _position():
    return lax.axis_index("x"), lax.axis_index("y"), lax.axis_index("c")


def _gather_small(in_ref, all_ref, send_sems, recv_sems, local_sem):
    x, y, c = _position()
    me = 4 * x + 2 * y + c
    mine = pltpu.make_async_copy(in_ref, all_ref.at[me], local_sem)
    sends = []
    for k in range(N_DEV - 1):
        fx, fy, fc = (k + 1) >> 2 & 1, (k + 1) >> 1 & 1, (k + 1) & 1
        sends.append(pltpu.make_async_remote_copy(
            src_ref=in_ref, dst_ref=all_ref.at[me],
            send_sem=send_sems.at[k], recv_sem=recv_sems.at[k],
            device_id=(x ^ fx, y ^ fy, c ^ fc), device_id_type=MESH))

    def start():
        mine.start()
        for cp in sends:
            cp.start()

    def wait():
        for k in range(N_DEV - 1):
            fx, fy, fc = (k + 1) >> 2 & 1, (k + 1) >> 1 & 1, (k + 1) & 1
            src_dev = 4 * (x ^ fx) + 2 * (y ^ fy) + (c ^ fc)
            pltpu.make_async_remote_copy(
                src_ref=in_ref, dst_ref=all_ref.at[src_dev],
                send_sem=send_sems.at[k], recv_sem=recv_sems.at[k],
                device_id=(x, y, c), device_id_type=MESH).wait_recv()
        for cp in sends:
            cp.wait_send()
        mine.wait()

    return start, wait


SMALL_SEMS = [pltpu.SemaphoreType.DMA((N_DEV - 1,)), pltpu.SemaphoreType.DMA((N_DEV - 1,)),
              pltpu.SemaphoreType.DMA]
VMEM_SPEC = pl.BlockSpec(memory_space=pltpu.VMEM)


def _other_chips(x, y):
    return [(1 - x, y), (x, 1 - y), (1 - x, 1 - y)]


def _any_specs(n):
    return [pl.BlockSpec(memory_space=pl.ANY)] * n


def _halves(rows, c):
    half = rows // 2
    return pl.ds(c * half, half), pl.ds((1 - c) * half, half)


CAST_ROWS = 256


def _gather_copy(out_ref, send_sems, recv_sems, k, quarter, half, to, src=None):
    dst = out_ref.at[quarter, half]
    return pltpu.make_async_remote_copy(
        src_ref=dst if src is None else src, dst_ref=dst,
        send_sem=send_sems.at[k], recv_sem=recv_sems.at[k], device_id=to, device_id_type=MESH)


SMALL_IN_ROWS = 24


def allgather_weights(quarters, exchange, smalls):
    n = len(quarters)
    shapes = [w.shape for w in quarters]
    moved = [i for i in range(n) if exchange[i]]

    def body(*refs):
        w_refs, (gkb_ref, hnw_ref, gb_ref, gkw_ref) = refs[:n], refs[n:n + 4]
        out_refs, (bgk_ref, hw_ref, gbias_ref, wgk_ref) = refs[n + 4:2 * n + 4], refs[2 * n + 4:2 * n + 8]
        refs = refs[2 * n + 8:]
        f32_bufs, bf_bufs = refs[:n], refs[n:2 * n]
        send_sems, recv_sems, local_sems, small_ref, small_all_ref = refs[2 * n:2 * n + 5]
        small_ref[...] = jnp.zeros_like(small_ref)
        small_ref[0:1, :] = gkb_ref[...]
        small_ref[1:2, 0:64] = hnw_ref[...]
        small_ref[2:2 + GROUPS, 0:64] = gb_ref[...]
        small_ref[8:8 + GATE_RANK, :] = gkw_ref[...]
        start_small, wait_small = _gather_small(small_ref, small_all_ref, *refs[2 * n + 5:])
        start_small()
        x, y, c = _position()
        q = 2 * x + y
        sibling = (x, y, 1 - c)
        chips = _other_chips(x, y)

        def copy(k, i, quarter, half, to, src=None):
            return _gather_copy(out_refs[i], send_sems, recv_sems, k * n + i, quarter, half, to, src)

        loads = [pltpu.make_async_copy(w_refs[i], f32_bufs[i], local_sems.at[i]) for i in range(n)]
        for cp in loads:
            cp.start()
        keeps, sends = [], []
        for i in range(n):
            loads[i].wait()
            for r0 in range(0, shapes[i][0], CAST_ROWS):
                bf_bufs[i][r0:r0 + CAST_ROWS, :] = _bf(f32_bufs[i][r0:r0 + CAST_ROWS, :])
            keep = pltpu.make_async_copy(bf_bufs[i], out_refs[i].at[q], local_sems.at[n + i])
            keep.start()
            keeps.append(keep)
            if not exchange[i]:
                continue
            mine, _ = _halves(shapes[i][0], c)
            for j, chip in enumerate(chips):
                cp = copy(j, i, q, mine, (*chip, c), src=bf_bufs[i].at[mine])
                cp.start()
                sends.append(cp)
        for j, chip in enumerate(chips):
            qj = 2 * chip[0] + chip[1]
            for i in moved:
                mine, _ = _halves(shapes[i][0], c)
                copy(j, i, qj, mine, (x, y, c)).wait_recv()
                cp = copy(3 + j, i, qj, mine, sibling)
                cp.start()
                sends.append(cp)
        for j, chip in enumerate(chips):
            qj = 2 * chip[0] + chip[1]
            for i in moved:
                _, other = _halves(shapes[i][0], c)
                copy(3 + j, i, qj, other, (x, y, c)).wait_recv()
        wait_small()
        wgk_ref[...] = jnp.zeros_like(wgk_ref)
        for j in range(N_CHIPS):
            block = small_all_ref.at[2 * j]
            bgk_ref[:, 128 * j:128 * (j + 1)] = block[0:1, :]
            for h in range(HEADS):
                hw_ref[:, HEAD_V * h + 64 * j:HEAD_V * h + 64 * (j + 1)] = block[1:2, 0:64]
            for g in range(GROUPS):
                gbias_ref[:, GROUP_DIM * g + 64 * j:GROUP_DIM * g + 64 * (j + 1)] = block[2 + g:3 + g, 0:64]
            wgk_ref[0:GATE_RANK, 128 * j:128 * (j + 1)] = _bf(block[8:8 + GATE_RANK, :])
        for cp in sends:
            cp.wait_send()
        for cp in keeps:
            cp.wait()

    outs = pl.pallas_call(
        body, name="allgather_weights",
        out_shape=[jax.ShapeDtypeStruct((N_CHIPS, *s), BF16) for s in shapes]
                  + [jax.ShapeDtypeStruct((1, KEY_W), F32), jax.ShapeDtypeStruct((1, D), F32),
                     jax.ShapeDtypeStruct((1, D), F32), jax.ShapeDtypeStruct((RANK_PAD, KEY_W), BF16)],
        in_specs=_any_specs(n) + [VMEM_SPEC] * 4, out_specs=_any_specs(n) + [VMEM_SPEC] * 4,
        scratch_shapes=([pltpu.VMEM(s, F32) for s in shapes] + [pltpu.VMEM(s, BF16) for s in shapes]
                        + [pltpu.SemaphoreType.DMA((6 * n,)), pltpu.SemaphoreType.DMA((6 * n,)),
                           pltpu.SemaphoreType.DMA((2 * n,)), pltpu.VMEM((SMALL_IN_ROWS, 128), F32),
                           pltpu.VMEM((N_DEV, SMALL_IN_ROWS, 128), F32)] + SMALL_SEMS),
        compiler_params=pltpu.CompilerParams(vmem_limit_bytes=VMEM_LIMIT),
    )(*quarters, *smalls)
    return outs[:n], outs[n:]


def _scatter_copies(b_refs, got_refs, send_sems, recv_sems):
    n = len(b_refs)
    x, y, c = _position()
    copies = []
    for j, chip in enumerate(_other_chips(x, y)):
        qj = 2 * chip[0] + chip[1]
        for i in range(n):
            copies.append(pltpu.make_async_remote_copy(
                src_ref=b_refs[i].at[qj], dst_ref=got_refs[i].at[j],
                send_sem=send_sems.at[j * n + i], recv_sem=recv_sems.at[j * n + i],
                device_id=(*chip, c), device_id_type=MESH))
    return copies


def _scatter_shapes(chip_sums):
    return [jax.ShapeDtypeStruct((N_CHIPS - 1, *b.shape[1:]), BF16) for b in chip_sums]


ADD_ROWS = 512
ADD_HALVES_ROWS = 128


def _spans(counts):
    starts, total = [], 0
    for count in counts:
        starts.append(total)
        total += count
    return starts, total


def _local_step(t, start, count):
    return jnp.clip(t - start, 0, count - 1)


def add_halves(grads, place, name):
    n = len(grads)
    whole = [len(g.shape) == 2 for g in grads]
    halves = [g.shape[-2] // 2 for g in grads]
    cols = [GLA_IN_QUARTER if w else g.shape[-1] for g, w in zip(grads, whole)]
    rbs = [min(ADD_HALVES_ROWS, h) for h in halves]
    counts = [h // rb for h, rb in zip(halves, rbs)]
    starts, total = _spans(counts)
    half_shapes = [(*g.shape[:-2], h, g.shape[-1]) for g, h in zip(grads, halves)]

    def rows_of(ref, i, start):
        return ref.at[pl.ds(start, rbs[i])] if whole[i] else ref.at[:, pl.ds(start, rbs[i])]

    def body(place_ref, *refs):
        a_refs, o_refs = refs[:n], refs[n:2 * n]
        f_refs, h_refs = refs[2 * n:3 * n], refs[3 * n:4 * n]
        send_refs, their_refs, (send_sems, recv_sems) = refs[4 * n:5 * n], refs[5 * n:6 * n], refs[6 * n:]
        t = pl.program_id(0)
        q = place_ref[1]
        x, y, c = _position()
        copies = [[pltpu.make_async_remote_copy(
            src_ref=rows_of(send_refs[i], i, k * rbs[i]), dst_ref=rows_of(their_refs[i], i, k * rbs[i]),
            send_sem=send_sems.at[starts[i] + k], recv_sem=recv_sems.at[starts[i] + k],
            device_id=(x, y, 1 - c), device_id_type=MESH) for k in range(counts[i])] for i in range(n)]

        for i in range(n):
            for k in range(counts[i]):
                @pl.when(t == starts[i] + k)
                def _(i=i, k=k):
                    rows_of(send_refs[i], i, k * rbs[i])[...] = _bf(o_refs[i][...])
                    copies[i][k].start()

        for i in range(n):
            for k in range(counts[i]):
                @pl.when(t == starts[i] + k + 1)
                def _(i=i, k=k):
                    copies[i][k].wait_recv()
                    b_ref = rows_of(their_refs[i], i, k * rbs[i])
                    if not whole[i]:
                        h_refs[i][...] = _bf(a_refs[i][...] + b_ref[...].astype(F32))
                        f_refs[i][...] = a_refs[i][q] + b_ref[q].astype(F32)
                        return
                    total_i = a_refs[i][...] + b_ref[...].astype(F32)
                    for k4 in range(N_CHIPS):
                        piece = total_i[:, k4 * cols[i]:(k4 + 1) * cols[i]]
                        h_refs[i][k4] = _bf(piece)

                        @pl.when(q == k4)
                        def _():
                            f_refs[i][...] = piece

        @pl.when(t == total)
        def _():
            for of_matrix in copies:
                for cp in of_matrix:
                    cp.wait_send()

    def specs(i):
        sent = lambda t: _local_step(t, starts[i], counts[i])
        added = lambda t: _local_step(t - 1, starts[i], counts[i])
        by_quarter = (N_CHIPS, rbs[i], cols[i])
        block = (rbs[i], grads[i].shape[-1]) if whole[i] else by_quarter
        lead = () if whole[i] else (0,)
        mine = pl.BlockSpec(block, lambda t, place: (*lead, place[0] * counts[i] + added(t), 0))
        other = pl.BlockSpec(block, lambda t, place: (*lead, (1 - place[0]) * counts[i] + sent(t), 0))
        sums = pl.BlockSpec(by_quarter, lambda t, place: (0, added(t), 0))
        own = pl.BlockSpec(by_quarter[1:], lambda t, place: (added(t), 0))
        return mine, other, own, sums

    all_specs = [specs(i) for i in range(n)]
    outs = pl.pallas_call(
        body, name=name,
        grid_spec=pltpu.PrefetchScalarGridSpec(
            num_scalar_prefetch=1, grid=(total + 1,),
            in_specs=[sp[0] for sp in all_specs] + [sp[1] for sp in all_specs],
            out_specs=[sp[2] for sp in all_specs] + [sp[3] for sp in all_specs],
            scratch_shapes=[pltpu.VMEM(sh, BF16) for sh in half_shapes] + [pltpu.VMEM(sh, BF16) for sh in half_shapes]
                           + [pltpu.SemaphoreType.DMA((total,)), pltpu.SemaphoreType.DMA((total,))]),
        out_shape=[jax.ShapeDtypeStruct((h, cl), F32) for h, cl in zip(halves, cols)]
                  + [jax.ShapeDtypeStruct((N_CHIPS, h, cl), BF16) for h, cl in zip(halves, cols)],
        compiler_params=_params("arbitrary"),
    )(place, *grads, *grads)
    return list(zip(outs[:n], outs[n:]))


SMALL_SUM_ROWS = 16


def join_halves(grad, place, owns, gots, small_pool, small_gla, small_top, g_gk_pad):
    n = len(owns)
    half, cols = grad.shape[1] // 2, grad.shape[2]
    rb = min(ADD_HALVES_ROWS, half)
    sent = half // rb
    shapes = [g.shape for g in gots] + [(N_CHIPS - 1, half, cols)]
    rbs = [min(ADD_ROWS, sh[1]) for sh in shapes]
    counts = [sh[1] // r for sh, r in zip(shapes, rbs)]
    starts, joined = _spans(counts)
    first_join = sent + 1
    steps = first_join + joined

    def body(place_ref, *refs):
        refs = iter(refs)
        take = lambda count: [next(refs) for _ in range(count)]
        (a_ref, o_ref), o_refs, g_refs = take(2), take(n), take(n)
        pool_ref, gla_ref, top_ref, gk_ref = take(4)
        out_refs, (total_ref,) = take(n + 1), take(1)
        send_buf, their_buf, sums_buf, got_buf = take(4)
        sum_refs = take(n + 1)
        to_core_sems, from_core_sems, to_chip_sems, from_chip_sems, local_sems, send_sems, recv_sems = take(7)
        all_ref, small_ref = take(2)
        t = pl.program_id(0)
        q = place_ref[1]
        x, y, c = _position()
        start_small, wait_small = _gather_small(small_ref, all_ref, *refs)
        block = lambda k: pl.ds(k * rb, rb)

        def to_core(k):
            return pltpu.make_async_remote_copy(
                src_ref=send_buf.at[:, block(k)], dst_ref=their_buf.at[:, block(k)],
                send_sem=to_core_sems.at[k], recv_sem=from_core_sems.at[k],
                device_id=(x, y, 1 - c), device_id_type=MESH)

        def to_owners(k):
            return [pltpu.make_async_remote_copy(
                src_ref=sums_buf.at[2 * chip[0] + chip[1], block(k)], dst_ref=got_buf.at[j, block(k)],
                send_sem=to_chip_sems.at[3 * k + j], recv_sem=from_chip_sems.at[3 * k + j],
                device_id=(*chip, c), device_id_type=MESH) for j, chip in enumerate(_other_chips(x, y))]

        def copies(i, k):
            src = sum_refs[i].at[pl.ds(k * rbs[i], rbs[i])]
            rows = pl.ds(c * shapes[i][1] + k * rbs[i], rbs[i])
            return (pltpu.make_async_copy(src, out_refs[i].at[rows], local_sems.at[starts[i] + k]),
                    pltpu.make_async_remote_copy(
                        src_ref=src, dst_ref=out_refs[i].at[rows],
                        send_sem=send_sems.at[starts[i] + k], recv_sem=recv_sems.at[starts[i] + k],
                        device_id=(x, y, 1 - c), device_id_type=MESH))

        @pl.when(t == 0)
        def _():
            small_ref[0:3, :] = pool_ref[0:3, :]
            small_ref[3:5, :] = gla_ref[0:2, :]
            small_ref[5:8, :] = top_ref[0:3, :]
            for r in range(GATE_RANK):
                small_ref[8 + r // 2:9 + r // 2, (r % 2) * KEY_W:(r % 2 + 1) * KEY_W] = gk_ref[r:r + 1, :]
            start_small()

        for k in range(sent):
            @pl.when(t == k)
            def _(k=k):
                send_buf[:, k * rb:(k + 1) * rb, :] = _bf(o_ref[...])
                to_core(k).start()

        for k in range(sent):
            @pl.when(t == k + 1)
            def _(k=k):
                to_core(k).wait_recv()
                theirs = their_buf.at[:, block(k)]
                sums_buf[:, k * rb:(k + 1) * rb, :] = _bf(a_ref[...] + theirs[...].astype(F32))
                sum_refs[n][k * rb:(k + 1) * rb, :] = a_ref[q] + theirs[q].astype(F32)
                for cp in to_owners(k):
                    cp.start()

        for i in range(n + 1):
            for k in range(counts[i]):
                @pl.when(t == first_join + starts[i] + k)
                def _(i=i, k=k):
                    rows = slice(k * rbs[i], (k + 1) * rbs[i])
                    if i < n:
                        total_i = o_refs[i][...]
                        arrived = [g_refs[i][j] for j in range(N_CHIPS - 1)]
                    else:
                        if k == 0:
                            for kk in range(sent):
                                for cp in to_owners(kk):
                                    cp.wait_recv()
                        total_i = sum_refs[n][rows, :]
                        arrived = [got_buf[j, rows, :] for j in range(N_CHIPS - 1)]
                    for part in arrived:
                        total_i = total_i + part.astype(F32)
                    sum_refs[i][rows, :] = total_i
                    for cp in copies(i, k):
                        cp.start()

        @pl.when(t == steps - 1)
        def _():
            wait_small()
            small_total = all_ref[0]
            for dev in range(1, N_DEV):
                small_total = small_total + all_ref[dev]
            total_ref[...] = small_total
            for k in range(sent):
                to_core(k).wait_send()
                for cp in to_owners(k):
                    cp.wait_send()
            for i in range(n + 1):
                for k in range(counts[i]):
                    for cp in copies(i, k):
                        cp.wait()

    def specs(i):
        step = lambda t: _local_step(t - first_join, starts[i], counts[i])
        return (pl.BlockSpec((rbs[i], shapes[i][2]), lambda t, place: (step(t), 0)),
                pl.BlockSpec((N_CHIPS - 1, rbs[i], shapes[i][2]), lambda t, place: (0, step(t), 0)))

    by_quarter = (N_CHIPS, rb, cols)
    mine = pl.BlockSpec(by_quarter, lambda t, place: (0, place[0] * sent + jnp.clip(t - 1, 0, sent - 1), 0))
    other = pl.BlockSpec(by_quarter, lambda t, place: (0, (1 - place[0]) * sent + jnp.clip(t, 0, sent - 1), 0))
    all_specs = [specs(i) for i in range(n)]
    sems = lambda count: pltpu.SemaphoreType.DMA((count,))
    outs = pl.pallas_call(
        body, name="join_halves",
        grid_spec=pltpu.PrefetchScalarGridSpec(
            num_scalar_prefetch=1, grid=(steps,),
            in_specs=[mine, other] + [sp[0] for sp in all_specs] + [sp[1] for sp in all_specs] + [VMEM_SPEC] * 4,
            out_specs=_any_specs(n + 1) + [VMEM_SPEC],
            scratch_shapes=[pltpu.VMEM((N_CHIPS, half, cols), BF16) for _ in range(3)]
                           + [pltpu.VMEM(shapes[n], BF16)] + [pltpu.VMEM(sh[1:], F32) for sh in shapes]
                           + [sems(sent), sems(sent), sems(3 * sent), sems(3 * sent),
                              sems(joined), sems(joined), sems(joined),
                              pltpu.VMEM((N_DEV, SMALL_SUM_ROWS, D), F32), pltpu.VMEM((SMALL_SUM_ROWS, D), F32)]
                           + SMALL_SEMS),
        out_shape=[jax.ShapeDtypeStruct((2 * sh[1], sh[2]), F32) for sh in shapes]
                  + [jax.ShapeDtypeStruct((SMALL_SUM_ROWS, D), F32)],
        compiler_params=_params("arbitrary"),
    )(place, grad, grad, *owns, *gots, small_pool, small_gla, small_top, g_gk_pad)
    return [outs[n]] + list(outs[:n]), outs[n + 1]


def _adam_math(w, g, m, v):
    m = ADAM_B1 * m + (1.0 - ADAM_B1) * g
    v = ADAM_B2 * v + (1.0 - ADAM_B2) * (g * g)
    m_hat = m / (1.0 - ADAM_B1 ** ADAM_STEP)
    v_hat = v / (1.0 - ADAM_B2 ** ADAM_STEP)
    delta = -ADAM_LR * (m_hat / (jnp.sqrt(v_hat) + ADAM_EPS) + ADAM_WD * w)
    return delta, m, v


ADAM_BLOCK_BYTES = 2 ** 19
ADAM_MOST_STEPS = 8


def adamw(params, name, echo):
    n = len(params)
    shapes = [p[0].shape for p in params]
    first_out, _ = _spans([4 if e else 3 for e in echo])

    def tile_rows(shape):
        rows, cols = shape[0], shape[-1]
        aligned = 1 if len(shape) == 3 else 8
        divisors = [t for t in range(aligned, rows + 1, aligned) if rows % t == 0]
        tile = max(t for t in divisors if t * cols * 4 <= ADAM_BLOCK_BYTES)
        if rows // tile > ADAM_MOST_STEPS:
            tile = min(t for t in divisors if rows // t <= ADAM_MOST_STEPS)
        return tile

    tiles = [tile_rows(sh) for sh in shapes]
    counts = [sh[0] // tl for sh, tl in zip(shapes, tiles)]
    starts, total = _spans(counts)

    def body(*refs):
        ins, outs = refs[:4 * n], refs[4 * n:]
        t = pl.program_id(0)
        for i in range(n):
            @pl.when((t >= starts[i]) & (t < starts[i] + counts[i]))
            def _(i=i):
                w_ref, g_ref, m_ref, v_ref = ins[4 * i:4 * i + 4]
                g = g_ref[...]
                d, nm, nv = _adam_math(w_ref[...], g, m_ref[...], v_ref[...])
                outs[first_out[i]][...] = d
                outs[first_out[i] + 1][...] = nm
                outs[first_out[i] + 2][...] = nv
                if echo[i]:
                    outs[first_out[i] + 3][...] = g

    def spec(i):
        block = (tiles[i],) + shapes[i][1:]
        zeros = (0,) * (len(block) - 1)
        return pl.BlockSpec(block, lambda t: (_local_step(t, starts[i], counts[i]),) + zeros)

    outs = pl.pallas_call(
        body, name=name, grid=(total,),
        out_shape=[jax.ShapeDtypeStruct(sh, F32) for sh, e in zip(shapes, echo) for _ in range(4 if e else 3)],
        in_specs=[spec(i) for i in range(n) for _ in range(4)],
        out_specs=[spec(i) for i in range(n) for _ in range(4 if echo[i] else 3)],
        compiler_params=_params("arbitrary"),
    )(*[a for p in params for a in p])
    return [tuple(outs[first_out[i]:first_out[i] + (4 if echo[i] else 3)]) for i in range(n)]


def adamw_small(params, total, place):
    n = len(params)

    def cut_gradients(total_ref, q, g_refs):
        g_norm, g_group_b, g_scale, g_gk_w, g_gk_b, g_head_norm, g_final = g_refs
        g_norm[0:1, :] = total_ref[0:1, :]
        g_norm[1:2, :] = total_ref[3:4, :]
        g_scale[...] = total_ref[1:2, :]
        g_final[...] = total_ref[5:6, :]
        g_gk_b[...] = total_ref[4:5, pl.ds(pl.multiple_of(q * 128, 128), 128)]
        for r in range(GATE_RANK):
            lanes = pl.ds(pl.multiple_of((r % 2) * KEY_W + q * 128, 128), 128)
            g_gk_w[r:r + 1, :] = total_ref[8 + r // 2:9 + r // 2, lanes]
        for k in range(N_CHIPS):
            @pl.when(q == k)
            def _(k=k):
                g_head_norm[...] = total_ref[6:7, 64 * k:64 * (k + 1)]
                for g in range(GROUPS):
                    g_group_b[g:g + 1, :] = total_ref[2:3, GROUP_DIM * g + 64 * k:GROUP_DIM * g + 64 * (k + 1)]

    def body(place_ref, total_ref, *refs):
        ins, outs = refs[:3 * n], refs[3 * n:]
        outs[4 * n][...] = total_ref[7:8, 0:1]
        cut_gradients(total_ref, place_ref[1], outs[0:4 * n:4])
        for k in range(n):
            w_ref, m_ref, v_ref = ins[3 * k:3 * k + 3]
            d, nm, nv = _adam_math(w_ref[...], outs[4 * k][...], m_ref[...], v_ref[...])
            outs[4 * k + 1][...] = d
            outs[4 * k + 2][...] = nm
            outs[4 * k + 3][...] = nv

    flat = [a for p in params for a in p]
    outs = pl.pallas_call(
        body, name="adamw_small",
        out_shape=[jax.ShapeDtypeStruct(p[0].shape, F32) for p in params for _ in range(4)]
                  + [jax.ShapeDtypeStruct((1, 1), F32)],
        in_specs=[pl.BlockSpec(memory_space=pltpu.SMEM)] + [VMEM_SPEC] * (1 + 3 * n),
        out_specs=[VMEM_SPEC] * (4 * n + 1),
    )(place, total, *flat)
    return [tuple(outs[4 * k:4 * k + 4]) for k in range(n)], outs[4 * n]


def matmul_tn(a, b, name, tile_n, place, by_column_tile=False, also_reduce=()):
    s, m = a.shape
    n = b.shape[1]
    steps = n // tile_n
    n_red = len(also_reduce)
    assert n_red == 0 or steps >= 2
    halves = [(g.shape[1] // 2, g.shape[2]) for g in also_reduce]
    if by_column_tile:
        out_shape = jax.ShapeDtypeStruct((steps, m, tile_n), F32)
        out_spec = pl.BlockSpec((None, m, tile_n), lambda j, place: (j, 0, 0))
    else:
        out_shape = jax.ShapeDtypeStruct((m, n), F32)
        out_spec = pl.BlockSpec((m, tile_n), lambda j, place: (0, j))

    def body(place_ref, a_ref, b_ref, *rest):
        rest = iter(rest)
        take = lambda count: [next(rest) for _ in range(count)]
        mine_refs, other_refs, (out_ref,) = take(n_red), take(n_red), take(1)
        own_refs, got_refs = take(n_red), take(n_red)
        send_bufs, their_bufs, sums_bufs = take(n_red), take(n_red), take(n_red)
        sems = list(rest)
        j = pl.program_id(0)
        q = place_ref[1]
        x, y, c = _position()

        def to_core(i):
            return pltpu.make_async_remote_copy(
                src_ref=send_bufs[i], dst_ref=their_bufs[i], send_sem=sems[0].at[i], recv_sem=sems[1].at[i],
                device_id=(x, y, 1 - c), device_id_type=MESH)

        def to_owners(i):
            return [pltpu.make_async_remote_copy(
                src_ref=sums_bufs[i].at[2 * chip[0] + chip[1]], dst_ref=got_refs[i].at[k],
                send_sem=sems[2].at[3 * i + k], recv_sem=sems[3].at[3 * i + k],
                device_id=(*chip, c), device_id_type=MESH) for k, chip in enumerate(_other_chips(x, y))]

        if n_red:
            @pl.when(j == 0)
            def _():
                for i in range(n_red):
                    send_bufs[i][...] = _bf(other_refs[i][...])
                    to_core(i).start()

            @pl.when(j == 1)
            def _():
                for i in range(n_red):
                    to_core(i).wait_recv()
                    sums_bufs[i][...] = _bf(mine_refs[i][...] + their_bufs[i][...].astype(F32))
                    own_refs[i][...] = mine_refs[i][q] + their_bufs[i][q].astype(F32)
                    for cp in to_owners(i):
                        cp.start()

        out_ref[...] = _tn(a_ref[...], b_ref[...])

        if n_red:
            @pl.when(j == steps - 1)
            def _():
                for i in range(n_red):
                    to_core(i).wait_send()
                    for cp in to_owners(i):
                        cp.wait()

    by_quarter = [(N_CHIPS, *h) for h in halves]
    outs = pl.pallas_call(
        body, name=name,
        grid_spec=pltpu.PrefetchScalarGridSpec(
            num_scalar_prefetch=1, grid=(steps,),
            in_specs=[pl.BlockSpec((s, m), lambda j, place: (0, 0)), pl.BlockSpec((s, tile_n), lambda j, place: (0, j))]
                     + [pl.BlockSpec(sh, lambda j, place: (0, place[0], 0)) for sh in by_quarter]
                     + [pl.BlockSpec(sh, lambda j, place: (0, 1 - place[0], 0)) for sh in by_quarter],
            out_specs=[out_spec] + [pl.BlockSpec(h, lambda j, place: (0, 0)) for h in halves] + _any_specs(n_red),
            scratch_shapes=[pltpu.VMEM(sh, BF16) for sh in by_quarter * 3]
                           + ([pltpu.SemaphoreType.DMA((n_red,)), pltpu.SemaphoreType.DMA((n_red,)),
                               pltpu.SemaphoreType.DMA((3 * n_red,)), pltpu.SemaphoreType.DMA((3 * n_red,))]
                              if n_red else [])),
        out_shape=[out_shape] + [jax.ShapeDtypeStruct(h, F32) for h in halves]
                  + [jax.ShapeDtypeStruct((N_CHIPS - 1, *h), BF16) for h in halves],
        compiler_params=_params("arbitrary"),
    )(place, a, b, *also_reduce, *also_reduce)
    return outs[0], list(zip(outs[1:1 + n_red], outs[1 + n_red:]))


ROW_TILE = 512


def _row_index(tile, rows):
    return tile * rows + lax.broadcasted_iota(jnp.int32, (rows, 1), 0)


def _inverse_counts(t_glob):
    return [1.0 / jnp.minimum(t_glob + 1, w).astype(F32) for w in POOL_WINDOWS]


def _sigmoid(z):
    return 1.0 / (1.0 + jnp.exp(-z))


def _trailing_sums(src, tmp, cols, window, rows):
    bufs = (src, tmp)
    span, level, start = 1, 0, 0
    while span < window:
        start += 8
        a, b = bufs[level % 2], bufs[(level + 1) % 2]
        n = HALO + rows - start
        b[start:start + n, cols] = a[start:start + n, cols] + a[start - span:start - span + n, cols]
        span, level = 2 * span, level + 1
    return bufs[level % 2][HALO:HALO + rows, cols]


def _leading_sums(src, tmp, cols, window, rows):
    bufs = (src, tmp)
    span, level, n = 1, 0, rows + HALO
    while span < window:
        n -= 8
        a, b = bufs[level % 2], bufs[(level + 1) % 2]
        b[0:n, cols] = a[0:n, cols] + a[span:span + n, cols]
        span, level = 2 * span, level + 1
    return bufs[level % 2][0:rows, cols]


def gather_in_background(step, last, out_refs, send_sems, recv_sems, finish):
    n = len(out_refs)
    x, y, c = _position()
    q = 2 * x + y
    chips = _other_chips(x, y)

    def copy(k, i, quarter, half, to):
        return _gather_copy(out_refs[i], send_sems, recv_sems, k * n + i, quarter, half, to)

    if not finish:
        @pl.when(step == 0)
        def _():
            for i in range(n):
                mine, _ = _halves(out_refs[i].shape[1], c)
                for j, chip in enumerate(chips):
                    copy(j, i, q, mine, (*chip, c)).start()

        @pl.when(step == last)
        def _():
            for j, chip in enumerate(chips):
                qj = 2 * chip[0] + chip[1]
                for i in range(n):
                    mine, _ = _halves(out_refs[i].shape[1], c)
                    copy(j, i, qj, mine, (x, y, c)).wait_recv()
                    copy(3 + j, i, qj, mine, (x, y, 1 - c)).start()
        return

    @pl.when(step == last)
    def _():
        for j, chip in enumerate(chips):
            qj = 2 * chip[0] + chip[1]
            for i in range(n):
                mine, other = _halves(out_refs[i].shape[1], c)
                copy(3 + j, i, qj, other, (x, y, c)).wait_recv()
                copy(j, i, q, mine, (x, y, c)).wait_send()
                copy(3 + j, i, qj, mine, (x, y, c)).wait_send()


def _group_matrix(gw_ref, g):
    rows = GROUP_DIM // N_CHIPS
    return jnp.concatenate([gw_ref[j, rows * g:rows * (g + 1), :] for j in range(N_CHIPS)], axis=0)


def pool_forward(x, w0, wpi, gw, gb, scale, wpo, later):
    s = x.shape[0]
    ts = ROW_TILE
    nt = s // ts
    assert nt >= 2
    n_later = len(later)

    def body(x_ref, w0_ref, wpi_ref, gw_ref, gb_ref, sc_ref, wpo_ref, *rest):
        rest = rest[n_later:]
        h1_ref, pooled_ref, gt_ref, n0_ref = rest[:4]
        later_refs = rest[4:4 + n_later]
        ubuf, tbuf, hist, send_sems, recv_sems = rest[4 + n_later:]
        i = pl.program_id(0)
        gather_in_background(i, nt - 1, later_refs, send_sems, recv_sems, finish=False)
        xv = x_ref[...]
        r = lax.rsqrt(jnp.mean(xv * xv, axis=-1, keepdims=True) + EPS)
        n0 = _bf(xv * r * w0_ref[...])
        n0_ref[...] = n0
        u = jnp.concatenate([_nn(n0, wpi_ref[0]), _nn(n0, wpi_ref[1])], axis=-1)
        gt = jnp.concatenate([_nn(n0, wpi_ref[2]), _nn(n0, wpi_ref[3])], axis=-1)
        gt_ref[...] = gt

        @pl.when(i == 0)
        def _():
            hist[...] = jnp.zeros_like(hist)

        ubuf[0:HALO, :] = hist[...]
        ubuf[HALO:HALO + ts, :] = u
        hist[...] = u[ts - HALO:, :]
        inv = _inverse_counts(_row_index(i, ts))
        mixed = []
        for g, w in enumerate(POOL_WINDOWS):
            cols = slice(g * GROUP_DIM, (g + 1) * GROUP_DIM)
            pooled = _bf(_trailing_sums(ubuf, tbuf, cols, w, ts) * inv[g] - u[:, cols])
            pooled_ref[:, cols] = pooled
            mixed.append(_nn(pooled, _group_matrix(gw_ref, g)))
        mixed = jnp.concatenate(mixed, axis=-1) + gb_ref[...]
        y = mixed * sc_ref[...] * (gt * _sigmoid(gt))
        h1_ref[...] = xv + _nn(_bf(y), wpo_ref[...])
        gather_in_background(i, nt - 1, later_refs, send_sems, recv_sems, finish=True)

    row = lambda cols: pl.BlockSpec((ts, cols), lambda i: (i, 0))
    outs = pl.pallas_call(
        body, name="pool_forward", grid=(nt,),
        out_shape=[jax.ShapeDtypeStruct((s, D), F32), jax.ShapeDtypeStruct((s, D), BF16),
                   jax.ShapeDtypeStruct((s, D), F32), jax.ShapeDtypeStruct((s, D), BF16)]
                  + [jax.ShapeDtypeStruct(a.shape, a.dtype) for a in later],
        in_specs=[row(D), _full((1, D)), _full((N_CHIPS, D, D // 2)), _full((GROUPS, GROUP_DIM, GROUP_DIM)),
                  _full((1, D)), _full((1, D)), _full((D, D))] + _any_specs(n_later),
        out_specs=[row(D), row(D), row(D), row(D)] + _any_specs(n_later),
        input_output_aliases={7 + k: 4 + k for k in range(n_later)},
        scratch_shapes=[pltpu.VMEM((HALO + ts, D), F32), pltpu.VMEM((HALO + ts, D), F32),
                        pltpu.VMEM((HALO, D), F32),
                        pltpu.SemaphoreType.DMA((6 * n_later,)), pltpu.SemaphoreType.DMA((6 * n_later,))],
        compiler_params=_params("arbitrary"),
    )(x, w0, wpi, gw, gb, scale, wpo, *later)
    return outs[:4], outs[4:]


def pool_backward(x, dh1, pooled, gt, w0, wpi, gw, gb, scale, wpo, chip_sums):
    s = x.shape[0]
    ts = ROW_TILE
    nt = s // ts
    n_sums = len(chip_sums)

    def body(x_ref, dh1_ref, pooled_ref, gt_ref, w0_ref, wpi_ref, gw_ref, gb_ref, sc_ref, wpo_ref, *rest):
        sum_refs, rest = rest[:n_sums], rest[n_sums:]
        dx_ref, dproj_ref, gpo_ref, ggw_ref, small_ref = rest[:5]
        got_refs = rest[5:5 + n_sums]
        ebuf, tbuf, ahead, send_sems, recv_sems = rest[5 + n_sums:]
        i = pl.program_id(0)
        copies = _scatter_copies(sum_refs, got_refs, send_sems, recv_sems)

        @pl.when(i == 0)
        def _():
            for cp in copies:
                cp.start()

        @pl.when(i == 0)
        def _():
            gpo_ref[...] = jnp.zeros_like(gpo_ref)
            ggw_ref[...] = jnp.zeros_like(ggw_ref)
            small_ref[...] = jnp.zeros_like(small_ref)
            ahead[...] = jnp.zeros_like(ahead)

        dh1 = dh1_ref[...]
        dh1_bf = _bf(dh1)
        gt = gt_ref[...]
        sc = sc_ref[...]
        dy = _nt(dh1_bf, wpo_ref[...])
        pooled_bf = []
        mixed = []
        for g in range(GROUPS):
            cols = slice(g * GROUP_DIM, (g + 1) * GROUP_DIM)
            pb = pooled_ref[:, cols]
            pooled_bf.append(pb)
            mixed.append(_nn(pb, _group_matrix(gw_ref, g)))
        mixed = jnp.concatenate(mixed, axis=-1) + gb_ref[...]
        sg = _sigmoid(gt)
        silu = gt * sg
        gpo_ref[...] += _tn(_bf(mixed * sc * silu), dh1_bf)
        dmixed = dy * sc * silu
        dgt = dy * mixed * sc * (sg * (1.0 + gt * (1.0 - sg)))
        dproj_ref[:, D:] = _bf(dgt)
        small_ref[1:2, :] += jnp.sum(dy * mixed * silu, axis=0, keepdims=True)
        small_ref[2:3, :] += jnp.sum(dmixed, axis=0, keepdims=True)

        inv = _inverse_counts(_row_index(nt - 1 - i, ts))
        rows_q = GROUP_DIM // N_CHIPS
        ebuf[ts:ts + HALO, :] = ahead[...]
        dpooled = []
        for g in range(GROUPS):
            cols = slice(g * GROUP_DIM, (g + 1) * GROUP_DIM)
            dm = _bf(dmixed[:, cols])
            ggw = _tn(pooled_bf[g], dm)
            for j in range(N_CHIPS):
                ggw_ref[j, rows_q * g:rows_q * (g + 1), :] += ggw[rows_q * j:rows_q * (j + 1), :]
            dp = _nt(dm, _group_matrix(gw_ref, g))
            dpooled.append(dp)
            ebuf[0:ts, cols] = dp * inv[g]
        ahead[...] = ebuf[0:HALO, :]
        du = []
        for g, w in enumerate(POOL_WINDOWS):
            cols = slice(g * GROUP_DIM, (g + 1) * GROUP_DIM)
            du.append(_leading_sums(ebuf, tbuf, cols, w, ts) - dpooled[g])
        du = _bf(jnp.concatenate(du, axis=-1))
        dproj_ref[:, :D] = du
        dgt_bf = _bf(dgt)
        half = D // 2
        dn0 = (_nt(du[:, :half], wpi_ref[0]) + _nt(du[:, half:], wpi_ref[1])
               + _nt(dgt_bf[:, :half], wpi_ref[2]) + _nt(dgt_bf[:, half:], wpi_ref[3]))

        xv = x_ref[...]
        r = lax.rsqrt(jnp.mean(xv * xv, axis=-1, keepdims=True) + EPS)
        xhat = xv * r
        small_ref[0:1, :] += jnp.sum(dn0 * xhat, axis=0, keepdims=True)
        dxh = dn0 * w0_ref[...]
        dx_ref[...] = dh1 + r * (dxh - xhat * jnp.mean(dxh * xhat, axis=-1, keepdims=True))

        @pl.when(i == nt - 1)
        def _():
            for cp in copies:
                cp.wait()

    row = lambda cols: pl.BlockSpec((ts, cols), lambda i: (nt - 1 - i, 0))
    outs = pl.pallas_call(
        body, name="pool_backward", grid=(nt,),
        out_shape=[jax.ShapeDtypeStruct((s, D), F32), jax.ShapeDtypeStruct((s, 2 * D), BF16),
                   jax.ShapeDtypeStruct((D, D), F32),
                   jax.ShapeDtypeStruct((GROUPS, GROUP_DIM, GROUP_DIM), F32),
                   jax.ShapeDtypeStruct((8, D), F32)] + _scatter_shapes(chip_sums),
        in_specs=[row(D), row(D), row(D), row(D), _full((1, D)), _full((N_CHIPS, D, D // 2)),
                  _full((GROUPS, GROUP_DIM, GROUP_DIM)), _full((1, D)), _full((1, D)), _full((D, D))]
                 + _any_specs(n_sums),
        out_specs=[row(D), row(2 * D), _full((D, D)), _full((GROUPS, GROUP_DIM, GROUP_DIM)), _full((8, D))]
                  + _any_specs(n_sums),
        scratch_shapes=[pltpu.VMEM((ts + HALO, D), F32), pltpu.VMEM((ts + HALO, D), F32),
                        pltpu.VMEM((HALO, D), F32),
                        pltpu.SemaphoreType.DMA((3 * n_sums,)), pltpu.SemaphoreType.DMA((3 * n_sums,))],
        compiler_params=_params("arbitrary"),
    )(x, dh1, pooled, gt, w0, wpi, gw, gb, scale, wpo, *chip_sums)
    return outs[:5], outs[5:]


def gla_project(h1, w1, wgi_q, wgk, bgk, later):
    s = h1.shape[0]
    ts = ROW_TILE
    nt = s // ts
    assert nt >= 2
    n_later = len(later)

    def body(h_ref, w1_ref, wq_ref, wgk_ref, bgk_ref, *rest):
        rest = rest[n_later:]
        qk_ref, v_ref, gate_ref, low_ref, cum_ref, n1_ref, wgi_ref = rest[:7]
        later_refs = rest[7:7 + n_later]
        send_sems, recv_sems = rest[7 + n_later:]
        gather_in_background(pl.program_id(0), nt - 1, later_refs, send_sems, recv_sems, finish=False)

        @pl.when(pl.program_id(0) == 0)
        def _():
            _assemble_gla_in(wq_ref, wgi_ref)

        hv = h_ref[...]
        r = lax.rsqrt(jnp.mean(hv * hv, axis=-1, keepdims=True) + EPS)
        n1 = _bf(hv * r * w1_ref[...])
        n1_ref[...] = n1
        qk_ref[...] = _nn(n1, wgi_ref[:, 0:2 * KEY_W])
        v_ref[...] = _bf(_nn(n1, wgi_ref[:, 2 * KEY_W:2 * KEY_W + D]))
        gate_ref[...] = _nn(n1, wgi_ref[:, 2 * KEY_W + D:GLA_MAIN])
        low = _bf(_nn(n1, wgi_ref[:, GLA_MAIN:]))
        low_ref[...] = low
        z = _nn(low, wgk_ref[...]) + bgk_ref[...]
        lg = (jnp.minimum(z, 0.0) - jnp.log(1.0 + jnp.exp(-jnp.abs(z)))) / GATE_NORM
        lower_f = _chunk_masks()[0].astype(F32)
        for r0 in range(0, ts, CHUNK):
            cum_ref[r0:r0 + CHUNK, :] = _nn_exact(lower_f, lg[r0:r0 + CHUNK, :])
        gather_in_background(pl.program_id(0), nt - 1, later_refs, send_sems, recv_sems, finish=True)

    row = lambda cols: pl.BlockSpec((ts, cols), lambda i: (i, 0))
    outs = pl.pallas_call(
        body, name="gla_project", grid=(nt,),
        out_shape=[jax.ShapeDtypeStruct((s, D), F32), jax.ShapeDtypeStruct((s, D), BF16),
                   jax.ShapeDtypeStruct((s, D), F32), jax.ShapeDtypeStruct((s, RANK_PAD), BF16),
                   jax.ShapeDtypeStruct((s, KEY_W), F32), jax.ShapeDtypeStruct((s, D), BF16),
                   jax.ShapeDtypeStruct((D, GLA_MAIN + RANK_PAD), BF16)]
                  + [jax.ShapeDtypeStruct(a.shape, a.dtype) for a in later],
        in_specs=[row(D), _full((1, D)), _full((N_CHIPS, D, GLA_IN_QUARTER)),
                  _full((RANK_PAD, KEY_W)), _full((1, KEY_W))] + _any_specs(n_later),
        out_specs=[row(D), row(D), row(D), row(RANK_PAD), row(KEY_W), row(D), _full((D, GLA_MAIN + RANK_PAD))]
                  + _any_specs(n_later),
        input_output_aliases={5 + k: 7 + k for k in range(n_later)},
        scratch_shapes=[pltpu.SemaphoreType.DMA((6 * n_later,)), pltpu.SemaphoreType.DMA((6 * n_later,))],
        compiler_params=_params("arbitrary"),
    )(h1, w1, wgi_q, wgk, bgk, *later)
    return outs[:7], outs[7:]


def _assemble_gla_in(wq_ref, wfull):
    pad = jnp.zeros((CAST_ROWS, GLA_MAIN + RANK_PAD - GLA_IN), BF16)
    for r0 in range(0, D, CAST_ROWS):
        rows = slice(r0, r0 + CAST_ROWS)
        wfull[rows, :] = jnp.concatenate([wq_ref[q, rows, :] for q in range(N_CHIPS)] + [pad], axis=1)


GLA_BLOCK = 512
CHUNKS_PER_BLOCK = GLA_BLOCK // CHUNK


def _chunk_masks():
    t = lax.broadcasted_iota(jnp.int32, (CHUNK, CHUNK), 0)
    u = lax.broadcasted_iota(jnp.int32, (CHUNK, CHUNK), 1)
    return t >= u, t <= u


def _gla_chunk_terms(q, cum):
    ep = jnp.exp(cum)
    en = jnp.exp(-cum)
    qs = q * (HEAD_K ** -0.5)
    last = cum[CHUNK - 1:CHUNK, :]
    ed = jnp.exp(last - cum)
    dec = jnp.exp(last)
    return ep, en, qs, ed, dec


def gla_forward(qk, v, cum):
    s = qk.shape[0]
    nb = s // GLA_BLOCK
    nc = s // CHUNK

    def body(q_ref, k_ref, v_ref, cum_ref, o_ref, st_ref, sc_ref, state):
        @pl.when(pl.program_id(0) == 0)
        def _():
            state[...] = jnp.zeros_like(state)

        lower, _ = _chunk_masks()

        def chunk(cc, carry):
            rows = pl.ds(pl.multiple_of(cc * CHUNK, CHUNK), CHUNK)
            for h in range(HEADS):
                kc = slice(h * HEAD_K, (h + 1) * HEAD_K)
                vc = slice(h * HEAD_V, (h + 1) * HEAD_V)
                q = q_ref[rows, kc]
                k = k_ref[rows, kc]
                v = v_ref[rows, vc]
                ep, en, qs, ed, dec = _gla_chunk_terms(q, cum_ref[rows, kc])
                a = _bf(qs * ep)
                fwd = _nt(a, _bf(k * en))
                bwd = _nt(_bf(qs * en), _bf(k * ep))
                scores = _bf(jnp.where(lower, fwd, bwd))
                sc_ref[rows, h * CHUNK:(h + 1) * CHUNK] = scores
                st = state[h]
                st_ref[cc, h] = st
                o_ref[rows, vc] = _nn(scores, v) + _nt(a, _bf(st))
                state[h] = st * dec + _tn(v, _bf(k * ed))
            return carry

        lax.fori_loop(0, CHUNKS_PER_BLOCK, chunk, 0, unroll=True)

    return pl.pallas_call(
        body, name="gla_forward", grid=(nb,),
        out_shape=(jax.ShapeDtypeStruct((s, D), F32),
                   jax.ShapeDtypeStruct((nc, HEADS, HEAD_V, HEAD_K), F32),
                   jax.ShapeDtypeStruct((s, HEADS * CHUNK), BF16)),
        in_specs=[pl.BlockSpec((GLA_BLOCK, KEY_W), lambda i: (i, 0)),
                  pl.BlockSpec((GLA_BLOCK, KEY_W), lambda i: (i, 1)),
                  pl.BlockSpec((GLA_BLOCK, D), lambda i: (i, 0)),
                  pl.BlockSpec((GLA_BLOCK, KEY_W), lambda i: (i, 0))],
        out_specs=(pl.BlockSpec((GLA_BLOCK, D), lambda i: (i, 0)),
                   pl.BlockSpec((CHUNKS_PER_BLOCK, HEADS, HEAD_V, HEAD_K), lambda i: (i, 0, 0, 0)),
                   pl.BlockSpec((GLA_BLOCK, HEADS * CHUNK), lambda i: (i, 0))),
        scratch_shapes=[pltpu.VMEM((HEADS, HEAD_V, HEAD_K), F32)],
        compiler_params=_params("arbitrary"),
    )(qk, qk, v, cum)


def gla_backward(qk, v, cum, do, states, scores):
    s = qk.shape[0]
    nb = s // GLA_BLOCK

    def body(q_ref, k_ref, v_ref, cum_ref, do_ref, st_ref, sc_ref, dq_ref, dk_ref, dv_ref, dcum_ref, dstate):
        @pl.when(pl.program_id(0) == 0)
        def _():
            dstate[...] = jnp.zeros_like(dstate)

        lower, _ = _chunk_masks()
        is_last = lax.broadcasted_iota(jnp.int32, (CHUNK, HEAD_K), 0) == CHUNK - 1

        def chunk(step, carry):
            cc = CHUNKS_PER_BLOCK - 1 - step
            rows = pl.ds(pl.multiple_of(cc * CHUNK, CHUNK), CHUNK)
            for h in range(HEADS):
                kc = slice(h * HEAD_K, (h + 1) * HEAD_K)
                vc = slice(h * HEAD_V, (h + 1) * HEAD_V)
                q = q_ref[rows, kc]
                k = k_ref[rows, kc]
                v = v_ref[rows, vc]
                do_c = do_ref[rows, vc]
                ep, en, qs, ed, dec = _gla_chunk_terms(q, cum_ref[rows, kc])
                a = _bf(qs * ep)
                b = _bf(k * en)
                c = _bf(qs * en)
                dk_dec = _bf(k * ep)
                kd = _bf(k * ed)
                scores = sc_ref[rows, h * CHUNK:(h + 1) * CHUNK]
                st = st_ref[cc, h]
                dst = dstate[h]
                dst_bf = _bf(dst)

                dscores = _nt(do_c, v)
                dfwd = _bf(jnp.where(lower, dscores, 0.0))
                dbwd = _bf(jnp.where(lower, 0.0, dscores))
                dv_ref[rows, vc] = _bf(_tn(scores, do_c) + _nt(kd, dst_bf))
                da = _nn(dfwd, b) + _nn(do_c, _bf(st))
                db = _tn(dfwd, a)
                dc = _nn(dbwd, dk_dec)
                ddk = _tn(dbwd, c)
                dkd = _nn(v, dst_bf)
                ddec = jnp.sum(dst * st, axis=0, keepdims=True)
                dstate[h] = dst * dec + _tn(do_c, a)

                m = dkd * k * ed
                dq_ref[rows, kc] = _bf((da * ep + dc * en) * (HEAD_K ** -0.5))
                dk_ref[rows, kc] = _bf(db * en + ddk * ep + dkd * ed)
                dcum = (da * qs + ddk * k) * ep - (db * k + dc * qs) * en - m
                dlast = jnp.sum(m, axis=0, keepdims=True) + ddec * dec
                dcum_ref[rows, kc] = dcum + jnp.where(is_last, dlast, 0.0)
            return carry

        lax.fori_loop(0, CHUNKS_PER_BLOCK, chunk, 0, unroll=True)

    rev = lambda cols, col_block: pl.BlockSpec((GLA_BLOCK, cols), lambda i: (nb - 1 - i, col_block))
    return pl.pallas_call(
        body, name="gla_backward", grid=(nb,),
        out_shape=(jax.ShapeDtypeStruct((s, KEY_W), BF16), jax.ShapeDtypeStruct((s, KEY_W), BF16),
                   jax.ShapeDtypeStruct((s, D), BF16), jax.ShapeDtypeStruct((s, KEY_W), F32)),
        in_specs=[rev(KEY_W, 0), rev(KEY_W, 1), rev(D, 0), rev(KEY_W, 0), rev(D, 0),
                  pl.BlockSpec((CHUNKS_PER_BLOCK, HEADS, HEAD_V, HEAD_K), lambda i: (nb - 1 - i, 0, 0, 0)),
                  rev(HEADS * CHUNK, 0)],
        out_specs=(rev(KEY_W, 0), rev(KEY_W, 0), rev(D, 0), rev(KEY_W, 0)),
        scratch_shapes=[pltpu.VMEM((HEADS, HEAD_V, HEAD_K), F32)],
        compiler_params=_params("arbitrary"),
    )(qk, qk, v, cum, do, states, scores)


def head_and_loss(o, gate, h1, target, hw, wgo, wf):
    s = o.shape[0]
    ts = ROW_TILE

    def body(o_ref, gate_ref, h1_ref, tgt_ref, hw_ref, wgo_ref, wf_ref,
             dh2_ref, do_ref, dgate_ref, ggo_ref, small_ref):
        @pl.when(pl.program_id(0) == 0)
        def _():
            ggo_ref[...] = jnp.zeros_like(ggo_ref)
            small_ref[...] = jnp.zeros_like(small_ref)

        gate = gate_ref[...]
        hw = hw_ref[...]
        sg = _sigmoid(gate)
        silu = gate * sg
        ohat, ro = [], []
        for h in range(HEADS):
            oh = o_ref[:, h * HEAD_V:(h + 1) * HEAD_V]
            rh = lax.rsqrt(jnp.mean(oh * oh, axis=-1, keepdims=True) + EPS)
            ro.append(rh)
            ohat.append(oh * rh)
        ohat = jnp.concatenate(ohat, axis=-1)
        on = ohat * hw
        y2 = _bf(on * silu)
        h2 = h1_ref[...] + _nn(y2, wgo_ref[...])
        rf = lax.rsqrt(jnp.mean(h2 * h2, axis=-1, keepdims=True) + EPS)
        h2hat = h2 * rf
        wf = wf_ref[...]
        diff = h2hat * wf - tgt_ref[...]
        small_ref[2:3, :] += jnp.zeros((1, D), F32) + 0.5 * jnp.sum(diff * diff) / D
        dout = diff / D
        small_ref[0:1, :] += jnp.sum(dout * h2hat, axis=0, keepdims=True)
        dxh = dout * wf
        dh2 = rf * (dxh - h2hat * jnp.mean(dxh * h2hat, axis=-1, keepdims=True))
        dh2_ref[...] = dh2
        dh2_bf = _bf(dh2)
        ggo_ref[...] += _tn(y2, dh2_bf)
        dy2 = _nt(dh2_bf, wgo_ref[...])
        don = dy2 * silu
        dgate_ref[...] = _bf(dy2 * on * (sg * (1.0 + gate * (1.0 - sg))))
        ghw = jnp.sum(don * ohat, axis=0, keepdims=True)
        small_ref[1:2, 0:HEAD_V] += sum(ghw[:, h * HEAD_V:(h + 1) * HEAD_V] for h in range(HEADS))
        dohat = don * hw
        for h in range(HEADS):
            cols = slice(h * HEAD_V, (h + 1) * HEAD_V)
            oh, dh = ohat[:, cols], dohat[:, cols]
            do_ref[:, cols] = _bf(ro[h] * (dh - oh * jnp.mean(dh * oh, axis=-1, keepdims=True)))

    row = lambda cols: pl.BlockSpec((ts, cols), lambda i: (i, 0))
    act = jax.ShapeDtypeStruct((s, D), F32)
    act_bf = jax.ShapeDtypeStruct((s, D), BF16)
    return pl.pallas_call(
        body, name="head_and_loss", grid=(s // ts,),
        out_shape=(act, act_bf, act_bf, jax.ShapeDtypeStruct((D, D), F32), jax.ShapeDtypeStruct((8, D), F32)),
        in_specs=[row(D), row(D), row(D), row(D),
                  _full((1, D)), _full((D, D)), _full((1, D))],
        out_specs=(row(D), row(D), row(D), _full((D, D)), _full((8, D))),
        compiler_params=_params("arbitrary"),
    )(o, gate, h1, target, hw, wgo, wf)


def gla_project_backward(dq, dk, dv, dgate, dcum, low, h1, dh2, w1, wgi, wgk, bgk):
    s = h1.shape[0]
    ts = ROW_TILE

    def body(dq_ref, dk_ref, dv_ref, dgate_ref, dcum_ref, low_ref, h1_ref, dh2_ref, w1_ref,
             wgi_ref, wgk_ref, bgk_ref, dh1_ref, dproj_ref, ggk_ref, small_ref):
        @pl.when(pl.program_id(0) == 0)
        def _():
            ggk_ref[...] = jnp.zeros_like(ggk_ref)
            small_ref[...] = jnp.zeros_like(small_ref)

        low = low_ref[...]
        z = _nn(low, wgk_ref[...]) + bgk_ref[...]
        upper_f = _chunk_masks()[1].astype(F32)
        dlg = jnp.concatenate([_nn_exact(upper_f, dcum_ref[r0:r0 + CHUNK, :]) for r0 in range(0, ts, CHUNK)],
                              axis=0)
        dz = dlg * (1.0 / GATE_NORM) * _sigmoid(-z)
        dz_bf = _bf(dz)
        ggk_ref[...] += _tn(low, dz_bf)
        small_ref[1:2, 0:KEY_W] += jnp.sum(dz, axis=0, keepdims=True)
        dlow = _bf(_nt(dz_bf, wgk_ref[...]))
        dproj_ref[:, GLA_MAIN:] = dlow
        dn1 = _nt(dlow, wgi_ref[:, GLA_MAIN:])
        for ref, lo, hi in ((dq_ref, 0, KEY_W), (dk_ref, KEY_W, 2 * KEY_W),
                            (dv_ref, 2 * KEY_W, 2 * KEY_W + D), (dgate_ref, 2 * KEY_W + D, GLA_MAIN)):
            piece = ref[...]
            dproj_ref[:, lo:hi] = piece
            dn1 = dn1 + _nt(piece, wgi_ref[:, lo:hi])
        hv = h1_ref[...]
        r = lax.rsqrt(jnp.mean(hv * hv, axis=-1, keepdims=True) + EPS)
        hhat = hv * r
        small_ref[0:1, :] += jnp.sum(dn1 * hhat, axis=0, keepdims=True)
        dxh = dn1 * w1_ref[...]
        dh1_ref[...] = dh2_ref[...] + r * (dxh - hhat * jnp.mean(dxh * hhat, axis=-1, keepdims=True))

    row = lambda cols: pl.BlockSpec((ts, cols), lambda i: (i, 0))
    return pl.pallas_call(
        body, name="gla_project_backward", grid=(s // ts,),
        out_shape=(jax.ShapeDtypeStruct((s, D), F32), jax.ShapeDtypeStruct((s, GLA_MAIN + RANK_PAD), BF16),
                   jax.ShapeDtypeStruct((RANK_PAD, KEY_W), F32),
                   jax.ShapeDtypeStruct((8, D), F32)),
        in_specs=[row(KEY_W), row(KEY_W), row(D), row(D), row(KEY_W), row(RANK_PAD), row(D), row(D),
                  _full((1, D)), _full((D, GLA_MAIN + RANK_PAD)), _full((RANK_PAD, KEY_W)),
                  _full((1, KEY_W))],
        out_specs=(row(D), row(GLA_MAIN + RANK_PAD), _full((RANK_PAD, KEY_W)), _full((8, D))),
        compiler_params=_params("arbitrary"),
    )(dq, dk, dv, dgate, dcum, low, h1, dh2, w1, wgi, wgk, bgk)


def local_gradients(xs, target, w0, w1, wf, wpi, gw, gb, scale, wpo, gla_quarters, wgk, bgk, hw_tiled, place):
    wgi_q, wgo_q = gla_quarters
    (h1, pooled, gt, n0), (wgi_q,) = pool_forward(xs, w0, wpi, gw, gb, scale, wpo, [wgi_q])
    (qk, v, gate, low, cum, n1, wgi), (wgo_q,) = gla_project(h1, w1, wgi_q, wgk, bgk, [wgo_q])
    wgo = wgo_q.reshape(D, D)
    o, states, scores = gla_forward(qk, v, cum)

    dh2, do, dgate, g_gla_out, small_top = head_and_loss(o, gate, h1, target, hw_tiled, wgo, wf)
    dq, dk, dv, dcum = gla_backward(qk, v, cum, do, states, scores)
    dh1, dproj, g_gk_pad, small_gla = gla_project_backward(
        dq, dk, dv, dgate, dcum, low, h1, dh2, w1, wgi, wgk, bgk)
    g_gla_in, _ = matmul_tn(n1, dproj, "grad_gla_in", (GLA_MAIN + RANK_PAD) // 5, place)

    gla_sums = add_halves([g_gla_in, g_gla_out.reshape(N_CHIPS, D // N_CHIPS, D)], place, "add_halves_gla")
    (dx, dpool, g_pool_out, g_group_w, small_pool), gla_got = pool_backward(
        xs, dh1, pooled, gt, w0, wpi, gw, gb, scale, wpo, [b for _, b in gla_sums])
    g_pool_in, mix = matmul_tn(n0, dpool, "grad_pool_in", D // 2, place, by_column_tile=True,
                               also_reduce=[g_group_w, g_pool_out.reshape(N_CHIPS, D // N_CHIPS, D)])

    reduced, total = join_halves(
        g_pool_in, place, [own for own, _ in mix] + [f for f, _ in gla_sums], [got for _, got in mix] + list(gla_got),
        small_pool, small_gla, small_top, g_gk_pad)
    return dx, reduced, total


def kernel(x, norm_w, pool_in_w, pool_group_w, pool_group_b, pool_scale, pool_out_w, gla_in_w, gla_gk_w, gla_gk_b, gla_head_norm_w, gla_out_w, final_norm_w, loss_target, m_norm_w, m_pool_in_w, m_pool_group_w, m_pool_group_b, m_pool_scale, m_pool_out_w, m_gla_in_w, m_gla_gk_w, m_gla_gk_b, m_gla_head_norm_w, m_gla_out_w, m_final_norm_w, v_norm_w, v_pool_in_w, v_pool_group_w, v_pool_group_b, v_pool_scale, v_pool_out_w, v_gla_in_w, v_gla_gk_w, v_gla_gk_b, v_gla_head_norm_w, v_gla_out_w, v_final_norm_w):
    xs = x[0]
    target = loss_target[0]
    q_chip = 2 * lax.axis_index("x") + lax.axis_index("y")
    place = jnp.stack([lax.axis_index("c"), q_chip]).astype(jnp.int32)

    (wpi, gw_q, wpo_q, wgi_q, wgo_q), (bgk, hw_tiled, gb, wgk) = allgather_weights(
        [pool_in_w[0], pool_group_w[0].reshape(GROUP_DIM, GROUP_DIM), pool_out_w[0], gla_in_w[0], gla_out_w[0]],
        exchange=(True, True, True, False, False),
        smalls=[gla_gk_b, gla_head_norm_w, pool_group_b[0], gla_gk_w[0]])
    wpo = wpo_q.reshape(D, D)

    w0 = norm_w[0:1]
    w1 = norm_w[1:2]
    wf = final_norm_w.reshape(1, D)

    dx, reduced, total = local_gradients(
        xs, target, w0, w1, wf, wpi, gw_q, gb, pool_scale, wpo, [wgi_q, wgo_q], wgk, bgk, hw_tiled, place)
    r_pool_in, r_group_w, r_pool_out, r_gla_in, r_gla_out = reduced
    r_group_w = r_group_w.reshape(GROUPS, 64, GROUP_DIM)

    turn = lambda a: jnp.transpose(a, (2, 0, 1))
    back = lambda a: jnp.transpose(a, (1, 2, 0))
    as2d = lambda a, w: a.reshape(-1, w.shape[-1])
    big_names = ("pool_in_w", "pool_group_w", "pool_out_w", "gla_in_w", "gla_out_w")
    big_args = [(pool_in_w, r_pool_in[None], m_pool_in_w, v_pool_in_w),
                (pool_group_w, r_group_w[None], m_pool_group_w, v_pool_group_w),
                (pool_out_w, r_pool_out[None], m_pool_out_w, v_pool_out_w),
                (gla_in_w, r_gla_in[None], m_gla_in_w, v_gla_in_w),
                (gla_out_w, r_gla_out[None], m_gla_out_w, v_gla_out_w)]
    to_kernel = lambda n, a, w: turn(a) if n == "gla_in_w" else as2d(a, w)
    from_kernel = lambda n, a, w: back(a) if n == "gla_in_w" else a.reshape(w.shape)
    big_in = [tuple(to_kernel(n, a, p[0]) for a in p) for n, p in zip(big_names, big_args)]
    big_out = adamw(big_in, "adamw", echo=[n != "gla_in_w" for n in big_names])
    big = {}
    for n, p, i, out in zip(big_names, big_args, big_in, big_out):
        g = out[3] if len(out) == 4 else i[1]
        big[n] = tuple(from_kernel(n, o, p[0]) for o in (g, *out[:3]))

    small_names = ("norm_w", "pool_group_b", "pool_scale", "gla_gk_w", "gla_gk_b", "gla_head_norm_w",
                   "final_norm_w")
    small_args = [(norm_w, m_norm_w, v_norm_w),
                  (pool_group_b, m_pool_group_b, v_pool_group_b),
                  (pool_scale, m_pool_scale, v_pool_scale),
                  (gla_gk_w, m_gla_gk_w, v_gla_gk_w),
                  (gla_gk_b, m_gla_gk_b, v_gla_gk_b),
                  (gla_head_norm_w, m_gla_head_norm_w, v_gla_head_norm_w),
                  (final_norm_w, m_final_norm_w, v_final_norm_w)]
    small_out, loss = adamw_small([tuple(as2d(a, p[0]) for a in p) for p in small_args], total, place)
    small = {n: tuple(o.reshape(p[0].shape) for o in out) for n, p, out in zip(small_names, small_args, small_out)}
    results = [
        small["norm_w"],
        big["pool_in_w"],
        big["pool_group_w"],
        small["pool_group_b"],
        small["pool_scale"],
        big["pool_out_w"],
        big["gla_in_w"],
        small["gla_gk_w"],
        small["gla_gk_b"],
        small["gla_head_norm_w"],
        big["gla_out_w"],
        small["final_norm_w"],
    ]
    grads, deltas, new_m, new_v = zip(*results)
    return (loss.reshape(()), dx[None], *grads, *deltas, *new_m, *new_v)
```

```python
import jax
import jax.numpy as jnp
from jax import lax
from jax.experimental import pallas as pl
from jax.experimental.pallas import tpu as pltpu

F32 = jnp.float32
BF16 = jnp.bfloat16
MESH = pl.DeviceIdType.MESH

D = 1024
POOL_WINDOWS = (2, 4, 8, 16)
GROUPS = 4
GROUP_DIM = 256
HEADS = 4
HEAD_K = 128
HEAD_V = 256
KEY_W = 512
CHUNK = 64
GATE_RANK = 16
GATE_NORM = 16.0
GLA_IN = 3088
GLA_MAIN = 3072
RANK_PAD = 128
EPS = 1e-6
HALO = 32

ADAM_LR = 0.001
ADAM_B1 = 0.9
ADAM_B2 = 0.999
ADAM_EPS = 1e-08
ADAM_WD = 0.01
ADAM_STEP = 10

N_CHIPS = 4
N_DEV = 8
GLA_IN_QUARTER = GLA_IN // N_CHIPS

VMEM_LIMIT = 56 * 1024 * 1024


def _nn(a, b):
    return lax.dot_general(a, b, (((1,), (0,)), ((), ())), preferred_element_type=F32)


def _nt(a, b):
    return lax.dot_general(a, b, (((1,), (1,)), ((), ())), preferred_element_type=F32)


def _tn(a, b):
    return lax.dot_general(a, b, (((0,), (0,)), ((), ())), preferred_element_type=F32)


def _nn_exact(a, b):
    return lax.dot_general(a, b, (((1,), (0,)), ((), ())), preferred_element_type=F32,
                           precision=lax.Precision.HIGHEST)


def _bf(a):
    return a.astype(BF16)


def _params(*sem):
    return pltpu.CompilerParams(dimension_semantics=sem, vmem_limit_bytes=VMEM_LIMIT)


def _full(shape):
    return pl.BlockSpec(shape, lambda i: (0,) * len(shape))


def _position():
    return lax.axis_index("x"), lax.axis_index("y"), lax.axis_index("c")


def _gather_small(in_ref, all_ref, send_sems, recv_sems, local_sem):
    x, y, c = _position()
    me = 4 * x + 2 * y + c
    mine = pltpu.make_async_copy(in_ref, all_ref.at[me], local_sem)
    sends = []
    for k in range(N_DEV - 1):
        fx, fy, fc = (k + 1) >> 2 & 1, (k + 1) >> 1 & 1, (k + 1) & 1
        sends.append(pltpu.make_async_remote_copy(
            src_ref=in_ref, dst_ref=all_ref.at[me],
            send_sem=send_sems.at[k], recv_sem=recv_sems.at[k],
            device_id=(x ^ fx, y ^ fy, c ^ fc), device_id_type=MESH))

    def start():
        mine.start()
        for cp in sends:
            cp.start()

    def wait():
        for k in range(N_DEV - 1):
            fx, fy, fc = (k + 1) >> 2 & 1, (k + 1) >> 1 & 1, (k + 1) & 1
            src_dev = 4 * (x ^ fx) + 2 * (y ^ fy) + (c ^ fc)
            pltpu.make_async_remote_copy(
                src_ref=in_ref, dst_ref=all_ref.at[src_dev],
                send_sem=send_sems.at[k], recv_sem=recv_sems.at[k],
                device_id=(x, y, c), device_id_type=MESH).wait_recv()
        for cp in sends:
            cp.wait_send()
        mine.wait()

    return start, wait


SMALL_SEMS = [pltpu.SemaphoreType.DMA((N_DEV - 1,)), pltpu.SemaphoreType.DMA((N_DEV - 1,)),
              pltpu.SemaphoreType.DMA]
VMEM_SPEC = pl.BlockSpec(memory_space=pltpu.VMEM)


def _other_chips(x, y):
    return [(1 - x, y), (x, 1 - y), (1 - x, 1 - y)]


def _any_specs(n):
    return [pl.BlockSpec(memory_space=pl.ANY)] * n


def _halves(rows, c):
    half = rows // 2
    return pl.ds(c * half, half), pl.ds((1 - c) * half, half)


CAST_ROWS = 256


def _gather_copy(out_ref, send_sems, recv_sems, k, quarter, half, to, src=None):
    dst = out_ref.at[quarter, half]
    return pltpu.make_async_remote_copy(
        src_ref=dst if src is None else src, dst_ref=dst,
        send_sem=send_sems.at[k], recv_sem=recv_sems.at[k], device_id=to, device_id_type=MESH)


SMALL_IN_ROWS = 24


def allgather_weights(quarters, exchange, smalls):
    n = len(quarters)
    shapes = [w.shape for w in quarters]
    moved = [i for i in range(n) if exchange[i]]

    def body(*refs):
        w_refs, (gkb_ref, hnw_ref, gb_ref, gkw_ref) = refs[:n], refs[n:n + 4]
        out_refs, (bgk_ref, hw_ref, gbias_ref, wgk_ref) = refs[n + 4:2 * n + 4], refs[2 * n + 4:2 * n + 8]
        refs = refs[2 * n + 8:]
        f32_bufs, bf_bufs = refs[:n], refs[n:2 * n]
        send_sems, recv_sems, local_sems, small_ref, small_all_ref = refs[2 * n:2 * n + 5]
        small_ref[...] = jnp.zeros_like(small_ref)
        small_ref[0:1, :] = gkb_ref[...]
        small_ref[1:2, 0:64] = hnw_ref[...]
        small_ref[2:2 + GROUPS, 0:64] = gb_ref[...]
        small_ref[8:8 + GATE_RANK, :] = gkw_ref[...]
        start_small, wait_small = _gather_small(small_ref, small_all_ref, *refs[2 * n + 5:])
        start_small()
        x, y, c = _position()
        q = 2 * x + y
        sibling = (x, y, 1 - c)
        chips = _other_chips(x, y)

        def copy(k, i, quarter, half, to, src=None):
            return _gather_copy(out_refs[i], send_sems, recv_sems, k * n + i, quarter, half, to, src)

        loads = [pltpu.make_async_copy(w_refs[i], f32_bufs[i], local_sems.at[i]) for i in range(n)]
        for cp in loads:
            cp.start()
        keeps, sends = [], []
        for i in range(n):
            loads[i].wait()
            for r0 in range(0, shapes[i][0], CAST_ROWS):
                bf_bufs[i][r0:r0 + CAST_ROWS, :] = _bf(f32_bufs[i][r0:r0 + CAST_ROWS, :])
            keep = pltpu.make_async_copy(bf_bufs[i], out_refs[i].at[q], local_sems.at[n + i])
            keep.start()
            keeps.append(keep)
            if not exchange[i]:
                continue
            mine, _ = _halves(shapes[i][0], c)
            for j, chip in enumerate(chips):
                cp = copy(j, i, q, mine, (*chip, c), src=bf_bufs[i].at[mine])
                cp.start()
                sends.append(cp)
        for j, chip in enumerate(chips):
            qj = 2 * chip[0] + chip[1]
            for i in moved:
                mine, _ = _halves(shapes[i][0], c)
                copy(j, i, qj, mine, (x, y, c)).wait_recv()
                cp = copy(3 + j, i, qj, mine, sibling)
                cp.start()
                sends.append(cp)
        for j, chip in enumerate(chips):
            qj = 2 * chip[0] + chip[1]
            for i in moved:
                _, other = _halves(shapes[i][0], c)
                copy(3 + j, i, qj, other, (x, y, c)).wait_recv()
        wait_small()
        wgk_ref[...] = jnp.zeros_like(wgk_ref)
        for j in range(N_CHIPS):
            block = small_all_ref.at[2 * j]
            bgk_ref[:, 128 * j:128 * (j + 1)] = block[0:1, :]
            for h in range(HEADS):
                hw_ref[:, HEAD_V * h + 64 * j:HEAD_V * h + 64 * (j + 1)] = block[1:2, 0:64]
            for g in range(GROUPS):
                gbias_ref[:, GROUP_DIM * g + 64 * j:GROUP_DIM * g + 64 * (j + 1)] = block[2 + g:3 + g, 0:64]
            wgk_ref[0:GATE_RANK, 128 * j:128 * (j + 1)] = _bf(block[8:8 + GATE_RANK, :])
        for cp in sends:
            cp.wait_send()
        for cp in keeps:
            cp.wait()

    outs = pl.pallas_call(
        body, name="allgather_weights",
        out_shape=[jax.ShapeDtypeStruct((N_CHIPS, *s), BF16) for s in shapes]
                  + [jax.ShapeDtypeStruct((1, KEY_W), F32), jax.ShapeDtypeStruct((1, D), F32),
                     jax.ShapeDtypeStruct((1, D), F32), jax.ShapeDtypeStruct((RANK_PAD, KEY_W), BF16)],
        in_specs=_any_specs(n) + [VMEM_SPEC] * 4, out_specs=_any_specs(n) + [VMEM_SPEC] * 4,
        scratch_shapes=([pltpu.VMEM(s, F32) for s in shapes] + [pltpu.VMEM(s, BF16) for s in shapes]
                        + [pltpu.SemaphoreType.DMA((6 * n,)), pltpu.SemaphoreType.DMA((6 * n,)),
                           pltpu.SemaphoreType.DMA((2 * n,)), pltpu.VMEM((SMALL_IN_ROWS, 128), F32),
                           pltpu.VMEM((N_DEV, SMALL_IN_ROWS, 128), F32)] + SMALL_SEMS),
        compiler_params=pltpu.CompilerParams(vmem_limit_bytes=VMEM_LIMIT),
    )(*quarters, *smalls)
    return outs[:n], outs[n:]


def _scatter_copies(b_refs, got_refs, send_sems, recv_sems):
    n = len(b_refs)
    x, y, c = _position()
    copies = []
    for j, chip in enumerate(_other_chips(x, y)):
        qj = 2 * chip[0] + chip[1]
        for i in range(n):
            copies.append(pltpu.make_async_remote_copy(
                src_ref=b_refs[i].at[qj], dst_ref=got_refs[i].at[j],
                send_sem=send_sems.at[j * n + i], recv_sem=recv_sems.at[j * n + i],
                device_id=(*chip, c), device_id_type=MESH))
    return copies


def _scatter_shapes(chip_sums):
    return [jax.ShapeDtypeStruct((N_CHIPS - 1, *b.shape[1:]), BF16) for b in chip_sums]


ADD_ROWS = 512
ADD_HALVES_ROWS = 128


def _spans(counts):
    starts, total = [], 0
    for count in counts:
        starts.append(total)
        total += count
    return starts, total


def _local_step(t, start, count):
    return jnp.clip(t - start, 0, count - 1)


def add_halves(grads, place, name):
    n = len(grads)
    whole = [len(g.shape) == 2 for g in grads]
    halves = [g.shape[-2] // 2 for g in grads]
    cols = [GLA_IN_QUARTER if w else g.shape[-1] for g, w in zip(grads, whole)]
    rbs = [min(ADD_HALVES_ROWS, h) for h in halves]
    counts = [h // rb for h, rb in zip(halves, rbs)]
    starts, total = _spans(counts)
    half_shapes = [(*g.shape[:-2], h, g.shape[-1]) for g, h in zip(grads, halves)]

    def rows_of(ref, i, start):
        return ref.at[pl.ds(start, rbs[i])] if whole[i] else ref.at[:, pl.ds(start, rbs[i])]

    def body(place_ref, *refs):
        a_refs, o_refs = refs[:n], refs[n:2 * n]
        f_refs, h_refs = refs[2 * n:3 * n], refs[3 * n:4 * n]
        send_refs, their_refs, (send_sems, recv_sems) = refs[4 * n:5 * n], refs[5 * n:6 * n], refs[6 * n:]
        t = pl.program_id(0)
        q = place_ref[1]
        x, y, c = _position()
        copies = [[pltpu.make_async_remote_copy(
            src_ref=rows_of(send_refs[i], i, k * rbs[i]), dst_ref=rows_of(their_refs[i], i, k * rbs[i]),
            send_sem=send_sems.at[starts[i] + k], recv_sem=recv_sems.at[starts[i] + k],
            device_id=(x, y, 1 - c), device_id_type=MESH) for k in range(counts[i])] for i in range(n)]

        for i in range(n):
            for k in range(counts[i]):
                @pl.when(t == starts[i] + k)
                def _(i=i, k=k):
                    rows_of(send_refs[i], i, k * rbs[i])[...] = _bf(o_refs[i][...])
                    copies[i][k].start()

        for i in range(n):
            for k in range(counts[i]):
                @pl.when(t == starts[i] + k + 1)
                def _(i=i, k=k):
                    copies[i][k].wait_recv()
                    b_ref = rows_of(their_refs[i], i, k * rbs[i])
                    if not whole[i]:
                        h_refs[i][...] = _bf(a_refs[i][...] + b_ref[...].astype(F32))
                        f_refs[i][...] = a_refs[i][q] + b_ref[q].astype(F32)
                        return
                    total_i = a_refs[i][...] + b_ref[...].astype(F32)
                    for k4 in range(N_CHIPS):
                        piece = total_i[:, k4 * cols[i]:(k4 + 1) * cols[i]]
                        h_refs[i][k4] = _bf(piece)

                        @pl.when(q == k4)
                        def _():
                            f_refs[i][...] = piece

        @pl.when(t == total)
        def _():
            for of_matrix in copies:
                for cp in of_matrix:
                    cp.wait_send()

    def specs(i):
        sent = lambda t: _local_step(t, starts[i], counts[i])
        added = lambda t: _local_step(t - 1, starts[i], counts[i])
        by_quarter = (N_CHIPS, rbs[i], cols[i])
        block = (rbs[i], grads[i].shape[-1]) if whole[i] else by_quarter
        lead = () if whole[i] else (0,)
        mine = pl.BlockSpec(block, lambda t, place: (*lead, place[0] * counts[i] + added(t), 0))
        other = pl.BlockSpec(block, lambda t, place: (*lead, (1 - place[0]) * counts[i] + sent(t), 0))
        sums = pl.BlockSpec(by_quarter, lambda t, place: (0, added(t), 0))
        own = pl.BlockSpec(by_quarter[1:], lambda t, place: (added(t), 0))
        return mine, other, own, sums

    all_specs = [specs(i) for i in range(n)]
    outs = pl.pallas_call(
        body, name=name,
        grid_spec=pltpu.PrefetchScalarGridSpec(
            num_scalar_prefetch=1, grid=(total + 1,),
            in_specs=[sp[0] for sp in all_specs] + [sp[1] for sp in all_specs],
            out_specs=[sp[2] for sp in all_specs] + [sp[3] for sp in all_specs],
            scratch_shapes=[pltpu.VMEM(sh, BF16) for sh in half_shapes] + [pltpu.VMEM(sh, BF16) for sh in half_shapes]
                           + [pltpu.SemaphoreType.DMA((total,)), pltpu.SemaphoreType.DMA((total,))]),
        out_shape=[jax.ShapeDtypeStruct((h, cl), F32) for h, cl in zip(halves, cols)]
                  + [jax.ShapeDtypeStruct((N_CHIPS, h, cl), BF16) for h, cl in zip(halves, cols)],
        compiler_params=_params("arbitrary"),
    )(place, *grads, *grads)
    return list(zip(outs[:n], outs[n:]))


SMALL_SUM_ROWS = 16


def join_halves(grad, place, owns, gots, small_pool, small_gla, small_top, g_gk_pad):
    n = len(owns)
    half, cols = grad.shape[1] // 2, grad.shape[2]
    rb = min(ADD_HALVES_ROWS, half)
    sent = half // rb
    shapes = [g.shape for g in gots] + [(N_CHIPS - 1, half, cols)]
    rbs = [min(ADD_ROWS, sh[1]) for sh in shapes]
    counts = [sh[1] // r for sh, r in zip(shapes, rbs)]
    starts, joined = _spans(counts)
    first_join = sent + 1
    steps = first_join + joined

    def body(place_ref, *refs):
        refs = iter(refs)
        take = lambda count: [next(refs) for _ in range(count)]
        (a_ref, o_ref), o_refs, g_refs = take(2), take(n), take(n)
        pool_ref, gla_ref, top_ref, gk_ref = take(4)
        out_refs, (total_ref,) = take(n + 1), take(1)
        send_buf, their_buf, sums_buf, got_buf = take(4)
        sum_refs = take(n + 1)
        to_core_sems, from_core_sems, to_chip_sems, from_chip_sems, local_sems, send_sems, recv_sems = take(7)
        all_ref, small_ref = take(2)
        t = pl.program_id(0)
        q = place_ref[1]
        x, y, c = _position()
        start_small, wait_small = _gather_small(small_ref, all_ref, *refs)
        block = lambda k: pl.ds(k * rb, rb)

        def to_core(k):
            return pltpu.make_async_remote_copy(
                src_ref=send_buf.at[:, block(k)], dst_ref=their_buf.at[:, block(k)],
                send_sem=to_core_sems.at[k], recv_sem=from_core_sems.at[k],
                device_id=(x, y, 1 - c), device_id_type=MESH)

        def to_owners(k):
            return [pltpu.make_async_remote_copy(
                src_ref=sums_buf.at[2 * chip[0] + chip[1], block(k)], dst_ref=got_buf.at[j, block(k)],
                send_sem=to_chip_sems.at[3 * k + j], recv_sem=from_chip_sems.at[3 * k + j],
                device_id=(*chip, c), device_id_type=MESH) for j, chip in enumerate(_other_chips(x, y))]

        def copies(i, k):
            src = sum_refs[i].at[pl.ds(k * rbs[i], rbs[i])]
            rows = pl.ds(c * shapes[i][1] + k * rbs[i], rbs[i])
            return (pltpu.make_async_copy(src, out_refs[i].at[rows], local_sems.at[starts[i] + k]),
                    pltpu.make_async_remote_copy(
                        src_ref=src, dst_ref=out_refs[i].at[rows],
                        send_sem=send_sems.at[starts[i] + k], recv_sem=recv_sems.at[starts[i] + k],
                        device_id=(x, y, 1 - c), device_id_type=MESH))

        @pl.when(t == 0)
        def _():
            small_ref[0:3, :] = pool_ref[0:3, :]
            small_ref[3:5, :] = gla_ref[0:2, :]
            small_ref[5:8, :] = top_ref[0:3, :]
            for r in range(GATE_RANK):
                small_ref[8 + r // 2:9 + r // 2, (r % 2) * KEY_W:(r % 2 + 1) * KEY_W] = gk_ref[r:r + 1, :]
            start_small()

        for k in range(sent):
            @pl.when(t == k)
            def _(k=k):
                send_buf[:, k * rb:(k + 1) * rb, :] = _bf(o_ref[...])
                to_core(k).start()

        for k in range(sent):
            @pl.when(t == k + 1)
            def _(k=k):
                to_core(k).wait_recv()
                theirs = their_buf.at[:, block(k)]
                sums_buf[:, k * rb:(k + 1) * rb, :] = _bf(a_ref[...] + theirs[...].astype(F32))
                sum_refs[n][k * rb:(k + 1) * rb, :] = a_ref[q] + theirs[q].astype(F32)
                for cp in to_owners(k):
                    cp.start()

        for i in range(n + 1):
            for k in range(counts[i]):
                @pl.when(t == first_join + starts[i] + k)
                def _(i=i, k=k):
                    rows = slice(k * rbs[i], (k + 1) * rbs[i])
                    if i < n:
                        total_i = o_refs[i][...]
                        arrived = [g_refs[i][j] for j in range(N_CHIPS - 1)]
                    else:
                        if k == 0:
                            for kk in range(sent):
                                for cp in to_owners(kk):
                                    cp.wait_recv()
                        total_i = sum_refs[n][rows, :]
                        arrived = [got_buf[j, rows, :] for j in range(N_CHIPS - 1)]
                    for part in arrived:
                        total_i = total_i + part.astype(F32)
                    sum_refs[i][rows, :] = total_i
                    for cp in copies(i, k):
                        cp.start()

        @pl.when(t == steps - 1)
        def _():
            wait_small()
            small_total = all_ref[0]
            for dev in range(1, N_DEV):
                small_total = small_total + all_ref[dev]
            total_ref[...] = small_total
            for k in range(sent):
                to_core(k).wait_send()
                for cp in to_owners(k):
                    cp.wait_send()
            for i in range(n + 1):
                for k in range(counts[i]):
                    for cp in copies(i, k):
                        cp.wait()

    def specs(i):
        step = lambda t: _local_step(t - first_join, starts[i], counts[i])
        return (pl.BlockSpec((rbs[i], shapes[i][2]), lambda t, place: (step(t), 0)),
                pl.BlockSpec((N_CHIPS - 1, rbs[i], shapes[i][2]), lambda t, place: (0, step(t), 0)))

    by_quarter = (N_CHIPS, rb, cols)
    mine = pl.BlockSpec(by_quarter, lambda t, place: (0, place[0] * sent + jnp.clip(t - 1, 0, sent - 1), 0))
    other = pl.BlockSpec(by_quarter, lambda t, place: (0, (1 - place[0]) * sent + jnp.clip(t, 0, sent - 1), 0))
    all_specs = [specs(i) for i in range(n)]
    sems = lambda count: pltpu.SemaphoreType.DMA((count,))
    outs = pl.pallas_call(
        body, name="join_halves",
        grid_spec=pltpu.PrefetchScalarGridSpec(
            num_scalar_prefetch=1, grid=(steps,),
            in_specs=[mine, other] + [sp[0] for sp in all_specs] + [sp[1] for sp in all_specs] + [VMEM_SPEC] * 4,
            out_specs=_any_specs(n + 1) + [VMEM_SPEC],
            scratch_shapes=[pltpu.VMEM((N_CHIPS, half, cols), BF16) for _ in range(3)]
                           + [pltpu.VMEM(shapes[n], BF16)] + [pltpu.VMEM(sh[1:], F32) for sh in shapes]
                           + [sems(sent), sems(sent), sems(3 * sent), sems(3 * sent),
                              sems(joined), sems(joined), sems(joined),
                              pltpu.VMEM((N_DEV, SMALL_SUM_ROWS, D), F32), pltpu.VMEM((SMALL_SUM_ROWS, D), F32)]
                           + SMALL_SEMS),
        out_shape=[jax.ShapeDtypeStruct((2 * sh[1], sh[2]), F32) for sh in shapes]
                  + [jax.ShapeDtypeStruct((SMALL_SUM_ROWS, D), F32)],
        compiler_params=_params("arbitrary"),
    )(place, grad, grad, *owns, *gots, small_pool, small_gla, small_top, g_gk_pad)
    return [outs[n]] + list(outs[:n]), outs[n + 1]


def _adam_math(w, g, m, v):
    m = ADAM_B1 * m + (1.0 - ADAM_B1) * g
    v = ADAM_B2 * v + (1.0 - ADAM_B2) * (g * g)
    m_hat = m / (1.0 - ADAM_B1 ** ADAM_STEP)
    v_hat = v / (1.0 - ADAM_B2 ** ADAM_STEP)
    delta = -ADAM_LR * (m_hat / (jnp.sqrt(v_hat) + ADAM_EPS) + ADAM_WD * w)
    return delta, m, v


ADAM_BLOCK_BYTES = 2 ** 19
ADAM_MOST_STEPS = 8


def adamw(params, name, echo):
    n = len(params)
    shapes = [p[0].shape for p in params]
    first_out, _ = _spans([4 if e else 3 for e in echo])

    def tile_rows(shape):
        rows, cols = shape[0], shape[-1]
        aligned = 1 if len(shape) == 3 else 8
        divisors = [t for t in range(aligned, rows + 1, aligned) if rows % t == 0]
        tile = max(t for t in divisors if t * cols * 4 <= ADAM_BLOCK_BYTES)
        if rows // tile > ADAM_MOST_STEPS:
            tile = min(t for t in divisors if rows // t <= ADAM_MOST_STEPS)
        return tile

    tiles = [tile_rows(sh) for sh in shapes]
    counts = [sh[0] // tl for sh, tl in zip(shapes, tiles)]
    starts, total = _spans(counts)

    def body(*refs):
        ins, outs = refs[:4 * n], refs[4 * n:]
        t = pl.program_id(0)
        for i in range(n):
            @pl.when((t >= starts[i]) & (t < starts[i] + counts[i]))
            def _(i=i):
                w_ref, g_ref, m_ref, v_ref = ins[4 * i:4 * i + 4]
                g = g_ref[...]
                d, nm, nv = _adam_math(w_ref[...], g, m_ref[...], v_ref[...])
                outs[first_out[i]][...] = d
                outs[first_out[i] + 1][...] = nm
                outs[first_out[i] + 2][...] = nv
                if echo[i]:
                    outs[first_out[i] + 3][...] = g

    def spec(i):
        block = (tiles[i],) + shapes[i][1:]
        zeros = (0,) * (len(block) - 1)
        return pl.BlockSpec(block, lambda t: (_local_step(t, starts[i], counts[i]),) + zeros)

    outs = pl.pallas_call(
        body, name=name, grid=(total,),
        out_shape=[jax.ShapeDtypeStruct(sh, F32) for sh, e in zip(shapes, echo) for _ in range(4 if e else 3)],
        in_specs=[spec(i) for i in range(n) for _ in range(4)],
        out_specs=[spec(i) for i in range(n) for _ in range(4 if echo[i] else 3)],
        compiler_params=_params("arbitrary"),
    )(*[a for p in params for a in p])
    return [tuple(outs[first_out[i]:first_out[i] + (4 if echo[i] else 3)]) for i in range(n)]


def adamw_small(params, total, place):
    n = len(params)

    def cut_gradients(total_ref, q, g_refs):
        g_norm, g_group_b, g_scale, g_gk_w, g_gk_b, g_head_norm, g_final = g_refs
        g_norm[0:1, :] = total_ref[0:1, :]
        g_norm[1:2, :] = total_ref[3:4, :]
        g_scale[...] = total_ref[1:2, :]
        g_final[...] = total_ref[5:6, :]
        g_gk_b[...] = total_ref[4:5, pl.ds(pl.multiple_of(q * 128, 128), 128)]
        for r in range(GATE_RANK):
            lanes = pl.ds(pl.multiple_of((r % 2) * KEY_W + q * 128, 128), 128)
            g_gk_w[r:r + 1, :] = total_ref[8 + r // 2:9 + r // 2, lanes]
        for k in range(N_CHIPS):
            @pl.when(q == k)
            def _(k=k):
                g_head_norm[...] = total_ref[6:7, 64 * k:64 * (k + 1)]
                for g in range(GROUPS):
                    g_group_b[g:g + 1, :] = total_ref[2:3, GROUP_DIM * g + 64 * k:GROUP_DIM * g + 64 * (k + 1)]

    def body(place_ref, total_ref, *refs):
        ins, outs = refs[:3 * n], refs[3 * n:]
        outs[4 * n][...] = total_ref[7:8, 0:1]
        cut_gradients(total_ref, place_ref[1], outs[0:4 * n:4])
        for k in range(n):
            w_ref, m_ref, v_ref = ins[3 * k:3 * k + 3]
            d, nm, nv = _adam_math(w_ref[...], outs[4 * k][...], m_ref[...], v_ref[...])
            outs[4 * k + 1][...] = d
            outs[4 * k + 2][...] = nm
            outs[4 * k + 3][...] = nv

    flat = [a for p in params for a in p]
    outs = pl.pallas_call(
        body, name="adamw_small",
        out_shape=[jax.ShapeDtypeStruct(p[0].shape, F32) for p in params for _ in range(4)]
                  + [jax.ShapeDtypeStruct((1, 1), F32)],
        in_specs=[pl.BlockSpec(memory_space=pltpu.SMEM)] + [VMEM_SPEC] * (1 + 3 * n),
        out_specs=[VMEM_SPEC] * (4 * n + 1),
    )(place, total, *flat)
    return [tuple(outs[4 * k:4 * k + 4]) for k in range(n)], outs[4 * n]


def matmul_tn(a, b, name, tile_n, place, by_column_tile=False, also_reduce=()):
    s, m = a.shape
    n = b.shape[1]
    steps = n // tile_n
    n_red = len(also_reduce)
    assert n_red == 0 or steps >= 2
    halves = [(g.shape[1] // 2, g.shape[2]) for g in also_reduce]
    if by_column_tile:
        out_shape = jax.ShapeDtypeStruct((steps, m, tile_n), F32)
        out_spec = pl.BlockSpec((None, m, tile_n), lambda j, place: (j, 0, 0))
    else:
        out_shape = jax.ShapeDtypeStruct((m, n), F32)
        out_spec = pl.BlockSpec((m, tile_n), lambda j, place: (0, j))

    def body(place_ref, a_ref, b_ref, *rest):
        rest = iter(rest)
        take = lambda count: [next(rest) for _ in range(count)]
        mine_refs, other_refs, (out_ref,) = take(n_red), take(n_red), take(1)
        own_refs, got_refs = take(n_red), take(n_red)
        send_bufs, their_bufs, sums_bufs = take(n_red), take(n_red), take(n_red)
        sems = list(rest)
        j = pl.program_id(0)
        q = place_ref[1]
        x, y, c = _position()

        def to_core(i):
            return pltpu.make_async_remote_copy(
                src_ref=send_bufs[i], dst_ref=their_bufs[i], send_sem=sems[0].at[i], recv_sem=sems[1].at[i],
                device_id=(x, y, 1 - c), device_id_type=MESH)

        def to_owners(i):
            return [pltpu.make_async_remote_copy(
                src_ref=sums_bufs[i].at[2 * chip[0] + chip[1]], dst_ref=got_refs[i].at[k],
                send_sem=sems[2].at[3 * i + k], recv_sem=sems[3].at[3 * i + k],
                device_id=(*chip, c), device_id_type=MESH) for k, chip in enumerate(_other_chips(x, y))]

        if n_red:
            @pl.when(j == 0)
            def _():
                for i in range(n_red):
                    send_bufs[i][...] = _bf(other_refs[i][...])
                    to_core(i).start()

            @pl.when(j == 1)
            def _():
                for i in range(n_red):
                    to_core(i).wait_recv()
                    sums_bufs[i][...] = _bf(mine_refs[i][...] + their_bufs[i][...].astype(F32))
                    own_refs[i][...] = mine_refs[i][q] + their_bufs[i][q].astype(F32)
                    for cp in to_owners(i):
                        cp.start()

        out_ref[...] = _tn(a_ref[...], b_ref[...])

        if n_red:
            @pl.when(j == steps - 1)
            def _():
                for i in range(n_red):
                    to_core(i).wait_send()
                    for cp in to_owners(i):
                        cp.wait()

    by_quarter = [(N_CHIPS, *h) for h in halves]
    outs = pl.pallas_call(
        body, name=name,
        grid_spec=pltpu.PrefetchScalarGridSpec(
            num_scalar_prefetch=1, grid=(steps,),
            in_specs=[pl.BlockSpec((s, m), lambda j, place: (0, 0)), pl.BlockSpec((s, tile_n), lambda j, place: (0, j))]
                     + [pl.BlockSpec(sh, lambda j, place: (0, place[0], 0)) for sh in by_quarter]
                     + [pl.BlockSpec(sh, lambda j, place: (0, 1 - place[0], 0)) for sh in by_quarter],
            out_specs=[out_spec] + [pl.BlockSpec(h, lambda j, place: (0, 0)) for h in halves] + _any_specs(n_red),
            scratch_shapes=[pltpu.VMEM(sh, BF16) for sh in by_quarter * 3]
                           + ([pltpu.SemaphoreType.DMA((n_red,)), pltpu.SemaphoreType.DMA((n_red,)),
                               pltpu.SemaphoreType.DMA((3 * n_red,)), pltpu.SemaphoreType.DMA((3 * n_red,))]
                              if n_red else [])),
        out_shape=[out_shape] + [jax.ShapeDtypeStruct(h, F32) for h in halves]
                  + [jax.ShapeDtypeStruct((N_CHIPS - 1, *h), BF16) for h in halves],
        compiler_params=_params("arbitrary"),
    )(place, a, b, *also_reduce, *also_reduce)
    return outs[0], list(zip(outs[1:1 + n_red], outs[1 + n_red:]))


ROW_TILE = 512


def _row_index(tile, rows):
    return tile * rows + lax.broadcasted_iota(jnp.int32, (rows, 1), 0)


def _inverse_counts(t_glob):
    return [1.0 / jnp.minimum(t_glob + 1, w).astype(F32) for w in POOL_WINDOWS]


def _sigmoid(z):
    return 1.0 / (1.0 + jnp.exp(-z))


def _trailing_sums(src, tmp, cols, window, rows):
    bufs = (src, tmp)
    span, level, start = 1, 0, 0
    while span < window:
        start += 8
        a, b = bufs[level % 2], bufs[(level + 1) % 2]
        n = HALO + rows - start
        b[start:start + n, cols] = a[start:start + n, cols] + a[start - span:start - span + n, cols]
        span, level = 2 * span, level + 1
    return bufs[level % 2][HALO:HALO + rows, cols]


def _leading_sums(src, tmp, cols, window, rows):
    bufs = (src, tmp)
    span, level, n = 1, 0, rows + HALO
    while span < window:
        n -= 8
        a, b = bufs[level % 2], bufs[(level + 1) % 2]
        b[0:n, cols] = a[0:n, cols] + a[span:span + n, cols]
        span, level = 2 * span, level + 1
    return bufs[level % 2][0:rows, cols]


def gather_in_background(step, last, out_refs, send_sems, recv_sems, finish):
    n = len(out_refs)
    x, y, c = _position()
    q = 2 * x + y
    chips = _other_chips(x, y)

    def copy(k, i, quarter, half, to):
        return _gather_copy(out_refs[i], send_sems, recv_sems, k * n + i, quarter, half, to)

    if not finish:
        @pl.when(step == 0)
        def _():
            for i in range(n):
                mine, _ = _halves(out_refs[i].shape[1], c)
                for j, chip in enumerate(chips):
                    copy(j, i, q, mine, (*chip, c)).start()

        @pl.when(step == last)
        def _():
            for j, chip in enumerate(chips):
                qj = 2 * chip[0] + chip[1]
                for i in range(n):
                    mine, _ = _halves(out_refs[i].shape[1], c)
                    copy(j, i, qj, mine, (x, y, c)).wait_recv()
                    copy(3 + j, i, qj, mine, (x, y, 1 - c)).start()
        return

    @pl.when(step == last)
    def _():
        for j, chip in enumerate(chips):
            qj = 2 * chip[0] + chip[1]
            for i in range(n):
                mine, other = _halves(out_refs[i].shape[1], c)
                copy(3 + j, i, qj, other, (x, y, c)).wait_recv()
                copy(j, i, q, mine, (x, y, c)).wait_send()
                copy(3 + j, i, qj, mine, (x, y, c)).wait_send()


def _group_matrix(gw_ref, g):
    rows = GROUP_DIM // N_CHIPS
    return jnp.concatenate([gw_ref[j, rows * g:rows * (g + 1), :] for j in range(N_CHIPS)], axis=0)


def pool_forward(x, w0, wpi, gw, gb, scale, wpo, later):
    s = x.shape[0]
    ts = ROW_TILE
    nt = s // ts
    assert nt >= 2
    n_later = len(later)

    def body(x_ref, w0_ref, wpi_ref, gw_ref, gb_ref, sc_ref, wpo_ref, *rest):
        rest = rest[n_later:]
        h1_ref, pooled_ref, gt_ref, n0_ref = rest[:4]
        later_refs = rest[4:4 + n_later]
        ubuf, tbuf, hist, send_sems, recv_sems = rest[4 + n_later:]
        i = pl.program_id(0)
        gather_in_background(i, nt - 1, later_refs, send_sems, recv_sems, finish=False)
        xv = x_ref[...]
        r = lax.rsqrt(jnp.mean(xv * xv, axis=-1, keepdims=True) + EPS)
        n0 = _bf(xv * r * w0_ref[...])
        n0_ref[...] = n0
        u = jnp.concatenate([_nn(n0, wpi_ref[0]), _nn(n0, wpi_ref[1])], axis=-1)
        gt = jnp.concatenate([_nn(n0, wpi_ref[2]), _nn(n0, wpi_ref[3])], axis=-1)
        gt_ref[...] = gt

        @pl.when(i == 0)
        def _():
            hist[...] = jnp.zeros_like(hist)

        ubuf[0:HALO, :] = hist[...]
        ubuf[HALO:HALO + ts, :] = u
        hist[...] = u[ts - HALO:, :]
        inv = _inverse_counts(_row_index(i, ts))
        mixed = []
        for g, w in enumerate(POOL_WINDOWS):
            cols = slice(g * GROUP_DIM, (g + 1) * GROUP_DIM)
            pooled = _bf(_trailing_sums(ubuf, tbuf, cols, w, ts) * inv[g] - u[:, cols])
            pooled_ref[:, cols] = pooled
            mixed.append(_nn(pooled, _group_matrix(gw_ref, g)))
        mixed = jnp.concatenate(mixed, axis=-1) + gb_ref[...]
        y = mixed * sc_ref[...] * (gt * _sigmoid(gt))
        h1_ref[...] = xv + _nn(_bf(y), wpo_ref[...])
        gather_in_background(i, nt - 1, later_refs, send_sems, recv_sems, finish=True)

    row = lambda cols: pl.BlockSpec((ts, cols), lambda i: (i, 0))
    outs = pl.pallas_call(
        body, name="pool_forward", grid=(nt,),
        out_shape=[jax.ShapeDtypeStruct((s, D), F32), jax.ShapeDtypeStruct((s, D), BF16),
                   jax.ShapeDtypeStruct((s, D), F32), jax.ShapeDtypeStruct((s, D), BF16)]
                  + [jax.ShapeDtypeStruct(a.shape, a.dtype) for a in later],
        in_specs=[row(D), _full((1, D)), _full((N_CHIPS, D, D // 2)), _full((GROUPS, GROUP_DIM, GROUP_DIM)),
                  _full((1, D)), _full((1, D)), _full((D, D))] + _any_specs(n_later),
        out_specs=[row(D), row(D), row(D), row(D)] + _any_specs(n_later),
        input_output_aliases={7 + k: 4 + k for k in range(n_later)},
        scratch_shapes=[pltpu.VMEM((HALO + ts, D), F32), pltpu.VMEM((HALO + ts, D), F32),
                        pltpu.VMEM((HALO, D), F32),
                        pltpu.SemaphoreType.DMA((6 * n_later,)), pltpu.SemaphoreType.DMA((6 * n_later,))],
        compiler_params=_params("arbitrary"),
    )(x, w0, wpi, gw, gb, scale, wpo, *later)
    return outs[:4], outs[4:]


def pool_backward(x, dh1, pooled, gt, w0, wpi, gw, gb, scale, wpo, chip_sums):
    s = x.shape[0]
    ts = ROW_TILE
    nt = s // ts
    n_sums = len(chip_sums)

    def body(x_ref, dh1_ref, pooled_ref, gt_ref, w0_ref, wpi_ref, gw_ref, gb_ref, sc_ref, wpo_ref, *rest):
        sum_refs, rest = rest[:n_sums], rest[n_sums:]
        dx_ref, dproj_ref, gpo_ref, ggw_ref, small_ref = rest[:5]
        got_refs = rest[5:5 + n_sums]
        ebuf, tbuf, ahead, send_sems, recv_sems = rest[5 + n_sums:]
        i = pl.program_id(0)
        copies = _scatter_copies(sum_refs, got_refs, send_sems, recv_sems)

        @pl.when(i == 0)
        def _():
            for cp in copies:
                cp.start()

        @pl.when(i == 0)
        def _():
            gpo_ref[...] = jnp.zeros_like(gpo_ref)
            ggw_ref[...] = jnp.zeros_like(ggw_ref)
            small_ref[...] = jnp.zeros_like(small_ref)
            ahead[...] = jnp.zeros_like(ahead)

        dh1 = dh1_ref[...]
        dh1_bf = _bf(dh1)
        gt = gt_ref[...]
        sc = sc_ref[...]
        dy = _nt(dh1_bf, wpo_ref[...])
        pooled_bf = []
        mixed = []
        for g in range(GROUPS):
            cols = slice(g * GROUP_DIM, (g + 1) * GROUP_DIM)
            pb = pooled_ref[:, cols]
            pooled_bf.append(pb)
            mixed.append(_nn(pb, _group_matrix(gw_ref, g)))
        mixed = jnp.concatenate(mixed, axis=-1) + gb_ref[...]
        sg = _sigmoid(gt)
        silu = gt * sg
        gpo_ref[...] += _tn(_bf(mixed * sc * silu), dh1_bf)
        dmixed = dy * sc * silu
        dgt = dy * mixed * sc * (sg * (1.0 + gt * (1.0 - sg)))
        dproj_ref[:, D:] = _bf(dgt)
        small_ref[1:2, :] += jnp.sum(dy * mixed * silu, axis=0, keepdims=True)
        small_ref[2:3, :] += jnp.sum(dmixed, axis=0, keepdims=True)

        inv = _inverse_counts(_row_index(nt - 1 - i, ts))
        rows_q = GROUP_DIM // N_CHIPS
        ebuf[ts:ts + HALO, :] = ahead[...]
        dpooled = []
        for g in range(GROUPS):
            cols = slice(g * GROUP_DIM, (g + 1) * GROUP_DIM)
            dm = _bf(dmixed[:, cols])
            ggw = _tn(pooled_bf[g], dm)
            for j in range(N_CHIPS):
                ggw_ref[j, rows_q * g:rows_q * (g + 1), :] += ggw[rows_q * j:rows_q * (j + 1), :]
            dp = _nt(dm, _group_matrix(gw_ref, g))
            dpooled.append(dp)
            ebuf[0:ts, cols] = dp * inv[g]
        ahead[...] = ebuf[0:HALO, :]
        du = []
        for g, w in enumerate(POOL_WINDOWS):
            cols = slice(g * GROUP_DIM, (g + 1) * GROUP_DIM)
            du.append(_leading_sums(ebuf, tbuf, cols, w, ts) - dpooled[g])
        du = _bf(jnp.concatenate(du, axis=-1))
        dproj_ref[:, :D] = du
        dgt_bf = _bf(dgt)
        half = D // 2
        dn0 = (_nt(du[:, :half], wpi_ref[0]) + _nt(du[:, half:], wpi_ref[1])
               + _nt(dgt_bf[:, :half], wpi_ref[2]) + _nt(dgt_bf[:, half:], wpi_ref[3]))

        xv = x_ref[...]
        r = lax.rsqrt(jnp.mean(xv * xv, axis=-1, keepdims=True) + EPS)
        xhat = xv * r
        small_ref[0:1, :] += jnp.sum(dn0 * xhat, axis=0, keepdims=True)
        dxh = dn0 * w0_ref[...]
        dx_ref[...] = dh1 + r * (dxh - xhat * jnp.mean(dxh * xhat, axis=-1, keepdims=True))

        @pl.when(i == nt - 1)
        def _():
            for cp in copies:
                cp.wait()

    row = lambda cols: pl.BlockSpec((ts, cols), lambda i: (nt - 1 - i, 0))
    outs = pl.pallas_call(
        body, name="pool_backward", grid=(nt,),
        out_shape=[jax.ShapeDtypeStruct((s, D), F32), jax.ShapeDtypeStruct((s, 2 * D), BF16),
                   jax.ShapeDtypeStruct((D, D), F32),
                   jax.ShapeDtypeStruct((GROUPS, GROUP_DIM, GROUP_DIM), F32),
                   jax.ShapeDtypeStruct((8, D), F32)] + _scatter_shapes(chip_sums),
        in_specs=[row(D), row(D), row(D), row(D), _full((1, D)), _full((N_CHIPS, D, D // 2)),
                  _full((GROUPS, GROUP_DIM, GROUP_DIM)), _full((1, D)), _full((1, D)), _full((D, D))]
                 + _any_specs(n_sums),
        out_specs=[row(D), row(2 * D), _full((D, D)), _full((GROUPS, GROUP_DIM, GROUP_DIM)), _full((8, D))]
                  + _any_specs(n_sums),
        scratch_shapes=[pltpu.VMEM((ts + HALO, D), F32), pltpu.VMEM((ts + HALO, D), F32),
                        pltpu.VMEM((HALO, D), F32),
                        pltpu.SemaphoreType.DMA((3 * n_sums,)), pltpu.SemaphoreType.DMA((3 * n_sums,))],
        compiler_params=_params("arbitrary"),
    )(x, dh1, pooled, gt, w0, wpi, gw, gb, scale, wpo, *chip_sums)
    return outs[:5], outs[5:]


def gla_project(h1, w1, wgi_q, wgk, bgk, later):
    s = h1.shape[0]
    ts = ROW_TILE
    nt = s // ts
    assert nt >= 2
    n_later = len(later)

    def body(h_ref, w1_ref, wq_ref, wgk_ref, bgk_ref, *rest):
        rest = rest[n_later:]
        qk_ref, v_ref, gate_ref, low_ref, cum_ref, n1_ref = rest[:6]
        later_refs = rest[6:6 + n_later]
        send_sems, recv_sems, wgi_ref = rest[6 + n_later:]
        gather_in_background(pl.program_id(0), nt - 1, later_refs, send_sems, recv_sems, finish=False)

        @pl.when(pl.program_id(0) == 0)
        def _():
            _assemble_gla_in(wq_ref, wgi_ref)

        hv = h_ref[...]
        r = lax.rsqrt(jnp.mean(hv * hv, axis=-1, keepdims=True) + EPS)
        n1 = _bf(hv * r * w1_ref[...])
        n1_ref[...] = n1
        qk_ref[...] = _nn(n1, wgi_ref[:, 0:2 * KEY_W])
        v_ref[...] = _bf(_nn(n1, wgi_ref[:, 2 * KEY_W:2 * KEY_W + D]))
        gate_ref[...] = _nn(n1, wgi_ref[:, 2 * KEY_W + D:GLA_MAIN])
        low = _bf(_nn(n1, wgi_ref[:, GLA_MAIN:]))
        low_ref[...] = low
        z = _nn(low, wgk_ref[...]) + bgk_ref[...]
        lg = (jnp.minimum(z, 0.0) - jnp.log(1.0 + jnp.exp(-jnp.abs(z)))) / GATE_NORM
        lower_f = _chunk_masks()[0].astype(F32)
        for r0 in range(0, ts, CHUNK):
            cum_ref[r0:r0 + CHUNK, :] = _nn_exact(lower_f, lg[r0:r0 + CHUNK, :])
        gather_in_background(pl.program_id(0), nt - 1, later_refs, send_sems, recv_sems, finish=True)

    row = lambda cols: pl.BlockSpec((ts, cols), lambda i: (i, 0))
    outs = pl.pallas_call(
        body, name="gla_project", grid=(nt,),
        out_shape=[jax.ShapeDtypeStruct((s, D), F32), jax.ShapeDtypeStruct((s, D), BF16),
                   jax.ShapeDtypeStruct((s, D), F32), jax.ShapeDtypeStruct((s, RANK_PAD), BF16),
                   jax.ShapeDtypeStruct((s, KEY_W), F32), jax.ShapeDtypeStruct((s, D), BF16)]
                  + [jax.ShapeDtypeStruct(a.shape, a.dtype) for a in later],
        in_specs=[row(D), _full((1, D)), _full((N_CHIPS, D, GLA_IN_QUARTER)),
                  _full((RANK_PAD, KEY_W)), _full((1, KEY_W))] + _any_specs(n_later),
        out_specs=[row(D), row(D), row(D), row(RANK_PAD), row(KEY_W), row(D)] + _any_specs(n_later),
        input_output_aliases={5 + k: 6 + k for k in range(n_later)},
        scratch_shapes=[pltpu.SemaphoreType.DMA((6 * n_later,)), pltpu.SemaphoreType.DMA((6 * n_later,)),
                        pltpu.VMEM((D, GLA_MAIN + RANK_PAD), BF16)],
        compiler_params=_params("arbitrary"),
    )(h1, w1, wgi_q, wgk, bgk, *later)
    return outs[:6], outs[6:]


def _assemble_gla_in(wq_ref, wfull):
    pad = jnp.zeros((CAST_ROWS, GLA_MAIN + RANK_PAD - GLA_IN), BF16)
    for r0 in range(0, D, CAST_ROWS):
        rows = slice(r0, r0 + CAST_ROWS)
        wfull[rows, :] = jnp.concatenate([wq_ref[q, rows, :] for q in range(N_CHIPS)] + [pad], axis=1)


GLA_BLOCK = 512
CHUNKS_PER_BLOCK = GLA_BLOCK // CHUNK


def _chunk_masks():
    t = lax.broadcasted_iota(jnp.int32, (CHUNK, CHUNK), 0)
    u = lax.broadcasted_iota(jnp.int32, (CHUNK, CHUNK), 1)
    return t >= u, t <= u


def _gla_chunk_terms(q, cum):
    ep = jnp.exp(cum)
    en = jnp.exp(-cum)
    qs = q * (HEAD_K ** -0.5)
    last = cum[CHUNK - 1:CHUNK, :]
    ed = jnp.exp(last - cum)
    dec = jnp.exp(last)
    return ep, en, qs, ed, dec


def gla_forward(qk, v, cum):
    s = qk.shape[0]
    nb = s // GLA_BLOCK
    nc = s // CHUNK

    def body(q_ref, k_ref, v_ref, cum_ref, o_ref, st_ref, sc_ref, state):
        @pl.when(pl.program_id(0) == 0)
        def _():
            state[...] = jnp.zeros_like(state)

        lower, _ = _chunk_masks()

        def chunk(cc, carry):
            rows = pl.ds(pl.multiple_of(cc * CHUNK, CHUNK), CHUNK)
            for h in range(HEADS):
                kc = slice(h * HEAD_K, (h + 1) * HEAD_K)
                vc = slice(h * HEAD_V, (h + 1) * HEAD_V)
                q = q_ref[rows, kc]
                k = k_ref[rows, kc]
                v = v_ref[rows, vc]
                ep, en, qs, ed, dec = _gla_chunk_terms(q, cum_ref[rows, kc])
                a = _bf(qs * ep)
                fwd = _nt(a, _bf(k * en))
                bwd = _nt(_bf(qs * en), _bf(k * ep))
                scores = _bf(jnp.where(lower, fwd, bwd))
                sc_ref[rows, h * CHUNK:(h + 1) * CHUNK] = scores
                st = state[h]
                st_ref[cc, h] = st
                o_ref[rows, vc] = _nn(scores, v) + _nt(a, _bf(st))
                state[h] = st * dec + _tn(v, _bf(k * ed))
            return carry

        lax.fori_loop(0, CHUNKS_PER_BLOCK, chunk, 0, unroll=True)

    return pl.pallas_call(
        body, name="gla_forward", grid=(nb,),
        out_shape=(jax.ShapeDtypeStruct((s, D), F32),
                   jax.ShapeDtypeStruct((nc, HEADS, HEAD_V, HEAD_K), F32),
                   jax.ShapeDtypeStruct((s, HEADS * CHUNK), BF16)),
        in_specs=[pl.BlockSpec((GLA_BLOCK, KEY_W), lambda i: (i, 0)),
                  pl.BlockSpec((GLA_BLOCK, KEY_W), lambda i: (i, 1)),
                  pl.BlockSpec((GLA_BLOCK, D), lambda i: (i, 0)),
                  pl.BlockSpec((GLA_BLOCK, KEY_W), lambda i: (i, 0))],
        out_specs=(pl.BlockSpec((GLA_BLOCK, D), lambda i: (i, 0)),
                   pl.BlockSpec((CHUNKS_PER_BLOCK, HEADS, HEAD_V, HEAD_K), lambda i: (i, 0, 0, 0)),
                   pl.BlockSpec((GLA_BLOCK, HEADS * CHUNK), lambda i: (i, 0))),
        scratch_shapes=[pltpu.VMEM((HEADS, HEAD_V, HEAD_K), F32)],
        compiler_params=_params("arbitrary"),
    )(qk, qk, v, cum)


def gla_backward(qk, v, cum, do, states, scores):
    s = qk.shape[0]
    nb = s // GLA_BLOCK

    def body(q_ref, k_ref, v_ref, cum_ref, do_ref, st_ref, sc_ref, dq_ref, dk_ref, dv_ref, dcum_ref, dstate):
        @pl.when(pl.program_id(0) == 0)
        def _():
            dstate[...] = jnp.zeros_like(dstate)

        lower, _ = _chunk_masks()
        is_last = lax.broadcasted_iota(jnp.int32, (CHUNK, HEAD_K), 0) == CHUNK - 1

        def chunk(step, carry):
            cc = CHUNKS_PER_BLOCK - 1 - step
            rows = pl.ds(pl.multiple_of(cc * CHUNK, CHUNK), CHUNK)
            for h in range(HEADS):
                kc = slice(h * HEAD_K, (h + 1) * HEAD_K)
                vc = slice(h * HEAD_V, (h + 1) * HEAD_V)
                q = q_ref[rows, kc]
                k = k_ref[rows, kc]
                v = v_ref[rows, vc]
                do_c = do_ref[rows, vc]
                ep, en, qs, ed, dec = _gla_chunk_terms(q, cum_ref[rows, kc])
                a = _bf(qs * ep)
                b = _bf(k * en)
                c = _bf(qs * en)
                dk_dec = _bf(k * ep)
                kd = _bf(k * ed)
                scores = sc_ref[rows, h * CHUNK:(h + 1) * CHUNK]
                st = st_ref[cc, h]
                dst = dstate[h]
                dst_bf = _bf(dst)

                dscores = _nt(do_c, v)
                dfwd = _bf(jnp.where(lower, dscores, 0.0))
                dbwd = _bf(jnp.where(lower, 0.0, dscores))
                dv_ref[rows, vc] = _bf(_tn(scores, do_c) + _nt(kd, dst_bf))
                da = _nn(dfwd, b) + _nn(do_c, _bf(st))
                db = _tn(dfwd, a)
                dc = _nn(dbwd, dk_dec)
                ddk = _tn(dbwd, c)
                dkd = _nn(v, dst_bf)
                ddec = jnp.sum(dst * st, axis=0, keepdims=True)
                dstate[h] = dst * dec + _tn(do_c, a)

                m = dkd * k * ed
                dq_ref[rows, kc] = _bf((da * ep + dc * en) * (HEAD_K ** -0.5))
                dk_ref[rows, kc] = _bf(db * en + ddk * ep + dkd * ed)
                dcum = (da * qs + ddk * k) * ep - (db * k + dc * qs) * en - m
                dlast = jnp.sum(m, axis=0, keepdims=True) + ddec * dec
                dcum_ref[rows, kc] = dcum + jnp.where(is_last, dlast, 0.0)
            return carry

        lax.fori_loop(0, CHUNKS_PER_BLOCK, chunk, 0, unroll=True)

    rev = lambda cols, col_block: pl.BlockSpec((GLA_BLOCK, cols), lambda i: (nb - 1 - i, col_block))
    return pl.pallas_call(
        body, name="gla_backward", grid=(nb,),
        out_shape=(jax.ShapeDtypeStruct((s, KEY_W), BF16), jax.ShapeDtypeStruct((s, KEY_W), BF16),
                   jax.ShapeDtypeStruct((s, D), BF16), jax.ShapeDtypeStruct((s, KEY_W), F32)),
        in_specs=[rev(KEY_W, 0), rev(KEY_W, 1), rev(D, 0), rev(KEY_W, 0), rev(D, 0),
                  pl.BlockSpec((CHUNKS_PER_BLOCK, HEADS, HEAD_V, HEAD_K), lambda i: (nb - 1 - i, 0, 0, 0)),
                  rev(HEADS * CHUNK, 0)],
        out_specs=(rev(KEY_W, 0), rev(KEY_W, 0), rev(D, 0), rev(KEY_W, 0)),
        scratch_shapes=[pltpu.VMEM((HEADS, HEAD_V, HEAD_K), F32)],
        compiler_params=_params("arbitrary"),
    )(qk, qk, v, cum, do, states, scores)


def head_and_loss(o, gate, h1, target, hw, wgo, wf):
    s = o.shape[0]
    ts = ROW_TILE

    def body(o_ref, gate_ref, h1_ref, tgt_ref, hw_ref, wgo_ref, wf_ref,
             dh2_ref, do_ref, dgate_ref, ggo_ref, small_ref):
        @pl.when(pl.program_id(0) == 0)
        def _():
            ggo_ref[...] = jnp.zeros_like(ggo_ref)
            small_ref[...] = jnp.zeros_like(small_ref)

        gate = gate_ref[...]
        hw = hw_ref[...]
        sg = _sigmoid(gate)
        silu = gate * sg
        ohat, ro = [], []
        for h in range(HEADS):
            oh = o_ref[:, h * HEAD_V:(h + 1) * HEAD_V]
            rh = lax.rsqrt(jnp.mean(oh * oh, axis=-1, keepdims=True) + EPS)
            ro.append(rh)
            ohat.append(oh * rh)
        ohat = jnp.concatenate(ohat, axis=-1)
        on = ohat * hw
        y2 = _bf(on * silu)
        h2 = h1_ref[...] + _nn(y2, wgo_ref[...])
        rf = lax.rsqrt(jnp.mean(h2 * h2, axis=-1, keepdims=True) + EPS)
        h2hat = h2 * rf
        wf = wf_ref[...]
        diff = h2hat * wf - tgt_ref[...]
        small_ref[2:3, :] += jnp.zeros((1, D), F32) + 0.5 * jnp.sum(diff * diff) / D
        small_ref[0:1, :] += jnp.sum(diff * h2hat, axis=0, keepdims=True) / D
        dxh = diff * wf
        dh2 = (rf / D) * (dxh - h2hat * jnp.mean(dxh * h2hat, axis=-1, keepdims=True))
        dh2_ref[...] = dh2
        dh2_bf = _bf(dh2)
        ggo_ref[...] += _tn(y2, dh2_bf)
        dy2 = _nt(dh2_bf, wgo_ref[...])
        don = dy2 * silu
        dgate_ref[...] = _bf(dy2 * on * (sg * (1.0 + gate * (1.0 - sg))))
        ghw = jnp.sum(don * ohat, axis=0, keepdims=True)
        small_ref[1:2, 0:HEAD_V] += sum(ghw[:, h * HEAD_V:(h + 1) * HEAD_V] for h in range(HEADS))
        dohat = don * hw
        for h in range(HEADS):
            cols = slice(h * HEAD_V, (h + 1) * HEAD_V)
            oh, dh = ohat[:, cols], dohat[:, cols]
            do_ref[:, cols] = _bf(ro[h] * (dh - oh * jnp.mean(dh * oh, axis=-1, keepdims=True)))

    row = lambda cols: pl.BlockSpec((ts, cols), lambda i: (i, 0))
    act = jax.ShapeDtypeStruct((s, D), F32)
    act_bf = jax.ShapeDtypeStruct((s, D), BF16)
    return pl.pallas_call(
        body, name="head_and_loss", grid=(s // ts,),
        out_shape=(act, act_bf, act_bf, jax.ShapeDtypeStruct((D, D), F32), jax.ShapeDtypeStruct((8, D), F32)),
        in_specs=[row(D), row(D), row(D), row(D),
                  _full((1, D)), _full((D, D)), _full((1, D))],
        out_specs=(row(D), row(D), row(D), _full((D, D)), _full((8, D))),
        compiler_params=_params("arbitrary"),
    )(o, gate, h1, target, hw, wgo, wf)


def gla_project_backward(dq, dk, dv, dgate, dcum, low, h1, dh2, w1, wgi_q, wgk, bgk):
    s = h1.shape[0]
    ts = ROW_TILE

    def body(dq_ref, dk_ref, dv_ref, dgate_ref, dcum_ref, low_ref, h1_ref, dh2_ref, w1_ref,
             wq_ref, wgk_ref, bgk_ref, dh1_ref, dproj_ref, ggk_ref, small_ref, wgi_ref):
        @pl.when(pl.program_id(0) == 0)
        def _():
            ggk_ref[...] = jnp.zeros_like(ggk_ref)
            small_ref[...] = jnp.zeros_like(small_ref)
            _assemble_gla_in(wq_ref, wgi_ref)

        low = low_ref[...]
        z = _nn(low, wgk_ref[...]) + bgk_ref[...]
        upper_f = _chunk_masks()[1].astype(F32)
        dlg = jnp.concatenate([_nn_exact(upper_f, dcum_ref[r0:r0 + CHUNK, :]) for r0 in range(0, ts, CHUNK)],
                              axis=0)
        dz = dlg * (1.0 / GATE_NORM) * _sigmoid(-z)
        dz_bf = _bf(dz)
        ggk_ref[...] += _tn(low, dz_bf)
        small_ref[1:2, 0:KEY_W] += jnp.sum(dz, axis=0, keepdims=True)
        dlow = _bf(_nt(dz_bf, wgk_ref[...]))
        dproj_ref[:, GLA_MAIN:] = dlow
        dn1 = _nt(dlow, wgi_ref[:, GLA_MAIN:])
        for ref, lo, hi in ((dq_ref, 0, KEY_W), (dk_ref, KEY_W, 2 * KEY_W),
                            (dv_ref, 2 * KEY_W, 2 * KEY_W + D), (dgate_ref, 2 * KEY_W + D, GLA_MAIN)):
            piece = ref[...]
            dproj_ref[:, lo:hi] = piece
            dn1 = dn1 + _nt(piece, wgi_ref[:, lo:hi])
        hv = h1_ref[...]
        r = lax.rsqrt(jnp.mean(hv * hv, axis=-1, keepdims=True) + EPS)
        hhat = hv * r
        small_ref[0:1, :] += jnp.sum(dn1 * hhat, axis=0, keepdims=True)
        dxh = dn1 * w1_ref[...]
        dh1_ref[...] = dh2_ref[...] + r * (dxh - hhat * jnp.mean(dxh * hhat, axis=-1, keepdims=True))

    row = lambda cols: pl.BlockSpec((ts, cols), lambda i: (i, 0))
    return pl.pallas_call(
        body, name="gla_project_backward", grid=(s // ts,),
        out_shape=(jax.ShapeDtypeStruct((s, D), F32), jax.ShapeDtypeStruct((s, GLA_MAIN + RANK_PAD), BF16),
                   jax.ShapeDtypeStruct((RANK_PAD, KEY_W), F32),
                   jax.ShapeDtypeStruct((8, D), F32)),
        in_specs=[row(KEY_W), row(KEY_W), row(D), row(D), row(KEY_W), row(RANK_PAD), row(D), row(D),
                  _full((1, D)), _full((N_CHIPS, D, GLA_IN_QUARTER)), _full((RANK_PAD, KEY_W)),
                  _full((1, KEY_W))],
        out_specs=(row(D), row(GLA_MAIN + RANK_PAD), _full((RANK_PAD, KEY_W)), _full((8, D))),
        scratch_shapes=[pltpu.VMEM((D, GLA_MAIN + RANK_PAD), BF16)],
        compiler_params=_params("arbitrary"),
    )(dq, dk, dv, dgate, dcum, low, h1, dh2, w1, wgi_q, wgk, bgk)


def local_gradients(xs, target, w0, w1, wf, wpi, gw, gb, scale, wpo, gla_quarters, wgk, bgk, hw_tiled, place):
    wgi_q, wgo_q = gla_quarters
    (h1, pooled, gt, n0), (wgi_q,) = pool_forward(xs, w0, wpi, gw, gb, scale, wpo, [wgi_q])
    (qk, v, gate, low, cum, n1), (wgo_q,) = gla_project(h1, w1, wgi_q, wgk, bgk, [wgo_q])
    wgo = wgo_q.reshape(D, D)
    o, states, scores = gla_forward(qk, v, cum)

    dh2, do, dgate, g_gla_out, small_top = head_and_loss(o, gate, h1, target, hw_tiled, wgo, wf)
    dq, dk, dv, dcum = gla_backward(qk, v, cum, do, states, scores)
    dh1, dproj, g_gk_pad, small_gla = gla_project_backward(
        dq, dk, dv, dgate, dcum, low, h1, dh2, w1, wgi_q, wgk, bgk)
    g_gla_in, _ = matmul_tn(n1, dproj, "grad_gla_in", (GLA_MAIN + RANK_PAD) // 5, place)

    gla_sums = add_halves([g_gla_in, g_gla_out.reshape(N_CHIPS, D // N_CHIPS, D)], place, "add_halves_gla")
    (dx, dpool, g_pool_out, g_group_w, small_pool), gla_got = pool_backward(
        xs, dh1, pooled, gt, w0, wpi, gw, gb, scale, wpo, [b for _, b in gla_sums])
    g_pool_in, mix = matmul_tn(n0, dpool, "grad_pool_in", D // 2, place, by_column_tile=True,
                               also_reduce=[g_group_w, g_pool_out.reshape(N_CHIPS, D // N_CHIPS, D)])

    reduced, total = join_halves(
        g_pool_in, place, [own for own, _ in mix] + [f for f, _ in gla_sums], [got for _, got in mix] + list(gla_got),
        small_pool, small_gla, small_top, g_gk_pad)
    return dx, reduced, total


def kernel(x, norm_w, pool_in_w, pool_group_w, pool_group_b, pool_scale, pool_out_w, gla_in_w, gla_gk_w, gla_gk_b, gla_head_norm_w, gla_out_w, final_norm_w, loss_target, m_norm_w, m_pool_in_w, m_pool_group_w, m_pool_group_b, m_pool_scale, m_pool_out_w, m_gla_in_w, m_gla_gk_w, m_gla_gk_b, m_gla_head_norm_w, m_gla_out_w, m_final_norm_w, v_norm_w, v_pool_in_w, v_pool_group_w, v_pool_group_b, v_pool_scale, v_pool_out_w, v_gla_in_w, v_gla_gk_w, v_gla_gk_b, v_gla_head_norm_w, v_gla_out_w, v_final_norm_w):
    xs = x[0]
    target = loss_target[0]
    q_chip = 2 * lax.axis_index("x") + lax.axis_index("y")
    place = jnp.stack([lax.axis_index("c"), q_chip]).astype(jnp.int32)

    (wpi, gw_q, wpo_q, wgi_q, wgo_q), (bgk, hw_tiled, gb, wgk) = allgather_weights(
        [pool_in_w[0], pool_group_w[0].reshape(GROUP_DIM, GROUP_DIM), pool_out_w[0], gla_in_w[0], gla_out_w[0]],
        exchange=(True, True, True, False, False),
        smalls=[gla_gk_b, gla_head_norm_w, pool_group_b[0], gla_gk_w[0]])
    wpo = wpo_q.reshape(D, D)

    w0 = norm_w[0:1]
    w1 = norm_w[1:2]
    wf = final_norm_w.reshape(1, D)

    dx, reduced, total = local_gradients(
        xs, target, w0, w1, wf, wpi, gw_q, gb, pool_scale, wpo, [wgi_q, wgo_q], wgk, bgk, hw_tiled, place)
    r_pool_in, r_group_w, r_pool_out, r_gla_in, r_gla_out = reduced
    r_group_w = r_group_w.reshape(GROUPS, 64, GROUP_DIM)

    turn = lambda a: jnp.transpose(a, (2, 0, 1))
    back = lambda a: jnp.transpose(a, (1, 2, 0))
    as2d = lambda a, w: a.reshape(-1, w.shape[-1])
    big_names = ("pool_in_w", "pool_group_w", "pool_out_w", "gla_in_w", "gla_out_w")
    big_args = [(pool_in_w, r_pool_in[None], m_pool_in_w, v_pool_in_w),
                (pool_group_w, r_group_w[None], m_pool_group_w, v_pool_group_w),
                (pool_out_w, r_pool_out[None], m_pool_out_w, v_pool_out_w),
                (gla_in_w, r_gla_in[None], m_gla_in_w, v_gla_in_w),
                (gla_out_w, r_gla_out[None], m_gla_out_w, v_gla_out_w)]
    to_kernel = lambda n, a, w: turn(a) if n == "gla_in_w" else as2d(a, w)
    from_kernel = lambda n, a, w: back(a) if n == "gla_in_w" else a.reshape(w.shape)
    big_in = [tuple(to_kernel(n, a, p[0]) for a in p) for n, p in zip(big_names, big_args)]
    big_out = adamw(big_in, "adamw", echo=[n != "gla_in_w" for n in big_names])
    big = {}
    for n, p, i, out in zip(big_names, big_args, big_in, big_out):
        g = out[3] if len(out) == 4 else i[1]
        big[n] = tuple(from_kernel(n, o, p[0]) for o in (g, *out[:3]))

    small_names = ("norm_w", "pool_group_b", "pool_scale", "gla_gk_w", "gla_gk_b", "gla_head_norm_w",
                   "final_norm_w")
    small_args = [(norm_w, m_norm_w, v_norm_w),
                  (pool_group_b, m_pool_group_b, v_pool_group_b),
                  (pool_scale, m_pool_scale, v_pool_scale),
                  (gla_gk_w, m_gla_gk_w, v_gla_gk_w),
                  (gla_gk_b, m_gla_gk_b, v_gla_gk_b),
                  (gla_head_norm_w, m_gla_head_norm_w, v_gla_head_norm_w),
                  (final_norm_w, m_final_norm_w, v_final_norm_w)]
    small_out, loss = adamw_small([tuple(as2d(a, p[0]) for a in p) for p in small_args], total, place)
    small = {n: tuple(o.reshape(p[0].shape) for o in out) for n, p, out in zip(small_names, small_args, small_out)}
    results = [
        small["norm_w"],
        big["pool_in_w"],
        big["pool_group_w"],
        small["pool_group_b"],
        small["pool_scale"],
        big["pool_out_w"],
        big["gla_in_w"],
        small["gla_gk_w"],
        small["gla_gk_b"],
        small["gla_head_norm_w"],
        big["gla_out_w"],
        small["final_norm_w"],
    ]
    grads, deltas, new_m, new_v = zip(*results)
    return (loss.reshape(()), dx[None], *grads, *deltas, *new_m, *new_v)
```

```python
import jax
import jax.numpy as jnp
from jax import lax
from jax.experimental import pallas as pl
from jax.experimental.pallas import tpu as pltpu

F32 = jnp.float32
BF16 = jnp.bfloat16
MESH = pl.DeviceIdType.MESH

D = 1024
POOL_WINDOWS = (2, 4, 8, 16)
GROUPS = 4
GROUP_DIM = 256
HEADS = 4
HEAD_K = 128
HEAD_V = 256
KEY_W = 512
CHUNK = 64
GATE_RANK = 16
GATE_NORM = 16.0
GLA_IN = 3088
GLA_MAIN = 3072
RANK_PAD = 128
EPS = 1e-6
HALO = 32

ADAM_LR = 0.001
ADAM_B1 = 0.9
ADAM_B2 = 0.999
ADAM_EPS = 1e-08
ADAM_WD = 0.01
ADAM_STEP = 10

N_CHIPS = 4
N_DEV = 8
GLA_IN_QUARTER = GLA_IN // N_CHIPS

VMEM_LIMIT = 56 * 1024 * 1024


def _nn(a, b):
    return lax.dot_general(a, b, (((1,), (0,)), ((), ())), preferred_element_type=F32)


def _nt(a, b):
    return lax.dot_general(a, b, (((1,), (1,)), ((), ())), preferred_element_type=F32)


def _tn(a, b):
    return lax.dot_general(a, b, (((0,), (0,)), ((), ())), preferred_element_type=F32)


def _nn_exact(a, b):
    return lax.dot_general(a, b, (((1,), (0,)), ((), ())), preferred_element_type=F32,
                           precision=lax.Precision.HIGHEST)


def _bf(a):
    return a.astype(BF16)


def _params(*sem):
    return pltpu.CompilerParams(dimension_semantics=sem, vmem_limit_bytes=VMEM_LIMIT)


def _full(shape):
    return pl.BlockSpec(shape, lambda i: (0,) * len(shape))


def _position():
    return lax.axis_index("x"), lax.axis_index("y"), lax.axis_index("c")


def _gather_small(in_ref, all_ref, send_sems, recv_sems, local_sem):
    x, y, c = _position()
    me = 4 * x + 2 * y + c
    mine = pltpu.make_async_copy(in_ref, all_ref.at[me], local_sem)
    sends = []
    for k in range(N_DEV - 1):
        fx, fy, fc = (k + 1) >> 2 & 1, (k + 1) >> 1 & 1, (k + 1) & 1
        sends.append(pltpu.make_async_remote_copy(
            src_ref=in_ref, dst_ref=all_ref.at[me],
            send_sem=send_sems.at[k], recv_sem=recv_sems.at[k],
            device_id=(x ^ fx, y ^ fy, c ^ fc), device_id_type=MESH))

    def start():
        mine.start()
        for cp in sends:
            cp.start()

    def wait():
        for k in range(N_DEV - 1):
            fx, fy, fc = (k + 1) >> 2 & 1, (k + 1) >> 1 & 1, (k + 1) & 1
            src_dev = 4 * (x ^ fx) + 2 * (y ^ fy) + (c ^ fc)
            pltpu.make_async_remote_copy(
                src_ref=in_ref, dst_ref=all_ref.at[src_dev],
                send_sem=send_sems.at[k], recv_sem=recv_sems.at[k],
                device_id=(x, y, c), device_id_type=MESH).wait_recv()
        for cp in sends:
            cp.wait_send()
        mine.wait()

    return start, wait


SMALL_SEMS = [pltpu.SemaphoreType.DMA((N_DEV - 1,)), pltpu.SemaphoreType.DMA((N_DEV - 1,)),
              pltpu.SemaphoreType.DMA]
VMEM_SPEC = pl.BlockSpec(memory_space=pltpu.VMEM)


def _other_chips(x, y):
    return [(1 - x, y), (x, 1 - y), (1 - x, 1 - y)]


def _any_specs(n):
    return [pl.BlockSpec(memory_space=pl.ANY)] * n


def _halves(rows, c):
    half = rows // 2
    return pl.ds(c * half, half), pl.ds((1 - c) * half, half)


CAST_ROWS = 256


def _gather_copy(out_ref, send_sems, recv_sems, k, quarter, half, to, src=None):
    dst = out_ref.at[quarter, half]
    return pltpu.make_async_remote_copy(
        src_ref=dst if src is None else src, dst_ref=dst,
        send_sem=send_sems.at[k], recv_sem=recv_sems.at[k], device_id=to, device_id_type=MESH)


SMALL_IN_ROWS = 24


def allgather_weights(quarters, exchange, smalls):
    n = len(quarters)
    shapes = [w.shape for w in quarters]
    moved = [i for i in range(n) if exchange[i]]

    def body(*refs):
        w_refs, (gkb_ref, hnw_ref, gb_ref, gkw_ref) = refs[:n], refs[n:n + 4]
        out_refs, (bgk_ref, hw_ref, gbias_ref, wgk_ref) = refs[n + 4:2 * n + 4], refs[2 * n + 4:2 * n + 8]
        refs = refs[2 * n + 8:]
        f32_bufs, bf_bufs = refs[:n], refs[n:2 * n]
        send_sems, recv_sems, local_sems, small_ref, small_all_ref = refs[2 * n:2 * n + 5]
        small_ref[...] = jnp.zeros_like(small_ref)
        small_ref[0:1, :] = gkb_ref[...]
        small_ref[1:2, 0:64] = hnw_ref[...]
        small_ref[2:2 + GROUPS, 0:64] = gb_ref[...]
        small_ref[8:8 + GATE_RANK, :] = gkw_ref[...]
        start_small, wait_small = _gather_small(small_ref, small_all_ref, *refs[2 * n + 5:])
        start_small()
        x, y, c = _position()
        q = 2 * x + y
        sibling = (x, y, 1 - c)
        chips = _other_chips(x, y)

        def copy(k, i, quarter, half, to, src=None):
            return _gather_copy(out_refs[i], send_sems, recv_sems, k * n + i, quarter, half, to, src)

        loads = [pltpu.make_async_copy(w_refs[i], f32_bufs[i], local_sems.at[i]) for i in range(n)]
        for cp in loads:
            cp.start()
        keeps, sends = [], []
        for i in range(n):
            loads[i].wait()
            for r0 in range(0, shapes[i][0], CAST_ROWS):
                bf_bufs[i][r0:r0 + CAST_ROWS, :] = _bf(f32_bufs[i][r0:r0 + CAST_ROWS, :])
            keep = pltpu.make_async_copy(bf_bufs[i], out_refs[i].at[q], local_sems.at[n + i])
            keep.start()
            keeps.append(keep)
            if not exchange[i]:
                continue
            mine, _ = _halves(shapes[i][0], c)
            for j, chip in enumerate(chips):
                cp = copy(j, i, q, mine, (*chip, c), src=bf_bufs[i].at[mine])
                cp.start()
                sends.append(cp)
        for j, chip in enumerate(chips):
            qj = 2 * chip[0] + chip[1]
            for i in moved:
                mine, _ = _halves(shapes[i][0], c)
                copy(j, i, qj, mine, (x, y, c)).wait_recv()
                cp = copy(3 + j, i, qj, mine, sibling)
                cp.start()
                sends.append(cp)
        for j, chip in enumerate(chips):
            qj = 2 * chip[0] + chip[1]
            for i in moved:
                _, other = _halves(shapes[i][0], c)
                copy(3 + j, i, qj, other, (x, y, c)).wait_recv()
        wait_small()
        wgk_ref[...] = jnp.zeros_like(wgk_ref)
        for j in range(N_CHIPS):
            block = small_all_ref.at[2 * j]
            bgk_ref[:, 128 * j:128 * (j + 1)] = block[0:1, :]
            for h in range(HEADS):
                hw_ref[:, HEAD_V * h + 64 * j:HEAD_V * h + 64 * (j + 1)] = block[1:2, 0:64]
            for g in range(GROUPS):
                gbias_ref[:, GROUP_DIM * g + 64 * j:GROUP_DIM * g + 64 * (j + 1)] = block[2 + g:3 + g, 0:64]
            wgk_ref[0:GATE_RANK, 128 * j:128 * (j + 1)] = _bf(block[8:8 + GATE_RANK, :])
        for cp in sends:
            cp.wait_send()
        for cp in keeps:
            cp.wait()

    outs = pl.pallas_call(
        body, name="allgather_weights",
        out_shape=[jax.ShapeDtypeStruct((N_CHIPS, *s), BF16) for s in shapes]
                  + [jax.ShapeDtypeStruct((1, KEY_W), F32), jax.ShapeDtypeStruct((1, D), F32),
                     jax.ShapeDtypeStruct((1, D), F32), jax.ShapeDtypeStruct((RANK_PAD, KEY_W), BF16)],
        in_specs=_any_specs(n) + [VMEM_SPEC] * 4, out_specs=_any_specs(n) + [VMEM_SPEC] * 4,
        scratch_shapes=([pltpu.VMEM(s, F32) for s in shapes] + [pltpu.VMEM(s, BF16) for s in shapes]
                        + [pltpu.SemaphoreType.DMA((6 * n,)), pltpu.SemaphoreType.DMA((6 * n,)),
                           pltpu.SemaphoreType.DMA((2 * n,)), pltpu.VMEM((SMALL_IN_ROWS, 128), F32),
                           pltpu.VMEM((N_DEV, SMALL_IN_ROWS, 128), F32)] + SMALL_SEMS),
        compiler_params=pltpu.CompilerParams(vmem_limit_bytes=VMEM_LIMIT),
    )(*quarters, *smalls)
    return outs[:n], outs[n:]


def _scatter_copies(b_refs, got_refs, send_sems, recv_sems):
    n = len(b_refs)
    x, y, c = _position()
    copies = []
    for j, chip in enumerate(_other_chips(x, y)):
        qj = 2 * chip[0] + chip[1]
        for i in range(n):
            copies.append(pltpu.make_async_remote_copy(
                src_ref=b_refs[i].at[qj], dst_ref=got_refs[i].at[j],
                send_sem=send_sems.at[j * n + i], recv_sem=recv_sems.at[j * n + i],
                device_id=(*chip, c), device_id_type=MESH))
    return copies


def _scatter_shapes(chip_sums):
    return [jax.ShapeDtypeStruct((N_CHIPS - 1, *b.shape[1:]), BF16) for b in chip_sums]


ADD_ROWS = 512
ADD_HALVES_ROWS = 128


def _spans(counts):
    starts, total = [], 0
    for count in counts:
        starts.append(total)
        total += count
    return starts, total


def _local_step(t, start, count):
    return jnp.clip(t - start, 0, count - 1)


def add_halves(grads, place, name):
    n = len(grads)
    whole = [len(g.shape) == 2 for g in grads]
    halves = [g.shape[-2] // 2 for g in grads]
    cols = [GLA_IN_QUARTER if w else g.shape[-1] for g, w in zip(grads, whole)]
    rbs = [min(ADD_HALVES_ROWS, h) for h in halves]
    counts = [h // rb for h, rb in zip(halves, rbs)]
    starts, total = _spans(counts)
    half_shapes = [(*g.shape[:-2], h, g.shape[-1]) for g, h in zip(grads, halves)]

    def rows_of(ref, i, start):
        return ref.at[pl.ds(start, rbs[i])] if whole[i] else ref.at[:, pl.ds(start, rbs[i])]

    def body(place_ref, *refs):
        a_refs, o_refs = refs[:n], refs[n:2 * n]
        f_refs, h_refs = refs[2 * n:3 * n], refs[3 * n:4 * n]
        send_refs, their_refs, (send_sems, recv_sems) = refs[4 * n:5 * n], refs[5 * n:6 * n], refs[6 * n:]
        t = pl.program_id(0)
        q = place_ref[1]
        x, y, c = _position()
        copies = [[pltpu.make_async_remote_copy(
            src_ref=rows_of(send_refs[i], i, k * rbs[i]), dst_ref=rows_of(their_refs[i], i, k * rbs[i]),
            send_sem=send_sems.at[starts[i] + k], recv_sem=recv_sems.at[starts[i] + k],
            device_id=(x, y, 1 - c), device_id_type=MESH) for k in range(counts[i])] for i in range(n)]

        for i in range(n):
            for k in range(counts[i]):
                @pl.when(t == starts[i] + k)
                def _(i=i, k=k):
                    rows_of(send_refs[i], i, k * rbs[i])[...] = _bf(o_refs[i][...])
                    copies[i][k].start()

        for i in range(n):
            for k in range(counts[i]):
                @pl.when(t == starts[i] + k + 1)
                def _(i=i, k=k):
                    copies[i][k].wait_recv()
                    b_ref = rows_of(their_refs[i], i, k * rbs[i])
                    if not whole[i]:
                        h_refs[i][...] = _bf(a_refs[i][...] + b_ref[...].astype(F32))
                        f_refs[i][...] = a_refs[i][q] + b_ref[q].astype(F32)
                        return
                    total_i = a_refs[i][...] + b_ref[...].astype(F32)
                    for k4 in range(N_CHIPS):
                        piece = total_i[:, k4 * cols[i]:(k4 + 1) * cols[i]]
                        h_refs[i][k4] = _bf(piece)

                        @pl.when(q == k4)
                        def _():
                            f_refs[i][...] = piece

        @pl.when(t == total)
        def _():
            for of_matrix in copies:
                for cp in of_matrix:
                    cp.wait_send()

    def specs(i):
        sent = lambda t: _local_step(t, starts[i], counts[i])
        added = lambda t: _local_step(t - 1, starts[i], counts[i])
        by_quarter = (N_CHIPS, rbs[i], cols[i])
        block = (rbs[i], grads[i].shape[-1]) if whole[i] else by_quarter
        lead = () if whole[i] else (0,)
        mine = pl.BlockSpec(block, lambda t, place: (*lead, place[0] * counts[i] + added(t), 0))
        other = pl.BlockSpec(block, lambda t, place: (*lead, (1 - place[0]) * counts[i] + sent(t), 0))
        sums = pl.BlockSpec(by_quarter, lambda t, place: (0, added(t), 0))
        own = pl.BlockSpec(by_quarter[1:], lambda t, place: (added(t), 0))
        return mine, other, own, sums

    all_specs = [specs(i) for i in range(n)]
    outs = pl.pallas_call(
        body, name=name,
        grid_spec=pltpu.PrefetchScalarGridSpec(
            num_scalar_prefetch=1, grid=(total + 1,),
            in_specs=[sp[0] for sp in all_specs] + [sp[1] for sp in all_specs],
            out_specs=[sp[2] for sp in all_specs] + [sp[3] for sp in all_specs],
            scratch_shapes=[pltpu.VMEM(sh, BF16) for sh in half_shapes] + [pltpu.VMEM(sh, BF16) for sh in half_shapes]
                           + [pltpu.SemaphoreType.DMA((total,)), pltpu.SemaphoreType.DMA((total,))]),
        out_shape=[jax.ShapeDtypeStruct((h, cl), F32) for h, cl in zip(halves, cols)]
                  + [jax.ShapeDtypeStruct((N_CHIPS, h, cl), BF16) for h, cl in zip(halves, cols)],
        compiler_params=_params("arbitrary"),
    )(place, *grads, *grads)
    return list(zip(outs[:n], outs[n:]))


SMALL_SUM_ROWS = 16


def join_halves(grad, place, owns, gots, small_pool, small_gla, small_top, g_gk_pad):
    n = len(owns)
    half, cols = grad.shape[1] // 2, grad.shape[2]
    rb = min(ADD_HALVES_ROWS, half)
    sent = half // rb
    shapes = [g.shape for g in gots] + [(N_CHIPS - 1, half, cols)]
    rbs = [min(ADD_ROWS, sh[1]) for sh in shapes]
    counts = [sh[1] // r for sh, r in zip(shapes, rbs)]
    starts, joined = _spans(counts)
    first_join = sent + 1
    steps = first_join + joined

    def body(place_ref, *refs):
        refs = iter(refs)
        take = lambda count: [next(refs) for _ in range(count)]
        (a_ref, o_ref), o_refs, g_refs = take(2), take(n), take(n)
        pool_ref, gla_ref, top_ref, gk_ref = take(4)
        out_refs, (total_ref,) = take(n + 1), take(1)
        send_buf, their_buf, sums_buf, got_buf = take(4)
        sum_refs = take(n + 1)
        to_core_sems, from_core_sems, to_chip_sems, from_chip_sems, local_sems, send_sems, recv_sems = take(7)
        all_ref, small_ref = take(2)
        t = pl.program_id(0)
        q = place_ref[1]
        x, y, c = _position()
        start_small, wait_small = _gather_small(small_ref, all_ref, *refs)
        block = lambda k: pl.ds(k * rb, rb)

        def to_core(k):
            return pltpu.make_async_remote_copy(
                src_ref=send_buf.at[:, block(k)], dst_ref=their_buf.at[:, block(k)],
                send_sem=to_core_sems.at[k], recv_sem=from_core_sems.at[k],
                device_id=(x, y, 1 - c), device_id_type=MESH)

        def to_owners(k):
            return [pltpu.make_async_remote_copy(
                src_ref=sums_buf.at[2 * chip[0] + chip[1], block(k)], dst_ref=got_buf.at[j, block(k)],
                send_sem=to_chip_sems.at[3 * k + j], recv_sem=from_chip_sems.at[3 * k + j],
                device_id=(*chip, c), device_id_type=MESH) for j, chip in enumerate(_other_chips(x, y))]

        def copies(i, k):
            src = sum_refs[i].at[pl.ds(k * rbs[i], rbs[i])]
            rows = pl.ds(c * shapes[i][1] + k * rbs[i], rbs[i])
            return (pltpu.make_async_copy(src, out_refs[i].at[rows], local_sems.at[starts[i] + k]),
                    pltpu.make_async_remote_copy(
                        src_ref=src, dst_ref=out_refs[i].at[rows],
                        send_sem=send_sems.at[starts[i] + k], recv_sem=recv_sems.at[starts[i] + k],
                        device_id=(x, y, 1 - c), device_id_type=MESH))

        @pl.when(t == 0)
        def _():
            small_ref[0:3, :] = pool_ref[0:3, :]
            small_ref[3:5, :] = gla_ref[0:2, :]
            small_ref[5:8, :] = top_ref[0:3, :]
            for r in range(GATE_RANK):
                small_ref[8 + r // 2:9 + r // 2, (r % 2) * KEY_W:(r % 2 + 1) * KEY_W] = gk_ref[r:r + 1, :]
            start_small()

        for k in range(sent):
            @pl.when(t == k)
            def _(k=k):
                send_buf[:, k * rb:(k + 1) * rb, :] = _bf(o_ref[...])
                to_core(k).start()

        for k in range(sent):
            @pl.when(t == k + 1)
            def _(k=k):
                to_core(k).wait_recv()
                theirs = their_buf.at[:, block(k)]
                sums_buf[:, k * rb:(k + 1) * rb, :] = _bf(a_ref[...] + theirs[...].astype(F32))
                sum_refs[n][k * rb:(k + 1) * rb, :] = a_ref[q] + theirs[q].astype(F32)
                for cp in to_owners(k):
                    cp.start()

        for i in range(n + 1):
            for k in range(counts[i]):
                @pl.when(t == first_join + starts[i] + k)
                def _(i=i, k=k):
                    rows = slice(k * rbs[i], (k + 1) * rbs[i])
                    if i < n:
                        total_i = o_refs[i][...]
                        arrived = [g_refs[i][j] for j in range(N_CHIPS - 1)]
                    else:
                        if k == 0:
                            for kk in range(sent):
                                for cp in to_owners(kk):
                                    cp.wait_recv()
                        total_i = sum_refs[n][rows, :]
                        arrived = [got_buf[j, rows, :] for j in range(N_CHIPS - 1)]
                    for part in arrived:
                        total_i = total_i + part.astype(F32)
                    sum_refs[i][rows, :] = total_i
                    for cp in copies(i, k):
                        cp.start()

        @pl.when(t == steps - 1)
        def _():
            wait_small()
            small_total = all_ref[0]
            for dev in range(1, N_DEV):
                small_total = small_total + all_ref[dev]
            total_ref[...] = small_total
            for k in range(sent):
                to_core(k).wait_send()
                for cp in to_owners(k):
                    cp.wait_send()
            for i in range(n + 1):
                for k in range(counts[i]):
                    for cp in copies(i, k):
                        cp.wait()

    def specs(i):
        step = lambda t: _local_step(t - first_join, starts[i], counts[i])
        return (pl.BlockSpec((rbs[i], shapes[i][2]), lambda t, place: (step(t), 0)),
                pl.BlockSpec((N_CHIPS - 1, rbs[i], shapes[i][2]), lambda t, place: (0, step(t), 0)))

    by_quarter = (N_CHIPS, rb, cols)
    mine = pl.BlockSpec(by_quarter, lambda t, place: (0, place[0] * sent + jnp.clip(t - 1, 0, sent - 1), 0))
    other = pl.BlockSpec(by_quarter, lambda t, place: (0, (1 - place[0]) * sent + jnp.clip(t, 0, sent - 1), 0))
    all_specs = [specs(i) for i in range(n)]
    sems = lambda count: pltpu.SemaphoreType.DMA((count,))
    outs = pl.pallas_call(
        body, name="join_halves",
        grid_spec=pltpu.PrefetchScalarGridSpec(
            num_scalar_prefetch=1, grid=(steps,),
            in_specs=[mine, other] + [sp[0] for sp in all_specs] + [sp[1] for sp in all_specs] + [VMEM_SPEC] * 4,
            out_specs=_any_specs(n + 1) + [VMEM_SPEC],
            scratch_shapes=[pltpu.VMEM((N_CHIPS, half, cols), BF16) for _ in range(3)]
                           + [pltpu.VMEM(shapes[n], BF16)] + [pltpu.VMEM(sh[1:], F32) for sh in shapes]
                           + [sems(sent), sems(sent), sems(3 * sent), sems(3 * sent),
                              sems(joined), sems(joined), sems(joined),
                              pltpu.VMEM((N_DEV, SMALL_SUM_ROWS, D), F32), pltpu.VMEM((SMALL_SUM_ROWS, D), F32)]
                           + SMALL_SEMS),
        out_shape=[jax.ShapeDtypeStruct((2 * sh[1], sh[2]), F32) for sh in shapes]
                  + [jax.ShapeDtypeStruct((SMALL_SUM_ROWS, D), F32)],
        compiler_params=_params("arbitrary"),
    )(place, grad, grad, *owns, *gots, small_pool, small_gla, small_top, g_gk_pad)
    return [outs[n]] + list(outs[:n]), outs[n + 1]


def _adam_math(w, g, m, v):
    m = ADAM_B1 * m + (1.0 - ADAM_B1) * g
    v = ADAM_B2 * v + (1.0 - ADAM_B2) * (g * g)
    m_hat = m / (1.0 - ADAM_B1 ** ADAM_STEP)
    v_hat = v / (1.0 - ADAM_B2 ** ADAM_STEP)
    delta = -ADAM_LR * (m_hat / (jnp.sqrt(v_hat) + ADAM_EPS) + ADAM_WD * w)
    return delta, m, v


ADAM_BLOCK_BYTES = 2 ** 19
ADAM_MOST_STEPS = 8


def adamw(params, name, echo):
    n = len(params)
    shapes = [p[0].shape for p in params]
    first_out, _ = _spans([4 if e else 3 for e in echo])

    def tile_rows(shape):
        rows, cols = shape[0], shape[-1]
        aligned = 1 if len(shape) == 3 else 8
        divisors = [t for t in range(aligned, rows + 1, aligned) if rows % t == 0]
        tile = max(t for t in divisors if t * cols * 4 <= ADAM_BLOCK_BYTES)
        if rows // tile > ADAM_MOST_STEPS:
            tile = min(t for t in divisors if rows // t <= ADAM_MOST_STEPS)
        return tile

    tiles = [tile_rows(sh) for sh in shapes]
    counts = [sh[0] // tl for sh, tl in zip(shapes, tiles)]
    starts, total = _spans(counts)

    def body(*refs):
        ins, outs = refs[:4 * n], refs[4 * n:]
        t = pl.program_id(0)
        for i in range(n):
            @pl.when((t >= starts[i]) & (t < starts[i] + counts[i]))
            def _(i=i):
                w_ref, g_ref, m_ref, v_ref = ins[4 * i:4 * i + 4]
                g = g_ref[...]
                d, nm, nv = _adam_math(w_ref[...], g, m_ref[...], v_ref[...])
                outs[first_out[i]][...] = d
                outs[first_out[i] + 1][...] = nm
                outs[first_out[i] + 2][...] = nv
                if echo[i]:
                    outs[first_out[i] + 3][...] = g

    def spec(i):
        block = (tiles[i],) + shapes[i][1:]
        zeros = (0,) * (len(block) - 1)
        return pl.BlockSpec(block, lambda t: (_local_step(t, starts[i], counts[i]),) + zeros)

    outs = pl.pallas_call(
        body, name=name, grid=(total,),
        out_shape=[jax.ShapeDtypeStruct(sh, F32) for sh, e in zip(shapes, echo) for _ in range(4 if e else 3)],
        in_specs=[spec(i) for i in range(n) for _ in range(4)],
        out_specs=[spec(i) for i in range(n) for _ in range(4 if echo[i] else 3)],
        compiler_params=_params("arbitrary"),
    )(*[a for p in params for a in p])
    return [tuple(outs[first_out[i]:first_out[i] + (4 if echo[i] else 3)]) for i in range(n)]


def adamw_small(params, total, place):
    n = len(params)

    def cut_gradients(total_ref, q, g_refs):
        g_norm, g_group_b, g_scale, g_gk_w, g_gk_b, g_head_norm, g_final = g_refs
        g_norm[0:1, :] = total_ref[0:1, :]
        g_norm[1:2, :] = total_ref[3:4, :]
        g_scale[...] = total_ref[1:2, :]
        g_final[...] = total_ref[5:6, :]
        g_gk_b[...] = total_ref[4:5, pl.ds(pl.multiple_of(q * 128, 128), 128)]
        for r in range(GATE_RANK):
            lanes = pl.ds(pl.multiple_of((r % 2) * KEY_W + q * 128, 128), 128)
            g_gk_w[r:r + 1, :] = total_ref[8 + r // 2:9 + r // 2, lanes]
        for k in range(N_CHIPS):
            @pl.when(q == k)
            def _(k=k):
                g_head_norm[...] = total_ref[6:7, 64 * k:64 * (k + 1)]
                for g in range(GROUPS):
                    g_group_b[g:g + 1, :] = total_ref[2:3, GROUP_DIM * g + 64 * k:GROUP_DIM * g + 64 * (k + 1)]

    def body(place_ref, total_ref, *refs):
        ins, outs = refs[:3 * n], refs[3 * n:]
        outs[4 * n][...] = total_ref[7:8, 0:1]
        cut_gradients(total_ref, place_ref[1], outs[0:4 * n:4])
        for k in range(n):
            w_ref, m_ref, v_ref = ins[3 * k:3 * k + 3]
            d, nm, nv = _adam_math(w_ref[...], outs[4 * k][...], m_ref[...], v_ref[...])
            outs[4 * k + 1][...] = d
            outs[4 * k + 2][...] = nm
            outs[4 * k + 3][...] = nv

    flat = [a for p in params for a in p]
    outs = pl.pallas_call(
        body, name="adamw_small",
        out_shape=[jax.ShapeDtypeStruct(p[0].shape, F32) for p in params for _ in range(4)]
                  + [jax.ShapeDtypeStruct((1, 1), F32)],
        in_specs=[pl.BlockSpec(memory_space=pltpu.SMEM)] + [VMEM_SPEC] * (1 + 3 * n),
        out_specs=[VMEM_SPEC] * (4 * n + 1),
    )(place, total, *flat)
    return [tuple(outs[4 * k:4 * k + 4]) for k in range(n)], outs[4 * n]


def matmul_tn(a, b, name, tile_n, place, by_column_tile=False, also_reduce=()):
    s, m = a.shape
    n = b.shape[1]
    steps = n // tile_n
    n_red = len(also_reduce)
    assert n_red == 0 or steps >= 2
    halves = [(g.shape[1] // 2, g.shape[2]) for g in also_reduce]
    if by_column_tile:
        out_shape = jax.ShapeDtypeStruct((steps, m, tile_n), F32)
        out_spec = pl.BlockSpec((None, m, tile_n), lambda j, place: (j, 0, 0))
    else:
        out_shape = jax.ShapeDtypeStruct((m, n), F32)
        out_spec = pl.BlockSpec((m, tile_n), lambda j, place: (0, j))

    def body(place_ref, a_ref, b_ref, *rest):
        rest = iter(rest)
        take = lambda count: [next(rest) for _ in range(count)]
        mine_refs, other_refs, (out_ref,) = take(n_red), take(n_red), take(1)
        own_refs, got_refs = take(n_red), take(n_red)
        send_bufs, their_bufs, sums_bufs = take(n_red), take(n_red), take(n_red)
        sems = list(rest)
        j = pl.program_id(0)
        q = place_ref[1]
        x, y, c = _position()

        def to_core(i):
            return pltpu.make_async_remote_copy(
                src_ref=send_bufs[i], dst_ref=their_bufs[i], send_sem=sems[0].at[i], recv_sem=sems[1].at[i],
                device_id=(x, y, 1 - c), device_id_type=MESH)

        def to_owners(i):
            return [pltpu.make_async_remote_copy(
                src_ref=sums_bufs[i].at[2 * chip[0] + chip[1]], dst_ref=got_refs[i].at[k],
                send_sem=sems[2].at[3 * i + k], recv_sem=sems[3].at[3 * i + k],
                device_id=(*chip, c), device_id_type=MESH) for k, chip in enumerate(_other_chips(x, y))]

        if n_red:
            @pl.when(j == 0)
            def _():
                for i in range(n_red):
                    send_bufs[i][...] = _bf(other_refs[i][...])
                    to_core(i).start()

            @pl.when(j == 1)
            def _():
                for i in range(n_red):
                    to_core(i).wait_recv()
                    sums_bufs[i][...] = _bf(mine_refs[i][...] + their_bufs[i][...].astype(F32))
                    own_refs[i][...] = mine_refs[i][q] + their_bufs[i][q].astype(F32)
                    for cp in to_owners(i):
                        cp.start()

        out_ref[...] = _tn(a_ref[...], b_ref[...])

        if n_red:
            @pl.when(j == steps - 1)
            def _():
                for i in range(n_red):
                    to_core(i).wait_send()
                    for cp in to_owners(i):
                        cp.wait()

    by_quarter = [(N_CHIPS, *h) for h in halves]
    outs = pl.pallas_call(
        body, name=name,
        grid_spec=pltpu.PrefetchScalarGridSpec(
            num_scalar_prefetch=1, grid=(steps,),
            in_specs=[pl.BlockSpec((s, m), lambda j, place: (0, 0)), pl.BlockSpec((s, tile_n), lambda j, place: (0, j))]
                     + [pl.BlockSpec(sh, lambda j, place: (0, place[0], 0)) for sh in by_quarter]
                     + [pl.BlockSpec(sh, lambda j, place: (0, 1 - place[0], 0)) for sh in by_quarter],
            out_specs=[out_spec] + [pl.BlockSpec(h, lambda j, place: (0, 0)) for h in halves] + _any_specs(n_red),
            scratch_shapes=[pltpu.VMEM(sh, BF16) for sh in by_quarter * 3]
                           + ([pltpu.SemaphoreType.DMA((n_red,)), pltpu.SemaphoreType.DMA((n_red,)),
                               pltpu.SemaphoreType.DMA((3 * n_red,)), pltpu.SemaphoreType.DMA((3 * n_red,))]
                              if n_red else [])),
        out_shape=[out_shape] + [jax.ShapeDtypeStruct(h, F32) for h in halves]
                  + [jax.ShapeDtypeStruct((N_CHIPS - 1, *h), BF16) for h in halves],
        compiler_params=_params("arbitrary"),
    )(place, a, b, *also_reduce, *also_reduce)
    return outs[0], list(zip(outs[1:1 + n_red], outs[1 + n_red:]))


ROW_TILE = 512


def _row_index(tile, rows):
    return tile * rows + lax.broadcasted_iota(jnp.int32, (rows, 1), 0)


def _inverse_counts(t_glob):
    return [1.0 / jnp.minimum(t_glob + 1, w).astype(F32) for w in POOL_WINDOWS]


def _sigmoid(z):
    return 1.0 / (1.0 + jnp.exp(-z))


def _trailing_sums(src, tmp, cols, window, rows):
    bufs = (src, tmp)
    span, level, start = 1, 0, 0
    while span < window:
        start += 8
        a, b = bufs[level % 2], bufs[(level + 1) % 2]
        n = HALO + rows - start
        b[start:start + n, cols] = a[start:start + n, cols] + a[start - span:start - span + n, cols]
        span, level = 2 * span, level + 1
    return bufs[level % 2][HALO:HALO + rows, cols]


def _leading_sums(src, tmp, cols, window, rows):
    bufs = (src, tmp)
    span, level, n = 1, 0, rows + HALO
    while span < window:
        n -= 8
        a, b = bufs[level % 2], bufs[(level + 1) % 2]
        b[0:n, cols] = a[0:n, cols] + a[span:span + n, cols]
        span, level = 2 * span, level + 1
    return bufs[level % 2][0:rows, cols]


def gather_in_background(step, last, out_refs, send_sems, recv_sems, finish, pass_on=None):
    n = len(out_refs)
    x, y, c = _position()
    q = 2 * x + y
    chips = _other_chips(x, y)

    def copy(k, i, quarter, half, to):
        return _gather_copy(out_refs[i], send_sems, recv_sems, k * n + i, quarter, half, to)

    def pass_landed_on():
        for j, chip in enumerate(chips):
            qj = 2 * chip[0] + chip[1]
            for i in range(n):
                mine, _ = _halves(out_refs[i].shape[1], c)
                copy(j, i, qj, mine, (x, y, c)).wait_recv()
                copy(3 + j, i, qj, mine, (x, y, 1 - c)).start()

    if not finish:
        @pl.when(step == 0)
        def _():
            for i in range(n):
                mine, _ = _halves(out_refs[i].shape[1], c)
                for j, chip in enumerate(chips):
                    copy(j, i, q, mine, (*chip, c)).start()

        if pass_on is None:
            pl.when(step == last)(pass_landed_on)
        return

    @pl.when(step == last)
    def _():
        if pass_on:
            pass_landed_on()
        for j, chip in enumerate(chips):
            qj = 2 * chip[0] + chip[1]
            for i in range(n):
                mine, other = _halves(out_refs[i].shape[1], c)
                copy(3 + j, i, qj, other, (x, y, c)).wait_recv()
                copy(j, i, q, mine, (x, y, c)).wait_send()
                copy(3 + j, i, qj, mine, (x, y, c)).wait_send()


def _group_matrix(gw_ref, g):
    rows = GROUP_DIM // N_CHIPS
    return jnp.concatenate([gw_ref[j, rows * g:rows * (g + 1), :] for j in range(N_CHIPS)], axis=0)


def pool_forward(x, w0, wpi, gw, gb, scale, wpo, later):
    s = x.shape[0]
    ts = ROW_TILE
    nt = s // ts
    assert nt >= 2
    n_later = len(later)

    def body(x_ref, w0_ref, wpi_ref, gw_ref, gb_ref, sc_ref, wpo_ref, *rest):
        rest = rest[n_later:]
        h1_ref, pooled_ref, gt_ref, n0_ref = rest[:4]
        later_refs = rest[4:4 + n_later]
        ubuf, tbuf, hist, send_sems, recv_sems = rest[4 + n_later:]
        i = pl.program_id(0)
        gather_in_background(i, nt - 1, later_refs, send_sems, recv_sems, finish=False, pass_on=False)
        xv = x_ref[...]
        r = lax.rsqrt(jnp.mean(xv * xv, axis=-1, keepdims=True) + EPS)
        n0 = _bf(xv * r * w0_ref[...])
        n0_ref[...] = n0
        u = jnp.concatenate([_nn(n0, wpi_ref[0]), _nn(n0, wpi_ref[1])], axis=-1)
        gt = jnp.concatenate([_nn(n0, wpi_ref[2]), _nn(n0, wpi_ref[3])], axis=-1)
        gt_ref[...] = gt

        @pl.when(i == 0)
        def _():
            hist[...] = jnp.zeros_like(hist)

        ubuf[0:HALO, :] = hist[...]
        ubuf[HALO:HALO + ts, :] = u
        hist[...] = u[ts - HALO:, :]
        inv = _inverse_counts(_row_index(i, ts))
        mixed = []
        for g, w in enumerate(POOL_WINDOWS):
            cols = slice(g * GROUP_DIM, (g + 1) * GROUP_DIM)
            pooled = _bf(_trailing_sums(ubuf, tbuf, cols, w, ts) * inv[g] - u[:, cols])
            pooled_ref[:, cols] = pooled
            mixed.append(_nn(pooled, _group_matrix(gw_ref, g)))
        mixed = jnp.concatenate(mixed, axis=-1) + gb_ref[...]
        y = mixed * sc_ref[...] * (gt * _sigmoid(gt))
        h1_ref[...] = xv + _nn(_bf(y), wpo_ref[...])
        gather_in_background(i, nt - 1, later_refs, send_sems, recv_sems, finish=True, pass_on=True)

    row = lambda cols: pl.BlockSpec((ts, cols), lambda i: (i, 0))
    outs = pl.pallas_call(
        body, name="pool_forward", grid=(nt,),
        out_shape=[jax.ShapeDtypeStruct((s, D), F32), jax.ShapeDtypeStruct((s, D), BF16),
                   jax.ShapeDtypeStruct((s, D), F32), jax.ShapeDtypeStruct((s, D), BF16)]
                  + [jax.ShapeDtypeStruct(a.shape, a.dtype) for a in later],
        in_specs=[row(D), _full((1, D)), _full((N_CHIPS, D, D // 2)), _full((GROUPS, GROUP_DIM, GROUP_DIM)),
                  _full((1, D)), _full((1, D)), _full((D, D))] + _any_specs(n_later),
        out_specs=[row(D), row(D), row(D), row(D)] + _any_specs(n_later),
        input_output_aliases={7 + k: 4 + k for k in range(n_later)},
        scratch_shapes=[pltpu.VMEM((HALO + ts, D), F32), pltpu.VMEM((HALO + ts, D), F32),
                        pltpu.VMEM((HALO, D), F32),
                        pltpu.SemaphoreType.DMA((6 * n_later,)), pltpu.SemaphoreType.DMA((6 * n_later,))],
        compiler_params=_params("arbitrary"),
    )(x, w0, wpi, gw, gb, scale, wpo, *later)
    return outs[:4], outs[4:]


def pool_backward(x, dh1, pooled, gt, w0, wpi, gw, gb, scale, wpo, chip_sums):
    s = x.shape[0]
    ts = ROW_TILE
    nt = s // ts
    n_sums = len(chip_sums)

    def body(x_ref, dh1_ref, pooled_ref, gt_ref, w0_ref, wpi_ref, gw_ref, gb_ref, sc_ref, wpo_ref, *rest):
        sum_refs, rest = rest[:n_sums], rest[n_sums:]
        dx_ref, dproj_ref, gpo_ref, ggw_ref, small_ref = rest[:5]
        got_refs = rest[5:5 + n_sums]
        ebuf, tbuf, ahead, send_sems, recv_sems = rest[5 + n_sums:]
        i = pl.program_id(0)
        copies = _scatter_copies(sum_refs, got_refs, send_sems, recv_sems)

        @pl.when(i == 0)
        def _():
            for cp in copies:
                cp.start()

        @pl.when(i == 0)
        def _():
            gpo_ref[...] = jnp.zeros_like(gpo_ref)
            ggw_ref[...] = jnp.zeros_like(ggw_ref)
            small_ref[...] = jnp.zeros_like(small_ref)
            ahead[...] = jnp.zeros_like(ahead)

        dh1 = dh1_ref[...]
        dh1_bf = _bf(dh1)
        gt = gt_ref[...]
        sc = sc_ref[...]
        dy = _nt(dh1_bf, wpo_ref[...])
        pooled_bf = []
        mixed = []
        for g in range(GROUPS):
            cols = slice(g * GROUP_DIM, (g + 1) * GROUP_DIM)
            pb = pooled_ref[:, cols]
            pooled_bf.append(pb)
            mixed.append(_nn(pb, _group_matrix(gw_ref, g)))
        mixed = jnp.concatenate(mixed, axis=-1) + gb_ref[...]
        sg = _sigmoid(gt)
        silu = gt * sg
        gpo_ref[...] += _tn(_bf(mixed * sc * silu), dh1_bf)
        dmixed = dy * sc * silu
        dgt = dy * mixed * sc * (sg * (1.0 + gt * (1.0 - sg)))
        dproj_ref[:, D:] = _bf(dgt)
        small_ref[1:2, :] += jnp.sum(dy * mixed * silu, axis=0, keepdims=True)
        small_ref[2:3, :] += jnp.sum(dmixed, axis=0, keepdims=True)

        inv = _inverse_counts(_row_index(nt - 1 - i, ts))
        rows_q = GROUP_DIM // N_CHIPS
        ebuf[ts:ts + HALO, :] = ahead[...]
        dpooled = []
        for g in range(GROUPS):
            cols = slice(g * GROUP_DIM, (g + 1) * GROUP_DIM)
            dm = _bf(dmixed[:, cols])
            ggw = _tn(pooled_bf[g], dm)
            for j in range(N_CHIPS):
                ggw_ref[j, rows_q * g:rows_q * (g + 1), :] += ggw[rows_q * j:rows_q * (j + 1), :]
            dp = _nt(dm, _group_matrix(gw_ref, g))
            dpooled.append(dp)
            ebuf[0:ts, cols] = dp * inv[g]
        ahead[...] = ebuf[0:HALO, :]
        du = []
        for g, w in enumerate(POOL_WINDOWS):
            cols = slice(g * GROUP_DIM, (g + 1) * GROUP_DIM)
            du.append(_leading_sums(ebuf, tbuf, cols, w, ts) - dpooled[g])
        du = _bf(jnp.concatenate(du, axis=-1))
        dproj_ref[:, :D] = du
        dgt_bf = _bf(dgt)
        half = D // 2
        dn0 = (_nt(du[:, :half], wpi_ref[0]) + _nt(du[:, half:], wpi_ref[1])
               + _nt(dgt_bf[:, :half], wpi_ref[2]) + _nt(dgt_bf[:, half:], wpi_ref[3]))

        xv = x_ref[...]
        r = lax.rsqrt(jnp.mean(xv * xv, axis=-1, keepdims=True) + EPS)
        xhat = xv * r
        small_ref[0:1, :] += jnp.sum(dn0 * xhat, axis=0, keepdims=True)
        dxh = dn0 * w0_ref[...]
        dx_ref[...] = dh1 + r * (dxh - xhat * jnp.mean(dxh * xhat, axis=-1, keepdims=True))

        @pl.when(i == nt - 1)
        def _():
            for cp in copies:
                cp.wait()

    row = lambda cols: pl.BlockSpec((ts, cols), lambda i: (nt - 1 - i, 0))
    outs = pl.pallas_call(
        body, name="pool_backward", grid=(nt,),
        out_shape=[jax.ShapeDtypeStruct((s, D), F32), jax.ShapeDtypeStruct((s, 2 * D), BF16),
                   jax.ShapeDtypeStruct((D, D), F32),
                   jax.ShapeDtypeStruct((GROUPS, GROUP_DIM, GROUP_DIM), F32),
                   jax.ShapeDtypeStruct((8, D), F32)] + _scatter_shapes(chip_sums),
        in_specs=[row(D), row(D), row(D), row(D), _full((1, D)), _full((N_CHIPS, D, D // 2)),
                  _full((GROUPS, GROUP_DIM, GROUP_DIM)), _full((1, D)), _full((1, D)), _full((D, D))]
                 + _any_specs(n_sums),
        out_specs=[row(D), row(2 * D), _full((D, D)), _full((GROUPS, GROUP_DIM, GROUP_DIM)), _full((8, D))]
                  + _any_specs(n_sums),
        scratch_shapes=[pltpu.VMEM((ts + HALO, D), F32), pltpu.VMEM((ts + HALO, D), F32),
                        pltpu.VMEM((HALO, D), F32),
                        pltpu.SemaphoreType.DMA((3 * n_sums,)), pltpu.SemaphoreType.DMA((3 * n_sums,))],
        compiler_params=_params("arbitrary"),
    )(x, dh1, pooled, gt, w0, wpi, gw, gb, scale, wpo, *chip_sums)
    return outs[:5], outs[5:]


def gla_project(h1, w1, wgi_q, wgk, bgk, later):
    s = h1.shape[0]
    ts = ROW_TILE
    nt = s // ts
    assert nt >= 2
    n_later = len(later)

    def body(h_ref, w1_ref, wq_ref, wgk_ref, bgk_ref, *rest):
        rest = rest[n_later:]
        qk_ref, v_ref, gate_ref, low_ref, cum_ref, n1_ref = rest[:6]
        later_refs = rest[6:6 + n_later]
        send_sems, recv_sems, wgi_ref = rest[6 + n_later:]
        gather_in_background(pl.program_id(0), nt - 1, later_refs, send_sems, recv_sems, finish=False)

        @pl.when(pl.program_id(0) == 0)
        def _():
            _assemble_gla_in(wq_ref, wgi_ref)

        hv = h_ref[...]
        r = lax.rsqrt(jnp.mean(hv * hv, axis=-1, keepdims=True) + EPS)
        n1 = _bf(hv * r * w1_ref[...])
        n1_ref[...] = n1
        qk_ref[...] = _nn(n1, wgi_ref[:, 0:2 * KEY_W])
        v_ref[...] = _bf(_nn(n1, wgi_ref[:, 2 * KEY_W:2 * KEY_W + D]))
        gate_ref[...] = _nn(n1, wgi_ref[:, 2 * KEY_W + D:GLA_MAIN])
        low = _bf(_nn(n1, wgi_ref[:, GLA_MAIN:]))
        low_ref[...] = low
        z = _nn(low, wgk_ref[...]) + bgk_ref[...]
        lg = (jnp.minimum(z, 0.0) - jnp.log(1.0 + jnp.exp(-jnp.abs(z)))) / GATE_NORM
        lower_f = _chunk_masks()[0].astype(F32)
        for r0 in range(0, ts, CHUNK):
            cum_ref[r0:r0 + CHUNK, :] = _nn_exact(lower_f, lg[r0:r0 + CHUNK, :])
        gather_in_background(pl.program_id(0), nt - 1, later_refs, send_sems, recv_sems, finish=True)

    row = lambda cols: pl.BlockSpec((ts, cols), lambda i: (i, 0))
    outs = pl.pallas_call(
        body, name="gla_project", grid=(nt,),
        out_shape=[jax.ShapeDtypeStruct((s, D), F32), jax.ShapeDtypeStruct((s, D), BF16),
                   jax.ShapeDtypeStruct((s, D), F32), jax.ShapeDtypeStruct((s, RANK_PAD), BF16),
                   jax.ShapeDtypeStruct((s, KEY_W), F32), jax.ShapeDtypeStruct((s, D), BF16)]
                  + [jax.ShapeDtypeStruct(a.shape, a.dtype) for a in later],
        in_specs=[row(D), _full((1, D)), _full((N_CHIPS, D, GLA_IN_QUARTER)),
                  _full((RANK_PAD, KEY_W)), _full((1, KEY_W))] + _any_specs(n_later),
        out_specs=[row(D), row(D), row(D), row(RANK_PAD), row(KEY_W), row(D)] + _any_specs(n_later),
        input_output_aliases={5 + k: 6 + k for k in range(n_later)},
        scratch_shapes=[pltpu.SemaphoreType.DMA((6 * n_later,)), pltpu.SemaphoreType.DMA((6 * n_later,)),
                        pltpu.VMEM((D, GLA_MAIN + RANK_PAD), BF16)],
        compiler_params=_params("arbitrary"),
    )(h1, w1, wgi_q, wgk, bgk, *later)
    return outs[:6], outs[6:]


def _assemble_gla_in(wq_ref, wfull):
    pad = jnp.zeros((CAST_ROWS, GLA_MAIN + RANK_PAD - GLA_IN), BF16)
    for r0 in range(0, D, CAST_ROWS):
        rows = slice(r0, r0 + CAST_ROWS)
        wfull[rows, :] = jnp.concatenate([wq_ref[q, rows, :] for q in range(N_CHIPS)] + [pad], axis=1)


GLA_BLOCK = 512
CHUNKS_PER_BLOCK = GLA_BLOCK // CHUNK


def _chunk_masks():
    t = lax.broadcasted_iota(jnp.int32, (CHUNK, CHUNK), 0)
    u = lax.broadcasted_iota(jnp.int32, (CHUNK, CHUNK), 1)
    return t >= u, t <= u


def _gla_chunk_terms(q, cum):
    ep = jnp.exp(cum)
    en = jnp.exp(-cum)
    qs = q * (HEAD_K ** -0.5)
    last = cum[CHUNK - 1:CHUNK, :]
    ed = jnp.exp(last - cum)
    dec = jnp.exp(last)
    return ep, en, qs, ed, dec


def gla_forward(qk, v, cum):
    s = qk.shape[0]
    nb = s // GLA_BLOCK
    nc = s // CHUNK

    def body(q_ref, k_ref, v_ref, cum_ref, o_ref, st_ref, sc_ref, state):
        @pl.when(pl.program_id(0) == 0)
        def _():
            state[...] = jnp.zeros_like(state)

        lower, _ = _chunk_masks()

        def chunk(cc, carry):
            rows = pl.ds(pl.multiple_of(cc * CHUNK, CHUNK), CHUNK)
            for h in range(HEADS):
                kc = slice(h * HEAD_K, (h + 1) * HEAD_K)
                vc = slice(h * HEAD_V, (h + 1) * HEAD_V)
                q = q_ref[rows, kc]
                k = k_ref[rows, kc]
                v = v_ref[rows, vc]
                ep, en, qs, ed, dec = _gla_chunk_terms(q, cum_ref[rows, kc])
                a = _bf(qs * ep)
                fwd = _nt(a, _bf(k * en))
                bwd = _nt(_bf(qs * en), _bf(k * ep))
                scores = _bf(jnp.where(lower, fwd, bwd))
                sc_ref[rows, h * CHUNK:(h + 1) * CHUNK] = scores
                st = state[h]
                st_ref[cc, h] = st
                o_ref[rows, vc] = _nn(scores, v) + _nt(a, _bf(st))
                state[h] = st * dec + _tn(v, _bf(k * ed))
            return carry

        lax.fori_loop(0, CHUNKS_PER_BLOCK, chunk, 0, unroll=True)

    return pl.pallas_call(
        body, name="gla_forward", grid=(nb,),
        out_shape=(jax.ShapeDtypeStruct((s, D), F32),
                   jax.ShapeDtypeStruct((nc, HEADS, HEAD_V, HEAD_K), F32),
                   jax.ShapeDtypeStruct((s, HEADS * CHUNK), BF16)),
        in_specs=[pl.BlockSpec((GLA_BLOCK, KEY_W), lambda i: (i, 0)),
                  pl.BlockSpec((GLA_BLOCK, KEY_W), lambda i: (i, 1)),
                  pl.BlockSpec((GLA_BLOCK, D), lambda i: (i, 0)),
                  pl.BlockSpec((GLA_BLOCK, KEY_W), lambda i: (i, 0))],
        out_specs=(pl.BlockSpec((GLA_BLOCK, D), lambda i: (i, 0)),
                   pl.BlockSpec((CHUNKS_PER_BLOCK, HEADS, HEAD_V, HEAD_K), lambda i: (i, 0, 0, 0)),
                   pl.BlockSpec((GLA_BLOCK, HEADS * CHUNK), lambda i: (i, 0))),
        scratch_shapes=[pltpu.VMEM((HEADS, HEAD_V, HEAD_K), F32)],
        compiler_params=_params("arbitrary"),
    )(qk, qk, v, cum)


def gla_backward(qk, v, cum, do, states, scores):
    s = qk.shape[0]
    nb = s // GLA_BLOCK

    def body(q_ref, k_ref, v_ref, cum_ref, do_ref, st_ref, sc_ref, dq_ref, dk_ref, dv_ref, dcum_ref, dstate):
        @pl.when(pl.program_id(0) == 0)
        def _():
            dstate[...] = jnp.zeros_like(dstate)

        lower, _ = _chunk_masks()
        is_last = lax.broadcasted_iota(jnp.int32, (CHUNK, HEAD_K), 0) == CHUNK - 1

        def chunk(step, carry):
            cc = CHUNKS_PER_BLOCK - 1 - step
            rows = pl.ds(pl.multiple_of(cc * CHUNK, CHUNK), CHUNK)
            for h in range(HEADS):
                kc = slice(h * HEAD_K, (h + 1) * HEAD_K)
                vc = slice(h * HEAD_V, (h + 1) * HEAD_V)
                q = q_ref[rows, kc]
                k = k_ref[rows, kc]
                v = v_ref[rows, vc]
                do_c = do_ref[rows, vc]
                ep, en, qs, ed, dec = _gla_chunk_terms(q, cum_ref[rows, kc])
                a = _bf(qs * ep)
                b = _bf(k * en)
                c = _bf(qs * en)
                dk_dec = _bf(k * ep)
                kd = _bf(k * ed)
                scores = sc_ref[rows, h * CHUNK:(h + 1) * CHUNK]
                st = st_ref[cc, h]
                dst = dstate[h]
                dst_bf = _bf(dst)

                dscores = _nt(do_c, v)
                dfwd = _bf(jnp.where(lower, dscores, 0.0))
                dbwd = _bf(jnp.where(lower, 0.0, dscores))
                dv_ref[rows, vc] = _bf(_tn(scores, do_c) + _nt(kd, dst_bf))
                da = _nn(dfwd, b) + _nn(do_c, _bf(st))
                db = _tn(dfwd, a)
                dc = _nn(dbwd, dk_dec)
                ddk = _tn(dbwd, c)
                dkd = _nn(v, dst_bf)
                ddec = jnp.sum(dst * st, axis=0, keepdims=True)
                dstate[h] = dst * dec + _tn(do_c, a)

                m = dkd * k * ed
                dq_ref[rows, kc] = _bf((da * ep + dc * en) * (HEAD_K ** -0.5))
                dk_ref[rows, kc] = _bf(db * en + ddk * ep + dkd * ed)
                dcum = (da * qs + ddk * k) * ep - (db * k + dc * qs) * en - m
                dlast = jnp.sum(m, axis=0, keepdims=True) + ddec * dec
                dcum_ref[rows, kc] = dcum + jnp.where(is_last, dlast, 0.0)
            return carry

        lax.fori_loop(0, CHUNKS_PER_BLOCK, chunk, 0, unroll=True)

    rev = lambda cols, col_block: pl.BlockSpec((GLA_BLOCK, cols), lambda i: (nb - 1 - i, col_block))
    return pl.pallas_call(
        body, name="gla_backward", grid=(nb,),
        out_shape=(jax.ShapeDtypeStruct((s, KEY_W), BF16), jax.ShapeDtypeStruct((s, KEY_W), BF16),
                   jax.ShapeDtypeStruct((s, D), BF16), jax.ShapeDtypeStruct((s, KEY_W), F32)),
        in_specs=[rev(KEY_W, 0), rev(KEY_W, 1), rev(D, 0), rev(KEY_W, 0), rev(D, 0),
                  pl.BlockSpec((CHUNKS_PER_BLOCK, HEADS, HEAD_V, HEAD_K), lambda i: (nb - 1 - i, 0, 0, 0)),
                  rev(HEADS * CHUNK, 0)],
        out_specs=(rev(KEY_W, 0), rev(KEY_W, 0), rev(D, 0), rev(KEY_W, 0)),
        scratch_shapes=[pltpu.VMEM((HEADS, HEAD_V, HEAD_K), F32)],
        compiler_params=_params("arbitrary"),
    )(qk, qk, v, cum, do, states, scores)


def head_and_loss(o, gate, h1, target, hw, wgo, wf):
    s = o.shape[0]
    ts = ROW_TILE

    def body(o_ref, gate_ref, h1_ref, tgt_ref, hw_ref, wgo_ref, wf_ref,
             dh2_ref, do_ref, dgate_ref, ggo_ref, small_ref):
        @pl.when(pl.program_id(0) == 0)
        def _():
            ggo_ref[...] = jnp.zeros_like(ggo_ref)
            small_ref[...] = jnp.zeros_like(small_ref)

        gate = gate_ref[...]
        hw = hw_ref[...]
        sg = _sigmoid(gate)
        silu = gate * sg
        ohat, ro = [], []
        for h in range(HEADS):
            oh = o_ref[:, h * HEAD_V:(h + 1) * HEAD_V]
            rh = lax.rsqrt(jnp.mean(oh * oh, axis=-1, keepdims=True) + EPS)
            ro.append(rh)
            ohat.append(oh * rh)
        ohat = jnp.concatenate(ohat, axis=-1)
        on = ohat * hw
        y2 = _bf(on * silu)
        h2 = h1_ref[...] + _nn(y2, wgo_ref[...])
        rf = lax.rsqrt(jnp.mean(h2 * h2, axis=-1, keepdims=True) + EPS)
        h2hat = h2 * rf
        wf = wf_ref[...]
        diff = h2hat * wf - tgt_ref[...]
        small_ref[2:3, :] += jnp.zeros((1, D), F32) + 0.5 * jnp.sum(diff * diff) / D
        dout = diff / D
        small_ref[0:1, :] += jnp.sum(dout * h2hat, axis=0, keepdims=True)
        dxh = dout * wf
        dh2 = rf * (dxh - h2hat * jnp.mean(dxh * h2hat, axis=-1, keepdims=True))
        dh2_ref[...] = dh2
        dh2_bf = _bf(dh2)
        ggo_ref[...] += _tn(y2, dh2_bf)
        dy2 = _nt(dh2_bf, wgo_ref[...])
        don = dy2 * silu
        dgate_ref[...] = _bf(dy2 * on * (sg * (1.0 + gate * (1.0 - sg))))
        ghw = jnp.sum(don * ohat, axis=0, keepdims=True)
        small_ref[1:2, 0:HEAD_V] += sum(ghw[:, h * HEAD_V:(h + 1) * HEAD_V] for h in range(HEADS))
        dohat = don * hw
        for h in range(HEADS):
            cols = slice(h * HEAD_V, (h + 1) * HEAD_V)
            oh, dh = ohat[:, cols], dohat[:, cols]
            do_ref[:, cols] = _bf(ro[h] * (dh - oh * jnp.mean(dh * oh, axis=-1, keepdims=True)))

    row = lambda cols: pl.BlockSpec((ts, cols), lambda i: (i, 0))
    act = jax.ShapeDtypeStruct((s, D), F32)
    act_bf = jax.ShapeDtypeStruct((s, D), BF16)
    return pl.pallas_call(
        body, name="head_and_loss", grid=(s // ts,),
        out_shape=(act, act_bf, act_bf, jax.ShapeDtypeStruct((D, D), F32), jax.ShapeDtypeStruct((8, D), F32)),
        in_specs=[row(D), row(D), row(D), row(D),
                  _full((1, D)), _full((D, D)), _full((1, D))],
        out_specs=(row(D), row(D), row(D), _full((D, D)), _full((8, D))),
        compiler_params=_params("arbitrary"),
    )(o, gate, h1, target, hw, wgo, wf)


def gla_project_backward(dq, dk, dv, dgate, dcum, low, h1, dh2, w1, wgi_q, wgk, bgk):
    s = h1.shape[0]
    ts = ROW_TILE

    def body(dq_ref, dk_ref, dv_ref, dgate_ref, dcum_ref, low_ref, h1_ref, dh2_ref, w1_ref,
             wq_ref, wgk_ref, bgk_ref, dh1_ref, dproj_ref, ggk_ref, small_ref, wgi_ref):
        @pl.when(pl.program_id(0) == 0)
        def _():
            ggk_ref[...] = jnp.zeros_like(ggk_ref)
            small_ref[...] = jnp.zeros_like(small_ref)
            _assemble_gla_in(wq_ref, wgi_ref)

        low = low_ref[...]
        z = _nn(low, wgk_ref[...]) + bgk_ref[...]
        upper_f = _chunk_masks()[1].astype(F32)
        dlg = jnp.concatenate([_nn_exact(upper_f, dcum_ref[r0:r0 + CHUNK, :]) for r0 in range(0, ts, CHUNK)],
                              axis=0)
        dz = dlg * (1.0 / GATE_NORM) * _sigmoid(-z)
        dz_bf = _bf(dz)
        ggk_ref[...] += _tn(low, dz_bf)
        small_ref[1:2, 0:KEY_W] += jnp.sum(dz, axis=0, keepdims=True)
        dlow = _bf(_nt(dz_bf, wgk_ref[...]))
        dproj_ref[:, GLA_MAIN:] = dlow
        dn1 = _nt(dlow, wgi_ref[:, GLA_MAIN:])
        for ref, lo, hi in ((dq_ref, 0, KEY_W), (dk_ref, KEY_W, 2 * KEY_W),
                            (dv_ref, 2 * KEY_W, 2 * KEY_W + D), (dgate_ref, 2 * KEY_W + D, GLA_MAIN)):
            piece = ref[...]
            dproj_ref[:, lo:hi] = piece
            dn1 = dn1 + _nt(piece, wgi_ref[:, lo:hi])
        hv = h1_ref[...]
        r = lax.rsqrt(jnp.mean(hv * hv, axis=-1, keepdims=True) + EPS)
        hhat = hv * r
        small_ref[0:1, :] += jnp.sum(dn1 * hhat, axis=0, keepdims=True)
        dxh = dn1 * w1_ref[...]
        dh1_ref[...] = dh2_ref[...] + r * (dxh - hhat * jnp.mean(dxh * hhat, axis=-1, keepdims=True))

    row = lambda cols: pl.BlockSpec((ts, cols), lambda i: (i, 0))
    return pl.pallas_call(
        body, name="gla_project_backward", grid=(s // ts,),
        out_shape=(jax.ShapeDtypeStruct((s, D), F32), jax.ShapeDtypeStruct((s, GLA_MAIN + RANK_PAD), BF16),
                   jax.ShapeDtypeStruct((RANK_PAD, KEY_W), F32),
                   jax.ShapeDtypeStruct((8, D), F32)),
        in_specs=[row(KEY_W), row(KEY_W), row(D), row(D), row(KEY_W), row(RANK_PAD), row(D), row(D),
                  _full((1, D)), _full((N_CHIPS, D, GLA_IN_QUARTER)), _full((RANK_PAD, KEY_W)),
                  _full((1, KEY_W))],
        out_specs=(row(D), row(GLA_MAIN + RANK_PAD), _full((RANK_PAD, KEY_W)), _full((8, D))),
        scratch_shapes=[pltpu.VMEM((D, GLA_MAIN + RANK_PAD), BF16)],
        compiler_params=_params("arbitrary"),
    )(dq, dk, dv, dgate, dcum, low, h1, dh2, w1, wgi_q, wgk, bgk)


def local_gradients(xs, target, w0, w1, wf, wpi, gw, gb, scale, wpo, gla_quarters, wgk, bgk, hw_tiled, place):
    wgi_q, wgo_q = gla_quarters
    (h1, pooled, gt, n0), (wgi_q,) = pool_forward(xs, w0, wpi, gw, gb, scale, wpo, [wgi_q])
    (qk, v, gate, low, cum, n1), (wgo_q,) = gla_project(h1, w1, wgi_q, wgk, bgk, [wgo_q])
    wgo = wgo_q.reshape(D, D)
    o, states, scores = gla_forward(qk, v, cum)

    dh2, do, dgate, g_gla_out, small_top = head_and_loss(o, gate, h1, target, hw_tiled, wgo, wf)
    dq, dk, dv, dcum = gla_backward(qk, v, cum, do, states, scores)
    dh1, dproj, g_gk_pad, small_gla = gla_project_backward(
        dq, dk, dv, dgate, dcum, low, h1, dh2, w1, wgi_q, wgk, bgk)
    g_gla_in, _ = matmul_tn(n1, dproj, "grad_gla_in", (GLA_MAIN + RANK_PAD) // 5, place)

    gla_sums = add_halves([g_gla_in, g_gla_out.reshape(N_CHIPS, D // N_CHIPS, D)], place, "add_halves_gla")
    (dx, dpool, g_pool_out, g_group_w, small_pool), gla_got = pool_backward(
        xs, dh1, pooled, gt, w0, wpi, gw, gb, scale, wpo, [b for _, b in gla_sums])
    g_pool_in, mix = matmul_tn(n0, dpool, "grad_pool_in", D // 2, place, by_column_tile=True,
                               also_reduce=[g_group_w, g_pool_out.reshape(N_CHIPS, D // N_CHIPS, D)])

    reduced, total = join_halves(
        g_pool_in, place, [own for own, _ in mix] + [f for f, _ in gla_sums], [got for _, got in mix] + list(gla_got),
        small_pool, small_gla, small_top, g_gk_pad)
    return dx, reduced, total


def kernel(x, norm_w, pool_in_w, pool_group_w, pool_group_b, pool_scale, pool_out_w, gla_in_w, gla_gk_w, gla_gk_b, gla_head_norm_w, gla_out_w, final_norm_w, loss_target, m_norm_w, m_pool_in_w, m_pool_group_w, m_pool_group_b, m_pool_scale, m_pool_out_w, m_gla_in_w, m_gla_gk_w, m_gla_gk_b, m_gla_head_norm_w, m_gla_out_w, m_final_norm_w, v_norm_w, v_pool_in_w, v_pool_group_w, v_pool_group_b, v_pool_scale, v_pool_out_w, v_gla_in_w, v_gla_gk_w, v_gla_gk_b, v_gla_head_norm_w, v_gla_out_w, v_final_norm_w):
    xs = x[0]
    target = loss_target[0]
    q_chip = 2 * lax.axis_index("x") + lax.axis_index("y")
    place = jnp.stack([lax.axis_index("c"), q_chip]).astype(jnp.int32)

    (wpi, gw_q, wpo_q, wgi_q, wgo_q), (bgk, hw_tiled, gb, wgk) = allgather_weights(
        [pool_in_w[0], pool_group_w[0].reshape(GROUP_DIM, GROUP_DIM), pool_out_w[0], gla_in_w[0], gla_out_w[0]],
        exchange=(True, True, True, False, False),
        smalls=[gla_gk_b, gla_head_norm_w, pool_group_b[0], gla_gk_w[0]])
    wpo = wpo_q.reshape(D, D)

    w0 = norm_w[0:1]
    w1 = norm_w[1:2]
    wf = final_norm_w.reshape(1, D)

    dx, reduced, total = local_gradients(
        xs, target, w0, w1, wf, wpi, gw_q, gb, pool_scale, wpo, [wgi_q, wgo_q], wgk, bgk, hw_tiled, place)
    r_pool_in, r_group_w, r_pool_out, r_gla_in, r_gla_out = reduced
    r_group_w = r_group_w.reshape(GROUPS, 64, GROUP_DIM)

    turn = lambda a: jnp.transpose(a, (2, 0, 1))
    back = lambda a: jnp.transpose(a, (1, 2, 0))
    as2d = lambda a, w: a.reshape(-1, w.shape[-1])
    big_names = ("pool_in_w", "pool_group_w", "pool_out_w", "gla_in_w", "gla_out_w")
    big_args = [(pool_in_w, r_pool_in[None], m_pool_in_w, v_pool_in_w),
                (pool_group_w, r_group_w[None], m_pool_group_w, v_pool_group_w),
                (pool_out_w, r_pool_out[None], m_pool_out_w, v_pool_out_w),
                (gla_in_w, r_gla_in[None], m_gla_in_w, v_gla_in_w),
                (gla_out_w, r_gla_out[None], m_gla_out_w, v_gla_out_w)]
    to_kernel = lambda n, a, w: turn(a) if n == "gla_in_w" else as2d(a, w)
    from_kernel = lambda n, a, w: back(a) if n == "gla_in_w" else a.reshape(w.shape)
    big_in = [tuple(to_kernel(n, a, p[0]) for a in p) for n, p in zip(big_names, big_args)]
    big_out = adamw(big_in, "adamw", echo=[n != "gla_in_w" for n in big_names])
    big = {}
    for n, p, i, out in zip(big_names, big_args, big_in, big_out):
        g = out[3] if len(out) == 4 else i[1]
        big[n] = tuple(from_kernel(n, o, p[0]) for o in (g, *out[:3]))

    small_names = ("norm_w", "pool_group_b", "pool_scale", "gla_gk_w", "gla_gk_b", "gla_head_norm_w",
                   "final_norm_w")
    small_args = [(norm_w, m_norm_w, v_norm_w),
                  (pool_group_b, m_pool_group_b, v_pool_group_b),
                  (pool_scale, m_pool_scale, v_pool_scale),
                  (gla_gk_w, m_gla_gk_w, v_gla_gk_w),
                  (gla_gk_b, m_gla_gk_b, v_gla_gk_b),
                  (gla_head_norm_w, m_gla_head_norm_w, v_gla_head_norm_w),
                  (final_norm_w, m_final_norm_w, v_final_norm_w)]
    small_out, loss = adamw_small([tuple(as2d(a, p[0]) for a in p) for p in small_args], total, place)
    small = {n: tuple(o.reshape(p[0].shape) for o in out) for n, p, out in zip(small_names, small_args, small_out)}
    results = [
        small["norm_w"],
        big["pool_in_w"],
        big["pool_group_w"],
        small["pool_group_b"],
        small["pool_scale"],
        big["pool_out_w"],
        big["gla_in_w"],
        small["gla_gk_w"],
        small["gla_gk_b"],
        small["gla_head_norm_w"],
        big["gla_out_w"],
        small["final_norm_w"],
    ]
    grads, deltas, new_m, new_v = zip(*results)
    return (loss.reshape(()), dx[None], *grads, *deltas, *new_m, *new_v)
```
